```python
import jax
import jax.numpy as jnp
from jax import lax
import numpy as np


D_MODEL = 1024
BATCH = 16
SEQ = 2048
DEPTH = 2

GRID_W = 64
CTX_LEN = 256
HEAD_DIM = 64
N_HEADS_A = 8
N_KV_A = 2
N_HEADS_B = 8
N_KV_B = 2
MIX_WIDTH = (N_HEADS_A + N_HEADS_B) * HEAD_DIM
Q_BLOCK = 128
WINDOW = 128
D_FF = 4 * D_MODEL
ROPE_THETA = 10000.0
EPS = 1e-6
NEG_BIG = -1e30
N_MOD = 6
COL_SIZES = (N_HEADS_A * HEAD_DIM, N_KV_A * HEAD_DIM, N_KV_A * HEAD_DIM,
             N_HEADS_B * HEAD_DIM, N_KV_B * HEAD_DIM, N_KV_B * HEAD_DIM)
IN_COLS = sum(COL_SIZES)
SPLITS = tuple(int(s) for s in np.cumsum(COL_SIZES)[:-1])

kernel_name = 'hybrid_dit_gqa_window_sink_block'


def rms_norm(x, g):
    xf = x.astype(jnp.float32)
    y = xf * lax.rsqrt(jnp.mean(xf * xf, axis=-1, keepdims=True) + EPS)
    return (y * g.astype(jnp.float32)).astype(x.dtype)


def modulate(x, shift, scale):
    return x * (1 + scale) + shift


def axial_rope_tables(n_tok):
    rows = n_tok // GRID_W
    row_ids = jnp.repeat(jnp.arange(rows, dtype=jnp.int32), GRID_W).astype(jnp.float32)
    col_ids = jnp.tile(jnp.arange(GRID_W, dtype=jnp.int32), rows).astype(jnp.float32)
    axis_dim = HEAD_DIM // 2
    inv = ROPE_THETA ** (-jnp.arange(0, axis_dim, 2, dtype=jnp.float32) / axis_dim)
    ang_r = row_ids[:, None] * inv[None, :]
    ang_c = col_ids[:, None] * inv[None, :]
    return (jnp.cos(ang_r), jnp.sin(ang_r), jnp.cos(ang_c), jnp.sin(ang_c))


def _rotate(a, cos, sin):
    a1, a2 = jnp.split(a, 2, axis=-1)
    return jnp.concatenate([a1 * cos - a2 * sin, a2 * cos + a1 * sin], axis=-1)


def apply_axial_rope(t, rope):
    cos_r, sin_r, cos_c, sin_c = rope
    tf = t.astype(jnp.float32)
    tr, tc = jnp.split(tf, 2, axis=-1)
    return jnp.concatenate([_rotate(tr, cos_r, sin_r), _rotate(tc, cos_c, sin_c)], axis=-1).astype(t.dtype)


def to_gqa(t, n_heads, n_kv):
    b, n, _ = t.shape
    return t.reshape(b, n, n_kv, n_heads // n_kv, HEAD_DIM).transpose(0, 2, 3, 1, 4)


def to_kv(t, n_kv):
    b, n, _ = t.shape
    return t.reshape(b, n, n_kv, HEAD_DIM).transpose(0, 2, 1, 3)


def merge_heads(o):
    b, hk, g, n, d = o.shape
    return o.transpose(0, 3, 1, 2, 4).reshape(b, n, hk * g * d)


def multi_source_attention(q, ks, vs, biases, sink):
    scale = HEAD_DIM ** -0.5
    logits = [jnp.einsum('bhgqd,bhkd->bhgqk', q, k, preferred_element_type=jnp.float32) * scale + bias
              for k, bias in zip(ks, biases)]
    sizes = [l.shape[-1] for l in logits]
    if sink is not None:
        s = sink.astype(jnp.float32)[None, :, :, None, None]
        logits.append(jnp.broadcast_to(s, logits[0].shape[:-1] + (1,)))
    p = jax.nn.softmax(jnp.concatenate(logits, axis=-1), axis=-1)
    out = None
    off = 0
    for v, n in zip(vs, sizes):
        term = jnp.einsum('bhgqk,bhkd->bhgqd', p[..., off:off + n].astype(v.dtype), v)
        out = term if out is None else out + term
        off += n
    return out


def dense_latent_attention(q, k_lat, v_lat, k_ctx, v_ctx):
    b, hk, g, s, d = q.shape
    nblk = s // Q_BLOCK
    qb = jnp.moveaxis(q.reshape(b, hk, g, nblk, Q_BLOCK, d), 3, 0)

    def one_block(qi):
        return multi_source_attention(qi, (k_ctx, k_lat), (v_ctx, v_lat), (0.0, 0.0), None)

    o = lax.map(one_block, qb)
    return jnp.moveaxis(o, 0, 3).reshape(b, hk, g, s, d)


def banded_blocks(t):
    b, hk, s, d = t.shape
    nblk = s // Q_BLOCK
    tp = jnp.pad(t, ((0, 0), (0, 0), (Q_BLOCK, Q_BLOCK), (0, 0))).reshape(b, hk, nblk + 2, Q_BLOCK, d)
    band = jnp.concatenate([tp[:, :, :-2], tp[:, :, 1:-1], tp[:, :, 2:]], axis=3)
    return jnp.moveaxis(band, 2, 0)


def window_bias(s):
    nblk = s // Q_BLOCK
    i = jnp.arange(nblk, dtype=jnp.int32)[:, None, None]
    r = jnp.arange(Q_BLOCK, dtype=jnp.int32)[None, :, None]
    j = jnp.arange(3 * Q_BLOCK, dtype=jnp.int32)[None, None, :]
    qpos = i * Q_BLOCK + r
    kpos = (i - 1) * Q_BLOCK + j
    valid = (jnp.abs(kpos - qpos) <= WINDOW) & (kpos >= 0) & (kpos < s)
    return jnp.where(valid, 0.0, NEG_BIG).astype(jnp.float32)


def windowed_latent_attention(q, k_lat, v_lat, k_ctx, v_ctx, sink):
    b, hk, g, s, d = q.shape
    nblk = s // Q_BLOCK
    qb = jnp.moveaxis(q.reshape(b, hk, g, nblk, Q_BLOCK, d), 3, 0)
    kb = banded_blocks(k_lat)
    vb = banded_blocks(v_lat)
    bias = window_bias(s)

    def one_block(args):
        qi, ki, vi, bi = args
        return multi_source_attention(qi, (ki, k_ctx), (vi, v_ctx), (bi[None, None, None], 0.0), sink)

    o = lax.map(one_block, (qb, kb, vb, bias))
    return jnp.moveaxis(o, 0, 3).reshape(b, hk, g, s, d)


def sq_relu_mlp(u, w_up, w_down):
    return jnp.square(jax.nn.relu(u @ w_up)) @ w_down


def _fwd_setup_inputs(seed: int = 0) -> dict:
    key = jax.random.key(seed)
    ks = jax.random.split(key, 18)
    f32 = jnp.float32
    nrm = lambda k, shape, s: jax.random.normal(k, shape, f32) * s
    return {
        'x': nrm(ks[0], (BATCH, SEQ, D_MODEL), 1.0),
        'c': nrm(ks[1], (BATCH, D_MODEL), 1.0),
        'ctx': nrm(ks[2], (BATCH, CTX_LEN, D_MODEL), 1.0),
        'c_ctx': nrm(ks[3], (D_MODEL,), 1.0),
        'w_ada': nrm(ks[4], (DEPTH, D_MODEL, N_MOD * D_MODEL), 0.02),
        'b_ada': nrm(ks[5], (DEPTH, N_MOD * D_MODEL), 0.01),
        'g_pre_mix': 1.0 + nrm(ks[6], (DEPTH, D_MODEL), 0.05),
        'g_post_mix': 1.0 + nrm(ks[7], (DEPTH, D_MODEL), 0.05),
        'g_pre_mlp': 1.0 + nrm(ks[8], (DEPTH, D_MODEL), 0.05),
        'g_post_mlp': 1.0 + nrm(ks[9], (DEPTH, D_MODEL), 0.05),
        'w_in': nrm(ks[10], (DEPTH, D_MODEL, IN_COLS), D_MODEL ** -0.5),
        'q_norm': 1.0 + nrm(ks[11], (DEPTH, HEAD_DIM), 0.05),
        'k_norm': 1.0 + nrm(ks[12], (DEPTH, HEAD_DIM), 0.05),
        'sink': nrm(ks[13], (DEPTH, N_HEADS_B), 0.5),
        'w_out': nrm(ks[14], (DEPTH, MIX_WIDTH, D_MODEL), MIX_WIDTH ** -0.5),
        'w_up': nrm(ks[15], (DEPTH, D_MODEL, D_FF), D_MODEL ** -0.5),
        'w_down': nrm(ks[16], (DEPTH, D_FF, D_MODEL), D_FF ** -0.5),
    }


def _fwd_reference(x, c, ctx, c_ctx, w_ada, b_ada, g_pre_mix, g_post_mix, g_pre_mlp, g_post_mlp,
              w_in, q_norm, k_norm, sink, w_out, w_up, w_down):
    b, s, d = x.shape
    rope = axial_rope_tables(s)
    silu_c = jax.nn.silu(c)
    silu_cc = jax.nn.silu(c_ctx)
    h, hc = x, ctx
    for l in range(DEPTH):
        last = l == DEPTH - 1
        mod = (silu_c @ w_ada[l] + b_ada[l]).reshape(b, N_MOD, 1, d)
        mod_c = (silu_cc @ w_ada[l] + b_ada[l]).reshape(N_MOD, 1, 1, d)
        sh_a, sc_a, g_a, sh_m, sc_m, g_m = [mod[:, i] for i in range(N_MOD)]
        csh_a, csc_a, cg_a, csh_m, csc_m, cg_m = [mod_c[i] for i in range(N_MOD)]

        u = modulate(rms_norm(h, g_pre_mix[l]), sh_a, sc_a)
        uc = modulate(rms_norm(hc, g_pre_mix[l]), csh_a, csc_a)
        qa, ka, va, qb, kb, vb = jnp.split(u @ w_in[l], SPLITS, axis=-1)
        qac, kac, vac, qbc, kbc, vbc = jnp.split(uc @ w_in[l], SPLITS, axis=-1)

        qa = apply_axial_rope(rms_norm(to_gqa(qa, N_HEADS_A, N_KV_A), q_norm[l]), rope)
        ka = apply_axial_rope(rms_norm(to_kv(ka, N_KV_A), k_norm[l]), rope)
        va = to_kv(va, N_KV_A)
        kac = rms_norm(to_kv(kac, N_KV_A), k_norm[l])
        vac = to_kv(vac, N_KV_A)
        qb = apply_axial_rope(to_gqa(qb, N_HEADS_B, N_KV_B), rope)
        kb = apply_axial_rope(to_kv(kb, N_KV_B), rope)
        vb = to_kv(vb, N_KV_B)
        kbc = to_kv(kbc, N_KV_B)
        vbc = to_kv(vbc, N_KV_B)
        sink_l = sink[l].reshape(N_KV_B, N_HEADS_B // N_KV_B)

        oa = dense_latent_attention(qa, ka, va, kac, vac)
        ob = windowed_latent_attention(qb, kb, vb, kbc, vbc, sink_l)
        mix = jnp.concatenate([merge_heads(oa), merge_heads(ob)], axis=-1) @ w_out[l]
        h = h + g_a * rms_norm(mix, g_post_mix[l])

        if not last:
            qac = rms_norm(to_gqa(qac, N_HEADS_A, N_KV_A), q_norm[l])
            oac = multi_source_attention(qac, (kac,), (vac,), (0.0,), None)
            obc = multi_source_attention(to_gqa(qbc, N_HEADS_B, N_KV_B), (kbc,), (vbc,), (0.0,), sink_l)
            mixc = jnp.concatenate([merge_heads(oac), merge_heads(obc)], axis=-1) @ w_out[l]
            hc = hc + cg_a * rms_norm(mixc, g_post_mix[l])

        y = sq_relu_mlp(modulate(rms_norm(h, g_pre_mlp[l]), sh_m, sc_m), w_up[l], w_down[l])
        h = h + g_m * rms_norm(y, g_post_mlp[l])
        if not last:
            yc = sq_relu_mlp(modulate(rms_norm(hc, g_pre_mlp[l]), csh_m, csc_m), w_up[l], w_down[l])
            hc = hc + cg_m * rms_norm(yc, g_post_mlp[l])
    return h


import jax as _jax
import jax.numpy as _jnp

TWIN_FORMAT = 'train_step'
FWD_PARAMS = ['x', 'c', 'ctx', 'c_ctx', 'w_ada', 'b_ada', 'g_pre_mix', 'g_post_mix', 'g_pre_mlp', 'g_post_mlp', 'w_in', 'q_norm', 'k_norm', 'sink', 'w_out', 'w_up', 'w_down']
TWIN_WEIGHTS = ['c_ctx', 'w_ada', 'b_ada', 'g_pre_mix', 'g_post_mix', 'g_pre_mlp', 'g_post_mlp', 'w_in', 'q_norm', 'k_norm', 'sink', 'w_out', 'w_up', 'w_down']
TWIN_DIFF_INPUT = 'x'
TWIN_INPUTS = ['x', 'c', 'ctx', 'c_ctx', 'w_ada', 'b_ada', 'g_pre_mix', 'g_post_mix', 'g_pre_mlp', 'g_post_mlp', 'w_in', 'q_norm', 'k_norm', 'sink', 'w_out', 'w_up', 'w_down', 'loss_target', 'm_c_ctx', 'm_w_ada', 'm_b_ada', 'm_g_pre_mix', 'm_g_post_mix', 'm_g_pre_mlp', 'm_g_post_mlp', 'm_w_in', 'm_q_norm', 'm_k_norm', 'm_sink', 'm_w_out', 'm_w_up', 'm_w_down', 'v_c_ctx', 'v_w_ada', 'v_b_ada', 'v_g_pre_mix', 'v_g_post_mix', 'v_g_pre_mlp', 'v_g_post_mlp', 'v_w_in', 'v_q_norm', 'v_k_norm', 'v_sink', 'v_w_out', 'v_w_up', 'v_w_down']
TWIN_OUTPUTS = ['loss', 'grad_x', 'grad_c_ctx', 'grad_w_ada', 'grad_b_ada', 'grad_g_pre_mix', 'grad_g_post_mix', 'grad_g_pre_mlp', 'grad_g_post_mlp', 'grad_w_in', 'grad_q_norm', 'grad_k_norm', 'grad_sink', 'grad_w_out', 'grad_w_up', 'grad_w_down', 'delta_c_ctx', 'delta_w_ada', 'delta_b_ada', 'delta_g_pre_mix', 'delta_g_post_mix', 'delta_g_pre_mlp', 'delta_g_post_mlp', 'delta_w_in', 'delta_q_norm', 'delta_k_norm', 'delta_sink', 'delta_w_out', 'delta_w_up', 'delta_w_down', 'new_m_c_ctx', 'new_m_w_ada', 'new_m_b_ada', 'new_m_g_pre_mix', 'new_m_g_post_mix', 'new_m_g_pre_mlp', 'new_m_g_post_mlp', 'new_m_w_in', 'new_m_q_norm', 'new_m_k_norm', 'new_m_sink', 'new_m_w_out', 'new_m_w_up', 'new_m_w_down', 'new_v_c_ctx', 'new_v_w_ada', 'new_v_b_ada', 'new_v_g_pre_mix', 'new_v_g_post_mix', 'new_v_g_pre_mlp', 'new_v_g_post_mlp', 'new_v_w_in', 'new_v_q_norm', 'new_v_k_norm', 'new_v_sink', 'new_v_w_out', 'new_v_w_up', 'new_v_w_down']
TWIN_LEAF_KINDS = {'loss': 'loss', 'grad_x': 'grad_x', 'grad_c_ctx': 'grad_w', 'grad_w_ada': 'grad_w', 'grad_b_ada': 'grad_w', 'grad_g_pre_mix': 'grad_w', 'grad_g_post_mix': 'grad_w', 'grad_g_pre_mlp': 'grad_w', 'grad_g_post_mlp': 'grad_w', 'grad_w_in': 'grad_w', 'grad_q_norm': 'grad_w', 'grad_k_norm': 'grad_w', 'grad_sink': 'grad_w', 'grad_w_out': 'grad_w', 'grad_w_up': 'grad_w', 'grad_w_down': 'grad_w', 'delta_c_ctx': 'delta_w', 'delta_w_ada': 'delta_w', 'delta_b_ada': 'delta_w', 'delta_g_pre_mix': 'delta_w', 'delta_g_post_mix': 'delta_w', 'delta_g_pre_mlp': 'delta_w', 'delta_g_post_mlp': 'delta_w', 'delta_w_in': 'delta_w', 'delta_q_norm': 'delta_w', 'delta_k_norm': 'delta_w', 'delta_sink': 'delta_w', 'delta_w_out': 'delta_w', 'delta_w_up': 'delta_w', 'delta_w_down': 'delta_w', 'new_m_c_ctx': 'new_m', 'new_m_w_ada': 'new_m', 'new_m_b_ada': 'new_m', 'new_m_g_pre_mix': 'new_m', 'new_m_g_post_mix': 'new_m', 'new_m_g_pre_mlp': 'new_m', 'new_m_g_post_mlp': 'new_m', 'new_m_w_in': 'new_m', 'new_m_q_norm': 'new_m', 'new_m_k_norm': 'new_m', 'new_m_sink': 'new_m', 'new_m_w_out': 'new_m', 'new_m_w_up': 'new_m', 'new_m_w_down': 'new_m', 'new_v_c_ctx': 'new_v', 'new_v_w_ada': 'new_v', 'new_v_b_ada': 'new_v', 'new_v_g_pre_mix': 'new_v', 'new_v_g_post_mix': 'new_v', 'new_v_g_pre_mlp': 'new_v', 'new_v_g_post_mlp': 'new_v', 'new_v_w_in': 'new_v', 'new_v_q_norm': 'new_v', 'new_v_k_norm': 'new_v', 'new_v_sink': 'new_v', 'new_v_w_out': 'new_v', 'new_v_w_up': 'new_v', 'new_v_w_down': 'new_v'}


def _forward(args):
    return _fwd_reference(*[args[k] for k in FWD_PARAMS])


def _output_shape():
    out = _jax.eval_shape(lambda: _forward(_fwd_setup_inputs(0)))
    return out.shape, out.dtype

N_MICROBATCH = 1
ADAM_LR = 0.001
ADAM_B1 = 0.9
ADAM_B2 = 0.999
ADAM_EPS = 1e-08
ADAM_WD = 0.01
ADAM_STEP = 10
PER_EXAMPLE_BATCH_AXIS = {'x': 0, 'c': 0, 'ctx': 0, 'loss_target': 0}
SHARED_INPUTS = []
_WEIGHT_DTYPES = {'c_ctx': _jnp.float32, 'w_ada': _jnp.float32, 'b_ada': _jnp.float32, 'g_pre_mix': _jnp.float32, 'g_post_mix': _jnp.float32, 'g_pre_mlp': _jnp.float32, 'g_post_mlp': _jnp.float32, 'w_in': _jnp.float32, 'q_norm': _jnp.float32, 'k_norm': _jnp.float32, 'sink': _jnp.float32, 'w_out': _jnp.float32, 'w_up': _jnp.float32, 'w_down': _jnp.float32}
MOMENT_SCALE = {'c_ctx': 2.382447e+00, 'w_ada': 5.341269e+00, 'b_ada': 9.747468e+00, 'g_pre_mix': 2.294699e+00, 'g_post_mix': 9.420422e+00, 'g_pre_mlp': 1.375735e+00, 'g_post_mlp': 7.138346e+00, 'w_in': 4.202176e+00, 'q_norm': 5.189237e-01, 'k_norm': 5.276672e-01, 'sink': 5.525631e-02, 'w_out': 5.076687e+00, 'w_up': 9.706779e-01, 'w_down': 2.699405e+00}


def _to_microbatches(a, axis):
    t = _jnp.moveaxis(a, axis, 0)
    t = t.reshape((N_MICROBATCH, t.shape[0] // N_MICROBATCH) + t.shape[1:])
    return _jnp.moveaxis(t, 1, axis + 1)


def setup_inputs(seed: int = 0) -> dict:
    inp = _fwd_setup_inputs(seed)
    key = _jax.random.fold_in(_jax.random.key(seed), 7919)
    shape, _ = _output_shape()
    out = dict(inp)
    out["loss_target"] = _jax.random.normal(_jax.random.fold_in(key, 0), shape, _jnp.float32)
    for i, name in enumerate(TWIN_WEIGHTS):
        w = inp[name].astype(_jnp.float32)
        if MOMENT_SCALE is None:
            s = _jnp.sqrt(_jnp.mean(_jnp.square(w)) + 1e-30)
        else:
            s = MOMENT_SCALE[name]
        km, kv = _jax.random.split(_jax.random.fold_in(key, i + 1))
        out[name] = w
        out["m_" + name] = s * _jax.random.normal(km, w.shape, _jnp.float32)
        out["v_" + name] = (s * s) * _jax.random.uniform(kv, w.shape, _jnp.float32, 0.5, 1.5)
    if N_MICROBATCH > 1:
        for name, axis in PER_EXAMPLE_BATCH_AXIS.items():
            out[name] = _to_microbatches(out[name], axis)
    return {'x': out['x'], 'c': out['c'], 'ctx': out['ctx'], 'c_ctx': out['c_ctx'], 'w_ada': out['w_ada'], 'b_ada': out['b_ada'], 'g_pre_mix': out['g_pre_mix'], 'g_post_mix': out['g_post_mix'], 'g_pre_mlp': out['g_pre_mlp'], 'g_post_mlp': out['g_post_mlp'], 'w_in': out['w_in'], 'q_norm': out['q_norm'], 'k_norm': out['k_norm'], 'sink': out['sink'], 'w_out': out['w_out'], 'w_up': out['w_up'], 'w_down': out['w_down'], 'loss_target': out['loss_target'], 'm_c_ctx': out['m_c_ctx'], 'm_w_ada': out['m_w_ada'], 'm_b_ada': out['m_b_ada'], 'm_g_pre_mix': out['m_g_pre_mix'], 'm_g_post_mix': out['m_g_post_mix'], 'm_g_pre_mlp': out['m_g_pre_mlp'], 'm_g_post_mlp': out['m_g_post_mlp'], 'm_w_in': out['m_w_in'], 'm_q_norm': out['m_q_norm'], 'm_k_norm': out['m_k_norm'], 'm_sink': out['m_sink'], 'm_w_out': out['m_w_out'], 'm_w_up': out['m_w_up'], 'm_w_down': out['m_w_down'], 'v_c_ctx': out['v_c_ctx'], 'v_w_ada': out['v_w_ada'], 'v_b_ada': out['v_b_ada'], 'v_g_pre_mix': out['v_g_pre_mix'], 'v_g_post_mix': out['v_g_post_mix'], 'v_g_pre_mlp': out['v_g_pre_mlp'], 'v_g_post_mlp': out['v_g_post_mlp'], 'v_w_in': out['v_w_in'], 'v_q_norm': out['v_q_norm'], 'v_k_norm': out['v_k_norm'], 'v_sink': out['v_sink'], 'v_w_out': out['v_w_out'], 'v_w_up': out['v_w_up'], 'v_w_down': out['v_w_down']}


def _loss(weights, diff, rest, loss_target):
    with _jax.named_scope("forward"):
        args = {**rest, TWIN_DIFF_INPUT: diff, **{k: w.astype(_WEIGHT_DTYPES[k]) for k, w in weights.items()}}
        y = _forward(args)
    with _jax.named_scope("loss_head"):
        err = _jnp.square(y.astype(_jnp.float32) - loss_target)
        return 0.5 * _jnp.sum(_jnp.mean(err, axis=-1)) if err.ndim else 0.5 * err


def _adamw(w, g, m, v):
    m = ADAM_B1 * m + (1.0 - ADAM_B1) * g
    v = ADAM_B2 * v + (1.0 - ADAM_B2) * _jnp.square(g)
    m_hat = m / (1.0 - ADAM_B1 ** ADAM_STEP)
    v_hat = v / (1.0 - ADAM_B2 ** ADAM_STEP)
    delta = -ADAM_LR * (m_hat / (_jnp.sqrt(v_hat) + ADAM_EPS) + ADAM_WD * w)
    return delta, m, v


def reference(x, c, ctx, c_ctx, w_ada, b_ada, g_pre_mix, g_post_mix, g_pre_mlp, g_post_mlp, w_in, q_norm, k_norm, sink, w_out, w_up, w_down, loss_target, m_c_ctx, m_w_ada, m_b_ada, m_g_pre_mix, m_g_post_mix, m_g_pre_mlp, m_g_post_mlp, m_w_in, m_q_norm, m_k_norm, m_sink, m_w_out, m_w_up, m_w_down, v_c_ctx, v_w_ada, v_b_ada, v_g_pre_mix, v_g_post_mix, v_g_pre_mlp, v_g_post_mlp, v_w_in, v_q_norm, v_k_norm, v_sink, v_w_out, v_w_up, v_w_down):
    given = dict(x=x, c=c, ctx=ctx, c_ctx=c_ctx, w_ada=w_ada, b_ada=b_ada, g_pre_mix=g_pre_mix, g_post_mix=g_post_mix, g_pre_mlp=g_pre_mlp, g_post_mlp=g_post_mlp, w_in=w_in, q_norm=q_norm, k_norm=k_norm, sink=sink, w_out=w_out, w_up=w_up, w_down=w_down, loss_target=loss_target, m_c_ctx=m_c_ctx, m_w_ada=m_w_ada, m_b_ada=m_b_ada, m_g_pre_mix=m_g_pre_mix, m_g_post_mix=m_g_post_mix, m_g_pre_mlp=m_g_pre_mlp, m_g_post_mlp=m_g_post_mlp, m_w_in=m_w_in, m_q_norm=m_q_norm, m_k_norm=m_k_norm, m_sink=m_sink, m_w_out=m_w_out, m_w_up=m_w_up, m_w_down=m_w_down, v_c_ctx=v_c_ctx, v_w_ada=v_w_ada, v_b_ada=v_b_ada, v_g_pre_mix=v_g_pre_mix, v_g_post_mix=v_g_post_mix, v_g_pre_mlp=v_g_pre_mlp, v_g_post_mlp=v_g_post_mlp, v_w_in=v_w_in, v_q_norm=v_q_norm, v_k_norm=v_k_norm, v_sink=v_sink, v_w_out=v_w_out, v_w_up=v_w_up, v_w_down=v_w_down)
    weights = {n: given[n] for n in TWIN_WEIGHTS}
    shared = {n: given[n] for n in SHARED_INPUTS}
    per_example = {n: given[n] for n in ['x', 'c', 'ctx']}
    grad_fn = _jax.value_and_grad(_loss, argnums=(0, 1))

    def one_microbatch(ex, loss_target):
        ex = dict(ex)
        diff = ex.pop(TWIN_DIFF_INPUT)
        return grad_fn(weights, diff, {**shared, **ex}, loss_target)

    if N_MICROBATCH == 1:
        loss, (grad_w, grad_x) = one_microbatch(per_example, given["loss_target"])
    else:
        def body(carry, xs):
            loss_sum, grad_sum = carry
            l_k, (gw_k, gx_k) = one_microbatch(xs[0], xs[1])
            with _jax.named_scope("update"):
                return (loss_sum + l_k, _jax.tree.map(_jnp.add, grad_sum, gw_k)), gx_k

        init = (_jnp.zeros((), _jnp.float32), _jax.tree.map(_jnp.zeros_like, weights))
        (loss, grad_w), grad_x = _jax.lax.scan(body, init, (per_example, given["loss_target"]))
    with _jax.named_scope("update"):
        delta_w, new_m, new_v = {}, {}, {}
        for n in TWIN_WEIGHTS:
            delta_w[n], new_m[n], new_v[n] = _adamw(weights[n], grad_w[n], given["m_" + n], given["v_" + n])
    return (loss, grad_x, *[grad_w[n] for n in TWIN_WEIGHTS], *[delta_w[n] for n in TWIN_WEIGHTS],
            *[new_m[n] for n in TWIN_WEIGHTS], *[new_v[n] for n in TWIN_WEIGHTS])
```

```python
import functools

import jax
import jax.numpy as jnp
import numpy as np
from jax import lax
from jax.experimental import pallas as pl
from jax.experimental.pallas import tpu as pltpu

F32 = jnp.float32
BF16 = jnp.bfloat16

D_MODEL = 1024
HEAD_DIM = 64
GROUP = 4
WINDOW = 128
N_MOD = 6
D_FF = 4 * D_MODEL
IN_COLS = 1536
GRID_W = 64
ROPE_THETA = 10000.0
EPS = 1e-6
NEG_BIG = -1e30
Q_SCALE = HEAD_DIM ** -0.5
DEPTH = 2
N_DEV = 8

ADAM_LR = 0.001
ADAM_B1 = 0.9
ADAM_B2 = 0.999
ADAM_EPS = 1e-08
ADAM_WD = 0.01
ADAM_STEP = 10

V7X_VMEM_BYTES = 64 * 1024 * 1024
VMEM_LIMIT = V7X_VMEM_BYTES - 8 * 1024 * 1024

MESH = pl.DeviceIdType.MESH

COL_KA, COL_VA, COL_KB, COL_VB = 4, 5, 10, 11

PACK_OFF = {("up", 0): 0, ("down", 0): 512, ("up", 1): 1024, ("down", 1): 1536,
            ("in", 0): 2048, ("out", 0): 2240, ("in", 1): 2368, ("out", 1): 2560}
PACK_ROWS = 2688
PACK_ORDER = (("up", 0), ("down", 0), ("up", 1), ("down", 1), ("in", 0), ("out", 0), ("in", 1), ("out", 1))


def _pick(n, cands):
    for t in cands:
        if n % t == 0:
            return t
    raise ValueError(f"no tile for {n}")


def _params(sem):
    return pltpu.CompilerParams(dimension_semantics=sem, vmem_limit_bytes=VMEM_LIMIT)


def _all_gather(x, name, in_hbm):
    m_per, n = x.shape
    space = pl.ANY if in_hbm else pltpu.VMEM

    def body(x_ref, out_ref, send_sems, recv_sems, local_sem):
        x_, y_, c_ = lax.axis_index("x"), lax.axis_index("y"), lax.axis_index("c")
        me, sibling = (x_, y_, c_), (x_, y_, 1 - c_)
        chips = [(1 - x_, y_), (x_, 1 - y_), (1 - x_, 1 - y_)]

        def rows(px, py, pc):
            return out_ref.at[pl.ds((4 * px + 2 * py + pc) * m_per, m_per), :]

        def copy(k, block, to, src=None):
            return pltpu.make_async_remote_copy(
                src_ref=rows(*block) if src is None else src, dst_ref=rows(*block),
                send_sem=send_sems.at[k], recv_sem=recv_sems.at[k], device_id=to, device_id_type=MESH)

        mine = pltpu.make_async_copy(x_ref, rows(*me), local_sem)
        mine.start()
        first = [copy(0, me, sibling, src=x_ref)]
        first += [copy(1 + j, me, (*chip, c_), src=x_ref) for j, chip in enumerate(chips)]
        for cp in first:
            cp.start()
        passed = [copy(4 + j, (*chip, c_), sibling) for j, chip in enumerate(chips)]
        for j, chip in enumerate(chips):
            copy(1 + j, (*chip, c_), me).wait_recv()
            passed[j].start()
        copy(0, sibling, me).wait_recv()
        for j, chip in enumerate(chips):
            copy(4 + j, (*chip, 1 - c_), me).wait_recv()
        for cp in first + passed:
            cp.wait_send()
        mine.wait()

    return pl.pallas_call(
        body, name=name,
        out_shape=jax.ShapeDtypeStruct((N_DEV * m_per, n), x.dtype),
        in_specs=[pl.BlockSpec(memory_space=space)],
        out_specs=pl.BlockSpec(memory_space=space),
        scratch_shapes=[pltpu.SemaphoreType.DMA((7,)), pltpu.SemaphoreType.DMA((7,)), pltpu.SemaphoreType.DMA],
    )(x)


def _sibling_exchange(x, name, pick_other_half):
    out_shape = x.shape[1:] if pick_other_half else x.shape

    def body(x_ref, out_ref, send_sem, recv_sem):
        x_, y_, c_ = lax.axis_index("x"), lax.axis_index("y"), lax.axis_index("c")
        src = x_ref.at[1 - c_] if pick_other_half else x_ref
        cp = pltpu.make_async_remote_copy(src_ref=src, dst_ref=out_ref, send_sem=send_sem, recv_sem=recv_sem,
                                          device_id=(x_, y_, 1 - c_), device_id_type=MESH)
        cp.start()
        cp.wait()

    return pl.pallas_call(
        body, name=name,
        out_shape=jax.ShapeDtypeStruct(out_shape, x.dtype),
        in_specs=[pl.BlockSpec(memory_space=pl.ANY)],
        out_specs=pl.BlockSpec(memory_space=pl.ANY),
        scratch_shapes=[pltpu.SemaphoreType.DMA, pltpu.SemaphoreType.DMA],
    )(x)


def _chip_exchange(a, name):
    _, r, c = a.shape

    def body(a_ref, out_ref, send_sems, recv_sems):
        x_, y_, c_ = lax.axis_index("x"), lax.axis_index("y"), lax.axis_index("c")
        chips = [(1 - x_, y_), (x_, 1 - y_), (1 - x_, 1 - y_)]
        cps = [pltpu.make_async_remote_copy(src_ref=a_ref.at[2 * tx + ty], dst_ref=out_ref.at[j],
                                            send_sem=send_sems.at[j], recv_sem=recv_sems.at[j],
                                            device_id=(tx, ty, c_), device_id_type=MESH)
               for j, (tx, ty) in enumerate(chips)]
        for cp in cps:
            cp.start()
        for cp in cps:
            cp.wait()

    return pl.pallas_call(
        body, name=name,
        out_shape=jax.ShapeDtypeStruct((3, r, c), a.dtype),
        in_specs=[pl.BlockSpec(memory_space=pl.ANY)],
        out_specs=pl.BlockSpec(memory_space=pl.ANY),
        scratch_shapes=[pltpu.SemaphoreType.DMA((3,)), pltpu.SemaphoreType.DMA((3,))],
    )(a)


def _pair_sum(p_halves, r1, c_idx, name):
    _, nb, r, c = p_halves.shape
    tr = _pick(r, (384, 256, 128, 64, 32, 16))

    def body(s_ref, p_ref, r_ref, o32_ref, o16_ref):
        v = p_ref[...] + r_ref[...]
        o32_ref[...] = v
        o16_ref[...] = v.astype(BF16)

    grid_spec = pltpu.PrefetchScalarGridSpec(
        num_scalar_prefetch=1, grid=(nb, r // tr),
        in_specs=[pl.BlockSpec((None, None, tr, c), lambda j, i, s: (s[0], j, i, 0)),
                  pl.BlockSpec((None, tr, c), lambda j, i, s: (j, i, 0))],
        out_specs=[pl.BlockSpec((None, tr, c), lambda j, i, s: (j, i, 0)),
                   pl.BlockSpec((None, tr, c), lambda j, i, s: (j, i, 0))])
    return pl.pallas_call(
        body, name=name, grid_spec=grid_spec,
        out_shape=[jax.ShapeDtypeStruct((nb, r, c), F32), jax.ShapeDtypeStruct((nb, r, c), BF16)],
        compiler_params=_params(("arbitrary", "arbitrary")),
    )(c_idx, p_halves, r1)


def _owner_sum(a32, r2, k_idx, name):
    _, r, c = a32.shape
    tr = _pick(r, (384, 256, 128, 64, 32, 16))

    def body(s_ref, a_ref, r_ref, o_ref):
        v = a_ref[...]
        for j in range(3):
            v = v + r_ref[j].astype(F32)
        o_ref[...] = v

    grid_spec = pltpu.PrefetchScalarGridSpec(
        num_scalar_prefetch=1, grid=(r // tr,),
        in_specs=[pl.BlockSpec((None, tr, c), lambda i, s: (s[0], i, 0)),
                  pl.BlockSpec((3, tr, c), lambda i, s: (0, i, 0))],
        out_specs=pl.BlockSpec((tr, c), lambda i, s: (i, 0)))
    return pl.pallas_call(
        body, name=name, grid_spec=grid_spec,
        out_shape=jax.ShapeDtypeStruct((r, c), F32),
        compiler_params=_params(("arbitrary",)),
    )(k_idx, a32, r2)


def _pack_local_half(w_in_s, w_out_s, w_up_s, w_down_s, c_idx):
    parts = []
    for kind, l in PACK_ORDER:
        if kind == "up":
            p = lax.dynamic_slice_in_dim(w_up_s[l], c_idx * 512, 512, 0)
        elif kind == "down":
            p = lax.dynamic_slice_in_dim(w_down_s[l], c_idx * 512, 512, 0)
        elif kind == "in":
            p = lax.dynamic_slice_in_dim(w_in_s[l], c_idx * 512, 512, 0).reshape(192, 1024)
        else:
            p = lax.dynamic_slice_in_dim(w_out_s[l], c_idx * 128, 128, 0)
        parts.append(p.astype(BF16))
    return jnp.concatenate(parts, axis=0)


def _unpack_gathered(g):
    g = g.reshape(4, 2, PACK_ROWS, 1024)
    w_in, w_out, w_up, w_down = [], [], [], []
    for l in range(DEPTH):
        o = PACK_OFF[("up", l)]
        w_up.append(g[:, :, o:o + 512].transpose(1, 2, 0, 3).reshape(1024, 4096))
        o = PACK_OFF[("down", l)]
        w_down.append(g[:, :, o:o + 512].reshape(4096, 1024))
        o = PACK_OFF[("in", l)]
        w_in.append(g[:, :, o:o + 192].reshape(4, 2, 512, 384).transpose(1, 2, 0, 3).reshape(1024, 1536))
        o = PACK_OFF[("out", l)]
        w_out.append(g[:, :, o:o + 128].reshape(1024, 1024))
    return w_in, w_out, w_up, w_down


def _pack_full_grads(dw_in, dw_out, dw_up, dw_down):
    pieces = {}
    for l in range(DEPTH):
        pieces[("up", l)] = dw_up[l].reshape(2, 512, 4, 1024).transpose(0, 2, 1, 3)
        pieces[("down", l)] = dw_down[l].reshape(4, 2, 512, 1024).transpose(1, 0, 2, 3)
        pieces[("in", l)] = dw_in[l].reshape(2, 512, 4, 384).transpose(0, 2, 1, 3).reshape(2, 4, 192, 1024)
        pieces[("out", l)] = dw_out[l].reshape(4, 2, 128, 1024).transpose(1, 0, 2, 3)
    return jnp.concatenate([pieces[k] for k in PACK_ORDER], axis=2)


def _unpack_shard(gh):
    w_in, w_out, w_up, w_down = [], [], [], []
    for l in range(DEPTH):
        o = PACK_OFF[("up", l)]
        w_up.append(gh[:, o:o + 512].reshape(1024, 1024))
        o = PACK_OFF[("down", l)]
        w_down.append(gh[:, o:o + 512].reshape(1024, 1024))
        o = PACK_OFF[("in", l)]
        w_in.append(gh[:, o:o + 192].reshape(1024, 384))
        o = PACK_OFF[("out", l)]
        w_out.append(gh[:, o:o + 128].reshape(256, 1024))
    return jnp.stack(w_in), jnp.stack(w_out), jnp.stack(w_up), jnp.stack(w_down)


def _matmul(a, b, mode, out_dtypes, name, epilogue=None, extras=(), tm=512, tn=1024, tk=1024):
    if mode == "nn":
        (m, k), n = a.shape, b.shape[1]
    elif mode == "nt":
        (m, k), n = a.shape, b.shape[0]
    else:
        (k, m), n = a.shape, b.shape[1]
    tm = _pick(m, (tm, 512, 256, 128, 64, 32, 16, 8))
    tn = _pick(n, (tn, 1024, 768, 512, 384, 256, 128))
    tk = _pick(k, (tk, 1024, 768, 512, 384, 256, 128, 32))
    nk = k // tk
    a_spec = pl.BlockSpec((tk, tm), lambda i, j, kk: (kk, i)) if mode == "tn" else pl.BlockSpec((tm, tk), lambda i, j, kk: (i, kk))
    b_spec = pl.BlockSpec((tn, tk), lambda i, j, kk: (j, kk)) if mode == "nt" else pl.BlockSpec((tk, tn), lambda i, j, kk: (kk, j))
    mn_spec = pl.BlockSpec((tm, tn), lambda i, j, kk: (i, j))
    dims = {"nn": (((1,), (0,)), ((), ())), "nt": (((1,), (1,)), ((), ())), "tn": (((0,), (0,)), ((), ()))}[mode]
    n_extra, n_out = len(extras), len(out_dtypes)

    def body(a_ref, b_ref, *rest):
        extra_refs, out_refs, acc_ref = rest[:n_extra], rest[n_extra:n_extra + n_out], rest[-1]
        kk = pl.program_id(2)

        @pl.when(kk == 0)
        def _():
            acc_ref[...] = jnp.zeros_like(acc_ref)

        acc_ref[...] += lax.dot_general(a_ref[...], b_ref[...], dims, preferred_element_type=F32)

        @pl.when(kk == nk - 1)
        def _():
            acc = acc_ref[...]
            vals = (acc,) if epilogue is None else epilogue(acc, *[e[...] for e in extra_refs])
            for o_ref, v in zip(out_refs, vals):
                o_ref[...] = v.astype(o_ref.dtype)

    outs = pl.pallas_call(
        body, name=name, grid=(m // tm, n // tn, nk),
        in_specs=[a_spec, b_spec] + [mn_spec] * n_extra,
        out_specs=[mn_spec] * n_out,
        out_shape=[jax.ShapeDtypeStruct((m, n), dt) for dt in out_dtypes],
        scratch_shapes=[pltpu.VMEM((tm, tn), F32)],
        compiler_params=_params(("parallel", "parallel", "arbitrary")),
    )(a, b, *extras)
    return outs[0] if n_out == 1 else outs


class _Rows:
    def __init__(self, nb, seq, ctx):
        self.nb, self.seq, self.ctx = nb, seq, ctx
        self.n_lat, self.n_ctx = nb * seq, nb * ctx
        self.rows = self.n_lat + self.n_ctx
        self.tm = _pick(np.gcd(seq, self.n_ctx), (512, 256, 128))
        self.tiles_per_ex = seq // self.tm
        self.n_tiles = self.rows // self.tm
        self.n_lat_tiles = self.n_lat // self.tm
        self.groups = nb + 1

    def group(self, i):
        return jnp.minimum(i // self.tiles_per_ex, self.nb)

    def first_of_group(self, i):
        return jnp.logical_and(i % self.tiles_per_ex == 0, i <= self.n_lat_tiles)


def _mod_spec(rt):
    return pl.BlockSpec((1, N_MOD, D_MODEL), lambda i: (rt.group(i), 0, 0))


def _row_spec(rt, cols):
    return pl.BlockSpec((rt.tm, cols), lambda i: (i, 0))


def _vec_spec(cols):
    return pl.BlockSpec((1, cols), lambda i: (0, 0))


def _group_spec(rt):
    return pl.BlockSpec((1, 1, D_MODEL), lambda i: (rt.group(i), 0, 0))


def _norm_mod_fwd(rt, h, gamma, mod, i_shift, i_scale, name):
    def body(h_ref, g_ref, mod_ref, u_ref):
        h_ = h_ref[...]
        rinv = lax.rsqrt(jnp.mean(h_ * h_, axis=-1, keepdims=True) + EPS)
        n = h_ * rinv * g_ref[...]
        u_ref[...] = (n * (1.0 + mod_ref[0, i_scale:i_scale + 1, :]) + mod_ref[0, i_shift:i_shift + 1, :]).astype(BF16)

    return pl.pallas_call(
        body, name=name, grid=(rt.n_tiles,),
        in_specs=[_row_spec(rt, D_MODEL), _vec_spec(D_MODEL), _mod_spec(rt)],
        out_specs=_row_spec(rt, D_MODEL),
        out_shape=jax.ShapeDtypeStruct((rt.rows, D_MODEL), BF16),
        compiler_params=_params(("parallel",)),
    )(h, gamma, mod)


def _norm_mod_bwd(rt, du, h, dres, gamma, mod, i_scale, name):
    def body(du_ref, h_ref, dres_ref, g_ref, mod_ref, dh_ref, dsh_ref, dsc_ref, dg_ref):
        i = pl.program_id(0)
        du_, h_ = du_ref[...], h_ref[...]
        g_ = g_ref[...]
        one_sc = 1.0 + mod_ref[0, i_scale:i_scale + 1, :]
        rinv = lax.rsqrt(jnp.mean(h_ * h_, axis=-1, keepdims=True) + EPS)
        n0 = h_ * rinv
        dn = du_ * g_ * one_sc
        dh_ref[...] = dres_ref[...] + rinv * (dn - n0 * jnp.mean(dn * n0, axis=-1, keepdims=True))

        @pl.when(rt.first_of_group(i))
        def _():
            dsh_ref[...] = jnp.zeros_like(dsh_ref)
            dsc_ref[...] = jnp.zeros_like(dsc_ref)

        @pl.when(i == 0)
        def _():
            dg_ref[...] = jnp.zeros_like(dg_ref)

        dsh_ref[0] += jnp.sum(du_, axis=0, keepdims=True)
        dsc_ref[0] += jnp.sum(du_ * n0 * g_, axis=0, keepdims=True)
        dg_ref[...] += jnp.sum(du_ * one_sc * n0, axis=0, keepdims=True)

    return pl.pallas_call(
        body, name=name, grid=(rt.n_tiles,),
        in_specs=[_row_spec(rt, D_MODEL), _row_spec(rt, D_MODEL), _row_spec(rt, D_MODEL), _vec_spec(D_MODEL), _mod_spec(rt)],
        out_specs=[_row_spec(rt, D_MODEL), _group_spec(rt), _group_spec(rt), _vec_spec(D_MODEL)],
        out_shape=[jax.ShapeDtypeStruct((rt.rows, D_MODEL), F32),
                   jax.ShapeDtypeStruct((rt.groups, 1, D_MODEL), F32),
                   jax.ShapeDtypeStruct((rt.groups, 1, D_MODEL), F32),
                   jax.ShapeDtypeStruct((1, D_MODEL), F32)],
        compiler_params=_params(("arbitrary",)),
    )(du, h, dres, gamma, mod)


def _post_norm_fwd(rt, h, z, gamma, mod, i_gate, name):
    def body(h_ref, z_ref, g_ref, mod_ref, o_ref):
        z_ = z_ref[...]
        rinv = lax.rsqrt(jnp.mean(z_ * z_, axis=-1, keepdims=True) + EPS)
        o_ref[...] = h_ref[...] + mod_ref[0, i_gate:i_gate + 1, :] * (z_ * rinv * g_ref[...])

    return pl.pallas_call(
        body, name=name, grid=(rt.n_tiles,),
        in_specs=[_row_spec(rt, D_MODEL), _row_spec(rt, D_MODEL), _vec_spec(D_MODEL), _mod_spec(rt)],
        out_specs=_row_spec(rt, D_MODEL),
        out_shape=jax.ShapeDtypeStruct((rt.rows, D_MODEL), F32),
        compiler_params=_params(("parallel",)),
    )(h, z, gamma, mod)


def _post_norm_bwd(rt, dh, z, gamma, mod, i_gate, name):
    def body(dh_ref, z_ref, g_ref, mod_ref, dz_ref, dgate_ref, dg_ref):
        i = pl.program_id(0)
        dh_, z_ = dh_ref[...], z_ref[...]
        g_ = g_ref[...]
        gate = mod_ref[0, i_gate:i_gate + 1, :]
        rinv = lax.rsqrt(jnp.mean(z_ * z_, axis=-1, keepdims=True) + EPS)
        n0 = z_ * rinv
        dn = dh_ * gate * g_
        dz_ref[...] = (rinv * (dn - n0 * jnp.mean(dn * n0, axis=-1, keepdims=True))).astype(BF16)

        @pl.when(rt.first_of_group(i))
        def _():
            dgate_ref[...] = jnp.zeros_like(dgate_ref)

        @pl.when(i == 0)
        def _():
            dg_ref[...] = jnp.zeros_like(dg_ref)

        dgate_ref[0] += jnp.sum(dh_ * n0 * g_, axis=0, keepdims=True)
        dg_ref[...] += jnp.sum(dh_ * gate * n0, axis=0, keepdims=True)

    return pl.pallas_call(
        body, name=name, grid=(rt.n_tiles,),
        in_specs=[_row_spec(rt, D_MODEL), _row_spec(rt, D_MODEL), _vec_spec(D_MODEL), _mod_spec(rt)],
        out_specs=[_row_spec(rt, D_MODEL), _group_spec(rt), _vec_spec(D_MODEL)],
        out_shape=[jax.ShapeDtypeStruct((rt.rows, D_MODEL), BF16),
                   jax.ShapeDtypeStruct((rt.groups, 1, D_MODEL), F32),
                   jax.ShapeDtypeStruct((1, D_MODEL), F32)],
        compiler_params=_params(("arbitrary",)),
    )(dh, z, gamma, mod)


def _loss_grad(rt, h, target, name):
    last = rt.n_lat_tiles - 1

    def body(h_ref, t_ref, dh_ref, sq_ref):
        i = pl.program_id(0)

        @pl.when(i == 0)
        def _():
            sq_ref[...] = jnp.zeros_like(sq_ref)

        @pl.when(i <= last)
        def _():
            e = h_ref[...] - t_ref[...]
            dh_ref[...] = e * (1.0 / D_MODEL)
            sq_ref[...] += jnp.sum(e * e, axis=0, keepdims=True)

        @pl.when(i > last)
        def _():
            dh_ref[...] = jnp.zeros_like(dh_ref)

    return pl.pallas_call(
        body, name=name, grid=(rt.n_tiles,),
        in_specs=[_row_spec(rt, D_MODEL), pl.BlockSpec((rt.tm, D_MODEL), lambda i: (jnp.minimum(i, last), 0))],
        out_specs=[_row_spec(rt, D_MODEL), _vec_spec(D_MODEL)],
        out_shape=[jax.ShapeDtypeStruct((rt.rows, D_MODEL), F32), jax.ShapeDtypeStruct((1, D_MODEL), F32)],
        compiler_params=_params(("arbitrary",)),
    )(h, target)


def _rope_tables(rt):
    pos = jnp.arange(rt.seq, dtype=jnp.int32)
    row_ids = (pos // GRID_W).astype(F32)
    col_ids = (pos % GRID_W).astype(F32)
    axis_dim = HEAD_DIM // 2
    inv = ROPE_THETA ** (-jnp.arange(0, axis_dim, 2, dtype=F32) / axis_dim)
    ang_r, ang_c = row_ids[:, None] * inv[None, :], col_ids[:, None] * inv[None, :]
    cr, sr, cc, sc = jnp.cos(ang_r), jnp.sin(ang_r), jnp.cos(ang_c), jnp.sin(ang_c)
    zero = jnp.zeros_like(sr)
    cos = jnp.concatenate([cr, cr, cc, cc], axis=1)
    s_lo = jnp.concatenate([zero, sr, zero, sc], axis=1)
    s_hi = jnp.concatenate([-sr, zero, -sc, zero], axis=1)

    def full(t, ctx_value):
        t = jnp.tile(t, (rt.nb, 2))
        return jnp.concatenate([t, jnp.full((rt.n_ctx, 128), ctx_value, F32)], axis=0)

    return full(cos, 1.0), full(s_lo, 0.0), full(s_hi, 0.0)


def _head_stats(t, lo):
    sq = t * t
    s_lo = jnp.sum(jnp.where(lo, sq, 0.0), axis=1, keepdims=True)
    s_hi = jnp.sum(jnp.where(lo, 0.0, sq), axis=1, keepdims=True)
    return lax.rsqrt(jnp.where(lo, s_lo, s_hi) * (1.0 / HEAD_DIM) + EPS)


def _prep_fwd(rt, qkv, tables, qn, kn, name):
    def body(qkv_ref, c_ref, s1_ref, s2_ref, qn_ref, kn_ref, out_ref):
        c, s1, s2 = c_ref[...], s1_ref[...], s2_ref[...]
        lo = lax.broadcasted_iota(jnp.int32, (rt.tm, 128), 1) < HEAD_DIM

        def rope(t):
            return t * c + pltpu.roll(t, 16, 1) * s1 + pltpu.roll(t, 112, 1) * s2

        for j in range(12):
            t = qkv_ref[:, j * 128:(j + 1) * 128]
            if j < 4:
                t = rope(t * _head_stats(t, lo) * qn_ref[...]) * Q_SCALE
            elif j == COL_KA:
                t = rope(t * _head_stats(t, lo) * kn_ref[...])
            elif 6 <= j < 10:
                t = rope(t) * Q_SCALE
            elif j == COL_KB:
                t = rope(t)
            out_ref[:, j * 128:(j + 1) * 128] = t.astype(BF16)

    return pl.pallas_call(
        body, name=name, grid=(rt.n_tiles,),
        in_specs=[_row_spec(rt, IN_COLS)] + [_row_spec(rt, 128)] * 3 + [_vec_spec(128)] * 2,
        out_specs=_row_spec(rt, IN_COLS),
        out_shape=jax.ShapeDtypeStruct((rt.rows, IN_COLS), BF16),
        compiler_params=_params(("parallel",)),
    )(qkv, *tables, qn, kn)


def _prep_bwd(rt, dq, dkv, qkv, tables, qn, kn, name):
    def body(dq_ref, dkv_ref, qkv_ref, c_ref, s1_ref, s2_ref, qn_ref, kn_ref, out_ref, dqn_ref, dkn_ref):
        i = pl.program_id(0)
        c, s1, s2 = c_ref[...], s1_ref[...], s2_ref[...]
        lo = lax.broadcasted_iota(jnp.int32, (rt.tm, 128), 1) < HEAD_DIM

        @pl.when(i == 0)
        def _():
            dqn_ref[...] = jnp.zeros_like(dqn_ref)
            dkn_ref[...] = jnp.zeros_like(dkn_ref)

        def rope_bwd(d):
            return d * c + pltpu.roll(d * s1, 112, 1) + pltpu.roll(d * s2, 16, 1)

        def norm_bwd(t, g, dy):
            rinv = _head_stats(t, lo)
            n = t * rinv
            dn = dy * g
            prod = dn * n
            m_lo = jnp.sum(jnp.where(lo, prod, 0.0), axis=1, keepdims=True)
            m_hi = jnp.sum(jnp.where(lo, 0.0, prod), axis=1, keepdims=True)
            mean = jnp.where(lo, m_lo, m_hi) * (1.0 / HEAD_DIM)
            return rinv * (dn - n * mean), jnp.sum(dy * n, axis=0, keepdims=True)

        for j in range(12):
            t = qkv_ref[:, j * 128:(j + 1) * 128]
            if j < 4:
                d, dg = norm_bwd(t, qn_ref[...], rope_bwd(dq_ref[:, j * 128:(j + 1) * 128] * Q_SCALE))
                dqn_ref[...] += dg
            elif j == COL_KA:
                d, dg = norm_bwd(t, kn_ref[...], rope_bwd(dkv_ref[:, 0:128]))
                dkn_ref[...] += dg
            elif j == COL_VA:
                d = dkv_ref[:, 128:256]
            elif j < 10:
                d = rope_bwd(dq_ref[:, (j - 2) * 128:(j - 1) * 128] * Q_SCALE)
            elif j == COL_KB:
                d = rope_bwd(dkv_ref[:, 256:384])
            else:
                d = dkv_ref[:, 384:512]
            out_ref[:, j * 128:(j + 1) * 128] = d.astype(BF16)

    return pl.pallas_call(
        body, name=name, grid=(rt.n_tiles,),
        in_specs=[_row_spec(rt, 1024), _row_spec(rt, 512), _row_spec(rt, IN_COLS)] + [_row_spec(rt, 128)] * 3 + [_vec_spec(128)] * 2,
        out_specs=[_row_spec(rt, IN_COLS), _vec_spec(128), _vec_spec(128)],
        out_shape=[jax.ShapeDtypeStruct((rt.rows, IN_COLS), BF16), jax.ShapeDtypeStruct((1, 128), F32), jax.ShapeDtypeStruct((1, 128), F32)],
        compiler_params=_params(("arbitrary",)),
    )(dq, dkv, qkv, *tables, qn, kn)


def _stack_heads(x, kvi):
    x = x.astype(F32)
    tq = x.shape[0]
    lane = lax.broadcasted_iota(jnp.int32, (tq, 128), 1)
    keep = lane < HEAD_DIM if kvi == 0 else lane >= HEAD_DIM
    parts = []
    for p in range(2):
        pair = x[:, p * 128:(p + 1) * 128]
        swapped = pltpu.roll(pair, HEAD_DIM, 1)
        lo_head, hi_head = (pair, swapped) if kvi == 0 else (swapped, pair)
        parts += [jnp.where(keep, lo_head, 0.0), jnp.where(keep, hi_head, 0.0)]
    return jnp.concatenate(parts, axis=0).astype(BF16)


def _unstack_heads(o4, kvi):
    tq = o4.shape[0] // GROUP
    lane = lax.broadcasted_iota(jnp.int32, (tq, 128), 1)
    outs = []
    for p in range(2):
        r_lo, r_hi = o4[(2 * p) * tq:(2 * p + 1) * tq], o4[(2 * p + 1) * tq:(2 * p + 2) * tq]
        if kvi == 0:
            lo, hi = r_lo, pltpu.roll(r_hi, HEAD_DIM, 1)
        else:
            lo, hi = pltpu.roll(r_lo, HEAD_DIM, 1), r_hi
        outs.append(jnp.where(lane < HEAD_DIM, lo, hi))
    return jnp.concatenate(outs, axis=1)


def _per_head(shape, axis, tq, values):
    head = lax.broadcasted_iota(jnp.int32, shape, axis) // tq
    out = jnp.zeros(shape, F32)
    for g in range(GROUP):
        out = jnp.where(head == g, values[g], out)
    return out


def _softmax_fwd(qs, sources, sink_col):
    logits = []
    for k, _, mask in sources:
        s = lax.dot_general(qs, k, (((1,), (1,)), ((), ())), preferred_element_type=F32)
        logits.append(s if mask is None else jnp.where(mask, s, NEG_BIG))
    m = functools.reduce(jnp.maximum, [jnp.max(s, axis=1, keepdims=True) for s in logits])
    if sink_col is not None:
        m = jnp.maximum(m, sink_col)
    l = jnp.zeros_like(m)
    o = jnp.zeros((qs.shape[0], 128), F32)
    for s, (_, v, _) in zip(logits, sources):
        p = jnp.exp(s - m)
        l = l + jnp.sum(p, axis=1, keepdims=True)
        o = o + jnp.dot(p.astype(BF16), v, preferred_element_type=F32)
    if sink_col is not None:
        l = l + jnp.exp(sink_col - m)
    return o / l


def _softmax_bwd(qs, dos, sources, sink_row):
    logits = []
    for k, _, mask in sources:
        s = lax.dot_general(k, qs, (((1,), (1,)), ((), ())), preferred_element_type=F32)
        logits.append(s if mask is None else jnp.where(mask, s, NEG_BIG))
    m = functools.reduce(jnp.maximum, [jnp.max(s, axis=0, keepdims=True) for s in logits])
    if sink_row is not None:
        m = jnp.maximum(m, sink_row)
    ps = [jnp.exp(s - m) for s in logits]
    l = functools.reduce(jnp.add, [jnp.sum(p, axis=0, keepdims=True) for p in ps])
    if sink_row is not None:
        l = l + jnp.exp(sink_row - m)
    inv = 1.0 / l
    ps = [p * inv for p in ps]
    dps = [lax.dot_general(v, dos, (((1,), (1,)), ((), ())), preferred_element_type=F32) for _, v, _ in sources]
    delta = functools.reduce(jnp.add, [jnp.sum(p * dp, axis=0, keepdims=True) for p, dp in zip(ps, dps)])
    dq = jnp.zeros((qs.shape[0], 128), F32)
    dks, dvs = [], []
    for p, dp, (k, _, _) in zip(ps, dps, sources):
        ds = (p * (dp - delta)).astype(BF16)
        dvs.append(jnp.dot(p.astype(BF16), dos, preferred_element_type=F32))
        dks.append(jnp.dot(ds, qs, preferred_element_type=F32))
        dq = dq + lax.dot_general(ds, k, (((0,), (0,)), ((), ())), preferred_element_type=F32)
    dsink = None if sink_row is None else -(jnp.exp(sink_row - m) * inv) * delta
    return dq, dks, dvs, dsink


def _band(qi, tq, seq):
    span = tq + 2 * WINDOW
    start = pl.multiple_of(jnp.clip(qi * tq - WINDOW, 0, seq - span), 64)
    return start, span


def _band_mask(qi, tq, start, span, query_axis):
    shape = (GROUP * tq, span) if query_axis == 0 else (span, GROUP * tq)
    qpos = qi * tq + lax.broadcasted_iota(jnp.int32, shape, query_axis) % tq
    kpos = start + lax.broadcasted_iota(jnp.int32, shape, 1 - query_axis)
    return jnp.abs(kpos - qpos) <= WINDOW


def _qkv_specs(rt, tq, q_row, ctx_row, with_latent):
    specs = [pl.BlockSpec((tq, 256), functools.partial(lambda b, i, col: (q_row(b, i), col), col=col)) for col in (0, 1, 3, 4)]
    if with_latent:
        specs += [pl.BlockSpec((rt.seq, 128), functools.partial(lambda b, i, col: (b, col), col=col))
                  for col in (COL_KA, COL_VA, COL_KB, COL_VB)]
    specs += [pl.BlockSpec((rt.ctx, 128), functools.partial(lambda b, i, col: (ctx_row(b), col), col=col))
              for col in (COL_KA, COL_VA, COL_KB, COL_VB)]
    return specs


def _attn_fwd(rt, qkvp, sink, latent, name):
    seq, ctx, nb = rt.seq, rt.ctx, rt.nb
    tq = 128 if latent else ctx
    nq = seq // tq if latent else 1
    ctx_blk0 = rt.n_lat // ctx
    q_row = (lambda b, i: b * nq + i) if latent else (lambda b, i: ctx_blk0 + b)
    n_q_rows = rt.n_lat if latent else rt.n_ctx

    def body(sink_ref, qa0, qa1, qb0, qb1, *rest):
        if latent:
            kal, val, kbl, vbl, kac, vac, kbc, vbc, o_ref = rest
        else:
            kac, vac, kbc, vbc, o_ref = rest
        qi = pl.program_id(1)
        for kvi, (qa, qb) in enumerate(((qa0, qb0), (qa1, qb1))):
            src_a = [(kac[...], vac[...], None)]
            src_b = [(kbc[...], vbc[...], None)]
            if latent:
                src_a.append((kal[...], val[...], None))
                start, span = _band(qi, tq, seq)
                src_b.append((kbl[pl.ds(start, span), :], vbl[pl.ds(start, span), :], _band_mask(qi, tq, start, span, 0)))
            oa = _softmax_fwd(_stack_heads(qa[...], kvi), src_a, None)
            o_ref[:, kvi * 256:(kvi + 1) * 256] = _unstack_heads(oa, kvi).astype(BF16)
            sink_col = _per_head((GROUP * tq, 1), 0, tq, [sink_ref[kvi * GROUP + g] for g in range(GROUP)])
            ob = _softmax_fwd(_stack_heads(qb[...], kvi), src_b, sink_col)
            o_ref[:, 512 + kvi * 256:512 + (kvi + 1) * 256] = _unstack_heads(ob, kvi).astype(BF16)

    specs = _qkv_specs(rt, tq, q_row, lambda b: ctx_blk0 + b, latent)
    return pl.pallas_call(
        body, name=name, grid=(nb, nq),
        in_specs=[pl.BlockSpec(memory_space=pltpu.SMEM)] + specs,
        out_specs=pl.BlockSpec((tq, 1024), lambda b, i: (b * nq + i, 0)),
        out_shape=jax.ShapeDtypeStruct((n_q_rows, 1024), BF16),
        compiler_params=_params(("parallel", "parallel")),
    )(sink, *([qkvp] * len(specs)))


def _attn_bwd(rt, qkvp, do, sink, latent, name):
    seq, ctx, nb = rt.seq, rt.ctx, rt.nb
    tq = 128 if latent else ctx
    nq = seq // tq if latent else 1
    ctx_blk0 = rt.n_lat // ctx
    q_row = (lambda b, i: b * nq + i) if latent else (lambda b, i: ctx_blk0 + b)
    n_q_rows = rt.n_lat if latent else rt.n_ctx

    def body(sink_ref, qa0, qa1, qb0, qb1, *rest):
        if latent:
            kal, val, kbl, vbl, kac, vac, kbc, vbc, do_ref, dq_ref, dl_ref, dc_ref, dsink_ref = rest
        else:
            kac, vac, kbc, vbc, do_ref, dq_ref, dc_ref, dsink_ref = rest
        b, qi = pl.program_id(0), pl.program_id(1)

        @pl.when(jnp.logical_and(b == 0, qi == 0))
        def _():
            dsink_ref[...] = jnp.zeros_like(dsink_ref)

        @pl.when(qi == 0)
        def _():
            dc_ref[...] = jnp.zeros_like(dc_ref)
            if latent:
                dl_ref[...] = jnp.zeros_like(dl_ref)

        head_row = lax.broadcasted_iota(jnp.int32, (8, 128), 0)
        for kvi, (qa, qb) in enumerate(((qa0, qb0), (qa1, qb1))):
            src = [(kac[...], vac[...], None)]
            if latent:
                src.append((kal[...], val[...], None))
            dq4, dks, dvs, _ = _softmax_bwd(_stack_heads(qa[...], kvi), _stack_heads(do_ref[:, kvi * 256:(kvi + 1) * 256], kvi), src, None)
            dq_ref[:, kvi * 256:(kvi + 1) * 256] = _unstack_heads(dq4, kvi)
            dc_ref[:, 0:128] += dks[0]
            dc_ref[:, 128:256] += dvs[0]
            if latent:
                dl_ref[:, 0:128] += dks[1]
                dl_ref[:, 128:256] += dvs[1]
            src = [(kbc[...], vbc[...], None)]
            if latent:
                start, span = _band(qi, tq, seq)
                src.append((kbl[pl.ds(start, span), :], vbl[pl.ds(start, span), :], _band_mask(qi, tq, start, span, 1)))
            sink_row = _per_head((1, GROUP * tq), 1, tq, [sink_ref[kvi * GROUP + g] for g in range(GROUP)])
            dq4, dks, dvs, dsink = _softmax_bwd(_stack_heads(qb[...], kvi),
                                                _stack_heads(do_ref[:, 512 + kvi * 256:512 + (kvi + 1) * 256], kvi), src, sink_row)
            dq_ref[:, 512 + kvi * 256:512 + (kvi + 1) * 256] = _unstack_heads(dq4, kvi)
            dc_ref[:, 256:384] += dks[0]
            dc_ref[:, 384:512] += dvs[0]
            if latent:
                dl_ref[pl.ds(start, span), 256:384] += dks[1]
                dl_ref[pl.ds(start, span), 384:512] += dvs[1]
            head = lax.broadcasted_iota(jnp.int32, (1, GROUP * tq), 1) // tq
            upd = jnp.zeros((8, 128), F32)
            for g in range(GROUP):
                upd = jnp.where(head_row == kvi * GROUP + g, jnp.sum(jnp.where(head == g, dsink, 0.0)), upd)
            dsink_ref[...] += upd

    specs = _qkv_specs(rt, tq, q_row, lambda b: ctx_blk0 + b, latent)
    specs.append(pl.BlockSpec((tq, 1024), lambda b, i: (q_row(b, i), 0)))
    out_specs = [pl.BlockSpec((tq, 1024), lambda b, i: (b * nq + i, 0))]
    out_shape = [jax.ShapeDtypeStruct((n_q_rows, 1024), F32)]
    if latent:
        out_specs.append(pl.BlockSpec((seq, 512), lambda b, i: (b, 0)))
        out_shape.append(jax.ShapeDtypeStruct((rt.n_lat, 512), F32))
    out_specs += [pl.BlockSpec((ctx, 512), lambda b, i: (b, 0)), pl.BlockSpec((8, 128), lambda b, i: (0, 0))]
    out_shape += [jax.ShapeDtypeStruct((rt.n_ctx, 512), F32), jax.ShapeDtypeStruct((8, 128), F32)]
    return pl.pallas_call(
        body, name=name, grid=(nb, nq),
        in_specs=[pl.BlockSpec(memory_space=pltpu.SMEM)] + specs,
        out_specs=out_specs, out_shape=out_shape,
        compiler_params=_params(("arbitrary", "arbitrary")),
    )(sink, *([qkvp] * (len(specs) - 1)), do)


def _silu(x):
    return x / (1.0 + jnp.exp(-x))


def _ada_fwd(cond, w_half, b_half, name):
    rows = cond.shape[0]
    cols = w_half.shape[2]

    def body(c_ref, w_ref, b_ref, x_ref, o_ref):
        xs = _silu(c_ref[...]).astype(BF16)
        x_ref[...] = xs
        for l in range(DEPTH):
            o_ref[l] = jnp.dot(xs, w_ref[l].astype(BF16), preferred_element_type=F32) + b_ref[l]

    return pl.pallas_call(
        body, name=name,
        out_shape=[jax.ShapeDtypeStruct((rows, D_MODEL), BF16), jax.ShapeDtypeStruct((DEPTH, rows, cols), F32)],
        compiler_params=pltpu.CompilerParams(vmem_limit_bytes=VMEM_LIMIT),
    )(cond, w_half, b_half)


def _dev_sum(x, name):
    _, r, c = x.shape

    def body(x_ref, o_ref):
        v = x_ref[0]
        for d in range(1, N_DEV):
            v = v + x_ref[d]
        o_ref[...] = v

    return pl.pallas_call(body, name=name, out_shape=jax.ShapeDtypeStruct((r, c), F32))(x)


def _c_ctx_grad(parts, c_ctx, name):
    def body(p_ref, c_ref, o_ref):
        v = p_ref[0, 0:1, :]
        for d in range(1, N_DEV):
            v = v + p_ref[d, 0:1, :]
        c = c_ref[...]
        sg = 1.0 / (1.0 + jnp.exp(-c))
        o_ref[...] = v * (sg * (1.0 + c * (1.0 - sg)))

    return pl.pallas_call(body, name=name, out_shape=jax.ShapeDtypeStruct((1, D_MODEL), F32))(parts, c_ctx)


def _adamw(w, g, m, v, name):
    r, c = w.shape
    tr = _pick(r, (256, 128, 64, 32, 24, 16, 8))
    c1 = 1.0 / (1.0 - ADAM_B1 ** ADAM_STEP)
    c2 = 1.0 / (1.0 - ADAM_B2 ** ADAM_STEP)

    def body(w_ref, g_ref, m_ref, v_ref, d_ref, nm_ref, nv_ref):
        g_ = g_ref[...]
        nm = ADAM_B1 * m_ref[...] + (1.0 - ADAM_B1) * g_
        nv = ADAM_B2 * v_ref[...] + (1.0 - ADAM_B2) * (g_ * g_)
        d_ref[...] = -ADAM_LR * ((nm * c1) / (jnp.sqrt(nv * c2) + ADAM_EPS) + ADAM_WD * w_ref[...])
        nm_ref[...] = nm
        nv_ref[...] = nv

    spec = pl.BlockSpec((tr, c), lambda i: (i, 0))
    return pl.pallas_call(
        body, name=name, grid=(r // tr,), in_specs=[spec] * 4, out_specs=[spec] * 3,
        out_shape=[jax.ShapeDtypeStruct((r, c), F32)] * 3,
        compiler_params=_params(("parallel",)),
    )(w, g, m, v)


def _relu2(acc):
    r = jnp.maximum(acc, 0.0)
    return acc, r * r


def _relu2_bwd(acc, a):
    return (acc * (2.0 * jnp.maximum(a, 0.0)),)


def _local_step(x, ctx, target, mods, gam, qn, kn, sink, w_in, w_out, w_up, w_down):
    nb, seq, _ = x.shape
    rt = _Rows(nb, seq, ctx.shape[1])
    tables = _rope_tables(rt)
    h = jnp.concatenate([x.reshape(rt.n_lat, D_MODEL), ctx.reshape(rt.n_ctx, D_MODEL)], axis=0)
    saved = []
    for l in range(DEPTH):
        g_pre_mix, g_post_mix, g_pre_mlp, g_post_mlp = gam[l]
        u = _norm_mod_fwd(rt, h, g_pre_mix, mods[l], 0, 1, f"norm_mix_fwd{l}")
        qkv = _matmul(u, w_in[l], "nn", [F32], f"proj_in_fwd{l}")
        qkvp = _prep_fwd(rt, qkv, tables, qn[l], kn[l], f"prep_fwd{l}")
        o = jnp.concatenate([_attn_fwd(rt, qkvp, sink[l], True, f"attn_lat_fwd{l}"),
                             _attn_fwd(rt, qkvp, sink[l], False, f"attn_ctx_fwd{l}")], axis=0)
        mix = _matmul(o, w_out[l], "nn", [F32], f"proj_out_fwd{l}")
        h1 = _post_norm_fwd(rt, h, mix, g_post_mix, mods[l], 2, f"post_mix_fwd{l}")
        u2 = _norm_mod_fwd(rt, h1, g_pre_mlp, mods[l], 3, 4, f"norm_mlp_fwd{l}")
        a, r = _matmul(u2, w_up[l], "nn", [F32, BF16], f"mlp_up_fwd{l}", epilogue=_relu2)
        y = _matmul(r, w_down[l], "nn", [F32], f"mlp_down_fwd{l}")
        h2 = _post_norm_fwd(rt, h1, y, g_post_mlp, mods[l], 5, f"post_mlp_fwd{l}")
        saved.append((h, u, qkv, qkvp, o, mix, h1, u2, a, r, y))
        h = h2

    dh, sq = _loss_grad(rt, h, target.reshape(rt.n_lat, D_MODEL), "loss_grad")

    layer_grads = [None] * DEPTH
    for l in reversed(range(DEPTH)):
        g_pre_mix, g_post_mix, g_pre_mlp, g_post_mlp = gam[l]
        h0, u, qkv, qkvp, o, mix, h1, u2, a, r, y = saved[l]
        dy, d_gate_m, d_g_post_mlp = _post_norm_bwd(rt, dh, y, g_post_mlp, mods[l], 5, f"post_mlp_bwd{l}")
        da = _matmul(dy, w_down[l], "nt", [BF16], f"mlp_down_bwd{l}", epilogue=_relu2_bwd, extras=(a,))
        dw_down = _matmul(r, dy, "tn", [F32], f"mlp_down_wgrad{l}", tk=512)
        du2 = _matmul(da, w_up[l], "nt", [F32], f"mlp_up_bwd{l}")
        dw_up = _matmul(u2, da, "tn", [F32], f"mlp_up_wgrad{l}", tk=512)
        dh1, d_sh_m, d_sc_m, d_g_pre_mlp = _norm_mod_bwd(rt, du2, h1, dh, g_pre_mlp, mods[l], 4, f"norm_mlp_bwd{l}")
        dmix, d_gate_a, d_g_post_mix = _post_norm_bwd(rt, dh1, mix, g_post_mix, mods[l], 2, f"post_mix_bwd{l}")
        do = _matmul(dmix, w_out[l], "nt", [BF16], f"proj_out_bwd{l}")
        dw_out = _matmul(o, dmix, "tn", [F32], f"proj_out_wgrad{l}", tk=512)
        dq_l, dkv_l, dkv_c1, dsink1 = _attn_bwd(rt, qkvp, do, sink[l], True, f"attn_lat_bwd{l}")
        dq_c, dkv_c2, dsink2 = _attn_bwd(rt, qkvp, do, sink[l], False, f"attn_ctx_bwd{l}")
        dq = jnp.concatenate([dq_l, dq_c], axis=0)
        dkv = jnp.concatenate([dkv_l, dkv_c1 + dkv_c2], axis=0)
        dqkv, dqn, dkn = _prep_bwd(rt, dq, dkv, qkv, tables, qn[l], kn[l], f"prep_bwd{l}")
        du = _matmul(dqkv, w_in[l], "nt", [F32], f"proj_in_bwd{l}")
        dw_in = _matmul(u, dqkv, "tn", [F32], f"proj_in_wgrad{l}", tk=512)
        dh, d_sh_a, d_sc_a, d_g_pre_mix = _norm_mod_bwd(rt, du, h0, dh1, g_pre_mix, mods[l], 1, f"norm_mix_bwd{l}")
        dmod = jnp.concatenate([d_sh_a, d_sc_a, d_gate_a, d_sh_m, d_sc_m, d_gate_m], axis=1)
        layer_grads[l] = dict(
            w_in=dw_in, w_out=dw_out, w_up=dw_up, w_down=dw_down, mod=dmod,
            gammas=jnp.concatenate([d_g_pre_mix, d_g_post_mix, d_g_pre_mlp, d_g_post_mlp], axis=0),
            q_norm=dqn, k_norm=dkn, sink=(dsink1 + dsink2)[:, 0])
    return sq, dh[:rt.n_lat].reshape(nb, seq, D_MODEL), layer_grads


SMALL_ROWS = 48


def kernel(x, c, ctx, c_ctx, w_ada, b_ada, g_pre_mix, g_post_mix, g_pre_mlp, g_post_mlp, w_in, q_norm, k_norm, sink, w_out, w_up, w_down, loss_target, m_c_ctx, m_w_ada, m_b_ada, m_g_pre_mix, m_g_post_mix, m_g_pre_mlp, m_g_post_mlp, m_w_in, m_q_norm, m_k_norm, m_sink, m_w_out, m_w_up, m_w_down, v_c_ctx, v_w_ada, v_b_ada, v_g_pre_mix, v_g_post_mix, v_g_pre_mlp, v_g_post_mlp, v_w_in, v_q_norm, v_k_norm, v_sink, v_w_out, v_w_up, v_w_down):
    nb = x.shape[0]
    ix, iy, ic = lax.axis_index("x"), lax.axis_index("y"), lax.axis_index("c")
    chip = 2 * ix + iy
    dev = 2 * chip + ic
    ada_cols = w_ada.shape[2] // 2

    c_all = _all_gather(c.reshape(8, (nb * D_MODEL) // 8), "gather_c", False).reshape(N_DEV * nb, D_MODEL)
    n_cond = N_DEV * nb + 1
    cond_rows = 8 * ((n_cond + 15) // 16) * 2
    cond = jnp.concatenate([c_all, c_ctx[None, :], jnp.zeros((cond_rows - n_cond, D_MODEL), F32)], axis=0)
    w_ada_half = lax.dynamic_slice_in_dim(w_ada, ic * ada_cols, ada_cols, 2)
    b_ada_half = lax.dynamic_slice_in_dim(b_ada, dev * ada_cols, ada_cols, 1)[:, None, :]
    x_ada, mod_part = _ada_fwd(cond, w_ada_half, b_ada_half, "ada_fwd")
    mod_g = _all_gather(mod_part.reshape(DEPTH * cond_rows, ada_cols), "gather_mod", False)
    mod_all = mod_g.reshape(N_DEV, DEPTH, cond_rows, ada_cols).transpose(1, 2, 0, 3).reshape(DEPTH, cond_rows, N_MOD * D_MODEL)
    mods = []
    for l in range(DEPTH):
        mine = lax.dynamic_slice_in_dim(mod_all[l], dev * nb, nb, 0)
        mods.append(jnp.concatenate([mine, mod_all[l, n_cond - 1:n_cond]], axis=0).reshape(nb + 1, N_MOD, D_MODEL))

    packed = _pack_local_half(w_in, w_out, w_up, w_down, ic)
    gathered = _all_gather(packed, "gather_weights", True)
    fw_in, fw_out, fw_up, fw_down = _unpack_gathered(gathered)

    gam = [(g_pre_mix[l][None], g_post_mix[l][None], g_pre_mlp[l][None], g_post_mlp[l][None]) for l in range(DEPTH)]
    qn = [jnp.tile(q_norm[l], 2)[None] for l in range(DEPTH)]
    kn = [jnp.tile(k_norm[l], 2)[None] for l in range(DEPTH)]
    sq, grad_x, lg = _local_step(x, ctx, loss_target, mods, gam, qn, kn, [sink[l] for l in range(DEPTH)],
                                 fw_in, fw_out, fw_up, fw_down)
    loss = lax.psum(0.5 * jnp.sum(sq) / D_MODEL, ("x", "y", "c"))

    p_halves = _pack_full_grads(*[[lg[l][k] for l in range(DEPTH)] for k in ("w_in", "w_out", "w_up", "w_down")])
    r1 = _sibling_exchange(p_halves, "grad_pair_exchange", True)
    a32, a16 = _pair_sum(p_halves, r1, ic.reshape(1).astype(jnp.int32), "grad_pair_sum")
    r2 = _chip_exchange(a16, "grad_chip_exchange")
    g_half = _owner_sum(a32, r2, chip.reshape(1).astype(jnp.int32), "grad_owner_sum")
    g_sib = _sibling_exchange(g_half, "grad_half_exchange", False)
    g_halves = jnp.where(ic == 0, jnp.stack([g_half, g_sib]), jnp.stack([g_sib, g_half]))
    grad_w_in, grad_w_out, grad_w_up, grad_w_down = _unpack_shard(g_halves)

    def lane_pad(v):
        return jnp.pad(v, (0, D_MODEL - v.shape[0]))[None]

    head_rows = [lane_pad(jnp.concatenate([lg[l]["q_norm"][0], lg[l]["k_norm"][0], lg[l]["sink"]])) for l in range(DEPTH)]
    small = jnp.concatenate([lg[l]["mod"].reshape((nb + 1) * N_MOD, D_MODEL) for l in range(DEPTH)]
                            + [lg[l]["gammas"] for l in range(DEPTH)] + head_rows, axis=0)
    small = jnp.pad(small, ((0, SMALL_ROWS - small.shape[0]), (0, 0)))
    small_g = _all_gather(small, "gather_small", False).reshape(N_DEV, SMALL_ROWS, D_MODEL)
    tot = _dev_sum(small_g, "small_sum")
    mod_rows = (nb + 1) * N_MOD
    o_gam, o_head = DEPTH * mod_rows, DEPTH * mod_rows + 4 * DEPTH
    grad_g = [jnp.stack([tot[o_gam + 4 * l + j] for l in range(DEPTH)]) for j in range(4)]
    grad_q_norm = jnp.stack([tot[o_head + l, 0:64] + tot[o_head + l, 64:128] for l in range(DEPTH)])
    grad_k_norm = jnp.stack([tot[o_head + l, 128:192] + tot[o_head + l, 192:256] for l in range(DEPTH)])
    grad_sink = jnp.stack([tot[o_head + l, 256:264] for l in range(DEPTH)])

    dmod_ex, dmod_ctx = [], []
    for l in range(DEPTH):
        ex = small_g[:, l * mod_rows:l * mod_rows + nb * N_MOD].reshape(N_DEV * nb, N_MOD * D_MODEL)
        cx = tot[l * mod_rows + nb * N_MOD:(l + 1) * mod_rows].reshape(1, N_MOD * D_MODEL)
        dmod_ex.append(ex)
        dmod_ctx.append(cx)
    grad_b_ada = jnp.stack([jnp.sum(dmod_ex[l], axis=0) + dmod_ctx[l][0] for l in range(DEPTH)])
    shard_cols = w_ada.shape[2]
    grad_w_ada, dcc_parts = [], []
    for l in range(DEPTH):
        dm = jnp.concatenate([dmod_ex[l], dmod_ctx[l], jnp.zeros((cond_rows - n_cond, N_MOD * D_MODEL), F32)], axis=0)
        dm_shard = lax.dynamic_slice_in_dim(dm, chip * shard_cols, shard_cols, 1).astype(BF16)
        grad_w_ada.append(_matmul(x_ada, dm_shard, "tn", [F32], f"ada_wgrad{l}"))
        dcx = lax.dynamic_slice_in_dim(dmod_ctx[l], dev * ada_cols, ada_cols, 1)
        dcx = jnp.pad(dcx, ((0, 15), (0, 0))).astype(BF16)
        dcc_parts.append(_matmul(dcx, w_ada_half[l].astype(BF16), "nt", [F32], f"ada_cond_bwd{l}"))
    grad_w_ada = jnp.stack(grad_w_ada)
    dcc = (dcc_parts[0] + dcc_parts[1])[0:8]
    dcc_g = _all_gather(dcc, "gather_cond_grad", False).reshape(N_DEV, 8, D_MODEL)
    grad_c_ctx = _c_ctx_grad(dcc_g, c_ctx[None], "c_ctx_grad")[0]

    def step(w, g, m, v, name):
        shape = w.shape
        cols = shape[-1]
        d, nm, nv = _adamw(w.reshape(-1, cols), g.reshape(-1, cols), m.reshape(-1, cols), v.reshape(-1, cols), name)
        return d.reshape(shape), nm.reshape(shape), nv.reshape(shape)

    small_names = ["c_ctx", "b_ada", "g_pre_mix", "g_post_mix", "g_pre_mlp", "g_post_mlp", "q_norm", "k_norm", "sink"]
    small_w = [c_ctx, b_ada, g_pre_mix, g_post_mix, g_pre_mlp, g_post_mlp, q_norm, k_norm, sink]
    small_gr = [grad_c_ctx, grad_b_ada] + grad_g + [grad_q_norm, grad_k_norm, grad_sink]
    small_m = [m_c_ctx, m_b_ada, m_g_pre_mix, m_g_post_mix, m_g_pre_mlp, m_g_post_mlp, m_q_norm, m_k_norm, m_sink]
    small_v = [v_c_ctx, v_b_ada, v_g_pre_mix, v_g_post_mix, v_g_pre_mlp, v_g_post_mlp, v_q_norm, v_k_norm, v_sink]
    sizes = [int(np.prod(w.shape)) for w in small_w]
    total = sum(sizes)
    flat_rows = 8 * ((total + 8 * D_MODEL - 1) // (8 * D_MODEL))

    def flat(arrs, fill):
        f = jnp.concatenate([a.reshape(-1) for a in arrs])
        return jnp.concatenate([f, jnp.full((flat_rows * D_MODEL - total,), fill, F32)]).reshape(flat_rows, D_MODEL)

    sd, snm, snv = _adamw(flat(small_w, 0.0), flat(small_gr, 0.0), flat(small_m, 0.0), flat(small_v, 1.0), "adamw_small")

    def unflat(f):
        f = f.reshape(-1)
        out, off = [], 0
        for w, n in zip(small_w, sizes):
            out.append(f[off:off + n].reshape(w.shape))
            off += n
        return out

    small_d, small_nm, small_nv = unflat(sd), unflat(snm), unflat(snv)
    res = {n: (g, d, nm, nv) for n, g, d, nm, nv in zip(small_names, small_gr, small_d, small_nm, small_nv)}
    res["w_ada"] = (grad_w_ada, *step(w_ada, grad_w_ada, m_w_ada, v_w_ada, "adamw_w_ada"))
    res["w_in"] = (grad_w_in, *step(w_in, grad_w_in, m_w_in, v_w_in, "adamw_w_in"))
    res["w_out"] = (grad_w_out, *step(w_out, grad_w_out, m_w_out, v_w_out, "adamw_w_out"))
    res["w_up"] = (grad_w_up, *step(w_up, grad_w_up, m_w_up, v_w_up, "adamw_w_up"))
    res["w_down"] = (grad_w_down, *step(w_down, grad_w_down, m_w_down, v_w_down, "adamw_w_down"))

    order = ["c_ctx", "w_ada", "b_ada", "g_pre_mix", "g_post_mix", "g_pre_mlp", "g_post_mlp", "w_in", "q_norm", "k_norm", "sink", "w_out", "w_up", "w_down"]
    return (loss, grad_x, *[res[n][0] for n in order], *[res[n][1] for n in order],
            *[res[n][2] for n in order], *[res[n][3] for n in order])
```

```python
import functools

import jax
import jax.numpy as jnp
import numpy as np
from jax import lax
from jax.experimental import pallas as pl
from jax.experimental.pallas import tpu as pltpu

F32 = jnp.float32
BF16 = jnp.bfloat16

D_MODEL = 1024
HEAD_DIM = 64
GROUP = 4
WINDOW = 128
N_MOD = 6
D_FF = 4 * D_MODEL
IN_COLS = 1536
GRID_W = 64
ROPE_THETA = 10000.0
EPS = 1e-6
NEG_BIG = -1e30
Q_SCALE = HEAD_DIM ** -0.5
DEPTH = 2
N_DEV = 8

ADAM_LR = 0.001
ADAM_B1 = 0.9
ADAM_B2 = 0.999
ADAM_EPS = 1e-08
ADAM_WD = 0.01
ADAM_STEP = 10

V7X_VMEM_BYTES = 64 * 1024 * 1024
VMEM_LIMIT = V7X_VMEM_BYTES - 8 * 1024 * 1024

MESH = pl.DeviceIdType.MESH
NT = (((1,), (1,)), ((), ()))
TN = (((0,), (0,)), ((), ()))

COL_KA, COL_VA, COL_KB, COL_VB = 4, 5, 10, 11

PACK_ORDER = (("up", 0), ("down", 0), ("up", 1), ("down", 1), ("out", 0), ("out", 1), ("in", 0), ("in", 1))
PACK_HEIGHT = {"up": 512, "down": 512, "out": 128, "in": 192}
PACK_OFF = {("up", 0): 0, ("down", 0): 512, ("up", 1): 1024, ("down", 1): 1536,
            ("out", 0): 2048, ("out", 1): 2176, ("in", 0): 2304, ("in", 1): 2496}
PACK_ROWS = 2688


def _pick(n, cands):
    for t in cands:
        if n % t == 0:
            return t
    raise ValueError(f"no tile for {n}")


def _params(sem):
    return pltpu.CompilerParams(dimension_semantics=sem, vmem_limit_bytes=VMEM_LIMIT)


def _all_gather(x, name, in_hbm):
    m_per, n = x.shape
    space = pl.ANY if in_hbm else pltpu.VMEM

    def body(x_ref, out_ref, send_sems, recv_sems, local_sem):
        x_, y_, c_ = lax.axis_index("x"), lax.axis_index("y"), lax.axis_index("c")
        me, sibling = (x_, y_, c_), (x_, y_, 1 - c_)
        chips = [(1 - x_, y_), (x_, 1 - y_), (1 - x_, 1 - y_)]

        def rows(px, py, pc):
            return out_ref.at[pl.ds((4 * px + 2 * py + pc) * m_per, m_per), :]

        def copy(k, block, to, src=None):
            return pltpu.make_async_remote_copy(
                src_ref=rows(*block) if src is None else src, dst_ref=rows(*block),
                send_sem=send_sems.at[k], recv_sem=recv_sems.at[k], device_id=to, device_id_type=MESH)

        mine = pltpu.make_async_copy(x_ref, rows(*me), local_sem)
        mine.start()
        first = [copy(0, me, sibling, src=x_ref)]
        first += [copy(1 + j, me, (*chip, c_), src=x_ref) for j, chip in enumerate(chips)]
        for cp in first:
            cp.start()
        passed = [copy(4 + j, (*chip, c_), sibling) for j, chip in enumerate(chips)]
        for j, chip in enumerate(chips):
            copy(1 + j, (*chip, c_), me).wait_recv()
            passed[j].start()
        copy(0, sibling, me).wait_recv()
        for j, chip in enumerate(chips):
            copy(4 + j, (*chip, 1 - c_), me).wait_recv()
        for cp in first + passed:
            cp.wait_send()
        mine.wait()

    return pl.pallas_call(
        body, name=name,
        out_shape=jax.ShapeDtypeStruct((N_DEV * m_per, n), x.dtype),
        in_specs=[pl.BlockSpec(memory_space=space)],
        out_specs=pl.BlockSpec(memory_space=space),
        scratch_shapes=[pltpu.SemaphoreType.DMA((7,)), pltpu.SemaphoreType.DMA((7,)), pltpu.SemaphoreType.DMA],
    )(x)


def _sibling_exchange(x, name, pick_other_half):
    out_shape = x.shape[1:] if pick_other_half else x.shape

    def body(x_ref, out_ref, send_sem, recv_sem):
        x_, y_, c_ = lax.axis_index("x"), lax.axis_index("y"), lax.axis_index("c")
        src = x_ref.at[1 - c_] if pick_other_half else x_ref
        cp = pltpu.make_async_remote_copy(src_ref=src, dst_ref=out_ref, send_sem=send_sem, recv_sem=recv_sem,
                                          device_id=(x_, y_, 1 - c_), device_id_type=MESH)
        cp.start()
        cp.wait()

    return pl.pallas_call(
        body, name=name,
        out_shape=jax.ShapeDtypeStruct(out_shape, x.dtype),
        in_specs=[pl.BlockSpec(memory_space=pl.ANY)],
        out_specs=pl.BlockSpec(memory_space=pl.ANY),
        scratch_shapes=[pltpu.SemaphoreType.DMA, pltpu.SemaphoreType.DMA],
    )(x)


def _chip_exchange(a, name):
    _, r, c = a.shape

    def body(a_ref, out_ref, send_sems, recv_sems):
        x_, y_, c_ = lax.axis_index("x"), lax.axis_index("y"), lax.axis_index("c")
        chips = [(1 - x_, y_), (x_, 1 - y_), (1 - x_, 1 - y_)]
        cps = [pltpu.make_async_remote_copy(src_ref=a_ref.at[2 * tx + ty], dst_ref=out_ref.at[j],
                                            send_sem=send_sems.at[j], recv_sem=recv_sems.at[j],
                                            device_id=(tx, ty, c_), device_id_type=MESH)
               for j, (tx, ty) in enumerate(chips)]
        for cp in cps:
            cp.start()
        for cp in cps:
            cp.wait()

    return pl.pallas_call(
        body, name=name,
        out_shape=jax.ShapeDtypeStruct((3, r, c), a.dtype),
        in_specs=[pl.BlockSpec(memory_space=pl.ANY)],
        out_specs=pl.BlockSpec(memory_space=pl.ANY),
        scratch_shapes=[pltpu.SemaphoreType.DMA((3,)), pltpu.SemaphoreType.DMA((3,))],
    )(a)


def _pair_sum(p_halves, r1, c_idx, name):
    _, nb, r, c = p_halves.shape
    tr = _pick(r, (384, 256, 128, 64, 32, 16))

    def body(s_ref, p_ref, r_ref, o32_ref, o16_ref):
        v = p_ref[...] + r_ref[...]
        o32_ref[...] = v
        o16_ref[...] = v.astype(BF16)

    grid_spec = pltpu.PrefetchScalarGridSpec(
        num_scalar_prefetch=1, grid=(nb, r // tr),
        in_specs=[pl.BlockSpec((None, None, tr, c), lambda j, i, s: (s[0], j, i, 0)),
                  pl.BlockSpec((None, tr, c), lambda j, i, s: (j, i, 0))],
        out_specs=[pl.BlockSpec((None, tr, c), lambda j, i, s: (j, i, 0)),
                   pl.BlockSpec((None, tr, c), lambda j, i, s: (j, i, 0))])
    return pl.pallas_call(
        body, name=name, grid_spec=grid_spec,
        out_shape=[jax.ShapeDtypeStruct((nb, r, c), F32), jax.ShapeDtypeStruct((nb, r, c), BF16)],
        compiler_params=_params(("arbitrary", "arbitrary")),
    )(c_idx, p_halves, r1)


def _owner_sum(a32, r2, k_idx, name):
    _, r, c = a32.shape
    tr = _pick(r, (384, 256, 128, 64, 32, 16))

    def body(s_ref, a_ref, r_ref, o_ref):
        v = a_ref[...]
        for j in range(3):
            v = v + r_ref[j].astype(F32)
        o_ref[...] = v

    grid_spec = pltpu.PrefetchScalarGridSpec(
        num_scalar_prefetch=1, grid=(r // tr,),
        in_specs=[pl.BlockSpec((None, tr, c), lambda i, s: (s[0], i, 0)),
                  pl.BlockSpec((3, tr, c), lambda i, s: (0, i, 0))],
        out_specs=pl.BlockSpec((tr, c), lambda i, s: (i, 0)))
    return pl.pallas_call(
        body, name=name, grid_spec=grid_spec,
        out_shape=jax.ShapeDtypeStruct((r, c), F32),
        compiler_params=_params(("arbitrary",)),
    )(k_idx, a32, r2)


def _pack_local_half(w_in_s, w_out_s, w_up_s, w_down_s, c_idx):
    parts = []
    for kind, l in PACK_ORDER:
        if kind == "up":
            p = lax.dynamic_slice_in_dim(w_up_s[l], c_idx * 512, 512, 0)
        elif kind == "down":
            p = lax.dynamic_slice_in_dim(w_down_s[l], c_idx * 512, 512, 0)
        elif kind == "in":
            p = lax.dynamic_slice_in_dim(w_in_s[l], c_idx * 512, 512, 0).reshape(192, 1024)
        else:
            p = lax.dynamic_slice_in_dim(w_out_s[l], c_idx * 128, 128, 0)
        parts.append(p.astype(BF16))
    return jnp.concatenate(parts, axis=0)


def _pack_in_grad(dw_in):
    return dw_in.reshape(2, 512, 4, 384).transpose(0, 2, 1, 3).reshape(2, 4, 192, 1024)


def _unpack_in_weight(wg, l):
    o = PACK_OFF[("in", l)]
    return wg.reshape(4, 2, PACK_ROWS, 1024)[:, :, o:o + 192].reshape(4, 2, 512, 384).transpose(1, 2, 0, 3).reshape(1024, IN_COLS)


def _unpack_shard(gh):
    w_in, w_out, w_up, w_down = [], [], [], []
    for l in range(DEPTH):
        o = PACK_OFF[("up", l)]
        w_up.append(gh[:, o:o + 512].reshape(1024, 1024))
        o = PACK_OFF[("down", l)]
        w_down.append(gh[:, o:o + 512].reshape(1024, 1024))
        o = PACK_OFF[("in", l)]
        w_in.append(gh[:, o:o + 192].reshape(1024, 384))
        o = PACK_OFF[("out", l)]
        w_out.append(gh[:, o:o + 128].reshape(256, 1024))
    return jnp.stack(w_in), jnp.stack(w_out), jnp.stack(w_up), jnp.stack(w_down)


class _Rows:
    def __init__(self, nb, seq, ctx):
        self.nb, self.seq, self.ctx = nb, seq, ctx
        self.n_lat, self.n_ctx = nb * seq, nb * ctx
        self.rows = self.n_lat + self.n_ctx
        self.tm = _pick(np.gcd(seq, self.n_ctx), (512, 256, 128))
        self.tiles_per_ex = seq // self.tm
        self.n_tiles = self.rows // self.tm
        self.n_lat_tiles = self.n_lat // self.tm
        self.groups = nb + 1

    def group(self, i):
        return jnp.minimum(i // self.tiles_per_ex, self.nb)

    def first_of_group(self, i):
        return jnp.logical_and(i % self.tiles_per_ex == 0, i <= self.n_lat_tiles)


def _mod_spec(rt):
    return pl.BlockSpec((1, N_MOD, D_MODEL), lambda i: (rt.group(i), 0, 0))


def _row_spec(rt, cols):
    return pl.BlockSpec((rt.tm, cols), lambda i: (i, 0))


def _vec_spec(cols):
    return pl.BlockSpec((1, cols), lambda i: (0, 0))


def _group_spec(rt):
    return pl.BlockSpec((1, 1, D_MODEL), lambda i: (rt.group(i), 0, 0))


def _gathered_spec(kind, l):
    h = PACK_HEIGHT[kind]
    return pl.BlockSpec((N_DEV, h, 1024), lambda i: (0, PACK_OFF[(kind, l)] // h, 0), pipeline_mode=pl.Buffered(1))


def _group_shape(rt):
    return jax.ShapeDtypeStruct((rt.groups, 1, D_MODEL), F32)


def _vec_shape(cols=D_MODEL):
    return jax.ShapeDtypeStruct((1, cols), F32)


def _rms_inv(v):
    return lax.rsqrt(jnp.mean(v * v, axis=-1, keepdims=True) + EPS)


def _norm_mod_val(h_, g_, mod_ref, i_shift, i_scale):
    n = h_ * _rms_inv(h_) * g_
    return n * (1.0 + mod_ref[0, i_scale:i_scale + 1, :]) + mod_ref[0, i_shift:i_shift + 1, :]


def _post_norm_val(h_, z_, g_, mod_ref, i_gate):
    return h_ + mod_ref[0, i_gate:i_gate + 1, :] * (z_ * _rms_inv(z_) * g_)


def _post_norm_bwd_val(dh_, z_, g_, gate):
    rinv = _rms_inv(z_)
    n0 = z_ * rinv
    dn = dh_ * gate * g_
    dz = rinv * (dn - n0 * jnp.mean(dn * n0, axis=-1, keepdims=True))
    return dz, jnp.sum(dh_ * n0 * g_, axis=0, keepdims=True), jnp.sum(dh_ * gate * n0, axis=0, keepdims=True)


def _norm_mod_bwd_val(du_, h_, g_, one_sc):
    rinv = _rms_inv(h_)
    n0 = h_ * rinv
    dn = du_ * g_ * one_sc
    dh = rinv * (dn - n0 * jnp.mean(dn * n0, axis=-1, keepdims=True))
    return (dh, jnp.sum(du_, axis=0, keepdims=True), jnp.sum(du_ * n0 * g_, axis=0, keepdims=True),
            jnp.sum(du_ * one_sc * n0, axis=0, keepdims=True))


def _accumulate(rt, i, group_pairs, global_pairs):
    @pl.when(rt.first_of_group(i))
    def _():
        for ref, _ in group_pairs:
            ref[...] = jnp.zeros_like(ref)

    @pl.when(i == 0)
    def _():
        for ref, _ in global_pairs:
            ref[...] = jnp.zeros_like(ref)

    for ref, val in group_pairs:
        ref[0] += val
    for ref, val in global_pairs:
        ref[...] += val


def _rope_tables(rt):
    pos = jnp.arange(rt.seq, dtype=jnp.int32)
    row_ids = (pos // GRID_W).astype(F32)
    col_ids = (pos % GRID_W).astype(F32)
    axis_dim = HEAD_DIM // 2
    inv = ROPE_THETA ** (-jnp.arange(0, axis_dim, 2, dtype=F32) / axis_dim)
    ang_r, ang_c = row_ids[:, None] * inv[None, :], col_ids[:, None] * inv[None, :]
    cr, sr, cc, sc = jnp.cos(ang_r), jnp.sin(ang_r), jnp.cos(ang_c), jnp.sin(ang_c)
    zero = jnp.zeros_like(sr)
    cos = jnp.concatenate([cr, cr, cc, cc], axis=1)
    s_lo = jnp.concatenate([zero, sr, zero, sc], axis=1)
    s_hi = jnp.concatenate([-sr, zero, -sc, zero], axis=1)

    def full(t, ctx_value):
        t = jnp.tile(t, (rt.nb, 2))
        return jnp.concatenate([t, jnp.full((rt.n_ctx, 128), ctx_value, F32)], axis=0)

    return full(cos, 1.0), full(s_lo, 0.0), full(s_hi, 0.0)


def _head_stats(t, lo):
    sq = t * t
    s_lo = jnp.sum(jnp.where(lo, sq, 0.0), axis=1, keepdims=True)
    s_hi = jnp.sum(jnp.where(lo, 0.0, sq), axis=1, keepdims=True)
    return lax.rsqrt(jnp.where(lo, s_lo, s_hi) * (1.0 / HEAD_DIM) + EPS)


def _prep_fwd_body(tm, qkv_ref, c, s1, s2, qn, kn, out_ref):
    lo = lax.broadcasted_iota(jnp.int32, (tm, 128), 1) < HEAD_DIM

    def rope(t):
        return t * c + pltpu.roll(t, 16, 1) * s1 + pltpu.roll(t, 112, 1) * s2

    for j in range(12):
        t = qkv_ref[:, j * 128:(j + 1) * 128]
        if j < 4:
            t = rope(t * _head_stats(t, lo) * qn) * Q_SCALE
        elif j == COL_KA:
            t = rope(t * _head_stats(t, lo) * kn)
        elif 6 <= j < 10:
            t = rope(t) * Q_SCALE
        elif j == COL_KB:
            t = rope(t)
        out_ref[:, j * 128:(j + 1) * 128] = t.astype(BF16)


def _prep_bwd_body(tm, dq_ref, dkv_ref, qkv_ref, c, s1, s2, qn, kn, out_ref):
    lo = lax.broadcasted_iota(jnp.int32, (tm, 128), 1) < HEAD_DIM

    def rope_bwd(d):
        return d * c + pltpu.roll(d * s1, 112, 1) + pltpu.roll(d * s2, 16, 1)

    def norm_bwd(t, g, dy):
        rinv = _head_stats(t, lo)
        n = t * rinv
        dn = dy * g
        prod = dn * n
        m_lo = jnp.sum(jnp.where(lo, prod, 0.0), axis=1, keepdims=True)
        m_hi = jnp.sum(jnp.where(lo, 0.0, prod), axis=1, keepdims=True)
        mean = jnp.where(lo, m_lo, m_hi) * (1.0 / HEAD_DIM)
        return rinv * (dn - n * mean), jnp.sum(dy * n, axis=0, keepdims=True)

    dqn = jnp.zeros((1, 128), F32)
    dkn = jnp.zeros((1, 128), F32)
    for j in range(12):
        t = qkv_ref[:, j * 128:(j + 1) * 128]
        if j < 4:
            d, dg = norm_bwd(t, qn, rope_bwd(dq_ref[:, j * 128:(j + 1) * 128] * Q_SCALE))
            dqn = dqn + dg
        elif j == COL_KA:
            d, dg = norm_bwd(t, kn, rope_bwd(dkv_ref[:, 0:128]))
            dkn = dkn + dg
        elif j == COL_VA:
            d = dkv_ref[:, 128:256]
        elif j < 10:
            d = rope_bwd(dq_ref[:, (j - 2) * 128:(j - 1) * 128] * Q_SCALE)
        elif j == COL_KB:
            d = rope_bwd(dkv_ref[:, 256:384])
        else:
            d = dkv_ref[:, 384:512]
        out_ref[:, j * 128:(j + 1) * 128] = d.astype(BF16)
    return dqn, dkn


def _in_fwd(rt, h, gamma, mod, w_in, tables, qn, kn, name):
    def body(h_ref, g_ref, mod_ref, w_ref, c_ref, s1_ref, s2_ref, qn_ref, kn_ref, u_ref, qkv_ref, qkvp_ref):
        u = _norm_mod_val(h_ref[...], g_ref[...], mod_ref, 0, 1).astype(BF16)
        u_ref[...] = u
        qkv_ref[...] = jnp.dot(u, w_ref[...], preferred_element_type=F32)
        _prep_fwd_body(rt.tm, qkv_ref, c_ref[...], s1_ref[...], s2_ref[...], qn_ref[...], kn_ref[...], qkvp_ref)

    return pl.pallas_call(
        body, name=name, grid=(rt.n_tiles,),
        in_specs=[_row_spec(rt, D_MODEL), _vec_spec(D_MODEL), _mod_spec(rt),
                  pl.BlockSpec((D_MODEL, IN_COLS), lambda i: (0, 0), pipeline_mode=pl.Buffered(1))]
        + [_row_spec(rt, 128)] * 3 + [_vec_spec(128)] * 2,
        out_specs=[_row_spec(rt, D_MODEL), _row_spec(rt, IN_COLS), _row_spec(rt, IN_COLS)],
        out_shape=[jax.ShapeDtypeStruct((rt.rows, D_MODEL), BF16), jax.ShapeDtypeStruct((rt.rows, IN_COLS), F32),
                   jax.ShapeDtypeStruct((rt.rows, IN_COLS), BF16)],
        compiler_params=_params(("parallel",)),
    )(h, gamma, mod, w_in, *tables, qn, kn)


def _in_bwd(rt, dq, dkv, qkv, tables, qn, kn, w_in, h, dres, mod, gamma, name):
    def body(dq_ref, dkv_ref, qkv_ref, c_ref, s1_ref, s2_ref, qn_ref, kn_ref, w_ref, h_ref, dres_ref, mod_ref, g_ref,
             dqkv_ref, dh_ref, dqn_ref, dkn_ref, dsh_ref, dsc_ref, dg_ref):
        i = pl.program_id(0)
        dqn, dkn = _prep_bwd_body(rt.tm, dq_ref, dkv_ref, qkv_ref, c_ref[...], s1_ref[...], s2_ref[...], qn_ref[...], kn_ref[...], dqkv_ref)
        du = lax.dot_general(dqkv_ref[...], w_ref[...], NT, preferred_element_type=F32)
        dh, dsh, dsc, dg = _norm_mod_bwd_val(du, h_ref[...], g_ref[...], 1.0 + mod_ref[0, 1:2, :])
        dh_ref[...] = dres_ref[...] + dh
        _accumulate(rt, i, [(dsh_ref, dsh), (dsc_ref, dsc)], [(dg_ref, dg), (dqn_ref, dqn), (dkn_ref, dkn)])

    return pl.pallas_call(
        body, name=name, grid=(rt.n_tiles,),
        in_specs=[_row_spec(rt, 1024), _row_spec(rt, 512), _row_spec(rt, IN_COLS)] + [_row_spec(rt, 128)] * 3 + [_vec_spec(128)] * 2
        + [pl.BlockSpec((D_MODEL, IN_COLS), lambda i: (0, 0), pipeline_mode=pl.Buffered(1)),
           _row_spec(rt, D_MODEL), _row_spec(rt, D_MODEL), _mod_spec(rt), _vec_spec(D_MODEL)],
        out_specs=[_row_spec(rt, IN_COLS), _row_spec(rt, D_MODEL), _vec_spec(128), _vec_spec(128),
                   _group_spec(rt), _group_spec(rt), _vec_spec(D_MODEL)],
        out_shape=[jax.ShapeDtypeStruct((rt.rows, IN_COLS), BF16), jax.ShapeDtypeStruct((rt.rows, D_MODEL), F32),
                   _vec_shape(128), _vec_shape(128), _group_shape(rt), _group_shape(rt), _vec_shape()],
        compiler_params=_params(("arbitrary",)),
    )(dq, dkv, qkv, *tables, qn, kn, w_in, h, dres, mod, gamma)


def _out_fwd(rt, o, wg, l, h, mod, g_post_mix, g_pre_mlp, name):
    def body(o_ref, w_ref, h_ref, mod_ref, gpost_ref, gpre_ref, mix_ref, h1_ref, u2_ref):
        mix = jnp.dot(o_ref[...], w_ref[...].reshape(D_MODEL, D_MODEL), preferred_element_type=F32)
        mix_ref[...] = mix
        h1 = _post_norm_val(h_ref[...], mix, gpost_ref[...], mod_ref, 2)
        h1_ref[...] = h1
        u2_ref[...] = _norm_mod_val(h1, gpre_ref[...], mod_ref, 3, 4).astype(BF16)

    return pl.pallas_call(
        body, name=name, grid=(rt.n_tiles,),
        in_specs=[_row_spec(rt, D_MODEL), _gathered_spec("out", l), _row_spec(rt, D_MODEL), _mod_spec(rt),
                  _vec_spec(D_MODEL), _vec_spec(D_MODEL)],
        out_specs=[_row_spec(rt, D_MODEL)] * 3,
        out_shape=[jax.ShapeDtypeStruct((rt.rows, D_MODEL), F32), jax.ShapeDtypeStruct((rt.rows, D_MODEL), F32),
                   jax.ShapeDtypeStruct((rt.rows, D_MODEL), BF16)],
        compiler_params=_params(("parallel",)),
    )(o, wg, h, mod, g_post_mix, g_pre_mlp)


def _out_bwd(rt, dh1, mix, wg, l, mod, g_post_mix, name):
    def body(dh_ref, mix_ref, w_ref, mod_ref, g_ref, dmix_ref, do_ref, dgate_ref, dg_ref):
        i = pl.program_id(0)
        dz, dgate, dg = _post_norm_bwd_val(dh_ref[...], mix_ref[...], g_ref[...], mod_ref[0, 2:3, :])
        dzb = dz.astype(BF16)
        dmix_ref[...] = dzb
        do_ref[...] = lax.dot_general(dzb, w_ref[...].reshape(D_MODEL, D_MODEL), NT, preferred_element_type=F32).astype(BF16)
        _accumulate(rt, i, [(dgate_ref, dgate)], [(dg_ref, dg)])

    return pl.pallas_call(
        body, name=name, grid=(rt.n_tiles,),
        in_specs=[_row_spec(rt, D_MODEL), _row_spec(rt, D_MODEL), _gathered_spec("out", l), _mod_spec(rt), _vec_spec(D_MODEL)],
        out_specs=[_row_spec(rt, D_MODEL), _row_spec(rt, D_MODEL), _group_spec(rt), _vec_spec(D_MODEL)],
        out_shape=[jax.ShapeDtypeStruct((rt.rows, D_MODEL), BF16), jax.ShapeDtypeStruct((rt.rows, D_MODEL), BF16),
                   _group_shape(rt), _vec_shape()],
        compiler_params=_params(("arbitrary",)),
    )(dh1, mix, wg, mod, g_post_mix)


def _w_chunk(w_ref, k):
    return w_ref[2 * k:2 * k + 2].reshape(1024, 1024)


def _mlp_fwd(rt, u2, h1, wg, l, mod, g_post_mlp, name):
    def body(u2_ref, h1_ref, wu_ref, wd_ref, mod_ref, g_ref, r_ref, y_ref, h2_ref):
        u2_ = u2_ref[...]
        y = jnp.zeros((rt.tm, D_MODEL), F32)
        for k in range(D_FF // 1024):
            a = jnp.maximum(jnp.dot(u2_, _w_chunk(wu_ref, k), preferred_element_type=F32), 0.0)
            rk = (a * a).astype(BF16)
            r_ref[:, k * 1024:(k + 1) * 1024] = rk
            y = y + jnp.dot(rk, _w_chunk(wd_ref, k), preferred_element_type=F32)
        y_ref[...] = y
        h2_ref[...] = _post_norm_val(h1_ref[...], y, g_ref[...], mod_ref, 5)

    return pl.pallas_call(
        body, name=name, grid=(rt.n_tiles,),
        in_specs=[_row_spec(rt, D_MODEL), _row_spec(rt, D_MODEL), _gathered_spec("up", l), _gathered_spec("down", l),
                  _mod_spec(rt), _vec_spec(D_MODEL)],
        out_specs=[_row_spec(rt, D_FF), _row_spec(rt, D_MODEL), _row_spec(rt, D_MODEL)],
        out_shape=[jax.ShapeDtypeStruct((rt.rows, D_FF), BF16), jax.ShapeDtypeStruct((rt.rows, D_MODEL), F32),
                   jax.ShapeDtypeStruct((rt.rows, D_MODEL), F32)],
        compiler_params=_params(("parallel",)),
    )(u2, h1, wg, wg, mod, g_post_mlp)


def _mlp_down_bwd(rt, dh, y, r, wg, l, mod, g_post_mlp, name):
    def body(dh_ref, y_ref, r_ref, wd_ref, mod_ref, g_ref, dy_ref, da_ref, dgate_ref, dg_ref):
        i = pl.program_id(0)
        dz, dgate, dg = _post_norm_bwd_val(dh_ref[...], y_ref[...], g_ref[...], mod_ref[0, 5:6, :])
        dyb = dz.astype(BF16)
        dy_ref[...] = dyb
        for k in range(D_FF // 1024):
            dr = lax.dot_general(dyb, _w_chunk(wd_ref, k), NT, preferred_element_type=F32)
            da_ref[:, k * 1024:(k + 1) * 1024] = (dr * (2.0 * jnp.sqrt(r_ref[:, k * 1024:(k + 1) * 1024].astype(F32)))).astype(BF16)
        _accumulate(rt, i, [(dgate_ref, dgate)], [(dg_ref, dg)])

    return pl.pallas_call(
        body, name=name, grid=(rt.n_tiles,),
        in_specs=[_row_spec(rt, D_MODEL), _row_spec(rt, D_MODEL), _row_spec(rt, D_FF), _gathered_spec("down", l),
                  _mod_spec(rt), _vec_spec(D_MODEL)],
        out_specs=[_row_spec(rt, D_MODEL), _row_spec(rt, D_FF), _group_spec(rt), _vec_spec(D_MODEL)],
        out_shape=[jax.ShapeDtypeStruct((rt.rows, D_MODEL), BF16), jax.ShapeDtypeStruct((rt.rows, D_FF), BF16),
                   _group_shape(rt), _vec_shape()],
        compiler_params=_params(("arbitrary",)),
    )(dh, y, r, wg, mod, g_post_mlp)


def _mlp_up_bwd(rt, da, wg, l, h1, dh, mod, g_pre_mlp, name):
    def body(da_ref, wu_ref, h1_ref, dh_ref, mod_ref, g_ref, dh1_ref, dsh_ref, dsc_ref, dg_ref):
        i = pl.program_id(0)
        du = jnp.zeros((rt.tm, D_MODEL), F32)
        for k in range(D_FF // 1024):
            du = du + lax.dot_general(da_ref[:, k * 1024:(k + 1) * 1024], _w_chunk(wu_ref, k), NT, preferred_element_type=F32)
        d, dsh, dsc, dg = _norm_mod_bwd_val(du, h1_ref[...], g_ref[...], 1.0 + mod_ref[0, 4:5, :])
        dh1_ref[...] = dh_ref[...] + d
        _accumulate(rt, i, [(dsh_ref, dsh), (dsc_ref, dsc)], [(dg_ref, dg)])

    return pl.pallas_call(
        body, name=name, grid=(rt.n_tiles,),
        in_specs=[_row_spec(rt, D_FF), _gathered_spec("up", l), _row_spec(rt, D_MODEL), _row_spec(rt, D_MODEL),
                  _mod_spec(rt), _vec_spec(D_MODEL)],
        out_specs=[_row_spec(rt, D_MODEL), _group_spec(rt), _group_spec(rt), _vec_spec(D_MODEL)],
        out_shape=[jax.ShapeDtypeStruct((rt.rows, D_MODEL), F32), _group_shape(rt), _group_shape(rt), _vec_shape()],
        compiler_params=_params(("arbitrary",)),
    )(da, wg, h1, dh, mod, g_pre_mlp)


def _wgrad_packed(rt, a, b, kind, l, p_prev, name):
    h = PACK_HEIGHT[kind]
    tk = rt.tm

    def body(a_ref, b_ref, *rest):
        o_ref = rest[-1]
        i = pl.program_id(0)

        @pl.when(i == 0)
        def _():
            o_ref[...] = jnp.zeros_like(o_ref)

        if kind == "out":
            res = lax.dot_general(a_ref[...], b_ref[...], TN, preferred_element_type=F32)
            for k in range(4):
                for c in range(2):
                    o_ref[c, k] += res[(2 * k + c) * h:(2 * k + c + 1) * h]
        else:
            for k in range(4):
                if kind == "up":
                    res = lax.dot_general(a_ref[...], b_ref[:, k * 1024:(k + 1) * 1024], TN, preferred_element_type=F32)
                else:
                    res = lax.dot_general(a_ref[:, k * 1024:(k + 1) * 1024], b_ref[...], TN, preferred_element_type=F32)
                o_ref[0, k] += res[0:h]
                o_ref[1, k] += res[h:2 * h]

    in_specs = [pl.BlockSpec((tk, a.shape[1]), lambda i: (i, 0)), pl.BlockSpec((tk, b.shape[1]), lambda i: (i, 0))]
    args = [a, b]
    aliases = {}
    if p_prev is not None:
        in_specs.append(pl.BlockSpec(memory_space=pl.ANY))
        args.append(p_prev)
        aliases = {2: 0}
    return pl.pallas_call(
        body, name=name, grid=(rt.rows // tk,),
        in_specs=in_specs,
        out_specs=pl.BlockSpec((2, 4, h, 1024), lambda i: (0, 0, PACK_OFF[(kind, l)] // h, 0)),
        out_shape=jax.ShapeDtypeStruct((2, 4, PACK_ROWS, 1024), F32),
        input_output_aliases=aliases,
        compiler_params=_params(("arbitrary",)),
    )(*args)


def _wgrad_plain(rt, a, b, name):
    tk = rt.tm

    def body(a_ref, b_ref, o_ref):
        @pl.when(pl.program_id(0) == 0)
        def _():
            o_ref[...] = jnp.zeros_like(o_ref)

        o_ref[...] += lax.dot_general(a_ref[...], b_ref[...], TN, preferred_element_type=F32)

    return pl.pallas_call(
        body, name=name, grid=(rt.rows // tk,),
        in_specs=[pl.BlockSpec((tk, a.shape[1]), lambda i: (i, 0)), pl.BlockSpec((tk, b.shape[1]), lambda i: (i, 0))],
        out_specs=pl.BlockSpec((a.shape[1], b.shape[1]), lambda i: (0, 0)),
        out_shape=jax.ShapeDtypeStruct((a.shape[1], b.shape[1]), F32),
        compiler_params=_params(("arbitrary",)),
    )(a, b)


def _matmul(a, b, mode, out_dtype, name):
    dims = TN if mode == "tn" else NT
    m = a.shape[1] if mode == "tn" else a.shape[0]
    n = b.shape[1] if mode == "tn" else b.shape[0]

    def body(a_ref, b_ref, o_ref):
        o_ref[...] = lax.dot_general(a_ref[...], b_ref[...], dims, preferred_element_type=F32).astype(out_dtype)

    return pl.pallas_call(body, name=name, out_shape=jax.ShapeDtypeStruct((m, n), out_dtype),
                          compiler_params=pltpu.CompilerParams(vmem_limit_bytes=VMEM_LIMIT))(a, b)


def _loss_grad(rt, h, target, name):
    last = rt.n_lat_tiles - 1

    def body(h_ref, t_ref, dh_ref, sq_ref):
        i = pl.program_id(0)

        @pl.when(i == 0)
        def _():
            sq_ref[...] = jnp.zeros_like(sq_ref)

        @pl.when(i <= last)
        def _():
            e = h_ref[...] - t_ref[...]
            dh_ref[...] = e * (1.0 / D_MODEL)
            sq_ref[...] += jnp.sum(e * e, axis=0, keepdims=True)

        @pl.when(i > last)
        def _():
            dh_ref[...] = jnp.zeros_like(dh_ref)

    return pl.pallas_call(
        body, name=name, grid=(rt.n_tiles,),
        in_specs=[_row_spec(rt, D_MODEL), pl.BlockSpec((rt.tm, D_MODEL), lambda i: (jnp.minimum(i, last), 0))],
        out_specs=[_row_spec(rt, D_MODEL), _vec_spec(D_MODEL)],
        out_shape=[jax.ShapeDtypeStruct((rt.rows, D_MODEL), F32), jax.ShapeDtypeStruct((1, D_MODEL), F32)],
        compiler_params=_params(("arbitrary",)),
    )(h, target)


def _stack_heads(x, kvi):
    x = x.astype(F32)
    tq = x.shape[0]
    lane = lax.broadcasted_iota(jnp.int32, (tq, 128), 1)
    keep = lane < HEAD_DIM if kvi == 0 else lane >= HEAD_DIM
    parts = []
    for p in range(2):
        pair = x[:, p * 128:(p + 1) * 128]
        swapped = pltpu.roll(pair, HEAD_DIM, 1)
        lo_head, hi_head = (pair, swapped) if kvi == 0 else (swapped, pair)
        parts += [jnp.where(keep, lo_head, 0.0), jnp.where(keep, hi_head, 0.0)]
    return jnp.concatenate(parts, axis=0).astype(BF16)


def _unstack_heads(o4, kvi):
    tq = o4.shape[0] // GROUP
    lane = lax.broadcasted_iota(jnp.int32, (tq, 128), 1)
    outs = []
    for p in range(2):
        r_lo, r_hi = o4[(2 * p) * tq:(2 * p + 1) * tq], o4[(2 * p + 1) * tq:(2 * p + 2) * tq]
        if kvi == 0:
            lo, hi = r_lo, pltpu.roll(r_hi, HEAD_DIM, 1)
        else:
            lo, hi = pltpu.roll(r_lo, HEAD_DIM, 1), r_hi
        outs.append(jnp.where(lane < HEAD_DIM, lo, hi))
    return jnp.concatenate(outs, axis=1)


def _per_head(shape, axis, tq, values):
    head = lax.broadcasted_iota(jnp.int32, shape, axis) // tq
    out = jnp.zeros(shape, F32)
    for g in range(GROUP):
        out = jnp.where(head == g, values[g], out)
    return out


def _softmax_fwd(qs, sources, sink_col):
    logits = []
    for k, _, mask in sources:
        s = lax.dot_general(qs, k, NT, preferred_element_type=F32)
        logits.append(s if mask is None else jnp.where(mask, s, NEG_BIG))
    m = functools.reduce(jnp.maximum, [jnp.max(s, axis=1, keepdims=True) for s in logits])
    if sink_col is not None:
        m = jnp.maximum(m, sink_col)
    l = jnp.zeros_like(m)
    o = jnp.zeros((qs.shape[0], 128), F32)
    for s, (_, v, _) in zip(logits, sources):
        p = jnp.exp(s - m)
        l = l + jnp.sum(p, axis=1, keepdims=True)
        o = o + jnp.dot(p.astype(BF16), v, preferred_element_type=F32)
    if sink_col is not None:
        l = l + jnp.exp(sink_col - m)
    return o / l


def _softmax_bwd(qs, dos, sources, sink_row):
    logits = []
    for k, _, mask in sources:
        s = lax.dot_general(k, qs, NT, preferred_element_type=F32)
        logits.append(s if mask is None else jnp.where(mask, s, NEG_BIG))
    m = functools.reduce(jnp.maximum, [jnp.max(s, axis=0, keepdims=True) for s in logits])
    if sink_row is not None:
        m = jnp.maximum(m, sink_row)
    ps = [jnp.exp(s - m) for s in logits]
    l = functools.reduce(jnp.add, [jnp.sum(p, axis=0, keepdims=True) for p in ps])
    if sink_row is not None:
        l = l + jnp.exp(sink_row - m)
    inv = 1.0 / l
    ps = [p * inv for p in ps]
    dps = [lax.dot_general(v, dos, NT, preferred_element_type=F32) for _, v, _ in sources]
    delta = functools.reduce(jnp.add, [jnp.sum(p * dp, axis=0, keepdims=True) for p, dp in zip(ps, dps)])
    dq = jnp.zeros((qs.shape[0], 128), F32)
    dks, dvs = [], []
    for p, dp, (k, _, _) in zip(ps, dps, sources):
        ds = (p * (dp - delta)).astype(BF16)
        dvs.append(jnp.dot(p.astype(BF16), dos, preferred_element_type=F32))
        dks.append(jnp.dot(ds, qs, preferred_element_type=F32))
        dq = dq + lax.dot_general(ds, k, TN, preferred_element_type=F32)
    dsink = None if sink_row is None else -(jnp.exp(sink_row - m) * inv) * delta
    return dq, dks, dvs, dsink


def _band(qi, tq, seq):
    span = tq + 2 * WINDOW
    start = pl.multiple_of(jnp.clip(qi * tq - WINDOW, 0, seq - span), 64)
    return start, span


def _band_mask(qi, tq, start, span, query_axis):
    shape = (GROUP * tq, span) if query_axis == 0 else (span, GROUP * tq)
    qpos = qi * tq + lax.broadcasted_iota(jnp.int32, shape, query_axis) % tq
    kpos = start + lax.broadcasted_iota(jnp.int32, shape, 1 - query_axis)
    return jnp.abs(kpos - qpos) <= WINDOW


def _qkv_specs(rt, tq, q_row, ctx_row, with_latent):
    specs = [pl.BlockSpec((tq, 256), functools.partial(lambda b, i, col: (q_row(b, i), col), col=col)) for col in (0, 1, 3, 4)]
    if with_latent:
        specs += [pl.BlockSpec((rt.seq, 128), functools.partial(lambda b, i, col: (b, col), col=col))
                  for col in (COL_KA, COL_VA, COL_KB, COL_VB)]
    specs += [pl.BlockSpec((rt.ctx, 128), functools.partial(lambda b, i, col: (ctx_row(b), col), col=col))
              for col in (COL_KA, COL_VA, COL_KB, COL_VB)]
    return specs


def _attn_fwd(rt, qkvp, sink, o_prev, name):
    latent = o_prev is None
    seq, ctx, nb = rt.seq, rt.ctx, rt.nb
    tq = 128 if latent else ctx
    nq = seq // tq if latent else 1
    ctx_blk0 = rt.n_lat // ctx
    q_row = (lambda b, i: b * nq + i) if latent else (lambda b, i: ctx_blk0 + b)

    def body(sink_ref, qa0, qa1, qb0, qb1, *rest):
        if latent:
            kal, val, kbl, vbl, kac, vac, kbc, vbc, o_ref = rest
        else:
            kac, vac, kbc, vbc, _, o_ref = rest
        qi = pl.program_id(1)
        for kvi, (qa, qb) in enumerate(((qa0, qb0), (qa1, qb1))):
            src_a = [(kac[...], vac[...], None)]
            src_b = [(kbc[...], vbc[...], None)]
            if latent:
                src_a.append((kal[...], val[...], None))
                start, span = _band(qi, tq, seq)
                src_b.append((kbl[pl.ds(start, span), :], vbl[pl.ds(start, span), :], _band_mask(qi, tq, start, span, 0)))
            oa = _softmax_fwd(_stack_heads(qa[...], kvi), src_a, None)
            o_ref[:, kvi * 256:(kvi + 1) * 256] = _unstack_heads(oa, kvi).astype(BF16)
            sink_col = _per_head((GROUP * tq, 1), 0, tq, [sink_ref[kvi * GROUP + g] for g in range(GROUP)])
            ob = _softmax_fwd(_stack_heads(qb[...], kvi), src_b, sink_col)
            o_ref[:, 512 + kvi * 256:512 + (kvi + 1) * 256] = _unstack_heads(ob, kvi).astype(BF16)

    specs = _qkv_specs(rt, tq, q_row, lambda b: ctx_blk0 + b, latent)
    args = [sink] + [qkvp] * len(specs)
    in_specs = [pl.BlockSpec(memory_space=pltpu.SMEM)] + specs
    aliases = {}
    if not latent:
        in_specs.append(pl.BlockSpec(memory_space=pl.ANY))
        args.append(o_prev)
        aliases = {len(args) - 1: 0}
    return pl.pallas_call(
        body, name=name, grid=(nb, nq),
        in_specs=in_specs,
        out_specs=pl.BlockSpec((tq, 1024), lambda b, i: (q_row(b, i), 0)),
        out_shape=jax.ShapeDtypeStruct((rt.rows, 1024), BF16),
        input_output_aliases=aliases,
        compiler_params=_params(("parallel", "parallel")),
    )(*args)


def _attn_bwd(rt, qkvp, do, sink, prev, name):
    latent = prev is None
    seq, ctx, nb = rt.seq, rt.ctx, rt.nb
    tq = 128 if latent else ctx
    nq = seq // tq if latent else 1
    ctx_blk0 = rt.n_lat // ctx
    q_row = (lambda b, i: b * nq + i) if latent else (lambda b, i: ctx_blk0 + b)

    def body(sink_ref, qa0, qa1, qb0, qb1, *rest):
        if latent:
            kal, val, kbl, vbl, kac, vac, kbc, vbc, do_ref, dq_ref, dl_ref, dc_ref, dsink_ref = rest
        else:
            kac, vac, kbc, vbc, do_ref, c1_ref, _, _, dq_ref, dc_ref, dsink_ref = rest
        b, qi = pl.program_id(0), pl.program_id(1)

        @pl.when(jnp.logical_and(b == 0, qi == 0))
        def _():
            dsink_ref[...] = jnp.zeros_like(dsink_ref)

        if latent:
            @pl.when(qi == 0)
            def _():
                dc_ref[...] = jnp.zeros_like(dc_ref)
                dl_ref[...] = jnp.zeros_like(dl_ref)
        else:
            dc_ref[...] = c1_ref[...]

        head_row = lax.broadcasted_iota(jnp.int32, (8, 128), 0)
        for kvi, (qa, qb) in enumerate(((qa0, qb0), (qa1, qb1))):
            src = [(kac[...], vac[...], None)]
            if latent:
                src.append((kal[...], val[...], None))
            dq4, dks, dvs, _ = _softmax_bwd(_stack_heads(qa[...], kvi), _stack_heads(do_ref[:, kvi * 256:(kvi + 1) * 256], kvi), src, None)
            dq_ref[:, kvi * 256:(kvi + 1) * 256] = _unstack_heads(dq4, kvi)
            dc_ref[:, 0:128] += dks[0]
            dc_ref[:, 128:256] += dvs[0]
            if latent:
                dl_ref[:, 0:128] += dks[1]
                dl_ref[:, 128:256] += dvs[1]
            src = [(kbc[...], vbc[...], None)]
            if latent:
                start, span = _band(qi, tq, seq)
                src.append((kbl[pl.ds(start, span), :], vbl[pl.ds(start, span), :], _band_mask(qi, tq, start, span, 1)))
            sink_row = _per_head((1, GROUP * tq), 1, tq, [sink_ref[kvi * GROUP + g] for g in range(GROUP)])
            dq4, dks, dvs, dsink = _softmax_bwd(_stack_heads(qb[...], kvi),
                                                _stack_heads(do_ref[:, 512 + kvi * 256:512 + (kvi + 1) * 256], kvi), src, sink_row)
            dq_ref[:, 512 + kvi * 256:512 + (kvi + 1) * 256] = _unstack_heads(dq4, kvi)
            dc_ref[:, 256:384] += dks[0]
            dc_ref[:, 384:512] += dvs[0]
            if latent:
                dl_ref[pl.ds(start, span), 256:384] += dks[1]
                dl_ref[pl.ds(start, span), 384:512] += dvs[1]
            head = lax.broadcasted_iota(jnp.int32, (1, GROUP * tq), 1) // tq
            upd = jnp.zeros((8, 128), F32)
            for g in range(GROUP):
                upd = jnp.where(head_row == kvi * GROUP + g, jnp.sum(jnp.where(head == g, dsink, 0.0)), upd)
            dsink_ref[...] += upd

    specs = _qkv_specs(rt, tq, q_row, lambda b: ctx_blk0 + b, latent)
    in_specs = [pl.BlockSpec(memory_space=pltpu.SMEM)] + specs + [pl.BlockSpec((tq, 1024), lambda b, i: (q_row(b, i), 0))]
    args = [sink] + [qkvp] * len(specs) + [do]
    dq_shape = jax.ShapeDtypeStruct((rt.rows, 1024), F32)
    dkv_shape = jax.ShapeDtypeStruct((rt.rows, 512), F32)
    dsink_spec, dsink_shape = pl.BlockSpec((8, 128), lambda b, i: (0, 0)), jax.ShapeDtypeStruct((8, 128), F32)
    dq_spec = pl.BlockSpec((tq, 1024), lambda b, i: (q_row(b, i), 0))
    if latent:
        out_specs = [dq_spec, pl.BlockSpec((seq, 512), lambda b, i: (b, 0)), pl.BlockSpec((ctx, 512), lambda b, i: (b, 0)), dsink_spec]
        out_shape = [dq_shape, dkv_shape, jax.ShapeDtypeStruct((rt.n_ctx, 512), F32), dsink_shape]
        aliases = {}
    else:
        dq_prev, dkv_prev, c1 = prev
        in_specs += [pl.BlockSpec((ctx, 512), lambda b, i: (b, 0)), pl.BlockSpec(memory_space=pl.ANY), pl.BlockSpec(memory_space=pl.ANY)]
        args += [c1, dq_prev, dkv_prev]
        out_specs = [dq_spec, pl.BlockSpec((ctx, 512), lambda b, i: (ctx_blk0 + b, 0)), dsink_spec]
        out_shape = [dq_shape, dkv_shape, dsink_shape]
        aliases = {len(args) - 2: 0, len(args) - 1: 1}
    return pl.pallas_call(
        body, name=name, grid=(nb, nq),
        in_specs=in_specs, out_specs=out_specs, out_shape=out_shape,
        input_output_aliases=aliases,
        compiler_params=_params(("arbitrary", "arbitrary")),
    )(*args)


def _silu(x):
    return x / (1.0 + jnp.exp(-x))


def _ada_fwd(cond, w_half, b_half, name):
    rows = cond.shape[0]
    cols = w_half.shape[2]

    def body(c_ref, w_ref, b_ref, x_ref, o_ref):
        xs = _silu(c_ref[...]).astype(BF16)
        x_ref[...] = xs
        for l in range(DEPTH):
            o_ref[l] = jnp.dot(xs, w_ref[l].astype(BF16), preferred_element_type=F32) + b_ref[l]

    return pl.pallas_call(
        body, name=name,
        out_shape=[jax.ShapeDtypeStruct((rows, D_MODEL), BF16), jax.ShapeDtypeStruct((DEPTH, rows, cols), F32)],
        compiler_params=pltpu.CompilerParams(vmem_limit_bytes=VMEM_LIMIT),
    )(cond, w_half, b_half)


def _dev_sum(x, name):
    _, r, c = x.shape

    def body(x_ref, o_ref):
        v = x_ref[0]
        for d in range(1, N_DEV):
            v = v + x_ref[d]
        o_ref[...] = v

    return pl.pallas_call(body, name=name, out_shape=jax.ShapeDtypeStruct((r, c), F32))(x)


def _c_ctx_grad(parts, c_ctx, name):
    def body(p_ref, c_ref, o_ref):
        v = p_ref[0, 0:1, :]
        for d in range(1, N_DEV):
            v = v + p_ref[d, 0:1, :]
        c = c_ref[...]
        sg = 1.0 / (1.0 + jnp.exp(-c))
        o_ref[...] = v * (sg * (1.0 + c * (1.0 - sg)))

    return pl.pallas_call(body, name=name, out_shape=jax.ShapeDtypeStruct((1, D_MODEL), F32))(parts, c_ctx)


def _adamw(w, g, m, v, name):
    r, c = w.shape
    tr = _pick(r, (256, 128, 64, 32, 24, 16, 8))
    c1 = 1.0 / (1.0 - ADAM_B1 ** ADAM_STEP)
    c2 = 1.0 / (1.0 - ADAM_B2 ** ADAM_STEP)

    def body(w_ref, g_ref, m_ref, v_ref, d_ref, nm_ref, nv_ref):
        g_ = g_ref[...]
        nm = ADAM_B1 * m_ref[...] + (1.0 - ADAM_B1) * g_
        nv = ADAM_B2 * v_ref[...] + (1.0 - ADAM_B2) * (g_ * g_)
        d_ref[...] = -ADAM_LR * ((nm * c1) / (jnp.sqrt(nv * c2) + ADAM_EPS) + ADAM_WD * w_ref[...])
        nm_ref[...] = nm
        nv_ref[...] = nv

    spec = pl.BlockSpec((tr, c), lambda i: (i, 0))
    return pl.pallas_call(
        body, name=name, grid=(r // tr,), in_specs=[spec] * 4, out_specs=[spec] * 3,
        out_shape=[jax.ShapeDtypeStruct((r, c), F32)] * 3,
        compiler_params=_params(("parallel",)),
    )(w, g, m, v)


def _local_step(x, ctx, target, mods, gam, qn, kn, sink, wg, w_in):
    nb, seq, _ = x.shape
    rt = _Rows(nb, seq, ctx.shape[1])
    tables = _rope_tables(rt)
    h = jnp.concatenate([x.reshape(rt.n_lat, D_MODEL), ctx.reshape(rt.n_ctx, D_MODEL)], axis=0)
    saved = []
    for l in range(DEPTH):
        g_pre_mix, g_post_mix, g_pre_mlp, g_post_mlp = gam[l]
        u, qkv, qkvp = _in_fwd(rt, h, g_pre_mix, mods[l], w_in[l], tables, qn[l], kn[l], f"in_fwd{l}")
        o = _attn_fwd(rt, qkvp, sink[l], None, f"attn_lat_fwd{l}")
        o = _attn_fwd(rt, qkvp, sink[l], o, f"attn_ctx_fwd{l}")
        mix, h1, u2 = _out_fwd(rt, o, wg, l, h, mods[l], g_post_mix, g_pre_mlp, f"out_fwd{l}")
        r, y, h2 = _mlp_fwd(rt, u2, h1, wg, l, mods[l], g_post_mlp, f"mlp_fwd{l}")
        saved.append((h, u, qkv, qkvp, o, mix, h1, u2, r, y))
        h = h2

    dh, sq = _loss_grad(rt, h, target.reshape(rt.n_lat, D_MODEL), "loss_grad")

    p = None
    small = [None] * DEPTH
    for l in reversed(range(DEPTH)):
        g_pre_mix, g_post_mix, g_pre_mlp, g_post_mlp = gam[l]
        h0, u, qkv, qkvp, o, mix, h1, u2, r, y = saved[l]
        dy, da, d_gate_m, d_g_post_mlp = _mlp_down_bwd(rt, dh, y, r, wg, l, mods[l], g_post_mlp, f"mlp_down_bwd{l}")
        p = _wgrad_packed(rt, r, dy, "down", l, p, f"mlp_down_wgrad{l}")
        dh1, d_sh_m, d_sc_m, d_g_pre_mlp = _mlp_up_bwd(rt, da, wg, l, h1, dh, mods[l], g_pre_mlp, f"mlp_up_bwd{l}")
        p = _wgrad_packed(rt, u2, da, "up", l, p, f"mlp_up_wgrad{l}")
        dmix, do, d_gate_a, d_g_post_mix = _out_bwd(rt, dh1, mix, wg, l, mods[l], g_post_mix, f"out_bwd{l}")
        p = _wgrad_packed(rt, o, dmix, "out", l, p, f"out_wgrad{l}")
        dq, dkv, dkv_c, dsink1 = _attn_bwd(rt, qkvp, do, sink[l], None, f"attn_lat_bwd{l}")
        dq, dkv, dsink2 = _attn_bwd(rt, qkvp, do, sink[l], (dq, dkv, dkv_c), f"attn_ctx_bwd{l}")
        dqkv, dh, dqn, dkn, d_sh_a, d_sc_a, d_g_pre_mix = _in_bwd(rt, dq, dkv, qkv, tables, qn[l], kn[l], w_in[l], h0, dh1,
                                                                  mods[l], g_pre_mix, f"in_bwd{l}")
        dw_in = _wgrad_plain(rt, u, dqkv, f"in_wgrad{l}")
        o_in = PACK_OFF[("in", l)]
        p = p.at[:, :, o_in:o_in + PACK_HEIGHT["in"]].set(_pack_in_grad(dw_in))
        dmod = jnp.concatenate([d_sh_a, d_sc_a, d_gate_a, d_sh_m, d_sc_m, d_gate_m], axis=1)
        small[l] = dict(mod=dmod, gammas=jnp.concatenate([d_g_pre_mix, d_g_post_mix, d_g_pre_mlp, d_g_post_mlp], axis=0),
                        q_norm=dqn, k_norm=dkn, sink=(dsink1 + dsink2)[:, 0])
    return sq, dh[:rt.n_lat].reshape(nb, seq, D_MODEL), p, small


SMALL_ROWS = 48


def kernel(x, c, ctx, c_ctx, w_ada, b_ada, g_pre_mix, g_post_mix, g_pre_mlp, g_post_mlp, w_in, q_norm, k_norm, sink, w_out, w_up, w_down, loss_target, m_c_ctx, m_w_ada, m_b_ada, m_g_pre_mix, m_g_post_mix, m_g_pre_mlp, m_g_post_mlp, m_w_in, m_q_norm, m_k_norm, m_sink, m_w_out, m_w_up, m_w_down, v_c_ctx, v_w_ada, v_b_ada, v_g_pre_mix, v_g_post_mix, v_g_pre_mlp, v_g_post_mlp, v_w_in, v_q_norm, v_k_norm, v_sink, v_w_out, v_w_up, v_w_down):
    nb = x.shape[0]
    ix, iy, ic = lax.axis_index("x"), lax.axis_index("y"), lax.axis_index("c")
    chip = 2 * ix + iy
    dev = 2 * chip + ic
    ada_cols = w_ada.shape[2] // 2

    c_all = _all_gather(c.reshape(8, (nb * D_MODEL) // 8), "gather_c", False).reshape(N_DEV * nb, D_MODEL)
    n_cond = N_DEV * nb + 1
    cond_rows = 16 * ((n_cond + 15) // 16)
    cond = jnp.concatenate([c_all, c_ctx[None, :], jnp.zeros((cond_rows - n_cond, D_MODEL), F32)], axis=0)
    w_ada_half = lax.dynamic_slice_in_dim(w_ada, ic * ada_cols, ada_cols, 2)
    b_ada_half = lax.dynamic_slice_in_dim(b_ada, dev * ada_cols, ada_cols, 1)[:, None, :]
    x_ada, mod_part = _ada_fwd(cond, w_ada_half, b_ada_half, "ada_fwd")
    mod_g = _all_gather(mod_part.reshape(DEPTH * cond_rows, ada_cols), "gather_mod", False)
    mod_all = mod_g.reshape(N_DEV, DEPTH, cond_rows, ada_cols).transpose(1, 2, 0, 3).reshape(DEPTH, cond_rows, N_MOD * D_MODEL)
    mods = []
    for l in range(DEPTH):
        mine = lax.dynamic_slice_in_dim(mod_all[l], dev * nb, nb, 0)
        mods.append(jnp.concatenate([mine, mod_all[l, n_cond - 1:n_cond]], axis=0).reshape(nb + 1, N_MOD, D_MODEL))

    packed = _pack_local_half(w_in, w_out, w_up, w_down, ic)
    wg = _all_gather(packed, "gather_weights", True).reshape(N_DEV, PACK_ROWS, 1024)
    fw_in = [_unpack_in_weight(wg, l) for l in range(DEPTH)]

    gam = [(g_pre_mix[l][None], g_post_mix[l][None], g_pre_mlp[l][None], g_post_mlp[l][None]) for l in range(DEPTH)]
    qn = [jnp.tile(q_norm[l], 2)[None] for l in range(DEPTH)]
    kn = [jnp.tile(k_norm[l], 2)[None] for l in range(DEPTH)]
    sq, grad_x, p_halves, lg = _local_step(x, ctx, loss_target, mods, gam, qn, kn, [sink[l] for l in range(DEPTH)], wg, fw_in)
    loss = lax.psum(0.5 * jnp.sum(sq) / D_MODEL, ("x", "y", "c"))

    r1 = _sibling_exchange(p_halves, "grad_pair_exchange", True)
    a32, a16 = _pair_sum(p_halves, r1, ic.reshape(1).astype(jnp.int32), "grad_pair_sum")
    r2 = _chip_exchange(a16, "grad_chip_exchange")
    g_half = _owner_sum(a32, r2, chip.reshape(1).astype(jnp.int32), "grad_owner_sum")
    g_sib = _sibling_exchange(g_half, "grad_half_exchange", False)
    g_halves = jnp.where(ic == 0, jnp.stack([g_half, g_sib]), jnp.stack([g_sib, g_half]))
    grad_w_in, grad_w_out, grad_w_up, grad_w_down = _unpack_shard(g_halves)

    def lane_pad(v):
        return jnp.pad(v, (0, D_MODEL - v.shape[0]))[None]

    head_rows = [lane_pad(jnp.concatenate([lg[l]["q_norm"][0], lg[l]["k_norm"][0], lg[l]["sink"]])) for l in range(DEPTH)]
    small = jnp.concatenate([lg[l]["mod"].reshape((nb + 1) * N_MOD, D_MODEL) for l in range(DEPTH)]
                            + [lg[l]["gammas"] for l in range(DEPTH)] + head_rows, axis=0)
    small = jnp.pad(small, ((0, SMALL_ROWS - small.shape[0]), (0, 0)))
    small_g = _all_gather(small, "gather_small", False).reshape(N_DEV, SMALL_ROWS, D_MODEL)
    tot = _dev_sum(small_g, "small_sum")
    mod_rows = (nb + 1) * N_MOD
    o_gam, o_head = DEPTH * mod_rows, DEPTH * mod_rows + 4 * DEPTH
    grad_g = [jnp.stack([tot[o_gam + 4 * l + j] for l in range(DEPTH)]) for j in range(4)]
    grad_q_norm = jnp.stack([tot[o_head + l, 0:64] + tot[o_head + l, 64:128] for l in range(DEPTH)])
    grad_k_norm = jnp.stack([tot[o_head + l, 128:192] + tot[o_head + l, 192:256] for l in range(DEPTH)])
    grad_sink = jnp.stack([tot[o_head + l, 256:264] for l in range(DEPTH)])

    dmod_ex, dmod_ctx = [], []
    for l in range(DEPTH):
        ex = small_g[:, l * mod_rows:l * mod_rows + nb * N_MOD].reshape(N_DEV * nb, N_MOD * D_MODEL)
        cx = tot[l * mod_rows + nb * N_MOD:(l + 1) * mod_rows].reshape(1, N_MOD * D_MODEL)
        dmod_ex.append(ex)
        dmod_ctx.append(cx)
    grad_b_ada = jnp.stack([jnp.sum(dmod_ex[l], axis=0) + dmod_ctx[l][0] for l in range(DEPTH)])
    shard_cols = w_ada.shape[2]
    grad_w_ada, dcc_parts = [], []
    for l in range(DEPTH):
        dm = jnp.concatenate([dmod_ex[l], dmod_ctx[l], jnp.zeros((cond_rows - n_cond, N_MOD * D_MODEL), F32)], axis=0)
        dm_shard = lax.dynamic_slice_in_dim(dm, chip * shard_cols, shard_cols, 1).astype(BF16)
        grad_w_ada.append(_matmul(x_ada, dm_shard, "tn", F32, f"ada_wgrad{l}"))
        dcx = lax.dynamic_slice_in_dim(dmod_ctx[l], dev * ada_cols, ada_cols, 1)
        dcx = jnp.pad(dcx, ((0, 15), (0, 0))).astype(BF16)
        dcc_parts.append(_matmul(dcx, w_ada_half[l].astype(BF16), "nt", F32, f"ada_cond_bwd{l}"))
    grad_w_ada = jnp.stack(grad_w_ada)
    dcc = (dcc_parts[0] + dcc_parts[1])[0:8]
    dcc_g = _all_gather(dcc, "gather_cond_grad", False).reshape(N_DEV, 8, D_MODEL)
    grad_c_ctx = _c_ctx_grad(dcc_g, c_ctx[None], "c_ctx_grad")[0]

    def step(w, g, m, v, name):
        shape = w.shape
        cols = shape[-1]
        d, nm, nv = _adamw(w.reshape(-1, cols), g.reshape(-1, cols), m.reshape(-1, cols), v.reshape(-1, cols), name)
        return d.reshape(shape), nm.reshape(shape), nv.reshape(shape)

    small_names = ["c_ctx", "b_ada", "g_pre_mix", "g_post_mix", "g_pre_mlp", "g_post_mlp", "q_norm", "k_norm", "sink"]
    small_w = [c_ctx, b_ada, g_pre_mix, g_post_mix, g_pre_mlp, g_post_mlp, q_norm, k_norm, sink]
    small_gr = [grad_c_ctx, grad_b_ada] + grad_g + [grad_q_norm, grad_k_norm, grad_sink]
    small_m = [m_c_ctx, m_b_ada, m_g_pre_mix, m_g_post_mix, m_g_pre_mlp, m_g_post_mlp, m_q_norm, m_k_norm, m_sink]
    small_v = [v_c_ctx, v_b_ada, v_g_pre_mix, v_g_post_mix, v_g_pre_mlp, v_g_post_mlp, v_q_norm, v_k_norm, v_sink]
    sizes = [int(np.prod(w.shape)) for w in small_w]
    total = sum(sizes)
    flat_rows = 8 * ((total + 8 * D_MODEL - 1) // (8 * D_MODEL))

    def flat(arrs, fill):
        f = jnp.concatenate([a.reshape(-1) for a in arrs])
        return jnp.concatenate([f, jnp.full((flat_rows * D_MODEL - total,), fill, F32)]).reshape(flat_rows, D_MODEL)

    sd, snm, snv = _adamw(flat(small_w, 0.0), flat(small_gr, 0.0), flat(small_m, 0.0), flat(small_v, 1.0), "adamw_small")

    def unflat(f):
        f = f.reshape(-1)
        out, off = [], 0
        for w, n in zip(small_w, sizes):
            out.append(f[off:off + n].reshape(w.shape))
            off += n
        return out

    small_d, small_nm, small_nv = unflat(sd), unflat(snm), unflat(snv)
    res = {n: (g, d, nm, nv) for n, g, d, nm, nv in zip(small_names, small_gr, small_d, small_nm, small_nv)}
    res["w_ada"] = (grad_w_ada, *step(w_ada, grad_w_ada, m_w_ada, v_w_ada, "adamw_w_ada"))
    res["w_in"] = (grad_w_in, *step(w_in, grad_w_in, m_w_in, v_w_in, "adamw_w_in"))
    res["w_out"] = (grad_w_out, *step(w_out, grad_w_out, m_w_out, v_w_out, "adamw_w_out"))
    res["w_up"] = (grad_w_up, *step(w_up, grad_w_up, m_w_up, v_w_up, "adamw_w_up"))
    res["w_down"] = (grad_w_down, *step(w_down, grad_w_down, m_w_down, v_w_down, "adamw_w_down"))

    order = ["c_ctx", "w_ada", "b_ada", "g_pre_mix", "g_post_mix", "g_pre_mlp", "g_post_mlp", "w_in", "q_norm", "k_norm", "sink", "w_out", "w_up", "w_down"]
    return (loss, grad_x, *[res[n][0] for n in order], *[res[n][1] for n in order],
            *[res[n][2] for n in order], *[res[n][3] for n in order])
```

```python
import functools

import jax
import jax.numpy as jnp
import numpy as np
from jax import lax
from jax.experimental import pallas as pl
from jax.experimental.pallas import tpu as pltpu

F32 = jnp.float32
BF16 = jnp.bfloat16

D_MODEL = 1024
HEAD_DIM = 64
GROUP = 4
WINDOW = 128
N_MOD = 6
D_FF = 4 * D_MODEL
IN_COLS = 1536
GRID_W = 64
ROPE_THETA = 10000.0
EPS = 1e-6
NEG_BIG = -1e30
Q_SCALE = HEAD_DIM ** -0.5
DEPTH = 2
N_DEV = 8

ADAM_LR = 0.001
ADAM_B1 = 0.9
ADAM_B2 = 0.999
ADAM_EPS = 1e-08
ADAM_WD = 0.01
ADAM_STEP = 10

V7X_VMEM_BYTES = 64 * 1024 * 1024
VMEM_LIMIT = V7X_VMEM_BYTES - 8 * 1024 * 1024

MESH = pl.DeviceIdType.MESH
NT = (((1,), (1,)), ((), ()))
TN = (((0,), (0,)), ((), ()))

COL_KA, COL_VA, COL_KB, COL_VB = 4, 5, 10, 11

PACK_HEIGHT = {"up": 512, "down": 512, "out": 128, "in": 192}
PACK_OFF = {("up", 0): 0, ("down", 0): 512, ("out", 0): 1024, ("in", 0): 1152,
            ("up", 1): 1344, ("down", 1): 1856, ("out", 1): 2368, ("in", 1): 2496}
PACK_ROWS = 2688
LAYER_ROWS = 1344
LOCAL_OFF = {"up": 0, "down": 512, "out": 1024, "in": 1152}
W_FIRST, W_LAYER0, W_LAYER1 = (1152, 192), (0, 1152), (1344, 1344)
G_LAYER1, G_MLP0, G_MIX0 = (1344, 1344), (0, 1024), (1024, 320)


def _pick(n, cands):
    for t in cands:
        if n % t == 0:
            return t
    raise ValueError(f"no tile for {n}")


def _params(sem):
    return pltpu.CompilerParams(dimension_semantics=sem, vmem_limit_bytes=VMEM_LIMIT)


def _all_gather(x, name, in_hbm):
    m_per, n = x.shape
    space = pl.ANY if in_hbm else pltpu.VMEM

    def body(x_ref, out_ref, send_sems, recv_sems, local_sem):
        x_, y_, c_ = lax.axis_index("x"), lax.axis_index("y"), lax.axis_index("c")
        me, sibling = (x_, y_, c_), (x_, y_, 1 - c_)
        chips = [(1 - x_, y_), (x_, 1 - y_), (1 - x_, 1 - y_)]

        def rows(px, py, pc):
            return out_ref.at[pl.ds((4 * px + 2 * py + pc) * m_per, m_per), :]

        def copy(k, block, to, src=None):
            return pltpu.make_async_remote_copy(
                src_ref=rows(*block) if src is None else src, dst_ref=rows(*block),
                send_sem=send_sems.at[k], recv_sem=recv_sems.at[k], device_id=to, device_id_type=MESH)

        mine = pltpu.make_async_copy(x_ref, rows(*me), local_sem)
        mine.start()
        first = [copy(0, me, sibling, src=x_ref)]
        first += [copy(1 + j, me, (*chip, c_), src=x_ref) for j, chip in enumerate(chips)]
        for cp in first:
            cp.start()
        passed = [copy(4 + j, (*chip, c_), sibling) for j, chip in enumerate(chips)]
        for j, chip in enumerate(chips):
            copy(1 + j, (*chip, c_), me).wait_recv()
            passed[j].start()
        copy(0, sibling, me).wait_recv()
        for j, chip in enumerate(chips):
            copy(4 + j, (*chip, 1 - c_), me).wait_recv()
        for cp in first + passed:
            cp.wait_send()
        mine.wait()

    return pl.pallas_call(
        body, name=name,
        out_shape=jax.ShapeDtypeStruct((N_DEV * m_per, n), x.dtype),
        in_specs=[pl.BlockSpec(memory_space=space)],
        out_specs=pl.BlockSpec(memory_space=space),
        scratch_shapes=[pltpu.SemaphoreType.DMA((7,)), pltpu.SemaphoreType.DMA((7,)), pltpu.SemaphoreType.DMA],
    )(x)


class _Comm:
    def __init__(self, inputs, out_shapes, aliases, n_send, n_recv, start, finish):
        self.inputs, self.out_shapes, self.aliases = list(inputs), list(out_shapes), dict(aliases)
        self.n_send, self.n_recv, self.start, self.finish = n_send, n_recv, start, finish


def _comm_call(compute, comm, *, name, grid, in_specs, out_specs, out_shape, args, aliases, semantics):
    in_specs, out_specs, out_shape, args, aliases = list(in_specs), list(out_specs), list(out_shape), list(args), dict(aliases)
    if comm is None:
        return pl.pallas_call(compute, name=name, grid=grid, in_specs=in_specs, out_specs=out_specs, out_shape=out_shape,
                              input_output_aliases=aliases, compiler_params=_params(semantics))(*args)
    n_in, n_out, n_ci, n_co = len(args), len(out_shape), len(comm.inputs), len(comm.out_shapes)
    hbm = pl.BlockSpec(memory_space=pl.ANY)
    aliases.update({n_in + i: n_out + o for i, o in comm.aliases.items()})

    def body(*refs):
        ins, c_ins = refs[:n_in], refs[n_in:n_in + n_ci]
        outs, c_outs = refs[n_in + n_ci:n_in + n_ci + n_out], refs[n_in + n_ci + n_out:n_in + n_ci + n_out + n_co]
        send_sems, recv_sems = refs[-2:]
        ids = [pl.program_id(a) for a in range(len(grid))]
        first = functools.reduce(jnp.logical_and, [i == 0 for i in ids])
        last = functools.reduce(jnp.logical_and, [i == g - 1 for i, g in zip(ids, grid)])

        @pl.when(first)
        def _():
            comm.start(c_ins, c_outs, send_sems, recv_sems)

        compute(*ins, *outs)

        @pl.when(last)
        def _():
            comm.finish(c_ins, c_outs, send_sems, recv_sems)

    return pl.pallas_call(
        body, name=name, grid=grid,
        in_specs=in_specs + [hbm] * n_ci, out_specs=out_specs + [hbm] * n_co, out_shape=out_shape + comm.out_shapes,
        input_output_aliases=aliases,
        scratch_shapes=[pltpu.SemaphoreType.DMA((comm.n_send,)), pltpu.SemaphoreType.DMA((comm.n_recv,))],
        compiler_params=_params(("arbitrary",) * len(grid)),
    )(*args, *comm.inputs)


def _place():
    x_, y_, c_ = lax.axis_index("x"), lax.axis_index("y"), lax.axis_index("c")
    return x_, y_, c_, [(1 - x_, y_), (x_, 1 - y_), (1 - x_, 1 - y_)]


def _gather_copies(packed_ref, wg_ref, send_sems, recv_sems, rows):
    r0, n = rows
    x_, y_, c_, chips = _place()
    me, sibling = (x_, y_, c_), (x_, y_, 1 - c_)
    src = packed_ref.at[pl.ds(r0, n), :]

    def slot(px, py, pc):
        return wg_ref.at[4 * px + 2 * py + pc]

    def copy(k, block, to, from_packed=False):
        return pltpu.make_async_remote_copy(src_ref=src if from_packed else slot(*block), dst_ref=slot(*block),
                                            send_sem=send_sems.at[k], recv_sem=recv_sems.at[k], device_id=to, device_id_type=MESH)

    own = [copy(0, me, sibling, True)] + [copy(1 + j, me, (*chip, c_), True) for j, chip in enumerate(chips)]
    passed = [copy(4 + j, (*chip, c_), sibling) for j, chip in enumerate(chips)]
    over_ici = [copy(1 + j, (*chip, c_), me) for j, chip in enumerate(chips)]
    from_sibling = [copy(0, sibling, me)] + [copy(4 + j, (*chip, 1 - c_), me) for j, chip in enumerate(chips)]
    mine = pltpu.make_async_copy(src, slot(*me), send_sems.at[7])
    return mine, own, passed, over_ici, from_sibling


def _gather_start(packed_ref, wg_ref, send_sems, recv_sems, rows):
    mine, own, _, _, _ = _gather_copies(packed_ref, wg_ref, send_sems, recv_sems, rows)
    mine.start()
    for cp in own:
        cp.start()


def _gather_finish(packed_ref, wg_ref, send_sems, recv_sems, rows):
    mine, own, passed, over_ici, from_sibling = _gather_copies(packed_ref, wg_ref, send_sems, recv_sems, rows)
    for arrived, onward in zip(over_ici, passed):
        arrived.wait_recv()
        onward.start()
    for arrived in from_sibling:
        arrived.wait_recv()
    for cp in own + passed:
        cp.wait_send()
    mine.wait()


def _gather_comm(packed, rows):
    return _Comm([packed], [jax.ShapeDtypeStruct((N_DEV, rows[1], packed.shape[1]), packed.dtype)], {}, 8, 7,
                 lambda ins, outs, ss, rs: _gather_start(ins[0], outs[0], ss, rs, rows),
                 lambda ins, outs, ss, rs: _gather_finish(ins[0], outs[0], ss, rs, rows))


def _gather_rows(packed, rows, name):
    def body(p_ref, wg_ref, send_sems, recv_sems):
        _gather_start(p_ref, wg_ref, send_sems, recv_sems, rows)
        _gather_finish(p_ref, wg_ref, send_sems, recv_sems, rows)

    return pl.pallas_call(
        body, name=name,
        out_shape=jax.ShapeDtypeStruct((N_DEV, rows[1], packed.shape[1]), packed.dtype),
        in_specs=[pl.BlockSpec(memory_space=pl.ANY)], out_specs=pl.BlockSpec(memory_space=pl.ANY),
        scratch_shapes=[pltpu.SemaphoreType.DMA((8,)), pltpu.SemaphoreType.DMA((7,))],
    )(packed)


def _pair_copy(p_ref, out_ref, send_sems, recv_sems):
    x_, y_, c_, _ = _place()
    return pltpu.make_async_remote_copy(src_ref=p_ref.at[1 - c_], dst_ref=out_ref,
                                        send_sem=send_sems.at[0], recv_sem=recv_sems.at[0],
                                        device_id=(x_, y_, 1 - c_), device_id_type=MESH)


def _pair_comm(p):
    return _Comm([p], [jax.ShapeDtypeStruct(p.shape[1:], p.dtype)], {}, 1, 1,
                 lambda ins, outs, ss, rs: _pair_copy(ins[0], outs[0], ss, rs).start(),
                 lambda ins, outs, ss, rs: _pair_copy(ins[0], outs[0], ss, rs).wait())


def _pair_exchange(p, name):
    def body(p_ref, out_ref, send_sems, recv_sems):
        cp = _pair_copy(p_ref, out_ref, send_sems, recv_sems)
        cp.start()
        cp.wait()

    return pl.pallas_call(
        body, name=name, out_shape=jax.ShapeDtypeStruct(p.shape[1:], p.dtype),
        in_specs=[pl.BlockSpec(memory_space=pl.ANY)], out_specs=pl.BlockSpec(memory_space=pl.ANY),
        scratch_shapes=[pltpu.SemaphoreType.DMA((1,)), pltpu.SemaphoreType.DMA((1,))],
    )(p)


def _chip_copies(a_refs, out_refs, send_sems, recv_sems):
    _, _, c_, chips = _place()
    return [pltpu.make_async_remote_copy(src_ref=a_ref.at[2 * tx + ty], dst_ref=o_ref.at[j],
                                         send_sem=send_sems.at[3 * g + j], recv_sem=recv_sems.at[3 * g + j],
                                         device_id=(tx, ty, c_), device_id_type=MESH)
            for g, (a_ref, o_ref) in enumerate(zip(a_refs, out_refs)) for j, (tx, ty) in enumerate(chips)]


def _chip_start(a_refs, out_refs, send_sems, recv_sems):
    for cp in _chip_copies(a_refs, out_refs, send_sems, recv_sems):
        cp.start()


def _chip_finish(a_refs, out_refs, send_sems, recv_sems):
    for cp in _chip_copies(a_refs, out_refs, send_sems, recv_sems):
        cp.wait()


def _chip_comm(arrays):
    shapes = [jax.ShapeDtypeStruct((3,) + a.shape[1:], a.dtype) for a in arrays]
    return _Comm(arrays, shapes, {}, 3 * len(arrays), 3 * len(arrays), _chip_start, _chip_finish)


def _chip_exchange(a, name):
    def body(a_ref, out_ref, send_sems, recv_sems):
        _chip_start([a_ref], [out_ref], send_sems, recv_sems)
        _chip_finish([a_ref], [out_ref], send_sems, recv_sems)

    return pl.pallas_call(
        body, name=name, out_shape=jax.ShapeDtypeStruct((3,) + a.shape[1:], a.dtype),
        in_specs=[pl.BlockSpec(memory_space=pl.ANY)], out_specs=pl.BlockSpec(memory_space=pl.ANY),
        scratch_shapes=[pltpu.SemaphoreType.DMA((3,)), pltpu.SemaphoreType.DMA((3,))],
    )(a)


def _sibling_exchange(x, name):
    def body(x_ref, out_ref, send_sem, recv_sem):
        x_, y_, c_, _ = _place()
        cp = pltpu.make_async_remote_copy(src_ref=x_ref, dst_ref=out_ref, send_sem=send_sem, recv_sem=recv_sem,
                                          device_id=(x_, y_, 1 - c_), device_id_type=MESH)
        cp.start()
        cp.wait()

    return pl.pallas_call(
        body, name=name, out_shape=jax.ShapeDtypeStruct(x.shape, x.dtype),
        in_specs=[pl.BlockSpec(memory_space=pl.ANY)], out_specs=pl.BlockSpec(memory_space=pl.ANY),
        scratch_shapes=[pltpu.SemaphoreType.DMA, pltpu.SemaphoreType.DMA],
    )(x)


SUM_TILES = (512, 384, 256, 192, 128, 64)


def _pair_sum(p, r1, c_idx, name):
    _, _, n, c = p.shape
    tr = _pick(n, SUM_TILES)

    def body(s_ref, p_ref, r_ref, o32_ref, o16_ref):
        v = p_ref[...] + r_ref[...]
        o32_ref[...] = v
        o16_ref[...] = v.astype(BF16)

    blk = pl.BlockSpec((None, tr, c), lambda j, i, s: (j, i, 0))
    grid_spec = pltpu.PrefetchScalarGridSpec(
        num_scalar_prefetch=1, grid=(4, n // tr),
        in_specs=[pl.BlockSpec((None, None, tr, c), lambda j, i, s: (s[0], j, i, 0)), blk],
        out_specs=[blk, blk])
    return pl.pallas_call(
        body, name=name, grid_spec=grid_spec,
        out_shape=[jax.ShapeDtypeStruct((4, n, c), F32), jax.ShapeDtypeStruct((4, n, c), BF16)],
        compiler_params=_params(("arbitrary", "arbitrary")),
    )(c_idx, p, r1)


def _owner_sum(a32, r2, k_idx, name):
    _, r, c = a32.shape
    tr = _pick(r, SUM_TILES)

    def body(s_ref, a_ref, r_ref, o_ref):
        v = a_ref[...]
        for j in range(3):
            v = v + r_ref[j].astype(F32)
        o_ref[...] = v

    grid_spec = pltpu.PrefetchScalarGridSpec(
        num_scalar_prefetch=1, grid=(r // tr,),
        in_specs=[pl.BlockSpec((None, tr, c), lambda i, s: (s[0], i, 0)),
                  pl.BlockSpec((3, tr, c), lambda i, s: (0, i, 0))],
        out_specs=pl.BlockSpec((tr, c), lambda i, s: (i, 0)))
    return pl.pallas_call(
        body, name=name, grid_spec=grid_spec,
        out_shape=jax.ShapeDtypeStruct((r, c), F32),
        compiler_params=_params(("arbitrary",)),
    )(k_idx, a32, r2)


def _pack_local_half(w_in_s, w_out_s, w_up_s, w_down_s, c_idx):
    parts, row = [], 0
    for (kind, l), off in sorted(PACK_OFF.items(), key=lambda kv: kv[1]):
        if off > row:
            parts.append(jnp.zeros((off - row, 1024), BF16))
        if kind == "up":
            p = lax.dynamic_slice_in_dim(w_up_s[l], c_idx * 512, 512, 0)
        elif kind == "down":
            p = lax.dynamic_slice_in_dim(w_down_s[l], c_idx * 512, 512, 0)
        elif kind == "in":
            p = lax.dynamic_slice_in_dim(w_in_s[l], c_idx * 512, 512, 0).reshape(192, 1024)
        else:
            p = lax.dynamic_slice_in_dim(w_out_s[l], c_idx * 128, 128, 0)
        parts.append(p.astype(BF16))
        row = off + PACK_HEIGHT[kind]
    return jnp.concatenate(parts, axis=0)


def _pack_in_grad(dw_in):
    return dw_in.reshape(2, 512, 4, 384).transpose(0, 2, 1, 3).reshape(2, 4, 192, 1024)


def _unpack_in_weight(pieces):
    return pieces.reshape(4, 2, 512, 384).transpose(1, 2, 0, 3).reshape(1024, IN_COLS)


def _unpack_shard(gh):
    w_in, w_out, w_up, w_down = [], [], [], []
    for l in range(DEPTH):
        o = PACK_OFF[("up", l)]
        w_up.append(gh[:, o:o + 512].reshape(1024, 1024))
        o = PACK_OFF[("down", l)]
        w_down.append(gh[:, o:o + 512].reshape(1024, 1024))
        o = PACK_OFF[("in", l)]
        w_in.append(gh[:, o:o + 192].reshape(1024, 384))
        o = PACK_OFF[("out", l)]
        w_out.append(gh[:, o:o + 128].reshape(256, 1024))
    return jnp.stack(w_in), jnp.stack(w_out), jnp.stack(w_up), jnp.stack(w_down)


class _Rows:
    def __init__(self, nb, seq, ctx):
        self.nb, self.seq, self.ctx = nb, seq, ctx
        self.n_lat, self.n_ctx = nb * seq, nb * ctx
        self.rows = self.n_lat + self.n_ctx
        self.tm = _pick(np.gcd(seq, self.n_ctx), (512, 256, 128))
        self.tiles_per_ex = seq // self.tm
        self.n_tiles = self.rows // self.tm
        self.n_lat_tiles = self.n_lat // self.tm
        self.groups = nb + 1

    def group(self, i):
        return jnp.minimum(i // self.tiles_per_ex, self.nb)

    def first_of_group(self, i):
        return jnp.logical_and(i % self.tiles_per_ex == 0, i <= self.n_lat_tiles)


def _mod_spec(rt):
    return pl.BlockSpec((1, N_MOD, D_MODEL), lambda i: (rt.group(i), 0, 0))


def _row_spec(rt, cols):
    return pl.BlockSpec((rt.tm, cols), lambda i: (i, 0))


def _vec_spec(cols):
    return pl.BlockSpec((1, cols), lambda i: (0, 0))


def _group_spec(rt):
    return pl.BlockSpec((1, 1, D_MODEL), lambda i: (rt.group(i), 0, 0))


def _gathered_spec(kind):
    h = PACK_HEIGHT[kind]
    return pl.BlockSpec((N_DEV, h, 1024), lambda *_: (0, LOCAL_OFF[kind] // h, 0), pipeline_mode=pl.Buffered(1))


def _group_shape(rt):
    return jax.ShapeDtypeStruct((rt.groups, 1, D_MODEL), F32)


def _vec_shape(cols=D_MODEL):
    return jax.ShapeDtypeStruct((1, cols), F32)


def _rms_inv(v):
    return lax.rsqrt(jnp.mean(v * v, axis=-1, keepdims=True) + EPS)


def _norm_mod_val(h_, g_, mod_ref, i_shift, i_scale):
    n = h_ * _rms_inv(h_) * g_
    return n * (1.0 + mod_ref[0, i_scale:i_scale + 1, :]) + mod_ref[0, i_shift:i_shift + 1, :]


def _post_norm_val(h_, z_, g_, mod_ref, i_gate):
    return h_ + mod_ref[0, i_gate:i_gate + 1, :] * (z_ * _rms_inv(z_) * g_)


def _post_norm_bwd_val(dh_, z_, g_, gate):
    rinv = _rms_inv(z_)
    n0 = z_ * rinv
    dn = dh_ * gate * g_
    dz = rinv * (dn - n0 * jnp.mean(dn * n0, axis=-1, keepdims=True))
    return dz, jnp.sum(dh_ * n0 * g_, axis=0, keepdims=True), jnp.sum(dh_ * gate * n0, axis=0, keepdims=True)


def _norm_mod_bwd_val(du_, h_, g_, one_sc):
    rinv = _rms_inv(h_)
    n0 = h_ * rinv
    dn = du_ * g_ * one_sc
    dh = rinv * (dn - n0 * jnp.mean(dn * n0, axis=-1, keepdims=True))
    return (dh, jnp.sum(du_, axis=0, keepdims=True), jnp.sum(du_ * n0 * g_, axis=0, keepdims=True),
            jnp.sum(du_ * one_sc * n0, axis=0, keepdims=True))


def _accumulate(rt, i, group_pairs, global_pairs):
    @pl.when(rt.first_of_group(i))
    def _():
        for ref, _ in group_pairs:
            ref[...] = jnp.zeros_like(ref)

    @pl.when(i == 0)
    def _():
        for ref, _ in global_pairs:
            ref[...] = jnp.zeros_like(ref)

    for ref, val in group_pairs:
        ref[0] += val
    for ref, val in global_pairs:
        ref[...] += val


def _rope_tables(rt):
    pos = jnp.arange(rt.seq, dtype=jnp.int32)
    row_ids = (pos // GRID_W).astype(F32)
    col_ids = (pos % GRID_W).astype(F32)
    axis_dim = HEAD_DIM // 2
    inv = ROPE_THETA ** (-jnp.arange(0, axis_dim, 2, dtype=F32) / axis_dim)
    ang_r, ang_c = row_ids[:, None] * inv[None, :], col_ids[:, None] * inv[None, :]
    cr, sr, cc, sc = jnp.cos(ang_r), jnp.sin(ang_r), jnp.cos(ang_c), jnp.sin(ang_c)
    zero = jnp.zeros_like(sr)
    cos = jnp.concatenate([cr, cr, cc, cc], axis=1)
    s_lo = jnp.concatenate([zero, sr, zero, sc], axis=1)
    s_hi = jnp.concatenate([-sr, zero, -sc, zero], axis=1)

    def full(t, ctx_value):
        t = jnp.tile(t, (rt.nb, 2))
        return jnp.concatenate([t, jnp.full((rt.n_ctx, 128), ctx_value, F32)], axis=0)

    return full(cos, 1.0), full(s_lo, 0.0), full(s_hi, 0.0)


def _head_stats(t, lo):
    sq = t * t
    s_lo = jnp.sum(jnp.where(lo, sq, 0.0), axis=1, keepdims=True)
    s_hi = jnp.sum(jnp.where(lo, 0.0, sq), axis=1, keepdims=True)
    return lax.rsqrt(jnp.where(lo, s_lo, s_hi) * (1.0 / HEAD_DIM) + EPS)


def _prep_fwd_body(tm, qkv_ref, c, s1, s2, qn, kn, out_ref):
    lo = lax.broadcasted_iota(jnp.int32, (tm, 128), 1) < HEAD_DIM

    def rope(t):
        return t * c + pltpu.roll(t, 16, 1) * s1 + pltpu.roll(t, 112, 1) * s2

    for j in range(12):
        t = qkv_ref[:, j * 128:(j + 1) * 128]
        if j < 4:
            t = rope(t * _head_stats(t, lo) * qn) * Q_SCALE
        elif j == COL_KA:
            t = rope(t * _head_stats(t, lo) * kn)
        elif 6 <= j < 10:
            t = rope(t) * Q_SCALE
        elif j == COL_KB:
            t = rope(t)
        out_ref[:, j * 128:(j + 1) * 128] = t.astype(BF16)


def _prep_bwd_body(tm, dq_ref, dkv_ref, qkv_ref, c, s1, s2, qn, kn, out_ref):
    lo = lax.broadcasted_iota(jnp.int32, (tm, 128), 1) < HEAD_DIM

    def rope_bwd(d):
        return d * c + pltpu.roll(d * s1, 112, 1) + pltpu.roll(d * s2, 16, 1)

    def norm_bwd(t, g, dy):
        rinv = _head_stats(t, lo)
        n = t * rinv
        dn = dy * g
        prod = dn * n
        m_lo = jnp.sum(jnp.where(lo, prod, 0.0), axis=1, keepdims=True)
        m_hi = jnp.sum(jnp.where(lo, 0.0, prod), axis=1, keepdims=True)
        mean = jnp.where(lo, m_lo, m_hi) * (1.0 / HEAD_DIM)
        return rinv * (dn - n * mean), jnp.sum(dy * n, axis=0, keepdims=True)

    dqn = jnp.zeros((1, 128), F32)
    dkn = jnp.zeros((1, 128), F32)
    for j in range(12):
        t = qkv_ref[:, j * 128:(j + 1) * 128]
        if j < 4:
            d, dg = norm_bwd(t, qn, rope_bwd(dq_ref[:, j * 128:(j + 1) * 128] * Q_SCALE))
            dqn = dqn + dg
        elif j == COL_KA:
            d, dg = norm_bwd(t, kn, rope_bwd(dkv_ref[:, 0:128]))
            dkn = dkn + dg
        elif j == COL_VA:
            d = dkv_ref[:, 128:256]
        elif j < 10:
            d = rope_bwd(dq_ref[:, (j - 2) * 128:(j - 1) * 128] * Q_SCALE)
        elif j == COL_KB:
            d = rope_bwd(dkv_ref[:, 256:384])
        else:
            d = dkv_ref[:, 384:512]
        out_ref[:, j * 128:(j + 1) * 128] = d.astype(BF16)
    return dqn, dkn


def _in_fwd(rt, h, gamma, mod, w_in, tables, qn, kn, name):
    def body(h_ref, g_ref, mod_ref, w_ref, c_ref, s1_ref, s2_ref, qn_ref, kn_ref, u_ref, qkv_ref, qkvp_ref):
        u = _norm_mod_val(h_ref[...], g_ref[...], mod_ref, 0, 1).astype(BF16)
        u_ref[...] = u
        qkv_ref[...] = jnp.dot(u, w_ref[...], preferred_element_type=F32)
        _prep_fwd_body(rt.tm, qkv_ref, c_ref[...], s1_ref[...], s2_ref[...], qn_ref[...], kn_ref[...], qkvp_ref)

    return pl.pallas_call(
        body, name=name, grid=(rt.n_tiles,),
        in_specs=[_row_spec(rt, D_MODEL), _vec_spec(D_MODEL), _mod_spec(rt),
                  pl.BlockSpec((D_MODEL, IN_COLS), lambda i: (0, 0), pipeline_mode=pl.Buffered(1))]
        + [_row_spec(rt, 128)] * 3 + [_vec_spec(128)] * 2,
        out_specs=[_row_spec(rt, D_MODEL), _row_spec(rt, IN_COLS), _row_spec(rt, IN_COLS)],
        out_shape=[jax.ShapeDtypeStruct((rt.rows, D_MODEL), BF16), jax.ShapeDtypeStruct((rt.rows, IN_COLS), F32),
                   jax.ShapeDtypeStruct((rt.rows, IN_COLS), BF16)],
        compiler_params=_params(("parallel",)),
    )(h, gamma, mod, w_in, *tables, qn, kn)


def _in_bwd(rt, dq, dkv, qkv, tables, qn, kn, w_in, h, dres, mod, gamma, name):
    def body(dq_ref, dkv_ref, qkv_ref, c_ref, s1_ref, s2_ref, qn_ref, kn_ref, w_ref, h_ref, dres_ref, mod_ref, g_ref,
             dqkv_ref, dh_ref, dqn_ref, dkn_ref, dsh_ref, dsc_ref, dg_ref):
        i = pl.program_id(0)
        dqn, dkn = _prep_bwd_body(rt.tm, dq_ref, dkv_ref, qkv_ref, c_ref[...], s1_ref[...], s2_ref[...], qn_ref[...], kn_ref[...], dqkv_ref)
        du = lax.dot_general(dqkv_ref[...], w_ref[...], NT, preferred_element_type=F32)
        dh, dsh, dsc, dg = _norm_mod_bwd_val(du, h_ref[...], g_ref[...], 1.0 + mod_ref[0, 1:2, :])
        dh_ref[...] = dres_ref[...] + dh
        _accumulate(rt, i, [(dsh_ref, dsh), (dsc_ref, dsc)], [(dg_ref, dg), (dqn_ref, dqn), (dkn_ref, dkn)])

    return pl.pallas_call(
        body, name=name, grid=(rt.n_tiles,),
        in_specs=[_row_spec(rt, 1024), _row_spec(rt, 512), _row_spec(rt, IN_COLS)] + [_row_spec(rt, 128)] * 3 + [_vec_spec(128)] * 2
        + [pl.BlockSpec((D_MODEL, IN_COLS), lambda i: (0, 0), pipeline_mode=pl.Buffered(1)),
           _row_spec(rt, D_MODEL), _row_spec(rt, D_MODEL), _mod_spec(rt), _vec_spec(D_MODEL)],
        out_specs=[_row_spec(rt, IN_COLS), _row_spec(rt, D_MODEL), _vec_spec(128), _vec_spec(128),
                   _group_spec(rt), _group_spec(rt), _vec_spec(D_MODEL)],
        out_shape=[jax.ShapeDtypeStruct((rt.rows, IN_COLS), BF16), jax.ShapeDtypeStruct((rt.rows, D_MODEL), F32),
                   _vec_shape(128), _vec_shape(128), _group_shape(rt), _group_shape(rt), _vec_shape()],
        compiler_params=_params(("arbitrary",)),
    )(dq, dkv, qkv, *tables, qn, kn, w_in, h, dres, mod, gamma)


def _out_fwd(rt, o, wg, h, mod, g_post_mix, g_pre_mlp, name):
    def body(o_ref, w_ref, h_ref, mod_ref, gpost_ref, gpre_ref, mix_ref, h1_ref, u2_ref):
        mix = jnp.dot(o_ref[...], w_ref[...].reshape(D_MODEL, D_MODEL), preferred_element_type=F32)
        mix_ref[...] = mix
        h1 = _post_norm_val(h_ref[...], mix, gpost_ref[...], mod_ref, 2)
        h1_ref[...] = h1
        u2_ref[...] = _norm_mod_val(h1, gpre_ref[...], mod_ref, 3, 4).astype(BF16)

    return pl.pallas_call(
        body, name=name, grid=(rt.n_tiles,),
        in_specs=[_row_spec(rt, D_MODEL), _gathered_spec("out"), _row_spec(rt, D_MODEL), _mod_spec(rt),
                  _vec_spec(D_MODEL), _vec_spec(D_MODEL)],
        out_specs=[_row_spec(rt, D_MODEL)] * 3,
        out_shape=[jax.ShapeDtypeStruct((rt.rows, D_MODEL), F32), jax.ShapeDtypeStruct((rt.rows, D_MODEL), F32),
                   jax.ShapeDtypeStruct((rt.rows, D_MODEL), BF16)],
        compiler_params=_params(("parallel",)),
    )(o, wg, h, mod, g_post_mix, g_pre_mlp)


def _out_bwd(rt, dh1, mix, wg, mod, g_post_mix, name, comm=None):
    def body(dh_ref, mix_ref, w_ref, mod_ref, g_ref, dmix_ref, do_ref, dgate_ref, dg_ref):
        i = pl.program_id(0)
        dz, dgate, dg = _post_norm_bwd_val(dh_ref[...], mix_ref[...], g_ref[...], mod_ref[0, 2:3, :])
        dzb = dz.astype(BF16)
        dmix_ref[...] = dzb
        do_ref[...] = lax.dot_general(dzb, w_ref[...].reshape(D_MODEL, D_MODEL), NT, preferred_element_type=F32).astype(BF16)
        _accumulate(rt, i, [(dgate_ref, dgate)], [(dg_ref, dg)])

    return _comm_call(
        body, comm, name=name, grid=(rt.n_tiles,),
        in_specs=[_row_spec(rt, D_MODEL), _row_spec(rt, D_MODEL), _gathered_spec("out"), _mod_spec(rt), _vec_spec(D_MODEL)],
        out_specs=[_row_spec(rt, D_MODEL), _row_spec(rt, D_MODEL), _group_spec(rt), _vec_spec(D_MODEL)],
        out_shape=[jax.ShapeDtypeStruct((rt.rows, D_MODEL), BF16), jax.ShapeDtypeStruct((rt.rows, D_MODEL), BF16),
                   _group_shape(rt), _vec_shape()],
        args=[dh1, mix, wg, mod, g_post_mix], aliases={}, semantics=("arbitrary",))


def _w_chunk(w_ref, k):
    return w_ref[2 * k:2 * k + 2].reshape(1024, 1024)


def _mlp_fwd(rt, u2, h1, wg, mod, g_post_mlp, name, comm=None):
    def body(u2_ref, h1_ref, wu_ref, wd_ref, mod_ref, g_ref, r_ref, y_ref, h2_ref):
        u2_ = u2_ref[...]
        y = jnp.zeros((rt.tm, D_MODEL), F32)
        for k in range(D_FF // 1024):
            a = jnp.maximum(jnp.dot(u2_, _w_chunk(wu_ref, k), preferred_element_type=F32), 0.0)
            rk = (a * a).astype(BF16)
            r_ref[:, k * 1024:(k + 1) * 1024] = rk
            y = y + jnp.dot(rk, _w_chunk(wd_ref, k), preferred_element_type=F32)
        y_ref[...] = y
        h2_ref[...] = _post_norm_val(h1_ref[...], y, g_ref[...], mod_ref, 5)

    return _comm_call(
        body, comm, name=name, grid=(rt.n_tiles,),
        in_specs=[_row_spec(rt, D_MODEL), _row_spec(rt, D_MODEL), _gathered_spec("up"), _gathered_spec("down"),
                  _mod_spec(rt), _vec_spec(D_MODEL)],
        out_specs=[_row_spec(rt, D_FF), _row_spec(rt, D_MODEL), _row_spec(rt, D_MODEL)],
        out_shape=[jax.ShapeDtypeStruct((rt.rows, D_FF), BF16), jax.ShapeDtypeStruct((rt.rows, D_MODEL), F32),
                   jax.ShapeDtypeStruct((rt.rows, D_MODEL), F32)],
        args=[u2, h1, wg, wg, mod, g_post_mlp], aliases={}, semantics=("parallel",))


def _mlp_down_bwd(rt, dh, y, r, wg, mod, g_post_mlp, name, comm=None):
    def body(dh_ref, y_ref, r_ref, wd_ref, mod_ref, g_ref, dy_ref, da_ref, dgate_ref, dg_ref):
        i = pl.program_id(0)
        dz, dgate, dg = _post_norm_bwd_val(dh_ref[...], y_ref[...], g_ref[...], mod_ref[0, 5:6, :])
        dyb = dz.astype(BF16)
        dy_ref[...] = dyb
        for k in range(D_FF // 1024):
            dr = lax.dot_general(dyb, _w_chunk(wd_ref, k), NT, preferred_element_type=F32)
            da_ref[:, k * 1024:(k + 1) * 1024] = (dr * (2.0 * jnp.sqrt(r_ref[:, k * 1024:(k + 1) * 1024].astype(F32)))).astype(BF16)
        _accumulate(rt, i, [(dgate_ref, dgate)], [(dg_ref, dg)])

    return _comm_call(
        body, comm, name=name, grid=(rt.n_tiles,),
        in_specs=[_row_spec(rt, D_MODEL), _row_spec(rt, D_MODEL), _row_spec(rt, D_FF), _gathered_spec("down"),
                  _mod_spec(rt), _vec_spec(D_MODEL)],
        out_specs=[_row_spec(rt, D_MODEL), _row_spec(rt, D_FF), _group_spec(rt), _vec_spec(D_MODEL)],
        out_shape=[jax.ShapeDtypeStruct((rt.rows, D_MODEL), BF16), jax.ShapeDtypeStruct((rt.rows, D_FF), BF16),
                   _group_shape(rt), _vec_shape()],
        args=[dh, y, r, wg, mod, g_post_mlp], aliases={}, semantics=("arbitrary",))


def _mlp_up_bwd(rt, da, wg, h1, dh, mod, g_pre_mlp, name):
    def body(da_ref, wu_ref, h1_ref, dh_ref, mod_ref, g_ref, dh1_ref, dsh_ref, dsc_ref, dg_ref):
        i = pl.program_id(0)
        du = jnp.zeros((rt.tm, D_MODEL), F32)
        for k in range(D_FF // 1024):
            du = du + lax.dot_general(da_ref[:, k * 1024:(k + 1) * 1024], _w_chunk(wu_ref, k), NT, preferred_element_type=F32)
        d, dsh, dsc, dg = _norm_mod_bwd_val(du, h1_ref[...], g_ref[...], 1.0 + mod_ref[0, 4:5, :])
        dh1_ref[...] = dh_ref[...] + d
        _accumulate(rt, i, [(dsh_ref, dsh), (dsc_ref, dsc)], [(dg_ref, dg)])

    return pl.pallas_call(
        body, name=name, grid=(rt.n_tiles,),
        in_specs=[_row_spec(rt, D_FF), _gathered_spec("up"), _row_spec(rt, D_MODEL), _row_spec(rt, D_MODEL),
                  _mod_spec(rt), _vec_spec(D_MODEL)],
        out_specs=[_row_spec(rt, D_MODEL), _group_spec(rt), _group_spec(rt), _vec_spec(D_MODEL)],
        out_shape=[jax.ShapeDtypeStruct((rt.rows, D_MODEL), F32), _group_shape(rt), _group_shape(rt), _vec_shape()],
        compiler_params=_params(("arbitrary",)),
    )(da, wg, h1, dh, mod, g_pre_mlp)


def _wgrad_packed(rt, a, b, kind, off, n_rows, p_prev, name):
    h = PACK_HEIGHT[kind]
    tk = rt.tm

    def body(a_ref, b_ref, *rest):
        o_ref = rest[-1]
        i = pl.program_id(0)

        @pl.when(i == 0)
        def _():
            o_ref[...] = jnp.zeros_like(o_ref)

        if kind == "out":
            res = lax.dot_general(a_ref[...], b_ref[...], TN, preferred_element_type=F32)
            for k in range(4):
                for c in range(2):
                    o_ref[c, k] += res[(2 * k + c) * h:(2 * k + c + 1) * h]
        else:
            for k in range(4):
                if kind == "up":
                    res = lax.dot_general(a_ref[...], b_ref[:, k * 1024:(k + 1) * 1024], TN, preferred_element_type=F32)
                else:
                    res = lax.dot_general(a_ref[:, k * 1024:(k + 1) * 1024], b_ref[...], TN, preferred_element_type=F32)
                o_ref[0, k] += res[0:h]
                o_ref[1, k] += res[h:2 * h]

    in_specs = [pl.BlockSpec((tk, a.shape[1]), lambda i: (i, 0)), pl.BlockSpec((tk, b.shape[1]), lambda i: (i, 0))]
    args = [a, b]
    aliases = {}
    if p_prev is not None:
        in_specs.append(pl.BlockSpec(memory_space=pl.ANY))
        args.append(p_prev)
        aliases = {2: 0}
    return pl.pallas_call(
        body, name=name, grid=(rt.rows // tk,),
        in_specs=in_specs,
        out_specs=pl.BlockSpec((2, 4, h, 1024), lambda i: (0, 0, off // h, 0)),
        out_shape=jax.ShapeDtypeStruct((2, 4, n_rows, 1024), F32),
        input_output_aliases=aliases,
        compiler_params=_params(("arbitrary",)),
    )(*args)


def _wgrad_plain(rt, a, b, name):
    tk = rt.tm

    def body(a_ref, b_ref, o_ref):
        @pl.when(pl.program_id(0) == 0)
        def _():
            o_ref[...] = jnp.zeros_like(o_ref)

        o_ref[...] += lax.dot_general(a_ref[...], b_ref[...], TN, preferred_element_type=F32)

    return pl.pallas_call(
        body, name=name, grid=(rt.rows // tk,),
        in_specs=[pl.BlockSpec((tk, a.shape[1]), lambda i: (i, 0)), pl.BlockSpec((tk, b.shape[1]), lambda i: (i, 0))],
        out_specs=pl.BlockSpec((a.shape[1], b.shape[1]), lambda i: (0, 0)),
        out_shape=jax.ShapeDtypeStruct((a.shape[1], b.shape[1]), F32),
        compiler_params=_params(("arbitrary",)),
    )(a, b)


def _matmul(a, b, mode, out_dtype, name):
    dims = TN if mode == "tn" else NT
    m = a.shape[1] if mode == "tn" else a.shape[0]
    n = b.shape[1] if mode == "tn" else b.shape[0]

    def body(a_ref, b_ref, o_ref):
        o_ref[...] = lax.dot_general(a_ref[...], b_ref[...], dims, preferred_element_type=F32).astype(out_dtype)

    return pl.pallas_call(body, name=name, out_shape=jax.ShapeDtypeStruct((m, n), out_dtype),
                          compiler_params=pltpu.CompilerParams(vmem_limit_bytes=VMEM_LIMIT))(a, b)


def _loss_grad(rt, h, target, name):
    last = rt.n_lat_tiles - 1

    def body(h_ref, t_ref, dh_ref, sq_ref):
        i = pl.program_id(0)

        @pl.when(i == 0)
        def _():
            sq_ref[...] = jnp.zeros_like(sq_ref)

        @pl.when(i <= last)
        def _():
            e = h_ref[...] - t_ref[...]
            dh_ref[...] = e * (1.0 / D_MODEL)
            sq_ref[...] += jnp.sum(e * e, axis=0, keepdims=True)

        @pl.when(i > last)
        def _():
            dh_ref[...] = jnp.zeros_like(dh_ref)

    return pl.pallas_call(
        body, name=name, grid=(rt.n_tiles,),
        in_specs=[_row_spec(rt, D_MODEL), pl.BlockSpec((rt.tm, D_MODEL), lambda i: (jnp.minimum(i, last), 0))],
        out_specs=[_row_spec(rt, D_MODEL), _vec_spec(D_MODEL)],
        out_shape=[jax.ShapeDtypeStruct((rt.rows, D_MODEL), F32), jax.ShapeDtypeStruct((1, D_MODEL), F32)],
        compiler_params=_params(("arbitrary",)),
    )(h, target)


def _stack_heads(x, kvi):
    x = x.astype(F32)
    tq = x.shape[0]
    lane = lax.broadcasted_iota(jnp.int32, (tq, 128), 1)
    keep = lane < HEAD_DIM if kvi == 0 else lane >= HEAD_DIM
    parts = []
    for p in range(2):
        pair = x[:, p * 128:(p + 1) * 128]
        swapped = pltpu.roll(pair, HEAD_DIM, 1)
        lo_head, hi_head = (pair, swapped) if kvi == 0 else (swapped, pair)
        parts += [jnp.where(keep, lo_head, 0.0), jnp.where(keep, hi_head, 0.0)]
    return jnp.concatenate(parts, axis=0).astype(BF16)


def _unstack_heads(o4, kvi):
    tq = o4.shape[0] // GROUP
    lane = lax.broadcasted_iota(jnp.int32, (tq, 128), 1)
    outs = []
    for p in range(2):
        r_lo, r_hi = o4[(2 * p) * tq:(2 * p + 1) * tq], o4[(2 * p + 1) * tq:(2 * p + 2) * tq]
        if kvi == 0:
            lo, hi = r_lo, pltpu.roll(r_hi, HEAD_DIM, 1)
        else:
            lo, hi = pltpu.roll(r_lo, HEAD_DIM, 1), r_hi
        outs.append(jnp.where(lane < HEAD_DIM, lo, hi))
    return jnp.concatenate(outs, axis=1)


def _per_head(shape, axis, tq, values):
    head = lax.broadcasted_iota(jnp.int32, shape, axis) // tq
    out = jnp.zeros(shape, F32)
    for g in range(GROUP):
        out = jnp.where(head == g, values[g], out)
    return out


def _softmax_fwd(qs, sources, sink_col):
    logits = []
    for k, _, mask in sources:
        s = lax.dot_general(qs, k, NT, preferred_element_type=F32)
        logits.append(s if mask is None else jnp.where(mask, s, NEG_BIG))
    m = functools.reduce(jnp.maximum, [jnp.max(s, axis=1, keepdims=True) for s in logits])
    if sink_col is not None:
        m = jnp.maximum(m, sink_col)
    l = jnp.zeros_like(m)
    o = jnp.zeros((qs.shape[0], 128), F32)
    for s, (_, v, _) in zip(logits, sources):
        p = jnp.exp(s - m)
        l = l + jnp.sum(p, axis=1, keepdims=True)
        o = o + jnp.dot(p.astype(BF16), v, preferred_element_type=F32)
    if sink_col is not None:
        l = l + jnp.exp(sink_col - m)
    return o / l


def _softmax_bwd(qs, dos, sources, sink_row):
    logits = []
    for k, _, mask in sources:
        s = lax.dot_general(k, qs, NT, preferred_element_type=F32)
        logits.append(s if mask is None else jnp.where(mask, s, NEG_BIG))
    m = functools.reduce(jnp.maximum, [jnp.max(s, axis=0, keepdims=True) for s in logits])
    if sink_row is not None:
        m = jnp.maximum(m, sink_row)
    ps = [jnp.exp(s - m) for s in logits]
    l = functools.reduce(jnp.add, [jnp.sum(p, axis=0, keepdims=True) for p in ps])
    if sink_row is not None:
        l = l + jnp.exp(sink_row - m)
    inv = 1.0 / l
    ps = [p * inv for p in ps]
    dps = [lax.dot_general(v, dos, NT, preferred_element_type=F32) for _, v, _ in sources]
    delta = functools.reduce(jnp.add, [jnp.sum(p * dp, axis=0, keepdims=True) for p, dp in zip(ps, dps)])
    dq = jnp.zeros((qs.shape[0], 128), F32)
    dks, dvs = [], []
    for p, dp, (k, _, _) in zip(ps, dps, sources):
        ds = (p * (dp - delta)).astype(BF16)
        dvs.append(jnp.dot(p.astype(BF16), dos, preferred_element_type=F32))
        dks.append(jnp.dot(ds, qs, preferred_element_type=F32))
        dq = dq + lax.dot_general(ds, k, TN, preferred_element_type=F32)
    dsink = None if sink_row is None else -(jnp.exp(sink_row - m) * inv) * delta
    return dq, dks, dvs, dsink


def _band(qi, tq, seq):
    span = tq + 2 * WINDOW
    start = pl.multiple_of(jnp.clip(qi * tq - WINDOW, 0, seq - span), 64)
    return start, span


def _band_mask(qi, tq, start, span, query_axis):
    shape = (GROUP * tq, span) if query_axis == 0 else (span, GROUP * tq)
    qpos = qi * tq + lax.broadcasted_iota(jnp.int32, shape, query_axis) % tq
    kpos = start + lax.broadcasted_iota(jnp.int32, shape, 1 - query_axis)
    return jnp.abs(kpos - qpos) <= WINDOW


def _qkv_specs(rt, tq, q_row, ctx_row, with_latent):
    specs = [pl.BlockSpec((tq, 256), functools.partial(lambda b, i, col: (q_row(b, i), col), col=col)) for col in (0, 1, 3, 4)]
    if with_latent:
        specs += [pl.BlockSpec((rt.seq, 128), functools.partial(lambda b, i, col: (b, col), col=col))
                  for col in (COL_KA, COL_VA, COL_KB, COL_VB)]
    specs += [pl.BlockSpec((rt.ctx, 128), functools.partial(lambda b, i, col: (ctx_row(b), col), col=col))
              for col in (COL_KA, COL_VA, COL_KB, COL_VB)]
    return specs


def _attn_fwd(rt, qkvp, sink, o_prev, name, comm=None):
    latent = o_prev is None
    seq, ctx, nb = rt.seq, rt.ctx, rt.nb
    tq = 128 if latent else ctx
    nq = seq // tq if latent else 1
    ctx_blk0 = rt.n_lat // ctx
    q_row = (lambda b, i: b * nq + i) if latent else (lambda b, i: ctx_blk0 + b)

    def body(sink_ref, qa0, qa1, qb0, qb1, *rest):
        if latent:
            kal, val, kbl, vbl, kac, vac, kbc, vbc, o_ref = rest
        else:
            kac, vac, kbc, vbc, _, o_ref = rest
        qi = pl.program_id(1)
        for kvi, (qa, qb) in enumerate(((qa0, qb0), (qa1, qb1))):
            src_a = [(kac[...], vac[...], None)]
            src_b = [(kbc[...], vbc[...], None)]
            if latent:
                src_a.append((kal[...], val[...], None))
                start, span = _band(qi, tq, seq)
                src_b.append((kbl[pl.ds(start, span), :], vbl[pl.ds(start, span), :], _band_mask(qi, tq, start, span, 0)))
            oa = _softmax_fwd(_stack_heads(qa[...], kvi), src_a, None)
            o_ref[:, kvi * 256:(kvi + 1) * 256] = _unstack_heads(oa, kvi).astype(BF16)
            sink_col = _per_head((GROUP * tq, 1), 0, tq, [sink_ref[kvi * GROUP + g] for g in range(GROUP)])
            ob = _softmax_fwd(_stack_heads(qb[...], kvi), src_b, sink_col)
            o_ref[:, 512 + kvi * 256:512 + (kvi + 1) * 256] = _unstack_heads(ob, kvi).astype(BF16)

    specs = _qkv_specs(rt, tq, q_row, lambda b: ctx_blk0 + b, latent)
    args = [sink] + [qkvp] * len(specs)
    in_specs = [pl.BlockSpec(memory_space=pltpu.SMEM)] + specs
    aliases = {}
    if not latent:
        in_specs.append(pl.BlockSpec(memory_space=pl.ANY))
        args.append(o_prev)
        aliases = {len(args) - 1: 0}
    outs = _comm_call(
        body, comm, name=name, grid=(nb, nq),
        in_specs=in_specs,
        out_specs=[pl.BlockSpec((tq, 1024), lambda b, i: (q_row(b, i), 0))],
        out_shape=[jax.ShapeDtypeStruct((rt.rows, 1024), BF16)],
        args=args, aliases=aliases, semantics=("parallel", "parallel"))
    return outs[0] if comm is None else outs


def _attn_bwd(rt, qkvp, do, sink, prev, name, comm=None):
    latent = prev is None
    seq, ctx, nb = rt.seq, rt.ctx, rt.nb
    tq = 128 if latent else ctx
    nq = seq // tq if latent else 1
    ctx_blk0 = rt.n_lat // ctx
    q_row = (lambda b, i: b * nq + i) if latent else (lambda b, i: ctx_blk0 + b)

    def body(sink_ref, qa0, qa1, qb0, qb1, *rest):
        if latent:
            kal, val, kbl, vbl, kac, vac, kbc, vbc, do_ref, dq_ref, dl_ref, dc_ref, dsink_ref = rest
        else:
            kac, vac, kbc, vbc, do_ref, c1_ref, _, _, dq_ref, dc_ref, dsink_ref = rest
        b, qi = pl.program_id(0), pl.program_id(1)

        @pl.when(jnp.logical_and(b == 0, qi == 0))
        def _():
            dsink_ref[...] = jnp.zeros_like(dsink_ref)

        if latent:
            @pl.when(qi == 0)
            def _():
                dc_ref[...] = jnp.zeros_like(dc_ref)
                dl_ref[...] = jnp.zeros_like(dl_ref)
        else:
            dc_ref[...] = c1_ref[...]

        head_row = lax.broadcasted_iota(jnp.int32, (8, 128), 0)
        for kvi, (qa, qb) in enumerate(((qa0, qb0), (qa1, qb1))):
            src = [(kac[...], vac[...], None)]
            if latent:
                src.append((kal[...], val[...], None))
            dq4, dks, dvs, _ = _softmax_bwd(_stack_heads(qa[...], kvi), _stack_heads(do_ref[:, kvi * 256:(kvi + 1) * 256], kvi), src, None)
            dq_ref[:, kvi * 256:(kvi + 1) * 256] = _unstack_heads(dq4, kvi)
            dc_ref[:, 0:128] += dks[0]
            dc_ref[:, 128:256] += dvs[0]
            if latent:
                dl_ref[:, 0:128] += dks[1]
                dl_ref[:, 128:256] += dvs[1]
            src = [(kbc[...], vbc[...], None)]
            if latent:
                start, span = _band(qi, tq, seq)
                src.append((kbl[pl.ds(start, span), :], vbl[pl.ds(start, span), :], _band_mask(qi, tq, start, span, 1)))
            sink_row = _per_head((1, GROUP * tq), 1, tq, [sink_ref[kvi * GROUP + g] for g in range(GROUP)])
            dq4, dks, dvs, dsink = _softmax_bwd(_stack_heads(qb[...], kvi),
                                                _stack_heads(do_ref[:, 512 + kvi * 256:512 + (kvi + 1) * 256], kvi), src, sink_row)
            dq_ref[:, 512 + kvi * 256:512 + (kvi + 1) * 256] = _unstack_heads(dq4, kvi)
            dc_ref[:, 256:384] += dks[0]
            dc_ref[:, 384:512] += dvs[0]
            if latent:
                dl_ref[pl.ds(start, span), 256:384] += dks[1]
                dl_ref[pl.ds(start, span), 384:512] += dvs[1]
            head = lax.broadcasted_iota(jnp.int32, (1, GROUP * tq), 1) // tq
            upd = jnp.zeros((8, 128), F32)
            for g in range(GROUP):
                upd = jnp.where(head_row == kvi * GROUP + g, jnp.sum(jnp.where(head == g, dsink, 0.0)), upd)
            dsink_ref[...] += upd

    specs = _qkv_specs(rt, tq, q_row, lambda b: ctx_blk0 + b, latent)
    in_specs = [pl.BlockSpec(memory_space=pltpu.SMEM)] + specs + [pl.BlockSpec((tq, 1024), lambda b, i: (q_row(b, i), 0))]
    args = [sink] + [qkvp] * len(specs) + [do]
    dq_shape = jax.ShapeDtypeStruct((rt.rows, 1024), F32)
    dkv_shape = jax.ShapeDtypeStruct((rt.rows, 512), F32)
    dsink_spec, dsink_shape = pl.BlockSpec((8, 128), lambda b, i: (0, 0)), jax.ShapeDtypeStruct((8, 128), F32)
    dq_spec = pl.BlockSpec((tq, 1024), lambda b, i: (q_row(b, i), 0))
    if latent:
        out_specs = [dq_spec, pl.BlockSpec((seq, 512), lambda b, i: (b, 0)), pl.BlockSpec((ctx, 512), lambda b, i: (b, 0)), dsink_spec]
        out_shape = [dq_shape, dkv_shape, jax.ShapeDtypeStruct((rt.n_ctx, 512), F32), dsink_shape]
        aliases = {}
    else:
        dq_prev, dkv_prev, c1 = prev
        in_specs += [pl.BlockSpec((ctx, 512), lambda b, i: (b, 0)), pl.BlockSpec(memory_space=pl.ANY), pl.BlockSpec(memory_space=pl.ANY)]
        args += [c1, dq_prev, dkv_prev]
        out_specs = [dq_spec, pl.BlockSpec((ctx, 512), lambda b, i: (ctx_blk0 + b, 0)), dsink_spec]
        out_shape = [dq_shape, dkv_shape, dsink_shape]
        aliases = {len(args) - 2: 0, len(args) - 1: 1}
    return _comm_call(body, comm, name=name, grid=(nb, nq), in_specs=in_specs, out_specs=out_specs, out_shape=out_shape,
                      args=args, aliases=aliases, semantics=("arbitrary", "arbitrary"))


def _silu(x):
    return x / (1.0 + jnp.exp(-x))


def _ada_fwd(cond, w_half, b_half, name):
    rows = cond.shape[0]
    cols = w_half.shape[2]

    def body(c_ref, w_ref, b_ref, x_ref, o_ref):
        xs = _silu(c_ref[...]).astype(BF16)
        x_ref[...] = xs
        for l in range(DEPTH):
            o_ref[l] = jnp.dot(xs, w_ref[l].astype(BF16), preferred_element_type=F32) + b_ref[l]

    return pl.pallas_call(
        body, name=name,
        out_shape=[jax.ShapeDtypeStruct((rows, D_MODEL), BF16), jax.ShapeDtypeStruct((DEPTH, rows, cols), F32)],
        compiler_params=pltpu.CompilerParams(vmem_limit_bytes=VMEM_LIMIT),
    )(cond, w_half, b_half)


def _dev_sum(x, name):
    _, r, c = x.shape

    def body(x_ref, o_ref):
        v = x_ref[0]
        for d in range(1, N_DEV):
            v = v + x_ref[d]
        o_ref[...] = v

    return pl.pallas_call(body, name=name, out_shape=jax.ShapeDtypeStruct((r, c), F32))(x)


def _c_ctx_grad(parts, c_ctx, name):
    def body(p_ref, c_ref, o_ref):
        v = p_ref[0, 0:1, :]
        for d in range(1, N_DEV):
            v = v + p_ref[d, 0:1, :]
        c = c_ref[...]
        sg = 1.0 / (1.0 + jnp.exp(-c))
        o_ref[...] = v * (sg * (1.0 + c * (1.0 - sg)))

    return pl.pallas_call(body, name=name, out_shape=jax.ShapeDtypeStruct((1, D_MODEL), F32))(parts, c_ctx)


def _adamw(w, g, m, v, name):
    r, c = w.shape
    tr = _pick(r, (256, 128, 64, 32, 24, 16, 8))
    c1 = 1.0 / (1.0 - ADAM_B1 ** ADAM_STEP)
    c2 = 1.0 / (1.0 - ADAM_B2 ** ADAM_STEP)

    def body(w_ref, g_ref, m_ref, v_ref, d_ref, nm_ref, nv_ref):
        g_ = g_ref[...]
        nm = ADAM_B1 * m_ref[...] + (1.0 - ADAM_B1) * g_
        nv = ADAM_B2 * v_ref[...] + (1.0 - ADAM_B2) * (g_ * g_)
        d_ref[...] = -ADAM_LR * ((nm * c1) / (jnp.sqrt(nv * c2) + ADAM_EPS) + ADAM_WD * w_ref[...])
        nm_ref[...] = nm
        nv_ref[...] = nv

    spec = pl.BlockSpec((tr, c), lambda i: (i, 0))
    return pl.pallas_call(
        body, name=name, grid=(r // tr,), in_specs=[spec] * 4, out_specs=[spec] * 3,
        out_shape=[jax.ShapeDtypeStruct((r, c), F32)] * 3,
        compiler_params=_params(("parallel",)),
    )(w, g, m, v)


def _local_step(x, ctx, target, mods, gam, qn, kn, sink, w_first, w_layers, packed, c_idx):
    nb, seq, _ = x.shape
    rt = _Rows(nb, seq, ctx.shape[1])
    tables = _rope_tables(rt)
    fuse = packed is not None
    h = jnp.concatenate([x.reshape(rt.n_lat, D_MODEL), ctx.reshape(rt.n_ctx, D_MODEL)], axis=0)
    wg = [None, None] if fuse else list(w_layers)
    w_in = [_unpack_in_weight(w_first), None]
    saved = []
    for l in range(DEPTH):
        g_pre_mix, g_post_mix, g_pre_mlp, g_post_mlp = gam[l]
        if l == 1:
            w_in[1] = _unpack_in_weight(wg[1][:, LOCAL_OFF["in"]:LOCAL_OFF["in"] + PACK_HEIGHT["in"]])
        u, qkv, qkvp = _in_fwd(rt, h, g_pre_mix, mods[l], w_in[l], tables, qn[l], kn[l], f"in_fwd{l}")
        if fuse and l == 0:
            o, wg[0] = _attn_fwd(rt, qkvp, sink[l], None, f"attn_lat_fwd{l}", comm=_gather_comm(packed, W_LAYER0))
        else:
            o = _attn_fwd(rt, qkvp, sink[l], None, f"attn_lat_fwd{l}")
        o = _attn_fwd(rt, qkvp, sink[l], o, f"attn_ctx_fwd{l}")
        mix, h1, u2 = _out_fwd(rt, o, wg[l], h, mods[l], g_post_mix, g_pre_mlp, f"out_fwd{l}")
        if fuse and l == 0:
            r, y, h2, wg[1] = _mlp_fwd(rt, u2, h1, wg[l], mods[l], g_post_mlp, f"mlp_fwd{l}", comm=_gather_comm(packed, W_LAYER1))
        else:
            r, y, h2 = _mlp_fwd(rt, u2, h1, wg[l], mods[l], g_post_mlp, f"mlp_fwd{l}")
        saved.append((h, u, qkv, qkvp, o, mix, h1, u2, r, y))
        h = h2

    dh, sq = _loss_grad(rt, h, target.reshape(rt.n_lat, D_MODEL), "loss_grad")

    small = [None] * DEPTH
    groups = {}
    for l in reversed(range(DEPTH)):
        g_pre_mix, g_post_mix, g_pre_mlp, g_post_mlp = gam[l]
        h0, u, qkv, qkvp, o, mix, h1, u2, r, y = saved[l]
        mlp_group, mix_group = (G_LAYER1, G_LAYER1) if l == 1 else (G_MLP0, G_MIX0)
        hide = fuse and l == 0

        outs = _mlp_down_bwd(rt, dh, y, r, wg[l], mods[l], g_post_mlp, f"mlp_down_bwd{l}",
                             comm=_pair_comm(groups[G_LAYER1]) if hide else None)
        dy, da, d_gate_m, d_g_post_mlp = outs[:4]
        if hide:
            sum1 = _pair_sum(groups[G_LAYER1], outs[4], c_idx, "grad_pair_sum_layer1")
        p_mlp = _wgrad_packed(rt, r, dy, "down", PACK_OFF[("down", l)] - mlp_group[0], mlp_group[1], None, f"mlp_down_wgrad{l}")
        dh1, d_sh_m, d_sc_m, d_g_pre_mlp = _mlp_up_bwd(rt, da, wg[l], h1, dh, mods[l], g_pre_mlp, f"mlp_up_bwd{l}")
        p_mlp = _wgrad_packed(rt, u2, da, "up", PACK_OFF[("up", l)] - mlp_group[0], mlp_group[1], p_mlp, f"mlp_up_wgrad{l}")
        outs = _out_bwd(rt, dh1, mix, wg[l], mods[l], g_post_mix, f"out_bwd{l}", comm=_pair_comm(p_mlp) if hide else None)
        dmix, do, d_gate_a, d_g_post_mix = outs[:4]
        if hide:
            sum0 = _pair_sum(p_mlp, outs[4], c_idx, "grad_pair_sum_mlp0")
        p_mix = _wgrad_packed(rt, o, dmix, "out", PACK_OFF[("out", l)] - mix_group[0], mix_group[1],
                              p_mlp if l == 1 else None, f"out_wgrad{l}")
        outs = _attn_bwd(rt, qkvp, do, sink[l], None, f"attn_lat_bwd{l}", comm=_chip_comm([sum1[1], sum0[1]]) if hide else None)
        dq, dkv, dkv_c, dsink1 = outs[:4]
        if hide:
            groups[G_LAYER1], groups[G_MLP0] = (sum1[0], outs[4]), (sum0[0], outs[5])
        dq, dkv, dsink2 = _attn_bwd(rt, qkvp, do, sink[l], (dq, dkv, dkv_c), f"attn_ctx_bwd{l}")
        dqkv, dh, dqn, dkn, d_sh_a, d_sc_a, d_g_pre_mix = _in_bwd(rt, dq, dkv, qkv, tables, qn[l], kn[l], w_in[l], h0, dh1,
                                                                  mods[l], g_pre_mix, f"in_bwd{l}")
        dw_in = _wgrad_plain(rt, u, dqkv, f"in_wgrad{l}")
        o_in = PACK_OFF[("in", l)] - mix_group[0]
        p_mix = p_mix.at[:, :, o_in:o_in + PACK_HEIGHT["in"]].set(_pack_in_grad(dw_in))
        groups[mix_group] = p_mix
        if not hide and l == 0:
            groups[G_MLP0] = p_mlp
        dmod = jnp.concatenate([d_sh_a, d_sc_a, d_gate_a, d_sh_m, d_sc_m, d_gate_m], axis=1)
        small[l] = dict(mod=dmod, gammas=jnp.concatenate([d_g_pre_mix, d_g_post_mix, d_g_pre_mlp, d_g_post_mlp], axis=0),
                        q_norm=dqn, k_norm=dkn, sink=(dsink1 + dsink2)[:, 0])
    return sq, dh[:rt.n_lat].reshape(nb, seq, D_MODEL), [groups[G_LAYER1], groups[G_MLP0], groups[G_MIX0]], small


SMALL_ROWS = 48


def kernel(x, c, ctx, c_ctx, w_ada, b_ada, g_pre_mix, g_post_mix, g_pre_mlp, g_post_mlp, w_in, q_norm, k_norm, sink, w_out, w_up, w_down, loss_target, m_c_ctx, m_w_ada, m_b_ada, m_g_pre_mix, m_g_post_mix, m_g_pre_mlp, m_g_post_mlp, m_w_in, m_q_norm, m_k_norm, m_sink, m_w_out, m_w_up, m_w_down, v_c_ctx, v_w_ada, v_b_ada, v_g_pre_mix, v_g_post_mix, v_g_pre_mlp, v_g_post_mlp, v_w_in, v_q_norm, v_k_norm, v_sink, v_w_out, v_w_up, v_w_down):
    nb = x.shape[0]
    ix, iy, ic = lax.axis_index("x"), lax.axis_index("y"), lax.axis_index("c")
    chip = 2 * ix + iy
    dev = 2 * chip + ic
    ada_cols = w_ada.shape[2] // 2

    c_all = _all_gather(c.reshape(8, (nb * D_MODEL) // 8), "gather_c", False).reshape(N_DEV * nb, D_MODEL)
    n_cond = N_DEV * nb + 1
    cond_rows = 16 * ((n_cond + 15) // 16)
    cond = jnp.concatenate([c_all, c_ctx[None, :], jnp.zeros((cond_rows - n_cond, D_MODEL), F32)], axis=0)
    w_ada_half = lax.dynamic_slice_in_dim(w_ada, ic * ada_cols, ada_cols, 2)
    b_ada_half = lax.dynamic_slice_in_dim(b_ada, dev * ada_cols, ada_cols, 1)[:, None, :]
    x_ada, mod_part = _ada_fwd(cond, w_ada_half, b_ada_half, "ada_fwd")
    mod_g = _all_gather(mod_part.reshape(DEPTH * cond_rows, ada_cols), "gather_mod", False)
    mod_all = mod_g.reshape(N_DEV, DEPTH, cond_rows, ada_cols).transpose(1, 2, 0, 3).reshape(DEPTH, cond_rows, N_MOD * D_MODEL)
    mods = []
    for l in range(DEPTH):
        mine = lax.dynamic_slice_in_dim(mod_all[l], dev * nb, nb, 0)
        mods.append(jnp.concatenate([mine, mod_all[l, n_cond - 1:n_cond]], axis=0).reshape(nb + 1, N_MOD, D_MODEL))

    packed = _pack_local_half(w_in, w_out, w_up, w_down, ic)
    w_first = _gather_rows(packed, W_FIRST, "gather_w_first")
    c_idx, k_idx = ic.reshape(1).astype(jnp.int32), chip.reshape(1).astype(jnp.int32)

    gam = [(g_pre_mix[l][None], g_post_mix[l][None], g_pre_mlp[l][None], g_post_mlp[l][None]) for l in range(DEPTH)]
    qn = [jnp.tile(q_norm[l], 2)[None] for l in range(DEPTH)]
    kn = [jnp.tile(k_norm[l], 2)[None] for l in range(DEPTH)]
    sq, grad_x, (g_layer1, g_mlp0, p_mix0), lg = _local_step(x, ctx, loss_target, mods, gam, qn, kn, [sink[l] for l in range(DEPTH)],
                                                           w_first, None, packed, c_idx)
    loss = lax.psum(0.5 * jnp.sum(sq) / D_MODEL, ("x", "y", "c"))

    r1 = _pair_exchange(p_mix0, "grad_pair_exchange_mix0")
    a32, a16 = _pair_sum(p_mix0, r1, c_idx, "grad_pair_sum_mix0")
    g_mix0 = (a32, _chip_exchange(a16, "grad_chip_exchange_mix0"))
    owned = [_owner_sum(a, r2, k_idx, f"grad_owner_sum_{n}") for n, (a, r2) in (("mlp0", g_mlp0), ("mix0", g_mix0), ("layer1", g_layer1))]
    g_half = jnp.concatenate(owned, axis=0)
    g_sib = _sibling_exchange(g_half, "grad_half_exchange")
    g_halves = jnp.where(ic == 0, jnp.stack([g_half, g_sib]), jnp.stack([g_sib, g_half]))
    grad_w_in, grad_w_out, grad_w_up, grad_w_down = _unpack_shard(g_halves)

    def lane_pad(v):
        return jnp.pad(v, (0, D_MODEL - v.shape[0]))[None]

    head_rows = [lane_pad(jnp.concatenate([lg[l]["q_norm"][0], lg[l]["k_norm"][0], lg[l]["sink"]])) for l in range(DEPTH)]
    small = jnp.concatenate([lg[l]["mod"].reshape((nb + 1) * N_MOD, D_MODEL) for l in range(DEPTH)]
                            + [lg[l]["gammas"] for l in range(DEPTH)] + head_rows, axis=0)
    small = jnp.pad(small, ((0, SMALL_ROWS - small.shape[0]), (0, 0)))
    small_g = _all_gather(small, "gather_small", False).reshape(N_DEV, SMALL_ROWS, D_MODEL)
    tot = _dev_sum(small_g, "small_sum")
    mod_rows = (nb + 1) * N_MOD
    o_gam, o_head = DEPTH * mod_rows, DEPTH * mod_rows + 4 * DEPTH
    grad_g = [jnp.stack([tot[o_gam + 4 * l + j] for l in range(DEPTH)]) for j in range(4)]
    grad_q_norm = jnp.stack([tot[o_head + l, 0:64] + tot[o_head + l, 64:128] for l in range(DEPTH)])
    grad_k_norm = jnp.stack([tot[o_head + l, 128:192] + tot[o_head + l, 192:256] for l in range(DEPTH)])
    grad_sink = jnp.stack([tot[o_head + l, 256:264] for l in range(DEPTH)])

    dmod_ex, dmod_ctx = [], []
    for l in range(DEPTH):
        ex = small_g[:, l * mod_rows:l * mod_rows + nb * N_MOD].reshape(N_DEV * nb, N_MOD * D_MODEL)
        cx = tot[l * mod_rows + nb * N_MOD:(l + 1) * mod_rows].reshape(1, N_MOD * D_MODEL)
        dmod_ex.append(ex)
        dmod_ctx.append(cx)
    grad_b_ada = jnp.stack([jnp.sum(dmod_ex[l], axis=0) + dmod_ctx[l][0] for l in range(DEPTH)])
    shard_cols = w_ada.shape[2]
    grad_w_ada, dcc_parts = [], []
    for l in range(DEPTH):
        dm = jnp.concatenate([dmod_ex[l], dmod_ctx[l], jnp.zeros((cond_rows - n_cond, N_MOD * D_MODEL), F32)], axis=0)
        dm_shard = lax.dynamic_slice_in_dim(dm, chip * shard_cols, shard_cols, 1).astype(BF16)
        grad_w_ada.append(_matmul(x_ada, dm_shard, "tn", F32, f"ada_wgrad{l}"))
        dcx = lax.dynamic_slice_in_dim(dmod_ctx[l], dev * ada_cols, ada_cols, 1)
        dcx = jnp.pad(dcx, ((0, 15), (0, 0))).astype(BF16)
        dcc_parts.append(_matmul(dcx, w_ada_half[l].astype(BF16), "nt", F32, f"ada_cond_bwd{l}"))
    grad_w_ada = jnp.stack(grad_w_ada)
    dcc = (dcc_parts[0] + dcc_parts[1])[0:8]
    dcc_g = _all_gather(dcc, "gather_cond_grad", False).reshape(N_DEV, 8, D_MODEL)
    grad_c_ctx = _c_ctx_grad(dcc_g, c_ctx[None], "c_ctx_grad")[0]

    def step(w, g, m, v, name):
        shape = w.shape
        cols = shape[-1]
        d, nm, nv = _adamw(w.reshape(-1, cols), g.reshape(-1, cols), m.reshape(-1, cols), v.reshape(-1, cols), name)
        return d.reshape(shape), nm.reshape(shape), nv.reshape(shape)

    small_names = ["c_ctx", "b_ada", "g_pre_mix", "g_post_mix", "g_pre_mlp", "g_post_mlp", "q_norm", "k_norm", "sink"]
    small_w = [c_ctx, b_ada, g_pre_mix, g_post_mix, g_pre_mlp, g_post_mlp, q_norm, k_norm, sink]
    small_gr = [grad_c_ctx, grad_b_ada] + grad_g + [grad_q_norm, grad_k_norm, grad_sink]
    small_m = [m_c_ctx, m_b_ada, m_g_pre_mix, m_g_post_mix, m_g_pre_mlp, m_g_post_mlp, m_q_norm, m_k_norm, m_sink]
    small_v = [v_c_ctx, v_b_ada, v_g_pre_mix, v_g_post_mix, v_g_pre_mlp, v_g_post_mlp, v_q_norm, v_k_norm, v_sink]
    sizes = [int(np.prod(w.shape)) for w in small_w]
    total = sum(sizes)
    flat_rows = 8 * ((total + 8 * D_MODEL - 1) // (8 * D_MODEL))

    def flat(arrs, fill):
        f = jnp.concatenate([a.reshape(-1) for a in arrs])
        return jnp.concatenate([f, jnp.full((flat_rows * D_MODEL - total,), fill, F32)]).reshape(flat_rows, D_MODEL)

    sd, snm, snv = _adamw(flat(small_w, 0.0), flat(small_gr, 0.0), flat(small_m, 0.0), flat(small_v, 1.0), "adamw_small")

    def unflat(f):
        f = f.reshape(-1)
        out, off = [], 0
        for w, n in zip(small_w, sizes):
            out.append(f[off:off + n].reshape(w.shape))
            off += n
        return out

    small_d, small_nm, small_nv = unflat(sd), unflat(snm), unflat(snv)
    res = {n: (g, d, nm, nv) for n, g, d, nm, nv in zip(small_names, small_gr, small_d, small_nm, small_nv)}
    res["w_ada"] = (grad_w_ada, *step(w_ada, grad_w_ada, m_w_ada, v_w_ada, "adamw_w_ada"))
    res["w_in"] = (grad_w_in, *step(w_in, grad_w_in, m_w_in, v_w_in, "adamw_w_in"))
    res["w_out"] = (grad_w_out, *step(w_out, grad_w_out, m_w_out, v_w_out, "adamw_w_out"))
    res["w_up"] = (grad_w_up, *step(w_up, grad_w_up, m_w_up, v_w_up, "adamw_w_up"))
    res["w_down"] = (grad_w_down, *step(w_down, grad_w_down, m_w_down, v_w_down, "adamw_w_down"))

    order = ["c_ctx", "w_ada", "b_ada", "g_pre_mix", "g_post_mix", "g_pre_mlp", "g_post_mlp", "w_in", "q_norm", "k_norm", "sink", "w_out", "w_up", "w_down"]
    return (loss, grad_x, *[res[n][0] for n in order], *[res[n][1] for n in order],
            *[res[n][2] for n in order], *[res[n][3] for n in order])
```

```python
import functools

import jax
import jax.numpy as jnp
import numpy as np
from jax import lax
from jax.experimental import pallas as pl
from jax.experimental.pallas import tpu as pltpu

F32 = jnp.float32
BF16 = jnp.bfloat16

D_MODEL = 1024
HEAD_DIM = 64
GROUP = 4
WINDOW = 128
N_MOD = 6
D_FF = 4 * D_MODEL
IN_COLS = 1536
GRID_W = 64
ROPE_THETA = 10000.0
EPS = 1e-6
NEG_BIG = -1e30
Q_SCALE = HEAD_DIM ** -0.5
DEPTH = 2
N_DEV = 8

ADAM_LR = 0.001
ADAM_B1 = 0.9
ADAM_B2 = 0.999
ADAM_EPS = 1e-08
ADAM_WD = 0.01
ADAM_STEP = 10

V7X_VMEM_BYTES = 64 * 1024 * 1024
VMEM_LIMIT = V7X_VMEM_BYTES - 8 * 1024 * 1024

MESH = pl.DeviceIdType.MESH
NT = (((1,), (1,)), ((), ()))
TN = (((0,), (0,)), ((), ()))

COL_KA, COL_VA, COL_KB, COL_VB = 4, 5, 10, 11

PACK_HEIGHT = {"up": 512, "down": 512, "out": 128, "in": 192}
PACK_OFF = {("up", 0): 0, ("down", 0): 512, ("out", 0): 1024, ("in", 0): 1152,
            ("up", 1): 1344, ("down", 1): 1856, ("out", 1): 2368, ("in", 1): 2496}
PACK_ROWS = 2688
LAYER_ROWS = 1344
LOCAL_OFF = {"up": 0, "down": 512, "out": 1024, "in": 1152}
W_FIRST, W_LAYER0, W_MLP1, W_MIX1 = (1152, 192), (0, 1152), (1344, 1024), (2368, 320)
G_LAYER1, G_MLP0, G_MIX0 = (1344, 1344), (0, 1024), (1024, 320)


def _pick(n, cands):
    for t in cands:
        if n % t == 0:
            return t
    raise ValueError(f"no tile for {n}")


def _params(sem):
    return pltpu.CompilerParams(dimension_semantics=sem, vmem_limit_bytes=VMEM_LIMIT)


def _all_gather(x, name, in_hbm):
    m_per, n = x.shape
    space = pl.ANY if in_hbm else pltpu.VMEM

    def body(x_ref, out_ref, send_sems, recv_sems, local_sem):
        x_, y_, c_ = lax.axis_index("x"), lax.axis_index("y"), lax.axis_index("c")
        me, sibling = (x_, y_, c_), (x_, y_, 1 - c_)
        chips = [(1 - x_, y_), (x_, 1 - y_), (1 - x_, 1 - y_)]

        def rows(px, py, pc):
            return out_ref.at[pl.ds((4 * px + 2 * py + pc) * m_per, m_per), :]

        def copy(k, block, to, src=None):
            return pltpu.make_async_remote_copy(
                src_ref=rows(*block) if src is None else src, dst_ref=rows(*block),
                send_sem=send_sems.at[k], recv_sem=recv_sems.at[k], device_id=to, device_id_type=MESH)

        mine = pltpu.make_async_copy(x_ref, rows(*me), local_sem)
        mine.start()
        first = [copy(0, me, sibling, src=x_ref)]
        first += [copy(1 + j, me, (*chip, c_), src=x_ref) for j, chip in enumerate(chips)]
        for cp in first:
            cp.start()
        passed = [copy(4 + j, (*chip, c_), sibling) for j, chip in enumerate(chips)]
        for j, chip in enumerate(chips):
            copy(1 + j, (*chip, c_), me).wait_recv()
            passed[j].start()
        copy(0, sibling, me).wait_recv()
        for j, chip in enumerate(chips):
            copy(4 + j, (*chip, 1 - c_), me).wait_recv()
        for cp in first + passed:
            cp.wait_send()
        mine.wait()

    return pl.pallas_call(
        body, name=name,
        out_shape=jax.ShapeDtypeStruct((N_DEV * m_per, n), x.dtype),
        in_specs=[pl.BlockSpec(memory_space=space)],
        out_specs=pl.BlockSpec(memory_space=space),
        scratch_shapes=[pltpu.SemaphoreType.DMA((7,)), pltpu.SemaphoreType.DMA((7,)), pltpu.SemaphoreType.DMA],
    )(x)


class _Comm:
    def __init__(self, inputs, out_shapes, aliases, n_send, n_recv, start, finish):
        self.inputs, self.out_shapes, self.aliases = list(inputs), list(out_shapes), dict(aliases)
        self.n_send, self.n_recv, self.start, self.finish = n_send, n_recv, start, finish


def _comm_call(compute, comm, *, name, grid, in_specs, out_specs, out_shape, args, aliases, semantics):
    in_specs, out_specs, out_shape, args, aliases = list(in_specs), list(out_specs), list(out_shape), list(args), dict(aliases)
    if comm is None:
        return pl.pallas_call(compute, name=name, grid=grid, in_specs=in_specs, out_specs=out_specs, out_shape=out_shape,
                              input_output_aliases=aliases, compiler_params=_params(semantics))(*args)
    n_in, n_out, n_ci, n_co = len(args), len(out_shape), len(comm.inputs), len(comm.out_shapes)
    hbm = pl.BlockSpec(memory_space=pl.ANY)
    aliases.update({n_in + i: n_out + o for i, o in comm.aliases.items()})

    def body(*refs):
        ins, c_ins = refs[:n_in], refs[n_in:n_in + n_ci]
        outs, c_outs = refs[n_in + n_ci:n_in + n_ci + n_out], refs[n_in + n_ci + n_out:n_in + n_ci + n_out + n_co]
        send_sems, recv_sems = refs[-2:]
        ids = [pl.program_id(a) for a in range(len(grid))]
        first = functools.reduce(jnp.logical_and, [i == 0 for i in ids])
        last = functools.reduce(jnp.logical_and, [i == g - 1 for i, g in zip(ids, grid)])

        @pl.when(first)
        def _():
            comm.start(c_ins, c_outs, send_sems, recv_sems)

        compute(*ins, *outs)

        @pl.when(last)
        def _():
            comm.finish(c_ins, c_outs, send_sems, recv_sems)

    return pl.pallas_call(
        body, name=name, grid=grid,
        in_specs=in_specs + [hbm] * n_ci, out_specs=out_specs + [hbm] * n_co, out_shape=out_shape + comm.out_shapes,
        input_output_aliases=aliases,
        scratch_shapes=[pltpu.SemaphoreType.DMA((comm.n_send,)), pltpu.SemaphoreType.DMA((comm.n_recv,))],
        compiler_params=_params(("arbitrary",) * len(grid)),
    )(*args, *comm.inputs)


def _place():
    x_, y_, c_ = lax.axis_index("x"), lax.axis_index("y"), lax.axis_index("c")
    return x_, y_, c_, [(1 - x_, y_), (x_, 1 - y_), (1 - x_, 1 - y_)]


GATHER_SENDS, GATHER_RECVS = 8, 7


def _gather_copies(packed_ref, wg_ref, send_sems, recv_sems, rows, nth=0):
    r0, n = rows
    x_, y_, c_, chips = _place()
    me, sibling = (x_, y_, c_), (x_, y_, 1 - c_)
    src = packed_ref.at[pl.ds(r0, n), :]

    def slot(px, py, pc):
        return wg_ref.at[4 * px + 2 * py + pc]

    def copy(k, block, to, from_packed=False):
        return pltpu.make_async_remote_copy(src_ref=src if from_packed else slot(*block), dst_ref=slot(*block),
                                            send_sem=send_sems.at[GATHER_SENDS * nth + k], recv_sem=recv_sems.at[GATHER_RECVS * nth + k],
                                            device_id=to, device_id_type=MESH)

    own = [copy(0, me, sibling, True)] + [copy(1 + j, me, (*chip, c_), True) for j, chip in enumerate(chips)]
    passed = [copy(4 + j, (*chip, c_), sibling) for j, chip in enumerate(chips)]
    over_ici = [copy(1 + j, (*chip, c_), me) for j, chip in enumerate(chips)]
    from_sibling = [copy(0, sibling, me)] + [copy(4 + j, (*chip, 1 - c_), me) for j, chip in enumerate(chips)]
    mine = pltpu.make_async_copy(src, slot(*me), send_sems.at[GATHER_SENDS * nth + 7])
    return mine, own, passed, over_ici, from_sibling


def _gather_start(packed_ref, wg_ref, send_sems, recv_sems, rows, nth=0):
    mine, own, _, _, _ = _gather_copies(packed_ref, wg_ref, send_sems, recv_sems, rows, nth)
    mine.start()
    for cp in own:
        cp.start()


def _gather_finish(packed_ref, wg_ref, send_sems, recv_sems, rows, nth=0):
    mine, own, passed, over_ici, from_sibling = _gather_copies(packed_ref, wg_ref, send_sems, recv_sems, rows, nth)
    for arrived, onward in zip(over_ici, passed):
        arrived.wait_recv()
        onward.start()
    for arrived in from_sibling:
        arrived.wait_recv()
    for cp in own + passed:
        cp.wait_send()
    mine.wait()


def _gather_comm(packed, ranges):
    shapes = [jax.ShapeDtypeStruct((N_DEV, n, packed.shape[1]), packed.dtype) for _, n in ranges]

    def start(ins, outs, ss, rs):
        for nth, rows in enumerate(ranges):
            _gather_start(ins[0], outs[nth], ss, rs, rows, nth)

    def finish(ins, outs, ss, rs):
        for nth, rows in enumerate(ranges):
            _gather_finish(ins[0], outs[nth], ss, rs, rows, nth)

    return _Comm([packed], shapes, {}, GATHER_SENDS * len(ranges), GATHER_RECVS * len(ranges), start, finish)


def _gather_rows(packed, rows, name):
    def body(p_ref, wg_ref, send_sems, recv_sems):
        _gather_start(p_ref, wg_ref, send_sems, recv_sems, rows)
        _gather_finish(p_ref, wg_ref, send_sems, recv_sems, rows)

    return pl.pallas_call(
        body, name=name,
        out_shape=jax.ShapeDtypeStruct((N_DEV, rows[1], packed.shape[1]), packed.dtype),
        in_specs=[pl.BlockSpec(memory_space=pl.ANY)], out_specs=pl.BlockSpec(memory_space=pl.ANY),
        scratch_shapes=[pltpu.SemaphoreType.DMA((8,)), pltpu.SemaphoreType.DMA((7,))],
    )(packed)


def _pair_copy(p_ref, out_ref, send_sems, recv_sems):
    x_, y_, c_, _ = _place()
    return pltpu.make_async_remote_copy(src_ref=p_ref.at[1 - c_], dst_ref=out_ref,
                                        send_sem=send_sems.at[0], recv_sem=recv_sems.at[0],
                                        device_id=(x_, y_, 1 - c_), device_id_type=MESH)


def _pair_comm(p):
    return _Comm([p], [jax.ShapeDtypeStruct(p.shape[1:], p.dtype)], {}, 1, 1,
                 lambda ins, outs, ss, rs: _pair_copy(ins[0], outs[0], ss, rs).start(),
                 lambda ins, outs, ss, rs: _pair_copy(ins[0], outs[0], ss, rs).wait())


def _chip_copies(a_refs, out_refs, send_sems, recv_sems):
    _, _, c_, chips = _place()
    return [pltpu.make_async_remote_copy(src_ref=a_ref.at[2 * tx + ty], dst_ref=o_ref.at[j],
                                         send_sem=send_sems.at[3 * g + j], recv_sem=recv_sems.at[3 * g + j],
                                         device_id=(tx, ty, c_), device_id_type=MESH)
            for g, (a_ref, o_ref) in enumerate(zip(a_refs, out_refs)) for j, (tx, ty) in enumerate(chips)]


def _chip_start(a_refs, out_refs, send_sems, recv_sems):
    for cp in _chip_copies(a_refs, out_refs, send_sems, recv_sems):
        cp.start()


def _chip_finish(a_refs, out_refs, send_sems, recv_sems):
    for cp in _chip_copies(a_refs, out_refs, send_sems, recv_sems):
        cp.wait()


def _chip_comm(arrays):
    shapes = [jax.ShapeDtypeStruct((3,) + a.shape[1:], a.dtype) for a in arrays]
    return _Comm(arrays, shapes, {}, 3 * len(arrays), 3 * len(arrays), _chip_start, _chip_finish)


def _halves_copies(x_refs, out_refs, send_sems, recv_sems):
    x_, y_, c_, _ = _place()
    remote = [pltpu.make_async_remote_copy(src_ref=x_ref, dst_ref=o_ref.at[c_], send_sem=send_sems.at[2 * i], recv_sem=recv_sems.at[i],
                                           device_id=(x_, y_, 1 - c_), device_id_type=MESH)
              for i, (x_ref, o_ref) in enumerate(zip(x_refs, out_refs))]
    local = [pltpu.make_async_copy(x_ref, o_ref.at[c_], send_sems.at[2 * i + 1]) for i, (x_ref, o_ref) in enumerate(zip(x_refs, out_refs))]
    return remote, local


def _halves_start(x_refs, out_refs, send_sems, recv_sems):
    remote, local = _halves_copies(x_refs, out_refs, send_sems, recv_sems)
    for cp in remote + local:
        cp.start()


def _halves_finish(x_refs, out_refs, send_sems, recv_sems):
    remote, local = _halves_copies(x_refs, out_refs, send_sems, recv_sems)
    for cp in remote + local:
        cp.wait()


def _halves_comm(arrays):
    shapes = [jax.ShapeDtypeStruct((2,) + a.shape, a.dtype) for a in arrays]
    return _Comm(arrays, shapes, {}, 2 * len(arrays), len(arrays), _halves_start, _halves_finish)


SUM_TILES = (512, 384, 320, 256, 192, 128, 64)


def _pair_sum(p, r1, c_idx, name):
    _, _, n, c = p.shape
    tr = _pick(n, SUM_TILES)

    def body(s_ref, p_ref, r_ref, o32_ref, o16_ref):
        v = p_ref[...] + r_ref[...]
        o32_ref[...] = v
        o16_ref[...] = v.astype(BF16)

    blk = pl.BlockSpec((None, tr, c), lambda j, i, s: (j, i, 0))
    grid_spec = pltpu.PrefetchScalarGridSpec(
        num_scalar_prefetch=1, grid=(4, n // tr),
        in_specs=[pl.BlockSpec((None, None, tr, c), lambda j, i, s: (s[0], j, i, 0)), blk],
        out_specs=[blk, blk])
    return pl.pallas_call(
        body, name=name, grid_spec=grid_spec,
        out_shape=[jax.ShapeDtypeStruct((4, n, c), F32), jax.ShapeDtypeStruct((4, n, c), BF16)],
        compiler_params=_params(("arbitrary", "arbitrary")),
    )(c_idx, p, r1)


def _owner_sum(a32, r2, k_idx, name):
    _, r, c = a32.shape
    tr = _pick(r, SUM_TILES)

    def body(s_ref, a_ref, r_ref, o_ref):
        v = a_ref[...]
        for j in range(3):
            v = v + r_ref[j].astype(F32)
        o_ref[...] = v

    grid_spec = pltpu.PrefetchScalarGridSpec(
        num_scalar_prefetch=1, grid=(r // tr,),
        in_specs=[pl.BlockSpec((None, tr, c), lambda i, s: (s[0], i, 0)),
                  pl.BlockSpec((3, tr, c), lambda i, s: (0, i, 0))],
        out_specs=pl.BlockSpec((tr, c), lambda i, s: (i, 0)))
    return pl.pallas_call(
        body, name=name, grid_spec=grid_spec,
        out_shape=jax.ShapeDtypeStruct((r, c), F32),
        compiler_params=_params(("arbitrary",)),
    )(k_idx, a32, r2)


def _pack_local_half(w_in_s, w_out_s, w_up_s, w_down_s, c_idx):
    parts, row = [], 0
    for (kind, l), off in sorted(PACK_OFF.items(), key=lambda kv: kv[1]):
        if off > row:
            parts.append(jnp.zeros((off - row, 1024), BF16))
        if kind == "up":
            p = lax.dynamic_slice_in_dim(w_up_s[l], c_idx * 512, 512, 0)
        elif kind == "down":
            p = lax.dynamic_slice_in_dim(w_down_s[l], c_idx * 512, 512, 0)
        elif kind == "in":
            p = lax.dynamic_slice_in_dim(w_in_s[l], c_idx * 512, 512, 0).reshape(192, 1024)
        else:
            p = lax.dynamic_slice_in_dim(w_out_s[l], c_idx * 128, 128, 0)
        parts.append(p.astype(BF16))
        row = off + PACK_HEIGHT[kind]
    return jnp.concatenate(parts, axis=0)


def _pack_in_grad(dw_in):
    return dw_in.reshape(2, 512, 4, 384).transpose(0, 2, 1, 3).reshape(2, 4, 192, 1024)


def _unpack_in_weight(pieces):
    return pieces.reshape(4, 2, 512, 384).transpose(1, 2, 0, 3).reshape(1024, IN_COLS)


class _Rows:
    def __init__(self, nb, seq, ctx):
        self.nb, self.seq, self.ctx = nb, seq, ctx
        self.n_lat, self.n_ctx = nb * seq, nb * ctx
        self.rows = self.n_lat + self.n_ctx
        self.tm = _pick(np.gcd(seq, self.n_ctx), (512, 256, 128))
        self.tiles_per_ex = seq // self.tm
        self.n_tiles = self.rows // self.tm
        self.n_lat_tiles = self.n_lat // self.tm
        self.groups = nb + 1

    def group(self, i):
        return jnp.minimum(i // self.tiles_per_ex, self.nb)

    def first_of_group(self, i):
        return jnp.logical_and(i % self.tiles_per_ex == 0, i <= self.n_lat_tiles)


def _mod_spec(rt):
    return pl.BlockSpec((1, N_MOD, D_MODEL), lambda i: (rt.group(i), 0, 0))


def _row_spec(rt, cols):
    return pl.BlockSpec((rt.tm, cols), lambda i: (i, 0))


def _vec_spec(cols):
    return pl.BlockSpec((1, cols), lambda i: (0, 0))


def _group_spec(rt):
    return pl.BlockSpec((1, 1, D_MODEL), lambda i: (rt.group(i), 0, 0))


def _gathered_spec(wg, kind):
    h, off = PACK_HEIGHT[kind], wg[kind][1]
    assert off % h == 0, (kind, off)
    return pl.BlockSpec((N_DEV, h, 1024), lambda *_: (0, off // h, 0), pipeline_mode=pl.Buffered(1))


def _group_shape(rt):
    return jax.ShapeDtypeStruct((rt.groups, 1, D_MODEL), F32)


def _vec_shape(cols=D_MODEL):
    return jax.ShapeDtypeStruct((1, cols), F32)


def _rms_inv(v):
    return lax.rsqrt(jnp.mean(v * v, axis=-1, keepdims=True) + EPS)


def _norm_mod_val(h_, g_, mod_ref, i_shift, i_scale):
    n = h_ * _rms_inv(h_) * g_
    return n * (1.0 + mod_ref[0, i_scale:i_scale + 1, :]) + mod_ref[0, i_shift:i_shift + 1, :]


def _post_norm_val(h_, z_, g_, mod_ref, i_gate):
    return h_ + mod_ref[0, i_gate:i_gate + 1, :] * (z_ * _rms_inv(z_) * g_)


def _post_norm_bwd_val(dh_, z_, g_, gate):
    rinv = _rms_inv(z_)
    n0 = z_ * rinv
    dn = dh_ * gate * g_
    dz = rinv * (dn - n0 * jnp.mean(dn * n0, axis=-1, keepdims=True))
    return dz, jnp.sum(dh_ * n0 * g_, axis=0, keepdims=True), jnp.sum(dh_ * gate * n0, axis=0, keepdims=True)


def _norm_mod_bwd_val(du_, h_, g_, one_sc):
    rinv = _rms_inv(h_)
    n0 = h_ * rinv
    dn = du_ * g_ * one_sc
    dh = rinv * (dn - n0 * jnp.mean(dn * n0, axis=-1, keepdims=True))
    return (dh, jnp.sum(du_, axis=0, keepdims=True), jnp.sum(du_ * n0 * g_, axis=0, keepdims=True),
            jnp.sum(du_ * one_sc * n0, axis=0, keepdims=True))


def _accumulate(rt, i, group_pairs, global_pairs):
    @pl.when(rt.first_of_group(i))
    def _():
        for ref, _ in group_pairs:
            ref[...] = jnp.zeros_like(ref)

    @pl.when(i == 0)
    def _():
        for ref, _ in global_pairs:
            ref[...] = jnp.zeros_like(ref)

    for ref, val in group_pairs:
        ref[0] += val
    for ref, val in global_pairs:
        ref[...] += val


def _rope_tables(rt):
    pos = jnp.arange(rt.seq, dtype=jnp.int32)
    row_ids = (pos // GRID_W).astype(F32)
    col_ids = (pos % GRID_W).astype(F32)
    axis_dim = HEAD_DIM // 2
    inv = ROPE_THETA ** (-jnp.arange(0, axis_dim, 2, dtype=F32) / axis_dim)
    ang_r, ang_c = row_ids[:, None] * inv[None, :], col_ids[:, None] * inv[None, :]
    cr, sr, cc, sc = jnp.cos(ang_r), jnp.sin(ang_r), jnp.cos(ang_c), jnp.sin(ang_c)
    zero = jnp.zeros_like(sr)
    cos = jnp.concatenate([cr, cr, cc, cc], axis=1)
    s_lo = jnp.concatenate([zero, sr, zero, sc], axis=1)
    s_hi = jnp.concatenate([-sr, zero, -sc, zero], axis=1)

    def full(t, ctx_value):
        t = jnp.tile(t, (rt.nb, 2))
        return jnp.concatenate([t, jnp.full((rt.n_ctx, 128), ctx_value, F32)], axis=0)

    return full(cos, 1.0), full(s_lo, 0.0), full(s_hi, 0.0)


def _head_stats(t, lo):
    sq = t * t
    s_lo = jnp.sum(jnp.where(lo, sq, 0.0), axis=1, keepdims=True)
    s_hi = jnp.sum(jnp.where(lo, 0.0, sq), axis=1, keepdims=True)
    return lax.rsqrt(jnp.where(lo, s_lo, s_hi) * (1.0 / HEAD_DIM) + EPS)


def _prep_fwd_body(tm, qkv_ref, c, s1, s2, qn, kn, out_ref):
    lo = lax.broadcasted_iota(jnp.int32, (tm, 128), 1) < HEAD_DIM

    def rope(t):
        return t * c + pltpu.roll(t, 16, 1) * s1 + pltpu.roll(t, 112, 1) * s2

    for j in range(12):
        t = qkv_ref[:, j * 128:(j + 1) * 128]
        if j < 4:
            t = rope(t * _head_stats(t, lo) * qn) * Q_SCALE
        elif j == COL_KA:
            t = rope(t * _head_stats(t, lo) * kn)
        elif 6 <= j < 10:
            t = rope(t) * Q_SCALE
        elif j == COL_KB:
            t = rope(t)
        out_ref[:, j * 128:(j + 1) * 128] = t.astype(BF16)


def _prep_bwd_body(tm, dq_ref, dkv_ref, qkv_ref, c, s1, s2, qn, kn, out_ref):
    lo = lax.broadcasted_iota(jnp.int32, (tm, 128), 1) < HEAD_DIM

    def rope_bwd(d):
        return d * c + pltpu.roll(d * s1, 112, 1) + pltpu.roll(d * s2, 16, 1)

    def norm_bwd(t, g, dy):
        rinv = _head_stats(t, lo)
        n = t * rinv
        dn = dy * g
        prod = dn * n
        m_lo = jnp.sum(jnp.where(lo, prod, 0.0), axis=1, keepdims=True)
        m_hi = jnp.sum(jnp.where(lo, 0.0, prod), axis=1, keepdims=True)
        mean = jnp.where(lo, m_lo, m_hi) * (1.0 / HEAD_DIM)
        return rinv * (dn - n * mean), jnp.sum(dy * n, axis=0, keepdims=True)

    dqn = jnp.zeros((1, 128), F32)
    dkn = jnp.zeros((1, 128), F32)
    for j in range(12):
        t = qkv_ref[:, j * 128:(j + 1) * 128]
        if j < 4:
            d, dg = norm_bwd(t, qn, rope_bwd(dq_ref[:, j * 128:(j + 1) * 128] * Q_SCALE))
            dqn = dqn + dg
        elif j == COL_KA:
            d, dg = norm_bwd(t, kn, rope_bwd(dkv_ref[:, 0:128]))
            dkn = dkn + dg
        elif j == COL_VA:
            d = dkv_ref[:, 128:256]
        elif j < 10:
            d = rope_bwd(dq_ref[:, (j - 2) * 128:(j - 1) * 128] * Q_SCALE)
        elif j == COL_KB:
            d = rope_bwd(dkv_ref[:, 256:384])
        else:
            d = dkv_ref[:, 384:512]
        out_ref[:, j * 128:(j + 1) * 128] = d.astype(BF16)
    return dqn, dkn


def _in_fwd(rt, h, gamma, mod, w_in, tables, qn, kn, name):
    def body(h_ref, g_ref, mod_ref, w_ref, c_ref, s1_ref, s2_ref, qn_ref, kn_ref, u_ref, qkv_ref, qkvp_ref):
        u = _norm_mod_val(h_ref[...], g_ref[...], mod_ref, 0, 1).astype(BF16)
        u_ref[...] = u
        qkv_ref[...] = jnp.dot(u, w_ref[...], preferred_element_type=F32)
        _prep_fwd_body(rt.tm, qkv_ref, c_ref[...], s1_ref[...], s2_ref[...], qn_ref[...], kn_ref[...], qkvp_ref)

    return pl.pallas_call(
        body, name=name, grid=(rt.n_tiles,),
        in_specs=[_row_spec(rt, D_MODEL), _vec_spec(D_MODEL), _mod_spec(rt),
                  pl.BlockSpec((D_MODEL, IN_COLS), lambda i: (0, 0), pipeline_mode=pl.Buffered(1))]
        + [_row_spec(rt, 128)] * 3 + [_vec_spec(128)] * 2,
        out_specs=[_row_spec(rt, D_MODEL), _row_spec(rt, IN_COLS), _row_spec(rt, IN_COLS)],
        out_shape=[jax.ShapeDtypeStruct((rt.rows, D_MODEL), BF16), jax.ShapeDtypeStruct((rt.rows, IN_COLS), F32),
                   jax.ShapeDtypeStruct((rt.rows, IN_COLS), BF16)],
        compiler_params=_params(("parallel",)),
    )(h, gamma, mod, w_in, *tables, qn, kn)


def _in_bwd(rt, dq, dkv, qkv, tables, qn, kn, w_in, h, dres, mod, gamma, latent_only, name, comm=None):
    last = rt.n_lat_tiles - 1

    def body(dq_ref, dkv_ref, qkv_ref, c_ref, s1_ref, s2_ref, qn_ref, kn_ref, w_ref, h_ref, dres_ref, mod_ref, g_ref,
             dqkv_ref, dh_ref, dqn_ref, dkn_ref, dsh_ref, dsc_ref, dg_ref):
        i = pl.program_id(0)
        dqn, dkn = _prep_bwd_body(rt.tm, dq_ref, dkv_ref, qkv_ref, c_ref[...], s1_ref[...], s2_ref[...], qn_ref[...], kn_ref[...], dqkv_ref)
        du = lax.dot_general(dqkv_ref[...], w_ref[...], NT, preferred_element_type=F32)
        dh, dsh, dsc, dg = _norm_mod_bwd_val(du, h_ref[...], g_ref[...], 1.0 + mod_ref[0, 1:2, :])
        if latent_only:
            @pl.when(i <= last)
            def _():
                dh_ref[...] = dres_ref[...] + dh
        else:
            dh_ref[...] = dres_ref[...] + dh
        _accumulate(rt, i, [(dsh_ref, dsh), (dsc_ref, dsc)], [(dg_ref, dg), (dqn_ref, dqn), (dkn_ref, dkn)])

    dh_spec = pl.BlockSpec((rt.tm, D_MODEL), lambda i: (jnp.minimum(i, last), 0)) if latent_only else _row_spec(rt, D_MODEL)
    return _comm_call(
        body, comm, name=name, grid=(rt.n_tiles,),
        in_specs=[_row_spec(rt, 1024), _row_spec(rt, 512), _row_spec(rt, IN_COLS)] + [_row_spec(rt, 128)] * 3 + [_vec_spec(128)] * 2
        + [pl.BlockSpec((D_MODEL, IN_COLS), lambda i: (0, 0), pipeline_mode=pl.Buffered(1)),
           _row_spec(rt, D_MODEL), _row_spec(rt, D_MODEL), _mod_spec(rt), _vec_spec(D_MODEL)],
        out_specs=[_row_spec(rt, IN_COLS), dh_spec, _vec_spec(128), _vec_spec(128),
                   _group_spec(rt), _group_spec(rt), _vec_spec(D_MODEL)],
        out_shape=[jax.ShapeDtypeStruct((rt.rows, IN_COLS), BF16),
                   jax.ShapeDtypeStruct((rt.n_lat if latent_only else rt.rows, D_MODEL), F32),
                   _vec_shape(128), _vec_shape(128), _group_shape(rt), _group_shape(rt), _vec_shape()],
        args=[dq, dkv, qkv, *tables, qn, kn, w_in, h, dres, mod, gamma], aliases={}, semantics=("arbitrary",))


def _out_fwd(rt, o, wg, h, mod, g_post_mix, g_pre_mlp, name):
    def body(o_ref, w_ref, h_ref, mod_ref, gpost_ref, gpre_ref, mix_ref, h1_ref, u2_ref):
        mix = jnp.dot(o_ref[...], w_ref[...].reshape(D_MODEL, D_MODEL), preferred_element_type=F32)
        mix_ref[...] = mix
        h1 = _post_norm_val(h_ref[...], mix, gpost_ref[...], mod_ref, 2)
        h1_ref[...] = h1
        u2_ref[...] = _norm_mod_val(h1, gpre_ref[...], mod_ref, 3, 4).astype(BF16)

    return pl.pallas_call(
        body, name=name, grid=(rt.n_tiles,),
        in_specs=[_row_spec(rt, D_MODEL), _gathered_spec(wg, "out"), _row_spec(rt, D_MODEL), _mod_spec(rt),
                  _vec_spec(D_MODEL), _vec_spec(D_MODEL)],
        out_specs=[_row_spec(rt, D_MODEL)] * 3,
        out_shape=[jax.ShapeDtypeStruct((rt.rows, D_MODEL), F32), jax.ShapeDtypeStruct((rt.rows, D_MODEL), F32),
                   jax.ShapeDtypeStruct((rt.rows, D_MODEL), BF16)],
        compiler_params=_params(("parallel",)),
    )(o, wg["out"][0], h, mod, g_post_mix, g_pre_mlp)


def _out_bwd(rt, dh1, mix, wg, mod, g_post_mix, name, comm=None):
    def body(dh_ref, mix_ref, w_ref, mod_ref, g_ref, dmix_ref, do_ref, dgate_ref, dg_ref):
        i = pl.program_id(0)
        dz, dgate, dg = _post_norm_bwd_val(dh_ref[...], mix_ref[...], g_ref[...], mod_ref[0, 2:3, :])
        dzb = dz.astype(BF16)
        dmix_ref[...] = dzb
        do_ref[...] = lax.dot_general(dzb, w_ref[...].reshape(D_MODEL, D_MODEL), NT, preferred_element_type=F32).astype(BF16)
        _accumulate(rt, i, [(dgate_ref, dgate)], [(dg_ref, dg)])

    return _comm_call(
        body, comm, name=name, grid=(rt.n_tiles,),
        in_specs=[_row_spec(rt, D_MODEL), _row_spec(rt, D_MODEL), _gathered_spec(wg, "out"), _mod_spec(rt), _vec_spec(D_MODEL)],
        out_specs=[_row_spec(rt, D_MODEL), _row_spec(rt, D_MODEL), _group_spec(rt), _vec_spec(D_MODEL)],
        out_shape=[jax.ShapeDtypeStruct((rt.rows, D_MODEL), BF16), jax.ShapeDtypeStruct((rt.rows, D_MODEL), BF16),
                   _group_shape(rt), _vec_shape()],
        args=[dh1, mix, wg["out"][0], mod, g_post_mix], aliases={}, semantics=("arbitrary",))


def _w_chunk(w_ref, k):
    return w_ref[2 * k:2 * k + 2].reshape(1024, 1024)


def _mlp_fwd(rt, u2, h1, wg, mod, g_post_mlp, name, comm=None):
    def body(u2_ref, h1_ref, wu_ref, wd_ref, mod_ref, g_ref, r_ref, y_ref, h2_ref):
        u2_ = u2_ref[...]
        y = jnp.zeros((rt.tm, D_MODEL), F32)
        for k in range(D_FF // 1024):
            a = jnp.maximum(jnp.dot(u2_, _w_chunk(wu_ref, k), preferred_element_type=F32), 0.0)
            rk = (a * a).astype(BF16)
            r_ref[:, k * 1024:(k + 1) * 1024] = rk
            y = y + jnp.dot(rk, _w_chunk(wd_ref, k), preferred_element_type=F32)
        y_ref[...] = y
        h2_ref[...] = _post_norm_val(h1_ref[...], y, g_ref[...], mod_ref, 5)

    return _comm_call(
        body, comm, name=name, grid=(rt.n_tiles,),
        in_specs=[_row_spec(rt, D_MODEL), _row_spec(rt, D_MODEL), _gathered_spec(wg, "up"), _gathered_spec(wg, "down"),
                  _mod_spec(rt), _vec_spec(D_MODEL)],
        out_specs=[_row_spec(rt, D_FF), _row_spec(rt, D_MODEL), _row_spec(rt, D_MODEL)],
        out_shape=[jax.ShapeDtypeStruct((rt.rows, D_FF), BF16), jax.ShapeDtypeStruct((rt.rows, D_MODEL), F32),
                   jax.ShapeDtypeStruct((rt.rows, D_MODEL), F32)],
        args=[u2, h1, wg["up"][0], wg["down"][0], mod, g_post_mlp], aliases={}, semantics=("parallel",))


def _mlp_down_bwd(rt, dh, y, r, wg, mod, g_post_mlp, name, comm=None):
    def body(dh_ref, y_ref, r_ref, wd_ref, mod_ref, g_ref, dy_ref, da_ref, dgate_ref, dg_ref):
        i = pl.program_id(0)
        dz, dgate, dg = _post_norm_bwd_val(dh_ref[...], y_ref[...], g_ref[...], mod_ref[0, 5:6, :])
        dyb = dz.astype(BF16)
        dy_ref[...] = dyb
        for k in range(D_FF // 1024):
            dr = lax.dot_general(dyb, _w_chunk(wd_ref, k), NT, preferred_element_type=F32)
            da_ref[:, k * 1024:(k + 1) * 1024] = (dr * (2.0 * jnp.sqrt(r_ref[:, k * 1024:(k + 1) * 1024].astype(F32)))).astype(BF16)
        _accumulate(rt, i, [(dgate_ref, dgate)], [(dg_ref, dg)])

    return _comm_call(
        body, comm, name=name, grid=(rt.n_tiles,),
        in_specs=[_row_spec(rt, D_MODEL), _row_spec(rt, D_MODEL), _row_spec(rt, D_FF), _gathered_spec(wg, "down"),
                  _mod_spec(rt), _vec_spec(D_MODEL)],
        out_specs=[_row_spec(rt, D_MODEL), _row_spec(rt, D_FF), _group_spec(rt), _vec_spec(D_MODEL)],
        out_shape=[jax.ShapeDtypeStruct((rt.rows, D_MODEL), BF16), jax.ShapeDtypeStruct((rt.rows, D_FF), BF16),
                   _group_shape(rt), _vec_shape()],
        args=[dh, y, r, wg["down"][0], mod, g_post_mlp], aliases={}, semantics=("arbitrary",))


def _mlp_up_bwd(rt, da, wg, h1, dh, mod, g_pre_mlp, name):
    def body(da_ref, wu_ref, h1_ref, dh_ref, mod_ref, g_ref, dh1_ref, dsh_ref, dsc_ref, dg_ref):
        i = pl.program_id(0)
        du = jnp.zeros((rt.tm, D_MODEL), F32)
        for k in range(D_FF // 1024):
            du = du + lax.dot_general(da_ref[:, k * 1024:(k + 1) * 1024], _w_chunk(wu_ref, k), NT, preferred_element_type=F32)
        d, dsh, dsc, dg = _norm_mod_bwd_val(du, h1_ref[...], g_ref[...], 1.0 + mod_ref[0, 4:5, :])
        dh1_ref[...] = dh_ref[...] + d
        _accumulate(rt, i, [(dsh_ref, dsh), (dsc_ref, dsc)], [(dg_ref, dg)])

    return pl.pallas_call(
        body, name=name, grid=(rt.n_tiles,),
        in_specs=[_row_spec(rt, D_FF), _gathered_spec(wg, "up"), _row_spec(rt, D_MODEL), _row_spec(rt, D_MODEL),
                  _mod_spec(rt), _vec_spec(D_MODEL)],
        out_specs=[_row_spec(rt, D_MODEL), _group_spec(rt), _group_spec(rt), _vec_spec(D_MODEL)],
        out_shape=[jax.ShapeDtypeStruct((rt.rows, D_MODEL), F32), _group_shape(rt), _group_shape(rt), _vec_shape()],
        compiler_params=_params(("arbitrary",)),
    )(da, wg["up"][0], h1, dh, mod, g_pre_mlp)


def _wgrad_packed(rt, a, b, kind, off, n_rows, p_prev, name):
    h = PACK_HEIGHT[kind]
    tk = rt.tm

    def body(a_ref, b_ref, *rest):
        o_ref = rest[-1]
        i = pl.program_id(0)

        @pl.when(i == 0)
        def _():
            o_ref[...] = jnp.zeros_like(o_ref)

        if kind == "out":
            res = lax.dot_general(a_ref[...], b_ref[...], TN, preferred_element_type=F32)
            for k in range(4):
                for c in range(2):
                    o_ref[c, k] += res[(2 * k + c) * h:(2 * k + c + 1) * h]
        else:
            for k in range(4):
                if kind == "up":
                    res = lax.dot_general(a_ref[...], b_ref[:, k * 1024:(k + 1) * 1024], TN, preferred_element_type=F32)
                else:
                    res = lax.dot_general(a_ref[:, k * 1024:(k + 1) * 1024], b_ref[...], TN, preferred_element_type=F32)
                o_ref[0, k] += res[0:h]
                o_ref[1, k] += res[h:2 * h]

    in_specs = [pl.BlockSpec((tk, a.shape[1]), lambda i: (i, 0)), pl.BlockSpec((tk, b.shape[1]), lambda i: (i, 0))]
    args = [a, b]
    aliases = {}
    if p_prev is not None:
        in_specs.append(pl.BlockSpec(memory_space=pl.ANY))
        args.append(p_prev)
        aliases = {2: 0}
    return pl.pallas_call(
        body, name=name, grid=(rt.rows // tk,),
        in_specs=in_specs,
        out_specs=pl.BlockSpec((2, 4, h, 1024), lambda i: (0, 0, off // h, 0)),
        out_shape=jax.ShapeDtypeStruct((2, 4, n_rows, 1024), F32),
        input_output_aliases=aliases,
        compiler_params=_params(("arbitrary",)),
    )(*args)


def _wgrad_plain(rt, a, b, name):
    tk = rt.tm

    def body(a_ref, b_ref, o_ref):
        @pl.when(pl.program_id(0) == 0)
        def _():
            o_ref[...] = jnp.zeros_like(o_ref)

        o_ref[...] += lax.dot_general(a_ref[...], b_ref[...], TN, preferred_element_type=F32)

    return pl.pallas_call(
        body, name=name, grid=(rt.rows // tk,),
        in_specs=[pl.BlockSpec((tk, a.shape[1]), lambda i: (i, 0)), pl.BlockSpec((tk, b.shape[1]), lambda i: (i, 0))],
        out_specs=pl.BlockSpec((a.shape[1], b.shape[1]), lambda i: (0, 0)),
        out_shape=jax.ShapeDtypeStruct((a.shape[1], b.shape[1]), F32),
        compiler_params=_params(("arbitrary",)),
    )(a, b)


def _matmul(a, b, mode, out_dtype, name):
    dims = TN if mode == "tn" else NT
    m = a.shape[1] if mode == "tn" else a.shape[0]
    n = b.shape[1] if mode == "tn" else b.shape[0]

    def body(a_ref, b_ref, o_ref):
        o_ref[...] = lax.dot_general(a_ref[...], b_ref[...], dims, preferred_element_type=F32).astype(out_dtype)

    return pl.pallas_call(body, name=name, out_shape=jax.ShapeDtypeStruct((m, n), out_dtype),
                          compiler_params=pltpu.CompilerParams(vmem_limit_bytes=VMEM_LIMIT))(a, b)


def _loss_grad(rt, h, target, name):
    last = rt.n_lat_tiles - 1

    def body(h_ref, t_ref, dh_ref, sq_ref):
        i = pl.program_id(0)

        @pl.when(i == 0)
        def _():
            sq_ref[...] = jnp.zeros_like(sq_ref)

        @pl.when(i <= last)
        def _():
            e = h_ref[...] - t_ref[...]
            dh_ref[...] = e * (1.0 / D_MODEL)
            sq_ref[...] += jnp.sum(e * e, axis=0, keepdims=True)

        @pl.when(i > last)
        def _():
            dh_ref[...] = jnp.zeros_like(dh_ref)

    return pl.pallas_call(
        body, name=name, grid=(rt.n_tiles,),
        in_specs=[_row_spec(rt, D_MODEL), pl.BlockSpec((rt.tm, D_MODEL), lambda i: (jnp.minimum(i, last), 0))],
        out_specs=[_row_spec(rt, D_MODEL), _vec_spec(D_MODEL)],
        out_shape=[jax.ShapeDtypeStruct((rt.rows, D_MODEL), F32), jax.ShapeDtypeStruct((1, D_MODEL), F32)],
        compiler_params=_params(("arbitrary",)),
    )(h, target)


def _stack_heads(x, kvi):
    x = x.astype(F32)
    tq = x.shape[0]
    lane = lax.broadcasted_iota(jnp.int32, (tq, 128), 1)
    keep = lane < HEAD_DIM if kvi == 0 else lane >= HEAD_DIM
    parts = []
    for p in range(2):
        pair = x[:, p * 128:(p + 1) * 128]
        swapped = pltpu.roll(pair, HEAD_DIM, 1)
        lo_head, hi_head = (pair, swapped) if kvi == 0 else (swapped, pair)
        parts += [jnp.where(keep, lo_head, 0.0), jnp.where(keep, hi_head, 0.0)]
    return jnp.concatenate(parts, axis=0).astype(BF16)


def _unstack_heads(o4, kvi):
    tq = o4.shape[0] // GROUP
    lane = lax.broadcasted_iota(jnp.int32, (tq, 128), 1)
    outs = []
    for p in range(2):
        r_lo, r_hi = o4[(2 * p) * tq:(2 * p + 1) * tq], o4[(2 * p + 1) * tq:(2 * p + 2) * tq]
        if kvi == 0:
            lo, hi = r_lo, pltpu.roll(r_hi, HEAD_DIM, 1)
        else:
            lo, hi = pltpu.roll(r_lo, HEAD_DIM, 1), r_hi
        outs.append(jnp.where(lane < HEAD_DIM, lo, hi))
    return jnp.concatenate(outs, axis=1)


def _per_head(shape, axis, tq, values):
    head = lax.broadcasted_iota(jnp.int32, shape, axis) // tq
    out = jnp.zeros(shape, F32)
    for g in range(GROUP):
        out = jnp.where(head == g, values[g], out)
    return out


KEY_CHUNK = 512


def _key_chunks(k_ref, v_ref, n):
    kc = min(KEY_CHUNK, n)
    return [(k_ref[c * kc:(c + 1) * kc, :], v_ref[c * kc:(c + 1) * kc, :], None) for c in range(n // kc)]


def _softmax_fwd(qs, chunks, sink_col):
    if sink_col is None:
        m = l = acc = None
    else:
        m, l, acc = sink_col, jnp.ones_like(sink_col), jnp.zeros((qs.shape[0], 128), F32)
    for k, v, mask in chunks:
        s = lax.dot_general(qs, k, NT, preferred_element_type=F32)
        if mask is not None:
            s = jnp.where(mask, s, NEG_BIG)
        cm = jnp.max(s, axis=1, keepdims=True)
        m_new = cm if m is None else jnp.maximum(m, cm)
        p = jnp.exp(s - m_new)
        ps = jnp.sum(p, axis=1, keepdims=True)
        pv = jnp.dot(p.astype(BF16), v, preferred_element_type=F32)
        if m is None:
            l, acc = ps, pv
        else:
            alpha = jnp.exp(m - m_new)
            l, acc = alpha * l + ps, alpha * acc + pv
        m = m_new
    return acc / l, m + jnp.log(l)


def _to_rows(col):
    return jnp.transpose(jnp.broadcast_to(col, (col.shape[0], 128)))[0:8, :]


def _softmax_bwd(qs, dos, lse_row, delta_row, chunks):
    dq = jnp.zeros((qs.shape[0], 128), F32)
    grads = []
    for k, v, mask in chunks:
        s = lax.dot_general(k, qs, NT, preferred_element_type=F32)
        if mask is not None:
            s = jnp.where(mask, s, NEG_BIG)
        p = jnp.exp(s - lse_row)
        dp = lax.dot_general(v, dos, NT, preferred_element_type=F32)
        ds = (p * (dp - delta_row)).astype(BF16)
        dv = jnp.dot(p.astype(BF16), dos, preferred_element_type=F32)
        dk = jnp.dot(ds, qs, preferred_element_type=F32)
        dq = dq + lax.dot_general(ds, k, TN, preferred_element_type=F32)
        grads.append((dk, dv))
    return dq, grads


def _band(qi, tq, seq):
    span = tq + 2 * WINDOW
    start = pl.multiple_of(jnp.clip(qi * tq - WINDOW, 0, seq - span), 64)
    return start, span


def _band_mask(qi, tq, start, span, query_axis):
    shape = (GROUP * tq, span) if query_axis == 0 else (span, GROUP * tq)
    qpos = qi * tq + lax.broadcasted_iota(jnp.int32, shape, query_axis) % tq
    kpos = start + lax.broadcasted_iota(jnp.int32, shape, 1 - query_axis)
    return jnp.abs(kpos - qpos) <= WINDOW


def _qkv_specs(rt, tq, q_row, ctx_row, with_latent):
    specs = [pl.BlockSpec((tq, 256), functools.partial(lambda b, i, col: (q_row(b, i), col), col=col)) for col in (0, 1, 3, 4)]
    if with_latent:
        specs += [pl.BlockSpec((rt.seq, 128), functools.partial(lambda b, i, col: (b, col), col=col))
                  for col in (COL_KA, COL_VA, COL_KB, COL_VB)]
    specs += [pl.BlockSpec((rt.ctx, 128), functools.partial(lambda b, i, col: (ctx_row(b), col), col=col))
              for col in (COL_KA, COL_VA, COL_KB, COL_VB)]
    return specs


def _attn_fwd(rt, qkvp, sink, o_prev, name, comm=None):
    latent = o_prev is None
    seq, ctx, nb = rt.seq, rt.ctx, rt.nb
    tq = 128 if latent else ctx
    nq = seq // tq if latent else 1
    ctx_blk0 = rt.n_lat // ctx
    q_row = (lambda b, i: b * nq + i) if latent else (lambda b, i: ctx_blk0 + b)

    def body(sink_ref, qa0, qa1, qb0, qb1, *rest):
        if latent:
            kal, val, kbl, vbl, kac, vac, kbc, vbc, o_ref, lse_ref = rest
        else:
            kac, vac, kbc, vbc, _, o_ref, lse_ref = rest
        qi = pl.program_id(1)
        for kvi, (qa, qb) in enumerate(((qa0, qb0), (qa1, qb1))):
            src_a = _key_chunks(kac, vac, ctx)
            src_b = _key_chunks(kbc, vbc, ctx)
            if latent:
                src_a += _key_chunks(kal, val, seq)
                start, span = _band(qi, tq, seq)
                src_b.append((kbl[pl.ds(start, span), :], vbl[pl.ds(start, span), :], _band_mask(qi, tq, start, span, 0)))
            oa, lse = _softmax_fwd(_stack_heads(qa[...], kvi), src_a, None)
            o_ref[:, kvi * 256:(kvi + 1) * 256] = _unstack_heads(oa, kvi).astype(BF16)
            lse_ref[0, kvi] = _to_rows(lse)
            sink_col = _per_head((GROUP * tq, 1), 0, tq, [sink_ref[kvi * GROUP + g] for g in range(GROUP)])
            ob, lse = _softmax_fwd(_stack_heads(qb[...], kvi), src_b, sink_col)
            o_ref[:, 512 + kvi * 256:512 + (kvi + 1) * 256] = _unstack_heads(ob, kvi).astype(BF16)
            lse_ref[0, 2 + kvi] = _to_rows(lse)

    specs = _qkv_specs(rt, tq, q_row, lambda b: ctx_blk0 + b, latent)
    args = [sink] + [qkvp] * len(specs)
    in_specs = [pl.BlockSpec(memory_space=pltpu.SMEM)] + specs
    aliases = {}
    if not latent:
        in_specs.append(pl.BlockSpec(memory_space=pl.ANY))
        args.append(o_prev)
        aliases = {len(args) - 1: 0}
    return _comm_call(
        body, comm, name=name, grid=(nb, nq),
        in_specs=in_specs,
        out_specs=[pl.BlockSpec((tq, 1024), lambda b, i: (q_row(b, i), 0)),
                   pl.BlockSpec((1, 4, 8, GROUP * tq), lambda b, i: (b * nq + i, 0, 0, 0))],
        out_shape=[jax.ShapeDtypeStruct((rt.rows, 1024), BF16), jax.ShapeDtypeStruct((nb * nq, 4, 8, GROUP * tq), F32)],
        args=args, aliases=aliases, semantics=("parallel", "parallel"))


def _attn_bwd(rt, qkvp, o, lse, do, sink, prev, name, comm=None):
    latent = prev is None
    seq, ctx, nb = rt.seq, rt.ctx, rt.nb
    tq = 128 if latent else ctx
    nq = seq // tq if latent else 1
    ctx_blk0 = rt.n_lat // ctx
    q_row = (lambda b, i: b * nq + i) if latent else (lambda b, i: ctx_blk0 + b)
    kc = min(KEY_CHUNK, seq)

    def body(sink_ref, qa0, qa1, qb0, qb1, *rest):
        if latent:
            kal, val, kbl, vbl, kac, vac, kbc, vbc, do_ref, o_ref, lse_ref, dq_ref, dl_ref, dc_ref, dsink_ref = rest
        else:
            kac, vac, kbc, vbc, do_ref, o_ref, lse_ref, c1_ref, _, _, dq_ref, dc_ref, dsink_ref = rest
        b, qi = pl.program_id(0), pl.program_id(1)

        def rows_of(cols, kvi, mixer):
            dos = _stack_heads(do_ref[:, cols], kvi)
            delta = jnp.sum(dos.astype(F32) * _stack_heads(o_ref[:, cols], kvi).astype(F32), axis=1, keepdims=True)
            return dos, lse_ref[0, 2 * mixer + kvi, 0:1, :], _to_rows(delta)[0:1, :]

        @pl.when(jnp.logical_and(b == 0, qi == 0))
        def _():
            dsink_ref[...] = jnp.zeros_like(dsink_ref)

        if latent:
            @pl.when(qi == 0)
            def _():
                dc_ref[...] = jnp.zeros_like(dc_ref)
                dl_ref[...] = jnp.zeros_like(dl_ref)
        else:
            dc_ref[...] = c1_ref[...]

        head_row = lax.broadcasted_iota(jnp.int32, (8, 128), 0)
        for kvi, (qa, qb) in enumerate(((qa0, qb0), (qa1, qb1))):
            cols = slice(kvi * 256, (kvi + 1) * 256)
            dos, lse_row, delta_row = rows_of(cols, kvi, 0)
            src = _key_chunks(kac, vac, ctx)
            if latent:
                src += _key_chunks(kal, val, seq)
            dq4, grads = _softmax_bwd(_stack_heads(qa[...], kvi), dos, lse_row, delta_row, src)
            dq_ref[:, cols] = _unstack_heads(dq4, kvi)
            dc_ref[:, 0:128] += grads[0][0]
            dc_ref[:, 128:256] += grads[0][1]
            for c, (dk, dv) in enumerate(grads[1:]):
                dl_ref[c * kc:(c + 1) * kc, 0:128] += dk
                dl_ref[c * kc:(c + 1) * kc, 128:256] += dv
            cols = slice(512 + kvi * 256, 512 + (kvi + 1) * 256)
            dos, lse_row, delta_row = rows_of(cols, kvi, 1)
            src = _key_chunks(kbc, vbc, ctx)
            if latent:
                start, span = _band(qi, tq, seq)
                src.append((kbl[pl.ds(start, span), :], vbl[pl.ds(start, span), :], _band_mask(qi, tq, start, span, 1)))
            dq4, grads = _softmax_bwd(_stack_heads(qb[...], kvi), dos, lse_row, delta_row, src)
            dq_ref[:, cols] = _unstack_heads(dq4, kvi)
            dc_ref[:, 256:384] += grads[0][0]
            dc_ref[:, 384:512] += grads[0][1]
            if latent:
                dl_ref[pl.ds(start, span), 256:384] += grads[1][0]
                dl_ref[pl.ds(start, span), 384:512] += grads[1][1]
            sink_row = _per_head((1, GROUP * tq), 1, tq, [sink_ref[kvi * GROUP + g] for g in range(GROUP)])
            dsink = -jnp.exp(sink_row - lse_row) * delta_row
            head = lax.broadcasted_iota(jnp.int32, (1, GROUP * tq), 1) // tq
            upd = jnp.zeros((8, 128), F32)
            for g in range(GROUP):
                upd = jnp.where(head_row == kvi * GROUP + g, jnp.sum(jnp.where(head == g, dsink, 0.0)), upd)
            dsink_ref[...] += upd

    specs = _qkv_specs(rt, tq, q_row, lambda b: ctx_blk0 + b, latent)
    q_rows_spec = pl.BlockSpec((tq, 1024), lambda b, i: (q_row(b, i), 0))
    in_specs = ([pl.BlockSpec(memory_space=pltpu.SMEM)] + specs
                + [q_rows_spec, q_rows_spec, pl.BlockSpec((1, 4, 8, GROUP * tq), lambda b, i: (b * nq + i, 0, 0, 0))])
    args = [sink] + [qkvp] * len(specs) + [do, o, lse]
    dq_shape = jax.ShapeDtypeStruct((rt.rows, 1024), F32)
    dkv_shape = jax.ShapeDtypeStruct((rt.rows, 512), F32)
    dsink_spec, dsink_shape = pl.BlockSpec((8, 128), lambda b, i: (0, 0)), jax.ShapeDtypeStruct((8, 128), F32)
    dq_spec = pl.BlockSpec((tq, 1024), lambda b, i: (q_row(b, i), 0))
    if latent:
        out_specs = [dq_spec, pl.BlockSpec((seq, 512), lambda b, i: (b, 0)), pl.BlockSpec((ctx, 512), lambda b, i: (b, 0)), dsink_spec]
        out_shape = [dq_shape, dkv_shape, jax.ShapeDtypeStruct((rt.n_ctx, 512), F32), dsink_shape]
        aliases = {}
    else:
        dq_prev, dkv_prev, c1 = prev
        in_specs += [pl.BlockSpec((ctx, 512), lambda b, i: (b, 0)), pl.BlockSpec(memory_space=pl.ANY), pl.BlockSpec(memory_space=pl.ANY)]
        args += [c1, dq_prev, dkv_prev]
        out_specs = [dq_spec, pl.BlockSpec((ctx, 512), lambda b, i: (ctx_blk0 + b, 0)), dsink_spec]
        out_shape = [dq_shape, dkv_shape, dsink_shape]
        aliases = {len(args) - 2: 0, len(args) - 1: 1}
    return _comm_call(body, comm, name=name, grid=(nb, nq), in_specs=in_specs, out_specs=out_specs, out_shape=out_shape,
                      args=args, aliases=aliases, semantics=("arbitrary", "arbitrary"))


def _silu(x):
    return x / (1.0 + jnp.exp(-x))


def _ada_fwd(cond, w_half, b_half, name):
    rows = cond.shape[0]
    cols = w_half.shape[2]

    def body(c_ref, w_ref, b_ref, x_ref, o_ref):
        xs = _silu(c_ref[...]).astype(BF16)
        x_ref[...] = xs
        for l in range(DEPTH):
            o_ref[l] = jnp.dot(xs, w_ref[l].astype(BF16), preferred_element_type=F32) + b_ref[l]

    return pl.pallas_call(
        body, name=name,
        out_shape=[jax.ShapeDtypeStruct((rows, D_MODEL), BF16), jax.ShapeDtypeStruct((DEPTH, rows, cols), F32)],
        compiler_params=pltpu.CompilerParams(vmem_limit_bytes=VMEM_LIMIT),
    )(cond, w_half, b_half)


def _dev_sum(x, name):
    _, r, c = x.shape

    def body(x_ref, o_ref):
        v = x_ref[0]
        for d in range(1, N_DEV):
            v = v + x_ref[d]
        o_ref[...] = v

    return pl.pallas_call(body, name=name, out_shape=jax.ShapeDtypeStruct((r, c), F32))(x)


def _c_ctx_grad(parts, c_ctx, name):
    def body(p_ref, c_ref, o_ref):
        v = p_ref[0, 0:1, :]
        for d in range(1, N_DEV):
            v = v + p_ref[d, 0:1, :]
        c = c_ref[...]
        sg = 1.0 / (1.0 + jnp.exp(-c))
        o_ref[...] = v * (sg * (1.0 + c * (1.0 - sg)))

    return pl.pallas_call(body, name=name, out_shape=jax.ShapeDtypeStruct((1, D_MODEL), F32))(parts, c_ctx)


def _adamw(w, g, m, v, name, comm=None):
    r, c = w.shape
    tr = _pick(r, (256, 128, 64, 32, 24, 16, 8))
    c1 = 1.0 / (1.0 - ADAM_B1 ** ADAM_STEP)
    c2 = 1.0 / (1.0 - ADAM_B2 ** ADAM_STEP)

    def body(w_ref, g_ref, m_ref, v_ref, d_ref, nm_ref, nv_ref):
        g_ = g_ref[...]
        nm = ADAM_B1 * m_ref[...] + (1.0 - ADAM_B1) * g_
        nv = ADAM_B2 * v_ref[...] + (1.0 - ADAM_B2) * (g_ * g_)
        d_ref[...] = -ADAM_LR * ((nm * c1) / (jnp.sqrt(nv * c2) + ADAM_EPS) + ADAM_WD * w_ref[...])
        nm_ref[...] = nm
        nv_ref[...] = nv

    spec = pl.BlockSpec((tr, c), lambda i: (i, 0))
    return _comm_call(body, comm, name=name, grid=(r // tr,), in_specs=[spec] * 4, out_specs=[spec] * 3,
                      out_shape=[jax.ShapeDtypeStruct((r, c), F32)] * 3, args=[w, g, m, v], aliases={}, semantics=("parallel",))


def _local_step(x, ctx, target, mods, gam, qn, kn, sink, w_first, w_layers, packed, c_idx, k_idx):
    nb, seq, _ = x.shape
    rt = _Rows(nb, seq, ctx.shape[1])
    tables = _rope_tables(rt)
    fuse = packed is not None
    h = jnp.concatenate([x.reshape(rt.n_lat, D_MODEL), ctx.reshape(rt.n_ctx, D_MODEL)], axis=0)
    wg = [{}, {}] if fuse else [dict(w) for w in w_layers]
    w_in = [_unpack_in_weight(w_first), None]
    saved = []
    for l in range(DEPTH):
        g_pre_mix, g_post_mix, g_pre_mlp, g_post_mlp = gam[l]
        if l == 1:
            buf, off = wg[1]["in"]
            w_in[1] = _unpack_in_weight(buf[:, off:off + PACK_HEIGHT["in"]])
        u, qkv, qkvp = _in_fwd(rt, h, g_pre_mix, mods[l], w_in[l], tables, qn[l], kn[l], f"in_fwd{l}")
        if fuse and l == 0:
            o, lse_lat, w_l0, w_mix1 = _attn_fwd(rt, qkvp, sink[l], None, f"attn_lat_fwd{l}", comm=_gather_comm(packed, [W_LAYER0, W_MIX1]))
            wg[0] = {kind: (w_l0, PACK_OFF[(kind, 0)] - W_LAYER0[0]) for kind in ("up", "down", "out")}
            wg[1] = {kind: (w_mix1, PACK_OFF[(kind, 1)] - W_MIX1[0]) for kind in ("out", "in")}
        else:
            o, lse_lat = _attn_fwd(rt, qkvp, sink[l], None, f"attn_lat_fwd{l}")
        o, lse_ctx = _attn_fwd(rt, qkvp, sink[l], o, f"attn_ctx_fwd{l}")
        mix, h1, u2 = _out_fwd(rt, o, wg[l], h, mods[l], g_post_mix, g_pre_mlp, f"out_fwd{l}")
        if fuse and l == 0:
            r, y, h2, w_mlp1 = _mlp_fwd(rt, u2, h1, wg[l], mods[l], g_post_mlp, f"mlp_fwd{l}", comm=_gather_comm(packed, [W_MLP1]))
            wg[1].update({kind: (w_mlp1, PACK_OFF[(kind, 1)] - W_MLP1[0]) for kind in ("up", "down")})
        else:
            r, y, h2 = _mlp_fwd(rt, u2, h1, wg[l], mods[l], g_post_mlp, f"mlp_fwd{l}")
        saved.append((h, u, qkv, qkvp, o, lse_lat, lse_ctx, mix, h1, u2, r, y))
        h = h2

    dh, sq = _loss_grad(rt, h, target.reshape(rt.n_lat, D_MODEL), "loss_grad")

    small = [None] * DEPTH
    groups = {}
    for l in reversed(range(DEPTH)):
        g_pre_mix, g_post_mix, g_pre_mlp, g_post_mlp = gam[l]
        h0, u, qkv, qkvp, o, lse_lat, lse_ctx, mix, h1, u2, r, y = saved[l]
        mlp_group, mix_group = (G_LAYER1, G_LAYER1) if l == 1 else (G_MLP0, G_MIX0)
        hide = fuse and l == 0

        outs = _mlp_down_bwd(rt, dh, y, r, wg[l], mods[l], g_post_mlp, f"mlp_down_bwd{l}",
                             comm=_pair_comm(groups[G_LAYER1]) if hide else None)
        dy, da, d_gate_m, d_g_post_mlp = outs[:4]
        if hide:
            sum1 = _pair_sum(groups[G_LAYER1], outs[4], c_idx, "grad_pair_sum_layer1")
        p_mlp = _wgrad_packed(rt, r, dy, "down", PACK_OFF[("down", l)] - mlp_group[0], mlp_group[1], None, f"mlp_down_wgrad{l}")
        dh1, d_sh_m, d_sc_m, d_g_pre_mlp = _mlp_up_bwd(rt, da, wg[l], h1, dh, mods[l], g_pre_mlp, f"mlp_up_bwd{l}")
        p_mlp = _wgrad_packed(rt, u2, da, "up", PACK_OFF[("up", l)] - mlp_group[0], mlp_group[1], p_mlp, f"mlp_up_wgrad{l}")
        outs = _out_bwd(rt, dh1, mix, wg[l], mods[l], g_post_mix, f"out_bwd{l}", comm=_pair_comm(p_mlp) if hide else None)
        dmix, do, d_gate_a, d_g_post_mix = outs[:4]
        if hide:
            sum0 = _pair_sum(p_mlp, outs[4], c_idx, "grad_pair_sum_mlp0")
        p_mix = _wgrad_packed(rt, o, dmix, "out", PACK_OFF[("out", l)] - mix_group[0], mix_group[1],
                              p_mlp if l == 1 else None, f"out_wgrad{l}")
        outs = _attn_bwd(rt, qkvp, o, lse_lat, do, sink[l], None, f"attn_lat_bwd{l}",
                         comm=_chip_comm([sum1[1], sum0[1]]) if hide else None)
        dq, dkv, dkv_c, dsink1 = outs[:4]
        halves = None
        if hide:
            halves = _halves_comm([_owner_sum(sum1[0], outs[4], k_idx, "grad_owner_sum_layer1"),
                                   _owner_sum(sum0[0], outs[5], k_idx, "grad_owner_sum_mlp0")])
        dq, dkv, dsink2 = _attn_bwd(rt, qkvp, o, lse_ctx, do, sink[l], (dq, dkv, dkv_c), f"attn_ctx_bwd{l}")
        outs = _in_bwd(rt, dq, dkv, qkv, tables, qn[l], kn[l], w_in[l], h0, dh1, mods[l], g_pre_mix, l == 0, f"in_bwd{l}", comm=halves)
        dqkv, dh, dqn, dkn, d_sh_a, d_sc_a, d_g_pre_mix = outs[:7]
        if hide:
            groups[G_LAYER1], groups[G_MLP0] = outs[7], outs[8]
        dw_in = _wgrad_plain(rt, u, dqkv, f"in_wgrad{l}")
        o_in = PACK_OFF[("in", l)] - mix_group[0]
        p_mix = p_mix.at[:, :, o_in:o_in + PACK_HEIGHT["in"]].set(_pack_in_grad(dw_in))
        groups[mix_group] = p_mix
        if not hide and l == 0:
            groups[G_MLP0] = p_mlp
        dmod = jnp.concatenate([d_sh_a, d_sc_a, d_gate_a, d_sh_m, d_sc_m, d_gate_m], axis=1)
        small[l] = dict(mod=dmod, gammas=jnp.concatenate([d_g_pre_mix, d_g_post_mix, d_g_pre_mlp, d_g_post_mlp], axis=0),
                        q_norm=dqn, k_norm=dkn, sink=(dsink1 + dsink2)[:, 0])
    return sq, dh.reshape(nb, seq, D_MODEL), [groups[G_LAYER1], groups[G_MLP0], groups[G_MIX0]], small


SMALL_ROWS = 48


def kernel(x, c, ctx, c_ctx, w_ada, b_ada, g_pre_mix, g_post_mix, g_pre_mlp, g_post_mlp, w_in, q_norm, k_norm, sink, w_out, w_up, w_down, loss_target, m_c_ctx, m_w_ada, m_b_ada, m_g_pre_mix, m_g_post_mix, m_g_pre_mlp, m_g_post_mlp, m_w_in, m_q_norm, m_k_norm, m_sink, m_w_out, m_w_up, m_w_down, v_c_ctx, v_w_ada, v_b_ada, v_g_pre_mix, v_g_post_mix, v_g_pre_mlp, v_g_post_mlp, v_w_in, v_q_norm, v_k_norm, v_sink, v_w_out, v_w_up, v_w_down):
    nb = x.shape[0]
    ix, iy, ic = lax.axis_index("x"), lax.axis_index("y"), lax.axis_index("c")
    chip = 2 * ix + iy
    dev = 2 * chip + ic
    ada_cols = w_ada.shape[2] // 2

    c_all = _all_gather(c.reshape(8, (nb * D_MODEL) // 8), "gather_c", False).reshape(N_DEV * nb, D_MODEL)
    n_cond = N_DEV * nb + 1
    cond_rows = 16 * ((n_cond + 15) // 16)
    cond = jnp.concatenate([c_all, c_ctx[None, :], jnp.zeros((cond_rows - n_cond, D_MODEL), F32)], axis=0)
    w_ada_half = lax.dynamic_slice_in_dim(w_ada, ic * ada_cols, ada_cols, 2)
    b_ada_half = lax.dynamic_slice_in_dim(b_ada, dev * ada_cols, ada_cols, 1)[:, None, :]
    x_ada, mod_part = _ada_fwd(cond, w_ada_half, b_ada_half, "ada_fwd")
    mod_g = _all_gather(mod_part.reshape(DEPTH * cond_rows, ada_cols), "gather_mod", False)
    mod_all = mod_g.reshape(N_DEV, DEPTH, cond_rows, ada_cols).transpose(1, 2, 0, 3).reshape(DEPTH, cond_rows, N_MOD * D_MODEL)
    mods = []
    for l in range(DEPTH):
        mine = lax.dynamic_slice_in_dim(mod_all[l], dev * nb, nb, 0)
        mods.append(jnp.concatenate([mine, mod_all[l, n_cond - 1:n_cond]], axis=0).reshape(nb + 1, N_MOD, D_MODEL))

    packed = _pack_local_half(w_in, w_out, w_up, w_down, ic)
    w_first = _gather_rows(packed, W_FIRST, "gather_w_first")
    c_idx, k_idx = ic.reshape(1).astype(jnp.int32), chip.reshape(1).astype(jnp.int32)

    gam = [(g_pre_mix[l][None], g_post_mix[l][None], g_pre_mlp[l][None], g_post_mlp[l][None]) for l in range(DEPTH)]
    qn = [jnp.tile(q_norm[l], 2)[None] for l in range(DEPTH)]
    kn = [jnp.tile(k_norm[l], 2)[None] for l in range(DEPTH)]
    sq, grad_x, (h_layer1, h_mlp0, p_mix0), lg = _local_step(x, ctx, loss_target, mods, gam, qn, kn, [sink[l] for l in range(DEPTH)],
                                                           w_first, None, packed, c_idx, k_idx)
    loss = lax.psum(0.5 * jnp.sum(sq) / D_MODEL, ("x", "y", "c"))

    def step(w, g, m, v, name, comm=None):
        shape = w.shape
        cols = shape[-1]
        outs = _adamw(w.reshape(-1, cols), g.reshape(-1, cols), m.reshape(-1, cols), v.reshape(-1, cols), name, comm)
        return tuple(a.reshape(shape) for a in outs[:3]), outs[3:]

    def piece(halves, kind, l, group):
        o = PACK_OFF[(kind, l)] - group[0]
        rows = halves[:, o:o + PACK_HEIGHT[kind]]
        return rows.reshape(1024, 384) if kind == "in" else rows.reshape(2 * PACK_HEIGHT[kind], 1024)

    grad_w_up = jnp.stack([piece(h_mlp0, "up", 0, G_MLP0), piece(h_layer1, "up", 1, G_LAYER1)])
    grad_w_down = jnp.stack([piece(h_mlp0, "down", 0, G_MLP0), piece(h_layer1, "down", 1, G_LAYER1)])
    res = {}
    upd, (r1,) = step(w_up, grad_w_up, m_w_up, v_w_up, "adamw_w_up", _pair_comm(p_mix0))
    res["w_up"] = (grad_w_up, *upd)
    a32, a16 = _pair_sum(p_mix0, r1, c_idx, "grad_pair_sum_mix0")
    upd, (r2,) = step(w_down, grad_w_down, m_w_down, v_w_down, "adamw_w_down", _chip_comm([a16]))
    res["w_down"] = (grad_w_down, *upd)
    g_mix0 = _owner_sum(a32, r2, k_idx, "grad_owner_sum_mix0")

    def lane_pad(v):
        return jnp.pad(v, (0, D_MODEL - v.shape[0]))[None]

    head_rows = [lane_pad(jnp.concatenate([lg[l]["q_norm"][0], lg[l]["k_norm"][0], lg[l]["sink"]])) for l in range(DEPTH)]
    small = jnp.concatenate([lg[l]["mod"].reshape((nb + 1) * N_MOD, D_MODEL) for l in range(DEPTH)]
                            + [lg[l]["gammas"] for l in range(DEPTH)] + head_rows, axis=0)
    small = jnp.pad(small, ((0, SMALL_ROWS - small.shape[0]), (0, 0)))
    small_g = _all_gather(small, "gather_small", False).reshape(N_DEV, SMALL_ROWS, D_MODEL)
    tot = _dev_sum(small_g, "small_sum")
    mod_rows = (nb + 1) * N_MOD
    o_gam, o_head = DEPTH * mod_rows, DEPTH * mod_rows + 4 * DEPTH
    grad_g = [jnp.stack([tot[o_gam + 4 * l + j] for l in range(DEPTH)]) for j in range(4)]
    grad_q_norm = jnp.stack([tot[o_head + l, 0:64] + tot[o_head + l, 64:128] for l in range(DEPTH)])
    grad_k_norm = jnp.stack([tot[o_head + l, 128:192] + tot[o_head + l, 192:256] for l in range(DEPTH)])
    grad_sink = jnp.stack([tot[o_head + l, 256:264] for l in range(DEPTH)])

    dmod_ex, dmod_ctx = [], []
    for l in range(DEPTH):
        ex = small_g[:, l * mod_rows:l * mod_rows + nb * N_MOD].reshape(N_DEV * nb, N_MOD * D_MODEL)
        cx = tot[l * mod_rows + nb * N_MOD:(l + 1) * mod_rows].reshape(1, N_MOD * D_MODEL)
        dmod_ex.append(ex)
        dmod_ctx.append(cx)
    grad_b_ada = jnp.stack([jnp.sum(dmod_ex[l], axis=0) + dmod_ctx[l][0] for l in range(DEPTH)])
    shard_cols = w_ada.shape[2]
    grad_w_ada, dcc_parts = [], []
    for l in range(DEPTH):
        dm = jnp.concatenate([dmod_ex[l], dmod_ctx[l], jnp.zeros((cond_rows - n_cond, N_MOD * D_MODEL), F32)], axis=0)
        dm_shard = lax.dynamic_slice_in_dim(dm, chip * shard_cols, shard_cols, 1).astype(BF16)
        grad_w_ada.append(_matmul(x_ada, dm_shard, "tn", F32, f"ada_wgrad{l}"))
        dcx = lax.dynamic_slice_in_dim(dmod_ctx[l], dev * ada_cols, ada_cols, 1)
        dcx = jnp.pad(dcx, ((0, 15), (0, 0))).astype(BF16)
        dcc_parts.append(_matmul(dcx, w_ada_half[l].astype(BF16), "nt", F32, f"ada_cond_bwd{l}"))
    grad_w_ada = jnp.stack(grad_w_ada)
    dcc = (dcc_parts[0] + dcc_parts[1])[0:8]
    dcc_g = _all_gather(dcc, "gather_cond_grad", False).reshape(N_DEV, 8, D_MODEL)
    grad_c_ctx = _c_ctx_grad(dcc_g, c_ctx[None], "c_ctx_grad")[0]

    small_names = ["c_ctx", "b_ada", "g_pre_mix", "g_post_mix", "g_pre_mlp", "g_post_mlp", "q_norm", "k_norm", "sink"]
    small_w = [c_ctx, b_ada, g_pre_mix, g_post_mix, g_pre_mlp, g_post_mlp, q_norm, k_norm, sink]
    small_gr = [grad_c_ctx, grad_b_ada] + grad_g + [grad_q_norm, grad_k_norm, grad_sink]
    small_m = [m_c_ctx, m_b_ada, m_g_pre_mix, m_g_post_mix, m_g_pre_mlp, m_g_post_mlp, m_q_norm, m_k_norm, m_sink]
    small_v = [v_c_ctx, v_b_ada, v_g_pre_mix, v_g_post_mix, v_g_pre_mlp, v_g_post_mlp, v_q_norm, v_k_norm, v_sink]
    sizes = [int(np.prod(w.shape)) for w in small_w]
    total = sum(sizes)
    flat_rows = 8 * ((total + 8 * D_MODEL - 1) // (8 * D_MODEL))

    def flat(arrs, fill):
        f = jnp.concatenate([a.reshape(-1) for a in arrs])
        return jnp.concatenate([f, jnp.full((flat_rows * D_MODEL - total,), fill, F32)]).reshape(flat_rows, D_MODEL)

    sd, snm, snv = _adamw(flat(small_w, 0.0), flat(small_gr, 0.0), flat(small_m, 0.0), flat(small_v, 1.0), "adamw_small")[:3]

    def unflat(f):
        f = f.reshape(-1)
        out, off = [], 0
        for w, n in zip(small_w, sizes):
            out.append(f[off:off + n].reshape(w.shape))
            off += n
        return out

    small_d, small_nm, small_nv = unflat(sd), unflat(snm), unflat(snv)
    res.update({n: (g, d, nm, nv) for n, g, d, nm, nv in zip(small_names, small_gr, small_d, small_nm, small_nv)})
    upd, (h_mix0,) = step(w_ada, grad_w_ada, m_w_ada, v_w_ada, "adamw_w_ada", _halves_comm([g_mix0]))
    res["w_ada"] = (grad_w_ada, *upd)
    grad_w_in = jnp.stack([piece(h_mix0, "in", 0, G_MIX0), piece(h_layer1, "in", 1, G_LAYER1)])
    grad_w_out = jnp.stack([piece(h_mix0, "out", 0, G_MIX0), piece(h_layer1, "out", 1, G_LAYER1)])
    res["w_in"] = (grad_w_in, *step(w_in, grad_w_in, m_w_in, v_w_in, "adamw_w_in")[0])
    res["w_out"] = (grad_w_out, *step(w_out, grad_w_out, m_w_out, v_w_out, "adamw_w_out")[0])

    order = ["c_ctx", "w_ada", "b_ada", "g_pre_mix", "g_post_mix", "g_pre_mlp", "g_post_mlp", "w_in", "q_norm", "k_norm", "sink", "w_out", "w_up", "w_down"]
    return (loss, grad_x, *[res[n][0] for n in order], *[res[n][1] for n in order],
            *[res[n][2] for n in order], *[res[n][3] for n in order])
```

```python
import functools

import jax
import jax.numpy as jnp
import numpy as np
from jax import lax
from jax.experimental import pallas as pl
from jax.experimental.pallas import tpu as pltpu

F32 = jnp.float32
BF16 = jnp.bfloat16

D_MODEL = 1024
HEAD_DIM = 64
GROUP = 4
WINDOW = 128
N_MOD = 6
D_FF = 4 * D_MODEL
IN_COLS = 1536
GRID_W = 64
ROPE_THETA = 10000.0
EPS = 1e-6
NEG_BIG = -1e30
Q_SCALE = HEAD_DIM ** -0.5
DEPTH = 2
N_DEV = 8

ADAM_LR = 0.001
ADAM_B1 = 0.9
ADAM_B2 = 0.999
ADAM_EPS = 1e-08
ADAM_WD = 0.01
ADAM_STEP = 10

V7X_VMEM_BYTES = 64 * 1024 * 1024
VMEM_LIMIT = V7X_VMEM_BYTES - 8 * 1024 * 1024

MESH = pl.DeviceIdType.MESH
NT = (((1,), (1,)), ((), ()))
TN = (((0,), (0,)), ((), ()))

COL_KA, COL_VA, COL_KB, COL_VB = 4, 5, 10, 11

PACK_HEIGHT = {"up": 512, "down": 512, "out": 128, "in": 192}
PACK_OFF = {("up", 0): 0, ("down", 0): 512, ("out", 0): 1024, ("in", 0): 1152,
            ("up", 1): 1344, ("down", 1): 1856, ("out", 1): 2368, ("in", 1): 2496}
PACK_ROWS = 2688
LAYER_ROWS = 1344
LOCAL_OFF = {"up": 0, "down": 512, "out": 1024, "in": 1152}
W_FIRST, W_LAYER0, W_MLP1, W_MIX1 = (1152, 192), (0, 1152), (1344, 1024), (2368, 320)
G_LAYER1, G_MLP0, G_MIX0 = (1344, 1344), (0, 1024), (1024, 320)


def _pick(n, cands):
    for t in cands:
        if n % t == 0:
            return t
    raise ValueError(f"no tile for {n}")


def _params(sem):
    return pltpu.CompilerParams(dimension_semantics=sem, vmem_limit_bytes=VMEM_LIMIT)


def _all_gather(x, name, in_hbm):
    m_per, n = x.shape
    space = pl.ANY if in_hbm else pltpu.VMEM

    def body(x_ref, out_ref, send_sems, recv_sems, local_sem):
        x_, y_, c_ = lax.axis_index("x"), lax.axis_index("y"), lax.axis_index("c")
        me, sibling = (x_, y_, c_), (x_, y_, 1 - c_)
        chips = [(1 - x_, y_), (x_, 1 - y_), (1 - x_, 1 - y_)]

        def rows(px, py, pc):
            return out_ref.at[pl.ds((4 * px + 2 * py + pc) * m_per, m_per), :]

        def copy(k, block, to, src=None):
            return pltpu.make_async_remote_copy(
                src_ref=rows(*block) if src is None else src, dst_ref=rows(*block),
                send_sem=send_sems.at[k], recv_sem=recv_sems.at[k], device_id=to, device_id_type=MESH)

        mine = pltpu.make_async_copy(x_ref, rows(*me), local_sem)
        mine.start()
        first = [copy(0, me, sibling, src=x_ref)]
        first += [copy(1 + j, me, (*chip, c_), src=x_ref) for j, chip in enumerate(chips)]
        for cp in first:
            cp.start()
        passed = [copy(4 + j, (*chip, c_), sibling) for j, chip in enumerate(chips)]
        for j, chip in enumerate(chips):
            copy(1 + j, (*chip, c_), me).wait_recv()
            passed[j].start()
        copy(0, sibling, me).wait_recv()
        for j, chip in enumerate(chips):
            copy(4 + j, (*chip, 1 - c_), me).wait_recv()
        for cp in first + passed:
            cp.wait_send()
        mine.wait()

    return pl.pallas_call(
        body, name=name,
        out_shape=jax.ShapeDtypeStruct((N_DEV * m_per, n), x.dtype),
        in_specs=[pl.BlockSpec(memory_space=space)],
        out_specs=pl.BlockSpec(memory_space=space),
        scratch_shapes=[pltpu.SemaphoreType.DMA((7,)), pltpu.SemaphoreType.DMA((7,)), pltpu.SemaphoreType.DMA],
    )(x)


class _Comm:
    def __init__(self, inputs, out_shapes, aliases, n_send, n_recv, start, finish):
        self.inputs, self.out_shapes, self.aliases = list(inputs), list(out_shapes), dict(aliases)
        self.n_send, self.n_recv, self.start, self.finish = n_send, n_recv, start, finish


def _comm_call(compute, comm, *, name, grid, in_specs, out_specs, out_shape, args, aliases, semantics):
    in_specs, out_specs, out_shape, args, aliases = list(in_specs), list(out_specs), list(out_shape), list(args), dict(aliases)
    if comm is None:
        return pl.pallas_call(compute, name=name, grid=grid, in_specs=in_specs, out_specs=out_specs, out_shape=out_shape,
                              input_output_aliases=aliases, compiler_params=_params(semantics))(*args)
    n_in, n_out, n_ci, n_co = len(args), len(out_shape), len(comm.inputs), len(comm.out_shapes)
    hbm = pl.BlockSpec(memory_space=pl.ANY)
    aliases.update({n_in + i: n_out + o for i, o in comm.aliases.items()})

    def body(*refs):
        ins, c_ins = refs[:n_in], refs[n_in:n_in + n_ci]
        outs, c_outs = refs[n_in + n_ci:n_in + n_ci + n_out], refs[n_in + n_ci + n_out:n_in + n_ci + n_out + n_co]
        send_sems, recv_sems = refs[-2:]
        ids = [pl.program_id(a) for a in range(len(grid))]
        first = functools.reduce(jnp.logical_and, [i == 0 for i in ids])
        last = functools.reduce(jnp.logical_and, [i == g - 1 for i, g in zip(ids, grid)])

        @pl.when(first)
        def _():
            comm.start(c_ins, c_outs, send_sems, recv_sems)

        compute(*ins, *outs)

        @pl.when(last)
        def _():
            comm.finish(c_ins, c_outs, send_sems, recv_sems)

    return pl.pallas_call(
        body, name=name, grid=grid,
        in_specs=in_specs + [hbm] * n_ci, out_specs=out_specs + [hbm] * n_co, out_shape=out_shape + comm.out_shapes,
        input_output_aliases=aliases,
        scratch_shapes=[pltpu.SemaphoreType.DMA((comm.n_send,)), pltpu.SemaphoreType.DMA((comm.n_recv,))],
        compiler_params=_params(("arbitrary",) * len(grid)),
    )(*args, *comm.inputs)


def _place():
    x_, y_, c_ = lax.axis_index("x"), lax.axis_index("y"), lax.axis_index("c")
    return x_, y_, c_, [(1 - x_, y_), (x_, 1 - y_), (1 - x_, 1 - y_)]


GATHER_SENDS, GATHER_RECVS = 8, 7


def _gather_copies(packed_ref, wg_ref, send_sems, recv_sems, rows, nth=0):
    r0, n = rows
    x_, y_, c_, chips = _place()
    me, sibling = (x_, y_, c_), (x_, y_, 1 - c_)
    src = packed_ref.at[pl.ds(r0, n), :]

    def slot(px, py, pc):
        return wg_ref.at[4 * px + 2 * py + pc]

    def copy(k, block, to, from_packed=False):
        return pltpu.make_async_remote_copy(src_ref=src if from_packed else slot(*block), dst_ref=slot(*block),
                                            send_sem=send_sems.at[GATHER_SENDS * nth + k], recv_sem=recv_sems.at[GATHER_RECVS * nth + k],
                                            device_id=to, device_id_type=MESH)

    own = [copy(0, me, sibling, True)] + [copy(1 + j, me, (*chip, c_), True) for j, chip in enumerate(chips)]
    passed = [copy(4 + j, (*chip, c_), sibling) for j, chip in enumerate(chips)]
    over_ici = [copy(1 + j, (*chip, c_), me) for j, chip in enumerate(chips)]
    from_sibling = [copy(0, sibling, me)] + [copy(4 + j, (*chip, 1 - c_), me) for j, chip in enumerate(chips)]
    mine = pltpu.make_async_copy(src, slot(*me), send_sems.at[GATHER_SENDS * nth + 7])
    return mine, own, passed, over_ici, from_sibling


def _gather_start(packed_ref, wg_ref, send_sems, recv_sems, rows, nth=0):
    mine, own, _, _, _ = _gather_copies(packed_ref, wg_ref, send_sems, recv_sems, rows, nth)
    mine.start()
    for cp in own:
        cp.start()


def _gather_finish(packed_ref, wg_ref, send_sems, recv_sems, rows, nth=0):
    mine, own, passed, over_ici, from_sibling = _gather_copies(packed_ref, wg_ref, send_sems, recv_sems, rows, nth)
    for arrived, onward in zip(over_ici, passed):
        arrived.wait_recv()
        onward.start()
    for arrived in from_sibling:
        arrived.wait_recv()
    for cp in own + passed:
        cp.wait_send()
    mine.wait()


def _gather_comm(packed, ranges):
    shapes = [jax.ShapeDtypeStruct((N_DEV, n, packed.shape[1]), packed.dtype) for _, n in ranges]

    def start(ins, outs, ss, rs):
        for nth, rows in enumerate(ranges):
            _gather_start(ins[0], outs[nth], ss, rs, rows, nth)

    def finish(ins, outs, ss, rs):
        for nth, rows in enumerate(ranges):
            _gather_finish(ins[0], outs[nth], ss, rs, rows, nth)

    return _Comm([packed], shapes, {}, GATHER_SENDS * len(ranges), GATHER_RECVS * len(ranges), start, finish)


def _pair_copy(p_ref, out_ref, send_sems, recv_sems):
    x_, y_, c_, _ = _place()
    return pltpu.make_async_remote_copy(src_ref=p_ref.at[1 - c_], dst_ref=out_ref,
                                        send_sem=send_sems.at[0], recv_sem=recv_sems.at[0],
                                        device_id=(x_, y_, 1 - c_), device_id_type=MESH)


def _pair_comm(p):
    return _Comm([p], [jax.ShapeDtypeStruct(p.shape[1:], p.dtype)], {}, 1, 1,
                 lambda ins, outs, ss, rs: _pair_copy(ins[0], outs[0], ss, rs).start(),
                 lambda ins, outs, ss, rs: _pair_copy(ins[0], outs[0], ss, rs).wait())


def _chip_copies(a_refs, out_refs, send_sems, recv_sems):
    _, _, c_, chips = _place()
    return [pltpu.make_async_remote_copy(src_ref=a_ref.at[2 * tx + ty], dst_ref=o_ref.at[j],
                                         send_sem=send_sems.at[3 * g + j], recv_sem=recv_sems.at[3 * g + j],
                                         device_id=(tx, ty, c_), device_id_type=MESH)
            for g, (a_ref, o_ref) in enumerate(zip(a_refs, out_refs)) for j, (tx, ty) in enumerate(chips)]


def _chip_start(a_refs, out_refs, send_sems, recv_sems):
    for cp in _chip_copies(a_refs, out_refs, send_sems, recv_sems):
        cp.start()


def _chip_finish(a_refs, out_refs, send_sems, recv_sems):
    for cp in _chip_copies(a_refs, out_refs, send_sems, recv_sems):
        cp.wait()


def _chip_comm(arrays):
    shapes = [jax.ShapeDtypeStruct((3,) + a.shape[1:], a.dtype) for a in arrays]
    return _Comm(arrays, shapes, {}, 3 * len(arrays), 3 * len(arrays), _chip_start, _chip_finish)


def _halves_copies(in_refs, out_refs, send_sems, recv_sems):
    x_, y_, c_, _ = _place()
    return [pltpu.make_async_remote_copy(src_ref=o_ref.at[c_], dst_ref=o_ref.at[c_], send_sem=send_sems.at[i], recv_sem=recv_sems.at[i],
                                         device_id=(x_, y_, 1 - c_), device_id_type=MESH)
            for i, o_ref in enumerate(out_refs)]


def _halves_start(in_refs, out_refs, send_sems, recv_sems):
    for cp in _halves_copies(in_refs, out_refs, send_sems, recv_sems):
        cp.start()


def _halves_finish(in_refs, out_refs, send_sems, recv_sems):
    for cp in _halves_copies(in_refs, out_refs, send_sems, recv_sems):
        cp.wait()


def _halves_comm(arrays):
    shapes = [jax.ShapeDtypeStruct(a.shape, a.dtype) for a in arrays]
    return _Comm(arrays, shapes, {i: i for i in range(len(arrays))}, len(arrays), len(arrays), _halves_start, _halves_finish)


def _comm_alone(comm, name):
    n_ci = len(comm.inputs)
    hbm = pl.BlockSpec(memory_space=pl.ANY)

    def body(*refs):
        c_ins, c_outs, send_sems, recv_sems = refs[:n_ci], refs[n_ci:-2], refs[-2], refs[-1]
        comm.start(c_ins, c_outs, send_sems, recv_sems)
        comm.finish(c_ins, c_outs, send_sems, recv_sems)

    return pl.pallas_call(
        body, name=name, out_shape=comm.out_shapes, in_specs=[hbm] * n_ci, out_specs=[hbm] * len(comm.out_shapes),
        input_output_aliases=comm.aliases,
        scratch_shapes=[pltpu.SemaphoreType.DMA((comm.n_send,)), pltpu.SemaphoreType.DMA((comm.n_recv,))],
    )(*comm.inputs)


SUM_TILES = (512, 384, 320, 256, 192, 128, 64)


def _pair_sum(p, r1, c_idx, name):
    _, _, n, c = p.shape
    tr = _pick(n, SUM_TILES)

    def body(s_ref, p_ref, r_ref, o32_ref, o16_ref):
        v = p_ref[...] + r_ref[...]
        o32_ref[...] = v
        o16_ref[...] = v.astype(BF16)

    blk = pl.BlockSpec((None, tr, c), lambda j, i, s: (j, i, 0))
    grid_spec = pltpu.PrefetchScalarGridSpec(
        num_scalar_prefetch=1, grid=(4, n // tr),
        in_specs=[pl.BlockSpec((None, None, tr, c), lambda j, i, s: (s[0], j, i, 0)), blk],
        out_specs=[blk, blk])
    return pl.pallas_call(
        body, name=name, grid_spec=grid_spec,
        out_shape=[jax.ShapeDtypeStruct((4, n, c), F32), jax.ShapeDtypeStruct((4, n, c), BF16)],
        compiler_params=_params(("arbitrary", "arbitrary")),
    )(c_idx, p, r1)


def _owner_sum(a32, r2, kc_idx, name):
    _, r, c = a32.shape
    tr = _pick(r, SUM_TILES)

    def body(s_ref, a_ref, r_ref, o_ref):
        v = a_ref[...]
        for j in range(3):
            v = v + r_ref[j].astype(F32)
        o_ref[...] = v

    grid_spec = pltpu.PrefetchScalarGridSpec(
        num_scalar_prefetch=1, grid=(r // tr,),
        in_specs=[pl.BlockSpec((None, tr, c), lambda i, s: (s[0], i, 0)),
                  pl.BlockSpec((3, tr, c), lambda i, s: (0, i, 0))],
        out_specs=pl.BlockSpec((None, tr, c), lambda i, s: (s[1], i, 0)))
    return pl.pallas_call(
        body, name=name, grid_spec=grid_spec,
        out_shape=jax.ShapeDtypeStruct((2, r, c), F32),
        compiler_params=_params(("arbitrary",)),
    )(kc_idx, a32, r2)


def _pack_local_half(w_in_s, w_out_s, w_up_s, w_down_s, c_idx):
    parts, row = [], 0
    for (kind, l), off in sorted(PACK_OFF.items(), key=lambda kv: kv[1]):
        if off > row:
            parts.append(jnp.zeros((off - row, 1024), BF16))
        if kind == "up":
            p = lax.dynamic_slice_in_dim(w_up_s[l], c_idx * 512, 512, 0)
        elif kind == "down":
            p = lax.dynamic_slice_in_dim(w_down_s[l], c_idx * 512, 512, 0)
        elif kind == "in":
            p = lax.dynamic_slice_in_dim(w_in_s[l], c_idx * 512, 512, 0).reshape(192, 1024)
        else:
            p = lax.dynamic_slice_in_dim(w_out_s[l], c_idx * 128, 128, 0)
        parts.append(p.astype(BF16))
        row = off + PACK_HEIGHT[kind]
    return jnp.concatenate(parts, axis=0)


def _pack_in_grad(dw_in):
    return dw_in.reshape(2, 512, 4, 384).transpose(0, 2, 1, 3).reshape(2, 4, 192, 1024)


def _unpack_in_weight(pieces):
    return pieces.reshape(4, 2, 512, 384).transpose(1, 2, 0, 3).reshape(1024, IN_COLS)


class _Rows:
    def __init__(self, nb, seq, ctx):
        self.nb, self.seq, self.ctx = nb, seq, ctx
        self.n_lat, self.n_ctx = nb * seq, nb * ctx
        self.rows = self.n_lat + self.n_ctx
        self.tm = _pick(np.gcd(seq, self.n_ctx), (512, 256, 128))
        self.tiles_per_ex = seq // self.tm
        self.n_tiles = self.rows // self.tm
        self.n_lat_tiles = self.n_lat // self.tm
        self.groups = nb + 1

    def group(self, i):
        return jnp.minimum(i // self.tiles_per_ex, self.nb)

    def first_of_group(self, i):
        return jnp.logical_and(i % self.tiles_per_ex == 0, i <= self.n_lat_tiles)


def _mod_spec(rt):
    return pl.BlockSpec((1, N_MOD, D_MODEL), lambda i: (rt.group(i), 0, 0))


def _row_spec(rt, cols):
    return pl.BlockSpec((rt.tm, cols), lambda i: (i, 0))


def _vec_spec(cols):
    return pl.BlockSpec((1, cols), lambda i: (0, 0))


def _group_spec(rt):
    return pl.BlockSpec((1, 1, D_MODEL), lambda i: (rt.group(i), 0, 0))


def _gathered_spec(wg, kind):
    h, off = PACK_HEIGHT[kind], wg[kind][1]
    assert off % h == 0, (kind, off)
    return pl.BlockSpec((N_DEV, h, 1024), lambda *_: (0, off // h, 0), pipeline_mode=pl.Buffered(1))


def _group_shape(rt):
    return jax.ShapeDtypeStruct((rt.groups, 1, D_MODEL), F32)


def _vec_shape(cols=D_MODEL):
    return jax.ShapeDtypeStruct((1, cols), F32)


def _rms_inv(v):
    return lax.rsqrt(jnp.mean(v * v, axis=-1, keepdims=True) + EPS)


def _norm_mod_val(h_, g_, mod_ref, i_shift, i_scale):
    n = h_ * _rms_inv(h_) * g_
    return n * (1.0 + mod_ref[0, i_scale:i_scale + 1, :]) + mod_ref[0, i_shift:i_shift + 1, :]


def _post_norm_val(h_, z_, g_, mod_ref, i_gate):
    return h_ + mod_ref[0, i_gate:i_gate + 1, :] * (z_ * _rms_inv(z_) * g_)


def _post_norm_bwd_val(dh_, z_, g_, gate):
    rinv = _rms_inv(z_)
    n0 = z_ * rinv
    dn = dh_ * gate * g_
    dz = rinv * (dn - n0 * jnp.mean(dn * n0, axis=-1, keepdims=True))
    return dz, jnp.sum(dh_ * n0 * g_, axis=0, keepdims=True), jnp.sum(dh_ * gate * n0, axis=0, keepdims=True)


def _norm_mod_bwd_val(du_, h_, g_, one_sc):
    rinv = _rms_inv(h_)
    n0 = h_ * rinv
    dn = du_ * g_ * one_sc
    dh = rinv * (dn - n0 * jnp.mean(dn * n0, axis=-1, keepdims=True))
    return (dh, jnp.sum(du_, axis=0, keepdims=True), jnp.sum(du_ * n0 * g_, axis=0, keepdims=True),
            jnp.sum(du_ * one_sc * n0, axis=0, keepdims=True))


def _accumulate(rt, i, group_pairs, global_pairs):
    @pl.when(rt.first_of_group(i))
    def _():
        for ref, _ in group_pairs:
            ref[...] = jnp.zeros_like(ref)

    @pl.when(i == 0)
    def _():
        for ref, _ in global_pairs:
            ref[...] = jnp.zeros_like(ref)

    for ref, val in group_pairs:
        ref[0] += val
    for ref, val in global_pairs:
        ref[...] += val


def _rope_tables(rt):
    pos = jnp.arange(rt.seq, dtype=jnp.int32)
    row_ids = (pos // GRID_W).astype(F32)
    col_ids = (pos % GRID_W).astype(F32)
    axis_dim = HEAD_DIM // 2
    inv = ROPE_THETA ** (-jnp.arange(0, axis_dim, 2, dtype=F32) / axis_dim)
    ang_r, ang_c = row_ids[:, None] * inv[None, :], col_ids[:, None] * inv[None, :]
    cr, sr, cc, sc = jnp.cos(ang_r), jnp.sin(ang_r), jnp.cos(ang_c), jnp.sin(ang_c)
    zero = jnp.zeros_like(sr)
    cos = jnp.concatenate([cr, cr, cc, cc], axis=1)
    s_lo = jnp.concatenate([zero, sr, zero, sc], axis=1)
    s_hi = jnp.concatenate([-sr, zero, -sc, zero], axis=1)

    def full(t, ctx_value):
        t = jnp.tile(t, (rt.nb, 2))
        return jnp.concatenate([t, jnp.full((rt.n_ctx, 128), ctx_value, F32)], axis=0)

    return full(cos, 1.0), full(s_lo, 0.0), full(s_hi, 0.0)


def _head_stats(t, lo):
    sq = t * t
    s_lo = jnp.sum(jnp.where(lo, sq, 0.0), axis=1, keepdims=True)
    s_hi = jnp.sum(jnp.where(lo, 0.0, sq), axis=1, keepdims=True)
    return lax.rsqrt(jnp.where(lo, s_lo, s_hi) * (1.0 / HEAD_DIM) + EPS)


def _prep_fwd_body(tm, qkv_ref, c, s1, s2, qn, kn, out_ref):
    lo = lax.broadcasted_iota(jnp.int32, (tm, 128), 1) < HEAD_DIM

    def rope(t):
        return t * c + pltpu.roll(t, 16, 1) * s1 + pltpu.roll(t, 112, 1) * s2

    for j in range(12):
        t = qkv_ref[:, j * 128:(j + 1) * 128]
        if j < 4:
            t = rope(t * _head_stats(t, lo) * qn) * Q_SCALE
        elif j == COL_KA:
            t = rope(t * _head_stats(t, lo) * kn)
        elif 6 <= j < 10:
            t = rope(t) * Q_SCALE
        elif j == COL_KB:
            t = rope(t)
        out_ref[:, j * 128:(j + 1) * 128] = t.astype(BF16)


def _prep_bwd_body(tm, dq_ref, dkv_ref, qkv_ref, c, s1, s2, qn, kn, out_ref):
    lo = lax.broadcasted_iota(jnp.int32, (tm, 128), 1) < HEAD_DIM

    def rope_bwd(d):
        return d * c + pltpu.roll(d * s1, 112, 1) + pltpu.roll(d * s2, 16, 1)

    def norm_bwd(t, g, dy):
        rinv = _head_stats(t, lo)
        n = t * rinv
        dn = dy * g
        prod = dn * n
        m_lo = jnp.sum(jnp.where(lo, prod, 0.0), axis=1, keepdims=True)
        m_hi = jnp.sum(jnp.where(lo, 0.0, prod), axis=1, keepdims=True)
        mean = jnp.where(lo, m_lo, m_hi) * (1.0 / HEAD_DIM)
        return rinv * (dn - n * mean), jnp.sum(dy * n, axis=0, keepdims=True)

    dqn = jnp.zeros((1, 128), F32)
    dkn = jnp.zeros((1, 128), F32)
    for j in range(12):
        t = qkv_ref[:, j * 128:(j + 1) * 128]
        if j < 4:
            d, dg = norm_bwd(t, qn, rope_bwd(dq_ref[:, j * 128:(j + 1) * 128] * Q_SCALE))
            dqn = dqn + dg
        elif j == COL_KA:
            d, dg = norm_bwd(t, kn, rope_bwd(dkv_ref[:, 0:128]))
            dkn = dkn + dg
        elif j == COL_VA:
            d = dkv_ref[:, 128:256]
        elif j < 10:
            d = rope_bwd(dq_ref[:, (j - 2) * 128:(j - 1) * 128] * Q_SCALE)
        elif j == COL_KB:
            d = rope_bwd(dkv_ref[:, 256:384])
        else:
            d = dkv_ref[:, 384:512]
        out_ref[:, j * 128:(j + 1) * 128] = d.astype(BF16)
    return dqn, dkn


def _in_fwd(rt, h, gamma, mod, w_in, tables, qn, kn, name):
    def body(h_ref, g_ref, mod_ref, w_ref, c_ref, s1_ref, s2_ref, qn_ref, kn_ref, u_ref, qkv_ref, qkvp_ref):
        u = _norm_mod_val(h_ref[...], g_ref[...], mod_ref, 0, 1).astype(BF16)
        u_ref[...] = u
        qkv_ref[...] = jnp.dot(u, w_ref[...], preferred_element_type=F32)
        _prep_fwd_body(rt.tm, qkv_ref, c_ref[...], s1_ref[...], s2_ref[...], qn_ref[...], kn_ref[...], qkvp_ref)

    return pl.pallas_call(
        body, name=name, grid=(rt.n_tiles,),
        in_specs=[_row_spec(rt, D_MODEL), _vec_spec(D_MODEL), _mod_spec(rt),
                  pl.BlockSpec((D_MODEL, IN_COLS), lambda i: (0, 0), pipeline_mode=pl.Buffered(1))]
        + [_row_spec(rt, 128)] * 3 + [_vec_spec(128)] * 2,
        out_specs=[_row_spec(rt, D_MODEL), _row_spec(rt, IN_COLS), _row_spec(rt, IN_COLS)],
        out_shape=[jax.ShapeDtypeStruct((rt.rows, D_MODEL), BF16), jax.ShapeDtypeStruct((rt.rows, IN_COLS), F32),
                   jax.ShapeDtypeStruct((rt.rows, IN_COLS), BF16)],
        compiler_params=_params(("parallel",)),
    )(h, gamma, mod, w_in, *tables, qn, kn)


def _in_bwd(rt, dq, dkv, qkv, tables, qn, kn, w_in, h, dres, mod, gamma, latent_only, name, comm=None):
    last = rt.n_lat_tiles - 1

    def body(dq_ref, dkv_ref, qkv_ref, c_ref, s1_ref, s2_ref, qn_ref, kn_ref, w_ref, h_ref, dres_ref, mod_ref, g_ref,
             dqkv_ref, dh_ref, dqn_ref, dkn_ref, dsh_ref, dsc_ref, dg_ref):
        i = pl.program_id(0)
        dqn, dkn = _prep_bwd_body(rt.tm, dq_ref, dkv_ref, qkv_ref, c_ref[...], s1_ref[...], s2_ref[...], qn_ref[...], kn_ref[...], dqkv_ref)
        du = lax.dot_general(dqkv_ref[...], w_ref[...], NT, preferred_element_type=F32)
        dh, dsh, dsc, dg = _norm_mod_bwd_val(du, h_ref[...], g_ref[...], 1.0 + mod_ref[0, 1:2, :])
        if latent_only:
            @pl.when(i <= last)
            def _():
                dh_ref[...] = dres_ref[...] + dh
        else:
            dh_ref[...] = dres_ref[...] + dh
        _accumulate(rt, i, [(dsh_ref, dsh), (dsc_ref, dsc)], [(dg_ref, dg), (dqn_ref, dqn), (dkn_ref, dkn)])

    dh_spec = pl.BlockSpec((rt.tm, D_MODEL), lambda i: (jnp.minimum(i, last), 0)) if latent_only else _row_spec(rt, D_MODEL)
    return _comm_call(
        body, comm, name=name, grid=(rt.n_tiles,),
        in_specs=[_row_spec(rt, 1024), _row_spec(rt, 512), _row_spec(rt, IN_COLS)] + [_row_spec(rt, 128)] * 3 + [_vec_spec(128)] * 2
        + [pl.BlockSpec((D_MODEL, IN_COLS), lambda i: (0, 0), pipeline_mode=pl.Buffered(1)),
           _row_spec(rt, D_MODEL), _row_spec(rt, D_MODEL), _mod_spec(rt), _vec_spec(D_MODEL)],
        out_specs=[_row_spec(rt, IN_COLS), dh_spec, _vec_spec(128), _vec_spec(128),
                   _group_spec(rt), _group_spec(rt), _vec_spec(D_MODEL)],
        out_shape=[jax.ShapeDtypeStruct((rt.rows, IN_COLS), BF16),
                   jax.ShapeDtypeStruct((rt.n_lat if latent_only else rt.rows, D_MODEL), F32),
                   _vec_shape(128), _vec_shape(128), _group_shape(rt), _group_shape(rt), _vec_shape()],
        args=[dq, dkv, qkv, *tables, qn, kn, w_in, h, dres, mod, gamma], aliases={}, semantics=("arbitrary",))


def _out_fwd(rt, o, wg, h, mod, g_post_mix, g_pre_mlp, name):
    def body(o_ref, w_ref, h_ref, mod_ref, gpost_ref, gpre_ref, mix_ref, h1_ref, u2_ref):
        mix = jnp.dot(o_ref[...], w_ref[...].reshape(D_MODEL, D_MODEL), preferred_element_type=F32)
        mix_ref[...] = mix
        h1 = _post_norm_val(h_ref[...], mix, gpost_ref[...], mod_ref, 2)
        h1_ref[...] = h1
        u2_ref[...] = _norm_mod_val(h1, gpre_ref[...], mod_ref, 3, 4).astype(BF16)

    return pl.pallas_call(
        body, name=name, grid=(rt.n_tiles,),
        in_specs=[_row_spec(rt, D_MODEL), _gathered_spec(wg, "out"), _row_spec(rt, D_MODEL), _mod_spec(rt),
                  _vec_spec(D_MODEL), _vec_spec(D_MODEL)],
        out_specs=[_row_spec(rt, D_MODEL)] * 3,
        out_shape=[jax.ShapeDtypeStruct((rt.rows, D_MODEL), F32), jax.ShapeDtypeStruct((rt.rows, D_MODEL), F32),
                   jax.ShapeDtypeStruct((rt.rows, D_MODEL), BF16)],
        compiler_params=_params(("parallel",)),
    )(o, wg["out"][0], h, mod, g_post_mix, g_pre_mlp)


def _out_bwd(rt, dh1, mix, wg, mod, g_post_mix, name, comm=None):
    def body(dh_ref, mix_ref, w_ref, mod_ref, g_ref, dmix_ref, do_ref, dgate_ref, dg_ref):
        i = pl.program_id(0)
        dz, dgate, dg = _post_norm_bwd_val(dh_ref[...], mix_ref[...], g_ref[...], mod_ref[0, 2:3, :])
        dzb = dz.astype(BF16)
        dmix_ref[...] = dzb
        do_ref[...] = lax.dot_general(dzb, w_ref[...].reshape(D_MODEL, D_MODEL), NT, preferred_element_type=F32).astype(BF16)
        _accumulate(rt, i, [(dgate_ref, dgate)], [(dg_ref, dg)])

    return _comm_call(
        body, comm, name=name, grid=(rt.n_tiles,),
        in_specs=[_row_spec(rt, D_MODEL), _row_spec(rt, D_MODEL), _gathered_spec(wg, "out"), _mod_spec(rt), _vec_spec(D_MODEL)],
        out_specs=[_row_spec(rt, D_MODEL), _row_spec(rt, D_MODEL), _group_spec(rt), _vec_spec(D_MODEL)],
        out_shape=[jax.ShapeDtypeStruct((rt.rows, D_MODEL), BF16), jax.ShapeDtypeStruct((rt.rows, D_MODEL), BF16),
                   _group_shape(rt), _vec_shape()],
        args=[dh1, mix, wg["out"][0], mod, g_post_mix], aliases={}, semantics=("arbitrary",))


def _w_chunk(w_ref, k):
    return w_ref[2 * k:2 * k + 2].reshape(1024, 1024)


def _mlp_fwd(rt, u2, h1, wg, mod, g_post_mlp, name, comm=None):
    def body(u2_ref, h1_ref, wu_ref, wd_ref, mod_ref, g_ref, r_ref, y_ref, h2_ref):
        u2_ = u2_ref[...]
        y = jnp.zeros((rt.tm, D_MODEL), F32)
        for k in range(D_FF // 1024):
            a = jnp.maximum(jnp.dot(u2_, _w_chunk(wu_ref, k), preferred_element_type=F32), 0.0)
            rk = (a * a).astype(BF16)
            r_ref[:, k * 1024:(k + 1) * 1024] = rk
            y = y + jnp.dot(rk, _w_chunk(wd_ref, k), preferred_element_type=F32)
        y_ref[...] = y
        h2_ref[...] = _post_norm_val(h1_ref[...], y, g_ref[...], mod_ref, 5)

    return _comm_call(
        body, comm, name=name, grid=(rt.n_tiles,),
        in_specs=[_row_spec(rt, D_MODEL), _row_spec(rt, D_MODEL), _gathered_spec(wg, "up"), _gathered_spec(wg, "down"),
                  _mod_spec(rt), _vec_spec(D_MODEL)],
        out_specs=[_row_spec(rt, D_FF), _row_spec(rt, D_MODEL), _row_spec(rt, D_MODEL)],
        out_shape=[jax.ShapeDtypeStruct((rt.rows, D_FF), BF16), jax.ShapeDtypeStruct((rt.rows, D_MODEL), F32),
                   jax.ShapeDtypeStruct((rt.rows, D_MODEL), F32)],
        args=[u2, h1, wg["up"][0], wg["down"][0], mod, g_post_mlp], aliases={}, semantics=("parallel",))


def _mlp_down_bwd(rt, dh, y, r, wg, mod, g_post_mlp, name, comm=None):
    def body(dh_ref, y_ref, r_ref, wd_ref, mod_ref, g_ref, dy_ref, da_ref, dgate_ref, dg_ref):
        i = pl.program_id(0)
        dz, dgate, dg = _post_norm_bwd_val(dh_ref[...], y_ref[...], g_ref[...], mod_ref[0, 5:6, :])
        dyb = dz.astype(BF16)
        dy_ref[...] = dyb
        for k in range(D_FF // 1024):
            dr = lax.dot_general(dyb, _w_chunk(wd_ref, k), NT, preferred_element_type=F32)
            da_ref[:, k * 1024:(k + 1) * 1024] = (dr * (2.0 * jnp.sqrt(r_ref[:, k * 1024:(k + 1) * 1024].astype(F32)))).astype(BF16)
        _accumulate(rt, i, [(dgate_ref, dgate)], [(dg_ref, dg)])

    return _comm_call(
        body, comm, name=name, grid=(rt.n_tiles,),
        in_specs=[_row_spec(rt, D_MODEL), _row_spec(rt, D_MODEL), _row_spec(rt, D_FF), _gathered_spec(wg, "down"),
                  _mod_spec(rt), _vec_spec(D_MODEL)],
        out_specs=[_row_spec(rt, D_MODEL), _row_spec(rt, D_FF), _group_spec(rt), _vec_spec(D_MODEL)],
        out_shape=[jax.ShapeDtypeStruct((rt.rows, D_MODEL), BF16), jax.ShapeDtypeStruct((rt.rows, D_FF), BF16),
                   _group_shape(rt), _vec_shape()],
        args=[dh, y, r, wg["down"][0], mod, g_post_mlp], aliases={}, semantics=("arbitrary",))


def _mlp_up_bwd(rt, da, wg, h1, dh, mod, g_pre_mlp, name):
    def body(da_ref, wu_ref, h1_ref, dh_ref, mod_ref, g_ref, dh1_ref, dsh_ref, dsc_ref, dg_ref):
        i = pl.program_id(0)
        du = jnp.zeros((rt.tm, D_MODEL), F32)
        for k in range(D_FF // 1024):
            du = du + lax.dot_general(da_ref[:, k * 1024:(k + 1) * 1024], _w_chunk(wu_ref, k), NT, preferred_element_type=F32)
        d, dsh, dsc, dg = _norm_mod_bwd_val(du, h1_ref[...], g_ref[...], 1.0 + mod_ref[0, 4:5, :])
        dh1_ref[...] = dh_ref[...] + d
        _accumulate(rt, i, [(dsh_ref, dsh), (dsc_ref, dsc)], [(dg_ref, dg)])

    return pl.pallas_call(
        body, name=name, grid=(rt.n_tiles,),
        in_specs=[_row_spec(rt, D_FF), _gathered_spec(wg, "up"), _row_spec(rt, D_MODEL), _row_spec(rt, D_MODEL),
                  _mod_spec(rt), _vec_spec(D_MODEL)],
        out_specs=[_row_spec(rt, D_MODEL), _group_spec(rt), _group_spec(rt), _vec_spec(D_MODEL)],
        out_shape=[jax.ShapeDtypeStruct((rt.rows, D_MODEL), F32), _group_shape(rt), _group_shape(rt), _vec_shape()],
        compiler_params=_params(("arbitrary",)),
    )(da, wg["up"][0], h1, dh, mod, g_pre_mlp)


def _wgrad_packed(rt, a, b, kind, off, n_rows, p_prev, name):
    h = PACK_HEIGHT[kind]
    tk = rt.tm

    def body(a_ref, b_ref, *rest):
        o_ref = rest[-1]
        i = pl.program_id(0)

        @pl.when(i == 0)
        def _():
            o_ref[...] = jnp.zeros_like(o_ref)

        if kind == "out":
            res = lax.dot_general(a_ref[...], b_ref[...], TN, preferred_element_type=F32)
            for k in range(4):
                for c in range(2):
                    o_ref[c, k] += res[(2 * k + c) * h:(2 * k + c + 1) * h]
        else:
            for k in range(4):
                if kind == "up":
                    res = lax.dot_general(a_ref[...], b_ref[:, k * 1024:(k + 1) * 1024], TN, preferred_element_type=F32)
                else:
                    res = lax.dot_general(a_ref[:, k * 1024:(k + 1) * 1024], b_ref[...], TN, preferred_element_type=F32)
                o_ref[0, k] += res[0:h]
                o_ref[1, k] += res[h:2 * h]

    in_specs = [pl.BlockSpec((tk, a.shape[1]), lambda i: (i, 0)), pl.BlockSpec((tk, b.shape[1]), lambda i: (i, 0))]
    args = [a, b]
    aliases = {}
    if p_prev is not None:
        in_specs.append(pl.BlockSpec(memory_space=pl.ANY))
        args.append(p_prev)
        aliases = {2: 0}
    return pl.pallas_call(
        body, name=name, grid=(rt.rows // tk,),
        in_specs=in_specs,
        out_specs=pl.BlockSpec((2, 4, h, 1024), lambda i: (0, 0, off // h, 0)),
        out_shape=jax.ShapeDtypeStruct((2, 4, n_rows, 1024), F32),
        input_output_aliases=aliases,
        compiler_params=_params(("arbitrary",)),
    )(*args)


def _wgrad_plain(rt, a, b, name):
    tk = rt.tm

    def body(a_ref, b_ref, o_ref):
        @pl.when(pl.program_id(0) == 0)
        def _():
            o_ref[...] = jnp.zeros_like(o_ref)

        o_ref[...] += lax.dot_general(a_ref[...], b_ref[...], TN, preferred_element_type=F32)

    return pl.pallas_call(
        body, name=name, grid=(rt.rows // tk,),
        in_specs=[pl.BlockSpec((tk, a.shape[1]), lambda i: (i, 0)), pl.BlockSpec((tk, b.shape[1]), lambda i: (i, 0))],
        out_specs=pl.BlockSpec((a.shape[1], b.shape[1]), lambda i: (0, 0)),
        out_shape=jax.ShapeDtypeStruct((a.shape[1], b.shape[1]), F32),
        compiler_params=_params(("arbitrary",)),
    )(a, b)


def _matmul(a, b, mode, out_dtype, name):
    dims = TN if mode == "tn" else NT
    m = a.shape[1] if mode == "tn" else a.shape[0]
    n = b.shape[1] if mode == "tn" else b.shape[0]

    def body(a_ref, b_ref, o_ref):
        o_ref[...] = lax.dot_general(a_ref[...], b_ref[...], dims, preferred_element_type=F32).astype(out_dtype)

    return pl.pallas_call(body, name=name, out_shape=jax.ShapeDtypeStruct((m, n), out_dtype),
                          compiler_params=pltpu.CompilerParams(vmem_limit_bytes=VMEM_LIMIT))(a, b)


def _loss_grad(rt, h, target, name):
    last = rt.n_lat_tiles - 1

    def body(h_ref, t_ref, dh_ref, sq_ref):
        i = pl.program_id(0)

        @pl.when(i == 0)
        def _():
            sq_ref[...] = jnp.zeros_like(sq_ref)

        @pl.when(i <= last)
        def _():
            e = h_ref[...] - t_ref[...]
            dh_ref[...] = e * (1.0 / D_MODEL)
            sq_ref[...] += jnp.sum(e * e, axis=0, keepdims=True)

        @pl.when(i > last)
        def _():
            dh_ref[...] = jnp.zeros_like(dh_ref)

    return pl.pallas_call(
        body, name=name, grid=(rt.n_tiles,),
        in_specs=[_row_spec(rt, D_MODEL), pl.BlockSpec((rt.tm, D_MODEL), lambda i: (jnp.minimum(i, last), 0))],
        out_specs=[_row_spec(rt, D_MODEL), _vec_spec(D_MODEL)],
        out_shape=[jax.ShapeDtypeStruct((rt.rows, D_MODEL), F32), jax.ShapeDtypeStruct((1, D_MODEL), F32)],
        compiler_params=_params(("arbitrary",)),
    )(h, target)


def _stack_heads(x, kvi):
    x = x.astype(F32)
    tq = x.shape[0]
    lane = lax.broadcasted_iota(jnp.int32, (tq, 128), 1)
    keep = lane < HEAD_DIM if kvi == 0 else lane >= HEAD_DIM
    parts = []
    for p in range(2):
        pair = x[:, p * 128:(p + 1) * 128]
        swapped = pltpu.roll(pair, HEAD_DIM, 1)
        lo_head, hi_head = (pair, swapped) if kvi == 0 else (swapped, pair)
        parts += [jnp.where(keep, lo_head, 0.0), jnp.where(keep, hi_head, 0.0)]
    return jnp.concatenate(parts, axis=0).astype(BF16)


def _unstack_heads(o4, kvi):
    tq = o4.shape[0] // GROUP
    lane = lax.broadcasted_iota(jnp.int32, (tq, 128), 1)
    outs = []
    for p in range(2):
        r_lo, r_hi = o4[(2 * p) * tq:(2 * p + 1) * tq], o4[(2 * p + 1) * tq:(2 * p + 2) * tq]
        if kvi == 0:
            lo, hi = r_lo, pltpu.roll(r_hi, HEAD_DIM, 1)
        else:
            lo, hi = pltpu.roll(r_lo, HEAD_DIM, 1), r_hi
        outs.append(jnp.where(lane < HEAD_DIM, lo, hi))
    return jnp.concatenate(outs, axis=1)


def _per_head(shape, axis, tq, values):
    head = lax.broadcasted_iota(jnp.int32, shape, axis) // tq
    out = jnp.zeros(shape, F32)
    for g in range(GROUP):
        out = jnp.where(head == g, values[g], out)
    return out


KEY_CHUNK = 512


def _key_chunks(k_ref, v_ref, n, kc=KEY_CHUNK):
    kc = min(kc, n)
    return [(k_ref[c * kc:(c + 1) * kc, :], v_ref[c * kc:(c + 1) * kc, :], None) for c in range(n // kc)]


def _softmax_fwd(qs, chunks, sink_col):
    logits = []
    for k, _, mask in chunks:
        s = lax.dot_general(qs, k, NT, preferred_element_type=F32)
        logits.append(s if mask is None else jnp.where(mask, s, NEG_BIG))
    m = functools.reduce(jnp.maximum, [jnp.max(s, axis=1, keepdims=True) for s in logits])
    if sink_col is not None:
        m = jnp.maximum(m, sink_col)
    l = jnp.zeros_like(m) if sink_col is None else jnp.exp(sink_col - m)
    acc = jnp.zeros((qs.shape[0], 128), F32)
    for s, (_, v, _) in zip(logits, chunks):
        p = jnp.exp(s - m)
        l = l + jnp.sum(p, axis=1, keepdims=True)
        acc = acc + jnp.dot(p.astype(BF16), v, preferred_element_type=F32)
    return acc / l, m + jnp.log(l)


def _to_rows(col):
    return jnp.transpose(jnp.broadcast_to(col, (col.shape[0], 128)))[0:8, :]


def _softmax_bwd(qs, dos, lse_row, delta_row, chunks):
    dq = jnp.zeros((qs.shape[0], 128), F32)
    grads = []
    for k, v, mask in chunks:
        s = lax.dot_general(k, qs, NT, preferred_element_type=F32)
        if mask is not None:
            s = jnp.where(mask, s, NEG_BIG)
        p = jnp.exp(s - lse_row)
        dp = lax.dot_general(v, dos, NT, preferred_element_type=F32)
        ds = (p * (dp - delta_row)).astype(BF16)
        dv = jnp.dot(p.astype(BF16), dos, preferred_element_type=F32)
        dk = jnp.dot(ds, qs, preferred_element_type=F32)
        dq = dq + lax.dot_general(ds, k, TN, preferred_element_type=F32)
        grads.append((dk, dv))
    return dq, grads


def _band(qi, tq, seq):
    span = tq + 2 * WINDOW
    start = pl.multiple_of(jnp.clip(qi * tq - WINDOW, 0, seq - span), 64)
    return start, span


def _band_mask(qi, tq, start, span, query_axis):
    shape = (GROUP * tq, span) if query_axis == 0 else (span, GROUP * tq)
    qpos = qi * tq + lax.broadcasted_iota(jnp.int32, shape, query_axis) % tq
    kpos = start + lax.broadcasted_iota(jnp.int32, shape, 1 - query_axis)
    return jnp.abs(kpos - qpos) <= WINDOW


def _qkv_specs(rt, tq, q_row, ctx_row, with_latent):
    specs = [pl.BlockSpec((tq, 256), functools.partial(lambda b, i, col: (q_row(b, i), col), col=col)) for col in (0, 1, 3, 4)]
    if with_latent:
        specs += [pl.BlockSpec((rt.seq, 128), functools.partial(lambda b, i, col: (b, col), col=col))
                  for col in (COL_KA, COL_VA, COL_KB, COL_VB)]
    specs += [pl.BlockSpec((rt.ctx, 128), functools.partial(lambda b, i, col: (ctx_row(b), col), col=col))
              for col in (COL_KA, COL_VA, COL_KB, COL_VB)]
    return specs


def _attn_fwd(rt, qkvp, sink, o_prev, name, comm=None):
    latent = o_prev is None
    seq, ctx, nb = rt.seq, rt.ctx, rt.nb
    tq = 128 if latent else ctx
    nq = seq // tq if latent else 1
    ctx_blk0 = rt.n_lat // ctx
    q_row = (lambda b, i: b * nq + i) if latent else (lambda b, i: ctx_blk0 + b)

    def body(sink_ref, qa0, qa1, qb0, qb1, *rest):
        if latent:
            kal, val, kbl, vbl, kac, vac, kbc, vbc, o_ref, lse_ref = rest
        else:
            kac, vac, kbc, vbc, _, o_ref, lse_ref = rest
        qi = pl.program_id(1)
        for kvi, (qa, qb) in enumerate(((qa0, qb0), (qa1, qb1))):
            src_a = _key_chunks(kac, vac, ctx)
            src_b = _key_chunks(kbc, vbc, ctx)
            if latent:
                src_a += _key_chunks(kal, val, seq, seq)
                start, span = _band(qi, tq, seq)
                src_b.append((kbl[pl.ds(start, span), :], vbl[pl.ds(start, span), :], _band_mask(qi, tq, start, span, 0)))
            oa, lse = _softmax_fwd(_stack_heads(qa[...], kvi), src_a, None)
            o_ref[:, kvi * 256:(kvi + 1) * 256] = _unstack_heads(oa, kvi).astype(BF16)
            lse_ref[0, kvi] = _to_rows(lse)
            sink_col = _per_head((GROUP * tq, 1), 0, tq, [sink_ref[kvi * GROUP + g] for g in range(GROUP)])
            ob, lse = _softmax_fwd(_stack_heads(qb[...], kvi), src_b, sink_col)
            o_ref[:, 512 + kvi * 256:512 + (kvi + 1) * 256] = _unstack_heads(ob, kvi).astype(BF16)
            lse_ref[0, 2 + kvi] = _to_rows(lse)

    specs = _qkv_specs(rt, tq, q_row, lambda b: ctx_blk0 + b, latent)
    args = [sink] + [qkvp] * len(specs)
    in_specs = [pl.BlockSpec(memory_space=pltpu.SMEM)] + specs
    aliases = {}
    if not latent:
        in_specs.append(pl.BlockSpec(memory_space=pl.ANY))
        args.append(o_prev)
        aliases = {len(args) - 1: 0}
    return _comm_call(
        body, comm, name=name, grid=(nb, nq),
        in_specs=in_specs,
        out_specs=[pl.BlockSpec((tq, 1024), lambda b, i: (q_row(b, i), 0)),
                   pl.BlockSpec((1, 4, 8, GROUP * tq), lambda b, i: (b * nq + i, 0, 0, 0))],
        out_shape=[jax.ShapeDtypeStruct((rt.rows, 1024), BF16), jax.ShapeDtypeStruct((nb * nq, 4, 8, GROUP * tq), F32)],
        args=args, aliases=aliases, semantics=("parallel", "parallel"))


def _attn_bwd(rt, qkvp, o, lse, do, sink, prev, name, comm=None):
    latent = prev is None
    seq, ctx, nb = rt.seq, rt.ctx, rt.nb
    tq = 128 if latent else ctx
    nq = seq // tq if latent else 1
    ctx_blk0 = rt.n_lat // ctx
    q_row = (lambda b, i: b * nq + i) if latent else (lambda b, i: ctx_blk0 + b)
    kc = min(KEY_CHUNK, seq)

    def body(sink_ref, qa0, qa1, qb0, qb1, *rest):
        if latent:
            kal, val, kbl, vbl, kac, vac, kbc, vbc, do_ref, o_ref, lse_ref, dq_ref, dl_ref, dc_ref, dsink_ref = rest
        else:
            kac, vac, kbc, vbc, do_ref, o_ref, lse_ref, c1_ref, _, _, dq_ref, dc_ref, dsink_ref = rest
        b, qi = pl.program_id(0), pl.program_id(1)

        def rows_of(cols, kvi, mixer):
            dos = _stack_heads(do_ref[:, cols], kvi)
            delta = jnp.sum(dos.astype(F32) * _stack_heads(o_ref[:, cols], kvi).astype(F32), axis=1, keepdims=True)
            return dos, lse_ref[0, 2 * mixer + kvi, 0:1, :], _to_rows(delta)[0:1, :]

        @pl.when(jnp.logical_and(b == 0, qi == 0))
        def _():
            dsink_ref[...] = jnp.zeros_like(dsink_ref)

        if latent:
            @pl.when(qi == 0)
            def _():
                dc_ref[...] = jnp.zeros_like(dc_ref)
                dl_ref[...] = jnp.zeros_like(dl_ref)
        else:
            dc_ref[...] = c1_ref[...]

        head_row = lax.broadcasted_iota(jnp.int32, (8, 128), 0)
        for kvi, (qa, qb) in enumerate(((qa0, qb0), (qa1, qb1))):
            cols = slice(kvi * 256, (kvi + 1) * 256)
            dos, lse_row, delta_row = rows_of(cols, kvi, 0)
            src = _key_chunks(kac, vac, ctx)
            if latent:
                src += _key_chunks(kal, val, seq)
            dq4, grads = _softmax_bwd(_stack_heads(qa[...], kvi), dos, lse_row, delta_row, src)
            dq_ref[:, cols] = _unstack_heads(dq4, kvi)
            dc_ref[:, 0:128] += grads[0][0]
            dc_ref[:, 128:256] += grads[0][1]
            for c, (dk, dv) in enumerate(grads[1:]):
                dl_ref[c * kc:(c + 1) * kc, 0:128] += dk
                dl_ref[c * kc:(c + 1) * kc, 128:256] += dv
            cols = slice(512 + kvi * 256, 512 + (kvi + 1) * 256)
            dos, lse_row, delta_row = rows_of(cols, kvi, 1)
            src = _key_chunks(kbc, vbc, ctx)
            if latent:
                start, span = _band(qi, tq, seq)
                src.append((kbl[pl.ds(start, span), :], vbl[pl.ds(start, span), :], _band_mask(qi, tq, start, span, 1)))
            dq4, grads = _softmax_bwd(_stack_heads(qb[...], kvi), dos, lse_row, delta_row, src)
            dq_ref[:, cols] = _unstack_heads(dq4, kvi)
            dc_ref[:, 256:384] += grads[0][0]
            dc_ref[:, 384:512] += grads[0][1]
            if latent:
                dl_ref[pl.ds(start, span), 256:384] += grads[1][0]
                dl_ref[pl.ds(start, span), 384:512] += grads[1][1]
            sink_row = _per_head((1, GROUP * tq), 1, tq, [sink_ref[kvi * GROUP + g] for g in range(GROUP)])
            dsink = -jnp.exp(sink_row - lse_row) * delta_row
            head = lax.broadcasted_iota(jnp.int32, (1, GROUP * tq), 1) // tq
            upd = jnp.zeros((8, 128), F32)
            for g in range(GROUP):
                upd = jnp.where(head_row == kvi * GROUP + g, jnp.sum(jnp.where(head == g, dsink, 0.0)), upd)
            dsink_ref[...] += upd

    specs = _qkv_specs(rt, tq, q_row, lambda b: ctx_blk0 + b, latent)
    q_rows_spec = pl.BlockSpec((tq, 1024), lambda b, i: (q_row(b, i), 0))
    in_specs = ([pl.BlockSpec(memory_space=pltpu.SMEM)] + specs
                + [q_rows_spec, q_rows_spec, pl.BlockSpec((1, 4, 8, GROUP * tq), lambda b, i: (b * nq + i, 0, 0, 0))])
    args = [sink] + [qkvp] * len(specs) + [do, o, lse]
    dq_shape = jax.ShapeDtypeStruct((rt.rows, 1024), F32)
    dkv_shape = jax.ShapeDtypeStruct((rt.rows, 512), F32)
    dsink_spec, dsink_shape = pl.BlockSpec((8, 128), lambda b, i: (0, 0)), jax.ShapeDtypeStruct((8, 128), F32)
    dq_spec = pl.BlockSpec((tq, 1024), lambda b, i: (q_row(b, i), 0))
    if latent:
        out_specs = [dq_spec, pl.BlockSpec((seq, 512), lambda b, i: (b, 0)), pl.BlockSpec((ctx, 512), lambda b, i: (b, 0)), dsink_spec]
        out_shape = [dq_shape, dkv_shape, jax.ShapeDtypeStruct((rt.n_ctx, 512), F32), dsink_shape]
        aliases = {}
    else:
        dq_prev, dkv_prev, c1 = prev
        in_specs += [pl.BlockSpec((ctx, 512), lambda b, i: (b, 0)), pl.BlockSpec(memory_space=pl.ANY), pl.BlockSpec(memory_space=pl.ANY)]
        args += [c1, dq_prev, dkv_prev]
        out_specs = [dq_spec, pl.BlockSpec((ctx, 512), lambda b, i: (ctx_blk0 + b, 0)), dsink_spec]
        out_shape = [dq_shape, dkv_shape, dsink_shape]
        aliases = {len(args) - 2: 0, len(args) - 1: 1}
    return _comm_call(body, comm, name=name, grid=(nb, nq), in_specs=in_specs, out_specs=out_specs, out_shape=out_shape,
                      args=args, aliases=aliases, semantics=("arbitrary", "arbitrary"))


def _silu(x):
    return x / (1.0 + jnp.exp(-x))


def _ada_fwd(cond, w_half, b_half, name):
    rows = cond.shape[0]
    cols = w_half.shape[2]

    def body(c_ref, w_ref, b_ref, x_ref, o_ref):
        xs = _silu(c_ref[...]).astype(BF16)
        x_ref[...] = xs
        for l in range(DEPTH):
            o_ref[l] = jnp.dot(xs, w_ref[l].astype(BF16), preferred_element_type=F32) + b_ref[l]

    return pl.pallas_call(
        body, name=name,
        out_shape=[jax.ShapeDtypeStruct((rows, D_MODEL), BF16), jax.ShapeDtypeStruct((DEPTH, rows, cols), F32)],
        compiler_params=pltpu.CompilerParams(vmem_limit_bytes=VMEM_LIMIT),
    )(cond, w_half, b_half)


def _dev_sum(x, name):
    _, r, c = x.shape

    def body(x_ref, o_ref):
        v = x_ref[0]
        for d in range(1, N_DEV):
            v = v + x_ref[d]
        o_ref[...] = v

    return pl.pallas_call(body, name=name, out_shape=jax.ShapeDtypeStruct((r, c), F32))(x)


def _c_ctx_grad(parts, c_ctx, name):
    def body(p_ref, c_ref, o_ref):
        v = p_ref[0, 0:1, :]
        for d in range(1, N_DEV):
            v = v + p_ref[d, 0:1, :]
        c = c_ref[...]
        sg = 1.0 / (1.0 + jnp.exp(-c))
        o_ref[...] = v * (sg * (1.0 + c * (1.0 - sg)))

    return pl.pallas_call(body, name=name, out_shape=jax.ShapeDtypeStruct((1, D_MODEL), F32))(parts, c_ctx)


def _adamw(w, g, m, v, name, comm=None):
    r, c = w.shape
    tr = _pick(r, (256, 128, 64, 32, 24, 16, 8))
    c1 = 1.0 / (1.0 - ADAM_B1 ** ADAM_STEP)
    c2 = 1.0 / (1.0 - ADAM_B2 ** ADAM_STEP)

    def body(w_ref, g_ref, m_ref, v_ref, d_ref, nm_ref, nv_ref):
        g_ = g_ref[...]
        nm = ADAM_B1 * m_ref[...] + (1.0 - ADAM_B1) * g_
        nv = ADAM_B2 * v_ref[...] + (1.0 - ADAM_B2) * (g_ * g_)
        d_ref[...] = -ADAM_LR * ((nm * c1) / (jnp.sqrt(nv * c2) + ADAM_EPS) + ADAM_WD * w_ref[...])
        nm_ref[...] = nm
        nv_ref[...] = nv

    spec = pl.BlockSpec((tr, c), lambda i: (i, 0))
    return _comm_call(body, comm, name=name, grid=(r // tr,), in_specs=[spec] * 4, out_specs=[spec] * 3,
                      out_shape=[jax.ShapeDtypeStruct((r, c), F32)] * 3, args=[w, g, m, v], aliases={}, semantics=("parallel",))


def _local_step(x, ctx, target, mods, gam, qn, kn, sink, w_first, w_layers, packed, c_idx, k_idx):
    nb, seq, _ = x.shape
    rt = _Rows(nb, seq, ctx.shape[1])
    tables = _rope_tables(rt)
    fuse = packed is not None
    h = jnp.concatenate([x.reshape(rt.n_lat, D_MODEL), ctx.reshape(rt.n_ctx, D_MODEL)], axis=0)
    wg = [{}, {}] if fuse else [dict(w) for w in w_layers]
    w_in = [_unpack_in_weight(w_first), None]
    saved = []
    for l in range(DEPTH):
        g_pre_mix, g_post_mix, g_pre_mlp, g_post_mlp = gam[l]
        if l == 1:
            buf, off = wg[1]["in"]
            w_in[1] = _unpack_in_weight(buf[:, off:off + PACK_HEIGHT["in"]])
        u, qkv, qkvp = _in_fwd(rt, h, g_pre_mix, mods[l], w_in[l], tables, qn[l], kn[l], f"in_fwd{l}")
        if fuse and l == 0:
            o, lse_lat, w_l0, w_mix1 = _attn_fwd(rt, qkvp, sink[l], None, f"attn_lat_fwd{l}", comm=_gather_comm(packed, [W_LAYER0, W_MIX1]))
            wg[0] = {kind: (w_l0, PACK_OFF[(kind, 0)] - W_LAYER0[0]) for kind in ("up", "down", "out")}
            wg[1] = {kind: (w_mix1, PACK_OFF[(kind, 1)] - W_MIX1[0]) for kind in ("out", "in")}
        else:
            o, lse_lat = _attn_fwd(rt, qkvp, sink[l], None, f"attn_lat_fwd{l}")
        o, lse_ctx = _attn_fwd(rt, qkvp, sink[l], o, f"attn_ctx_fwd{l}")
        mix, h1, u2 = _out_fwd(rt, o, wg[l], h, mods[l], g_post_mix, g_pre_mlp, f"out_fwd{l}")
        if fuse and l == 0:
            r, y, h2, w_mlp1 = _mlp_fwd(rt, u2, h1, wg[l], mods[l], g_post_mlp, f"mlp_fwd{l}", comm=_gather_comm(packed, [W_MLP1]))
            wg[1].update({kind: (w_mlp1, PACK_OFF[(kind, 1)] - W_MLP1[0]) for kind in ("up", "down")})
        else:
            r, y, h2 = _mlp_fwd(rt, u2, h1, wg[l], mods[l], g_post_mlp, f"mlp_fwd{l}")
        saved.append((h, u, qkv, qkvp, o, lse_lat, lse_ctx, mix, h1, u2, r, y))
        h = h2

    dh, sq = _loss_grad(rt, h, target.reshape(rt.n_lat, D_MODEL), "loss_grad")

    small = [None] * DEPTH
    groups = {}
    for l in reversed(range(DEPTH)):
        g_pre_mix, g_post_mix, g_pre_mlp, g_post_mlp = gam[l]
        h0, u, qkv, qkvp, o, lse_lat, lse_ctx, mix, h1, u2, r, y = saved[l]
        mlp_group, mix_group = (G_LAYER1, G_LAYER1) if l == 1 else (G_MLP0, G_MIX0)
        hide = fuse and l == 0

        outs = _mlp_down_bwd(rt, dh, y, r, wg[l], mods[l], g_post_mlp, f"mlp_down_bwd{l}",
                             comm=_pair_comm(groups[G_LAYER1]) if hide else None)
        dy, da, d_gate_m, d_g_post_mlp = outs[:4]
        if hide:
            sum1 = _pair_sum(groups[G_LAYER1], outs[4], c_idx, "grad_pair_sum_layer1")
        p_mlp = _wgrad_packed(rt, r, dy, "down", PACK_OFF[("down", l)] - mlp_group[0], mlp_group[1], None, f"mlp_down_wgrad{l}")
        dh1, d_sh_m, d_sc_m, d_g_pre_mlp = _mlp_up_bwd(rt, da, wg[l], h1, dh, mods[l], g_pre_mlp, f"mlp_up_bwd{l}")
        p_mlp = _wgrad_packed(rt, u2, da, "up", PACK_OFF[("up", l)] - mlp_group[0], mlp_group[1], p_mlp, f"mlp_up_wgrad{l}")
        outs = _out_bwd(rt, dh1, mix, wg[l], mods[l], g_post_mix, f"out_bwd{l}", comm=_pair_comm(p_mlp) if hide else None)
        dmix, do, d_gate_a, d_g_post_mix = outs[:4]
        if hide:
            sum0 = _pair_sum(p_mlp, outs[4], c_idx, "grad_pair_sum_mlp0")
        p_mix = _wgrad_packed(rt, o, dmix, "out", PACK_OFF[("out", l)] - mix_group[0], mix_group[1],
                              p_mlp if l == 1 else None, f"out_wgrad{l}")
        outs = _attn_bwd(rt, qkvp, o, lse_lat, do, sink[l], None, f"attn_lat_bwd{l}",
                         comm=_chip_comm([sum1[1], sum0[1]]) if hide else None)
        dq, dkv, dkv_c, dsink1 = outs[:4]
        if hide:
            groups[G_LAYER1] = _owner_sum(sum1[0], outs[4], k_idx, "grad_owner_sum_layer1")
            groups[G_MLP0] = _owner_sum(sum0[0], outs[5], k_idx, "grad_owner_sum_mlp0")
        dq, dkv, dsink2 = _attn_bwd(rt, qkvp, o, lse_ctx, do, sink[l], (dq, dkv, dkv_c), f"attn_ctx_bwd{l}")
        dqkv, dh, dqn, dkn, d_sh_a, d_sc_a, d_g_pre_mix = _in_bwd(rt, dq, dkv, qkv, tables, qn[l], kn[l], w_in[l], h0, dh1, mods[l],
                                                                  g_pre_mix, l == 0, f"in_bwd{l}")
        dw_in = _wgrad_plain(rt, u, dqkv, f"in_wgrad{l}")
        o_in = PACK_OFF[("in", l)] - mix_group[0]
        p_mix = p_mix.at[:, :, o_in:o_in + PACK_HEIGHT["in"]].set(_pack_in_grad(dw_in))
        groups[mix_group] = p_mix
        if not hide and l == 0:
            groups[G_MLP0] = p_mlp
        dmod = jnp.concatenate([d_sh_a, d_sc_a, d_gate_a, d_sh_m, d_sc_m, d_gate_m], axis=1)
        small[l] = dict(mod=dmod, gammas=jnp.concatenate([d_g_pre_mix, d_g_post_mix, d_g_pre_mlp, d_g_post_mlp], axis=0),
                        q_norm=dqn, k_norm=dkn, sink=(dsink1 + dsink2)[:, 0])
    return sq, dh.reshape(nb, seq, D_MODEL), [groups[G_LAYER1], groups[G_MLP0], groups[G_MIX0]], small


SMALL_ROWS = 48


def kernel(x, c, ctx, c_ctx, w_ada, b_ada, g_pre_mix, g_post_mix, g_pre_mlp, g_post_mlp, w_in, q_norm, k_norm, sink, w_out, w_up, w_down, loss_target, m_c_ctx, m_w_ada, m_b_ada, m_g_pre_mix, m_g_post_mix, m_g_pre_mlp, m_g_post_mlp, m_w_in, m_q_norm, m_k_norm, m_sink, m_w_out, m_w_up, m_w_down, v_c_ctx, v_w_ada, v_b_ada, v_g_pre_mix, v_g_post_mix, v_g_pre_mlp, v_g_post_mlp, v_w_in, v_q_norm, v_k_norm, v_sink, v_w_out, v_w_up, v_w_down):
    nb = x.shape[0]
    ix, iy, ic = lax.axis_index("x"), lax.axis_index("y"), lax.axis_index("c")
    chip = 2 * ix + iy
    dev = 2 * chip + ic
    ada_cols = w_ada.shape[2] // 2

    c_all = _all_gather(c.reshape(8, (nb * D_MODEL) // 8), "gather_c", False).reshape(N_DEV * nb, D_MODEL)
    n_cond = N_DEV * nb + 1
    cond_rows = 16 * ((n_cond + 15) // 16)
    cond = jnp.concatenate([c_all, c_ctx[None, :], jnp.zeros((cond_rows - n_cond, D_MODEL), F32)], axis=0)
    w_ada_half = lax.dynamic_slice_in_dim(w_ada, ic * ada_cols, ada_cols, 2)
    b_ada_half = lax.dynamic_slice_in_dim(b_ada, dev * ada_cols, ada_cols, 1)[:, None, :]
    x_ada, mod_part = _ada_fwd(cond, w_ada_half, b_ada_half, "ada_fwd")
    mod_g = _all_gather(mod_part.reshape(DEPTH * cond_rows, ada_cols), "gather_mod", False)
    mod_all = mod_g.reshape(N_DEV, DEPTH, cond_rows, ada_cols).transpose(1, 2, 0, 3).reshape(DEPTH, cond_rows, N_MOD * D_MODEL)
    mods = []
    for l in range(DEPTH):
        mine = lax.dynamic_slice_in_dim(mod_all[l], dev * nb, nb, 0)
        mods.append(jnp.concatenate([mine, mod_all[l, n_cond - 1:n_cond]], axis=0).reshape(nb + 1, N_MOD, D_MODEL))

    packed = _pack_local_half(w_in, w_out, w_up, w_down, ic)
    w_first, = _comm_alone(_gather_comm(packed, [W_FIRST]), "gather_w_first")
    c_idx = ic.reshape(1).astype(jnp.int32)
    kc_idx = jnp.stack([chip, ic]).astype(jnp.int32)

    gam = [(g_pre_mix[l][None], g_post_mix[l][None], g_pre_mlp[l][None], g_post_mlp[l][None]) for l in range(DEPTH)]
    qn = [jnp.tile(q_norm[l], 2)[None] for l in range(DEPTH)]
    kn = [jnp.tile(k_norm[l], 2)[None] for l in range(DEPTH)]
    sq, grad_x, (h_layer1, h_mlp0, p_mix0), lg = _local_step(x, ctx, loss_target, mods, gam, qn, kn, [sink[l] for l in range(DEPTH)],
                                                           w_first, None, packed, c_idx, kc_idx)
    loss = lax.psum(0.5 * jnp.sum(sq) / D_MODEL, ("x", "y", "c"))

    def step(w, g, m, v, name):
        shape = w.shape
        cols = shape[-1]
        outs = _adamw(w.reshape(-1, cols), g.reshape(-1, cols), m.reshape(-1, cols), v.reshape(-1, cols), name)
        return tuple(a.reshape(shape) for a in outs)

    def piece(halves, kind, l, group):
        o = PACK_OFF[(kind, l)] - group[0]
        rows = halves[:, o:o + PACK_HEIGHT[kind]]
        return rows.reshape(1024, 384) if kind == "in" else rows.reshape(2 * PACK_HEIGHT[kind], 1024)

    r1, = _comm_alone(_pair_comm(p_mix0), "grad_pair_exchange_mix0")
    a32, a16 = _pair_sum(p_mix0, r1, c_idx, "grad_pair_sum_mix0")
    r2, = _comm_alone(_chip_comm([a16]), "grad_chip_exchange_mix0")
    h_mix0 = _owner_sum(a32, r2, kc_idx, "grad_owner_sum_mix0")
    h_layer1, h_mlp0, h_mix0 = _comm_alone(_halves_comm([h_layer1, h_mlp0, h_mix0]), "grad_halves_exchange")
    grad_w_up = jnp.stack([piece(h_mlp0, "up", 0, G_MLP0), piece(h_layer1, "up", 1, G_LAYER1)])
    grad_w_down = jnp.stack([piece(h_mlp0, "down", 0, G_MLP0), piece(h_layer1, "down", 1, G_LAYER1)])
    grad_w_in = jnp.stack([piece(h_mix0, "in", 0, G_MIX0), piece(h_layer1, "in", 1, G_LAYER1)])
    grad_w_out = jnp.stack([piece(h_mix0, "out", 0, G_MIX0), piece(h_layer1, "out", 1, G_LAYER1)])

    def lane_pad(v):
        return jnp.pad(v, (0, D_MODEL - v.shape[0]))[None]

    head_rows = [lane_pad(jnp.concatenate([lg[l]["q_norm"][0], lg[l]["k_norm"][0], lg[l]["sink"]])) for l in range(DEPTH)]
    small = jnp.concatenate([lg[l]["mod"].reshape((nb + 1) * N_MOD, D_MODEL) for l in range(DEPTH)]
                            + [lg[l]["gammas"] for l in range(DEPTH)] + head_rows, axis=0)
    small = jnp.pad(small, ((0, SMALL_ROWS - small.shape[0]), (0, 0)))
    small_g = _all_gather(small, "gather_small", False).reshape(N_DEV, SMALL_ROWS, D_MODEL)
    tot = _dev_sum(small_g, "small_sum")
    mod_rows = (nb + 1) * N_MOD
    o_gam, o_head = DEPTH * mod_rows, DEPTH * mod_rows + 4 * DEPTH
    grad_g = [jnp.stack([tot[o_gam + 4 * l + j] for l in range(DEPTH)]) for j in range(4)]
    grad_q_norm = jnp.stack([tot[o_head + l, 0:64] + tot[o_head + l, 64:128] for l in range(DEPTH)])
    grad_k_norm = jnp.stack([tot[o_head + l, 128:192] + tot[o_head + l, 192:256] for l in range(DEPTH)])
    grad_sink = jnp.stack([tot[o_head + l, 256:264] for l in range(DEPTH)])

    dmod_ex, dmod_ctx = [], []
    for l in range(DEPTH):
        ex = small_g[:, l * mod_rows:l * mod_rows + nb * N_MOD].reshape(N_DEV * nb, N_MOD * D_MODEL)
        cx = tot[l * mod_rows + nb * N_MOD:(l + 1) * mod_rows].reshape(1, N_MOD * D_MODEL)
        dmod_ex.append(ex)
        dmod_ctx.append(cx)
    grad_b_ada = jnp.stack([jnp.sum(dmod_ex[l], axis=0) + dmod_ctx[l][0] for l in range(DEPTH)])
    shard_cols = w_ada.shape[2]
    grad_w_ada, dcc_parts = [], []
    for l in range(DEPTH):
        dm = jnp.concatenate([dmod_ex[l], dmod_ctx[l], jnp.zeros((cond_rows - n_cond, N_MOD * D_MODEL), F32)], axis=0)
        dm_shard = lax.dynamic_slice_in_dim(dm, chip * shard_cols, shard_cols, 1).astype(BF16)
        grad_w_ada.append(_matmul(x_ada, dm_shard, "tn", F32, f"ada_wgrad{l}"))
        dcx = lax.dynamic_slice_in_dim(dmod_ctx[l], dev * ada_cols, ada_cols, 1)
        dcx = jnp.pad(dcx, ((0, 15), (0, 0))).astype(BF16)
        dcc_parts.append(_matmul(dcx, w_ada_half[l].astype(BF16), "nt", F32, f"ada_cond_bwd{l}"))
    grad_w_ada = jnp.stack(grad_w_ada)
    dcc = (dcc_parts[0] + dcc_parts[1])[0:8]
    dcc_g = _all_gather(dcc, "gather_cond_grad", False).reshape(N_DEV, 8, D_MODEL)
    grad_c_ctx = _c_ctx_grad(dcc_g, c_ctx[None], "c_ctx_grad")[0]

    small_names = ["c_ctx", "b_ada", "g_pre_mix", "g_post_mix", "g_pre_mlp", "g_post_mlp", "q_norm", "k_norm", "sink"]
    small_w = [c_ctx, b_ada, g_pre_mix, g_post_mix, g_pre_mlp, g_post_mlp, q_norm, k_norm, sink]
    small_gr = [grad_c_ctx, grad_b_ada] + grad_g + [grad_q_norm, grad_k_norm, grad_sink]
    small_m = [m_c_ctx, m_b_ada, m_g_pre_mix, m_g_post_mix, m_g_pre_mlp, m_g_post_mlp, m_q_norm, m_k_norm, m_sink]
    small_v = [v_c_ctx, v_b_ada, v_g_pre_mix, v_g_post_mix, v_g_pre_mlp, v_g_post_mlp, v_q_norm, v_k_norm, v_sink]
    sizes = [int(np.prod(w.shape)) for w in small_w]
    total = sum(sizes)
    flat_rows = 8 * ((total + 8 * D_MODEL - 1) // (8 * D_MODEL))

    def flat(arrs, fill):
        f = jnp.concatenate([a.reshape(-1) for a in arrs])
        return jnp.concatenate([f, jnp.full((flat_rows * D_MODEL - total,), fill, F32)]).reshape(flat_rows, D_MODEL)

    sd, snm, snv = _adamw(flat(small_w, 0.0), flat(small_gr, 0.0), flat(small_m, 0.0), flat(small_v, 1.0), "adamw_small")[:3]

    def unflat(f):
        f = f.reshape(-1)
        out, off = [], 0
        for w, n in zip(small_w, sizes):
            out.append(f[off:off + n].reshape(w.shape))
            off += n
        return out

    small_d, small_nm, small_nv = unflat(sd), unflat(snm), unflat(snv)
    res = {n: (g, d, nm, nv) for n, g, d, nm, nv in zip(small_names, small_gr, small_d, small_nm, small_nv)}
    res["w_ada"] = (grad_w_ada, *step(w_ada, grad_w_ada, m_w_ada, v_w_ada, "adamw_w_ada"))
    res["w_in"] = (grad_w_in, *step(w_in, grad_w_in, m_w_in, v_w_in, "adamw_w_in"))
    res["w_out"] = (grad_w_out, *step(w_out, grad_w_out, m_w_out, v_w_out, "adamw_w_out"))
    res["w_up"] = (grad_w_up, *step(w_up, grad_w_up, m_w_up, v_w_up, "adamw_w_up"))
    res["w_down"] = (grad_w_down, *step(w_down, grad_w_down, m_w_down, v_w_down, "adamw_w_down"))

    order = ["c_ctx", "w_ada", "b_ada", "g_pre_mix", "g_post_mix", "g_pre_mlp", "g_post_mlp", "w_in", "q_norm", "k_norm", "sink", "w_out", "w_up", "w_down"]
    return (loss, grad_x, *[res[n][0] for n in order], *[res[n][1] for n in order],
            *[res[n][2] for n in order], *[res[n][3] for n in order])
```

```python
import functools

import jax
import jax.numpy as jnp
import numpy as np
from jax import lax
from jax.experimental import pallas as pl
from jax.experimental.pallas import tpu as pltpu

F32 = jnp.float32
BF16 = jnp.bfloat16

D_MODEL = 1024
HEAD_DIM = 64
GROUP = 4
WINDOW = 128
N_MOD = 6
D_FF = 4 * D_MODEL
IN_COLS = 1536
GRID_W = 64
ROPE_THETA = 10000.0
EPS = 1e-6
NEG_BIG = -1e30
Q_SCALE = HEAD_DIM ** -0.5
DEPTH = 2
N_DEV = 8

ADAM_LR = 0.001
ADAM_B1 = 0.9
ADAM_B2 = 0.999
ADAM_EPS = 1e-08
ADAM_WD = 0.01
ADAM_STEP = 10

V7X_VMEM_BYTES = 64 * 1024 * 1024
VMEM_LIMIT = V7X_VMEM_BYTES - 8 * 1024 * 1024

MESH = pl.DeviceIdType.MESH
NT = (((1,), (1,)), ((), ()))
TN = (((0,), (0,)), ((), ()))

COL_KA, COL_VA, COL_KB, COL_VB = 4, 5, 10, 11
NORMED_COLS = 640

PACK_HEIGHT = {"up": 512, "down": 512, "out": 128, "in": 192}
PACK_OFF = {("up", 0): 0, ("down", 0): 512, ("out", 0): 1024, ("in", 0): 1152,
            ("up", 1): 1344, ("down", 1): 1856, ("out", 1): 2368, ("in", 1): 2496}
PACK_ROWS = 2688
LAYER_ROWS = 1344
LOCAL_OFF = {"up": 0, "down": 512, "out": 1024, "in": 1152}
W_FIRST, W_LAYER0, W_MLP1, W_MIX1 = (1152, 192), (0, 1152), (1344, 1024), (2368, 320)
G_LAYER1, G_MLP0, G_MIX0 = (1344, 1344), (0, 1024), (1024, 320)


def _pick(n, cands):
    for t in cands:
        if n % t == 0:
            return t
    raise ValueError(f"no tile for {n}")


def _params(sem):
    return pltpu.CompilerParams(dimension_semantics=sem, vmem_limit_bytes=VMEM_LIMIT)


def _all_gather(x, name, in_hbm):
    m_per, n = x.shape
    space = pl.ANY if in_hbm else pltpu.VMEM

    def body(x_ref, out_ref, send_sems, recv_sems, local_sem):
        x_, y_, c_ = lax.axis_index("x"), lax.axis_index("y"), lax.axis_index("c")
        me, sibling = (x_, y_, c_), (x_, y_, 1 - c_)
        chips = [(1 - x_, y_), (x_, 1 - y_), (1 - x_, 1 - y_)]

        def rows(px, py, pc):
            return out_ref.at[pl.ds((4 * px + 2 * py + pc) * m_per, m_per), :]

        def copy(k, block, to, src=None):
            return pltpu.make_async_remote_copy(
                src_ref=rows(*block) if src is None else src, dst_ref=rows(*block),
                send_sem=send_sems.at[k], recv_sem=recv_sems.at[k], device_id=to, device_id_type=MESH)

        mine = pltpu.make_async_copy(x_ref, rows(*me), local_sem)
        mine.start()
        first = [copy(0, me, sibling, src=x_ref)]
        first += [copy(1 + j, me, (*chip, c_), src=x_ref) for j, chip in enumerate(chips)]
        for cp in first:
            cp.start()
        passed = [copy(4 + j, (*chip, c_), sibling) for j, chip in enumerate(chips)]
        for j, chip in enumerate(chips):
            copy(1 + j, (*chip, c_), me).wait_recv()
            passed[j].start()
        copy(0, sibling, me).wait_recv()
        for j, chip in enumerate(chips):
            copy(4 + j, (*chip, 1 - c_), me).wait_recv()
        for cp in first + passed:
            cp.wait_send()
        mine.wait()

    return pl.pallas_call(
        body, name=name,
        out_shape=jax.ShapeDtypeStruct((N_DEV * m_per, n), x.dtype),
        in_specs=[pl.BlockSpec(memory_space=space)],
        out_specs=pl.BlockSpec(memory_space=space),
        scratch_shapes=[pltpu.SemaphoreType.DMA((7,)), pltpu.SemaphoreType.DMA((7,)), pltpu.SemaphoreType.DMA],
    )(x)


class _Comm:
    def __init__(self, inputs, out_shapes, aliases, n_send, n_recv, start, finish):
        self.inputs, self.out_shapes, self.aliases = list(inputs), list(out_shapes), dict(aliases)
        self.n_send, self.n_recv, self.start, self.finish = n_send, n_recv, start, finish


def _comm_call(compute, comm, *, name, grid, in_specs, out_specs, out_shape, args, aliases, semantics):
    in_specs, out_specs, out_shape, args, aliases = list(in_specs), list(out_specs), list(out_shape), list(args), dict(aliases)
    if comm is None:
        return pl.pallas_call(compute, name=name, grid=grid, in_specs=in_specs, out_specs=out_specs, out_shape=out_shape,
                              input_output_aliases=aliases, compiler_params=_params(semantics))(*args)
    n_in, n_out, n_ci, n_co = len(args), len(out_shape), len(comm.inputs), len(comm.out_shapes)
    hbm = pl.BlockSpec(memory_space=pl.ANY)
    aliases.update({n_in + i: n_out + o for i, o in comm.aliases.items()})

    def body(*refs):
        ins, c_ins = refs[:n_in], refs[n_in:n_in + n_ci]
        outs, c_outs = refs[n_in + n_ci:n_in + n_ci + n_out], refs[n_in + n_ci + n_out:n_in + n_ci + n_out + n_co]
        send_sems, recv_sems = refs[-2:]
        ids = [pl.program_id(a) for a in range(len(grid))]
        first = functools.reduce(jnp.logical_and, [i == 0 for i in ids])
        last = functools.reduce(jnp.logical_and, [i == g - 1 for i, g in zip(ids, grid)])

        @pl.when(first)
        def _():
            comm.start(c_ins, c_outs, send_sems, recv_sems)

        compute(*ins, *outs)

        @pl.when(last)
        def _():
            comm.finish(c_ins, c_outs, send_sems, recv_sems)

    return pl.pallas_call(
        body, name=name, grid=grid,
        in_specs=in_specs + [hbm] * n_ci, out_specs=out_specs + [hbm] * n_co, out_shape=out_shape + comm.out_shapes,
        input_output_aliases=aliases,
        scratch_shapes=[pltpu.SemaphoreType.DMA((comm.n_send,)), pltpu.SemaphoreType.DMA((comm.n_recv,))],
        compiler_params=_params(("arbitrary",) * len(grid)),
    )(*args, *comm.inputs)


def _place():
    x_, y_, c_ = lax.axis_index("x"), lax.axis_index("y"), lax.axis_index("c")
    return x_, y_, c_, [(1 - x_, y_), (x_, 1 - y_), (1 - x_, 1 - y_)]


GATHER_SENDS, GATHER_RECVS = 8, 7


def _gather_copies(packed_ref, wg_ref, send_sems, recv_sems, rows, nth=0):
    r0, n = rows
    x_, y_, c_, chips = _place()
    me, sibling = (x_, y_, c_), (x_, y_, 1 - c_)
    src = packed_ref.at[pl.ds(r0, n), :]

    def slot(px, py, pc):
        return wg_ref.at[4 * px + 2 * py + pc]

    def copy(k, block, to, from_packed=False):
        return pltpu.make_async_remote_copy(src_ref=src if from_packed else slot(*block), dst_ref=slot(*block),
                                            send_sem=send_sems.at[GATHER_SENDS * nth + k], recv_sem=recv_sems.at[GATHER_RECVS * nth + k],
                                            device_id=to, device_id_type=MESH)

    own = [copy(0, me, sibling, True)] + [copy(1 + j, me, (*chip, c_), True) for j, chip in enumerate(chips)]
    passed = [copy(4 + j, (*chip, c_), sibling) for j, chip in enumerate(chips)]
    over_ici = [copy(1 + j, (*chip, c_), me) for j, chip in enumerate(chips)]
    from_sibling = [copy(0, sibling, me)] + [copy(4 + j, (*chip, 1 - c_), me) for j, chip in enumerate(chips)]
    mine = pltpu.make_async_copy(src, slot(*me), send_sems.at[GATHER_SENDS * nth + 7])
    return mine, own, passed, over_ici, from_sibling


def _gather_start(packed_ref, wg_ref, send_sems, recv_sems, rows, nth=0):
    mine, own, _, _, _ = _gather_copies(packed_ref, wg_ref, send_sems, recv_sems, rows, nth)
    mine.start()
    for cp in own:
        cp.start()


def _gather_finish(packed_ref, wg_ref, send_sems, recv_sems, rows, nth=0):
    mine, own, passed, over_ici, from_sibling = _gather_copies(packed_ref, wg_ref, send_sems, recv_sems, rows, nth)
    for arrived, onward in zip(over_ici, passed):
        arrived.wait_recv()
        onward.start()
    for arrived in from_sibling:
        arrived.wait_recv()
    for cp in own + passed:
        cp.wait_send()
    mine.wait()


def _gather_comm(packed, ranges):
    shapes = [jax.ShapeDtypeStruct((N_DEV, n, packed.shape[1]), packed.dtype) for _, n in ranges]

    def start(ins, outs, ss, rs):
        for nth, rows in enumerate(ranges):
            _gather_start(ins[0], outs[nth], ss, rs, rows, nth)

    def finish(ins, outs, ss, rs):
        for nth, rows in enumerate(ranges):
            _gather_finish(ins[0], outs[nth], ss, rs, rows, nth)

    return _Comm([packed], shapes, {}, GATHER_SENDS * len(ranges), GATHER_RECVS * len(ranges), start, finish)


def _pair_copy(p_ref, out_ref, send_sems, recv_sems):
    x_, y_, c_, _ = _place()
    return pltpu.make_async_remote_copy(src_ref=p_ref.at[1 - c_], dst_ref=out_ref,
                                        send_sem=send_sems.at[0], recv_sem=recv_sems.at[0],
                                        device_id=(x_, y_, 1 - c_), device_id_type=MESH)


def _pair_comm(p):
    return _Comm([p], [jax.ShapeDtypeStruct(p.shape[1:], p.dtype)], {}, 1, 1,
                 lambda ins, outs, ss, rs: _pair_copy(ins[0], outs[0], ss, rs).start(),
                 lambda ins, outs, ss, rs: _pair_copy(ins[0], outs[0], ss, rs).wait())


def _chip_copies(a_refs, out_refs, send_sems, recv_sems):
    _, _, c_, chips = _place()
    return [pltpu.make_async_remote_copy(src_ref=a_ref.at[2 * tx + ty], dst_ref=o_ref.at[j],
                                         send_sem=send_sems.at[3 * g + j], recv_sem=recv_sems.at[3 * g + j],
                                         device_id=(tx, ty, c_), device_id_type=MESH)
            for g, (a_ref, o_ref) in enumerate(zip(a_refs, out_refs)) for j, (tx, ty) in enumerate(chips)]


def _chip_start(a_refs, out_refs, send_sems, recv_sems):
    for cp in _chip_copies(a_refs, out_refs, send_sems, recv_sems):
        cp.start()


def _chip_finish(a_refs, out_refs, send_sems, recv_sems):
    for cp in _chip_copies(a_refs, out_refs, send_sems, recv_sems):
        cp.wait()


def _chip_comm(arrays):
    shapes = [jax.ShapeDtypeStruct((3,) + a.shape[1:], a.dtype) for a in arrays]
    return _Comm(arrays, shapes, {}, 3 * len(arrays), 3 * len(arrays), _chip_start, _chip_finish)


def _halves_copies(in_refs, out_refs, send_sems, recv_sems):
    x_, y_, c_, _ = _place()
    return [pltpu.make_async_remote_copy(src_ref=o_ref.at[c_], dst_ref=o_ref.at[c_], send_sem=send_sems.at[i], recv_sem=recv_sems.at[i],
                                         device_id=(x_, y_, 1 - c_), device_id_type=MESH)
            for i, o_ref in enumerate(out_refs)]


def _halves_start(in_refs, out_refs, send_sems, recv_sems):
    for cp in _halves_copies(in_refs, out_refs, send_sems, recv_sems):
        cp.start()


def _halves_finish(in_refs, out_refs, send_sems, recv_sems):
    for cp in _halves_copies(in_refs, out_refs, send_sems, recv_sems):
        cp.wait()


def _halves_comm(arrays):
    shapes = [jax.ShapeDtypeStruct(a.shape, a.dtype) for a in arrays]
    return _Comm(arrays, shapes, {i: i for i in range(len(arrays))}, len(arrays), len(arrays), _halves_start, _halves_finish)


def _comm_alone(comm, name):
    n_ci = len(comm.inputs)
    hbm = pl.BlockSpec(memory_space=pl.ANY)

    def body(*refs):
        c_ins, c_outs, send_sems, recv_sems = refs[:n_ci], refs[n_ci:-2], refs[-2], refs[-1]
        comm.start(c_ins, c_outs, send_sems, recv_sems)
        comm.finish(c_ins, c_outs, send_sems, recv_sems)

    return pl.pallas_call(
        body, name=name, out_shape=comm.out_shapes, in_specs=[hbm] * n_ci, out_specs=[hbm] * len(comm.out_shapes),
        input_output_aliases=comm.aliases,
        scratch_shapes=[pltpu.SemaphoreType.DMA((comm.n_send,)), pltpu.SemaphoreType.DMA((comm.n_recv,))],
    )(*comm.inputs)


SUM_TILES = (512, 384, 320, 256, 192, 128, 64)


def _pair_sum(p, r1, c_idx, name):
    _, _, n, c = p.shape
    tr = _pick(n, SUM_TILES)

    def body(s_ref, p_ref, r_ref, o32_ref, o16_ref):
        v = p_ref[...] + r_ref[...]
        o32_ref[...] = v
        o16_ref[...] = v.astype(BF16)

    blk = pl.BlockSpec((None, tr, c), lambda j, i, s: (j, i, 0))
    grid_spec = pltpu.PrefetchScalarGridSpec(
        num_scalar_prefetch=1, grid=(4, n // tr),
        in_specs=[pl.BlockSpec((None, None, tr, c), lambda j, i, s: (s[0], j, i, 0)), blk],
        out_specs=[blk, blk])
    return pl.pallas_call(
        body, name=name, grid_spec=grid_spec,
        out_shape=[jax.ShapeDtypeStruct((4, n, c), F32), jax.ShapeDtypeStruct((4, n, c), BF16)],
        compiler_params=_params(("arbitrary", "arbitrary")),
    )(c_idx, p, r1)


def _owner_sum(a32, r2, kc_idx, name):
    _, r, c = a32.shape
    tr = _pick(r, SUM_TILES)

    def body(s_ref, a_ref, r_ref, o_ref):
        v = a_ref[...]
        for j in range(3):
            v = v + r_ref[j].astype(F32)
        o_ref[...] = v

    grid_spec = pltpu.PrefetchScalarGridSpec(
        num_scalar_prefetch=1, grid=(r // tr,),
        in_specs=[pl.BlockSpec((None, tr, c), lambda i, s: (s[0], i, 0)),
                  pl.BlockSpec((3, tr, c), lambda i, s: (0, i, 0))],
        out_specs=pl.BlockSpec((None, tr, c), lambda i, s: (s[1], i, 0)))
    return pl.pallas_call(
        body, name=name, grid_spec=grid_spec,
        out_shape=jax.ShapeDtypeStruct((2, r, c), F32),
        compiler_params=_params(("arbitrary",)),
    )(kc_idx, a32, r2)


def _pack_local_half(w_in_s, w_out_s, w_up_s, w_down_s, c_idx):
    parts, row = [], 0
    for (kind, l), off in sorted(PACK_OFF.items(), key=lambda kv: kv[1]):
        if off > row:
            parts.append(jnp.zeros((off - row, 1024), BF16))
        if kind == "up":
            p = lax.dynamic_slice_in_dim(w_up_s[l], c_idx * 512, 512, 0)
        elif kind == "down":
            p = lax.dynamic_slice_in_dim(w_down_s[l], c_idx * 512, 512, 0)
        elif kind == "in":
            p = lax.dynamic_slice_in_dim(w_in_s[l], c_idx * 512, 512, 0).reshape(192, 1024)
        else:
            p = lax.dynamic_slice_in_dim(w_out_s[l], c_idx * 128, 128, 0)
        parts.append(p.astype(BF16))
        row = off + PACK_HEIGHT[kind]
    return jnp.concatenate(parts, axis=0)


def _pack_in_grad(dw_in):
    return dw_in.reshape(2, 512, 4, 384).transpose(0, 2, 1, 3).reshape(2, 4, 192, 1024)


def _unpack_in_weight(pieces):
    return pieces.reshape(4, 2, 512, 384).transpose(1, 2, 0, 3).reshape(1024, IN_COLS)


class _Rows:
    def __init__(self, nb, seq, ctx):
        self.nb, self.seq, self.ctx = nb, seq, ctx
        self.n_lat, self.n_ctx = nb * seq, nb * ctx
        self.rows = self.n_lat + self.n_ctx
        self.tm = _pick(np.gcd(seq, self.n_ctx), (512, 256, 128))
        self.tiles_per_ex = seq // self.tm
        self.n_tiles = self.rows // self.tm
        self.n_lat_tiles = self.n_lat // self.tm
        self.groups = nb + 1

    def group(self, i):
        return jnp.minimum(i // self.tiles_per_ex, self.nb)

    def first_of_group(self, i):
        return jnp.logical_and(i % self.tiles_per_ex == 0, i <= self.n_lat_tiles)


def _mod_spec(rt):
    return pl.BlockSpec((1, N_MOD, D_MODEL), lambda i: (rt.group(i), 0, 0))


def _row_spec(rt, cols):
    return pl.BlockSpec((rt.tm, cols), lambda i: (i, 0))


def _vec_spec(cols):
    return pl.BlockSpec((1, cols), lambda i: (0, 0))


def _group_spec(rt):
    return pl.BlockSpec((1, 1, D_MODEL), lambda i: (rt.group(i), 0, 0))


def _gathered_spec(wg, kind):
    h, off = PACK_HEIGHT[kind], wg[kind][1]
    assert off % h == 0, (kind, off)
    return pl.BlockSpec((N_DEV, h, 1024), lambda *_: (0, off // h, 0), pipeline_mode=pl.Buffered(1))


def _group_shape(rt):
    return jax.ShapeDtypeStruct((rt.groups, 1, D_MODEL), F32)


def _vec_shape(cols=D_MODEL):
    return jax.ShapeDtypeStruct((1, cols), F32)


def _rms_inv(v):
    return lax.rsqrt(jnp.mean(v * v, axis=-1, keepdims=True) + EPS)


def _norm_mod_val(h_, g_, mod_ref, i_shift, i_scale):
    n = h_ * _rms_inv(h_) * g_
    return n * (1.0 + mod_ref[0, i_scale:i_scale + 1, :]) + mod_ref[0, i_shift:i_shift + 1, :]


def _post_norm_val(h_, z_, g_, mod_ref, i_gate):
    return h_ + mod_ref[0, i_gate:i_gate + 1, :] * (z_ * _rms_inv(z_) * g_)


def _post_norm_bwd_val(dh_, z_, g_, gate):
    rinv = _rms_inv(z_)
    n0 = z_ * rinv
    dn = dh_ * gate * g_
    dz = rinv * (dn - n0 * jnp.mean(dn * n0, axis=-1, keepdims=True))
    return dz, jnp.sum(dh_ * n0 * g_, axis=0, keepdims=True), jnp.sum(dh_ * gate * n0, axis=0, keepdims=True)


def _norm_mod_bwd_val(du_, h_, g_, one_sc):
    rinv = _rms_inv(h_)
    n0 = h_ * rinv
    dn = du_ * g_ * one_sc
    dh = rinv * (dn - n0 * jnp.mean(dn * n0, axis=-1, keepdims=True))
    return (dh, jnp.sum(du_, axis=0, keepdims=True), jnp.sum(du_ * n0 * g_, axis=0, keepdims=True),
            jnp.sum(du_ * one_sc * n0, axis=0, keepdims=True))


def _accumulate(rt, i, group_pairs, global_pairs):
    @pl.when(rt.first_of_group(i))
    def _():
        for ref, _ in group_pairs:
            ref[...] = jnp.zeros_like(ref)

    @pl.when(i == 0)
    def _():
        for ref, _ in global_pairs:
            ref[...] = jnp.zeros_like(ref)

    for ref, val in group_pairs:
        ref[0] += val
    for ref, val in global_pairs:
        ref[...] += val


def _rope_tables(rt):
    pos = jnp.arange(rt.seq, dtype=jnp.int32)
    row_ids = (pos // GRID_W).astype(F32)
    col_ids = (pos % GRID_W).astype(F32)
    axis_dim = HEAD_DIM // 2
    inv = ROPE_THETA ** (-jnp.arange(0, axis_dim, 2, dtype=F32) / axis_dim)
    ang_r, ang_c = row_ids[:, None] * inv[None, :], col_ids[:, None] * inv[None, :]
    cr, sr, cc, sc = jnp.cos(ang_r), jnp.sin(ang_r), jnp.cos(ang_c), jnp.sin(ang_c)
    zero = jnp.zeros_like(sr)
    cos = jnp.concatenate([cr, cr, cc, cc], axis=1)
    s_lo = jnp.concatenate([zero, sr, zero, sc], axis=1)
    s_hi = jnp.concatenate([-sr, zero, -sc, zero], axis=1)

    def full(t, ctx_value):
        t = jnp.tile(t, (rt.nb, 2))
        return jnp.concatenate([t, jnp.full((rt.n_ctx, 128), ctx_value, F32)], axis=0)

    return full(cos, 1.0), full(s_lo, 0.0), full(s_hi, 0.0)


def _head_stats(t, lo):
    sq = t * t
    s_lo = jnp.sum(jnp.where(lo, sq, 0.0), axis=1, keepdims=True)
    s_hi = jnp.sum(jnp.where(lo, 0.0, sq), axis=1, keepdims=True)
    return lax.rsqrt(jnp.where(lo, s_lo, s_hi) * (1.0 / HEAD_DIM) + EPS)


def _prep_fwd_body(tm, qkv_ref, c, s1, s2, qn, kn, out_ref):
    lo = lax.broadcasted_iota(jnp.int32, (tm, 128), 1) < HEAD_DIM

    def rope(t):
        return t * c + pltpu.roll(t, 16, 1) * s1 + pltpu.roll(t, 112, 1) * s2

    for j in range(12):
        t = qkv_ref[:, j * 128:(j + 1) * 128]
        if j < 4:
            t = rope(t * _head_stats(t, lo) * qn) * Q_SCALE
        elif j == COL_KA:
            t = rope(t * _head_stats(t, lo) * kn)
        elif 6 <= j < 10:
            t = rope(t) * Q_SCALE
        elif j == COL_KB:
            t = rope(t)
        out_ref[:, j * 128:(j + 1) * 128] = t.astype(BF16)


def _prep_bwd_body(tm, dq_ref, dkv_ref, qkv_ref, c, s1, s2, qn, kn, out_ref):
    lo = lax.broadcasted_iota(jnp.int32, (tm, 128), 1) < HEAD_DIM

    def rope_bwd(d):
        return d * c + pltpu.roll(d * s1, 112, 1) + pltpu.roll(d * s2, 16, 1)

    def norm_bwd(t, g, dy):
        rinv = _head_stats(t, lo)
        n = t * rinv
        dn = dy * g
        prod = dn * n
        m_lo = jnp.sum(jnp.where(lo, prod, 0.0), axis=1, keepdims=True)
        m_hi = jnp.sum(jnp.where(lo, 0.0, prod), axis=1, keepdims=True)
        mean = jnp.where(lo, m_lo, m_hi) * (1.0 / HEAD_DIM)
        return rinv * (dn - n * mean), jnp.sum(dy * n, axis=0, keepdims=True)

    dqn = jnp.zeros((1, 128), F32)
    dkn = jnp.zeros((1, 128), F32)
    for j in range(12):
        if j < 4:
            d, dg = norm_bwd(qkv_ref[:, j * 128:(j + 1) * 128], qn, rope_bwd(dq_ref[:, j * 128:(j + 1) * 128] * Q_SCALE))
            dqn = dqn + dg
        elif j == COL_KA:
            d, dg = norm_bwd(qkv_ref[:, j * 128:(j + 1) * 128], kn, rope_bwd(dkv_ref[:, 0:128]))
            dkn = dkn + dg
        elif j == COL_VA:
            d = dkv_ref[:, 128:256]
        elif j < 10:
            d = rope_bwd(dq_ref[:, (j - 2) * 128:(j - 1) * 128] * Q_SCALE)
        elif j == COL_KB:
            d = rope_bwd(dkv_ref[:, 256:384])
        else:
            d = dkv_ref[:, 384:512]
        out_ref[:, j * 128:(j + 1) * 128] = d.astype(BF16)
    return dqn, dkn


def _in_fwd(rt, h, gamma, mod, w_in, tables, qn, kn, name):
    def body(h_ref, g_ref, mod_ref, w_ref, c_ref, s1_ref, s2_ref, qn_ref, kn_ref, u_ref, qkn_ref, qkvp_ref, qkv_ref):
        u = _norm_mod_val(h_ref[...], g_ref[...], mod_ref, 0, 1).astype(BF16)
        u_ref[...] = u
        qkv_ref[...] = jnp.dot(u, w_ref[...], preferred_element_type=F32)
        qkn_ref[...] = qkv_ref[:, 0:NORMED_COLS]
        _prep_fwd_body(rt.tm, qkv_ref, c_ref[...], s1_ref[...], s2_ref[...], qn_ref[...], kn_ref[...], qkvp_ref)

    return pl.pallas_call(
        body, name=name, grid=(rt.n_tiles,),
        in_specs=[_row_spec(rt, D_MODEL), _vec_spec(D_MODEL), _mod_spec(rt),
                  pl.BlockSpec((D_MODEL, IN_COLS), lambda i: (0, 0), pipeline_mode=pl.Buffered(1))]
        + [_row_spec(rt, 128)] * 3 + [_vec_spec(128)] * 2,
        out_specs=[_row_spec(rt, D_MODEL), _row_spec(rt, NORMED_COLS), _row_spec(rt, IN_COLS)],
        out_shape=[jax.ShapeDtypeStruct((rt.rows, D_MODEL), BF16), jax.ShapeDtypeStruct((rt.rows, NORMED_COLS), F32),
                   jax.ShapeDtypeStruct((rt.rows, IN_COLS), BF16)],
        scratch_shapes=[pltpu.VMEM((rt.tm, IN_COLS), F32)],
        compiler_params=_params(("parallel",)),
    )(h, gamma, mod, w_in, *tables, qn, kn)


def _in_bwd(rt, dq, dkv, qkv, tables, qn, kn, w_in, h, dres, mod, gamma, latent_only, name, comm=None):
    last = rt.n_lat_tiles - 1

    def body(dq_ref, dkv_ref, qkv_ref, c_ref, s1_ref, s2_ref, qn_ref, kn_ref, w_ref, h_ref, dres_ref, mod_ref, g_ref,
             dqkv_ref, dh_ref, dqn_ref, dkn_ref, dsh_ref, dsc_ref, dg_ref):
        i = pl.program_id(0)
        dqn, dkn = _prep_bwd_body(rt.tm, dq_ref, dkv_ref, qkv_ref, c_ref[...], s1_ref[...], s2_ref[...], qn_ref[...], kn_ref[...], dqkv_ref)
        du = lax.dot_general(dqkv_ref[...], w_ref[...], NT, preferred_element_type=F32)
        dh, dsh, dsc, dg = _norm_mod_bwd_val(du, h_ref[...], g_ref[...], 1.0 + mod_ref[0, 1:2, :])
        if latent_only:
            @pl.when(i <= last)
            def _():
                dh_ref[...] = dres_ref[...] + dh
        else:
            dh_ref[...] = dres_ref[...] + dh
        _accumulate(rt, i, [(dsh_ref, dsh), (dsc_ref, dsc)], [(dg_ref, dg), (dqn_ref, dqn), (dkn_ref, dkn)])

    dh_spec = pl.BlockSpec((rt.tm, D_MODEL), lambda i: (jnp.minimum(i, last), 0)) if latent_only else _row_spec(rt, D_MODEL)
    return _comm_call(
        body, comm, name=name, grid=(rt.n_tiles,),
        in_specs=[_row_spec(rt, 1024), _row_spec(rt, 512), _row_spec(rt, NORMED_COLS)] + [_row_spec(rt, 128)] * 3 + [_vec_spec(128)] * 2
        + [pl.BlockSpec((D_MODEL, IN_COLS), lambda i: (0, 0), pipeline_mode=pl.Buffered(1)),
           _row_spec(rt, D_MODEL), _row_spec(rt, D_MODEL), _mod_spec(rt), _vec_spec(D_MODEL)],
        out_specs=[_row_spec(rt, IN_COLS), dh_spec, _vec_spec(128), _vec_spec(128),
                   _group_spec(rt), _group_spec(rt), _vec_spec(D_MODEL)],
        out_shape=[jax.ShapeDtypeStruct((rt.rows, IN_COLS), BF16),
                   jax.ShapeDtypeStruct((rt.n_lat if latent_only else rt.rows, D_MODEL), F32),
                   _vec_shape(128), _vec_shape(128), _group_shape(rt), _group_shape(rt), _vec_shape()],
        args=[dq, dkv, qkv, *tables, qn, kn, w_in, h, dres, mod, gamma], aliases={}, semantics=("arbitrary",))


def _out_fwd(rt, o, wg, h, mod, g_post_mix, g_pre_mlp, name):
    def body(o_ref, w_ref, h_ref, mod_ref, gpost_ref, gpre_ref, mix_ref, h1_ref, u2_ref):
        mix = jnp.dot(o_ref[...], w_ref[...].reshape(D_MODEL, D_MODEL), preferred_element_type=F32)
        mix_ref[...] = mix
        h1 = _post_norm_val(h_ref[...], mix, gpost_ref[...], mod_ref, 2)
        h1_ref[...] = h1
        u2_ref[...] = _norm_mod_val(h1, gpre_ref[...], mod_ref, 3, 4).astype(BF16)

    return pl.pallas_call(
        body, name=name, grid=(rt.n_tiles,),
        in_specs=[_row_spec(rt, D_MODEL), _gathered_spec(wg, "out"), _row_spec(rt, D_MODEL), _mod_spec(rt),
                  _vec_spec(D_MODEL), _vec_spec(D_MODEL)],
        out_specs=[_row_spec(rt, D_MODEL)] * 3,
        out_shape=[jax.ShapeDtypeStruct((rt.rows, D_MODEL), F32), jax.ShapeDtypeStruct((rt.rows, D_MODEL), F32),
                   jax.ShapeDtypeStruct((rt.rows, D_MODEL), BF16)],
        compiler_params=_params(("parallel",)),
    )(o, wg["out"][0], h, mod, g_post_mix, g_pre_mlp)


def _out_bwd(rt, dh1, mix, wg, mod, g_post_mix, name, comm=None):
    def body(dh_ref, mix_ref, w_ref, mod_ref, g_ref, dmix_ref, do_ref, dgate_ref, dg_ref):
        i = pl.program_id(0)
        dz, dgate, dg = _post_norm_bwd_val(dh_ref[...], mix_ref[...], g_ref[...], mod_ref[0, 2:3, :])
        dzb = dz.astype(BF16)
        dmix_ref[...] = dzb
        do_ref[...] = lax.dot_general(dzb, w_ref[...].reshape(D_MODEL, D_MODEL), NT, preferred_element_type=F32).astype(BF16)
        _accumulate(rt, i, [(dgate_ref, dgate)], [(dg_ref, dg)])

    return _comm_call(
        body, comm, name=name, grid=(rt.n_tiles,),
        in_specs=[_row_spec(rt, D_MODEL), _row_spec(rt, D_MODEL), _gathered_spec(wg, "out"), _mod_spec(rt), _vec_spec(D_MODEL)],
        out_specs=[_row_spec(rt, D_MODEL), _row_spec(rt, D_MODEL), _group_spec(rt), _vec_spec(D_MODEL)],
        out_shape=[jax.ShapeDtypeStruct((rt.rows, D_MODEL), BF16), jax.ShapeDtypeStruct((rt.rows, D_MODEL), BF16),
                   _group_shape(rt), _vec_shape()],
        args=[dh1, mix, wg["out"][0], mod, g_post_mix], aliases={}, semantics=("arbitrary",))


def _w_chunk(w_ref, k):
    return w_ref[2 * k:2 * k + 2].reshape(1024, 1024)


def _mlp_fwd(rt, u2, h1, wg, mod, g_post_mlp, name, comm=None):
    def body(u2_ref, h1_ref, wu_ref, wd_ref, mod_ref, g_ref, ra_ref, y_ref, h2_ref):
        u2_ = u2_ref[...]
        y = jnp.zeros((rt.tm, D_MODEL), F32)
        for k in range(D_FF // 1024):
            a = jnp.maximum(jnp.dot(u2_, _w_chunk(wu_ref, k), preferred_element_type=F32), 0.0)
            ra_ref[:, k * 1024:(k + 1) * 1024] = a.astype(BF16)
            y = y + jnp.dot((a * a).astype(BF16), _w_chunk(wd_ref, k), preferred_element_type=F32)
        y_ref[...] = y
        h2_ref[...] = _post_norm_val(h1_ref[...], y, g_ref[...], mod_ref, 5)

    return _comm_call(
        body, comm, name=name, grid=(rt.n_tiles,),
        in_specs=[_row_spec(rt, D_MODEL), _row_spec(rt, D_MODEL), _gathered_spec(wg, "up"), _gathered_spec(wg, "down"),
                  _mod_spec(rt), _vec_spec(D_MODEL)],
        out_specs=[_row_spec(rt, D_FF), _row_spec(rt, D_MODEL), _row_spec(rt, D_MODEL)],
        out_shape=[jax.ShapeDtypeStruct((rt.rows, D_FF), BF16), jax.ShapeDtypeStruct((rt.rows, D_MODEL), F32),
                   jax.ShapeDtypeStruct((rt.rows, D_MODEL), F32)],
        args=[u2, h1, wg["up"][0], wg["down"][0], mod, g_post_mlp], aliases={}, semantics=("parallel",))


def _mlp_down_bwd(rt, dh, y, ra, wg, mod, g_post_mlp, name, comm=None):
    def body(dh_ref, y_ref, ra_ref, wd_ref, mod_ref, g_ref, dy_ref, da_ref, dgate_ref, dg_ref):
        i = pl.program_id(0)
        dz, dgate, dg = _post_norm_bwd_val(dh_ref[...], y_ref[...], g_ref[...], mod_ref[0, 5:6, :])
        dyb = dz.astype(BF16)
        dy_ref[...] = dyb
        for k in range(D_FF // 1024):
            dr = lax.dot_general(dyb, _w_chunk(wd_ref, k), NT, preferred_element_type=F32)
            da_ref[:, k * 1024:(k + 1) * 1024] = (dr * (2.0 * ra_ref[:, k * 1024:(k + 1) * 1024].astype(F32))).astype(BF16)
        _accumulate(rt, i, [(dgate_ref, dgate)], [(dg_ref, dg)])

    return _comm_call(
        body, comm, name=name, grid=(rt.n_tiles,),
        in_specs=[_row_spec(rt, D_MODEL), _row_spec(rt, D_MODEL), _row_spec(rt, D_FF), _gathered_spec(wg, "down"),
                  _mod_spec(rt), _vec_spec(D_MODEL)],
        out_specs=[_row_spec(rt, D_MODEL), _row_spec(rt, D_FF), _group_spec(rt), _vec_spec(D_MODEL)],
        out_shape=[jax.ShapeDtypeStruct((rt.rows, D_MODEL), BF16), jax.ShapeDtypeStruct((rt.rows, D_FF), BF16),
                   _group_shape(rt), _vec_shape()],
        args=[dh, y, ra, wg["down"][0], mod, g_post_mlp], aliases={}, semantics=("arbitrary",))


def _mlp_up_bwd(rt, da, wg, h1, dh, mod, g_pre_mlp, name):
    def body(da_ref, wu_ref, h1_ref, dh_ref, mod_ref, g_ref, dh1_ref, dsh_ref, dsc_ref, dg_ref):
        i = pl.program_id(0)
        du = jnp.zeros((rt.tm, D_MODEL), F32)
        for k in range(D_FF // 1024):
            du = du + lax.dot_general(da_ref[:, k * 1024:(k + 1) * 1024], _w_chunk(wu_ref, k), NT, preferred_element_type=F32)
        d, dsh, dsc, dg = _norm_mod_bwd_val(du, h1_ref[...], g_ref[...], 1.0 + mod_ref[0, 4:5, :])
        dh1_ref[...] = dh_ref[...] + d
        _accumulate(rt, i, [(dsh_ref, dsh), (dsc_ref, dsc)], [(dg_ref, dg)])

    return pl.pallas_call(
        body, name=name, grid=(rt.n_tiles,),
        in_specs=[_row_spec(rt, D_FF), _gathered_spec(wg, "up"), _row_spec(rt, D_MODEL), _row_spec(rt, D_MODEL),
                  _mod_spec(rt), _vec_spec(D_MODEL)],
        out_specs=[_row_spec(rt, D_MODEL), _group_spec(rt), _group_spec(rt), _vec_spec(D_MODEL)],
        out_shape=[jax.ShapeDtypeStruct((rt.rows, D_MODEL), F32), _group_shape(rt), _group_shape(rt), _vec_shape()],
        compiler_params=_params(("arbitrary",)),
    )(da, wg["up"][0], h1, dh, mod, g_pre_mlp)


def _wgrad_packed(rt, a, b, kind, off, n_rows, p_prev, name):
    h = PACK_HEIGHT[kind]
    tk = rt.tm

    def body(a_ref, b_ref, *rest):
        o_ref = rest[-1]
        i = pl.program_id(0)

        @pl.when(i == 0)
        def _():
            o_ref[...] = jnp.zeros_like(o_ref)

        if kind == "out":
            res = lax.dot_general(a_ref[...], b_ref[...], TN, preferred_element_type=F32)
            for k in range(4):
                for c in range(2):
                    o_ref[c, k] += res[(2 * k + c) * h:(2 * k + c + 1) * h]
        else:
            for k in range(4):
                if kind == "up":
                    res = lax.dot_general(a_ref[...], b_ref[:, k * 1024:(k + 1) * 1024], TN, preferred_element_type=F32)
                else:
                    ra = a_ref[:, k * 1024:(k + 1) * 1024].astype(F32)
                    res = lax.dot_general((ra * ra).astype(BF16), b_ref[...], TN, preferred_element_type=F32)
                o_ref[0, k] += res[0:h]
                o_ref[1, k] += res[h:2 * h]

    in_specs = [pl.BlockSpec((tk, a.shape[1]), lambda i: (i, 0)), pl.BlockSpec((tk, b.shape[1]), lambda i: (i, 0))]
    args = [a, b]
    aliases = {}
    if p_prev is not None:
        in_specs.append(pl.BlockSpec(memory_space=pl.ANY))
        args.append(p_prev)
        aliases = {2: 0}
    return pl.pallas_call(
        body, name=name, grid=(rt.rows // tk,),
        in_specs=in_specs,
        out_specs=pl.BlockSpec((2, 4, h, 1024), lambda i: (0, 0, off // h, 0)),
        out_shape=jax.ShapeDtypeStruct((2, 4, n_rows, 1024), F32),
        input_output_aliases=aliases,
        compiler_params=_params(("arbitrary",)),
    )(*args)


def _wgrad_plain(rt, a, b, name):
    tk = rt.tm

    def body(a_ref, b_ref, o_ref):
        @pl.when(pl.program_id(0) == 0)
        def _():
            o_ref[...] = jnp.zeros_like(o_ref)

        o_ref[...] += lax.dot_general(a_ref[...], b_ref[...], TN, preferred_element_type=F32)

    return pl.pallas_call(
        body, name=name, grid=(rt.rows // tk,),
        in_specs=[pl.BlockSpec((tk, a.shape[1]), lambda i: (i, 0)), pl.BlockSpec((tk, b.shape[1]), lambda i: (i, 0))],
        out_specs=pl.BlockSpec((a.shape[1], b.shape[1]), lambda i: (0, 0)),
        out_shape=jax.ShapeDtypeStruct((a.shape[1], b.shape[1]), F32),
        compiler_params=_params(("arbitrary",)),
    )(a, b)


def _ada_wgrad(xs, dm, name):
    depth, _, cols = dm.shape

    def body(x_ref, d_ref, o_ref):
        for l in range(depth):
            o_ref[l] = lax.dot_general(x_ref[...], d_ref[l], TN, preferred_element_type=F32)

    return pl.pallas_call(body, name=name, out_shape=jax.ShapeDtypeStruct((depth, xs.shape[1], cols), F32),
                          compiler_params=pltpu.CompilerParams(vmem_limit_bytes=VMEM_LIMIT))(xs, dm)


def _loss_grad(rt, h, target, name):
    last = rt.n_lat_tiles - 1

    def body(h_ref, t_ref, dh_ref, sq_ref):
        i = pl.program_id(0)

        @pl.when(i == 0)
        def _():
            sq_ref[...] = jnp.zeros_like(sq_ref)

        @pl.when(i <= last)
        def _():
            e = h_ref[...] - t_ref[...]
            dh_ref[...] = e * (1.0 / D_MODEL)
            sq_ref[...] += jnp.sum(e * e, axis=0, keepdims=True)

        @pl.when(i > last)
        def _():
            dh_ref[...] = jnp.zeros_like(dh_ref)

    return pl.pallas_call(
        body, name=name, grid=(rt.n_tiles,),
        in_specs=[_row_spec(rt, D_MODEL), pl.BlockSpec((rt.tm, D_MODEL), lambda i: (jnp.minimum(i, last), 0))],
        out_specs=[_row_spec(rt, D_MODEL), _vec_spec(D_MODEL)],
        out_shape=[jax.ShapeDtypeStruct((rt.rows, D_MODEL), F32), jax.ShapeDtypeStruct((1, D_MODEL), F32)],
        compiler_params=_params(("arbitrary",)),
    )(h, target)


def _stack_heads(x, kvi):
    x = x.astype(F32)
    tq = x.shape[0]
    lane = lax.broadcasted_iota(jnp.int32, (tq, 128), 1)
    keep = lane < HEAD_DIM if kvi == 0 else lane >= HEAD_DIM
    parts = []
    for p in range(2):
        pair = x[:, p * 128:(p + 1) * 128]
        swapped = pltpu.roll(pair, HEAD_DIM, 1)
        lo_head, hi_head = (pair, swapped) if kvi == 0 else (swapped, pair)
        parts += [jnp.where(keep, lo_head, 0.0), jnp.where(keep, hi_head, 0.0)]
    return jnp.concatenate(parts, axis=0).astype(BF16)


def _unstack_heads(o4, kvi):
    tq = o4.shape[0] // GROUP
    lane = lax.broadcasted_iota(jnp.int32, (tq, 128), 1)
    outs = []
    for p in range(2):
        r_lo, r_hi = o4[(2 * p) * tq:(2 * p + 1) * tq], o4[(2 * p + 1) * tq:(2 * p + 2) * tq]
        if kvi == 0:
            lo, hi = r_lo, pltpu.roll(r_hi, HEAD_DIM, 1)
        else:
            lo, hi = pltpu.roll(r_lo, HEAD_DIM, 1), r_hi
        outs.append(jnp.where(lane < HEAD_DIM, lo, hi))
    return jnp.concatenate(outs, axis=1)


def _per_head(shape, axis, tq, values):
    head = lax.broadcasted_iota(jnp.int32, shape, axis) // tq
    out = jnp.zeros(shape, F32)
    for g in range(GROUP):
        out = jnp.where(head == g, values[g], out)
    return out


KEY_CHUNK = 512


def _key_chunks(k_ref, v_ref, n, kc=KEY_CHUNK):
    kc = min(kc, n)
    return [(k_ref[c * kc:(c + 1) * kc, :], v_ref[c * kc:(c + 1) * kc, :], None) for c in range(n // kc)]


def _softmax_fwd(qs, chunks, sink_col):
    logits = []
    for k, _, mask in chunks:
        s = lax.dot_general(qs, k, NT, preferred_element_type=F32)
        logits.append(s if mask is None else jnp.where(mask, s, NEG_BIG))
    m = functools.reduce(jnp.maximum, [jnp.max(s, axis=1, keepdims=True) for s in logits])
    if sink_col is not None:
        m = jnp.maximum(m, sink_col)
    l = jnp.zeros_like(m) if sink_col is None else jnp.exp(sink_col - m)
    acc = jnp.zeros((qs.shape[0], 128), F32)
    for s, (_, v, _) in zip(logits, chunks):
        p = jnp.exp(s - m)
        l = l + jnp.sum(p, axis=1, keepdims=True)
        acc = acc + jnp.dot(p.astype(BF16), v, preferred_element_type=F32)
    return acc / l, m + jnp.log(l)


def _to_rows(col):
    return jnp.transpose(jnp.broadcast_to(col, (col.shape[0], 128)))[0:8, :]


def _softmax_bwd(qs, dos, lse_row, delta_row, chunks):
    dq = jnp.zeros((qs.shape[0], 128), F32)
    grads = []
    for k, v, mask in chunks:
        s = lax.dot_general(k, qs, NT, preferred_element_type=F32)
        if mask is not None:
            s = jnp.where(mask, s, NEG_BIG)
        p = jnp.exp(s - lse_row)
        dp = lax.dot_general(v, dos, NT, preferred_element_type=F32)
        ds = (p * (dp - delta_row)).astype(BF16)
        dv = jnp.dot(p.astype(BF16), dos, preferred_element_type=F32)
        dk = jnp.dot(ds, qs, preferred_element_type=F32)
        dq = dq + lax.dot_general(ds, k, TN, preferred_element_type=F32)
        grads.append((dk, dv))
    return dq, grads


def _band(qi, tq, seq):
    span = tq + 2 * WINDOW
    start = pl.multiple_of(jnp.clip(qi * tq - WINDOW, 0, seq - span), 64)
    return start, span


def _band_mask(qi, tq, start, span, query_axis):
    shape = (GROUP * tq, span) if query_axis == 0 else (span, GROUP * tq)
    qpos = qi * tq + lax.broadcasted_iota(jnp.int32, shape, query_axis) % tq
    kpos = start + lax.broadcasted_iota(jnp.int32, shape, 1 - query_axis)
    return jnp.abs(kpos - qpos) <= WINDOW


def _qkv_specs(rt, tq, q_row, ctx_row, with_latent):
    specs = [pl.BlockSpec((tq, 256), functools.partial(lambda b, i, col: (q_row(b, i), col), col=col)) for col in (0, 1, 3, 4)]
    if with_latent:
        specs += [pl.BlockSpec((rt.seq, 128), functools.partial(lambda b, i, col: (b, col), col=col))
                  for col in (COL_KA, COL_VA, COL_KB, COL_VB)]
    specs += [pl.BlockSpec((rt.ctx, 128), functools.partial(lambda b, i, col: (ctx_row(b), col), col=col))
              for col in (COL_KA, COL_VA, COL_KB, COL_VB)]
    return specs


def _attn_fwd(rt, qkvp, sink, o_prev, name, comm=None):
    latent = o_prev is None
    seq, ctx, nb = rt.seq, rt.ctx, rt.nb
    tq = 128 if latent else ctx
    nq = seq // tq if latent else 1
    ctx_blk0 = rt.n_lat // ctx
    q_row = (lambda b, i: b * nq + i) if latent else (lambda b, i: ctx_blk0 + b)

    def body(sink_ref, qa0, qa1, qb0, qb1, *rest):
        if latent:
            kal, val, kbl, vbl, kac, vac, kbc, vbc, o_ref, lse_ref = rest
        else:
            kac, vac, kbc, vbc, _, o_ref, lse_ref = rest
        qi = pl.program_id(1)
        for kvi, (qa, qb) in enumerate(((qa0, qb0), (qa1, qb1))):
            src_a = _key_chunks(kac, vac, ctx)
            src_b = _key_chunks(kbc, vbc, ctx)
            if latent:
                src_a += _key_chunks(kal, val, seq, seq)
                start, span = _band(qi, tq, seq)
                src_b.append((kbl[pl.ds(start, span), :], vbl[pl.ds(start, span), :], _band_mask(qi, tq, start, span, 0)))
            oa, lse = _softmax_fwd(_stack_heads(qa[...], kvi), src_a, None)
            o_ref[:, kvi * 256:(kvi + 1) * 256] = _unstack_heads(oa, kvi).astype(BF16)
            lse_ref[0, kvi] = _to_rows(lse)
            sink_col = _per_head((GROUP * tq, 1), 0, tq, [sink_ref[kvi * GROUP + g] for g in range(GROUP)])
            ob, lse = _softmax_fwd(_stack_heads(qb[...], kvi), src_b, sink_col)
            o_ref[:, 512 + kvi * 256:512 + (kvi + 1) * 256] = _unstack_heads(ob, kvi).astype(BF16)
            lse_ref[0, 2 + kvi] = _to_rows(lse)

    specs = _qkv_specs(rt, tq, q_row, lambda b: ctx_blk0 + b, latent)
    args = [sink] + [qkvp] * len(specs)
    in_specs = [pl.BlockSpec(memory_space=pltpu.SMEM)] + specs
    aliases = {}
    if not latent:
        in_specs.append(pl.BlockSpec(memory_space=pl.ANY))
        args.append(o_prev)
        aliases = {len(args) - 1: 0}
    return _comm_call(
        body, comm, name=name, grid=(nb, nq),
        in_specs=in_specs,
        out_specs=[pl.BlockSpec((tq, 1024), lambda b, i: (q_row(b, i), 0)),
                   pl.BlockSpec((1, 4, 8, GROUP * tq), lambda b, i: (b * nq + i, 0, 0, 0))],
        out_shape=[jax.ShapeDtypeStruct((rt.rows, 1024), BF16), jax.ShapeDtypeStruct((nb * nq, 4, 8, GROUP * tq), F32)],
        args=args, aliases=aliases, semantics=("parallel", "parallel"))


def _attn_bwd(rt, qkvp, o, lse, do, sink, prev, name, comm=None):
    latent = prev is None
    seq, ctx, nb = rt.seq, rt.ctx, rt.nb
    tq = 128 if latent else ctx
    nq = seq // tq if latent else 1
    ctx_blk0 = rt.n_lat // ctx
    q_row = (lambda b, i: b * nq + i) if latent else (lambda b, i: ctx_blk0 + b)
    kc = min(KEY_CHUNK, seq)

    def body(sink_ref, qa0, qa1, qb0, qb1, *rest):
        if latent:
            kal, val, kbl, vbl, kac, vac, kbc, vbc, do_ref, o_ref, lse_ref, dq_ref, dl_ref, dc_ref, dsink_ref = rest
        else:
            kac, vac, kbc, vbc, do_ref, o_ref, lse_ref, c1_ref, _, _, dq_ref, dc_ref, dsink_ref = rest
        b, qi = pl.program_id(0), pl.program_id(1)

        def rows_of(cols, kvi, mixer):
            dos = _stack_heads(do_ref[:, cols], kvi)
            delta = jnp.sum(dos.astype(F32) * _stack_heads(o_ref[:, cols], kvi).astype(F32), axis=1, keepdims=True)
            return dos, lse_ref[0, 2 * mixer + kvi, 0:1, :], _to_rows(delta)[0:1, :]

        @pl.when(jnp.logical_and(b == 0, qi == 0))
        def _():
            dsink_ref[...] = jnp.zeros_like(dsink_ref)

        if latent:
            @pl.when(qi == 0)
            def _():
                dc_ref[...] = jnp.zeros_like(dc_ref)
                dl_ref[...] = jnp.zeros_like(dl_ref)
        else:
            dc_ref[...] = c1_ref[...]

        head_row = lax.broadcasted_iota(jnp.int32, (8, 128), 0)
        for kvi, (qa, qb) in enumerate(((qa0, qb0), (qa1, qb1))):
            cols = slice(kvi * 256, (kvi + 1) * 256)
            dos, lse_row, delta_row = rows_of(cols, kvi, 0)
            src = _key_chunks(kac, vac, ctx)
            if latent:
                src += _key_chunks(kal, val, seq)
            dq4, grads = _softmax_bwd(_stack_heads(qa[...], kvi), dos, lse_row, delta_row, src)
            dq_ref[:, cols] = _unstack_heads(dq4, kvi)
            dc_ref[:, 0:128] += grads[0][0]
            dc_ref[:, 128:256] += grads[0][1]
            for c, (dk, dv) in enumerate(grads[1:]):
                dl_ref[c * kc:(c + 1) * kc, 0:128] += dk
                dl_ref[c * kc:(c + 1) * kc, 128:256] += dv
            cols = slice(512 + kvi * 256, 512 + (kvi + 1) * 256)
            dos, lse_row, delta_row = rows_of(cols, kvi, 1)
            src = _key_chunks(kbc, vbc, ctx)
            if latent:
                start, span = _band(qi, tq, seq)
                src.append((kbl[pl.ds(start, span), :], vbl[pl.ds(start, span), :], _band_mask(qi, tq, start, span, 1)))
            dq4, grads = _softmax_bwd(_stack_heads(qb[...], kvi), dos, lse_row, delta_row, src)
            dq_ref[:, cols] = _unstack_heads(dq4, kvi)
            dc_ref[:, 256:384] += grads[0][0]
            dc_ref[:, 384:512] += grads[0][1]
            if latent:
                dl_ref[pl.ds(start, span), 256:384] += grads[1][0]
                dl_ref[pl.ds(start, span), 384:512] += grads[1][1]
            sink_row = _per_head((1, GROUP * tq), 1, tq, [sink_ref[kvi * GROUP + g] for g in range(GROUP)])
            dsink = -jnp.exp(sink_row - lse_row) * delta_row
            head = lax.broadcasted_iota(jnp.int32, (1, GROUP * tq), 1) // tq
            upd = jnp.zeros((8, 128), F32)
            for g in range(GROUP):
                upd = jnp.where(head_row == kvi * GROUP + g, jnp.sum(jnp.where(head == g, dsink, 0.0)), upd)
            dsink_ref[...] += upd

    specs = _qkv_specs(rt, tq, q_row, lambda b: ctx_blk0 + b, latent)
    q_rows_spec = pl.BlockSpec((tq, 1024), lambda b, i: (q_row(b, i), 0))
    in_specs = ([pl.BlockSpec(memory_space=pltpu.SMEM)] + specs
                + [q_rows_spec, q_rows_spec, pl.BlockSpec((1, 4, 8, GROUP * tq), lambda b, i: (b * nq + i, 0, 0, 0))])
    args = [sink] + [qkvp] * len(specs) + [do, o, lse]
    dq_shape = jax.ShapeDtypeStruct((rt.rows, 1024), F32)
    dkv_shape = jax.ShapeDtypeStruct((rt.rows, 512), F32)
    dsink_spec, dsink_shape = pl.BlockSpec((8, 128), lambda b, i: (0, 0)), jax.ShapeDtypeStruct((8, 128), F32)
    dq_spec = pl.BlockSpec((tq, 1024), lambda b, i: (q_row(b, i), 0))
    if latent:
        out_specs = [dq_spec, pl.BlockSpec((seq, 512), lambda b, i: (b, 0)), pl.BlockSpec((ctx, 512), lambda b, i: (b, 0)), dsink_spec]
        out_shape = [dq_shape, dkv_shape, jax.ShapeDtypeStruct((rt.n_ctx, 512), F32), dsink_shape]
        aliases = {}
    else:
        dq_prev, dkv_prev, c1 = prev
        in_specs += [pl.BlockSpec((ctx, 512), lambda b, i: (b, 0)), pl.BlockSpec(memory_space=pl.ANY), pl.BlockSpec(memory_space=pl.ANY)]
        args += [c1, dq_prev, dkv_prev]
        out_specs = [dq_spec, pl.BlockSpec((ctx, 512), lambda b, i: (ctx_blk0 + b, 0)), dsink_spec]
        out_shape = [dq_shape, dkv_shape, dsink_shape]
        aliases = {len(args) - 2: 0, len(args) - 1: 1}
    return _comm_call(body, comm, name=name, grid=(nb, nq), in_specs=in_specs, out_specs=out_specs, out_shape=out_shape,
                      args=args, aliases=aliases, semantics=("arbitrary", "arbitrary"))


def _silu(x):
    return x / (1.0 + jnp.exp(-x))


def _whole(shape):
    return pl.BlockSpec(shape, lambda i, s: (0,) * len(shape))


def _ada_half_spec(cols):
    return pl.BlockSpec((DEPTH, D_MODEL, cols), lambda i, s: (0, 0, s[0]))


def _ada_fwd(cond, w_ada, b_half, c_idx, name):
    rows = cond.shape[0]
    cols = w_ada.shape[2] // 2

    def body(s_ref, c_ref, w_ref, b_ref, x_ref, o_ref):
        xs = _silu(c_ref[...]).astype(BF16)
        x_ref[...] = xs
        for l in range(DEPTH):
            o_ref[l] = jnp.dot(xs, w_ref[l].astype(BF16), preferred_element_type=F32) + b_ref[l]

    grid_spec = pltpu.PrefetchScalarGridSpec(
        num_scalar_prefetch=1, grid=(1,),
        in_specs=[_whole(cond.shape), _ada_half_spec(cols), _whole(b_half.shape)],
        out_specs=[_whole((rows, D_MODEL)), _whole((DEPTH, rows, cols))])
    return pl.pallas_call(
        body, name=name, grid_spec=grid_spec,
        out_shape=[jax.ShapeDtypeStruct((rows, D_MODEL), BF16), jax.ShapeDtypeStruct((DEPTH, rows, cols), F32)],
        compiler_params=_params(("arbitrary",)),
    )(c_idx, cond, w_ada, b_half)


def _ada_cond_bwd(dcx, w_ada, c_idx, name):
    _, rows, cols = dcx.shape

    def body(s_ref, d_ref, w_ref, o_ref):
        acc = jnp.zeros((rows, D_MODEL), F32)
        for l in range(DEPTH):
            acc = acc + lax.dot_general(d_ref[l], w_ref[l].astype(BF16), NT, preferred_element_type=F32)
        o_ref[...] = acc

    grid_spec = pltpu.PrefetchScalarGridSpec(
        num_scalar_prefetch=1, grid=(1,),
        in_specs=[_whole(dcx.shape), _ada_half_spec(cols)], out_specs=_whole((rows, D_MODEL)))
    return pl.pallas_call(body, name=name, grid_spec=grid_spec, out_shape=jax.ShapeDtypeStruct((rows, D_MODEL), F32),
                          compiler_params=_params(("arbitrary",)))(c_idx, dcx, w_ada)


def _dev_sum(x, name):
    _, r, c = x.shape

    def body(x_ref, o_ref):
        v = x_ref[0]
        for d in range(1, N_DEV):
            v = v + x_ref[d]
        o_ref[...] = v

    return pl.pallas_call(body, name=name, out_shape=jax.ShapeDtypeStruct((r, c), F32))(x)


def _adam_val(w, g, m, v):
    c1 = 1.0 / (1.0 - ADAM_B1 ** ADAM_STEP)
    c2 = 1.0 / (1.0 - ADAM_B2 ** ADAM_STEP)
    nm = ADAM_B1 * m + (1.0 - ADAM_B1) * g
    nv = ADAM_B2 * v + (1.0 - ADAM_B2) * (g * g)
    return -ADAM_LR * ((nm * c1) / (jnp.sqrt(nv * c2) + ADAM_EPS) + ADAM_WD * w), nm, nv


def _small_update(tot, dcc_parts, params, n_groups, name):
    n_p = len(params)
    mod_rows = n_groups * N_MOD

    def body(tot_ref, dcc_ref, *refs):
        ins, outs = refs[:3 * n_p], refs[3 * n_p:]

        def update(p, rows, cols, g):
            w_ref, m_ref, v_ref = ins[3 * p:3 * p + 3]
            g_ref, d_ref, nm_ref, nv_ref = outs[4 * p:4 * p + 4]
            d, nm, nv = _adam_val(w_ref[rows, cols], g, m_ref[rows, cols], v_ref[rows, cols])
            g_ref[rows, cols] = g
            d_ref[rows, cols] = d
            nm_ref[rows, cols] = nm
            nv_ref[rows, cols] = nv

        acc = dcc_ref[0, 0:1, :]
        for d in range(1, N_DEV):
            acc = acc + dcc_ref[d, 0:1, :]
        c = ins[0][...]
        sg = 1.0 / (1.0 + jnp.exp(-c))
        update(0, slice(0, 1), slice(None), acc * (sg * (1.0 + c * (1.0 - sg))))
        for l in range(DEPTH):
            for i in range(N_MOD):
                g = tot_ref[l * mod_rows + i:l * mod_rows + i + 1, :]
                for grp in range(1, n_groups):
                    g = g + tot_ref[l * mod_rows + grp * N_MOD + i:l * mod_rows + grp * N_MOD + i + 1, :]
                update(1, slice(l, l + 1), slice(i * D_MODEL, (i + 1) * D_MODEL), g)
            for j in range(4):
                row = DEPTH * mod_rows + 4 * l + j
                update(2 + j, slice(l, l + 1), slice(None), tot_ref[row:row + 1, :])

    shapes = [jax.ShapeDtypeStruct(w.shape, F32) for w, _, _ in params for _ in range(4)]
    outs = pl.pallas_call(body, name=name, out_shape=shapes)(tot, dcc_parts, *[a for p in params for a in p])
    return [tuple(outs[4 * p:4 * p + 4]) for p in range(n_p)]


def _adamw(w, g, m, v, name):
    r, c = w.shape
    tr = _pick(r, (256, 128, 64, 32, 24, 16, 8))

    def body(w_ref, g_ref, m_ref, v_ref, d_ref, nm_ref, nv_ref):
        d_ref[...], nm_ref[...], nv_ref[...] = _adam_val(w_ref[...], g_ref[...], m_ref[...], v_ref[...])

    spec = pl.BlockSpec((tr, c), lambda i: (i, 0))
    return pl.pallas_call(body, name=name, grid=(r // tr,), in_specs=[spec] * 4, out_specs=[spec] * 3,
                          out_shape=[jax.ShapeDtypeStruct((r, c), F32)] * 3, compiler_params=_params(("parallel",)))(w, g, m, v)


def _local_step(x, ctx, target, mods, gam, qn, kn, sink, w_first, w_layers, packed, c_idx, k_idx):
    nb, seq, _ = x.shape
    rt = _Rows(nb, seq, ctx.shape[1])
    tables = _rope_tables(rt)
    fuse = packed is not None
    h = jnp.concatenate([x.reshape(rt.n_lat, D_MODEL), ctx.reshape(rt.n_ctx, D_MODEL)], axis=0)
    wg = [{}, {}] if fuse else [dict(w) for w in w_layers]
    w_in = [_unpack_in_weight(w_first), None]
    saved = []
    for l in range(DEPTH):
        g_pre_mix, g_post_mix, g_pre_mlp, g_post_mlp = gam[l]
        if l == 1:
            buf, off = wg[1]["in"]
            w_in[1] = _unpack_in_weight(buf[:, off:off + PACK_HEIGHT["in"]])
        u, qkv, qkvp = _in_fwd(rt, h, g_pre_mix, mods[l], w_in[l], tables, qn[l], kn[l], f"in_fwd{l}")
        if fuse and l == 0:
            o, lse_lat, w_l0, w_mix1 = _attn_fwd(rt, qkvp, sink[l], None, f"attn_lat_fwd{l}", comm=_gather_comm(packed, [W_LAYER0, W_MIX1]))
            wg[0] = {kind: (w_l0, PACK_OFF[(kind, 0)] - W_LAYER0[0]) for kind in ("up", "down", "out")}
            wg[1] = {kind: (w_mix1, PACK_OFF[(kind, 1)] - W_MIX1[0]) for kind in ("out", "in")}
        else:
            o, lse_lat = _attn_fwd(rt, qkvp, sink[l], None, f"attn_lat_fwd{l}")
        o, lse_ctx = _attn_fwd(rt, qkvp, sink[l], o, f"attn_ctx_fwd{l}")
        mix, h1, u2 = _out_fwd(rt, o, wg[l], h, mods[l], g_post_mix, g_pre_mlp, f"out_fwd{l}")
        if fuse and l == 0:
            r, y, h2, w_mlp1 = _mlp_fwd(rt, u2, h1, wg[l], mods[l], g_post_mlp, f"mlp_fwd{l}", comm=_gather_comm(packed, [W_MLP1]))
            wg[1].update({kind: (w_mlp1, PACK_OFF[(kind, 1)] - W_MLP1[0]) for kind in ("up", "down")})
        else:
            r, y, h2 = _mlp_fwd(rt, u2, h1, wg[l], mods[l], g_post_mlp, f"mlp_fwd{l}")
        saved.append((h, u, qkv, qkvp, o, lse_lat, lse_ctx, mix, h1, u2, r, y))
        h = h2

    dh, sq = _loss_grad(rt, h, target.reshape(rt.n_lat, D_MODEL), "loss_grad")

    small = [None] * DEPTH
    groups = {}
    for l in reversed(range(DEPTH)):
        g_pre_mix, g_post_mix, g_pre_mlp, g_post_mlp = gam[l]
        h0, u, qkv, qkvp, o, lse_lat, lse_ctx, mix, h1, u2, r, y = saved[l]
        mlp_group, mix_group = (G_LAYER1, G_LAYER1) if l == 1 else (G_MLP0, G_MIX0)
        hide = fuse and l == 0

        outs = _mlp_down_bwd(rt, dh, y, r, wg[l], mods[l], g_post_mlp, f"mlp_down_bwd{l}",
                             comm=_pair_comm(groups[G_LAYER1]) if hide else None)
        dy, da, d_gate_m, d_g_post_mlp = outs[:4]
        if hide:
            sum1 = _pair_sum(groups[G_LAYER1], outs[4], c_idx, "grad_pair_sum_layer1")
        p_mlp = _wgrad_packed(rt, r, dy, "down", PACK_OFF[("down", l)] - mlp_group[0], mlp_group[1], None, f"mlp_down_wgrad{l}")
        dh1, d_sh_m, d_sc_m, d_g_pre_mlp = _mlp_up_bwd(rt, da, wg[l], h1, dh, mods[l], g_pre_mlp, f"mlp_up_bwd{l}")
        p_mlp = _wgrad_packed(rt, u2, da, "up", PACK_OFF[("up", l)] - mlp_group[0], mlp_group[1], p_mlp, f"mlp_up_wgrad{l}")
        outs = _out_bwd(rt, dh1, mix, wg[l], mods[l], g_post_mix, f"out_bwd{l}", comm=_pair_comm(p_mlp) if hide else None)
        dmix, do, d_gate_a, d_g_post_mix = outs[:4]
        if hide:
            sum0 = _pair_sum(p_mlp, outs[4], c_idx, "grad_pair_sum_mlp0")
        p_mix = _wgrad_packed(rt, o, dmix, "out", PACK_OFF[("out", l)] - mix_group[0], mix_group[1],
                              p_mlp if l == 1 else None, f"out_wgrad{l}")
        outs = _attn_bwd(rt, qkvp, o, lse_lat, do, sink[l], None, f"attn_lat_bwd{l}",
                         comm=_chip_comm([sum1[1], sum0[1]]) if hide else None)
        dq, dkv, dkv_c, dsink1 = outs[:4]
        if hide:
            groups[G_LAYER1] = _owner_sum(sum1[0], outs[4], k_idx, "grad_owner_sum_layer1")
            groups[G_MLP0] = _owner_sum(sum0[0], outs[5], k_idx, "grad_owner_sum_mlp0")
        dq, dkv, dsink2 = _attn_bwd(rt, qkvp, o, lse_ctx, do, sink[l], (dq, dkv, dkv_c), f"attn_ctx_bwd{l}")
        dqkv, dh, dqn, dkn, d_sh_a, d_sc_a, d_g_pre_mix = _in_bwd(rt, dq, dkv, qkv, tables, qn[l], kn[l], w_in[l], h0, dh1, mods[l],
                                                                  g_pre_mix, l == 0, f"in_bwd{l}")
        dw_in = _wgrad_plain(rt, u, dqkv, f"in_wgrad{l}")
        o_in = PACK_OFF[("in", l)] - mix_group[0]
        p_mix = p_mix.at[:, :, o_in:o_in + PACK_HEIGHT["in"]].set(_pack_in_grad(dw_in))
        groups[mix_group] = p_mix
        if not hide and l == 0:
            groups[G_MLP0] = p_mlp
        dmod = jnp.concatenate([d_sh_a, d_sc_a, d_gate_a, d_sh_m, d_sc_m, d_gate_m], axis=1)
        small[l] = dict(mod=dmod, gammas=jnp.concatenate([d_g_pre_mix, d_g_post_mix, d_g_pre_mlp, d_g_post_mlp], axis=0),
                        q_norm=dqn, k_norm=dkn, sink=(dsink1 + dsink2)[:, 0])
    return sq, dh.reshape(nb, seq, D_MODEL), [groups[G_LAYER1], groups[G_MLP0], groups[G_MIX0]], small


SMALL_ROWS = 48


def kernel(x, c, ctx, c_ctx, w_ada, b_ada, g_pre_mix, g_post_mix, g_pre_mlp, g_post_mlp, w_in, q_norm, k_norm, sink, w_out, w_up, w_down, loss_target, m_c_ctx, m_w_ada, m_b_ada, m_g_pre_mix, m_g_post_mix, m_g_pre_mlp, m_g_post_mlp, m_w_in, m_q_norm, m_k_norm, m_sink, m_w_out, m_w_up, m_w_down, v_c_ctx, v_w_ada, v_b_ada, v_g_pre_mix, v_g_post_mix, v_g_pre_mlp, v_g_post_mlp, v_w_in, v_q_norm, v_k_norm, v_sink, v_w_out, v_w_up, v_w_down):
    nb = x.shape[0]
    ix, iy, ic = lax.axis_index("x"), lax.axis_index("y"), lax.axis_index("c")
    chip = 2 * ix + iy
    dev = 2 * chip + ic
    ada_cols = w_ada.shape[2] // 2

    c_all = _all_gather(c.reshape(8, (nb * D_MODEL) // 8), "gather_c", False).reshape(N_DEV * nb, D_MODEL)
    n_cond = N_DEV * nb + 1
    cond_rows = 16 * ((n_cond + 15) // 16)
    cond = jnp.concatenate([c_all, c_ctx[None, :], jnp.zeros((cond_rows - n_cond, D_MODEL), F32)], axis=0)
    c_idx = ic.reshape(1).astype(jnp.int32)
    kc_idx = jnp.stack([chip, ic]).astype(jnp.int32)
    b_ada_half = lax.dynamic_slice_in_dim(b_ada, dev * ada_cols, ada_cols, 1)[:, None, :]
    x_ada, mod_part = _ada_fwd(cond, w_ada, b_ada_half, c_idx, "ada_fwd")
    mod_g = _all_gather(mod_part.reshape(DEPTH * cond_rows, ada_cols), "gather_mod", False)
    mod_all = mod_g.reshape(N_DEV, DEPTH, cond_rows, ada_cols).transpose(1, 2, 0, 3).reshape(DEPTH, cond_rows, N_MOD * D_MODEL)
    mods = []
    for l in range(DEPTH):
        mine = lax.dynamic_slice_in_dim(mod_all[l], dev * nb, nb, 0)
        mods.append(jnp.concatenate([mine, mod_all[l, n_cond - 1:n_cond]], axis=0).reshape(nb + 1, N_MOD, D_MODEL))

    packed = _pack_local_half(w_in, w_out, w_up, w_down, ic)
    w_first, = _comm_alone(_gather_comm(packed, [W_FIRST]), "gather_w_first")

    gam = [(g_pre_mix[l][None], g_post_mix[l][None], g_pre_mlp[l][None], g_post_mlp[l][None]) for l in range(DEPTH)]
    qn = [jnp.tile(q_norm[l], 2)[None] for l in range(DEPTH)]
    kn = [jnp.tile(k_norm[l], 2)[None] for l in range(DEPTH)]
    sq, grad_x, (h_layer1, h_mlp0, p_mix0), lg = _local_step(x, ctx, loss_target, mods, gam, qn, kn, [sink[l] for l in range(DEPTH)],
                                                           w_first, None, packed, c_idx, kc_idx)

    def step(w, g, m, v, name):
        shape = w.shape
        cols = shape[-1]
        outs = _adamw(w.reshape(-1, cols), g.reshape(-1, cols), m.reshape(-1, cols), v.reshape(-1, cols), name)
        return tuple(a.reshape(shape) for a in outs)

    def piece(halves, kind, l, group):
        o = PACK_OFF[(kind, l)] - group[0]
        rows = halves[:, o:o + PACK_HEIGHT[kind]]
        return rows.reshape(1024, 384) if kind == "in" else rows.reshape(2 * PACK_HEIGHT[kind], 1024)

    r1, = _comm_alone(_pair_comm(p_mix0), "grad_pair_exchange_mix0")
    a32, a16 = _pair_sum(p_mix0, r1, c_idx, "grad_pair_sum_mix0")
    r2, = _comm_alone(_chip_comm([a16]), "grad_chip_exchange_mix0")
    h_mix0 = _owner_sum(a32, r2, kc_idx, "grad_owner_sum_mix0")
    h_layer1, h_mlp0, h_mix0 = _comm_alone(_halves_comm([h_layer1, h_mlp0, h_mix0]), "grad_halves_exchange")
    grad_w_up = jnp.stack([piece(h_mlp0, "up", 0, G_MLP0), piece(h_layer1, "up", 1, G_LAYER1)])
    grad_w_down = jnp.stack([piece(h_mlp0, "down", 0, G_MLP0), piece(h_layer1, "down", 1, G_LAYER1)])
    grad_w_in = jnp.stack([piece(h_mix0, "in", 0, G_MIX0), piece(h_layer1, "in", 1, G_LAYER1)])
    grad_w_out = jnp.stack([piece(h_mix0, "out", 0, G_MIX0), piece(h_layer1, "out", 1, G_LAYER1)])

    def lane_pad(v):
        return jnp.pad(v, (0, D_MODEL - v.shape[0]))[None]

    head_rows = [lane_pad(jnp.concatenate([lg[l]["q_norm"][0], lg[l]["k_norm"][0], lg[l]["sink"]])) for l in range(DEPTH)]
    loss_row = lane_pad((0.5 / D_MODEL) * jnp.sum(sq, keepdims=True)[0])
    small = jnp.concatenate([lg[l]["mod"].reshape((nb + 1) * N_MOD, D_MODEL) for l in range(DEPTH)]
                            + [lg[l]["gammas"] for l in range(DEPTH)] + head_rows + [loss_row], axis=0)
    small = jnp.pad(small, ((0, SMALL_ROWS - small.shape[0]), (0, 0)))
    small_g = _all_gather(small, "gather_small", False).reshape(N_DEV, SMALL_ROWS, D_MODEL)
    tot = _dev_sum(small_g, "small_sum")
    mod_rows = (nb + 1) * N_MOD
    o_head = DEPTH * mod_rows + 4 * DEPTH
    loss = tot[o_head + DEPTH, 0]
    grad_q_norm = tot[o_head:o_head + DEPTH, 0:64] + tot[o_head:o_head + DEPTH, 64:128]
    grad_k_norm = tot[o_head:o_head + DEPTH, 128:192] + tot[o_head:o_head + DEPTH, 192:256]
    grad_sink = tot[o_head:o_head + DEPTH, 256:264]

    ex = small_g[:, :DEPTH * mod_rows].reshape(N_DEV, DEPTH, nb + 1, N_MOD * D_MODEL)[:, :, :nb]
    ex = ex.transpose(1, 0, 2, 3).reshape(DEPTH, N_DEV * nb, N_MOD * D_MODEL)
    cx = tot[:DEPTH * mod_rows].reshape(DEPTH, nb + 1, N_MOD * D_MODEL)[:, nb:]
    dm = jnp.concatenate([ex, cx, jnp.zeros((DEPTH, cond_rows - n_cond, N_MOD * D_MODEL), F32)], axis=1)
    shard_cols = w_ada.shape[2]
    grad_w_ada = _ada_wgrad(x_ada, lax.dynamic_slice_in_dim(dm, chip * shard_cols, shard_cols, 2).astype(BF16), "ada_wgrad")
    dcx = jnp.pad(lax.dynamic_slice_in_dim(cx, dev * ada_cols, ada_cols, 2), ((0, 0), (0, 15), (0, 0))).astype(BF16)
    dcc = _ada_cond_bwd(dcx, w_ada, c_idx, "ada_cond_bwd")[0:8]
    dcc_g = _all_gather(dcc, "gather_cond_grad", False).reshape(N_DEV, 8, D_MODEL)

    dense_names = ["c_ctx", "b_ada", "g_pre_mix", "g_post_mix", "g_pre_mlp", "g_post_mlp"]
    dense = _small_update(tot, dcc_g, [(c_ctx[None], m_c_ctx[None], v_c_ctx[None]), (b_ada, m_b_ada, v_b_ada),
                                       (g_pre_mix, m_g_pre_mix, v_g_pre_mix), (g_post_mix, m_g_post_mix, v_g_post_mix),
                                       (g_pre_mlp, m_g_pre_mlp, v_g_pre_mlp), (g_post_mlp, m_g_post_mlp, v_g_post_mlp)],
                          nb + 1, "small_update")
    res = {n: r for n, r in zip(dense_names, dense)}
    res["c_ctx"] = tuple(a[0] for a in res["c_ctx"])
    small_names = ["q_norm", "k_norm", "sink"]
    small_w = [q_norm, k_norm, sink]
    small_gr = [grad_q_norm, grad_k_norm, grad_sink]
    small_m = [m_q_norm, m_k_norm, m_sink]
    small_v = [v_q_norm, v_k_norm, v_sink]
    sizes = [int(np.prod(w.shape)) for w in small_w]
    total = sum(sizes)
    flat_rows = 8 * ((total + 8 * D_MODEL - 1) // (8 * D_MODEL))

    def flat(arrs, fill):
        f = jnp.concatenate([a.reshape(-1) for a in arrs])
        return jnp.concatenate([f, jnp.full((flat_rows * D_MODEL - total,), fill, F32)]).reshape(flat_rows, D_MODEL)

    sd, snm, snv = _adamw(flat(small_w, 0.0), flat(small_gr, 0.0), flat(small_m, 0.0), flat(small_v, 1.0), "adamw_small")[:3]

    def unflat(f):
        f = f.reshape(-1)
        out, off = [], 0
        for w, n in zip(small_w, sizes):
            out.append(f[off:off + n].reshape(w.shape))
            off += n
        return out

    small_d, small_nm, small_nv = unflat(sd), unflat(snm), unflat(snv)
    res.update({n: (g, d, nm, nv) for n, g, d, nm, nv in zip(small_names, small_gr, small_d, small_nm, small_nv)})
    res["w_ada"] = (grad_w_ada, *step(w_ada, grad_w_ada, m_w_ada, v_w_ada, "adamw_w_ada"))
    res["w_in"] = (grad_w_in, *step(w_in, grad_w_in, m_w_in, v_w_in, "adamw_w_in"))
    res["w_out"] = (grad_w_out, *step(w_out, grad_w_out, m_w_out, v_w_out, "adamw_w_out"))
    res["w_up"] = (grad_w_up, *step(w_up, grad_w_up, m_w_up, v_w_up, "adamw_w_up"))
    res["w_down"] = (grad_w_down, *step(w_down, grad_w_down, m_w_down, v_w_down, "adamw_w_down"))

    order = ["c_ctx", "w_ada", "b_ada", "g_pre_mix", "g_post_mix", "g_pre_mlp", "g_post_mlp", "w_in", "q_norm", "k_norm", "sink", "w_out", "w_up", "w_down"]
    return (loss, grad_x, *[res[n][0] for n in order], *[res[n][1] for n in order],
            *[res[n][2] for n in order], *[res[n][3] for n in order])
```

```python
import functools

import jax
import jax.numpy as jnp
import numpy as np
from jax import lax
from jax.experimental import pallas as pl
from jax.experimental.pallas import tpu as pltpu

F32 = jnp.float32
BF16 = jnp.bfloat16

D_MODEL = 1024
HEAD_DIM = 64
GROUP = 4
WINDOW = 128
N_MOD = 6
D_FF = 4 * D_MODEL
IN_COLS = 1536
GRID_W = 64
ROPE_THETA = 10000.0
EPS = 1e-6
NEG_BIG = -1e30
Q_SCALE = HEAD_DIM ** -0.5
DEPTH = 2
N_DEV = 8

ADAM_LR = 0.001
ADAM_B1 = 0.9
ADAM_B2 = 0.999
ADAM_EPS = 1e-08
ADAM_WD = 0.01
ADAM_STEP = 10

V7X_VMEM_BYTES = 64 * 1024 * 1024
VMEM_LIMIT = V7X_VMEM_BYTES - 8 * 1024 * 1024

MESH = pl.DeviceIdType.MESH
NT = (((1,), (1,)), ((), ()))
TN = (((0,), (0,)), ((), ()))

COL_KA, COL_VA, COL_KB, COL_VB = 4, 5, 10, 11
NORMED_COLS = 640

PACK_HEIGHT = {"up": 512, "down": 512, "in": 256, "out": 128}
IN_PIECE_COLS = 384
PACK_OFF = {("up", 0): 0, ("down", 0): 512, ("in", 0): 1024, ("out", 0): 1280,
            ("up", 1): 1408, ("down", 1): 1920, ("in", 1): 2432, ("out", 1): 2688}
PACK_ROWS = 2816
W_FIRST, W_MLP0, W_OUT0, W_MLP1, W_MIX1 = (1024, 256), (0, 1024), (1280, 128), (1408, 1024), (2432, 384)
G_LAYER1, G_MLP0, G_MIX0 = (1408, 1408), (0, 1024), (1024, 384)


def _pick(n, cands):
    for t in cands:
        if n % t == 0:
            return t
    raise ValueError(f"no tile for {n}")


def _params(sem):
    return pltpu.CompilerParams(dimension_semantics=sem, vmem_limit_bytes=VMEM_LIMIT)


def _all_gather(x, name, in_hbm):
    m_per, n = x.shape
    space = pl.ANY if in_hbm else pltpu.VMEM

    def body(x_ref, out_ref, send_sems, recv_sems, local_sem):
        x_, y_, c_ = lax.axis_index("x"), lax.axis_index("y"), lax.axis_index("c")
        me, sibling = (x_, y_, c_), (x_, y_, 1 - c_)
        chips = [(1 - x_, y_), (x_, 1 - y_), (1 - x_, 1 - y_)]

        def rows(px, py, pc):
            return out_ref.at[pl.ds((4 * px + 2 * py + pc) * m_per, m_per), :]

        def copy(k, block, to, src=None):
            return pltpu.make_async_remote_copy(
                src_ref=rows(*block) if src is None else src, dst_ref=rows(*block),
                send_sem=send_sems.at[k], recv_sem=recv_sems.at[k], device_id=to, device_id_type=MESH)

        mine = pltpu.make_async_copy(x_ref, rows(*me), local_sem)
        mine.start()
        first = [copy(0, me, sibling, src=x_ref)]
        first += [copy(1 + j, me, (*chip, c_), src=x_ref) for j, chip in enumerate(chips)]
        for cp in first:
            cp.start()
        passed = [copy(4 + j, (*chip, c_), sibling) for j, chip in enumerate(chips)]
        for j, chip in enumerate(chips):
            copy(1 + j, (*chip, c_), me).wait_recv()
            passed[j].start()
        copy(0, sibling, me).wait_recv()
        for j, chip in enumerate(chips):
            copy(4 + j, (*chip, 1 - c_), me).wait_recv()
        for cp in first + passed:
            cp.wait_send()
        mine.wait()

    return pl.pallas_call(
        body, name=name,
        out_shape=jax.ShapeDtypeStruct((N_DEV * m_per, n), x.dtype),
        in_specs=[pl.BlockSpec(memory_space=space)],
        out_specs=pl.BlockSpec(memory_space=space),
        scratch_shapes=[pltpu.SemaphoreType.DMA((7,)), pltpu.SemaphoreType.DMA((7,)), pltpu.SemaphoreType.DMA],
    )(x)


class _Comm:
    def __init__(self, inputs, out_shapes, aliases, n_send, n_recv, start, finish):
        self.inputs, self.out_shapes, self.aliases = list(inputs), list(out_shapes), dict(aliases)
        self.n_send, self.n_recv, self.start, self.finish = n_send, n_recv, start, finish


def _comm_call(compute, comm, *, name, grid, in_specs, out_specs, out_shape, args, aliases, semantics, scratch=()):
    in_specs, out_specs, out_shape, args, aliases = list(in_specs), list(out_specs), list(out_shape), list(args), dict(aliases)
    scratch = list(scratch)
    if comm is None:
        return pl.pallas_call(compute, name=name, grid=grid, in_specs=in_specs, out_specs=out_specs, out_shape=out_shape,
                              input_output_aliases=aliases, scratch_shapes=scratch, compiler_params=_params(semantics))(*args)
    n_in, n_out, n_ci, n_co = len(args), len(out_shape), len(comm.inputs), len(comm.out_shapes)
    hbm = pl.BlockSpec(memory_space=pl.ANY)
    aliases.update({n_in + i: n_out + o for i, o in comm.aliases.items()})

    def body(*refs):
        ins, c_ins = refs[:n_in], refs[n_in:n_in + n_ci]
        outs, c_outs = refs[n_in + n_ci:n_in + n_ci + n_out], refs[n_in + n_ci + n_out:n_in + n_ci + n_out + n_co]
        scr = refs[n_in + n_ci + n_out + n_co:-2]
        send_sems, recv_sems = refs[-2:]
        ids = [pl.program_id(a) for a in range(len(grid))]
        first = functools.reduce(jnp.logical_and, [i == 0 for i in ids])
        last = functools.reduce(jnp.logical_and, [i == g - 1 for i, g in zip(ids, grid)])

        @pl.when(first)
        def _():
            comm.start(c_ins, c_outs, send_sems, recv_sems)

        compute(*ins, *outs, *scr)

        @pl.when(last)
        def _():
            comm.finish(c_ins, c_outs, send_sems, recv_sems)

    return pl.pallas_call(
        body, name=name, grid=grid,
        in_specs=in_specs + [hbm] * n_ci, out_specs=out_specs + [hbm] * n_co, out_shape=out_shape + comm.out_shapes,
        input_output_aliases=aliases,
        scratch_shapes=scratch + [pltpu.SemaphoreType.DMA((comm.n_send,)), pltpu.SemaphoreType.DMA((comm.n_recv,))],
        compiler_params=_params(("arbitrary",) * len(grid)),
    )(*args, *comm.inputs)


def _place():
    x_, y_, c_ = lax.axis_index("x"), lax.axis_index("y"), lax.axis_index("c")
    return x_, y_, c_, [(1 - x_, y_), (x_, 1 - y_), (1 - x_, 1 - y_)]


GATHER_SENDS, GATHER_RECVS = 8, 7


def _gather_copies(packed_ref, wg_ref, send_sems, recv_sems, rows, nth=0):
    r0, n = rows
    x_, y_, c_, chips = _place()
    me, sibling = (x_, y_, c_), (x_, y_, 1 - c_)
    src = packed_ref.at[pl.ds(r0, n), :]

    def slot(px, py, pc):
        return wg_ref.at[4 * px + 2 * py + pc]

    def copy(k, block, to, from_packed=False):
        return pltpu.make_async_remote_copy(src_ref=src if from_packed else slot(*block), dst_ref=slot(*block),
                                            send_sem=send_sems.at[GATHER_SENDS * nth + k], recv_sem=recv_sems.at[GATHER_RECVS * nth + k],
                                            device_id=to, device_id_type=MESH)

    own = [copy(0, me, sibling, True)] + [copy(1 + j, me, (*chip, c_), True) for j, chip in enumerate(chips)]
    passed = [copy(4 + j, (*chip, c_), sibling) for j, chip in enumerate(chips)]
    over_ici = [copy(1 + j, (*chip, c_), me) for j, chip in enumerate(chips)]
    from_sibling = [copy(0, sibling, me)] + [copy(4 + j, (*chip, 1 - c_), me) for j, chip in enumerate(chips)]
    mine = pltpu.make_async_copy(src, slot(*me), send_sems.at[GATHER_SENDS * nth + 7])
    return mine, own, passed, over_ici, from_sibling


def _gather_start(packed_ref, wg_ref, send_sems, recv_sems, rows, nth=0):
    mine, own, _, _, _ = _gather_copies(packed_ref, wg_ref, send_sems, recv_sems, rows, nth)
    mine.start()
    for cp in own:
        cp.start()


def _gather_finish(packed_ref, wg_ref, send_sems, recv_sems, rows, nth=0):
    mine, own, passed, over_ici, from_sibling = _gather_copies(packed_ref, wg_ref, send_sems, recv_sems, rows, nth)
    for arrived, onward in zip(over_ici, passed):
        arrived.wait_recv()
        onward.start()
    for arrived in from_sibling:
        arrived.wait_recv()
    for cp in own + passed:
        cp.wait_send()
    mine.wait()


def _gather_comm(packed, ranges):
    shapes = [jax.ShapeDtypeStruct((N_DEV, n, packed.shape[1]), packed.dtype) for _, n in ranges]

    def start(ins, outs, ss, rs):
        for nth, rows in enumerate(ranges):
            _gather_start(ins[0], outs[nth], ss, rs, rows, nth)

    def finish(ins, outs, ss, rs):
        for nth, rows in enumerate(ranges):
            _gather_finish(ins[0], outs[nth], ss, rs, rows, nth)

    return _Comm([packed], shapes, {}, GATHER_SENDS * len(ranges), GATHER_RECVS * len(ranges), start, finish)


def _pair_copy(p_ref, out_ref, send_sems, recv_sems):
    x_, y_, c_, _ = _place()
    return pltpu.make_async_remote_copy(src_ref=p_ref.at[1 - c_], dst_ref=out_ref,
                                        send_sem=send_sems.at[0], recv_sem=recv_sems.at[0],
                                        device_id=(x_, y_, 1 - c_), device_id_type=MESH)


def _pair_comm(p):
    return _Comm([p], [jax.ShapeDtypeStruct(p.shape[1:], p.dtype)], {}, 1, 1,
                 lambda ins, outs, ss, rs: _pair_copy(ins[0], outs[0], ss, rs).start(),
                 lambda ins, outs, ss, rs: _pair_copy(ins[0], outs[0], ss, rs).wait())


def _chip_copies(a_refs, out_refs, send_sems, recv_sems):
    _, _, c_, chips = _place()
    return [pltpu.make_async_remote_copy(src_ref=a_ref.at[2 * tx + ty], dst_ref=o_ref.at[j],
                                         send_sem=send_sems.at[3 * g + j], recv_sem=recv_sems.at[3 * g + j],
                                         device_id=(tx, ty, c_), device_id_type=MESH)
            for g, (a_ref, o_ref) in enumerate(zip(a_refs, out_refs)) for j, (tx, ty) in enumerate(chips)]


def _chip_start(a_refs, out_refs, send_sems, recv_sems):
    for cp in _chip_copies(a_refs, out_refs, send_sems, recv_sems):
        cp.start()


def _chip_finish(a_refs, out_refs, send_sems, recv_sems):
    for cp in _chip_copies(a_refs, out_refs, send_sems, recv_sems):
        cp.wait()


def _chip_comm(arrays):
    shapes = [jax.ShapeDtypeStruct((3,) + a.shape[1:], a.dtype) for a in arrays]
    return _Comm(arrays, shapes, {}, 3 * len(arrays), 3 * len(arrays), _chip_start, _chip_finish)


def _halves_copies(in_refs, out_refs, send_sems, recv_sems):
    x_, y_, c_, _ = _place()
    return [pltpu.make_async_remote_copy(src_ref=o_ref.at[c_], dst_ref=o_ref.at[c_], send_sem=send_sems.at[i], recv_sem=recv_sems.at[i],
                                         device_id=(x_, y_, 1 - c_), device_id_type=MESH)
            for i, o_ref in enumerate(out_refs)]


def _halves_start(in_refs, out_refs, send_sems, recv_sems):
    for cp in _halves_copies(in_refs, out_refs, send_sems, recv_sems):
        cp.start()


def _halves_finish(in_refs, out_refs, send_sems, recv_sems):
    for cp in _halves_copies(in_refs, out_refs, send_sems, recv_sems):
        cp.wait()


def _halves_comm(arrays):
    shapes = [jax.ShapeDtypeStruct(a.shape, a.dtype) for a in arrays]
    return _Comm(arrays, shapes, {i: i for i in range(len(arrays))}, len(arrays), len(arrays), _halves_start, _halves_finish)


def _comm_alone(comm, name):
    n_ci = len(comm.inputs)
    hbm = pl.BlockSpec(memory_space=pl.ANY)

    def body(*refs):
        c_ins, c_outs, send_sems, recv_sems = refs[:n_ci], refs[n_ci:-2], refs[-2], refs[-1]
        comm.start(c_ins, c_outs, send_sems, recv_sems)
        comm.finish(c_ins, c_outs, send_sems, recv_sems)

    return pl.pallas_call(
        body, name=name, out_shape=comm.out_shapes, in_specs=[hbm] * n_ci, out_specs=[hbm] * len(comm.out_shapes),
        input_output_aliases=comm.aliases,
        scratch_shapes=[pltpu.SemaphoreType.DMA((comm.n_send,)), pltpu.SemaphoreType.DMA((comm.n_recv,))],
    )(*comm.inputs)


SUM_TILES = (512, 384, 320, 256, 192, 128, 64)


def _pair_sum(p, r1, c_idx, name):
    _, _, n, c = p.shape
    tr = _pick(n, SUM_TILES)

    def body(s_ref, p_ref, r_ref, o32_ref, o16_ref):
        v = p_ref[...] + r_ref[...]
        o32_ref[...] = v
        o16_ref[...] = v.astype(BF16)

    blk = pl.BlockSpec((None, tr, c), lambda j, i, s: (j, i, 0))
    grid_spec = pltpu.PrefetchScalarGridSpec(
        num_scalar_prefetch=1, grid=(4, n // tr),
        in_specs=[pl.BlockSpec((None, None, tr, c), lambda j, i, s: (s[0], j, i, 0)), blk],
        out_specs=[blk, blk])
    return pl.pallas_call(
        body, name=name, grid_spec=grid_spec,
        out_shape=[jax.ShapeDtypeStruct((4, n, c), F32), jax.ShapeDtypeStruct((4, n, c), BF16)],
        compiler_params=_params(("arbitrary", "arbitrary")),
    )(c_idx, p, r1)


def _owner_sum(a32, r2, kc_idx, name):
    _, r, c = a32.shape
    tr = _pick(r, SUM_TILES)

    def body(s_ref, a_ref, r_ref, o_ref):
        v = a_ref[...]
        for j in range(3):
            v = v + r_ref[j].astype(F32)
        o_ref[...] = v

    grid_spec = pltpu.PrefetchScalarGridSpec(
        num_scalar_prefetch=1, grid=(r // tr,),
        in_specs=[pl.BlockSpec((None, tr, c), lambda i, s: (s[0], i, 0)),
                  pl.BlockSpec((3, tr, c), lambda i, s: (0, i, 0))],
        out_specs=pl.BlockSpec((None, tr, c), lambda i, s: (s[1], i, 0)))
    return pl.pallas_call(
        body, name=name, grid_spec=grid_spec,
        out_shape=jax.ShapeDtypeStruct((2, r, c), F32),
        compiler_params=_params(("arbitrary",)),
    )(kc_idx, a32, r2)


def _pack_local_half(w_in_s, w_out_s, w_up_s, w_down_s, c_idx):
    parts, row = [], 0
    for (kind, l), off in sorted(PACK_OFF.items(), key=lambda kv: kv[1]):
        if off > row:
            parts.append(jnp.zeros((off - row, 1024), BF16))
        if kind == "up":
            p = lax.dynamic_slice_in_dim(w_up_s[l], c_idx * 512, 512, 0)
        elif kind == "down":
            p = lax.dynamic_slice_in_dim(w_down_s[l], c_idx * 512, 512, 0)
        elif kind == "in":
            p = lax.dynamic_slice_in_dim(w_in_s[l], c_idx * 512, 512, 0)
            p = p.reshape(2, 256, IN_PIECE_COLS).transpose(1, 0, 2).reshape(256, 2 * IN_PIECE_COLS)
            p = jnp.pad(p, ((0, 0), (0, 1024 - 2 * IN_PIECE_COLS)))
        else:
            p = lax.dynamic_slice_in_dim(w_out_s[l], c_idx * 128, 128, 0)
        parts.append(p.astype(BF16))
        row = off + PACK_HEIGHT[kind]
    return jnp.concatenate(parts, axis=0)


def _unpack_in_pieces(w_ref, w_scr):
    for d in range(N_DEV):
        k, c = d // 2, d % 2
        for t in range(2):
            w_scr[c * 512 + t * 256:c * 512 + (t + 1) * 256, k * IN_PIECE_COLS:(k + 1) * IN_PIECE_COLS] = \
                w_ref[d, :, t * IN_PIECE_COLS:(t + 1) * IN_PIECE_COLS]


class _Rows:
    def __init__(self, nb, seq, ctx):
        self.nb, self.seq, self.ctx = nb, seq, ctx
        self.n_lat, self.n_ctx = nb * seq, nb * ctx
        self.rows = self.n_lat + self.n_ctx
        self.tm = _pick(np.gcd(seq, self.n_ctx), (512, 256, 128))
        self.tiles_per_ex = seq // self.tm
        self.n_tiles = self.rows // self.tm
        self.n_lat_tiles = self.n_lat // self.tm
        self.groups = nb + 1

    def group(self, i):
        return jnp.minimum(i // self.tiles_per_ex, self.nb)

    def first_of_group(self, i):
        return jnp.logical_and(i % self.tiles_per_ex == 0, i <= self.n_lat_tiles)


def _mod_spec(rt):
    return pl.BlockSpec((1, N_MOD, D_MODEL), lambda i: (rt.group(i), 0, 0))


def _row_spec(rt, cols):
    return pl.BlockSpec((rt.tm, cols), lambda i: (i, 0))


def _vec_spec(cols):
    return pl.BlockSpec((1, cols), lambda i: (0, 0))


def _group_spec(rt):
    return pl.BlockSpec((1, 1, D_MODEL), lambda i: (rt.group(i), 0, 0))


def _gathered_spec(wg, kind):
    h, off = PACK_HEIGHT[kind], wg[kind][1]
    assert off % h == 0, (kind, off)
    return pl.BlockSpec((N_DEV, h, 1024), lambda *_: (0, off // h, 0), pipeline_mode=pl.Buffered(1))


def _group_shape(rt):
    return jax.ShapeDtypeStruct((rt.groups, 1, D_MODEL), F32)


def _vec_shape(cols=D_MODEL):
    return jax.ShapeDtypeStruct((1, cols), F32)


def _rms_inv(v):
    return lax.rsqrt(jnp.mean(v * v, axis=-1, keepdims=True) + EPS)


def _norm_mod_val(h_, g_, mod_ref, i_shift, i_scale):
    n = h_ * _rms_inv(h_) * g_
    return n * (1.0 + mod_ref[0, i_scale:i_scale + 1, :]) + mod_ref[0, i_shift:i_shift + 1, :]


def _post_norm_val(h_, z_, g_, mod_ref, i_gate):
    return h_ + mod_ref[0, i_gate:i_gate + 1, :] * (z_ * _rms_inv(z_) * g_)


def _post_norm_bwd_val(dh_, z_, g_, gate):
    rinv = _rms_inv(z_)
    n0 = z_ * rinv
    dn = dh_ * gate * g_
    dz = rinv * (dn - n0 * jnp.mean(dn * n0, axis=-1, keepdims=True))
    return dz, jnp.sum(dh_ * n0 * g_, axis=0, keepdims=True), jnp.sum(dh_ * gate * n0, axis=0, keepdims=True)


def _norm_mod_bwd_val(du_, h_, g_, one_sc):
    rinv = _rms_inv(h_)
    n0 = h_ * rinv
    dn = du_ * g_ * one_sc
    dh = rinv * (dn - n0 * jnp.mean(dn * n0, axis=-1, keepdims=True))
    return (dh, jnp.sum(du_, axis=0, keepdims=True), jnp.sum(du_ * n0 * g_, axis=0, keepdims=True),
            jnp.sum(du_ * one_sc * n0, axis=0, keepdims=True))


def _accumulate(rt, i, group_pairs, global_pairs):
    @pl.when(rt.first_of_group(i))
    def _():
        for ref, _ in group_pairs:
            ref[...] = jnp.zeros_like(ref)

    @pl.when(i == 0)
    def _():
        for ref, _ in global_pairs:
            ref[...] = jnp.zeros_like(ref)

    for ref, val in group_pairs:
        ref[0] += val
    for ref, val in global_pairs:
        ref[...] += val


def _rope_tables(rt):
    pos = jnp.arange(rt.seq, dtype=jnp.int32)
    row_ids = (pos // GRID_W).astype(F32)
    col_ids = (pos % GRID_W).astype(F32)
    axis_dim = HEAD_DIM // 2
    inv = ROPE_THETA ** (-jnp.arange(0, axis_dim, 2, dtype=F32) / axis_dim)
    ang_r, ang_c = row_ids[:, None] * inv[None, :], col_ids[:, None] * inv[None, :]
    cr, sr, cc, sc = jnp.cos(ang_r), jnp.sin(ang_r), jnp.cos(ang_c), jnp.sin(ang_c)
    zero = jnp.zeros_like(sr)
    cos = jnp.concatenate([cr, cr, cc, cc], axis=1)
    s_lo = jnp.concatenate([zero, sr, zero, sc], axis=1)
    s_hi = jnp.concatenate([-sr, zero, -sc, zero], axis=1)

    def full(t, ctx_value):
        t = jnp.tile(t, (rt.nb, 2))
        return jnp.concatenate([t, jnp.full((rt.n_ctx, 128), ctx_value, F32)], axis=0)

    return full(cos, 1.0), full(s_lo, 0.0), full(s_hi, 0.0)


def _head_stats(t, lo):
    sq = t * t
    s_lo = jnp.sum(jnp.where(lo, sq, 0.0), axis=1, keepdims=True)
    s_hi = jnp.sum(jnp.where(lo, 0.0, sq), axis=1, keepdims=True)
    return lax.rsqrt(jnp.where(lo, s_lo, s_hi) * (1.0 / HEAD_DIM) + EPS)


def _prep_fwd_body(tm, qkv_ref, c, s1, s2, qn, kn, out_ref):
    lo = lax.broadcasted_iota(jnp.int32, (tm, 128), 1) < HEAD_DIM

    def rope(t):
        return t * c + pltpu.roll(t, 16, 1) * s1 + pltpu.roll(t, 112, 1) * s2

    for j in range(12):
        t = qkv_ref[:, j * 128:(j + 1) * 128]
        if j < 4:
            t = rope(t * _head_stats(t, lo) * qn) * Q_SCALE
        elif j == COL_KA:
            t = rope(t * _head_stats(t, lo) * kn)
        elif 6 <= j < 10:
            t = rope(t) * Q_SCALE
        elif j == COL_KB:
            t = rope(t)
        out_ref[:, j * 128:(j + 1) * 128] = t.astype(BF16)


def _prep_bwd_body(tm, dq_ref, dkv_ref, qkv_ref, c, s1, s2, qn, kn, out_ref):
    lo = lax.broadcasted_iota(jnp.int32, (tm, 128), 1) < HEAD_DIM

    def rope_bwd(d):
        return d * c + pltpu.roll(d * s1, 112, 1) + pltpu.roll(d * s2, 16, 1)

    def norm_bwd(t, g, dy):
        rinv = _head_stats(t, lo)
        n = t * rinv
        dn = dy * g
        prod = dn * n
        m_lo = jnp.sum(jnp.where(lo, prod, 0.0), axis=1, keepdims=True)
        m_hi = jnp.sum(jnp.where(lo, 0.0, prod), axis=1, keepdims=True)
        mean = jnp.where(lo, m_lo, m_hi) * (1.0 / HEAD_DIM)
        return rinv * (dn - n * mean), jnp.sum(dy * n, axis=0, keepdims=True)

    dqn = jnp.zeros((1, 128), F32)
    dkn = jnp.zeros((1, 128), F32)
    for j in range(12):
        if j < 4:
            d, dg = norm_bwd(qkv_ref[:, j * 128:(j + 1) * 128], qn, rope_bwd(dq_ref[:, j * 128:(j + 1) * 128] * Q_SCALE))
            dqn = dqn + dg
        elif j == COL_KA:
            d, dg = norm_bwd(qkv_ref[:, j * 128:(j + 1) * 128], kn, rope_bwd(dkv_ref[:, 0:128]))
            dkn = dkn + dg
        elif j == COL_VA:
            d = dkv_ref[:, 128:256]
        elif j < 10:
            d = rope_bwd(dq_ref[:, (j - 2) * 128:(j - 1) * 128] * Q_SCALE)
        elif j == COL_KB:
            d = rope_bwd(dkv_ref[:, 256:384])
        else:
            d = dkv_ref[:, 384:512]
        out_ref[:, j * 128:(j + 1) * 128] = d.astype(BF16)
    return dqn, dkn


def _in_fwd(rt, h, gamma, mod, wg, tables, qn, kn, name):
    def body(h_ref, g_ref, mod_ref, w_ref, c_ref, s1_ref, s2_ref, qn_ref, kn_ref, u_ref, qkn_ref, qkvp_ref, qkv_ref, w_scr):
        @pl.when(pl.program_id(0) == 0)
        def _():
            _unpack_in_pieces(w_ref, w_scr)

        u = _norm_mod_val(h_ref[...], g_ref[...], mod_ref, 0, 1).astype(BF16)
        u_ref[...] = u
        qkv_ref[...] = jnp.dot(u, w_scr[...], preferred_element_type=F32)
        qkn_ref[...] = qkv_ref[:, 0:NORMED_COLS]
        _prep_fwd_body(rt.tm, qkv_ref, c_ref[...], s1_ref[...], s2_ref[...], qn_ref[...], kn_ref[...], qkvp_ref)

    return pl.pallas_call(
        body, name=name, grid=(rt.n_tiles,),
        in_specs=[_row_spec(rt, D_MODEL), _vec_spec(D_MODEL), _mod_spec(rt), _gathered_spec(wg, "in")]
        + [_row_spec(rt, 128)] * 3 + [_vec_spec(128)] * 2,
        out_specs=[_row_spec(rt, D_MODEL), _row_spec(rt, NORMED_COLS), _row_spec(rt, IN_COLS)],
        out_shape=[jax.ShapeDtypeStruct((rt.rows, D_MODEL), BF16), jax.ShapeDtypeStruct((rt.rows, NORMED_COLS), F32),
                   jax.ShapeDtypeStruct((rt.rows, IN_COLS), BF16)],
        scratch_shapes=[pltpu.VMEM((rt.tm, IN_COLS), F32), pltpu.VMEM((D_MODEL, IN_COLS), BF16)],
        compiler_params=_params(("arbitrary",)),
    )(h, gamma, mod, wg["in"][0], *tables, qn, kn)


def _in_bwd(rt, dq, dkv, qkv, tables, qn, kn, wg, h, dres, mod, gamma, latent_only, name, comm=None):
    last = rt.n_lat_tiles - 1

    def body(dq_ref, dkv_ref, qkv_ref, c_ref, s1_ref, s2_ref, qn_ref, kn_ref, w_ref, h_ref, dres_ref, mod_ref, g_ref,
             dqkv_ref, dh_ref, dqn_ref, dkn_ref, dsh_ref, dsc_ref, dg_ref, w_scr):
        i = pl.program_id(0)

        @pl.when(i == 0)
        def _():
            _unpack_in_pieces(w_ref, w_scr)

        dqn, dkn = _prep_bwd_body(rt.tm, dq_ref, dkv_ref, qkv_ref, c_ref[...], s1_ref[...], s2_ref[...], qn_ref[...], kn_ref[...], dqkv_ref)
        du = lax.dot_general(dqkv_ref[...], w_scr[...], NT, preferred_element_type=F32)
        dh, dsh, dsc, dg = _norm_mod_bwd_val(du, h_ref[...], g_ref[...], 1.0 + mod_ref[0, 1:2, :])
        if latent_only:
            @pl.when(i <= last)
            def _():
                dh_ref[...] = dres_ref[...] + dh
        else:
            dh_ref[...] = dres_ref[...] + dh
        _accumulate(rt, i, [(dsh_ref, dsh), (dsc_ref, dsc)], [(dg_ref, dg), (dqn_ref, dqn), (dkn_ref, dkn)])

    dh_spec = pl.BlockSpec((rt.tm, D_MODEL), lambda i: (jnp.minimum(i, last), 0)) if latent_only else _row_spec(rt, D_MODEL)
    return _comm_call(
        body, comm, name=name, grid=(rt.n_tiles,),
        in_specs=[_row_spec(rt, 1024), _row_spec(rt, 512), _row_spec(rt, NORMED_COLS)] + [_row_spec(rt, 128)] * 3 + [_vec_spec(128)] * 2
        + [_gathered_spec(wg, "in"), _row_spec(rt, D_MODEL), _row_spec(rt, D_MODEL), _mod_spec(rt), _vec_spec(D_MODEL)],
        out_specs=[_row_spec(rt, IN_COLS), dh_spec, _vec_spec(128), _vec_spec(128),
                   _group_spec(rt), _group_spec(rt), _vec_spec(D_MODEL)],
        out_shape=[jax.ShapeDtypeStruct((rt.rows, IN_COLS), BF16),
                   jax.ShapeDtypeStruct((rt.n_lat if latent_only else rt.rows, D_MODEL), F32),
                   _vec_shape(128), _vec_shape(128), _group_shape(rt), _group_shape(rt), _vec_shape()],
        args=[dq, dkv, qkv, *tables, qn, kn, wg["in"][0], h, dres, mod, gamma], aliases={}, semantics=("arbitrary",),
        scratch=[pltpu.VMEM((D_MODEL, IN_COLS), BF16)])


def _out_fwd(rt, o, wg, h, mod, g_post_mix, g_pre_mlp, name):
    def body(o_ref, w_ref, h_ref, mod_ref, gpost_ref, gpre_ref, mix_ref, h1_ref, u2_ref):
        mix = jnp.dot(o_ref[...], w_ref[...].reshape(D_MODEL, D_MODEL), preferred_element_type=F32)
        mix_ref[...] = mix
        h1 = _post_norm_val(h_ref[...], mix, gpost_ref[...], mod_ref, 2)
        h1_ref[...] = h1
        u2_ref[...] = _norm_mod_val(h1, gpre_ref[...], mod_ref, 3, 4).astype(BF16)

    return pl.pallas_call(
        body, name=name, grid=(rt.n_tiles,),
        in_specs=[_row_spec(rt, D_MODEL), _gathered_spec(wg, "out"), _row_spec(rt, D_MODEL), _mod_spec(rt),
                  _vec_spec(D_MODEL), _vec_spec(D_MODEL)],
        out_specs=[_row_spec(rt, D_MODEL)] * 3,
        out_shape=[jax.ShapeDtypeStruct((rt.rows, D_MODEL), F32), jax.ShapeDtypeStruct((rt.rows, D_MODEL), F32),
                   jax.ShapeDtypeStruct((rt.rows, D_MODEL), BF16)],
        compiler_params=_params(("parallel",)),
    )(o, wg["out"][0], h, mod, g_post_mix, g_pre_mlp)


def _out_bwd(rt, dh1, mix, wg, mod, g_post_mix, name, comm=None):
    def body(dh_ref, mix_ref, w_ref, mod_ref, g_ref, dmix_ref, do_ref, dgate_ref, dg_ref):
        i = pl.program_id(0)
        dz, dgate, dg = _post_norm_bwd_val(dh_ref[...], mix_ref[...], g_ref[...], mod_ref[0, 2:3, :])
        dzb = dz.astype(BF16)
        dmix_ref[...] = dzb
        do_ref[...] = lax.dot_general(dzb, w_ref[...].reshape(D_MODEL, D_MODEL), NT, preferred_element_type=F32).astype(BF16)
        _accumulate(rt, i, [(dgate_ref, dgate)], [(dg_ref, dg)])

    return _comm_call(
        body, comm, name=name, grid=(rt.n_tiles,),
        in_specs=[_row_spec(rt, D_MODEL), _row_spec(rt, D_MODEL), _gathered_spec(wg, "out"), _mod_spec(rt), _vec_spec(D_MODEL)],
        out_specs=[_row_spec(rt, D_MODEL), _row_spec(rt, D_MODEL), _group_spec(rt), _vec_spec(D_MODEL)],
        out_shape=[jax.ShapeDtypeStruct((rt.rows, D_MODEL), BF16), jax.ShapeDtypeStruct((rt.rows, D_MODEL), BF16),
                   _group_shape(rt), _vec_shape()],
        args=[dh1, mix, wg["out"][0], mod, g_post_mix], aliases={}, semantics=("arbitrary",))


def _w_chunk(w_ref, k):
    return w_ref[2 * k:2 * k + 2].reshape(1024, 1024)


def _mlp_fwd(rt, u2, h1, wg, mod, g_post_mlp, name, comm=None):
    def body(u2_ref, h1_ref, wu_ref, wd_ref, mod_ref, g_ref, ra_ref, y_ref, h2_ref):
        u2_ = u2_ref[...]
        y = jnp.zeros((rt.tm, D_MODEL), F32)
        for k in range(D_FF // 1024):
            a = jnp.maximum(jnp.dot(u2_, _w_chunk(wu_ref, k), preferred_element_type=F32), 0.0)
            ra_ref[:, k * 1024:(k + 1) * 1024] = a.astype(BF16)
            y = y + jnp.dot((a * a).astype(BF16), _w_chunk(wd_ref, k), preferred_element_type=F32)
        y_ref[...] = y
        h2_ref[...] = _post_norm_val(h1_ref[...], y, g_ref[...], mod_ref, 5)

    return _comm_call(
        body, comm, name=name, grid=(rt.n_tiles,),
        in_specs=[_row_spec(rt, D_MODEL), _row_spec(rt, D_MODEL), _gathered_spec(wg, "up"), _gathered_spec(wg, "down"),
                  _mod_spec(rt), _vec_spec(D_MODEL)],
        out_specs=[_row_spec(rt, D_FF), _row_spec(rt, D_MODEL), _row_spec(rt, D_MODEL)],
        out_shape=[jax.ShapeDtypeStruct((rt.rows, D_FF), BF16), jax.ShapeDtypeStruct((rt.rows, D_MODEL), F32),
                   jax.ShapeDtypeStruct((rt.rows, D_MODEL), F32)],
        args=[u2, h1, wg["up"][0], wg["down"][0], mod, g_post_mlp], aliases={}, semantics=("parallel",))


def _mlp_down_bwd(rt, dh, y, ra, wg, mod, g_post_mlp, name, comm=None):
    def body(dh_ref, y_ref, ra_ref, wd_ref, mod_ref, g_ref, dy_ref, da_ref, dgate_ref, dg_ref):
        i = pl.program_id(0)
        dz, dgate, dg = _post_norm_bwd_val(dh_ref[...], y_ref[...], g_ref[...], mod_ref[0, 5:6, :])
        dyb = dz.astype(BF16)
        dy_ref[...] = dyb
        for k in range(D_FF // 1024):
            dr = lax.dot_general(dyb, _w_chunk(wd_ref, k), NT, preferred_element_type=F32)
            da_ref[:, k * 1024:(k + 1) * 1024] = (dr * (2.0 * ra_ref[:, k * 1024:(k + 1) * 1024].astype(F32))).astype(BF16)
        _accumulate(rt, i, [(dgate_ref, dgate)], [(dg_ref, dg)])

    return _comm_call(
        body, comm, name=name, grid=(rt.n_tiles,),
        in_specs=[_row_spec(rt, D_MODEL), _row_spec(rt, D_MODEL), _row_spec(rt, D_FF), _gathered_spec(wg, "down"),
                  _mod_spec(rt), _vec_spec(D_MODEL)],
        out_specs=[_row_spec(rt, D_MODEL), _row_spec(rt, D_FF), _group_spec(rt), _vec_spec(D_MODEL)],
        out_shape=[jax.ShapeDtypeStruct((rt.rows, D_MODEL), BF16), jax.ShapeDtypeStruct((rt.rows, D_FF), BF16),
                   _group_shape(rt), _vec_shape()],
        args=[dh, y, ra, wg["down"][0], mod, g_post_mlp], aliases={}, semantics=("arbitrary",))


def _mlp_up_bwd(rt, da, wg, h1, dh, mod, g_pre_mlp, name):
    def body(da_ref, wu_ref, h1_ref, dh_ref, mod_ref, g_ref, dh1_ref, dsh_ref, dsc_ref, dg_ref):
        i = pl.program_id(0)
        du = jnp.zeros((rt.tm, D_MODEL), F32)
        for k in range(D_FF // 1024):
            du = du + lax.dot_general(da_ref[:, k * 1024:(k + 1) * 1024], _w_chunk(wu_ref, k), NT, preferred_element_type=F32)
        d, dsh, dsc, dg = _norm_mod_bwd_val(du, h1_ref[...], g_ref[...], 1.0 + mod_ref[0, 4:5, :])
        dh1_ref[...] = dh_ref[...] + d
        _accumulate(rt, i, [(dsh_ref, dsh), (dsc_ref, dsc)], [(dg_ref, dg)])

    return pl.pallas_call(
        body, name=name, grid=(rt.n_tiles,),
        in_specs=[_row_spec(rt, D_FF), _gathered_spec(wg, "up"), _row_spec(rt, D_MODEL), _row_spec(rt, D_MODEL),
                  _mod_spec(rt), _vec_spec(D_MODEL)],
        out_specs=[_row_spec(rt, D_MODEL), _group_spec(rt), _group_spec(rt), _vec_spec(D_MODEL)],
        out_shape=[jax.ShapeDtypeStruct((rt.rows, D_MODEL), F32), _group_shape(rt), _group_shape(rt), _vec_shape()],
        compiler_params=_params(("arbitrary",)),
    )(da, wg["up"][0], h1, dh, mod, g_pre_mlp)


def _wgrad_packed(rt, a, b, kind, off, n_rows, p_prev, name):
    h = PACK_HEIGHT[kind]
    tk = rt.tm
    assert off % h == 0, (kind, off)

    def body(a_ref, b_ref, *rest):
        o_ref = rest[-1]
        i = pl.program_id(0)

        @pl.when(i == 0)
        def _():
            o_ref[...] = jnp.zeros_like(o_ref)

        if kind == "in":
            res = lax.dot_general(a_ref[...], b_ref[...], TN, preferred_element_type=F32)
            for k in range(4):
                for c in range(2):
                    for t in range(2):
                        o_ref[c, k, :, t * IN_PIECE_COLS:(t + 1) * IN_PIECE_COLS] += \
                            res[c * 512 + t * h:c * 512 + (t + 1) * h, k * IN_PIECE_COLS:(k + 1) * IN_PIECE_COLS]
        elif kind == "out":
            res = lax.dot_general(a_ref[...], b_ref[...], TN, preferred_element_type=F32)
            for k in range(4):
                for c in range(2):
                    o_ref[c, k] += res[(2 * k + c) * h:(2 * k + c + 1) * h]
        else:
            for k in range(4):
                if kind == "up":
                    res = lax.dot_general(a_ref[...], b_ref[:, k * 1024:(k + 1) * 1024], TN, preferred_element_type=F32)
                else:
                    ra = a_ref[:, k * 1024:(k + 1) * 1024].astype(F32)
                    res = lax.dot_general((ra * ra).astype(BF16), b_ref[...], TN, preferred_element_type=F32)
                o_ref[0, k] += res[0:h]
                o_ref[1, k] += res[h:2 * h]

    in_specs = [pl.BlockSpec((tk, a.shape[1]), lambda i: (i, 0)), pl.BlockSpec((tk, b.shape[1]), lambda i: (i, 0))]
    args = [a, b]
    aliases = {}
    if p_prev is not None:
        in_specs.append(pl.BlockSpec(memory_space=pl.ANY))
        args.append(p_prev)
        aliases = {2: 0}
    return pl.pallas_call(
        body, name=name, grid=(rt.rows // tk,),
        in_specs=in_specs,
        out_specs=pl.BlockSpec((2, 4, h, 1024), lambda i: (0, 0, off // h, 0)),
        out_shape=jax.ShapeDtypeStruct((2, 4, n_rows, 1024), F32),
        input_output_aliases=aliases,
        compiler_params=_params(("arbitrary",)),
    )(*args)


def _ada_wgrad(xs, dm, name):
    depth, _, cols = dm.shape

    def body(x_ref, d_ref, o_ref):
        for l in range(depth):
            o_ref[l] = lax.dot_general(x_ref[...], d_ref[l], TN, preferred_element_type=F32)

    return pl.pallas_call(body, name=name, out_shape=jax.ShapeDtypeStruct((depth, xs.shape[1], cols), F32),
                          compiler_params=pltpu.CompilerParams(vmem_limit_bytes=VMEM_LIMIT))(xs, dm)


def _loss_grad(rt, h, target, name):
    last = rt.n_lat_tiles - 1

    def body(h_ref, t_ref, dh_ref, sq_ref):
        i = pl.program_id(0)

        @pl.when(i == 0)
        def _():
            sq_ref[...] = jnp.zeros_like(sq_ref)

        @pl.when(i <= last)
        def _():
            e = h_ref[...] - t_ref[...]
            dh_ref[...] = e * (1.0 / D_MODEL)
            sq_ref[...] += jnp.sum(e * e, axis=0, keepdims=True)

        @pl.when(i > last)
        def _():
            dh_ref[...] = jnp.zeros_like(dh_ref)

    return pl.pallas_call(
        body, name=name, grid=(rt.n_tiles,),
        in_specs=[_row_spec(rt, D_MODEL), pl.BlockSpec((rt.tm, D_MODEL), lambda i: (jnp.minimum(i, last), 0))],
        out_specs=[_row_spec(rt, D_MODEL), _vec_spec(D_MODEL)],
        out_shape=[jax.ShapeDtypeStruct((rt.rows, D_MODEL), F32), jax.ShapeDtypeStruct((1, D_MODEL), F32)],
        compiler_params=_params(("arbitrary",)),
    )(h, target)


def _stack_heads(x, kvi):
    x = x.astype(F32)
    tq = x.shape[0]
    lane = lax.broadcasted_iota(jnp.int32, (tq, 128), 1)
    keep = lane < HEAD_DIM if kvi == 0 else lane >= HEAD_DIM
    parts = []
    for p in range(2):
        pair = x[:, p * 128:(p + 1) * 128]
        swapped = pltpu.roll(pair, HEAD_DIM, 1)
        lo_head, hi_head = (pair, swapped) if kvi == 0 else (swapped, pair)
        parts += [jnp.where(keep, lo_head, 0.0), jnp.where(keep, hi_head, 0.0)]
    return jnp.concatenate(parts, axis=0).astype(BF16)


def _unstack_heads(o4, kvi):
    tq = o4.shape[0] // GROUP
    lane = lax.broadcasted_iota(jnp.int32, (tq, 128), 1)
    outs = []
    for p in range(2):
        r_lo, r_hi = o4[(2 * p) * tq:(2 * p + 1) * tq], o4[(2 * p + 1) * tq:(2 * p + 2) * tq]
        if kvi == 0:
            lo, hi = r_lo, pltpu.roll(r_hi, HEAD_DIM, 1)
        else:
            lo, hi = pltpu.roll(r_lo, HEAD_DIM, 1), r_hi
        outs.append(jnp.where(lane < HEAD_DIM, lo, hi))
    return jnp.concatenate(outs, axis=1)


def _per_head(shape, axis, tq, values):
    head = lax.broadcasted_iota(jnp.int32, shape, axis) // tq
    out = jnp.zeros(shape, F32)
    for g in range(GROUP):
        out = jnp.where(head == g, values[g], out)
    return out


KEY_CHUNK = 512


def _key_chunks(k_ref, v_ref, n, kc=KEY_CHUNK):
    kc = min(kc, n)
    return [(k_ref[c * kc:(c + 1) * kc, :], v_ref[c * kc:(c + 1) * kc, :], None) for c in range(n // kc)]


def _softmax_fwd(qs, chunks, sink_col):
    logits = []
    for k, _, mask in chunks:
        s = lax.dot_general(qs, k, NT, preferred_element_type=F32)
        logits.append(s if mask is None else jnp.where(mask, s, NEG_BIG))
    m = functools.reduce(jnp.maximum, [jnp.max(s, axis=1, keepdims=True) for s in logits])
    if sink_col is not None:
        m = jnp.maximum(m, sink_col)
    l = jnp.zeros_like(m) if sink_col is None else jnp.exp(sink_col - m)
    acc = jnp.zeros((qs.shape[0], 128), F32)
    for s, (_, v, _) in zip(logits, chunks):
        p = jnp.exp(s - m)
        l = l + jnp.sum(p, axis=1, keepdims=True)
        acc = acc + jnp.dot(p.astype(BF16), v, preferred_element_type=F32)
    return acc / l, m + jnp.log(l)


def _to_rows(col):
    return jnp.transpose(jnp.broadcast_to(col, (col.shape[0], 128)))[0:8, :]


def _softmax_bwd(qs, dos, lse_row, delta_row, chunks):
    dq = jnp.zeros((qs.shape[0], 128), F32)
    grads = []
    for k, v, mask in chunks:
        s = lax.dot_general(k, qs, NT, preferred_element_type=F32)
        if mask is not None:
            s = jnp.where(mask, s, NEG_BIG)
        p = jnp.exp(s - lse_row)
        dp = lax.dot_general(v, dos, NT, preferred_element_type=F32)
        ds = (p * (dp - delta_row)).astype(BF16)
        dv = jnp.dot(p.astype(BF16), dos, preferred_element_type=F32)
        dk = jnp.dot(ds, qs, preferred_element_type=F32)
        dq = dq + lax.dot_general(ds, k, TN, preferred_element_type=F32)
        grads.append((dk, dv))
    return dq, grads


def _band(qi, tq, seq):
    span = tq + 2 * WINDOW
    start = pl.multiple_of(jnp.clip(qi * tq - WINDOW, 0, seq - span), 64)
    return start, span


def _band_mask(qi, tq, start, span, query_axis):
    shape = (GROUP * tq, span) if query_axis == 0 else (span, GROUP * tq)
    qpos = qi * tq + lax.broadcasted_iota(jnp.int32, shape, query_axis) % tq
    kpos = start + lax.broadcasted_iota(jnp.int32, shape, 1 - query_axis)
    return jnp.abs(kpos - qpos) <= WINDOW


def _qkv_specs(rt, tq, q_row, ctx_row, with_latent):
    specs = [pl.BlockSpec((tq, 256), functools.partial(lambda b, i, col: (q_row(b, i), col), col=col)) for col in (0, 1, 3, 4)]
    if with_latent:
        specs += [pl.BlockSpec((rt.seq, 128), functools.partial(lambda b, i, col: (b, col), col=col))
                  for col in (COL_KA, COL_VA, COL_KB, COL_VB)]
    specs += [pl.BlockSpec((rt.ctx, 128), functools.partial(lambda b, i, col: (ctx_row(b), col), col=col))
              for col in (COL_KA, COL_VA, COL_KB, COL_VB)]
    return specs


def _attn_fwd(rt, qkvp, sink, o_prev, name, comm=None):
    latent = o_prev is None
    seq, ctx, nb = rt.seq, rt.ctx, rt.nb
    tq = 128 if latent else ctx
    nq = seq // tq if latent else 1
    ctx_blk0 = rt.n_lat // ctx
    q_row = (lambda b, i: b * nq + i) if latent else (lambda b, i: ctx_blk0 + b)

    def body(sink_ref, qa0, qa1, qb0, qb1, *rest):
        if latent:
            kal, val, kbl, vbl, kac, vac, kbc, vbc, o_ref, lse_ref = rest
        else:
            kac, vac, kbc, vbc, _, o_ref, lse_ref = rest
        qi = pl.program_id(1)
        for kvi, (qa, qb) in enumerate(((qa0, qb0), (qa1, qb1))):
            src_a = _key_chunks(kac, vac, ctx)
            src_b = _key_chunks(kbc, vbc, ctx)
            if latent:
                src_a += _key_chunks(kal, val, seq, seq)
                start, span = _band(qi, tq, seq)
                src_b.append((kbl[pl.ds(start, span), :], vbl[pl.ds(start, span), :], _band_mask(qi, tq, start, span, 0)))
            oa, lse = _softmax_fwd(_stack_heads(qa[...], kvi), src_a, None)
            o_ref[:, kvi * 256:(kvi + 1) * 256] = _unstack_heads(oa, kvi).astype(BF16)
            lse_ref[0, kvi] = _to_rows(lse)
            sink_col = _per_head((GROUP * tq, 1), 0, tq, [sink_ref[kvi * GROUP + g] for g in range(GROUP)])
            ob, lse = _softmax_fwd(_stack_heads(qb[...], kvi), src_b, sink_col)
            o_ref[:, 512 + kvi * 256:512 + (kvi + 1) * 256] = _unstack_heads(ob, kvi).astype(BF16)
            lse_ref[0, 2 + kvi] = _to_rows(lse)

    specs = _qkv_specs(rt, tq, q_row, lambda b: ctx_blk0 + b, latent)
    args = [sink] + [qkvp] * len(specs)
    in_specs = [pl.BlockSpec(memory_space=pltpu.SMEM)] + specs
    aliases = {}
    if not latent:
        in_specs.append(pl.BlockSpec(memory_space=pl.ANY))
        args.append(o_prev)
        aliases = {len(args) - 1: 0}
    return _comm_call(
        body, comm, name=name, grid=(nb, nq),
        in_specs=in_specs,
        out_specs=[pl.BlockSpec((tq, 1024), lambda b, i: (q_row(b, i), 0)),
                   pl.BlockSpec((1, 4, 8, GROUP * tq), lambda b, i: (b * nq + i, 0, 0, 0))],
        out_shape=[jax.ShapeDtypeStruct((rt.rows, 1024), BF16), jax.ShapeDtypeStruct((nb * nq, 4, 8, GROUP * tq), F32)],
        args=args, aliases=aliases, semantics=("parallel", "parallel"))


def _attn_bwd(rt, qkvp, o, lse, do, sink, prev, name, comm=None):
    latent = prev is None
    seq, ctx, nb = rt.seq, rt.ctx, rt.nb
    tq = 128 if latent else ctx
    nq = seq // tq if latent else 1
    ctx_blk0 = rt.n_lat // ctx
    q_row = (lambda b, i: b * nq + i) if latent else (lambda b, i: ctx_blk0 + b)
    kc = min(KEY_CHUNK, seq)

    def body(sink_ref, qa0, qa1, qb0, qb1, *rest):
        if latent:
            kal, val, kbl, vbl, kac, vac, kbc, vbc, do_ref, o_ref, lse_ref, dq_ref, dl_ref, dc_ref, dsink_ref = rest
        else:
            kac, vac, kbc, vbc, do_ref, o_ref, lse_ref, c1_ref, _, _, dq_ref, dc_ref, dsink_ref = rest
        b, qi = pl.program_id(0), pl.program_id(1)

        def rows_of(cols, kvi, mixer):
            dos = _stack_heads(do_ref[:, cols], kvi)
            delta = jnp.sum(dos.astype(F32) * _stack_heads(o_ref[:, cols], kvi).astype(F32), axis=1, keepdims=True)
            return dos, lse_ref[0, 2 * mixer + kvi, 0:1, :], _to_rows(delta)[0:1, :]

        @pl.when(jnp.logical_and(b == 0, qi == 0))
        def _():
            dsink_ref[...] = jnp.zeros_like(dsink_ref)

        if latent:
            @pl.when(qi == 0)
            def _():
                dc_ref[...] = jnp.zeros_like(dc_ref)
                dl_ref[...] = jnp.zeros_like(dl_ref)
        else:
            dc_ref[...] = c1_ref[...]

        head_row = lax.broadcasted_iota(jnp.int32, (8, 128), 0)
        for kvi, (qa, qb) in enumerate(((qa0, qb0), (qa1, qb1))):
            cols = slice(kvi * 256, (kvi + 1) * 256)
            dos, lse_row, delta_row = rows_of(cols, kvi, 0)
            src = _key_chunks(kac, vac, ctx)
            if latent:
                src += _key_chunks(kal, val, seq)
            dq4, grads = _softmax_bwd(_stack_heads(qa[...], kvi), dos, lse_row, delta_row, src)
            dq_ref[:, cols] = _unstack_heads(dq4, kvi)
            dc_ref[:, 0:128] += grads[0][0]
            dc_ref[:, 128:256] += grads[0][1]
            for c, (dk, dv) in enumerate(grads[1:]):
                dl_ref[c * kc:(c + 1) * kc, 0:128] += dk
                dl_ref[c * kc:(c + 1) * kc, 128:256] += dv
            cols = slice(512 + kvi * 256, 512 + (kvi + 1) * 256)
            dos, lse_row, delta_row = rows_of(cols, kvi, 1)
            src = _key_chunks(kbc, vbc, ctx)
            if latent:
                start, span = _band(qi, tq, seq)
                src.append((kbl[pl.ds(start, span), :], vbl[pl.ds(start, span), :], _band_mask(qi, tq, start, span, 1)))
            dq4, grads = _softmax_bwd(_stack_heads(qb[...], kvi), dos, lse_row, delta_row, src)
            dq_ref[:, cols] = _unstack_heads(dq4, kvi)
            dc_ref[:, 256:384] += grads[0][0]
            dc_ref[:, 384:512] += grads[0][1]
            if latent:
                dl_ref[pl.ds(start, span), 256:384] += grads[1][0]
                dl_ref[pl.ds(start, span), 384:512] += grads[1][1]
            sink_row = _per_head((1, GROUP * tq), 1, tq, [sink_ref[kvi * GROUP + g] for g in range(GROUP)])
            dsink = -jnp.exp(sink_row - lse_row) * delta_row
            head = lax.broadcasted_iota(jnp.int32, (1, GROUP * tq), 1) // tq
            upd = jnp.zeros((8, 128), F32)
            for g in range(GROUP):
                upd = jnp.where(head_row == kvi * GROUP + g, jnp.sum(jnp.where(head == g, dsink, 0.0)), upd)
            dsink_ref[...] += upd

    specs = _qkv_specs(rt, tq, q_row, lambda b: ctx_blk0 + b, latent)
    q_rows_spec = pl.BlockSpec((tq, 1024), lambda b, i: (q_row(b, i), 0))
    in_specs = ([pl.BlockSpec(memory_space=pltpu.SMEM)] + specs
                + [q_rows_spec, q_rows_spec, pl.BlockSpec((1, 4, 8, GROUP * tq), lambda b, i: (b * nq + i, 0, 0, 0))])
    args = [sink] + [qkvp] * len(specs) + [do, o, lse]
    dq_shape = jax.ShapeDtypeStruct((rt.rows, 1024), F32)
    dkv_shape = jax.ShapeDtypeStruct((rt.rows, 512), F32)
    dsink_spec, dsink_shape = pl.BlockSpec((8, 128), lambda b, i: (0, 0)), jax.ShapeDtypeStruct((8, 128), F32)
    dq_spec = pl.BlockSpec((tq, 1024), lambda b, i: (q_row(b, i), 0))
    if latent:
        out_specs = [dq_spec, pl.BlockSpec((seq, 512), lambda b, i: (b, 0)), pl.BlockSpec((ctx, 512), lambda b, i: (b, 0)), dsink_spec]
        out_shape = [dq_shape, dkv_shape, jax.ShapeDtypeStruct((rt.n_ctx, 512), F32), dsink_shape]
        aliases = {}
    else:
        dq_prev, dkv_prev, c1 = prev
        in_specs += [pl.BlockSpec((ctx, 512), lambda b, i: (b, 0)), pl.BlockSpec(memory_space=pl.ANY), pl.BlockSpec(memory_space=pl.ANY)]
        args += [c1, dq_prev, dkv_prev]
        out_specs = [dq_spec, pl.BlockSpec((ctx, 512), lambda b, i: (ctx_blk0 + b, 0)), dsink_spec]
        out_shape = [dq_shape, dkv_shape, dsink_shape]
        aliases = {len(args) - 2: 0, len(args) - 1: 1}
    return _comm_call(body, comm, name=name, grid=(nb, nq), in_specs=in_specs, out_specs=out_specs, out_shape=out_shape,
                      args=args, aliases=aliases, semantics=("arbitrary", "arbitrary"))


def _silu(x):
    return x / (1.0 + jnp.exp(-x))


def _whole(shape):
    return pl.BlockSpec(shape, lambda i, s: (0,) * len(shape))


def _ada_half_spec(cols):
    return pl.BlockSpec((DEPTH, D_MODEL, cols), lambda i, s: (0, 0, s[0]))


def _ada_fwd(cond, w_ada, b_half, c_idx, name):
    rows = cond.shape[0]
    cols = w_ada.shape[2] // 2

    def body(s_ref, c_ref, w_ref, b_ref, x_ref, o_ref):
        xs = _silu(c_ref[...]).astype(BF16)
        x_ref[...] = xs
        for l in range(DEPTH):
            o_ref[l] = jnp.dot(xs, w_ref[l].astype(BF16), preferred_element_type=F32) + b_ref[l]

    grid_spec = pltpu.PrefetchScalarGridSpec(
        num_scalar_prefetch=1, grid=(1,),
        in_specs=[_whole(cond.shape), _ada_half_spec(cols), _whole(b_half.shape)],
        out_specs=[_whole((rows, D_MODEL)), _whole((DEPTH, rows, cols))])
    return pl.pallas_call(
        body, name=name, grid_spec=grid_spec,
        out_shape=[jax.ShapeDtypeStruct((rows, D_MODEL), BF16), jax.ShapeDtypeStruct((DEPTH, rows, cols), F32)],
        compiler_params=_params(("arbitrary",)),
    )(c_idx, cond, w_ada, b_half)


def _ada_cond_bwd(dcx, w_ada, c_idx, name):
    _, rows, cols = dcx.shape

    def body(s_ref, d_ref, w_ref, o_ref):
        acc = jnp.zeros((rows, D_MODEL), F32)
        for l in range(DEPTH):
            acc = acc + lax.dot_general(d_ref[l], w_ref[l].astype(BF16), NT, preferred_element_type=F32)
        o_ref[...] = acc

    grid_spec = pltpu.PrefetchScalarGridSpec(
        num_scalar_prefetch=1, grid=(1,),
        in_specs=[_whole(dcx.shape), _ada_half_spec(cols)], out_specs=_whole((rows, D_MODEL)))
    return pl.pallas_call(body, name=name, grid_spec=grid_spec, out_shape=jax.ShapeDtypeStruct((rows, D_MODEL), F32),
                          compiler_params=_params(("arbitrary",)))(c_idx, dcx, w_ada)


def _dev_sum(x, name):
    _, r, c = x.shape

    def body(x_ref, o_ref):
        v = x_ref[0]
        for d in range(1, N_DEV):
            v = v + x_ref[d]
        o_ref[...] = v

    return pl.pallas_call(body, name=name, out_shape=jax.ShapeDtypeStruct((r, c), F32))(x)


def _adam_val(w, g, m, v):
    c1 = 1.0 / (1.0 - ADAM_B1 ** ADAM_STEP)
    c2 = 1.0 / (1.0 - ADAM_B2 ** ADAM_STEP)
    nm = ADAM_B1 * m + (1.0 - ADAM_B1) * g
    nv = ADAM_B2 * v + (1.0 - ADAM_B2) * (g * g)
    return -ADAM_LR * ((nm * c1) / (jnp.sqrt(nv * c2) + ADAM_EPS) + ADAM_WD * w), nm, nv


def _small_update(tot, dcc_parts, params, n_groups, name):
    n_p = len(params)
    mod_rows = n_groups * N_MOD

    def body(tot_ref, dcc_ref, *refs):
        ins, outs = refs[:3 * n_p], refs[3 * n_p:]

        def update(p, rows, cols, g):
            w_ref, m_ref, v_ref = ins[3 * p:3 * p + 3]
            g_ref, d_ref, nm_ref, nv_ref = outs[4 * p:4 * p + 4]
            d, nm, nv = _adam_val(w_ref[rows, cols], g, m_ref[rows, cols], v_ref[rows, cols])
            g_ref[rows, cols] = g
            d_ref[rows, cols] = d
            nm_ref[rows, cols] = nm
            nv_ref[rows, cols] = nv

        acc = dcc_ref[0, 0:1, :]
        for d in range(1, N_DEV):
            acc = acc + dcc_ref[d, 0:1, :]
        c = ins[0][...]
        sg = 1.0 / (1.0 + jnp.exp(-c))
        update(0, slice(0, 1), slice(None), acc * (sg * (1.0 + c * (1.0 - sg))))
        for l in range(DEPTH):
            for i in range(N_MOD):
                g = tot_ref[l * mod_rows + i:l * mod_rows + i + 1, :]
                for grp in range(1, n_groups):
                    g = g + tot_ref[l * mod_rows + grp * N_MOD + i:l * mod_rows + grp * N_MOD + i + 1, :]
                update(1, slice(l, l + 1), slice(i * D_MODEL, (i + 1) * D_MODEL), g)
            for j in range(4):
                row = DEPTH * mod_rows + 4 * l + j
                update(2 + j, slice(l, l + 1), slice(None), tot_ref[row:row + 1, :])

    shapes = [jax.ShapeDtypeStruct(w.shape, F32) for w, _, _ in params for _ in range(4)]
    outs = pl.pallas_call(body, name=name, out_shape=shapes)(tot, dcc_parts, *[a for p in params for a in p])
    return [tuple(outs[4 * p:4 * p + 4]) for p in range(n_p)]


def _adamw(w, g, m, v, name):
    r, c = w.shape
    tr = _pick(r, (256, 128, 64, 32, 24, 16, 8))

    def body(w_ref, g_ref, m_ref, v_ref, d_ref, nm_ref, nv_ref):
        d_ref[...], nm_ref[...], nv_ref[...] = _adam_val(w_ref[...], g_ref[...], m_ref[...], v_ref[...])

    spec = pl.BlockSpec((tr, c), lambda i: (i, 0))
    return pl.pallas_call(body, name=name, grid=(r // tr,), in_specs=[spec] * 4, out_specs=[spec] * 3,
                          out_shape=[jax.ShapeDtypeStruct((r, c), F32)] * 3, compiler_params=_params(("parallel",)))(w, g, m, v)


def _local_step(x, ctx, target, mods, gam, qn, kn, sink, w_first, w_layers, packed, c_idx, k_idx):
    nb, seq, _ = x.shape
    rt = _Rows(nb, seq, ctx.shape[1])
    tables = _rope_tables(rt)
    fuse = packed is not None
    h = jnp.concatenate([x.reshape(rt.n_lat, D_MODEL), ctx.reshape(rt.n_ctx, D_MODEL)], axis=0)
    wg = [{}, {}] if fuse else [dict(w) for w in w_layers]
    wg[0]["in"] = (w_first, 0)
    saved = []
    for l in range(DEPTH):
        g_pre_mix, g_post_mix, g_pre_mlp, g_post_mlp = gam[l]
        u, qkv, qkvp = _in_fwd(rt, h, g_pre_mix, mods[l], wg[l], tables, qn[l], kn[l], f"in_fwd{l}")
        if fuse and l == 0:
            o, lse_lat, w_mlp0, w_out0, w_mix1 = _attn_fwd(rt, qkvp, sink[l], None, f"attn_lat_fwd{l}",
                                                          comm=_gather_comm(packed, [W_MLP0, W_OUT0, W_MIX1]))
            wg[0].update({kind: (w_mlp0, PACK_OFF[(kind, 0)] - W_MLP0[0]) for kind in ("up", "down")})
            wg[0]["out"] = (w_out0, 0)
            wg[1] = {kind: (w_mix1, PACK_OFF[(kind, 1)] - W_MIX1[0]) for kind in ("out", "in")}
        else:
            o, lse_lat = _attn_fwd(rt, qkvp, sink[l], None, f"attn_lat_fwd{l}")
        o, lse_ctx = _attn_fwd(rt, qkvp, sink[l], o, f"attn_ctx_fwd{l}")
        mix, h1, u2 = _out_fwd(rt, o, wg[l], h, mods[l], g_post_mix, g_pre_mlp, f"out_fwd{l}")
        if fuse and l == 0:
            r, y, h2, w_mlp1 = _mlp_fwd(rt, u2, h1, wg[l], mods[l], g_post_mlp, f"mlp_fwd{l}", comm=_gather_comm(packed, [W_MLP1]))
            wg[1].update({kind: (w_mlp1, PACK_OFF[(kind, 1)] - W_MLP1[0]) for kind in ("up", "down")})
        else:
            r, y, h2 = _mlp_fwd(rt, u2, h1, wg[l], mods[l], g_post_mlp, f"mlp_fwd{l}")
        saved.append((h, u, qkv, qkvp, o, lse_lat, lse_ctx, mix, h1, u2, r, y))
        h = h2

    dh, sq = _loss_grad(rt, h, target.reshape(rt.n_lat, D_MODEL), "loss_grad")

    small = [None] * DEPTH
    groups = {}
    for l in reversed(range(DEPTH)):
        g_pre_mix, g_post_mix, g_pre_mlp, g_post_mlp = gam[l]
        h0, u, qkv, qkvp, o, lse_lat, lse_ctx, mix, h1, u2, r, y = saved[l]
        mlp_group, mix_group = (G_LAYER1, G_LAYER1) if l == 1 else (G_MLP0, G_MIX0)
        hide = fuse and l == 0

        outs = _mlp_down_bwd(rt, dh, y, r, wg[l], mods[l], g_post_mlp, f"mlp_down_bwd{l}",
                             comm=_pair_comm(groups[G_LAYER1]) if hide else None)
        dy, da, d_gate_m, d_g_post_mlp = outs[:4]
        if hide:
            sum1 = _pair_sum(groups[G_LAYER1], outs[4], c_idx, "grad_pair_sum_layer1")
        p_mlp = _wgrad_packed(rt, r, dy, "down", PACK_OFF[("down", l)] - mlp_group[0], mlp_group[1], None, f"mlp_down_wgrad{l}")
        dh1, d_sh_m, d_sc_m, d_g_pre_mlp = _mlp_up_bwd(rt, da, wg[l], h1, dh, mods[l], g_pre_mlp, f"mlp_up_bwd{l}")
        p_mlp = _wgrad_packed(rt, u2, da, "up", PACK_OFF[("up", l)] - mlp_group[0], mlp_group[1], p_mlp, f"mlp_up_wgrad{l}")
        outs = _out_bwd(rt, dh1, mix, wg[l], mods[l], g_post_mix, f"out_bwd{l}", comm=_pair_comm(p_mlp) if hide else None)
        dmix, do, d_gate_a, d_g_post_mix = outs[:4]
        if hide:
            sum0 = _pair_sum(p_mlp, outs[4], c_idx, "grad_pair_sum_mlp0")
        p_mix = _wgrad_packed(rt, o, dmix, "out", PACK_OFF[("out", l)] - mix_group[0], mix_group[1],
                              p_mlp if l == 1 else None, f"out_wgrad{l}")
        outs = _attn_bwd(rt, qkvp, o, lse_lat, do, sink[l], None, f"attn_lat_bwd{l}",
                         comm=_chip_comm([sum1[1], sum0[1]]) if hide else None)
        dq, dkv, dkv_c, dsink1 = outs[:4]
        if hide:
            groups[G_LAYER1] = _owner_sum(sum1[0], outs[4], k_idx, "grad_owner_sum_layer1")
            groups[G_MLP0] = _owner_sum(sum0[0], outs[5], k_idx, "grad_owner_sum_mlp0")
        dq, dkv, dsink2 = _attn_bwd(rt, qkvp, o, lse_ctx, do, sink[l], (dq, dkv, dkv_c), f"attn_ctx_bwd{l}")
        dqkv, dh, dqn, dkn, d_sh_a, d_sc_a, d_g_pre_mix = _in_bwd(rt, dq, dkv, qkv, tables, qn[l], kn[l], wg[l], h0, dh1, mods[l],
                                                                  g_pre_mix, l == 0, f"in_bwd{l}")
        groups[mix_group] = _wgrad_packed(rt, u, dqkv, "in", PACK_OFF[("in", l)] - mix_group[0], mix_group[1], p_mix, f"in_wgrad{l}")
        if not hide and l == 0:
            groups[G_MLP0] = p_mlp
        dmod = jnp.concatenate([d_sh_a, d_sc_a, d_gate_a, d_sh_m, d_sc_m, d_gate_m], axis=1)
        small[l] = dict(mod=dmod, gammas=jnp.concatenate([d_g_pre_mix, d_g_post_mix, d_g_pre_mlp, d_g_post_mlp], axis=0),
                        q_norm=dqn, k_norm=dkn, sink=(dsink1 + dsink2)[:, 0])
    return sq, dh.reshape(nb, seq, D_MODEL), [groups[G_LAYER1], groups[G_MLP0], groups[G_MIX0]], small


SMALL_ROWS = 48


def kernel(x, c, ctx, c_ctx, w_ada, b_ada, g_pre_mix, g_post_mix, g_pre_mlp, g_post_mlp, w_in, q_norm, k_norm, sink, w_out, w_up, w_down, loss_target, m_c_ctx, m_w_ada, m_b_ada, m_g_pre_mix, m_g_post_mix, m_g_pre_mlp, m_g_post_mlp, m_w_in, m_q_norm, m_k_norm, m_sink, m_w_out, m_w_up, m_w_down, v_c_ctx, v_w_ada, v_b_ada, v_g_pre_mix, v_g_post_mix, v_g_pre_mlp, v_g_post_mlp, v_w_in, v_q_norm, v_k_norm, v_sink, v_w_out, v_w_up, v_w_down):
    nb = x.shape[0]
    ix, iy, ic = lax.axis_index("x"), lax.axis_index("y"), lax.axis_index("c")
    chip = 2 * ix + iy
    dev = 2 * chip + ic
    ada_cols = w_ada.shape[2] // 2

    c_all = _all_gather(c.reshape(8, (nb * D_MODEL) // 8), "gather_c", False).reshape(N_DEV * nb, D_MODEL)
    n_cond = N_DEV * nb + 1
    cond_rows = 16 * ((n_cond + 15) // 16)
    cond = jnp.concatenate([c_all, c_ctx[None, :], jnp.zeros((cond_rows - n_cond, D_MODEL), F32)], axis=0)
    c_idx = ic.reshape(1).astype(jnp.int32)
    kc_idx = jnp.stack([chip, ic]).astype(jnp.int32)
    b_ada_half = lax.dynamic_slice_in_dim(b_ada, dev * ada_cols, ada_cols, 1)[:, None, :]
    x_ada, mod_part = _ada_fwd(cond, w_ada, b_ada_half, c_idx, "ada_fwd")
    mod_g = _all_gather(mod_part.reshape(DEPTH * cond_rows, ada_cols), "gather_mod", False)
    mod_all = mod_g.reshape(N_DEV, DEPTH, cond_rows, ada_cols).transpose(1, 2, 0, 3).reshape(DEPTH, cond_rows, N_MOD * D_MODEL)
    mods = []
    for l in range(DEPTH):
        mine = lax.dynamic_slice_in_dim(mod_all[l], dev * nb, nb, 0)
        mods.append(jnp.concatenate([mine, mod_all[l, n_cond - 1:n_cond]], axis=0).reshape(nb + 1, N_MOD, D_MODEL))

    packed = _pack_local_half(w_in, w_out, w_up, w_down, ic)
    w_first, = _comm_alone(_gather_comm(packed, [W_FIRST]), "gather_w_first")

    gam = [(g_pre_mix[l][None], g_post_mix[l][None], g_pre_mlp[l][None], g_post_mlp[l][None]) for l in range(DEPTH)]
    qn = [jnp.tile(q_norm[l], 2)[None] for l in range(DEPTH)]
    kn = [jnp.tile(k_norm[l], 2)[None] for l in range(DEPTH)]
    sq, grad_x, (h_layer1, h_mlp0, p_mix0), lg = _local_step(x, ctx, loss_target, mods, gam, qn, kn, [sink[l] for l in range(DEPTH)],
                                                           w_first, None, packed, c_idx, kc_idx)

    def step(w, g, m, v, name):
        shape = w.shape
        cols = shape[-1]
        outs = _adamw(w.reshape(-1, cols), g.reshape(-1, cols), m.reshape(-1, cols), v.reshape(-1, cols), name)
        return tuple(a.reshape(shape) for a in outs)

    def piece(halves, kind, l, group):
        o = PACK_OFF[(kind, l)] - group[0]
        rows = halves[:, o:o + PACK_HEIGHT[kind]]
        if kind == "in":
            rows = rows[:, :, :2 * IN_PIECE_COLS].reshape(2, 256, 2, IN_PIECE_COLS).transpose(0, 2, 1, 3)
            return rows.reshape(1024, IN_PIECE_COLS)
        return rows.reshape(2 * PACK_HEIGHT[kind], 1024)

    r1, = _comm_alone(_pair_comm(p_mix0), "grad_pair_exchange_mix0")
    a32, a16 = _pair_sum(p_mix0, r1, c_idx, "grad_pair_sum_mix0")
    r2, = _comm_alone(_chip_comm([a16]), "grad_chip_exchange_mix0")
    h_mix0 = _owner_sum(a32, r2, kc_idx, "grad_owner_sum_mix0")
    h_layer1, h_mlp0, h_mix0 = _comm_alone(_halves_comm([h_layer1, h_mlp0, h_mix0]), "grad_halves_exchange")
    grad_w_up = jnp.stack([piece(h_mlp0, "up", 0, G_MLP0), piece(h_layer1, "up", 1, G_LAYER1)])
    grad_w_down = jnp.stack([piece(h_mlp0, "down", 0, G_MLP0), piece(h_layer1, "down", 1, G_LAYER1)])
    grad_w_in = jnp.stack([piece(h_mix0, "in", 0, G_MIX0), piece(h_layer1, "in", 1, G_LAYER1)])
    grad_w_out = jnp.stack([piece(h_mix0, "out", 0, G_MIX0), piece(h_layer1, "out", 1, G_LAYER1)])

    def lane_pad(v):
        return jnp.pad(v, (0, D_MODEL - v.shape[0]))[None]

    head_rows = [lane_pad(jnp.concatenate([lg[l]["q_norm"][0], lg[l]["k_norm"][0], lg[l]["sink"]])) for l in range(DEPTH)]
    loss_row = lane_pad((0.5 / D_MODEL) * jnp.sum(sq, keepdims=True)[0])
    small = jnp.concatenate([lg[l]["mod"].reshape((nb + 1) * N_MOD, D_MODEL) for l in range(DEPTH)]
                            + [lg[l]["gammas"] for l in range(DEPTH)] + head_rows + [loss_row], axis=0)
    small = jnp.pad(small, ((0, SMALL_ROWS - small.shape[0]), (0, 0)))
    small_g = _all_gather(small, "gather_small", False).reshape(N_DEV, SMALL_ROWS, D_MODEL)
    tot = _dev_sum(small_g, "small_sum")
    mod_rows = (nb + 1) * N_MOD
    o_head = DEPTH * mod_rows + 4 * DEPTH
    loss = tot[o_head + DEPTH, 0]
    grad_q_norm = tot[o_head:o_head + DEPTH, 0:64] + tot[o_head:o_head + DEPTH, 64:128]
    grad_k_norm = tot[o_head:o_head + DEPTH, 128:192] + tot[o_head:o_head + DEPTH, 192:256]
    grad_sink = tot[o_head:o_head + DEPTH, 256:264]

    ex = small_g[:, :DEPTH * mod_rows].reshape(N_DEV, DEPTH, nb + 1, N_MOD * D_MODEL)[:, :, :nb]
    ex = ex.transpose(1, 0, 2, 3).reshape(DEPTH, N_DEV * nb, N_MOD * D_MODEL)
    cx = tot[:DEPTH * mod_rows].reshape(DEPTH, nb + 1, N_MOD * D_MODEL)[:, nb:]
    dm = jnp.concatenate([ex, cx, jnp.zeros((DEPTH, cond_rows - n_cond, N_MOD * D_MODEL), F32)], axis=1)
    shard_cols = w_ada.shape[2]
    grad_w_ada = _ada_wgrad(x_ada, lax.dynamic_slice_in_dim(dm, chip * shard_cols, shard_cols, 2).astype(BF16), "ada_wgrad")
    dcx = jnp.pad(lax.dynamic_slice_in_dim(cx, dev * ada_cols, ada_cols, 2), ((0, 0), (0, 15), (0, 0))).astype(BF16)
    dcc = _ada_cond_bwd(dcx, w_ada, c_idx, "ada_cond_bwd")[0:8]
    dcc_g = _all_gather(dcc, "gather_cond_grad", False).reshape(N_DEV, 8, D_MODEL)

    dense_names = ["c_ctx", "b_ada", "g_pre_mix", "g_post_mix", "g_pre_mlp", "g_post_mlp"]
    dense = _small_update(tot, dcc_g, [(c_ctx[None], m_c_ctx[None], v_c_ctx[None]), (b_ada, m_b_ada, v_b_ada),
                                       (g_pre_mix, m_g_pre_mix, v_g_pre_mix), (g_post_mix, m_g_post_mix, v_g_post_mix),
                                       (g_pre_mlp, m_g_pre_mlp, v_g_pre_mlp), (g_post_mlp, m_g_post_mlp, v_g_post_mlp)],
                          nb + 1, "small_update")
    res = {n: r for n, r in zip(dense_names, dense)}
    res["c_ctx"] = tuple(a[0] for a in res["c_ctx"])
    small_names = ["q_norm", "k_norm", "sink"]
    small_w = [q_norm, k_norm, sink]
    small_gr = [grad_q_norm, grad_k_norm, grad_sink]
    small_m = [m_q_norm, m_k_norm, m_sink]
    small_v = [v_q_norm, v_k_norm, v_sink]
    sizes = [int(np.prod(w.shape)) for w in small_w]
    total = sum(sizes)
    flat_rows = 8 * ((total + 8 * D_MODEL - 1) // (8 * D_MODEL))

    def flat(arrs, fill):
        f = jnp.concatenate([a.reshape(-1) for a in arrs])
        return jnp.concatenate([f, jnp.full((flat_rows * D_MODEL - total,), fill, F32)]).reshape(flat_rows, D_MODEL)

    sd, snm, snv = _adamw(flat(small_w, 0.0), flat(small_gr, 0.0), flat(small_m, 0.0), flat(small_v, 1.0), "adamw_small")[:3]

    def unflat(f):
        f = f.reshape(-1)
        out, off = [], 0
        for w, n in zip(small_w, sizes):
            out.append(f[off:off + n].reshape(w.shape))
            off += n
        return out

    small_d, small_nm, small_nv = unflat(sd), unflat(snm), unflat(snv)
    res.update({n: (g, d, nm, nv) for n, g, d, nm, nv in zip(small_names, small_gr, small_d, small_nm, small_nv)})
    res["w_ada"] = (grad_w_ada, *step(w_ada, grad_w_ada, m_w_ada, v_w_ada, "adamw_w_ada"))
    res["w_in"] = (grad_w_in, *step(w_in, grad_w_in, m_w_in, v_w_in, "adamw_w_in"))
    res["w_out"] = (grad_w_out, *step(w_out, grad_w_out, m_w_out, v_w_out, "adamw_w_out"))
    res["w_up"] = (grad_w_up, *step(w_up, grad_w_up, m_w_up, v_w_up, "adamw_w_up"))
    res["w_down"] = (grad_w_down, *step(w_down, grad_w_down, m_w_down, v_w_down, "adamw_w_down"))

    order = ["c_ctx", "w_ada", "b_ada", "g_pre_mix", "g_post_mix", "g_pre_mlp", "g_post_mlp", "w_in", "q_norm", "k_norm", "sink", "w_out", "w_up", "w_down"]
    return (loss, grad_x, *[res[n][0] for n in order], *[res[n][1] for n in order],
            *[res[n][2] for n in order], *[res[n][3] for n in order])
```

```python
import functools

import jax
import jax.numpy as jnp
import numpy as np
from jax import lax
from jax.experimental import pallas as pl
from jax.experimental.pallas import tpu as pltpu

F32 = jnp.float32
BF16 = jnp.bfloat16

D_MODEL = 1024
HEAD_DIM = 64
GROUP = 4
WINDOW = 128
N_MOD = 6
D_FF = 4 * D_MODEL
IN_COLS = 1536
GRID_W = 64
ROPE_THETA = 10000.0
EPS = 1e-6
NEG_BIG = -1e30
Q_SCALE = HEAD_DIM ** -0.5
DEPTH = 2
N_DEV = 8

ADAM_LR = 0.001
ADAM_B1 = 0.9
ADAM_B2 = 0.999
ADAM_EPS = 1e-08
ADAM_WD = 0.01
ADAM_STEP = 10

V7X_VMEM_BYTES = 64 * 1024 * 1024
VMEM_LIMIT = V7X_VMEM_BYTES - 8 * 1024 * 1024

MESH = pl.DeviceIdType.MESH
NT = (((1,), (1,)), ((), ()))
TN = (((0,), (0,)), ((), ()))

COL_KA, COL_VA, COL_KB, COL_VB = 4, 5, 10, 11
NORMED_COLS = 640

PACK_HEIGHT = {"up": 512, "down": 512, "in": 256, "out": 128}
IN_PIECE_COLS = 384
PACK_OFF = {("up", 0): 0, ("down", 0): 512, ("in", 0): 1024, ("out", 0): 1280,
            ("up", 1): 1408, ("down", 1): 1920, ("in", 1): 2432, ("out", 1): 2688}
PACK_ROWS = 2816
W_FIRST, W_MLP0, W_OUT0, W_MLP1, W_MIX1 = (1024, 256), (0, 1024), (1280, 128), (1408, 1024), (2432, 384)
G_LAYER1, G_MLP0, G_MIX0 = (1408, 1408), (0, 1024), (1024, 384)


def _pick(n, cands):
    for t in cands:
        if n % t == 0:
            return t
    raise ValueError(f"no tile for {n}")


def _params(sem):
    return pltpu.CompilerParams(dimension_semantics=sem, vmem_limit_bytes=VMEM_LIMIT)


def _all_gather(x, name, in_hbm):
    m_per, n = x.shape
    space = pl.ANY if in_hbm else pltpu.VMEM

    def body(x_ref, out_ref, send_sems, recv_sems, local_sem):
        x_, y_, c_ = lax.axis_index("x"), lax.axis_index("y"), lax.axis_index("c")
        me, sibling = (x_, y_, c_), (x_, y_, 1 - c_)
        chips = [(1 - x_, y_), (x_, 1 - y_), (1 - x_, 1 - y_)]

        def rows(px, py, pc):
            return out_ref.at[pl.ds((4 * px + 2 * py + pc) * m_per, m_per), :]

        def copy(k, block, to, src=None):
            return pltpu.make_async_remote_copy(
                src_ref=rows(*block) if src is None else src, dst_ref=rows(*block),
                send_sem=send_sems.at[k], recv_sem=recv_sems.at[k], device_id=to, device_id_type=MESH)

        mine = pltpu.make_async_copy(x_ref, rows(*me), local_sem)
        mine.start()
        first = [copy(0, me, sibling, src=x_ref)]
        first += [copy(1 + j, me, (*chip, c_), src=x_ref) for j, chip in enumerate(chips)]
        for cp in first:
            cp.start()
        passed = [copy(4 + j, (*chip, c_), sibling) for j, chip in enumerate(chips)]
        for j, chip in enumerate(chips):
            copy(1 + j, (*chip, c_), me).wait_recv()
            passed[j].start()
        copy(0, sibling, me).wait_recv()
        for j, chip in enumerate(chips):
            copy(4 + j, (*chip, 1 - c_), me).wait_recv()
        for cp in first + passed:
            cp.wait_send()
        mine.wait()

    return pl.pallas_call(
        body, name=name,
        out_shape=jax.ShapeDtypeStruct((N_DEV * m_per, n), x.dtype),
        in_specs=[pl.BlockSpec(memory_space=space)],
        out_specs=pl.BlockSpec(memory_space=space),
        scratch_shapes=[pltpu.SemaphoreType.DMA((7,)), pltpu.SemaphoreType.DMA((7,)), pltpu.SemaphoreType.DMA],
    )(x)


class _Comm:
    def __init__(self, inputs, out_shapes, aliases, n_send, n_recv, start, finish):
        self.inputs, self.out_shapes, self.aliases = list(inputs), list(out_shapes), dict(aliases)
        self.n_send, self.n_recv, self.start, self.finish = n_send, n_recv, start, finish


def _comm_call(compute, comm, *, name, grid, in_specs, out_specs, out_shape, args, aliases, semantics, scratch=()):
    in_specs, out_specs, out_shape, args, aliases = list(in_specs), list(out_specs), list(out_shape), list(args), dict(aliases)
    scratch = list(scratch)
    if comm is None:
        return pl.pallas_call(compute, name=name, grid=grid, in_specs=in_specs, out_specs=out_specs, out_shape=out_shape,
                              input_output_aliases=aliases, scratch_shapes=scratch, compiler_params=_params(semantics))(*args)
    n_in, n_out, n_ci, n_co = len(args), len(out_shape), len(comm.inputs), len(comm.out_shapes)
    hbm = pl.BlockSpec(memory_space=pl.ANY)
    aliases.update({n_in + i: n_out + o for i, o in comm.aliases.items()})

    def body(*refs):
        ins, c_ins = refs[:n_in], refs[n_in:n_in + n_ci]
        outs, c_outs = refs[n_in + n_ci:n_in + n_ci + n_out], refs[n_in + n_ci + n_out:n_in + n_ci + n_out + n_co]
        scr = refs[n_in + n_ci + n_out + n_co:-2]
        send_sems, recv_sems = refs[-2:]
        ids = [pl.program_id(a) for a in range(len(grid))]
        first = functools.reduce(jnp.logical_and, [i == 0 for i in ids])
        last = functools.reduce(jnp.logical_and, [i == g - 1 for i, g in zip(ids, grid)])

        @pl.when(first)
        def _():
            comm.start(c_ins, c_outs, send_sems, recv_sems)

        compute(*ins, *outs, *scr)

        @pl.when(last)
        def _():
            comm.finish(c_ins, c_outs, send_sems, recv_sems)

    return pl.pallas_call(
        body, name=name, grid=grid,
        in_specs=in_specs + [hbm] * n_ci, out_specs=out_specs + [hbm] * n_co, out_shape=out_shape + comm.out_shapes,
        input_output_aliases=aliases,
        scratch_shapes=scratch + [pltpu.SemaphoreType.DMA((comm.n_send,)), pltpu.SemaphoreType.DMA((comm.n_recv,))],
        compiler_params=_params(("arbitrary",) * len(grid)),
    )(*args, *comm.inputs)


def _place():
    x_, y_, c_ = lax.axis_index("x"), lax.axis_index("y"), lax.axis_index("c")
    return x_, y_, c_, [(1 - x_, y_), (x_, 1 - y_), (1 - x_, 1 - y_)]


GATHER_SENDS, GATHER_RECVS = 8, 7


def _gather_copies(packed_ref, wg_ref, send_sems, recv_sems, rows, nth=0):
    r0, n = rows
    x_, y_, c_, chips = _place()
    me, sibling = (x_, y_, c_), (x_, y_, 1 - c_)
    src = packed_ref.at[pl.ds(r0, n), :]

    def slot(px, py, pc):
        return wg_ref.at[4 * px + 2 * py + pc]

    def copy(k, block, to, from_packed=False):
        return pltpu.make_async_remote_copy(src_ref=src if from_packed else slot(*block), dst_ref=slot(*block),
                                            send_sem=send_sems.at[GATHER_SENDS * nth + k], recv_sem=recv_sems.at[GATHER_RECVS * nth + k],
                                            device_id=to, device_id_type=MESH)

    own = [copy(0, me, sibling, True)] + [copy(1 + j, me, (*chip, c_), True) for j, chip in enumerate(chips)]
    passed = [copy(4 + j, (*chip, c_), sibling) for j, chip in enumerate(chips)]
    over_ici = [copy(1 + j, (*chip, c_), me) for j, chip in enumerate(chips)]
    from_sibling = [copy(0, sibling, me)] + [copy(4 + j, (*chip, 1 - c_), me) for j, chip in enumerate(chips)]
    mine = pltpu.make_async_copy(src, slot(*me), send_sems.at[GATHER_SENDS * nth + 7])
    return mine, own, passed, over_ici, from_sibling


def _gather_start(packed_ref, wg_ref, send_sems, recv_sems, rows, nth=0, copy_own=True):
    mine, own, _, _, _ = _gather_copies(packed_ref, wg_ref, send_sems, recv_sems, rows, nth)
    if copy_own:
        mine.start()
    for cp in own:
        cp.start()


def _gather_finish(packed_ref, wg_ref, send_sems, recv_sems, rows, nth=0, copy_own=True):
    mine, own, passed, over_ici, from_sibling = _gather_copies(packed_ref, wg_ref, send_sems, recv_sems, rows, nth)
    for arrived, onward in zip(over_ici, passed):
        arrived.wait_recv()
        onward.start()
    for arrived in from_sibling:
        arrived.wait_recv()
    for cp in own + passed:
        cp.wait_send()
    if copy_own:
        mine.wait()


def _gather_comm(packed, ranges, copy_own=True):
    shapes = [jax.ShapeDtypeStruct((N_DEV, n, packed.shape[1]), packed.dtype) for _, n in ranges]

    def start(ins, outs, ss, rs):
        for nth, rows in enumerate(ranges):
            _gather_start(ins[0], outs[nth], ss, rs, rows, nth, copy_own)

    def finish(ins, outs, ss, rs):
        for nth, rows in enumerate(ranges):
            _gather_finish(ins[0], outs[nth], ss, rs, rows, nth, copy_own)

    return _Comm([packed], shapes, {}, GATHER_SENDS * len(ranges), GATHER_RECVS * len(ranges), start, finish)


def _pair_copy(p_ref, out_ref, send_sems, recv_sems):
    x_, y_, c_, _ = _place()
    return pltpu.make_async_remote_copy(src_ref=p_ref.at[1 - c_], dst_ref=out_ref,
                                        send_sem=send_sems.at[0], recv_sem=recv_sems.at[0],
                                        device_id=(x_, y_, 1 - c_), device_id_type=MESH)


def _pair_comm(p):
    return _Comm([p], [jax.ShapeDtypeStruct(p.shape[1:], p.dtype)], {}, 1, 1,
                 lambda ins, outs, ss, rs: _pair_copy(ins[0], outs[0], ss, rs).start(),
                 lambda ins, outs, ss, rs: _pair_copy(ins[0], outs[0], ss, rs).wait())


def _chip_copies(a_refs, out_refs, send_sems, recv_sems):
    _, _, c_, chips = _place()
    return [pltpu.make_async_remote_copy(src_ref=a_ref.at[2 * tx + ty], dst_ref=o_ref.at[j],
                                         send_sem=send_sems.at[3 * g + j], recv_sem=recv_sems.at[3 * g + j],
                                         device_id=(tx, ty, c_), device_id_type=MESH)
            for g, (a_ref, o_ref) in enumerate(zip(a_refs, out_refs)) for j, (tx, ty) in enumerate(chips)]


def _chip_start(a_refs, out_refs, send_sems, recv_sems):
    for cp in _chip_copies(a_refs, out_refs, send_sems, recv_sems):
        cp.start()


def _chip_finish(a_refs, out_refs, send_sems, recv_sems):
    for cp in _chip_copies(a_refs, out_refs, send_sems, recv_sems):
        cp.wait()


def _chip_comm(arrays):
    shapes = [jax.ShapeDtypeStruct((3,) + a.shape[1:], a.dtype) for a in arrays]
    return _Comm(arrays, shapes, {}, 3 * len(arrays), 3 * len(arrays), _chip_start, _chip_finish)


def _halves_copies(in_refs, out_refs, send_sems, recv_sems):
    x_, y_, c_, _ = _place()
    return [pltpu.make_async_remote_copy(src_ref=o_ref.at[c_], dst_ref=o_ref.at[c_], send_sem=send_sems.at[i], recv_sem=recv_sems.at[i],
                                         device_id=(x_, y_, 1 - c_), device_id_type=MESH)
            for i, o_ref in enumerate(out_refs)]


def _halves_start(in_refs, out_refs, send_sems, recv_sems):
    for cp in _halves_copies(in_refs, out_refs, send_sems, recv_sems):
        cp.start()


def _halves_finish(in_refs, out_refs, send_sems, recv_sems):
    for cp in _halves_copies(in_refs, out_refs, send_sems, recv_sems):
        cp.wait()


def _halves_comm(arrays):
    shapes = [jax.ShapeDtypeStruct(a.shape, a.dtype) for a in arrays]
    return _Comm(arrays, shapes, {i: i for i in range(len(arrays))}, len(arrays), len(arrays), _halves_start, _halves_finish)


def _comm_alone(comm, name):
    n_ci = len(comm.inputs)
    hbm = pl.BlockSpec(memory_space=pl.ANY)

    def body(*refs):
        c_ins, c_outs, send_sems, recv_sems = refs[:n_ci], refs[n_ci:-2], refs[-2], refs[-1]
        comm.start(c_ins, c_outs, send_sems, recv_sems)
        comm.finish(c_ins, c_outs, send_sems, recv_sems)

    return pl.pallas_call(
        body, name=name, out_shape=comm.out_shapes, in_specs=[hbm] * n_ci, out_specs=[hbm] * len(comm.out_shapes),
        input_output_aliases=comm.aliases,
        scratch_shapes=[pltpu.SemaphoreType.DMA((comm.n_send,)), pltpu.SemaphoreType.DMA((comm.n_recv,))],
    )(*comm.inputs)


SUM_TILES = (704, 512, 384, 320, 256, 192, 128, 64)


def _pair_sum(p, r1, c_idx, name):
    _, _, n, c = p.shape
    tr = _pick(n, SUM_TILES)

    def body(s_ref, p_ref, r_ref, o32_ref, o16_ref):
        v = p_ref[...] + r_ref[...]
        o32_ref[...] = v
        o16_ref[...] = v.astype(BF16)

    blk = pl.BlockSpec((None, tr, c), lambda j, i, s: (j, i, 0))
    grid_spec = pltpu.PrefetchScalarGridSpec(
        num_scalar_prefetch=1, grid=(4, n // tr),
        in_specs=[pl.BlockSpec((None, None, tr, c), lambda j, i, s: (s[0], j, i, 0)), blk],
        out_specs=[blk, blk])
    return pl.pallas_call(
        body, name=name, grid_spec=grid_spec,
        out_shape=[jax.ShapeDtypeStruct((4, n, c), F32), jax.ShapeDtypeStruct((4, n, c), BF16)],
        compiler_params=_params(("arbitrary", "arbitrary")),
    )(c_idx, p, r1)


def _owner_sum(a32, r2, kc_idx, name):
    _, r, c = a32.shape
    tr = _pick(r, SUM_TILES)

    def body(s_ref, a_ref, r_ref, o_ref):
        v = a_ref[...]
        for j in range(3):
            v = v + r_ref[j].astype(F32)
        o_ref[...] = v

    grid_spec = pltpu.PrefetchScalarGridSpec(
        num_scalar_prefetch=1, grid=(r // tr,),
        in_specs=[pl.BlockSpec((None, tr, c), lambda i, s: (s[0], i, 0)),
                  pl.BlockSpec((3, tr, c), lambda i, s: (0, i, 0))],
        out_specs=pl.BlockSpec((None, tr, c), lambda i, s: (s[1], i, 0)))
    return pl.pallas_call(
        body, name=name, grid_spec=grid_spec,
        out_shape=jax.ShapeDtypeStruct((2, r, c), F32),
        compiler_params=_params(("arbitrary",)),
    )(kc_idx, a32, r2)


def _pack_local_half(w_in_s, w_out_s, w_up_s, w_down_s, c_idx):
    parts, row = [], 0
    for (kind, l), off in sorted(PACK_OFF.items(), key=lambda kv: kv[1]):
        if off > row:
            parts.append(jnp.zeros((off - row, 1024), BF16))
        if kind == "up":
            p = lax.dynamic_slice_in_dim(w_up_s[l], c_idx * 512, 512, 0)
        elif kind == "down":
            p = lax.dynamic_slice_in_dim(w_down_s[l], c_idx * 512, 512, 0)
        elif kind == "in":
            p = lax.dynamic_slice_in_dim(w_in_s[l], c_idx * 512, 512, 0)
            p = p.reshape(2, 256, IN_PIECE_COLS).transpose(1, 0, 2).reshape(256, 2 * IN_PIECE_COLS)
            p = jnp.pad(p, ((0, 0), (0, 1024 - 2 * IN_PIECE_COLS)))
        else:
            p = lax.dynamic_slice_in_dim(w_out_s[l], c_idx * 128, 128, 0)
        parts.append(p.astype(BF16))
        row = off + PACK_HEIGHT[kind]
    return jnp.concatenate(parts, axis=0)


def _unpack_in_pieces(w_ref, own_ref, w_scr):
    if own_ref is not None:
        me = 4 * lax.axis_index("x") + 2 * lax.axis_index("y") + lax.axis_index("c")
    for d in range(N_DEV):
        k, c = d // 2, d % 2
        for t in range(2):
            piece = w_ref[d, :, t * IN_PIECE_COLS:(t + 1) * IN_PIECE_COLS]
            if own_ref is not None:
                piece = jnp.where(me == d, own_ref[:, t * IN_PIECE_COLS:(t + 1) * IN_PIECE_COLS], piece)
            w_scr[c * 512 + t * 256:c * 512 + (t + 1) * 256, k * IN_PIECE_COLS:(k + 1) * IN_PIECE_COLS] = piece


def _in_weight_operands(wg):
    specs, args = [_gathered_spec(wg, "in")], [wg["in"][0]]
    if "in_own" in wg:
        own, off = wg["in_own"]
        h = PACK_HEIGHT["in"]
        assert off % h == 0
        specs.append(pl.BlockSpec((h, 1024), lambda *_: (off // h, 0), pipeline_mode=pl.Buffered(1)))
        args.append(own)
    return specs, args


class _Rows:
    def __init__(self, nb, seq, ctx):
        self.nb, self.seq, self.ctx = nb, seq, ctx
        self.n_lat, self.n_ctx = nb * seq, nb * ctx
        self.rows = self.n_lat + self.n_ctx
        self.tm = _pick(np.gcd(seq, self.n_ctx), (512, 256, 128))
        self.tiles_per_ex = seq // self.tm
        self.n_tiles = self.rows // self.tm
        self.n_lat_tiles = self.n_lat // self.tm
        self.groups = nb + 1

    def group(self, i):
        return jnp.minimum(i // self.tiles_per_ex, self.nb)

    def first_of_group(self, i):
        return jnp.logical_and(i % self.tiles_per_ex == 0, i <= self.n_lat_tiles)


def _mod_spec(rt):
    return pl.BlockSpec((1, N_MOD, D_MODEL), lambda i: (rt.group(i), 0, 0))


def _row_spec(rt, cols):
    return pl.BlockSpec((rt.tm, cols), lambda i: (i, 0))


def _vec_spec(cols):
    return pl.BlockSpec((1, cols), lambda i: (0, 0))


def _group_spec(rt):
    return pl.BlockSpec((1, 1, D_MODEL), lambda i: (rt.group(i), 0, 0))


def _gathered_spec(wg, kind):
    h, off = PACK_HEIGHT[kind], wg[kind][1]
    assert off % h == 0, (kind, off)
    return pl.BlockSpec((N_DEV, h, 1024), lambda *_: (0, off // h, 0), pipeline_mode=pl.Buffered(1))


def _group_shape(rt):
    return jax.ShapeDtypeStruct((rt.groups, 1, D_MODEL), F32)


def _vec_shape(cols=D_MODEL):
    return jax.ShapeDtypeStruct((1, cols), F32)


def _rms_inv(v):
    return lax.rsqrt(jnp.mean(v * v, axis=-1, keepdims=True) + EPS)


def _norm_mod_val(h_, g_, mod_ref, i_shift, i_scale):
    n = h_ * _rms_inv(h_) * g_
    return n * (1.0 + mod_ref[0, i_scale:i_scale + 1, :]) + mod_ref[0, i_shift:i_shift + 1, :]


def _post_norm_val(h_, z_, g_, mod_ref, i_gate):
    return h_ + mod_ref[0, i_gate:i_gate + 1, :] * (z_ * _rms_inv(z_) * g_)


def _post_norm_bwd_val(dh_, z_, g_, gate):
    rinv = _rms_inv(z_)
    n0 = z_ * rinv
    dn = dh_ * gate * g_
    dz = rinv * (dn - n0 * jnp.mean(dn * n0, axis=-1, keepdims=True))
    return dz, jnp.sum(dh_ * n0 * g_, axis=0, keepdims=True), jnp.sum(dh_ * gate * n0, axis=0, keepdims=True)


def _norm_mod_bwd_val(du_, h_, g_, one_sc):
    rinv = _rms_inv(h_)
    n0 = h_ * rinv
    dn = du_ * g_ * one_sc
    dh = rinv * (dn - n0 * jnp.mean(dn * n0, axis=-1, keepdims=True))
    return (dh, jnp.sum(du_, axis=0, keepdims=True), jnp.sum(du_ * n0 * g_, axis=0, keepdims=True),
            jnp.sum(du_ * one_sc * n0, axis=0, keepdims=True))


def _accumulate(rt, i, group_pairs, global_pairs):
    @pl.when(rt.first_of_group(i))
    def _():
        for ref, _ in group_pairs:
            ref[...] = jnp.zeros_like(ref)

    @pl.when(i == 0)
    def _():
        for ref, _ in global_pairs:
            ref[...] = jnp.zeros_like(ref)

    for ref, val in group_pairs:
        ref[0] += val
    for ref, val in global_pairs:
        ref[...] += val


def _rope_tables(rt):
    pos = jnp.arange(rt.seq, dtype=jnp.int32)
    row_ids = (pos // GRID_W).astype(F32)
    col_ids = (pos % GRID_W).astype(F32)
    axis_dim = HEAD_DIM // 2
    inv = ROPE_THETA ** (-jnp.arange(0, axis_dim, 2, dtype=F32) / axis_dim)
    ang_r, ang_c = row_ids[:, None] * inv[None, :], col_ids[:, None] * inv[None, :]
    cr, sr, cc, sc = jnp.cos(ang_r), jnp.sin(ang_r), jnp.cos(ang_c), jnp.sin(ang_c)
    zero = jnp.zeros_like(sr)
    cos = jnp.concatenate([cr, cr, cc, cc], axis=1)
    s_lo = jnp.concatenate([zero, sr, zero, sc], axis=1)
    s_hi = jnp.concatenate([-sr, zero, -sc, zero], axis=1)

    def full(t, ctx_value):
        t = jnp.tile(t, (rt.nb, 2))
        return jnp.concatenate([t, jnp.full((rt.n_ctx, 128), ctx_value, F32)], axis=0)

    return full(cos, 1.0), full(s_lo, 0.0), full(s_hi, 0.0)


def _head_stats(t, lo):
    sq = t * t
    s_lo = jnp.sum(jnp.where(lo, sq, 0.0), axis=1, keepdims=True)
    s_hi = jnp.sum(jnp.where(lo, 0.0, sq), axis=1, keepdims=True)
    return lax.rsqrt(jnp.where(lo, s_lo, s_hi) * (1.0 / HEAD_DIM) + EPS)


def _prep_fwd_body(tm, qkv_ref, c, s1, s2, qn, kn, out_ref):
    lo = lax.broadcasted_iota(jnp.int32, (tm, 128), 1) < HEAD_DIM

    def rope(t):
        return t * c + pltpu.roll(t, 16, 1) * s1 + pltpu.roll(t, 112, 1) * s2

    for j in range(12):
        t = qkv_ref[:, j * 128:(j + 1) * 128]
        if j < 4:
            t = rope(t * _head_stats(t, lo) * qn) * Q_SCALE
        elif j == COL_KA:
            t = rope(t * _head_stats(t, lo) * kn)
        elif 6 <= j < 10:
            t = rope(t) * Q_SCALE
        elif j == COL_KB:
            t = rope(t)
        out_ref[:, j * 128:(j + 1) * 128] = t.astype(BF16)


def _prep_bwd_body(tm, dq_ref, dkv_ref, qkv_ref, c, s1, s2, qn, kn, out_ref):
    lo = lax.broadcasted_iota(jnp.int32, (tm, 128), 1) < HEAD_DIM

    def rope_bwd(d):
        return d * c + pltpu.roll(d * s1, 112, 1) + pltpu.roll(d * s2, 16, 1)

    def norm_bwd(t, g, dy):
        rinv = _head_stats(t, lo)
        n = t * rinv
        dn = dy * g
        prod = dn * n
        m_lo = jnp.sum(jnp.where(lo, prod, 0.0), axis=1, keepdims=True)
        m_hi = jnp.sum(jnp.where(lo, 0.0, prod), axis=1, keepdims=True)
        mean = jnp.where(lo, m_lo, m_hi) * (1.0 / HEAD_DIM)
        return rinv * (dn - n * mean), jnp.sum(dy * n, axis=0, keepdims=True)

    dqn = jnp.zeros((1, 128), F32)
    dkn = jnp.zeros((1, 128), F32)
    for j in range(12):
        if j < 4:
            d, dg = norm_bwd(qkv_ref[:, j * 128:(j + 1) * 128], qn, rope_bwd(dq_ref[:, j * 128:(j + 1) * 128] * Q_SCALE))
            dqn = dqn + dg
        elif j == COL_KA:
            d, dg = norm_bwd(qkv_ref[:, j * 128:(j + 1) * 128], kn, rope_bwd(dkv_ref[:, 0:128]))
            dkn = dkn + dg
        elif j == COL_VA:
            d = dkv_ref[:, 128:256]
        elif j < 10:
            d = rope_bwd(dq_ref[:, (j - 2) * 128:(j - 1) * 128] * Q_SCALE)
        elif j == COL_KB:
            d = rope_bwd(dkv_ref[:, 256:384])
        else:
            d = dkv_ref[:, 384:512]
        out_ref[:, j * 128:(j + 1) * 128] = d.astype(BF16)
    return dqn, dkn


def _in_fwd(rt, h, gamma, mod, wg, tables, qn, kn, name):
    w_specs, w_args = _in_weight_operands(wg)
    n_w = len(w_args)

    def body(h_ref, g_ref, mod_ref, *rest):
        c_ref, s1_ref, s2_ref, qn_ref, kn_ref, u_ref, qkn_ref, qkvp_ref, qkv_ref, w_scr = rest[n_w:]

        @pl.when(pl.program_id(0) == 0)
        def _():
            _unpack_in_pieces(rest[0], rest[1] if n_w == 2 else None, w_scr)

        u = _norm_mod_val(h_ref[...], g_ref[...], mod_ref, 0, 1).astype(BF16)
        u_ref[...] = u
        qkv_ref[...] = jnp.dot(u, w_scr[...], preferred_element_type=F32)
        qkn_ref[...] = qkv_ref[:, 0:NORMED_COLS]
        _prep_fwd_body(rt.tm, qkv_ref, c_ref[...], s1_ref[...], s2_ref[...], qn_ref[...], kn_ref[...], qkvp_ref)

    return pl.pallas_call(
        body, name=name, grid=(rt.n_tiles,),
        in_specs=[_row_spec(rt, D_MODEL), _vec_spec(D_MODEL), _mod_spec(rt)] + w_specs + [_row_spec(rt, 128)] * 3 + [_vec_spec(128)] * 2,
        out_specs=[_row_spec(rt, D_MODEL), _row_spec(rt, NORMED_COLS), _row_spec(rt, IN_COLS)],
        out_shape=[jax.ShapeDtypeStruct((rt.rows, D_MODEL), BF16), jax.ShapeDtypeStruct((rt.rows, NORMED_COLS), F32),
                   jax.ShapeDtypeStruct((rt.rows, IN_COLS), BF16)],
        scratch_shapes=[pltpu.VMEM((rt.tm, IN_COLS), F32), pltpu.VMEM((D_MODEL, IN_COLS), BF16)],
        compiler_params=_params(("arbitrary",)),
    )(h, gamma, mod, *w_args, *tables, qn, kn)


def _in_bwd(rt, dq, dkv, qkv, tables, qn, kn, wg, h, dres, mod, gamma, latent_only, name, comm=None):
    last = rt.n_lat_tiles - 1
    w_specs, w_args = _in_weight_operands(wg)
    n_w = len(w_args)

    def body(dq_ref, dkv_ref, qkv_ref, c_ref, s1_ref, s2_ref, qn_ref, kn_ref, *rest):
        h_ref, dres_ref, mod_ref, g_ref, dqkv_ref, dh_ref, dqn_ref, dkn_ref, dsh_ref, dsc_ref, dg_ref, w_scr = rest[n_w:]
        i = pl.program_id(0)

        @pl.when(i == 0)
        def _():
            _unpack_in_pieces(rest[0], rest[1] if n_w == 2 else None, w_scr)

        dqn, dkn = _prep_bwd_body(rt.tm, dq_ref, dkv_ref, qkv_ref, c_ref[...], s1_ref[...], s2_ref[...], qn_ref[...], kn_ref[...], dqkv_ref)
        du = lax.dot_general(dqkv_ref[...], w_scr[...], NT, preferred_element_type=F32)
        dh, dsh, dsc, dg = _norm_mod_bwd_val(du, h_ref[...], g_ref[...], 1.0 + mod_ref[0, 1:2, :])
        if latent_only:
            @pl.when(i <= last)
            def _():
                dh_ref[...] = dres_ref[...] + dh
        else:
            dh_ref[...] = dres_ref[...] + dh
        _accumulate(rt, i, [(dsh_ref, dsh), (dsc_ref, dsc)], [(dg_ref, dg), (dqn_ref, dqn), (dkn_ref, dkn)])

    dh_spec = pl.BlockSpec((rt.tm, D_MODEL), lambda i: (jnp.minimum(i, last), 0)) if latent_only else _row_spec(rt, D_MODEL)
    return _comm_call(
        body, comm, name=name, grid=(rt.n_tiles,),
        in_specs=[_row_spec(rt, 1024), _row_spec(rt, 512), _row_spec(rt, NORMED_COLS)] + [_row_spec(rt, 128)] * 3 + [_vec_spec(128)] * 2
        + w_specs + [_row_spec(rt, D_MODEL), _row_spec(rt, D_MODEL), _mod_spec(rt), _vec_spec(D_MODEL)],
        out_specs=[_row_spec(rt, IN_COLS), dh_spec, _vec_spec(128), _vec_spec(128),
                   _group_spec(rt), _group_spec(rt), _vec_spec(D_MODEL)],
        out_shape=[jax.ShapeDtypeStruct((rt.rows, IN_COLS), BF16),
                   jax.ShapeDtypeStruct((rt.n_lat if latent_only else rt.rows, D_MODEL), F32),
                   _vec_shape(128), _vec_shape(128), _group_shape(rt), _group_shape(rt), _vec_shape()],
        args=[dq, dkv, qkv, *tables, qn, kn, *w_args, h, dres, mod, gamma], aliases={}, semantics=("arbitrary",),
        scratch=[pltpu.VMEM((D_MODEL, IN_COLS), BF16)])


def _out_fwd(rt, o, wg, h, mod, g_post_mix, g_pre_mlp, name):
    def body(o_ref, w_ref, h_ref, mod_ref, gpost_ref, gpre_ref, mix_ref, h1_ref, u2_ref):
        mix = jnp.dot(o_ref[...], w_ref[...].reshape(D_MODEL, D_MODEL), preferred_element_type=F32)
        mix_ref[...] = mix
        h1 = _post_norm_val(h_ref[...], mix, gpost_ref[...], mod_ref, 2)
        h1_ref[...] = h1
        u2_ref[...] = _norm_mod_val(h1, gpre_ref[...], mod_ref, 3, 4).astype(BF16)

    return pl.pallas_call(
        body, name=name, grid=(rt.n_tiles,),
        in_specs=[_row_spec(rt, D_MODEL), _gathered_spec(wg, "out"), _row_spec(rt, D_MODEL), _mod_spec(rt),
                  _vec_spec(D_MODEL), _vec_spec(D_MODEL)],
        out_specs=[_row_spec(rt, D_MODEL)] * 3,
        out_shape=[jax.ShapeDtypeStruct((rt.rows, D_MODEL), F32), jax.ShapeDtypeStruct((rt.rows, D_MODEL), F32),
                   jax.ShapeDtypeStruct((rt.rows, D_MODEL), BF16)],
        compiler_params=_params(("parallel",)),
    )(o, wg["out"][0], h, mod, g_post_mix, g_pre_mlp)


def _out_bwd(rt, dh1, mix, wg, mod, g_post_mix, name, comm=None):
    def body(dh_ref, mix_ref, w_ref, mod_ref, g_ref, dmix_ref, do_ref, dgate_ref, dg_ref):
        i = pl.program_id(0)
        dz, dgate, dg = _post_norm_bwd_val(dh_ref[...], mix_ref[...], g_ref[...], mod_ref[0, 2:3, :])
        dzb = dz.astype(BF16)
        dmix_ref[...] = dzb
        do_ref[...] = lax.dot_general(dzb, w_ref[...].reshape(D_MODEL, D_MODEL), NT, preferred_element_type=F32).astype(BF16)
        _accumulate(rt, i, [(dgate_ref, dgate)], [(dg_ref, dg)])

    return _comm_call(
        body, comm, name=name, grid=(rt.n_tiles,),
        in_specs=[_row_spec(rt, D_MODEL), _row_spec(rt, D_MODEL), _gathered_spec(wg, "out"), _mod_spec(rt), _vec_spec(D_MODEL)],
        out_specs=[_row_spec(rt, D_MODEL), _row_spec(rt, D_MODEL), _group_spec(rt), _vec_spec(D_MODEL)],
        out_shape=[jax.ShapeDtypeStruct((rt.rows, D_MODEL), BF16), jax.ShapeDtypeStruct((rt.rows, D_MODEL), BF16),
                   _group_shape(rt), _vec_shape()],
        args=[dh1, mix, wg["out"][0], mod, g_post_mix], aliases={}, semantics=("arbitrary",))


def _w_chunk(w_ref, k):
    return w_ref[2 * k:2 * k + 2].reshape(1024, 1024)


def _mlp_fwd(rt, u2, h1, wg, mod, g_post_mlp, name, comm=None):
    def body(u2_ref, h1_ref, wu_ref, wd_ref, mod_ref, g_ref, ra_ref, y_ref, h2_ref):
        u2_ = u2_ref[...]
        y = jnp.zeros((rt.tm, D_MODEL), F32)
        for k in range(D_FF // 1024):
            a = jnp.maximum(jnp.dot(u2_, _w_chunk(wu_ref, k), preferred_element_type=F32), 0.0)
            ra_ref[:, k * 1024:(k + 1) * 1024] = a.astype(BF16)
            y = y + jnp.dot((a * a).astype(BF16), _w_chunk(wd_ref, k), preferred_element_type=F32)
        y_ref[...] = y
        h2_ref[...] = _post_norm_val(h1_ref[...], y, g_ref[...], mod_ref, 5)

    return _comm_call(
        body, comm, name=name, grid=(rt.n_tiles,),
        in_specs=[_row_spec(rt, D_MODEL), _row_spec(rt, D_MODEL), _gathered_spec(wg, "up"), _gathered_spec(wg, "down"),
                  _mod_spec(rt), _vec_spec(D_MODEL)],
        out_specs=[_row_spec(rt, D_FF), _row_spec(rt, D_MODEL), _row_spec(rt, D_MODEL)],
        out_shape=[jax.ShapeDtypeStruct((rt.rows, D_FF), BF16), jax.ShapeDtypeStruct((rt.rows, D_MODEL), F32),
                   jax.ShapeDtypeStruct((rt.rows, D_MODEL), F32)],
        args=[u2, h1, wg["up"][0], wg["down"][0], mod, g_post_mlp], aliases={}, semantics=("parallel",))


def _mlp_down_bwd(rt, dh, y, ra, wg, mod, g_post_mlp, name, comm=None):
    def body(dh_ref, y_ref, ra_ref, wd_ref, mod_ref, g_ref, dy_ref, da_ref, dgate_ref, dg_ref):
        i = pl.program_id(0)
        dz, dgate, dg = _post_norm_bwd_val(dh_ref[...], y_ref[...], g_ref[...], mod_ref[0, 5:6, :])
        dyb = dz.astype(BF16)
        dy_ref[...] = dyb
        for k in range(D_FF // 1024):
            dr = lax.dot_general(dyb, _w_chunk(wd_ref, k), NT, preferred_element_type=F32)
            da_ref[:, k * 1024:(k + 1) * 1024] = (dr * (2.0 * ra_ref[:, k * 1024:(k + 1) * 1024].astype(F32))).astype(BF16)
        _accumulate(rt, i, [(dgate_ref, dgate)], [(dg_ref, dg)])

    return _comm_call(
        body, comm, name=name, grid=(rt.n_tiles,),
        in_specs=[_row_spec(rt, D_MODEL), _row_spec(rt, D_MODEL), _row_spec(rt, D_FF), _gathered_spec(wg, "down"),
                  _mod_spec(rt), _vec_spec(D_MODEL)],
        out_specs=[_row_spec(rt, D_MODEL), _row_spec(rt, D_FF), _group_spec(rt), _vec_spec(D_MODEL)],
        out_shape=[jax.ShapeDtypeStruct((rt.rows, D_MODEL), BF16), jax.ShapeDtypeStruct((rt.rows, D_FF), BF16),
                   _group_shape(rt), _vec_shape()],
        args=[dh, y, ra, wg["down"][0], mod, g_post_mlp], aliases={}, semantics=("arbitrary",))


def _mlp_up_bwd(rt, da, wg, h1, dh, mod, g_pre_mlp, name):
    def body(da_ref, wu_ref, h1_ref, dh_ref, mod_ref, g_ref, dh1_ref, dsh_ref, dsc_ref, dg_ref):
        i = pl.program_id(0)
        du = jnp.zeros((rt.tm, D_MODEL), F32)
        for k in range(D_FF // 1024):
            du = du + lax.dot_general(da_ref[:, k * 1024:(k + 1) * 1024], _w_chunk(wu_ref, k), NT, preferred_element_type=F32)
        d, dsh, dsc, dg = _norm_mod_bwd_val(du, h1_ref[...], g_ref[...], 1.0 + mod_ref[0, 4:5, :])
        dh1_ref[...] = dh_ref[...] + d
        _accumulate(rt, i, [(dsh_ref, dsh), (dsc_ref, dsc)], [(dg_ref, dg)])

    return pl.pallas_call(
        body, name=name, grid=(rt.n_tiles,),
        in_specs=[_row_spec(rt, D_FF), _gathered_spec(wg, "up"), _row_spec(rt, D_MODEL), _row_spec(rt, D_MODEL),
                  _mod_spec(rt), _vec_spec(D_MODEL)],
        out_specs=[_row_spec(rt, D_MODEL), _group_spec(rt), _group_spec(rt), _vec_spec(D_MODEL)],
        out_shape=[jax.ShapeDtypeStruct((rt.rows, D_MODEL), F32), _group_shape(rt), _group_shape(rt), _vec_shape()],
        compiler_params=_params(("arbitrary",)),
    )(da, wg["up"][0], h1, dh, mod, g_pre_mlp)


def _wgrad_packed(rt, a, b, kind, off, n_rows, p_prev, name, comm=None):
    h = PACK_HEIGHT[kind]
    tk = rt.tm
    assert off % h == 0, (kind, off)

    def body(a_ref, b_ref, *rest):
        o_ref = rest[-1]
        i = pl.program_id(0)

        @pl.when(i == 0)
        def _():
            o_ref[...] = jnp.zeros_like(o_ref)

        if kind == "in":
            res = lax.dot_general(a_ref[...], b_ref[...], TN, preferred_element_type=F32)
            for k in range(4):
                for c in range(2):
                    for t in range(2):
                        o_ref[c, k, :, t * IN_PIECE_COLS:(t + 1) * IN_PIECE_COLS] += \
                            res[c * 512 + t * h:c * 512 + (t + 1) * h, k * IN_PIECE_COLS:(k + 1) * IN_PIECE_COLS]
        elif kind == "out":
            res = lax.dot_general(a_ref[...], b_ref[...], TN, preferred_element_type=F32)
            for k in range(4):
                for c in range(2):
                    o_ref[c, k] += res[(2 * k + c) * h:(2 * k + c + 1) * h]
        else:
            for k in range(4):
                if kind == "up":
                    res = lax.dot_general(a_ref[...], b_ref[:, k * 1024:(k + 1) * 1024], TN, preferred_element_type=F32)
                else:
                    ra = a_ref[:, k * 1024:(k + 1) * 1024].astype(F32)
                    res = lax.dot_general((ra * ra).astype(BF16), b_ref[...], TN, preferred_element_type=F32)
                o_ref[0, k] += res[0:h]
                o_ref[1, k] += res[h:2 * h]

    in_specs = [pl.BlockSpec((tk, a.shape[1]), lambda i: (i, 0)), pl.BlockSpec((tk, b.shape[1]), lambda i: (i, 0))]
    args = [a, b]
    aliases = {}
    if p_prev is not None:
        in_specs.append(pl.BlockSpec(memory_space=pl.ANY))
        args.append(p_prev)
        aliases = {2: 0}
    outs = _comm_call(
        body, comm, name=name, grid=(rt.rows // tk,),
        in_specs=in_specs,
        out_specs=[pl.BlockSpec((2, 4, h, 1024), lambda i: (0, 0, off // h, 0))],
        out_shape=[jax.ShapeDtypeStruct((2, 4, n_rows, 1024), F32)],
        args=args, aliases=aliases, semantics=("arbitrary",))
    return outs[0] if comm is None else outs


def _ada_wgrad(xs, dm, name):
    depth, _, cols = dm.shape

    def body(x_ref, d_ref, o_ref):
        for l in range(depth):
            o_ref[l] = lax.dot_general(x_ref[...], d_ref[l], TN, preferred_element_type=F32)

    return pl.pallas_call(body, name=name, out_shape=jax.ShapeDtypeStruct((depth, xs.shape[1], cols), F32),
                          compiler_params=pltpu.CompilerParams(vmem_limit_bytes=VMEM_LIMIT))(xs, dm)


def _loss_grad(rt, h, target, name):
    last = rt.n_lat_tiles - 1

    def body(h_ref, t_ref, dh_ref, sq_ref):
        i = pl.program_id(0)

        @pl.when(i == 0)
        def _():
            sq_ref[...] = jnp.zeros_like(sq_ref)

        @pl.when(i <= last)
        def _():
            e = h_ref[...] - t_ref[...]
            dh_ref[...] = e * (1.0 / D_MODEL)
            sq_ref[...] += jnp.sum(e * e, axis=0, keepdims=True)

        @pl.when(i > last)
        def _():
            dh_ref[...] = jnp.zeros_like(dh_ref)

    return pl.pallas_call(
        body, name=name, grid=(rt.n_tiles,),
        in_specs=[_row_spec(rt, D_MODEL), pl.BlockSpec((rt.tm, D_MODEL), lambda i: (jnp.minimum(i, last), 0))],
        out_specs=[_row_spec(rt, D_MODEL), _vec_spec(D_MODEL)],
        out_shape=[jax.ShapeDtypeStruct((rt.rows, D_MODEL), F32), jax.ShapeDtypeStruct((1, D_MODEL), F32)],
        compiler_params=_params(("arbitrary",)),
    )(h, target)


def _stack_heads(x, kvi):
    x = x.astype(F32)
    tq = x.shape[0]
    lane = lax.broadcasted_iota(jnp.int32, (tq, 128), 1)
    keep = lane < HEAD_DIM if kvi == 0 else lane >= HEAD_DIM
    parts = []
    for p in range(2):
        pair = x[:, p * 128:(p + 1) * 128]
        swapped = pltpu.roll(pair, HEAD_DIM, 1)
        lo_head, hi_head = (pair, swapped) if kvi == 0 else (swapped, pair)
        parts += [jnp.where(keep, lo_head, 0.0), jnp.where(keep, hi_head, 0.0)]
    return jnp.concatenate(parts, axis=0).astype(BF16)


def _unstack_heads(o4, kvi):
    tq = o4.shape[0] // GROUP
    lane = lax.broadcasted_iota(jnp.int32, (tq, 128), 1)
    outs = []
    for p in range(2):
        r_lo, r_hi = o4[(2 * p) * tq:(2 * p + 1) * tq], o4[(2 * p + 1) * tq:(2 * p + 2) * tq]
        if kvi == 0:
            lo, hi = r_lo, pltpu.roll(r_hi, HEAD_DIM, 1)
        else:
            lo, hi = pltpu.roll(r_lo, HEAD_DIM, 1), r_hi
        outs.append(jnp.where(lane < HEAD_DIM, lo, hi))
    return jnp.concatenate(outs, axis=1)


def _per_head(shape, axis, tq, values):
    head = lax.broadcasted_iota(jnp.int32, shape, axis) // tq
    out = jnp.zeros(shape, F32)
    for g in range(GROUP):
        out = jnp.where(head == g, values[g], out)
    return out


KEY_CHUNK = 512


def _key_chunks(k_ref, v_ref, n, kc=KEY_CHUNK):
    kc = min(kc, n)
    return [(k_ref[c * kc:(c + 1) * kc, :], v_ref[c * kc:(c + 1) * kc, :], None) for c in range(n // kc)]


def _softmax_fwd(qs, kvi, chunks, sink_col):
    logits = []
    for k, _, mask in chunks:
        s = lax.dot_general(qs, k, NT, preferred_element_type=F32)
        logits.append(s if mask is None else jnp.where(mask, s, NEG_BIG))
    m = functools.reduce(jnp.maximum, [jnp.max(s, axis=1, keepdims=True) for s in logits])
    if sink_col is not None:
        m = jnp.maximum(m, sink_col)
    acc = jnp.zeros((qs.shape[0], 128), F32)
    for s, (_, v, _) in zip(logits, chunks):
        lane = lax.broadcasted_iota(jnp.int32, v.shape, 1)
        own = lane < HEAD_DIM if kvi == 0 else lane >= HEAD_DIM
        acc = acc + jnp.dot(jnp.exp(s - m).astype(BF16), jnp.where(own, v, jnp.ones_like(v)), preferred_element_type=F32)
    lane = lax.broadcasted_iota(jnp.int32, acc.shape, 1)
    other = lane >= HEAD_DIM if kvi == 0 else lane < HEAD_DIM
    l = jnp.sum(jnp.where(other, acc, 0.0), axis=1, keepdims=True) * (1.0 / HEAD_DIM)
    if sink_col is not None:
        l = l + jnp.exp(sink_col - m)
    return acc / l, m + jnp.log(l)


def _to_rows(col):
    return jnp.transpose(jnp.broadcast_to(col, (col.shape[0], 128)))[0:8, :]


def _softmax_bwd(qs, dos, lse_row, delta_row, chunks):
    dq = jnp.zeros((qs.shape[0], 128), F32)
    grads = []
    for k, v, mask in chunks:
        s = lax.dot_general(k, qs, NT, preferred_element_type=F32)
        if mask is not None:
            s = jnp.where(mask, s, NEG_BIG)
        p = jnp.exp(s - lse_row)
        dp = lax.dot_general(v, dos, NT, preferred_element_type=F32)
        ds = (p * (dp - delta_row)).astype(BF16)
        dv = jnp.dot(p.astype(BF16), dos, preferred_element_type=F32)
        dk = jnp.dot(ds, qs, preferred_element_type=F32)
        dq = dq + lax.dot_general(ds, k, TN, preferred_element_type=F32)
        grads.append((dk, dv))
    return dq, grads


def _band(qi, tq, seq):
    span = tq + 2 * WINDOW
    start = pl.multiple_of(jnp.clip(qi * tq - WINDOW, 0, seq - span), 64)
    return start, span


def _band_mask(qi, tq, start, span, query_axis):
    shape = (GROUP * tq, span) if query_axis == 0 else (span, GROUP * tq)
    qpos = qi * tq + lax.broadcasted_iota(jnp.int32, shape, query_axis) % tq
    kpos = start + lax.broadcasted_iota(jnp.int32, shape, 1 - query_axis)
    return jnp.abs(kpos - qpos) <= WINDOW


def _qkv_specs(rt, tq, q_row, ctx_row, with_latent):
    specs = [pl.BlockSpec((tq, 256), functools.partial(lambda b, i, col: (q_row(b, i), col), col=col)) for col in (0, 1, 3, 4)]
    if with_latent:
        specs += [pl.BlockSpec((rt.seq, 128), functools.partial(lambda b, i, col: (b, col), col=col))
                  for col in (COL_KA, COL_VA, COL_KB, COL_VB)]
    specs += [pl.BlockSpec((rt.ctx, 128), functools.partial(lambda b, i, col: (ctx_row(b), col), col=col))
              for col in (COL_KA, COL_VA, COL_KB, COL_VB)]
    return specs


def _attn_fwd(rt, qkvp, sink, o_prev, name, comm=None):
    latent = o_prev is None
    seq, ctx, nb = rt.seq, rt.ctx, rt.nb
    tq = 128 if latent else ctx
    nq = seq // tq if latent else 1
    ctx_blk0 = rt.n_lat // ctx
    q_row = (lambda b, i: b * nq + i) if latent else (lambda b, i: ctx_blk0 + b)

    def body(sink_ref, qa0, qa1, qb0, qb1, *rest):
        if latent:
            kal, val, kbl, vbl, kac, vac, kbc, vbc, o_ref, lse_ref = rest
        else:
            kac, vac, kbc, vbc, _, o_ref, lse_ref = rest
        qi = pl.program_id(1)
        for kvi, (qa, qb) in enumerate(((qa0, qb0), (qa1, qb1))):
            src_a = _key_chunks(kac, vac, ctx)
            src_b = _key_chunks(kbc, vbc, ctx)
            if latent:
                src_a += _key_chunks(kal, val, seq, seq)
                start, span = _band(qi, tq, seq)
                src_b.append((kbl[pl.ds(start, span), :], vbl[pl.ds(start, span), :], _band_mask(qi, tq, start, span, 0)))
            oa, lse = _softmax_fwd(_stack_heads(qa[...], kvi), kvi, src_a, None)
            o_ref[:, kvi * 256:(kvi + 1) * 256] = _unstack_heads(oa, kvi).astype(BF16)
            lse_ref[0, kvi] = _to_rows(lse)
            sink_col = _per_head((GROUP * tq, 1), 0, tq, [sink_ref[kvi * GROUP + g] for g in range(GROUP)])
            ob, lse = _softmax_fwd(_stack_heads(qb[...], kvi), kvi, src_b, sink_col)
            o_ref[:, 512 + kvi * 256:512 + (kvi + 1) * 256] = _unstack_heads(ob, kvi).astype(BF16)
            lse_ref[0, 2 + kvi] = _to_rows(lse)

    specs = _qkv_specs(rt, tq, q_row, lambda b: ctx_blk0 + b, latent)
    args = [sink] + [qkvp] * len(specs)
    in_specs = [pl.BlockSpec(memory_space=pltpu.SMEM)] + specs
    aliases = {}
    if not latent:
        in_specs.append(pl.BlockSpec(memory_space=pl.ANY))
        args.append(o_prev)
        aliases = {len(args) - 1: 0}
    return _comm_call(
        body, comm, name=name, grid=(nb, nq),
        in_specs=in_specs,
        out_specs=[pl.BlockSpec((tq, 1024), lambda b, i: (q_row(b, i), 0)),
                   pl.BlockSpec((1, 4, 8, GROUP * tq), lambda b, i: (b * nq + i, 0, 0, 0))],
        out_shape=[jax.ShapeDtypeStruct((rt.rows, 1024), BF16), jax.ShapeDtypeStruct((nb * nq, 4, 8, GROUP * tq), F32)],
        args=args, aliases=aliases, semantics=("parallel", "parallel"))


def _attn_bwd(rt, qkvp, o, lse, do, sink, prev, name, comm=None):
    latent = prev is None
    seq, ctx, nb = rt.seq, rt.ctx, rt.nb
    tq = 128 if latent else ctx
    nq = seq // tq if latent else 1
    ctx_blk0 = rt.n_lat // ctx
    q_row = (lambda b, i: b * nq + i) if latent else (lambda b, i: ctx_blk0 + b)
    kc = min(KEY_CHUNK, seq)

    def body(sink_ref, qa0, qa1, qb0, qb1, *rest):
        if latent:
            kal, val, kbl, vbl, kac, vac, kbc, vbc, do_ref, o_ref, lse_ref, dq_ref, dl_ref, dc_ref, dsink_ref = rest
        else:
            kac, vac, kbc, vbc, do_ref, o_ref, lse_ref, c1_ref, _, _, dq_ref, dc_ref, dsink_ref = rest
        b, qi = pl.program_id(0), pl.program_id(1)

        def rows_of(cols, kvi, mixer):
            dos = _stack_heads(do_ref[:, cols], kvi)
            delta = jnp.sum(dos.astype(F32) * _stack_heads(o_ref[:, cols], kvi).astype(F32), axis=1, keepdims=True)
            return dos, lse_ref[0, 2 * mixer + kvi, 0:1, :], _to_rows(delta)[0:1, :]

        @pl.when(jnp.logical_and(b == 0, qi == 0))
        def _():
            dsink_ref[...] = jnp.zeros_like(dsink_ref)

        if latent:
            @pl.when(qi == 0)
            def _():
                dc_ref[...] = jnp.zeros_like(dc_ref)
                dl_ref[...] = jnp.zeros_like(dl_ref)
        else:
            dc_ref[...] = c1_ref[...]

        head_row = lax.broadcasted_iota(jnp.int32, (8, 128), 0)
        for kvi, (qa, qb) in enumerate(((qa0, qb0), (qa1, qb1))):
            cols = slice(kvi * 256, (kvi + 1) * 256)
            dos, lse_row, delta_row = rows_of(cols, kvi, 0)
            src = _key_chunks(kac, vac, ctx)
            if latent:
                src += _key_chunks(kal, val, seq)
            dq4, grads = _softmax_bwd(_stack_heads(qa[...], kvi), dos, lse_row, delta_row, src)
            dq_ref[:, cols] = _unstack_heads(dq4, kvi)
            dc_ref[:, 0:128] += grads[0][0]
            dc_ref[:, 128:256] += grads[0][1]
            for c, (dk, dv) in enumerate(grads[1:]):
                dl_ref[c * kc:(c + 1) * kc, 0:128] += dk
                dl_ref[c * kc:(c + 1) * kc, 128:256] += dv
            cols = slice(512 + kvi * 256, 512 + (kvi + 1) * 256)
            dos, lse_row, delta_row = rows_of(cols, kvi, 1)
            src = _key_chunks(kbc, vbc, ctx)
            if latent:
                start, span = _band(qi, tq, seq)
                src.append((kbl[pl.ds(start, span), :], vbl[pl.ds(start, span), :], _band_mask(qi, tq, start, span, 1)))
            dq4, grads = _softmax_bwd(_stack_heads(qb[...], kvi), dos, lse_row, delta_row, src)
            dq_ref[:, cols] = _unstack_heads(dq4, kvi)
            dc_ref[:, 256:384] += grads[0][0]
            dc_ref[:, 384:512] += grads[0][1]
            if latent:
                dl_ref[pl.ds(start, span), 256:384] += grads[1][0]
                dl_ref[pl.ds(start, span), 384:512] += grads[1][1]
            sink_row = _per_head((1, GROUP * tq), 1, tq, [sink_ref[kvi * GROUP + g] for g in range(GROUP)])
            dsink = -jnp.exp(sink_row - lse_row) * delta_row
            head = lax.broadcasted_iota(jnp.int32, (1, GROUP * tq), 1) // tq
            upd = jnp.zeros((8, 128), F32)
            for g in range(GROUP):
                upd = jnp.where(head_row == kvi * GROUP + g, jnp.sum(jnp.where(head == g, dsink, 0.0)), upd)
            dsink_ref[...] += upd

    specs = _qkv_specs(rt, tq, q_row, lambda b: ctx_blk0 + b, latent)
    q_rows_spec = pl.BlockSpec((tq, 1024), lambda b, i: (q_row(b, i), 0))
    in_specs = ([pl.BlockSpec(memory_space=pltpu.SMEM)] + specs
                + [q_rows_spec, q_rows_spec, pl.BlockSpec((1, 4, 8, GROUP * tq), lambda b, i: (b * nq + i, 0, 0, 0))])
    args = [sink] + [qkvp] * len(specs) + [do, o, lse]
    dq_shape = jax.ShapeDtypeStruct((rt.rows, 1024), F32)
    dkv_shape = jax.ShapeDtypeStruct((rt.rows, 512), F32)
    dsink_spec, dsink_shape = pl.BlockSpec((8, 128), lambda b, i: (0, 0)), jax.ShapeDtypeStruct((8, 128), F32)
    dq_spec = pl.BlockSpec((tq, 1024), lambda b, i: (q_row(b, i), 0))
    if latent:
        out_specs = [dq_spec, pl.BlockSpec((seq, 512), lambda b, i: (b, 0)), pl.BlockSpec((ctx, 512), lambda b, i: (b, 0)), dsink_spec]
        out_shape = [dq_shape, dkv_shape, jax.ShapeDtypeStruct((rt.n_ctx, 512), F32), dsink_shape]
        aliases = {}
    else:
        dq_prev, dkv_prev, c1 = prev
        in_specs += [pl.BlockSpec((ctx, 512), lambda b, i: (b, 0)), pl.BlockSpec(memory_space=pl.ANY), pl.BlockSpec(memory_space=pl.ANY)]
        args += [c1, dq_prev, dkv_prev]
        out_specs = [dq_spec, pl.BlockSpec((ctx, 512), lambda b, i: (ctx_blk0 + b, 0)), dsink_spec]
        out_shape = [dq_shape, dkv_shape, dsink_shape]
        aliases = {len(args) - 2: 0, len(args) - 1: 1}
    return _comm_call(body, comm, name=name, grid=(nb, nq), in_specs=in_specs, out_specs=out_specs, out_shape=out_shape,
                      args=args, aliases=aliases, semantics=("arbitrary", "arbitrary"))


def _silu(x):
    return x / (1.0 + jnp.exp(-x))


def _whole(shape):
    return pl.BlockSpec(shape, lambda i, s: (0,) * len(shape))


def _ada_half_spec(cols):
    return pl.BlockSpec((DEPTH, D_MODEL, cols), lambda i, s: (0, 0, s[0]))


def _ada_fwd(cond, w_ada, b_half, c_idx, name):
    rows = cond.shape[0]
    cols = w_ada.shape[2] // 2

    def body(s_ref, c_ref, w_ref, b_ref, x_ref, o_ref):
        xs = _silu(c_ref[...]).astype(BF16)
        x_ref[...] = xs
        for l in range(DEPTH):
            o_ref[l] = jnp.dot(xs, w_ref[l].astype(BF16), preferred_element_type=F32) + b_ref[l]

    grid_spec = pltpu.PrefetchScalarGridSpec(
        num_scalar_prefetch=1, grid=(1,),
        in_specs=[_whole(cond.shape), _ada_half_spec(cols), _whole(b_half.shape)],
        out_specs=[_whole((rows, D_MODEL)), _whole((DEPTH, rows, cols))])
    return pl.pallas_call(
        body, name=name, grid_spec=grid_spec,
        out_shape=[jax.ShapeDtypeStruct((rows, D_MODEL), BF16), jax.ShapeDtypeStruct((DEPTH, rows, cols), F32)],
        compiler_params=_params(("arbitrary",)),
    )(c_idx, cond, w_ada, b_half)


def _ada_cond_bwd(dcx, w_ada, c_idx, name):
    _, rows, cols = dcx.shape

    def body(s_ref, d_ref, w_ref, o_ref):
        acc = jnp.zeros((rows, D_MODEL), F32)
        for l in range(DEPTH):
            acc = acc + lax.dot_general(d_ref[l], w_ref[l].astype(BF16), NT, preferred_element_type=F32)
        o_ref[...] = acc

    grid_spec = pltpu.PrefetchScalarGridSpec(
        num_scalar_prefetch=1, grid=(1,),
        in_specs=[_whole(dcx.shape), _ada_half_spec(cols)], out_specs=_whole((rows, D_MODEL)))
    return pl.pallas_call(body, name=name, grid_spec=grid_spec, out_shape=jax.ShapeDtypeStruct((rows, D_MODEL), F32),
                          compiler_params=_params(("arbitrary",)))(c_idx, dcx, w_ada)


def _dev_sum(x, name):
    _, r, c = x.shape

    def body(x_ref, o_ref):
        v = x_ref[0]
        for d in range(1, N_DEV):
            v = v + x_ref[d]
        o_ref[...] = v

    return pl.pallas_call(body, name=name, out_shape=jax.ShapeDtypeStruct((r, c), F32))(x)


def _adam_val(w, g, m, v):
    c1 = 1.0 / (1.0 - ADAM_B1 ** ADAM_STEP)
    c2 = 1.0 / (1.0 - ADAM_B2 ** ADAM_STEP)
    nm = ADAM_B1 * m + (1.0 - ADAM_B1) * g
    nv = ADAM_B2 * v + (1.0 - ADAM_B2) * (g * g)
    return -ADAM_LR * ((nm * c1) / (jnp.sqrt(nv * c2) + ADAM_EPS) + ADAM_WD * w), nm, nv


def _small_update(tot, dcc_parts, params, n_groups, name):
    n_p = len(params)
    mod_rows = n_groups * N_MOD

    def body(tot_ref, dcc_ref, *refs):
        ins, outs = refs[:3 * n_p], refs[3 * n_p:]

        def update(p, rows, cols, g):
            w_ref, m_ref, v_ref = ins[3 * p:3 * p + 3]
            g_ref, d_ref, nm_ref, nv_ref = outs[4 * p:4 * p + 4]
            d, nm, nv = _adam_val(w_ref[rows, cols], g, m_ref[rows, cols], v_ref[rows, cols])
            g_ref[rows, cols] = g
            d_ref[rows, cols] = d
            nm_ref[rows, cols] = nm
            nv_ref[rows, cols] = nv

        acc = dcc_ref[0, 0:1, :]
        for d in range(1, N_DEV):
            acc = acc + dcc_ref[d, 0:1, :]
        c = ins[0][...]
        sg = 1.0 / (1.0 + jnp.exp(-c))
        update(0, slice(0, 1), slice(None), acc * (sg * (1.0 + c * (1.0 - sg))))
        for l in range(DEPTH):
            for i in range(N_MOD):
                g = tot_ref[l * mod_rows + i:l * mod_rows + i + 1, :]
                for grp in range(1, n_groups):
                    g = g + tot_ref[l * mod_rows + grp * N_MOD + i:l * mod_rows + grp * N_MOD + i + 1, :]
                update(1, slice(l, l + 1), slice(i * D_MODEL, (i + 1) * D_MODEL), g)
            for j in range(4):
                row = DEPTH * mod_rows + 4 * l + j
                update(2 + j, slice(l, l + 1), slice(None), tot_ref[row:row + 1, :])

    shapes = [jax.ShapeDtypeStruct(w.shape, F32) for w, _, _ in params for _ in range(4)]
    outs = pl.pallas_call(body, name=name, out_shape=shapes)(tot, dcc_parts, *[a for p in params for a in p])
    return [tuple(outs[4 * p:4 * p + 4]) for p in range(n_p)]


def _adamw(w, g, m, v, name):
    r, c = w.shape
    tr = _pick(r, (256, 128, 64, 32, 24, 16, 8))

    def body(w_ref, g_ref, m_ref, v_ref, d_ref, nm_ref, nv_ref):
        d_ref[...], nm_ref[...], nv_ref[...] = _adam_val(w_ref[...], g_ref[...], m_ref[...], v_ref[...])

    spec = pl.BlockSpec((tr, c), lambda i: (i, 0))
    return pl.pallas_call(body, name=name, grid=(r // tr,), in_specs=[spec] * 4, out_specs=[spec] * 3,
                          out_shape=[jax.ShapeDtypeStruct((r, c), F32)] * 3, compiler_params=_params(("parallel",)))(w, g, m, v)


SMALL_ROWS = 48


def _small_rows(small, sq):
    def lane_pad(v):
        return jnp.pad(v, (0, D_MODEL - v.shape[0]))[None]

    head_rows = [lane_pad(jnp.concatenate([s["q_norm"][0], s["k_norm"][0], s["sink"]])) for s in small]
    loss_row = lane_pad((0.5 / D_MODEL) * jnp.sum(sq, keepdims=True)[0])
    rows = jnp.concatenate([s["mod"].reshape(-1, D_MODEL) for s in small] + [s["gammas"] for s in small] + head_rows + [loss_row], axis=0)
    return jnp.pad(rows, ((0, SMALL_ROWS - rows.shape[0]), (0, 0)))


def _local_step(x, ctx, target, mods, gam, qn, kn, sink, w_first, w_layers, packed, c_idx, k_idx):
    nb, seq, _ = x.shape
    rt = _Rows(nb, seq, ctx.shape[1])
    tables = _rope_tables(rt)
    fuse = packed is not None
    h = jnp.concatenate([x.reshape(rt.n_lat, D_MODEL), ctx.reshape(rt.n_ctx, D_MODEL)], axis=0)
    wg = [{}, {}] if fuse else [dict(w) for w in w_layers]
    wg[0]["in"] = (w_first, 0)
    if fuse:
        wg[0]["in_own"] = (packed, W_FIRST[0])
    saved = []
    for l in range(DEPTH):
        g_pre_mix, g_post_mix, g_pre_mlp, g_post_mlp = gam[l]
        u, qkv, qkvp = _in_fwd(rt, h, g_pre_mix, mods[l], wg[l], tables, qn[l], kn[l], f"in_fwd{l}")
        if fuse and l == 0:
            o, lse_lat, w_mlp0, w_out0, w_mix1 = _attn_fwd(rt, qkvp, sink[l], None, f"attn_lat_fwd{l}",
                                                          comm=_gather_comm(packed, [W_MLP0, W_OUT0, W_MIX1]))
            wg[0].update({kind: (w_mlp0, PACK_OFF[(kind, 0)] - W_MLP0[0]) for kind in ("up", "down")})
            wg[0]["out"] = (w_out0, 0)
            wg[1] = {kind: (w_mix1, PACK_OFF[(kind, 1)] - W_MIX1[0]) for kind in ("out", "in")}
        else:
            o, lse_lat = _attn_fwd(rt, qkvp, sink[l], None, f"attn_lat_fwd{l}")
        o, lse_ctx = _attn_fwd(rt, qkvp, sink[l], o, f"attn_ctx_fwd{l}")
        mix, h1, u2 = _out_fwd(rt, o, wg[l], h, mods[l], g_post_mix, g_pre_mlp, f"out_fwd{l}")
        if fuse and l == 0:
            r, y, h2, w_mlp1 = _mlp_fwd(rt, u2, h1, wg[l], mods[l], g_post_mlp, f"mlp_fwd{l}", comm=_gather_comm(packed, [W_MLP1]))
            wg[1].update({kind: (w_mlp1, PACK_OFF[(kind, 1)] - W_MLP1[0]) for kind in ("up", "down")})
        else:
            r, y, h2 = _mlp_fwd(rt, u2, h1, wg[l], mods[l], g_post_mlp, f"mlp_fwd{l}")
        saved.append((h, u, qkv, qkvp, o, lse_lat, lse_ctx, mix, h1, u2, r, y))
        h = h2

    dh, sq = _loss_grad(rt, h, target.reshape(rt.n_lat, D_MODEL), "loss_grad")

    small = [None] * DEPTH
    groups = {}
    for l in reversed(range(DEPTH)):
        g_pre_mix, g_post_mix, g_pre_mlp, g_post_mlp = gam[l]
        h0, u, qkv, qkvp, o, lse_lat, lse_ctx, mix, h1, u2, r, y = saved[l]
        mlp_group, mix_group = (G_LAYER1, G_LAYER1) if l == 1 else (G_MLP0, G_MIX0)
        hide = fuse and l == 0

        outs = _mlp_down_bwd(rt, dh, y, r, wg[l], mods[l], g_post_mlp, f"mlp_down_bwd{l}",
                             comm=_pair_comm(groups[G_LAYER1]) if hide else None)
        dy, da, d_gate_m, d_g_post_mlp = outs[:4]
        if hide:
            sum1 = _pair_sum(groups[G_LAYER1], outs[4], c_idx, "grad_pair_sum_layer1")
        p_mlp = _wgrad_packed(rt, r, dy, "down", PACK_OFF[("down", l)] - mlp_group[0], mlp_group[1], None, f"mlp_down_wgrad{l}")
        dh1, d_sh_m, d_sc_m, d_g_pre_mlp = _mlp_up_bwd(rt, da, wg[l], h1, dh, mods[l], g_pre_mlp, f"mlp_up_bwd{l}")
        p_mlp = _wgrad_packed(rt, u2, da, "up", PACK_OFF[("up", l)] - mlp_group[0], mlp_group[1], p_mlp, f"mlp_up_wgrad{l}")
        outs = _out_bwd(rt, dh1, mix, wg[l], mods[l], g_post_mix, f"out_bwd{l}", comm=_pair_comm(p_mlp) if hide else None)
        dmix, do, d_gate_a, d_g_post_mix = outs[:4]
        if hide:
            sum0 = _pair_sum(p_mlp, outs[4], c_idx, "grad_pair_sum_mlp0")
        p_mix = _wgrad_packed(rt, o, dmix, "out", PACK_OFF[("out", l)] - mix_group[0], mix_group[1],
                              p_mlp if l == 1 else None, f"out_wgrad{l}")
        outs = _attn_bwd(rt, qkvp, o, lse_lat, do, sink[l], None, f"attn_lat_bwd{l}",
                         comm=_chip_comm([sum1[1], sum0[1]]) if hide else None)
        dq, dkv, dkv_c, dsink1 = outs[:4]
        if hide:
            groups[G_LAYER1] = _owner_sum(sum1[0], outs[4], k_idx, "grad_owner_sum_layer1")
            groups[G_MLP0] = _owner_sum(sum0[0], outs[5], k_idx, "grad_owner_sum_mlp0")
        dq, dkv, dsink2 = _attn_bwd(rt, qkvp, o, lse_ctx, do, sink[l], (dq, dkv, dkv_c), f"attn_ctx_bwd{l}")
        dqkv, dh, dqn, dkn, d_sh_a, d_sc_a, d_g_pre_mix = _in_bwd(rt, dq, dkv, qkv, tables, qn[l], kn[l], wg[l], h0, dh1, mods[l],
                                                                  g_pre_mix, l == 0, f"in_bwd{l}")
        dmod = jnp.concatenate([d_sh_a, d_sc_a, d_gate_a, d_sh_m, d_sc_m, d_gate_m], axis=1)
        small[l] = dict(mod=dmod, gammas=jnp.concatenate([d_g_pre_mix, d_g_post_mix, d_g_pre_mlp, d_g_post_mlp], axis=0),
                        q_norm=dqn, k_norm=dkn, sink=(dsink1 + dsink2)[:, 0])
        gather = _gather_comm(_small_rows(small, sq), [(0, SMALL_ROWS)]) if hide else None
        outs = _wgrad_packed(rt, u, dqkv, "in", PACK_OFF[("in", l)] - mix_group[0], mix_group[1], p_mix, f"in_wgrad{l}", comm=gather)
        groups[mix_group], small_g = outs if hide else (outs, None)
        if not hide and l == 0:
            groups[G_MLP0] = p_mlp
    return sq, dh.reshape(nb, seq, D_MODEL), [groups[G_LAYER1], groups[G_MLP0], groups[G_MIX0]], small, small_g


def kernel(x, c, ctx, c_ctx, w_ada, b_ada, g_pre_mix, g_post_mix, g_pre_mlp, g_post_mlp, w_in, q_norm, k_norm, sink, w_out, w_up, w_down, loss_target, m_c_ctx, m_w_ada, m_b_ada, m_g_pre_mix, m_g_post_mix, m_g_pre_mlp, m_g_post_mlp, m_w_in, m_q_norm, m_k_norm, m_sink, m_w_out, m_w_up, m_w_down, v_c_ctx, v_w_ada, v_b_ada, v_g_pre_mix, v_g_post_mix, v_g_pre_mlp, v_g_post_mlp, v_w_in, v_q_norm, v_k_norm, v_sink, v_w_out, v_w_up, v_w_down):
    nb = x.shape[0]
    ix, iy, ic = lax.axis_index("x"), lax.axis_index("y"), lax.axis_index("c")
    chip = 2 * ix + iy
    dev = 2 * chip + ic
    ada_cols = w_ada.shape[2] // 2

    c_all = _all_gather(c.reshape(8, (nb * D_MODEL) // 8), "gather_c", False).reshape(N_DEV * nb, D_MODEL)
    n_cond = N_DEV * nb + 1
    cond_rows = 16 * ((n_cond + 15) // 16)
    cond = jnp.concatenate([c_all, c_ctx[None, :], jnp.zeros((cond_rows - n_cond, D_MODEL), F32)], axis=0)
    c_idx = ic.reshape(1).astype(jnp.int32)
    kc_idx = jnp.stack([chip, ic]).astype(jnp.int32)
    b_ada_half = lax.dynamic_slice_in_dim(b_ada, dev * ada_cols, ada_cols, 1)[:, None, :]
    x_ada, mod_part = _ada_fwd(cond, w_ada, b_ada_half, c_idx, "ada_fwd")
    mod_g = _all_gather(mod_part.reshape(DEPTH * cond_rows, ada_cols), "gather_mod", False)
    mod_all = mod_g.reshape(N_DEV, DEPTH, cond_rows, ada_cols).transpose(1, 2, 0, 3).reshape(DEPTH, cond_rows, N_MOD * D_MODEL)
    mods = []
    for l in range(DEPTH):
        mine = lax.dynamic_slice_in_dim(mod_all[l], dev * nb, nb, 0)
        mods.append(jnp.concatenate([mine, mod_all[l, n_cond - 1:n_cond]], axis=0).reshape(nb + 1, N_MOD, D_MODEL))

    packed = _pack_local_half(w_in, w_out, w_up, w_down, ic)
    w_first, = _comm_alone(_gather_comm(packed, [W_FIRST], copy_own=False), "gather_w_first")

    gam = [(g_pre_mix[l][None], g_post_mix[l][None], g_pre_mlp[l][None], g_post_mlp[l][None]) for l in range(DEPTH)]
    qn = [jnp.tile(q_norm[l], 2)[None] for l in range(DEPTH)]
    kn = [jnp.tile(k_norm[l], 2)[None] for l in range(DEPTH)]
    _, grad_x, (h_layer1, h_mlp0, p_mix0), _, small_g = _local_step(x, ctx, loss_target, mods, gam, qn, kn, [sink[l] for l in range(DEPTH)],
                                                                 w_first, None, packed, c_idx, kc_idx)

    def step(w, g, m, v, name):
        shape = w.shape
        cols = shape[-1]
        outs = _adamw(w.reshape(-1, cols), g.reshape(-1, cols), m.reshape(-1, cols), v.reshape(-1, cols), name)
        return tuple(a.reshape(shape) for a in outs)

    def piece(halves, kind, l, group):
        o = PACK_OFF[(kind, l)] - group[0]
        rows = halves[:, o:o + PACK_HEIGHT[kind]]
        if kind == "in":
            rows = rows[:, :, :2 * IN_PIECE_COLS].reshape(2, 256, 2, IN_PIECE_COLS).transpose(0, 2, 1, 3)
            return rows.reshape(1024, IN_PIECE_COLS)
        return rows.reshape(2 * PACK_HEIGHT[kind], 1024)

    r1, = _comm_alone(_pair_comm(p_mix0), "grad_pair_exchange_mix0")
    a32, a16 = _pair_sum(p_mix0, r1, c_idx, "grad_pair_sum_mix0")
    r2, = _comm_alone(_chip_comm([a16]), "grad_chip_exchange_mix0")
    h_mix0 = _owner_sum(a32, r2, kc_idx, "grad_owner_sum_mix0")
    h_layer1, h_mlp0, h_mix0 = _comm_alone(_halves_comm([h_layer1, h_mlp0, h_mix0]), "grad_halves_exchange")
    grad_w_up = jnp.stack([piece(h_mlp0, "up", 0, G_MLP0), piece(h_layer1, "up", 1, G_LAYER1)])
    grad_w_down = jnp.stack([piece(h_mlp0, "down", 0, G_MLP0), piece(h_layer1, "down", 1, G_LAYER1)])
    grad_w_in = jnp.stack([piece(h_mix0, "in", 0, G_MIX0), piece(h_layer1, "in", 1, G_LAYER1)])
    grad_w_out = jnp.stack([piece(h_mix0, "out", 0, G_MIX0), piece(h_layer1, "out", 1, G_LAYER1)])

    tot = _dev_sum(small_g, "small_sum")
    mod_rows = (nb + 1) * N_MOD
    o_head = DEPTH * mod_rows + 4 * DEPTH
    loss = tot[o_head + DEPTH, 0]
    grad_q_norm = tot[o_head:o_head + DEPTH, 0:64] + tot[o_head:o_head + DEPTH, 64:128]
    grad_k_norm = tot[o_head:o_head + DEPTH, 128:192] + tot[o_head:o_head + DEPTH, 192:256]
    grad_sink = tot[o_head:o_head + DEPTH, 256:264]

    ex = small_g[:, :DEPTH * mod_rows].reshape(N_DEV, DEPTH, nb + 1, N_MOD * D_MODEL)[:, :, :nb]
    ex = ex.transpose(1, 0, 2, 3).reshape(DEPTH, N_DEV * nb, N_MOD * D_MODEL)
    cx = tot[:DEPTH * mod_rows].reshape(DEPTH, nb + 1, N_MOD * D_MODEL)[:, nb:]
    dm = jnp.concatenate([ex, cx, jnp.zeros((DEPTH, cond_rows - n_cond, N_MOD * D_MODEL), F32)], axis=1)
    shard_cols = w_ada.shape[2]
    grad_w_ada = _ada_wgrad(x_ada, lax.dynamic_slice_in_dim(dm, chip * shard_cols, shard_cols, 2).astype(BF16), "ada_wgrad")
    dcx = jnp.pad(lax.dynamic_slice_in_dim(cx, dev * ada_cols, ada_cols, 2), ((0, 0), (0, 15), (0, 0))).astype(BF16)
    dcc = _ada_cond_bwd(dcx, w_ada, c_idx, "ada_cond_bwd")[0:8]
    dcc_g = _all_gather(dcc, "gather_cond_grad", False).reshape(N_DEV, 8, D_MODEL)

    dense_names = ["c_ctx", "b_ada", "g_pre_mix", "g_post_mix", "g_pre_mlp", "g_post_mlp"]
    dense = _small_update(tot, dcc_g, [(c_ctx[None], m_c_ctx[None], v_c_ctx[None]), (b_ada, m_b_ada, v_b_ada),
                                       (g_pre_mix, m_g_pre_mix, v_g_pre_mix), (g_post_mix, m_g_post_mix, v_g_post_mix),
                                       (g_pre_mlp, m_g_pre_mlp, v_g_pre_mlp), (g_post_mlp, m_g_post_mlp, v_g_post_mlp)],
                          nb + 1, "small_update")
    res = {n: r for n, r in zip(dense_names, dense)}
    res["c_ctx"] = tuple(a[0] for a in res["c_ctx"])
    small_names = ["q_norm", "k_norm", "sink"]
    small_w = [q_norm, k_norm, sink]
    small_gr = [grad_q_norm, grad_k_norm, grad_sink]
    small_m = [m_q_norm, m_k_norm, m_sink]
    small_v = [v_q_norm, v_k_norm, v_sink]
    sizes = [int(np.prod(w.shape)) for w in small_w]
    total = sum(sizes)
    flat_rows = 8 * ((total + 8 * D_MODEL - 1) // (8 * D_MODEL))

    def flat(arrs, fill):
        f = jnp.concatenate([a.reshape(-1) for a in arrs])
        return jnp.concatenate([f, jnp.full((flat_rows * D_MODEL - total,), fill, F32)]).reshape(flat_rows, D_MODEL)

    sd, snm, snv = _adamw(flat(small_w, 0.0), flat(small_gr, 0.0), flat(small_m, 0.0), flat(small_v, 1.0), "adamw_small")[:3]

    def unflat(f):
        f = f.reshape(-1)
        out, off = [], 0
        for w, n in zip(small_w, sizes):
            out.append(f[off:off + n].reshape(w.shape))
            off += n
        return out

    small_d, small_nm, small_nv = unflat(sd), unflat(snm), unflat(snv)
    res.update({n: (g, d, nm, nv) for n, g, d, nm, nv in zip(small_names, small_gr, small_d, small_nm, small_nv)})
    res["w_ada"] = (grad_w_ada, *step(w_ada, grad_w_ada, m_w_ada, v_w_ada, "adamw_w_ada"))
    res["w_in"] = (grad_w_in, *step(w_in, grad_w_in, m_w_in, v_w_in, "adamw_w_in"))
    res["w_out"] = (grad_w_out, *step(w_out, grad_w_out, m_w_out, v_w_out, "adamw_w_out"))
    res["w_up"] = (grad_w_up, *step(w_up, grad_w_up, m_w_up, v_w_up, "adamw_w_up"))
    res["w_down"] = (grad_w_down, *step(w_down, grad_w_down, m_w_down, v_w_down, "adamw_w_down"))

    order = ["c_ctx", "w_ada", "b_ada", "g_pre_mix", "g_post_mix", "g_pre_mlp", "g_post_mlp", "w_in", "q_norm", "k_norm", "sink", "w_out", "w_up", "w_down"]
    return (loss, grad_x, *[res[n][0] for n in order], *[res[n][1] for n in order],
            *[res[n][2] for n in order], *[res[n][3] for n in order])
```

```python
import functools

import jax
import jax.numpy as jnp
import numpy as np
from jax import lax
from jax.experimental import pallas as pl
from jax.experimental.pallas import tpu as pltpu

F32 = jnp.float32
BF16 = jnp.bfloat16

D_MODEL = 1024
HEAD_DIM = 64
GROUP = 4
WINDOW = 128
N_MOD = 6
D_FF = 4 * D_MODEL
IN_COLS = 1536
GRID_W = 64
ROPE_THETA = 10000.0
EPS = 1e-6
NEG_BIG = -1e30
Q_SCALE = HEAD_DIM ** -0.5
DEPTH = 2
N_DEV = 8

ADAM_LR = 0.001
ADAM_B1 = 0.9
ADAM_B2 = 0.999
ADAM_EPS = 1e-08
ADAM_WD = 0.01
ADAM_STEP = 10

V7X_VMEM_BYTES = 64 * 1024 * 1024
VMEM_LIMIT = V7X_VMEM_BYTES - 8 * 1024 * 1024

MESH = pl.DeviceIdType.MESH
NT = (((1,), (1,)), ((), ()))
TN = (((0,), (0,)), ((), ()))

COL_KA, COL_VA, COL_KB, COL_VB = 4, 5, 10, 11
NORMED_COLS = 640

PACK_HEIGHT = {"up": 512, "down": 512, "in": 256, "out": 128}
IN_PIECE_COLS = 384
PACK_OFF = {("up", 0): 0, ("down", 0): 512, ("in", 0): 1024, ("out", 0): 1280,
            ("up", 1): 1408, ("down", 1): 1920, ("in", 1): 2432, ("out", 1): 2688}
PACK_ROWS = 2816
W_FIRST, W_MLP0, W_OUT0, W_MLP1, W_MIX1 = (1024, 256), (0, 1024), (1280, 128), (1408, 1024), (2432, 384)
G_LAYER1, G_MLP0, G_MIX0 = (1408, 1408), (0, 1024), (1024, 384)


def _pick(n, cands):
    for t in cands:
        if n % t == 0:
            return t
    raise ValueError(f"no tile for {n}")


def _params(sem):
    return pltpu.CompilerParams(dimension_semantics=sem, vmem_limit_bytes=VMEM_LIMIT)


def _all_gather(x, name, in_hbm):
    m_per, n = x.shape
    space = pl.ANY if in_hbm else pltpu.VMEM

    def body(x_ref, out_ref, send_sems, recv_sems, local_sem):
        x_, y_, c_ = lax.axis_index("x"), lax.axis_index("y"), lax.axis_index("c")
        me, sibling = (x_, y_, c_), (x_, y_, 1 - c_)
        chips = [(1 - x_, y_), (x_, 1 - y_), (1 - x_, 1 - y_)]

        def rows(px, py, pc):
            return out_ref.at[pl.ds((4 * px + 2 * py + pc) * m_per, m_per), :]

        def copy(k, block, to, src=None):
            return pltpu.make_async_remote_copy(
                src_ref=rows(*block) if src is None else src, dst_ref=rows(*block),
                send_sem=send_sems.at[k], recv_sem=recv_sems.at[k], device_id=to, device_id_type=MESH)

        mine = pltpu.make_async_copy(x_ref, rows(*me), local_sem)
        mine.start()
        first = [copy(0, me, sibling, src=x_ref)]
        first += [copy(1 + j, me, (*chip, c_), src=x_ref) for j, chip in enumerate(chips)]
        for cp in first:
            cp.start()
        passed = [copy(4 + j, (*chip, c_), sibling) for j, chip in enumerate(chips)]
        for j, chip in enumerate(chips):
            copy(1 + j, (*chip, c_), me).wait_recv()
            passed[j].start()
        copy(0, sibling, me).wait_recv()
        for j, chip in enumerate(chips):
            copy(4 + j, (*chip, 1 - c_), me).wait_recv()
        for cp in first + passed:
            cp.wait_send()
        mine.wait()

    return pl.pallas_call(
        body, name=name,
        out_shape=jax.ShapeDtypeStruct((N_DEV * m_per, n), x.dtype),
        in_specs=[pl.BlockSpec(memory_space=space)],
        out_specs=pl.BlockSpec(memory_space=space),
        scratch_shapes=[pltpu.SemaphoreType.DMA((7,)), pltpu.SemaphoreType.DMA((7,)), pltpu.SemaphoreType.DMA],
    )(x)


class _Comm:
    def __init__(self, inputs, out_shapes, aliases, n_send, n_recv, start, finish):
        self.inputs, self.out_shapes, self.aliases = list(inputs), list(out_shapes), dict(aliases)
        self.n_send, self.n_recv, self.start, self.finish = n_send, n_recv, start, finish


def _comm_call(compute, comm, *, name, grid, in_specs, out_specs, out_shape, args, aliases, semantics, scratch=()):
    in_specs, out_specs, out_shape, args, aliases = list(in_specs), list(out_specs), list(out_shape), list(args), dict(aliases)
    scratch = list(scratch)
    if comm is None:
        return pl.pallas_call(compute, name=name, grid=grid, in_specs=in_specs, out_specs=out_specs, out_shape=out_shape,
                              input_output_aliases=aliases, scratch_shapes=scratch, compiler_params=_params(semantics))(*args)
    n_in, n_out, n_ci, n_co = len(args), len(out_shape), len(comm.inputs), len(comm.out_shapes)
    hbm = pl.BlockSpec(memory_space=pl.ANY)
    aliases.update({n_in + i: n_out + o for i, o in comm.aliases.items()})

    def body(*refs):
        ins, c_ins = refs[:n_in], refs[n_in:n_in + n_ci]
        outs, c_outs = refs[n_in + n_ci:n_in + n_ci + n_out], refs[n_in + n_ci + n_out:n_in + n_ci + n_out + n_co]
        scr = refs[n_in + n_ci + n_out + n_co:-2]
        send_sems, recv_sems = refs[-2:]
        ids = [pl.program_id(a) for a in range(len(grid))]
        first = functools.reduce(jnp.logical_and, [i == 0 for i in ids])
        last = functools.reduce(jnp.logical_and, [i == g - 1 for i, g in zip(ids, grid)])

        @pl.when(first)
        def _():
            comm.start(c_ins, c_outs, send_sems, recv_sems)

        compute(*ins, *outs, *scr)

        @pl.when(last)
        def _():
            comm.finish(c_ins, c_outs, send_sems, recv_sems)

    return pl.pallas_call(
        body, name=name, grid=grid,
        in_specs=in_specs + [hbm] * n_ci, out_specs=out_specs + [hbm] * n_co, out_shape=out_shape + comm.out_shapes,
        input_output_aliases=aliases,
        scratch_shapes=scratch + [pltpu.SemaphoreType.DMA((comm.n_send,)), pltpu.SemaphoreType.DMA((comm.n_recv,))],
        compiler_params=_params(("arbitrary",) * len(grid)),
    )(*args, *comm.inputs)


def _place():
    x_, y_, c_ = lax.axis_index("x"), lax.axis_index("y"), lax.axis_index("c")
    return x_, y_, c_, [(1 - x_, y_), (x_, 1 - y_), (1 - x_, 1 - y_)]


GATHER_SENDS, GATHER_RECVS = 8, 7


def _gather_copies(packed_ref, wg_ref, send_sems, recv_sems, rows, nth=0):
    r0, n = rows
    x_, y_, c_, chips = _place()
    me, sibling = (x_, y_, c_), (x_, y_, 1 - c_)
    src = packed_ref.at[pl.ds(r0, n), :]

    def slot(px, py, pc):
        return wg_ref.at[4 * px + 2 * py + pc]

    def copy(k, block, to, from_packed=False):
        return pltpu.make_async_remote_copy(src_ref=src if from_packed else slot(*block), dst_ref=slot(*block),
                                            send_sem=send_sems.at[GATHER_SENDS * nth + k], recv_sem=recv_sems.at[GATHER_RECVS * nth + k],
                                            device_id=to, device_id_type=MESH)

    own = [copy(0, me, sibling, True)] + [copy(1 + j, me, (*chip, c_), True) for j, chip in enumerate(chips)]
    passed = [copy(4 + j, (*chip, c_), sibling) for j, chip in enumerate(chips)]
    over_ici = [copy(1 + j, (*chip, c_), me) for j, chip in enumerate(chips)]
    from_sibling = [copy(0, sibling, me)] + [copy(4 + j, (*chip, 1 - c_), me) for j, chip in enumerate(chips)]
    mine = pltpu.make_async_copy(src, slot(*me), send_sems.at[GATHER_SENDS * nth + 7])
    return mine, own, passed, over_ici, from_sibling


def _gather_start(packed_ref, wg_ref, send_sems, recv_sems, rows, nth=0, copy_own=True):
    mine, own, _, _, _ = _gather_copies(packed_ref, wg_ref, send_sems, recv_sems, rows, nth)
    if copy_own:
        mine.start()
    for cp in own:
        cp.start()


def _gather_finish(packed_ref, wg_ref, send_sems, recv_sems, rows, nth=0, copy_own=True):
    mine, own, passed, over_ici, from_sibling = _gather_copies(packed_ref, wg_ref, send_sems, recv_sems, rows, nth)
    for arrived, onward in zip(over_ici, passed):
        arrived.wait_recv()
        onward.start()
    for arrived in from_sibling:
        arrived.wait_recv()
    for cp in own + passed:
        cp.wait_send()
    if copy_own:
        mine.wait()


def _gather_comm(packed, ranges, copy_own=True):
    shapes = [jax.ShapeDtypeStruct((N_DEV, n, packed.shape[1]), packed.dtype) for _, n in ranges]

    def start(ins, outs, ss, rs):
        for nth, rows in enumerate(ranges):
            _gather_start(ins[0], outs[nth], ss, rs, rows, nth, copy_own)

    def finish(ins, outs, ss, rs):
        for nth, rows in enumerate(ranges):
            _gather_finish(ins[0], outs[nth], ss, rs, rows, nth, copy_own)

    return _Comm([packed], shapes, {}, GATHER_SENDS * len(ranges), GATHER_RECVS * len(ranges), start, finish)


def _pair_copy(p_ref, out_ref, send_sems, recv_sems):
    x_, y_, c_, _ = _place()
    return pltpu.make_async_remote_copy(src_ref=p_ref.at[1 - c_], dst_ref=out_ref,
                                        send_sem=send_sems.at[0], recv_sem=recv_sems.at[0],
                                        device_id=(x_, y_, 1 - c_), device_id_type=MESH)


def _pair_comm(p):
    return _Comm([p], [jax.ShapeDtypeStruct(p.shape[1:], p.dtype)], {}, 1, 1,
                 lambda ins, outs, ss, rs: _pair_copy(ins[0], outs[0], ss, rs).start(),
                 lambda ins, outs, ss, rs: _pair_copy(ins[0], outs[0], ss, rs).wait())


def _chip_copies(a_refs, out_refs, send_sems, recv_sems):
    _, _, c_, chips = _place()
    return [pltpu.make_async_remote_copy(src_ref=a_ref.at[2 * tx + ty], dst_ref=o_ref.at[j],
                                         send_sem=send_sems.at[3 * g + j], recv_sem=recv_sems.at[3 * g + j],
                                         device_id=(tx, ty, c_), device_id_type=MESH)
            for g, (a_ref, o_ref) in enumerate(zip(a_refs, out_refs)) for j, (tx, ty) in enumerate(chips)]


def _chip_start(a_refs, out_refs, send_sems, recv_sems):
    for cp in _chip_copies(a_refs, out_refs, send_sems, recv_sems):
        cp.start()


def _chip_finish(a_refs, out_refs, send_sems, recv_sems):
    for cp in _chip_copies(a_refs, out_refs, send_sems, recv_sems):
        cp.wait()


def _chip_comm(arrays):
    shapes = [jax.ShapeDtypeStruct((3,) + a.shape[1:], a.dtype) for a in arrays]
    return _Comm(arrays, shapes, {}, 3 * len(arrays), 3 * len(arrays), _chip_start, _chip_finish)


def _halves_copies(in_refs, out_refs, send_sems, recv_sems):
    x_, y_, c_, _ = _place()
    return [pltpu.make_async_remote_copy(src_ref=o_ref.at[c_], dst_ref=o_ref.at[c_], send_sem=send_sems.at[i], recv_sem=recv_sems.at[i],
                                         device_id=(x_, y_, 1 - c_), device_id_type=MESH)
            for i, o_ref in enumerate(out_refs)]


def _halves_start(in_refs, out_refs, send_sems, recv_sems):
    for cp in _halves_copies(in_refs, out_refs, send_sems, recv_sems):
        cp.start()


def _halves_finish(in_refs, out_refs, send_sems, recv_sems):
    for cp in _halves_copies(in_refs, out_refs, send_sems, recv_sems):
        cp.wait()


def _halves_comm(arrays):
    shapes = [jax.ShapeDtypeStruct(a.shape, a.dtype) for a in arrays]
    return _Comm(arrays, shapes, {i: i for i in range(len(arrays))}, len(arrays), len(arrays), _halves_start, _halves_finish)


class _SemSlice:
    class _At:
        def __init__(self, sems, first):
            self.sems, self.first = sems, first

        def __getitem__(self, k):
            return self.sems.at[self.first + k]

    def __init__(self, sems, first):
        self.at = _SemSlice._At(sems, first)


def _merge(comms):
    inputs = [a for c in comms for a in c.inputs]
    shapes = [s for c in comms for s in c.out_shapes]
    aliases, spans = {}, []
    i0 = o0 = s0 = r0 = 0
    for c in comms:
        aliases.update({i0 + i: o0 + o for i, o in c.aliases.items()})
        spans.append((slice(i0, i0 + len(c.inputs)), slice(o0, o0 + len(c.out_shapes)), s0, r0))
        i0, o0, s0, r0 = i0 + len(c.inputs), o0 + len(c.out_shapes), s0 + c.n_send, r0 + c.n_recv

    def start(ins, outs, ss, rs):
        for c, (i, o, s, r) in zip(comms, spans):
            c.start(ins[i], outs[o], _SemSlice(ss, s), _SemSlice(rs, r))

    def finish(ins, outs, ss, rs):
        for c, (i, o, s, r) in zip(comms, spans):
            c.finish(ins[i], outs[o], _SemSlice(ss, s), _SemSlice(rs, r))

    return _Comm(inputs, shapes, aliases, s0, r0, start, finish)


def _comm_alone(comm, name):
    n_ci = len(comm.inputs)
    hbm = pl.BlockSpec(memory_space=pl.ANY)

    def body(*refs):
        c_ins, c_outs, send_sems, recv_sems = refs[:n_ci], refs[n_ci:-2], refs[-2], refs[-1]
        comm.start(c_ins, c_outs, send_sems, recv_sems)
        comm.finish(c_ins, c_outs, send_sems, recv_sems)

    return pl.pallas_call(
        body, name=name, out_shape=comm.out_shapes, in_specs=[hbm] * n_ci, out_specs=[hbm] * len(comm.out_shapes),
        input_output_aliases=comm.aliases,
        scratch_shapes=[pltpu.SemaphoreType.DMA((comm.n_send,)), pltpu.SemaphoreType.DMA((comm.n_recv,))],
    )(*comm.inputs)


SUM_TILES = (704, 512, 384, 320, 256, 192, 128, 64)


def _pair_sum(p, r1, kc_idx, name):
    _, _, n, c = p.shape
    tr = _pick(n, SUM_TILES)

    def body(s_ref, p_ref, r_ref, o32_ref, o16_ref):
        v = p_ref[...] + r_ref[...]
        o16_ref[...] = v.astype(BF16)

        @pl.when(pl.program_id(1) == s_ref[0])
        def _():
            o32_ref[...] = v

    blk = pl.BlockSpec((None, tr, c), lambda i, j, s: (j, i, 0))
    grid_spec = pltpu.PrefetchScalarGridSpec(
        num_scalar_prefetch=1, grid=(n // tr, 4),
        in_specs=[pl.BlockSpec((None, None, tr, c), lambda i, j, s: (s[1], j, i, 0)), blk],
        out_specs=[pl.BlockSpec((tr, c), lambda i, j, s: (i, 0)), blk])
    return pl.pallas_call(
        body, name=name, grid_spec=grid_spec,
        out_shape=[jax.ShapeDtypeStruct((n, c), F32), jax.ShapeDtypeStruct((4, n, c), BF16)],
        compiler_params=_params(("arbitrary", "arbitrary")),
    )(kc_idx, p, r1)


def _owner_sum(a32, r2, kc_idx, name):
    r, c = a32.shape
    tr = _pick(r, SUM_TILES)

    def body(s_ref, a_ref, r_ref, o_ref):
        v = a_ref[...]
        for j in range(3):
            v = v + r_ref[j].astype(F32)
        o_ref[...] = v

    grid_spec = pltpu.PrefetchScalarGridSpec(
        num_scalar_prefetch=1, grid=(r // tr,),
        in_specs=[pl.BlockSpec((tr, c), lambda i, s: (i, 0)),
                  pl.BlockSpec((3, tr, c), lambda i, s: (0, i, 0))],
        out_specs=pl.BlockSpec((None, tr, c), lambda i, s: (s[1], i, 0)))
    return pl.pallas_call(
        body, name=name, grid_spec=grid_spec,
        out_shape=jax.ShapeDtypeStruct((2, r, c), F32),
        compiler_params=_params(("arbitrary",)),
    )(kc_idx, a32, r2)


def _pack_local_half(w_in_s, w_out_s, w_up_s, w_down_s, c_idx):
    parts, row = [], 0
    for (kind, l), off in sorted(PACK_OFF.items(), key=lambda kv: kv[1]):
        if off > row:
            parts.append(jnp.zeros((off - row, 1024), BF16))
        if kind == "up":
            p = lax.dynamic_slice_in_dim(w_up_s[l], c_idx * 512, 512, 0)
        elif kind == "down":
            p = lax.dynamic_slice_in_dim(w_down_s[l], c_idx * 512, 512, 0)
        elif kind == "in":
            p = lax.dynamic_slice_in_dim(w_in_s[l], c_idx * 512, 512, 0)
            p = p.reshape(2, 256, IN_PIECE_COLS).transpose(1, 0, 2).reshape(256, 2 * IN_PIECE_COLS)
            p = jnp.pad(p, ((0, 0), (0, 1024 - 2 * IN_PIECE_COLS)))
        else:
            p = lax.dynamic_slice_in_dim(w_out_s[l], c_idx * 128, 128, 0)
        parts.append(p.astype(BF16))
        row = off + PACK_HEIGHT[kind]
    return jnp.concatenate(parts, axis=0)


def _unpack_in_pieces(w_ref, own_ref, w_scr):
    if own_ref is not None:
        me = 4 * lax.axis_index("x") + 2 * lax.axis_index("y") + lax.axis_index("c")
    for d in range(N_DEV):
        k, c = d // 2, d % 2
        for t in range(2):
            piece = w_ref[d, :, t * IN_PIECE_COLS:(t + 1) * IN_PIECE_COLS]
            if own_ref is not None:
                piece = jnp.where(me == d, own_ref[:, t * IN_PIECE_COLS:(t + 1) * IN_PIECE_COLS], piece)
            w_scr[c * 512 + t * 256:c * 512 + (t + 1) * 256, k * IN_PIECE_COLS:(k + 1) * IN_PIECE_COLS] = piece


def _in_weight_operands(wg):
    specs, args = [_gathered_spec(wg, "in")], [wg["in"][0]]
    if "in_own" in wg:
        own, off = wg["in_own"]
        h = PACK_HEIGHT["in"]
        assert off % h == 0
        specs.append(pl.BlockSpec((h, 1024), lambda *_: (off // h, 0), pipeline_mode=pl.Buffered(1)))
        args.append(own)
    return specs, args


class _Rows:
    def __init__(self, nb, seq, ctx):
        self.nb, self.seq, self.ctx = nb, seq, ctx
        self.n_lat, self.n_ctx = nb * seq, nb * ctx
        self.rows = self.n_lat + self.n_ctx
        self.tm = _pick(np.gcd(seq, self.n_ctx), (512, 256, 128))
        self.tiles_per_ex = seq // self.tm
        self.n_tiles = self.rows // self.tm
        self.n_lat_tiles = self.n_lat // self.tm
        self.groups = nb + 1

    def group(self, i):
        return jnp.minimum(i // self.tiles_per_ex, self.nb)

    def first_of_group(self, i):
        return jnp.logical_and(i % self.tiles_per_ex == 0, i <= self.n_lat_tiles)


def _mod_spec(rt):
    return pl.BlockSpec((1, N_MOD, D_MODEL), lambda i: (rt.group(i), 0, 0))


def _row_spec(rt, cols):
    return pl.BlockSpec((rt.tm, cols), lambda i: (i, 0))


def _vec_spec(cols):
    return pl.BlockSpec((1, cols), lambda i: (0, 0))


def _group_spec(rt):
    return pl.BlockSpec((1, 1, D_MODEL), lambda i: (rt.group(i), 0, 0))


def _gathered_spec(wg, kind):
    h, off = PACK_HEIGHT[kind], wg[kind][1]
    assert off % h == 0, (kind, off)
    return pl.BlockSpec((N_DEV, h, 1024), lambda *_: (0, off // h, 0), pipeline_mode=pl.Buffered(1))


def _group_shape(rt):
    return jax.ShapeDtypeStruct((rt.groups, 1, D_MODEL), F32)


def _vec_shape(cols=D_MODEL):
    return jax.ShapeDtypeStruct((1, cols), F32)


def _rms_inv(v):
    return lax.rsqrt(jnp.mean(v * v, axis=-1, keepdims=True) + EPS)


def _norm_mod_val(h_, g_, mod_ref, i_shift, i_scale):
    n = h_ * _rms_inv(h_) * g_
    return n * (1.0 + mod_ref[0, i_scale:i_scale + 1, :]) + mod_ref[0, i_shift:i_shift + 1, :]


def _post_norm_val(h_, z_, g_, mod_ref, i_gate):
    return h_ + mod_ref[0, i_gate:i_gate + 1, :] * (z_ * _rms_inv(z_) * g_)


def _post_norm_bwd_val(dh_, z_, g_, gate):
    rinv = _rms_inv(z_)
    n0 = z_ * rinv
    dn = dh_ * gate * g_
    dz = rinv * (dn - n0 * jnp.mean(dn * n0, axis=-1, keepdims=True))
    return dz, jnp.sum(dh_ * n0 * g_, axis=0, keepdims=True), jnp.sum(dh_ * gate * n0, axis=0, keepdims=True)


def _norm_mod_bwd_val(du_, h_, g_, one_sc):
    rinv = _rms_inv(h_)
    n0 = h_ * rinv
    dn = du_ * g_ * one_sc
    dh = rinv * (dn - n0 * jnp.mean(dn * n0, axis=-1, keepdims=True))
    return (dh, jnp.sum(du_, axis=0, keepdims=True), jnp.sum(du_ * n0 * g_, axis=0, keepdims=True),
            jnp.sum(du_ * one_sc * n0, axis=0, keepdims=True))


def _accumulate(rt, i, group_pairs, global_pairs):
    @pl.when(rt.first_of_group(i))
    def _():
        for ref, _ in group_pairs:
            ref[...] = jnp.zeros_like(ref)

    @pl.when(i == 0)
    def _():
        for ref, _ in global_pairs:
            ref[...] = jnp.zeros_like(ref)

    for ref, val in group_pairs:
        ref[0] += val
    for ref, val in global_pairs:
        ref[...] += val


def _rope_tables(rt):
    pos = np.arange(rt.seq)
    axis_dim = HEAD_DIM // 2
    inv = (ROPE_THETA ** (-np.arange(0, axis_dim, 2, dtype=np.float32) / axis_dim)).astype(np.float32)
    ang_r = (pos // GRID_W).astype(np.float32)[:, None] * inv[None, :]
    ang_c = (pos % GRID_W).astype(np.float32)[:, None] * inv[None, :]
    cr, sr, cc, sc = np.cos(ang_r), np.sin(ang_r), np.cos(ang_c), np.sin(ang_c)
    zero = np.zeros_like(sr)
    cos = np.concatenate([cr, cr, cc, cc], axis=1)
    s_lo = np.concatenate([zero, sr, zero, sc], axis=1)
    s_hi = np.concatenate([-sr, zero, -sc, zero], axis=1)

    def full(t, ctx_value):
        return jnp.asarray(np.concatenate([np.tile(t, (1, 2)), np.full((rt.tm, 128), ctx_value)], axis=0), F32)

    return full(cos, 1.0), full(s_lo, 0.0), full(s_hi, 0.0)


def _table_spec(rt):
    return pl.BlockSpec((rt.tm, 128), lambda i: (jnp.where(i < rt.n_lat_tiles, i % rt.tiles_per_ex, rt.tiles_per_ex), 0))


def _head_stats(t, lo):
    sq = t * t
    s_lo = jnp.sum(jnp.where(lo, sq, 0.0), axis=1, keepdims=True)
    s_hi = jnp.sum(jnp.where(lo, 0.0, sq), axis=1, keepdims=True)
    return lax.rsqrt(jnp.where(lo, s_lo, s_hi) * (1.0 / HEAD_DIM) + EPS)


def _prep_fwd_body(tm, qkv_ref, c, s1, s2, qn, kn, out_ref):
    lo = lax.broadcasted_iota(jnp.int32, (tm, 128), 1) < HEAD_DIM

    def rope(t):
        return t * c + pltpu.roll(t, 16, 1) * s1 + pltpu.roll(t, 112, 1) * s2

    for j in range(12):
        t = qkv_ref[:, j * 128:(j + 1) * 128]
        if j < 4:
            t = rope(t * _head_stats(t, lo) * qn) * Q_SCALE
        elif j == COL_KA:
            t = rope(t * _head_stats(t, lo) * kn)
        elif 6 <= j < 10:
            t = rope(t) * Q_SCALE
        elif j == COL_KB:
            t = rope(t)
        out_ref[:, j * 128:(j + 1) * 128] = t.astype(BF16)


def _prep_bwd_body(tm, dq_ref, dkv_ref, qkv_ref, c, s1, s2, qn, kn, out_ref):
    lo = lax.broadcasted_iota(jnp.int32, (tm, 128), 1) < HEAD_DIM

    def rope_bwd(d):
        return d * c + pltpu.roll(d * s1, 112, 1) + pltpu.roll(d * s2, 16, 1)

    def norm_bwd(t, g, dy):
        rinv = _head_stats(t, lo)
        n = t * rinv
        dn = dy * g
        prod = dn * n
        m_lo = jnp.sum(jnp.where(lo, prod, 0.0), axis=1, keepdims=True)
        m_hi = jnp.sum(jnp.where(lo, 0.0, prod), axis=1, keepdims=True)
        mean = jnp.where(lo, m_lo, m_hi) * (1.0 / HEAD_DIM)
        return rinv * (dn - n * mean), jnp.sum(dy * n, axis=0, keepdims=True)

    dqn = jnp.zeros((1, 128), F32)
    dkn = jnp.zeros((1, 128), F32)
    for j in range(12):
        if j < 4:
            d, dg = norm_bwd(qkv_ref[:, j * 128:(j + 1) * 128], qn, rope_bwd(dq_ref[:, j * 128:(j + 1) * 128] * Q_SCALE))
            dqn = dqn + dg
        elif j == COL_KA:
            d, dg = norm_bwd(qkv_ref[:, j * 128:(j + 1) * 128], kn, rope_bwd(dkv_ref[:, 0:128]))
            dkn = dkn + dg
        elif j == COL_VA:
            d = dkv_ref[:, 128:256]
        elif j < 10:
            d = rope_bwd(dq_ref[:, (j - 2) * 128:(j - 1) * 128] * Q_SCALE)
        elif j == COL_KB:
            d = rope_bwd(dkv_ref[:, 256:384])
        else:
            d = dkv_ref[:, 384:512]
        out_ref[:, j * 128:(j + 1) * 128] = d.astype(BF16)
    return dqn, dkn


def _in_fwd(rt, h, gamma, mod, wg, tables, qn, kn, name):
    w_specs, w_args = _in_weight_operands(wg)
    n_w = len(w_args)

    def body(h_ref, g_ref, mod_ref, *rest):
        c_ref, s1_ref, s2_ref, qn_ref, kn_ref, u_ref, qkn_ref, qkvp_ref, qkv_ref, w_scr = rest[n_w:]

        @pl.when(pl.program_id(0) == 0)
        def _():
            _unpack_in_pieces(rest[0], rest[1] if n_w == 2 else None, w_scr)

        u = _norm_mod_val(h_ref[...], g_ref[...], mod_ref, 0, 1).astype(BF16)
        u_ref[...] = u
        qkv_ref[...] = jnp.dot(u, w_scr[...], preferred_element_type=F32)
        qkn_ref[...] = qkv_ref[:, 0:NORMED_COLS]
        _prep_fwd_body(rt.tm, qkv_ref, c_ref[...], s1_ref[...], s2_ref[...], qn_ref[...], kn_ref[...], qkvp_ref)

    return pl.pallas_call(
        body, name=name, grid=(rt.n_tiles,),
        in_specs=[_row_spec(rt, D_MODEL), _vec_spec(D_MODEL), _mod_spec(rt)] + w_specs + [_table_spec(rt)] * 3 + [_vec_spec(128)] * 2,
        out_specs=[_row_spec(rt, D_MODEL), _row_spec(rt, NORMED_COLS), _row_spec(rt, IN_COLS)],
        out_shape=[jax.ShapeDtypeStruct((rt.rows, D_MODEL), BF16), jax.ShapeDtypeStruct((rt.rows, NORMED_COLS), F32),
                   jax.ShapeDtypeStruct((rt.rows, IN_COLS), BF16)],
        scratch_shapes=[pltpu.VMEM((rt.tm, IN_COLS), F32), pltpu.VMEM((D_MODEL, IN_COLS), BF16)],
        compiler_params=_params(("arbitrary",)),
    )(h, gamma, mod, *w_args, *tables, qn, kn)


def _in_bwd(rt, dq, dkv, qkv, tables, qn, kn, wg, h, dres, mod, gamma, latent_only, name, comm=None):
    last = rt.n_lat_tiles - 1
    w_specs, w_args = _in_weight_operands(wg)
    n_w = len(w_args)

    def body(dq_ref, dkv_ref, qkv_ref, c_ref, s1_ref, s2_ref, qn_ref, kn_ref, *rest):
        h_ref, dres_ref, mod_ref, g_ref, dqkv_ref, dh_ref, dqn_ref, dkn_ref, dsh_ref, dsc_ref, dg_ref, w_scr = rest[n_w:]
        i = pl.program_id(0)

        @pl.when(i == 0)
        def _():
            _unpack_in_pieces(rest[0], rest[1] if n_w == 2 else None, w_scr)

        dqn, dkn = _prep_bwd_body(rt.tm, dq_ref, dkv_ref, qkv_ref, c_ref[...], s1_ref[...], s2_ref[...], qn_ref[...], kn_ref[...], dqkv_ref)
        du = lax.dot_general(dqkv_ref[...], w_scr[...], NT, preferred_element_type=F32)
        dh, dsh, dsc, dg = _norm_mod_bwd_val(du, h_ref[...], g_ref[...], 1.0 + mod_ref[0, 1:2, :])
        if latent_only:
            @pl.when(i <= last)
            def _():
                dh_ref[...] = dres_ref[...] + dh
        else:
            dh_ref[...] = dres_ref[...] + dh
        _accumulate(rt, i, [(dsh_ref, dsh), (dsc_ref, dsc)], [(dg_ref, dg), (dqn_ref, dqn), (dkn_ref, dkn)])

    dh_spec = pl.BlockSpec((rt.tm, D_MODEL), lambda i: (jnp.minimum(i, last), 0)) if latent_only else _row_spec(rt, D_MODEL)
    return _comm_call(
        body, comm, name=name, grid=(rt.n_tiles,),
        in_specs=[_row_spec(rt, 1024), _row_spec(rt, 512), _row_spec(rt, NORMED_COLS)] + [_table_spec(rt)] * 3 + [_vec_spec(128)] * 2
        + w_specs + [_row_spec(rt, D_MODEL), _row_spec(rt, D_MODEL), _mod_spec(rt), _vec_spec(D_MODEL)],
        out_specs=[_row_spec(rt, IN_COLS), dh_spec, _vec_spec(128), _vec_spec(128),
                   _group_spec(rt), _group_spec(rt), _vec_spec(D_MODEL)],
        out_shape=[jax.ShapeDtypeStruct((rt.rows, IN_COLS), BF16),
                   jax.ShapeDtypeStruct((rt.n_lat if latent_only else rt.rows, D_MODEL), F32),
                   _vec_shape(128), _vec_shape(128), _group_shape(rt), _group_shape(rt), _vec_shape()],
        args=[dq, dkv, qkv, *tables, qn, kn, *w_args, h, dres, mod, gamma], aliases={}, semantics=("arbitrary",),
        scratch=[pltpu.VMEM((D_MODEL, IN_COLS), BF16)])


def _out_fwd(rt, o, wg, h, mod, g_post_mix, g_pre_mlp, name):
    def body(o_ref, w_ref, h_ref, mod_ref, gpost_ref, gpre_ref, mix_ref, h1_ref, u2_ref):
        mix = jnp.dot(o_ref[...], w_ref[...].reshape(D_MODEL, D_MODEL), preferred_element_type=F32)
        mix_ref[...] = mix
        h1 = _post_norm_val(h_ref[...], mix, gpost_ref[...], mod_ref, 2)
        h1_ref[...] = h1
        u2_ref[...] = _norm_mod_val(h1, gpre_ref[...], mod_ref, 3, 4).astype(BF16)

    return pl.pallas_call(
        body, name=name, grid=(rt.n_tiles,),
        in_specs=[_row_spec(rt, D_MODEL), _gathered_spec(wg, "out"), _row_spec(rt, D_MODEL), _mod_spec(rt),
                  _vec_spec(D_MODEL), _vec_spec(D_MODEL)],
        out_specs=[_row_spec(rt, D_MODEL)] * 3,
        out_shape=[jax.ShapeDtypeStruct((rt.rows, D_MODEL), F32), jax.ShapeDtypeStruct((rt.rows, D_MODEL), F32),
                   jax.ShapeDtypeStruct((rt.rows, D_MODEL), BF16)],
        compiler_params=_params(("parallel",)),
    )(o, wg["out"][0], h, mod, g_post_mix, g_pre_mlp)


def _out_bwd(rt, dh1, mix, wg, mod, g_post_mix, name, comm=None):
    def body(dh_ref, mix_ref, w_ref, mod_ref, g_ref, dmix_ref, do_ref, dgate_ref, dg_ref):
        i = pl.program_id(0)
        dz, dgate, dg = _post_norm_bwd_val(dh_ref[...], mix_ref[...], g_ref[...], mod_ref[0, 2:3, :])
        dzb = dz.astype(BF16)
        dmix_ref[...] = dzb
        do_ref[...] = lax.dot_general(dzb, w_ref[...].reshape(D_MODEL, D_MODEL), NT, preferred_element_type=F32).astype(BF16)
        _accumulate(rt, i, [(dgate_ref, dgate)], [(dg_ref, dg)])

    return _comm_call(
        body, comm, name=name, grid=(rt.n_tiles,),
        in_specs=[_row_spec(rt, D_MODEL), _row_spec(rt, D_MODEL), _gathered_spec(wg, "out"), _mod_spec(rt), _vec_spec(D_MODEL)],
        out_specs=[_row_spec(rt, D_MODEL), _row_spec(rt, D_MODEL), _group_spec(rt), _vec_spec(D_MODEL)],
        out_shape=[jax.ShapeDtypeStruct((rt.rows, D_MODEL), BF16), jax.ShapeDtypeStruct((rt.rows, D_MODEL), BF16),
                   _group_shape(rt), _vec_shape()],
        args=[dh1, mix, wg["out"][0], mod, g_post_mix], aliases={}, semantics=("arbitrary",))


def _w_chunk(w_ref, k):
    return w_ref[2 * k:2 * k + 2].reshape(1024, 1024)


def _mlp_fwd(rt, u2, h1, wg, mod, g_post_mlp, name, comm=None, target=None):
    last = rt.n_lat_tiles - 1

    def body(u2_ref, h1_ref, wu_ref, wd_ref, mod_ref, g_ref, *rest):
        u2_ = u2_ref[...]
        y = jnp.zeros((rt.tm, D_MODEL), F32)
        for k in range(D_FF // 1024):
            a = jnp.maximum(jnp.dot(u2_, _w_chunk(wu_ref, k), preferred_element_type=F32), 0.0)
            rest[-3 if target is None else -4][:, k * 1024:(k + 1) * 1024] = a.astype(BF16)
            y = y + jnp.dot((a * a).astype(BF16), _w_chunk(wd_ref, k), preferred_element_type=F32)
        h2 = _post_norm_val(h1_ref[...], y, g_ref[...], mod_ref, 5)
        if target is None:
            _, y_ref, h2_ref = rest
            y_ref[...] = y
            h2_ref[...] = h2
        else:
            t_ref, _, y_ref, dh_ref, sq_ref = rest
            y_ref[...] = y
            i = pl.program_id(0)

            @pl.when(i == 0)
            def _():
                sq_ref[...] = jnp.zeros_like(sq_ref)

            @pl.when(i <= last)
            def _():
                e = h2 - t_ref[...]
                dh_ref[...] = e * (1.0 / D_MODEL)
                sq_ref[...] += jnp.sum(e * e, axis=0, keepdims=True)

            @pl.when(i > last)
            def _():
                dh_ref[...] = jnp.zeros_like(dh_ref)

    in_specs = [_row_spec(rt, D_MODEL), _row_spec(rt, D_MODEL), _gathered_spec(wg, "up"), _gathered_spec(wg, "down"),
                _mod_spec(rt), _vec_spec(D_MODEL)]
    args = [u2, h1, wg["up"][0], wg["down"][0], mod, g_post_mlp]
    out_specs = [_row_spec(rt, D_FF), _row_spec(rt, D_MODEL), _row_spec(rt, D_MODEL)]
    out_shape = [jax.ShapeDtypeStruct((rt.rows, D_FF), BF16), jax.ShapeDtypeStruct((rt.rows, D_MODEL), F32),
                 jax.ShapeDtypeStruct((rt.rows, D_MODEL), F32)]
    if target is not None:
        in_specs.append(pl.BlockSpec((rt.tm, D_MODEL), lambda i: (jnp.minimum(i, last), 0)))
        args.append(target)
        out_specs.append(_vec_spec(D_MODEL))
        out_shape.append(_vec_shape())
    return _comm_call(body, comm, name=name, grid=(rt.n_tiles,), in_specs=in_specs, out_specs=out_specs, out_shape=out_shape,
                      args=args, aliases={}, semantics=("parallel",) if target is None else ("arbitrary",))


def _mlp_down_bwd(rt, dh, y, ra, wg, mod, g_post_mlp, name, comm=None):
    def body(dh_ref, y_ref, ra_ref, wd_ref, mod_ref, g_ref, dy_ref, da_ref, dgate_ref, dg_ref):
        i = pl.program_id(0)
        dz, dgate, dg = _post_norm_bwd_val(dh_ref[...], y_ref[...], g_ref[...], mod_ref[0, 5:6, :])
        dyb = dz.astype(BF16)
        dy_ref[...] = dyb
        for k in range(D_FF // 1024):
            dr = lax.dot_general(dyb, _w_chunk(wd_ref, k), NT, preferred_element_type=F32)
            da_ref[:, k * 1024:(k + 1) * 1024] = (dr * (2.0 * ra_ref[:, k * 1024:(k + 1) * 1024].astype(F32))).astype(BF16)
        _accumulate(rt, i, [(dgate_ref, dgate)], [(dg_ref, dg)])

    return _comm_call(
        body, comm, name=name, grid=(rt.n_tiles,),
        in_specs=[_row_spec(rt, D_MODEL), _row_spec(rt, D_MODEL), _row_spec(rt, D_FF), _gathered_spec(wg, "down"),
                  _mod_spec(rt), _vec_spec(D_MODEL)],
        out_specs=[_row_spec(rt, D_MODEL), _row_spec(rt, D_FF), _group_spec(rt), _vec_spec(D_MODEL)],
        out_shape=[jax.ShapeDtypeStruct((rt.rows, D_MODEL), BF16), jax.ShapeDtypeStruct((rt.rows, D_FF), BF16),
                   _group_shape(rt), _vec_shape()],
        args=[dh, y, ra, wg["down"][0], mod, g_post_mlp], aliases={}, semantics=("arbitrary",))


def _mlp_up_bwd(rt, da, wg, h1, dh, mod, g_pre_mlp, name):
    def body(da_ref, wu_ref, h1_ref, dh_ref, mod_ref, g_ref, dh1_ref, dsh_ref, dsc_ref, dg_ref):
        i = pl.program_id(0)
        du = jnp.zeros((rt.tm, D_MODEL), F32)
        for k in range(D_FF // 1024):
            du = du + lax.dot_general(da_ref[:, k * 1024:(k + 1) * 1024], _w_chunk(wu_ref, k), NT, preferred_element_type=F32)
        d, dsh, dsc, dg = _norm_mod_bwd_val(du, h1_ref[...], g_ref[...], 1.0 + mod_ref[0, 4:5, :])
        dh1_ref[...] = dh_ref[...] + d
        _accumulate(rt, i, [(dsh_ref, dsh), (dsc_ref, dsc)], [(dg_ref, dg)])

    return pl.pallas_call(
        body, name=name, grid=(rt.n_tiles,),
        in_specs=[_row_spec(rt, D_FF), _gathered_spec(wg, "up"), _row_spec(rt, D_MODEL), _row_spec(rt, D_MODEL),
                  _mod_spec(rt), _vec_spec(D_MODEL)],
        out_specs=[_row_spec(rt, D_MODEL), _group_spec(rt), _group_spec(rt), _vec_spec(D_MODEL)],
        out_shape=[jax.ShapeDtypeStruct((rt.rows, D_MODEL), F32), _group_shape(rt), _group_shape(rt), _vec_shape()],
        compiler_params=_params(("arbitrary",)),
    )(da, wg["up"][0], h1, dh, mod, g_pre_mlp)


def _wgrad_packed(rt, a, b, kind, off, n_rows, p_prev, name, comm=None):
    h = PACK_HEIGHT[kind]
    tk = rt.tm
    assert off % h == 0, (kind, off)

    def body(a_ref, b_ref, *rest):
        o_ref = rest[-1]
        i = pl.program_id(0)

        @pl.when(i == 0)
        def _():
            o_ref[...] = jnp.zeros_like(o_ref)

        if kind == "in":
            res = lax.dot_general(a_ref[...], b_ref[...], TN, preferred_element_type=F32)
            for k in range(4):
                for c in range(2):
                    for t in range(2):
                        o_ref[c, k, :, t * IN_PIECE_COLS:(t + 1) * IN_PIECE_COLS] += \
                            res[c * 512 + t * h:c * 512 + (t + 1) * h, k * IN_PIECE_COLS:(k + 1) * IN_PIECE_COLS]
        elif kind == "out":
            res = lax.dot_general(a_ref[...], b_ref[...], TN, preferred_element_type=F32)
            for k in range(4):
                for c in range(2):
                    o_ref[c, k] += res[(2 * k + c) * h:(2 * k + c + 1) * h]
        else:
            for k in range(4):
                if kind == "up":
                    res = lax.dot_general(a_ref[...], b_ref[:, k * 1024:(k + 1) * 1024], TN, preferred_element_type=F32)
                else:
                    ra = a_ref[:, k * 1024:(k + 1) * 1024].astype(F32)
                    res = lax.dot_general((ra * ra).astype(BF16), b_ref[...], TN, preferred_element_type=F32)
                o_ref[0, k] += res[0:h]
                o_ref[1, k] += res[h:2 * h]

    in_specs = [pl.BlockSpec((tk, a.shape[1]), lambda i: (i, 0)), pl.BlockSpec((tk, b.shape[1]), lambda i: (i, 0))]
    args = [a, b]
    aliases = {}
    if p_prev is not None:
        in_specs.append(pl.BlockSpec(memory_space=pl.ANY))
        args.append(p_prev)
        aliases = {2: 0}
    outs = _comm_call(
        body, comm, name=name, grid=(rt.rows // tk,),
        in_specs=in_specs,
        out_specs=[pl.BlockSpec((2, 4, h, 1024), lambda i: (0, 0, off // h, 0))],
        out_shape=[jax.ShapeDtypeStruct((2, 4, n_rows, 1024), F32)],
        args=args, aliases=aliases, semantics=("arbitrary",))
    return outs[0] if comm is None else outs


def _ada_wgrad(xs, dm, name):
    depth, _, cols = dm.shape

    def body(x_ref, d_ref, o_ref):
        for l in range(depth):
            o_ref[l] = lax.dot_general(x_ref[...], d_ref[l], TN, preferred_element_type=F32)

    return pl.pallas_call(body, name=name, out_shape=jax.ShapeDtypeStruct((depth, xs.shape[1], cols), F32),
                          compiler_params=pltpu.CompilerParams(vmem_limit_bytes=VMEM_LIMIT))(xs, dm)


def _stack_heads(x, kvi):
    x = x.astype(F32)
    tq = x.shape[0]
    lane = lax.broadcasted_iota(jnp.int32, (tq, 128), 1)
    keep = lane < HEAD_DIM if kvi == 0 else lane >= HEAD_DIM
    parts = []
    for p in range(2):
        pair = x[:, p * 128:(p + 1) * 128]
        swapped = pltpu.roll(pair, HEAD_DIM, 1)
        lo_head, hi_head = (pair, swapped) if kvi == 0 else (swapped, pair)
        parts += [jnp.where(keep, lo_head, 0.0), jnp.where(keep, hi_head, 0.0)]
    return jnp.concatenate(parts, axis=0).astype(BF16)


def _unstack_heads(o4, kvi):
    tq = o4.shape[0] // GROUP
    lane = lax.broadcasted_iota(jnp.int32, (tq, 128), 1)
    outs = []
    for p in range(2):
        r_lo, r_hi = o4[(2 * p) * tq:(2 * p + 1) * tq], o4[(2 * p + 1) * tq:(2 * p + 2) * tq]
        if kvi == 0:
            lo, hi = r_lo, pltpu.roll(r_hi, HEAD_DIM, 1)
        else:
            lo, hi = pltpu.roll(r_lo, HEAD_DIM, 1), r_hi
        outs.append(jnp.where(lane < HEAD_DIM, lo, hi))
    return jnp.concatenate(outs, axis=1)


def _per_head(shape, axis, tq, values):
    head = lax.broadcasted_iota(jnp.int32, shape, axis) // tq
    out = jnp.zeros(shape, F32)
    for g in range(GROUP):
        out = jnp.where(head == g, values[g], out)
    return out


KEY_CHUNK = 512


def _key_chunks(k_ref, v_ref, n, kc=KEY_CHUNK):
    kc = min(kc, n)
    return [(k_ref[c * kc:(c + 1) * kc, :], v_ref[c * kc:(c + 1) * kc, :], None) for c in range(n // kc)]


def _softmax_fwd(qs, chunks, sink_col):
    logits = []
    for k, _, mask in chunks:
        s = lax.dot_general(qs, k, NT, preferred_element_type=F32)
        logits.append(s if mask is None else jnp.where(mask, s, NEG_BIG))
    m = functools.reduce(jnp.maximum, [jnp.max(s, axis=1, keepdims=True) for s in logits])
    if sink_col is not None:
        m = jnp.maximum(m, sink_col)
    l = jnp.zeros_like(m) if sink_col is None else jnp.exp(sink_col - m)
    acc = jnp.zeros((qs.shape[0], 128), F32)
    for s, (_, v, _) in zip(logits, chunks):
        p = jnp.exp(s - m)
        l = l + jnp.sum(p, axis=1, keepdims=True)
        acc = acc + jnp.dot(p.astype(BF16), v, preferred_element_type=F32)
    return acc / l, m + jnp.log(l)


def _to_rows(col):
    return jnp.transpose(jnp.broadcast_to(col, (col.shape[0], 128)))[0:8, :]


def _softmax_bwd(qs, dos, lse_row, delta_row, chunks):
    dq = jnp.zeros((qs.shape[0], 128), F32)
    grads = []
    for k, v, mask in chunks:
        s = lax.dot_general(k, qs, NT, preferred_element_type=F32)
        if mask is not None:
            s = jnp.where(mask, s, NEG_BIG)
        p = jnp.exp(s - lse_row)
        dp = lax.dot_general(v, dos, NT, preferred_element_type=F32)
        ds = (p * (dp - delta_row)).astype(BF16)
        dv = jnp.dot(p.astype(BF16), dos, preferred_element_type=F32)
        dk = jnp.dot(ds, qs, preferred_element_type=F32)
        dq = dq + lax.dot_general(ds, k, TN, preferred_element_type=F32)
        grads.append((dk, dv))
    return dq, grads


def _band(qi, tq, seq):
    span = tq + 2 * WINDOW
    start = pl.multiple_of(jnp.clip(qi * tq - WINDOW, 0, seq - span), 64)
    return start, span


def _band_mask(qi, tq, start, span, query_axis):
    shape = (GROUP * tq, span) if query_axis == 0 else (span, GROUP * tq)
    qpos = qi * tq + lax.broadcasted_iota(jnp.int32, shape, query_axis) % tq
    kpos = start + lax.broadcasted_iota(jnp.int32, shape, 1 - query_axis)
    return jnp.abs(kpos - qpos) <= WINDOW


def _qkv_specs(rt, tq, q_row, ctx_row, with_latent):
    specs = [pl.BlockSpec((tq, 256), functools.partial(lambda b, i, col: (q_row(b, i), col), col=col)) for col in (0, 1, 3, 4)]
    if with_latent:
        specs += [pl.BlockSpec((rt.seq, 128), functools.partial(lambda b, i, col: (b, col), col=col))
                  for col in (COL_KA, COL_VA, COL_KB, COL_VB)]
    specs += [pl.BlockSpec((rt.ctx, 128), functools.partial(lambda b, i, col: (ctx_row(b), col), col=col))
              for col in (COL_KA, COL_VA, COL_KB, COL_VB)]
    return specs


def _attn_fwd(rt, qkvp, sink, o_prev, name, comm=None):
    latent = o_prev is None
    seq, ctx, nb = rt.seq, rt.ctx, rt.nb
    tq = 128 if latent else ctx
    nq = seq // tq if latent else 1
    ctx_blk0 = rt.n_lat // ctx
    q_row = (lambda b, i: b * nq + i) if latent else (lambda b, i: ctx_blk0 + b)

    def body(sink_ref, qa0, qa1, qb0, qb1, *rest):
        if latent:
            kal, val, kbl, vbl, kac, vac, kbc, vbc, o_ref, lse_ref = rest
        else:
            kac, vac, kbc, vbc, _, o_ref, lse_ref = rest
        qi = pl.program_id(1)
        for kvi, (qa, qb) in enumerate(((qa0, qb0), (qa1, qb1))):
            src_a = _key_chunks(kac, vac, ctx)
            src_b = _key_chunks(kbc, vbc, ctx)
            if latent:
                src_a += _key_chunks(kal, val, seq, seq)
                start, span = _band(qi, tq, seq)
                src_b.append((kbl[pl.ds(start, span), :], vbl[pl.ds(start, span), :], _band_mask(qi, tq, start, span, 0)))
            oa, lse = _softmax_fwd(_stack_heads(qa[...], kvi), src_a, None)
            o_ref[:, kvi * 256:(kvi + 1) * 256] = _unstack_heads(oa, kvi).astype(BF16)
            lse_ref[0, kvi] = _to_rows(lse)
            sink_col = _per_head((GROUP * tq, 1), 0, tq, [sink_ref[kvi * GROUP + g] for g in range(GROUP)])
            ob, lse = _softmax_fwd(_stack_heads(qb[...], kvi), src_b, sink_col)
            o_ref[:, 512 + kvi * 256:512 + (kvi + 1) * 256] = _unstack_heads(ob, kvi).astype(BF16)
            lse_ref[0, 2 + kvi] = _to_rows(lse)

    specs = _qkv_specs(rt, tq, q_row, lambda b: ctx_blk0 + b, latent)
    args = [sink] + [qkvp] * len(specs)
    in_specs = [pl.BlockSpec(memory_space=pltpu.SMEM)] + specs
    aliases = {}
    if not latent:
        in_specs.append(pl.BlockSpec(memory_space=pl.ANY))
        args.append(o_prev)
        aliases = {len(args) - 1: 0}
    return _comm_call(
        body, comm, name=name, grid=(nb, nq),
        in_specs=in_specs,
        out_specs=[pl.BlockSpec((tq, 1024), lambda b, i: (q_row(b, i), 0)),
                   pl.BlockSpec((1, 4, 8, GROUP * tq), lambda b, i: (b * nq + i, 0, 0, 0))],
        out_shape=[jax.ShapeDtypeStruct((rt.rows, 1024), BF16), jax.ShapeDtypeStruct((nb * nq, 4, 8, GROUP * tq), F32)],
        args=args, aliases=aliases, semantics=("parallel", "parallel"))


def _attn_bwd(rt, qkvp, o, lse, do, sink, prev, name, comm=None):
    latent = prev is None
    seq, ctx, nb = rt.seq, rt.ctx, rt.nb
    tq = 128 if latent else ctx
    nq = seq // tq if latent else 1
    ctx_blk0 = rt.n_lat // ctx
    q_row = (lambda b, i: b * nq + i) if latent else (lambda b, i: ctx_blk0 + b)
    kc = min(KEY_CHUNK, seq)

    def body(sink_ref, qa0, qa1, qb0, qb1, *rest):
        if latent:
            kal, val, kbl, vbl, kac, vac, kbc, vbc, do_ref, o_ref, lse_ref, dq_ref, dl_ref, dc_ref, dsink_ref = rest
        else:
            kac, vac, kbc, vbc, do_ref, o_ref, lse_ref, c1_ref, _, _, dq_ref, dc_ref, dsink_ref = rest
        b, qi = pl.program_id(0), pl.program_id(1)

        def rows_of(cols, kvi, mixer):
            dos = _stack_heads(do_ref[:, cols], kvi)
            delta = jnp.sum(dos.astype(F32) * _stack_heads(o_ref[:, cols], kvi).astype(F32), axis=1, keepdims=True)
            return dos, lse_ref[0, 2 * mixer + kvi, 0:1, :], _to_rows(delta)[0:1, :]

        @pl.when(jnp.logical_and(b == 0, qi == 0))
        def _():
            dsink_ref[...] = jnp.zeros_like(dsink_ref)

        if latent:
            @pl.when(qi == 0)
            def _():
                dc_ref[...] = jnp.zeros_like(dc_ref)
                dl_ref[...] = jnp.zeros_like(dl_ref)
        else:
            dc_ref[...] = c1_ref[...]

        head_row = lax.broadcasted_iota(jnp.int32, (8, 128), 0)
        for kvi, (qa, qb) in enumerate(((qa0, qb0), (qa1, qb1))):
            cols = slice(kvi * 256, (kvi + 1) * 256)
            dos, lse_row, delta_row = rows_of(cols, kvi, 0)
            src = _key_chunks(kac, vac, ctx)
            if latent:
                src += _key_chunks(kal, val, seq)
            dq4, grads = _softmax_bwd(_stack_heads(qa[...], kvi), dos, lse_row, delta_row, src)
            dq_ref[:, cols] = _unstack_heads(dq4, kvi)
            dc_ref[:, 0:128] += grads[0][0]
            dc_ref[:, 128:256] += grads[0][1]
            for c, (dk, dv) in enumerate(grads[1:]):
                dl_ref[c * kc:(c + 1) * kc, 0:128] += dk
                dl_ref[c * kc:(c + 1) * kc, 128:256] += dv
            cols = slice(512 + kvi * 256, 512 + (kvi + 1) * 256)
            dos, lse_row, delta_row = rows_of(cols, kvi, 1)
            src = _key_chunks(kbc, vbc, ctx)
            if latent:
                start, span = _band(qi, tq, seq)
                src.append((kbl[pl.ds(start, span), :], vbl[pl.ds(start, span), :], _band_mask(qi, tq, start, span, 1)))
            dq4, grads = _softmax_bwd(_stack_heads(qb[...], kvi), dos, lse_row, delta_row, src)
            dq_ref[:, cols] = _unstack_heads(dq4, kvi)
            dc_ref[:, 256:384] += grads[0][0]
            dc_ref[:, 384:512] += grads[0][1]
            if latent:
                dl_ref[pl.ds(start, span), 256:384] += grads[1][0]
                dl_ref[pl.ds(start, span), 384:512] += grads[1][1]
            sink_row = _per_head((1, GROUP * tq), 1, tq, [sink_ref[kvi * GROUP + g] for g in range(GROUP)])
            dsink = -jnp.exp(sink_row - lse_row) * delta_row
            head = lax.broadcasted_iota(jnp.int32, (1, GROUP * tq), 1) // tq
            upd = jnp.zeros((8, 128), F32)
            for g in range(GROUP):
                upd = jnp.where(head_row == kvi * GROUP + g, jnp.sum(jnp.where(head == g, dsink, 0.0)), upd)
            dsink_ref[...] += upd

    specs = _qkv_specs(rt, tq, q_row, lambda b: ctx_blk0 + b, latent)
    q_rows_spec = pl.BlockSpec((tq, 1024), lambda b, i: (q_row(b, i), 0))
    in_specs = ([pl.BlockSpec(memory_space=pltpu.SMEM)] + specs
                + [q_rows_spec, q_rows_spec, pl.BlockSpec((1, 4, 8, GROUP * tq), lambda b, i: (b * nq + i, 0, 0, 0))])
    args = [sink] + [qkvp] * len(specs) + [do, o, lse]
    dq_shape = jax.ShapeDtypeStruct((rt.rows, 1024), F32)
    dkv_shape = jax.ShapeDtypeStruct((rt.rows, 512), F32)
    dsink_spec, dsink_shape = pl.BlockSpec((8, 128), lambda b, i: (0, 0)), jax.ShapeDtypeStruct((8, 128), F32)
    dq_spec = pl.BlockSpec((tq, 1024), lambda b, i: (q_row(b, i), 0))
    if latent:
        out_specs = [dq_spec, pl.BlockSpec((seq, 512), lambda b, i: (b, 0)), pl.BlockSpec((ctx, 512), lambda b, i: (b, 0)), dsink_spec]
        out_shape = [dq_shape, dkv_shape, jax.ShapeDtypeStruct((rt.n_ctx, 512), F32), dsink_shape]
        aliases = {}
    else:
        dq_prev, dkv_prev, c1 = prev
        in_specs += [pl.BlockSpec((ctx, 512), lambda b, i: (b, 0)), pl.BlockSpec(memory_space=pl.ANY), pl.BlockSpec(memory_space=pl.ANY)]
        args += [c1, dq_prev, dkv_prev]
        out_specs = [dq_spec, pl.BlockSpec((ctx, 512), lambda b, i: (ctx_blk0 + b, 0)), dsink_spec]
        out_shape = [dq_shape, dkv_shape, dsink_shape]
        aliases = {len(args) - 2: 0, len(args) - 1: 1}
    return _comm_call(body, comm, name=name, grid=(nb, nq), in_specs=in_specs, out_specs=out_specs, out_shape=out_shape,
                      args=args, aliases=aliases, semantics=("arbitrary", "arbitrary"))


def _silu(x):
    return x / (1.0 + jnp.exp(-x))


def _whole(shape):
    return pl.BlockSpec(shape, lambda i, s: (0,) * len(shape))


def _ada_half_spec(cols):
    return pl.BlockSpec((DEPTH, D_MODEL, cols), lambda i, s: (0, 0, s[0]))


def _ada_fwd(cond, w_ada, b_half, c_idx, name):
    rows = cond.shape[0]
    cols = w_ada.shape[2] // 2

    def body(s_ref, c_ref, w_ref, b_ref, x_ref, o_ref):
        xs = _silu(c_ref[...]).astype(BF16)
        x_ref[...] = xs
        for l in range(DEPTH):
            o_ref[l] = jnp.dot(xs, w_ref[l].astype(BF16), preferred_element_type=F32) + b_ref[l]

    grid_spec = pltpu.PrefetchScalarGridSpec(
        num_scalar_prefetch=1, grid=(1,),
        in_specs=[_whole(cond.shape), _ada_half_spec(cols), _whole(b_half.shape)],
        out_specs=[_whole((rows, D_MODEL)), _whole((DEPTH, rows, cols))])
    return pl.pallas_call(
        body, name=name, grid_spec=grid_spec,
        out_shape=[jax.ShapeDtypeStruct((rows, D_MODEL), BF16), jax.ShapeDtypeStruct((DEPTH, rows, cols), F32)],
        compiler_params=_params(("arbitrary",)),
    )(c_idx, cond, w_ada, b_half)


def _ada_cond_bwd(dcx, w_ada, c_idx, name):
    _, rows, cols = dcx.shape

    def body(s_ref, d_ref, w_ref, o_ref):
        acc = jnp.zeros((rows, D_MODEL), F32)
        for l in range(DEPTH):
            acc = acc + lax.dot_general(d_ref[l], w_ref[l].astype(BF16), NT, preferred_element_type=F32)
        o_ref[...] = acc

    grid_spec = pltpu.PrefetchScalarGridSpec(
        num_scalar_prefetch=1, grid=(1,),
        in_specs=[_whole(dcx.shape), _ada_half_spec(cols)], out_specs=_whole((rows, D_MODEL)))
    return pl.pallas_call(body, name=name, grid_spec=grid_spec, out_shape=jax.ShapeDtypeStruct((rows, D_MODEL), F32),
                          compiler_params=_params(("arbitrary",)))(c_idx, dcx, w_ada)


def _dev_sum(x, name):
    _, r, c = x.shape

    def body(x_ref, o_ref):
        v = x_ref[0]
        for d in range(1, N_DEV):
            v = v + x_ref[d]
        o_ref[...] = v

    return pl.pallas_call(body, name=name, out_shape=jax.ShapeDtypeStruct((r, c), F32))(x)


def _adam_val(w, g, m, v):
    c1 = 1.0 / (1.0 - ADAM_B1 ** ADAM_STEP)
    c2 = 1.0 / (1.0 - ADAM_B2 ** ADAM_STEP)
    nm = ADAM_B1 * m + (1.0 - ADAM_B1) * g
    nv = ADAM_B2 * v + (1.0 - ADAM_B2) * (g * g)
    return -ADAM_LR * ((nm * c1) / (jnp.sqrt(nv * c2) + ADAM_EPS) + ADAM_WD * w), nm, nv


def _small_update(tot, dcc_parts, params, n_groups, name):
    n_p = len(params)
    mod_rows = n_groups * N_MOD

    def body(tot_ref, dcc_ref, *refs):
        ins, outs = refs[:3 * n_p], refs[3 * n_p:]

        def update(p, rows, cols, g):
            w_ref, m_ref, v_ref = ins[3 * p:3 * p + 3]
            g_ref, d_ref, nm_ref, nv_ref = outs[4 * p:4 * p + 4]
            d, nm, nv = _adam_val(w_ref[rows, cols], g, m_ref[rows, cols], v_ref[rows, cols])
            g_ref[rows, cols] = g
            d_ref[rows, cols] = d
            nm_ref[rows, cols] = nm
            nv_ref[rows, cols] = nv

        acc = dcc_ref[0, 0:1, :]
        for d in range(1, N_DEV):
            acc = acc + dcc_ref[d, 0:1, :]
        c = ins[0][...]
        sg = 1.0 / (1.0 + jnp.exp(-c))
        update(0, slice(0, 1), slice(None), acc * (sg * (1.0 + c * (1.0 - sg))))
        for l in range(DEPTH):
            for i in range(N_MOD):
                g = tot_ref[l * mod_rows + i:l * mod_rows + i + 1, :]
                for grp in range(1, n_groups):
                    g = g + tot_ref[l * mod_rows + grp * N_MOD + i:l * mod_rows + grp * N_MOD + i + 1, :]
                update(1, slice(l, l + 1), slice(i * D_MODEL, (i + 1) * D_MODEL), g)
            for j in range(4):
                row = DEPTH * mod_rows + 4 * l + j
                update(2 + j, slice(l, l + 1), slice(None), tot_ref[row:row + 1, :])

    shapes = [jax.ShapeDtypeStruct(w.shape, F32) for w, _, _ in params for _ in range(4)]
    outs = pl.pallas_call(body, name=name, out_shape=shapes)(tot, dcc_parts, *[a for p in params for a in p])
    return [tuple(outs[4 * p:4 * p + 4]) for p in range(n_p)]


def _adamw(w, g, m, v, name):
    r, c = w.shape
    tr = _pick(r, (256, 128, 64, 32, 24, 16, 8))

    def body(w_ref, g_ref, m_ref, v_ref, d_ref, nm_ref, nv_ref):
        d_ref[...], nm_ref[...], nv_ref[...] = _adam_val(w_ref[...], g_ref[...], m_ref[...], v_ref[...])

    spec = pl.BlockSpec((tr, c), lambda i: (i, 0))
    return pl.pallas_call(body, name=name, grid=(r // tr,), in_specs=[spec] * 4, out_specs=[spec] * 3,
                          out_shape=[jax.ShapeDtypeStruct((r, c), F32)] * 3, compiler_params=_params(("parallel",)))(w, g, m, v)


SMALL_ROWS = 48


def _small_rows(small, sq):
    def lane_pad(v):
        return jnp.pad(v, (0, D_MODEL - v.shape[0]))[None]

    head_rows = [lane_pad(jnp.concatenate([s["q_norm"][0], s["k_norm"][0], s["sink"]])) for s in small]
    loss_row = lane_pad((0.5 / D_MODEL) * jnp.sum(sq, keepdims=True)[0])
    rows = jnp.concatenate([s["mod"].reshape(-1, D_MODEL) for s in small] + [s["gammas"] for s in small] + head_rows + [loss_row], axis=0)
    return jnp.pad(rows, ((0, SMALL_ROWS - rows.shape[0]), (0, 0)))


def _local_step(x, ctx, target, mods, gam, qn, kn, sink, w_first, w_layers, packed, kc_idx):
    nb, seq, _ = x.shape
    rt = _Rows(nb, seq, ctx.shape[1])
    tables = _rope_tables(rt)
    fuse = packed is not None
    h = jnp.concatenate([x.reshape(rt.n_lat, D_MODEL), ctx.reshape(rt.n_ctx, D_MODEL)], axis=0)
    wg = [{}, {}] if fuse else [dict(w) for w in w_layers]
    wg[0]["in"] = (w_first, 0)
    if fuse:
        wg[0]["in_own"] = (packed, W_FIRST[0])
    saved = []
    for l in range(DEPTH):
        g_pre_mix, g_post_mix, g_pre_mlp, g_post_mlp = gam[l]
        u, qkv, qkvp = _in_fwd(rt, h, g_pre_mix, mods[l], wg[l], tables, qn[l], kn[l], f"in_fwd{l}")
        if fuse and l == 0:
            o, lse_lat, w_mlp0, w_out0, w_mix1 = _attn_fwd(rt, qkvp, sink[l], None, f"attn_lat_fwd{l}",
                                                          comm=_gather_comm(packed, [W_MLP0, W_OUT0, W_MIX1]))
            wg[0].update({kind: (w_mlp0, PACK_OFF[(kind, 0)] - W_MLP0[0]) for kind in ("up", "down")})
            wg[0]["out"] = (w_out0, 0)
            wg[1] = {kind: (w_mix1, PACK_OFF[(kind, 1)] - W_MIX1[0]) for kind in ("out", "in")}
        else:
            o, lse_lat = _attn_fwd(rt, qkvp, sink[l], None, f"attn_lat_fwd{l}")
        o, lse_ctx = _attn_fwd(rt, qkvp, sink[l], o, f"attn_ctx_fwd{l}")
        mix, h1, u2 = _out_fwd(rt, o, wg[l], h, mods[l], g_post_mix, g_pre_mlp, f"out_fwd{l}")
        if fuse and l == 0:
            r, y, h2, w_mlp1 = _mlp_fwd(rt, u2, h1, wg[l], mods[l], g_post_mlp, f"mlp_fwd{l}", comm=_gather_comm(packed, [W_MLP1]))
            wg[1].update({kind: (w_mlp1, PACK_OFF[(kind, 1)] - W_MLP1[0]) for kind in ("up", "down")})
        elif l < DEPTH - 1:
            r, y, h2 = _mlp_fwd(rt, u2, h1, wg[l], mods[l], g_post_mlp, f"mlp_fwd{l}")
        else:
            r, y, dh, sq = _mlp_fwd(rt, u2, h1, wg[l], mods[l], g_post_mlp, f"mlp_fwd{l}", target=target.reshape(rt.n_lat, D_MODEL))
        saved.append((h, u, qkv, qkvp, o, lse_lat, lse_ctx, mix, h1, u2, r, y))
        h = h2

    small = [None] * DEPTH
    groups = {}
    for l in reversed(range(DEPTH)):
        g_pre_mix, g_post_mix, g_pre_mlp, g_post_mlp = gam[l]
        h0, u, qkv, qkvp, o, lse_lat, lse_ctx, mix, h1, u2, r, y = saved[l]
        mlp_group, mix_group = (G_LAYER1, G_LAYER1) if l == 1 else (G_MLP0, G_MIX0)
        hide = fuse and l == 0

        outs = _mlp_down_bwd(rt, dh, y, r, wg[l], mods[l], g_post_mlp, f"mlp_down_bwd{l}",
                             comm=_pair_comm(groups[G_LAYER1]) if hide else None)
        dy, da, d_gate_m, d_g_post_mlp = outs[:4]
        if hide:
            sum1 = _pair_sum(groups[G_LAYER1], outs[4], kc_idx, "grad_pair_sum_layer1")
        p_mlp = _wgrad_packed(rt, r, dy, "down", PACK_OFF[("down", l)] - mlp_group[0], mlp_group[1], None, f"mlp_down_wgrad{l}")
        dh1, d_sh_m, d_sc_m, d_g_pre_mlp = _mlp_up_bwd(rt, da, wg[l], h1, dh, mods[l], g_pre_mlp, f"mlp_up_bwd{l}")
        p_mlp = _wgrad_packed(rt, u2, da, "up", PACK_OFF[("up", l)] - mlp_group[0], mlp_group[1], p_mlp, f"mlp_up_wgrad{l}")
        outs = _out_bwd(rt, dh1, mix, wg[l], mods[l], g_post_mix, f"out_bwd{l}", comm=_pair_comm(p_mlp) if hide else None)
        dmix, do, d_gate_a, d_g_post_mix = outs[:4]
        if hide:
            sum0 = _pair_sum(p_mlp, outs[4], kc_idx, "grad_pair_sum_mlp0")
        p_mix = _wgrad_packed(rt, o, dmix, "out", PACK_OFF[("out", l)] - mix_group[0], mix_group[1],
                              p_mlp if l == 1 else None, f"out_wgrad{l}")
        outs = _attn_bwd(rt, qkvp, o, lse_lat, do, sink[l], None, f"attn_lat_bwd{l}",
                         comm=_chip_comm([sum1[1], sum0[1]]) if hide else None)
        dq, dkv, dkv_c, dsink1 = outs[:4]
        if hide:
            groups[G_LAYER1] = _owner_sum(sum1[0], outs[4], kc_idx, "grad_owner_sum_layer1")
            groups[G_MLP0] = _owner_sum(sum0[0], outs[5], kc_idx, "grad_owner_sum_mlp0")
        dq, dkv, dsink2 = _attn_bwd(rt, qkvp, o, lse_ctx, do, sink[l], (dq, dkv, dkv_c), f"attn_ctx_bwd{l}")
        dqkv, dh, dqn, dkn, d_sh_a, d_sc_a, d_g_pre_mix = _in_bwd(rt, dq, dkv, qkv, tables, qn[l], kn[l], wg[l], h0, dh1, mods[l],
                                                                  g_pre_mix, l == 0, f"in_bwd{l}")
        dmod = jnp.concatenate([d_sh_a, d_sc_a, d_gate_a, d_sh_m, d_sc_m, d_gate_m], axis=1)
        small[l] = dict(mod=dmod, gammas=jnp.concatenate([d_g_pre_mix, d_g_post_mix, d_g_pre_mlp, d_g_post_mlp], axis=0),
                        q_norm=dqn, k_norm=dkn, sink=(dsink1 + dsink2)[:, 0])
        gather = _gather_comm(_small_rows(small, sq), [(0, SMALL_ROWS)]) if hide else None
        outs = _wgrad_packed(rt, u, dqkv, "in", PACK_OFF[("in", l)] - mix_group[0], mix_group[1], p_mix, f"in_wgrad{l}", comm=gather)
        groups[mix_group], small_g = outs if hide else (outs, None)
        if not hide and l == 0:
            groups[G_MLP0] = p_mlp
    return sq, dh.reshape(nb, seq, D_MODEL), [groups[G_LAYER1], groups[G_MLP0], groups[G_MIX0]], small, small_g


def kernel(x, c, ctx, c_ctx, w_ada, b_ada, g_pre_mix, g_post_mix, g_pre_mlp, g_post_mlp, w_in, q_norm, k_norm, sink, w_out, w_up, w_down, loss_target, m_c_ctx, m_w_ada, m_b_ada, m_g_pre_mix, m_g_post_mix, m_g_pre_mlp, m_g_post_mlp, m_w_in, m_q_norm, m_k_norm, m_sink, m_w_out, m_w_up, m_w_down, v_c_ctx, v_w_ada, v_b_ada, v_g_pre_mix, v_g_post_mix, v_g_pre_mlp, v_g_post_mlp, v_w_in, v_q_norm, v_k_norm, v_sink, v_w_out, v_w_up, v_w_down):
    nb = x.shape[0]
    ix, iy, ic = lax.axis_index("x"), lax.axis_index("y"), lax.axis_index("c")
    chip = 2 * ix + iy
    dev = 2 * chip + ic
    ada_cols = w_ada.shape[2] // 2

    packed = _pack_local_half(w_in, w_out, w_up, w_down, ic)
    c_rows = c.reshape(8, (nb * D_MODEL) // 8)
    c_all, w_first = _comm_alone(_merge([_gather_comm(c_rows, [(0, c_rows.shape[0])]), _gather_comm(packed, [W_FIRST], copy_own=False)]),
                                 "gather_c_w_first")
    c_all = c_all.reshape(N_DEV * nb, D_MODEL)

    n_cond = N_DEV * nb + 1
    cond_rows = 16 * ((n_cond + 15) // 16)
    cond = jnp.concatenate([c_all, c_ctx[None, :], jnp.zeros((cond_rows - n_cond, D_MODEL), F32)], axis=0)
    c_idx = ic.reshape(1).astype(jnp.int32)
    kc_idx = jnp.stack([chip, ic]).astype(jnp.int32)
    b_ada_half = lax.dynamic_slice_in_dim(b_ada, dev * ada_cols, ada_cols, 1)[:, None, :]
    x_ada, mod_part = _ada_fwd(cond, w_ada, b_ada_half, c_idx, "ada_fwd")
    mod_g = _all_gather(mod_part.reshape(DEPTH * cond_rows, ada_cols), "gather_mod", False)
    mod_all = mod_g.reshape(N_DEV, DEPTH, cond_rows, ada_cols).transpose(1, 2, 0, 3).reshape(DEPTH, cond_rows, N_MOD * D_MODEL)
    mods = []
    for l in range(DEPTH):
        mine = lax.dynamic_slice_in_dim(mod_all[l], dev * nb, nb, 0)
        mods.append(jnp.concatenate([mine, mod_all[l, n_cond - 1:n_cond]], axis=0).reshape(nb + 1, N_MOD, D_MODEL))

    gam = [(g_pre_mix[l][None], g_post_mix[l][None], g_pre_mlp[l][None], g_post_mlp[l][None]) for l in range(DEPTH)]
    qn = [jnp.tile(q_norm[l], 2)[None] for l in range(DEPTH)]
    kn = [jnp.tile(k_norm[l], 2)[None] for l in range(DEPTH)]
    _, grad_x, (h_layer1, h_mlp0, p_mix0), _, small_g = _local_step(x, ctx, loss_target, mods, gam, qn, kn, [sink[l] for l in range(DEPTH)],
                                                                 w_first, None, packed, kc_idx)

    def step(w, g, m, v, name):
        shape = w.shape
        cols = shape[-1]
        outs = _adamw(w.reshape(-1, cols), g.reshape(-1, cols), m.reshape(-1, cols), v.reshape(-1, cols), name)
        return tuple(a.reshape(shape) for a in outs)

    def piece(halves, kind, l, group):
        o = PACK_OFF[(kind, l)] - group[0]
        rows = halves[:, o:o + PACK_HEIGHT[kind]]
        if kind == "in":
            rows = rows[:, :, :2 * IN_PIECE_COLS].reshape(2, 256, 2, IN_PIECE_COLS).transpose(0, 2, 1, 3)
            return rows.reshape(1024, IN_PIECE_COLS)
        return rows.reshape(2 * PACK_HEIGHT[kind], 1024)

    r1, = _comm_alone(_pair_comm(p_mix0), "grad_pair_exchange_mix0")
    a32, a16 = _pair_sum(p_mix0, r1, kc_idx, "grad_pair_sum_mix0")
    r2, = _comm_alone(_chip_comm([a16]), "grad_chip_exchange_mix0")
    h_mix0 = _owner_sum(a32, r2, kc_idx, "grad_owner_sum_mix0")
    h_layer1, h_mlp0, h_mix0 = _comm_alone(_halves_comm([h_layer1, h_mlp0, h_mix0]), "grad_halves_exchange")
    grad_w_up = jnp.stack([piece(h_mlp0, "up", 0, G_MLP0), piece(h_layer1, "up", 1, G_LAYER1)])
    grad_w_down = jnp.stack([piece(h_mlp0, "down", 0, G_MLP0), piece(h_layer1, "down", 1, G_LAYER1)])
    grad_w_in = jnp.stack([piece(h_mix0, "in", 0, G_MIX0), piece(h_layer1, "in", 1, G_LAYER1)])
    grad_w_out = jnp.stack([piece(h_mix0, "out", 0, G_MIX0), piece(h_layer1, "out", 1, G_LAYER1)])

    tot = _dev_sum(small_g, "small_sum")
    mod_rows = (nb + 1) * N_MOD
    o_head = DEPTH * mod_rows + 4 * DEPTH
    loss = tot[o_head + DEPTH, 0]
    grad_q_norm = tot[o_head:o_head + DEPTH, 0:64] + tot[o_head:o_head + DEPTH, 64:128]
    grad_k_norm = tot[o_head:o_head + DEPTH, 128:192] + tot[o_head:o_head + DEPTH, 192:256]
    grad_sink = tot[o_head:o_head + DEPTH, 256:264]

    ex = small_g[:, :DEPTH * mod_rows].reshape(N_DEV, DEPTH, nb + 1, N_MOD * D_MODEL)[:, :, :nb]
    ex = ex.transpose(1, 0, 2, 3).reshape(DEPTH, N_DEV * nb, N_MOD * D_MODEL)
    cx = tot[:DEPTH * mod_rows].reshape(DEPTH, nb + 1, N_MOD * D_MODEL)[:, nb:]
    dm = jnp.concatenate([ex, cx, jnp.zeros((DEPTH, cond_rows - n_cond, N_MOD * D_MODEL), F32)], axis=1)
    shard_cols = w_ada.shape[2]
    grad_w_ada = _ada_wgrad(x_ada, lax.dynamic_slice_in_dim(dm, chip * shard_cols, shard_cols, 2).astype(BF16), "ada_wgrad")
    dcx = jnp.pad(lax.dynamic_slice_in_dim(cx, dev * ada_cols, ada_cols, 2), ((0, 0), (0, 15), (0, 0))).astype(BF16)
    dcc = _ada_cond_bwd(dcx, w_ada, c_idx, "ada_cond_bwd")[0:8]
    dcc_g = _all_gather(dcc, "gather_cond_grad", False).reshape(N_DEV, 8, D_MODEL)

    dense_names = ["c_ctx", "b_ada", "g_pre_mix", "g_post_mix", "g_pre_mlp", "g_post_mlp"]
    dense = _small_update(tot, dcc_g, [(c_ctx[None], m_c_ctx[None], v_c_ctx[None]), (b_ada, m_b_ada, v_b_ada),
                                       (g_pre_mix, m_g_pre_mix, v_g_pre_mix), (g_post_mix, m_g_post_mix, v_g_post_mix),
                                       (g_pre_mlp, m_g_pre_mlp, v_g_pre_mlp), (g_post_mlp, m_g_post_mlp, v_g_post_mlp)],
                          nb + 1, "small_update")
    res = {n: r for n, r in zip(dense_names, dense)}
    res["c_ctx"] = tuple(a[0] for a in res["c_ctx"])
    small_names = ["q_norm", "k_norm", "sink"]
    small_w = [q_norm, k_norm, sink]
    small_gr = [grad_q_norm, grad_k_norm, grad_sink]
    small_m = [m_q_norm, m_k_norm, m_sink]
    small_v = [v_q_norm, v_k_norm, v_sink]
    sizes = [int(np.prod(w.shape)) for w in small_w]
    total = sum(sizes)
    flat_rows = 8 * ((total + 8 * D_MODEL - 1) // (8 * D_MODEL))

    def flat(arrs, fill):
        f = jnp.concatenate([a.reshape(-1) for a in arrs])
        return jnp.concatenate([f, jnp.full((flat_rows * D_MODEL - total,), fill, F32)]).reshape(flat_rows, D_MODEL)

    sd, snm, snv = _adamw(flat(small_w, 0.0), flat(small_gr, 0.0), flat(small_m, 0.0), flat(small_v, 1.0), "adamw_small")[:3]

    def unflat(f):
        f = f.reshape(-1)
        out, off = [], 0
        for w, n in zip(small_w, sizes):
            out.append(f[off:off + n].reshape(w.shape))
            off += n
        return out

    small_d, small_nm, small_nv = unflat(sd), unflat(snm), unflat(snv)
    res.update({n: (g, d, nm, nv) for n, g, d, nm, nv in zip(small_names, small_gr, small_d, small_nm, small_nv)})
    res["w_ada"] = (grad_w_ada, *step(w_ada, grad_w_ada, m_w_ada, v_w_ada, "adamw_w_ada"))
    res["w_in"] = (grad_w_in, *step(w_in, grad_w_in, m_w_in, v_w_in, "adamw_w_in"))
    res["w_out"] = (grad_w_out, *step(w_out, grad_w_out, m_w_out, v_w_out, "adamw_w_out"))
    res["w_up"] = (grad_w_up, *step(w_up, grad_w_up, m_w_up, v_w_up, "adamw_w_up"))
    res["w_down"] = (grad_w_down, *step(w_down, grad_w_down, m_w_down, v_w_down, "adamw_w_down"))

    order = ["c_ctx", "w_ada", "b_ada", "g_pre_mix", "g_post_mix", "g_pre_mlp", "g_post_mlp", "w_in", "q_norm", "k_norm", "sink", "w_out", "w_up", "w_down"]
    return (loss, grad_x, *[res[n][0] for n in order], *[res[n][1] for n in order],
            *[res[n][2] for n in order], *[res[n][3] for n in order])
```

```python
import functools

import jax
import jax.numpy as jnp
import numpy as np
from jax import lax
from jax.experimental import pallas as pl
from jax.experimental.pallas import tpu as pltpu

F32 = jnp.float32
BF16 = jnp.bfloat16

D_MODEL = 1024
HEAD_DIM = 64
GROUP = 4
WINDOW = 128
N_MOD = 6
D_FF = 4 * D_MODEL
IN_COLS = 1536
GRID_W = 64
ROPE_THETA = 10000.0
EPS = 1e-6
NEG_BIG = -1e30
Q_SCALE = HEAD_DIM ** -0.5
DEPTH = 2
N_DEV = 8

ADAM_LR = 0.001
ADAM_B1 = 0.9
ADAM_B2 = 0.999
ADAM_EPS = 1e-08
ADAM_WD = 0.01
ADAM_STEP = 10

V7X_VMEM_BYTES = 64 * 1024 * 1024
VMEM_LIMIT = V7X_VMEM_BYTES - 8 * 1024 * 1024

MESH = pl.DeviceIdType.MESH
NT = (((1,), (1,)), ((), ()))
TN = (((0,), (0,)), ((), ()))

COL_KA, COL_VA, COL_KB, COL_VB = 4, 5, 10, 11
NORMED_COLS = 640

PACK_HEIGHT = {"up": 512, "down": 512, "in": 256, "out": 128}
IN_PIECE_COLS = 384
PACK_OFF = {("up", 0): 0, ("down", 0): 512, ("in", 0): 1024, ("out", 0): 1280,
            ("up", 1): 1408, ("down", 1): 1920, ("in", 1): 2432, ("out", 1): 2688}
PACK_ROWS = 2816
W_FIRST, W_MLP0, W_OUT0, W_MLP1, W_MIX1 = (1024, 256), (0, 1024), (1280, 128), (1408, 1024), (2432, 384)
G_LAYER1, G_MLP0, G_MIX0 = (1408, 1408), (0, 1024), (1024, 384)


def _pick(n, cands):
    for t in cands:
        if n % t == 0:
            return t
    raise ValueError(f"no tile for {n}")


def _params(sem):
    return pltpu.CompilerParams(dimension_semantics=sem, vmem_limit_bytes=VMEM_LIMIT)


def _all_gather(x, name, in_hbm):
    m_per, n = x.shape
    space = pl.ANY if in_hbm else pltpu.VMEM

    def body(x_ref, out_ref, send_sems, recv_sems, local_sem):
        x_, y_, c_ = lax.axis_index("x"), lax.axis_index("y"), lax.axis_index("c")
        me, sibling = (x_, y_, c_), (x_, y_, 1 - c_)
        chips = [(1 - x_, y_), (x_, 1 - y_), (1 - x_, 1 - y_)]

        def rows(px, py, pc):
            return out_ref.at[pl.ds((4 * px + 2 * py + pc) * m_per, m_per), :]

        def copy(k, block, to, src=None):
            return pltpu.make_async_remote_copy(
                src_ref=rows(*block) if src is None else src, dst_ref=rows(*block),
                send_sem=send_sems.at[k], recv_sem=recv_sems.at[k], device_id=to, device_id_type=MESH)

        mine = pltpu.make_async_copy(x_ref, rows(*me), local_sem)
        mine.start()
        first = [copy(0, me, sibling, src=x_ref)]
        first += [copy(1 + j, me, (*chip, c_), src=x_ref) for j, chip in enumerate(chips)]
        for cp in first:
            cp.start()
        passed = [copy(4 + j, (*chip, c_), sibling) for j, chip in enumerate(chips)]
        for j, chip in enumerate(chips):
            copy(1 + j, (*chip, c_), me).wait_recv()
            passed[j].start()
        copy(0, sibling, me).wait_recv()
        for j, chip in enumerate(chips):
            copy(4 + j, (*chip, 1 - c_), me).wait_recv()
        for cp in first + passed:
            cp.wait_send()
        mine.wait()

    return pl.pallas_call(
        body, name=name,
        out_shape=jax.ShapeDtypeStruct((N_DEV * m_per, n), x.dtype),
        in_specs=[pl.BlockSpec(memory_space=space)],
        out_specs=pl.BlockSpec(memory_space=space),
        scratch_shapes=[pltpu.SemaphoreType.DMA((7,)), pltpu.SemaphoreType.DMA((7,)), pltpu.SemaphoreType.DMA],
    )(x)


class _Comm:
    def __init__(self, inputs, out_shapes, aliases, n_send, n_recv, start, finish):
        self.inputs, self.out_shapes, self.aliases = list(inputs), list(out_shapes), dict(aliases)
        self.n_send, self.n_recv, self.start, self.finish = n_send, n_recv, start, finish


def _comm_call(compute, comm, *, name, grid, in_specs, out_specs, out_shape, args, aliases, semantics, scratch=()):
    in_specs, out_specs, out_shape, args, aliases = list(in_specs), list(out_specs), list(out_shape), list(args), dict(aliases)
    scratch = list(scratch)
    if comm is None:
        return pl.pallas_call(compute, name=name, grid=grid, in_specs=in_specs, out_specs=out_specs, out_shape=out_shape,
                              input_output_aliases=aliases, scratch_shapes=scratch, compiler_params=_params(semantics))(*args)
    n_in, n_out, n_ci, n_co = len(args), len(out_shape), len(comm.inputs), len(comm.out_shapes)
    hbm = pl.BlockSpec(memory_space=pl.ANY)
    aliases.update({n_in + i: n_out + o for i, o in comm.aliases.items()})

    def body(*refs):
        ins, c_ins = refs[:n_in], refs[n_in:n_in + n_ci]
        outs, c_outs = refs[n_in + n_ci:n_in + n_ci + n_out], refs[n_in + n_ci + n_out:n_in + n_ci + n_out + n_co]
        scr = refs[n_in + n_ci + n_out + n_co:-2]
        send_sems, recv_sems = refs[-2:]
        ids = [pl.program_id(a) for a in range(len(grid))]
        first = functools.reduce(jnp.logical_and, [i == 0 for i in ids])
        last = functools.reduce(jnp.logical_and, [i == g - 1 for i, g in zip(ids, grid)])

        @pl.when(first)
        def _():
            comm.start(c_ins, c_outs, send_sems, recv_sems)

        compute(*ins, *outs, *scr)

        @pl.when(last)
        def _():
            comm.finish(c_ins, c_outs, send_sems, recv_sems)

    return pl.pallas_call(
        body, name=name, grid=grid,
        in_specs=in_specs + [hbm] * n_ci, out_specs=out_specs + [hbm] * n_co, out_shape=out_shape + comm.out_shapes,
        input_output_aliases=aliases,
        scratch_shapes=scratch + [pltpu.SemaphoreType.DMA((comm.n_send,)), pltpu.SemaphoreType.DMA((comm.n_recv,))],
        compiler_params=_params(("arbitrary",) * len(grid)),
    )(*args, *comm.inputs)


def _place():
    x_, y_, c_ = lax.axis_index("x"), lax.axis_index("y"), lax.axis_index("c")
    return x_, y_, c_, [(1 - x_, y_), (x_, 1 - y_), (1 - x_, 1 - y_)]


GATHER_SENDS, GATHER_RECVS = 8, 7


def _gather_copies(packed_ref, wg_ref, send_sems, recv_sems, rows, nth=0):
    r0, n = rows
    x_, y_, c_, chips = _place()
    me, sibling = (x_, y_, c_), (x_, y_, 1 - c_)
    src = packed_ref.at[pl.ds(r0, n), :]

    def slot(px, py, pc):
        return wg_ref.at[4 * px + 2 * py + pc]

    def copy(k, block, to, from_packed=False):
        return pltpu.make_async_remote_copy(src_ref=src if from_packed else slot(*block), dst_ref=slot(*block),
                                            send_sem=send_sems.at[GATHER_SENDS * nth + k], recv_sem=recv_sems.at[GATHER_RECVS * nth + k],
                                            device_id=to, device_id_type=MESH)

    own = [copy(0, me, sibling, True)] + [copy(1 + j, me, (*chip, c_), True) for j, chip in enumerate(chips)]
    passed = [copy(4 + j, (*chip, c_), sibling) for j, chip in enumerate(chips)]
    over_ici = [copy(1 + j, (*chip, c_), me) for j, chip in enumerate(chips)]
    from_sibling = [copy(0, sibling, me)] + [copy(4 + j, (*chip, 1 - c_), me) for j, chip in enumerate(chips)]
    mine = pltpu.make_async_copy(src, slot(*me), send_sems.at[GATHER_SENDS * nth + 7])
    return mine, own, passed, over_ici, from_sibling


def _gather_start(packed_ref, wg_ref, send_sems, recv_sems, rows, nth=0, copy_own=True):
    mine, own, _, _, _ = _gather_copies(packed_ref, wg_ref, send_sems, recv_sems, rows, nth)
    if copy_own:
        mine.start()
    for cp in own:
        cp.start()


def _gather_finish(packed_ref, wg_ref, send_sems, recv_sems, rows, nth=0, copy_own=True):
    mine, own, passed, over_ici, from_sibling = _gather_copies(packed_ref, wg_ref, send_sems, recv_sems, rows, nth)
    for arrived, onward in zip(over_ici, passed):
        arrived.wait_recv()
        onward.start()
    for arrived in from_sibling:
        arrived.wait_recv()
    for cp in own + passed:
        cp.wait_send()
    if copy_own:
        mine.wait()


def _gather_comm(packed, ranges, copy_own=True):
    shapes = [jax.ShapeDtypeStruct((N_DEV, n, packed.shape[1]), packed.dtype) for _, n in ranges]

    def start(ins, outs, ss, rs):
        for nth, rows in enumerate(ranges):
            _gather_start(ins[0], outs[nth], ss, rs, rows, nth, copy_own)

    def finish(ins, outs, ss, rs):
        for nth, rows in enumerate(ranges):
            _gather_finish(ins[0], outs[nth], ss, rs, rows, nth, copy_own)

    return _Comm([packed], shapes, {}, GATHER_SENDS * len(ranges), GATHER_RECVS * len(ranges), start, finish)


def _pair_copy(p_ref, out_ref, send_sems, recv_sems):
    x_, y_, c_, _ = _place()
    return pltpu.make_async_remote_copy(src_ref=p_ref.at[1 - c_], dst_ref=out_ref,
                                        send_sem=send_sems.at[0], recv_sem=recv_sems.at[0],
                                        device_id=(x_, y_, 1 - c_), device_id_type=MESH)


def _pair_comm(p):
    return _Comm([p], [jax.ShapeDtypeStruct(p.shape[1:], p.dtype)], {}, 1, 1,
                 lambda ins, outs, ss, rs: _pair_copy(ins[0], outs[0], ss, rs).start(),
                 lambda ins, outs, ss, rs: _pair_copy(ins[0], outs[0], ss, rs).wait())


def _chip_copies(a_refs, out_refs, send_sems, recv_sems):
    _, _, c_, chips = _place()
    return [pltpu.make_async_remote_copy(src_ref=a_ref.at[2 * tx + ty], dst_ref=o_ref.at[j],
                                         send_sem=send_sems.at[3 * g + j], recv_sem=recv_sems.at[3 * g + j],
                                         device_id=(tx, ty, c_), device_id_type=MESH)
            for g, (a_ref, o_ref) in enumerate(zip(a_refs, out_refs)) for j, (tx, ty) in enumerate(chips)]


def _chip_start(a_refs, out_refs, send_sems, recv_sems):
    for cp in _chip_copies(a_refs, out_refs, send_sems, recv_sems):
        cp.start()


def _chip_finish(a_refs, out_refs, send_sems, recv_sems):
    for cp in _chip_copies(a_refs, out_refs, send_sems, recv_sems):
        cp.wait()


def _chip_comm(arrays):
    shapes = [jax.ShapeDtypeStruct((3,) + a.shape[1:], a.dtype) for a in arrays]
    return _Comm(arrays, shapes, {}, 3 * len(arrays), 3 * len(arrays), _chip_start, _chip_finish)


def _halves_copies(in_refs, out_refs, send_sems, recv_sems):
    x_, y_, c_, _ = _place()
    return [pltpu.make_async_remote_copy(src_ref=o_ref.at[c_], dst_ref=o_ref.at[c_], send_sem=send_sems.at[i], recv_sem=recv_sems.at[i],
                                         device_id=(x_, y_, 1 - c_), device_id_type=MESH)
            for i, o_ref in enumerate(out_refs)]


def _halves_start(in_refs, out_refs, send_sems, recv_sems):
    for cp in _halves_copies(in_refs, out_refs, send_sems, recv_sems):
        cp.start()


def _halves_finish(in_refs, out_refs, send_sems, recv_sems):
    for cp in _halves_copies(in_refs, out_refs, send_sems, recv_sems):
        cp.wait()


def _halves_comm(arrays):
    shapes = [jax.ShapeDtypeStruct(a.shape, a.dtype) for a in arrays]
    return _Comm(arrays, shapes, {i: i for i in range(len(arrays))}, len(arrays), len(arrays), _halves_start, _halves_finish)


class _SemSlice:
    class _At:
        def __init__(self, sems, first):
            self.sems, self.first = sems, first

        def __getitem__(self, k):
            return self.sems.at[self.first + k]

    def __init__(self, sems, first):
        self.at = _SemSlice._At(sems, first)


def _merge(comms):
    inputs = [a for c in comms for a in c.inputs]
    shapes = [s for c in comms for s in c.out_shapes]
    aliases, spans = {}, []
    i0 = o0 = s0 = r0 = 0
    for c in comms:
        aliases.update({i0 + i: o0 + o for i, o in c.aliases.items()})
        spans.append((slice(i0, i0 + len(c.inputs)), slice(o0, o0 + len(c.out_shapes)), s0, r0))
        i0, o0, s0, r0 = i0 + len(c.inputs), o0 + len(c.out_shapes), s0 + c.n_send, r0 + c.n_recv

    def start(ins, outs, ss, rs):
        for c, (i, o, s, r) in zip(comms, spans):
            c.start(ins[i], outs[o], _SemSlice(ss, s), _SemSlice(rs, r))

    def finish(ins, outs, ss, rs):
        for c, (i, o, s, r) in zip(comms, spans):
            c.finish(ins[i], outs[o], _SemSlice(ss, s), _SemSlice(rs, r))

    return _Comm(inputs, shapes, aliases, s0, r0, start, finish)


def _comm_alone(comm, name):
    n_ci = len(comm.inputs)
    hbm = pl.BlockSpec(memory_space=pl.ANY)

    def body(*refs):
        c_ins, c_outs, send_sems, recv_sems = refs[:n_ci], refs[n_ci:-2], refs[-2], refs[-1]
        comm.start(c_ins, c_outs, send_sems, recv_sems)
        comm.finish(c_ins, c_outs, send_sems, recv_sems)

    return pl.pallas_call(
        body, name=name, out_shape=comm.out_shapes, in_specs=[hbm] * n_ci, out_specs=[hbm] * len(comm.out_shapes),
        input_output_aliases=comm.aliases,
        scratch_shapes=[pltpu.SemaphoreType.DMA((comm.n_send,)), pltpu.SemaphoreType.DMA((comm.n_recv,))],
    )(*comm.inputs)


SUM_TILES = (704, 512, 384, 320, 256, 192, 128, 64)


def _pair_sum(p, r1, kc_idx, name):
    _, _, n, c = p.shape
    tr = _pick(n, SUM_TILES)

    def body(s_ref, p_ref, r_ref, o32_ref, o16_ref):
        v = p_ref[...] + r_ref[...]
        o16_ref[...] = v.astype(BF16)

        @pl.when(pl.program_id(1) == s_ref[0])
        def _():
            o32_ref[...] = v

    blk = pl.BlockSpec((None, tr, c), lambda i, j, s: (j, i, 0))
    grid_spec = pltpu.PrefetchScalarGridSpec(
        num_scalar_prefetch=1, grid=(n // tr, 4),
        in_specs=[pl.BlockSpec((None, None, tr, c), lambda i, j, s: (s[1], j, i, 0)), blk],
        out_specs=[pl.BlockSpec((tr, c), lambda i, j, s: (i, 0)), blk])
    return pl.pallas_call(
        body, name=name, grid_spec=grid_spec,
        out_shape=[jax.ShapeDtypeStruct((n, c), F32), jax.ShapeDtypeStruct((4, n, c), BF16)],
        compiler_params=_params(("arbitrary", "arbitrary")),
    )(kc_idx, p, r1)


def _owner_sum(a32, r2, kc_idx, name):
    r, c = a32.shape
    tr = _pick(r, SUM_TILES)

    def body(s_ref, a_ref, r_ref, o_ref):
        v = a_ref[...]
        for j in range(3):
            v = v + r_ref[j].astype(F32)
        o_ref[...] = v

    grid_spec = pltpu.PrefetchScalarGridSpec(
        num_scalar_prefetch=1, grid=(r // tr,),
        in_specs=[pl.BlockSpec((tr, c), lambda i, s: (i, 0)),
                  pl.BlockSpec((3, tr, c), lambda i, s: (0, i, 0))],
        out_specs=pl.BlockSpec((None, tr, c), lambda i, s: (s[1], i, 0)))
    return pl.pallas_call(
        body, name=name, grid_spec=grid_spec,
        out_shape=jax.ShapeDtypeStruct((2, r, c), F32),
        compiler_params=_params(("arbitrary",)),
    )(kc_idx, a32, r2)


def _pack_local_half(w_in_s, w_out_s, w_up_s, w_down_s, c_idx):
    parts, row = [], 0
    for (kind, l), off in sorted(PACK_OFF.items(), key=lambda kv: kv[1]):
        if off > row:
            parts.append(jnp.zeros((off - row, 1024), BF16))
        if kind == "up":
            p = lax.dynamic_slice_in_dim(w_up_s[l], c_idx * 512, 512, 0)
        elif kind == "down":
            p = lax.dynamic_slice_in_dim(w_down_s[l], c_idx * 512, 512, 0)
        elif kind == "in":
            p = lax.dynamic_slice_in_dim(w_in_s[l], c_idx * 512, 512, 0)
            p = p.reshape(2, 256, IN_PIECE_COLS).transpose(1, 0, 2).reshape(256, 2 * IN_PIECE_COLS)
            p = jnp.pad(p, ((0, 0), (0, 1024 - 2 * IN_PIECE_COLS)))
        else:
            p = lax.dynamic_slice_in_dim(w_out_s[l], c_idx * 128, 128, 0)
        parts.append(p.astype(BF16))
        row = off + PACK_HEIGHT[kind]
    return jnp.concatenate(parts, axis=0)


def _unpack_in_pieces(w_ref, own_ref, w_scr):
    if own_ref is not None:
        me = 4 * lax.axis_index("x") + 2 * lax.axis_index("y") + lax.axis_index("c")
    for d in range(N_DEV):
        k, c = d // 2, d % 2
        for t in range(2):
            piece = w_ref[d, :, t * IN_PIECE_COLS:(t + 1) * IN_PIECE_COLS]
            if own_ref is not None:
                piece = jnp.where(me == d, own_ref[:, t * IN_PIECE_COLS:(t + 1) * IN_PIECE_COLS], piece)
            w_scr[c * 512 + t * 256:c * 512 + (t + 1) * 256, k * IN_PIECE_COLS:(k + 1) * IN_PIECE_COLS] = piece


def _in_weight_operands(wg):
    specs, args = [_gathered_spec(wg, "in")], [wg["in"][0]]
    if "in_own" in wg:
        own, off = wg["in_own"]
        h = PACK_HEIGHT["in"]
        assert off % h == 0
        specs.append(pl.BlockSpec((h, 1024), lambda *_: (off // h, 0), pipeline_mode=pl.Buffered(1)))
        args.append(own)
    return specs, args


class _Rows:
    def __init__(self, nb, seq, ctx):
        self.nb, self.seq, self.ctx = nb, seq, ctx
        self.n_lat, self.n_ctx = nb * seq, nb * ctx
        self.rows = self.n_lat + self.n_ctx
        self.tm = _pick(np.gcd(seq, self.n_ctx), (512, 256, 128))
        self.tiles_per_ex = seq // self.tm
        self.n_tiles = self.rows // self.tm
        self.n_lat_tiles = self.n_lat // self.tm
        self.groups = nb + 1

    def group(self, i):
        return jnp.minimum(i // self.tiles_per_ex, self.nb)

    def first_of_group(self, i):
        return jnp.logical_and(i % self.tiles_per_ex == 0, i <= self.n_lat_tiles)


def _mod_spec(rt):
    return pl.BlockSpec((1, N_MOD, D_MODEL), lambda i: (rt.group(i), 0, 0))


def _row_spec(rt, cols):
    return pl.BlockSpec((rt.tm, cols), lambda i: (i, 0))


def _vec_spec(cols):
    return pl.BlockSpec((1, cols), lambda i: (0, 0))


def _group_spec(rt):
    return pl.BlockSpec((1, 1, D_MODEL), lambda i: (rt.group(i), 0, 0))


def _gathered_spec(wg, kind):
    h, off = PACK_HEIGHT[kind], wg[kind][1]
    assert off % h == 0, (kind, off)
    return pl.BlockSpec((N_DEV, h, 1024), lambda *_: (0, off // h, 0), pipeline_mode=pl.Buffered(1))


def _group_shape(rt):
    return jax.ShapeDtypeStruct((rt.groups, 1, D_MODEL), F32)


def _vec_shape(cols=D_MODEL):
    return jax.ShapeDtypeStruct((1, cols), F32)


def _rms_inv(v):
    return lax.rsqrt(jnp.mean(v * v, axis=-1, keepdims=True) + EPS)


def _norm_mod_val(h_, g_, mod_ref, i_shift, i_scale):
    n = h_ * _rms_inv(h_) * g_
    return n * (1.0 + mod_ref[0, i_scale:i_scale + 1, :]) + mod_ref[0, i_shift:i_shift + 1, :]


def _post_norm_val(h_, z_, g_, mod_ref, i_gate):
    return h_ + mod_ref[0, i_gate:i_gate + 1, :] * (z_ * _rms_inv(z_) * g_)


def _post_norm_bwd_val(dh_, z_, g_, gate):
    rinv = _rms_inv(z_)
    n0 = z_ * rinv
    dn = dh_ * gate * g_
    dz = rinv * (dn - n0 * jnp.mean(dn * n0, axis=-1, keepdims=True))
    return dz, jnp.sum(dh_ * n0 * g_, axis=0, keepdims=True), jnp.sum(dh_ * gate * n0, axis=0, keepdims=True)


def _norm_mod_bwd_val(du_, h_, g_, one_sc):
    rinv = _rms_inv(h_)
    n0 = h_ * rinv
    dn = du_ * g_ * one_sc
    dh = rinv * (dn - n0 * jnp.mean(dn * n0, axis=-1, keepdims=True))
    return (dh, jnp.sum(du_, axis=0, keepdims=True), jnp.sum(du_ * n0 * g_, axis=0, keepdims=True),
            jnp.sum(du_ * one_sc * n0, axis=0, keepdims=True))


def _accumulate(rt, i, group_pairs, global_pairs):
    @pl.when(rt.first_of_group(i))
    def _():
        for ref, _ in group_pairs:
            ref[...] = jnp.zeros_like(ref)

    @pl.when(i == 0)
    def _():
        for ref, _ in global_pairs:
            ref[...] = jnp.zeros_like(ref)

    for ref, val in group_pairs:
        ref[0] += val
    for ref, val in global_pairs:
        ref[...] += val


def _rope_tables(rt):
    pos = np.arange(rt.seq)
    axis_dim = HEAD_DIM // 2
    inv = (ROPE_THETA ** (-np.arange(0, axis_dim, 2, dtype=np.float32) / axis_dim)).astype(np.float32)
    ang_r = (pos // GRID_W).astype(np.float32)[:, None] * inv[None, :]
    ang_c = (pos % GRID_W).astype(np.float32)[:, None] * inv[None, :]
    cr, sr, cc, sc = np.cos(ang_r), np.sin(ang_r), np.cos(ang_c), np.sin(ang_c)
    zero = np.zeros_like(sr)
    cos = np.concatenate([cr, cr, cc, cc], axis=1)
    s_lo = np.concatenate([zero, sr, zero, sc], axis=1)
    s_hi = np.concatenate([-sr, zero, -sc, zero], axis=1)

    def full(t, ctx_value):
        return jnp.asarray(np.concatenate([np.tile(t, (1, 2)), np.full((rt.tm, 128), ctx_value)], axis=0), F32)

    return full(cos, 1.0), full(s_lo, 0.0), full(s_hi, 0.0)


def _table_spec(rt):
    return pl.BlockSpec((rt.tm, 128), lambda i: (jnp.where(i < rt.n_lat_tiles, i % rt.tiles_per_ex, rt.tiles_per_ex), 0))


def _head_mean(x):
    r = lax.broadcasted_iota(jnp.int32, (128, 128), 0) // HEAD_DIM
    c = lax.broadcasted_iota(jnp.int32, (128, 128), 1) // HEAD_DIM
    ones = jnp.where(r == c, 1.0 / HEAD_DIM, 0.0).astype(F32)
    return jnp.dot(x, ones, preferred_element_type=F32, precision=lax.Precision.HIGH)


def _head_stats(t):
    return lax.rsqrt(_head_mean(t * t) + EPS)


def _prep_fwd_body(tm, qkv_ref, c, s1, s2, qn, kn, out_ref):
    def rope(t):
        return t * c + pltpu.roll(t, 16, 1) * s1 + pltpu.roll(t, 112, 1) * s2

    for j in range(12):
        t = qkv_ref[:, j * 128:(j + 1) * 128]
        if j < 4:
            t = rope(t * _head_stats(t) * qn) * Q_SCALE
        elif j == COL_KA:
            t = rope(t * _head_stats(t) * kn)
        elif 6 <= j < 10:
            t = rope(t) * Q_SCALE
        elif j == COL_KB:
            t = rope(t)
        out_ref[:, j * 128:(j + 1) * 128] = t.astype(BF16)


def _prep_bwd_body(dq_ref, dkv_ref, qkv_ref, c, s1, s2, qn, kn, out_ref):
    rows = slice(None)

    def rope_bwd(d):
        return d * c + pltpu.roll(d * s1, 112, 1) + pltpu.roll(d * s2, 16, 1)

    def norm_bwd(t, g, dy):
        rinv = _head_stats(t)
        n = t * rinv
        dn = dy * g
        return rinv * (dn - n * _head_mean(dn * n)), jnp.sum(dy * n, axis=0, keepdims=True)

    dqn = jnp.zeros((1, 128), F32)
    dkn = jnp.zeros((1, 128), F32)
    for j in range(12):
        if j < 4:
            d, dg = norm_bwd(qkv_ref[rows, j * 128:(j + 1) * 128], qn, rope_bwd(dq_ref[rows, j * 128:(j + 1) * 128] * Q_SCALE))
            dqn = dqn + dg
        elif j == COL_KA:
            d, dg = norm_bwd(qkv_ref[rows, j * 128:(j + 1) * 128], kn, rope_bwd(dkv_ref[rows, 0:128]))
            dkn = dkn + dg
        elif j == COL_VA:
            d = dkv_ref[rows, 128:256]
        elif j < 10:
            d = rope_bwd(dq_ref[rows, (j - 2) * 128:(j - 1) * 128] * Q_SCALE)
        elif j == COL_KB:
            d = rope_bwd(dkv_ref[rows, 256:384])
        else:
            d = dkv_ref[rows, 384:512]
        out_ref[rows, j * 128:(j + 1) * 128] = d.astype(BF16)
    return dqn, dkn


def _in_fwd(rt, h, gamma, mod, wg, tables, qn, kn, name):
    w_specs, w_args = _in_weight_operands(wg)
    n_w = len(w_args)

    def body(h_ref, g_ref, mod_ref, *rest):
        c_ref, s1_ref, s2_ref, qn_ref, kn_ref, u_ref, qkn_ref, qkvp_ref, qkv_ref, w_scr = rest[n_w:]

        @pl.when(pl.program_id(0) == 0)
        def _():
            _unpack_in_pieces(rest[0], rest[1] if n_w == 2 else None, w_scr)

        u = _norm_mod_val(h_ref[...], g_ref[...], mod_ref, 0, 1).astype(BF16)
        u_ref[...] = u
        qkv_ref[...] = jnp.dot(u, w_scr[...], preferred_element_type=F32)
        qkn_ref[...] = qkv_ref[:, 0:NORMED_COLS]
        _prep_fwd_body(rt.tm, qkv_ref, c_ref[...], s1_ref[...], s2_ref[...], qn_ref[...], kn_ref[...], qkvp_ref)

    return pl.pallas_call(
        body, name=name, grid=(rt.n_tiles,),
        in_specs=[_row_spec(rt, D_MODEL), _vec_spec(D_MODEL), _mod_spec(rt)] + w_specs + [_table_spec(rt)] * 3 + [_vec_spec(128)] * 2,
        out_specs=[_row_spec(rt, D_MODEL), _row_spec(rt, NORMED_COLS), _row_spec(rt, IN_COLS)],
        out_shape=[jax.ShapeDtypeStruct((rt.rows, D_MODEL), BF16), jax.ShapeDtypeStruct((rt.rows, NORMED_COLS), F32),
                   jax.ShapeDtypeStruct((rt.rows, IN_COLS), BF16)],
        scratch_shapes=[pltpu.VMEM((rt.tm, IN_COLS), F32), pltpu.VMEM((D_MODEL, IN_COLS), BF16)],
        compiler_params=_params(("arbitrary",)),
    )(h, gamma, mod, *w_args, *tables, qn, kn)


def _in_bwd(rt, dq, dkv, qkv, tables, qn, kn, wg, h, dres, mod, gamma, latent_only, name, comm=None):
    last = rt.n_lat_tiles - 1
    w_specs, w_args = _in_weight_operands(wg)
    n_w = len(w_args)

    def body(dq_ref, dkv_ref, qkv_ref, c_ref, s1_ref, s2_ref, qn_ref, kn_ref, *rest):
        h_ref, dres_ref, mod_ref, g_ref, dqkv_ref, dh_ref, dqn_ref, dkn_ref, dsh_ref, dsc_ref, dg_ref, w_scr = rest[n_w:]
        i = pl.program_id(0)

        @pl.when(i == 0)
        def _():
            _unpack_in_pieces(rest[0], rest[1] if n_w == 2 else None, w_scr)

        dqn, dkn = _prep_bwd_body(dq_ref, dkv_ref, qkv_ref, c_ref[...], s1_ref[...], s2_ref[...], qn_ref[...], kn_ref[...], dqkv_ref)
        du = lax.dot_general(dqkv_ref[...], w_scr[...], NT, preferred_element_type=F32)
        dh, dsh, dsc, dg = _norm_mod_bwd_val(du, h_ref[...], g_ref[...], 1.0 + mod_ref[0, 1:2, :])
        if latent_only:
            @pl.when(i <= last)
            def _():
                dh_ref[...] = dres_ref[...] + dh
        else:
            dh_ref[...] = dres_ref[...] + dh
        _accumulate(rt, i, [(dsh_ref, dsh), (dsc_ref, dsc)], [(dg_ref, dg), (dqn_ref, dqn), (dkn_ref, dkn)])

    dh_spec = pl.BlockSpec((rt.tm, D_MODEL), lambda i: (jnp.minimum(i, last), 0)) if latent_only else _row_spec(rt, D_MODEL)
    return _comm_call(
        body, comm, name=name, grid=(rt.n_tiles,),
        in_specs=[_row_spec(rt, 1024), _row_spec(rt, 512), _row_spec(rt, NORMED_COLS)] + [_table_spec(rt)] * 3 + [_vec_spec(128)] * 2
        + w_specs + [_row_spec(rt, D_MODEL), _row_spec(rt, D_MODEL), _mod_spec(rt), _vec_spec(D_MODEL)],
        out_specs=[_row_spec(rt, IN_COLS), dh_spec, _vec_spec(128), _vec_spec(128),
                   _group_spec(rt), _group_spec(rt), _vec_spec(D_MODEL)],
        out_shape=[jax.ShapeDtypeStruct((rt.rows, IN_COLS), BF16),
                   jax.ShapeDtypeStruct((rt.n_lat if latent_only else rt.rows, D_MODEL), F32),
                   _vec_shape(128), _vec_shape(128), _group_shape(rt), _group_shape(rt), _vec_shape()],
        args=[dq, dkv, qkv, *tables, qn, kn, *w_args, h, dres, mod, gamma], aliases={}, semantics=("arbitrary",),
        scratch=[pltpu.VMEM((D_MODEL, IN_COLS), BF16)])


def _out_fwd(rt, o, wg, h, mod, g_post_mix, g_pre_mlp, name):
    def body(o_ref, w_ref, h_ref, mod_ref, gpost_ref, gpre_ref, mix_ref, h1_ref, u2_ref):
        mix = jnp.dot(o_ref[...], w_ref[...].reshape(D_MODEL, D_MODEL), preferred_element_type=F32)
        mix_ref[...] = mix
        h1 = _post_norm_val(h_ref[...], mix, gpost_ref[...], mod_ref, 2)
        h1_ref[...] = h1
        u2_ref[...] = _norm_mod_val(h1, gpre_ref[...], mod_ref, 3, 4).astype(BF16)

    return pl.pallas_call(
        body, name=name, grid=(rt.n_tiles,),
        in_specs=[_row_spec(rt, D_MODEL), _gathered_spec(wg, "out"), _row_spec(rt, D_MODEL), _mod_spec(rt),
                  _vec_spec(D_MODEL), _vec_spec(D_MODEL)],
        out_specs=[_row_spec(rt, D_MODEL)] * 3,
        out_shape=[jax.ShapeDtypeStruct((rt.rows, D_MODEL), F32), jax.ShapeDtypeStruct((rt.rows, D_MODEL), F32),
                   jax.ShapeDtypeStruct((rt.rows, D_MODEL), BF16)],
        compiler_params=_params(("parallel",)),
    )(o, wg["out"][0], h, mod, g_post_mix, g_pre_mlp)


def _out_bwd(rt, dh1, mix, wg, mod, g_post_mix, name, comm=None):
    def body(dh_ref, mix_ref, w_ref, mod_ref, g_ref, dmix_ref, do_ref, dgate_ref, dg_ref):
        i = pl.program_id(0)
        dz, dgate, dg = _post_norm_bwd_val(dh_ref[...], mix_ref[...], g_ref[...], mod_ref[0, 2:3, :])
        dzb = dz.astype(BF16)
        dmix_ref[...] = dzb
        do_ref[...] = lax.dot_general(dzb, w_ref[...].reshape(D_MODEL, D_MODEL), NT, preferred_element_type=F32).astype(BF16)
        _accumulate(rt, i, [(dgate_ref, dgate)], [(dg_ref, dg)])

    return _comm_call(
        body, comm, name=name, grid=(rt.n_tiles,),
        in_specs=[_row_spec(rt, D_MODEL), _row_spec(rt, D_MODEL), _gathered_spec(wg, "out"), _mod_spec(rt), _vec_spec(D_MODEL)],
        out_specs=[_row_spec(rt, D_MODEL), _row_spec(rt, D_MODEL), _group_spec(rt), _vec_spec(D_MODEL)],
        out_shape=[jax.ShapeDtypeStruct((rt.rows, D_MODEL), BF16), jax.ShapeDtypeStruct((rt.rows, D_MODEL), BF16),
                   _group_shape(rt), _vec_shape()],
        args=[dh1, mix, wg["out"][0], mod, g_post_mix], aliases={}, semantics=("arbitrary",))


def _w_chunk(w_ref, k):
    return w_ref[2 * k:2 * k + 2].reshape(1024, 1024)


def _mlp_fwd(rt, u2, h1, wg, mod, g_post_mlp, name, comm=None, target=None):
    last = rt.n_lat_tiles - 1

    def body(u2_ref, h1_ref, wu_ref, wd_ref, mod_ref, g_ref, *rest):
        u2_ = u2_ref[...]
        y = jnp.zeros((rt.tm, D_MODEL), F32)
        for k in range(D_FF // 1024):
            a = jnp.maximum(jnp.dot(u2_, _w_chunk(wu_ref, k), preferred_element_type=F32), 0.0)
            rest[-3 if target is None else -4][:, k * 1024:(k + 1) * 1024] = a.astype(BF16)
            y = y + jnp.dot((a * a).astype(BF16), _w_chunk(wd_ref, k), preferred_element_type=F32)
        h2 = _post_norm_val(h1_ref[...], y, g_ref[...], mod_ref, 5)
        if target is None:
            _, y_ref, h2_ref = rest
            y_ref[...] = y
            h2_ref[...] = h2
        else:
            t_ref, _, y_ref, dh_ref, sq_ref = rest
            y_ref[...] = y
            i = pl.program_id(0)

            @pl.when(i == 0)
            def _():
                sq_ref[...] = jnp.zeros_like(sq_ref)

            @pl.when(i <= last)
            def _():
                e = h2 - t_ref[...]
                dh_ref[...] = e * (1.0 / D_MODEL)
                sq_ref[...] += jnp.sum(e * e, axis=0, keepdims=True)

            @pl.when(i > last)
            def _():
                dh_ref[...] = jnp.zeros_like(dh_ref)

    in_specs = [_row_spec(rt, D_MODEL), _row_spec(rt, D_MODEL), _gathered_spec(wg, "up"), _gathered_spec(wg, "down"),
                _mod_spec(rt), _vec_spec(D_MODEL)]
    args = [u2, h1, wg["up"][0], wg["down"][0], mod, g_post_mlp]
    out_specs = [_row_spec(rt, D_FF), _row_spec(rt, D_MODEL), _row_spec(rt, D_MODEL)]
    out_shape = [jax.ShapeDtypeStruct((rt.rows, D_FF), BF16), jax.ShapeDtypeStruct((rt.rows, D_MODEL), F32),
                 jax.ShapeDtypeStruct((rt.rows, D_MODEL), F32)]
    if target is not None:
        in_specs.append(pl.BlockSpec((rt.tm, D_MODEL), lambda i: (jnp.minimum(i, last), 0)))
        args.append(target)
        out_specs.append(_vec_spec(D_MODEL))
        out_shape.append(_vec_shape())
    return _comm_call(body, comm, name=name, grid=(rt.n_tiles,), in_specs=in_specs, out_specs=out_specs, out_shape=out_shape,
                      args=args, aliases={}, semantics=("parallel",) if target is None else ("arbitrary",))


def _mlp_down_bwd(rt, dh, y, ra, wg, mod, g_post_mlp, name, comm=None):
    def body(dh_ref, y_ref, ra_ref, wd_ref, mod_ref, g_ref, dy_ref, da_ref, dgate_ref, dg_ref):
        i = pl.program_id(0)
        dz, dgate, dg = _post_norm_bwd_val(dh_ref[...], y_ref[...], g_ref[...], mod_ref[0, 5:6, :])
        dyb = dz.astype(BF16)
        dy_ref[...] = dyb
        for k in range(D_FF // 1024):
            dr = lax.dot_general(dyb, _w_chunk(wd_ref, k), NT, preferred_element_type=F32)
            da_ref[:, k * 1024:(k + 1) * 1024] = (dr * (2.0 * ra_ref[:, k * 1024:(k + 1) * 1024].astype(F32))).astype(BF16)
        _accumulate(rt, i, [(dgate_ref, dgate)], [(dg_ref, dg)])

    return _comm_call(
        body, comm, name=name, grid=(rt.n_tiles,),
        in_specs=[_row_spec(rt, D_MODEL), _row_spec(rt, D_MODEL), _row_spec(rt, D_FF), _gathered_spec(wg, "down"),
                  _mod_spec(rt), _vec_spec(D_MODEL)],
        out_specs=[_row_spec(rt, D_MODEL), _row_spec(rt, D_FF), _group_spec(rt), _vec_spec(D_MODEL)],
        out_shape=[jax.ShapeDtypeStruct((rt.rows, D_MODEL), BF16), jax.ShapeDtypeStruct((rt.rows, D_FF), BF16),
                   _group_shape(rt), _vec_shape()],
        args=[dh, y, ra, wg["down"][0], mod, g_post_mlp], aliases={}, semantics=("arbitrary",))


def _mlp_up_bwd(rt, da, wg, h1, dh, mod, g_pre_mlp, name):
    def body(da_ref, wu_ref, h1_ref, dh_ref, mod_ref, g_ref, dh1_ref, dsh_ref, dsc_ref, dg_ref):
        i = pl.program_id(0)
        du = jnp.zeros((rt.tm, D_MODEL), F32)
        for k in range(D_FF // 1024):
            du = du + lax.dot_general(da_ref[:, k * 1024:(k + 1) * 1024], _w_chunk(wu_ref, k), NT, preferred_element_type=F32)
        d, dsh, dsc, dg = _norm_mod_bwd_val(du, h1_ref[...], g_ref[...], 1.0 + mod_ref[0, 4:5, :])
        dh1_ref[...] = dh_ref[...] + d
        _accumulate(rt, i, [(dsh_ref, dsh), (dsc_ref, dsc)], [(dg_ref, dg)])

    return pl.pallas_call(
        body, name=name, grid=(rt.n_tiles,),
        in_specs=[_row_spec(rt, D_FF), _gathered_spec(wg, "up"), _row_spec(rt, D_MODEL), _row_spec(rt, D_MODEL),
                  _mod_spec(rt), _vec_spec(D_MODEL)],
        out_specs=[_row_spec(rt, D_MODEL), _group_spec(rt), _group_spec(rt), _vec_spec(D_MODEL)],
        out_shape=[jax.ShapeDtypeStruct((rt.rows, D_MODEL), F32), _group_shape(rt), _group_shape(rt), _vec_shape()],
        compiler_params=_params(("arbitrary",)),
    )(da, wg["up"][0], h1, dh, mod, g_pre_mlp)


def _wgrad_packed(rt, a, b, kind, off, n_rows, p_prev, name, comm=None):
    h = PACK_HEIGHT[kind]
    tk = rt.tm
    assert off % h == 0, (kind, off)

    def body(a_ref, b_ref, *rest):
        o_ref = rest[-1]
        i = pl.program_id(0)

        @pl.when(i == 0)
        def _():
            o_ref[...] = jnp.zeros_like(o_ref)

        if kind == "in":
            res = lax.dot_general(a_ref[...], b_ref[...], TN, preferred_element_type=F32)
            for k in range(4):
                for c in range(2):
                    for t in range(2):
                        o_ref[c, k, :, t * IN_PIECE_COLS:(t + 1) * IN_PIECE_COLS] += \
                            res[c * 512 + t * h:c * 512 + (t + 1) * h, k * IN_PIECE_COLS:(k + 1) * IN_PIECE_COLS]
        elif kind == "out":
            res = lax.dot_general(a_ref[...], b_ref[...], TN, preferred_element_type=F32)
            for k in range(4):
                for c in range(2):
                    o_ref[c, k] += res[(2 * k + c) * h:(2 * k + c + 1) * h]
        else:
            for k in range(4):
                if kind == "up":
                    res = lax.dot_general(a_ref[...], b_ref[:, k * 1024:(k + 1) * 1024], TN, preferred_element_type=F32)
                else:
                    ra = a_ref[:, k * 1024:(k + 1) * 1024].astype(F32)
                    res = lax.dot_general((ra * ra).astype(BF16), b_ref[...], TN, preferred_element_type=F32)
                o_ref[0, k] += res[0:h]
                o_ref[1, k] += res[h:2 * h]

    in_specs = [pl.BlockSpec((tk, a.shape[1]), lambda i: (i, 0)), pl.BlockSpec((tk, b.shape[1]), lambda i: (i, 0))]
    args = [a, b]
    aliases = {}
    if p_prev is not None:
        in_specs.append(pl.BlockSpec(memory_space=pl.ANY))
        args.append(p_prev)
        aliases = {2: 0}
    outs = _comm_call(
        body, comm, name=name, grid=(rt.rows // tk,),
        in_specs=in_specs,
        out_specs=[pl.BlockSpec((2, 4, h, 1024), lambda i: (0, 0, off // h, 0))],
        out_shape=[jax.ShapeDtypeStruct((2, 4, n_rows, 1024), F32)],
        args=args, aliases=aliases, semantics=("arbitrary",))
    return outs[0] if comm is None else outs


def _ada_wgrad(xs, dm, name):
    depth, _, cols = dm.shape

    def body(x_ref, d_ref, o_ref):
        for l in range(depth):
            o_ref[l] = lax.dot_general(x_ref[...], d_ref[l], TN, preferred_element_type=F32)

    return pl.pallas_call(body, name=name, out_shape=jax.ShapeDtypeStruct((depth, xs.shape[1], cols), F32),
                          compiler_params=pltpu.CompilerParams(vmem_limit_bytes=VMEM_LIMIT))(xs, dm)


def _stack_heads(x, kvi):
    x = x.astype(F32)
    tq = x.shape[0]
    lane = lax.broadcasted_iota(jnp.int32, (tq, 128), 1)
    keep = lane < HEAD_DIM if kvi == 0 else lane >= HEAD_DIM
    parts = []
    for p in range(2):
        pair = x[:, p * 128:(p + 1) * 128]
        swapped = pltpu.roll(pair, HEAD_DIM, 1)
        lo_head, hi_head = (pair, swapped) if kvi == 0 else (swapped, pair)
        parts += [jnp.where(keep, lo_head, 0.0), jnp.where(keep, hi_head, 0.0)]
    return jnp.concatenate(parts, axis=0).astype(BF16)


def _unstack_heads(o4, kvi):
    tq = o4.shape[0] // GROUP
    lane = lax.broadcasted_iota(jnp.int32, (tq, 128), 1)
    outs = []
    for p in range(2):
        r_lo, r_hi = o4[(2 * p) * tq:(2 * p + 1) * tq], o4[(2 * p + 1) * tq:(2 * p + 2) * tq]
        if kvi == 0:
            lo, hi = r_lo, pltpu.roll(r_hi, HEAD_DIM, 1)
        else:
            lo, hi = pltpu.roll(r_lo, HEAD_DIM, 1), r_hi
        outs.append(jnp.where(lane < HEAD_DIM, lo, hi))
    return jnp.concatenate(outs, axis=1)


def _per_head(shape, axis, tq, values):
    head = lax.broadcasted_iota(jnp.int32, shape, axis) // tq
    out = jnp.zeros(shape, F32)
    for g in range(GROUP):
        out = jnp.where(head == g, values[g], out)
    return out


KEY_CHUNK = 512


def _key_chunks(k_ref, v_ref, n, kc=KEY_CHUNK):
    kc = min(kc, n)
    return [(k_ref[c * kc:(c + 1) * kc, :], v_ref[c * kc:(c + 1) * kc, :], None) for c in range(n // kc)]


def _softmax_fwd(qs, chunks, sink_col):
    logits = []
    for k, _, mask in chunks:
        s = lax.dot_general(qs, k, NT, preferred_element_type=F32)
        logits.append(s if mask is None else jnp.where(mask, s, NEG_BIG))
    m = functools.reduce(jnp.maximum, [jnp.max(s, axis=1, keepdims=True) for s in logits])
    if sink_col is not None:
        m = jnp.maximum(m, sink_col)
    l = jnp.zeros_like(m) if sink_col is None else jnp.exp(sink_col - m)
    acc = jnp.zeros((qs.shape[0], 128), F32)
    for s, (_, v, _) in zip(logits, chunks):
        p = jnp.exp(s - m)
        l = l + jnp.sum(p, axis=1, keepdims=True)
        acc = acc + jnp.dot(p.astype(BF16), v, preferred_element_type=F32)
    return acc / l, m + jnp.log(l)


def _to_rows(col):
    return jnp.transpose(jnp.broadcast_to(col, (col.shape[0], 128)))[0:8, :]


def _softmax_bwd(qs, dos, lse_row, delta_row, chunks):
    dq = jnp.zeros((qs.shape[0], 128), F32)
    grads = []
    for k, v, mask in chunks:
        s = lax.dot_general(k, qs, NT, preferred_element_type=F32)
        if mask is not None:
            s = jnp.where(mask, s, NEG_BIG)
        p = jnp.exp(s - lse_row)
        dp = lax.dot_general(v, dos, NT, preferred_element_type=F32)
        ds = (p * (dp - delta_row)).astype(BF16)
        dv = jnp.dot(p.astype(BF16), dos, preferred_element_type=F32)
        dk = jnp.dot(ds, qs, preferred_element_type=F32)
        dq = dq + lax.dot_general(ds, k, TN, preferred_element_type=F32)
        grads.append((dk, dv))
    return dq, grads


def _band(qi, tq, seq):
    span = tq + 2 * WINDOW
    start = pl.multiple_of(jnp.clip(qi * tq - WINDOW, 0, seq - span), 64)
    return start, span


def _band_mask(qi, tq, start, span, query_axis):
    shape = (GROUP * tq, span) if query_axis == 0 else (span, GROUP * tq)
    qpos = qi * tq + lax.broadcasted_iota(jnp.int32, shape, query_axis) % tq
    kpos = start + lax.broadcasted_iota(jnp.int32, shape, 1 - query_axis)
    return jnp.abs(kpos - qpos) <= WINDOW


def _qkv_specs(rt, tq, q_row, ctx_row, with_latent):
    specs = [pl.BlockSpec((tq, 256), functools.partial(lambda b, i, col: (q_row(b, i), col), col=col)) for col in (0, 1, 3, 4)]
    if with_latent:
        specs += [pl.BlockSpec((rt.seq, 128), functools.partial(lambda b, i, col: (b, col), col=col))
                  for col in (COL_KA, COL_VA, COL_KB, COL_VB)]
    specs += [pl.BlockSpec((rt.ctx, 128), functools.partial(lambda b, i, col: (ctx_row(b), col), col=col))
              for col in (COL_KA, COL_VA, COL_KB, COL_VB)]
    return specs


def _attn_fwd(rt, qkvp, sink, o_prev, name, comm=None):
    latent = o_prev is None
    seq, ctx, nb = rt.seq, rt.ctx, rt.nb
    tq = 128 if latent else ctx
    nq = seq // tq if latent else 1
    ctx_blk0 = rt.n_lat // ctx
    q_row = (lambda b, i: b * nq + i) if latent else (lambda b, i: ctx_blk0 + b)

    def body(sink_ref, qa0, qa1, qb0, qb1, *rest):
        if latent:
            kal, val, kbl, vbl, kac, vac, kbc, vbc, o_ref, lse_ref = rest
        else:
            kac, vac, kbc, vbc, _, o_ref, lse_ref = rest
        qi = pl.program_id(1)
        for kvi, (qa, qb) in enumerate(((qa0, qb0), (qa1, qb1))):
            src_a = _key_chunks(kac, vac, ctx)
            src_b = _key_chunks(kbc, vbc, ctx)
            if latent:
                src_a += _key_chunks(kal, val, seq, seq)
                start, span = _band(qi, tq, seq)
                src_b.append((kbl[pl.ds(start, span), :], vbl[pl.ds(start, span), :], _band_mask(qi, tq, start, span, 0)))
            oa, lse = _softmax_fwd(_stack_heads(qa[...], kvi), src_a, None)
            o_ref[:, kvi * 256:(kvi + 1) * 256] = _unstack_heads(oa, kvi).astype(BF16)
            lse_ref[0, kvi] = _to_rows(lse)
            sink_col = _per_head((GROUP * tq, 1), 0, tq, [sink_ref[kvi * GROUP + g] for g in range(GROUP)])
            ob, lse = _softmax_fwd(_stack_heads(qb[...], kvi), src_b, sink_col)
            o_ref[:, 512 + kvi * 256:512 + (kvi + 1) * 256] = _unstack_heads(ob, kvi).astype(BF16)
            lse_ref[0, 2 + kvi] = _to_rows(lse)

    specs = _qkv_specs(rt, tq, q_row, lambda b: ctx_blk0 + b, latent)
    args = [sink] + [qkvp] * len(specs)
    in_specs = [pl.BlockSpec(memory_space=pltpu.SMEM)] + specs
    aliases = {}
    if not latent:
        in_specs.append(pl.BlockSpec(memory_space=pl.ANY))
        args.append(o_prev)
        aliases = {len(args) - 1: 0}
    return _comm_call(
        body, comm, name=name, grid=(nb, nq),
        in_specs=in_specs,
        out_specs=[pl.BlockSpec((tq, 1024), lambda b, i: (q_row(b, i), 0)),
                   pl.BlockSpec((1, 4, 8, GROUP * tq), lambda b, i: (b * nq + i, 0, 0, 0))],
        out_shape=[jax.ShapeDtypeStruct((rt.rows, 1024), BF16), jax.ShapeDtypeStruct((nb * nq, 4, 8, GROUP * tq), F32)],
        args=args, aliases=aliases, semantics=("parallel", "parallel"))


def _attn_bwd(rt, qkvp, o, lse, do, sink, prev, name, comm=None):
    latent = prev is None
    seq, ctx, nb = rt.seq, rt.ctx, rt.nb
    tq = 128 if latent else ctx
    nq = seq // tq if latent else 1
    ctx_blk0 = rt.n_lat // ctx
    q_row = (lambda b, i: b * nq + i) if latent else (lambda b, i: ctx_blk0 + b)
    kc = min(KEY_CHUNK, seq)

    def body(sink_ref, qa0, qa1, qb0, qb1, *rest):
        if latent:
            kal, val, kbl, vbl, kac, vac, kbc, vbc, do_ref, o_ref, lse_ref, dq_ref, dl_ref, dc_ref, dsink_ref = rest
        else:
            kac, vac, kbc, vbc, do_ref, o_ref, lse_ref, c1_ref, _, _, dq_ref, dc_ref, dsink_ref = rest
        b, qi = pl.program_id(0), pl.program_id(1)

        def rows_of(cols, kvi, mixer):
            dos = _stack_heads(do_ref[:, cols], kvi)
            delta = jnp.sum(dos.astype(F32) * _stack_heads(o_ref[:, cols], kvi).astype(F32), axis=1, keepdims=True)
            return dos, lse_ref[0, 2 * mixer + kvi, 0:1, :], _to_rows(delta)[0:1, :]

        @pl.when(jnp.logical_and(b == 0, qi == 0))
        def _():
            dsink_ref[...] = jnp.zeros_like(dsink_ref)

        if latent:
            @pl.when(qi == 0)
            def _():
                dc_ref[...] = jnp.zeros_like(dc_ref)
                dl_ref[...] = jnp.zeros_like(dl_ref)
        else:
            dc_ref[...] = c1_ref[...]

        head_row = lax.broadcasted_iota(jnp.int32, (8, 128), 0)
        for kvi, (qa, qb) in enumerate(((qa0, qb0), (qa1, qb1))):
            cols = slice(kvi * 256, (kvi + 1) * 256)
            dos, lse_row, delta_row = rows_of(cols, kvi, 0)
            src = _key_chunks(kac, vac, ctx)
            if latent:
                src += _key_chunks(kal, val, seq)
            dq4, grads = _softmax_bwd(_stack_heads(qa[...], kvi), dos, lse_row, delta_row, src)
            dq_ref[:, cols] = _unstack_heads(dq4, kvi)
            dc_ref[:, 0:128] += grads[0][0]
            dc_ref[:, 128:256] += grads[0][1]
            for c, (dk, dv) in enumerate(grads[1:]):
                dl_ref[c * kc:(c + 1) * kc, 0:128] += dk
                dl_ref[c * kc:(c + 1) * kc, 128:256] += dv
            cols = slice(512 + kvi * 256, 512 + (kvi + 1) * 256)
            dos, lse_row, delta_row = rows_of(cols, kvi, 1)
            src = _key_chunks(kbc, vbc, ctx)
            if latent:
                start, span = _band(qi, tq, seq)
                src.append((kbl[pl.ds(start, span), :], vbl[pl.ds(start, span), :], _band_mask(qi, tq, start, span, 1)))
            dq4, grads = _softmax_bwd(_stack_heads(qb[...], kvi), dos, lse_row, delta_row, src)
            dq_ref[:, cols] = _unstack_heads(dq4, kvi)
            dc_ref[:, 256:384] += grads[0][0]
            dc_ref[:, 384:512] += grads[0][1]
            if latent:
                dl_ref[pl.ds(start, span), 256:384] += grads[1][0]
                dl_ref[pl.ds(start, span), 384:512] += grads[1][1]
            sink_row = _per_head((1, GROUP * tq), 1, tq, [sink_ref[kvi * GROUP + g] for g in range(GROUP)])
            dsink = -jnp.exp(sink_row - lse_row) * delta_row
            head = lax.broadcasted_iota(jnp.int32, (1, GROUP * tq), 1) // tq
            upd = jnp.zeros((8, 128), F32)
            for g in range(GROUP):
                upd = jnp.where(head_row == kvi * GROUP + g, jnp.sum(jnp.where(head == g, dsink, 0.0)), upd)
            dsink_ref[...] += upd

    specs = _qkv_specs(rt, tq, q_row, lambda b: ctx_blk0 + b, latent)
    q_rows_spec = pl.BlockSpec((tq, 1024), lambda b, i: (q_row(b, i), 0))
    in_specs = ([pl.BlockSpec(memory_space=pltpu.SMEM)] + specs
                + [q_rows_spec, q_rows_spec, pl.BlockSpec((1, 4, 8, GROUP * tq), lambda b, i: (b * nq + i, 0, 0, 0))])
    args = [sink] + [qkvp] * len(specs) + [do, o, lse]
    dq_shape = jax.ShapeDtypeStruct((rt.rows, 1024), F32)
    dkv_shape = jax.ShapeDtypeStruct((rt.rows, 512), F32)
    dsink_spec, dsink_shape = pl.BlockSpec((8, 128), lambda b, i: (0, 0)), jax.ShapeDtypeStruct((8, 128), F32)
    dq_spec = pl.BlockSpec((tq, 1024), lambda b, i: (q_row(b, i), 0))
    if latent:
        out_specs = [dq_spec, pl.BlockSpec((seq, 512), lambda b, i: (b, 0)), pl.BlockSpec((ctx, 512), lambda b, i: (b, 0)), dsink_spec]
        out_shape = [dq_shape, dkv_shape, jax.ShapeDtypeStruct((rt.n_ctx, 512), F32), dsink_shape]
        aliases = {}
    else:
        dq_prev, dkv_prev, c1 = prev
        in_specs += [pl.BlockSpec((ctx, 512), lambda b, i: (b, 0)), pl.BlockSpec(memory_space=pl.ANY), pl.BlockSpec(memory_space=pl.ANY)]
        args += [c1, dq_prev, dkv_prev]
        out_specs = [dq_spec, pl.BlockSpec((ctx, 512), lambda b, i: (ctx_blk0 + b, 0)), dsink_spec]
        out_shape = [dq_shape, dkv_shape, dsink_shape]
        aliases = {len(args) - 2: 0, len(args) - 1: 1}
    return _comm_call(body, comm, name=name, grid=(nb, nq), in_specs=in_specs, out_specs=out_specs, out_shape=out_shape,
                      args=args, aliases=aliases, semantics=("arbitrary", "arbitrary"))


def _silu(x):
    return x / (1.0 + jnp.exp(-x))


def _whole(shape):
    return pl.BlockSpec(shape, lambda i, s: (0,) * len(shape))


def _ada_half_spec(cols):
    return pl.BlockSpec((DEPTH, D_MODEL, cols), lambda i, s: (0, 0, s[0]))


def _ada_fwd(cond, w_ada, b_half, c_idx, name):
    rows = cond.shape[0]
    cols = w_ada.shape[2] // 2

    def body(s_ref, c_ref, w_ref, b_ref, x_ref, o_ref):
        xs = _silu(c_ref[...]).astype(BF16)
        x_ref[...] = xs
        for l in range(DEPTH):
            o_ref[l] = jnp.dot(xs, w_ref[l].astype(BF16), preferred_element_type=F32) + b_ref[l]

    grid_spec = pltpu.PrefetchScalarGridSpec(
        num_scalar_prefetch=1, grid=(1,),
        in_specs=[_whole(cond.shape), _ada_half_spec(cols), _whole(b_half.shape)],
        out_specs=[_whole((rows, D_MODEL)), _whole((DEPTH, rows, cols))])
    return pl.pallas_call(
        body, name=name, grid_spec=grid_spec,
        out_shape=[jax.ShapeDtypeStruct((rows, D_MODEL), BF16), jax.ShapeDtypeStruct((DEPTH, rows, cols), F32)],
        compiler_params=_params(("arbitrary",)),
    )(c_idx, cond, w_ada, b_half)


def _ada_cond_bwd(dcx, w_ada, c_idx, name):
    _, rows, cols = dcx.shape

    def body(s_ref, d_ref, w_ref, o_ref):
        acc = jnp.zeros((rows, D_MODEL), F32)
        for l in range(DEPTH):
            acc = acc + lax.dot_general(d_ref[l], w_ref[l].astype(BF16), NT, preferred_element_type=F32)
        o_ref[...] = acc

    grid_spec = pltpu.PrefetchScalarGridSpec(
        num_scalar_prefetch=1, grid=(1,),
        in_specs=[_whole(dcx.shape), _ada_half_spec(cols)], out_specs=_whole((rows, D_MODEL)))
    return pl.pallas_call(body, name=name, grid_spec=grid_spec, out_shape=jax.ShapeDtypeStruct((rows, D_MODEL), F32),
                          compiler_params=_params(("arbitrary",)))(c_idx, dcx, w_ada)


def _dev_sum(x, name):
    _, r, c = x.shape

    def body(x_ref, o_ref):
        v = x_ref[0]
        for d in range(1, N_DEV):
            v = v + x_ref[d]
        o_ref[...] = v

    return pl.pallas_call(body, name=name, out_shape=jax.ShapeDtypeStruct((r, c), F32))(x)


def _adam_val(w, g, m, v):
    c1 = 1.0 / (1.0 - ADAM_B1 ** ADAM_STEP)
    c2 = 1.0 / (1.0 - ADAM_B2 ** ADAM_STEP)
    nm = ADAM_B1 * m + (1.0 - ADAM_B1) * g
    nv = ADAM_B2 * v + (1.0 - ADAM_B2) * (g * g)
    return -ADAM_LR * ((nm * c1) / (jnp.sqrt(nv * c2) + ADAM_EPS) + ADAM_WD * w), nm, nv


def _small_update(tot, dcc_parts, params, n_groups, name):
    n_p = len(params)
    mod_rows = n_groups * N_MOD

    def body(tot_ref, dcc_ref, *refs):
        ins, outs = refs[:3 * n_p], refs[3 * n_p:]

        def update(p, rows, cols, g):
            w_ref, m_ref, v_ref = ins[3 * p:3 * p + 3]
            g_ref, d_ref, nm_ref, nv_ref = outs[4 * p:4 * p + 4]
            d, nm, nv = _adam_val(w_ref[rows, cols], g, m_ref[rows, cols], v_ref[rows, cols])
            g_ref[rows, cols] = g
            d_ref[rows, cols] = d
            nm_ref[rows, cols] = nm
            nv_ref[rows, cols] = nv

        acc = dcc_ref[0, 0:1, :]
        for d in range(1, N_DEV):
            acc = acc + dcc_ref[d, 0:1, :]
        c = ins[0][...]
        sg = 1.0 / (1.0 + jnp.exp(-c))
        update(0, slice(0, 1), slice(None), acc * (sg * (1.0 + c * (1.0 - sg))))
        for l in range(DEPTH):
            for i in range(N_MOD):
                g = tot_ref[l * mod_rows + i:l * mod_rows + i + 1, :]
                for grp in range(1, n_groups):
                    g = g + tot_ref[l * mod_rows + grp * N_MOD + i:l * mod_rows + grp * N_MOD + i + 1, :]
                update(1, slice(l, l + 1), slice(i * D_MODEL, (i + 1) * D_MODEL), g)
            for j in range(4):
                row = DEPTH * mod_rows + 4 * l + j
                update(2 + j, slice(l, l + 1), slice(None), tot_ref[row:row + 1, :])

    shapes = [jax.ShapeDtypeStruct(w.shape, F32) for w, _, _ in params for _ in range(4)]
    outs = pl.pallas_call(body, name=name, out_shape=shapes)(tot, dcc_parts, *[a for p in params for a in p])
    return [tuple(outs[4 * p:4 * p + 4]) for p in range(n_p)]


def _adamw(w, g, m, v, name):
    r, c = w.shape
    tr = _pick(r, (256, 128, 64, 32, 24, 16, 8))

    def body(w_ref, g_ref, m_ref, v_ref, d_ref, nm_ref, nv_ref):
        d_ref[...], nm_ref[...], nv_ref[...] = _adam_val(w_ref[...], g_ref[...], m_ref[...], v_ref[...])

    spec = pl.BlockSpec((tr, c), lambda i: (i, 0))
    return pl.pallas_call(body, name=name, grid=(r // tr,), in_specs=[spec] * 4, out_specs=[spec] * 3,
                          out_shape=[jax.ShapeDtypeStruct((r, c), F32)] * 3, compiler_params=_params(("parallel",)))(w, g, m, v)


SMALL_ROWS = 48


def _small_rows(small, sq):
    def lane_pad(v):
        return jnp.pad(v, (0, D_MODEL - v.shape[0]))[None]

    head_rows = [lane_pad(jnp.concatenate([s["q_norm"][0], s["k_norm"][0], s["sink"]])) for s in small]
    loss_row = lane_pad((0.5 / D_MODEL) * jnp.sum(sq, keepdims=True)[0])
    rows = jnp.concatenate([s["mod"].reshape(-1, D_MODEL) for s in small] + [s["gammas"] for s in small] + head_rows + [loss_row], axis=0)
    return jnp.pad(rows, ((0, SMALL_ROWS - rows.shape[0]), (0, 0)))


def _local_step(x, ctx, target, mods, gam, qn, kn, sink, w_first, w_layers, packed, kc_idx):
    nb, seq, _ = x.shape
    rt = _Rows(nb, seq, ctx.shape[1])
    tables = _rope_tables(rt)
    fuse = packed is not None
    h = jnp.concatenate([x.reshape(rt.n_lat, D_MODEL), ctx.reshape(rt.n_ctx, D_MODEL)], axis=0)
    wg = [{}, {}] if fuse else [dict(w) for w in w_layers]
    wg[0]["in"] = (w_first, 0)
    if fuse:
        wg[0]["in_own"] = (packed, W_FIRST[0])
    saved = []
    for l in range(DEPTH):
        g_pre_mix, g_post_mix, g_pre_mlp, g_post_mlp = gam[l]
        u, qkv, qkvp = _in_fwd(rt, h, g_pre_mix, mods[l], wg[l], tables, qn[l], kn[l], f"in_fwd{l}")
        if fuse and l == 0:
            o, lse_lat, w_mlp0, w_out0, w_mix1 = _attn_fwd(rt, qkvp, sink[l], None, f"attn_lat_fwd{l}",
                                                          comm=_gather_comm(packed, [W_MLP0, W_OUT0, W_MIX1]))
            wg[0].update({kind: (w_mlp0, PACK_OFF[(kind, 0)] - W_MLP0[0]) for kind in ("up", "down")})
            wg[0]["out"] = (w_out0, 0)
            wg[1] = {kind: (w_mix1, PACK_OFF[(kind, 1)] - W_MIX1[0]) for kind in ("out", "in")}
        else:
            o, lse_lat = _attn_fwd(rt, qkvp, sink[l], None, f"attn_lat_fwd{l}")
        o, lse_ctx = _attn_fwd(rt, qkvp, sink[l], o, f"attn_ctx_fwd{l}")
        mix, h1, u2 = _out_fwd(rt, o, wg[l], h, mods[l], g_post_mix, g_pre_mlp, f"out_fwd{l}")
        if fuse and l == 0:
            r, y, h2, w_mlp1 = _mlp_fwd(rt, u2, h1, wg[l], mods[l], g_post_mlp, f"mlp_fwd{l}", comm=_gather_comm(packed, [W_MLP1]))
            wg[1].update({kind: (w_mlp1, PACK_OFF[(kind, 1)] - W_MLP1[0]) for kind in ("up", "down")})
        elif l < DEPTH - 1:
            r, y, h2 = _mlp_fwd(rt, u2, h1, wg[l], mods[l], g_post_mlp, f"mlp_fwd{l}")
        else:
            r, y, dh, sq = _mlp_fwd(rt, u2, h1, wg[l], mods[l], g_post_mlp, f"mlp_fwd{l}", target=target.reshape(rt.n_lat, D_MODEL))
        saved.append((h, u, qkv, qkvp, o, lse_lat, lse_ctx, mix, h1, u2, r, y))
        h = h2

    small = [None] * DEPTH
    groups = {}
    for l in reversed(range(DEPTH)):
        g_pre_mix, g_post_mix, g_pre_mlp, g_post_mlp = gam[l]
        h0, u, qkv, qkvp, o, lse_lat, lse_ctx, mix, h1, u2, r, y = saved[l]
        mlp_group, mix_group = (G_LAYER1, G_LAYER1) if l == 1 else (G_MLP0, G_MIX0)
        hide = fuse and l == 0

        outs = _mlp_down_bwd(rt, dh, y, r, wg[l], mods[l], g_post_mlp, f"mlp_down_bwd{l}",
                             comm=_pair_comm(groups[G_LAYER1]) if hide else None)
        dy, da, d_gate_m, d_g_post_mlp = outs[:4]
        if hide:
            sum1 = _pair_sum(groups[G_LAYER1], outs[4], kc_idx, "grad_pair_sum_layer1")
        p_mlp = _wgrad_packed(rt, r, dy, "down", PACK_OFF[("down", l)] - mlp_group[0], mlp_group[1], None, f"mlp_down_wgrad{l}")
        dh1, d_sh_m, d_sc_m, d_g_pre_mlp = _mlp_up_bwd(rt, da, wg[l], h1, dh, mods[l], g_pre_mlp, f"mlp_up_bwd{l}")
        p_mlp = _wgrad_packed(rt, u2, da, "up", PACK_OFF[("up", l)] - mlp_group[0], mlp_group[1], p_mlp, f"mlp_up_wgrad{l}")
        outs = _out_bwd(rt, dh1, mix, wg[l], mods[l], g_post_mix, f"out_bwd{l}", comm=_pair_comm(p_mlp) if hide else None)
        dmix, do, d_gate_a, d_g_post_mix = outs[:4]
        if hide:
            sum0 = _pair_sum(p_mlp, outs[4], kc_idx, "grad_pair_sum_mlp0")
        p_mix = _wgrad_packed(rt, o, dmix, "out", PACK_OFF[("out", l)] - mix_group[0], mix_group[1],
                              p_mlp if l == 1 else None, f"out_wgrad{l}")
        outs = _attn_bwd(rt, qkvp, o, lse_lat, do, sink[l], None, f"attn_lat_bwd{l}",
                         comm=_chip_comm([sum1[1], sum0[1]]) if hide else None)
        dq, dkv, dkv_c, dsink1 = outs[:4]
        if hide:
            groups[G_LAYER1] = _owner_sum(sum1[0], outs[4], kc_idx, "grad_owner_sum_layer1")
            groups[G_MLP0] = _owner_sum(sum0[0], outs[5], kc_idx, "grad_owner_sum_mlp0")
        dq, dkv, dsink2 = _attn_bwd(rt, qkvp, o, lse_ctx, do, sink[l], (dq, dkv, dkv_c), f"attn_ctx_bwd{l}")
        dqkv, dh, dqn, dkn, d_sh_a, d_sc_a, d_g_pre_mix = _in_bwd(rt, dq, dkv, qkv, tables, qn[l], kn[l], wg[l], h0, dh1, mods[l],
                                                                  g_pre_mix, l == 0, f"in_bwd{l}")
        dmod = jnp.concatenate([d_sh_a, d_sc_a, d_gate_a, d_sh_m, d_sc_m, d_gate_m], axis=1)
        small[l] = dict(mod=dmod, gammas=jnp.concatenate([d_g_pre_mix, d_g_post_mix, d_g_pre_mlp, d_g_post_mlp], axis=0),
                        q_norm=dqn, k_norm=dkn, sink=(dsink1 + dsink2)[:, 0])
        gather = _gather_comm(_small_rows(small, sq), [(0, SMALL_ROWS)]) if hide else None
        outs = _wgrad_packed(rt, u, dqkv, "in", PACK_OFF[("in", l)] - mix_group[0], mix_group[1], p_mix, f"in_wgrad{l}", comm=gather)
        groups[mix_group], small_g = outs if hide else (outs, None)
        if not hide and l == 0:
            groups[G_MLP0] = p_mlp
    return sq, dh.reshape(nb, seq, D_MODEL), [groups[G_LAYER1], groups[G_MLP0], groups[G_MIX0]], small, small_g


def kernel(x, c, ctx, c_ctx, w_ada, b_ada, g_pre_mix, g_post_mix, g_pre_mlp, g_post_mlp, w_in, q_norm, k_norm, sink, w_out, w_up, w_down, loss_target, m_c_ctx, m_w_ada, m_b_ada, m_g_pre_mix, m_g_post_mix, m_g_pre_mlp, m_g_post_mlp, m_w_in, m_q_norm, m_k_norm, m_sink, m_w_out, m_w_up, m_w_down, v_c_ctx, v_w_ada, v_b_ada, v_g_pre_mix, v_g_post_mix, v_g_pre_mlp, v_g_post_mlp, v_w_in, v_q_norm, v_k_norm, v_sink, v_w_out, v_w_up, v_w_down):
    nb = x.shape[0]
    ix, iy, ic = lax.axis_index("x"), lax.axis_index("y"), lax.axis_index("c")
    chip = 2 * ix + iy
    dev = 2 * chip + ic
    ada_cols = w_ada.shape[2] // 2

    packed = _pack_local_half(w_in, w_out, w_up, w_down, ic)
    c_rows = c.reshape(8, (nb * D_MODEL) // 8)
    c_all, w_first = _comm_alone(_merge([_gather_comm(c_rows, [(0, c_rows.shape[0])]), _gather_comm(packed, [W_FIRST], copy_own=False)]),
                                 "gather_c_w_first")
    c_all = c_all.reshape(N_DEV * nb, D_MODEL)

    n_cond = N_DEV * nb + 1
    cond_rows = 16 * ((n_cond + 15) // 16)
    cond = jnp.concatenate([c_all, c_ctx[None, :], jnp.zeros((cond_rows - n_cond, D_MODEL), F32)], axis=0)
    c_idx = ic.reshape(1).astype(jnp.int32)
    kc_idx = jnp.stack([chip, ic]).astype(jnp.int32)
    b_ada_half = lax.dynamic_slice_in_dim(b_ada, dev * ada_cols, ada_cols, 1)[:, None, :]
    x_ada, mod_part = _ada_fwd(cond, w_ada, b_ada_half, c_idx, "ada_fwd")
    mod_g = _all_gather(mod_part.reshape(DEPTH * cond_rows, ada_cols), "gather_mod", False)
    mod_all = mod_g.reshape(N_DEV, DEPTH, cond_rows, ada_cols).transpose(1, 2, 0, 3).reshape(DEPTH, cond_rows, N_MOD * D_MODEL)
    mods = []
    for l in range(DEPTH):
        mine = lax.dynamic_slice_in_dim(mod_all[l], dev * nb, nb, 0)
        mods.append(jnp.concatenate([mine, mod_all[l, n_cond - 1:n_cond]], axis=0).reshape(nb + 1, N_MOD, D_MODEL))

    gam = [(g_pre_mix[l][None], g_post_mix[l][None], g_pre_mlp[l][None], g_post_mlp[l][None]) for l in range(DEPTH)]
    qn = [jnp.tile(q_norm[l], 2)[None] for l in range(DEPTH)]
    kn = [jnp.tile(k_norm[l], 2)[None] for l in range(DEPTH)]
    _, grad_x, (h_layer1, h_mlp0, p_mix0), _, small_g = _local_step(x, ctx, loss_target, mods, gam, qn, kn, [sink[l] for l in range(DEPTH)],
                                                                 w_first, None, packed, kc_idx)

    def step(w, g, m, v, name):
        shape = w.shape
        cols = shape[-1]
        outs = _adamw(w.reshape(-1, cols), g.reshape(-1, cols), m.reshape(-1, cols), v.reshape(-1, cols), name)
        return tuple(a.reshape(shape) for a in outs)

    def piece(halves, kind, l, group):
        o = PACK_OFF[(kind, l)] - group[0]
        rows = halves[:, o:o + PACK_HEIGHT[kind]]
        if kind == "in":
            rows = rows[:, :, :2 * IN_PIECE_COLS].reshape(2, 256, 2, IN_PIECE_COLS).transpose(0, 2, 1, 3)
            return rows.reshape(1024, IN_PIECE_COLS)
        return rows.reshape(2 * PACK_HEIGHT[kind], 1024)

    r1, = _comm_alone(_pair_comm(p_mix0), "grad_pair_exchange_mix0")
    a32, a16 = _pair_sum(p_mix0, r1, kc_idx, "grad_pair_sum_mix0")
    r2, = _comm_alone(_chip_comm([a16]), "grad_chip_exchange_mix0")
    h_mix0 = _owner_sum(a32, r2, kc_idx, "grad_owner_sum_mix0")
    h_layer1, h_mlp0, h_mix0 = _comm_alone(_halves_comm([h_layer1, h_mlp0, h_mix0]), "grad_halves_exchange")
    grad_w_up = jnp.stack([piece(h_mlp0, "up", 0, G_MLP0), piece(h_layer1, "up", 1, G_LAYER1)])
    grad_w_down = jnp.stack([piece(h_mlp0, "down", 0, G_MLP0), piece(h_layer1, "down", 1, G_LAYER1)])
    grad_w_in = jnp.stack([piece(h_mix0, "in", 0, G_MIX0), piece(h_layer1, "in", 1, G_LAYER1)])
    grad_w_out = jnp.stack([piece(h_mix0, "out", 0, G_MIX0), piece(h_layer1, "out", 1, G_LAYER1)])

    tot = _dev_sum(small_g, "small_sum")
    mod_rows = (nb + 1) * N_MOD
    o_head = DEPTH * mod_rows + 4 * DEPTH
    loss = tot[o_head + DEPTH, 0]
    grad_q_norm = tot[o_head:o_head + DEPTH, 0:64] + tot[o_head:o_head + DEPTH, 64:128]
    grad_k_norm = tot[o_head:o_head + DEPTH, 128:192] + tot[o_head:o_head + DEPTH, 192:256]
    grad_sink = tot[o_head:o_head + DEPTH, 256:264]

    ex = small_g[:, :DEPTH * mod_rows].reshape(N_DEV, DEPTH, nb + 1, N_MOD * D_MODEL)[:, :, :nb]
    ex = ex.transpose(1, 0, 2, 3).reshape(DEPTH, N_DEV * nb, N_MOD * D_MODEL)
    cx = tot[:DEPTH * mod_rows].reshape(DEPTH, nb + 1, N_MOD * D_MODEL)[:, nb:]
    dm = jnp.concatenate([ex, cx, jnp.zeros((DEPTH, cond_rows - n_cond, N_MOD * D_MODEL), F32)], axis=1)
    shard_cols = w_ada.shape[2]
    grad_w_ada = _ada_wgrad(x_ada, lax.dynamic_slice_in_dim(dm, chip * shard_cols, shard_cols, 2).astype(BF16), "ada_wgrad")
    dcx = jnp.pad(lax.dynamic_slice_in_dim(cx, dev * ada_cols, ada_cols, 2), ((0, 0), (0, 15), (0, 0))).astype(BF16)
    dcc = _ada_cond_bwd(dcx, w_ada, c_idx, "ada_cond_bwd")[0:8]
    dcc_g = _all_gather(dcc, "gather_cond_grad", False).reshape(N_DEV, 8, D_MODEL)

    dense_names = ["c_ctx", "b_ada", "g_pre_mix", "g_post_mix", "g_pre_mlp", "g_post_mlp"]
    dense = _small_update(tot, dcc_g, [(c_ctx[None], m_c_ctx[None], v_c_ctx[None]), (b_ada, m_b_ada, v_b_ada),
                                       (g_pre_mix, m_g_pre_mix, v_g_pre_mix), (g_post_mix, m_g_post_mix, v_g_post_mix),
                                       (g_pre_mlp, m_g_pre_mlp, v_g_pre_mlp), (g_post_mlp, m_g_post_mlp, v_g_post_mlp)],
                          nb + 1, "small_update")
    res = {n: r for n, r in zip(dense_names, dense)}
    res["c_ctx"] = tuple(a[0] for a in res["c_ctx"])
    small_names = ["q_norm", "k_norm", "sink"]
    small_w = [q_norm, k_norm, sink]
    small_gr = [grad_q_norm, grad_k_norm, grad_sink]
    small_m = [m_q_norm, m_k_norm, m_sink]
    small_v = [v_q_norm, v_k_norm, v_sink]
    sizes = [int(np.prod(w.shape)) for w in small_w]
    total = sum(sizes)
    flat_rows = 8 * ((total + 8 * D_MODEL - 1) // (8 * D_MODEL))

    def flat(arrs, fill):
        f = jnp.concatenate([a.reshape(-1) for a in arrs])
        return jnp.concatenate([f, jnp.full((flat_rows * D_MODEL - total,), fill, F32)]).reshape(flat_rows, D_MODEL)

    sd, snm, snv = _adamw(flat(small_w, 0.0), flat(small_gr, 0.0), flat(small_m, 0.0), flat(small_v, 1.0), "adamw_small")[:3]

    def unflat(f):
        f = f.reshape(-1)
        out, off = [], 0
        for w, n in zip(small_w, sizes):
            out.append(f[off:off + n].reshape(w.shape))
            off += n
        return out

    small_d, small_nm, small_nv = unflat(sd), unflat(snm), unflat(snv)
    res.update({n: (g, d, nm, nv) for n, g, d, nm, nv in zip(small_names, small_gr, small_d, small_nm, small_nv)})
    res["w_ada"] = (grad_w_ada, *step(w_ada, grad_w_ada, m_w_ada, v_w_ada, "adamw_w_ada"))
    res["w_in"] = (grad_w_in, *step(w_in, grad_w_in, m_w_in, v_w_in, "adamw_w_in"))
    res["w_out"] = (grad_w_out, *step(w_out, grad_w_out, m_w_out, v_w_out, "adamw_w_out"))
    res["w_up"] = (grad_w_up, *step(w_up, grad_w_up, m_w_up, v_w_up, "adamw_w_up"))
    res["w_down"] = (grad_w_down, *step(w_down, grad_w_down, m_w_down, v_w_down, "adamw_w_down"))

    order = ["c_ctx", "w_ada", "b_ada", "g_pre_mix", "g_post_mix", "g_pre_mlp", "g_post_mlp", "w_in", "q_norm", "k_norm", "sink", "w_out", "w_up", "w_down"]
    return (loss, grad_x, *[res[n][0] for n in order], *[res[n][1] for n in order],
            *[res[n][2] for n in order], *[res[n][3] for n in order])
```

```python
import functools

import jax
import jax.numpy as jnp
import numpy as np
from jax import lax
from jax.experimental import pallas as pl
from jax.experimental.pallas import tpu as pltpu

F32 = jnp.float32
BF16 = jnp.bfloat16

D_MODEL = 1024
HEAD_DIM = 64
GROUP = 4
WINDOW = 128
N_MOD = 6
D_FF = 4 * D_MODEL
IN_COLS = 1536
GRID_W = 64
ROPE_THETA = 10000.0
EPS = 1e-6
NEG_BIG = -1e30
Q_SCALE = HEAD_DIM ** -0.5
DEPTH = 2
N_DEV = 8

ADAM_LR = 0.001
ADAM_B1 = 0.9
ADAM_B2 = 0.999
ADAM_EPS = 1e-08
ADAM_WD = 0.01
ADAM_STEP = 10

V7X_VMEM_BYTES = 64 * 1024 * 1024
VMEM_LIMIT = V7X_VMEM_BYTES - 8 * 1024 * 1024

MESH = pl.DeviceIdType.MESH
NT = (((1,), (1,)), ((), ()))
TN = (((0,), (0,)), ((), ()))

COL_KA, COL_VA, COL_KB, COL_VB = 4, 5, 10, 11
NORMED_COLS = 640

PACK_HEIGHT = {"up": 512, "down": 512, "in": 256, "out": 128}
IN_PIECE_COLS = 384
PACK_OFF = {("up", 0): 0, ("down", 0): 512, ("in", 0): 1024, ("out", 0): 1280,
            ("up", 1): 1408, ("down", 1): 1920, ("in", 1): 2432, ("out", 1): 2688}
PACK_ROWS = 2816
W_FIRST, W_MLP0, W_OUT0, W_MLP1, W_MIX1 = (1024, 256), (0, 1024), (1280, 128), (1408, 1024), (2432, 384)
G_LAYER1, G_MLP0, G_MIX0 = (1408, 1408), (0, 1024), (1024, 384)


def _pick(n, cands):
    for t in cands:
        if n % t == 0:
            return t
    raise ValueError(f"no tile for {n}")


def _params(sem):
    return pltpu.CompilerParams(dimension_semantics=sem, vmem_limit_bytes=VMEM_LIMIT)


def _all_gather(x, name, in_hbm):
    m_per, n = x.shape
    space = pl.ANY if in_hbm else pltpu.VMEM

    def body(x_ref, out_ref, send_sems, recv_sems, local_sem):
        x_, y_, c_ = lax.axis_index("x"), lax.axis_index("y"), lax.axis_index("c")
        me, sibling = (x_, y_, c_), (x_, y_, 1 - c_)
        chips = [(1 - x_, y_), (x_, 1 - y_), (1 - x_, 1 - y_)]

        def rows(px, py, pc):
            return out_ref.at[pl.ds((4 * px + 2 * py + pc) * m_per, m_per), :]

        def copy(k, block, to, src=None):
            return pltpu.make_async_remote_copy(
                src_ref=rows(*block) if src is None else src, dst_ref=rows(*block),
                send_sem=send_sems.at[k], recv_sem=recv_sems.at[k], device_id=to, device_id_type=MESH)

        mine = pltpu.make_async_copy(x_ref, rows(*me), local_sem)
        mine.start()
        first = [copy(0, me, sibling, src=x_ref)]
        first += [copy(1 + j, me, (*chip, c_), src=x_ref) for j, chip in enumerate(chips)]
        for cp in first:
            cp.start()
        passed = [copy(4 + j, (*chip, c_), sibling) for j, chip in enumerate(chips)]
        for j, chip in enumerate(chips):
            copy(1 + j, (*chip, c_), me).wait_recv()
            passed[j].start()
        copy(0, sibling, me).wait_recv()
        for j, chip in enumerate(chips):
            copy(4 + j, (*chip, 1 - c_), me).wait_recv()
        for cp in first + passed:
            cp.wait_send()
        mine.wait()

    return pl.pallas_call(
        body, name=name,
        out_shape=jax.ShapeDtypeStruct((N_DEV * m_per, n), x.dtype),
        in_specs=[pl.BlockSpec(memory_space=space)],
        out_specs=pl.BlockSpec(memory_space=space),
        scratch_shapes=[pltpu.SemaphoreType.DMA((7,)), pltpu.SemaphoreType.DMA((7,)), pltpu.SemaphoreType.DMA],
    )(x)


class _Comm:
    def __init__(self, inputs, out_shapes, aliases, n_send, n_recv, start, finish, relay=None, lead=0):
        self.inputs, self.out_shapes, self.aliases = list(inputs), list(out_shapes), dict(aliases)
        self.n_send, self.n_recv, self.start, self.finish, self.relay, self.lead = n_send, n_recv, start, finish, relay, lead


def _comm_call(compute, comm, *, name, grid, in_specs, out_specs, out_shape, args, aliases, semantics, scratch=()):
    in_specs, out_specs, out_shape, args, aliases = list(in_specs), list(out_specs), list(out_shape), list(args), dict(aliases)
    scratch = list(scratch)
    if comm is None:
        return pl.pallas_call(compute, name=name, grid=grid, in_specs=in_specs, out_specs=out_specs, out_shape=out_shape,
                              input_output_aliases=aliases, scratch_shapes=scratch, compiler_params=_params(semantics))(*args)
    n_in, n_out, n_ci, n_co = len(args), len(out_shape), len(comm.inputs), len(comm.out_shapes)
    hbm = pl.BlockSpec(memory_space=pl.ANY)
    aliases.update({n_in + i: n_out + o for i, o in comm.aliases.items()})

    def body(*refs):
        ins, c_ins = refs[:n_in], refs[n_in:n_in + n_ci]
        outs, c_outs = refs[n_in + n_ci:n_in + n_ci + n_out], refs[n_in + n_ci + n_out:n_in + n_ci + n_out + n_co]
        scr = refs[n_in + n_ci + n_out + n_co:-2]
        send_sems, recv_sems = refs[-2:]
        ids = [pl.program_id(a) for a in range(len(grid))]
        first = functools.reduce(jnp.logical_and, [i == 0 for i in ids])
        last = functools.reduce(jnp.logical_and, [i == g - 1 for i, g in zip(ids, grid)])

        @pl.when(first)
        def _():
            comm.start(c_ins, c_outs, send_sems, recv_sems)

        compute(*ins, *outs, *scr)

        if comm.relay is not None:
            step = functools.reduce(lambda acc, ig: acc * ig[1] + ig[0], zip(ids, grid), 0)

            @pl.when(step == int(np.prod(grid)) - 1 - comm.lead)
            def _():
                comm.relay(c_ins, c_outs, send_sems, recv_sems)

        @pl.when(last)
        def _():
            comm.finish(c_ins, c_outs, send_sems, recv_sems)

    return pl.pallas_call(
        body, name=name, grid=grid,
        in_specs=in_specs + [hbm] * n_ci, out_specs=out_specs + [hbm] * n_co, out_shape=out_shape + comm.out_shapes,
        input_output_aliases=aliases,
        scratch_shapes=scratch + [pltpu.SemaphoreType.DMA((comm.n_send,)), pltpu.SemaphoreType.DMA((comm.n_recv,))],
        compiler_params=_params(("arbitrary",) * len(grid)),
    )(*args, *comm.inputs)


def _place():
    x_, y_, c_ = lax.axis_index("x"), lax.axis_index("y"), lax.axis_index("c")
    return x_, y_, c_, [(1 - x_, y_), (x_, 1 - y_), (1 - x_, 1 - y_)]


GATHER_SENDS, GATHER_RECVS = 8, 7


def _gather_copies(packed_ref, wg_ref, send_sems, recv_sems, rows, nth=0):
    r0, n = rows
    x_, y_, c_, chips = _place()
    me, sibling = (x_, y_, c_), (x_, y_, 1 - c_)
    src = packed_ref.at[pl.ds(r0, n), :]

    def slot(px, py, pc):
        return wg_ref.at[4 * px + 2 * py + pc]

    def copy(k, block, to, from_packed=False):
        return pltpu.make_async_remote_copy(src_ref=src if from_packed else slot(*block), dst_ref=slot(*block),
                                            send_sem=send_sems.at[GATHER_SENDS * nth + k], recv_sem=recv_sems.at[GATHER_RECVS * nth + k],
                                            device_id=to, device_id_type=MESH)

    own = [copy(0, me, sibling, True)] + [copy(1 + j, me, (*chip, c_), True) for j, chip in enumerate(chips)]
    passed = [copy(4 + j, (*chip, c_), sibling) for j, chip in enumerate(chips)]
    over_ici = [copy(1 + j, (*chip, c_), me) for j, chip in enumerate(chips)]
    from_sibling = [copy(0, sibling, me)] + [copy(4 + j, (*chip, 1 - c_), me) for j, chip in enumerate(chips)]
    mine = pltpu.make_async_copy(src, slot(*me), send_sems.at[GATHER_SENDS * nth + 7])
    return mine, own, passed, over_ici, from_sibling


def _gather_start(packed_ref, wg_ref, send_sems, recv_sems, rows, nth=0, copy_own=True):
    mine, own, _, _, _ = _gather_copies(packed_ref, wg_ref, send_sems, recv_sems, rows, nth)
    if copy_own:
        mine.start()
    for cp in own:
        cp.start()


def _gather_relay(packed_ref, wg_ref, send_sems, recv_sems, rows, nth=0):
    _, _, passed, over_ici, _ = _gather_copies(packed_ref, wg_ref, send_sems, recv_sems, rows, nth)
    for arrived, onward in zip(over_ici, passed):
        arrived.wait_recv()
        onward.start()


def _gather_finish(packed_ref, wg_ref, send_sems, recv_sems, rows, nth=0, copy_own=True):
    mine, own, passed, _, from_sibling = _gather_copies(packed_ref, wg_ref, send_sems, recv_sems, rows, nth)
    for arrived in from_sibling:
        arrived.wait_recv()
    for cp in own + passed:
        cp.wait_send()
    if copy_own:
        mine.wait()


def _gather_comm(packed, ranges, copy_own=True, lead=0):
    shapes = [jax.ShapeDtypeStruct((N_DEV, n, packed.shape[1]), packed.dtype) for _, n in ranges]

    def start(ins, outs, ss, rs):
        for nth, rows in enumerate(ranges):
            _gather_start(ins[0], outs[nth], ss, rs, rows, nth, copy_own)

    def relay(ins, outs, ss, rs):
        for nth, rows in enumerate(ranges):
            _gather_relay(ins[0], outs[nth], ss, rs, rows, nth)

    def finish(ins, outs, ss, rs):
        for nth, rows in enumerate(ranges):
            _gather_finish(ins[0], outs[nth], ss, rs, rows, nth, copy_own)

    return _Comm([packed], shapes, {}, GATHER_SENDS * len(ranges), GATHER_RECVS * len(ranges), start, finish, relay, lead)


def _pair_copy(p_ref, out_ref, send_sems, recv_sems):
    x_, y_, c_, _ = _place()
    return pltpu.make_async_remote_copy(src_ref=p_ref.at[1 - c_], dst_ref=out_ref,
                                        send_sem=send_sems.at[0], recv_sem=recv_sems.at[0],
                                        device_id=(x_, y_, 1 - c_), device_id_type=MESH)


def _pair_comm(p):
    return _Comm([p], [jax.ShapeDtypeStruct(p.shape[1:], p.dtype)], {}, 1, 1,
                 lambda ins, outs, ss, rs: _pair_copy(ins[0], outs[0], ss, rs).start(),
                 lambda ins, outs, ss, rs: _pair_copy(ins[0], outs[0], ss, rs).wait())


def _chip_copies(a_refs, out_refs, send_sems, recv_sems):
    _, _, c_, chips = _place()
    return [pltpu.make_async_remote_copy(src_ref=a_ref.at[2 * tx + ty], dst_ref=o_ref.at[j],
                                         send_sem=send_sems.at[3 * g + j], recv_sem=recv_sems.at[3 * g + j],
                                         device_id=(tx, ty, c_), device_id_type=MESH)
            for g, (a_ref, o_ref) in enumerate(zip(a_refs, out_refs)) for j, (tx, ty) in enumerate(chips)]


def _chip_start(a_refs, out_refs, send_sems, recv_sems):
    for cp in _chip_copies(a_refs, out_refs, send_sems, recv_sems):
        cp.start()


def _chip_finish(a_refs, out_refs, send_sems, recv_sems):
    for cp in _chip_copies(a_refs, out_refs, send_sems, recv_sems):
        cp.wait()


def _chip_comm(arrays):
    shapes = [jax.ShapeDtypeStruct((3,) + a.shape[1:], a.dtype) for a in arrays]
    return _Comm(arrays, shapes, {}, 3 * len(arrays), 3 * len(arrays), _chip_start, _chip_finish)


def _halves_copies(in_refs, out_refs, send_sems, recv_sems):
    x_, y_, c_, _ = _place()
    return [pltpu.make_async_remote_copy(src_ref=o_ref.at[c_], dst_ref=o_ref.at[c_], send_sem=send_sems.at[i], recv_sem=recv_sems.at[i],
                                         device_id=(x_, y_, 1 - c_), device_id_type=MESH)
            for i, o_ref in enumerate(out_refs)]


def _halves_start(in_refs, out_refs, send_sems, recv_sems):
    for cp in _halves_copies(in_refs, out_refs, send_sems, recv_sems):
        cp.start()


def _halves_finish(in_refs, out_refs, send_sems, recv_sems):
    for cp in _halves_copies(in_refs, out_refs, send_sems, recv_sems):
        cp.wait()


def _halves_comm(arrays):
    shapes = [jax.ShapeDtypeStruct(a.shape, a.dtype) for a in arrays]
    return _Comm(arrays, shapes, {i: i for i in range(len(arrays))}, len(arrays), len(arrays), _halves_start, _halves_finish)


class _SemSlice:
    class _At:
        def __init__(self, sems, first):
            self.sems, self.first = sems, first

        def __getitem__(self, k):
            return self.sems.at[self.first + k]

    def __init__(self, sems, first):
        self.at = _SemSlice._At(sems, first)


def _merge(comms):
    inputs = [a for c in comms for a in c.inputs]
    shapes = [s for c in comms for s in c.out_shapes]
    aliases, spans = {}, []
    i0 = o0 = s0 = r0 = 0
    for c in comms:
        aliases.update({i0 + i: o0 + o for i, o in c.aliases.items()})
        spans.append((slice(i0, i0 + len(c.inputs)), slice(o0, o0 + len(c.out_shapes)), s0, r0))
        i0, o0, s0, r0 = i0 + len(c.inputs), o0 + len(c.out_shapes), s0 + c.n_send, r0 + c.n_recv

    def start(ins, outs, ss, rs):
        for c, (i, o, s, r) in zip(comms, spans):
            c.start(ins[i], outs[o], _SemSlice(ss, s), _SemSlice(rs, r))

    def finish(ins, outs, ss, rs):
        for c, (i, o, s, r) in zip(comms, spans):
            if c.relay is not None:
                c.relay(ins[i], outs[o], _SemSlice(ss, s), _SemSlice(rs, r))
            c.finish(ins[i], outs[o], _SemSlice(ss, s), _SemSlice(rs, r))

    return _Comm(inputs, shapes, aliases, s0, r0, start, finish)


def _comm_alone(comm, name):
    n_ci = len(comm.inputs)
    hbm = pl.BlockSpec(memory_space=pl.ANY)

    def body(*refs):
        c_ins, c_outs, send_sems, recv_sems = refs[:n_ci], refs[n_ci:-2], refs[-2], refs[-1]
        comm.start(c_ins, c_outs, send_sems, recv_sems)
        if comm.relay is not None:
            comm.relay(c_ins, c_outs, send_sems, recv_sems)
        comm.finish(c_ins, c_outs, send_sems, recv_sems)

    return pl.pallas_call(
        body, name=name, out_shape=comm.out_shapes, in_specs=[hbm] * n_ci, out_specs=[hbm] * len(comm.out_shapes),
        input_output_aliases=comm.aliases,
        scratch_shapes=[pltpu.SemaphoreType.DMA((comm.n_send,)), pltpu.SemaphoreType.DMA((comm.n_recv,))],
    )(*comm.inputs)


SUM_TILES = (704, 512, 384, 320, 256, 192, 128, 64)


def _pair_sum(p, r1, kc_idx, name):
    _, _, n, c = p.shape
    tr = _pick(n, SUM_TILES)

    def body(s_ref, p_ref, r_ref, o32_ref, o16_ref):
        v = p_ref[...] + r_ref[...]
        o16_ref[...] = v.astype(BF16)

        @pl.when(pl.program_id(1) == s_ref[0])
        def _():
            o32_ref[...] = v

    blk = pl.BlockSpec((None, tr, c), lambda i, j, s: (j, i, 0))
    grid_spec = pltpu.PrefetchScalarGridSpec(
        num_scalar_prefetch=1, grid=(n // tr, 4),
        in_specs=[pl.BlockSpec((None, None, tr, c), lambda i, j, s: (s[1], j, i, 0)), blk],
        out_specs=[pl.BlockSpec((tr, c), lambda i, j, s: (i, 0)), blk])
    return pl.pallas_call(
        body, name=name, grid_spec=grid_spec,
        out_shape=[jax.ShapeDtypeStruct((n, c), F32), jax.ShapeDtypeStruct((4, n, c), BF16)],
        compiler_params=_params(("arbitrary", "arbitrary")),
    )(kc_idx, p, r1)


def _owner_sum(a32, r2, kc_idx, name):
    r, c = a32.shape
    tr = _pick(r, SUM_TILES)

    def body(s_ref, a_ref, r_ref, o_ref):
        v = a_ref[...]
        for j in range(3):
            v = v + r_ref[j].astype(F32)
        o_ref[...] = v

    grid_spec = pltpu.PrefetchScalarGridSpec(
        num_scalar_prefetch=1, grid=(r // tr,),
        in_specs=[pl.BlockSpec((tr, c), lambda i, s: (i, 0)),
                  pl.BlockSpec((3, tr, c), lambda i, s: (0, i, 0))],
        out_specs=pl.BlockSpec((None, tr, c), lambda i, s: (s[1], i, 0)))
    return pl.pallas_call(
        body, name=name, grid_spec=grid_spec,
        out_shape=jax.ShapeDtypeStruct((2, r, c), F32),
        compiler_params=_params(("arbitrary",)),
    )(kc_idx, a32, r2)


def _pack_local_half(w_in_s, w_out_s, w_up_s, w_down_s, c_idx):
    parts, row = [], 0
    for (kind, l), off in sorted(PACK_OFF.items(), key=lambda kv: kv[1]):
        if off > row:
            parts.append(jnp.zeros((off - row, 1024), BF16))
        if kind == "up":
            p = lax.dynamic_slice_in_dim(w_up_s[l], c_idx * 512, 512, 0)
        elif kind == "down":
            p = lax.dynamic_slice_in_dim(w_down_s[l], c_idx * 512, 512, 0)
        elif kind == "in":
            p = lax.dynamic_slice_in_dim(w_in_s[l], c_idx * 512, 512, 0)
            p = p.reshape(2, 256, IN_PIECE_COLS).transpose(1, 0, 2).reshape(256, 2 * IN_PIECE_COLS)
            p = jnp.pad(p, ((0, 0), (0, 1024 - 2 * IN_PIECE_COLS)))
        else:
            p = lax.dynamic_slice_in_dim(w_out_s[l], c_idx * 128, 128, 0)
        parts.append(p.astype(BF16))
        row = off + PACK_HEIGHT[kind]
    return jnp.concatenate(parts, axis=0)


def _unpack_in_pieces(w_ref, own_ref, w_scr):
    if own_ref is not None:
        me = 4 * lax.axis_index("x") + 2 * lax.axis_index("y") + lax.axis_index("c")
    for d in range(N_DEV):
        k, c = d // 2, d % 2
        for t in range(2):
            piece = w_ref[d, :, t * IN_PIECE_COLS:(t + 1) * IN_PIECE_COLS]
            if own_ref is not None:
                piece = jnp.where(me == d, own_ref[:, t * IN_PIECE_COLS:(t + 1) * IN_PIECE_COLS], piece)
            w_scr[c * 512 + t * 256:c * 512 + (t + 1) * 256, k * IN_PIECE_COLS:(k + 1) * IN_PIECE_COLS] = piece


def _in_weight_operands(wg):
    specs, args = [_gathered_spec(wg, "in")], [wg["in"][0]]
    if "in_own" in wg:
        own, off = wg["in_own"]
        h = PACK_HEIGHT["in"]
        assert off % h == 0
        specs.append(pl.BlockSpec((h, 1024), lambda *_: (off // h, 0), pipeline_mode=pl.Buffered(1)))
        args.append(own)
    return specs, args


class _Rows:
    def __init__(self, nb, seq, ctx):
        self.nb, self.seq, self.ctx = nb, seq, ctx
        self.n_lat, self.n_ctx = nb * seq, nb * ctx
        self.rows = self.n_lat + self.n_ctx
        self.tm = _pick(np.gcd(seq, self.n_ctx), (512, 256, 128))
        self.tiles_per_ex = seq // self.tm
        self.n_tiles = self.rows // self.tm
        self.n_lat_tiles = self.n_lat // self.tm
        self.groups = nb + 1

    def group(self, i):
        return jnp.minimum(i // self.tiles_per_ex, self.nb)

    def first_of_group(self, i):
        return jnp.logical_and(i % self.tiles_per_ex == 0, i <= self.n_lat_tiles)


def _mod_spec(rt):
    return pl.BlockSpec((1, N_MOD, D_MODEL), lambda i: (rt.group(i), 0, 0))


def _row_spec(rt, cols):
    return pl.BlockSpec((rt.tm, cols), lambda i: (i, 0))


def _vec_spec(cols):
    return pl.BlockSpec((1, cols), lambda i: (0, 0))


def _group_spec(rt):
    return pl.BlockSpec((1, 1, D_MODEL), lambda i: (rt.group(i), 0, 0))


def _gathered_spec(wg, kind):
    h, off = PACK_HEIGHT[kind], wg[kind][1]
    assert off % h == 0, (kind, off)
    return pl.BlockSpec((N_DEV, h, 1024), lambda *_: (0, off // h, 0), pipeline_mode=pl.Buffered(1))


def _group_shape(rt):
    return jax.ShapeDtypeStruct((rt.groups, 1, D_MODEL), F32)


def _vec_shape(cols=D_MODEL):
    return jax.ShapeDtypeStruct((1, cols), F32)


def _rms_inv(v):
    return lax.rsqrt(jnp.mean(v * v, axis=-1, keepdims=True) + EPS)


def _norm_mod_val(h_, g_, mod_ref, i_shift, i_scale):
    n = h_ * _rms_inv(h_) * g_
    return n * (1.0 + mod_ref[0, i_scale:i_scale + 1, :]) + mod_ref[0, i_shift:i_shift + 1, :]


def _post_norm_val(h_, z_, g_, mod_ref, i_gate):
    return h_ + mod_ref[0, i_gate:i_gate + 1, :] * (z_ * _rms_inv(z_) * g_)


def _post_norm_bwd_val(dh_, z_, g_, gate):
    rinv = _rms_inv(z_)
    n0 = z_ * rinv
    dn = dh_ * gate * g_
    dz = rinv * (dn - n0 * jnp.mean(dn * n0, axis=-1, keepdims=True))
    return dz, jnp.sum(dh_ * n0 * g_, axis=0, keepdims=True), jnp.sum(dh_ * gate * n0, axis=0, keepdims=True)


def _norm_mod_bwd_val(du_, h_, g_, one_sc):
    rinv = _rms_inv(h_)
    n0 = h_ * rinv
    dn = du_ * g_ * one_sc
    dh = rinv * (dn - n0 * jnp.mean(dn * n0, axis=-1, keepdims=True))
    return (dh, jnp.sum(du_, axis=0, keepdims=True), jnp.sum(du_ * n0 * g_, axis=0, keepdims=True),
            jnp.sum(du_ * one_sc * n0, axis=0, keepdims=True))


def _accumulate(rt, i, group_pairs, global_pairs):
    @pl.when(rt.first_of_group(i))
    def _():
        for ref, _ in group_pairs:
            ref[...] = jnp.zeros_like(ref)

    @pl.when(i == 0)
    def _():
        for ref, _ in global_pairs:
            ref[...] = jnp.zeros_like(ref)

    for ref, val in group_pairs:
        ref[0] += val
    for ref, val in global_pairs:
        ref[...] += val


def _rope_tables(rt):
    pos = np.arange(rt.seq)
    axis_dim = HEAD_DIM // 2
    inv = (ROPE_THETA ** (-np.arange(0, axis_dim, 2, dtype=np.float32) / axis_dim)).astype(np.float32)
    ang_r = (pos // GRID_W).astype(np.float32)[:, None] * inv[None, :]
    ang_c = (pos % GRID_W).astype(np.float32)[:, None] * inv[None, :]
    cr, sr, cc, sc = np.cos(ang_r), np.sin(ang_r), np.cos(ang_c), np.sin(ang_c)
    zero = np.zeros_like(sr)
    cos = np.concatenate([cr, cr, cc, cc], axis=1)
    s_lo = np.concatenate([zero, sr, zero, sc], axis=1)
    s_hi = np.concatenate([-sr, zero, -sc, zero], axis=1)

    def full(t, ctx_value):
        return jnp.asarray(np.concatenate([np.tile(t, (1, 2)), np.full((rt.tm, 128), ctx_value)], axis=0), F32)

    return full(cos, 1.0), full(s_lo, 0.0), full(s_hi, 0.0)


def _table_spec(rt):
    return pl.BlockSpec((rt.tm, 128), lambda i: (jnp.where(i < rt.n_lat_tiles, i % rt.tiles_per_ex, rt.tiles_per_ex), 0))


def _head_mean(x):
    r = lax.broadcasted_iota(jnp.int32, (128, 128), 0) // HEAD_DIM
    c = lax.broadcasted_iota(jnp.int32, (128, 128), 1) // HEAD_DIM
    ones = jnp.where(r == c, 1.0 / HEAD_DIM, 0.0).astype(F32)
    return jnp.dot(x, ones, preferred_element_type=F32, precision=lax.Precision.HIGH)


def _head_stats(t):
    return lax.rsqrt(_head_mean(t * t) + EPS)


def _prep_fwd_body(tm, qkv_ref, c, s1, s2, qn, kn, out_ref):
    def rope(t):
        return t * c + pltpu.roll(t, 16, 1) * s1 + pltpu.roll(t, 112, 1) * s2

    for j in range(12):
        t = qkv_ref[:, j * 128:(j + 1) * 128]
        if j < 4:
            t = rope(t * _head_stats(t) * qn) * Q_SCALE
        elif j == COL_KA:
            t = rope(t * _head_stats(t) * kn)
        elif 6 <= j < 10:
            t = rope(t) * Q_SCALE
        elif j == COL_KB:
            t = rope(t)
        out_ref[:, j * 128:(j + 1) * 128] = t.astype(BF16)


def _prep_bwd_body(dq_ref, dkv_ref, qkv_ref, c, s1, s2, qn, kn, out_ref):
    rows = slice(None)

    def rope_bwd(d):
        return d * c + pltpu.roll(d * s1, 112, 1) + pltpu.roll(d * s2, 16, 1)

    def norm_bwd(t, g, dy):
        rinv = _head_stats(t)
        n = t * rinv
        dn = dy * g
        return rinv * (dn - n * _head_mean(dn * n)), jnp.sum(dy * n, axis=0, keepdims=True)

    dqn = jnp.zeros((1, 128), F32)
    dkn = jnp.zeros((1, 128), F32)
    for j in range(12):
        if j < 4:
            d, dg = norm_bwd(qkv_ref[rows, j * 128:(j + 1) * 128], qn, rope_bwd(dq_ref[rows, j * 128:(j + 1) * 128] * Q_SCALE))
            dqn = dqn + dg
        elif j == COL_KA:
            d, dg = norm_bwd(qkv_ref[rows, j * 128:(j + 1) * 128], kn, rope_bwd(dkv_ref[rows, 0:128]))
            dkn = dkn + dg
        elif j == COL_VA:
            d = dkv_ref[rows, 128:256]
        elif j < 10:
            d = rope_bwd(dq_ref[rows, (j - 2) * 128:(j - 1) * 128] * Q_SCALE)
        elif j == COL_KB:
            d = rope_bwd(dkv_ref[rows, 256:384])
        else:
            d = dkv_ref[rows, 384:512]
        out_ref[rows, j * 128:(j + 1) * 128] = d.astype(BF16)
    return dqn, dkn


def _in_fwd(rt, h, gamma, mod, wg, tables, qn, kn, name):
    w_specs, w_args = _in_weight_operands(wg)
    n_w = len(w_args)

    def body(h_ref, g_ref, mod_ref, *rest):
        c_ref, s1_ref, s2_ref, qn_ref, kn_ref, u_ref, qkn_ref, qkvp_ref, qkv_ref, w_scr = rest[n_w:]

        @pl.when(pl.program_id(0) == 0)
        def _():
            _unpack_in_pieces(rest[0], rest[1] if n_w == 2 else None, w_scr)

        u = _norm_mod_val(h_ref[...], g_ref[...], mod_ref, 0, 1).astype(BF16)
        u_ref[...] = u
        qkv_ref[...] = jnp.dot(u, w_scr[...], preferred_element_type=F32)
        qkn_ref[...] = qkv_ref[:, 0:NORMED_COLS]
        _prep_fwd_body(rt.tm, qkv_ref, c_ref[...], s1_ref[...], s2_ref[...], qn_ref[...], kn_ref[...], qkvp_ref)

    return pl.pallas_call(
        body, name=name, grid=(rt.n_tiles,),
        in_specs=[_row_spec(rt, D_MODEL), _vec_spec(D_MODEL), _mod_spec(rt)] + w_specs + [_table_spec(rt)] * 3 + [_vec_spec(128)] * 2,
        out_specs=[_row_spec(rt, D_MODEL), _row_spec(rt, NORMED_COLS), _row_spec(rt, IN_COLS)],
        out_shape=[jax.ShapeDtypeStruct((rt.rows, D_MODEL), BF16), jax.ShapeDtypeStruct((rt.rows, NORMED_COLS), F32),
                   jax.ShapeDtypeStruct((rt.rows, IN_COLS), BF16)],
        scratch_shapes=[pltpu.VMEM((rt.tm, IN_COLS), F32), pltpu.VMEM((D_MODEL, IN_COLS), BF16)],
        compiler_params=_params(("arbitrary",)),
    )(h, gamma, mod, *w_args, *tables, qn, kn)


def _in_bwd(rt, dq, dkv, qkv, tables, qn, kn, wg, h, dres, mod, gamma, latent_only, name, comm=None):
    last = rt.n_lat_tiles - 1
    w_specs, w_args = _in_weight_operands(wg)
    n_w = len(w_args)

    def body(dq_ref, dkv_ref, qkv_ref, c_ref, s1_ref, s2_ref, qn_ref, kn_ref, *rest):
        h_ref, dres_ref, mod_ref, g_ref, dqkv_ref, dh_ref, dqn_ref, dkn_ref, dsh_ref, dsc_ref, dg_ref, w_scr = rest[n_w:]
        i = pl.program_id(0)

        @pl.when(i == 0)
        def _():
            _unpack_in_pieces(rest[0], rest[1] if n_w == 2 else None, w_scr)

        dqn, dkn = _prep_bwd_body(dq_ref, dkv_ref, qkv_ref, c_ref[...], s1_ref[...], s2_ref[...], qn_ref[...], kn_ref[...], dqkv_ref)
        du = lax.dot_general(dqkv_ref[...], w_scr[...], NT, preferred_element_type=F32)
        dh, dsh, dsc, dg = _norm_mod_bwd_val(du, h_ref[...], g_ref[...], 1.0 + mod_ref[0, 1:2, :])
        if latent_only:
            @pl.when(i <= last)
            def _():
                dh_ref[...] = dres_ref[...] + dh
        else:
            dh_ref[...] = dres_ref[...] + dh
        _accumulate(rt, i, [(dsh_ref, dsh), (dsc_ref, dsc)], [(dg_ref, dg), (dqn_ref, dqn), (dkn_ref, dkn)])

    dh_spec = pl.BlockSpec((rt.tm, D_MODEL), lambda i: (jnp.minimum(i, last), 0)) if latent_only else _row_spec(rt, D_MODEL)
    return _comm_call(
        body, comm, name=name, grid=(rt.n_tiles,),
        in_specs=[_row_spec(rt, 1024), _row_spec(rt, 512), _row_spec(rt, NORMED_COLS)] + [_table_spec(rt)] * 3 + [_vec_spec(128)] * 2
        + w_specs + [_row_spec(rt, D_MODEL), _row_spec(rt, D_MODEL), _mod_spec(rt), _vec_spec(D_MODEL)],
        out_specs=[_row_spec(rt, IN_COLS), dh_spec, _vec_spec(128), _vec_spec(128),
                   _group_spec(rt), _group_spec(rt), _vec_spec(D_MODEL)],
        out_shape=[jax.ShapeDtypeStruct((rt.rows, IN_COLS), BF16),
                   jax.ShapeDtypeStruct((rt.n_lat if latent_only else rt.rows, D_MODEL), F32),
                   _vec_shape(128), _vec_shape(128), _group_shape(rt), _group_shape(rt), _vec_shape()],
        args=[dq, dkv, qkv, *tables, qn, kn, *w_args, h, dres, mod, gamma], aliases={}, semantics=("arbitrary",),
        scratch=[pltpu.VMEM((D_MODEL, IN_COLS), BF16)])


def _out_fwd(rt, o, wg, h, mod, g_post_mix, g_pre_mlp, name):
    def body(o_ref, w_ref, h_ref, mod_ref, gpost_ref, gpre_ref, mix_ref, h1_ref, u2_ref):
        mix = jnp.dot(o_ref[...], w_ref[...].reshape(D_MODEL, D_MODEL), preferred_element_type=F32)
        mix_ref[...] = mix
        h1 = _post_norm_val(h_ref[...], mix, gpost_ref[...], mod_ref, 2)
        h1_ref[...] = h1
        u2_ref[...] = _norm_mod_val(h1, gpre_ref[...], mod_ref, 3, 4).astype(BF16)

    return pl.pallas_call(
        body, name=name, grid=(rt.n_tiles,),
        in_specs=[_row_spec(rt, D_MODEL), _gathered_spec(wg, "out"), _row_spec(rt, D_MODEL), _mod_spec(rt),
                  _vec_spec(D_MODEL), _vec_spec(D_MODEL)],
        out_specs=[_row_spec(rt, D_MODEL)] * 3,
        out_shape=[jax.ShapeDtypeStruct((rt.rows, D_MODEL), F32), jax.ShapeDtypeStruct((rt.rows, D_MODEL), F32),
                   jax.ShapeDtypeStruct((rt.rows, D_MODEL), BF16)],
        compiler_params=_params(("parallel",)),
    )(o, wg["out"][0], h, mod, g_post_mix, g_pre_mlp)


def _out_bwd(rt, dh1, mix, wg, mod, g_post_mix, name, comm=None):
    def body(dh_ref, mix_ref, w_ref, mod_ref, g_ref, dmix_ref, do_ref, dgate_ref, dg_ref):
        i = pl.program_id(0)
        dz, dgate, dg = _post_norm_bwd_val(dh_ref[...], mix_ref[...], g_ref[...], mod_ref[0, 2:3, :])
        dzb = dz.astype(BF16)
        dmix_ref[...] = dzb
        do_ref[...] = lax.dot_general(dzb, w_ref[...].reshape(D_MODEL, D_MODEL), NT, preferred_element_type=F32).astype(BF16)
        _accumulate(rt, i, [(dgate_ref, dgate)], [(dg_ref, dg)])

    return _comm_call(
        body, comm, name=name, grid=(rt.n_tiles,),
        in_specs=[_row_spec(rt, D_MODEL), _row_spec(rt, D_MODEL), _gathered_spec(wg, "out"), _mod_spec(rt), _vec_spec(D_MODEL)],
        out_specs=[_row_spec(rt, D_MODEL), _row_spec(rt, D_MODEL), _group_spec(rt), _vec_spec(D_MODEL)],
        out_shape=[jax.ShapeDtypeStruct((rt.rows, D_MODEL), BF16), jax.ShapeDtypeStruct((rt.rows, D_MODEL), BF16),
                   _group_shape(rt), _vec_shape()],
        args=[dh1, mix, wg["out"][0], mod, g_post_mix], aliases={}, semantics=("arbitrary",))


def _w_chunk(w_ref, k):
    return w_ref[2 * k:2 * k + 2].reshape(1024, 1024)


def _mlp_fwd(rt, u2, h1, wg, mod, g_post_mlp, name, comm=None, target=None):
    last = rt.n_lat_tiles - 1

    def body(u2_ref, h1_ref, wu_ref, wd_ref, mod_ref, g_ref, *rest):
        u2_ = u2_ref[...]
        y = jnp.zeros((rt.tm, D_MODEL), F32)
        for k in range(D_FF // 1024):
            a = jnp.maximum(jnp.dot(u2_, _w_chunk(wu_ref, k), preferred_element_type=F32), 0.0)
            rest[-3 if target is None else -4][:, k * 1024:(k + 1) * 1024] = a.astype(BF16)
            y = y + jnp.dot((a * a).astype(BF16), _w_chunk(wd_ref, k), preferred_element_type=F32)
        h2 = _post_norm_val(h1_ref[...], y, g_ref[...], mod_ref, 5)
        if target is None:
            _, y_ref, h2_ref = rest
            y_ref[...] = y
            h2_ref[...] = h2
        else:
            t_ref, _, y_ref, dh_ref, sq_ref = rest
            y_ref[...] = y
            i = pl.program_id(0)

            @pl.when(i == 0)
            def _():
                sq_ref[...] = jnp.zeros_like(sq_ref)

            @pl.when(i <= last)
            def _():
                e = h2 - t_ref[...]
                dh_ref[...] = e * (1.0 / D_MODEL)
                sq_ref[...] += jnp.sum(e * e, axis=0, keepdims=True)

            @pl.when(i > last)
            def _():
                dh_ref[...] = jnp.zeros_like(dh_ref)

    in_specs = [_row_spec(rt, D_MODEL), _row_spec(rt, D_MODEL), _gathered_spec(wg, "up"), _gathered_spec(wg, "down"),
                _mod_spec(rt), _vec_spec(D_MODEL)]
    args = [u2, h1, wg["up"][0], wg["down"][0], mod, g_post_mlp]
    out_specs = [_row_spec(rt, D_FF), _row_spec(rt, D_MODEL), _row_spec(rt, D_MODEL)]
    out_shape = [jax.ShapeDtypeStruct((rt.rows, D_FF), BF16), jax.ShapeDtypeStruct((rt.rows, D_MODEL), F32),
                 jax.ShapeDtypeStruct((rt.rows, D_MODEL), F32)]
    if target is not None:
        in_specs.append(pl.BlockSpec((rt.tm, D_MODEL), lambda i: (jnp.minimum(i, last), 0)))
        args.append(target)
        out_specs.append(_vec_spec(D_MODEL))
        out_shape.append(_vec_shape())
    return _comm_call(body, comm, name=name, grid=(rt.n_tiles,), in_specs=in_specs, out_specs=out_specs, out_shape=out_shape,
                      args=args, aliases={}, semantics=("parallel",) if target is None else ("arbitrary",))


def _mlp_down_bwd(rt, dh, y, ra, wg, mod, g_post_mlp, name, comm=None):
    def body(dh_ref, y_ref, ra_ref, wd_ref, mod_ref, g_ref, dy_ref, da_ref, dgate_ref, dg_ref):
        i = pl.program_id(0)
        dz, dgate, dg = _post_norm_bwd_val(dh_ref[...], y_ref[...], g_ref[...], mod_ref[0, 5:6, :])
        dyb = dz.astype(BF16)
        dy_ref[...] = dyb
        for k in range(D_FF // 1024):
            dr = lax.dot_general(dyb, _w_chunk(wd_ref, k), NT, preferred_element_type=F32)
            da_ref[:, k * 1024:(k + 1) * 1024] = (dr * (2.0 * ra_ref[:, k * 1024:(k + 1) * 1024].astype(F32))).astype(BF16)
        _accumulate(rt, i, [(dgate_ref, dgate)], [(dg_ref, dg)])

    return _comm_call(
        body, comm, name=name, grid=(rt.n_tiles,),
        in_specs=[_row_spec(rt, D_MODEL), _row_spec(rt, D_MODEL), _row_spec(rt, D_FF), _gathered_spec(wg, "down"),
                  _mod_spec(rt), _vec_spec(D_MODEL)],
        out_specs=[_row_spec(rt, D_MODEL), _row_spec(rt, D_FF), _group_spec(rt), _vec_spec(D_MODEL)],
        out_shape=[jax.ShapeDtypeStruct((rt.rows, D_MODEL), BF16), jax.ShapeDtypeStruct((rt.rows, D_FF), BF16),
                   _group_shape(rt), _vec_shape()],
        args=[dh, y, ra, wg["down"][0], mod, g_post_mlp], aliases={}, semantics=("arbitrary",))


def _mlp_up_bwd(rt, da, wg, h1, dh, mod, g_pre_mlp, name):
    def body(da_ref, wu_ref, h1_ref, dh_ref, mod_ref, g_ref, dh1_ref, dsh_ref, dsc_ref, dg_ref):
        i = pl.program_id(0)
        du = jnp.zeros((rt.tm, D_MODEL), F32)
        for k in range(D_FF // 1024):
            du = du + lax.dot_general(da_ref[:, k * 1024:(k + 1) * 1024], _w_chunk(wu_ref, k), NT, preferred_element_type=F32)
        d, dsh, dsc, dg = _norm_mod_bwd_val(du, h1_ref[...], g_ref[...], 1.0 + mod_ref[0, 4:5, :])
        dh1_ref[...] = dh_ref[...] + d
        _accumulate(rt, i, [(dsh_ref, dsh), (dsc_ref, dsc)], [(dg_ref, dg)])

    return pl.pallas_call(
        body, name=name, grid=(rt.n_tiles,),
        in_specs=[_row_spec(rt, D_FF), _gathered_spec(wg, "up"), _row_spec(rt, D_MODEL), _row_spec(rt, D_MODEL),
                  _mod_spec(rt), _vec_spec(D_MODEL)],
        out_specs=[_row_spec(rt, D_MODEL), _group_spec(rt), _group_spec(rt), _vec_spec(D_MODEL)],
        out_shape=[jax.ShapeDtypeStruct((rt.rows, D_MODEL), F32), _group_shape(rt), _group_shape(rt), _vec_shape()],
        compiler_params=_params(("arbitrary",)),
    )(da, wg["up"][0], h1, dh, mod, g_pre_mlp)


def _wgrad_packed(rt, a, b, kind, off, n_rows, p_prev, name, comm=None):
    h = PACK_HEIGHT[kind]
    tk = rt.tm
    assert off % h == 0, (kind, off)

    def body(a_ref, b_ref, *rest):
        o_ref = rest[-1]
        i = pl.program_id(0)

        @pl.when(i == 0)
        def _():
            o_ref[...] = jnp.zeros_like(o_ref)

        if kind == "in":
            res = lax.dot_general(a_ref[...], b_ref[...], TN, preferred_element_type=F32)
            for k in range(4):
                for c in range(2):
                    for t in range(2):
                        o_ref[c, k, :, t * IN_PIECE_COLS:(t + 1) * IN_PIECE_COLS] += \
                            res[c * 512 + t * h:c * 512 + (t + 1) * h, k * IN_PIECE_COLS:(k + 1) * IN_PIECE_COLS]
        elif kind == "out":
            res = lax.dot_general(a_ref[...], b_ref[...], TN, preferred_element_type=F32)
            for k in range(4):
                for c in range(2):
                    o_ref[c, k] += res[(2 * k + c) * h:(2 * k + c + 1) * h]
        else:
            for k in range(4):
                if kind == "up":
                    res = lax.dot_general(a_ref[...], b_ref[:, k * 1024:(k + 1) * 1024], TN, preferred_element_type=F32)
                else:
                    ra = a_ref[:, k * 1024:(k + 1) * 1024].astype(F32)
                    res = lax.dot_general((ra * ra).astype(BF16), b_ref[...], TN, preferred_element_type=F32)
                o_ref[0, k] += res[0:h]
                o_ref[1, k] += res[h:2 * h]

    in_specs = [pl.BlockSpec((tk, a.shape[1]), lambda i: (i, 0)), pl.BlockSpec((tk, b.shape[1]), lambda i: (i, 0))]
    args = [a, b]
    aliases = {}
    if p_prev is not None:
        in_specs.append(pl.BlockSpec(memory_space=pl.ANY))
        args.append(p_prev)
        aliases = {2: 0}
    outs = _comm_call(
        body, comm, name=name, grid=(rt.rows // tk,),
        in_specs=in_specs,
        out_specs=[pl.BlockSpec((2, 4, h, 1024), lambda i: (0, 0, off // h, 0))],
        out_shape=[jax.ShapeDtypeStruct((2, 4, n_rows, 1024), F32)],
        args=args, aliases=aliases, semantics=("arbitrary",))
    return outs[0] if comm is None else outs


def _ada_wgrad(xs, dm, name):
    depth, _, cols = dm.shape

    def body(x_ref, d_ref, o_ref):
        for l in range(depth):
            o_ref[l] = lax.dot_general(x_ref[...], d_ref[l], TN, preferred_element_type=F32)

    return pl.pallas_call(body, name=name, out_shape=jax.ShapeDtypeStruct((depth, xs.shape[1], cols), F32),
                          compiler_params=pltpu.CompilerParams(vmem_limit_bytes=VMEM_LIMIT))(xs, dm)


def _stack_heads(x, kvi):
    x = x.astype(F32)
    tq = x.shape[0]
    lane = lax.broadcasted_iota(jnp.int32, (tq, 128), 1)
    keep = lane < HEAD_DIM if kvi == 0 else lane >= HEAD_DIM
    parts = []
    for p in range(2):
        pair = x[:, p * 128:(p + 1) * 128]
        swapped = pltpu.roll(pair, HEAD_DIM, 1)
        lo_head, hi_head = (pair, swapped) if kvi == 0 else (swapped, pair)
        parts += [jnp.where(keep, lo_head, 0.0), jnp.where(keep, hi_head, 0.0)]
    return jnp.concatenate(parts, axis=0).astype(BF16)


def _unstack_heads(o4, kvi):
    tq = o4.shape[0] // GROUP
    lane = lax.broadcasted_iota(jnp.int32, (tq, 128), 1)
    outs = []
    for p in range(2):
        r_lo, r_hi = o4[(2 * p) * tq:(2 * p + 1) * tq], o4[(2 * p + 1) * tq:(2 * p + 2) * tq]
        if kvi == 0:
            lo, hi = r_lo, pltpu.roll(r_hi, HEAD_DIM, 1)
        else:
            lo, hi = pltpu.roll(r_lo, HEAD_DIM, 1), r_hi
        outs.append(jnp.where(lane < HEAD_DIM, lo, hi))
    return jnp.concatenate(outs, axis=1)


def _per_head(shape, axis, tq, values):
    head = lax.broadcasted_iota(jnp.int32, shape, axis) // tq
    out = jnp.zeros(shape, F32)
    for g in range(GROUP):
        out = jnp.where(head == g, values[g], out)
    return out


KEY_CHUNK = 512
Q_TILE = 128
Q_TILE_FWD = 256


def _key_chunks(k_ref, v_ref, n, kc=KEY_CHUNK):
    kc = min(kc, n)
    return [(k_ref[c * kc:(c + 1) * kc, :], v_ref[c * kc:(c + 1) * kc, :], None) for c in range(n // kc)]


def _softmax_fwd(qs, chunks, sink_col):
    logits = []
    for k, _, mask in chunks:
        s = lax.dot_general(qs, k, NT, preferred_element_type=F32)
        logits.append(s if mask is None else jnp.where(mask, s, NEG_BIG))
    m = functools.reduce(jnp.maximum, [jnp.max(s, axis=1, keepdims=True) for s in logits])
    if sink_col is not None:
        m = jnp.maximum(m, sink_col)
    l = jnp.zeros_like(m) if sink_col is None else jnp.exp(sink_col - m)
    acc = jnp.zeros((qs.shape[0], 128), F32)
    for s, (_, v, _) in zip(logits, chunks):
        p = jnp.exp(s - m)
        l = l + jnp.sum(p, axis=1, keepdims=True)
        acc = acc + jnp.dot(p.astype(BF16), v, preferred_element_type=F32)
    return acc / l, m + jnp.log(l)


def _to_rows(col):
    return jnp.transpose(jnp.broadcast_to(col, (col.shape[0], 128)))[0:8, :]


def _softmax_bwd(qs, dos, lse_row, delta_row, chunks):
    dq = jnp.zeros((qs.shape[0], 128), F32)
    grads = []
    for k, v, mask in chunks:
        s = lax.dot_general(k, qs, NT, preferred_element_type=F32)
        if mask is not None:
            s = jnp.where(mask, s, NEG_BIG)
        p = jnp.exp(s - lse_row)
        dp = lax.dot_general(v, dos, NT, preferred_element_type=F32)
        ds = (p * (dp - delta_row)).astype(BF16)
        dv = jnp.dot(p.astype(BF16), dos, preferred_element_type=F32)
        dk = jnp.dot(ds, qs, preferred_element_type=F32)
        dq = dq + lax.dot_general(ds, k, TN, preferred_element_type=F32)
        grads.append((dk, dv))
    return dq, grads


def _band(qi, tq, seq):
    span = tq + 2 * WINDOW
    start = pl.multiple_of(jnp.clip(qi * tq - WINDOW, 0, seq - span), 64)
    return start, span


def _band_mask(qi, tq, start, span, query_axis):
    shape = (GROUP * tq, span) if query_axis == 0 else (span, GROUP * tq)
    qpos = qi * tq + lax.broadcasted_iota(jnp.int32, shape, query_axis) % tq
    kpos = start + lax.broadcasted_iota(jnp.int32, shape, 1 - query_axis)
    return jnp.abs(kpos - qpos) <= WINDOW


def _qkv_specs(rt, tq, q_row, ctx_row, with_latent):
    specs = [pl.BlockSpec((tq, 256), functools.partial(lambda b, i, col: (q_row(b, i), col), col=col)) for col in (0, 1, 3, 4)]
    if with_latent:
        specs += [pl.BlockSpec((rt.seq, 128), functools.partial(lambda b, i, col: (b, col), col=col))
                  for col in (COL_KA, COL_VA, COL_KB, COL_VB)]
    specs += [pl.BlockSpec((rt.ctx, 128), functools.partial(lambda b, i, col: (ctx_row(b), col), col=col))
              for col in (COL_KA, COL_VA, COL_KB, COL_VB)]
    return specs


def _attn_fwd(rt, qkvp, sink, o_prev, name, comm=None):
    latent = o_prev is None
    seq, ctx, nb = rt.seq, rt.ctx, rt.nb
    tq = Q_TILE_FWD if latent else ctx
    tile = Q_TILE if latent else ctx
    parts = tq // tile
    nq = seq // tq if latent else 1
    ctx_blk0 = rt.n_lat // ctx
    q_row = (lambda b, i: b * nq + i) if latent else (lambda b, i: ctx_blk0 + b)

    def store_lse(lse_ref, j, lse_col):
        rows = _to_rows(lse_col)
        for part in range(parts):
            lse_ref[part, j] = jnp.concatenate([rows[:, g * tq + part * tile:g * tq + (part + 1) * tile] for g in range(GROUP)], axis=1)

    def body(sink_ref, qa0, qa1, qb0, qb1, *rest):
        if latent:
            kal, val, kbl, vbl, kac, vac, kbc, vbc, o_ref, lse_ref = rest
        else:
            kac, vac, kbc, vbc, _, o_ref, lse_ref = rest
        qi = pl.program_id(1)
        for kvi, (qa, qb) in enumerate(((qa0, qb0), (qa1, qb1))):
            src_a = _key_chunks(kac, vac, ctx)
            src_b = _key_chunks(kbc, vbc, ctx)
            if latent:
                src_a += _key_chunks(kal, val, seq, seq)
                start, span = _band(qi, tq, seq)
                src_b.append((kbl[pl.ds(start, span), :], vbl[pl.ds(start, span), :], _band_mask(qi, tq, start, span, 0)))
            oa, lse = _softmax_fwd(_stack_heads(qa[...], kvi), src_a, None)
            o_ref[:, kvi * 256:(kvi + 1) * 256] = _unstack_heads(oa, kvi).astype(BF16)
            store_lse(lse_ref, kvi, lse)
            sink_col = _per_head((GROUP * tq, 1), 0, tq, [sink_ref[kvi * GROUP + g] for g in range(GROUP)])
            ob, lse = _softmax_fwd(_stack_heads(qb[...], kvi), src_b, sink_col)
            o_ref[:, 512 + kvi * 256:512 + (kvi + 1) * 256] = _unstack_heads(ob, kvi).astype(BF16)
            store_lse(lse_ref, 2 + kvi, lse)

    specs = _qkv_specs(rt, tq, q_row, lambda b: ctx_blk0 + b, latent)
    args = [sink] + [qkvp] * len(specs)
    in_specs = [pl.BlockSpec(memory_space=pltpu.SMEM)] + specs
    aliases = {}
    if not latent:
        in_specs.append(pl.BlockSpec(memory_space=pl.ANY))
        args.append(o_prev)
        aliases = {len(args) - 1: 0}
    return _comm_call(
        body, comm, name=name, grid=(nb, nq),
        in_specs=in_specs,
        out_specs=[pl.BlockSpec((tq, 1024), lambda b, i: (q_row(b, i), 0)),
                   pl.BlockSpec((parts, 4, 8, GROUP * tile), lambda b, i: (b * nq + i, 0, 0, 0))],
        out_shape=[jax.ShapeDtypeStruct((rt.rows, 1024), BF16), jax.ShapeDtypeStruct((nb * nq * parts, 4, 8, GROUP * tile), F32)],
        args=args, aliases=aliases, semantics=("parallel", "parallel"))


def _attn_bwd(rt, qkvp, o, lse, do, sink, prev, name, comm=None):
    latent = prev is None
    seq, ctx, nb = rt.seq, rt.ctx, rt.nb
    tq = Q_TILE if latent else ctx
    nq = seq // tq if latent else 1
    ctx_blk0 = rt.n_lat // ctx
    q_row = (lambda b, i: b * nq + i) if latent else (lambda b, i: ctx_blk0 + b)
    kc = min(KEY_CHUNK, seq)

    def body(sink_ref, qa0, qa1, qb0, qb1, *rest):
        if latent:
            kal, val, kbl, vbl, kac, vac, kbc, vbc, do_ref, o_ref, lse_ref, dq_ref, dl_ref, dc_ref, dsink_ref = rest
        else:
            kac, vac, kbc, vbc, do_ref, o_ref, lse_ref, c1_ref, _, _, dq_ref, dc_ref, dsink_ref = rest
        b, qi = pl.program_id(0), pl.program_id(1)

        def rows_of(cols, kvi, mixer):
            dos = _stack_heads(do_ref[:, cols], kvi)
            delta = jnp.sum(dos.astype(F32) * _stack_heads(o_ref[:, cols], kvi).astype(F32), axis=1, keepdims=True)
            return dos, lse_ref[0, 2 * mixer + kvi, 0:1, :], _to_rows(delta)[0:1, :]

        @pl.when(jnp.logical_and(b == 0, qi == 0))
        def _():
            dsink_ref[...] = jnp.zeros_like(dsink_ref)

        if latent:
            @pl.when(qi == 0)
            def _():
                dc_ref[...] = jnp.zeros_like(dc_ref)
                dl_ref[...] = jnp.zeros_like(dl_ref)
        else:
            dc_ref[...] = c1_ref[...]

        head_row = lax.broadcasted_iota(jnp.int32, (8, 128), 0)
        for kvi, (qa, qb) in enumerate(((qa0, qb0), (qa1, qb1))):
            cols = slice(kvi * 256, (kvi + 1) * 256)
            dos, lse_row, delta_row = rows_of(cols, kvi, 0)
            src = _key_chunks(kac, vac, ctx)
            if latent:
                src += _key_chunks(kal, val, seq)
            dq4, grads = _softmax_bwd(_stack_heads(qa[...], kvi), dos, lse_row, delta_row, src)
            dq_ref[:, cols] = _unstack_heads(dq4, kvi)
            dc_ref[:, 0:128] += grads[0][0]
            dc_ref[:, 128:256] += grads[0][1]
            for c, (dk, dv) in enumerate(grads[1:]):
                dl_ref[c * kc:(c + 1) * kc, 0:128] += dk
                dl_ref[c * kc:(c + 1) * kc, 128:256] += dv
            cols = slice(512 + kvi * 256, 512 + (kvi + 1) * 256)
            dos, lse_row, delta_row = rows_of(cols, kvi, 1)
            src = _key_chunks(kbc, vbc, ctx)
            if latent:
                start, span = _band(qi, tq, seq)
                src.append((kbl[pl.ds(start, span), :], vbl[pl.ds(start, span), :], _band_mask(qi, tq, start, span, 1)))
            dq4, grads = _softmax_bwd(_stack_heads(qb[...], kvi), dos, lse_row, delta_row, src)
            dq_ref[:, cols] = _unstack_heads(dq4, kvi)
            dc_ref[:, 256:384] += grads[0][0]
            dc_ref[:, 384:512] += grads[0][1]
            if latent:
                dl_ref[pl.ds(start, span), 256:384] += grads[1][0]
                dl_ref[pl.ds(start, span), 384:512] += grads[1][1]
            sink_row = _per_head((1, GROUP * tq), 1, tq, [sink_ref[kvi * GROUP + g] for g in range(GROUP)])
            dsink = -jnp.exp(sink_row - lse_row) * delta_row
            head = lax.broadcasted_iota(jnp.int32, (1, GROUP * tq), 1) // tq
            upd = jnp.zeros((8, 128), F32)
            for g in range(GROUP):
                upd = jnp.where(head_row == kvi * GROUP + g, jnp.sum(jnp.where(head == g, dsink, 0.0)), upd)
            dsink_ref[...] += upd

    specs = _qkv_specs(rt, tq, q_row, lambda b: ctx_blk0 + b, latent)
    q_rows_spec = pl.BlockSpec((tq, 1024), lambda b, i: (q_row(b, i), 0))
    in_specs = ([pl.BlockSpec(memory_space=pltpu.SMEM)] + specs
                + [q_rows_spec, q_rows_spec, pl.BlockSpec((1, 4, 8, GROUP * tq), lambda b, i: (b * nq + i, 0, 0, 0))])
    args = [sink] + [qkvp] * len(specs) + [do, o, lse]
    dq_shape = jax.ShapeDtypeStruct((rt.rows, 1024), F32)
    dkv_shape = jax.ShapeDtypeStruct((rt.rows, 512), F32)
    dsink_spec, dsink_shape = pl.BlockSpec((8, 128), lambda b, i: (0, 0)), jax.ShapeDtypeStruct((8, 128), F32)
    dq_spec = pl.BlockSpec((tq, 1024), lambda b, i: (q_row(b, i), 0))
    if latent:
        out_specs = [dq_spec, pl.BlockSpec((seq, 512), lambda b, i: (b, 0)), pl.BlockSpec((ctx, 512), lambda b, i: (b, 0)), dsink_spec]
        out_shape = [dq_shape, dkv_shape, jax.ShapeDtypeStruct((rt.n_ctx, 512), F32), dsink_shape]
        aliases = {}
    else:
        dq_prev, dkv_prev, c1 = prev
        in_specs += [pl.BlockSpec((ctx, 512), lambda b, i: (b, 0)), pl.BlockSpec(memory_space=pl.ANY), pl.BlockSpec(memory_space=pl.ANY)]
        args += [c1, dq_prev, dkv_prev]
        out_specs = [dq_spec, pl.BlockSpec((ctx, 512), lambda b, i: (ctx_blk0 + b, 0)), dsink_spec]
        out_shape = [dq_shape, dkv_shape, dsink_shape]
        aliases = {len(args) - 2: 0, len(args) - 1: 1}
    return _comm_call(body, comm, name=name, grid=(nb, nq), in_specs=in_specs, out_specs=out_specs, out_shape=out_shape,
                      args=args, aliases=aliases, semantics=("arbitrary", "arbitrary"))


def _silu(x):
    return x / (1.0 + jnp.exp(-x))


def _whole(shape):
    return pl.BlockSpec(shape, lambda i, s: (0,) * len(shape))


def _ada_half_spec(cols):
    return pl.BlockSpec((DEPTH, D_MODEL, cols), lambda i, s: (0, 0, s[0]))


def _ada_fwd(cond, w_ada, b_half, c_idx, name):
    rows = cond.shape[0]
    cols = w_ada.shape[2] // 2

    def body(s_ref, c_ref, w_ref, b_ref, x_ref, o_ref):
        xs = _silu(c_ref[...]).astype(BF16)
        x_ref[...] = xs
        for l in range(DEPTH):
            o_ref[l] = jnp.dot(xs, w_ref[l].astype(BF16), preferred_element_type=F32) + b_ref[l]

    grid_spec = pltpu.PrefetchScalarGridSpec(
        num_scalar_prefetch=1, grid=(1,),
        in_specs=[_whole(cond.shape), _ada_half_spec(cols), _whole(b_half.shape)],
        out_specs=[_whole((rows, D_MODEL)), _whole((DEPTH, rows, cols))])
    return pl.pallas_call(
        body, name=name, grid_spec=grid_spec,
        out_shape=[jax.ShapeDtypeStruct((rows, D_MODEL), BF16), jax.ShapeDtypeStruct((DEPTH, rows, cols), F32)],
        compiler_params=_params(("arbitrary",)),
    )(c_idx, cond, w_ada, b_half)


def _ada_cond_bwd(dcx, w_ada, c_idx, name):
    _, rows, cols = dcx.shape

    def body(s_ref, d_ref, w_ref, o_ref):
        acc = jnp.zeros((rows, D_MODEL), F32)
        for l in range(DEPTH):
            acc = acc + lax.dot_general(d_ref[l], w_ref[l].astype(BF16), NT, preferred_element_type=F32)
        o_ref[...] = acc

    grid_spec = pltpu.PrefetchScalarGridSpec(
        num_scalar_prefetch=1, grid=(1,),
        in_specs=[_whole(dcx.shape), _ada_half_spec(cols)], out_specs=_whole((rows, D_MODEL)))
    return pl.pallas_call(body, name=name, grid_spec=grid_spec, out_shape=jax.ShapeDtypeStruct((rows, D_MODEL), F32),
                          compiler_params=_params(("arbitrary",)))(c_idx, dcx, w_ada)


def _dev_sum(x, name):
    _, r, c = x.shape

    def body(x_ref, o_ref):
        v = x_ref[0]
        for d in range(1, N_DEV):
            v = v + x_ref[d]
        o_ref[...] = v

    return pl.pallas_call(body, name=name, out_shape=jax.ShapeDtypeStruct((r, c), F32))(x)


def _adam_val(w, g, m, v):
    c1 = 1.0 / (1.0 - ADAM_B1 ** ADAM_STEP)
    c2 = 1.0 / (1.0 - ADAM_B2 ** ADAM_STEP)
    nm = ADAM_B1 * m + (1.0 - ADAM_B1) * g
    nv = ADAM_B2 * v + (1.0 - ADAM_B2) * (g * g)
    return -ADAM_LR * ((nm * c1) / (jnp.sqrt(nv * c2) + ADAM_EPS) + ADAM_WD * w), nm, nv


def _small_update(tot, dcc_parts, params, n_groups, name):
    n_p = len(params)
    mod_rows = n_groups * N_MOD

    def body(tot_ref, dcc_ref, *refs):
        ins, outs = refs[:3 * n_p], refs[3 * n_p:]

        def update(p, rows, cols, g):
            w_ref, m_ref, v_ref = ins[3 * p:3 * p + 3]
            g_ref, d_ref, nm_ref, nv_ref = outs[4 * p:4 * p + 4]
            d, nm, nv = _adam_val(w_ref[rows, cols], g, m_ref[rows, cols], v_ref[rows, cols])
            g_ref[rows, cols] = g
            d_ref[rows, cols] = d
            nm_ref[rows, cols] = nm
            nv_ref[rows, cols] = nv

        acc = dcc_ref[0, 0:1, :]
        for d in range(1, N_DEV):
            acc = acc + dcc_ref[d, 0:1, :]
        c = ins[0][...]
        sg = 1.0 / (1.0 + jnp.exp(-c))
        update(0, slice(0, 1), slice(None), acc * (sg * (1.0 + c * (1.0 - sg))))
        for l in range(DEPTH):
            for i in range(N_MOD):
                g = tot_ref[l * mod_rows + i:l * mod_rows + i + 1, :]
                for grp in range(1, n_groups):
                    g = g + tot_ref[l * mod_rows + grp * N_MOD + i:l * mod_rows + grp * N_MOD + i + 1, :]
                update(1, slice(l, l + 1), slice(i * D_MODEL, (i + 1) * D_MODEL), g)
            for j in range(4):
                row = DEPTH * mod_rows + 4 * l + j
                update(2 + j, slice(l, l + 1), slice(None), tot_ref[row:row + 1, :])

    shapes = [jax.ShapeDtypeStruct(w.shape, F32) for w, _, _ in params for _ in range(4)]
    outs = pl.pallas_call(body, name=name, out_shape=shapes)(tot, dcc_parts, *[a for p in params for a in p])
    return [tuple(outs[4 * p:4 * p + 4]) for p in range(n_p)]


def _adamw(w, g, m, v, name):
    r, c = w.shape
    tr = _pick(r, (256, 128, 64, 32, 24, 16, 8))

    def body(w_ref, g_ref, m_ref, v_ref, d_ref, nm_ref, nv_ref):
        d_ref[...], nm_ref[...], nv_ref[...] = _adam_val(w_ref[...], g_ref[...], m_ref[...], v_ref[...])

    spec = pl.BlockSpec((tr, c), lambda i: (i, 0))
    return pl.pallas_call(body, name=name, grid=(r // tr,), in_specs=[spec] * 4, out_specs=[spec] * 3,
                          out_shape=[jax.ShapeDtypeStruct((r, c), F32)] * 3, compiler_params=_params(("parallel",)))(w, g, m, v)


SMALL_ROWS = 48


def _small_rows(small, sq):
    def lane_pad(v):
        return jnp.pad(v, (0, D_MODEL - v.shape[0]))[None]

    head_rows = [lane_pad(jnp.concatenate([s["q_norm"][0], s["k_norm"][0], s["sink"]])) for s in small]
    loss_row = lane_pad((0.5 / D_MODEL) * jnp.sum(sq, keepdims=True)[0])
    rows = jnp.concatenate([s["mod"].reshape(-1, D_MODEL) for s in small] + [s["gammas"] for s in small] + head_rows + [loss_row], axis=0)
    return jnp.pad(rows, ((0, SMALL_ROWS - rows.shape[0]), (0, 0)))


def _local_step(x, ctx, target, mods, gam, qn, kn, sink, w_first, w_layers, packed, kc_idx):
    nb, seq, _ = x.shape
    rt = _Rows(nb, seq, ctx.shape[1])
    tables = _rope_tables(rt)
    fuse = packed is not None
    h = jnp.concatenate([x.reshape(rt.n_lat, D_MODEL), ctx.reshape(rt.n_ctx, D_MODEL)], axis=0)
    wg = [{}, {}] if fuse else [dict(w) for w in w_layers]
    wg[0]["in"] = (w_first, 0)
    if fuse:
        wg[0]["in_own"] = (packed, W_FIRST[0])
    saved = []
    for l in range(DEPTH):
        g_pre_mix, g_post_mix, g_pre_mlp, g_post_mlp = gam[l]
        u, qkv, qkvp = _in_fwd(rt, h, g_pre_mix, mods[l], wg[l], tables, qn[l], kn[l], f"in_fwd{l}")
        if fuse and l == 0:
            o, lse_lat, w_mlp0, w_out0, w_mix1 = _attn_fwd(rt, qkvp, sink[l], None, f"attn_lat_fwd{l}",
                                                          comm=_gather_comm(packed, [W_MLP0, W_OUT0, W_MIX1], lead=2))
            wg[0].update({kind: (w_mlp0, PACK_OFF[(kind, 0)] - W_MLP0[0]) for kind in ("up", "down")})
            wg[0]["out"] = (w_out0, 0)
            wg[1] = {kind: (w_mix1, PACK_OFF[(kind, 1)] - W_MIX1[0]) for kind in ("out", "in")}
        else:
            o, lse_lat = _attn_fwd(rt, qkvp, sink[l], None, f"attn_lat_fwd{l}")
        o, lse_ctx = _attn_fwd(rt, qkvp, sink[l], o, f"attn_ctx_fwd{l}")
        mix, h1, u2 = _out_fwd(rt, o, wg[l], h, mods[l], g_post_mix, g_pre_mlp, f"out_fwd{l}")
        if fuse and l == 0:
            r, y, h2, w_mlp1 = _mlp_fwd(rt, u2, h1, wg[l], mods[l], g_post_mlp, f"mlp_fwd{l}", comm=_gather_comm(packed, [W_MLP1]))
            wg[1].update({kind: (w_mlp1, PACK_OFF[(kind, 1)] - W_MLP1[0]) for kind in ("up", "down")})
        elif l < DEPTH - 1:
            r, y, h2 = _mlp_fwd(rt, u2, h1, wg[l], mods[l], g_post_mlp, f"mlp_fwd{l}")
        else:
            r, y, dh, sq = _mlp_fwd(rt, u2, h1, wg[l], mods[l], g_post_mlp, f"mlp_fwd{l}", target=target.reshape(rt.n_lat, D_MODEL))
        saved.append((h, u, qkv, qkvp, o, lse_lat, lse_ctx, mix, h1, u2, r, y))
        h = h2

    small = [None] * DEPTH
    groups = {}
    for l in reversed(range(DEPTH)):
        g_pre_mix, g_post_mix, g_pre_mlp, g_post_mlp = gam[l]
        h0, u, qkv, qkvp, o, lse_lat, lse_ctx, mix, h1, u2, r, y = saved[l]
        mlp_group, mix_group = (G_LAYER1, G_LAYER1) if l == 1 else (G_MLP0, G_MIX0)
        hide = fuse and l == 0

        outs = _mlp_down_bwd(rt, dh, y, r, wg[l], mods[l], g_post_mlp, f"mlp_down_bwd{l}",
                             comm=_pair_comm(groups[G_LAYER1]) if hide else None)
        dy, da, d_gate_m, d_g_post_mlp = outs[:4]
        if hide:
            sum1 = _pair_sum(groups[G_LAYER1], outs[4], kc_idx, "grad_pair_sum_layer1")
        p_mlp = _wgrad_packed(rt, r, dy, "down", PACK_OFF[("down", l)] - mlp_group[0], mlp_group[1], None, f"mlp_down_wgrad{l}")
        dh1, d_sh_m, d_sc_m, d_g_pre_mlp = _mlp_up_bwd(rt, da, wg[l], h1, dh, mods[l], g_pre_mlp, f"mlp_up_bwd{l}")
        p_mlp = _wgrad_packed(rt, u2, da, "up", PACK_OFF[("up", l)] - mlp_group[0], mlp_group[1], p_mlp, f"mlp_up_wgrad{l}")
        outs = _out_bwd(rt, dh1, mix, wg[l], mods[l], g_post_mix, f"out_bwd{l}", comm=_pair_comm(p_mlp) if hide else None)
        dmix, do, d_gate_a, d_g_post_mix = outs[:4]
        if hide:
            sum0 = _pair_sum(p_mlp, outs[4], kc_idx, "grad_pair_sum_mlp0")
        p_mix = _wgrad_packed(rt, o, dmix, "out", PACK_OFF[("out", l)] - mix_group[0], mix_group[1],
                              p_mlp if l == 1 else None, f"out_wgrad{l}")
        outs = _attn_bwd(rt, qkvp, o, lse_lat, do, sink[l], None, f"attn_lat_bwd{l}",
                         comm=_chip_comm([sum1[1], sum0[1]]) if hide else None)
        dq, dkv, dkv_c, dsink1 = outs[:4]
        if hide:
            groups[G_LAYER1] = _owner_sum(sum1[0], outs[4], kc_idx, "grad_owner_sum_layer1")
            groups[G_MLP0] = _owner_sum(sum0[0], outs[5], kc_idx, "grad_owner_sum_mlp0")
        dq, dkv, dsink2 = _attn_bwd(rt, qkvp, o, lse_ctx, do, sink[l], (dq, dkv, dkv_c), f"attn_ctx_bwd{l}")
        dqkv, dh, dqn, dkn, d_sh_a, d_sc_a, d_g_pre_mix = _in_bwd(rt, dq, dkv, qkv, tables, qn[l], kn[l], wg[l], h0, dh1, mods[l],
                                                                  g_pre_mix, l == 0, f"in_bwd{l}")
        dmod = jnp.concatenate([d_sh_a, d_sc_a, d_gate_a, d_sh_m, d_sc_m, d_gate_m], axis=1)
        small[l] = dict(mod=dmod, gammas=jnp.concatenate([d_g_pre_mix, d_g_post_mix, d_g_pre_mlp, d_g_post_mlp], axis=0),
                        q_norm=dqn, k_norm=dkn, sink=(dsink1 + dsink2)[:, 0])
        gather = _gather_comm(_small_rows(small, sq), [(0, SMALL_ROWS)]) if hide else None
        outs = _wgrad_packed(rt, u, dqkv, "in", PACK_OFF[("in", l)] - mix_group[0], mix_group[1], p_mix, f"in_wgrad{l}", comm=gather)
        groups[mix_group], small_g = outs if hide else (outs, None)
        if not hide and l == 0:
            groups[G_MLP0] = p_mlp
    return sq, dh.reshape(nb, seq, D_MODEL), [groups[G_LAYER1], groups[G_MLP0], groups[G_MIX0]], small, small_g


def kernel(x, c, ctx, c_ctx, w_ada, b_ada, g_pre_mix, g_post_mix, g_pre_mlp, g_post_mlp, w_in, q_norm, k_norm, sink, w_out, w_up, w_down, loss_target, m_c_ctx, m_w_ada, m_b_ada, m_g_pre_mix, m_g_post_mix, m_g_pre_mlp, m_g_post_mlp, m_w_in, m_q_norm, m_k_norm, m_sink, m_w_out, m_w_up, m_w_down, v_c_ctx, v_w_ada, v_b_ada, v_g_pre_mix, v_g_post_mix, v_g_pre_mlp, v_g_post_mlp, v_w_in, v_q_norm, v_k_norm, v_sink, v_w_out, v_w_up, v_w_down):
    nb = x.shape[0]
    ix, iy, ic = lax.axis_index("x"), lax.axis_index("y"), lax.axis_index("c")
    chip = 2 * ix + iy
    dev = 2 * chip + ic
    ada_cols = w_ada.shape[2] // 2

    packed = _pack_local_half(w_in, w_out, w_up, w_down, ic)
    c_rows = c.reshape(8, (nb * D_MODEL) // 8)
    c_all, w_first = _comm_alone(_merge([_gather_comm(c_rows, [(0, c_rows.shape[0])]), _gather_comm(packed, [W_FIRST], copy_own=False)]),
                                 "gather_c_w_first")
    c_all = c_all.reshape(N_DEV * nb, D_MODEL)

    n_cond = N_DEV * nb + 1
    cond_rows = 16 * ((n_cond + 15) // 16)
    cond = jnp.concatenate([c_all, c_ctx[None, :], jnp.zeros((cond_rows - n_cond, D_MODEL), F32)], axis=0)
    c_idx = ic.reshape(1).astype(jnp.int32)
    kc_idx = jnp.stack([chip, ic]).astype(jnp.int32)
    b_ada_half = lax.dynamic_slice_in_dim(b_ada, dev * ada_cols, ada_cols, 1)[:, None, :]
    x_ada, mod_part = _ada_fwd(cond, w_ada, b_ada_half, c_idx, "ada_fwd")
    mod_g = _all_gather(mod_part.reshape(DEPTH * cond_rows, ada_cols), "gather_mod", False)
    mod_all = mod_g.reshape(N_DEV, DEPTH, cond_rows, ada_cols).transpose(1, 2, 0, 3).reshape(DEPTH, cond_rows, N_MOD * D_MODEL)
    mods = []
    for l in range(DEPTH):
        mine = lax.dynamic_slice_in_dim(mod_all[l], dev * nb, nb, 0)
        mods.append(jnp.concatenate([mine, mod_all[l, n_cond - 1:n_cond]], axis=0).reshape(nb + 1, N_MOD, D_MODEL))

    gam = [(g_pre_mix[l][None], g_post_mix[l][None], g_pre_mlp[l][None], g_post_mlp[l][None]) for l in range(DEPTH)]
    qn = [jnp.tile(q_norm[l], 2)[None] for l in range(DEPTH)]
    kn = [jnp.tile(k_norm[l], 2)[None] for l in range(DEPTH)]
    _, grad_x, (h_layer1, h_mlp0, p_mix0), _, small_g = _local_step(x, ctx, loss_target, mods, gam, qn, kn, [sink[l] for l in range(DEPTH)],
                                                                 w_first, None, packed, kc_idx)

    def step(w, g, m, v, name):
        shape = w.shape
        cols = shape[-1]
        outs = _adamw(w.reshape(-1, cols), g.reshape(-1, cols), m.reshape(-1, cols), v.reshape(-1, cols), name)
        return tuple(a.reshape(shape) for a in outs)

    def piece(halves, kind, l, group):
        o = PACK_OFF[(kind, l)] - group[0]
        rows = halves[:, o:o + PACK_HEIGHT[kind]]
        if kind == "in":
            rows = rows[:, :, :2 * IN_PIECE_COLS].reshape(2, 256, 2, IN_PIECE_COLS).transpose(0, 2, 1, 3)
            return rows.reshape(1024, IN_PIECE_COLS)
        return rows.reshape(2 * PACK_HEIGHT[kind], 1024)

    tot = _dev_sum(small_g, "small_sum")
    mod_rows = (nb + 1) * N_MOD
    o_head = DEPTH * mod_rows + 4 * DEPTH
    loss = tot[o_head + DEPTH, 0]
    grad_q_norm = tot[o_head:o_head + DEPTH, 0:64] + tot[o_head:o_head + DEPTH, 64:128]
    grad_k_norm = tot[o_head:o_head + DEPTH, 128:192] + tot[o_head:o_head + DEPTH, 192:256]
    grad_sink = tot[o_head:o_head + DEPTH, 256:264]

    ex = small_g[:, :DEPTH * mod_rows].reshape(N_DEV, DEPTH, nb + 1, N_MOD * D_MODEL)[:, :, :nb]
    ex = ex.transpose(1, 0, 2, 3).reshape(DEPTH, N_DEV * nb, N_MOD * D_MODEL)
    cx = tot[:DEPTH * mod_rows].reshape(DEPTH, nb + 1, N_MOD * D_MODEL)[:, nb:]
    dm = jnp.concatenate([ex, cx, jnp.zeros((DEPTH, cond_rows - n_cond, N_MOD * D_MODEL), F32)], axis=1)
    shard_cols = w_ada.shape[2]
    grad_w_ada = _ada_wgrad(x_ada, lax.dynamic_slice_in_dim(dm, chip * shard_cols, shard_cols, 2).astype(BF16), "ada_wgrad")
    dcx = jnp.pad(lax.dynamic_slice_in_dim(cx, dev * ada_cols, ada_cols, 2), ((0, 0), (0, 15), (0, 0))).astype(BF16)
    dcc = _ada_cond_bwd(dcx, w_ada, c_idx, "ada_cond_bwd")[0:8]

    r1, = _comm_alone(_pair_comm(p_mix0), "grad_pair_exchange_mix0")
    a32, a16 = _pair_sum(p_mix0, r1, kc_idx, "grad_pair_sum_mix0")
    r2, dcc_g = _comm_alone(_merge([_chip_comm([a16]), _gather_comm(dcc, [(0, dcc.shape[0])])]), "grad_chip_exchange_mix0")
    h_mix0 = _owner_sum(a32, r2, kc_idx, "grad_owner_sum_mix0")
    h_layer1, h_mlp0, h_mix0 = _comm_alone(_halves_comm([h_layer1, h_mlp0, h_mix0]), "grad_halves_exchange")
    grad_w_up = jnp.stack([piece(h_mlp0, "up", 0, G_MLP0), piece(h_layer1, "up", 1, G_LAYER1)])
    grad_w_down = jnp.stack([piece(h_mlp0, "down", 0, G_MLP0), piece(h_layer1, "down", 1, G_LAYER1)])
    grad_w_in = jnp.stack([piece(h_mix0, "in", 0, G_MIX0), piece(h_layer1, "in", 1, G_LAYER1)])
    grad_w_out = jnp.stack([piece(h_mix0, "out", 0, G_MIX0), piece(h_layer1, "out", 1, G_LAYER1)])

    dense_names = ["c_ctx", "b_ada", "g_pre_mix", "g_post_mix", "g_pre_mlp", "g_post_mlp"]
    dense = _small_update(tot, dcc_g, [(c_ctx[None], m_c_ctx[None], v_c_ctx[None]), (b_ada, m_b_ada, v_b_ada),
                                       (g_pre_mix, m_g_pre_mix, v_g_pre_mix), (g_post_mix, m_g_post_mix, v_g_post_mix),
                                       (g_pre_mlp, m_g_pre_mlp, v_g_pre_mlp), (g_post_mlp, m_g_post_mlp, v_g_post_mlp)],
                          nb + 1, "small_update")
    res = {n: r for n, r in zip(dense_names, dense)}
    res["c_ctx"] = tuple(a[0] for a in res["c_ctx"])
    small_names = ["q_norm", "k_norm", "sink"]
    small_w = [q_norm, k_norm, sink]
    small_gr = [grad_q_norm, grad_k_norm, grad_sink]
    small_m = [m_q_norm, m_k_norm, m_sink]
    small_v = [v_q_norm, v_k_norm, v_sink]
    sizes = [int(np.prod(w.shape)) for w in small_w]
    total = sum(sizes)
    flat_rows = 8 * ((total + 8 * D_MODEL - 1) // (8 * D_MODEL))

    def flat(arrs, fill):
        f = jnp.concatenate([a.reshape(-1) for a in arrs])
        return jnp.concatenate([f, jnp.full((flat_rows * D_MODEL - total,), fill, F32)]).reshape(flat_rows, D_MODEL)

    sd, snm, snv = _adamw(flat(small_w, 0.0), flat(small_gr, 0.0), flat(small_m, 0.0), flat(small_v, 1.0), "adamw_small")[:3]

    def unflat(f):
        f = f.reshape(-1)
        out, off = [], 0
        for w, n in zip(small_w, sizes):
            out.append(f[off:off + n].reshape(w.shape))
            off += n
        return out

    small_d, small_nm, small_nv = unflat(sd), unflat(snm), unflat(snv)
    res.update({n: (g, d, nm, nv) for n, g, d, nm, nv in zip(small_names, small_gr, small_d, small_nm, small_nv)})
    res["w_ada"] = (grad_w_ada, *step(w_ada, grad_w_ada, m_w_ada, v_w_ada, "adamw_w_ada"))
    res["w_in"] = (grad_w_in, *step(w_in, grad_w_in, m_w_in, v_w_in, "adamw_w_in"))
    res["w_out"] = (grad_w_out, *step(w_out, grad_w_out, m_w_out, v_w_out, "adamw_w_out"))
    res["w_up"] = (grad_w_up, *step(w_up, grad_w_up, m_w_up, v_w_up, "adamw_w_up"))
    res["w_down"] = (grad_w_down, *step(w_down, grad_w_down, m_w_down, v_w_down, "adamw_w_down"))

    order = ["c_ctx", "w_ada", "b_ada", "g_pre_mix", "g_post_mix", "g_pre_mlp", "g_post_mlp", "w_in", "q_norm", "k_norm", "sink", "w_out", "w_up", "w_down"]
    return (loss, grad_x, *[res[n][0] for n in order], *[res[n][1] for n in order],
            *[res[n][2] for n in order], *[res[n][3] for n in order])
```

```python
import functools

import jax
import jax.numpy as jnp
import numpy as np
from jax import lax
from jax.experimental import pallas as pl
from jax.experimental.pallas import tpu as pltpu

F32 = jnp.float32
BF16 = jnp.bfloat16

D_MODEL = 1024
HEAD_DIM = 64
GROUP = 4
WINDOW = 128
N_MOD = 6
D_FF = 4 * D_MODEL
IN_COLS = 1536
GRID_W = 64
ROPE_THETA = 10000.0
EPS = 1e-6
NEG_BIG = -1e30
Q_SCALE = HEAD_DIM ** -0.5
DEPTH = 2
N_DEV = 8

ADAM_LR = 0.001
ADAM_B1 = 0.9
ADAM_B2 = 0.999
ADAM_EPS = 1e-08
ADAM_WD = 0.01
ADAM_STEP = 10

V7X_VMEM_BYTES = 64 * 1024 * 1024
VMEM_LIMIT = V7X_VMEM_BYTES - 8 * 1024 * 1024

MESH = pl.DeviceIdType.MESH
NT = (((1,), (1,)), ((), ()))
TN = (((0,), (0,)), ((), ()))

COL_KA, COL_VA, COL_KB, COL_VB = 4, 5, 10, 11
NORMED_COLS = 640

PACK_HEIGHT = {"up": 512, "down": 512, "in": 256, "out": 128}
IN_PIECE_COLS = 384
PACK_OFF = {("up", 0): 0, ("down", 0): 512, ("in", 0): 1024, ("out", 0): 1280,
            ("up", 1): 1408, ("down", 1): 1920, ("in", 1): 2432, ("out", 1): 2688}
PACK_ROWS = 2816
W_FIRST, W_MLP0, W_OUT0, W_MLP1, W_MIX1 = (1024, 256), (0, 1024), (1280, 128), (1408, 1024), (2432, 384)
G_LAYER1, G_MLP0, G_MIX0 = (1408, 1408), (0, 1024), (1024, 384)


def _pick(n, cands):
    for t in cands:
        if n % t == 0:
            return t
    raise ValueError(f"no tile for {n}")


def _params(sem):
    return pltpu.CompilerParams(dimension_semantics=sem, vmem_limit_bytes=VMEM_LIMIT)


def _all_gather(x, name, in_hbm):
    m_per, n = x.shape
    space = pl.ANY if in_hbm else pltpu.VMEM

    def body(x_ref, out_ref, send_sems, recv_sems, local_sem):
        x_, y_, c_ = lax.axis_index("x"), lax.axis_index("y"), lax.axis_index("c")
        me, sibling = (x_, y_, c_), (x_, y_, 1 - c_)
        chips = [(1 - x_, y_), (x_, 1 - y_), (1 - x_, 1 - y_)]

        def rows(px, py, pc):
            return out_ref.at[pl.ds((4 * px + 2 * py + pc) * m_per, m_per), :]

        def copy(k, block, to, src=None):
            return pltpu.make_async_remote_copy(
                src_ref=rows(*block) if src is None else src, dst_ref=rows(*block),
                send_sem=send_sems.at[k], recv_sem=recv_sems.at[k], device_id=to, device_id_type=MESH)

        mine = pltpu.make_async_copy(x_ref, rows(*me), local_sem)
        mine.start()
        first = [copy(0, me, sibling, src=x_ref)]
        first += [copy(1 + j, me, (*chip, c_), src=x_ref) for j, chip in enumerate(chips)]
        for cp in first:
            cp.start()
        passed = [copy(4 + j, (*chip, c_), sibling) for j, chip in enumerate(chips)]
        for j, chip in enumerate(chips):
            copy(1 + j, (*chip, c_), me).wait_recv()
            passed[j].start()
        copy(0, sibling, me).wait_recv()
        for j, chip in enumerate(chips):
            copy(4 + j, (*chip, 1 - c_), me).wait_recv()
        for cp in first + passed:
            cp.wait_send()
        mine.wait()

    return pl.pallas_call(
        body, name=name,
        out_shape=jax.ShapeDtypeStruct((N_DEV * m_per, n), x.dtype),
        in_specs=[pl.BlockSpec(memory_space=space)],
        out_specs=pl.BlockSpec(memory_space=space),
        scratch_shapes=[pltpu.SemaphoreType.DMA((7,)), pltpu.SemaphoreType.DMA((7,)), pltpu.SemaphoreType.DMA],
    )(x)


class _Comm:
    def __init__(self, inputs, out_shapes, aliases, n_send, n_recv, start, finish, relay=None, lead=0):
        self.inputs, self.out_shapes, self.aliases = list(inputs), list(out_shapes), dict(aliases)
        self.n_send, self.n_recv, self.start, self.finish, self.relay, self.lead = n_send, n_recv, start, finish, relay, lead


def _comm_call(compute, comm, *, name, grid, in_specs, out_specs, out_shape, args, aliases, semantics, scratch=()):
    in_specs, out_specs, out_shape, args, aliases = list(in_specs), list(out_specs), list(out_shape), list(args), dict(aliases)
    scratch = list(scratch)
    if comm is None:
        return pl.pallas_call(compute, name=name, grid=grid, in_specs=in_specs, out_specs=out_specs, out_shape=out_shape,
                              input_output_aliases=aliases, scratch_shapes=scratch, compiler_params=_params(semantics))(*args)
    n_in, n_out, n_ci, n_co = len(args), len(out_shape), len(comm.inputs), len(comm.out_shapes)
    hbm = pl.BlockSpec(memory_space=pl.ANY)
    aliases.update({n_in + i: n_out + o for i, o in comm.aliases.items()})

    def body(*refs):
        ins, c_ins = refs[:n_in], refs[n_in:n_in + n_ci]
        outs, c_outs = refs[n_in + n_ci:n_in + n_ci + n_out], refs[n_in + n_ci + n_out:n_in + n_ci + n_out + n_co]
        scr = refs[n_in + n_ci + n_out + n_co:-2]
        send_sems, recv_sems = refs[-2:]
        ids = [pl.program_id(a) for a in range(len(grid))]
        first = functools.reduce(jnp.logical_and, [i == 0 for i in ids])
        last = functools.reduce(jnp.logical_and, [i == g - 1 for i, g in zip(ids, grid)])

        @pl.when(first)
        def _():
            comm.start(c_ins, c_outs, send_sems, recv_sems)

        compute(*ins, *outs, *scr)

        if comm.relay is not None:
            step = functools.reduce(lambda acc, ig: acc * ig[1] + ig[0], zip(ids, grid), 0)

            @pl.when(step == int(np.prod(grid)) - 1 - comm.lead)
            def _():
                comm.relay(c_ins, c_outs, send_sems, recv_sems)

        @pl.when(last)
        def _():
            comm.finish(c_ins, c_outs, send_sems, recv_sems)

    return pl.pallas_call(
        body, name=name, grid=grid,
        in_specs=in_specs + [hbm] * n_ci, out_specs=out_specs + [hbm] * n_co, out_shape=out_shape + comm.out_shapes,
        input_output_aliases=aliases,
        scratch_shapes=scratch + [pltpu.SemaphoreType.DMA((comm.n_send,)), pltpu.SemaphoreType.DMA((comm.n_recv,))],
        compiler_params=_params(("arbitrary",) * len(grid)),
    )(*args, *comm.inputs)


def _place():
    x_, y_, c_ = lax.axis_index("x"), lax.axis_index("y"), lax.axis_index("c")
    return x_, y_, c_, [(1 - x_, y_), (x_, 1 - y_), (1 - x_, 1 - y_)]


GATHER_SENDS, GATHER_RECVS = 8, 7


def _gather_copies(packed_ref, wg_ref, send_sems, recv_sems, rows, nth=0):
    r0, n = rows
    x_, y_, c_, chips = _place()
    me, sibling = (x_, y_, c_), (x_, y_, 1 - c_)
    src = packed_ref.at[pl.ds(r0, n), :]

    def slot(px, py, pc):
        return wg_ref.at[4 * px + 2 * py + pc]

    def copy(k, block, to, from_packed=False):
        return pltpu.make_async_remote_copy(src_ref=src if from_packed else slot(*block), dst_ref=slot(*block),
                                            send_sem=send_sems.at[GATHER_SENDS * nth + k], recv_sem=recv_sems.at[GATHER_RECVS * nth + k],
                                            device_id=to, device_id_type=MESH)

    own = [copy(0, me, sibling, True)] + [copy(1 + j, me, (*chip, c_), True) for j, chip in enumerate(chips)]
    passed = [copy(4 + j, (*chip, c_), sibling) for j, chip in enumerate(chips)]
    over_ici = [copy(1 + j, (*chip, c_), me) for j, chip in enumerate(chips)]
    from_sibling = [copy(0, sibling, me)] + [copy(4 + j, (*chip, 1 - c_), me) for j, chip in enumerate(chips)]
    mine = pltpu.make_async_copy(src, slot(*me), send_sems.at[GATHER_SENDS * nth + 7])
    return mine, own, passed, over_ici, from_sibling


def _gather_start(packed_ref, wg_ref, send_sems, recv_sems, rows, nth=0, copy_own=True):
    mine, own, _, _, _ = _gather_copies(packed_ref, wg_ref, send_sems, recv_sems, rows, nth)
    if copy_own:
        mine.start()
    for cp in own:
        cp.start()


def _gather_relay(packed_ref, wg_ref, send_sems, recv_sems, rows, nth=0):
    _, _, passed, over_ici, _ = _gather_copies(packed_ref, wg_ref, send_sems, recv_sems, rows, nth)
    for arrived, onward in zip(over_ici, passed):
        arrived.wait_recv()
        onward.start()


def _gather_finish(packed_ref, wg_ref, send_sems, recv_sems, rows, nth=0, copy_own=True):
    mine, own, passed, _, from_sibling = _gather_copies(packed_ref, wg_ref, send_sems, recv_sems, rows, nth)
    for arrived in from_sibling:
        arrived.wait_recv()
    for cp in own + passed:
        cp.wait_send()
    if copy_own:
        mine.wait()


def _gather_comm(packed, ranges, copy_own=True, lead=0):
    shapes = [jax.ShapeDtypeStruct((N_DEV, n, packed.shape[1]), packed.dtype) for _, n in ranges]

    def start(ins, outs, ss, rs):
        for nth, rows in enumerate(ranges):
            _gather_start(ins[0], outs[nth], ss, rs, rows, nth, copy_own)

    def relay(ins, outs, ss, rs):
        for nth, rows in enumerate(ranges):
            _gather_relay(ins[0], outs[nth], ss, rs, rows, nth)

    def finish(ins, outs, ss, rs):
        for nth, rows in enumerate(ranges):
            _gather_finish(ins[0], outs[nth], ss, rs, rows, nth, copy_own)

    return _Comm([packed], shapes, {}, GATHER_SENDS * len(ranges), GATHER_RECVS * len(ranges), start, finish, relay, lead)


def _pair_copy(p_ref, out_ref, send_sems, recv_sems):
    x_, y_, c_, _ = _place()
    return pltpu.make_async_remote_copy(src_ref=p_ref.at[1 - c_], dst_ref=out_ref,
                                        send_sem=send_sems.at[0], recv_sem=recv_sems.at[0],
                                        device_id=(x_, y_, 1 - c_), device_id_type=MESH)


def _pair_comm(p):
    return _Comm([p], [jax.ShapeDtypeStruct(p.shape[1:], p.dtype)], {}, 1, 1,
                 lambda ins, outs, ss, rs: _pair_copy(ins[0], outs[0], ss, rs).start(),
                 lambda ins, outs, ss, rs: _pair_copy(ins[0], outs[0], ss, rs).wait())


def _chip_copies(a_refs, out_refs, send_sems, recv_sems):
    _, _, c_, chips = _place()
    return [pltpu.make_async_remote_copy(src_ref=a_ref.at[2 * tx + ty], dst_ref=o_ref.at[j],
                                         send_sem=send_sems.at[3 * g + j], recv_sem=recv_sems.at[3 * g + j],
                                         device_id=(tx, ty, c_), device_id_type=MESH)
            for g, (a_ref, o_ref) in enumerate(zip(a_refs, out_refs)) for j, (tx, ty) in enumerate(chips)]


def _chip_start(a_refs, out_refs, send_sems, recv_sems):
    for cp in _chip_copies(a_refs, out_refs, send_sems, recv_sems):
        cp.start()


def _chip_finish(a_refs, out_refs, send_sems, recv_sems):
    for cp in _chip_copies(a_refs, out_refs, send_sems, recv_sems):
        cp.wait()


def _chip_comm(arrays):
    shapes = [jax.ShapeDtypeStruct((3,) + a.shape[1:], a.dtype) for a in arrays]
    return _Comm(arrays, shapes, {}, 3 * len(arrays), 3 * len(arrays), _chip_start, _chip_finish)


def _halves_copies(in_refs, out_refs, send_sems, recv_sems):
    x_, y_, c_, _ = _place()
    return [pltpu.make_async_remote_copy(src_ref=o_ref.at[c_], dst_ref=o_ref.at[c_], send_sem=send_sems.at[i], recv_sem=recv_sems.at[i],
                                         device_id=(x_, y_, 1 - c_), device_id_type=MESH)
            for i, o_ref in enumerate(out_refs)]


def _halves_start(in_refs, out_refs, send_sems, recv_sems):
    for cp in _halves_copies(in_refs, out_refs, send_sems, recv_sems):
        cp.start()


def _halves_finish(in_refs, out_refs, send_sems, recv_sems):
    for cp in _halves_copies(in_refs, out_refs, send_sems, recv_sems):
        cp.wait()


def _halves_comm(arrays):
    shapes = [jax.ShapeDtypeStruct(a.shape, a.dtype) for a in arrays]
    return _Comm(arrays, shapes, {i: i for i in range(len(arrays))}, len(arrays), len(arrays), _halves_start, _halves_finish)


class _SemSlice:
    class _At:
        def __init__(self, sems, first):
            self.sems, self.first = sems, first

        def __getitem__(self, k):
            return self.sems.at[self.first + k]

    def __init__(self, sems, first):
        self.at = _SemSlice._At(sems, first)


def _merge(comms):
    inputs = [a for c in comms for a in c.inputs]
    shapes = [s for c in comms for s in c.out_shapes]
    aliases, spans = {}, []
    i0 = o0 = s0 = r0 = 0
    for c in comms:
        aliases.update({i0 + i: o0 + o for i, o in c.aliases.items()})
        spans.append((slice(i0, i0 + len(c.inputs)), slice(o0, o0 + len(c.out_shapes)), s0, r0))
        i0, o0, s0, r0 = i0 + len(c.inputs), o0 + len(c.out_shapes), s0 + c.n_send, r0 + c.n_recv

    def start(ins, outs, ss, rs):
        for c, (i, o, s, r) in zip(comms, spans):
            c.start(ins[i], outs[o], _SemSlice(ss, s), _SemSlice(rs, r))

    def finish(ins, outs, ss, rs):
        for c, (i, o, s, r) in zip(comms, spans):
            if c.relay is not None:
                c.relay(ins[i], outs[o], _SemSlice(ss, s), _SemSlice(rs, r))
            c.finish(ins[i], outs[o], _SemSlice(ss, s), _SemSlice(rs, r))

    return _Comm(inputs, shapes, aliases, s0, r0, start, finish)


def _comm_alone(comm, name):
    n_ci = len(comm.inputs)
    hbm = pl.BlockSpec(memory_space=pl.ANY)

    def body(*refs):
        c_ins, c_outs, send_sems, recv_sems = refs[:n_ci], refs[n_ci:-2], refs[-2], refs[-1]
        comm.start(c_ins, c_outs, send_sems, recv_sems)
        if comm.relay is not None:
            comm.relay(c_ins, c_outs, send_sems, recv_sems)
        comm.finish(c_ins, c_outs, send_sems, recv_sems)

    return pl.pallas_call(
        body, name=name, out_shape=comm.out_shapes, in_specs=[hbm] * n_ci, out_specs=[hbm] * len(comm.out_shapes),
        input_output_aliases=comm.aliases,
        scratch_shapes=[pltpu.SemaphoreType.DMA((comm.n_send,)), pltpu.SemaphoreType.DMA((comm.n_recv,))],
    )(*comm.inputs)


SUM_TILES = (704, 512, 384, 320, 256, 192, 128, 64)


def _pair_sum(p, r1, kc_idx, name):
    _, _, n, c = p.shape
    tr = _pick(n, SUM_TILES)

    def body(s_ref, p_ref, r_ref, o32_ref, o16_ref):
        v = p_ref[...] + r_ref[...]
        o16_ref[...] = v.astype(BF16)

        @pl.when(pl.program_id(1) == s_ref[0])
        def _():
            o32_ref[...] = v

    blk = pl.BlockSpec((None, tr, c), lambda i, j, s: (j, i, 0))
    grid_spec = pltpu.PrefetchScalarGridSpec(
        num_scalar_prefetch=1, grid=(n // tr, 4),
        in_specs=[pl.BlockSpec((None, None, tr, c), lambda i, j, s: (s[1], j, i, 0)), blk],
        out_specs=[pl.BlockSpec((tr, c), lambda i, j, s: (i, 0)), blk])
    return pl.pallas_call(
        body, name=name, grid_spec=grid_spec,
        out_shape=[jax.ShapeDtypeStruct((n, c), F32), jax.ShapeDtypeStruct((4, n, c), BF16)],
        compiler_params=_params(("arbitrary", "arbitrary")),
    )(kc_idx, p, r1)


def _owner_sum(a32, r2, kc_idx, name):
    r, c = a32.shape
    tr = _pick(r, SUM_TILES)

    def body(s_ref, a_ref, r_ref, o_ref):
        v = a_ref[...]
        for j in range(3):
            v = v + r_ref[j].astype(F32)
        o_ref[...] = v

    grid_spec = pltpu.PrefetchScalarGridSpec(
        num_scalar_prefetch=1, grid=(r // tr,),
        in_specs=[pl.BlockSpec((tr, c), lambda i, s: (i, 0)),
                  pl.BlockSpec((3, tr, c), lambda i, s: (0, i, 0))],
        out_specs=pl.BlockSpec((None, tr, c), lambda i, s: (s[1], i, 0)))
    return pl.pallas_call(
        body, name=name, grid_spec=grid_spec,
        out_shape=jax.ShapeDtypeStruct((2, r, c), F32),
        compiler_params=_params(("arbitrary",)),
    )(kc_idx, a32, r2)


def _pack_local_half(w_in_s, w_out_s, w_up_s, w_down_s, c_idx):
    parts, row = [], 0
    for (kind, l), off in sorted(PACK_OFF.items(), key=lambda kv: kv[1]):
        if off > row:
            parts.append(jnp.zeros((off - row, 1024), BF16))
        if kind == "up":
            p = lax.dynamic_slice_in_dim(w_up_s[l], c_idx * 512, 512, 0)
        elif kind == "down":
            p = lax.dynamic_slice_in_dim(w_down_s[l], c_idx * 512, 512, 0)
        elif kind == "in":
            p = lax.dynamic_slice_in_dim(w_in_s[l], c_idx * 512, 512, 0)
            p = p.reshape(2, 256, IN_PIECE_COLS).transpose(1, 0, 2).reshape(256, 2 * IN_PIECE_COLS)
            p = jnp.pad(p, ((0, 0), (0, 1024 - 2 * IN_PIECE_COLS)))
        else:
            p = lax.dynamic_slice_in_dim(w_out_s[l], c_idx * 128, 128, 0)
        parts.append(p.astype(BF16))
        row = off + PACK_HEIGHT[kind]
    return jnp.concatenate(parts, axis=0)


def _unpack_in_pieces(w_ref, own_ref, w_scr):
    if own_ref is not None:
        me = 4 * lax.axis_index("x") + 2 * lax.axis_index("y") + lax.axis_index("c")
    for d in range(N_DEV):
        k, c = d // 2, d % 2
        for t in range(2):
            piece = w_ref[d, :, t * IN_PIECE_COLS:(t + 1) * IN_PIECE_COLS]
            if own_ref is not None:
                piece = jnp.where(me == d, own_ref[:, t * IN_PIECE_COLS:(t + 1) * IN_PIECE_COLS], piece)
            w_scr[c * 512 + t * 256:c * 512 + (t + 1) * 256, k * IN_PIECE_COLS:(k + 1) * IN_PIECE_COLS] = piece


def _in_weight_operands(wg):
    specs, args = [_gathered_spec(wg, "in")], [wg["in"][0]]
    if "in_own" in wg:
        own, off = wg["in_own"]
        h = PACK_HEIGHT["in"]
        assert off % h == 0
        specs.append(pl.BlockSpec((h, 1024), lambda *_: (off // h, 0), pipeline_mode=pl.Buffered(1)))
        args.append(own)
    return specs, args


class _Rows:
    def __init__(self, nb, seq, ctx):
        self.nb, self.seq, self.ctx = nb, seq, ctx
        self.n_lat, self.n_ctx = nb * seq, nb * ctx
        self.rows = self.n_lat + self.n_ctx
        self.tm = _pick(np.gcd(seq, self.n_ctx), (512, 256, 128))
        self.tiles_per_ex = seq // self.tm
        self.n_tiles = self.rows // self.tm
        self.n_lat_tiles = self.n_lat // self.tm
        self.groups = nb + 1

    def group(self, i):
        return jnp.minimum(i // self.tiles_per_ex, self.nb)

    def first_of_group(self, i):
        return jnp.logical_and(i % self.tiles_per_ex == 0, i <= self.n_lat_tiles)


def _mod_spec(rt):
    return pl.BlockSpec((1, N_MOD, D_MODEL), lambda i: (rt.group(i), 0, 0))


def _row_spec(rt, cols):
    return pl.BlockSpec((rt.tm, cols), lambda i: (i, 0))


def _vec_spec(cols):
    return pl.BlockSpec((1, cols), lambda i: (0, 0))


def _group_spec(rt):
    return pl.BlockSpec((1, 1, D_MODEL), lambda i: (rt.group(i), 0, 0))


def _gathered_spec(wg, kind):
    h, off = PACK_HEIGHT[kind], wg[kind][1]
    assert off % h == 0, (kind, off)
    return pl.BlockSpec((N_DEV, h, 1024), lambda *_: (0, off // h, 0), pipeline_mode=pl.Buffered(1))


def _group_shape(rt):
    return jax.ShapeDtypeStruct((rt.groups, 1, D_MODEL), F32)


def _vec_shape(cols=D_MODEL):
    return jax.ShapeDtypeStruct((1, cols), F32)


def _rms_inv(v):
    return lax.rsqrt(jnp.mean(v * v, axis=-1, keepdims=True) + EPS)


def _norm_mod_val(h_, g_, mod_ref, i_shift, i_scale):
    n = h_ * _rms_inv(h_) * g_
    return n * (1.0 + mod_ref[0, i_scale:i_scale + 1, :]) + mod_ref[0, i_shift:i_shift + 1, :]


def _post_norm_val(h_, z_, g_, mod_ref, i_gate):
    return h_ + mod_ref[0, i_gate:i_gate + 1, :] * (z_ * _rms_inv(z_) * g_)


def _post_norm_bwd_val(dh_, z_, g_, gate):
    rinv = _rms_inv(z_)
    n0 = z_ * rinv
    dn = dh_ * gate * g_
    dz = rinv * (dn - n0 * jnp.mean(dn * n0, axis=-1, keepdims=True))
    return dz, jnp.sum(dh_ * n0 * g_, axis=0, keepdims=True), jnp.sum(dh_ * gate * n0, axis=0, keepdims=True)


def _norm_mod_bwd_val(du_, h_, g_, one_sc):
    rinv = _rms_inv(h_)
    n0 = h_ * rinv
    dn = du_ * g_ * one_sc
    dh = rinv * (dn - n0 * jnp.mean(dn * n0, axis=-1, keepdims=True))
    return (dh, jnp.sum(du_, axis=0, keepdims=True), jnp.sum(du_ * n0 * g_, axis=0, keepdims=True),
            jnp.sum(du_ * one_sc * n0, axis=0, keepdims=True))


def _accumulate(rt, i, group_pairs, global_pairs):
    @pl.when(rt.first_of_group(i))
    def _():
        for ref, _ in group_pairs:
            ref[...] = jnp.zeros_like(ref)

    @pl.when(i == 0)
    def _():
        for ref, _ in global_pairs:
            ref[...] = jnp.zeros_like(ref)

    for ref, val in group_pairs:
        ref[0] += val
    for ref, val in global_pairs:
        ref[...] += val


def _rope_tables(rt):
    pos = np.arange(rt.seq)
    axis_dim = HEAD_DIM // 2
    inv = (ROPE_THETA ** (-np.arange(0, axis_dim, 2, dtype=np.float32) / axis_dim)).astype(np.float32)
    ang_r = (pos // GRID_W).astype(np.float32)[:, None] * inv[None, :]
    ang_c = (pos % GRID_W).astype(np.float32)[:, None] * inv[None, :]
    cr, sr, cc, sc = np.cos(ang_r), np.sin(ang_r), np.cos(ang_c), np.sin(ang_c)
    zero = np.zeros_like(sr)
    cos = np.concatenate([cr, cr, cc, cc], axis=1)
    s_lo = np.concatenate([zero, sr, zero, sc], axis=1)
    s_hi = np.concatenate([-sr, zero, -sc, zero], axis=1)

    def full(t, ctx_value):
        return jnp.asarray(np.concatenate([np.tile(t, (1, 2)), np.full((rt.tm, 128), ctx_value)], axis=0), F32)

    return full(cos, 1.0), full(s_lo, 0.0), full(s_hi, 0.0)


def _table_spec(rt):
    return pl.BlockSpec((rt.tm, 128), lambda i: (jnp.where(i < rt.n_lat_tiles, i % rt.tiles_per_ex, rt.tiles_per_ex), 0))


def _head_mean(x):
    r = lax.broadcasted_iota(jnp.int32, (128, 128), 0) // HEAD_DIM
    c = lax.broadcasted_iota(jnp.int32, (128, 128), 1) // HEAD_DIM
    ones = jnp.where(r == c, 1.0 / HEAD_DIM, 0.0).astype(F32)
    return jnp.dot(x, ones, preferred_element_type=F32, precision=lax.Precision.HIGH)


def _head_stats(t):
    return lax.rsqrt(_head_mean(t * t) + EPS)


def _prep_fwd_body(tm, qkv_ref, c, s1, s2, qn, kn, out_ref):
    def rope(t):
        return t * c + pltpu.roll(t, 16, 1) * s1 + pltpu.roll(t, 112, 1) * s2

    for j in range(12):
        t = qkv_ref[:, j * 128:(j + 1) * 128]
        if j < 4:
            t = rope(t * _head_stats(t) * qn) * Q_SCALE
        elif j == COL_KA:
            t = rope(t * _head_stats(t) * kn)
        elif 6 <= j < 10:
            t = rope(t) * Q_SCALE
        elif j == COL_KB:
            t = rope(t)
        out_ref[:, j * 128:(j + 1) * 128] = t.astype(BF16)


def _prep_bwd_body(dq_ref, dkv_ref, qkv_ref, c, s1, s2, qn, kn, out_ref):
    rows = slice(None)

    def rope_bwd(d):
        return d * c + pltpu.roll(d * s1, 112, 1) + pltpu.roll(d * s2, 16, 1)

    def norm_bwd(t, g, dy):
        rinv = _head_stats(t)
        n = t * rinv
        dn = dy * g
        return rinv * (dn - n * _head_mean(dn * n)), jnp.sum(dy * n, axis=0, keepdims=True)

    dqn = jnp.zeros((1, 128), F32)
    dkn = jnp.zeros((1, 128), F32)
    for j in range(12):
        if j < 4:
            d, dg = norm_bwd(qkv_ref[rows, j * 128:(j + 1) * 128], qn, rope_bwd(dq_ref[rows, j * 128:(j + 1) * 128] * Q_SCALE))
            dqn = dqn + dg
        elif j == COL_KA:
            d, dg = norm_bwd(qkv_ref[rows, j * 128:(j + 1) * 128], kn, rope_bwd(dkv_ref[rows, 0:128]))
            dkn = dkn + dg
        elif j == COL_VA:
            d = dkv_ref[rows, 128:256]
        elif j < 10:
            d = rope_bwd(dq_ref[rows, (j - 2) * 128:(j - 1) * 128] * Q_SCALE)
        elif j == COL_KB:
            d = rope_bwd(dkv_ref[rows, 256:384])
        else:
            d = dkv_ref[rows, 384:512]
        out_ref[rows, j * 128:(j + 1) * 128] = d.astype(BF16)
    return dqn, dkn


def _in_fwd(rt, h, gamma, mod, wg, tables, qn, kn, name):
    w_specs, w_args = _in_weight_operands(wg)
    n_w = len(w_args)

    def body(h_ref, g_ref, mod_ref, *rest):
        c_ref, s1_ref, s2_ref, qn_ref, kn_ref, u_ref, qkn_ref, qkvp_ref, qkv_ref, w_scr = rest[n_w:]

        @pl.when(pl.program_id(0) == 0)
        def _():
            _unpack_in_pieces(rest[0], rest[1] if n_w == 2 else None, w_scr)

        u = _norm_mod_val(h_ref[...], g_ref[...], mod_ref, 0, 1).astype(BF16)
        u_ref[...] = u
        qkv_ref[...] = jnp.dot(u, w_scr[...], preferred_element_type=F32)
        qkn_ref[...] = qkv_ref[:, 0:NORMED_COLS]
        _prep_fwd_body(rt.tm, qkv_ref, c_ref[...], s1_ref[...], s2_ref[...], qn_ref[...], kn_ref[...], qkvp_ref)

    return pl.pallas_call(
        body, name=name, grid=(rt.n_tiles,),
        in_specs=[_row_spec(rt, D_MODEL), _vec_spec(D_MODEL), _mod_spec(rt)] + w_specs + [_table_spec(rt)] * 3 + [_vec_spec(128)] * 2,
        out_specs=[_row_spec(rt, D_MODEL), _row_spec(rt, NORMED_COLS), _row_spec(rt, IN_COLS)],
        out_shape=[jax.ShapeDtypeStruct((rt.rows, D_MODEL), BF16), jax.ShapeDtypeStruct((rt.rows, NORMED_COLS), F32),
                   jax.ShapeDtypeStruct((rt.rows, IN_COLS), BF16)],
        scratch_shapes=[pltpu.VMEM((rt.tm, IN_COLS), F32), pltpu.VMEM((D_MODEL, IN_COLS), BF16)],
        compiler_params=_params(("arbitrary",)),
    )(h, gamma, mod, *w_args, *tables, qn, kn)


def _in_bwd(rt, dq, dkv, qkv, tables, qn, kn, wg, h, dres, mod, gamma, latent_only, name, comm=None):
    last = rt.n_lat_tiles - 1
    w_specs, w_args = _in_weight_operands(wg)
    n_w = len(w_args)

    def body(dq_ref, dkv_ref, qkv_ref, c_ref, s1_ref, s2_ref, qn_ref, kn_ref, *rest):
        h_ref, dres_ref, mod_ref, g_ref, dqkv_ref, dh_ref, dqn_ref, dkn_ref, dsh_ref, dsc_ref, dg_ref, w_scr = rest[n_w:]
        i = pl.program_id(0)

        @pl.when(i == 0)
        def _():
            _unpack_in_pieces(rest[0], rest[1] if n_w == 2 else None, w_scr)

        dqn, dkn = _prep_bwd_body(dq_ref, dkv_ref, qkv_ref, c_ref[...], s1_ref[...], s2_ref[...], qn_ref[...], kn_ref[...], dqkv_ref)
        du = lax.dot_general(dqkv_ref[...], w_scr[...], NT, preferred_element_type=F32)
        dh, dsh, dsc, dg = _norm_mod_bwd_val(du, h_ref[...], g_ref[...], 1.0 + mod_ref[0, 1:2, :])
        if latent_only:
            @pl.when(i <= last)
            def _():
                dh_ref[...] = dres_ref[...] + dh
        else:
            dh_ref[...] = dres_ref[...] + dh
        _accumulate(rt, i, [(dsh_ref, dsh), (dsc_ref, dsc)], [(dg_ref, dg), (dqn_ref, dqn), (dkn_ref, dkn)])

    dh_spec = pl.BlockSpec((rt.tm, D_MODEL), lambda i: (jnp.minimum(i, last), 0)) if latent_only else _row_spec(rt, D_MODEL)
    return _comm_call(
        body, comm, name=name, grid=(rt.n_tiles,),
        in_specs=[_row_spec(rt, 1024), _row_spec(rt, 512), _row_spec(rt, NORMED_COLS)] + [_table_spec(rt)] * 3 + [_vec_spec(128)] * 2
        + w_specs + [_row_spec(rt, D_MODEL), _row_spec(rt, D_MODEL), _mod_spec(rt), _vec_spec(D_MODEL)],
        out_specs=[_row_spec(rt, IN_COLS), dh_spec, _vec_spec(128), _vec_spec(128),
                   _group_spec(rt), _group_spec(rt), _vec_spec(D_MODEL)],
        out_shape=[jax.ShapeDtypeStruct((rt.rows, IN_COLS), BF16),
                   jax.ShapeDtypeStruct((rt.n_lat if latent_only else rt.rows, D_MODEL), F32),
                   _vec_shape(128), _vec_shape(128), _group_shape(rt), _group_shape(rt), _vec_shape()],
        args=[dq, dkv, qkv, *tables, qn, kn, *w_args, h, dres, mod, gamma], aliases={}, semantics=("arbitrary",),
        scratch=[pltpu.VMEM((D_MODEL, IN_COLS), BF16)])


def _out_fwd(rt, o, wg, h, mod, g_post_mix, g_pre_mlp, name):
    def body(o_ref, w_ref, h_ref, mod_ref, gpost_ref, gpre_ref, mix_ref, h1_ref, u2_ref):
        mix = jnp.dot(o_ref[...], w_ref[...].reshape(D_MODEL, D_MODEL), preferred_element_type=F32)
        mix_ref[...] = mix
        h1 = _post_norm_val(h_ref[...], mix, gpost_ref[...], mod_ref, 2)
        h1_ref[...] = h1
        u2_ref[...] = _norm_mod_val(h1, gpre_ref[...], mod_ref, 3, 4).astype(BF16)

    return pl.pallas_call(
        body, name=name, grid=(rt.n_tiles,),
        in_specs=[_row_spec(rt, D_MODEL), _gathered_spec(wg, "out"), _row_spec(rt, D_MODEL), _mod_spec(rt),
                  _vec_spec(D_MODEL), _vec_spec(D_MODEL)],
        out_specs=[_row_spec(rt, D_MODEL)] * 3,
        out_shape=[jax.ShapeDtypeStruct((rt.rows, D_MODEL), F32), jax.ShapeDtypeStruct((rt.rows, D_MODEL), F32),
                   jax.ShapeDtypeStruct((rt.rows, D_MODEL), BF16)],
        compiler_params=_params(("parallel",)),
    )(o, wg["out"][0], h, mod, g_post_mix, g_pre_mlp)


def _out_bwd(rt, dh1, mix, wg, mod, g_post_mix, name, comm=None):
    def body(dh_ref, mix_ref, w_ref, mod_ref, g_ref, dmix_ref, do_ref, dgate_ref, dg_ref):
        i = pl.program_id(0)
        dz, dgate, dg = _post_norm_bwd_val(dh_ref[...], mix_ref[...], g_ref[...], mod_ref[0, 2:3, :])
        dzb = dz.astype(BF16)
        dmix_ref[...] = dzb
        do_ref[...] = lax.dot_general(dzb, w_ref[...].reshape(D_MODEL, D_MODEL), NT, preferred_element_type=F32).astype(BF16)
        _accumulate(rt, i, [(dgate_ref, dgate)], [(dg_ref, dg)])

    return _comm_call(
        body, comm, name=name, grid=(rt.n_tiles,),
        in_specs=[_row_spec(rt, D_MODEL), _row_spec(rt, D_MODEL), _gathered_spec(wg, "out"), _mod_spec(rt), _vec_spec(D_MODEL)],
        out_specs=[_row_spec(rt, D_MODEL), _row_spec(rt, D_MODEL), _group_spec(rt), _vec_spec(D_MODEL)],
        out_shape=[jax.ShapeDtypeStruct((rt.rows, D_MODEL), BF16), jax.ShapeDtypeStruct((rt.rows, D_MODEL), BF16),
                   _group_shape(rt), _vec_shape()],
        args=[dh1, mix, wg["out"][0], mod, g_post_mix], aliases={}, semantics=("arbitrary",))


def _w_chunk(w_ref, k):
    return w_ref[2 * k:2 * k + 2].reshape(1024, 1024)


def _mlp_fwd(rt, u2, h1, wg, mod, g_post_mlp, name, comm=None, target=None):
    last = rt.n_lat_tiles - 1

    def body(u2_ref, h1_ref, wu_ref, wd_ref, mod_ref, g_ref, *rest):
        u2_ = u2_ref[...]
        y = jnp.zeros((rt.tm, D_MODEL), F32)
        for k in range(D_FF // 1024):
            a = jnp.maximum(jnp.dot(u2_, _w_chunk(wu_ref, k), preferred_element_type=F32), 0.0)
            rest[-3 if target is None else -4][:, k * 1024:(k + 1) * 1024] = a.astype(BF16)
            y = y + jnp.dot((a * a).astype(BF16), _w_chunk(wd_ref, k), preferred_element_type=F32)
        h2 = _post_norm_val(h1_ref[...], y, g_ref[...], mod_ref, 5)
        if target is None:
            _, y_ref, h2_ref = rest
            y_ref[...] = y
            h2_ref[...] = h2
        else:
            t_ref, _, y_ref, dh_ref, sq_ref = rest
            y_ref[...] = y
            i = pl.program_id(0)

            @pl.when(i == 0)
            def _():
                sq_ref[...] = jnp.zeros_like(sq_ref)

            @pl.when(i <= last)
            def _():
                e = h2 - t_ref[...]
                dh_ref[...] = e * (1.0 / D_MODEL)
                sq_ref[...] += jnp.sum(e * e, axis=0, keepdims=True)

            @pl.when(i > last)
            def _():
                dh_ref[...] = jnp.zeros_like(dh_ref)

    in_specs = [_row_spec(rt, D_MODEL), _row_spec(rt, D_MODEL), _gathered_spec(wg, "up"), _gathered_spec(wg, "down"),
                _mod_spec(rt), _vec_spec(D_MODEL)]
    args = [u2, h1, wg["up"][0], wg["down"][0], mod, g_post_mlp]
    out_specs = [_row_spec(rt, D_FF), _row_spec(rt, D_MODEL), _row_spec(rt, D_MODEL)]
    out_shape = [jax.ShapeDtypeStruct((rt.rows, D_FF), BF16), jax.ShapeDtypeStruct((rt.rows, D_MODEL), F32),
                 jax.ShapeDtypeStruct((rt.rows, D_MODEL), F32)]
    if target is not None:
        in_specs.append(pl.BlockSpec((rt.tm, D_MODEL), lambda i: (jnp.minimum(i, last), 0)))
        args.append(target)
        out_specs.append(_vec_spec(D_MODEL))
        out_shape.append(_vec_shape())
    return _comm_call(body, comm, name=name, grid=(rt.n_tiles,), in_specs=in_specs, out_specs=out_specs, out_shape=out_shape,
                      args=args, aliases={}, semantics=("parallel",) if target is None else ("arbitrary",))


def _mlp_down_bwd(rt, dh, y, ra, wg, mod, g_post_mlp, name, comm=None):
    def body(dh_ref, y_ref, ra_ref, wd_ref, mod_ref, g_ref, dy_ref, da_ref, dgate_ref, dg_ref):
        i = pl.program_id(0)
        dz, dgate, dg = _post_norm_bwd_val(dh_ref[...], y_ref[...], g_ref[...], mod_ref[0, 5:6, :])
        dyb = dz.astype(BF16)
        dy_ref[...] = dyb
        for k in range(D_FF // 1024):
            dr = lax.dot_general(dyb, _w_chunk(wd_ref, k), NT, preferred_element_type=F32)
            da_ref[:, k * 1024:(k + 1) * 1024] = (dr * (2.0 * ra_ref[:, k * 1024:(k + 1) * 1024].astype(F32))).astype(BF16)
        _accumulate(rt, i, [(dgate_ref, dgate)], [(dg_ref, dg)])

    return _comm_call(
        body, comm, name=name, grid=(rt.n_tiles,),
        in_specs=[_row_spec(rt, D_MODEL), _row_spec(rt, D_MODEL), _row_spec(rt, D_FF), _gathered_spec(wg, "down"),
                  _mod_spec(rt), _vec_spec(D_MODEL)],
        out_specs=[_row_spec(rt, D_MODEL), _row_spec(rt, D_FF), _group_spec(rt), _vec_spec(D_MODEL)],
        out_shape=[jax.ShapeDtypeStruct((rt.rows, D_MODEL), BF16), jax.ShapeDtypeStruct((rt.rows, D_FF), BF16),
                   _group_shape(rt), _vec_shape()],
        args=[dh, y, ra, wg["down"][0], mod, g_post_mlp], aliases={}, semantics=("arbitrary",))


def _mlp_up_bwd(rt, da, wg, h1, dh, mod, g_pre_mlp, name):
    def body(da_ref, wu_ref, h1_ref, dh_ref, mod_ref, g_ref, dh1_ref, dsh_ref, dsc_ref, dg_ref):
        i = pl.program_id(0)
        du = jnp.zeros((rt.tm, D_MODEL), F32)
        for k in range(D_FF // 1024):
            du = du + lax.dot_general(da_ref[:, k * 1024:(k + 1) * 1024], _w_chunk(wu_ref, k), NT, preferred_element_type=F32)
        d, dsh, dsc, dg = _norm_mod_bwd_val(du, h1_ref[...], g_ref[...], 1.0 + mod_ref[0, 4:5, :])
        dh1_ref[...] = dh_ref[...] + d
        _accumulate(rt, i, [(dsh_ref, dsh), (dsc_ref, dsc)], [(dg_ref, dg)])

    return pl.pallas_call(
        body, name=name, grid=(rt.n_tiles,),
        in_specs=[_row_spec(rt, D_FF), _gathered_spec(wg, "up"), _row_spec(rt, D_MODEL), _row_spec(rt, D_MODEL),
                  _mod_spec(rt), _vec_spec(D_MODEL)],
        out_specs=[_row_spec(rt, D_MODEL), _group_spec(rt), _group_spec(rt), _vec_spec(D_MODEL)],
        out_shape=[jax.ShapeDtypeStruct((rt.rows, D_MODEL), F32), _group_shape(rt), _group_shape(rt), _vec_shape()],
        compiler_params=_params(("arbitrary",)),
    )(da, wg["up"][0], h1, dh, mod, g_pre_mlp)


def _wgrad_packed(rt, a, b, kind, off, n_rows, p_prev, name, comm=None):
    h = PACK_HEIGHT[kind]
    tk = rt.tm
    assert off % h == 0, (kind, off)

    def body(a_ref, b_ref, *rest):
        o_ref = rest[-1]
        i = pl.program_id(0)

        @pl.when(i == 0)
        def _():
            o_ref[...] = jnp.zeros_like(o_ref)

        if kind == "in":
            res = lax.dot_general(a_ref[...], b_ref[...], TN, preferred_element_type=F32)
            for k in range(4):
                for c in range(2):
                    for t in range(2):
                        o_ref[c, k, :, t * IN_PIECE_COLS:(t + 1) * IN_PIECE_COLS] += \
                            res[c * 512 + t * h:c * 512 + (t + 1) * h, k * IN_PIECE_COLS:(k + 1) * IN_PIECE_COLS]
        elif kind == "out":
            res = lax.dot_general(a_ref[...], b_ref[...], TN, preferred_element_type=F32)
            for k in range(4):
                for c in range(2):
                    o_ref[c, k] += res[(2 * k + c) * h:(2 * k + c + 1) * h]
        else:
            for k in range(4):
                if kind == "up":
                    res = lax.dot_general(a_ref[...], b_ref[:, k * 1024:(k + 1) * 1024], TN, preferred_element_type=F32)
                else:
                    ra = a_ref[:, k * 1024:(k + 1) * 1024].astype(F32)
                    res = lax.dot_general((ra * ra).astype(BF16), b_ref[...], TN, preferred_element_type=F32)
                o_ref[0, k] += res[0:h]
                o_ref[1, k] += res[h:2 * h]

    in_specs = [pl.BlockSpec((tk, a.shape[1]), lambda i: (i, 0)), pl.BlockSpec((tk, b.shape[1]), lambda i: (i, 0))]
    args = [a, b]
    aliases = {}
    if p_prev is not None:
        in_specs.append(pl.BlockSpec(memory_space=pl.ANY))
        args.append(p_prev)
        aliases = {2: 0}
    outs = _comm_call(
        body, comm, name=name, grid=(rt.rows // tk,),
        in_specs=in_specs,
        out_specs=[pl.BlockSpec((2, 4, h, 1024), lambda i: (0, 0, off // h, 0))],
        out_shape=[jax.ShapeDtypeStruct((2, 4, n_rows, 1024), F32)],
        args=args, aliases=aliases, semantics=("arbitrary",))
    return outs[0] if comm is None else outs


def _ada_wgrad(xs, dm, name):
    depth, _, cols = dm.shape

    def body(x_ref, d_ref, o_ref):
        for l in range(depth):
            o_ref[l] = lax.dot_general(x_ref[...], d_ref[l], TN, preferred_element_type=F32)

    return pl.pallas_call(body, name=name, out_shape=jax.ShapeDtypeStruct((depth, xs.shape[1], cols), F32),
                          compiler_params=pltpu.CompilerParams(vmem_limit_bytes=VMEM_LIMIT))(xs, dm)


def _stack_heads(x, kvi):
    x = x.astype(F32)
    tq = x.shape[0]
    lane = lax.broadcasted_iota(jnp.int32, (tq, 128), 1)
    keep = lane < HEAD_DIM if kvi == 0 else lane >= HEAD_DIM
    parts = []
    for p in range(2):
        pair = x[:, p * 128:(p + 1) * 128]
        swapped = pltpu.roll(pair, HEAD_DIM, 1)
        lo_head, hi_head = (pair, swapped) if kvi == 0 else (swapped, pair)
        parts += [jnp.where(keep, lo_head, 0.0), jnp.where(keep, hi_head, 0.0)]
    return jnp.concatenate(parts, axis=0).astype(BF16)


def _unstack_heads(o4, kvi):
    tq = o4.shape[0] // GROUP
    lane = lax.broadcasted_iota(jnp.int32, (tq, 128), 1)
    outs = []
    for p in range(2):
        r_lo, r_hi = o4[(2 * p) * tq:(2 * p + 1) * tq], o4[(2 * p + 1) * tq:(2 * p + 2) * tq]
        if kvi == 0:
            lo, hi = r_lo, pltpu.roll(r_hi, HEAD_DIM, 1)
        else:
            lo, hi = pltpu.roll(r_lo, HEAD_DIM, 1), r_hi
        outs.append(jnp.where(lane < HEAD_DIM, lo, hi))
    return jnp.concatenate(outs, axis=1)


def _per_head(shape, axis, tq, values):
    head = lax.broadcasted_iota(jnp.int32, shape, axis) // tq
    out = jnp.zeros(shape, F32)
    for g in range(GROUP):
        out = jnp.where(head == g, values[g], out)
    return out


KEY_CHUNK = 512
Q_TILE = 128
Q_TILE_FWD = 256


def _key_chunks(k_ref, v_ref, n, kc=KEY_CHUNK):
    kc = min(kc, n)
    return [(k_ref[c * kc:(c + 1) * kc, :], v_ref[c * kc:(c + 1) * kc, :], None) for c in range(n // kc)]


def _softmax_fwd(qs, chunks, sink_col):
    logits = []
    for k, _, mask in chunks:
        s = lax.dot_general(qs, k, NT, preferred_element_type=F32)
        logits.append(s if mask is None else jnp.where(mask, s, NEG_BIG))
    m = functools.reduce(jnp.maximum, [jnp.max(s, axis=1, keepdims=True) for s in logits])
    if sink_col is not None:
        m = jnp.maximum(m, sink_col)
    l = jnp.zeros_like(m) if sink_col is None else jnp.exp(sink_col - m)
    acc = jnp.zeros((qs.shape[0], 128), F32)
    for s, (_, v, _) in zip(logits, chunks):
        p = jnp.exp(s - m)
        l = l + jnp.sum(p, axis=1, keepdims=True)
        acc = acc + jnp.dot(p.astype(BF16), v, preferred_element_type=F32)
    return acc / l, m + jnp.log(l)


def _to_rows(col):
    return jnp.transpose(jnp.broadcast_to(col, (col.shape[0], 128)))[0:8, :]


def _softmax_bwd(qs, dos, lse_row, delta_row, chunks):
    dq = jnp.zeros((qs.shape[0], 128), F32)
    grads = []
    for k, v, mask in chunks:
        s = lax.dot_general(k, qs, NT, preferred_element_type=F32)
        if mask is not None:
            s = jnp.where(mask, s, NEG_BIG)
        p = jnp.exp(s - lse_row)
        dp = lax.dot_general(v, dos, NT, preferred_element_type=F32)
        ds = (p * (dp - delta_row)).astype(BF16)
        dv = jnp.dot(p.astype(BF16), dos, preferred_element_type=F32)
        dk = jnp.dot(ds, qs, preferred_element_type=F32)
        dq = dq + lax.dot_general(ds, k, TN, preferred_element_type=F32)
        grads.append((dk, dv))
    return dq, grads


def _band(qi, tq, seq):
    span = tq + 2 * WINDOW
    start = pl.multiple_of(jnp.clip(qi * tq - WINDOW, 0, seq - span), 64)
    return start, span


def _band_mask(qi, tq, start, span, query_axis):
    shape = (GROUP * tq, span) if query_axis == 0 else (span, GROUP * tq)
    qpos = qi * tq + lax.broadcasted_iota(jnp.int32, shape, query_axis) % tq
    kpos = start + lax.broadcasted_iota(jnp.int32, shape, 1 - query_axis)
    return jnp.abs(kpos - qpos) <= WINDOW


def _qkv_specs(rt, tq, q_row, ctx_row, with_latent):
    specs = [pl.BlockSpec((tq, 256), functools.partial(lambda b, i, col: (q_row(b, i), col), col=col)) for col in (0, 1, 3, 4)]
    if with_latent:
        specs += [pl.BlockSpec((rt.seq, 128), functools.partial(lambda b, i, col: (b, col), col=col))
                  for col in (COL_KA, COL_VA, COL_KB, COL_VB)]
    specs += [pl.BlockSpec((rt.ctx, 128), functools.partial(lambda b, i, col: (ctx_row(b), col), col=col))
              for col in (COL_KA, COL_VA, COL_KB, COL_VB)]
    return specs


def _attn_fwd(rt, qkvp, sink, o_prev, name, comm=None):
    latent = o_prev is None
    seq, ctx, nb = rt.seq, rt.ctx, rt.nb
    tq = Q_TILE_FWD if latent else ctx
    tile = Q_TILE if latent else ctx
    parts = tq // tile
    nq = seq // tq if latent else 1
    ctx_blk0 = rt.n_lat // ctx
    q_row = (lambda b, i: b * nq + i) if latent else (lambda b, i: ctx_blk0 + b)

    def store_lse(lse_ref, j, lse_col):
        rows = _to_rows(lse_col)
        for part in range(parts):
            lse_ref[part, j] = jnp.concatenate([rows[:, g * tq + part * tile:g * tq + (part + 1) * tile] for g in range(GROUP)], axis=1)

    def body(sink_ref, qa0, qa1, qb0, qb1, *rest):
        if latent:
            kal, val, kbl, vbl, kac, vac, kbc, vbc, o_ref, lse_ref = rest
        else:
            kac, vac, kbc, vbc, _, o_ref, lse_ref = rest
        qi = pl.program_id(1)
        for kvi, (qa, qb) in enumerate(((qa0, qb0), (qa1, qb1))):
            src_a = _key_chunks(kac, vac, ctx)
            src_b = _key_chunks(kbc, vbc, ctx)
            if latent:
                src_a += _key_chunks(kal, val, seq, seq)
                start, span = _band(qi, tq, seq)
                src_b.append((kbl[pl.ds(start, span), :], vbl[pl.ds(start, span), :], _band_mask(qi, tq, start, span, 0)))
            oa, lse = _softmax_fwd(_stack_heads(qa[...], kvi), src_a, None)
            o_ref[:, kvi * 256:(kvi + 1) * 256] = _unstack_heads(oa, kvi).astype(BF16)
            store_lse(lse_ref, kvi, lse)
            sink_col = _per_head((GROUP * tq, 1), 0, tq, [sink_ref[kvi * GROUP + g] for g in range(GROUP)])
            ob, lse = _softmax_fwd(_stack_heads(qb[...], kvi), src_b, sink_col)
            o_ref[:, 512 + kvi * 256:512 + (kvi + 1) * 256] = _unstack_heads(ob, kvi).astype(BF16)
            store_lse(lse_ref, 2 + kvi, lse)

    specs = _qkv_specs(rt, tq, q_row, lambda b: ctx_blk0 + b, latent)
    args = [sink] + [qkvp] * len(specs)
    in_specs = [pl.BlockSpec(memory_space=pltpu.SMEM)] + specs
    aliases = {}
    if not latent:
        in_specs.append(pl.BlockSpec(memory_space=pl.ANY))
        args.append(o_prev)
        aliases = {len(args) - 1: 0}
    return _comm_call(
        body, comm, name=name, grid=(nb, nq),
        in_specs=in_specs,
        out_specs=[pl.BlockSpec((tq, 1024), lambda b, i: (q_row(b, i), 0)),
                   pl.BlockSpec((parts, 4, 8, GROUP * tile), lambda b, i: (b * nq + i, 0, 0, 0))],
        out_shape=[jax.ShapeDtypeStruct((rt.rows, 1024), BF16), jax.ShapeDtypeStruct((nb * nq * parts, 4, 8, GROUP * tile), F32)],
        args=args, aliases=aliases, semantics=("parallel", "parallel"))


def _attn_bwd(rt, qkvp, o, lse, do, sink, prev, name, comm=None):
    latent = prev is None
    seq, ctx, nb = rt.seq, rt.ctx, rt.nb
    tq = Q_TILE if latent else ctx
    nq = seq // tq if latent else 1
    ctx_blk0 = rt.n_lat // ctx
    q_row = (lambda b, i: b * nq + i) if latent else (lambda b, i: ctx_blk0 + b)
    kc = min(KEY_CHUNK, seq)

    def body(sink_ref, qa0, qa1, qb0, qb1, *rest):
        if latent:
            kal, val, kbl, vbl, kac, vac, kbc, vbc, do_ref, o_ref, lse_ref, dq_ref, dl_ref, dc_ref, dsink_ref = rest
        else:
            kac, vac, kbc, vbc, do_ref, o_ref, lse_ref, c1_ref, _, _, dq_ref, dc_ref, dsink_ref = rest
        b, qi = pl.program_id(0), pl.program_id(1)

        def rows_of(cols, kvi, mixer):
            dos = _stack_heads(do_ref[:, cols], kvi)
            delta = jnp.sum(dos.astype(F32) * _stack_heads(o_ref[:, cols], kvi).astype(F32), axis=1, keepdims=True)
            return dos, lse_ref[0, 2 * mixer + kvi, 0:1, :], _to_rows(delta)[0:1, :]

        @pl.when(jnp.logical_and(b == 0, qi == 0))
        def _():
            dsink_ref[...] = jnp.zeros_like(dsink_ref)

        if latent:
            @pl.when(qi == 0)
            def _():
                dc_ref[...] = jnp.zeros_like(dc_ref)
                dl_ref[...] = jnp.zeros_like(dl_ref)
        else:
            dc_ref[...] = c1_ref[...]

        head_row = lax.broadcasted_iota(jnp.int32, (8, 128), 0)
        for kvi, (qa, qb) in enumerate(((qa0, qb0), (qa1, qb1))):
            cols = slice(kvi * 256, (kvi + 1) * 256)
            dos, lse_row, delta_row = rows_of(cols, kvi, 0)
            src = _key_chunks(kac, vac, ctx)
            if latent:
                src += _key_chunks(kal, val, seq)
            dq4, grads = _softmax_bwd(_stack_heads(qa[...], kvi), dos, lse_row, delta_row, src)
            dq_ref[:, cols] = _unstack_heads(dq4, kvi)
            dc_ref[:, 0:128] += grads[0][0]
            dc_ref[:, 128:256] += grads[0][1]
            for c, (dk, dv) in enumerate(grads[1:]):
                dl_ref[c * kc:(c + 1) * kc, 0:128] += dk
                dl_ref[c * kc:(c + 1) * kc, 128:256] += dv
            cols = slice(512 + kvi * 256, 512 + (kvi + 1) * 256)
            dos, lse_row, delta_row = rows_of(cols, kvi, 1)
            src = _key_chunks(kbc, vbc, ctx)
            if latent:
                start, span = _band(qi, tq, seq)
                src.append((kbl[pl.ds(start, span), :], vbl[pl.ds(start, span), :], _band_mask(qi, tq, start, span, 1)))
            dq4, grads = _softmax_bwd(_stack_heads(qb[...], kvi), dos, lse_row, delta_row, src)
            dq_ref[:, cols] = _unstack_heads(dq4, kvi)
            dc_ref[:, 256:384] += grads[0][0]
            dc_ref[:, 384:512] += grads[0][1]
            if latent:
                dl_ref[pl.ds(start, span), 256:384] += grads[1][0]
                dl_ref[pl.ds(start, span), 384:512] += grads[1][1]
            sink_row = _per_head((1, GROUP * tq), 1, tq, [sink_ref[kvi * GROUP + g] for g in range(GROUP)])
            dsink = -jnp.exp(sink_row - lse_row) * delta_row
            head = lax.broadcasted_iota(jnp.int32, (1, GROUP * tq), 1) // tq
            upd = jnp.zeros((8, 128), F32)
            for g in range(GROUP):
                upd = jnp.where(head_row == kvi * GROUP + g, jnp.sum(jnp.where(head == g, dsink, 0.0)), upd)
            dsink_ref[...] += upd

    specs = _qkv_specs(rt, tq, q_row, lambda b: ctx_blk0 + b, latent)
    q_rows_spec = pl.BlockSpec((tq, 1024), lambda b, i: (q_row(b, i), 0))
    in_specs = ([pl.BlockSpec(memory_space=pltpu.SMEM)] + specs
                + [q_rows_spec, q_rows_spec, pl.BlockSpec((1, 4, 8, GROUP * tq), lambda b, i: (b * nq + i, 0, 0, 0))])
    args = [sink] + [qkvp] * len(specs) + [do, o, lse]
    dq_shape = jax.ShapeDtypeStruct((rt.rows, 1024), F32)
    dkv_shape = jax.ShapeDtypeStruct((rt.rows, 512), F32)
    dsink_spec, dsink_shape = pl.BlockSpec((8, 128), lambda b, i: (0, 0)), jax.ShapeDtypeStruct((8, 128), F32)
    dq_spec = pl.BlockSpec((tq, 1024), lambda b, i: (q_row(b, i), 0))
    if latent:
        out_specs = [dq_spec, pl.BlockSpec((seq, 512), lambda b, i: (b, 0)), pl.BlockSpec((ctx, 512), lambda b, i: (b, 0)), dsink_spec]
        out_shape = [dq_shape, dkv_shape, jax.ShapeDtypeStruct((rt.n_ctx, 512), F32), dsink_shape]
        aliases = {}
    else:
        dq_prev, dkv_prev, c1 = prev
        in_specs += [pl.BlockSpec((ctx, 512), lambda b, i: (b, 0)), pl.BlockSpec(memory_space=pl.ANY), pl.BlockSpec(memory_space=pl.ANY)]
        args += [c1, dq_prev, dkv_prev]
        out_specs = [dq_spec, pl.BlockSpec((ctx, 512), lambda b, i: (ctx_blk0 + b, 0)), dsink_spec]
        out_shape = [dq_shape, dkv_shape, dsink_shape]
        aliases = {len(args) - 2: 0, len(args) - 1: 1}
    return _comm_call(body, comm, name=name, grid=(nb, nq), in_specs=in_specs, out_specs=out_specs, out_shape=out_shape,
                      args=args, aliases=aliases, semantics=("arbitrary", "arbitrary"))


def _silu(x):
    return x / (1.0 + jnp.exp(-x))


def _whole(shape):
    return pl.BlockSpec(shape, lambda i, s: (0,) * len(shape))


def _ada_half_spec(cols):
    return pl.BlockSpec((DEPTH, D_MODEL, cols), lambda i, s: (0, 0, s[0]))


def _ada_fwd(cond, w_ada, b_half, c_idx, name):
    rows = cond.shape[0]
    cols = w_ada.shape[2] // 2

    def body(s_ref, c_ref, w_ref, b_ref, x_ref, o_ref):
        xs = _silu(c_ref[...]).astype(BF16)
        x_ref[...] = xs
        for l in range(DEPTH):
            o_ref[l] = jnp.dot(xs, w_ref[l].astype(BF16), preferred_element_type=F32) + b_ref[l]

    grid_spec = pltpu.PrefetchScalarGridSpec(
        num_scalar_prefetch=1, grid=(1,),
        in_specs=[_whole(cond.shape), _ada_half_spec(cols), _whole(b_half.shape)],
        out_specs=[_whole((rows, D_MODEL)), _whole((DEPTH, rows, cols))])
    return pl.pallas_call(
        body, name=name, grid_spec=grid_spec,
        out_shape=[jax.ShapeDtypeStruct((rows, D_MODEL), BF16), jax.ShapeDtypeStruct((DEPTH, rows, cols), F32)],
        compiler_params=_params(("arbitrary",)),
    )(c_idx, cond, w_ada, b_half)


def _ada_cond_bwd(dcx, w_ada, c_idx, name):
    _, rows, cols = dcx.shape

    def body(s_ref, d_ref, w_ref, o_ref):
        acc = jnp.zeros((rows, D_MODEL), F32)
        for l in range(DEPTH):
            acc = acc + lax.dot_general(d_ref[l], w_ref[l].astype(BF16), NT, preferred_element_type=F32)
        o_ref[...] = acc

    grid_spec = pltpu.PrefetchScalarGridSpec(
        num_scalar_prefetch=1, grid=(1,),
        in_specs=[_whole(dcx.shape), _ada_half_spec(cols)], out_specs=_whole((rows, D_MODEL)))
    return pl.pallas_call(body, name=name, grid_spec=grid_spec, out_shape=jax.ShapeDtypeStruct((rows, D_MODEL), F32),
                          compiler_params=_params(("arbitrary",)))(c_idx, dcx, w_ada)


def _dev_sum(x, name):
    _, r, c = x.shape

    def body(x_ref, o_ref):
        v = x_ref[0]
        for d in range(1, N_DEV):
            v = v + x_ref[d]
        o_ref[...] = v

    return pl.pallas_call(body, name=name, out_shape=jax.ShapeDtypeStruct((r, c), F32))(x)


def _adam_val(w, g, m, v):
    c1 = 1.0 / (1.0 - ADAM_B1 ** ADAM_STEP)
    c2 = 1.0 / (1.0 - ADAM_B2 ** ADAM_STEP)
    nm = ADAM_B1 * m + (1.0 - ADAM_B1) * g
    nv = ADAM_B2 * v + (1.0 - ADAM_B2) * (g * g)
    return -ADAM_LR * ((nm * c1) / (jnp.sqrt(nv * c2) + ADAM_EPS) + ADAM_WD * w), nm, nv


def _small_update(tot, dcc_parts, params, n_groups, name):
    n_p = len(params)
    mod_rows = n_groups * N_MOD

    def body(tot_ref, dcc_ref, *refs):
        ins, outs = refs[:3 * n_p], refs[3 * n_p:]

        def update(p, rows, cols, g):
            w_ref, m_ref, v_ref = ins[3 * p:3 * p + 3]
            g_ref, d_ref, nm_ref, nv_ref = outs[4 * p:4 * p + 4]
            d, nm, nv = _adam_val(w_ref[rows, cols], g, m_ref[rows, cols], v_ref[rows, cols])
            g_ref[rows, cols] = g
            d_ref[rows, cols] = d
            nm_ref[rows, cols] = nm
            nv_ref[rows, cols] = nv

        acc = dcc_ref[0, 0:1, :]
        for d in range(1, N_DEV):
            acc = acc + dcc_ref[d, 0:1, :]
        c = ins[0][...]
        sg = 1.0 / (1.0 + jnp.exp(-c))
        update(0, slice(0, 1), slice(None), acc * (sg * (1.0 + c * (1.0 - sg))))
        for l in range(DEPTH):
            for i in range(N_MOD):
                g = tot_ref[l * mod_rows + i:l * mod_rows + i + 1, :]
                for grp in range(1, n_groups):
                    g = g + tot_ref[l * mod_rows + grp * N_MOD + i:l * mod_rows + grp * N_MOD + i + 1, :]
                update(1, slice(l, l + 1), slice(i * D_MODEL, (i + 1) * D_MODEL), g)
            for j in range(4):
                row = DEPTH * mod_rows + 4 * l + j
                update(2 + j, slice(l, l + 1), slice(None), tot_ref[row:row + 1, :])

    shapes = [jax.ShapeDtypeStruct(w.shape, F32) for w, _, _ in params for _ in range(4)]
    outs = pl.pallas_call(body, name=name, out_shape=shapes)(tot, dcc_parts, *[a for p in params for a in p])
    return [tuple(outs[4 * p:4 * p + 4]) for p in range(n_p)]


def _adamw(w, g, m, v, name):
    r, c = w.shape
    tr = _pick(r, (256, 128, 64, 32, 24, 16, 8))

    def body(w_ref, g_ref, m_ref, v_ref, d_ref, nm_ref, nv_ref):
        d_ref[...], nm_ref[...], nv_ref[...] = _adam_val(w_ref[...], g_ref[...], m_ref[...], v_ref[...])

    spec = pl.BlockSpec((tr, c), lambda i: (i, 0))
    return pl.pallas_call(body, name=name, grid=(r // tr,), in_specs=[spec] * 4, out_specs=[spec] * 3,
                          out_shape=[jax.ShapeDtypeStruct((r, c), F32)] * 3, compiler_params=_params(("parallel",)))(w, g, m, v)


def _adamw_shard(kind, l, w, m, v, halves, off, prev, name):
    h = PACK_HEIGHT[kind]
    assert off % h == 0, (kind, off)
    _, r, c = w.shape
    rows = r // 2

    def body(w_ref, m_ref, v_ref, p_ref, *rest):
        g_ref, d_ref, nm_ref, nv_ref = rest[-4:]
        if kind == "in":
            for t in range(2):
                g = p_ref[:, t * IN_PIECE_COLS:(t + 1) * IN_PIECE_COLS]
                rs = slice(t * h, (t + 1) * h)
                g_ref[rs, :] = g
                d_ref[rs, :], nm_ref[rs, :], nv_ref[rs, :] = _adam_val(w_ref[rs, :], g, m_ref[rs, :], v_ref[rs, :])
        else:
            g = p_ref[...]
            g_ref[...] = g
            d_ref[...], nm_ref[...], nv_ref[...] = _adam_val(w_ref[...], g, m_ref[...], v_ref[...])

    blk = pl.BlockSpec((None, rows, c), lambda half: (l, half, 0))
    in_specs = [blk, blk, blk, pl.BlockSpec((None, h, 1024), lambda half: (half, off // h, 0))]
    args = [w, m, v, halves]
    aliases = {}
    if prev is not None:
        in_specs += [pl.BlockSpec(memory_space=pl.ANY)] * 4
        args += list(prev)
        aliases = {4 + j: j for j in range(4)}
    return pl.pallas_call(
        body, name=name, grid=(2,), in_specs=in_specs, out_specs=[blk] * 4,
        out_shape=[jax.ShapeDtypeStruct(w.shape, F32)] * 4, input_output_aliases=aliases,
        compiler_params=_params(("parallel",)))(*args)


SMALL_ROWS = 48


def _small_rows(small, sq):
    def lane_pad(v):
        return jnp.pad(v, (0, D_MODEL - v.shape[0]))[None]

    head_rows = [lane_pad(jnp.concatenate([s["q_norm"][0], s["k_norm"][0], s["sink"]])) for s in small]
    loss_row = lane_pad((0.5 / D_MODEL) * jnp.sum(sq, keepdims=True)[0])
    rows = jnp.concatenate([s["mod"].reshape(-1, D_MODEL) for s in small] + [s["gammas"] for s in small] + head_rows + [loss_row], axis=0)
    return jnp.pad(rows, ((0, SMALL_ROWS - rows.shape[0]), (0, 0)))


def _local_step(x, ctx, target, mods, gam, qn, kn, sink, w_first, w_layers, packed, kc_idx):
    nb, seq, _ = x.shape
    rt = _Rows(nb, seq, ctx.shape[1])
    tables = _rope_tables(rt)
    fuse = packed is not None
    h = jnp.concatenate([x.reshape(rt.n_lat, D_MODEL), ctx.reshape(rt.n_ctx, D_MODEL)], axis=0)
    wg = [{}, {}] if fuse else [dict(w) for w in w_layers]
    wg[0]["in"] = (w_first, 0)
    if fuse:
        wg[0]["in_own"] = (packed, W_FIRST[0])
    saved = []
    for l in range(DEPTH):
        g_pre_mix, g_post_mix, g_pre_mlp, g_post_mlp = gam[l]
        u, qkv, qkvp = _in_fwd(rt, h, g_pre_mix, mods[l], wg[l], tables, qn[l], kn[l], f"in_fwd{l}")
        if fuse and l == 0:
            o, lse_lat, w_mlp0, w_out0, w_mix1 = _attn_fwd(rt, qkvp, sink[l], None, f"attn_lat_fwd{l}",
                                                          comm=_gather_comm(packed, [W_MLP0, W_OUT0, W_MIX1], lead=2))
            wg[0].update({kind: (w_mlp0, PACK_OFF[(kind, 0)] - W_MLP0[0]) for kind in ("up", "down")})
            wg[0]["out"] = (w_out0, 0)
            wg[1] = {kind: (w_mix1, PACK_OFF[(kind, 1)] - W_MIX1[0]) for kind in ("out", "in")}
        elif fuse:
            o, lse_lat, w_mlp1 = _attn_fwd(rt, qkvp, sink[l], None, f"attn_lat_fwd{l}", comm=_gather_comm(packed, [W_MLP1], lead=2))
            wg[1].update({kind: (w_mlp1, PACK_OFF[(kind, 1)] - W_MLP1[0]) for kind in ("up", "down")})
        else:
            o, lse_lat = _attn_fwd(rt, qkvp, sink[l], None, f"attn_lat_fwd{l}")
        o, lse_ctx = _attn_fwd(rt, qkvp, sink[l], o, f"attn_ctx_fwd{l}")
        mix, h1, u2 = _out_fwd(rt, o, wg[l], h, mods[l], g_post_mix, g_pre_mlp, f"out_fwd{l}")
        if l < DEPTH - 1:
            r, y, h2 = _mlp_fwd(rt, u2, h1, wg[l], mods[l], g_post_mlp, f"mlp_fwd{l}")
        else:
            r, y, dh, sq = _mlp_fwd(rt, u2, h1, wg[l], mods[l], g_post_mlp, f"mlp_fwd{l}", target=target.reshape(rt.n_lat, D_MODEL))
        saved.append((h, u, qkv, qkvp, o, lse_lat, lse_ctx, mix, h1, u2, r, y))
        h = h2

    small = [None] * DEPTH
    groups = {}
    for l in reversed(range(DEPTH)):
        g_pre_mix, g_post_mix, g_pre_mlp, g_post_mlp = gam[l]
        h0, u, qkv, qkvp, o, lse_lat, lse_ctx, mix, h1, u2, r, y = saved[l]
        mlp_group, mix_group = (G_LAYER1, G_LAYER1) if l == 1 else (G_MLP0, G_MIX0)
        hide = fuse and l == 0

        outs = _mlp_down_bwd(rt, dh, y, r, wg[l], mods[l], g_post_mlp, f"mlp_down_bwd{l}",
                             comm=_pair_comm(groups[G_LAYER1]) if hide else None)
        dy, da, d_gate_m, d_g_post_mlp = outs[:4]
        if hide:
            sum1 = _pair_sum(groups[G_LAYER1], outs[4], kc_idx, "grad_pair_sum_layer1")
        p_mlp = _wgrad_packed(rt, r, dy, "down", PACK_OFF[("down", l)] - mlp_group[0], mlp_group[1], None, f"mlp_down_wgrad{l}")
        dh1, d_sh_m, d_sc_m, d_g_pre_mlp = _mlp_up_bwd(rt, da, wg[l], h1, dh, mods[l], g_pre_mlp, f"mlp_up_bwd{l}")
        p_mlp = _wgrad_packed(rt, u2, da, "up", PACK_OFF[("up", l)] - mlp_group[0], mlp_group[1], p_mlp, f"mlp_up_wgrad{l}")
        outs = _out_bwd(rt, dh1, mix, wg[l], mods[l], g_post_mix, f"out_bwd{l}", comm=_pair_comm(p_mlp) if hide else None)
        dmix, do, d_gate_a, d_g_post_mix = outs[:4]
        if hide:
            sum0 = _pair_sum(p_mlp, outs[4], kc_idx, "grad_pair_sum_mlp0")
        p_mix = _wgrad_packed(rt, o, dmix, "out", PACK_OFF[("out", l)] - mix_group[0], mix_group[1],
                              p_mlp if l == 1 else None, f"out_wgrad{l}")
        outs = _attn_bwd(rt, qkvp, o, lse_lat, do, sink[l], None, f"attn_lat_bwd{l}",
                         comm=_chip_comm([sum1[1], sum0[1]]) if hide else None)
        dq, dkv, dkv_c, dsink1 = outs[:4]
        if hide:
            groups[G_LAYER1] = _owner_sum(sum1[0], outs[4], kc_idx, "grad_owner_sum_layer1")
            groups[G_MLP0] = _owner_sum(sum0[0], outs[5], kc_idx, "grad_owner_sum_mlp0")
        dq, dkv, dsink2 = _attn_bwd(rt, qkvp, o, lse_ctx, do, sink[l], (dq, dkv, dkv_c), f"attn_ctx_bwd{l}")
        dqkv, dh, dqn, dkn, d_sh_a, d_sc_a, d_g_pre_mix = _in_bwd(rt, dq, dkv, qkv, tables, qn[l], kn[l], wg[l], h0, dh1, mods[l],
                                                                  g_pre_mix, l == 0, f"in_bwd{l}")
        dmod = jnp.concatenate([d_sh_a, d_sc_a, d_gate_a, d_sh_m, d_sc_m, d_gate_m], axis=1)
        small[l] = dict(mod=dmod, gammas=jnp.concatenate([d_g_pre_mix, d_g_post_mix, d_g_pre_mlp, d_g_post_mlp], axis=0),
                        q_norm=dqn, k_norm=dkn, sink=(dsink1 + dsink2)[:, 0])
        gather = _gather_comm(_small_rows(small, sq), [(0, SMALL_ROWS)]) if hide else None
        outs = _wgrad_packed(rt, u, dqkv, "in", PACK_OFF[("in", l)] - mix_group[0], mix_group[1], p_mix, f"in_wgrad{l}", comm=gather)
        groups[mix_group], small_g = outs if hide else (outs, None)
        if not hide and l == 0:
            groups[G_MLP0] = p_mlp
    return sq, dh.reshape(nb, seq, D_MODEL), [groups[G_LAYER1], groups[G_MLP0], groups[G_MIX0]], small, small_g


def kernel(x, c, ctx, c_ctx, w_ada, b_ada, g_pre_mix, g_post_mix, g_pre_mlp, g_post_mlp, w_in, q_norm, k_norm, sink, w_out, w_up, w_down, loss_target, m_c_ctx, m_w_ada, m_b_ada, m_g_pre_mix, m_g_post_mix, m_g_pre_mlp, m_g_post_mlp, m_w_in, m_q_norm, m_k_norm, m_sink, m_w_out, m_w_up, m_w_down, v_c_ctx, v_w_ada, v_b_ada, v_g_pre_mix, v_g_post_mix, v_g_pre_mlp, v_g_post_mlp, v_w_in, v_q_norm, v_k_norm, v_sink, v_w_out, v_w_up, v_w_down):
    nb = x.shape[0]
    ix, iy, ic = lax.axis_index("x"), lax.axis_index("y"), lax.axis_index("c")
    chip = 2 * ix + iy
    dev = 2 * chip + ic
    ada_cols = w_ada.shape[2] // 2

    packed = _pack_local_half(w_in, w_out, w_up, w_down, ic)
    c_rows = c.reshape(8, (nb * D_MODEL) // 8)
    c_all, w_first = _comm_alone(_merge([_gather_comm(c_rows, [(0, c_rows.shape[0])]), _gather_comm(packed, [W_FIRST], copy_own=False)]),
                                 "gather_c_w_first")
    c_all = c_all.reshape(N_DEV * nb, D_MODEL)

    n_cond = N_DEV * nb + 1
    cond_rows = 16 * ((n_cond + 15) // 16)
    cond = jnp.concatenate([c_all, c_ctx[None, :], jnp.zeros((cond_rows - n_cond, D_MODEL), F32)], axis=0)
    c_idx = ic.reshape(1).astype(jnp.int32)
    kc_idx = jnp.stack([chip, ic]).astype(jnp.int32)
    b_ada_half = lax.dynamic_slice_in_dim(b_ada, dev * ada_cols, ada_cols, 1)[:, None, :]
    x_ada, mod_part = _ada_fwd(cond, w_ada, b_ada_half, c_idx, "ada_fwd")
    mod_g = _all_gather(mod_part.reshape(DEPTH * cond_rows, ada_cols), "gather_mod", False)
    mod_all = mod_g.reshape(N_DEV, DEPTH, cond_rows, ada_cols).transpose(1, 2, 0, 3).reshape(DEPTH, cond_rows, N_MOD * D_MODEL)
    mods = []
    for l in range(DEPTH):
        mine = lax.dynamic_slice_in_dim(mod_all[l], dev * nb, nb, 0)
        mods.append(jnp.concatenate([mine, mod_all[l, n_cond - 1:n_cond]], axis=0).reshape(nb + 1, N_MOD, D_MODEL))

    gam = [(g_pre_mix[l][None], g_post_mix[l][None], g_pre_mlp[l][None], g_post_mlp[l][None]) for l in range(DEPTH)]
    qn = [jnp.tile(q_norm[l], 2)[None] for l in range(DEPTH)]
    kn = [jnp.tile(k_norm[l], 2)[None] for l in range(DEPTH)]
    _, grad_x, (h_layer1, h_mlp0, p_mix0), _, small_g = _local_step(x, ctx, loss_target, mods, gam, qn, kn, [sink[l] for l in range(DEPTH)],
                                                                 w_first, None, packed, kc_idx)

    def step(w, g, m, v, name):
        shape = w.shape
        cols = shape[-1]
        outs = _adamw(w.reshape(-1, cols), g.reshape(-1, cols), m.reshape(-1, cols), v.reshape(-1, cols), name)
        return tuple(a.reshape(shape) for a in outs)

    def shard_update(kind, w, m, v, layer0, layer1):
        outs = None
        for l, (halves, group) in enumerate((layer0, layer1)):
            outs = _adamw_shard(kind, l, w, m, v, halves, PACK_OFF[(kind, l)] - group[0], outs, f"adamw_w_{kind}{l}")
        return tuple(outs)

    tot = _dev_sum(small_g, "small_sum")
    mod_rows = (nb + 1) * N_MOD
    o_head = DEPTH * mod_rows + 4 * DEPTH
    loss = tot[o_head + DEPTH, 0]
    grad_q_norm = tot[o_head:o_head + DEPTH, 0:64] + tot[o_head:o_head + DEPTH, 64:128]
    grad_k_norm = tot[o_head:o_head + DEPTH, 128:192] + tot[o_head:o_head + DEPTH, 192:256]
    grad_sink = tot[o_head:o_head + DEPTH, 256:264]

    ex = small_g[:, :DEPTH * mod_rows].reshape(N_DEV, DEPTH, nb + 1, N_MOD * D_MODEL)[:, :, :nb]
    ex = ex.transpose(1, 0, 2, 3).reshape(DEPTH, N_DEV * nb, N_MOD * D_MODEL)
    cx = tot[:DEPTH * mod_rows].reshape(DEPTH, nb + 1, N_MOD * D_MODEL)[:, nb:]
    dm = jnp.concatenate([ex, cx, jnp.zeros((DEPTH, cond_rows - n_cond, N_MOD * D_MODEL), F32)], axis=1)
    shard_cols = w_ada.shape[2]
    grad_w_ada = _ada_wgrad(x_ada, lax.dynamic_slice_in_dim(dm, chip * shard_cols, shard_cols, 2).astype(BF16), "ada_wgrad")
    dcx = jnp.pad(lax.dynamic_slice_in_dim(cx, dev * ada_cols, ada_cols, 2), ((0, 0), (0, 15), (0, 0))).astype(BF16)
    dcc = _ada_cond_bwd(dcx, w_ada, c_idx, "ada_cond_bwd")[0:8]

    r1, = _comm_alone(_pair_comm(p_mix0), "grad_pair_exchange_mix0")
    a32, a16 = _pair_sum(p_mix0, r1, kc_idx, "grad_pair_sum_mix0")
    r2, dcc_g = _comm_alone(_merge([_chip_comm([a16]), _gather_comm(dcc, [(0, dcc.shape[0])])]), "grad_chip_exchange_mix0")
    h_mix0 = _owner_sum(a32, r2, kc_idx, "grad_owner_sum_mix0")
    h_layer1, h_mlp0, h_mix0 = _comm_alone(_halves_comm([h_layer1, h_mlp0, h_mix0]), "grad_halves_exchange")

    dense_names = ["c_ctx", "b_ada", "g_pre_mix", "g_post_mix", "g_pre_mlp", "g_post_mlp"]
    dense = _small_update(tot, dcc_g, [(c_ctx[None], m_c_ctx[None], v_c_ctx[None]), (b_ada, m_b_ada, v_b_ada),
                                       (g_pre_mix, m_g_pre_mix, v_g_pre_mix), (g_post_mix, m_g_post_mix, v_g_post_mix),
                                       (g_pre_mlp, m_g_pre_mlp, v_g_pre_mlp), (g_post_mlp, m_g_post_mlp, v_g_post_mlp)],
                          nb + 1, "small_update")
    res = {n: r for n, r in zip(dense_names, dense)}
    res["c_ctx"] = tuple(a[0] for a in res["c_ctx"])
    small_names = ["q_norm", "k_norm", "sink"]
    small_w = [q_norm, k_norm, sink]
    small_gr = [grad_q_norm, grad_k_norm, grad_sink]
    small_m = [m_q_norm, m_k_norm, m_sink]
    small_v = [v_q_norm, v_k_norm, v_sink]
    sizes = [int(np.prod(w.shape)) for w in small_w]
    total = sum(sizes)
    flat_rows = 8 * ((total + 8 * D_MODEL - 1) // (8 * D_MODEL))

    def flat(arrs, fill):
        f = jnp.concatenate([a.reshape(-1) for a in arrs])
        return jnp.concatenate([f, jnp.full((flat_rows * D_MODEL - total,), fill, F32)]).reshape(flat_rows, D_MODEL)

    sd, snm, snv = _adamw(flat(small_w, 0.0), flat(small_gr, 0.0), flat(small_m, 0.0), flat(small_v, 1.0), "adamw_small")[:3]

    def unflat(f):
        f = f.reshape(-1)
        out, off = [], 0
        for w, n in zip(small_w, sizes):
            out.append(f[off:off + n].reshape(w.shape))
            off += n
        return out

    small_d, small_nm, small_nv = unflat(sd), unflat(snm), unflat(snv)
    res.update({n: (g, d, nm, nv) for n, g, d, nm, nv in zip(small_names, small_gr, small_d, small_nm, small_nv)})
    res["w_ada"] = (grad_w_ada, *step(w_ada, grad_w_ada, m_w_ada, v_w_ada, "adamw_w_ada"))
    res["w_up"] = shard_update("up", w_up, m_w_up, v_w_up, (h_mlp0, G_MLP0), (h_layer1, G_LAYER1))
    res["w_down"] = shard_update("down", w_down, m_w_down, v_w_down, (h_mlp0, G_MLP0), (h_layer1, G_LAYER1))
    res["w_in"] = shard_update("in", w_in, m_w_in, v_w_in, (h_mix0, G_MIX0), (h_layer1, G_LAYER1))
    res["w_out"] = shard_update("out", w_out, m_w_out, v_w_out, (h_mix0, G_MIX0), (h_layer1, G_LAYER1))

    order = ["c_ctx", "w_ada", "b_ada", "g_pre_mix", "g_post_mix", "g_pre_mlp", "g_post_mlp", "w_in", "q_norm", "k_norm", "sink", "w_out", "w_up", "w_down"]
    return (loss, grad_x, *[res[n][0] for n in order], *[res[n][1] for n in order],
            *[res[n][2] for n in order], *[res[n][3] for n in order])
```

```python
import functools

import jax
import jax.numpy as jnp
import numpy as np
from jax import lax
from jax.experimental import pallas as pl
from jax.experimental.pallas import tpu as pltpu

F32 = jnp.float32
BF16 = jnp.bfloat16

D_MODEL = 1024
HEAD_DIM = 64
GROUP = 4
WINDOW = 128
N_MOD = 6
D_FF = 4 * D_MODEL
IN_COLS = 1536
GRID_W = 64
ROPE_THETA = 10000.0
EPS = 1e-6
NEG_BIG = -1e30
Q_SCALE = HEAD_DIM ** -0.5
DEPTH = 2
N_DEV = 8

ADAM_LR = 0.001
ADAM_B1 = 0.9
ADAM_B2 = 0.999
ADAM_EPS = 1e-08
ADAM_WD = 0.01
ADAM_STEP = 10

V7X_VMEM_BYTES = 64 * 1024 * 1024
VMEM_LIMIT = V7X_VMEM_BYTES - 8 * 1024 * 1024

MESH = pl.DeviceIdType.MESH
NT = (((1,), (1,)), ((), ()))
TN = (((0,), (0,)), ((), ()))

COL_KA, COL_VA, COL_KB, COL_VB = 4, 5, 10, 11
NORMED_COLS = 640

PACK_HEIGHT = {"up": 512, "down": 512, "in": 256, "out": 128}
IN_PIECE_COLS = 384
PACK_OFF = {("up", 0): 0, ("down", 0): 512, ("in", 0): 1024, ("out", 0): 1280,
            ("up", 1): 1408, ("down", 1): 1920, ("in", 1): 2432, ("out", 1): 2688}
PACK_ROWS = 2816
W_FIRST, W_MLP0, W_OUT0, W_MLP1, W_MIX1 = (1024, 256), (0, 1024), (1280, 128), (1408, 1024), (2432, 384)
G_LAYER1, G_MLP0, G_MIX0 = (1408, 1408), (0, 1024), (1024, 384)


def _pick(n, cands):
    for t in cands:
        if n % t == 0:
            return t
    raise ValueError(f"no tile for {n}")


def _params(sem):
    return pltpu.CompilerParams(dimension_semantics=sem, vmem_limit_bytes=VMEM_LIMIT)


def _all_gather(x, name, in_hbm):
    m_per, n = x.shape
    space = pl.ANY if in_hbm else pltpu.VMEM

    def body(x_ref, out_ref, send_sems, recv_sems, local_sem):
        x_, y_, c_ = lax.axis_index("x"), lax.axis_index("y"), lax.axis_index("c")
        me, sibling = (x_, y_, c_), (x_, y_, 1 - c_)
        chips = [(1 - x_, y_), (x_, 1 - y_), (1 - x_, 1 - y_)]

        def rows(px, py, pc):
            return out_ref.at[pl.ds((4 * px + 2 * py + pc) * m_per, m_per), :]

        def copy(k, block, to, src=None):
            return pltpu.make_async_remote_copy(
                src_ref=rows(*block) if src is None else src, dst_ref=rows(*block),
                send_sem=send_sems.at[k], recv_sem=recv_sems.at[k], device_id=to, device_id_type=MESH)

        mine = pltpu.make_async_copy(x_ref, rows(*me), local_sem)
        mine.start()
        first = [copy(0, me, sibling, src=x_ref)]
        first += [copy(1 + j, me, (*chip, c_), src=x_ref) for j, chip in enumerate(chips)]
        for cp in first:
            cp.start()
        passed = [copy(4 + j, (*chip, c_), sibling) for j, chip in enumerate(chips)]
        for j, chip in enumerate(chips):
            copy(1 + j, (*chip, c_), me).wait_recv()
            passed[j].start()
        copy(0, sibling, me).wait_recv()
        for j, chip in enumerate(chips):
            copy(4 + j, (*chip, 1 - c_), me).wait_recv()
        for cp in first + passed:
            cp.wait_send()
        mine.wait()

    return pl.pallas_call(
        body, name=name,
        out_shape=jax.ShapeDtypeStruct((N_DEV * m_per, n), x.dtype),
        in_specs=[pl.BlockSpec(memory_space=space)],
        out_specs=pl.BlockSpec(memory_space=space),
        scratch_shapes=[pltpu.SemaphoreType.DMA((7,)), pltpu.SemaphoreType.DMA((7,)), pltpu.SemaphoreType.DMA],
    )(x)


class _Comm:
    def __init__(self, inputs, out_shapes, aliases, n_send, n_recv, start, finish, relay=None, lead=0):
        self.inputs, self.out_shapes, self.aliases = list(inputs), list(out_shapes), dict(aliases)
        self.n_send, self.n_recv, self.start, self.finish, self.relay, self.lead = n_send, n_recv, start, finish, relay, lead


def _comm_call(compute, comm, *, name, grid, in_specs, out_specs, out_shape, args, aliases, semantics, scratch=()):
    in_specs, out_specs, out_shape, args, aliases = list(in_specs), list(out_specs), list(out_shape), list(args), dict(aliases)
    scratch = list(scratch)
    if comm is None:
        return pl.pallas_call(compute, name=name, grid=grid, in_specs=in_specs, out_specs=out_specs, out_shape=out_shape,
                              input_output_aliases=aliases, scratch_shapes=scratch, compiler_params=_params(semantics))(*args)
    n_in, n_out, n_ci, n_co = len(args), len(out_shape), len(comm.inputs), len(comm.out_shapes)
    hbm = pl.BlockSpec(memory_space=pl.ANY)
    aliases.update({n_in + i: n_out + o for i, o in comm.aliases.items()})

    def body(*refs):
        ins, c_ins = refs[:n_in], refs[n_in:n_in + n_ci]
        outs, c_outs = refs[n_in + n_ci:n_in + n_ci + n_out], refs[n_in + n_ci + n_out:n_in + n_ci + n_out + n_co]
        scr = refs[n_in + n_ci + n_out + n_co:-2]
        send_sems, recv_sems = refs[-2:]
        ids = [pl.program_id(a) for a in range(len(grid))]
        first = functools.reduce(jnp.logical_and, [i == 0 for i in ids])
        last = functools.reduce(jnp.logical_and, [i == g - 1 for i, g in zip(ids, grid)])

        @pl.when(first)
        def _():
            comm.start(c_ins, c_outs, send_sems, recv_sems)

        compute(*ins, *outs, *scr)

        if comm.relay is not None:
            step = functools.reduce(lambda acc, ig: acc * ig[1] + ig[0], zip(ids, grid), 0)

            @pl.when(step == int(np.prod(grid)) - 1 - comm.lead)
            def _():
                comm.relay(c_ins, c_outs, send_sems, recv_sems)

        @pl.when(last)
        def _():
            comm.finish(c_ins, c_outs, send_sems, recv_sems)

    return pl.pallas_call(
        body, name=name, grid=grid,
        in_specs=in_specs + [hbm] * n_ci, out_specs=out_specs + [hbm] * n_co, out_shape=out_shape + comm.out_shapes,
        input_output_aliases=aliases,
        scratch_shapes=scratch + [pltpu.SemaphoreType.DMA((comm.n_send,)), pltpu.SemaphoreType.DMA((comm.n_recv,))],
        compiler_params=_params(("arbitrary",) * len(grid)),
    )(*args, *comm.inputs)


def _place():
    x_, y_, c_ = lax.axis_index("x"), lax.axis_index("y"), lax.axis_index("c")
    return x_, y_, c_, [(1 - x_, y_), (x_, 1 - y_), (1 - x_, 1 - y_)]


GATHER_SENDS, GATHER_RECVS = 8, 7


def _gather_copies(packed_ref, wg_ref, send_sems, recv_sems, rows, nth=0):
    r0, n = rows
    x_, y_, c_, chips = _place()
    me, sibling = (x_, y_, c_), (x_, y_, 1 - c_)
    src = packed_ref.at[pl.ds(r0, n), :]

    def slot(px, py, pc):
        return wg_ref.at[4 * px + 2 * py + pc]

    def copy(k, block, to, from_packed=False):
        return pltpu.make_async_remote_copy(src_ref=src if from_packed else slot(*block), dst_ref=slot(*block),
                                            send_sem=send_sems.at[GATHER_SENDS * nth + k], recv_sem=recv_sems.at[GATHER_RECVS * nth + k],
                                            device_id=to, device_id_type=MESH)

    own = [copy(0, me, sibling, True)] + [copy(1 + j, me, (*chip, c_), True) for j, chip in enumerate(chips)]
    passed = [copy(4 + j, (*chip, c_), sibling) for j, chip in enumerate(chips)]
    over_ici = [copy(1 + j, (*chip, c_), me) for j, chip in enumerate(chips)]
    from_sibling = [copy(0, sibling, me)] + [copy(4 + j, (*chip, 1 - c_), me) for j, chip in enumerate(chips)]
    mine = pltpu.make_async_copy(src, slot(*me), send_sems.at[GATHER_SENDS * nth + 7])
    return mine, own, passed, over_ici, from_sibling


def _gather_start(packed_ref, wg_ref, send_sems, recv_sems, rows, nth=0, copy_own=True):
    mine, own, _, _, _ = _gather_copies(packed_ref, wg_ref, send_sems, recv_sems, rows, nth)
    if copy_own:
        mine.start()
    for cp in own:
        cp.start()


def _gather_relay(packed_ref, wg_ref, send_sems, recv_sems, rows, nth=0):
    _, _, passed, over_ici, _ = _gather_copies(packed_ref, wg_ref, send_sems, recv_sems, rows, nth)
    for arrived, onward in zip(over_ici, passed):
        arrived.wait_recv()
        onward.start()


def _gather_finish(packed_ref, wg_ref, send_sems, recv_sems, rows, nth=0, copy_own=True):
    mine, own, passed, _, from_sibling = _gather_copies(packed_ref, wg_ref, send_sems, recv_sems, rows, nth)
    for arrived in from_sibling:
        arrived.wait_recv()
    for cp in own + passed:
        cp.wait_send()
    if copy_own:
        mine.wait()


def _gather_comm(packed, ranges, copy_own=True, lead=0):
    shapes = [jax.ShapeDtypeStruct((N_DEV, n, packed.shape[1]), packed.dtype) for _, n in ranges]

    def start(ins, outs, ss, rs):
        for nth, rows in enumerate(ranges):
            _gather_start(ins[0], outs[nth], ss, rs, rows, nth, copy_own)

    def relay(ins, outs, ss, rs):
        for nth, rows in enumerate(ranges):
            _gather_relay(ins[0], outs[nth], ss, rs, rows, nth)

    def finish(ins, outs, ss, rs):
        for nth, rows in enumerate(ranges):
            _gather_finish(ins[0], outs[nth], ss, rs, rows, nth, copy_own)

    return _Comm([packed], shapes, {}, GATHER_SENDS * len(ranges), GATHER_RECVS * len(ranges), start, finish, relay, lead)


def _pair_copy(p_ref, out_ref, send_sems, recv_sems):
    x_, y_, c_, _ = _place()
    return pltpu.make_async_remote_copy(src_ref=p_ref.at[1 - c_], dst_ref=out_ref,
                                        send_sem=send_sems.at[0], recv_sem=recv_sems.at[0],
                                        device_id=(x_, y_, 1 - c_), device_id_type=MESH)


def _pair_comm(p):
    return _Comm([p], [jax.ShapeDtypeStruct(p.shape[1:], p.dtype)], {}, 1, 1,
                 lambda ins, outs, ss, rs: _pair_copy(ins[0], outs[0], ss, rs).start(),
                 lambda ins, outs, ss, rs: _pair_copy(ins[0], outs[0], ss, rs).wait())


def _chip_copies(a_refs, out_refs, send_sems, recv_sems):
    _, _, c_, chips = _place()
    return [pltpu.make_async_remote_copy(src_ref=a_ref.at[2 * tx + ty], dst_ref=o_ref.at[j],
                                         send_sem=send_sems.at[3 * g + j], recv_sem=recv_sems.at[3 * g + j],
                                         device_id=(tx, ty, c_), device_id_type=MESH)
            for g, (a_ref, o_ref) in enumerate(zip(a_refs, out_refs)) for j, (tx, ty) in enumerate(chips)]


def _chip_start(a_refs, out_refs, send_sems, recv_sems):
    for cp in _chip_copies(a_refs, out_refs, send_sems, recv_sems):
        cp.start()


def _chip_finish(a_refs, out_refs, send_sems, recv_sems):
    for cp in _chip_copies(a_refs, out_refs, send_sems, recv_sems):
        cp.wait()


def _chip_comm(arrays):
    shapes = [jax.ShapeDtypeStruct((3,) + a.shape[1:], a.dtype) for a in arrays]
    return _Comm(arrays, shapes, {}, 3 * len(arrays), 3 * len(arrays), _chip_start, _chip_finish)


def _halves_copies(in_refs, out_refs, send_sems, recv_sems):
    x_, y_, c_, _ = _place()
    return [pltpu.make_async_remote_copy(src_ref=o_ref.at[c_], dst_ref=o_ref.at[c_], send_sem=send_sems.at[i], recv_sem=recv_sems.at[i],
                                         device_id=(x_, y_, 1 - c_), device_id_type=MESH)
            for i, o_ref in enumerate(out_refs)]


def _halves_start(in_refs, out_refs, send_sems, recv_sems):
    for cp in _halves_copies(in_refs, out_refs, send_sems, recv_sems):
        cp.start()


def _halves_finish(in_refs, out_refs, send_sems, recv_sems):
    for cp in _halves_copies(in_refs, out_refs, send_sems, recv_sems):
        cp.wait()


def _halves_comm(arrays):
    shapes = [jax.ShapeDtypeStruct(a.shape, a.dtype) for a in arrays]
    return _Comm(arrays, shapes, {i: i for i in range(len(arrays))}, len(arrays), len(arrays), _halves_start, _halves_finish)


class _SemSlice:
    class _At:
        def __init__(self, sems, first):
            self.sems, self.first = sems, first

        def __getitem__(self, k):
            return self.sems.at[self.first + k]

    def __init__(self, sems, first):
        self.at = _SemSlice._At(sems, first)


def _merge(comms):
    inputs = [a for c in comms for a in c.inputs]
    shapes = [s for c in comms for s in c.out_shapes]
    aliases, spans = {}, []
    i0 = o0 = s0 = r0 = 0
    for c in comms:
        aliases.update({i0 + i: o0 + o for i, o in c.aliases.items()})
        spans.append((slice(i0, i0 + len(c.inputs)), slice(o0, o0 + len(c.out_shapes)), s0, r0))
        i0, o0, s0, r0 = i0 + len(c.inputs), o0 + len(c.out_shapes), s0 + c.n_send, r0 + c.n_recv

    def start(ins, outs, ss, rs):
        for c, (i, o, s, r) in zip(comms, spans):
            c.start(ins[i], outs[o], _SemSlice(ss, s), _SemSlice(rs, r))

    def finish(ins, outs, ss, rs):
        for c, (i, o, s, r) in zip(comms, spans):
            if c.relay is not None:
                c.relay(ins[i], outs[o], _SemSlice(ss, s), _SemSlice(rs, r))
            c.finish(ins[i], outs[o], _SemSlice(ss, s), _SemSlice(rs, r))

    return _Comm(inputs, shapes, aliases, s0, r0, start, finish)


def _comm_alone(comm, name):
    n_ci = len(comm.inputs)
    hbm = pl.BlockSpec(memory_space=pl.ANY)

    def body(*refs):
        c_ins, c_outs, send_sems, recv_sems = refs[:n_ci], refs[n_ci:-2], refs[-2], refs[-1]
        comm.start(c_ins, c_outs, send_sems, recv_sems)
        if comm.relay is not None:
            comm.relay(c_ins, c_outs, send_sems, recv_sems)
        comm.finish(c_ins, c_outs, send_sems, recv_sems)

    return pl.pallas_call(
        body, name=name, out_shape=comm.out_shapes, in_specs=[hbm] * n_ci, out_specs=[hbm] * len(comm.out_shapes),
        input_output_aliases=comm.aliases,
        scratch_shapes=[pltpu.SemaphoreType.DMA((comm.n_send,)), pltpu.SemaphoreType.DMA((comm.n_recv,))],
    )(*comm.inputs)


SUM_TILES = (704, 512, 384, 320, 256, 192, 128, 64)


def _pair_sum(p, r1, kc_idx, name):
    _, _, n, c = p.shape
    tr = _pick(n, SUM_TILES)

    def body(s_ref, p_ref, r_ref, o32_ref, o16_ref):
        v = p_ref[...] + r_ref[...]
        o16_ref[...] = v.astype(BF16)

        @pl.when(pl.program_id(1) == s_ref[0])
        def _():
            o32_ref[...] = v

    blk = pl.BlockSpec((None, tr, c), lambda i, j, s: (j, i, 0))
    grid_spec = pltpu.PrefetchScalarGridSpec(
        num_scalar_prefetch=1, grid=(n // tr, 4),
        in_specs=[pl.BlockSpec((None, None, tr, c), lambda i, j, s: (s[1], j, i, 0)), blk],
        out_specs=[pl.BlockSpec((tr, c), lambda i, j, s: (i, 0)), blk])
    return pl.pallas_call(
        body, name=name, grid_spec=grid_spec,
        out_shape=[jax.ShapeDtypeStruct((n, c), F32), jax.ShapeDtypeStruct((4, n, c), BF16)],
        compiler_params=_params(("arbitrary", "arbitrary")),
    )(kc_idx, p, r1)


def _owner_sum(a32, r2, kc_idx, name):
    r, c = a32.shape
    tr = _pick(r, SUM_TILES)

    def body(s_ref, a_ref, r_ref, o_ref):
        v = a_ref[...]
        for j in range(3):
            v = v + r_ref[j].astype(F32)
        o_ref[...] = v

    grid_spec = pltpu.PrefetchScalarGridSpec(
        num_scalar_prefetch=1, grid=(r // tr,),
        in_specs=[pl.BlockSpec((tr, c), lambda i, s: (i, 0)),
                  pl.BlockSpec((3, tr, c), lambda i, s: (0, i, 0))],
        out_specs=pl.BlockSpec((None, tr, c), lambda i, s: (s[1], i, 0)))
    return pl.pallas_call(
        body, name=name, grid_spec=grid_spec,
        out_shape=jax.ShapeDtypeStruct((2, r, c), F32),
        compiler_params=_params(("arbitrary",)),
    )(kc_idx, a32, r2)


def _pack_local_half(w_in_s, w_out_s, w_up_s, w_down_s, c_idx):
    parts, row = [], 0
    for (kind, l), off in sorted(PACK_OFF.items(), key=lambda kv: kv[1]):
        if off > row:
            parts.append(jnp.zeros((off - row, 1024), BF16))
        if kind == "up":
            p = lax.dynamic_slice_in_dim(w_up_s[l], c_idx * 512, 512, 0)
        elif kind == "down":
            p = lax.dynamic_slice_in_dim(w_down_s[l], c_idx * 512, 512, 0)
        elif kind == "in":
            p = lax.dynamic_slice_in_dim(w_in_s[l], c_idx * 512, 512, 0)
            p = p.reshape(2, 256, IN_PIECE_COLS).transpose(1, 0, 2).reshape(256, 2 * IN_PIECE_COLS)
            p = jnp.pad(p, ((0, 0), (0, 1024 - 2 * IN_PIECE_COLS)))
        else:
            p = lax.dynamic_slice_in_dim(w_out_s[l], c_idx * 128, 128, 0)
        parts.append(p.astype(BF16))
        row = off + PACK_HEIGHT[kind]
    return jnp.concatenate(parts, axis=0)


def _unpack_in_pieces(w_ref, own_ref, w_scr):
    if own_ref is not None:
        me = 4 * lax.axis_index("x") + 2 * lax.axis_index("y") + lax.axis_index("c")
    for d in range(N_DEV):
        k, c = d // 2, d % 2
        for t in range(2):
            piece = w_ref[d, :, t * IN_PIECE_COLS:(t + 1) * IN_PIECE_COLS]
            if own_ref is not None:
                piece = jnp.where(me == d, own_ref[:, t * IN_PIECE_COLS:(t + 1) * IN_PIECE_COLS], piece)
            w_scr[c * 512 + t * 256:c * 512 + (t + 1) * 256, k * IN_PIECE_COLS:(k + 1) * IN_PIECE_COLS] = piece


def _in_weight_operands(wg):
    specs, args = [_gathered_spec(wg, "in")], [wg["in"][0]]
    if "in_own" in wg:
        own, off = wg["in_own"]
        h = PACK_HEIGHT["in"]
        assert off % h == 0
        specs.append(pl.BlockSpec((h, 1024), lambda *_: (off // h, 0), pipeline_mode=pl.Buffered(1)))
        args.append(own)
    return specs, args


class _Rows:
    def __init__(self, nb, seq, ctx):
        self.nb, self.seq, self.ctx = nb, seq, ctx
        self.n_lat, self.n_ctx = nb * seq, nb * ctx
        self.rows = self.n_lat + self.n_ctx
        self.tm = _pick(np.gcd(seq, self.n_ctx), (512, 256, 128))
        self.tiles_per_ex = seq // self.tm
        self.n_tiles = self.rows // self.tm
        self.n_lat_tiles = self.n_lat // self.tm
        self.groups = nb + 1

    def latent_only(self):
        rt = _Rows(self.nb, self.seq, self.ctx)
        rt.n_tiles = self.n_lat_tiles
        return rt

    def group(self, i):
        return jnp.minimum(i // self.tiles_per_ex, self.nb)

    def first_of_group(self, i):
        return jnp.logical_and(i % self.tiles_per_ex == 0, i <= self.n_lat_tiles)


def _mod_spec(rt):
    return pl.BlockSpec((1, N_MOD, D_MODEL), lambda i: (rt.group(i), 0, 0))


def _row_spec(rt, cols):
    return pl.BlockSpec((rt.tm, cols), lambda i: (i, 0))


def _vec_spec(cols):
    return pl.BlockSpec((1, cols), lambda i: (0, 0))


def _group_spec(rt):
    return pl.BlockSpec((1, 1, D_MODEL), lambda i: (rt.group(i), 0, 0))


def _gathered_spec(wg, kind):
    h, off = PACK_HEIGHT[kind], wg[kind][1]
    assert off % h == 0, (kind, off)
    return pl.BlockSpec((N_DEV, h, 1024), lambda *_: (0, off // h, 0), pipeline_mode=pl.Buffered(1))


def _group_shape(rt):
    return jax.ShapeDtypeStruct((rt.groups, 1, D_MODEL), F32)


def _vec_shape(cols=D_MODEL):
    return jax.ShapeDtypeStruct((1, cols), F32)


def _rms_inv(v):
    return lax.rsqrt(jnp.mean(v * v, axis=-1, keepdims=True) + EPS)


def _norm_mod_val(h_, g_, mod_ref, i_shift, i_scale):
    n = h_ * _rms_inv(h_) * g_
    return n * (1.0 + mod_ref[0, i_scale:i_scale + 1, :]) + mod_ref[0, i_shift:i_shift + 1, :]


def _post_norm_val(h_, z_, g_, mod_ref, i_gate):
    return h_ + mod_ref[0, i_gate:i_gate + 1, :] * (z_ * _rms_inv(z_) * g_)


def _post_norm_bwd_val(dh_, z_, g_, gate):
    rinv = _rms_inv(z_)
    n0 = z_ * rinv
    dn = dh_ * gate * g_
    dz = rinv * (dn - n0 * jnp.mean(dn * n0, axis=-1, keepdims=True))
    return dz, jnp.sum(dh_ * n0 * g_, axis=0, keepdims=True), jnp.sum(dh_ * gate * n0, axis=0, keepdims=True)


def _norm_mod_bwd_val(du_, h_, g_, one_sc):
    rinv = _rms_inv(h_)
    n0 = h_ * rinv
    dn = du_ * g_ * one_sc
    dh = rinv * (dn - n0 * jnp.mean(dn * n0, axis=-1, keepdims=True))
    return (dh, jnp.sum(du_, axis=0, keepdims=True), jnp.sum(du_ * n0 * g_, axis=0, keepdims=True),
            jnp.sum(du_ * one_sc * n0, axis=0, keepdims=True))


def _accumulate(rt, i, group_pairs, global_pairs):
    @pl.when(rt.first_of_group(i))
    def _():
        for ref, _ in group_pairs:
            ref[...] = jnp.zeros_like(ref)

    @pl.when(i == 0)
    def _():
        for ref, _ in global_pairs:
            ref[...] = jnp.zeros_like(ref)

    for ref, val in group_pairs:
        ref[0] += val
    for ref, val in global_pairs:
        ref[...] += val


def _rope_tables(rt):
    pos = np.arange(rt.seq)
    axis_dim = HEAD_DIM // 2
    inv = (ROPE_THETA ** (-np.arange(0, axis_dim, 2, dtype=np.float32) / axis_dim)).astype(np.float32)
    ang_r = (pos // GRID_W).astype(np.float32)[:, None] * inv[None, :]
    ang_c = (pos % GRID_W).astype(np.float32)[:, None] * inv[None, :]
    cr, sr, cc, sc = np.cos(ang_r), np.sin(ang_r), np.cos(ang_c), np.sin(ang_c)
    zero = np.zeros_like(sr)
    cos = np.concatenate([cr, cr, cc, cc], axis=1)
    s_lo = np.concatenate([zero, sr, zero, sc], axis=1)
    s_hi = np.concatenate([-sr, zero, -sc, zero], axis=1)

    def full(t, ctx_value):
        return jnp.asarray(np.concatenate([np.tile(t, (1, 2)), np.full((rt.tm, 128), ctx_value)], axis=0), F32)

    return full(cos, 1.0), full(s_lo, 0.0), full(s_hi, 0.0)


def _table_spec(rt):
    return pl.BlockSpec((rt.tm, 128), lambda i: (jnp.where(i < rt.n_lat_tiles, i % rt.tiles_per_ex, rt.tiles_per_ex), 0))


def _head_mean(x):
    r = lax.broadcasted_iota(jnp.int32, (128, 128), 0) // HEAD_DIM
    c = lax.broadcasted_iota(jnp.int32, (128, 128), 1) // HEAD_DIM
    ones = jnp.where(r == c, 1.0 / HEAD_DIM, 0.0).astype(F32)
    return jnp.dot(x, ones, preferred_element_type=F32, precision=lax.Precision.HIGH)


def _head_stats(t):
    return lax.rsqrt(_head_mean(t * t) + EPS)


def _prep_fwd_body(tm, qkv_ref, c, s1, s2, qn, kn, out_ref):
    def rope(t):
        return t * c + pltpu.roll(t, 16, 1) * s1 + pltpu.roll(t, 112, 1) * s2

    for j in range(12):
        t = qkv_ref[:, j * 128:(j + 1) * 128]
        if j < 4:
            t = rope(t * _head_stats(t) * qn) * Q_SCALE
        elif j == COL_KA:
            t = rope(t * _head_stats(t) * kn)
        elif 6 <= j < 10:
            t = rope(t) * Q_SCALE
        elif j == COL_KB:
            t = rope(t)
        out_ref[:, j * 128:(j + 1) * 128] = t.astype(BF16)


def _prep_bwd_body(dq, dkv, qkv_ref, c, s1, s2, qn, kn, out_ref):
    rows = slice(None)

    def rope_bwd(d):
        return d * c + pltpu.roll(d * s1, 112, 1) + pltpu.roll(d * s2, 16, 1)

    def norm_bwd(t, g, dy):
        rinv = _head_stats(t)
        n = t * rinv
        dn = dy * g
        return rinv * (dn - n * _head_mean(dn * n)), jnp.sum(dy * n, axis=0, keepdims=True)

    dqn = jnp.zeros((1, 128), F32)
    dkn = jnp.zeros((1, 128), F32)
    for j in range(12):
        if j < 4:
            d, dg = norm_bwd(qkv_ref[rows, j * 128:(j + 1) * 128], qn, rope_bwd(dq(slice(j * 128, (j + 1) * 128)) * Q_SCALE))
            dqn = dqn + dg
        elif j == COL_KA:
            d, dg = norm_bwd(qkv_ref[rows, j * 128:(j + 1) * 128], kn, rope_bwd(dkv(slice(0, 128))))
            dkn = dkn + dg
        elif j == COL_VA:
            d = dkv(slice(128, 256))
        elif j < 10:
            d = rope_bwd(dq(slice((j - 2) * 128, (j - 1) * 128)) * Q_SCALE)
        elif j == COL_KB:
            d = rope_bwd(dkv(slice(256, 384)))
        else:
            d = dkv(slice(384, 512))
        out_ref[rows, j * 128:(j + 1) * 128] = d.astype(BF16)
    return dqn, dkn


def _in_fwd(rt, h, gamma, mod, wg, tables, qn, kn, name):
    w_specs, w_args = _in_weight_operands(wg)
    n_w = len(w_args)

    def body(h_ref, g_ref, mod_ref, *rest):
        c_ref, s1_ref, s2_ref, qn_ref, kn_ref, u_ref, qkn_ref, qkvp_ref, qkv_ref, w_scr = rest[n_w:]

        @pl.when(pl.program_id(0) == 0)
        def _():
            _unpack_in_pieces(rest[0], rest[1] if n_w == 2 else None, w_scr)

        u = _norm_mod_val(h_ref[...], g_ref[...], mod_ref, 0, 1).astype(BF16)
        u_ref[...] = u
        qkv_ref[...] = jnp.dot(u, w_scr[...], preferred_element_type=F32)
        qkn_ref[...] = qkv_ref[:, 0:NORMED_COLS]
        _prep_fwd_body(rt.tm, qkv_ref, c_ref[...], s1_ref[...], s2_ref[...], qn_ref[...], kn_ref[...], qkvp_ref)

    return pl.pallas_call(
        body, name=name, grid=(rt.n_tiles,),
        in_specs=[_row_spec(rt, D_MODEL), _vec_spec(D_MODEL), _mod_spec(rt)] + w_specs + [_table_spec(rt)] * 3 + [_vec_spec(128)] * 2,
        out_specs=[_row_spec(rt, D_MODEL), _row_spec(rt, NORMED_COLS), _row_spec(rt, IN_COLS)],
        out_shape=[jax.ShapeDtypeStruct((rt.rows, D_MODEL), BF16), jax.ShapeDtypeStruct((rt.rows, NORMED_COLS), F32),
                   jax.ShapeDtypeStruct((rt.rows, IN_COLS), BF16)],
        scratch_shapes=[pltpu.VMEM((rt.tm, IN_COLS), F32), pltpu.VMEM((D_MODEL, IN_COLS), BF16)],
        compiler_params=_params(("arbitrary",)),
    )(h, gamma, mod, *w_args, *tables, qn, kn)


def _in_bwd(rt, dq, dkv, qkv, tables, qn, kn, wg, h, dres, mod, gamma, latent_only, name, comm=None, dead_ctx_dkv=None):
    last = rt.n_lat_tiles - 1
    w_specs, w_args = _in_weight_operands(wg)
    n_w = len(w_args)
    n_dead = 0 if dead_ctx_dkv is None else 1

    def body(dq_ref, dkv_ref, qkv_ref, c_ref, s1_ref, s2_ref, qn_ref, kn_ref, *rest):
        h_ref, dres_ref, mod_ref, g_ref, dqkv_ref, dh_ref, dqn_ref, dkn_ref, dsh_ref, dsc_ref, dg_ref, w_scr = rest[n_w + n_dead:]
        i = pl.program_id(0)

        @pl.when(i == 0)
        def _():
            _unpack_in_pieces(rest[0], rest[1] if n_w == 2 else None, w_scr)

        if n_dead:
            c1_ref, lat = rest[n_w], i <= last
            load_dq = lambda cols: jnp.where(lat, dq_ref[:, cols], 0.0)
            load_dkv = lambda cols: jnp.where(lat, dkv_ref[:, cols], c1_ref[:, cols])
            dres_ = jnp.where(lat, dres_ref[...], 0.0)
        else:
            load_dq, load_dkv, dres_ = (lambda cols: dq_ref[:, cols]), (lambda cols: dkv_ref[:, cols]), dres_ref[...]
        dqn, dkn = _prep_bwd_body(load_dq, load_dkv, qkv_ref, c_ref[...], s1_ref[...], s2_ref[...], qn_ref[...], kn_ref[...], dqkv_ref)
        du = lax.dot_general(dqkv_ref[...], w_scr[...], NT, preferred_element_type=F32)
        dh, dsh, dsc, dg = _norm_mod_bwd_val(du, h_ref[...], g_ref[...], 1.0 + mod_ref[0, 1:2, :])
        if latent_only:
            @pl.when(i <= last)
            def _():
                dh_ref[...] = dres_ + dh
        else:
            dh_ref[...] = dres_ + dh
        _accumulate(rt, i, [(dsh_ref, dsh), (dsc_ref, dsc)], [(dg_ref, dg), (dqn_ref, dqn), (dkn_ref, dkn)])

    dh_spec = pl.BlockSpec((rt.tm, D_MODEL), lambda i: (jnp.minimum(i, last), 0)) if latent_only else _row_spec(rt, D_MODEL)
    dead_specs = [] if dead_ctx_dkv is None else [pl.BlockSpec((rt.tm, 512), lambda i: (jnp.maximum(i - rt.n_lat_tiles, 0), 0))]
    dead_args = [] if dead_ctx_dkv is None else [dead_ctx_dkv]
    return _comm_call(
        body, comm, name=name, grid=(rt.n_tiles,),
        in_specs=[_row_spec(rt, 1024), _row_spec(rt, 512), _row_spec(rt, NORMED_COLS)] + [_table_spec(rt)] * 3 + [_vec_spec(128)] * 2
        + w_specs + dead_specs + [_row_spec(rt, D_MODEL), _row_spec(rt, D_MODEL), _mod_spec(rt), _vec_spec(D_MODEL)],
        out_specs=[_row_spec(rt, IN_COLS), dh_spec, _vec_spec(128), _vec_spec(128),
                   _group_spec(rt), _group_spec(rt), _vec_spec(D_MODEL)],
        out_shape=[jax.ShapeDtypeStruct((rt.rows, IN_COLS), BF16),
                   jax.ShapeDtypeStruct((rt.n_lat if latent_only else rt.rows, D_MODEL), F32),
                   _vec_shape(128), _vec_shape(128), _group_shape(rt), _group_shape(rt), _vec_shape()],
        args=[dq, dkv, qkv, *tables, qn, kn, *w_args, *dead_args, h, dres, mod, gamma], aliases={}, semantics=("arbitrary",),
        scratch=[pltpu.VMEM((D_MODEL, IN_COLS), BF16)])


def _out_fwd(rt, o, wg, h, mod, g_post_mix, g_pre_mlp, name):
    def body(o_ref, w_ref, h_ref, mod_ref, gpost_ref, gpre_ref, mix_ref, h1_ref, u2_ref):
        mix = jnp.dot(o_ref[...], w_ref[...].reshape(D_MODEL, D_MODEL), preferred_element_type=F32)
        mix_ref[...] = mix
        h1 = _post_norm_val(h_ref[...], mix, gpost_ref[...], mod_ref, 2)
        h1_ref[...] = h1
        u2_ref[...] = _norm_mod_val(h1, gpre_ref[...], mod_ref, 3, 4).astype(BF16)

    return pl.pallas_call(
        body, name=name, grid=(rt.n_tiles,),
        in_specs=[_row_spec(rt, D_MODEL), _gathered_spec(wg, "out"), _row_spec(rt, D_MODEL), _mod_spec(rt),
                  _vec_spec(D_MODEL), _vec_spec(D_MODEL)],
        out_specs=[_row_spec(rt, D_MODEL)] * 3,
        out_shape=[jax.ShapeDtypeStruct((rt.rows, D_MODEL), F32), jax.ShapeDtypeStruct((rt.rows, D_MODEL), F32),
                   jax.ShapeDtypeStruct((rt.rows, D_MODEL), BF16)],
        compiler_params=_params(("parallel",)),
    )(o, wg["out"][0], h, mod, g_post_mix, g_pre_mlp)


def _out_bwd(rt, dh1, mix, wg, mod, g_post_mix, name, comm=None):
    def body(dh_ref, mix_ref, w_ref, mod_ref, g_ref, dmix_ref, do_ref, dgate_ref, dg_ref):
        i = pl.program_id(0)
        dz, dgate, dg = _post_norm_bwd_val(dh_ref[...], mix_ref[...], g_ref[...], mod_ref[0, 2:3, :])
        dzb = dz.astype(BF16)
        dmix_ref[...] = dzb
        do_ref[...] = lax.dot_general(dzb, w_ref[...].reshape(D_MODEL, D_MODEL), NT, preferred_element_type=F32).astype(BF16)
        _accumulate(rt, i, [(dgate_ref, dgate)], [(dg_ref, dg)])

    return _comm_call(
        body, comm, name=name, grid=(rt.n_tiles,),
        in_specs=[_row_spec(rt, D_MODEL), _row_spec(rt, D_MODEL), _gathered_spec(wg, "out"), _mod_spec(rt), _vec_spec(D_MODEL)],
        out_specs=[_row_spec(rt, D_MODEL), _row_spec(rt, D_MODEL), _group_spec(rt), _vec_spec(D_MODEL)],
        out_shape=[jax.ShapeDtypeStruct((rt.rows, D_MODEL), BF16), jax.ShapeDtypeStruct((rt.rows, D_MODEL), BF16),
                   _group_shape(rt), _vec_shape()],
        args=[dh1, mix, wg["out"][0], mod, g_post_mix], aliases={}, semantics=("arbitrary",))


def _w_chunk(w_ref, k):
    return w_ref[2 * k:2 * k + 2].reshape(1024, 1024)


def _mlp_fwd(rt, u2, h1, wg, mod, g_post_mlp, name, comm=None, target=None):
    last = rt.n_lat_tiles - 1

    def body(u2_ref, h1_ref, wu_ref, wd_ref, mod_ref, g_ref, *rest):
        u2_ = u2_ref[...]
        y = jnp.zeros((rt.tm, D_MODEL), F32)
        for k in range(D_FF // 1024):
            a = jnp.maximum(jnp.dot(u2_, _w_chunk(wu_ref, k), preferred_element_type=F32), 0.0)
            rest[-3 if target is None else -4][:, k * 1024:(k + 1) * 1024] = a.astype(BF16)
            y = y + jnp.dot((a * a).astype(BF16), _w_chunk(wd_ref, k), preferred_element_type=F32)
        h2 = _post_norm_val(h1_ref[...], y, g_ref[...], mod_ref, 5)
        if target is None:
            _, y_ref, h2_ref = rest
            y_ref[...] = y
            h2_ref[...] = h2
        else:
            t_ref, _, y_ref, dh_ref, sq_ref = rest
            y_ref[...] = y
            i = pl.program_id(0)

            @pl.when(i == 0)
            def _():
                sq_ref[...] = jnp.zeros_like(sq_ref)

            @pl.when(i <= last)
            def _():
                e = h2 - t_ref[...]
                dh_ref[...] = e * (1.0 / D_MODEL)
                sq_ref[...] += jnp.sum(e * e, axis=0, keepdims=True)

            @pl.when(i > last)
            def _():
                dh_ref[...] = jnp.zeros_like(dh_ref)

    in_specs = [_row_spec(rt, D_MODEL), _row_spec(rt, D_MODEL), _gathered_spec(wg, "up"), _gathered_spec(wg, "down"),
                _mod_spec(rt), _vec_spec(D_MODEL)]
    args = [u2, h1, wg["up"][0], wg["down"][0], mod, g_post_mlp]
    out_specs = [_row_spec(rt, D_FF), _row_spec(rt, D_MODEL), _row_spec(rt, D_MODEL)]
    out_shape = [jax.ShapeDtypeStruct((rt.rows, D_FF), BF16), jax.ShapeDtypeStruct((rt.rows, D_MODEL), F32),
                 jax.ShapeDtypeStruct((rt.rows, D_MODEL), F32)]
    if target is not None:
        in_specs.append(pl.BlockSpec((rt.tm, D_MODEL), lambda i: (jnp.minimum(i, last), 0)))
        args.append(target)
        out_specs.append(_vec_spec(D_MODEL))
        out_shape.append(_vec_shape())
    return _comm_call(body, comm, name=name, grid=(rt.n_tiles,), in_specs=in_specs, out_specs=out_specs, out_shape=out_shape,
                      args=args, aliases={}, semantics=("parallel",) if target is None else ("arbitrary",))


def _mlp_down_bwd(rt, dh, y, ra, wg, mod, g_post_mlp, name, comm=None):
    def body(dh_ref, y_ref, ra_ref, wd_ref, mod_ref, g_ref, dy_ref, da_ref, dgate_ref, dg_ref):
        i = pl.program_id(0)
        dz, dgate, dg = _post_norm_bwd_val(dh_ref[...], y_ref[...], g_ref[...], mod_ref[0, 5:6, :])
        dyb = dz.astype(BF16)
        dy_ref[...] = dyb
        for k in range(D_FF // 1024):
            dr = lax.dot_general(dyb, _w_chunk(wd_ref, k), NT, preferred_element_type=F32)
            da_ref[:, k * 1024:(k + 1) * 1024] = (dr * (2.0 * ra_ref[:, k * 1024:(k + 1) * 1024].astype(F32))).astype(BF16)
        _accumulate(rt, i, [(dgate_ref, dgate)], [(dg_ref, dg)])

    return _comm_call(
        body, comm, name=name, grid=(rt.n_tiles,),
        in_specs=[_row_spec(rt, D_MODEL), _row_spec(rt, D_MODEL), _row_spec(rt, D_FF), _gathered_spec(wg, "down"),
                  _mod_spec(rt), _vec_spec(D_MODEL)],
        out_specs=[_row_spec(rt, D_MODEL), _row_spec(rt, D_FF), _group_spec(rt), _vec_spec(D_MODEL)],
        out_shape=[jax.ShapeDtypeStruct((rt.rows, D_MODEL), BF16), jax.ShapeDtypeStruct((rt.rows, D_FF), BF16),
                   _group_shape(rt), _vec_shape()],
        args=[dh, y, ra, wg["down"][0], mod, g_post_mlp], aliases={}, semantics=("arbitrary",))


def _mlp_up_bwd(rt, da, wg, h1, dh, mod, g_pre_mlp, name):
    def body(da_ref, wu_ref, h1_ref, dh_ref, mod_ref, g_ref, dh1_ref, dsh_ref, dsc_ref, dg_ref):
        i = pl.program_id(0)
        du = jnp.zeros((rt.tm, D_MODEL), F32)
        for k in range(D_FF // 1024):
            du = du + lax.dot_general(da_ref[:, k * 1024:(k + 1) * 1024], _w_chunk(wu_ref, k), NT, preferred_element_type=F32)
        d, dsh, dsc, dg = _norm_mod_bwd_val(du, h1_ref[...], g_ref[...], 1.0 + mod_ref[0, 4:5, :])
        dh1_ref[...] = dh_ref[...] + d
        _accumulate(rt, i, [(dsh_ref, dsh), (dsc_ref, dsc)], [(dg_ref, dg)])

    return pl.pallas_call(
        body, name=name, grid=(rt.n_tiles,),
        in_specs=[_row_spec(rt, D_FF), _gathered_spec(wg, "up"), _row_spec(rt, D_MODEL), _row_spec(rt, D_MODEL),
                  _mod_spec(rt), _vec_spec(D_MODEL)],
        out_specs=[_row_spec(rt, D_MODEL), _group_spec(rt), _group_spec(rt), _vec_spec(D_MODEL)],
        out_shape=[jax.ShapeDtypeStruct((rt.rows, D_MODEL), F32), _group_shape(rt), _group_shape(rt), _vec_shape()],
        compiler_params=_params(("arbitrary",)),
    )(da, wg["up"][0], h1, dh, mod, g_pre_mlp)


def _wgrad_packed(rt, a, b, kind, off, n_rows, p_prev, name, comm=None):
    h = PACK_HEIGHT[kind]
    tk = rt.tm
    assert off % h == 0, (kind, off)

    def body(a_ref, b_ref, *rest):
        o_ref = rest[-1]
        i = pl.program_id(0)

        @pl.when(i == 0)
        def _():
            o_ref[...] = jnp.zeros_like(o_ref)

        if kind == "in":
            res = lax.dot_general(a_ref[...], b_ref[...], TN, preferred_element_type=F32)
            for k in range(4):
                for c in range(2):
                    for t in range(2):
                        o_ref[c, k, :, t * IN_PIECE_COLS:(t + 1) * IN_PIECE_COLS] += \
                            res[c * 512 + t * h:c * 512 + (t + 1) * h, k * IN_PIECE_COLS:(k + 1) * IN_PIECE_COLS]
        elif kind == "out":
            res = lax.dot_general(a_ref[...], b_ref[...], TN, preferred_element_type=F32)
            for k in range(4):
                for c in range(2):
                    o_ref[c, k] += res[(2 * k + c) * h:(2 * k + c + 1) * h]
        else:
            for k in range(4):
                if kind == "up":
                    res = lax.dot_general(a_ref[...], b_ref[:, k * 1024:(k + 1) * 1024], TN, preferred_element_type=F32)
                else:
                    ra = a_ref[:, k * 1024:(k + 1) * 1024].astype(F32)
                    res = lax.dot_general((ra * ra).astype(BF16), b_ref[...], TN, preferred_element_type=F32)
                o_ref[0, k] += res[0:h]
                o_ref[1, k] += res[h:2 * h]

    in_specs = [pl.BlockSpec((tk, a.shape[1]), lambda i: (i, 0)), pl.BlockSpec((tk, b.shape[1]), lambda i: (i, 0))]
    args = [a, b]
    aliases = {}
    if p_prev is not None:
        in_specs.append(pl.BlockSpec(memory_space=pl.ANY))
        args.append(p_prev)
        aliases = {2: 0}
    outs = _comm_call(
        body, comm, name=name, grid=(rt.n_tiles,),
        in_specs=in_specs,
        out_specs=[pl.BlockSpec((2, 4, h, 1024), lambda i: (0, 0, off // h, 0))],
        out_shape=[jax.ShapeDtypeStruct((2, 4, n_rows, 1024), F32)],
        args=args, aliases=aliases, semantics=("arbitrary",))
    return outs[0] if comm is None else outs


def _ada_wgrad(xs, dm, name):
    depth, _, cols = dm.shape

    def body(x_ref, d_ref, o_ref):
        for l in range(depth):
            o_ref[l] = lax.dot_general(x_ref[...], d_ref[l], TN, preferred_element_type=F32)

    return pl.pallas_call(body, name=name, out_shape=jax.ShapeDtypeStruct((depth, xs.shape[1], cols), F32),
                          compiler_params=pltpu.CompilerParams(vmem_limit_bytes=VMEM_LIMIT))(xs, dm)


def _stack_heads(x, kvi):
    x = x.astype(F32)
    tq = x.shape[0]
    lane = lax.broadcasted_iota(jnp.int32, (tq, 128), 1)
    keep = lane < HEAD_DIM if kvi == 0 else lane >= HEAD_DIM
    parts = []
    for p in range(2):
        pair = x[:, p * 128:(p + 1) * 128]
        swapped = pltpu.roll(pair, HEAD_DIM, 1)
        lo_head, hi_head = (pair, swapped) if kvi == 0 else (swapped, pair)
        parts += [jnp.where(keep, lo_head, 0.0), jnp.where(keep, hi_head, 0.0)]
    return jnp.concatenate(parts, axis=0).astype(BF16)


def _unstack_heads(o4, kvi):
    tq = o4.shape[0] // GROUP
    lane = lax.broadcasted_iota(jnp.int32, (tq, 128), 1)
    outs = []
    for p in range(2):
        r_lo, r_hi = o4[(2 * p) * tq:(2 * p + 1) * tq], o4[(2 * p + 1) * tq:(2 * p + 2) * tq]
        if kvi == 0:
            lo, hi = r_lo, pltpu.roll(r_hi, HEAD_DIM, 1)
        else:
            lo, hi = pltpu.roll(r_lo, HEAD_DIM, 1), r_hi
        outs.append(jnp.where(lane < HEAD_DIM, lo, hi))
    return jnp.concatenate(outs, axis=1)


def _per_head(shape, axis, tq, values):
    head = lax.broadcasted_iota(jnp.int32, shape, axis) // tq
    out = jnp.zeros(shape, F32)
    for g in range(GROUP):
        out = jnp.where(head == g, values[g], out)
    return out


KEY_CHUNK = 512
Q_TILE = 128
Q_TILE_FWD = 256


def _key_chunks(k_ref, v_ref, n, kc=KEY_CHUNK):
    kc = min(kc, n)
    return [(k_ref[c * kc:(c + 1) * kc, :], v_ref[c * kc:(c + 1) * kc, :], None) for c in range(n // kc)]


def _softmax_fwd(qs, chunks, sink_col):
    logits = []
    for k, _, mask in chunks:
        s = lax.dot_general(qs, k, NT, preferred_element_type=F32)
        logits.append(s if mask is None else jnp.where(mask, s, NEG_BIG))
    m = functools.reduce(jnp.maximum, [jnp.max(s, axis=1, keepdims=True) for s in logits])
    if sink_col is not None:
        m = jnp.maximum(m, sink_col)
    l = jnp.zeros_like(m) if sink_col is None else jnp.exp(sink_col - m)
    acc = jnp.zeros((qs.shape[0], 128), F32)
    for s, (_, v, _) in zip(logits, chunks):
        p = jnp.exp(s - m)
        l = l + jnp.sum(p, axis=1, keepdims=True)
        acc = acc + jnp.dot(p.astype(BF16), v, preferred_element_type=F32)
    return acc / l, m + jnp.log(l)


def _to_rows(col):
    return jnp.transpose(jnp.broadcast_to(col, (col.shape[0], 128)))[0:8, :]


def _softmax_bwd(qs, dos, lse_row, delta_row, chunks):
    dq = jnp.zeros((qs.shape[0], 128), F32)
    grads = []
    for k, v, mask in chunks:
        s = lax.dot_general(k, qs, NT, preferred_element_type=F32)
        if mask is not None:
            s = jnp.where(mask, s, NEG_BIG)
        p = jnp.exp(s - lse_row)
        dp = lax.dot_general(v, dos, NT, preferred_element_type=F32)
        ds = (p * (dp - delta_row)).astype(BF16)
        dv = jnp.dot(p.astype(BF16), dos, preferred_element_type=F32)
        dk = jnp.dot(ds, qs, preferred_element_type=F32)
        dq = dq + lax.dot_general(ds, k, TN, preferred_element_type=F32)
        grads.append((dk, dv))
    return dq, grads


def _band(qi, tq, seq):
    span = tq + 2 * WINDOW
    start = pl.multiple_of(jnp.clip(qi * tq - WINDOW, 0, seq - span), 64)
    return start, span


def _band_mask(qi, tq, start, span, query_axis):
    shape = (GROUP * tq, span) if query_axis == 0 else (span, GROUP * tq)
    qpos = qi * tq + lax.broadcasted_iota(jnp.int32, shape, query_axis) % tq
    kpos = start + lax.broadcasted_iota(jnp.int32, shape, 1 - query_axis)
    return jnp.abs(kpos - qpos) <= WINDOW


def _qkv_specs(rt, tq, q_row, ctx_row, with_latent):
    specs = [pl.BlockSpec((tq, 256), functools.partial(lambda b, i, col: (q_row(b, i), col), col=col)) for col in (0, 1, 3, 4)]
    if with_latent:
        specs += [pl.BlockSpec((rt.seq, 128), functools.partial(lambda b, i, col: (b, col), col=col))
                  for col in (COL_KA, COL_VA, COL_KB, COL_VB)]
    specs += [pl.BlockSpec((rt.ctx, 128), functools.partial(lambda b, i, col: (ctx_row(b), col), col=col))
              for col in (COL_KA, COL_VA, COL_KB, COL_VB)]
    return specs


def _attn_fwd(rt, qkvp, sink, o_prev, name, comm=None):
    latent = o_prev is None
    seq, ctx, nb = rt.seq, rt.ctx, rt.nb
    tq = Q_TILE_FWD if latent else ctx
    tile = Q_TILE if latent else ctx
    parts = tq // tile
    nq = seq // tq if latent else 1
    ctx_blk0 = rt.n_lat // ctx
    q_row = (lambda b, i: b * nq + i) if latent else (lambda b, i: ctx_blk0 + b)

    def store_lse(lse_ref, j, lse_col):
        rows = _to_rows(lse_col)
        for part in range(parts):
            lse_ref[part, j] = jnp.concatenate([rows[:, g * tq + part * tile:g * tq + (part + 1) * tile] for g in range(GROUP)], axis=1)

    def body(sink_ref, qa0, qa1, qb0, qb1, *rest):
        if latent:
            kal, val, kbl, vbl, kac, vac, kbc, vbc, o_ref, lse_ref = rest
        else:
            kac, vac, kbc, vbc, _, o_ref, lse_ref = rest
        qi = pl.program_id(1)
        for kvi, (qa, qb) in enumerate(((qa0, qb0), (qa1, qb1))):
            src_a = _key_chunks(kac, vac, ctx)
            src_b = _key_chunks(kbc, vbc, ctx)
            if latent:
                src_a += _key_chunks(kal, val, seq, seq)
                start, span = _band(qi, tq, seq)
                src_b.append((kbl[pl.ds(start, span), :], vbl[pl.ds(start, span), :], _band_mask(qi, tq, start, span, 0)))
            oa, lse = _softmax_fwd(_stack_heads(qa[...], kvi), src_a, None)
            o_ref[:, kvi * 256:(kvi + 1) * 256] = _unstack_heads(oa, kvi).astype(BF16)
            store_lse(lse_ref, kvi, lse)
            sink_col = _per_head((GROUP * tq, 1), 0, tq, [sink_ref[kvi * GROUP + g] for g in range(GROUP)])
            ob, lse = _softmax_fwd(_stack_heads(qb[...], kvi), src_b, sink_col)
            o_ref[:, 512 + kvi * 256:512 + (kvi + 1) * 256] = _unstack_heads(ob, kvi).astype(BF16)
            store_lse(lse_ref, 2 + kvi, lse)

    specs = _qkv_specs(rt, tq, q_row, lambda b: ctx_blk0 + b, latent)
    args = [sink] + [qkvp] * len(specs)
    in_specs = [pl.BlockSpec(memory_space=pltpu.SMEM)] + specs
    aliases = {}
    if not latent:
        in_specs.append(pl.BlockSpec(memory_space=pl.ANY))
        args.append(o_prev)
        aliases = {len(args) - 1: 0}
    return _comm_call(
        body, comm, name=name, grid=(nb, nq),
        in_specs=in_specs,
        out_specs=[pl.BlockSpec((tq, 1024), lambda b, i: (q_row(b, i), 0)),
                   pl.BlockSpec((parts, 4, 8, GROUP * tile), lambda b, i: (b * nq + i, 0, 0, 0))],
        out_shape=[jax.ShapeDtypeStruct((rt.rows, 1024), BF16), jax.ShapeDtypeStruct((nb * nq * parts, 4, 8, GROUP * tile), F32)],
        args=args, aliases=aliases, semantics=("parallel", "parallel"))


def _attn_bwd(rt, qkvp, o, lse, do, sink, prev, name, comm=None):
    latent = prev is None
    seq, ctx, nb = rt.seq, rt.ctx, rt.nb
    tq = Q_TILE if latent else ctx
    nq = seq // tq if latent else 1
    ctx_blk0 = rt.n_lat // ctx
    q_row = (lambda b, i: b * nq + i) if latent else (lambda b, i: ctx_blk0 + b)
    kc = min(KEY_CHUNK, seq)

    def body(sink_ref, qa0, qa1, qb0, qb1, *rest):
        if latent:
            kal, val, kbl, vbl, kac, vac, kbc, vbc, do_ref, o_ref, lse_ref, dq_ref, dl_ref, dc_ref, dsink_ref = rest
        else:
            kac, vac, kbc, vbc, do_ref, o_ref, lse_ref, c1_ref, _, _, dq_ref, dc_ref, dsink_ref = rest
        b, qi = pl.program_id(0), pl.program_id(1)

        def rows_of(cols, kvi, mixer):
            dos = _stack_heads(do_ref[:, cols], kvi)
            delta = jnp.sum(dos.astype(F32) * _stack_heads(o_ref[:, cols], kvi).astype(F32), axis=1, keepdims=True)
            return dos, lse_ref[0, 2 * mixer + kvi, 0:1, :], _to_rows(delta)[0:1, :]

        @pl.when(jnp.logical_and(b == 0, qi == 0))
        def _():
            dsink_ref[...] = jnp.zeros_like(dsink_ref)

        if latent:
            @pl.when(qi == 0)
            def _():
                dc_ref[...] = jnp.zeros_like(dc_ref)
                dl_ref[...] = jnp.zeros_like(dl_ref)
        else:
            dc_ref[...] = c1_ref[...]

        head_row = lax.broadcasted_iota(jnp.int32, (8, 128), 0)
        for kvi, (qa, qb) in enumerate(((qa0, qb0), (qa1, qb1))):
            cols = slice(kvi * 256, (kvi + 1) * 256)
            dos, lse_row, delta_row = rows_of(cols, kvi, 0)
            src = _key_chunks(kac, vac, ctx)
            if latent:
                src += _key_chunks(kal, val, seq)
            dq4, grads = _softmax_bwd(_stack_heads(qa[...], kvi), dos, lse_row, delta_row, src)
            dq_ref[:, cols] = _unstack_heads(dq4, kvi)
            dc_ref[:, 0:128] += grads[0][0]
            dc_ref[:, 128:256] += grads[0][1]
            for c, (dk, dv) in enumerate(grads[1:]):
                dl_ref[c * kc:(c + 1) * kc, 0:128] += dk
                dl_ref[c * kc:(c + 1) * kc, 128:256] += dv
            cols = slice(512 + kvi * 256, 512 + (kvi + 1) * 256)
            dos, lse_row, delta_row = rows_of(cols, kvi, 1)
            src = _key_chunks(kbc, vbc, ctx)
            if latent:
                start, span = _band(qi, tq, seq)
                src.append((kbl[pl.ds(start, span), :], vbl[pl.ds(start, span), :], _band_mask(qi, tq, start, span, 1)))
            dq4, grads = _softmax_bwd(_stack_heads(qb[...], kvi), dos, lse_row, delta_row, src)
            dq_ref[:, cols] = _unstack_heads(dq4, kvi)
            dc_ref[:, 256:384] += grads[0][0]
            dc_ref[:, 384:512] += grads[0][1]
            if latent:
                dl_ref[pl.ds(start, span), 256:384] += grads[1][0]
                dl_ref[pl.ds(start, span), 384:512] += grads[1][1]
            sink_row = _per_head((1, GROUP * tq), 1, tq, [sink_ref[kvi * GROUP + g] for g in range(GROUP)])
            dsink = -jnp.exp(sink_row - lse_row) * delta_row
            head = lax.broadcasted_iota(jnp.int32, (1, GROUP * tq), 1) // tq
            upd = jnp.zeros((8, 128), F32)
            for g in range(GROUP):
                upd = jnp.where(head_row == kvi * GROUP + g, jnp.sum(jnp.where(head == g, dsink, 0.0)), upd)
            dsink_ref[...] += upd

    specs = _qkv_specs(rt, tq, q_row, lambda b: ctx_blk0 + b, latent)
    q_rows_spec = pl.BlockSpec((tq, 1024), lambda b, i: (q_row(b, i), 0))
    in_specs = ([pl.BlockSpec(memory_space=pltpu.SMEM)] + specs
                + [q_rows_spec, q_rows_spec, pl.BlockSpec((1, 4, 8, GROUP * tq), lambda b, i: (b * nq + i, 0, 0, 0))])
    args = [sink] + [qkvp] * len(specs) + [do, o, lse]
    dq_shape = jax.ShapeDtypeStruct((rt.rows, 1024), F32)
    dkv_shape = jax.ShapeDtypeStruct((rt.rows, 512), F32)
    dsink_spec, dsink_shape = pl.BlockSpec((8, 128), lambda b, i: (0, 0)), jax.ShapeDtypeStruct((8, 128), F32)
    dq_spec = pl.BlockSpec((tq, 1024), lambda b, i: (q_row(b, i), 0))
    if latent:
        out_specs = [dq_spec, pl.BlockSpec((seq, 512), lambda b, i: (b, 0)), pl.BlockSpec((ctx, 512), lambda b, i: (b, 0)), dsink_spec]
        out_shape = [dq_shape, dkv_shape, jax.ShapeDtypeStruct((rt.n_ctx, 512), F32), dsink_shape]
        aliases = {}
    else:
        dq_prev, dkv_prev, c1 = prev
        in_specs += [pl.BlockSpec((ctx, 512), lambda b, i: (b, 0)), pl.BlockSpec(memory_space=pl.ANY), pl.BlockSpec(memory_space=pl.ANY)]
        args += [c1, dq_prev, dkv_prev]
        out_specs = [dq_spec, pl.BlockSpec((ctx, 512), lambda b, i: (ctx_blk0 + b, 0)), dsink_spec]
        out_shape = [dq_shape, dkv_shape, dsink_shape]
        aliases = {len(args) - 2: 0, len(args) - 1: 1}
    return _comm_call(body, comm, name=name, grid=(nb, nq), in_specs=in_specs, out_specs=out_specs, out_shape=out_shape,
                      args=args, aliases=aliases, semantics=("arbitrary", "arbitrary"))


def _silu(x):
    return x / (1.0 + jnp.exp(-x))


def _whole(shape):
    return pl.BlockSpec(shape, lambda i, s: (0,) * len(shape))


def _ada_half_spec(cols):
    return pl.BlockSpec((DEPTH, D_MODEL, cols), lambda i, s: (0, 0, s[0]))


def _ada_fwd(cond, w_ada, b_half, c_idx, name):
    rows = cond.shape[0]
    cols = w_ada.shape[2] // 2

    def body(s_ref, c_ref, w_ref, b_ref, x_ref, o_ref):
        xs = _silu(c_ref[...]).astype(BF16)
        x_ref[...] = xs
        for l in range(DEPTH):
            o_ref[l] = jnp.dot(xs, w_ref[l].astype(BF16), preferred_element_type=F32) + b_ref[l]

    grid_spec = pltpu.PrefetchScalarGridSpec(
        num_scalar_prefetch=1, grid=(1,),
        in_specs=[_whole(cond.shape), _ada_half_spec(cols), _whole(b_half.shape)],
        out_specs=[_whole((rows, D_MODEL)), _whole((DEPTH, rows, cols))])
    return pl.pallas_call(
        body, name=name, grid_spec=grid_spec,
        out_shape=[jax.ShapeDtypeStruct((rows, D_MODEL), BF16), jax.ShapeDtypeStruct((DEPTH, rows, cols), F32)],
        compiler_params=_params(("arbitrary",)),
    )(c_idx, cond, w_ada, b_half)


def _ada_cond_bwd(dcx, w_ada, c_idx, name):
    _, rows, cols = dcx.shape

    def body(s_ref, d_ref, w_ref, o_ref):
        acc = jnp.zeros((rows, D_MODEL), F32)
        for l in range(DEPTH):
            acc = acc + lax.dot_general(d_ref[l], w_ref[l].astype(BF16), NT, preferred_element_type=F32)
        o_ref[...] = acc

    grid_spec = pltpu.PrefetchScalarGridSpec(
        num_scalar_prefetch=1, grid=(1,),
        in_specs=[_whole(dcx.shape), _ada_half_spec(cols)], out_specs=_whole((rows, D_MODEL)))
    return pl.pallas_call(body, name=name, grid_spec=grid_spec, out_shape=jax.ShapeDtypeStruct((rows, D_MODEL), F32),
                          compiler_params=_params(("arbitrary",)))(c_idx, dcx, w_ada)


def _dev_sum(x, name):
    _, r, c = x.shape

    def body(x_ref, o_ref):
        v = x_ref[0]
        for d in range(1, N_DEV):
            v = v + x_ref[d]
        o_ref[...] = v

    return pl.pallas_call(body, name=name, out_shape=jax.ShapeDtypeStruct((r, c), F32))(x)


def _adam_val(w, g, m, v):
    c1 = 1.0 / (1.0 - ADAM_B1 ** ADAM_STEP)
    c2 = 1.0 / (1.0 - ADAM_B2 ** ADAM_STEP)
    nm = ADAM_B1 * m + (1.0 - ADAM_B1) * g
    nv = ADAM_B2 * v + (1.0 - ADAM_B2) * (g * g)
    return -ADAM_LR * ((nm * c1) / (jnp.sqrt(nv * c2) + ADAM_EPS) + ADAM_WD * w), nm, nv


def _small_update(tot, dcc_parts, params, n_groups, name):
    n_p = len(params)
    mod_rows = n_groups * N_MOD

    def body(tot_ref, dcc_ref, *refs):
        ins, outs = refs[:3 * n_p], refs[3 * n_p:]

        def update(p, rows, cols, g):
            w_ref, m_ref, v_ref = ins[3 * p:3 * p + 3]
            g_ref, d_ref, nm_ref, nv_ref = outs[4 * p:4 * p + 4]
            d, nm, nv = _adam_val(w_ref[rows, cols], g, m_ref[rows, cols], v_ref[rows, cols])
            g_ref[rows, cols] = g
            d_ref[rows, cols] = d
            nm_ref[rows, cols] = nm
            nv_ref[rows, cols] = nv

        acc = dcc_ref[0, 0:1, :]
        for d in range(1, N_DEV):
            acc = acc + dcc_ref[d, 0:1, :]
        c = ins[0][...]
        sg = 1.0 / (1.0 + jnp.exp(-c))
        update(0, slice(0, 1), slice(None), acc * (sg * (1.0 + c * (1.0 - sg))))
        for l in range(DEPTH):
            for i in range(N_MOD):
                g = tot_ref[l * mod_rows + i:l * mod_rows + i + 1, :]
                for grp in range(1, n_groups):
                    g = g + tot_ref[l * mod_rows + grp * N_MOD + i:l * mod_rows + grp * N_MOD + i + 1, :]
                update(1, slice(l, l + 1), slice(i * D_MODEL, (i + 1) * D_MODEL), g)
            for j in range(4):
                row = DEPTH * mod_rows + 4 * l + j
                update(2 + j, slice(l, l + 1), slice(None), tot_ref[row:row + 1, :])

    shapes = [jax.ShapeDtypeStruct(w.shape, F32) for w, _, _ in params for _ in range(4)]
    outs = pl.pallas_call(body, name=name, out_shape=shapes)(tot, dcc_parts, *[a for p in params for a in p])
    return [tuple(outs[4 * p:4 * p + 4]) for p in range(n_p)]


def _adamw(w, g, m, v, name):
    r, c = w.shape
    tr = _pick(r, (256, 128, 64, 32, 24, 16, 8))

    def body(w_ref, g_ref, m_ref, v_ref, d_ref, nm_ref, nv_ref):
        d_ref[...], nm_ref[...], nv_ref[...] = _adam_val(w_ref[...], g_ref[...], m_ref[...], v_ref[...])

    spec = pl.BlockSpec((tr, c), lambda i: (i, 0))
    return pl.pallas_call(body, name=name, grid=(r // tr,), in_specs=[spec] * 4, out_specs=[spec] * 3,
                          out_shape=[jax.ShapeDtypeStruct((r, c), F32)] * 3, compiler_params=_params(("parallel",)))(w, g, m, v)


def _adamw_shard(kind, l, w, m, v, halves, off, prev, name):
    h = PACK_HEIGHT[kind]
    assert off % h == 0, (kind, off)
    _, r, c = w.shape
    rows = r // 2

    def body(w_ref, m_ref, v_ref, p_ref, *rest):
        g_ref, d_ref, nm_ref, nv_ref = rest[-4:]
        if kind == "in":
            for t in range(2):
                g = p_ref[:, t * IN_PIECE_COLS:(t + 1) * IN_PIECE_COLS]
                rs = slice(t * h, (t + 1) * h)
                g_ref[rs, :] = g
                d_ref[rs, :], nm_ref[rs, :], nv_ref[rs, :] = _adam_val(w_ref[rs, :], g, m_ref[rs, :], v_ref[rs, :])
        else:
            g = p_ref[...]
            g_ref[...] = g
            d_ref[...], nm_ref[...], nv_ref[...] = _adam_val(w_ref[...], g, m_ref[...], v_ref[...])

    blk = pl.BlockSpec((None, rows, c), lambda half: (l, half, 0))
    in_specs = [blk, blk, blk, pl.BlockSpec((None, h, 1024), lambda half: (half, off // h, 0))]
    args = [w, m, v, halves]
    aliases = {}
    if prev is not None:
        in_specs += [pl.BlockSpec(memory_space=pl.ANY)] * 4
        args += list(prev)
        aliases = {4 + j: j for j in range(4)}
    return pl.pallas_call(
        body, name=name, grid=(2,), in_specs=in_specs, out_specs=[blk] * 4,
        out_shape=[jax.ShapeDtypeStruct(w.shape, F32)] * 4, input_output_aliases=aliases,
        compiler_params=_params(("parallel",)))(*args)


SMALL_ROWS = 48


def _small_rows(small, sq):
    def lane_pad(v):
        return jnp.pad(v, (0, D_MODEL - v.shape[0]))[None]

    head_rows = [lane_pad(jnp.concatenate([s["q_norm"][0], s["k_norm"][0], s["sink"]])) for s in small]
    loss_row = lane_pad((0.5 / D_MODEL) * jnp.sum(sq, keepdims=True)[0])
    rows = jnp.concatenate([s["mod"].reshape(-1, D_MODEL) for s in small] + [s["gammas"] for s in small] + head_rows + [loss_row], axis=0)
    return jnp.pad(rows, ((0, SMALL_ROWS - rows.shape[0]), (0, 0)))


def _local_step(x, ctx, target, mods, gam, qn, kn, sink, w_first, w_layers, packed, kc_idx):
    nb, seq, _ = x.shape
    rt = _Rows(nb, seq, ctx.shape[1])
    rt_lat = rt.latent_only()
    tables = _rope_tables(rt)
    fuse = packed is not None
    h = jnp.concatenate([x.reshape(rt.n_lat, D_MODEL), ctx.reshape(rt.n_ctx, D_MODEL)], axis=0)
    wg = [{}, {}] if fuse else [dict(w) for w in w_layers]
    wg[0]["in"] = (w_first, 0)
    if fuse:
        wg[0]["in_own"] = (packed, W_FIRST[0])
    saved = []
    for l in range(DEPTH):
        g_pre_mix, g_post_mix, g_pre_mlp, g_post_mlp = gam[l]
        u, qkv, qkvp = _in_fwd(rt, h, g_pre_mix, mods[l], wg[l], tables, qn[l], kn[l], f"in_fwd{l}")
        if fuse and l == 0:
            o, lse_lat, w_mlp0, w_out0, w_mix1 = _attn_fwd(rt, qkvp, sink[l], None, f"attn_lat_fwd{l}",
                                                          comm=_gather_comm(packed, [W_MLP0, W_OUT0, W_MIX1], lead=2))
            wg[0].update({kind: (w_mlp0, PACK_OFF[(kind, 0)] - W_MLP0[0]) for kind in ("up", "down")})
            wg[0]["out"] = (w_out0, 0)
            wg[1] = {kind: (w_mix1, PACK_OFF[(kind, 1)] - W_MIX1[0]) for kind in ("out", "in")}
        elif fuse:
            o, lse_lat, w_mlp1 = _attn_fwd(rt, qkvp, sink[l], None, f"attn_lat_fwd{l}", comm=_gather_comm(packed, [W_MLP1], lead=2))
            wg[1].update({kind: (w_mlp1, PACK_OFF[(kind, 1)] - W_MLP1[0]) for kind in ("up", "down")})
        else:
            o, lse_lat = _attn_fwd(rt, qkvp, sink[l], None, f"attn_lat_fwd{l}")
        if l < DEPTH - 1:
            o, lse_ctx = _attn_fwd(rt, qkvp, sink[l], o, f"attn_ctx_fwd{l}")
            mix, h1, u2 = _out_fwd(rt, o, wg[l], h, mods[l], g_post_mix, g_pre_mlp, f"out_fwd{l}")
            r, y, h2 = _mlp_fwd(rt, u2, h1, wg[l], mods[l], g_post_mlp, f"mlp_fwd{l}")
        else:
            lse_ctx = None
            mix, h1, u2 = _out_fwd(rt_lat, o, wg[l], h, mods[l], g_post_mix, g_pre_mlp, f"out_fwd{l}")
            r, y, dh, sq = _mlp_fwd(rt_lat, u2, h1, wg[l], mods[l], g_post_mlp, f"mlp_fwd{l}", target=target.reshape(rt.n_lat, D_MODEL))
        saved.append((h, u, qkv, qkvp, o, lse_lat, lse_ctx, mix, h1, u2, r, y))
        h = h2

    small = [None] * DEPTH
    groups = {}
    for l in reversed(range(DEPTH)):
        g_pre_mix, g_post_mix, g_pre_mlp, g_post_mlp = gam[l]
        h0, u, qkv, qkvp, o, lse_lat, lse_ctx, mix, h1, u2, r, y = saved[l]
        mlp_group, mix_group = (G_LAYER1, G_LAYER1) if l == 1 else (G_MLP0, G_MIX0)
        hide = fuse and l == 0

        dead_ctx = l == DEPTH - 1
        rt_b = rt_lat if dead_ctx else rt
        outs = _mlp_down_bwd(rt_b, dh, y, r, wg[l], mods[l], g_post_mlp, f"mlp_down_bwd{l}",
                             comm=_pair_comm(groups[G_LAYER1]) if hide else None)
        dy, da, d_gate_m, d_g_post_mlp = outs[:4]
        if hide:
            sum1 = _pair_sum(groups[G_LAYER1], outs[4], kc_idx, "grad_pair_sum_layer1")
        p_mlp = _wgrad_packed(rt_b, r, dy, "down", PACK_OFF[("down", l)] - mlp_group[0], mlp_group[1], None, f"mlp_down_wgrad{l}")
        dh1, d_sh_m, d_sc_m, d_g_pre_mlp = _mlp_up_bwd(rt_b, da, wg[l], h1, dh, mods[l], g_pre_mlp, f"mlp_up_bwd{l}")
        p_mlp = _wgrad_packed(rt_b, u2, da, "up", PACK_OFF[("up", l)] - mlp_group[0], mlp_group[1], p_mlp, f"mlp_up_wgrad{l}")
        outs = _out_bwd(rt_b, dh1, mix, wg[l], mods[l], g_post_mix, f"out_bwd{l}", comm=_pair_comm(p_mlp) if hide else None)
        dmix, do, d_gate_a, d_g_post_mix = outs[:4]
        if hide:
            sum0 = _pair_sum(p_mlp, outs[4], kc_idx, "grad_pair_sum_mlp0")
        p_mix = _wgrad_packed(rt_b, o, dmix, "out", PACK_OFF[("out", l)] - mix_group[0], mix_group[1],
                              p_mlp if l == 1 else None, f"out_wgrad{l}")
        outs = _attn_bwd(rt, qkvp, o, lse_lat, do, sink[l], None, f"attn_lat_bwd{l}",
                         comm=_chip_comm([sum1[1], sum0[1]]) if hide else None)
        dq, dkv, dkv_c, dsink1 = outs[:4]
        if hide:
            groups[G_LAYER1] = _owner_sum(sum1[0], outs[4], kc_idx, "grad_owner_sum_layer1")
            groups[G_MLP0] = _owner_sum(sum0[0], outs[5], kc_idx, "grad_owner_sum_mlp0")
        if dead_ctx:
            dsink2 = jnp.zeros_like(dsink1)
            d_gate_m, d_sh_m, d_sc_m, d_gate_a = [a.at[nb].set(0.0) for a in (d_gate_m, d_sh_m, d_sc_m, d_gate_a)]
        else:
            dq, dkv, dsink2 = _attn_bwd(rt, qkvp, o, lse_ctx, do, sink[l], (dq, dkv, dkv_c), f"attn_ctx_bwd{l}")
        dqkv, dh, dqn, dkn, d_sh_a, d_sc_a, d_g_pre_mix = _in_bwd(rt, dq, dkv, qkv, tables, qn[l], kn[l], wg[l], h0, dh1, mods[l],
                                                                  g_pre_mix, l == 0, f"in_bwd{l}",
                                                                  dead_ctx_dkv=dkv_c if dead_ctx else None)
        dmod = jnp.concatenate([d_sh_a, d_sc_a, d_gate_a, d_sh_m, d_sc_m, d_gate_m], axis=1)
        small[l] = dict(mod=dmod, gammas=jnp.concatenate([d_g_pre_mix, d_g_post_mix, d_g_pre_mlp, d_g_post_mlp], axis=0),
                        q_norm=dqn, k_norm=dkn, sink=(dsink1 + dsink2)[:, 0])
        gather = _gather_comm(_small_rows(small, sq), [(0, SMALL_ROWS)]) if hide else None
        outs = _wgrad_packed(rt, u, dqkv, "in", PACK_OFF[("in", l)] - mix_group[0], mix_group[1], p_mix, f"in_wgrad{l}", comm=gather)
        groups[mix_group], small_g = outs if hide else (outs, None)
        if not hide and l == 0:
            groups[G_MLP0] = p_mlp
    return sq, dh.reshape(nb, seq, D_MODEL), [groups[G_LAYER1], groups[G_MLP0], groups[G_MIX0]], small, small_g


def kernel(x, c, ctx, c_ctx, w_ada, b_ada, g_pre_mix, g_post_mix, g_pre_mlp, g_post_mlp, w_in, q_norm, k_norm, sink, w_out, w_up, w_down, loss_target, m_c_ctx, m_w_ada, m_b_ada, m_g_pre_mix, m_g_post_mix, m_g_pre_mlp, m_g_post_mlp, m_w_in, m_q_norm, m_k_norm, m_sink, m_w_out, m_w_up, m_w_down, v_c_ctx, v_w_ada, v_b_ada, v_g_pre_mix, v_g_post_mix, v_g_pre_mlp, v_g_post_mlp, v_w_in, v_q_norm, v_k_norm, v_sink, v_w_out, v_w_up, v_w_down):
    nb = x.shape[0]
    ix, iy, ic = lax.axis_index("x"), lax.axis_index("y"), lax.axis_index("c")
    chip = 2 * ix + iy
    dev = 2 * chip + ic
    ada_cols = w_ada.shape[2] // 2

    packed = _pack_local_half(w_in, w_out, w_up, w_down, ic)
    c_rows = c.reshape(8, (nb * D_MODEL) // 8)
    c_all, w_first = _comm_alone(_merge([_gather_comm(c_rows, [(0, c_rows.shape[0])]), _gather_comm(packed, [W_FIRST], copy_own=False)]),
                                 "gather_c_w_first")
    c_all = c_all.reshape(N_DEV * nb, D_MODEL)

    n_cond = N_DEV * nb + 1
    cond_rows = 16 * ((n_cond + 15) // 16)
    cond = jnp.concatenate([c_all, c_ctx[None, :], jnp.zeros((cond_rows - n_cond, D_MODEL), F32)], axis=0)
    c_idx = ic.reshape(1).astype(jnp.int32)
    kc_idx = jnp.stack([chip, ic]).astype(jnp.int32)
    b_ada_half = lax.dynamic_slice_in_dim(b_ada, dev * ada_cols, ada_cols, 1)[:, None, :]
    x_ada, mod_part = _ada_fwd(cond, w_ada, b_ada_half, c_idx, "ada_fwd")
    mod_g = _all_gather(mod_part.reshape(DEPTH * cond_rows, ada_cols), "gather_mod", False)
    mod_all = mod_g.reshape(N_DEV, DEPTH, cond_rows, ada_cols).transpose(1, 2, 0, 3).reshape(DEPTH, cond_rows, N_MOD * D_MODEL)
    mods = []
    for l in range(DEPTH):
        mine = lax.dynamic_slice_in_dim(mod_all[l], dev * nb, nb, 0)
        mods.append(jnp.concatenate([mine, mod_all[l, n_cond - 1:n_cond]], axis=0).reshape(nb + 1, N_MOD, D_MODEL))

    gam = [(g_pre_mix[l][None], g_post_mix[l][None], g_pre_mlp[l][None], g_post_mlp[l][None]) for l in range(DEPTH)]
    qn = [jnp.tile(q_norm[l], 2)[None] for l in range(DEPTH)]
    kn = [jnp.tile(k_norm[l], 2)[None] for l in range(DEPTH)]
    _, grad_x, (h_layer1, h_mlp0, p_mix0), _, small_g = _local_step(x, ctx, loss_target, mods, gam, qn, kn, [sink[l] for l in range(DEPTH)],
                                                                 w_first, None, packed, kc_idx)

    def step(w, g, m, v, name):
        shape = w.shape
        cols = shape[-1]
        outs = _adamw(w.reshape(-1, cols), g.reshape(-1, cols), m.reshape(-1, cols), v.reshape(-1, cols), name)
        return tuple(a.reshape(shape) for a in outs)

    def shard_update(kind, w, m, v, layer0, layer1):
        outs = None
        for l, (halves, group) in enumerate((layer0, layer1)):
            outs = _adamw_shard(kind, l, w, m, v, halves, PACK_OFF[(kind, l)] - group[0], outs, f"adamw_w_{kind}{l}")
        return tuple(outs)

    tot = _dev_sum(small_g, "small_sum")
    mod_rows = (nb + 1) * N_MOD
    o_head = DEPTH * mod_rows + 4 * DEPTH
    loss = tot[o_head + DEPTH, 0]
    grad_q_norm = tot[o_head:o_head + DEPTH, 0:64] + tot[o_head:o_head + DEPTH, 64:128]
    grad_k_norm = tot[o_head:o_head + DEPTH, 128:192] + tot[o_head:o_head + DEPTH, 192:256]
    grad_sink = tot[o_head:o_head + DEPTH, 256:264]

    ex = small_g[:, :DEPTH * mod_rows].reshape(N_DEV, DEPTH, nb + 1, N_MOD * D_MODEL)[:, :, :nb]
    ex = ex.transpose(1, 0, 2, 3).reshape(DEPTH, N_DEV * nb, N_MOD * D_MODEL)
    cx = tot[:DEPTH * mod_rows].reshape(DEPTH, nb + 1, N_MOD * D_MODEL)[:, nb:]
    dm = jnp.concatenate([ex, cx, jnp.zeros((DEPTH, cond_rows - n_cond, N_MOD * D_MODEL), F32)], axis=1)
    shard_cols = w_ada.shape[2]
    grad_w_ada = _ada_wgrad(x_ada, lax.dynamic_slice_in_dim(dm, chip * shard_cols, shard_cols, 2).astype(BF16), "ada_wgrad")
    dcx = jnp.pad(lax.dynamic_slice_in_dim(cx, dev * ada_cols, ada_cols, 2), ((0, 0), (0, 15), (0, 0))).astype(BF16)
    dcc = _ada_cond_bwd(dcx, w_ada, c_idx, "ada_cond_bwd")[0:8]

    r1, = _comm_alone(_pair_comm(p_mix0), "grad_pair_exchange_mix0")
    a32, a16 = _pair_sum(p_mix0, r1, kc_idx, "grad_pair_sum_mix0")
    r2, dcc_g = _comm_alone(_merge([_chip_comm([a16]), _gather_comm(dcc, [(0, dcc.shape[0])])]), "grad_chip_exchange_mix0")
    h_mix0 = _owner_sum(a32, r2, kc_idx, "grad_owner_sum_mix0")
    h_layer1, h_mlp0, h_mix0 = _comm_alone(_halves_comm([h_layer1, h_mlp0, h_mix0]), "grad_halves_exchange")

    dense_names = ["c_ctx", "b_ada", "g_pre_mix", "g_post_mix", "g_pre_mlp", "g_post_mlp"]
    dense = _small_update(tot, dcc_g, [(c_ctx[None], m_c_ctx[None], v_c_ctx[None]), (b_ada, m_b_ada, v_b_ada),
                                       (g_pre_mix, m_g_pre_mix, v_g_pre_mix), (g_post_mix, m_g_post_mix, v_g_post_mix),
                                       (g_pre_mlp, m_g_pre_mlp, v_g_pre_mlp), (g_post_mlp, m_g_post_mlp, v_g_post_mlp)],
                          nb + 1, "small_update")
    res = {n: r for n, r in zip(dense_names, dense)}
    res["c_ctx"] = tuple(a[0] for a in res["c_ctx"])
    small_names = ["q_norm", "k_norm", "sink"]
    small_w = [q_norm, k_norm, sink]
    small_gr = [grad_q_norm, grad_k_norm, grad_sink]
    small_m = [m_q_norm, m_k_norm, m_sink]
    small_v = [v_q_norm, v_k_norm, v_sink]
    sizes = [int(np.prod(w.shape)) for w in small_w]
    total = sum(sizes)
    flat_rows = 8 * ((total + 8 * D_MODEL - 1) // (8 * D_MODEL))

    def flat(arrs, fill):
        f = jnp.concatenate([a.reshape(-1) for a in arrs])
        return jnp.concatenate([f, jnp.full((flat_rows * D_MODEL - total,), fill, F32)]).reshape(flat_rows, D_MODEL)

    sd, snm, snv = _adamw(flat(small_w, 0.0), flat(small_gr, 0.0), flat(small_m, 0.0), flat(small_v, 1.0), "adamw_small")[:3]

    def unflat(f):
        f = f.reshape(-1)
        out, off = [], 0
        for w, n in zip(small_w, sizes):
            out.append(f[off:off + n].reshape(w.shape))
            off += n
        return out

    small_d, small_nm, small_nv = unflat(sd), unflat(snm), unflat(snv)
    res.update({n: (g, d, nm, nv) for n, g, d, nm, nv in zip(small_names, small_gr, small_d, small_nm, small_nv)})
    res["w_ada"] = (grad_w_ada, *step(w_ada, grad_w_ada, m_w_ada, v_w_ada, "adamw_w_ada"))
    res["w_up"] = shard_update("up", w_up, m_w_up, v_w_up, (h_mlp0, G_MLP0), (h_layer1, G_LAYER1))
    res["w_down"] = shard_update("down", w_down, m_w_down, v_w_down, (h_mlp0, G_MLP0), (h_layer1, G_LAYER1))
    res["w_in"] = shard_update("in", w_in, m_w_in, v_w_in, (h_mix0, G_MIX0), (h_layer1, G_LAYER1))
    res["w_out"] = shard_update("out", w_out, m_w_out, v_w_out, (h_mix0, G_MIX0), (h_layer1, G_LAYER1))

    order = ["c_ctx", "w_ada", "b_ada", "g_pre_mix", "g_post_mix", "g_pre_mlp", "g_post_mlp", "w_in", "q_norm", "k_norm", "sink", "w_out", "w_up", "w_down"]
    return (loss, grad_x, *[res[n][0] for n in order], *[res[n][1] for n in order],
            *[res[n][2] for n in order], *[res[n][3] for n in order])
```

```python
import functools

import jax
import jax.numpy as jnp
import numpy as np
from jax import lax
from jax.experimental import pallas as pl
from jax.experimental.pallas import tpu as pltpu

F32 = jnp.float32
BF16 = jnp.bfloat16

D_MODEL = 1024
HEAD_DIM = 64
GROUP = 4
WINDOW = 128
N_MOD = 6
D_FF = 4 * D_MODEL
IN_COLS = 1536
GRID_W = 64
ROPE_THETA = 10000.0
EPS = 1e-6
NEG_BIG = -1e30
Q_SCALE = HEAD_DIM ** -0.5
DEPTH = 2
N_DEV = 8

ADAM_LR = 0.001
ADAM_B1 = 0.9
ADAM_B2 = 0.999
ADAM_EPS = 1e-08
ADAM_WD = 0.01
ADAM_STEP = 10

V7X_VMEM_BYTES = 64 * 1024 * 1024
VMEM_LIMIT = V7X_VMEM_BYTES - 8 * 1024 * 1024

MESH = pl.DeviceIdType.MESH
NT = (((1,), (1,)), ((), ()))
TN = (((0,), (0,)), ((), ()))

COL_KA, COL_VA, COL_KB, COL_VB = 4, 5, 10, 11
NORMED_COLS = 640

PACK_HEIGHT = {"up": 512, "down": 512, "in": 256, "out": 128}
IN_PIECE_COLS = 384
PACK_OFF = {("up", 0): 0, ("down", 0): 512, ("in", 0): 1024, ("out", 0): 1280,
            ("up", 1): 1408, ("down", 1): 1920, ("in", 1): 2432, ("out", 1): 2688}
PACK_ROWS = 2816
W_FIRST, W_MLP0, W_OUT0, W_MLP1, W_MIX1 = (1024, 256), (0, 1024), (1280, 128), (1408, 1024), (2432, 384)
G_LAYER1, G_MLP0, G_MIX0 = (1408, 1408), (0, 1024), (1024, 384)


def _pick(n, cands):
    for t in cands:
        if n % t == 0:
            return t
    raise ValueError(f"no tile for {n}")


def _params(sem):
    return pltpu.CompilerParams(dimension_semantics=sem, vmem_limit_bytes=VMEM_LIMIT)


class _Comm:
    def __init__(self, inputs, out_shapes, aliases, n_send, n_recv, start, finish, relay=None, lead=0):
        self.inputs, self.out_shapes, self.aliases = list(inputs), list(out_shapes), dict(aliases)
        self.n_send, self.n_recv, self.start, self.finish, self.relay, self.lead = n_send, n_recv, start, finish, relay, lead


def _comm_call(compute, comm, *, name, grid, in_specs, out_specs, out_shape, args, aliases, semantics, scratch=()):
    in_specs, out_specs, out_shape, args, aliases = list(in_specs), list(out_specs), list(out_shape), list(args), dict(aliases)
    scratch = list(scratch)
    if comm is None:
        return pl.pallas_call(compute, name=name, grid=grid, in_specs=in_specs, out_specs=out_specs, out_shape=out_shape,
                              input_output_aliases=aliases, scratch_shapes=scratch, compiler_params=_params(semantics))(*args)
    n_in, n_out, n_ci, n_co = len(args), len(out_shape), len(comm.inputs), len(comm.out_shapes)
    hbm = pl.BlockSpec(memory_space=pl.ANY)
    aliases.update({n_in + i: n_out + o for i, o in comm.aliases.items()})

    def body(*refs):
        ins, c_ins = refs[:n_in], refs[n_in:n_in + n_ci]
        outs, c_outs = refs[n_in + n_ci:n_in + n_ci + n_out], refs[n_in + n_ci + n_out:n_in + n_ci + n_out + n_co]
        scr = refs[n_in + n_ci + n_out + n_co:-2]
        send_sems, recv_sems = refs[-2:]
        ids = [pl.program_id(a) for a in range(len(grid))]
        first = functools.reduce(jnp.logical_and, [i == 0 for i in ids])
        last = functools.reduce(jnp.logical_and, [i == g - 1 for i, g in zip(ids, grid)])

        @pl.when(first)
        def _():
            comm.start(c_ins, c_outs, send_sems, recv_sems)

        compute(*ins, *outs, *scr)

        if comm.relay is not None:
            step = functools.reduce(lambda acc, ig: acc * ig[1] + ig[0], zip(ids, grid), 0)

            @pl.when(step == int(np.prod(grid)) - 1 - comm.lead)
            def _():
                comm.relay(c_ins, c_outs, send_sems, recv_sems)

        @pl.when(last)
        def _():
            comm.finish(c_ins, c_outs, send_sems, recv_sems)

    return pl.pallas_call(
        body, name=name, grid=grid,
        in_specs=in_specs + [hbm] * n_ci, out_specs=out_specs + [hbm] * n_co, out_shape=out_shape + comm.out_shapes,
        input_output_aliases=aliases,
        scratch_shapes=scratch + [pltpu.SemaphoreType.DMA((comm.n_send,)), pltpu.SemaphoreType.DMA((comm.n_recv,))],
        compiler_params=_params(("arbitrary",) * len(grid)),
    )(*args, *comm.inputs)


def _place():
    x_, y_, c_ = lax.axis_index("x"), lax.axis_index("y"), lax.axis_index("c")
    return x_, y_, c_, [(1 - x_, y_), (x_, 1 - y_), (1 - x_, 1 - y_)]


GATHER_SENDS, GATHER_RECVS = 8, 7


def _gather_copies(packed_ref, wg_ref, send_sems, recv_sems, rows, nth=0):
    r0, n = rows
    x_, y_, c_, chips = _place()
    me, sibling = (x_, y_, c_), (x_, y_, 1 - c_)
    src = packed_ref.at[pl.ds(r0, n), :]

    def slot(px, py, pc):
        return wg_ref.at[4 * px + 2 * py + pc]

    def copy(k, block, to, from_packed=False):
        return pltpu.make_async_remote_copy(src_ref=src if from_packed else slot(*block), dst_ref=slot(*block),
                                            send_sem=send_sems.at[GATHER_SENDS * nth + k], recv_sem=recv_sems.at[GATHER_RECVS * nth + k],
                                            device_id=to, device_id_type=MESH)

    own = [copy(0, me, sibling, True)] + [copy(1 + j, me, (*chip, c_), True) for j, chip in enumerate(chips)]
    passed = [copy(4 + j, (*chip, c_), sibling) for j, chip in enumerate(chips)]
    over_ici = [copy(1 + j, (*chip, c_), me) for j, chip in enumerate(chips)]
    from_sibling = [copy(0, sibling, me)] + [copy(4 + j, (*chip, 1 - c_), me) for j, chip in enumerate(chips)]
    mine = pltpu.make_async_copy(src, slot(*me), send_sems.at[GATHER_SENDS * nth + 7])
    return mine, own, passed, over_ici, from_sibling


def _gather_start(packed_ref, wg_ref, send_sems, recv_sems, rows, nth=0, copy_own=True):
    mine, own, _, _, _ = _gather_copies(packed_ref, wg_ref, send_sems, recv_sems, rows, nth)
    if copy_own:
        mine.start()
    for cp in own:
        cp.start()


def _gather_relay(packed_ref, wg_ref, send_sems, recv_sems, rows, nth=0):
    _, _, passed, over_ici, _ = _gather_copies(packed_ref, wg_ref, send_sems, recv_sems, rows, nth)
    for arrived, onward in zip(over_ici, passed):
        arrived.wait_recv()
        onward.start()


def _gather_finish(packed_ref, wg_ref, send_sems, recv_sems, rows, nth=0, copy_own=True):
    mine, own, passed, _, from_sibling = _gather_copies(packed_ref, wg_ref, send_sems, recv_sems, rows, nth)
    for arrived in from_sibling:
        arrived.wait_recv()
    for cp in own + passed:
        cp.wait_send()
    if copy_own:
        mine.wait()


def _gather_comm(packed, ranges, copy_own=True, lead=0):
    shapes = [jax.ShapeDtypeStruct((N_DEV, n, packed.shape[1]), packed.dtype) for _, n in ranges]

    def start(ins, outs, ss, rs):
        for nth, rows in enumerate(ranges):
            _gather_start(ins[0], outs[nth], ss, rs, rows, nth, copy_own)

    def relay(ins, outs, ss, rs):
        for nth, rows in enumerate(ranges):
            _gather_relay(ins[0], outs[nth], ss, rs, rows, nth)

    def finish(ins, outs, ss, rs):
        for nth, rows in enumerate(ranges):
            _gather_finish(ins[0], outs[nth], ss, rs, rows, nth, copy_own)

    return _Comm([packed], shapes, {}, GATHER_SENDS * len(ranges), GATHER_RECVS * len(ranges), start, finish, relay, lead)


def _pair_copy(p_ref, out_ref, send_sems, recv_sems):
    x_, y_, c_, _ = _place()
    return pltpu.make_async_remote_copy(src_ref=p_ref.at[1 - c_], dst_ref=out_ref,
                                        send_sem=send_sems.at[0], recv_sem=recv_sems.at[0],
                                        device_id=(x_, y_, 1 - c_), device_id_type=MESH)


def _pair_comm(p):
    return _Comm([p], [jax.ShapeDtypeStruct(p.shape[1:], p.dtype)], {}, 1, 1,
                 lambda ins, outs, ss, rs: _pair_copy(ins[0], outs[0], ss, rs).start(),
                 lambda ins, outs, ss, rs: _pair_copy(ins[0], outs[0], ss, rs).wait())


def _chip_copies(a_refs, out_refs, send_sems, recv_sems):
    _, _, c_, chips = _place()
    return [pltpu.make_async_remote_copy(src_ref=a_ref.at[2 * tx + ty], dst_ref=o_ref.at[j],
                                         send_sem=send_sems.at[3 * g + j], recv_sem=recv_sems.at[3 * g + j],
                                         device_id=(tx, ty, c_), device_id_type=MESH)
            for g, (a_ref, o_ref) in enumerate(zip(a_refs, out_refs)) for j, (tx, ty) in enumerate(chips)]


def _chip_start(a_refs, out_refs, send_sems, recv_sems):
    for cp in _chip_copies(a_refs, out_refs, send_sems, recv_sems):
        cp.start()


def _chip_finish(a_refs, out_refs, send_sems, recv_sems):
    for cp in _chip_copies(a_refs, out_refs, send_sems, recv_sems):
        cp.wait()


def _chip_comm(arrays):
    shapes = [jax.ShapeDtypeStruct((3,) + a.shape[1:], a.dtype) for a in arrays]
    return _Comm(arrays, shapes, {}, 3 * len(arrays), 3 * len(arrays), _chip_start, _chip_finish)


def _halves_copies(in_refs, out_refs, send_sems, recv_sems):
    x_, y_, c_, _ = _place()
    return [pltpu.make_async_remote_copy(src_ref=o_ref.at[c_], dst_ref=o_ref.at[c_], send_sem=send_sems.at[i], recv_sem=recv_sems.at[i],
                                         device_id=(x_, y_, 1 - c_), device_id_type=MESH)
            for i, o_ref in enumerate(out_refs)]


def _halves_start(in_refs, out_refs, send_sems, recv_sems):
    for cp in _halves_copies(in_refs, out_refs, send_sems, recv_sems):
        cp.start()


def _halves_finish(in_refs, out_refs, send_sems, recv_sems):
    for cp in _halves_copies(in_refs, out_refs, send_sems, recv_sems):
        cp.wait()


def _halves_comm(arrays):
    shapes = [jax.ShapeDtypeStruct(a.shape, a.dtype) for a in arrays]
    return _Comm(arrays, shapes, {i: i for i in range(len(arrays))}, len(arrays), len(arrays), _halves_start, _halves_finish)


class _SemSlice:
    class _At:
        def __init__(self, sems, first):
            self.sems, self.first = sems, first

        def __getitem__(self, k):
            return self.sems.at[self.first + k]

    def __init__(self, sems, first):
        self.at = _SemSlice._At(sems, first)


def _merge(comms):
    inputs = [a for c in comms for a in c.inputs]
    shapes = [s for c in comms for s in c.out_shapes]
    aliases, spans = {}, []
    i0 = o0 = s0 = r0 = 0
    for c in comms:
        aliases.update({i0 + i: o0 + o for i, o in c.aliases.items()})
        spans.append((slice(i0, i0 + len(c.inputs)), slice(o0, o0 + len(c.out_shapes)), s0, r0))
        i0, o0, s0, r0 = i0 + len(c.inputs), o0 + len(c.out_shapes), s0 + c.n_send, r0 + c.n_recv

    def start(ins, outs, ss, rs):
        for c, (i, o, s, r) in zip(comms, spans):
            c.start(ins[i], outs[o], _SemSlice(ss, s), _SemSlice(rs, r))

    def finish(ins, outs, ss, rs):
        for c, (i, o, s, r) in zip(comms, spans):
            if c.relay is not None:
                c.relay(ins[i], outs[o], _SemSlice(ss, s), _SemSlice(rs, r))
            c.finish(ins[i], outs[o], _SemSlice(ss, s), _SemSlice(rs, r))

    return _Comm(inputs, shapes, aliases, s0, r0, start, finish)


def _comm_alone(comm, name):
    n_ci = len(comm.inputs)
    hbm = pl.BlockSpec(memory_space=pl.ANY)

    def body(*refs):
        c_ins, c_outs, send_sems, recv_sems = refs[:n_ci], refs[n_ci:-2], refs[-2], refs[-1]
        comm.start(c_ins, c_outs, send_sems, recv_sems)
        if comm.relay is not None:
            comm.relay(c_ins, c_outs, send_sems, recv_sems)
        comm.finish(c_ins, c_outs, send_sems, recv_sems)

    return pl.pallas_call(
        body, name=name, out_shape=comm.out_shapes, in_specs=[hbm] * n_ci, out_specs=[hbm] * len(comm.out_shapes),
        input_output_aliases=comm.aliases,
        scratch_shapes=[pltpu.SemaphoreType.DMA((comm.n_send,)), pltpu.SemaphoreType.DMA((comm.n_recv,))],
    )(*comm.inputs)


SUM_TILES = (704, 512, 384, 320, 256, 192, 128, 64)


def _pair_sum(p, r1, kc_idx, name):
    _, _, n, c = p.shape
    tr = _pick(n, SUM_TILES)

    def body(s_ref, p_ref, r_ref, o32_ref, o16_ref):
        v = p_ref[...] + r_ref[...]
        o16_ref[...] = v.astype(BF16)

        @pl.when(pl.program_id(1) == s_ref[0])
        def _():
            o32_ref[...] = v

    blk = pl.BlockSpec((None, tr, c), lambda i, j, s: (j, i, 0))
    grid_spec = pltpu.PrefetchScalarGridSpec(
        num_scalar_prefetch=1, grid=(n // tr, 4),
        in_specs=[pl.BlockSpec((None, None, tr, c), lambda i, j, s: (s[1], j, i, 0)), blk],
        out_specs=[pl.BlockSpec((tr, c), lambda i, j, s: (i, 0)), blk])
    return pl.pallas_call(
        body, name=name, grid_spec=grid_spec,
        out_shape=[jax.ShapeDtypeStruct((n, c), F32), jax.ShapeDtypeStruct((4, n, c), BF16)],
        compiler_params=_params(("arbitrary", "arbitrary")),
    )(kc_idx, p, r1)


def _owner_sum(a32, r2, kc_idx, name):
    r, c = a32.shape
    tr = _pick(r, SUM_TILES)

    def body(s_ref, a_ref, r_ref, o_ref):
        v = a_ref[...]
        for j in range(3):
            v = v + r_ref[j].astype(F32)
        o_ref[...] = v

    grid_spec = pltpu.PrefetchScalarGridSpec(
        num_scalar_prefetch=1, grid=(r // tr,),
        in_specs=[pl.BlockSpec((tr, c), lambda i, s: (i, 0)),
                  pl.BlockSpec((3, tr, c), lambda i, s: (0, i, 0))],
        out_specs=pl.BlockSpec((None, tr, c), lambda i, s: (s[1], i, 0)))
    return pl.pallas_call(
        body, name=name, grid_spec=grid_spec,
        out_shape=jax.ShapeDtypeStruct((2, r, c), F32),
        compiler_params=_params(("arbitrary",)),
    )(kc_idx, a32, r2)


def _pack_local_half(w_in_s, w_out_s, w_up_s, w_down_s, c_idx):
    parts, row = [], 0
    for (kind, l), off in sorted(PACK_OFF.items(), key=lambda kv: kv[1]):
        if off > row:
            parts.append(jnp.zeros((off - row, 1024), BF16))
        if kind == "up":
            p = lax.dynamic_slice_in_dim(w_up_s[l], c_idx * 512, 512, 0)
        elif kind == "down":
            p = lax.dynamic_slice_in_dim(w_down_s[l], c_idx * 512, 512, 0)
        elif kind == "in":
            p = lax.dynamic_slice_in_dim(w_in_s[l], c_idx * 512, 512, 0)
            p = p.reshape(2, 256, IN_PIECE_COLS).transpose(1, 0, 2).reshape(256, 2 * IN_PIECE_COLS)
            p = jnp.pad(p, ((0, 0), (0, 1024 - 2 * IN_PIECE_COLS)))
        else:
            p = lax.dynamic_slice_in_dim(w_out_s[l], c_idx * 128, 128, 0)
        parts.append(p.astype(BF16))
        row = off + PACK_HEIGHT[kind]
    return jnp.concatenate(parts, axis=0)


def _unpack_in_pieces(w_ref, own_ref, w_scr):
    if own_ref is not None:
        me = 4 * lax.axis_index("x") + 2 * lax.axis_index("y") + lax.axis_index("c")
    for d in range(N_DEV):
        k, c = d // 2, d % 2
        for t in range(2):
            piece = w_ref[d, :, t * IN_PIECE_COLS:(t + 1) * IN_PIECE_COLS]
            if own_ref is not None:
                piece = jnp.where(me == d, own_ref[:, t * IN_PIECE_COLS:(t + 1) * IN_PIECE_COLS], piece)
            w_scr[c * 512 + t * 256:c * 512 + (t + 1) * 256, k * IN_PIECE_COLS:(k + 1) * IN_PIECE_COLS] = piece


def _in_weight_operands(wg):
    specs, args = [_gathered_spec(wg, "in")], [wg["in"][0]]
    if "in_own" in wg:
        own, off = wg["in_own"]
        h = PACK_HEIGHT["in"]
        assert off % h == 0
        specs.append(pl.BlockSpec((h, 1024), lambda *_: (off // h, 0), pipeline_mode=pl.Buffered(1)))
        args.append(own)
    return specs, args


class _Rows:
    def __init__(self, nb, seq, ctx):
        self.nb, self.seq, self.ctx = nb, seq, ctx
        self.n_lat, self.n_ctx = nb * seq, nb * ctx
        self.rows = self.n_lat + self.n_ctx
        self.tm = _pick(np.gcd(seq, self.n_ctx), (512, 256, 128))
        self.tiles_per_ex = seq // self.tm
        self.n_tiles = self.rows // self.tm
        self.n_lat_tiles = self.n_lat // self.tm
        self.groups = nb + 1

    def latent_only(self):
        rt = _Rows(self.nb, self.seq, self.ctx)
        rt.n_tiles = self.n_lat_tiles
        return rt

    def group(self, i):
        return jnp.minimum(i // self.tiles_per_ex, self.nb)

    def first_of_group(self, i):
        return jnp.logical_and(i % self.tiles_per_ex == 0, i <= self.n_lat_tiles)


def _mod_spec(rt):
    return pl.BlockSpec((1, N_MOD, D_MODEL), lambda i: (rt.group(i), 0, 0))


def _row_spec(rt, cols):
    return pl.BlockSpec((rt.tm, cols), lambda i: (i, 0))


def _vec_spec(cols):
    return pl.BlockSpec((1, cols), lambda i: (0, 0))


def _group_spec(rt):
    return pl.BlockSpec((1, 1, D_MODEL), lambda i: (rt.group(i), 0, 0))


def _gathered_spec(wg, kind):
    h, off = PACK_HEIGHT[kind], wg[kind][1]
    assert off % h == 0, (kind, off)
    return pl.BlockSpec((N_DEV, h, 1024), lambda *_: (0, off // h, 0), pipeline_mode=pl.Buffered(1))


def _group_shape(rt):
    return jax.ShapeDtypeStruct((rt.groups, 1, D_MODEL), F32)


def _vec_shape(cols=D_MODEL):
    return jax.ShapeDtypeStruct((1, cols), F32)


def _rms_inv(v):
    return lax.rsqrt(jnp.mean(v * v, axis=-1, keepdims=True) + EPS)


def _norm_mod_val(h_, g_, mod_ref, i_shift, i_scale):
    n = h_ * _rms_inv(h_) * g_
    return n * (1.0 + mod_ref[0, i_scale:i_scale + 1, :]) + mod_ref[0, i_shift:i_shift + 1, :]


def _post_norm_val(h_, z_, g_, mod_ref, i_gate):
    return h_ + mod_ref[0, i_gate:i_gate + 1, :] * (z_ * _rms_inv(z_) * g_)


def _post_norm_bwd_val(dh_, z_, g_, gate):
    rinv = _rms_inv(z_)
    n0 = z_ * rinv
    dn = dh_ * gate * g_
    dz = rinv * (dn - n0 * jnp.mean(dn * n0, axis=-1, keepdims=True))
    return dz, jnp.sum(dh_ * n0 * g_, axis=0, keepdims=True), jnp.sum(dh_ * gate * n0, axis=0, keepdims=True)


def _norm_mod_bwd_val(du_, h_, g_, one_sc):
    rinv = _rms_inv(h_)
    n0 = h_ * rinv
    dn = du_ * g_ * one_sc
    dh = rinv * (dn - n0 * jnp.mean(dn * n0, axis=-1, keepdims=True))
    return (dh, jnp.sum(du_, axis=0, keepdims=True), jnp.sum(du_ * n0 * g_, axis=0, keepdims=True),
            jnp.sum(du_ * one_sc * n0, axis=0, keepdims=True))


def _accumulate(rt, i, group_pairs, global_pairs):
    @pl.when(rt.first_of_group(i))
    def _():
        for ref, _ in group_pairs:
            ref[...] = jnp.zeros_like(ref)

    @pl.when(i == 0)
    def _():
        for ref, _ in global_pairs:
            ref[...] = jnp.zeros_like(ref)

    for ref, val in group_pairs:
        ref[0] += val
    for ref, val in global_pairs:
        ref[...] += val


def _rope_tables(rt):
    pos = np.arange(rt.seq)
    axis_dim = HEAD_DIM // 2
    inv = (ROPE_THETA ** (-np.arange(0, axis_dim, 2, dtype=np.float32) / axis_dim)).astype(np.float32)
    ang_r = (pos // GRID_W).astype(np.float32)[:, None] * inv[None, :]
    ang_c = (pos % GRID_W).astype(np.float32)[:, None] * inv[None, :]
    cr, sr, cc, sc = np.cos(ang_r), np.sin(ang_r), np.cos(ang_c), np.sin(ang_c)
    zero = np.zeros_like(sr)
    cos = np.concatenate([cr, cr, cc, cc], axis=1)
    s_lo = np.concatenate([zero, sr, zero, sc], axis=1)
    s_hi = np.concatenate([-sr, zero, -sc, zero], axis=1)

    def full(t, ctx_value):
        return jnp.asarray(np.concatenate([np.tile(t, (1, 2)), np.full((rt.tm, 128), ctx_value)], axis=0), F32)

    return full(cos, 1.0), full(s_lo, 0.0), full(s_hi, 0.0)


def _table_spec(rt):
    return pl.BlockSpec((rt.tm, 128), lambda i: (jnp.where(i < rt.n_lat_tiles, i % rt.tiles_per_ex, rt.tiles_per_ex), 0))


def _head_mean(x):
    r = lax.broadcasted_iota(jnp.int32, (128, 128), 0) // HEAD_DIM
    c = lax.broadcasted_iota(jnp.int32, (128, 128), 1) // HEAD_DIM
    ones = jnp.where(r == c, 1.0 / HEAD_DIM, 0.0).astype(F32)
    return jnp.dot(x, ones, preferred_element_type=F32, precision=lax.Precision.HIGH)


def _head_stats(t):
    return lax.rsqrt(_head_mean(t * t) + EPS)


def _prep_fwd_body(tm, qkv_ref, c, s1, s2, qn, kn, out_ref):
    def rope(t):
        return t * c + pltpu.roll(t, 16, 1) * s1 + pltpu.roll(t, 112, 1) * s2

    for j in range(12):
        t = qkv_ref[:, j * 128:(j + 1) * 128]
        if j < 4:
            t = rope(t * _head_stats(t) * qn) * Q_SCALE
        elif j == COL_KA:
            t = rope(t * _head_stats(t) * kn)
        elif 6 <= j < 10:
            t = rope(t) * Q_SCALE
        elif j == COL_KB:
            t = rope(t)
        out_ref[:, j * 128:(j + 1) * 128] = t.astype(BF16)


def _prep_bwd_body(dq, dkv, qkv_ref, c, s1, s2, qn, kn, out_ref):
    rows = slice(None)

    def rope_bwd(d):
        return d * c + pltpu.roll(d * s1, 112, 1) + pltpu.roll(d * s2, 16, 1)

    def norm_bwd(t, g, dy):
        rinv = _head_stats(t)
        n = t * rinv
        dn = dy * g
        return rinv * (dn - n * _head_mean(dn * n)), jnp.sum(dy * n, axis=0, keepdims=True)

    dqn = jnp.zeros((1, 128), F32)
    dkn = jnp.zeros((1, 128), F32)
    for j in range(12):
        if j < 4:
            d, dg = norm_bwd(qkv_ref[rows, j * 128:(j + 1) * 128], qn, rope_bwd(dq(slice(j * 128, (j + 1) * 128)) * Q_SCALE))
            dqn = dqn + dg
        elif j == COL_KA:
            d, dg = norm_bwd(qkv_ref[rows, j * 128:(j + 1) * 128], kn, rope_bwd(dkv(slice(0, 128))))
            dkn = dkn + dg
        elif j == COL_VA:
            d = dkv(slice(128, 256))
        elif j < 10:
            d = rope_bwd(dq(slice((j - 2) * 128, (j - 1) * 128)) * Q_SCALE)
        elif j == COL_KB:
            d = rope_bwd(dkv(slice(256, 384)))
        else:
            d = dkv(slice(384, 512))
        out_ref[rows, j * 128:(j + 1) * 128] = d.astype(BF16)
    return dqn, dkn


def _in_fwd(rt, h, gamma, mod, wg, tables, qn, kn, name):
    w_specs, w_args = _in_weight_operands(wg)
    n_w = len(w_args)
    joined = not isinstance(h, (tuple, list))
    n_h = 1 if joined else 2

    def body(*refs):
        g_ref, mod_ref = refs[n_h:n_h + 2]
        rest = refs[n_h + 2:]
        c_ref, s1_ref, s2_ref, qn_ref, kn_ref, u_ref, qkn_ref, qkvp_ref = rest[n_w:n_w + 8]
        qkv_ref, w_scr = rest[-2:]
        i = pl.program_id(0)

        @pl.when(i == 0)
        def _():
            _unpack_in_pieces(rest[0], rest[1] if n_w == 2 else None, w_scr)

        if joined:
            h_ = refs[0][...]
        else:
            h_ = jnp.where(i < rt.n_lat_tiles, refs[0][...], refs[1][...])
            rest[n_w + 8][...] = h_
        u = _norm_mod_val(h_, g_ref[...], mod_ref, 0, 1).astype(BF16)
        u_ref[...] = u
        qkv_ref[...] = jnp.dot(u, w_scr[...], preferred_element_type=F32)
        qkn_ref[...] = qkv_ref[:, 0:NORMED_COLS]
        _prep_fwd_body(rt.tm, qkv_ref, c_ref[...], s1_ref[...], s2_ref[...], qn_ref[...], kn_ref[...], qkvp_ref)

    if joined:
        h_specs, h_args = [_row_spec(rt, D_MODEL)], [h]
    else:
        h_specs = [pl.BlockSpec((rt.tm, D_MODEL), lambda i: (jnp.minimum(i, rt.n_lat_tiles - 1), 0)),
                   pl.BlockSpec((rt.tm, D_MODEL), lambda i: (jnp.maximum(i - rt.n_lat_tiles, 0), 0))]
        h_args = list(h)
    out_specs = [_row_spec(rt, D_MODEL), _row_spec(rt, NORMED_COLS), _row_spec(rt, IN_COLS)]
    out_shape = [jax.ShapeDtypeStruct((rt.rows, D_MODEL), BF16), jax.ShapeDtypeStruct((rt.rows, NORMED_COLS), F32),
                 jax.ShapeDtypeStruct((rt.rows, IN_COLS), BF16)]
    if not joined:
        out_specs.append(_row_spec(rt, D_MODEL))
        out_shape.append(jax.ShapeDtypeStruct((rt.rows, D_MODEL), F32))
    return pl.pallas_call(
        body, name=name, grid=(rt.n_tiles,),
        in_specs=h_specs + [_vec_spec(D_MODEL), _mod_spec(rt)] + w_specs + [_table_spec(rt)] * 3 + [_vec_spec(128)] * 2,
        out_specs=out_specs, out_shape=out_shape,
        scratch_shapes=[pltpu.VMEM((rt.tm, IN_COLS), F32), pltpu.VMEM((D_MODEL, IN_COLS), BF16)],
        compiler_params=_params(("arbitrary",)),
    )(*h_args, gamma, mod, *w_args, *tables, qn, kn)


def _in_bwd(rt, dq, dkv, qkv, tables, qn, kn, wg, h, dres, mod, gamma, latent_only, name, comm=None, dead_ctx_dkv=None):
    last = rt.n_lat_tiles - 1
    w_specs, w_args = _in_weight_operands(wg)
    n_w = len(w_args)
    n_dead = 0 if dead_ctx_dkv is None else 1

    def body(dq_ref, dkv_ref, qkv_ref, c_ref, s1_ref, s2_ref, qn_ref, kn_ref, *rest):
        h_ref, dres_ref, mod_ref, g_ref, dqkv_ref, dh_ref, dqn_ref, dkn_ref, dsh_ref, dsc_ref, dg_ref, w_scr = rest[n_w + n_dead:]
        i = pl.program_id(0)

        @pl.when(i == 0)
        def _():
            _unpack_in_pieces(rest[0], rest[1] if n_w == 2 else None, w_scr)

        if n_dead:
            c1_ref, lat = rest[n_w], i <= last
            load_dq = lambda cols: jnp.where(lat, dq_ref[:, cols], 0.0)
            load_dkv = lambda cols: jnp.where(lat, dkv_ref[:, cols], c1_ref[:, cols])
            dres_ = jnp.where(lat, dres_ref[...], 0.0)
        else:
            load_dq, load_dkv, dres_ = (lambda cols: dq_ref[:, cols]), (lambda cols: dkv_ref[:, cols]), dres_ref[...]
        dqn, dkn = _prep_bwd_body(load_dq, load_dkv, qkv_ref, c_ref[...], s1_ref[...], s2_ref[...], qn_ref[...], kn_ref[...], dqkv_ref)
        du = lax.dot_general(dqkv_ref[...], w_scr[...], NT, preferred_element_type=F32)
        dh, dsh, dsc, dg = _norm_mod_bwd_val(du, h_ref[...], g_ref[...], 1.0 + mod_ref[0, 1:2, :])
        if latent_only:
            @pl.when(i <= last)
            def _():
                dh_ref[...] = dres_ + dh
        else:
            dh_ref[...] = dres_ + dh
        _accumulate(rt, i, [(dsh_ref, dsh), (dsc_ref, dsc)], [(dg_ref, dg), (dqn_ref, dqn), (dkn_ref, dkn)])

    dh_spec = pl.BlockSpec((rt.tm, D_MODEL), lambda i: (jnp.minimum(i, last), 0)) if latent_only else _row_spec(rt, D_MODEL)
    dead_specs = [] if dead_ctx_dkv is None else [pl.BlockSpec((rt.tm, 512), lambda i: (jnp.maximum(i - rt.n_lat_tiles, 0), 0))]
    dead_args = [] if dead_ctx_dkv is None else [dead_ctx_dkv]
    return _comm_call(
        body, comm, name=name, grid=(rt.n_tiles,),
        in_specs=[_row_spec(rt, 1024), _row_spec(rt, 512), _row_spec(rt, NORMED_COLS)] + [_table_spec(rt)] * 3 + [_vec_spec(128)] * 2
        + w_specs + dead_specs + [_row_spec(rt, D_MODEL), _row_spec(rt, D_MODEL), _mod_spec(rt), _vec_spec(D_MODEL)],
        out_specs=[_row_spec(rt, IN_COLS), dh_spec, _vec_spec(128), _vec_spec(128),
                   _group_spec(rt), _group_spec(rt), _vec_spec(D_MODEL)],
        out_shape=[jax.ShapeDtypeStruct((rt.rows, IN_COLS), BF16),
                   jax.ShapeDtypeStruct((rt.n_lat if latent_only else rt.rows, D_MODEL), F32),
                   _vec_shape(128), _vec_shape(128), _group_shape(rt), _group_shape(rt), _vec_shape()],
        args=[dq, dkv, qkv, *tables, qn, kn, *w_args, *dead_args, h, dres, mod, gamma], aliases={}, semantics=("arbitrary",),
        scratch=[pltpu.VMEM((D_MODEL, IN_COLS), BF16)])


def _out_fwd(rt, o, wg, h, mod, g_post_mix, g_pre_mlp, name):
    def body(o_ref, w_ref, h_ref, mod_ref, gpost_ref, gpre_ref, mix_ref, h1_ref, u2_ref):
        mix = jnp.dot(o_ref[...], w_ref[...].reshape(D_MODEL, D_MODEL), preferred_element_type=F32)
        mix_ref[...] = mix
        h1 = _post_norm_val(h_ref[...], mix, gpost_ref[...], mod_ref, 2)
        h1_ref[...] = h1
        u2_ref[...] = _norm_mod_val(h1, gpre_ref[...], mod_ref, 3, 4).astype(BF16)

    return pl.pallas_call(
        body, name=name, grid=(rt.n_tiles,),
        in_specs=[_row_spec(rt, D_MODEL), _gathered_spec(wg, "out"), _row_spec(rt, D_MODEL), _mod_spec(rt),
                  _vec_spec(D_MODEL), _vec_spec(D_MODEL)],
        out_specs=[_row_spec(rt, D_MODEL)] * 3,
        out_shape=[jax.ShapeDtypeStruct((rt.rows, D_MODEL), F32), jax.ShapeDtypeStruct((rt.rows, D_MODEL), F32),
                   jax.ShapeDtypeStruct((rt.rows, D_MODEL), BF16)],
        compiler_params=_params(("parallel",)),
    )(o, wg["out"][0], h, mod, g_post_mix, g_pre_mlp)


def _out_bwd(rt, dh1, mix, wg, mod, g_post_mix, name, comm=None):
    def body(dh_ref, mix_ref, w_ref, mod_ref, g_ref, dmix_ref, do_ref, dgate_ref, dg_ref):
        i = pl.program_id(0)
        dz, dgate, dg = _post_norm_bwd_val(dh_ref[...], mix_ref[...], g_ref[...], mod_ref[0, 2:3, :])
        dzb = dz.astype(BF16)
        dmix_ref[...] = dzb
        do_ref[...] = lax.dot_general(dzb, w_ref[...].reshape(D_MODEL, D_MODEL), NT, preferred_element_type=F32).astype(BF16)
        _accumulate(rt, i, [(dgate_ref, dgate)], [(dg_ref, dg)])

    return _comm_call(
        body, comm, name=name, grid=(rt.n_tiles,),
        in_specs=[_row_spec(rt, D_MODEL), _row_spec(rt, D_MODEL), _gathered_spec(wg, "out"), _mod_spec(rt), _vec_spec(D_MODEL)],
        out_specs=[_row_spec(rt, D_MODEL), _row_spec(rt, D_MODEL), _group_spec(rt), _vec_spec(D_MODEL)],
        out_shape=[jax.ShapeDtypeStruct((rt.rows, D_MODEL), BF16), jax.ShapeDtypeStruct((rt.rows, D_MODEL), BF16),
                   _group_shape(rt), _vec_shape()],
        args=[dh1, mix, wg["out"][0], mod, g_post_mix], aliases={}, semantics=("arbitrary",))


def _w_chunk(w_ref, k):
    return w_ref[2 * k:2 * k + 2].reshape(1024, 1024)


def _mlp_fwd(rt, u2, h1, wg, mod, g_post_mlp, name, comm=None, target=None):
    last = rt.n_lat_tiles - 1

    def body(u2_ref, h1_ref, wu_ref, wd_ref, mod_ref, g_ref, *rest):
        u2_ = u2_ref[...]
        y = jnp.zeros((rt.tm, D_MODEL), F32)
        for k in range(D_FF // 1024):
            a = jnp.maximum(jnp.dot(u2_, _w_chunk(wu_ref, k), preferred_element_type=F32), 0.0)
            rest[-3 if target is None else -4][:, k * 1024:(k + 1) * 1024] = a.astype(BF16)
            y = y + jnp.dot((a * a).astype(BF16), _w_chunk(wd_ref, k), preferred_element_type=F32)
        h2 = _post_norm_val(h1_ref[...], y, g_ref[...], mod_ref, 5)
        if target is None:
            _, y_ref, h2_ref = rest
            y_ref[...] = y
            h2_ref[...] = h2
        else:
            t_ref, _, y_ref, dh_ref, sq_ref = rest
            y_ref[...] = y
            i = pl.program_id(0)

            @pl.when(i == 0)
            def _():
                sq_ref[...] = jnp.zeros_like(sq_ref)

            @pl.when(i <= last)
            def _():
                e = h2 - t_ref[...]
                dh_ref[...] = e * (1.0 / D_MODEL)
                sq_ref[...] += jnp.sum(e * e, axis=0, keepdims=True)

            @pl.when(i > last)
            def _():
                dh_ref[...] = jnp.zeros_like(dh_ref)

    in_specs = [_row_spec(rt, D_MODEL), _row_spec(rt, D_MODEL), _gathered_spec(wg, "up"), _gathered_spec(wg, "down"),
                _mod_spec(rt), _vec_spec(D_MODEL)]
    args = [u2, h1, wg["up"][0], wg["down"][0], mod, g_post_mlp]
    out_specs = [_row_spec(rt, D_FF), _row_spec(rt, D_MODEL), _row_spec(rt, D_MODEL)]
    out_shape = [jax.ShapeDtypeStruct((rt.rows, D_FF), BF16), jax.ShapeDtypeStruct((rt.rows, D_MODEL), F32),
                 jax.ShapeDtypeStruct((rt.rows, D_MODEL), F32)]
    if target is not None:
        in_specs.append(pl.BlockSpec((rt.tm, D_MODEL), lambda i: (jnp.minimum(i, last), 0)))
        args.append(target)
        out_specs.append(_vec_spec(D_MODEL))
        out_shape.append(_vec_shape())
    return _comm_call(body, comm, name=name, grid=(rt.n_tiles,), in_specs=in_specs, out_specs=out_specs, out_shape=out_shape,
                      args=args, aliases={}, semantics=("parallel",) if target is None else ("arbitrary",))


def _mlp_down_bwd(rt, dh, y, ra, wg, mod, g_post_mlp, name, comm=None):
    def body(dh_ref, y_ref, ra_ref, wd_ref, mod_ref, g_ref, dy_ref, da_ref, dgate_ref, dg_ref):
        i = pl.program_id(0)
        dz, dgate, dg = _post_norm_bwd_val(dh_ref[...], y_ref[...], g_ref[...], mod_ref[0, 5:6, :])
        dyb = dz.astype(BF16)
        dy_ref[...] = dyb
        for k in range(D_FF // 1024):
            dr = lax.dot_general(dyb, _w_chunk(wd_ref, k), NT, preferred_element_type=F32)
            da_ref[:, k * 1024:(k + 1) * 1024] = (dr * (2.0 * ra_ref[:, k * 1024:(k + 1) * 1024].astype(F32))).astype(BF16)
        _accumulate(rt, i, [(dgate_ref, dgate)], [(dg_ref, dg)])

    return _comm_call(
        body, comm, name=name, grid=(rt.n_tiles,),
        in_specs=[_row_spec(rt, D_MODEL), _row_spec(rt, D_MODEL), _row_spec(rt, D_FF), _gathered_spec(wg, "down"),
                  _mod_spec(rt), _vec_spec(D_MODEL)],
        out_specs=[_row_spec(rt, D_MODEL), _row_spec(rt, D_FF), _group_spec(rt), _vec_spec(D_MODEL)],
        out_shape=[jax.ShapeDtypeStruct((rt.rows, D_MODEL), BF16), jax.ShapeDtypeStruct((rt.rows, D_FF), BF16),
                   _group_shape(rt), _vec_shape()],
        args=[dh, y, ra, wg["down"][0], mod, g_post_mlp], aliases={}, semantics=("arbitrary",))


def _mlp_up_bwd(rt, da, wg, h1, dh, mod, g_pre_mlp, name):
    def body(da_ref, wu_ref, h1_ref, dh_ref, mod_ref, g_ref, dh1_ref, dsh_ref, dsc_ref, dg_ref):
        i = pl.program_id(0)
        du = jnp.zeros((rt.tm, D_MODEL), F32)
        for k in range(D_FF // 1024):
            du = du + lax.dot_general(da_ref[:, k * 1024:(k + 1) * 1024], _w_chunk(wu_ref, k), NT, preferred_element_type=F32)
        d, dsh, dsc, dg = _norm_mod_bwd_val(du, h1_ref[...], g_ref[...], 1.0 + mod_ref[0, 4:5, :])
        dh1_ref[...] = dh_ref[...] + d
        _accumulate(rt, i, [(dsh_ref, dsh), (dsc_ref, dsc)], [(dg_ref, dg)])

    return pl.pallas_call(
        body, name=name, grid=(rt.n_tiles,),
        in_specs=[_row_spec(rt, D_FF), _gathered_spec(wg, "up"), _row_spec(rt, D_MODEL), _row_spec(rt, D_MODEL),
                  _mod_spec(rt), _vec_spec(D_MODEL)],
        out_specs=[_row_spec(rt, D_MODEL), _group_spec(rt), _group_spec(rt), _vec_spec(D_MODEL)],
        out_shape=[jax.ShapeDtypeStruct((rt.rows, D_MODEL), F32), _group_shape(rt), _group_shape(rt), _vec_shape()],
        compiler_params=_params(("arbitrary",)),
    )(da, wg["up"][0], h1, dh, mod, g_pre_mlp)


def _wgrad_packed(rt, a, b, kind, off, n_rows, p_prev, name, comm=None):
    h = PACK_HEIGHT[kind]
    tk = rt.tm
    assert off % h == 0, (kind, off)

    def body(a_ref, b_ref, *rest):
        o_ref = rest[-1]
        i = pl.program_id(0)

        @pl.when(i == 0)
        def _():
            o_ref[...] = jnp.zeros_like(o_ref)

        if kind == "in":
            res = lax.dot_general(a_ref[...], b_ref[...], TN, preferred_element_type=F32)
            for k in range(4):
                for c in range(2):
                    for t in range(2):
                        o_ref[c, k, :, t * IN_PIECE_COLS:(t + 1) * IN_PIECE_COLS] += \
                            res[c * 512 + t * h:c * 512 + (t + 1) * h, k * IN_PIECE_COLS:(k + 1) * IN_PIECE_COLS]
        elif kind == "out":
            res = lax.dot_general(a_ref[...], b_ref[...], TN, preferred_element_type=F32)
            for k in range(4):
                for c in range(2):
                    o_ref[c, k] += res[(2 * k + c) * h:(2 * k + c + 1) * h]
        else:
            for k in range(4):
                if kind == "up":
                    res = lax.dot_general(a_ref[...], b_ref[:, k * 1024:(k + 1) * 1024], TN, preferred_element_type=F32)
                else:
                    ra = a_ref[:, k * 1024:(k + 1) * 1024].astype(F32)
                    res = lax.dot_general((ra * ra).astype(BF16), b_ref[...], TN, preferred_element_type=F32)
                o_ref[0, k] += res[0:h]
                o_ref[1, k] += res[h:2 * h]

    in_specs = [pl.BlockSpec((tk, a.shape[1]), lambda i: (i, 0)), pl.BlockSpec((tk, b.shape[1]), lambda i: (i, 0))]
    args = [a, b]
    aliases = {}
    if p_prev is not None:
        in_specs.append(pl.BlockSpec(memory_space=pl.ANY))
        args.append(p_prev)
        aliases = {2: 0}
    outs = _comm_call(
        body, comm, name=name, grid=(rt.n_tiles,),
        in_specs=in_specs,
        out_specs=[pl.BlockSpec((2, 4, h, 1024), lambda i: (0, 0, off // h, 0))],
        out_shape=[jax.ShapeDtypeStruct((2, 4, n_rows, 1024), F32)],
        args=args, aliases=aliases, semantics=("arbitrary",))
    return outs[0] if comm is None else outs


def _ada_wgrad(xs, dm, name):
    depth, _, cols = dm.shape

    def body(x_ref, d_ref, o_ref):
        for l in range(depth):
            o_ref[l] = lax.dot_general(x_ref[...], d_ref[l], TN, preferred_element_type=F32)

    return pl.pallas_call(body, name=name, out_shape=jax.ShapeDtypeStruct((depth, xs.shape[1], cols), F32),
                          compiler_params=pltpu.CompilerParams(vmem_limit_bytes=VMEM_LIMIT))(xs, dm)


def _stack_heads(x, kvi):
    x = x.astype(F32)
    tq = x.shape[0]
    lane = lax.broadcasted_iota(jnp.int32, (tq, 128), 1)
    keep = lane < HEAD_DIM if kvi == 0 else lane >= HEAD_DIM
    parts = []
    for p in range(2):
        pair = x[:, p * 128:(p + 1) * 128]
        swapped = pltpu.roll(pair, HEAD_DIM, 1)
        lo_head, hi_head = (pair, swapped) if kvi == 0 else (swapped, pair)
        parts += [jnp.where(keep, lo_head, 0.0), jnp.where(keep, hi_head, 0.0)]
    return jnp.concatenate(parts, axis=0).astype(BF16)


def _unstack_heads(o4, kvi):
    tq = o4.shape[0] // GROUP
    lane = lax.broadcasted_iota(jnp.int32, (tq, 128), 1)
    outs = []
    for p in range(2):
        r_lo, r_hi = o4[(2 * p) * tq:(2 * p + 1) * tq], o4[(2 * p + 1) * tq:(2 * p + 2) * tq]
        if kvi == 0:
            lo, hi = r_lo, pltpu.roll(r_hi, HEAD_DIM, 1)
        else:
            lo, hi = pltpu.roll(r_lo, HEAD_DIM, 1), r_hi
        outs.append(jnp.where(lane < HEAD_DIM, lo, hi))
    return jnp.concatenate(outs, axis=1)


def _per_head(shape, axis, tq, values):
    head = lax.broadcasted_iota(jnp.int32, shape, axis) // tq
    out = jnp.zeros(shape, F32)
    for g in range(GROUP):
        out = jnp.where(head == g, values[g], out)
    return out


KEY_CHUNK = 512
Q_TILE = 128
Q_TILE_FWD = 256


def _key_chunks(k_ref, v_ref, n, kc=KEY_CHUNK):
    kc = min(kc, n)
    return [(k_ref[c * kc:(c + 1) * kc, :], v_ref[c * kc:(c + 1) * kc, :], None) for c in range(n // kc)]


def _softmax_fwd(qs, chunks, sink_col):
    logits = []
    for k, _, mask in chunks:
        s = lax.dot_general(qs, k, NT, preferred_element_type=F32)
        logits.append(s if mask is None else jnp.where(mask, s, NEG_BIG))
    m = functools.reduce(jnp.maximum, [jnp.max(s, axis=1, keepdims=True) for s in logits])
    if sink_col is not None:
        m = jnp.maximum(m, sink_col)
    l = jnp.zeros_like(m) if sink_col is None else jnp.exp(sink_col - m)
    acc = jnp.zeros((qs.shape[0], 128), F32)
    for s, (_, v, _) in zip(logits, chunks):
        p = jnp.exp(s - m)
        l = l + jnp.sum(p, axis=1, keepdims=True)
        acc = acc + jnp.dot(p.astype(BF16), v, preferred_element_type=F32)
    return acc / l, m + jnp.log(l)


def _to_rows(col):
    return jnp.transpose(jnp.broadcast_to(col, (col.shape[0], 128)))[0:8, :]


def _softmax_bwd(qs, dos, lse_row, delta_row, chunks):
    dq = jnp.zeros((qs.shape[0], 128), F32)
    grads = []
    for k, v, mask in chunks:
        s = lax.dot_general(k, qs, NT, preferred_element_type=F32)
        if mask is not None:
            s = jnp.where(mask, s, NEG_BIG)
        p = jnp.exp(s - lse_row)
        dp = lax.dot_general(v, dos, NT, preferred_element_type=F32)
        ds = (p * (dp - delta_row)).astype(BF16)
        dv = jnp.dot(p.astype(BF16), dos, preferred_element_type=F32)
        dk = jnp.dot(ds, qs, preferred_element_type=F32)
        dq = dq + lax.dot_general(ds, k, TN, preferred_element_type=F32)
        grads.append((dk, dv))
    return dq, grads


def _band(qi, tq, seq):
    span = tq + 2 * WINDOW
    start = pl.multiple_of(jnp.clip(qi * tq - WINDOW, 0, seq - span), 64)
    return start, span


def _band_mask(qi, tq, start, span, query_axis):
    shape = (GROUP * tq, span) if query_axis == 0 else (span, GROUP * tq)
    qpos = qi * tq + lax.broadcasted_iota(jnp.int32, shape, query_axis) % tq
    kpos = start + lax.broadcasted_iota(jnp.int32, shape, 1 - query_axis)
    return jnp.abs(kpos - qpos) <= WINDOW


def _qkv_specs(rt, tq, q_row, ctx_row, with_latent):
    specs = [pl.BlockSpec((tq, 256), functools.partial(lambda b, i, col: (q_row(b, i), col), col=col)) for col in (0, 1, 3, 4)]
    if with_latent:
        specs += [pl.BlockSpec((rt.seq, 128), functools.partial(lambda b, i, col: (b, col), col=col))
                  for col in (COL_KA, COL_VA, COL_KB, COL_VB)]
    specs += [pl.BlockSpec((rt.ctx, 128), functools.partial(lambda b, i, col: (ctx_row(b), col), col=col))
              for col in (COL_KA, COL_VA, COL_KB, COL_VB)]
    return specs


def _attn_fwd(rt, qkvp, sink, o_prev, name, comm=None):
    latent = o_prev is None
    seq, ctx, nb = rt.seq, rt.ctx, rt.nb
    tq = Q_TILE_FWD if latent else ctx
    tile = Q_TILE if latent else ctx
    parts = tq // tile
    nq = seq // tq if latent else 1
    ctx_blk0 = rt.n_lat // ctx
    q_row = (lambda b, i: b * nq + i) if latent else (lambda b, i: ctx_blk0 + b)

    def store_lse(lse_ref, j, lse_col):
        rows = _to_rows(lse_col)
        for part in range(parts):
            lse_ref[part, j] = jnp.concatenate([rows[:, g * tq + part * tile:g * tq + (part + 1) * tile] for g in range(GROUP)], axis=1)

    def body(sink_ref, qa0, qa1, qb0, qb1, *rest):
        if latent:
            kal, val, kbl, vbl, kac, vac, kbc, vbc, o_ref, lse_ref = rest
        else:
            kac, vac, kbc, vbc, _, o_ref, lse_ref = rest
        qi = pl.program_id(1)
        for kvi, (qa, qb) in enumerate(((qa0, qb0), (qa1, qb1))):
            src_a = _key_chunks(kac, vac, ctx)
            src_b = _key_chunks(kbc, vbc, ctx)
            if latent:
                src_a += _key_chunks(kal, val, seq, seq)
                start, span = _band(qi, tq, seq)
                src_b.append((kbl[pl.ds(start, span), :], vbl[pl.ds(start, span), :], _band_mask(qi, tq, start, span, 0)))
            oa, lse = _softmax_fwd(_stack_heads(qa[...], kvi), src_a, None)
            o_ref[:, kvi * 256:(kvi + 1) * 256] = _unstack_heads(oa, kvi).astype(BF16)
            store_lse(lse_ref, kvi, lse)
            sink_col = _per_head((GROUP * tq, 1), 0, tq, [sink_ref[kvi * GROUP + g] for g in range(GROUP)])
            ob, lse = _softmax_fwd(_stack_heads(qb[...], kvi), src_b, sink_col)
            o_ref[:, 512 + kvi * 256:512 + (kvi + 1) * 256] = _unstack_heads(ob, kvi).astype(BF16)
            store_lse(lse_ref, 2 + kvi, lse)

    specs = _qkv_specs(rt, tq, q_row, lambda b: ctx_blk0 + b, latent)
    args = [sink] + [qkvp] * len(specs)
    in_specs = [pl.BlockSpec(memory_space=pltpu.SMEM)] + specs
    aliases = {}
    if not latent:
        in_specs.append(pl.BlockSpec(memory_space=pl.ANY))
        args.append(o_prev)
        aliases = {len(args) - 1: 0}
    return _comm_call(
        body, comm, name=name, grid=(nb, nq),
        in_specs=in_specs,
        out_specs=[pl.BlockSpec((tq, 1024), lambda b, i: (q_row(b, i), 0)),
                   pl.BlockSpec((parts, 4, 8, GROUP * tile), lambda b, i: (b * nq + i, 0, 0, 0))],
        out_shape=[jax.ShapeDtypeStruct((rt.rows, 1024), BF16), jax.ShapeDtypeStruct((nb * nq * parts, 4, 8, GROUP * tile), F32)],
        args=args, aliases=aliases, semantics=("parallel", "parallel"))


def _attn_bwd(rt, qkvp, o, lse, do, sink, prev, name, comm=None):
    latent = prev is None
    seq, ctx, nb = rt.seq, rt.ctx, rt.nb
    tq = Q_TILE if latent else ctx
    nq = seq // tq if latent else 1
    ctx_blk0 = rt.n_lat // ctx
    q_row = (lambda b, i: b * nq + i) if latent else (lambda b, i: ctx_blk0 + b)
    kc = min(KEY_CHUNK, seq)

    def body(sink_ref, qa0, qa1, qb0, qb1, *rest):
        if latent:
            kal, val, kbl, vbl, kac, vac, kbc, vbc, do_ref, o_ref, lse_ref, dq_ref, dl_ref, dc_ref, dsink_ref = rest
        else:
            kac, vac, kbc, vbc, do_ref, o_ref, lse_ref, c1_ref, _, _, dq_ref, dc_ref, dsink_ref = rest
        b, qi = pl.program_id(0), pl.program_id(1)

        def rows_of(cols, kvi, mixer):
            dos = _stack_heads(do_ref[:, cols], kvi)
            delta = jnp.sum(dos.astype(F32) * _stack_heads(o_ref[:, cols], kvi).astype(F32), axis=1, keepdims=True)
            return dos, lse_ref[0, 2 * mixer + kvi, 0:1, :], _to_rows(delta)[0:1, :]

        @pl.when(jnp.logical_and(b == 0, qi == 0))
        def _():
            dsink_ref[...] = jnp.zeros_like(dsink_ref)

        if latent:
            @pl.when(qi == 0)
            def _():
                dc_ref[...] = jnp.zeros_like(dc_ref)
                dl_ref[...] = jnp.zeros_like(dl_ref)
        else:
            dc_ref[...] = c1_ref[...]

        head_row = lax.broadcasted_iota(jnp.int32, (8, 128), 0)
        for kvi, (qa, qb) in enumerate(((qa0, qb0), (qa1, qb1))):
            cols = slice(kvi * 256, (kvi + 1) * 256)
            dos, lse_row, delta_row = rows_of(cols, kvi, 0)
            src = _key_chunks(kac, vac, ctx)
            if latent:
                src += _key_chunks(kal, val, seq)
            dq4, grads = _softmax_bwd(_stack_heads(qa[...], kvi), dos, lse_row, delta_row, src)
            dq_ref[:, cols] = _unstack_heads(dq4, kvi)
            dc_ref[:, 0:128] += grads[0][0]
            dc_ref[:, 128:256] += grads[0][1]
            for c, (dk, dv) in enumerate(grads[1:]):
                dl_ref[c * kc:(c + 1) * kc, 0:128] += dk
                dl_ref[c * kc:(c + 1) * kc, 128:256] += dv
            cols = slice(512 + kvi * 256, 512 + (kvi + 1) * 256)
            dos, lse_row, delta_row = rows_of(cols, kvi, 1)
            src = _key_chunks(kbc, vbc, ctx)
            if latent:
                start, span = _band(qi, tq, seq)
                src.append((kbl[pl.ds(start, span), :], vbl[pl.ds(start, span), :], _band_mask(qi, tq, start, span, 1)))
            dq4, grads = _softmax_bwd(_stack_heads(qb[...], kvi), dos, lse_row, delta_row, src)
            dq_ref[:, cols] = _unstack_heads(dq4, kvi)
            dc_ref[:, 256:384] += grads[0][0]
            dc_ref[:, 384:512] += grads[0][1]
            if latent:
                dl_ref[pl.ds(start, span), 256:384] += grads[1][0]
                dl_ref[pl.ds(start, span), 384:512] += grads[1][1]
            sink_row = _per_head((1, GROUP * tq), 1, tq, [sink_ref[kvi * GROUP + g] for g in range(GROUP)])
            dsink = -jnp.exp(sink_row - lse_row) * delta_row
            head = lax.broadcasted_iota(jnp.int32, (1, GROUP * tq), 1) // tq
            upd = jnp.zeros((8, 128), F32)
            for g in range(GROUP):
                upd = jnp.where(head_row == kvi * GROUP + g, jnp.sum(jnp.where(head == g, dsink, 0.0)), upd)
            dsink_ref[...] += upd

    specs = _qkv_specs(rt, tq, q_row, lambda b: ctx_blk0 + b, latent)
    q_rows_spec = pl.BlockSpec((tq, 1024), lambda b, i: (q_row(b, i), 0))
    in_specs = ([pl.BlockSpec(memory_space=pltpu.SMEM)] + specs
                + [q_rows_spec, q_rows_spec, pl.BlockSpec((1, 4, 8, GROUP * tq), lambda b, i: (b * nq + i, 0, 0, 0))])
    args = [sink] + [qkvp] * len(specs) + [do, o, lse]
    dq_shape = jax.ShapeDtypeStruct((rt.rows, 1024), F32)
    dkv_shape = jax.ShapeDtypeStruct((rt.rows, 512), F32)
    dsink_spec, dsink_shape = pl.BlockSpec((8, 128), lambda b, i: (0, 0)), jax.ShapeDtypeStruct((8, 128), F32)
    dq_spec = pl.BlockSpec((tq, 1024), lambda b, i: (q_row(b, i), 0))
    if latent:
        out_specs = [dq_spec, pl.BlockSpec((seq, 512), lambda b, i: (b, 0)), pl.BlockSpec((ctx, 512), lambda b, i: (b, 0)), dsink_spec]
        out_shape = [dq_shape, dkv_shape, jax.ShapeDtypeStruct((rt.n_ctx, 512), F32), dsink_shape]
        aliases = {}
    else:
        dq_prev, dkv_prev, c1 = prev
        in_specs += [pl.BlockSpec((ctx, 512), lambda b, i: (b, 0)), pl.BlockSpec(memory_space=pl.ANY), pl.BlockSpec(memory_space=pl.ANY)]
        args += [c1, dq_prev, dkv_prev]
        out_specs = [dq_spec, pl.BlockSpec((ctx, 512), lambda b, i: (ctx_blk0 + b, 0)), dsink_spec]
        out_shape = [dq_shape, dkv_shape, dsink_shape]
        aliases = {len(args) - 2: 0, len(args) - 1: 1}
    return _comm_call(body, comm, name=name, grid=(nb, nq), in_specs=in_specs, out_specs=out_specs, out_shape=out_shape,
                      args=args, aliases=aliases, semantics=("arbitrary", "arbitrary"))


def _silu(x):
    return x / (1.0 + jnp.exp(-x))


def _whole(shape):
    return pl.BlockSpec(shape, lambda i, s: (0,) * len(shape))


def _ada_half_spec(cols):
    return pl.BlockSpec((DEPTH, D_MODEL, cols), lambda i, s: (0, 0, s[0]))


def _ada_fwd(cond, w_ada, b_half, c_idx, name):
    rows = cond.shape[0]
    cols = w_ada.shape[2] // 2

    def body(s_ref, c_ref, w_ref, b_ref, x_ref, o_ref):
        xs = _silu(c_ref[...]).astype(BF16)
        x_ref[...] = xs
        for l in range(DEPTH):
            o_ref[l] = jnp.dot(xs, w_ref[l].astype(BF16), preferred_element_type=F32) + b_ref[l]

    grid_spec = pltpu.PrefetchScalarGridSpec(
        num_scalar_prefetch=1, grid=(1,),
        in_specs=[_whole(cond.shape), _ada_half_spec(cols), _whole(b_half.shape)],
        out_specs=[_whole((rows, D_MODEL)), _whole((DEPTH, rows, cols))])
    return pl.pallas_call(
        body, name=name, grid_spec=grid_spec,
        out_shape=[jax.ShapeDtypeStruct((rows, D_MODEL), BF16), jax.ShapeDtypeStruct((DEPTH, rows, cols), F32)],
        compiler_params=_params(("arbitrary",)),
    )(c_idx, cond, w_ada, b_half)


def _ada_cond_bwd(dcx, w_ada, c_idx, name):
    _, rows, cols = dcx.shape

    def body(s_ref, d_ref, w_ref, o_ref):
        acc = jnp.zeros((rows, D_MODEL), F32)
        for l in range(DEPTH):
            acc = acc + lax.dot_general(d_ref[l], w_ref[l].astype(BF16), NT, preferred_element_type=F32)
        o_ref[...] = acc

    grid_spec = pltpu.PrefetchScalarGridSpec(
        num_scalar_prefetch=1, grid=(1,),
        in_specs=[_whole(dcx.shape), _ada_half_spec(cols)], out_specs=_whole((rows, D_MODEL)))
    return pl.pallas_call(body, name=name, grid_spec=grid_spec, out_shape=jax.ShapeDtypeStruct((rows, D_MODEL), F32),
                          compiler_params=_params(("arbitrary",)))(c_idx, dcx, w_ada)


def _dev_sum(x, name):
    _, r, c = x.shape

    def body(x_ref, o_ref):
        v = x_ref[0]
        for d in range(1, N_DEV):
            v = v + x_ref[d]
        o_ref[...] = v

    return pl.pallas_call(body, name=name, out_shape=jax.ShapeDtypeStruct((r, c), F32))(x)


def _adam_val(w, g, m, v):
    c1 = 1.0 / (1.0 - ADAM_B1 ** ADAM_STEP)
    c2 = 1.0 / (1.0 - ADAM_B2 ** ADAM_STEP)
    nm = ADAM_B1 * m + (1.0 - ADAM_B1) * g
    nv = ADAM_B2 * v + (1.0 - ADAM_B2) * (g * g)
    return -ADAM_LR * ((nm * c1) / (jnp.sqrt(nv * c2) + ADAM_EPS) + ADAM_WD * w), nm, nv


def _small_update(tot, dcc_parts, params, n_groups, name):
    n_p = len(params)
    mod_rows = n_groups * N_MOD

    def body(tot_ref, dcc_ref, *refs):
        ins, outs = refs[:3 * n_p], refs[3 * n_p:]

        def update(p, rows, cols, g):
            w_ref, m_ref, v_ref = ins[3 * p:3 * p + 3]
            g_ref, d_ref, nm_ref, nv_ref = outs[4 * p:4 * p + 4]
            d, nm, nv = _adam_val(w_ref[rows, cols], g, m_ref[rows, cols], v_ref[rows, cols])
            g_ref[rows, cols] = g
            d_ref[rows, cols] = d
            nm_ref[rows, cols] = nm
            nv_ref[rows, cols] = nv

        acc = dcc_ref[0, 0:1, :]
        for d in range(1, N_DEV):
            acc = acc + dcc_ref[d, 0:1, :]
        c = ins[0][...]
        sg = 1.0 / (1.0 + jnp.exp(-c))
        update(0, slice(0, 1), slice(None), acc * (sg * (1.0 + c * (1.0 - sg))))
        for l in range(DEPTH):
            for i in range(N_MOD):
                g = tot_ref[l * mod_rows + i:l * mod_rows + i + 1, :]
                for grp in range(1, n_groups):
                    g = g + tot_ref[l * mod_rows + grp * N_MOD + i:l * mod_rows + grp * N_MOD + i + 1, :]
                update(1, slice(l, l + 1), slice(i * D_MODEL, (i + 1) * D_MODEL), g)
            for j in range(4):
                row = DEPTH * mod_rows + 4 * l + j
                update(2 + j, slice(l, l + 1), slice(None), tot_ref[row:row + 1, :])

    shapes = [jax.ShapeDtypeStruct(w.shape, F32) for w, _, _ in params for _ in range(4)]
    outs = pl.pallas_call(body, name=name, out_shape=shapes)(tot, dcc_parts, *[a for p in params for a in p])
    return [tuple(outs[4 * p:4 * p + 4]) for p in range(n_p)]


def _adamw(w, g, m, v, name):
    r, c = w.shape
    tr = _pick(r, (256, 128, 64, 32, 24, 16, 8))

    def body(w_ref, g_ref, m_ref, v_ref, d_ref, nm_ref, nv_ref):
        d_ref[...], nm_ref[...], nv_ref[...] = _adam_val(w_ref[...], g_ref[...], m_ref[...], v_ref[...])

    spec = pl.BlockSpec((tr, c), lambda i: (i, 0))
    return pl.pallas_call(body, name=name, grid=(r // tr,), in_specs=[spec] * 4, out_specs=[spec] * 3,
                          out_shape=[jax.ShapeDtypeStruct((r, c), F32)] * 3, compiler_params=_params(("parallel",)))(w, g, m, v)


def _adamw_shard(kind, l, w, m, v, halves, off, prev, name):
    h = PACK_HEIGHT[kind]
    assert off % h == 0, (kind, off)
    _, r, c = w.shape
    rows = r // 2

    def body(w_ref, m_ref, v_ref, p_ref, *rest):
        g_ref, d_ref, nm_ref, nv_ref = rest[-4:]
        if kind == "in":
            for t in range(2):
                g = p_ref[:, t * IN_PIECE_COLS:(t + 1) * IN_PIECE_COLS]
                rs = slice(t * h, (t + 1) * h)
                g_ref[rs, :] = g
                d_ref[rs, :], nm_ref[rs, :], nv_ref[rs, :] = _adam_val(w_ref[rs, :], g, m_ref[rs, :], v_ref[rs, :])
        else:
            g = p_ref[...]
            g_ref[...] = g
            d_ref[...], nm_ref[...], nv_ref[...] = _adam_val(w_ref[...], g, m_ref[...], v_ref[...])

    blk = pl.BlockSpec((None, rows, c), lambda half: (l, half, 0))
    in_specs = [blk, blk, blk, pl.BlockSpec((None, h, 1024), lambda half: (half, off // h, 0))]
    args = [w, m, v, halves]
    aliases = {}
    if prev is not None:
        in_specs += [pl.BlockSpec(memory_space=pl.ANY)] * 4
        args += list(prev)
        aliases = {4 + j: j for j in range(4)}
    return pl.pallas_call(
        body, name=name, grid=(2,), in_specs=in_specs, out_specs=[blk] * 4,
        out_shape=[jax.ShapeDtypeStruct(w.shape, F32)] * 4, input_output_aliases=aliases,
        compiler_params=_params(("parallel",)))(*args)


SMALL_ROWS = 48


def _small_rows(small, sq):
    def lane_pad(v):
        return jnp.pad(v, (0, D_MODEL - v.shape[0]))[None]

    head_rows = [lane_pad(jnp.concatenate([s["q_norm"][0], s["k_norm"][0], s["sink"]])) for s in small]
    loss_row = lane_pad((0.5 / D_MODEL) * jnp.sum(sq, keepdims=True)[0])
    rows = jnp.concatenate([s["mod"].reshape(-1, D_MODEL) for s in small] + [s["gammas"] for s in small] + head_rows + [loss_row], axis=0)
    return jnp.pad(rows, ((0, SMALL_ROWS - rows.shape[0]), (0, 0)))


def _local_step(x, ctx, target, mods, gam, qn, kn, sink, w_first, w_layers, packed, kc_idx):
    nb, seq, _ = x.shape
    rt = _Rows(nb, seq, ctx.shape[1])
    rt_lat = rt.latent_only()
    tables = _rope_tables(rt)
    fuse = packed is not None
    h = (x.reshape(rt.n_lat, D_MODEL), ctx.reshape(rt.n_ctx, D_MODEL))
    wg = [{}, {}] if fuse else [dict(w) for w in w_layers]
    wg[0]["in"] = (w_first, 0)
    if fuse:
        wg[0]["in_own"] = (packed, W_FIRST[0])
    saved = []
    for l in range(DEPTH):
        g_pre_mix, g_post_mix, g_pre_mlp, g_post_mlp = gam[l]
        if l == 0:
            u, qkv, qkvp, h = _in_fwd(rt, h, g_pre_mix, mods[l], wg[l], tables, qn[l], kn[l], f"in_fwd{l}")
        else:
            u, qkv, qkvp = _in_fwd(rt, h, g_pre_mix, mods[l], wg[l], tables, qn[l], kn[l], f"in_fwd{l}")
        if fuse and l == 0:
            o, lse_lat, w_mlp0, w_out0, w_mix1 = _attn_fwd(rt, qkvp, sink[l], None, f"attn_lat_fwd{l}",
                                                          comm=_gather_comm(packed, [W_MLP0, W_OUT0, W_MIX1], lead=2))
            wg[0].update({kind: (w_mlp0, PACK_OFF[(kind, 0)] - W_MLP0[0]) for kind in ("up", "down")})
            wg[0]["out"] = (w_out0, 0)
            wg[1] = {kind: (w_mix1, PACK_OFF[(kind, 1)] - W_MIX1[0]) for kind in ("out", "in")}
        elif fuse:
            o, lse_lat, w_mlp1 = _attn_fwd(rt, qkvp, sink[l], None, f"attn_lat_fwd{l}", comm=_gather_comm(packed, [W_MLP1], lead=2))
            wg[1].update({kind: (w_mlp1, PACK_OFF[(kind, 1)] - W_MLP1[0]) for kind in ("up", "down")})
        else:
            o, lse_lat = _attn_fwd(rt, qkvp, sink[l], None, f"attn_lat_fwd{l}")
        if l < DEPTH - 1:
            o, lse_ctx = _attn_fwd(rt, qkvp, sink[l], o, f"attn_ctx_fwd{l}")
            mix, h1, u2 = _out_fwd(rt, o, wg[l], h, mods[l], g_post_mix, g_pre_mlp, f"out_fwd{l}")
            r, y, h2 = _mlp_fwd(rt, u2, h1, wg[l], mods[l], g_post_mlp, f"mlp_fwd{l}")
        else:
            lse_ctx = None
            mix, h1, u2 = _out_fwd(rt_lat, o, wg[l], h, mods[l], g_post_mix, g_pre_mlp, f"out_fwd{l}")
            r, y, dh, sq = _mlp_fwd(rt_lat, u2, h1, wg[l], mods[l], g_post_mlp, f"mlp_fwd{l}", target=target.reshape(rt.n_lat, D_MODEL))
        saved.append((h, u, qkv, qkvp, o, lse_lat, lse_ctx, mix, h1, u2, r, y))
        h = h2

    small = [None] * DEPTH
    groups = {}
    for l in reversed(range(DEPTH)):
        g_pre_mix, g_post_mix, g_pre_mlp, g_post_mlp = gam[l]
        h0, u, qkv, qkvp, o, lse_lat, lse_ctx, mix, h1, u2, r, y = saved[l]
        mlp_group, mix_group = (G_LAYER1, G_LAYER1) if l == 1 else (G_MLP0, G_MIX0)
        hide = fuse and l == 0

        dead_ctx = l == DEPTH - 1
        rt_b = rt_lat if dead_ctx else rt
        outs = _mlp_down_bwd(rt_b, dh, y, r, wg[l], mods[l], g_post_mlp, f"mlp_down_bwd{l}",
                             comm=_pair_comm(groups[G_LAYER1]) if hide else None)
        dy, da, d_gate_m, d_g_post_mlp = outs[:4]
        if hide:
            sum1 = _pair_sum(groups[G_LAYER1], outs[4], kc_idx, "grad_pair_sum_layer1")
        p_mlp = _wgrad_packed(rt_b, r, dy, "down", PACK_OFF[("down", l)] - mlp_group[0], mlp_group[1], None, f"mlp_down_wgrad{l}")
        dh1, d_sh_m, d_sc_m, d_g_pre_mlp = _mlp_up_bwd(rt_b, da, wg[l], h1, dh, mods[l], g_pre_mlp, f"mlp_up_bwd{l}")
        p_mlp = _wgrad_packed(rt_b, u2, da, "up", PACK_OFF[("up", l)] - mlp_group[0], mlp_group[1], p_mlp, f"mlp_up_wgrad{l}")
        outs = _out_bwd(rt_b, dh1, mix, wg[l], mods[l], g_post_mix, f"out_bwd{l}", comm=_pair_comm(p_mlp) if hide else None)
        dmix, do, d_gate_a, d_g_post_mix = outs[:4]
        if hide:
            sum0 = _pair_sum(p_mlp, outs[4], kc_idx, "grad_pair_sum_mlp0")
        p_mix = _wgrad_packed(rt_b, o, dmix, "out", PACK_OFF[("out", l)] - mix_group[0], mix_group[1],
                              p_mlp if l == 1 else None, f"out_wgrad{l}")
        outs = _attn_bwd(rt, qkvp, o, lse_lat, do, sink[l], None, f"attn_lat_bwd{l}",
                         comm=_chip_comm([sum1[1], sum0[1]]) if hide else None)
        dq, dkv, dkv_c, dsink1 = outs[:4]
        if hide:
            groups[G_LAYER1] = _owner_sum(sum1[0], outs[4], kc_idx, "grad_owner_sum_layer1")
            groups[G_MLP0] = _owner_sum(sum0[0], outs[5], kc_idx, "grad_owner_sum_mlp0")
        if dead_ctx:
            dsink2 = jnp.zeros_like(dsink1)
            d_gate_m, d_sh_m, d_sc_m, d_gate_a = [a.at[nb].set(0.0) for a in (d_gate_m, d_sh_m, d_sc_m, d_gate_a)]
        else:
            dq, dkv, dsink2 = _attn_bwd(rt, qkvp, o, lse_ctx, do, sink[l], (dq, dkv, dkv_c), f"attn_ctx_bwd{l}")
        dqkv, dh, dqn, dkn, d_sh_a, d_sc_a, d_g_pre_mix = _in_bwd(rt, dq, dkv, qkv, tables, qn[l], kn[l], wg[l], h0, dh1, mods[l],
                                                                  g_pre_mix, l == 0, f"in_bwd{l}",
                                                                  dead_ctx_dkv=dkv_c if dead_ctx else None)
        dmod = jnp.concatenate([d_sh_a, d_sc_a, d_gate_a, d_sh_m, d_sc_m, d_gate_m], axis=1)
        small[l] = dict(mod=dmod, gammas=jnp.concatenate([d_g_pre_mix, d_g_post_mix, d_g_pre_mlp, d_g_post_mlp], axis=0),
                        q_norm=dqn, k_norm=dkn, sink=(dsink1 + dsink2)[:, 0])
        gather = _gather_comm(_small_rows(small, sq), [(0, SMALL_ROWS)]) if hide else None
        outs = _wgrad_packed(rt, u, dqkv, "in", PACK_OFF[("in", l)] - mix_group[0], mix_group[1], p_mix, f"in_wgrad{l}", comm=gather)
        groups[mix_group], small_g = outs if hide else (outs, None)
        if not hide and l == 0:
            groups[G_MLP0] = p_mlp
    return sq, dh.reshape(nb, seq, D_MODEL), [groups[G_LAYER1], groups[G_MLP0], groups[G_MIX0]], small, small_g


def kernel(x, c, ctx, c_ctx, w_ada, b_ada, g_pre_mix, g_post_mix, g_pre_mlp, g_post_mlp, w_in, q_norm, k_norm, sink, w_out, w_up, w_down, loss_target, m_c_ctx, m_w_ada, m_b_ada, m_g_pre_mix, m_g_post_mix, m_g_pre_mlp, m_g_post_mlp, m_w_in, m_q_norm, m_k_norm, m_sink, m_w_out, m_w_up, m_w_down, v_c_ctx, v_w_ada, v_b_ada, v_g_pre_mix, v_g_post_mix, v_g_pre_mlp, v_g_post_mlp, v_w_in, v_q_norm, v_k_norm, v_sink, v_w_out, v_w_up, v_w_down):
    nb = x.shape[0]
    ix, iy, ic = lax.axis_index("x"), lax.axis_index("y"), lax.axis_index("c")
    chip = 2 * ix + iy
    dev = 2 * chip + ic
    ada_cols = w_ada.shape[2] // 2

    packed = _pack_local_half(w_in, w_out, w_up, w_down, ic)
    c_rows = c.reshape(8, (nb * D_MODEL) // 8)
    c_all, = _comm_alone(_gather_comm(c_rows, [(0, c_rows.shape[0])]), "gather_c")
    c_all = c_all.reshape(N_DEV * nb, D_MODEL)
    n_cond = N_DEV * nb + 1
    cond_rows = 16 * ((n_cond + 15) // 16)
    cond = jnp.concatenate([c_all, c_ctx[None, :], jnp.zeros((cond_rows - n_cond, D_MODEL), F32)], axis=0)
    c_idx = ic.reshape(1).astype(jnp.int32)
    kc_idx = jnp.stack([chip, ic]).astype(jnp.int32)
    b_ada_half = lax.dynamic_slice_in_dim(b_ada, dev * ada_cols, ada_cols, 1)[:, None, :]
    x_ada, mod_part = _ada_fwd(cond, w_ada, b_ada_half, c_idx, "ada_fwd")
    mod_rows2d = mod_part.reshape(DEPTH * cond_rows, ada_cols)
    mod_g, w_first = _comm_alone(_merge([_gather_comm(mod_rows2d, [(0, mod_rows2d.shape[0])]),
                                         _gather_comm(packed, [W_FIRST], copy_own=False)]), "gather_mod_w_first")
    mod_all = mod_g.reshape(N_DEV, DEPTH, cond_rows, ada_cols).transpose(1, 2, 0, 3).reshape(DEPTH, cond_rows, N_MOD * D_MODEL)
    mods = []
    for l in range(DEPTH):
        mine = lax.dynamic_slice_in_dim(mod_all[l], dev * nb, nb, 0)
        mods.append(jnp.concatenate([mine, mod_all[l, n_cond - 1:n_cond]], axis=0).reshape(nb + 1, N_MOD, D_MODEL))

    gam = [(g_pre_mix[l][None], g_post_mix[l][None], g_pre_mlp[l][None], g_post_mlp[l][None]) for l in range(DEPTH)]
    qn = [jnp.tile(q_norm[l], 2)[None] for l in range(DEPTH)]
    kn = [jnp.tile(k_norm[l], 2)[None] for l in range(DEPTH)]
    _, grad_x, (h_layer1, h_mlp0, p_mix0), _, small_g = _local_step(x, ctx, loss_target, mods, gam, qn, kn, [sink[l] for l in range(DEPTH)],
                                                                 w_first, None, packed, kc_idx)

    def step(w, g, m, v, name):
        shape = w.shape
        cols = shape[-1]
        outs = _adamw(w.reshape(-1, cols), g.reshape(-1, cols), m.reshape(-1, cols), v.reshape(-1, cols), name)
        return tuple(a.reshape(shape) for a in outs)

    def shard_update(kind, w, m, v, layer0, layer1):
        outs = None
        for l, (halves, group) in enumerate((layer0, layer1)):
            outs = _adamw_shard(kind, l, w, m, v, halves, PACK_OFF[(kind, l)] - group[0], outs, f"adamw_w_{kind}{l}")
        return tuple(outs)

    tot = _dev_sum(small_g, "small_sum")
    mod_rows = (nb + 1) * N_MOD
    o_head = DEPTH * mod_rows + 4 * DEPTH
    loss = tot[o_head + DEPTH, 0]
    grad_q_norm = tot[o_head:o_head + DEPTH, 0:64] + tot[o_head:o_head + DEPTH, 64:128]
    grad_k_norm = tot[o_head:o_head + DEPTH, 128:192] + tot[o_head:o_head + DEPTH, 192:256]
    grad_sink = tot[o_head:o_head + DEPTH, 256:264]

    ex = small_g[:, :DEPTH * mod_rows].reshape(N_DEV, DEPTH, nb + 1, N_MOD * D_MODEL)[:, :, :nb]
    ex = ex.transpose(1, 0, 2, 3).reshape(DEPTH, N_DEV * nb, N_MOD * D_MODEL)
    cx = tot[:DEPTH * mod_rows].reshape(DEPTH, nb + 1, N_MOD * D_MODEL)[:, nb:]
    dm = jnp.concatenate([ex, cx, jnp.zeros((DEPTH, cond_rows - n_cond, N_MOD * D_MODEL), F32)], axis=1)
    shard_cols = w_ada.shape[2]
    grad_w_ada = _ada_wgrad(x_ada, lax.dynamic_slice_in_dim(dm, chip * shard_cols, shard_cols, 2).astype(BF16), "ada_wgrad")
    dcx = jnp.pad(lax.dynamic_slice_in_dim(cx, dev * ada_cols, ada_cols, 2), ((0, 0), (0, 15), (0, 0))).astype(BF16)
    dcc = _ada_cond_bwd(dcx, w_ada, c_idx, "ada_cond_bwd")[0:8]

    r1, = _comm_alone(_pair_comm(p_mix0), "grad_pair_exchange_mix0")
    a32, a16 = _pair_sum(p_mix0, r1, kc_idx, "grad_pair_sum_mix0")
    r2, dcc_g = _comm_alone(_merge([_chip_comm([a16]), _gather_comm(dcc, [(0, dcc.shape[0])])]), "grad_chip_exchange_mix0")
    h_mix0 = _owner_sum(a32, r2, kc_idx, "grad_owner_sum_mix0")
    h_layer1, h_mlp0, h_mix0 = _comm_alone(_halves_comm([h_layer1, h_mlp0, h_mix0]), "grad_halves_exchange")

    dense_names = ["c_ctx", "b_ada", "g_pre_mix", "g_post_mix", "g_pre_mlp", "g_post_mlp"]
    dense = _small_update(tot, dcc_g, [(c_ctx[None], m_c_ctx[None], v_c_ctx[None]), (b_ada, m_b_ada, v_b_ada),
                                       (g_pre_mix, m_g_pre_mix, v_g_pre_mix), (g_post_mix, m_g_post_mix, v_g_post_mix),
                                       (g_pre_mlp, m_g_pre_mlp, v_g_pre_mlp), (g_post_mlp, m_g_post_mlp, v_g_post_mlp)],
                          nb + 1, "small_update")
    res = {n: r for n, r in zip(dense_names, dense)}
    res["c_ctx"] = tuple(a[0] for a in res["c_ctx"])
    small_names = ["q_norm", "k_norm", "sink"]
    small_w = [q_norm, k_norm, sink]
    small_gr = [grad_q_norm, grad_k_norm, grad_sink]
    small_m = [m_q_norm, m_k_norm, m_sink]
    small_v = [v_q_norm, v_k_norm, v_sink]
    sizes = [int(np.prod(w.shape)) for w in small_w]
    total = sum(sizes)
    flat_rows = 8 * ((total + 8 * D_MODEL - 1) // (8 * D_MODEL))

    def flat(arrs, fill):
        f = jnp.concatenate([a.reshape(-1) for a in arrs])
        return jnp.concatenate([f, jnp.full((flat_rows * D_MODEL - total,), fill, F32)]).reshape(flat_rows, D_MODEL)

    sd, snm, snv = _adamw(flat(small_w, 0.0), flat(small_gr, 0.0), flat(small_m, 0.0), flat(small_v, 1.0), "adamw_small")[:3]

    def unflat(f):
        f = f.reshape(-1)
        out, off = [], 0
        for w, n in zip(small_w, sizes):
            out.append(f[off:off + n].reshape(w.shape))
            off += n
        return out

    small_d, small_nm, small_nv = unflat(sd), unflat(snm), unflat(snv)
    res.update({n: (g, d, nm, nv) for n, g, d, nm, nv in zip(small_names, small_gr, small_d, small_nm, small_nv)})
    res["w_ada"] = (grad_w_ada, *step(w_ada, grad_w_ada, m_w_ada, v_w_ada, "adamw_w_ada"))
    res["w_up"] = shard_update("up", w_up, m_w_up, v_w_up, (h_mlp0, G_MLP0), (h_layer1, G_LAYER1))
    res["w_down"] = shard_update("down", w_down, m_w_down, v_w_down, (h_mlp0, G_MLP0), (h_layer1, G_LAYER1))
    res["w_in"] = shard_update("in", w_in, m_w_in, v_w_in, (h_mix0, G_MIX0), (h_layer1, G_LAYER1))
    res["w_out"] = shard_update("out", w_out, m_w_out, v_w_out, (h_mix0, G_MIX0), (h_layer1, G_LAYER1))

    order = ["c_ctx", "w_ada", "b_ada", "g_pre_mix", "g_post_mix", "g_pre_mlp", "g_post_mlp", "w_in", "q_norm", "k_norm", "sink", "w_out", "w_up", "w_down"]
    return (loss, grad_x, *[res[n][0] for n in order], *[res[n][1] for n in order],
            *[res[n][2] for n in order], *[res[n][3] for n in order])
```

```python
import functools

import jax
import jax.numpy as jnp
import numpy as np
from jax import lax
from jax.experimental import pallas as pl
from jax.experimental.pallas import tpu as pltpu

F32 = jnp.float32
BF16 = jnp.bfloat16

D_MODEL = 1024
HEAD_DIM = 64
GROUP = 4
WINDOW = 128
N_MOD = 6
D_FF = 4 * D_MODEL
IN_COLS = 1536
GRID_W = 64
ROPE_THETA = 10000.0
EPS = 1e-6
NEG_BIG = -1e30
Q_SCALE = HEAD_DIM ** -0.5
DEPTH = 2
N_DEV = 8

ADAM_LR = 0.001
ADAM_B1 = 0.9
ADAM_B2 = 0.999
ADAM_EPS = 1e-08
ADAM_WD = 0.01
ADAM_STEP = 10

V7X_VMEM_BYTES = 64 * 1024 * 1024
VMEM_LIMIT = V7X_VMEM_BYTES - 8 * 1024 * 1024

MESH = pl.DeviceIdType.MESH
NT = (((1,), (1,)), ((), ()))
TN = (((0,), (0,)), ((), ()))

COL_KA, COL_VA, COL_KB, COL_VB = 4, 5, 10, 11
NORMED_COLS = 640

PACK_HEIGHT = {"up": 512, "down": 512, "in": 256, "out": 128}
IN_PIECE_COLS = 384
PACK_OFF = {("up", 0): 0, ("down", 0): 512, ("in", 0): 1024, ("out", 0): 1280,
            ("up", 1): 1408, ("down", 1): 1920, ("in", 1): 2432, ("out", 1): 2688}
PACK_ROWS = 2816
W_FIRST, W_MLP0, W_OUT0, W_MLP1, W_MIX1 = (1024, 256), (0, 1024), (1280, 128), (1408, 1024), (2432, 384)
G_LAYER1, G_MLP0, G_MIX0 = (1408, 1408), (0, 1024), (1024, 384)


def _pick(n, cands):
    for t in cands:
        if n % t == 0:
            return t
    raise ValueError(f"no tile for {n}")


def _params(sem):
    return pltpu.CompilerParams(dimension_semantics=sem, vmem_limit_bytes=VMEM_LIMIT)


class _Comm:
    def __init__(self, inputs, out_shapes, aliases, n_send, n_recv, start, finish, relay=None, lead=0):
        self.inputs, self.out_shapes, self.aliases = list(inputs), list(out_shapes), dict(aliases)
        self.n_send, self.n_recv, self.start, self.finish, self.relay, self.lead = n_send, n_recv, start, finish, relay, lead


def _comm_call(compute, comm, *, name, grid, in_specs, out_specs, out_shape, args, aliases, semantics, scratch=()):
    in_specs, out_specs, out_shape, args, aliases = list(in_specs), list(out_specs), list(out_shape), list(args), dict(aliases)
    scratch = list(scratch)
    if comm is None:
        return pl.pallas_call(compute, name=name, grid=grid, in_specs=in_specs, out_specs=out_specs, out_shape=out_shape,
                              input_output_aliases=aliases, scratch_shapes=scratch, compiler_params=_params(semantics))(*args)
    n_in, n_out, n_ci, n_co = len(args), len(out_shape), len(comm.inputs), len(comm.out_shapes)
    hbm = pl.BlockSpec(memory_space=pl.ANY)
    aliases.update({n_in + i: n_out + o for i, o in comm.aliases.items()})

    def body(*refs):
        ins, c_ins = refs[:n_in], refs[n_in:n_in + n_ci]
        outs, c_outs = refs[n_in + n_ci:n_in + n_ci + n_out], refs[n_in + n_ci + n_out:n_in + n_ci + n_out + n_co]
        scr = refs[n_in + n_ci + n_out + n_co:-2]
        send_sems, recv_sems = refs[-2:]
        ids = [pl.program_id(a) for a in range(len(grid))]
        first = functools.reduce(jnp.logical_and, [i == 0 for i in ids])
        last = functools.reduce(jnp.logical_and, [i == g - 1 for i, g in zip(ids, grid)])

        @pl.when(first)
        def _():
            comm.start(c_ins, c_outs, send_sems, recv_sems)

        compute(*ins, *outs, *scr)

        if comm.relay is not None:
            step = functools.reduce(lambda acc, ig: acc * ig[1] + ig[0], zip(ids, grid), 0)

            @pl.when(step == int(np.prod(grid)) - 1 - comm.lead)
            def _():
                comm.relay(c_ins, c_outs, send_sems, recv_sems)

        @pl.when(last)
        def _():
            comm.finish(c_ins, c_outs, send_sems, recv_sems)

    return pl.pallas_call(
        body, name=name, grid=grid,
        in_specs=in_specs + [hbm] * n_ci, out_specs=out_specs + [hbm] * n_co, out_shape=out_shape + comm.out_shapes,
        input_output_aliases=aliases,
        scratch_shapes=scratch + [pltpu.SemaphoreType.DMA((comm.n_send,)), pltpu.SemaphoreType.DMA((comm.n_recv,))],
        compiler_params=_params(("arbitrary",) * len(grid)),
    )(*args, *comm.inputs)


def _place():
    x_, y_, c_ = lax.axis_index("x"), lax.axis_index("y"), lax.axis_index("c")
    return x_, y_, c_, [(1 - x_, y_), (x_, 1 - y_), (1 - x_, 1 - y_)]


GATHER_SENDS, GATHER_RECVS = 8, 7


def _gather_copies(packed_ref, wg_ref, send_sems, recv_sems, rows, nth=0):
    r0, n = rows
    x_, y_, c_, chips = _place()
    me, sibling = (x_, y_, c_), (x_, y_, 1 - c_)
    src = packed_ref.at[pl.ds(r0, n), :]

    def slot(px, py, pc):
        return wg_ref.at[4 * px + 2 * py + pc]

    def copy(k, block, to, from_packed=False):
        return pltpu.make_async_remote_copy(src_ref=src if from_packed else slot(*block), dst_ref=slot(*block),
                                            send_sem=send_sems.at[GATHER_SENDS * nth + k], recv_sem=recv_sems.at[GATHER_RECVS * nth + k],
                                            device_id=to, device_id_type=MESH)

    own = [copy(0, me, sibling, True)] + [copy(1 + j, me, (*chip, c_), True) for j, chip in enumerate(chips)]
    passed = [copy(4 + j, (*chip, c_), sibling) for j, chip in enumerate(chips)]
    over_ici = [copy(1 + j, (*chip, c_), me) for j, chip in enumerate(chips)]
    from_sibling = [copy(0, sibling, me)] + [copy(4 + j, (*chip, 1 - c_), me) for j, chip in enumerate(chips)]
    mine = pltpu.make_async_copy(src, slot(*me), send_sems.at[GATHER_SENDS * nth + 7])
    return mine, own, passed, over_ici, from_sibling


def _gather_start(packed_ref, wg_ref, send_sems, recv_sems, rows, nth=0, copy_own=True):
    mine, own, _, _, _ = _gather_copies(packed_ref, wg_ref, send_sems, recv_sems, rows, nth)
    if copy_own:
        mine.start()
    for cp in own:
        cp.start()


def _gather_relay(packed_ref, wg_ref, send_sems, recv_sems, rows, nth=0):
    _, _, passed, over_ici, _ = _gather_copies(packed_ref, wg_ref, send_sems, recv_sems, rows, nth)
    for arrived, onward in zip(over_ici, passed):
        arrived.wait_recv()
        onward.start()


def _gather_finish(packed_ref, wg_ref, send_sems, recv_sems, rows, nth=0, copy_own=True):
    mine, own, passed, _, from_sibling = _gather_copies(packed_ref, wg_ref, send_sems, recv_sems, rows, nth)
    for arrived in from_sibling:
        arrived.wait_recv()
    for cp in own + passed:
        cp.wait_send()
    if copy_own:
        mine.wait()


def _gather_comm(packed, ranges, copy_own=True, lead=0):
    shapes = [jax.ShapeDtypeStruct((N_DEV, n, packed.shape[1]), packed.dtype) for _, n in ranges]

    def start(ins, outs, ss, rs):
        for nth, rows in enumerate(ranges):
            _gather_start(ins[0], outs[nth], ss, rs, rows, nth, copy_own)

    def relay(ins, outs, ss, rs):
        for nth, rows in enumerate(ranges):
            _gather_relay(ins[0], outs[nth], ss, rs, rows, nth)

    def finish(ins, outs, ss, rs):
        for nth, rows in enumerate(ranges):
            _gather_finish(ins[0], outs[nth], ss, rs, rows, nth, copy_own)

    return _Comm([packed], shapes, {}, GATHER_SENDS * len(ranges), GATHER_RECVS * len(ranges), start, finish, relay, lead)


def _pair_copy(p_ref, out_ref, send_sems, recv_sems):
    x_, y_, c_, _ = _place()
    return pltpu.make_async_remote_copy(src_ref=p_ref.at[1 - c_], dst_ref=out_ref,
                                        send_sem=send_sems.at[0], recv_sem=recv_sems.at[0],
                                        device_id=(x_, y_, 1 - c_), device_id_type=MESH)


def _pair_comm(p):
    return _Comm([p], [jax.ShapeDtypeStruct(p.shape[1:], p.dtype)], {}, 1, 1,
                 lambda ins, outs, ss, rs: _pair_copy(ins[0], outs[0], ss, rs).start(),
                 lambda ins, outs, ss, rs: _pair_copy(ins[0], outs[0], ss, rs).wait())


def _chip_copies(a_refs, out_refs, send_sems, recv_sems):
    _, _, c_, chips = _place()
    return [pltpu.make_async_remote_copy(src_ref=a_ref.at[2 * tx + ty], dst_ref=o_ref.at[j],
                                         send_sem=send_sems.at[3 * g + j], recv_sem=recv_sems.at[3 * g + j],
                                         device_id=(tx, ty, c_), device_id_type=MESH)
            for g, (a_ref, o_ref) in enumerate(zip(a_refs, out_refs)) for j, (tx, ty) in enumerate(chips)]


def _chip_start(a_refs, out_refs, send_sems, recv_sems):
    for cp in _chip_copies(a_refs, out_refs, send_sems, recv_sems):
        cp.start()


def _chip_finish(a_refs, out_refs, send_sems, recv_sems):
    for cp in _chip_copies(a_refs, out_refs, send_sems, recv_sems):
        cp.wait()


def _chip_comm(arrays):
    shapes = [jax.ShapeDtypeStruct((3,) + a.shape[1:], a.dtype) for a in arrays]
    return _Comm(arrays, shapes, {}, 3 * len(arrays), 3 * len(arrays), _chip_start, _chip_finish)


def _halves_copies(in_refs, out_refs, send_sems, recv_sems):
    x_, y_, c_, _ = _place()
    return [pltpu.make_async_remote_copy(src_ref=o_ref.at[c_], dst_ref=o_ref.at[c_], send_sem=send_sems.at[i], recv_sem=recv_sems.at[i],
                                         device_id=(x_, y_, 1 - c_), device_id_type=MESH)
            for i, o_ref in enumerate(out_refs)]


def _halves_start(in_refs, out_refs, send_sems, recv_sems):
    for cp in _halves_copies(in_refs, out_refs, send_sems, recv_sems):
        cp.start()


def _halves_finish(in_refs, out_refs, send_sems, recv_sems):
    for cp in _halves_copies(in_refs, out_refs, send_sems, recv_sems):
        cp.wait()


def _halves_comm(arrays):
    shapes = [jax.ShapeDtypeStruct(a.shape, a.dtype) for a in arrays]
    return _Comm(arrays, shapes, {i: i for i in range(len(arrays))}, len(arrays), len(arrays), _halves_start, _halves_finish)


class _SemSlice:
    class _At:
        def __init__(self, sems, first):
            self.sems, self.first = sems, first

        def __getitem__(self, k):
            return self.sems.at[self.first + k]

    def __init__(self, sems, first):
        self.at = _SemSlice._At(sems, first)


def _merge(comms):
    inputs = [a for c in comms for a in c.inputs]
    shapes = [s for c in comms for s in c.out_shapes]
    aliases, spans = {}, []
    i0 = o0 = s0 = r0 = 0
    for c in comms:
        aliases.update({i0 + i: o0 + o for i, o in c.aliases.items()})
        spans.append((slice(i0, i0 + len(c.inputs)), slice(o0, o0 + len(c.out_shapes)), s0, r0))
        i0, o0, s0, r0 = i0 + len(c.inputs), o0 + len(c.out_shapes), s0 + c.n_send, r0 + c.n_recv

    def start(ins, outs, ss, rs):
        for c, (i, o, s, r) in zip(comms, spans):
            c.start(ins[i], outs[o], _SemSlice(ss, s), _SemSlice(rs, r))

    def finish(ins, outs, ss, rs):
        for c, (i, o, s, r) in zip(comms, spans):
            if c.relay is not None:
                c.relay(ins[i], outs[o], _SemSlice(ss, s), _SemSlice(rs, r))
            c.finish(ins[i], outs[o], _SemSlice(ss, s), _SemSlice(rs, r))

    return _Comm(inputs, shapes, aliases, s0, r0, start, finish)


def _comm_alone(comm, name):
    n_ci = len(comm.inputs)
    hbm = pl.BlockSpec(memory_space=pl.ANY)

    def body(*refs):
        c_ins, c_outs, send_sems, recv_sems = refs[:n_ci], refs[n_ci:-2], refs[-2], refs[-1]
        comm.start(c_ins, c_outs, send_sems, recv_sems)
        if comm.relay is not None:
            comm.relay(c_ins, c_outs, send_sems, recv_sems)
        comm.finish(c_ins, c_outs, send_sems, recv_sems)

    return pl.pallas_call(
        body, name=name, out_shape=comm.out_shapes, in_specs=[hbm] * n_ci, out_specs=[hbm] * len(comm.out_shapes),
        input_output_aliases=comm.aliases,
        scratch_shapes=[pltpu.SemaphoreType.DMA((comm.n_send,)), pltpu.SemaphoreType.DMA((comm.n_recv,))],
    )(*comm.inputs)


SUM_TILES = (704, 512, 384, 320, 256, 192, 128, 64)


def _pair_sum(p, r1, kc_idx, name):
    _, _, n, c = p.shape
    tr = _pick(n, SUM_TILES)

    def body(s_ref, p_ref, r_ref, o32_ref, o16_ref):
        v = p_ref[...] + r_ref[...]
        o16_ref[...] = v.astype(BF16)

        @pl.when(pl.program_id(1) == s_ref[0])
        def _():
            o32_ref[...] = v

    blk = pl.BlockSpec((None, tr, c), lambda i, j, s: (j, i, 0))
    grid_spec = pltpu.PrefetchScalarGridSpec(
        num_scalar_prefetch=1, grid=(n // tr, 4),
        in_specs=[pl.BlockSpec((None, None, tr, c), lambda i, j, s: (s[1], j, i, 0)), blk],
        out_specs=[pl.BlockSpec((tr, c), lambda i, j, s: (i, 0)), blk])
    return pl.pallas_call(
        body, name=name, grid_spec=grid_spec,
        out_shape=[jax.ShapeDtypeStruct((n, c), F32), jax.ShapeDtypeStruct((4, n, c), BF16)],
        compiler_params=_params(("arbitrary", "arbitrary")),
    )(kc_idx, p, r1)


def _owner_sum(a32, r2, kc_idx, name):
    r, c = a32.shape
    tr = _pick(r, SUM_TILES)

    def body(s_ref, a_ref, r_ref, o_ref):
        v = a_ref[...]
        for j in range(3):
            v = v + r_ref[j].astype(F32)
        o_ref[...] = v

    grid_spec = pltpu.PrefetchScalarGridSpec(
        num_scalar_prefetch=1, grid=(r // tr,),
        in_specs=[pl.BlockSpec((tr, c), lambda i, s: (i, 0)),
                  pl.BlockSpec((3, tr, c), lambda i, s: (0, i, 0))],
        out_specs=pl.BlockSpec((None, tr, c), lambda i, s: (s[1], i, 0)))
    return pl.pallas_call(
        body, name=name, grid_spec=grid_spec,
        out_shape=jax.ShapeDtypeStruct((2, r, c), F32),
        compiler_params=_params(("arbitrary",)),
    )(kc_idx, a32, r2)


def _pack_local_half(w_in_s, w_out_s, w_up_s, w_down_s, c_idx):
    parts, row = [], 0
    for (kind, l), off in sorted(PACK_OFF.items(), key=lambda kv: kv[1]):
        if off > row:
            parts.append(jnp.zeros((off - row, 1024), BF16))
        if kind == "up":
            p = lax.dynamic_slice_in_dim(w_up_s[l], c_idx * 512, 512, 0)
        elif kind == "down":
            p = lax.dynamic_slice_in_dim(w_down_s[l], c_idx * 512, 512, 0)
        elif kind == "in":
            p = lax.dynamic_slice_in_dim(w_in_s[l], c_idx * 512, 512, 0)
            p = p.reshape(2, 256, IN_PIECE_COLS).transpose(1, 0, 2).reshape(256, 2 * IN_PIECE_COLS)
            p = jnp.pad(p, ((0, 0), (0, 1024 - 2 * IN_PIECE_COLS)))
        else:
            p = lax.dynamic_slice_in_dim(w_out_s[l], c_idx * 128, 128, 0)
        parts.append(p.astype(BF16))
        row = off + PACK_HEIGHT[kind]
    return jnp.concatenate(parts, axis=0)


def _unpack_in_pieces(w_ref, own_ref, w_scr):
    if own_ref is not None:
        me = 4 * lax.axis_index("x") + 2 * lax.axis_index("y") + lax.axis_index("c")
    for d in range(N_DEV):
        k, c = d // 2, d % 2
        for t in range(2):
            piece = w_ref[d, :, t * IN_PIECE_COLS:(t + 1) * IN_PIECE_COLS]
            if own_ref is not None:
                piece = jnp.where(me == d, own_ref[:, t * IN_PIECE_COLS:(t + 1) * IN_PIECE_COLS], piece)
            w_scr[c * 512 + t * 256:c * 512 + (t + 1) * 256, k * IN_PIECE_COLS:(k + 1) * IN_PIECE_COLS] = piece


def _in_weight_operands(wg):
    specs, args = [_gathered_spec(wg, "in")], [wg["in"][0]]
    if "in_own" in wg:
        own, off = wg["in_own"]
        h = PACK_HEIGHT["in"]
        assert off % h == 0
        specs.append(pl.BlockSpec((h, 1024), lambda *_: (off // h, 0), pipeline_mode=pl.Buffered(1)))
        args.append(own)
    return specs, args


class _Rows:
    def __init__(self, nb, seq, ctx):
        self.nb, self.seq, self.ctx = nb, seq, ctx
        self.n_lat, self.n_ctx = nb * seq, nb * ctx
        self.rows = self.n_lat + self.n_ctx
        self.tm = _pick(np.gcd(seq, self.n_ctx), (512, 256, 128))
        self.tiles_per_ex = seq // self.tm
        self.n_tiles = self.rows // self.tm
        self.n_lat_tiles = self.n_lat // self.tm
        self.groups = nb + 1

    def latent_only(self):
        rt = _Rows(self.nb, self.seq, self.ctx)
        rt.n_tiles = self.n_lat_tiles
        return rt

    def group(self, i):
        return jnp.minimum(i // self.tiles_per_ex, self.nb)

    def first_of_group(self, i):
        return jnp.logical_and(i % self.tiles_per_ex == 0, i <= self.n_lat_tiles)


def _mod_spec(rt):
    return pl.BlockSpec((1, N_MOD, D_MODEL), lambda i: (rt.group(i), 0, 0))


def _row_spec(rt, cols):
    return pl.BlockSpec((rt.tm, cols), lambda i: (i, 0))


def _vec_spec(cols):
    return pl.BlockSpec((1, cols), lambda i: (0, 0))


def _group_spec(rt):
    return pl.BlockSpec((1, 1, D_MODEL), lambda i: (rt.group(i), 0, 0))


def _gathered_spec(wg, kind):
    h, off = PACK_HEIGHT[kind], wg[kind][1]
    assert off % h == 0, (kind, off)
    return pl.BlockSpec((N_DEV, h, 1024), lambda *_: (0, off // h, 0), pipeline_mode=pl.Buffered(1))


def _group_shape(rt):
    return jax.ShapeDtypeStruct((rt.groups, 1, D_MODEL), F32)


def _vec_shape(cols=D_MODEL):
    return jax.ShapeDtypeStruct((1, cols), F32)


def _rms_inv(v):
    return lax.rsqrt(jnp.mean(v * v, axis=-1, keepdims=True) + EPS)


def _norm_mod_val(h_, g_, mod_ref, i_shift, i_scale):
    n = h_ * _rms_inv(h_) * g_
    return n * (1.0 + mod_ref[0, i_scale:i_scale + 1, :]) + mod_ref[0, i_shift:i_shift + 1, :]


def _post_norm_val(h_, z_, g_, mod_ref, i_gate):
    return h_ + mod_ref[0, i_gate:i_gate + 1, :] * (z_ * _rms_inv(z_) * g_)


def _post_norm_bwd_val(dh_, z_, g_, gate):
    rinv = _rms_inv(z_)
    n0 = z_ * rinv
    dn = dh_ * gate * g_
    dz = rinv * (dn - n0 * jnp.mean(dn * n0, axis=-1, keepdims=True))
    return dz, jnp.sum(dh_ * n0 * g_, axis=0, keepdims=True), jnp.sum(dh_ * gate * n0, axis=0, keepdims=True)


def _norm_mod_bwd_val(du_, h_, g_, one_sc):
    rinv = _rms_inv(h_)
    n0 = h_ * rinv
    dn = du_ * g_ * one_sc
    dh = rinv * (dn - n0 * jnp.mean(dn * n0, axis=-1, keepdims=True))
    return (dh, jnp.sum(du_, axis=0, keepdims=True), jnp.sum(du_ * n0 * g_, axis=0, keepdims=True),
            jnp.sum(du_ * one_sc * n0, axis=0, keepdims=True))


def _accumulate(rt, i, group_pairs, global_pairs):
    @pl.when(rt.first_of_group(i))
    def _():
        for ref, _ in group_pairs:
            ref[...] = jnp.zeros_like(ref)

    @pl.when(i == 0)
    def _():
        for ref, _ in global_pairs:
            ref[...] = jnp.zeros_like(ref)

    for ref, val in group_pairs:
        ref[0] += val
    for ref, val in global_pairs:
        ref[...] += val


def _rope_tables(rt):
    pos = np.arange(rt.seq)
    axis_dim = HEAD_DIM // 2
    inv = (ROPE_THETA ** (-np.arange(0, axis_dim, 2, dtype=np.float32) / axis_dim)).astype(np.float32)
    ang_r = (pos // GRID_W).astype(np.float32)[:, None] * inv[None, :]
    ang_c = (pos % GRID_W).astype(np.float32)[:, None] * inv[None, :]
    cr, sr, cc, sc = np.cos(ang_r), np.sin(ang_r), np.cos(ang_c), np.sin(ang_c)
    zero = np.zeros_like(sr)
    cos = np.concatenate([cr, cr, cc, cc], axis=1)
    s_lo = np.concatenate([zero, sr, zero, sc], axis=1)
    s_hi = np.concatenate([-sr, zero, -sc, zero], axis=1)

    def full(t, ctx_value):
        return jnp.asarray(np.concatenate([np.tile(t, (1, 2)), np.full((rt.tm, 128), ctx_value)], axis=0), F32)

    return full(cos, 1.0), full(s_lo, 0.0), full(s_hi, 0.0)


def _table_spec(rt):
    return pl.BlockSpec((rt.tm, 128), lambda i: (jnp.where(i < rt.n_lat_tiles, i % rt.tiles_per_ex, rt.tiles_per_ex), 0))


def _head_mean(x):
    r = lax.broadcasted_iota(jnp.int32, (128, 128), 0) // HEAD_DIM
    c = lax.broadcasted_iota(jnp.int32, (128, 128), 1) // HEAD_DIM
    ones = jnp.where(r == c, 1.0 / HEAD_DIM, 0.0).astype(F32)
    return jnp.dot(x, ones, preferred_element_type=F32, precision=lax.Precision.HIGH)


def _head_stats(t):
    return lax.rsqrt(_head_mean(t * t) + EPS)


def _prep_fwd_body(tm, qkv_ref, c, s1, s2, qn, kn, out_ref):
    def rope(t):
        return t * c + pltpu.roll(t, 16, 1) * s1 + pltpu.roll(t, 112, 1) * s2

    for j in range(12):
        t = qkv_ref[:, j * 128:(j + 1) * 128]
        if j < 4:
            t = rope(t * _head_stats(t) * qn) * Q_SCALE
        elif j == COL_KA:
            t = rope(t * _head_stats(t) * kn)
        elif 6 <= j < 10:
            t = rope(t) * Q_SCALE
        elif j == COL_KB:
            t = rope(t)
        out_ref[:, j * 128:(j + 1) * 128] = t.astype(BF16)


def _prep_bwd_body(dq, dkv, qkv_ref, c, s1, s2, qn, kn, out_ref):
    rows = slice(None)

    def rope_bwd(d):
        return d * c + pltpu.roll(d * s1, 112, 1) + pltpu.roll(d * s2, 16, 1)

    def norm_bwd(t, g, dy):
        rinv = _head_stats(t)
        n = t * rinv
        dn = dy * g
        return rinv * (dn - n * _head_mean(dn * n)), jnp.sum(dy * n, axis=0, keepdims=True)

    dqn = jnp.zeros((1, 128), F32)
    dkn = jnp.zeros((1, 128), F32)
    for j in range(12):
        if j < 4:
            d, dg = norm_bwd(qkv_ref[rows, j * 128:(j + 1) * 128], qn, rope_bwd(dq(slice(j * 128, (j + 1) * 128)) * Q_SCALE))
            dqn = dqn + dg
        elif j == COL_KA:
            d, dg = norm_bwd(qkv_ref[rows, j * 128:(j + 1) * 128], kn, rope_bwd(dkv(slice(0, 128))))
            dkn = dkn + dg
        elif j == COL_VA:
            d = dkv(slice(128, 256))
        elif j < 10:
            d = rope_bwd(dq(slice((j - 2) * 128, (j - 1) * 128)) * Q_SCALE)
        elif j == COL_KB:
            d = rope_bwd(dkv(slice(256, 384)))
        else:
            d = dkv(slice(384, 512))
        out_ref[rows, j * 128:(j + 1) * 128] = d.astype(BF16)
    return dqn, dkn


def _in_fwd(rt, h, gamma, mod, wg, tables, qn, kn, name):
    w_specs, w_args = _in_weight_operands(wg)
    n_w = len(w_args)
    joined = not isinstance(h, (tuple, list))
    n_h = 1 if joined else 2

    def body(*refs):
        g_ref, mod_ref = refs[n_h:n_h + 2]
        rest = refs[n_h + 2:]
        c_ref, s1_ref, s2_ref, qn_ref, kn_ref, u_ref, qkn_ref, qkvp_ref = rest[n_w:n_w + 8]
        qkv_ref, w_scr = rest[-2:]
        i = pl.program_id(0)

        @pl.when(i == 0)
        def _():
            _unpack_in_pieces(rest[0], rest[1] if n_w == 2 else None, w_scr)

        if joined:
            h_ = refs[0][...]
        else:
            h_ = jnp.where(i < rt.n_lat_tiles, refs[0][...], refs[1][...])
            rest[n_w + 8][...] = h_
        u = _norm_mod_val(h_, g_ref[...], mod_ref, 0, 1).astype(BF16)
        u_ref[...] = u
        qkv_ref[...] = jnp.dot(u, w_scr[...], preferred_element_type=F32)
        qkn_ref[...] = qkv_ref[:, 0:NORMED_COLS]
        _prep_fwd_body(rt.tm, qkv_ref, c_ref[...], s1_ref[...], s2_ref[...], qn_ref[...], kn_ref[...], qkvp_ref)

    if joined:
        h_specs, h_args = [_row_spec(rt, D_MODEL)], [h]
    else:
        h_specs = [pl.BlockSpec((rt.tm, D_MODEL), lambda i: (jnp.minimum(i, rt.n_lat_tiles - 1), 0)),
                   pl.BlockSpec((rt.tm, D_MODEL), lambda i: (jnp.maximum(i - rt.n_lat_tiles, 0), 0))]
        h_args = list(h)
    out_specs = [_row_spec(rt, D_MODEL), _row_spec(rt, NORMED_COLS), _row_spec(rt, IN_COLS)]
    out_shape = [jax.ShapeDtypeStruct((rt.rows, D_MODEL), BF16), jax.ShapeDtypeStruct((rt.rows, NORMED_COLS), F32),
                 jax.ShapeDtypeStruct((rt.rows, IN_COLS), BF16)]
    if not joined:
        out_specs.append(_row_spec(rt, D_MODEL))
        out_shape.append(jax.ShapeDtypeStruct((rt.rows, D_MODEL), F32))
    return pl.pallas_call(
        body, name=name, grid=(rt.n_tiles,),
        in_specs=h_specs + [_vec_spec(D_MODEL), _mod_spec(rt)] + w_specs + [_table_spec(rt)] * 3 + [_vec_spec(128)] * 2,
        out_specs=out_specs, out_shape=out_shape,
        scratch_shapes=[pltpu.VMEM((rt.tm, IN_COLS), F32), pltpu.VMEM((D_MODEL, IN_COLS), BF16)],
        compiler_params=_params(("arbitrary",)),
    )(*h_args, gamma, mod, *w_args, *tables, qn, kn)


def _in_bwd(rt, dq, dkv, qkv, tables, qn, kn, wg, h, dres, mod, gamma, latent_only, name, comm=None, dead_ctx_dkv=None):
    last = rt.n_lat_tiles - 1
    w_specs, w_args = _in_weight_operands(wg)
    n_w = len(w_args)
    n_dead = 0 if dead_ctx_dkv is None else 1

    def body(dq_ref, dkv_ref, qkv_ref, c_ref, s1_ref, s2_ref, qn_ref, kn_ref, *rest):
        h_ref, dres_ref, mod_ref, g_ref, dqkv_ref, dh_ref, dqn_ref, dkn_ref, dsh_ref, dsc_ref, dg_ref, w_scr = rest[n_w + n_dead:]
        i = pl.program_id(0)

        @pl.when(i == 0)
        def _():
            _unpack_in_pieces(rest[0], rest[1] if n_w == 2 else None, w_scr)

        if n_dead:
            c1_ref, lat = rest[n_w], i <= last
            load_dq = lambda cols: jnp.where(lat, dq_ref[:, cols], 0.0)
            load_dkv = lambda cols: jnp.where(lat, dkv_ref[:, cols], c1_ref[:, cols])
            dres_ = jnp.where(lat, dres_ref[...], 0.0)
        else:
            load_dq, load_dkv, dres_ = (lambda cols: dq_ref[:, cols]), (lambda cols: dkv_ref[:, cols]), dres_ref[...]
        dqn, dkn = _prep_bwd_body(load_dq, load_dkv, qkv_ref, c_ref[...], s1_ref[...], s2_ref[...], qn_ref[...], kn_ref[...], dqkv_ref)
        du = lax.dot_general(dqkv_ref[...], w_scr[...], NT, preferred_element_type=F32)
        dh, dsh, dsc, dg = _norm_mod_bwd_val(du, h_ref[...], g_ref[...], 1.0 + mod_ref[0, 1:2, :])
        if latent_only:
            @pl.when(i <= last)
            def _():
                dh_ref[...] = dres_ + dh
        else:
            dh_ref[...] = dres_ + dh
        _accumulate(rt, i, [(dsh_ref, dsh), (dsc_ref, dsc)], [(dg_ref, dg), (dqn_ref, dqn), (dkn_ref, dkn)])

    dh_spec = pl.BlockSpec((rt.tm, D_MODEL), lambda i: (jnp.minimum(i, last), 0)) if latent_only else _row_spec(rt, D_MODEL)
    dead_specs = [] if dead_ctx_dkv is None else [pl.BlockSpec((rt.tm, 512), lambda i: (jnp.maximum(i - rt.n_lat_tiles, 0), 0))]
    dead_args = [] if dead_ctx_dkv is None else [dead_ctx_dkv]
    return _comm_call(
        body, comm, name=name, grid=(rt.n_tiles,),
        in_specs=[_row_spec(rt, 1024), _row_spec(rt, 512), _row_spec(rt, NORMED_COLS)] + [_table_spec(rt)] * 3 + [_vec_spec(128)] * 2
        + w_specs + dead_specs + [_row_spec(rt, D_MODEL), _row_spec(rt, D_MODEL), _mod_spec(rt), _vec_spec(D_MODEL)],
        out_specs=[_row_spec(rt, IN_COLS), dh_spec, _vec_spec(128), _vec_spec(128),
                   _group_spec(rt), _group_spec(rt), _vec_spec(D_MODEL)],
        out_shape=[jax.ShapeDtypeStruct((rt.rows, IN_COLS), BF16),
                   jax.ShapeDtypeStruct((rt.n_lat if latent_only else rt.rows, D_MODEL), F32),
                   _vec_shape(128), _vec_shape(128), _group_shape(rt), _group_shape(rt), _vec_shape()],
        args=[dq, dkv, qkv, *tables, qn, kn, *w_args, *dead_args, h, dres, mod, gamma], aliases={}, semantics=("arbitrary",),
        scratch=[pltpu.VMEM((D_MODEL, IN_COLS), BF16)])


def _out_fwd(rt, o, wg, h, mod, g_post_mix, g_pre_mlp, name):
    def body(o_ref, w_ref, h_ref, mod_ref, gpost_ref, gpre_ref, mix_ref, h1_ref, u2_ref):
        mix = jnp.dot(o_ref[...], w_ref[...].reshape(D_MODEL, D_MODEL), preferred_element_type=F32)
        mix_ref[...] = mix
        h1 = _post_norm_val(h_ref[...], mix, gpost_ref[...], mod_ref, 2)
        h1_ref[...] = h1
        u2_ref[...] = _norm_mod_val(h1, gpre_ref[...], mod_ref, 3, 4).astype(BF16)

    return pl.pallas_call(
        body, name=name, grid=(rt.n_tiles,),
        in_specs=[_row_spec(rt, D_MODEL), _gathered_spec(wg, "out"), _row_spec(rt, D_MODEL), _mod_spec(rt),
                  _vec_spec(D_MODEL), _vec_spec(D_MODEL)],
        out_specs=[_row_spec(rt, D_MODEL)] * 3,
        out_shape=[jax.ShapeDtypeStruct((rt.rows, D_MODEL), F32), jax.ShapeDtypeStruct((rt.rows, D_MODEL), F32),
                   jax.ShapeDtypeStruct((rt.rows, D_MODEL), BF16)],
        compiler_params=_params(("parallel",)),
    )(o, wg["out"][0], h, mod, g_post_mix, g_pre_mlp)


def _out_bwd(rt, dh1, mix, wg, mod, g_post_mix, name, comm=None):
    def body(dh_ref, mix_ref, w_ref, mod_ref, g_ref, dmix_ref, do_ref, dgate_ref, dg_ref):
        i = pl.program_id(0)
        dz, dgate, dg = _post_norm_bwd_val(dh_ref[...], mix_ref[...], g_ref[...], mod_ref[0, 2:3, :])
        dzb = dz.astype(BF16)
        dmix_ref[...] = dzb
        do_ref[...] = lax.dot_general(dzb, w_ref[...].reshape(D_MODEL, D_MODEL), NT, preferred_element_type=F32).astype(BF16)
        _accumulate(rt, i, [(dgate_ref, dgate)], [(dg_ref, dg)])

    return _comm_call(
        body, comm, name=name, grid=(rt.n_tiles,),
        in_specs=[_row_spec(rt, D_MODEL), _row_spec(rt, D_MODEL), _gathered_spec(wg, "out"), _mod_spec(rt), _vec_spec(D_MODEL)],
        out_specs=[_row_spec(rt, D_MODEL), _row_spec(rt, D_MODEL), _group_spec(rt), _vec_spec(D_MODEL)],
        out_shape=[jax.ShapeDtypeStruct((rt.rows, D_MODEL), BF16), jax.ShapeDtypeStruct((rt.rows, D_MODEL), BF16),
                   _group_shape(rt), _vec_shape()],
        args=[dh1, mix, wg["out"][0], mod, g_post_mix], aliases={}, semantics=("arbitrary",))


def _w_chunk(w_ref, k):
    return w_ref[2 * k:2 * k + 2].reshape(1024, 1024)


def _mlp_fwd(rt, u2, h1, wg, mod, g_post_mlp, name, comm=None, target=None):
    last = rt.n_lat_tiles - 1

    def body(u2_ref, h1_ref, wu_ref, wd_ref, mod_ref, g_ref, *rest):
        u2_ = u2_ref[...]
        y = jnp.zeros((rt.tm, D_MODEL), F32)
        for k in range(D_FF // 1024):
            a = jnp.maximum(jnp.dot(u2_, _w_chunk(wu_ref, k), preferred_element_type=F32), 0.0)
            rest[-3 if target is None else -4][:, k * 1024:(k + 1) * 1024] = a.astype(BF16)
            y = y + jnp.dot((a * a).astype(BF16), _w_chunk(wd_ref, k), preferred_element_type=F32)
        h2 = _post_norm_val(h1_ref[...], y, g_ref[...], mod_ref, 5)
        if target is None:
            _, y_ref, h2_ref = rest
            y_ref[...] = y
            h2_ref[...] = h2
        else:
            t_ref, _, y_ref, dh_ref, sq_ref = rest
            y_ref[...] = y
            i = pl.program_id(0)

            @pl.when(i == 0)
            def _():
                sq_ref[...] = jnp.zeros_like(sq_ref)

            @pl.when(i <= last)
            def _():
                e = h2 - t_ref[...]
                dh_ref[...] = e * (1.0 / D_MODEL)
                sq_ref[...] += jnp.sum(e * e, axis=0, keepdims=True)

            @pl.when(i > last)
            def _():
                dh_ref[...] = jnp.zeros_like(dh_ref)

    in_specs = [_row_spec(rt, D_MODEL), _row_spec(rt, D_MODEL), _gathered_spec(wg, "up"), _gathered_spec(wg, "down"),
                _mod_spec(rt), _vec_spec(D_MODEL)]
    args = [u2, h1, wg["up"][0], wg["down"][0], mod, g_post_mlp]
    out_specs = [_row_spec(rt, D_FF), _row_spec(rt, D_MODEL), _row_spec(rt, D_MODEL)]
    out_shape = [jax.ShapeDtypeStruct((rt.rows, D_FF), BF16), jax.ShapeDtypeStruct((rt.rows, D_MODEL), F32),
                 jax.ShapeDtypeStruct((rt.rows, D_MODEL), F32)]
    if target is not None:
        in_specs.append(pl.BlockSpec((rt.tm, D_MODEL), lambda i: (jnp.minimum(i, last), 0)))
        args.append(target)
        out_specs.append(_vec_spec(D_MODEL))
        out_shape.append(_vec_shape())
    return _comm_call(body, comm, name=name, grid=(rt.n_tiles,), in_specs=in_specs, out_specs=out_specs, out_shape=out_shape,
                      args=args, aliases={}, semantics=("parallel",) if target is None else ("arbitrary",))


def _mlp_down_bwd(rt, dh, y, ra, wg, mod, g_post_mlp, name, comm=None):
    def body(dh_ref, y_ref, ra_ref, wd_ref, mod_ref, g_ref, dy_ref, da_ref, dgate_ref, dg_ref):
        i = pl.program_id(0)
        dz, dgate, dg = _post_norm_bwd_val(dh_ref[...], y_ref[...], g_ref[...], mod_ref[0, 5:6, :])
        dyb = dz.astype(BF16)
        dy_ref[...] = dyb
        for k in range(D_FF // 1024):
            dr = lax.dot_general(dyb, _w_chunk(wd_ref, k), NT, preferred_element_type=F32)
            da_ref[:, k * 1024:(k + 1) * 1024] = (dr * (2.0 * ra_ref[:, k * 1024:(k + 1) * 1024].astype(F32))).astype(BF16)
        _accumulate(rt, i, [(dgate_ref, dgate)], [(dg_ref, dg)])

    return _comm_call(
        body, comm, name=name, grid=(rt.n_tiles,),
        in_specs=[_row_spec(rt, D_MODEL), _row_spec(rt, D_MODEL), _row_spec(rt, D_FF), _gathered_spec(wg, "down"),
                  _mod_spec(rt), _vec_spec(D_MODEL)],
        out_specs=[_row_spec(rt, D_MODEL), _row_spec(rt, D_FF), _group_spec(rt), _vec_spec(D_MODEL)],
        out_shape=[jax.ShapeDtypeStruct((rt.rows, D_MODEL), BF16), jax.ShapeDtypeStruct((rt.rows, D_FF), BF16),
                   _group_shape(rt), _vec_shape()],
        args=[dh, y, ra, wg["down"][0], mod, g_post_mlp], aliases={}, semantics=("arbitrary",))


def _mlp_up_bwd(rt, da, wg, h1, dh, mod, g_pre_mlp, name):
    def body(da_ref, wu_ref, h1_ref, dh_ref, mod_ref, g_ref, dh1_ref, dsh_ref, dsc_ref, dg_ref):
        i = pl.program_id(0)
        du = jnp.zeros((rt.tm, D_MODEL), F32)
        for k in range(D_FF // 1024):
            du = du + lax.dot_general(da_ref[:, k * 1024:(k + 1) * 1024], _w_chunk(wu_ref, k), NT, preferred_element_type=F32)
        d, dsh, dsc, dg = _norm_mod_bwd_val(du, h1_ref[...], g_ref[...], 1.0 + mod_ref[0, 4:5, :])
        dh1_ref[...] = dh_ref[...] + d
        _accumulate(rt, i, [(dsh_ref, dsh), (dsc_ref, dsc)], [(dg_ref, dg)])

    return pl.pallas_call(
        body, name=name, grid=(rt.n_tiles,),
        in_specs=[_row_spec(rt, D_FF), _gathered_spec(wg, "up"), _row_spec(rt, D_MODEL), _row_spec(rt, D_MODEL),
                  _mod_spec(rt), _vec_spec(D_MODEL)],
        out_specs=[_row_spec(rt, D_MODEL), _group_spec(rt), _group_spec(rt), _vec_spec(D_MODEL)],
        out_shape=[jax.ShapeDtypeStruct((rt.rows, D_MODEL), F32), _group_shape(rt), _group_shape(rt), _vec_shape()],
        compiler_params=_params(("arbitrary",)),
    )(da, wg["up"][0], h1, dh, mod, g_pre_mlp)


def _wgrad_packed(rt, a, b, kind, off, n_rows, p_prev, name, comm=None):
    h = PACK_HEIGHT[kind]
    tk = rt.tm
    assert off % h == 0, (kind, off)

    def body(a_ref, b_ref, *rest):
        o_ref = rest[-1]
        i = pl.program_id(0)

        @pl.when(i == 0)
        def _():
            o_ref[...] = jnp.zeros_like(o_ref)

        if kind == "in":
            res = lax.dot_general(a_ref[...], b_ref[...], TN, preferred_element_type=F32)
            for k in range(4):
                for c in range(2):
                    for t in range(2):
                        o_ref[c, k, :, t * IN_PIECE_COLS:(t + 1) * IN_PIECE_COLS] += \
                            res[c * 512 + t * h:c * 512 + (t + 1) * h, k * IN_PIECE_COLS:(k + 1) * IN_PIECE_COLS]
        elif kind == "out":
            res = lax.dot_general(a_ref[...], b_ref[...], TN, preferred_element_type=F32)
            for k in range(4):
                for c in range(2):
                    o_ref[c, k] += res[(2 * k + c) * h:(2 * k + c + 1) * h]
        else:
            for k in range(4):
                if kind == "up":
                    res = lax.dot_general(a_ref[...], b_ref[:, k * 1024:(k + 1) * 1024], TN, preferred_element_type=F32)
                else:
                    ra = a_ref[:, k * 1024:(k + 1) * 1024].astype(F32)
                    res = lax.dot_general((ra * ra).astype(BF16), b_ref[...], TN, preferred_element_type=F32)
                o_ref[0, k] += res[0:h]
                o_ref[1, k] += res[h:2 * h]

    in_specs = [pl.BlockSpec((tk, a.shape[1]), lambda i: (i, 0)), pl.BlockSpec((tk, b.shape[1]), lambda i: (i, 0))]
    args = [a, b]
    aliases = {}
    if p_prev is not None:
        in_specs.append(pl.BlockSpec(memory_space=pl.ANY))
        args.append(p_prev)
        aliases = {2: 0}
    outs = _comm_call(
        body, comm, name=name, grid=(rt.n_tiles,),
        in_specs=in_specs,
        out_specs=[pl.BlockSpec((2, 4, h, 1024), lambda i: (0, 0, off // h, 0))],
        out_shape=[jax.ShapeDtypeStruct((2, 4, n_rows, 1024), F32)],
        args=args, aliases=aliases, semantics=("arbitrary",))
    return outs[0] if comm is None else outs


def _ada_wgrad(xs, dm, name):
    depth, _, cols = dm.shape

    def body(x_ref, d_ref, o_ref):
        for l in range(depth):
            o_ref[l] = lax.dot_general(x_ref[...], d_ref[l], TN, preferred_element_type=F32)

    return pl.pallas_call(body, name=name, out_shape=jax.ShapeDtypeStruct((depth, xs.shape[1], cols), F32),
                          compiler_params=pltpu.CompilerParams(vmem_limit_bytes=VMEM_LIMIT))(xs, dm)


def _stack_heads(x, kvi):
    x = x.astype(F32)
    tq = x.shape[0]
    lane = lax.broadcasted_iota(jnp.int32, (tq, 128), 1)
    keep = lane < HEAD_DIM if kvi == 0 else lane >= HEAD_DIM
    parts = []
    for p in range(2):
        pair = x[:, p * 128:(p + 1) * 128]
        swapped = pltpu.roll(pair, HEAD_DIM, 1)
        lo_head, hi_head = (pair, swapped) if kvi == 0 else (swapped, pair)
        parts += [jnp.where(keep, lo_head, 0.0), jnp.where(keep, hi_head, 0.0)]
    return jnp.concatenate(parts, axis=0).astype(BF16)


def _unstack_heads(o4, kvi):
    tq = o4.shape[0] // GROUP
    lane = lax.broadcasted_iota(jnp.int32, (tq, 128), 1)
    outs = []
    for p in range(2):
        r_lo, r_hi = o4[(2 * p) * tq:(2 * p + 1) * tq], o4[(2 * p + 1) * tq:(2 * p + 2) * tq]
        if kvi == 0:
            lo, hi = r_lo, pltpu.roll(r_hi, HEAD_DIM, 1)
        else:
            lo, hi = pltpu.roll(r_lo, HEAD_DIM, 1), r_hi
        outs.append(jnp.where(lane < HEAD_DIM, lo, hi))
    return jnp.concatenate(outs, axis=1)


def _per_head(shape, axis, tq, values):
    head = lax.broadcasted_iota(jnp.int32, shape, axis) // tq
    out = jnp.zeros(shape, F32)
    for g in range(GROUP):
        out = jnp.where(head == g, values[g], out)
    return out


KEY_CHUNK = 512
Q_TILE = 128
Q_TILE_FWD = 256


def _key_chunks(k_ref, v_ref, n, kc=KEY_CHUNK):
    kc = min(kc, n)
    return [(k_ref[c * kc:(c + 1) * kc, :], v_ref[c * kc:(c + 1) * kc, :], None) for c in range(n // kc)]


def _softmax_fwd(qs, chunks, sink_col):
    logits = []
    for k, _, mask in chunks:
        s = lax.dot_general(qs, k, NT, preferred_element_type=F32)
        logits.append(s if mask is None else jnp.where(mask, s, NEG_BIG))
    m = functools.reduce(jnp.maximum, [jnp.max(s, axis=1, keepdims=True) for s in logits])
    if sink_col is not None:
        m = jnp.maximum(m, sink_col)
    l = jnp.zeros_like(m) if sink_col is None else jnp.exp(sink_col - m)
    acc = jnp.zeros((qs.shape[0], 128), F32)
    for s, (_, v, _) in zip(logits, chunks):
        p = jnp.exp(s - m)
        l = l + jnp.sum(p, axis=1, keepdims=True)
        acc = acc + jnp.dot(p.astype(BF16), v, preferred_element_type=F32)
    return acc / l, m + jnp.log(l)


def _to_rows(col):
    return jnp.transpose(jnp.broadcast_to(col, (col.shape[0], 128)))[0:8, :]


def _softmax_bwd(qs, dos, lse_row, delta_row, chunks):
    dq = jnp.zeros((qs.shape[0], 128), F32)
    grads = []
    for k, v, mask in chunks:
        s = lax.dot_general(k, qs, NT, preferred_element_type=F32)
        if mask is not None:
            s = jnp.where(mask, s, NEG_BIG)
        p = jnp.exp(s - lse_row)
        dp = lax.dot_general(v, dos, NT, preferred_element_type=F32)
        ds = (p * (dp - delta_row)).astype(BF16)
        dv = jnp.dot(p.astype(BF16), dos, preferred_element_type=F32)
        dk = jnp.dot(ds, qs, preferred_element_type=F32)
        dq = dq + lax.dot_general(ds, k, TN, preferred_element_type=F32)
        grads.append((dk, dv))
    return dq, grads


def _band(qi, tq, seq):
    span = tq + 2 * WINDOW
    start = pl.multiple_of(jnp.clip(qi * tq - WINDOW, 0, seq - span), 64)
    return start, span


def _band_mask(qi, tq, start, span, query_axis):
    shape = (GROUP * tq, span) if query_axis == 0 else (span, GROUP * tq)
    qpos = qi * tq + lax.broadcasted_iota(jnp.int32, shape, query_axis) % tq
    kpos = start + lax.broadcasted_iota(jnp.int32, shape, 1 - query_axis)
    return jnp.abs(kpos - qpos) <= WINDOW


def _qkv_specs(rt, tq, q_row, ctx_row, with_latent):
    specs = [pl.BlockSpec((tq, 256), functools.partial(lambda b, i, col: (q_row(b, i), col), col=col)) for col in (0, 1, 3, 4)]
    if with_latent:
        specs += [pl.BlockSpec((rt.seq, 128), functools.partial(lambda b, i, col: (b, col), col=col))
                  for col in (COL_KA, COL_VA, COL_KB, COL_VB)]
    specs += [pl.BlockSpec((rt.ctx, 128), functools.partial(lambda b, i, col: (ctx_row(b), col), col=col))
              for col in (COL_KA, COL_VA, COL_KB, COL_VB)]
    return specs


def _attn_fwd(rt, qkvp, sink, o_prev, name, comm=None):
    latent = o_prev is None
    seq, ctx, nb = rt.seq, rt.ctx, rt.nb
    tq = Q_TILE_FWD if latent else ctx
    tile = Q_TILE if latent else ctx
    parts = tq // tile
    nq = seq // tq if latent else 1
    ctx_blk0 = rt.n_lat // ctx
    q_row = (lambda b, i: b * nq + i) if latent else (lambda b, i: ctx_blk0 + b)

    def store_lse(lse_ref, j, lse_col):
        rows = _to_rows(lse_col)
        for part in range(parts):
            lse_ref[part, j] = jnp.concatenate([rows[:, g * tq + part * tile:g * tq + (part + 1) * tile] for g in range(GROUP)], axis=1)

    def body(sink_ref, qa0, qa1, qb0, qb1, *rest):
        if latent:
            kal, val, kbl, vbl, kac, vac, kbc, vbc, o_ref, lse_ref = rest
        else:
            kac, vac, kbc, vbc, _, o_ref, lse_ref = rest
        qi = pl.program_id(1)
        for kvi, (qa, qb) in enumerate(((qa0, qb0), (qa1, qb1))):
            src_a = _key_chunks(kac, vac, ctx)
            src_b = _key_chunks(kbc, vbc, ctx)
            if latent:
                src_a += _key_chunks(kal, val, seq, seq)
                start, span = _band(qi, tq, seq)
                src_b.append((kbl[pl.ds(start, span), :], vbl[pl.ds(start, span), :], _band_mask(qi, tq, start, span, 0)))
            oa, lse = _softmax_fwd(_stack_heads(qa[...], kvi), src_a, None)
            o_ref[:, kvi * 256:(kvi + 1) * 256] = _unstack_heads(oa, kvi).astype(BF16)
            store_lse(lse_ref, kvi, lse)
            sink_col = _per_head((GROUP * tq, 1), 0, tq, [sink_ref[kvi * GROUP + g] for g in range(GROUP)])
            ob, lse = _softmax_fwd(_stack_heads(qb[...], kvi), src_b, sink_col)
            o_ref[:, 512 + kvi * 256:512 + (kvi + 1) * 256] = _unstack_heads(ob, kvi).astype(BF16)
            store_lse(lse_ref, 2 + kvi, lse)

    specs = _qkv_specs(rt, tq, q_row, lambda b: ctx_blk0 + b, latent)
    args = [sink] + [qkvp] * len(specs)
    in_specs = [pl.BlockSpec(memory_space=pltpu.SMEM)] + specs
    aliases = {}
    if not latent:
        in_specs.append(pl.BlockSpec(memory_space=pl.ANY))
        args.append(o_prev)
        aliases = {len(args) - 1: 0}
    return _comm_call(
        body, comm, name=name, grid=(nb, nq),
        in_specs=in_specs,
        out_specs=[pl.BlockSpec((tq, 1024), lambda b, i: (q_row(b, i), 0)),
                   pl.BlockSpec((parts, 4, 8, GROUP * tile), lambda b, i: (b * nq + i, 0, 0, 0))],
        out_shape=[jax.ShapeDtypeStruct((rt.rows, 1024), BF16), jax.ShapeDtypeStruct((nb * nq * parts, 4, 8, GROUP * tile), F32)],
        args=args, aliases=aliases, semantics=("parallel", "parallel"))


def _attn_bwd(rt, qkvp, o, lse, do, sink, prev, name, comm=None):
    latent = prev is None
    seq, ctx, nb = rt.seq, rt.ctx, rt.nb
    tq = Q_TILE if latent else ctx
    nq = seq // tq if latent else 1
    ctx_blk0 = rt.n_lat // ctx
    q_row = (lambda b, i: b * nq + i) if latent else (lambda b, i: ctx_blk0 + b)
    kc = min(KEY_CHUNK, seq)

    def body(sink_ref, qa0, qa1, qb0, qb1, *rest):
        if latent:
            kal, val, kbl, vbl, kac, vac, kbc, vbc, do_ref, o_ref, lse_ref, dq_ref, dl_ref, dc_ref, dsink_ref = rest
        else:
            kac, vac, kbc, vbc, do_ref, o_ref, lse_ref, c1_ref, _, _, dq_ref, dc_ref, dsink_ref = rest
        b, qi = pl.program_id(0), pl.program_id(1)

        def rows_of(cols, kvi, mixer):
            dos = _stack_heads(do_ref[:, cols], kvi)
            delta = jnp.sum(dos.astype(F32) * _stack_heads(o_ref[:, cols], kvi).astype(F32), axis=1, keepdims=True)
            return dos, lse_ref[0, 2 * mixer + kvi, 0:1, :], _to_rows(delta)[0:1, :]

        @pl.when(jnp.logical_and(b == 0, qi == 0))
        def _():
            dsink_ref[...] = jnp.zeros_like(dsink_ref)

        if latent:
            @pl.when(qi == 0)
            def _():
                dc_ref[...] = jnp.zeros_like(dc_ref)
                dl_ref[...] = jnp.zeros_like(dl_ref)
        else:
            dc_ref[...] = c1_ref[...]

        head_row = lax.broadcasted_iota(jnp.int32, (8, 128), 0)
        for kvi, (qa, qb) in enumerate(((qa0, qb0), (qa1, qb1))):
            cols = slice(kvi * 256, (kvi + 1) * 256)
            dos, lse_row, delta_row = rows_of(cols, kvi, 0)
            src = _key_chunks(kac, vac, ctx)
            if latent:
                src += _key_chunks(kal, val, seq)
            dq4, grads = _softmax_bwd(_stack_heads(qa[...], kvi), dos, lse_row, delta_row, src)
            dq_ref[:, cols] = _unstack_heads(dq4, kvi)
            dc_ref[:, 0:128] += grads[0][0]
            dc_ref[:, 128:256] += grads[0][1]
            for c, (dk, dv) in enumerate(grads[1:]):
                dl_ref[c * kc:(c + 1) * kc, 0:128] += dk
                dl_ref[c * kc:(c + 1) * kc, 128:256] += dv
            cols = slice(512 + kvi * 256, 512 + (kvi + 1) * 256)
            dos, lse_row, delta_row = rows_of(cols, kvi, 1)
            src = _key_chunks(kbc, vbc, ctx)
            if latent:
                start, span = _band(qi, tq, seq)
                src.append((kbl[pl.ds(start, span), :], vbl[pl.ds(start, span), :], _band_mask(qi, tq, start, span, 1)))
            dq4, grads = _softmax_bwd(_stack_heads(qb[...], kvi), dos, lse_row, delta_row, src)
            dq_ref[:, cols] = _unstack_heads(dq4, kvi)
            dc_ref[:, 256:384] += grads[0][0]
            dc_ref[:, 384:512] += grads[0][1]
            if latent:
                dl_ref[pl.ds(start, span), 256:384] += grads[1][0]
                dl_ref[pl.ds(start, span), 384:512] += grads[1][1]
            sink_row = _per_head((1, GROUP * tq), 1, tq, [sink_ref[kvi * GROUP + g] for g in range(GROUP)])
            dsink = -jnp.exp(sink_row - lse_row) * delta_row
            head = lax.broadcasted_iota(jnp.int32, (1, GROUP * tq), 1) // tq
            upd = jnp.zeros((8, 128), F32)
            for g in range(GROUP):
                upd = jnp.where(head_row == kvi * GROUP + g, jnp.sum(jnp.where(head == g, dsink, 0.0)), upd)
            dsink_ref[...] += upd

    specs = _qkv_specs(rt, tq, q_row, lambda b: ctx_blk0 + b, latent)
    q_rows_spec = pl.BlockSpec((tq, 1024), lambda b, i: (q_row(b, i), 0))
    in_specs = ([pl.BlockSpec(memory_space=pltpu.SMEM)] + specs
                + [q_rows_spec, q_rows_spec, pl.BlockSpec((1, 4, 8, GROUP * tq), lambda b, i: (b * nq + i, 0, 0, 0))])
    args = [sink] + [qkvp] * len(specs) + [do, o, lse]
    dq_shape = jax.ShapeDtypeStruct((rt.rows, 1024), F32)
    dkv_shape = jax.ShapeDtypeStruct((rt.rows, 512), F32)
    dsink_spec, dsink_shape = pl.BlockSpec((8, 128), lambda b, i: (0, 0)), jax.ShapeDtypeStruct((8, 128), F32)
    dq_spec = pl.BlockSpec((tq, 1024), lambda b, i: (q_row(b, i), 0))
    if latent:
        out_specs = [dq_spec, pl.BlockSpec((seq, 512), lambda b, i: (b, 0)), pl.BlockSpec((ctx, 512), lambda b, i: (b, 0)), dsink_spec]
        out_shape = [dq_shape, dkv_shape, jax.ShapeDtypeStruct((rt.n_ctx, 512), F32), dsink_shape]
        aliases = {}
    else:
        dq_prev, dkv_prev, c1 = prev
        in_specs += [pl.BlockSpec((ctx, 512), lambda b, i: (b, 0)), pl.BlockSpec(memory_space=pl.ANY), pl.BlockSpec(memory_space=pl.ANY)]
        args += [c1, dq_prev, dkv_prev]
        out_specs = [dq_spec, pl.BlockSpec((ctx, 512), lambda b, i: (ctx_blk0 + b, 0)), dsink_spec]
        out_shape = [dq_shape, dkv_shape, dsink_shape]
        aliases = {len(args) - 2: 0, len(args) - 1: 1}
    return _comm_call(body, comm, name=name, grid=(nb, nq), in_specs=in_specs, out_specs=out_specs, out_shape=out_shape,
                      args=args, aliases=aliases, semantics=("arbitrary", "arbitrary"))


def _silu(x):
    return x / (1.0 + jnp.exp(-x))


def _whole(shape):
    return pl.BlockSpec(shape, lambda i, s: (0,) * len(shape))


def _ada_half_spec(cols):
    return pl.BlockSpec((DEPTH, D_MODEL, cols), lambda i, s: (0, 0, s[0]))


def _ada_fwd(cond, w_ada, b_half, c_idx, name):
    rows = cond.shape[0]
    cols = w_ada.shape[2] // 2

    def body(s_ref, c_ref, w_ref, b_ref, x_ref, o_ref):
        xs = _silu(c_ref[...]).astype(BF16)
        x_ref[...] = xs
        for l in range(DEPTH):
            o_ref[l] = jnp.dot(xs, w_ref[l].astype(BF16), preferred_element_type=F32) + b_ref[l]

    grid_spec = pltpu.PrefetchScalarGridSpec(
        num_scalar_prefetch=1, grid=(1,),
        in_specs=[_whole(cond.shape), _ada_half_spec(cols), _whole(b_half.shape)],
        out_specs=[_whole((rows, D_MODEL)), _whole((DEPTH, rows, cols))])
    return pl.pallas_call(
        body, name=name, grid_spec=grid_spec,
        out_shape=[jax.ShapeDtypeStruct((rows, D_MODEL), BF16), jax.ShapeDtypeStruct((DEPTH, rows, cols), F32)],
        compiler_params=_params(("arbitrary",)),
    )(c_idx, cond, w_ada, b_half)


def _ada_cond_bwd(dcx, w_ada, c_idx, name):
    _, rows, cols = dcx.shape

    def body(s_ref, d_ref, w_ref, o_ref):
        acc = jnp.zeros((rows, D_MODEL), F32)
        for l in range(DEPTH):
            acc = acc + lax.dot_general(d_ref[l], w_ref[l].astype(BF16), NT, preferred_element_type=F32)
        o_ref[...] = acc

    grid_spec = pltpu.PrefetchScalarGridSpec(
        num_scalar_prefetch=1, grid=(1,),
        in_specs=[_whole(dcx.shape), _ada_half_spec(cols)], out_specs=_whole((rows, D_MODEL)))
    return pl.pallas_call(body, name=name, grid_spec=grid_spec, out_shape=jax.ShapeDtypeStruct((rows, D_MODEL), F32),
                          compiler_params=_params(("arbitrary",)))(c_idx, dcx, w_ada)


def _dev_sum(x, name):
    _, r, c = x.shape

    def body(x_ref, o_ref):
        v = x_ref[0]
        for d in range(1, N_DEV):
            v = v + x_ref[d]
        o_ref[...] = v

    return pl.pallas_call(body, name=name, out_shape=jax.ShapeDtypeStruct((r, c), F32))(x)


def _adam_val(w, g, m, v):
    c1 = 1.0 / (1.0 - ADAM_B1 ** ADAM_STEP)
    c2 = 1.0 / (1.0 - ADAM_B2 ** ADAM_STEP)
    nm = ADAM_B1 * m + (1.0 - ADAM_B1) * g
    nv = ADAM_B2 * v + (1.0 - ADAM_B2) * (g * g)
    return -ADAM_LR * ((nm * c1) / (jnp.sqrt(nv * c2) + ADAM_EPS) + ADAM_WD * w), nm, nv


def _small_update(tot, dcc_parts, params, n_groups, name):
    n_p = len(params)
    mod_rows = n_groups * N_MOD

    def body(tot_ref, dcc_ref, *refs):
        ins, outs = refs[:3 * n_p], refs[3 * n_p:]

        def update(p, rows, cols, g):
            w_ref, m_ref, v_ref = ins[3 * p:3 * p + 3]
            g_ref, d_ref, nm_ref, nv_ref = outs[4 * p:4 * p + 4]
            d, nm, nv = _adam_val(w_ref[rows, cols], g, m_ref[rows, cols], v_ref[rows, cols])
            g_ref[rows, cols] = g
            d_ref[rows, cols] = d
            nm_ref[rows, cols] = nm
            nv_ref[rows, cols] = nv

        acc = dcc_ref[0, 0:1, :]
        for d in range(1, N_DEV):
            acc = acc + dcc_ref[d, 0:1, :]
        c = ins[0][...]
        sg = 1.0 / (1.0 + jnp.exp(-c))
        update(0, slice(0, 1), slice(None), acc * (sg * (1.0 + c * (1.0 - sg))))
        for l in range(DEPTH):
            for i in range(N_MOD):
                g = tot_ref[l * mod_rows + i:l * mod_rows + i + 1, :]
                for grp in range(1, n_groups):
                    g = g + tot_ref[l * mod_rows + grp * N_MOD + i:l * mod_rows + grp * N_MOD + i + 1, :]
                update(1, slice(l, l + 1), slice(i * D_MODEL, (i + 1) * D_MODEL), g)
            for j in range(4):
                row = DEPTH * mod_rows + 4 * l + j
                update(2 + j, slice(l, l + 1), slice(None), tot_ref[row:row + 1, :])

    shapes = [jax.ShapeDtypeStruct(w.shape, F32) for w, _, _ in params for _ in range(4)]
    outs = pl.pallas_call(body, name=name, out_shape=shapes)(tot, dcc_parts, *[a for p in params for a in p])
    return [tuple(outs[4 * p:4 * p + 4]) for p in range(n_p)]


def _adamw(w, g, m, v, name):
    r, c = w.shape
    tr = _pick(r, (256, 128, 64, 32, 24, 16, 8))

    def body(w_ref, g_ref, m_ref, v_ref, d_ref, nm_ref, nv_ref):
        d_ref[...], nm_ref[...], nv_ref[...] = _adam_val(w_ref[...], g_ref[...], m_ref[...], v_ref[...])

    spec = pl.BlockSpec((tr, c), lambda i: (i, 0))
    return pl.pallas_call(body, name=name, grid=(r // tr,), in_specs=[spec] * 4, out_specs=[spec] * 3,
                          out_shape=[jax.ShapeDtypeStruct((r, c), F32)] * 3, compiler_params=_params(("parallel",)))(w, g, m, v)


def _adamw_shard(kind, l, w, m, v, halves, off, prev, name):
    h = PACK_HEIGHT[kind]
    assert off % h == 0, (kind, off)
    _, r, c = w.shape
    rows = r // 2

    def body(w_ref, m_ref, v_ref, p_ref, *rest):
        g_ref, d_ref, nm_ref, nv_ref = rest[-4:]
        if kind == "in":
            for t in range(2):
                g = p_ref[:, t * IN_PIECE_COLS:(t + 1) * IN_PIECE_COLS]
                rs = slice(t * h, (t + 1) * h)
                g_ref[rs, :] = g
                d_ref[rs, :], nm_ref[rs, :], nv_ref[rs, :] = _adam_val(w_ref[rs, :], g, m_ref[rs, :], v_ref[rs, :])
        else:
            g = p_ref[...]
            g_ref[...] = g
            d_ref[...], nm_ref[...], nv_ref[...] = _adam_val(w_ref[...], g, m_ref[...], v_ref[...])

    blk = pl.BlockSpec((None, rows, c), lambda half: (l, half, 0))
    in_specs = [blk, blk, blk, pl.BlockSpec((None, h, 1024), lambda half: (half, off // h, 0))]
    args = [w, m, v, halves]
    aliases = {}
    if prev is not None:
        in_specs += [pl.BlockSpec(memory_space=pl.ANY)] * 4
        args += list(prev)
        aliases = {4 + j: j for j in range(4)}
    return pl.pallas_call(
        body, name=name, grid=(2,), in_specs=in_specs, out_specs=[blk] * 4,
        out_shape=[jax.ShapeDtypeStruct(w.shape, F32)] * 4, input_output_aliases=aliases,
        compiler_params=_params(("parallel",)))(*args)


SMALL_ROWS = 48


def _small_rows(small, sq):
    def lane_pad(v):
        return jnp.pad(v, (0, D_MODEL - v.shape[0]))[None]

    head_rows = [lane_pad(jnp.concatenate([s["q_norm"][0], s["k_norm"][0], s["sink"]])) for s in small]
    loss_row = lane_pad((0.5 / D_MODEL) * jnp.sum(sq, keepdims=True)[0])
    rows = jnp.concatenate([s["mod"].reshape(-1, D_MODEL) for s in small] + [s["gammas"] for s in small] + head_rows + [loss_row], axis=0)
    return jnp.pad(rows, ((0, SMALL_ROWS - rows.shape[0]), (0, 0)))


def _local_step(x, ctx, target, mods, gam, qn, kn, sink, w_first, w_layers, packed, kc_idx):
    nb, seq, _ = x.shape
    rt = _Rows(nb, seq, ctx.shape[1])
    rt_lat = rt.latent_only()
    tables = _rope_tables(rt)
    fuse = packed is not None
    h = (x.reshape(rt.n_lat, D_MODEL), ctx.reshape(rt.n_ctx, D_MODEL))
    wg = [{}, {}] if fuse else [dict(w) for w in w_layers]
    wg[0]["in"] = (w_first, 0)
    if fuse:
        wg[0]["in_own"] = (packed, W_FIRST[0])
    saved = []
    for l in range(DEPTH):
        g_pre_mix, g_post_mix, g_pre_mlp, g_post_mlp = gam[l]
        if l == 0:
            u, qkv, qkvp, h = _in_fwd(rt, h, g_pre_mix, mods[l], wg[l], tables, qn[l], kn[l], f"in_fwd{l}")
        else:
            u, qkv, qkvp = _in_fwd(rt, h, g_pre_mix, mods[l], wg[l], tables, qn[l], kn[l], f"in_fwd{l}")
        if fuse and l == 0:
            o, lse_lat, w_mlp0, w_out0, w_mix1 = _attn_fwd(rt, qkvp, sink[l], None, f"attn_lat_fwd{l}",
                                                          comm=_gather_comm(packed, [W_MLP0, W_OUT0, W_MIX1], lead=2))
            wg[0].update({kind: (w_mlp0, PACK_OFF[(kind, 0)] - W_MLP0[0]) for kind in ("up", "down")})
            wg[0]["out"] = (w_out0, 0)
            wg[1] = {kind: (w_mix1, PACK_OFF[(kind, 1)] - W_MIX1[0]) for kind in ("out", "in")}
        elif fuse:
            o, lse_lat, w_mlp1 = _attn_fwd(rt, qkvp, sink[l], None, f"attn_lat_fwd{l}", comm=_gather_comm(packed, [W_MLP1], lead=2))
            wg[1].update({kind: (w_mlp1, PACK_OFF[(kind, 1)] - W_MLP1[0]) for kind in ("up", "down")})
        else:
            o, lse_lat = _attn_fwd(rt, qkvp, sink[l], None, f"attn_lat_fwd{l}")
        if l < DEPTH - 1:
            o, lse_ctx = _attn_fwd(rt, qkvp, sink[l], o, f"attn_ctx_fwd{l}")
            mix, h1, u2 = _out_fwd(rt, o, wg[l], h, mods[l], g_post_mix, g_pre_mlp, f"out_fwd{l}")
            r, y, h2 = _mlp_fwd(rt, u2, h1, wg[l], mods[l], g_post_mlp, f"mlp_fwd{l}")
        else:
            lse_ctx = None
            mix, h1, u2 = _out_fwd(rt_lat, o, wg[l], h, mods[l], g_post_mix, g_pre_mlp, f"out_fwd{l}")
            r, y, dh, sq = _mlp_fwd(rt_lat, u2, h1, wg[l], mods[l], g_post_mlp, f"mlp_fwd{l}", target=target.reshape(rt.n_lat, D_MODEL))
        saved.append((h, u, qkv, qkvp, o, lse_lat, lse_ctx, mix, h1, u2, r, y))
        h = h2

    small = [None] * DEPTH
    groups = {}
    for l in reversed(range(DEPTH)):
        g_pre_mix, g_post_mix, g_pre_mlp, g_post_mlp = gam[l]
        h0, u, qkv, qkvp, o, lse_lat, lse_ctx, mix, h1, u2, r, y = saved[l]
        mlp_group, mix_group = (G_LAYER1, G_LAYER1) if l == 1 else (G_MLP0, G_MIX0)
        hide = fuse and l == 0

        dead_ctx = l == DEPTH - 1
        rt_b = rt_lat if dead_ctx else rt
        outs = _mlp_down_bwd(rt_b, dh, y, r, wg[l], mods[l], g_post_mlp, f"mlp_down_bwd{l}",
                             comm=_pair_comm(groups[G_LAYER1]) if hide else None)
        dy, da, d_gate_m, d_g_post_mlp = outs[:4]
        if hide:
            sum1 = _pair_sum(groups[G_LAYER1], outs[4], kc_idx, "grad_pair_sum_layer1")
        p_mlp = _wgrad_packed(rt_b, r, dy, "down", PACK_OFF[("down", l)] - mlp_group[0], mlp_group[1], None, f"mlp_down_wgrad{l}")
        dh1, d_sh_m, d_sc_m, d_g_pre_mlp = _mlp_up_bwd(rt_b, da, wg[l], h1, dh, mods[l], g_pre_mlp, f"mlp_up_bwd{l}")
        p_mlp = _wgrad_packed(rt_b, u2, da, "up", PACK_OFF[("up", l)] - mlp_group[0], mlp_group[1], p_mlp, f"mlp_up_wgrad{l}")
        outs = _out_bwd(rt_b, dh1, mix, wg[l], mods[l], g_post_mix, f"out_bwd{l}", comm=_pair_comm(p_mlp) if hide else None)
        dmix, do, d_gate_a, d_g_post_mix = outs[:4]
        if hide:
            sum0 = _pair_sum(p_mlp, outs[4], kc_idx, "grad_pair_sum_mlp0")
        p_mix = _wgrad_packed(rt_b, o, dmix, "out", PACK_OFF[("out", l)] - mix_group[0], mix_group[1],
                              p_mlp if l == 1 else None, f"out_wgrad{l}")
        outs = _attn_bwd(rt, qkvp, o, lse_lat, do, sink[l], None, f"attn_lat_bwd{l}",
                         comm=_chip_comm([sum1[1], sum0[1]]) if hide else None)
        dq, dkv, dkv_c, dsink1 = outs[:4]
        if hide:
            groups[G_LAYER1] = _owner_sum(sum1[0], outs[4], kc_idx, "grad_owner_sum_layer1")
            groups[G_MLP0] = _owner_sum(sum0[0], outs[5], kc_idx, "grad_owner_sum_mlp0")
        if dead_ctx:
            dsink2 = jnp.zeros_like(dsink1)
            d_gate_m, d_sh_m, d_sc_m, d_gate_a = [a.at[nb].set(0.0) for a in (d_gate_m, d_sh_m, d_sc_m, d_gate_a)]
        else:
            dq, dkv, dsink2 = _attn_bwd(rt, qkvp, o, lse_ctx, do, sink[l], (dq, dkv, dkv_c), f"attn_ctx_bwd{l}")
        dqkv, dh, dqn, dkn, d_sh_a, d_sc_a, d_g_pre_mix = _in_bwd(rt, dq, dkv, qkv, tables, qn[l], kn[l], wg[l], h0, dh1, mods[l],
                                                                  g_pre_mix, l == 0, f"in_bwd{l}",
                                                                  dead_ctx_dkv=dkv_c if dead_ctx else None)
        dmod = jnp.concatenate([d_sh_a, d_sc_a, d_gate_a, d_sh_m, d_sc_m, d_gate_m], axis=1)
        small[l] = dict(mod=dmod, gammas=jnp.concatenate([d_g_pre_mix, d_g_post_mix, d_g_pre_mlp, d_g_post_mlp], axis=0),
                        q_norm=dqn, k_norm=dkn, sink=(dsink1 + dsink2)[:, 0])
        tail = _merge([_gather_comm(_small_rows(small, sq), [(0, SMALL_ROWS)]),
                       _halves_comm([groups[G_LAYER1], groups[G_MLP0]])]) if hide else None
        outs = _wgrad_packed(rt, u, dqkv, "in", PACK_OFF[("in", l)] - mix_group[0], mix_group[1], p_mix, f"in_wgrad{l}", comm=tail)
        if hide:
            groups[mix_group], small_g, groups[G_LAYER1], groups[G_MLP0] = outs
        else:
            groups[mix_group], small_g = outs, None
            if l == 0:
                groups[G_MLP0] = p_mlp
    return sq, dh.reshape(nb, seq, D_MODEL), [groups[G_LAYER1], groups[G_MLP0], groups[G_MIX0]], small, small_g


def kernel(x, c, ctx, c_ctx, w_ada, b_ada, g_pre_mix, g_post_mix, g_pre_mlp, g_post_mlp, w_in, q_norm, k_norm, sink, w_out, w_up, w_down, loss_target, m_c_ctx, m_w_ada, m_b_ada, m_g_pre_mix, m_g_post_mix, m_g_pre_mlp, m_g_post_mlp, m_w_in, m_q_norm, m_k_norm, m_sink, m_w_out, m_w_up, m_w_down, v_c_ctx, v_w_ada, v_b_ada, v_g_pre_mix, v_g_post_mix, v_g_pre_mlp, v_g_post_mlp, v_w_in, v_q_norm, v_k_norm, v_sink, v_w_out, v_w_up, v_w_down):
    nb = x.shape[0]
    ix, iy, ic = lax.axis_index("x"), lax.axis_index("y"), lax.axis_index("c")
    chip = 2 * ix + iy
    dev = 2 * chip + ic
    ada_cols = w_ada.shape[2] // 2

    packed = _pack_local_half(w_in, w_out, w_up, w_down, ic)
    c_rows = c.reshape(8, (nb * D_MODEL) // 8)
    c_all, = _comm_alone(_gather_comm(c_rows, [(0, c_rows.shape[0])]), "gather_c")
    c_all = c_all.reshape(N_DEV * nb, D_MODEL)
    n_cond = N_DEV * nb + 1
    cond_rows = 16 * ((n_cond + 15) // 16)
    cond = jnp.concatenate([c_all, c_ctx[None, :], jnp.zeros((cond_rows - n_cond, D_MODEL), F32)], axis=0)
    c_idx = ic.reshape(1).astype(jnp.int32)
    kc_idx = jnp.stack([chip, ic]).astype(jnp.int32)
    b_ada_half = lax.dynamic_slice_in_dim(b_ada, dev * ada_cols, ada_cols, 1)[:, None, :]
    x_ada, mod_part = _ada_fwd(cond, w_ada, b_ada_half, c_idx, "ada_fwd")
    mod_rows2d = mod_part.reshape(DEPTH * cond_rows, ada_cols)
    mod_g, w_first = _comm_alone(_merge([_gather_comm(mod_rows2d, [(0, mod_rows2d.shape[0])]),
                                         _gather_comm(packed, [W_FIRST], copy_own=False)]), "gather_mod_w_first")
    mod_all = mod_g.reshape(N_DEV, DEPTH, cond_rows, ada_cols).transpose(1, 2, 0, 3).reshape(DEPTH, cond_rows, N_MOD * D_MODEL)
    mods = []
    for l in range(DEPTH):
        mine = lax.dynamic_slice_in_dim(mod_all[l], dev * nb, nb, 0)
        mods.append(jnp.concatenate([mine, mod_all[l, n_cond - 1:n_cond]], axis=0).reshape(nb + 1, N_MOD, D_MODEL))

    gam = [(g_pre_mix[l][None], g_post_mix[l][None], g_pre_mlp[l][None], g_post_mlp[l][None]) for l in range(DEPTH)]
    qn = [jnp.tile(q_norm[l], 2)[None] for l in range(DEPTH)]
    kn = [jnp.tile(k_norm[l], 2)[None] for l in range(DEPTH)]
    _, grad_x, (h_layer1, h_mlp0, p_mix0), _, small_g = _local_step(x, ctx, loss_target, mods, gam, qn, kn, [sink[l] for l in range(DEPTH)],
                                                                 w_first, None, packed, kc_idx)

    def step(w, g, m, v, name):
        shape = w.shape
        cols = shape[-1]
        outs = _adamw(w.reshape(-1, cols), g.reshape(-1, cols), m.reshape(-1, cols), v.reshape(-1, cols), name)
        return tuple(a.reshape(shape) for a in outs)

    def shard_update(kind, w, m, v, layer0, layer1):
        outs = None
        for l, (halves, group) in enumerate((layer0, layer1)):
            outs = _adamw_shard(kind, l, w, m, v, halves, PACK_OFF[(kind, l)] - group[0], outs, f"adamw_w_{kind}{l}")
        return tuple(outs)

    tot = _dev_sum(small_g, "small_sum")
    mod_rows = (nb + 1) * N_MOD
    o_head = DEPTH * mod_rows + 4 * DEPTH
    loss = tot[o_head + DEPTH, 0]
    grad_q_norm = tot[o_head:o_head + DEPTH, 0:64] + tot[o_head:o_head + DEPTH, 64:128]
    grad_k_norm = tot[o_head:o_head + DEPTH, 128:192] + tot[o_head:o_head + DEPTH, 192:256]
    grad_sink = tot[o_head:o_head + DEPTH, 256:264]

    ex = small_g[:, :DEPTH * mod_rows].reshape(N_DEV, DEPTH, nb + 1, N_MOD * D_MODEL)[:, :, :nb]
    ex = ex.transpose(1, 0, 2, 3).reshape(DEPTH, N_DEV * nb, N_MOD * D_MODEL)
    cx = tot[:DEPTH * mod_rows].reshape(DEPTH, nb + 1, N_MOD * D_MODEL)[:, nb:]
    dm = jnp.concatenate([ex, cx, jnp.zeros((DEPTH, cond_rows - n_cond, N_MOD * D_MODEL), F32)], axis=1)
    shard_cols = w_ada.shape[2]
    grad_w_ada = _ada_wgrad(x_ada, lax.dynamic_slice_in_dim(dm, chip * shard_cols, shard_cols, 2).astype(BF16), "ada_wgrad")
    dcx = jnp.pad(lax.dynamic_slice_in_dim(cx, dev * ada_cols, ada_cols, 2), ((0, 0), (0, 15), (0, 0))).astype(BF16)
    dcc = _ada_cond_bwd(dcx, w_ada, c_idx, "ada_cond_bwd")[0:8]

    r1, = _comm_alone(_pair_comm(p_mix0), "grad_pair_exchange_mix0")
    a32, a16 = _pair_sum(p_mix0, r1, kc_idx, "grad_pair_sum_mix0")
    r2, dcc_g = _comm_alone(_merge([_chip_comm([a16]), _gather_comm(dcc, [(0, dcc.shape[0])])]), "grad_chip_exchange_mix0")
    h_mix0 = _owner_sum(a32, r2, kc_idx, "grad_owner_sum_mix0")
    h_mix0, = _comm_alone(_halves_comm([h_mix0]), "grad_halves_exchange_mix0")

    dense_names = ["c_ctx", "b_ada", "g_pre_mix", "g_post_mix", "g_pre_mlp", "g_post_mlp"]
    dense = _small_update(tot, dcc_g, [(c_ctx[None], m_c_ctx[None], v_c_ctx[None]), (b_ada, m_b_ada, v_b_ada),
                                       (g_pre_mix, m_g_pre_mix, v_g_pre_mix), (g_post_mix, m_g_post_mix, v_g_post_mix),
                                       (g_pre_mlp, m_g_pre_mlp, v_g_pre_mlp), (g_post_mlp, m_g_post_mlp, v_g_post_mlp)],
                          nb + 1, "small_update")
    res = {n: r for n, r in zip(dense_names, dense)}
    res["c_ctx"] = tuple(a[0] for a in res["c_ctx"])
    small_names = ["q_norm", "k_norm", "sink"]
    small_w = [q_norm, k_norm, sink]
    small_gr = [grad_q_norm, grad_k_norm, grad_sink]
    small_m = [m_q_norm, m_k_norm, m_sink]
    small_v = [v_q_norm, v_k_norm, v_sink]
    sizes = [int(np.prod(w.shape)) for w in small_w]
    total = sum(sizes)
    flat_rows = 8 * ((total + 8 * D_MODEL - 1) // (8 * D_MODEL))

    def flat(arrs, fill):
        f = jnp.concatenate([a.reshape(-1) for a in arrs])
        return jnp.concatenate([f, jnp.full((flat_rows * D_MODEL - total,), fill, F32)]).reshape(flat_rows, D_MODEL)

    sd, snm, snv = _adamw(flat(small_w, 0.0), flat(small_gr, 0.0), flat(small_m, 0.0), flat(small_v, 1.0), "adamw_small")[:3]

    def unflat(f):
        f = f.reshape(-1)
        out, off = [], 0
        for w, n in zip(small_w, sizes):
            out.append(f[off:off + n].reshape(w.shape))
            off += n
        return out

    small_d, small_nm, small_nv = unflat(sd), unflat(snm), unflat(snv)
    res.update({n: (g, d, nm, nv) for n, g, d, nm, nv in zip(small_names, small_gr, small_d, small_nm, small_nv)})
    res["w_ada"] = (grad_w_ada, *step(w_ada, grad_w_ada, m_w_ada, v_w_ada, "adamw_w_ada"))
    res["w_up"] = shard_update("up", w_up, m_w_up, v_w_up, (h_mlp0, G_MLP0), (h_layer1, G_LAYER1))
    res["w_down"] = shard_update("down", w_down, m_w_down, v_w_down, (h_mlp0, G_MLP0), (h_layer1, G_LAYER1))
    res["w_in"] = shard_update("in", w_in, m_w_in, v_w_in, (h_mix0, G_MIX0), (h_layer1, G_LAYER1))
    res["w_out"] = shard_update("out", w_out, m_w_out, v_w_out, (h_mix0, G_MIX0), (h_layer1, G_LAYER1))

    order = ["c_ctx", "w_ada", "b_ada", "g_pre_mix", "g_post_mix", "g_pre_mlp", "g_post_mlp", "w_in", "q_norm", "k_norm", "sink", "w_out", "w_up", "w_down"]
    return (loss, grad_x, *[res[n][0] for n in order], *[res[n][1] for n in order],
            *[res[n][2] for n in order], *[res[n][3] for n in order])
```

```python
import functools

import jax
import jax.numpy as jnp
import numpy as np
from jax import lax
from jax.experimental import pallas as pl
from jax.experimental.pallas import tpu as pltpu

F32 = jnp.float32
BF16 = jnp.bfloat16

D_MODEL = 1024
HEAD_DIM = 64
GROUP = 4
WINDOW = 128
N_MOD = 6
D_FF = 4 * D_MODEL
IN_COLS = 1536
GRID_W = 64
ROPE_THETA = 10000.0
EPS = 1e-6
NEG_BIG = -1e30
Q_SCALE = HEAD_DIM ** -0.5
DEPTH = 2
N_DEV = 8

ADAM_LR = 0.001
ADAM_B1 = 0.9
ADAM_B2 = 0.999
ADAM_EPS = 1e-08
ADAM_WD = 0.01
ADAM_STEP = 10

V7X_VMEM_BYTES = 64 * 1024 * 1024
VMEM_LIMIT = V7X_VMEM_BYTES - 8 * 1024 * 1024

MESH = pl.DeviceIdType.MESH
NT = (((1,), (1,)), ((), ()))
TN = (((0,), (0,)), ((), ()))

COL_KA, COL_VA, COL_KB, COL_VB = 4, 5, 10, 11
NORMED_COLS = 640

PACK_HEIGHT = {"up": 512, "down": 512, "in": 256, "out": 128}
IN_PIECE_COLS = 384
PACK_OFF = {("up", 0): 0, ("down", 0): 512, ("in", 0): 1024, ("out", 0): 1280,
            ("up", 1): 1408, ("down", 1): 1920, ("in", 1): 2432, ("out", 1): 2688}
PACK_ROWS = 2816
W_FIRST, W_MLP0, W_OUT0, W_MLP1, W_MIX1 = (1024, 256), (0, 1024), (1280, 128), (1408, 1024), (2432, 384)
G_LAYER1, G_MLP0, G_MIX0 = (1408, 1408), (0, 1024), (1024, 384)


def _pick(n, cands):
    for t in cands:
        if n % t == 0:
            return t
    raise ValueError(f"no tile for {n}")


def _params(sem):
    return pltpu.CompilerParams(dimension_semantics=sem, vmem_limit_bytes=VMEM_LIMIT)


class _Comm:
    def __init__(self, inputs, out_shapes, aliases, n_send, n_recv, start, finish, relay=None, lead=0):
        self.inputs, self.out_shapes, self.aliases = list(inputs), list(out_shapes), dict(aliases)
        self.n_send, self.n_recv, self.start, self.finish, self.relay, self.lead = n_send, n_recv, start, finish, relay, lead


def _comm_call(compute, comm, *, name, grid, in_specs, out_specs, out_shape, args, aliases, semantics, scratch=()):
    in_specs, out_specs, out_shape, args, aliases = list(in_specs), list(out_specs), list(out_shape), list(args), dict(aliases)
    scratch = list(scratch)
    if comm is None:
        return pl.pallas_call(compute, name=name, grid=grid, in_specs=in_specs, out_specs=out_specs, out_shape=out_shape,
                              input_output_aliases=aliases, scratch_shapes=scratch, compiler_params=_params(semantics))(*args)
    n_in, n_out, n_ci, n_co = len(args), len(out_shape), len(comm.inputs), len(comm.out_shapes)
    hbm = pl.BlockSpec(memory_space=pl.ANY)
    aliases.update({n_in + i: n_out + o for i, o in comm.aliases.items()})

    def body(*refs):
        ins, c_ins = refs[:n_in], refs[n_in:n_in + n_ci]
        outs, c_outs = refs[n_in + n_ci:n_in + n_ci + n_out], refs[n_in + n_ci + n_out:n_in + n_ci + n_out + n_co]
        scr = refs[n_in + n_ci + n_out + n_co:-2]
        send_sems, recv_sems = refs[-2:]
        ids = [pl.program_id(a) for a in range(len(grid))]
        first = functools.reduce(jnp.logical_and, [i == 0 for i in ids])
        last = functools.reduce(jnp.logical_and, [i == g - 1 for i, g in zip(ids, grid)])

        @pl.when(first)
        def _():
            comm.start(c_ins, c_outs, send_sems, recv_sems)

        compute(*ins, *outs, *scr)

        if comm.relay is not None:
            step = functools.reduce(lambda acc, ig: acc * ig[1] + ig[0], zip(ids, grid), 0)

            @pl.when(step == int(np.prod(grid)) - 1 - comm.lead)
            def _():
                comm.relay(c_ins, c_outs, send_sems, recv_sems)

        @pl.when(last)
        def _():
            comm.finish(c_ins, c_outs, send_sems, recv_sems)

    return pl.pallas_call(
        body, name=name, grid=grid,
        in_specs=in_specs + [hbm] * n_ci, out_specs=out_specs + [hbm] * n_co, out_shape=out_shape + comm.out_shapes,
        input_output_aliases=aliases,
        scratch_shapes=scratch + [pltpu.SemaphoreType.DMA((comm.n_send,)), pltpu.SemaphoreType.DMA((comm.n_recv,))],
        compiler_params=_params(("arbitrary",) * len(grid)),
    )(*args, *comm.inputs)


def _place():
    x_, y_, c_ = lax.axis_index("x"), lax.axis_index("y"), lax.axis_index("c")
    return x_, y_, c_, [(1 - x_, y_), (x_, 1 - y_), (1 - x_, 1 - y_)]


GATHER_SENDS, GATHER_RECVS = 8, 7


def _gather_copies(packed_ref, wg_ref, send_sems, recv_sems, rows, nth=0):
    r0, n = rows
    x_, y_, c_, chips = _place()
    me, sibling = (x_, y_, c_), (x_, y_, 1 - c_)
    src = packed_ref.at[pl.ds(r0, n), :]

    def slot(px, py, pc):
        return wg_ref.at[4 * px + 2 * py + pc]

    def copy(k, block, to, from_packed=False):
        return pltpu.make_async_remote_copy(src_ref=src if from_packed else slot(*block), dst_ref=slot(*block),
                                            send_sem=send_sems.at[GATHER_SENDS * nth + k], recv_sem=recv_sems.at[GATHER_RECVS * nth + k],
                                            device_id=to, device_id_type=MESH)

    own = [copy(0, me, sibling, True)] + [copy(1 + j, me, (*chip, c_), True) for j, chip in enumerate(chips)]
    passed = [copy(4 + j, (*chip, c_), sibling) for j, chip in enumerate(chips)]
    over_ici = [copy(1 + j, (*chip, c_), me) for j, chip in enumerate(chips)]
    from_sibling = [copy(0, sibling, me)] + [copy(4 + j, (*chip, 1 - c_), me) for j, chip in enumerate(chips)]
    mine = pltpu.make_async_copy(src, slot(*me), send_sems.at[GATHER_SENDS * nth + 7])
    return mine, own, passed, over_ici, from_sibling


def _gather_start(packed_ref, wg_ref, send_sems, recv_sems, rows, nth=0, copy_own=True):
    mine, own, _, _, _ = _gather_copies(packed_ref, wg_ref, send_sems, recv_sems, rows, nth)
    if copy_own:
        mine.start()
    for cp in own:
        cp.start()


def _gather_relay(packed_ref, wg_ref, send_sems, recv_sems, rows, nth=0):
    _, _, passed, over_ici, _ = _gather_copies(packed_ref, wg_ref, send_sems, recv_sems, rows, nth)
    for arrived, onward in zip(over_ici, passed):
        arrived.wait_recv()
        onward.start()


def _gather_finish(packed_ref, wg_ref, send_sems, recv_sems, rows, nth=0, copy_own=True):
    mine, own, passed, _, from_sibling = _gather_copies(packed_ref, wg_ref, send_sems, recv_sems, rows, nth)
    for arrived in from_sibling:
        arrived.wait_recv()
    for cp in own + passed:
        cp.wait_send()
    if copy_own:
        mine.wait()


def _gather_comm(packed, ranges, copy_own=True, lead=0):
    shapes = [jax.ShapeDtypeStruct((N_DEV, n, packed.shape[1]), packed.dtype) for _, n in ranges]

    def start(ins, outs, ss, rs):
        for nth, rows in enumerate(ranges):
            _gather_start(ins[0], outs[nth], ss, rs, rows, nth, copy_own)

    def relay(ins, outs, ss, rs):
        for nth, rows in enumerate(ranges):
            _gather_relay(ins[0], outs[nth], ss, rs, rows, nth)

    def finish(ins, outs, ss, rs):
        for nth, rows in enumerate(ranges):
            _gather_finish(ins[0], outs[nth], ss, rs, rows, nth, copy_own)

    return _Comm([packed], shapes, {}, GATHER_SENDS * len(ranges), GATHER_RECVS * len(ranges), start, finish, relay, lead)


def _pair_copy(p_ref, out_ref, send_sems, recv_sems):
    x_, y_, c_, _ = _place()
    return pltpu.make_async_remote_copy(src_ref=p_ref.at[1 - c_], dst_ref=out_ref,
                                        send_sem=send_sems.at[0], recv_sem=recv_sems.at[0],
                                        device_id=(x_, y_, 1 - c_), device_id_type=MESH)


def _pair_comm(p):
    return _Comm([p], [jax.ShapeDtypeStruct(p.shape[1:], p.dtype)], {}, 1, 1,
                 lambda ins, outs, ss, rs: _pair_copy(ins[0], outs[0], ss, rs).start(),
                 lambda ins, outs, ss, rs: _pair_copy(ins[0], outs[0], ss, rs).wait())


def _chip_copies(a_refs, out_refs, send_sems, recv_sems):
    _, _, c_, chips = _place()
    return [pltpu.make_async_remote_copy(src_ref=a_ref.at[2 * tx + ty], dst_ref=o_ref.at[j],
                                         send_sem=send_sems.at[3 * g + j], recv_sem=recv_sems.at[3 * g + j],
                                         device_id=(tx, ty, c_), device_id_type=MESH)
            for g, (a_ref, o_ref) in enumerate(zip(a_refs, out_refs)) for j, (tx, ty) in enumerate(chips)]


def _chip_start(a_refs, out_refs, send_sems, recv_sems):
    for cp in _chip_copies(a_refs, out_refs, send_sems, recv_sems):
        cp.start()


def _chip_finish(a_refs, out_refs, send_sems, recv_sems):
    for cp in _chip_copies(a_refs, out_refs, send_sems, recv_sems):
        cp.wait()


def _chip_comm(arrays):
    shapes = [jax.ShapeDtypeStruct((3,) + a.shape[1:], a.dtype) for a in arrays]
    return _Comm(arrays, shapes, {}, 3 * len(arrays), 3 * len(arrays), _chip_start, _chip_finish)


def _halves_copies(in_refs, out_refs, send_sems, recv_sems):
    x_, y_, c_, _ = _place()
    return [pltpu.make_async_remote_copy(src_ref=o_ref.at[c_], dst_ref=o_ref.at[c_], send_sem=send_sems.at[i], recv_sem=recv_sems.at[i],
                                         device_id=(x_, y_, 1 - c_), device_id_type=MESH)
            for i, o_ref in enumerate(out_refs)]


def _halves_start(in_refs, out_refs, send_sems, recv_sems):
    for cp in _halves_copies(in_refs, out_refs, send_sems, recv_sems):
        cp.start()


def _halves_finish(in_refs, out_refs, send_sems, recv_sems):
    for cp in _halves_copies(in_refs, out_refs, send_sems, recv_sems):
        cp.wait()


def _halves_comm(arrays):
    shapes = [jax.ShapeDtypeStruct(a.shape, a.dtype) for a in arrays]
    return _Comm(arrays, shapes, {i: i for i in range(len(arrays))}, len(arrays), len(arrays), _halves_start, _halves_finish)


class _SemSlice:
    class _At:
        def __init__(self, sems, first):
            self.sems, self.first = sems, first

        def __getitem__(self, k):
            return self.sems.at[self.first + k]

    def __init__(self, sems, first):
        self.at = _SemSlice._At(sems, first)


def _merge(comms):
    inputs = [a for c in comms for a in c.inputs]
    shapes = [s for c in comms for s in c.out_shapes]
    aliases, spans = {}, []
    i0 = o0 = s0 = r0 = 0
    for c in comms:
        aliases.update({i0 + i: o0 + o for i, o in c.aliases.items()})
        spans.append((slice(i0, i0 + len(c.inputs)), slice(o0, o0 + len(c.out_shapes)), s0, r0))
        i0, o0, s0, r0 = i0 + len(c.inputs), o0 + len(c.out_shapes), s0 + c.n_send, r0 + c.n_recv

    def start(ins, outs, ss, rs):
        for c, (i, o, s, r) in zip(comms, spans):
            c.start(ins[i], outs[o], _SemSlice(ss, s), _SemSlice(rs, r))

    def finish(ins, outs, ss, rs):
        for c, (i, o, s, r) in zip(comms, spans):
            if c.relay is not None:
                c.relay(ins[i], outs[o], _SemSlice(ss, s), _SemSlice(rs, r))
            c.finish(ins[i], outs[o], _SemSlice(ss, s), _SemSlice(rs, r))

    return _Comm(inputs, shapes, aliases, s0, r0, start, finish)


def _comm_alone(comm, name):
    n_ci = len(comm.inputs)
    hbm = pl.BlockSpec(memory_space=pl.ANY)

    def body(*refs):
        c_ins, c_outs, send_sems, recv_sems = refs[:n_ci], refs[n_ci:-2], refs[-2], refs[-1]
        comm.start(c_ins, c_outs, send_sems, recv_sems)
        if comm.relay is not None:
            comm.relay(c_ins, c_outs, send_sems, recv_sems)
        comm.finish(c_ins, c_outs, send_sems, recv_sems)

    return pl.pallas_call(
        body, name=name, out_shape=comm.out_shapes, in_specs=[hbm] * n_ci, out_specs=[hbm] * len(comm.out_shapes),
        input_output_aliases=comm.aliases,
        scratch_shapes=[pltpu.SemaphoreType.DMA((comm.n_send,)), pltpu.SemaphoreType.DMA((comm.n_recv,))],
    )(*comm.inputs)


SUM_TILES = (704, 512, 384, 320, 256, 192, 128, 64)


def _pair_sum(p, r1, kc_idx, name):
    _, _, n, c = p.shape
    tr = _pick(n, SUM_TILES)

    def body(s_ref, p_ref, r_ref, o32_ref, o16_ref):
        v = p_ref[...] + r_ref[...]
        o16_ref[...] = v.astype(BF16)

        @pl.when(pl.program_id(1) == s_ref[0])
        def _():
            o32_ref[...] = v

    blk = pl.BlockSpec((None, tr, c), lambda i, j, s: (j, i, 0))
    grid_spec = pltpu.PrefetchScalarGridSpec(
        num_scalar_prefetch=1, grid=(n // tr, 4),
        in_specs=[pl.BlockSpec((None, None, tr, c), lambda i, j, s: (s[1], j, i, 0)), blk],
        out_specs=[pl.BlockSpec((tr, c), lambda i, j, s: (i, 0)), blk])
    return pl.pallas_call(
        body, name=name, grid_spec=grid_spec,
        out_shape=[jax.ShapeDtypeStruct((n, c), F32), jax.ShapeDtypeStruct((4, n, c), BF16)],
        compiler_params=_params(("arbitrary", "arbitrary")),
    )(kc_idx, p, r1)


def _owner_sum(a32, r2, kc_idx, name):
    r, c = a32.shape
    tr = _pick(r, SUM_TILES)

    def body(s_ref, a_ref, r_ref, o_ref):
        v = a_ref[...]
        for j in range(3):
            v = v + r_ref[j].astype(F32)
        o_ref[...] = v

    grid_spec = pltpu.PrefetchScalarGridSpec(
        num_scalar_prefetch=1, grid=(r // tr,),
        in_specs=[pl.BlockSpec((tr, c), lambda i, s: (i, 0)),
                  pl.BlockSpec((3, tr, c), lambda i, s: (0, i, 0))],
        out_specs=pl.BlockSpec((None, tr, c), lambda i, s: (s[1], i, 0)))
    return pl.pallas_call(
        body, name=name, grid_spec=grid_spec,
        out_shape=jax.ShapeDtypeStruct((2, r, c), F32),
        compiler_params=_params(("arbitrary",)),
    )(kc_idx, a32, r2)


def _pack_local_half(w_in_s, w_out_s, w_up_s, w_down_s, comm, name):
    shards = {"in": w_in_s, "out": w_out_s, "up": w_up_s, "down": w_down_s}
    kinds = list(shards)
    assert sorted(off + PACK_HEIGHT[kind] for (kind, _), off in PACK_OFF.items())[:-1] == sorted(PACK_OFF.values())[1:]
    for kind in kinds:
        assert shards[kind].shape[1] == (4 if kind == "in" else 2) * PACK_HEIGHT[kind], (kind, shards[kind].shape)

    def body(*refs):
        w_refs, p_ref = dict(zip(kinds, refs[:4])), refs[4]
        scr, sems = dict(zip(kinds, refs[5:9])), refs[9]
        c = lax.axis_index("c")
        copies = {}
        for n, (kind, l) in enumerate(sorted(PACK_OFF)):
            rows = scr[kind].shape[1]
            copies[(kind, l)] = pltpu.make_async_copy(w_refs[kind].at[l, pl.ds(c * rows, rows)], scr[kind].at[l], sems.at[n])
            copies[(kind, l)].start()
        for (kind, l), off in sorted(PACK_OFF.items(), key=lambda kv: kv[1]):
            copies[(kind, l)].wait()
            h = PACK_HEIGHT[kind]
            if kind == "in":
                for t in range(2):
                    p_ref[off:off + h, t * IN_PIECE_COLS:(t + 1) * IN_PIECE_COLS] = scr[kind][l, t * h:(t + 1) * h, :].astype(BF16)
                p_ref[off:off + h, 2 * IN_PIECE_COLS:] = jnp.zeros((h, 1024 - 2 * IN_PIECE_COLS), BF16)
            else:
                p_ref[off:off + h, :] = scr[kind][l].astype(BF16)

    hbm = pl.BlockSpec(memory_space=pl.ANY)
    scratch = [pltpu.VMEM((DEPTH, shards[kind].shape[1] // 2, shards[kind].shape[2]), F32) for kind in kinds]
    outs = _comm_call(
        body, comm, name=name, grid=(1,), in_specs=[hbm] * 4,
        out_specs=[pl.BlockSpec((PACK_ROWS, 1024), lambda i: (0, 0))],
        out_shape=[jax.ShapeDtypeStruct((PACK_ROWS, 1024), BF16)],
        args=[shards[kind] for kind in kinds], aliases={}, semantics=("arbitrary",),
        scratch=scratch + [pltpu.SemaphoreType.DMA((len(PACK_OFF),))])
    return outs


def _unpack_in_pieces(w_ref, own_ref, w_scr):
    if own_ref is not None:
        me = 4 * lax.axis_index("x") + 2 * lax.axis_index("y") + lax.axis_index("c")
    for d in range(N_DEV):
        k, c = d // 2, d % 2
        for t in range(2):
            piece = w_ref[d, :, t * IN_PIECE_COLS:(t + 1) * IN_PIECE_COLS]
            if own_ref is not None:
                piece = jnp.where(me == d, own_ref[:, t * IN_PIECE_COLS:(t + 1) * IN_PIECE_COLS], piece)
            w_scr[c * 512 + t * 256:c * 512 + (t + 1) * 256, k * IN_PIECE_COLS:(k + 1) * IN_PIECE_COLS] = piece


def _in_weight_operands(wg):
    specs, args = [_gathered_spec(wg, "in")], [wg["in"][0]]
    if "in_own" in wg:
        own, off = wg["in_own"]
        h = PACK_HEIGHT["in"]
        assert off % h == 0
        specs.append(pl.BlockSpec((h, 1024), lambda *_: (off // h, 0), pipeline_mode=pl.Buffered(1)))
        args.append(own)
    return specs, args


class _Rows:
    def __init__(self, nb, seq, ctx):
        self.nb, self.seq, self.ctx = nb, seq, ctx
        self.n_lat, self.n_ctx = nb * seq, nb * ctx
        self.rows = self.n_lat + self.n_ctx
        self.tm = _pick(np.gcd(seq, self.n_ctx), (512, 256, 128))
        self.tiles_per_ex = seq // self.tm
        self.n_tiles = self.rows // self.tm
        self.n_lat_tiles = self.n_lat // self.tm
        self.groups = nb + 1

    def latent_only(self):
        rt = _Rows(self.nb, self.seq, self.ctx)
        rt.n_tiles = self.n_lat_tiles
        return rt

    def group(self, i):
        return jnp.minimum(i // self.tiles_per_ex, self.nb)

    def first_of_group(self, i):
        return jnp.logical_and(i % self.tiles_per_ex == 0, i <= self.n_lat_tiles)


def _mod_spec(rt):
    return pl.BlockSpec((1, N_MOD, D_MODEL), lambda i: (rt.group(i), 0, 0))


def _row_spec(rt, cols):
    return pl.BlockSpec((rt.tm, cols), lambda i: (i, 0))


def _vec_spec(cols):
    return pl.BlockSpec((1, cols), lambda i: (0, 0))


def _group_spec(rt):
    return pl.BlockSpec((1, 1, D_MODEL), lambda i: (rt.group(i), 0, 0))


def _gathered_spec(wg, kind):
    h, off = PACK_HEIGHT[kind], wg[kind][1]
    assert off % h == 0, (kind, off)
    return pl.BlockSpec((N_DEV, h, 1024), lambda *_: (0, off // h, 0), pipeline_mode=pl.Buffered(1))


def _group_shape(rt):
    return jax.ShapeDtypeStruct((rt.groups, 1, D_MODEL), F32)


def _vec_shape(cols=D_MODEL):
    return jax.ShapeDtypeStruct((1, cols), F32)


def _rms_inv(v):
    return lax.rsqrt(jnp.mean(v * v, axis=-1, keepdims=True) + EPS)


def _norm_mod_val(h_, g_, mod_ref, i_shift, i_scale):
    n = h_ * _rms_inv(h_) * g_
    return n * (1.0 + mod_ref[0, i_scale:i_scale + 1, :]) + mod_ref[0, i_shift:i_shift + 1, :]


def _post_norm_val(h_, z_, g_, mod_ref, i_gate):
    return h_ + mod_ref[0, i_gate:i_gate + 1, :] * (z_ * _rms_inv(z_) * g_)


def _post_norm_bwd_val(dh_, z_, g_, gate):
    rinv = _rms_inv(z_)
    n0 = z_ * rinv
    dn = dh_ * gate * g_
    dz = rinv * (dn - n0 * jnp.mean(dn * n0, axis=-1, keepdims=True))
    return dz, jnp.sum(dh_ * n0 * g_, axis=0, keepdims=True), jnp.sum(dh_ * gate * n0, axis=0, keepdims=True)


def _norm_mod_bwd_val(du_, h_, g_, one_sc):
    rinv = _rms_inv(h_)
    n0 = h_ * rinv
    dn = du_ * g_ * one_sc
    dh = rinv * (dn - n0 * jnp.mean(dn * n0, axis=-1, keepdims=True))
    return (dh, jnp.sum(du_, axis=0, keepdims=True), jnp.sum(du_ * n0 * g_, axis=0, keepdims=True),
            jnp.sum(du_ * one_sc * n0, axis=0, keepdims=True))


def _accumulate(rt, i, group_pairs, global_pairs):
    @pl.when(rt.first_of_group(i))
    def _():
        for ref, _ in group_pairs:
            ref[...] = jnp.zeros_like(ref)

    @pl.when(i == 0)
    def _():
        for ref, _ in global_pairs:
            ref[...] = jnp.zeros_like(ref)

    for ref, val in group_pairs:
        ref[0] += val
    for ref, val in global_pairs:
        ref[...] += val


def _rope_tables(rt):
    pos = np.arange(rt.seq)
    axis_dim = HEAD_DIM // 2
    inv = (ROPE_THETA ** (-np.arange(0, axis_dim, 2, dtype=np.float32) / axis_dim)).astype(np.float32)
    ang_r = (pos // GRID_W).astype(np.float32)[:, None] * inv[None, :]
    ang_c = (pos % GRID_W).astype(np.float32)[:, None] * inv[None, :]
    cr, sr, cc, sc = np.cos(ang_r), np.sin(ang_r), np.cos(ang_c), np.sin(ang_c)
    zero = np.zeros_like(sr)
    cos = np.concatenate([cr, cr, cc, cc], axis=1)
    s_lo = np.concatenate([zero, sr, zero, sc], axis=1)
    s_hi = np.concatenate([-sr, zero, -sc, zero], axis=1)

    def full(t, ctx_value):
        return jnp.asarray(np.concatenate([np.tile(t, (1, 2)), np.full((rt.tm, 128), ctx_value)], axis=0), F32)

    return full(cos, 1.0), full(s_lo, 0.0), full(s_hi, 0.0)


def _table_spec(rt):
    return pl.BlockSpec((rt.tm, 128), lambda i: (jnp.where(i < rt.n_lat_tiles, i % rt.tiles_per_ex, rt.tiles_per_ex), 0))


def _head_mean(x):
    r = lax.broadcasted_iota(jnp.int32, (128, 128), 0) // HEAD_DIM
    c = lax.broadcasted_iota(jnp.int32, (128, 128), 1) // HEAD_DIM
    ones = jnp.where(r == c, 1.0 / HEAD_DIM, 0.0).astype(F32)
    return jnp.dot(x, ones, preferred_element_type=F32, precision=lax.Precision.HIGH)


def _head_stats(t):
    return lax.rsqrt(_head_mean(t * t) + EPS)


def _prep_fwd_body(tm, qkv_ref, c, s1, s2, qn, kn, out_ref):
    def rope(t):
        return t * c + pltpu.roll(t, 16, 1) * s1 + pltpu.roll(t, 112, 1) * s2

    for j in range(12):
        t = qkv_ref[:, j * 128:(j + 1) * 128]
        if j < 4:
            t = rope(t * _head_stats(t) * qn) * Q_SCALE
        elif j == COL_KA:
            t = rope(t * _head_stats(t) * kn)
        elif 6 <= j < 10:
            t = rope(t) * Q_SCALE
        elif j == COL_KB:
            t = rope(t)
        out_ref[:, j * 128:(j + 1) * 128] = t.astype(BF16)


def _prep_bwd_body(dq, dkv, qkv_ref, c, s1, s2, qn, kn, out_ref):
    rows = slice(None)

    def rope_bwd(d):
        return d * c + pltpu.roll(d * s1, 112, 1) + pltpu.roll(d * s2, 16, 1)

    def norm_bwd(t, g, dy):
        rinv = _head_stats(t)
        n = t * rinv
        dn = dy * g
        return rinv * (dn - n * _head_mean(dn * n)), jnp.sum(dy * n, axis=0, keepdims=True)

    dqn = jnp.zeros((1, 128), F32)
    dkn = jnp.zeros((1, 128), F32)
    for j in range(12):
        if j < 4:
            d, dg = norm_bwd(qkv_ref[rows, j * 128:(j + 1) * 128], qn, rope_bwd(dq(slice(j * 128, (j + 1) * 128)) * Q_SCALE))
            dqn = dqn + dg
        elif j == COL_KA:
            d, dg = norm_bwd(qkv_ref[rows, j * 128:(j + 1) * 128], kn, rope_bwd(dkv(slice(0, 128))))
            dkn = dkn + dg
        elif j == COL_VA:
            d = dkv(slice(128, 256))
        elif j < 10:
            d = rope_bwd(dq(slice((j - 2) * 128, (j - 1) * 128)) * Q_SCALE)
        elif j == COL_KB:
            d = rope_bwd(dkv(slice(256, 384)))
        else:
            d = dkv(slice(384, 512))
        out_ref[rows, j * 128:(j + 1) * 128] = d.astype(BF16)
    return dqn, dkn


def _in_fwd(rt, h, gamma, mod, wg, tables, qn, kn, name):
    w_specs, w_args = _in_weight_operands(wg)
    n_w = len(w_args)
    joined = not isinstance(h, (tuple, list))
    n_h = 1 if joined else 2

    def body(*refs):
        g_ref, mod_ref = refs[n_h:n_h + 2]
        rest = refs[n_h + 2:]
        c_ref, s1_ref, s2_ref, qn_ref, kn_ref, u_ref, qkn_ref, qkvp_ref = rest[n_w:n_w + 8]
        qkv_ref, w_scr = rest[-2:]
        i = pl.program_id(0)

        @pl.when(i == 0)
        def _():
            _unpack_in_pieces(rest[0], rest[1] if n_w == 2 else None, w_scr)

        if joined:
            h_ = refs[0][...]
        else:
            h_ = jnp.where(i < rt.n_lat_tiles, refs[0][...], refs[1][...])
            rest[n_w + 8][...] = h_
        u = _norm_mod_val(h_, g_ref[...], mod_ref, 0, 1).astype(BF16)
        u_ref[...] = u
        qkv_ref[...] = jnp.dot(u, w_scr[...], preferred_element_type=F32)
        qkn_ref[...] = qkv_ref[:, 0:NORMED_COLS]
        _prep_fwd_body(rt.tm, qkv_ref, c_ref[...], s1_ref[...], s2_ref[...], qn_ref[...], kn_ref[...], qkvp_ref)

    if joined:
        h_specs, h_args = [_row_spec(rt, D_MODEL)], [h]
    else:
        h_specs = [pl.BlockSpec((rt.tm, D_MODEL), lambda i: (jnp.minimum(i, rt.n_lat_tiles - 1), 0)),
                   pl.BlockSpec((rt.tm, D_MODEL), lambda i: (jnp.maximum(i - rt.n_lat_tiles, 0), 0))]
        h_args = list(h)
    out_specs = [_row_spec(rt, D_MODEL), _row_spec(rt, NORMED_COLS), _row_spec(rt, IN_COLS)]
    out_shape = [jax.ShapeDtypeStruct((rt.rows, D_MODEL), BF16), jax.ShapeDtypeStruct((rt.rows, NORMED_COLS), F32),
                 jax.ShapeDtypeStruct((rt.rows, IN_COLS), BF16)]
    if not joined:
        out_specs.append(_row_spec(rt, D_MODEL))
        out_shape.append(jax.ShapeDtypeStruct((rt.rows, D_MODEL), F32))
    return pl.pallas_call(
        body, name=name, grid=(rt.n_tiles,),
        in_specs=h_specs + [_vec_spec(D_MODEL), _mod_spec(rt)] + w_specs + [_table_spec(rt)] * 3 + [_vec_spec(128)] * 2,
        out_specs=out_specs, out_shape=out_shape,
        scratch_shapes=[pltpu.VMEM((rt.tm, IN_COLS), F32), pltpu.VMEM((D_MODEL, IN_COLS), BF16)],
        compiler_params=_params(("arbitrary",)),
    )(*h_args, gamma, mod, *w_args, *tables, qn, kn)


def _in_bwd(rt, dq, dkv, qkv, tables, qn, kn, wg, h, dres, mod, gamma, latent_only, name, comm=None, dead_ctx_dkv=None):
    last = rt.n_lat_tiles - 1
    w_specs, w_args = _in_weight_operands(wg)
    n_w = len(w_args)
    n_dead = 0 if dead_ctx_dkv is None else 1

    def body(dq_ref, dkv_ref, qkv_ref, c_ref, s1_ref, s2_ref, qn_ref, kn_ref, *rest):
        h_ref, dres_ref, mod_ref, g_ref, dqkv_ref, dh_ref, dqn_ref, dkn_ref, dsh_ref, dsc_ref, dg_ref, w_scr = rest[n_w + n_dead:]
        i = pl.program_id(0)

        @pl.when(i == 0)
        def _():
            _unpack_in_pieces(rest[0], rest[1] if n_w == 2 else None, w_scr)

        if n_dead:
            c1_ref, lat = rest[n_w], i <= last
            load_dq = lambda cols: jnp.where(lat, dq_ref[:, cols], 0.0)
            load_dkv = lambda cols: jnp.where(lat, dkv_ref[:, cols], c1_ref[:, cols])
            dres_ = jnp.where(lat, dres_ref[...], 0.0)
        else:
            load_dq, load_dkv, dres_ = (lambda cols: dq_ref[:, cols]), (lambda cols: dkv_ref[:, cols]), dres_ref[...]
        dqn, dkn = _prep_bwd_body(load_dq, load_dkv, qkv_ref, c_ref[...], s1_ref[...], s2_ref[...], qn_ref[...], kn_ref[...], dqkv_ref)
        du = lax.dot_general(dqkv_ref[...], w_scr[...], NT, preferred_element_type=F32)
        dh, dsh, dsc, dg = _norm_mod_bwd_val(du, h_ref[...], g_ref[...], 1.0 + mod_ref[0, 1:2, :])
        if latent_only:
            @pl.when(i <= last)
            def _():
                dh_ref[...] = dres_ + dh
        else:
            dh_ref[...] = dres_ + dh
        _accumulate(rt, i, [(dsh_ref, dsh), (dsc_ref, dsc)], [(dg_ref, dg), (dqn_ref, dqn), (dkn_ref, dkn)])

    dh_spec = pl.BlockSpec((rt.tm, D_MODEL), lambda i: (jnp.minimum(i, last), 0)) if latent_only else _row_spec(rt, D_MODEL)
    dead_specs = [] if dead_ctx_dkv is None else [pl.BlockSpec((rt.tm, 512), lambda i: (jnp.maximum(i - rt.n_lat_tiles, 0), 0))]
    dead_args = [] if dead_ctx_dkv is None else [dead_ctx_dkv]
    return _comm_call(
        body, comm, name=name, grid=(rt.n_tiles,),
        in_specs=[_row_spec(rt, 1024), _row_spec(rt, 512), _row_spec(rt, NORMED_COLS)] + [_table_spec(rt)] * 3 + [_vec_spec(128)] * 2
        + w_specs + dead_specs + [_row_spec(rt, D_MODEL), _row_spec(rt, D_MODEL), _mod_spec(rt), _vec_spec(D_MODEL)],
        out_specs=[_row_spec(rt, IN_COLS), dh_spec, _vec_spec(128), _vec_spec(128),
                   _group_spec(rt), _group_spec(rt), _vec_spec(D_MODEL)],
        out_shape=[jax.ShapeDtypeStruct((rt.rows, IN_COLS), BF16),
                   jax.ShapeDtypeStruct((rt.n_lat if latent_only else rt.rows, D_MODEL), F32),
                   _vec_shape(128), _vec_shape(128), _group_shape(rt), _group_shape(rt), _vec_shape()],
        args=[dq, dkv, qkv, *tables, qn, kn, *w_args, *dead_args, h, dres, mod, gamma], aliases={}, semantics=("arbitrary",),
        scratch=[pltpu.VMEM((D_MODEL, IN_COLS), BF16)])


def _out_fwd(rt, o, wg, h, mod, g_post_mix, g_pre_mlp, name):
    def body(o_ref, w_ref, h_ref, mod_ref, gpost_ref, gpre_ref, mix_ref, h1_ref, u2_ref):
        mix = jnp.dot(o_ref[...], w_ref[...].reshape(D_MODEL, D_MODEL), preferred_element_type=F32)
        mix_ref[...] = mix
        h1 = _post_norm_val(h_ref[...], mix, gpost_ref[...], mod_ref, 2)
        h1_ref[...] = h1
        u2_ref[...] = _norm_mod_val(h1, gpre_ref[...], mod_ref, 3, 4).astype(BF16)

    return pl.pallas_call(
        body, name=name, grid=(rt.n_tiles,),
        in_specs=[_row_spec(rt, D_MODEL), _gathered_spec(wg, "out"), _row_spec(rt, D_MODEL), _mod_spec(rt),
                  _vec_spec(D_MODEL), _vec_spec(D_MODEL)],
        out_specs=[_row_spec(rt, D_MODEL)] * 3,
        out_shape=[jax.ShapeDtypeStruct((rt.rows, D_MODEL), F32), jax.ShapeDtypeStruct((rt.rows, D_MODEL), F32),
                   jax.ShapeDtypeStruct((rt.rows, D_MODEL), BF16)],
        compiler_params=_params(("parallel",)),
    )(o, wg["out"][0], h, mod, g_post_mix, g_pre_mlp)


def _out_bwd(rt, dh1, mix, wg, mod, g_post_mix, name, comm=None):
    def body(dh_ref, mix_ref, w_ref, mod_ref, g_ref, dmix_ref, do_ref, dgate_ref, dg_ref):
        i = pl.program_id(0)
        dz, dgate, dg = _post_norm_bwd_val(dh_ref[...], mix_ref[...], g_ref[...], mod_ref[0, 2:3, :])
        dzb = dz.astype(BF16)
        dmix_ref[...] = dzb
        do_ref[...] = lax.dot_general(dzb, w_ref[...].reshape(D_MODEL, D_MODEL), NT, preferred_element_type=F32).astype(BF16)
        _accumulate(rt, i, [(dgate_ref, dgate)], [(dg_ref, dg)])

    return _comm_call(
        body, comm, name=name, grid=(rt.n_tiles,),
        in_specs=[_row_spec(rt, D_MODEL), _row_spec(rt, D_MODEL), _gathered_spec(wg, "out"), _mod_spec(rt), _vec_spec(D_MODEL)],
        out_specs=[_row_spec(rt, D_MODEL), _row_spec(rt, D_MODEL), _group_spec(rt), _vec_spec(D_MODEL)],
        out_shape=[jax.ShapeDtypeStruct((rt.rows, D_MODEL), BF16), jax.ShapeDtypeStruct((rt.rows, D_MODEL), BF16),
                   _group_shape(rt), _vec_shape()],
        args=[dh1, mix, wg["out"][0], mod, g_post_mix], aliases={}, semantics=("arbitrary",))


def _w_chunk(w_ref, k):
    return w_ref[2 * k:2 * k + 2].reshape(1024, 1024)


def _mlp_fwd(rt, u2, h1, wg, mod, g_post_mlp, name, comm=None, target=None):
    last = rt.n_lat_tiles - 1

    def body(u2_ref, h1_ref, wu_ref, wd_ref, mod_ref, g_ref, *rest):
        u2_ = u2_ref[...]
        y = jnp.zeros((rt.tm, D_MODEL), F32)
        for k in range(D_FF // 1024):
            a = jnp.maximum(jnp.dot(u2_, _w_chunk(wu_ref, k), preferred_element_type=F32), 0.0)
            rest[-3 if target is None else -4][:, k * 1024:(k + 1) * 1024] = a.astype(BF16)
            y = y + jnp.dot((a * a).astype(BF16), _w_chunk(wd_ref, k), preferred_element_type=F32)
        h2 = _post_norm_val(h1_ref[...], y, g_ref[...], mod_ref, 5)
        if target is None:
            _, y_ref, h2_ref = rest
            y_ref[...] = y
            h2_ref[...] = h2
        else:
            t_ref, _, y_ref, dh_ref, sq_ref = rest
            y_ref[...] = y
            i = pl.program_id(0)

            @pl.when(i == 0)
            def _():
                sq_ref[...] = jnp.zeros_like(sq_ref)

            @pl.when(i <= last)
            def _():
                e = h2 - t_ref[...]
                dh_ref[...] = e * (1.0 / D_MODEL)
                sq_ref[...] += jnp.sum(e * e, axis=0, keepdims=True)

            @pl.when(i > last)
            def _():
                dh_ref[...] = jnp.zeros_like(dh_ref)

    in_specs = [_row_spec(rt, D_MODEL), _row_spec(rt, D_MODEL), _gathered_spec(wg, "up"), _gathered_spec(wg, "down"),
                _mod_spec(rt), _vec_spec(D_MODEL)]
    args = [u2, h1, wg["up"][0], wg["down"][0], mod, g_post_mlp]
    out_specs = [_row_spec(rt, D_FF), _row_spec(rt, D_MODEL), _row_spec(rt, D_MODEL)]
    out_shape = [jax.ShapeDtypeStruct((rt.rows, D_FF), BF16), jax.ShapeDtypeStruct((rt.rows, D_MODEL), F32),
                 jax.ShapeDtypeStruct((rt.rows, D_MODEL), F32)]
    if target is not None:
        in_specs.append(pl.BlockSpec((rt.tm, D_MODEL), lambda i: (jnp.minimum(i, last), 0)))
        args.append(target)
        out_specs.append(_vec_spec(D_MODEL))
        out_shape.append(_vec_shape())
    return _comm_call(body, comm, name=name, grid=(rt.n_tiles,), in_specs=in_specs, out_specs=out_specs, out_shape=out_shape,
                      args=args, aliases={}, semantics=("parallel",) if target is None else ("arbitrary",))


def _mlp_down_bwd(rt, dh, y, ra, wg, mod, g_post_mlp, name, comm=None):
    def body(dh_ref, y_ref, ra_ref, wd_ref, mod_ref, g_ref, dy_ref, da_ref, dgate_ref, dg_ref):
        i = pl.program_id(0)
        dz, dgate, dg = _post_norm_bwd_val(dh_ref[...], y_ref[...], g_ref[...], mod_ref[0, 5:6, :])
        dyb = dz.astype(BF16)
        dy_ref[...] = dyb
        for k in range(D_FF // 1024):
            dr = lax.dot_general(dyb, _w_chunk(wd_ref, k), NT, preferred_element_type=F32)
            da_ref[:, k * 1024:(k + 1) * 1024] = (dr * (2.0 * ra_ref[:, k * 1024:(k + 1) * 1024].astype(F32))).astype(BF16)
        _accumulate(rt, i, [(dgate_ref, dgate)], [(dg_ref, dg)])

    return _comm_call(
        body, comm, name=name, grid=(rt.n_tiles,),
        in_specs=[_row_spec(rt, D_MODEL), _row_spec(rt, D_MODEL), _row_spec(rt, D_FF), _gathered_spec(wg, "down"),
                  _mod_spec(rt), _vec_spec(D_MODEL)],
        out_specs=[_row_spec(rt, D_MODEL), _row_spec(rt, D_FF), _group_spec(rt), _vec_spec(D_MODEL)],
        out_shape=[jax.ShapeDtypeStruct((rt.rows, D_MODEL), BF16), jax.ShapeDtypeStruct((rt.rows, D_FF), BF16),
                   _group_shape(rt), _vec_shape()],
        args=[dh, y, ra, wg["down"][0], mod, g_post_mlp], aliases={}, semantics=("arbitrary",))


def _mlp_up_bwd(rt, da, wg, h1, dh, mod, g_pre_mlp, name):
    def body(da_ref, wu_ref, h1_ref, dh_ref, mod_ref, g_ref, dh1_ref, dsh_ref, dsc_ref, dg_ref):
        i = pl.program_id(0)
        du = jnp.zeros((rt.tm, D_MODEL), F32)
        for k in range(D_FF // 1024):
            du = du + lax.dot_general(da_ref[:, k * 1024:(k + 1) * 1024], _w_chunk(wu_ref, k), NT, preferred_element_type=F32)
        d, dsh, dsc, dg = _norm_mod_bwd_val(du, h1_ref[...], g_ref[...], 1.0 + mod_ref[0, 4:5, :])
        dh1_ref[...] = dh_ref[...] + d
        _accumulate(rt, i, [(dsh_ref, dsh), (dsc_ref, dsc)], [(dg_ref, dg)])

    return pl.pallas_call(
        body, name=name, grid=(rt.n_tiles,),
        in_specs=[_row_spec(rt, D_FF), _gathered_spec(wg, "up"), _row_spec(rt, D_MODEL), _row_spec(rt, D_MODEL),
                  _mod_spec(rt), _vec_spec(D_MODEL)],
        out_specs=[_row_spec(rt, D_MODEL), _group_spec(rt), _group_spec(rt), _vec_spec(D_MODEL)],
        out_shape=[jax.ShapeDtypeStruct((rt.rows, D_MODEL), F32), _group_shape(rt), _group_shape(rt), _vec_shape()],
        compiler_params=_params(("arbitrary",)),
    )(da, wg["up"][0], h1, dh, mod, g_pre_mlp)


def _wgrad_packed(rt, a, b, kind, off, n_rows, p_prev, name, comm=None):
    h = PACK_HEIGHT[kind]
    tk = rt.tm
    assert off % h == 0, (kind, off)

    def body(a_ref, b_ref, *rest):
        o_ref = rest[-1]
        i = pl.program_id(0)

        @pl.when(i == 0)
        def _():
            o_ref[...] = jnp.zeros_like(o_ref)

        if kind == "in":
            res = lax.dot_general(a_ref[...], b_ref[...], TN, preferred_element_type=F32)
            for k in range(4):
                for c in range(2):
                    for t in range(2):
                        o_ref[c, k, :, t * IN_PIECE_COLS:(t + 1) * IN_PIECE_COLS] += \
                            res[c * 512 + t * h:c * 512 + (t + 1) * h, k * IN_PIECE_COLS:(k + 1) * IN_PIECE_COLS]
        elif kind == "out":
            res = lax.dot_general(a_ref[...], b_ref[...], TN, preferred_element_type=F32)
            for k in range(4):
                for c in range(2):
                    o_ref[c, k] += res[(2 * k + c) * h:(2 * k + c + 1) * h]
        else:
            for k in range(4):
                if kind == "up":
                    res = lax.dot_general(a_ref[...], b_ref[:, k * 1024:(k + 1) * 1024], TN, preferred_element_type=F32)
                else:
                    ra = a_ref[:, k * 1024:(k + 1) * 1024].astype(F32)
                    res = lax.dot_general((ra * ra).astype(BF16), b_ref[...], TN, preferred_element_type=F32)
                o_ref[0, k] += res[0:h]
                o_ref[1, k] += res[h:2 * h]

    in_specs = [pl.BlockSpec((tk, a.shape[1]), lambda i: (i, 0)), pl.BlockSpec((tk, b.shape[1]), lambda i: (i, 0))]
    args = [a, b]
    aliases = {}
    if p_prev is not None:
        in_specs.append(pl.BlockSpec(memory_space=pl.ANY))
        args.append(p_prev)
        aliases = {2: 0}
    outs = _comm_call(
        body, comm, name=name, grid=(rt.n_tiles,),
        in_specs=in_specs,
        out_specs=[pl.BlockSpec((2, 4, h, 1024), lambda i: (0, 0, off // h, 0))],
        out_shape=[jax.ShapeDtypeStruct((2, 4, n_rows, 1024), F32)],
        args=args, aliases=aliases, semantics=("arbitrary",))
    return outs[0] if comm is None else outs


def _ada_wgrad(xs, dm, name):
    depth, _, cols = dm.shape

    def body(x_ref, d_ref, o_ref):
        for l in range(depth):
            o_ref[l] = lax.dot_general(x_ref[...], d_ref[l], TN, preferred_element_type=F32)

    return pl.pallas_call(body, name=name, out_shape=jax.ShapeDtypeStruct((depth, xs.shape[1], cols), F32),
                          compiler_params=pltpu.CompilerParams(vmem_limit_bytes=VMEM_LIMIT))(xs, dm)


def _stack_heads(x, kvi):
    x = x.astype(F32)
    tq = x.shape[0]
    lane = lax.broadcasted_iota(jnp.int32, (tq, 128), 1)
    keep = lane < HEAD_DIM if kvi == 0 else lane >= HEAD_DIM
    parts = []
    for p in range(2):
        pair = x[:, p * 128:(p + 1) * 128]
        swapped = pltpu.roll(pair, HEAD_DIM, 1)
        lo_head, hi_head = (pair, swapped) if kvi == 0 else (swapped, pair)
        parts += [jnp.where(keep, lo_head, 0.0), jnp.where(keep, hi_head, 0.0)]
    return jnp.concatenate(parts, axis=0).astype(BF16)


def _unstack_heads(o4, kvi):
    tq = o4.shape[0] // GROUP
    lane = lax.broadcasted_iota(jnp.int32, (tq, 128), 1)
    outs = []
    for p in range(2):
        r_lo, r_hi = o4[(2 * p) * tq:(2 * p + 1) * tq], o4[(2 * p + 1) * tq:(2 * p + 2) * tq]
        if kvi == 0:
            lo, hi = r_lo, pltpu.roll(r_hi, HEAD_DIM, 1)
        else:
            lo, hi = pltpu.roll(r_lo, HEAD_DIM, 1), r_hi
        outs.append(jnp.where(lane < HEAD_DIM, lo, hi))
    return jnp.concatenate(outs, axis=1)


def _per_head(shape, axis, tq, values):
    head = lax.broadcasted_iota(jnp.int32, shape, axis) // tq
    out = jnp.zeros(shape, F32)
    for g in range(GROUP):
        out = jnp.where(head == g, values[g], out)
    return out


KEY_CHUNK = 512
Q_TILE = 128
Q_TILE_FWD = 256


def _key_chunks(k_ref, v_ref, n, kc=KEY_CHUNK):
    kc = min(kc, n)
    return [(k_ref[c * kc:(c + 1) * kc, :], v_ref[c * kc:(c + 1) * kc, :], None) for c in range(n // kc)]


def _softmax_fwd(qs, chunks, sink_col):
    logits = []
    for k, _, mask in chunks:
        s = lax.dot_general(qs, k, NT, preferred_element_type=F32)
        logits.append(s if mask is None else jnp.where(mask, s, NEG_BIG))
    m = functools.reduce(jnp.maximum, [jnp.max(s, axis=1, keepdims=True) for s in logits])
    if sink_col is not None:
        m = jnp.maximum(m, sink_col)
    l = jnp.zeros_like(m) if sink_col is None else jnp.exp(sink_col - m)
    acc = jnp.zeros((qs.shape[0], 128), F32)
    for s, (_, v, _) in zip(logits, chunks):
        p = jnp.exp(s - m)
        l = l + jnp.sum(p, axis=1, keepdims=True)
        acc = acc + jnp.dot(p.astype(BF16), v, preferred_element_type=F32)
    return acc / l, m + jnp.log(l)


def _to_rows(col):
    return jnp.transpose(jnp.broadcast_to(col, (col.shape[0], 128)))[0:8, :]


def _softmax_bwd(qs, dos, lse_row, delta_row, chunks):
    dq = jnp.zeros((qs.shape[0], 128), F32)
    grads = []
    for k, v, mask in chunks:
        s = lax.dot_general(k, qs, NT, preferred_element_type=F32)
        if mask is not None:
            s = jnp.where(mask, s, NEG_BIG)
        p = jnp.exp(s - lse_row)
        dp = lax.dot_general(v, dos, NT, preferred_element_type=F32)
        ds = (p * (dp - delta_row)).astype(BF16)
        dv = jnp.dot(p.astype(BF16), dos, preferred_element_type=F32)
        dk = jnp.dot(ds, qs, preferred_element_type=F32)
        dq = dq + lax.dot_general(ds, k, TN, preferred_element_type=F32)
        grads.append((dk, dv))
    return dq, grads


def _band(qi, tq, seq):
    span = tq + 2 * WINDOW
    start = pl.multiple_of(jnp.clip(qi * tq - WINDOW, 0, seq - span), 64)
    return start, span


def _band_mask(qi, tq, start, span, query_axis):
    shape = (GROUP * tq, span) if query_axis == 0 else (span, GROUP * tq)
    qpos = qi * tq + lax.broadcasted_iota(jnp.int32, shape, query_axis) % tq
    kpos = start + lax.broadcasted_iota(jnp.int32, shape, 1 - query_axis)
    return jnp.abs(kpos - qpos) <= WINDOW


def _qkv_specs(rt, tq, q_row, ctx_row, with_latent):
    specs = [pl.BlockSpec((tq, 256), functools.partial(lambda b, i, col: (q_row(b, i), col), col=col)) for col in (0, 1, 3, 4)]
    if with_latent:
        specs += [pl.BlockSpec((rt.seq, 128), functools.partial(lambda b, i, col: (b, col), col=col))
                  for col in (COL_KA, COL_VA, COL_KB, COL_VB)]
    specs += [pl.BlockSpec((rt.ctx, 128), functools.partial(lambda b, i, col: (ctx_row(b), col), col=col))
              for col in (COL_KA, COL_VA, COL_KB, COL_VB)]
    return specs


def _attn_fwd(rt, qkvp, sink, o_prev, name, comm=None):
    latent = o_prev is None
    seq, ctx, nb = rt.seq, rt.ctx, rt.nb
    tq = Q_TILE_FWD if latent else ctx
    tile = Q_TILE if latent else ctx
    parts = tq // tile
    nq = seq // tq if latent else 1
    ctx_blk0 = rt.n_lat // ctx
    q_row = (lambda b, i: b * nq + i) if latent else (lambda b, i: ctx_blk0 + b)

    def store_lse(lse_ref, j, lse_col):
        rows = _to_rows(lse_col)
        for part in range(parts):
            lse_ref[part, j] = jnp.concatenate([rows[:, g * tq + part * tile:g * tq + (part + 1) * tile] for g in range(GROUP)], axis=1)

    def body(sink_ref, qa0, qa1, qb0, qb1, *rest):
        if latent:
            kal, val, kbl, vbl, kac, vac, kbc, vbc, o_ref, lse_ref = rest
        else:
            kac, vac, kbc, vbc, _, o_ref, lse_ref = rest
        qi = pl.program_id(1)
        for kvi, (qa, qb) in enumerate(((qa0, qb0), (qa1, qb1))):
            src_a = _key_chunks(kac, vac, ctx)
            src_b = _key_chunks(kbc, vbc, ctx)
            if latent:
                src_a += _key_chunks(kal, val, seq, seq)
                start, span = _band(qi, tq, seq)
                src_b.append((kbl[pl.ds(start, span), :], vbl[pl.ds(start, span), :], _band_mask(qi, tq, start, span, 0)))
            oa, lse = _softmax_fwd(_stack_heads(qa[...], kvi), src_a, None)
            o_ref[:, kvi * 256:(kvi + 1) * 256] = _unstack_heads(oa, kvi).astype(BF16)
            store_lse(lse_ref, kvi, lse)
            sink_col = _per_head((GROUP * tq, 1), 0, tq, [sink_ref[kvi * GROUP + g] for g in range(GROUP)])
            ob, lse = _softmax_fwd(_stack_heads(qb[...], kvi), src_b, sink_col)
            o_ref[:, 512 + kvi * 256:512 + (kvi + 1) * 256] = _unstack_heads(ob, kvi).astype(BF16)
            store_lse(lse_ref, 2 + kvi, lse)

    specs = _qkv_specs(rt, tq, q_row, lambda b: ctx_blk0 + b, latent)
    args = [sink] + [qkvp] * len(specs)
    in_specs = [pl.BlockSpec(memory_space=pltpu.SMEM)] + specs
    aliases = {}
    if not latent:
        in_specs.append(pl.BlockSpec(memory_space=pl.ANY))
        args.append(o_prev)
        aliases = {len(args) - 1: 0}
    return _comm_call(
        body, comm, name=name, grid=(nb, nq),
        in_specs=in_specs,
        out_specs=[pl.BlockSpec((tq, 1024), lambda b, i: (q_row(b, i), 0)),
                   pl.BlockSpec((parts, 4, 8, GROUP * tile), lambda b, i: (b * nq + i, 0, 0, 0))],
        out_shape=[jax.ShapeDtypeStruct((rt.rows, 1024), BF16), jax.ShapeDtypeStruct((nb * nq * parts, 4, 8, GROUP * tile), F32)],
        args=args, aliases=aliases, semantics=("parallel", "parallel"))


def _attn_bwd(rt, qkvp, o, lse, do, sink, prev, name, comm=None):
    latent = prev is None
    seq, ctx, nb = rt.seq, rt.ctx, rt.nb
    tq = Q_TILE if latent else ctx
    nq = seq // tq if latent else 1
    ctx_blk0 = rt.n_lat // ctx
    q_row = (lambda b, i: b * nq + i) if latent else (lambda b, i: ctx_blk0 + b)
    kc = min(KEY_CHUNK, seq)

    def body(sink_ref, qa0, qa1, qb0, qb1, *rest):
        if latent:
            kal, val, kbl, vbl, kac, vac, kbc, vbc, do_ref, o_ref, lse_ref, dq_ref, dl_ref, dc_ref, dsink_ref = rest
        else:
            kac, vac, kbc, vbc, do_ref, o_ref, lse_ref, c1_ref, _, _, dq_ref, dc_ref, dsink_ref = rest
        b, qi = pl.program_id(0), pl.program_id(1)

        def rows_of(cols, kvi, mixer):
            dos = _stack_heads(do_ref[:, cols], kvi)
            delta = jnp.sum(dos.astype(F32) * _stack_heads(o_ref[:, cols], kvi).astype(F32), axis=1, keepdims=True)
            return dos, lse_ref[0, 2 * mixer + kvi, 0:1, :], _to_rows(delta)[0:1, :]

        @pl.when(jnp.logical_and(b == 0, qi == 0))
        def _():
            dsink_ref[...] = jnp.zeros_like(dsink_ref)

        if latent:
            @pl.when(qi == 0)
            def _():
                dc_ref[...] = jnp.zeros_like(dc_ref)
                dl_ref[...] = jnp.zeros_like(dl_ref)
        else:
            dc_ref[...] = c1_ref[...]

        head_row = lax.broadcasted_iota(jnp.int32, (8, 128), 0)
        for kvi, (qa, qb) in enumerate(((qa0, qb0), (qa1, qb1))):
            cols = slice(kvi * 256, (kvi + 1) * 256)
            dos, lse_row, delta_row = rows_of(cols, kvi, 0)
            src = _key_chunks(kac, vac, ctx)
            if latent:
                src += _key_chunks(kal, val, seq)
            dq4, grads = _softmax_bwd(_stack_heads(qa[...], kvi), dos, lse_row, delta_row, src)
            dq_ref[:, cols] = _unstack_heads(dq4, kvi)
            dc_ref[:, 0:128] += grads[0][0]
            dc_ref[:, 128:256] += grads[0][1]
            for c, (dk, dv) in enumerate(grads[1:]):
                dl_ref[c * kc:(c + 1) * kc, 0:128] += dk
                dl_ref[c * kc:(c + 1) * kc, 128:256] += dv
            cols = slice(512 + kvi * 256, 512 + (kvi + 1) * 256)
            dos, lse_row, delta_row = rows_of(cols, kvi, 1)
            src = _key_chunks(kbc, vbc, ctx)
            if latent:
                start, span = _band(qi, tq, seq)
                src.append((kbl[pl.ds(start, span), :], vbl[pl.ds(start, span), :], _band_mask(qi, tq, start, span, 1)))
            dq4, grads = _softmax_bwd(_stack_heads(qb[...], kvi), dos, lse_row, delta_row, src)
            dq_ref[:, cols] = _unstack_heads(dq4, kvi)
            dc_ref[:, 256:384] += grads[0][0]
            dc_ref[:, 384:512] += grads[0][1]
            if latent:
                dl_ref[pl.ds(start, span), 256:384] += grads[1][0]
                dl_ref[pl.ds(start, span), 384:512] += grads[1][1]
            sink_row = _per_head((1, GROUP * tq), 1, tq, [sink_ref[kvi * GROUP + g] for g in range(GROUP)])
            dsink = -jnp.exp(sink_row - lse_row) * delta_row
            head = lax.broadcasted_iota(jnp.int32, (1, GROUP * tq), 1) // tq
            upd = jnp.zeros((8, 128), F32)
            for g in range(GROUP):
                upd = jnp.where(head_row == kvi * GROUP + g, jnp.sum(jnp.where(head == g, dsink, 0.0)), upd)
            dsink_ref[...] += upd

    specs = _qkv_specs(rt, tq, q_row, lambda b: ctx_blk0 + b, latent)
    q_rows_spec = pl.BlockSpec((tq, 1024), lambda b, i: (q_row(b, i), 0))
    in_specs = ([pl.BlockSpec(memory_space=pltpu.SMEM)] + specs
                + [q_rows_spec, q_rows_spec, pl.BlockSpec((1, 4, 8, GROUP * tq), lambda b, i: (b * nq + i, 0, 0, 0))])
    args = [sink] + [qkvp] * len(specs) + [do, o, lse]
    dq_shape = jax.ShapeDtypeStruct((rt.rows, 1024), F32)
    dkv_shape = jax.ShapeDtypeStruct((rt.rows, 512), F32)
    dsink_spec, dsink_shape = pl.BlockSpec((8, 128), lambda b, i: (0, 0)), jax.ShapeDtypeStruct((8, 128), F32)
    dq_spec = pl.BlockSpec((tq, 1024), lambda b, i: (q_row(b, i), 0))
    if latent:
        out_specs = [dq_spec, pl.BlockSpec((seq, 512), lambda b, i: (b, 0)), pl.BlockSpec((ctx, 512), lambda b, i: (b, 0)), dsink_spec]
        out_shape = [dq_shape, dkv_shape, jax.ShapeDtypeStruct((rt.n_ctx, 512), F32), dsink_shape]
        aliases = {}
    else:
        dq_prev, dkv_prev, c1 = prev
        in_specs += [pl.BlockSpec((ctx, 512), lambda b, i: (b, 0)), pl.BlockSpec(memory_space=pl.ANY), pl.BlockSpec(memory_space=pl.ANY)]
        args += [c1, dq_prev, dkv_prev]
        out_specs = [dq_spec, pl.BlockSpec((ctx, 512), lambda b, i: (ctx_blk0 + b, 0)), dsink_spec]
        out_shape = [dq_shape, dkv_shape, dsink_shape]
        aliases = {len(args) - 2: 0, len(args) - 1: 1}
    return _comm_call(body, comm, name=name, grid=(nb, nq), in_specs=in_specs, out_specs=out_specs, out_shape=out_shape,
                      args=args, aliases=aliases, semantics=("arbitrary", "arbitrary"))


def _silu(x):
    return x / (1.0 + jnp.exp(-x))


def _whole(shape):
    return pl.BlockSpec(shape, lambda i, s: (0,) * len(shape))


def _ada_half_spec(cols):
    return pl.BlockSpec((DEPTH, D_MODEL, cols), lambda i, s: (0, 0, s[0]))


def _ada_fwd(cond, w_ada, b_half, c_idx, name):
    rows = cond.shape[0]
    cols = w_ada.shape[2] // 2

    def body(s_ref, c_ref, w_ref, b_ref, x_ref, o_ref):
        xs = _silu(c_ref[...]).astype(BF16)
        x_ref[...] = xs
        for l in range(DEPTH):
            o_ref[l] = jnp.dot(xs, w_ref[l].astype(BF16), preferred_element_type=F32) + b_ref[l]

    grid_spec = pltpu.PrefetchScalarGridSpec(
        num_scalar_prefetch=1, grid=(1,),
        in_specs=[_whole(cond.shape), _ada_half_spec(cols), _whole(b_half.shape)],
        out_specs=[_whole((rows, D_MODEL)), _whole((DEPTH, rows, cols))])
    return pl.pallas_call(
        body, name=name, grid_spec=grid_spec,
        out_shape=[jax.ShapeDtypeStruct((rows, D_MODEL), BF16), jax.ShapeDtypeStruct((DEPTH, rows, cols), F32)],
        compiler_params=_params(("arbitrary",)),
    )(c_idx, cond, w_ada, b_half)


def _ada_cond_bwd(dcx, w_ada, c_idx, name):
    _, rows, cols = dcx.shape

    def body(s_ref, d_ref, w_ref, o_ref):
        acc = jnp.zeros((rows, D_MODEL), F32)
        for l in range(DEPTH):
            acc = acc + lax.dot_general(d_ref[l], w_ref[l].astype(BF16), NT, preferred_element_type=F32)
        o_ref[...] = acc

    grid_spec = pltpu.PrefetchScalarGridSpec(
        num_scalar_prefetch=1, grid=(1,),
        in_specs=[_whole(dcx.shape), _ada_half_spec(cols)], out_specs=_whole((rows, D_MODEL)))
    return pl.pallas_call(body, name=name, grid_spec=grid_spec, out_shape=jax.ShapeDtypeStruct((rows, D_MODEL), F32),
                          compiler_params=_params(("arbitrary",)))(c_idx, dcx, w_ada)


def _dev_sum(x, name):
    _, r, c = x.shape

    def body(x_ref, o_ref):
        v = x_ref[0]
        for d in range(1, N_DEV):
            v = v + x_ref[d]
        o_ref[...] = v

    return pl.pallas_call(body, name=name, out_shape=jax.ShapeDtypeStruct((r, c), F32))(x)


def _adam_val(w, g, m, v):
    c1 = 1.0 / (1.0 - ADAM_B1 ** ADAM_STEP)
    c2 = 1.0 / (1.0 - ADAM_B2 ** ADAM_STEP)
    nm = ADAM_B1 * m + (1.0 - ADAM_B1) * g
    nv = ADAM_B2 * v + (1.0 - ADAM_B2) * (g * g)
    return -ADAM_LR * ((nm * c1) / (jnp.sqrt(nv * c2) + ADAM_EPS) + ADAM_WD * w), nm, nv


def _small_update(tot, dcc_parts, params, n_groups, name):
    n_p = len(params)
    mod_rows = n_groups * N_MOD
    head_row = DEPTH * mod_rows + 4 * DEPTH

    def body(tot_ref, dcc_ref, *refs):
        ins, outs = refs[:3 * n_p], refs[3 * n_p:]

        def update(p, rows, cols, g):
            w_ref, m_ref, v_ref = ins[3 * p:3 * p + 3]
            g_ref, d_ref, nm_ref, nv_ref = outs[4 * p:4 * p + 4]
            d, nm, nv = _adam_val(w_ref[rows, cols], g, m_ref[rows, cols], v_ref[rows, cols])
            g_ref[rows, cols] = g
            d_ref[rows, cols] = d
            nm_ref[rows, cols] = nm
            nv_ref[rows, cols] = nv

        acc = dcc_ref[0, 0:1, :]
        for d in range(1, N_DEV):
            acc = acc + dcc_ref[d, 0:1, :]
        c = ins[0][...]
        sg = 1.0 / (1.0 + jnp.exp(-c))
        update(0, slice(0, 1), slice(None), acc * (sg * (1.0 + c * (1.0 - sg))))
        for l in range(DEPTH):
            for i in range(N_MOD):
                g = tot_ref[l * mod_rows + i:l * mod_rows + i + 1, :]
                for grp in range(1, n_groups):
                    g = g + tot_ref[l * mod_rows + grp * N_MOD + i:l * mod_rows + grp * N_MOD + i + 1, :]
                update(1, slice(l, l + 1), slice(i * D_MODEL, (i + 1) * D_MODEL), g)
            for j in range(4):
                row = DEPTH * mod_rows + 4 * l + j
                update(2 + j, slice(l, l + 1), slice(None), tot_ref[row:row + 1, :])
            head = tot_ref[head_row + l:head_row + l + 1, :]
            update(6, slice(l, l + 1), slice(None), head[:, 0:HEAD_DIM] + head[:, HEAD_DIM:2 * HEAD_DIM])
            update(7, slice(l, l + 1), slice(None), head[:, 2 * HEAD_DIM:3 * HEAD_DIM] + head[:, 3 * HEAD_DIM:4 * HEAD_DIM])
            update(8, slice(l, l + 1), slice(None), head[:, 4 * HEAD_DIM:4 * HEAD_DIM + ins[3 * 8].shape[1]])

    shapes = [jax.ShapeDtypeStruct(w.shape, F32) for w, _, _ in params for _ in range(4)]
    outs = pl.pallas_call(body, name=name, out_shape=shapes)(tot, dcc_parts, *[a for p in params for a in p])
    return [tuple(outs[4 * p:4 * p + 4]) for p in range(n_p)]


def _adamw(w, g, m, v, name):
    r, c = w.shape
    tr = _pick(r, (256, 128, 64, 32, 24, 16, 8))

    def body(w_ref, g_ref, m_ref, v_ref, d_ref, nm_ref, nv_ref):
        d_ref[...], nm_ref[...], nv_ref[...] = _adam_val(w_ref[...], g_ref[...], m_ref[...], v_ref[...])

    spec = pl.BlockSpec((tr, c), lambda i: (i, 0))
    return pl.pallas_call(body, name=name, grid=(r // tr,), in_specs=[spec] * 4, out_specs=[spec] * 3,
                          out_shape=[jax.ShapeDtypeStruct((r, c), F32)] * 3, compiler_params=_params(("parallel",)))(w, g, m, v)


def _adamw_shard(kind, l, w, m, v, halves, off, prev, name):
    h = PACK_HEIGHT[kind]
    assert off % h == 0, (kind, off)
    _, r, c = w.shape
    rows = r // 2

    def body(w_ref, m_ref, v_ref, p_ref, *rest):
        g_ref, d_ref, nm_ref, nv_ref = rest[-4:]
        if kind == "in":
            for t in range(2):
                g = p_ref[:, t * IN_PIECE_COLS:(t + 1) * IN_PIECE_COLS]
                rs = slice(t * h, (t + 1) * h)
                g_ref[rs, :] = g
                d_ref[rs, :], nm_ref[rs, :], nv_ref[rs, :] = _adam_val(w_ref[rs, :], g, m_ref[rs, :], v_ref[rs, :])
        else:
            g = p_ref[...]
            g_ref[...] = g
            d_ref[...], nm_ref[...], nv_ref[...] = _adam_val(w_ref[...], g, m_ref[...], v_ref[...])

    blk = pl.BlockSpec((None, rows, c), lambda half: (l, half, 0))
    in_specs = [blk, blk, blk, pl.BlockSpec((None, h, 1024), lambda half: (half, off // h, 0))]
    args = [w, m, v, halves]
    aliases = {}
    if prev is not None:
        in_specs += [pl.BlockSpec(memory_space=pl.ANY)] * 4
        args += list(prev)
        aliases = {4 + j: j for j in range(4)}
    return pl.pallas_call(
        body, name=name, grid=(2,), in_specs=in_specs, out_specs=[blk] * 4,
        out_shape=[jax.ShapeDtypeStruct(w.shape, F32)] * 4, input_output_aliases=aliases,
        compiler_params=_params(("parallel",)))(*args)


SMALL_ROWS = 48


def _small_rows(small, sq):
    def lane_pad(v):
        return jnp.pad(v, (0, D_MODEL - v.shape[0]))[None]

    head_rows = [lane_pad(jnp.concatenate([s["q_norm"][0], s["k_norm"][0], s["sink"]])) for s in small]
    loss_row = lane_pad((0.5 / D_MODEL) * jnp.sum(sq, keepdims=True)[0])
    rows = jnp.concatenate([s["mod"].reshape(-1, D_MODEL) for s in small] + [s["gammas"] for s in small] + head_rows + [loss_row], axis=0)
    return jnp.pad(rows, ((0, SMALL_ROWS - rows.shape[0]), (0, 0)))


def _local_step(x, ctx, target, mods, gam, qn, kn, sink, w_first, w_layers, packed, kc_idx):
    nb, seq, _ = x.shape
    rt = _Rows(nb, seq, ctx.shape[1])
    rt_lat = rt.latent_only()
    tables = _rope_tables(rt)
    fuse = packed is not None
    h = (x.reshape(rt.n_lat, D_MODEL), ctx.reshape(rt.n_ctx, D_MODEL))
    wg = [{}, {}] if fuse else [dict(w) for w in w_layers]
    wg[0]["in"] = (w_first, 0)
    if fuse:
        wg[0]["in_own"] = (packed, W_FIRST[0])
    saved = []
    for l in range(DEPTH):
        g_pre_mix, g_post_mix, g_pre_mlp, g_post_mlp = gam[l]
        if l == 0:
            u, qkv, qkvp, h = _in_fwd(rt, h, g_pre_mix, mods[l], wg[l], tables, qn[l], kn[l], f"in_fwd{l}")
        else:
            u, qkv, qkvp = _in_fwd(rt, h, g_pre_mix, mods[l], wg[l], tables, qn[l], kn[l], f"in_fwd{l}")
        if fuse and l == 0:
            o, lse_lat, w_mlp0, w_out0, w_mix1 = _attn_fwd(rt, qkvp, sink[l], None, f"attn_lat_fwd{l}",
                                                          comm=_gather_comm(packed, [W_MLP0, W_OUT0, W_MIX1], lead=2))
            wg[0].update({kind: (w_mlp0, PACK_OFF[(kind, 0)] - W_MLP0[0]) for kind in ("up", "down")})
            wg[0]["out"] = (w_out0, 0)
            wg[1] = {kind: (w_mix1, PACK_OFF[(kind, 1)] - W_MIX1[0]) for kind in ("out", "in")}
        elif fuse:
            o, lse_lat, w_mlp1 = _attn_fwd(rt, qkvp, sink[l], None, f"attn_lat_fwd{l}", comm=_gather_comm(packed, [W_MLP1], lead=2))
            wg[1].update({kind: (w_mlp1, PACK_OFF[(kind, 1)] - W_MLP1[0]) for kind in ("up", "down")})
        else:
            o, lse_lat = _attn_fwd(rt, qkvp, sink[l], None, f"attn_lat_fwd{l}")
        if l < DEPTH - 1:
            o, lse_ctx = _attn_fwd(rt, qkvp, sink[l], o, f"attn_ctx_fwd{l}")
            mix, h1, u2 = _out_fwd(rt, o, wg[l], h, mods[l], g_post_mix, g_pre_mlp, f"out_fwd{l}")
            r, y, h2 = _mlp_fwd(rt, u2, h1, wg[l], mods[l], g_post_mlp, f"mlp_fwd{l}")
        else:
            lse_ctx = None
            mix, h1, u2 = _out_fwd(rt_lat, o, wg[l], h, mods[l], g_post_mix, g_pre_mlp, f"out_fwd{l}")
            r, y, dh, sq = _mlp_fwd(rt_lat, u2, h1, wg[l], mods[l], g_post_mlp, f"mlp_fwd{l}", target=target.reshape(rt.n_lat, D_MODEL))
        saved.append((h, u, qkv, qkvp, o, lse_lat, lse_ctx, mix, h1, u2, r, y))
        h = h2

    small = [None] * DEPTH
    groups = {}
    for l in reversed(range(DEPTH)):
        g_pre_mix, g_post_mix, g_pre_mlp, g_post_mlp = gam[l]
        h0, u, qkv, qkvp, o, lse_lat, lse_ctx, mix, h1, u2, r, y = saved[l]
        mlp_group, mix_group = (G_LAYER1, G_LAYER1) if l == 1 else (G_MLP0, G_MIX0)
        hide = fuse and l == 0

        dead_ctx = l == DEPTH - 1
        rt_b = rt_lat if dead_ctx else rt
        outs = _mlp_down_bwd(rt_b, dh, y, r, wg[l], mods[l], g_post_mlp, f"mlp_down_bwd{l}",
                             comm=_pair_comm(groups[G_LAYER1]) if hide else None)
        dy, da, d_gate_m, d_g_post_mlp = outs[:4]
        if hide:
            sum1 = _pair_sum(groups[G_LAYER1], outs[4], kc_idx, "grad_pair_sum_layer1")
        p_mlp = _wgrad_packed(rt_b, r, dy, "down", PACK_OFF[("down", l)] - mlp_group[0], mlp_group[1], None, f"mlp_down_wgrad{l}")
        dh1, d_sh_m, d_sc_m, d_g_pre_mlp = _mlp_up_bwd(rt_b, da, wg[l], h1, dh, mods[l], g_pre_mlp, f"mlp_up_bwd{l}")
        p_mlp = _wgrad_packed(rt_b, u2, da, "up", PACK_OFF[("up", l)] - mlp_group[0], mlp_group[1], p_mlp, f"mlp_up_wgrad{l}")
        outs = _out_bwd(rt_b, dh1, mix, wg[l], mods[l], g_post_mix, f"out_bwd{l}", comm=_pair_comm(p_mlp) if hide else None)
        dmix, do, d_gate_a, d_g_post_mix = outs[:4]
        if hide:
            sum0 = _pair_sum(p_mlp, outs[4], kc_idx, "grad_pair_sum_mlp0")
        p_mix = _wgrad_packed(rt_b, o, dmix, "out", PACK_OFF[("out", l)] - mix_group[0], mix_group[1],
                              p_mlp if l == 1 else None, f"out_wgrad{l}")
        outs = _attn_bwd(rt, qkvp, o, lse_lat, do, sink[l], None, f"attn_lat_bwd{l}",
                         comm=_chip_comm([sum1[1], sum0[1]]) if hide else None)
        dq, dkv, dkv_c, dsink1 = outs[:4]
        if hide:
            groups[G_LAYER1] = _owner_sum(sum1[0], outs[4], kc_idx, "grad_owner_sum_layer1")
            groups[G_MLP0] = _owner_sum(sum0[0], outs[5], kc_idx, "grad_owner_sum_mlp0")
        if dead_ctx:
            dsink2 = jnp.zeros_like(dsink1)
            d_gate_m, d_sh_m, d_sc_m, d_gate_a = [a.at[nb].set(0.0) for a in (d_gate_m, d_sh_m, d_sc_m, d_gate_a)]
        else:
            dq, dkv, dsink2 = _attn_bwd(rt, qkvp, o, lse_ctx, do, sink[l], (dq, dkv, dkv_c), f"attn_ctx_bwd{l}")
        dqkv, dh, dqn, dkn, d_sh_a, d_sc_a, d_g_pre_mix = _in_bwd(rt, dq, dkv, qkv, tables, qn[l], kn[l], wg[l], h0, dh1, mods[l],
                                                                  g_pre_mix, l == 0, f"in_bwd{l}",
                                                                  dead_ctx_dkv=dkv_c if dead_ctx else None)
        dmod = jnp.concatenate([d_sh_a, d_sc_a, d_gate_a, d_sh_m, d_sc_m, d_gate_m], axis=1)
        small[l] = dict(mod=dmod, gammas=jnp.concatenate([d_g_pre_mix, d_g_post_mix, d_g_pre_mlp, d_g_post_mlp], axis=0),
                        q_norm=dqn, k_norm=dkn, sink=(dsink1 + dsink2)[:, 0])
        tail = _merge([_gather_comm(_small_rows(small, sq), [(0, SMALL_ROWS)]),
                       _halves_comm([groups[G_LAYER1], groups[G_MLP0]])]) if hide else None
        outs = _wgrad_packed(rt, u, dqkv, "in", PACK_OFF[("in", l)] - mix_group[0], mix_group[1], p_mix, f"in_wgrad{l}", comm=tail)
        if hide:
            groups[mix_group], small_g, groups[G_LAYER1], groups[G_MLP0] = outs
        else:
            groups[mix_group], small_g = outs, None
            if l == 0:
                groups[G_MLP0] = p_mlp
    return sq, dh.reshape(nb, seq, D_MODEL), [groups[G_LAYER1], groups[G_MLP0], groups[G_MIX0]], small, small_g


def kernel(x, c, ctx, c_ctx, w_ada, b_ada, g_pre_mix, g_post_mix, g_pre_mlp, g_post_mlp, w_in, q_norm, k_norm, sink, w_out, w_up, w_down, loss_target, m_c_ctx, m_w_ada, m_b_ada, m_g_pre_mix, m_g_post_mix, m_g_pre_mlp, m_g_post_mlp, m_w_in, m_q_norm, m_k_norm, m_sink, m_w_out, m_w_up, m_w_down, v_c_ctx, v_w_ada, v_b_ada, v_g_pre_mix, v_g_post_mix, v_g_pre_mlp, v_g_post_mlp, v_w_in, v_q_norm, v_k_norm, v_sink, v_w_out, v_w_up, v_w_down):
    nb = x.shape[0]
    ix, iy, ic = lax.axis_index("x"), lax.axis_index("y"), lax.axis_index("c")
    chip = 2 * ix + iy
    dev = 2 * chip + ic
    ada_cols = w_ada.shape[2] // 2

    c_rows = c.reshape(8, (nb * D_MODEL) // 8)
    packed, c_all = _pack_local_half(w_in, w_out, w_up, w_down, _gather_comm(c_rows, [(0, c_rows.shape[0])]), "pack_gather_c")
    c_all = c_all.reshape(N_DEV * nb, D_MODEL)
    n_cond = N_DEV * nb + 1
    cond_rows = 16 * ((n_cond + 15) // 16)
    cond = jnp.concatenate([c_all, c_ctx[None, :], jnp.zeros((cond_rows - n_cond, D_MODEL), F32)], axis=0)
    c_idx = ic.reshape(1).astype(jnp.int32)
    kc_idx = jnp.stack([chip, ic]).astype(jnp.int32)
    b_ada_half = lax.dynamic_slice_in_dim(b_ada, dev * ada_cols, ada_cols, 1)[:, None, :]
    x_ada, mod_part = _ada_fwd(cond, w_ada, b_ada_half, c_idx, "ada_fwd")
    mod_rows2d = mod_part.reshape(DEPTH * cond_rows, ada_cols)
    mod_g, w_first = _comm_alone(_merge([_gather_comm(mod_rows2d, [(0, mod_rows2d.shape[0])]),
                                         _gather_comm(packed, [W_FIRST], copy_own=False)]), "gather_mod_w_first")
    mod_all = mod_g.reshape(N_DEV, DEPTH, cond_rows, ada_cols).transpose(1, 2, 0, 3).reshape(DEPTH, cond_rows, N_MOD * D_MODEL)
    mods = []
    for l in range(DEPTH):
        mine = lax.dynamic_slice_in_dim(mod_all[l], dev * nb, nb, 0)
        mods.append(jnp.concatenate([mine, mod_all[l, n_cond - 1:n_cond]], axis=0).reshape(nb + 1, N_MOD, D_MODEL))

    gam = [(g_pre_mix[l][None], g_post_mix[l][None], g_pre_mlp[l][None], g_post_mlp[l][None]) for l in range(DEPTH)]
    qn = [jnp.tile(q_norm[l], 2)[None] for l in range(DEPTH)]
    kn = [jnp.tile(k_norm[l], 2)[None] for l in range(DEPTH)]
    _, grad_x, (h_layer1, h_mlp0, p_mix0), _, small_g = _local_step(x, ctx, loss_target, mods, gam, qn, kn, [sink[l] for l in range(DEPTH)],
                                                                 w_first, None, packed, kc_idx)

    def step(w, g, m, v, name):
        shape = w.shape
        cols = shape[-1]
        outs = _adamw(w.reshape(-1, cols), g.reshape(-1, cols), m.reshape(-1, cols), v.reshape(-1, cols), name)
        return tuple(a.reshape(shape) for a in outs)

    def shard_update(kind, w, m, v, layer0, layer1):
        outs = None
        for l, (halves, group) in enumerate((layer0, layer1)):
            outs = _adamw_shard(kind, l, w, m, v, halves, PACK_OFF[(kind, l)] - group[0], outs, f"adamw_w_{kind}{l}")
        return tuple(outs)

    tot = _dev_sum(small_g, "small_sum")
    mod_rows = (nb + 1) * N_MOD
    loss = tot[DEPTH * mod_rows + 4 * DEPTH + DEPTH, 0]

    ex = small_g[:, :DEPTH * mod_rows].reshape(N_DEV, DEPTH, nb + 1, N_MOD * D_MODEL)[:, :, :nb]
    ex = ex.transpose(1, 0, 2, 3).reshape(DEPTH, N_DEV * nb, N_MOD * D_MODEL)
    cx = tot[:DEPTH * mod_rows].reshape(DEPTH, nb + 1, N_MOD * D_MODEL)[:, nb:]
    dm = jnp.concatenate([ex, cx, jnp.zeros((DEPTH, cond_rows - n_cond, N_MOD * D_MODEL), F32)], axis=1)
    shard_cols = w_ada.shape[2]
    grad_w_ada = _ada_wgrad(x_ada, lax.dynamic_slice_in_dim(dm, chip * shard_cols, shard_cols, 2).astype(BF16), "ada_wgrad")
    dcx = jnp.pad(lax.dynamic_slice_in_dim(cx, dev * ada_cols, ada_cols, 2), ((0, 0), (0, 15), (0, 0))).astype(BF16)
    dcc = _ada_cond_bwd(dcx, w_ada, c_idx, "ada_cond_bwd")[0:8]

    r1, = _comm_alone(_pair_comm(p_mix0), "grad_pair_exchange_mix0")
    a32, a16 = _pair_sum(p_mix0, r1, kc_idx, "grad_pair_sum_mix0")
    r2, dcc_g = _comm_alone(_merge([_chip_comm([a16]), _gather_comm(dcc, [(0, dcc.shape[0])])]), "grad_chip_exchange_mix0")
    h_mix0 = _owner_sum(a32, r2, kc_idx, "grad_owner_sum_mix0")
    h_mix0, = _comm_alone(_halves_comm([h_mix0]), "grad_halves_exchange_mix0")

    small_names = ["c_ctx", "b_ada", "g_pre_mix", "g_post_mix", "g_pre_mlp", "g_post_mlp", "q_norm", "k_norm", "sink"]
    assert q_norm.shape[1] == HEAD_DIM and k_norm.shape[1] == HEAD_DIM
    small_res = _small_update(tot, dcc_g, [(c_ctx[None], m_c_ctx[None], v_c_ctx[None]), (b_ada, m_b_ada, v_b_ada),
                                           (g_pre_mix, m_g_pre_mix, v_g_pre_mix), (g_post_mix, m_g_post_mix, v_g_post_mix),
                                           (g_pre_mlp, m_g_pre_mlp, v_g_pre_mlp), (g_post_mlp, m_g_post_mlp, v_g_post_mlp),
                                           (q_norm, m_q_norm, v_q_norm), (k_norm, m_k_norm, v_k_norm), (sink, m_sink, v_sink)],
                              nb + 1, "small_update")
    res = {n: r for n, r in zip(small_names, small_res)}
    res["c_ctx"] = tuple(a[0] for a in res["c_ctx"])
    res["w_ada"] = (grad_w_ada, *step(w_ada, grad_w_ada, m_w_ada, v_w_ada, "adamw_w_ada"))
    res["w_up"] = shard_update("up", w_up, m_w_up, v_w_up, (h_mlp0, G_MLP0), (h_layer1, G_LAYER1))
    res["w_down"] = shard_update("down", w_down, m_w_down, v_w_down, (h_mlp0, G_MLP0), (h_layer1, G_LAYER1))
    res["w_in"] = shard_update("in", w_in, m_w_in, v_w_in, (h_mix0, G_MIX0), (h_layer1, G_LAYER1))
    res["w_out"] = shard_update("out", w_out, m_w_out, v_w_out, (h_mix0, G_MIX0), (h_layer1, G_LAYER1))

    order = ["c_ctx", "w_ada", "b_ada", "g_pre_mix", "g_post_mix", "g_pre_mlp", "g_post_mlp", "w_in", "q_norm", "k_norm", "sink", "w_out", "w_up", "w_down"]
    return (loss, grad_x, *[res[n][0] for n in order], *[res[n][1] for n in order],
            *[res[n][2] for n in order], *[res[n][3] for n in order])
```

```python
import functools

import jax
import jax.numpy as jnp
import numpy as np
from jax import lax
from jax.experimental import pallas as pl
from jax.experimental.pallas import tpu as pltpu

F32 = jnp.float32
BF16 = jnp.bfloat16

D_MODEL = 1024
HEAD_DIM = 64
GROUP = 4
WINDOW = 128
N_MOD = 6
D_FF = 4 * D_MODEL
IN_COLS = 1536
GRID_W = 64
ROPE_THETA = 10000.0
EPS = 1e-6
NEG_BIG = -1e30
Q_SCALE = HEAD_DIM ** -0.5
DEPTH = 2
N_DEV = 8

ADAM_LR = 0.001
ADAM_B1 = 0.9
ADAM_B2 = 0.999
ADAM_EPS = 1e-08
ADAM_WD = 0.01
ADAM_STEP = 10

V7X_VMEM_BYTES = 64 * 1024 * 1024
VMEM_LIMIT = V7X_VMEM_BYTES - 8 * 1024 * 1024

MESH = pl.DeviceIdType.MESH
NT = (((1,), (1,)), ((), ()))
TN = (((0,), (0,)), ((), ()))

COL_KA, COL_VA, COL_KB, COL_VB = 4, 5, 10, 11
NORMED_COLS = 640

PACK_HEIGHT = {"up": 512, "down": 512, "in": 256, "out": 128}
IN_PIECE_COLS = 384
PACK_OFF = {("up", 0): 0, ("down", 0): 512, ("in", 0): 1024, ("out", 0): 1280,
            ("up", 1): 1408, ("down", 1): 1920, ("in", 1): 2432, ("out", 1): 2688}
PACK_ROWS = 2816
W_FIRST, W_MLP0, W_OUT0, W_IN1, W_MLP1, W_OUT1 = (1024, 256), (0, 1024), (1280, 128), (2432, 256), (1408, 1024), (2688, 128)
G_LAYER1, G_MLP0, G_MIX0 = (1408, 1408), (0, 1024), (1024, 384)


def _pick(n, cands):
    for t in cands:
        if n % t == 0:
            return t
    raise ValueError(f"no tile for {n}")


def _params(sem):
    return pltpu.CompilerParams(dimension_semantics=sem, vmem_limit_bytes=VMEM_LIMIT)


class _Comm:
    def __init__(self, inputs, out_shapes, aliases, n_send, n_recv, start, finish, relay=None, lead=0):
        self.inputs, self.out_shapes, self.aliases = list(inputs), list(out_shapes), dict(aliases)
        self.n_send, self.n_recv, self.start, self.finish, self.relay, self.lead = n_send, n_recv, start, finish, relay, lead


def _comm_call(compute, comm, *, name, grid, in_specs, out_specs, out_shape, args, aliases, semantics, scratch=()):
    in_specs, out_specs, out_shape, args, aliases = list(in_specs), list(out_specs), list(out_shape), list(args), dict(aliases)
    scratch = list(scratch)
    if comm is None:
        return pl.pallas_call(compute, name=name, grid=grid, in_specs=in_specs, out_specs=out_specs, out_shape=out_shape,
                              input_output_aliases=aliases, scratch_shapes=scratch, compiler_params=_params(semantics))(*args)
    n_in, n_out, n_ci, n_co = len(args), len(out_shape), len(comm.inputs), len(comm.out_shapes)
    hbm = pl.BlockSpec(memory_space=pl.ANY)
    aliases.update({n_in + i: n_out + o for i, o in comm.aliases.items()})

    def body(*refs):
        ins, c_ins = refs[:n_in], refs[n_in:n_in + n_ci]
        outs, c_outs = refs[n_in + n_ci:n_in + n_ci + n_out], refs[n_in + n_ci + n_out:n_in + n_ci + n_out + n_co]
        scr = refs[n_in + n_ci + n_out + n_co:-2]
        send_sems, recv_sems = refs[-2:]
        ids = [pl.program_id(a) for a in range(len(grid))]
        first = functools.reduce(jnp.logical_and, [i == 0 for i in ids])
        last = functools.reduce(jnp.logical_and, [i == g - 1 for i, g in zip(ids, grid)])

        @pl.when(first)
        def _():
            comm.start(c_ins, c_outs, send_sems, recv_sems)

        compute(*ins, *outs, *scr)

        if comm.relay is not None:
            step = functools.reduce(lambda acc, ig: acc * ig[1] + ig[0], zip(ids, grid), 0)

            @pl.when(step == int(np.prod(grid)) - 1 - comm.lead)
            def _():
                comm.relay(c_ins, c_outs, send_sems, recv_sems)

        @pl.when(last)
        def _():
            comm.finish(c_ins, c_outs, send_sems, recv_sems)

    return pl.pallas_call(
        body, name=name, grid=grid,
        in_specs=in_specs + [hbm] * n_ci, out_specs=out_specs + [hbm] * n_co, out_shape=out_shape + comm.out_shapes,
        input_output_aliases=aliases,
        scratch_shapes=scratch + [pltpu.SemaphoreType.DMA((comm.n_send,)), pltpu.SemaphoreType.DMA((comm.n_recv,))],
        compiler_params=_params(("arbitrary",) * len(grid)),
    )(*args, *comm.inputs)


def _place():
    x_, y_, c_ = lax.axis_index("x"), lax.axis_index("y"), lax.axis_index("c")
    return x_, y_, c_, [(1 - x_, y_), (x_, 1 - y_), (1 - x_, 1 - y_)]


GATHER_SENDS, GATHER_RECVS = 8, 7


def _gather_copies(packed_ref, wg_ref, send_sems, recv_sems, rows, nth=0):
    r0, n = rows
    x_, y_, c_, chips = _place()
    me, sibling = (x_, y_, c_), (x_, y_, 1 - c_)
    src = packed_ref.at[pl.ds(r0, n), :]

    def slot(px, py, pc):
        return wg_ref.at[4 * px + 2 * py + pc]

    def copy(k, block, to, from_packed=False):
        return pltpu.make_async_remote_copy(src_ref=src if from_packed else slot(*block), dst_ref=slot(*block),
                                            send_sem=send_sems.at[GATHER_SENDS * nth + k], recv_sem=recv_sems.at[GATHER_RECVS * nth + k],
                                            device_id=to, device_id_type=MESH)

    own = [copy(0, me, sibling, True)] + [copy(1 + j, me, (*chip, c_), True) for j, chip in enumerate(chips)]
    passed = [copy(4 + j, (*chip, c_), sibling) for j, chip in enumerate(chips)]
    over_ici = [copy(1 + j, (*chip, c_), me) for j, chip in enumerate(chips)]
    from_sibling = [copy(0, sibling, me)] + [copy(4 + j, (*chip, 1 - c_), me) for j, chip in enumerate(chips)]
    mine = pltpu.make_async_copy(src, slot(*me), send_sems.at[GATHER_SENDS * nth + 7])
    return mine, own, passed, over_ici, from_sibling


def _gather_start(packed_ref, wg_ref, send_sems, recv_sems, rows, nth=0, copy_own=True):
    mine, own, _, _, _ = _gather_copies(packed_ref, wg_ref, send_sems, recv_sems, rows, nth)
    if copy_own:
        mine.start()
    for cp in own:
        cp.start()


def _gather_relay(packed_ref, wg_ref, send_sems, recv_sems, rows, nth=0):
    _, _, passed, over_ici, _ = _gather_copies(packed_ref, wg_ref, send_sems, recv_sems, rows, nth)
    for arrived, onward in zip(over_ici, passed):
        arrived.wait_recv()
        onward.start()


def _gather_finish(packed_ref, wg_ref, send_sems, recv_sems, rows, nth=0, copy_own=True):
    mine, own, passed, _, from_sibling = _gather_copies(packed_ref, wg_ref, send_sems, recv_sems, rows, nth)
    for arrived in from_sibling:
        arrived.wait_recv()
    for cp in own + passed:
        cp.wait_send()
    if copy_own:
        mine.wait()


def _gather_comm(packed, ranges, copy_own=True, lead=0):
    shapes = [jax.ShapeDtypeStruct((N_DEV, n, packed.shape[1]), packed.dtype) for _, n in ranges]

    def start(ins, outs, ss, rs):
        for nth, rows in enumerate(ranges):
            _gather_start(ins[0], outs[nth], ss, rs, rows, nth, copy_own)

    def relay(ins, outs, ss, rs):
        for nth, rows in enumerate(ranges):
            _gather_relay(ins[0], outs[nth], ss, rs, rows, nth)

    def finish(ins, outs, ss, rs):
        for nth, rows in enumerate(ranges):
            _gather_finish(ins[0], outs[nth], ss, rs, rows, nth, copy_own)

    return _Comm([packed], shapes, {}, GATHER_SENDS * len(ranges), GATHER_RECVS * len(ranges), start, finish, relay, lead)


def _pair_copy(p_ref, out_ref, send_sems, recv_sems):
    x_, y_, c_, _ = _place()
    return pltpu.make_async_remote_copy(src_ref=p_ref.at[1 - c_], dst_ref=out_ref,
                                        send_sem=send_sems.at[0], recv_sem=recv_sems.at[0],
                                        device_id=(x_, y_, 1 - c_), device_id_type=MESH)


def _pair_comm(p):
    return _Comm([p], [jax.ShapeDtypeStruct(p.shape[1:], p.dtype)], {}, 1, 1,
                 lambda ins, outs, ss, rs: _pair_copy(ins[0], outs[0], ss, rs).start(),
                 lambda ins, outs, ss, rs: _pair_copy(ins[0], outs[0], ss, rs).wait())


def _chip_copies(a_refs, out_refs, send_sems, recv_sems):
    _, _, c_, chips = _place()
    return [pltpu.make_async_remote_copy(src_ref=a_ref.at[2 * tx + ty], dst_ref=o_ref.at[j],
                                         send_sem=send_sems.at[3 * g + j], recv_sem=recv_sems.at[3 * g + j],
                                         device_id=(tx, ty, c_), device_id_type=MESH)
            for g, (a_ref, o_ref) in enumerate(zip(a_refs, out_refs)) for j, (tx, ty) in enumerate(chips)]


def _chip_start(a_refs, out_refs, send_sems, recv_sems):
    for cp in _chip_copies(a_refs, out_refs, send_sems, recv_sems):
        cp.start()


def _chip_finish(a_refs, out_refs, send_sems, recv_sems):
    for cp in _chip_copies(a_refs, out_refs, send_sems, recv_sems):
        cp.wait()


def _chip_comm(arrays):
    shapes = [jax.ShapeDtypeStruct((3,) + a.shape[1:], a.dtype) for a in arrays]
    return _Comm(arrays, shapes, {}, 3 * len(arrays), 3 * len(arrays), _chip_start, _chip_finish)


def _halves_copies(in_refs, out_refs, send_sems, recv_sems):
    x_, y_, c_, _ = _place()
    return [pltpu.make_async_remote_copy(src_ref=o_ref.at[c_], dst_ref=o_ref.at[c_], send_sem=send_sems.at[i], recv_sem=recv_sems.at[i],
                                         device_id=(x_, y_, 1 - c_), device_id_type=MESH)
            for i, o_ref in enumerate(out_refs)]


def _halves_start(in_refs, out_refs, send_sems, recv_sems):
    for cp in _halves_copies(in_refs, out_refs, send_sems, recv_sems):
        cp.start()


def _halves_finish(in_refs, out_refs, send_sems, recv_sems):
    for cp in _halves_copies(in_refs, out_refs, send_sems, recv_sems):
        cp.wait()


def _halves_comm(arrays):
    shapes = [jax.ShapeDtypeStruct(a.shape, a.dtype) for a in arrays]
    return _Comm(arrays, shapes, {i: i for i in range(len(arrays))}, len(arrays), len(arrays), _halves_start, _halves_finish)


class _SemSlice:
    class _At:
        def __init__(self, sems, first):
            self.sems, self.first = sems, first

        def __getitem__(self, k):
            return self.sems.at[self.first + k]

    def __init__(self, sems, first):
        self.at = _SemSlice._At(sems, first)


def _merge(comms):
    inputs = [a for c in comms for a in c.inputs]
    shapes = [s for c in comms for s in c.out_shapes]
    aliases, spans = {}, []
    i0 = o0 = s0 = r0 = 0
    for c in comms:
        aliases.update({i0 + i: o0 + o for i, o in c.aliases.items()})
        spans.append((slice(i0, i0 + len(c.inputs)), slice(o0, o0 + len(c.out_shapes)), s0, r0))
        i0, o0, s0, r0 = i0 + len(c.inputs), o0 + len(c.out_shapes), s0 + c.n_send, r0 + c.n_recv

    def start(ins, outs, ss, rs):
        for c, (i, o, s, r) in zip(comms, spans):
            c.start(ins[i], outs[o], _SemSlice(ss, s), _SemSlice(rs, r))

    def finish(ins, outs, ss, rs):
        for c, (i, o, s, r) in zip(comms, spans):
            if c.relay is not None:
                c.relay(ins[i], outs[o], _SemSlice(ss, s), _SemSlice(rs, r))
            c.finish(ins[i], outs[o], _SemSlice(ss, s), _SemSlice(rs, r))

    return _Comm(inputs, shapes, aliases, s0, r0, start, finish)


def _comm_alone(comm, name):
    n_ci = len(comm.inputs)
    hbm = pl.BlockSpec(memory_space=pl.ANY)

    def body(*refs):
        c_ins, c_outs, send_sems, recv_sems = refs[:n_ci], refs[n_ci:-2], refs[-2], refs[-1]
        comm.start(c_ins, c_outs, send_sems, recv_sems)
        if comm.relay is not None:
            comm.relay(c_ins, c_outs, send_sems, recv_sems)
        comm.finish(c_ins, c_outs, send_sems, recv_sems)

    return pl.pallas_call(
        body, name=name, out_shape=comm.out_shapes, in_specs=[hbm] * n_ci, out_specs=[hbm] * len(comm.out_shapes),
        input_output_aliases=comm.aliases,
        scratch_shapes=[pltpu.SemaphoreType.DMA((comm.n_send,)), pltpu.SemaphoreType.DMA((comm.n_recv,))],
    )(*comm.inputs)


SUM_TILES = (704, 512, 384, 320, 256, 192, 128, 64)


def _pair_sum(p, r1, kc_idx, name):
    _, _, n, c = p.shape
    tr = _pick(n, SUM_TILES)

    def body(s_ref, p_ref, r_ref, o32_ref, o16_ref):
        v = p_ref[...] + r_ref[...]
        o16_ref[...] = v.astype(BF16)

        @pl.when(pl.program_id(1) == s_ref[0])
        def _():
            o32_ref[...] = v

    blk = pl.BlockSpec((None, tr, c), lambda i, j, s: (j, i, 0))
    grid_spec = pltpu.PrefetchScalarGridSpec(
        num_scalar_prefetch=1, grid=(n // tr, 4),
        in_specs=[pl.BlockSpec((None, None, tr, c), lambda i, j, s: (s[1], j, i, 0)), blk],
        out_specs=[pl.BlockSpec((tr, c), lambda i, j, s: (i, 0)), blk])
    return pl.pallas_call(
        body, name=name, grid_spec=grid_spec,
        out_shape=[jax.ShapeDtypeStruct((n, c), F32), jax.ShapeDtypeStruct((4, n, c), BF16)],
        compiler_params=_params(("arbitrary", "arbitrary")),
    )(kc_idx, p, r1)


def _owner_sum(a32, r2, kc_idx, name):
    r, c = a32.shape
    tr = _pick(r, SUM_TILES)

    def body(s_ref, a_ref, r_ref, o_ref):
        v = a_ref[...]
        for j in range(3):
            v = v + r_ref[j].astype(F32)
        o_ref[...] = v

    grid_spec = pltpu.PrefetchScalarGridSpec(
        num_scalar_prefetch=1, grid=(r // tr,),
        in_specs=[pl.BlockSpec((tr, c), lambda i, s: (i, 0)),
                  pl.BlockSpec((3, tr, c), lambda i, s: (0, i, 0))],
        out_specs=pl.BlockSpec((None, tr, c), lambda i, s: (s[1], i, 0)))
    return pl.pallas_call(
        body, name=name, grid_spec=grid_spec,
        out_shape=jax.ShapeDtypeStruct((2, r, c), F32),
        compiler_params=_params(("arbitrary",)),
    )(kc_idx, a32, r2)


def _pack_local_half(w_in_s, w_out_s, w_up_s, w_down_s, comm, name):
    shards = {"in": w_in_s, "out": w_out_s, "up": w_up_s, "down": w_down_s}
    kinds = list(shards)
    assert sorted(off + PACK_HEIGHT[kind] for (kind, _), off in PACK_OFF.items()) == sorted(PACK_OFF.values())[1:] + [PACK_ROWS]
    for kind in kinds:
        assert shards[kind].shape[1] == (4 if kind == "in" else 2) * PACK_HEIGHT[kind], (kind, shards[kind].shape)

    def body(*refs):
        w_refs, p_ref = dict(zip(kinds, refs[:4])), refs[4]
        scr, sems = dict(zip(kinds, refs[5:9])), refs[9]
        c = lax.axis_index("c")
        copies = {}
        for n, (kind, l) in enumerate(sorted(PACK_OFF)):
            rows = scr[kind].shape[1]
            copies[(kind, l)] = pltpu.make_async_copy(w_refs[kind].at[l, pl.ds(c * rows, rows)], scr[kind].at[l], sems.at[n])
            copies[(kind, l)].start()
        for (kind, l), off in sorted(PACK_OFF.items(), key=lambda kv: kv[1]):
            copies[(kind, l)].wait()
            h = PACK_HEIGHT[kind]
            if kind == "in":
                for t in range(2):
                    p_ref[off:off + h, t * IN_PIECE_COLS:(t + 1) * IN_PIECE_COLS] = scr[kind][l, t * h:(t + 1) * h, :].astype(BF16)
                p_ref[off:off + h, 2 * IN_PIECE_COLS:] = jnp.zeros((h, 1024 - 2 * IN_PIECE_COLS), BF16)
            else:
                p_ref[off:off + h, :] = scr[kind][l].astype(BF16)

    hbm = pl.BlockSpec(memory_space=pl.ANY)
    scratch = [pltpu.VMEM((DEPTH, shards[kind].shape[1] // 2, shards[kind].shape[2]), F32) for kind in kinds]
    outs = _comm_call(
        body, comm, name=name, grid=(1,), in_specs=[hbm] * 4,
        out_specs=[pl.BlockSpec((PACK_ROWS, 1024), lambda i: (0, 0))],
        out_shape=[jax.ShapeDtypeStruct((PACK_ROWS, 1024), BF16)],
        args=[shards[kind] for kind in kinds], aliases={}, semantics=("arbitrary",),
        scratch=scratch + [pltpu.SemaphoreType.DMA((len(PACK_OFF),))])
    return outs


def _unpack_in_pieces(w_ref, own_ref, w_scr):
    if own_ref is not None:
        me = 4 * lax.axis_index("x") + 2 * lax.axis_index("y") + lax.axis_index("c")
    for d in range(N_DEV):
        k, c = d // 2, d % 2
        for t in range(2):
            piece = w_ref[d, :, t * IN_PIECE_COLS:(t + 1) * IN_PIECE_COLS]
            if own_ref is not None:
                piece = jnp.where(me == d, own_ref[:, t * IN_PIECE_COLS:(t + 1) * IN_PIECE_COLS], piece)
            w_scr[c * 512 + t * 256:c * 512 + (t + 1) * 256, k * IN_PIECE_COLS:(k + 1) * IN_PIECE_COLS] = piece


def _in_weight_operands(wg):
    specs, args = [_gathered_spec(wg, "in")], [wg["in"][0]]
    if "in_own" in wg:
        own, off = wg["in_own"]
        h = PACK_HEIGHT["in"]
        assert off % h == 0
        specs.append(pl.BlockSpec((h, 1024), lambda *_: (off // h, 0), pipeline_mode=pl.Buffered(1)))
        args.append(own)
    return specs, args


class _Rows:
    def __init__(self, nb, seq, ctx):
        self.nb, self.seq, self.ctx = nb, seq, ctx
        self.n_lat, self.n_ctx = nb * seq, nb * ctx
        self.rows = self.n_lat + self.n_ctx
        self.tm = _pick(np.gcd(seq, self.n_ctx), (512, 256, 128))
        self.tiles_per_ex = seq // self.tm
        self.n_tiles = self.rows // self.tm
        self.n_lat_tiles = self.n_lat // self.tm
        self.groups = nb + 1

    def latent_only(self):
        rt = _Rows(self.nb, self.seq, self.ctx)
        rt.n_tiles = self.n_lat_tiles
        return rt

    def group(self, i):
        return jnp.minimum(i // self.tiles_per_ex, self.nb)

    def first_of_group(self, i):
        return jnp.logical_and(i % self.tiles_per_ex == 0, i <= self.n_lat_tiles)


def _mod_spec(rt):
    return pl.BlockSpec((1, N_MOD, D_MODEL), lambda i: (rt.group(i), 0, 0))


def _row_spec(rt, cols):
    return pl.BlockSpec((rt.tm, cols), lambda i: (i, 0))


def _vec_spec(cols):
    return pl.BlockSpec((1, cols), lambda i: (0, 0))


def _group_spec(rt):
    return pl.BlockSpec((1, 1, D_MODEL), lambda i: (rt.group(i), 0, 0))


def _gathered_spec(wg, kind):
    h, off = PACK_HEIGHT[kind], wg[kind][1]
    assert off % h == 0, (kind, off)
    return pl.BlockSpec((N_DEV, h, 1024), lambda *_: (0, off // h, 0), pipeline_mode=pl.Buffered(1))


def _group_shape(rt):
    return jax.ShapeDtypeStruct((rt.groups, 1, D_MODEL), F32)


def _vec_shape(cols=D_MODEL):
    return jax.ShapeDtypeStruct((1, cols), F32)


def _rms_inv(v):
    return lax.rsqrt(jnp.mean(v * v, axis=-1, keepdims=True) + EPS)


def _norm_mod_val(h_, g_, mod_ref, i_shift, i_scale):
    n = h_ * _rms_inv(h_) * g_
    return n * (1.0 + mod_ref[0, i_scale:i_scale + 1, :]) + mod_ref[0, i_shift:i_shift + 1, :]


def _post_norm_val(h_, z_, g_, mod_ref, i_gate):
    return h_ + mod_ref[0, i_gate:i_gate + 1, :] * (z_ * _rms_inv(z_) * g_)


def _post_norm_bwd_val(dh_, z_, g_, gate):
    rinv = _rms_inv(z_)
    n0 = z_ * rinv
    dn = dh_ * gate * g_
    dz = rinv * (dn - n0 * jnp.mean(dn * n0, axis=-1, keepdims=True))
    return dz, jnp.sum(dh_ * n0 * g_, axis=0, keepdims=True), jnp.sum(dh_ * gate * n0, axis=0, keepdims=True)


def _norm_mod_bwd_val(du_, h_, g_, one_sc):
    rinv = _rms_inv(h_)
    n0 = h_ * rinv
    dn = du_ * g_ * one_sc
    dh = rinv * (dn - n0 * jnp.mean(dn * n0, axis=-1, keepdims=True))
    return (dh, jnp.sum(du_, axis=0, keepdims=True), jnp.sum(du_ * n0 * g_, axis=0, keepdims=True),
            jnp.sum(du_ * one_sc * n0, axis=0, keepdims=True))


def _accumulate(rt, i, group_pairs, global_pairs):
    @pl.when(rt.first_of_group(i))
    def _():
        for ref, _ in group_pairs:
            ref[...] = jnp.zeros_like(ref)

    @pl.when(i == 0)
    def _():
        for ref, _ in global_pairs:
            ref[...] = jnp.zeros_like(ref)

    for ref, val in group_pairs:
        ref[0] += val
    for ref, val in global_pairs:
        ref[...] += val


def _rope_tables(rt):
    pos = np.arange(rt.seq)
    axis_dim = HEAD_DIM // 2
    inv = (ROPE_THETA ** (-np.arange(0, axis_dim, 2, dtype=np.float32) / axis_dim)).astype(np.float32)
    ang_r = (pos // GRID_W).astype(np.float32)[:, None] * inv[None, :]
    ang_c = (pos % GRID_W).astype(np.float32)[:, None] * inv[None, :]
    cr, sr, cc, sc = np.cos(ang_r), np.sin(ang_r), np.cos(ang_c), np.sin(ang_c)
    zero = np.zeros_like(sr)
    cos = np.concatenate([cr, cr, cc, cc], axis=1)
    s_lo = np.concatenate([zero, sr, zero, sc], axis=1)
    s_hi = np.concatenate([-sr, zero, -sc, zero], axis=1)

    def full(t, ctx_value):
        return jnp.asarray(np.concatenate([np.tile(t, (1, 2)), np.full((rt.tm, 128), ctx_value)], axis=0), F32)

    return full(cos, 1.0), full(s_lo, 0.0), full(s_hi, 0.0)


def _table_spec(rt):
    return pl.BlockSpec((rt.tm, 128), lambda i: (jnp.where(i < rt.n_lat_tiles, i % rt.tiles_per_ex, rt.tiles_per_ex), 0))


def _head_mean(x):
    r = lax.broadcasted_iota(jnp.int32, (128, 128), 0) // HEAD_DIM
    c = lax.broadcasted_iota(jnp.int32, (128, 128), 1) // HEAD_DIM
    ones = jnp.where(r == c, 1.0 / HEAD_DIM, 0.0).astype(F32)
    return jnp.dot(x, ones, preferred_element_type=F32, precision=lax.Precision.HIGH)


def _head_stats(t):
    return lax.rsqrt(_head_mean(t * t) + EPS)


def _prep_fwd_body(tm, qkv_ref, c, s1, s2, qn, kn, out_ref):
    def rope(t):
        return t * c + pltpu.roll(t, 16, 1) * s1 + pltpu.roll(t, 112, 1) * s2

    for j in range(12):
        t = qkv_ref[:, j * 128:(j + 1) * 128]
        if j < 4:
            t = rope(t * _head_stats(t) * qn) * Q_SCALE
        elif j == COL_KA:
            t = rope(t * _head_stats(t) * kn)
        elif 6 <= j < 10:
            t = rope(t) * Q_SCALE
        elif j == COL_KB:
            t = rope(t)
        out_ref[:, j * 128:(j + 1) * 128] = t.astype(BF16)


def _prep_bwd_body(dq, dkv, qkv_ref, c, s1, s2, qn, kn, out_ref):
    rows = slice(None)

    def rope_bwd(d):
        return d * c + pltpu.roll(d * s1, 112, 1) + pltpu.roll(d * s2, 16, 1)

    def norm_bwd(t, g, dy):
        rinv = _head_stats(t)
        n = t * rinv
        dn = dy * g
        return rinv * (dn - n * _head_mean(dn * n)), jnp.sum(dy * n, axis=0, keepdims=True)

    dqn = jnp.zeros((1, 128), F32)
    dkn = jnp.zeros((1, 128), F32)
    for j in range(12):
        if j < 4:
            d, dg = norm_bwd(qkv_ref[rows, j * 128:(j + 1) * 128], qn, rope_bwd(dq(slice(j * 128, (j + 1) * 128)) * Q_SCALE))
            dqn = dqn + dg
        elif j == COL_KA:
            d, dg = norm_bwd(qkv_ref[rows, j * 128:(j + 1) * 128], kn, rope_bwd(dkv(slice(0, 128))))
            dkn = dkn + dg
        elif j == COL_VA:
            d = dkv(slice(128, 256))
        elif j < 10:
            d = rope_bwd(dq(slice((j - 2) * 128, (j - 1) * 128)) * Q_SCALE)
        elif j == COL_KB:
            d = rope_bwd(dkv(slice(256, 384)))
        else:
            d = dkv(slice(384, 512))
        out_ref[rows, j * 128:(j + 1) * 128] = d.astype(BF16)
    return dqn, dkn


def _in_fwd(rt, h, gamma, mod, wg, tables, qn, kn, name):
    w_specs, w_args = _in_weight_operands(wg)
    n_w = len(w_args)
    joined = not isinstance(h, (tuple, list))
    n_h = 1 if joined else 2

    def body(*refs):
        g_ref, mod_ref = refs[n_h:n_h + 2]
        rest = refs[n_h + 2:]
        c_ref, s1_ref, s2_ref, qn_ref, kn_ref, u_ref, qkn_ref, qkvp_ref = rest[n_w:n_w + 8]
        qkv_ref, w_scr = rest[-2:]
        i = pl.program_id(0)

        @pl.when(i == 0)
        def _():
            _unpack_in_pieces(rest[0], rest[1] if n_w == 2 else None, w_scr)

        if joined:
            h_ = refs[0][...]
        else:
            h_ = jnp.where(i < rt.n_lat_tiles, refs[0][...], refs[1][...])
            rest[n_w + 8][...] = h_
        u = _norm_mod_val(h_, g_ref[...], mod_ref, 0, 1).astype(BF16)
        u_ref[...] = u
        qkv_ref[...] = jnp.dot(u, w_scr[...], preferred_element_type=F32)
        qkn_ref[...] = qkv_ref[:, 0:NORMED_COLS]
        _prep_fwd_body(rt.tm, qkv_ref, c_ref[...], s1_ref[...], s2_ref[...], qn_ref[...], kn_ref[...], qkvp_ref)

    if joined:
        h_specs, h_args = [_row_spec(rt, D_MODEL)], [h]
    else:
        h_specs = [pl.BlockSpec((rt.tm, D_MODEL), lambda i: (jnp.minimum(i, rt.n_lat_tiles - 1), 0)),
                   pl.BlockSpec((rt.tm, D_MODEL), lambda i: (jnp.maximum(i - rt.n_lat_tiles, 0), 0))]
        h_args = list(h)
    out_specs = [_row_spec(rt, D_MODEL), _row_spec(rt, NORMED_COLS), _row_spec(rt, IN_COLS)]
    out_shape = [jax.ShapeDtypeStruct((rt.rows, D_MODEL), BF16), jax.ShapeDtypeStruct((rt.rows, NORMED_COLS), F32),
                 jax.ShapeDtypeStruct((rt.rows, IN_COLS), BF16)]
    if not joined:
        out_specs.append(_row_spec(rt, D_MODEL))
        out_shape.append(jax.ShapeDtypeStruct((rt.rows, D_MODEL), F32))
    return pl.pallas_call(
        body, name=name, grid=(rt.n_tiles,),
        in_specs=h_specs + [_vec_spec(D_MODEL), _mod_spec(rt)] + w_specs + [_table_spec(rt)] * 3 + [_vec_spec(128)] * 2,
        out_specs=out_specs, out_shape=out_shape,
        scratch_shapes=[pltpu.VMEM((rt.tm, IN_COLS), F32), pltpu.VMEM((D_MODEL, IN_COLS), BF16)],
        compiler_params=_params(("arbitrary",)),
    )(*h_args, gamma, mod, *w_args, *tables, qn, kn)


def _in_bwd(rt, dq, dkv, qkv, tables, qn, kn, wg, h, dres, mod, gamma, latent_only, name, comm=None, dead_ctx_dkv=None):
    last = rt.n_lat_tiles - 1
    w_specs, w_args = _in_weight_operands(wg)
    n_w = len(w_args)
    n_dead = 0 if dead_ctx_dkv is None else 1

    def body(dq_ref, dkv_ref, qkv_ref, c_ref, s1_ref, s2_ref, qn_ref, kn_ref, *rest):
        h_ref, dres_ref, mod_ref, g_ref, dqkv_ref, dh_ref, dqn_ref, dkn_ref, dsh_ref, dsc_ref, dg_ref, w_scr = rest[n_w + n_dead:]
        i = pl.program_id(0)

        @pl.when(i == 0)
        def _():
            _unpack_in_pieces(rest[0], rest[1] if n_w == 2 else None, w_scr)

        if n_dead:
            c1_ref, lat = rest[n_w], i <= last
            load_dq = lambda cols: jnp.where(lat, dq_ref[:, cols], 0.0)
            load_dkv = lambda cols: jnp.where(lat, dkv_ref[:, cols], c1_ref[:, cols])
            dres_ = jnp.where(lat, dres_ref[...], 0.0)
        else:
            load_dq, load_dkv, dres_ = (lambda cols: dq_ref[:, cols]), (lambda cols: dkv_ref[:, cols]), dres_ref[...]
        dqn, dkn = _prep_bwd_body(load_dq, load_dkv, qkv_ref, c_ref[...], s1_ref[...], s2_ref[...], qn_ref[...], kn_ref[...], dqkv_ref)
        du = lax.dot_general(dqkv_ref[...], w_scr[...], NT, preferred_element_type=F32)
        dh, dsh, dsc, dg = _norm_mod_bwd_val(du, h_ref[...], g_ref[...], 1.0 + mod_ref[0, 1:2, :])
        if latent_only:
            @pl.when(i <= last)
            def _():
                dh_ref[...] = dres_ + dh
        else:
            dh_ref[...] = dres_ + dh
        _accumulate(rt, i, [(dsh_ref, dsh), (dsc_ref, dsc)], [(dg_ref, dg), (dqn_ref, dqn), (dkn_ref, dkn)])

    dh_spec = pl.BlockSpec((rt.tm, D_MODEL), lambda i: (jnp.minimum(i, last), 0)) if latent_only else _row_spec(rt, D_MODEL)
    dead_specs = [] if dead_ctx_dkv is None else [pl.BlockSpec((rt.tm, 512), lambda i: (jnp.maximum(i - rt.n_lat_tiles, 0), 0))]
    dead_args = [] if dead_ctx_dkv is None else [dead_ctx_dkv]
    return _comm_call(
        body, comm, name=name, grid=(rt.n_tiles,),
        in_specs=[_row_spec(rt, 1024), _row_spec(rt, 512), _row_spec(rt, NORMED_COLS)] + [_table_spec(rt)] * 3 + [_vec_spec(128)] * 2
        + w_specs + dead_specs + [_row_spec(rt, D_MODEL), _row_spec(rt, D_MODEL), _mod_spec(rt), _vec_spec(D_MODEL)],
        out_specs=[_row_spec(rt, IN_COLS), dh_spec, _vec_spec(128), _vec_spec(128),
                   _group_spec(rt), _group_spec(rt), _vec_spec(D_MODEL)],
        out_shape=[jax.ShapeDtypeStruct((rt.rows, IN_COLS), BF16),
                   jax.ShapeDtypeStruct((rt.n_lat if latent_only else rt.rows, D_MODEL), F32),
                   _vec_shape(128), _vec_shape(128), _group_shape(rt), _group_shape(rt), _vec_shape()],
        args=[dq, dkv, qkv, *tables, qn, kn, *w_args, *dead_args, h, dres, mod, gamma], aliases={}, semantics=("arbitrary",),
        scratch=[pltpu.VMEM((D_MODEL, IN_COLS), BF16)])


def _out_fwd(rt, o, wg, h, mod, g_post_mix, g_pre_mlp, name):
    def body(o_ref, w_ref, h_ref, mod_ref, gpost_ref, gpre_ref, mix_ref, h1_ref, u2_ref):
        mix = jnp.dot(o_ref[...], w_ref[...].reshape(D_MODEL, D_MODEL), preferred_element_type=F32)
        mix_ref[...] = mix
        h1 = _post_norm_val(h_ref[...], mix, gpost_ref[...], mod_ref, 2)
        h1_ref[...] = h1
        u2_ref[...] = _norm_mod_val(h1, gpre_ref[...], mod_ref, 3, 4).astype(BF16)

    return pl.pallas_call(
        body, name=name, grid=(rt.n_tiles,),
        in_specs=[_row_spec(rt, D_MODEL), _gathered_spec(wg, "out"), _row_spec(rt, D_MODEL), _mod_spec(rt),
                  _vec_spec(D_MODEL), _vec_spec(D_MODEL)],
        out_specs=[_row_spec(rt, D_MODEL)] * 3,
        out_shape=[jax.ShapeDtypeStruct((rt.rows, D_MODEL), F32), jax.ShapeDtypeStruct((rt.rows, D_MODEL), F32),
                   jax.ShapeDtypeStruct((rt.rows, D_MODEL), BF16)],
        compiler_params=_params(("parallel",)),
    )(o, wg["out"][0], h, mod, g_post_mix, g_pre_mlp)


def _out_bwd(rt, dh1, mix, wg, mod, g_post_mix, name, comm=None):
    def body(dh_ref, mix_ref, w_ref, mod_ref, g_ref, dmix_ref, do_ref, dgate_ref, dg_ref):
        i = pl.program_id(0)
        dz, dgate, dg = _post_norm_bwd_val(dh_ref[...], mix_ref[...], g_ref[...], mod_ref[0, 2:3, :])
        dzb = dz.astype(BF16)
        dmix_ref[...] = dzb
        do_ref[...] = lax.dot_general(dzb, w_ref[...].reshape(D_MODEL, D_MODEL), NT, preferred_element_type=F32).astype(BF16)
        _accumulate(rt, i, [(dgate_ref, dgate)], [(dg_ref, dg)])

    return _comm_call(
        body, comm, name=name, grid=(rt.n_tiles,),
        in_specs=[_row_spec(rt, D_MODEL), _row_spec(rt, D_MODEL), _gathered_spec(wg, "out"), _mod_spec(rt), _vec_spec(D_MODEL)],
        out_specs=[_row_spec(rt, D_MODEL), _row_spec(rt, D_MODEL), _group_spec(rt), _vec_spec(D_MODEL)],
        out_shape=[jax.ShapeDtypeStruct((rt.rows, D_MODEL), BF16), jax.ShapeDtypeStruct((rt.rows, D_MODEL), BF16),
                   _group_shape(rt), _vec_shape()],
        args=[dh1, mix, wg["out"][0], mod, g_post_mix], aliases={}, semantics=("arbitrary",))


def _w_chunk(w_ref, k):
    return w_ref[2 * k:2 * k + 2].reshape(1024, 1024)


def _mlp_fwd(rt, u2, h1, wg, mod, g_post_mlp, name, comm=None, target=None):
    last = rt.n_lat_tiles - 1

    def body(u2_ref, h1_ref, wu_ref, wd_ref, mod_ref, g_ref, *rest):
        u2_ = u2_ref[...]
        y = jnp.zeros((rt.tm, D_MODEL), F32)
        for k in range(D_FF // 1024):
            a = jnp.maximum(jnp.dot(u2_, _w_chunk(wu_ref, k), preferred_element_type=F32), 0.0)
            rest[-3 if target is None else -4][:, k * 1024:(k + 1) * 1024] = a.astype(BF16)
            y = y + jnp.dot((a * a).astype(BF16), _w_chunk(wd_ref, k), preferred_element_type=F32)
        h2 = _post_norm_val(h1_ref[...], y, g_ref[...], mod_ref, 5)
        if target is None:
            _, y_ref, h2_ref = rest
            y_ref[...] = y
            h2_ref[...] = h2
        else:
            t_ref, _, y_ref, dh_ref, sq_ref = rest
            y_ref[...] = y
            i = pl.program_id(0)

            @pl.when(i == 0)
            def _():
                sq_ref[...] = jnp.zeros_like(sq_ref)

            @pl.when(i <= last)
            def _():
                e = h2 - t_ref[...]
                dh_ref[...] = e * (1.0 / D_MODEL)
                sq_ref[...] += jnp.sum(e * e, axis=0, keepdims=True)

            @pl.when(i > last)
            def _():
                dh_ref[...] = jnp.zeros_like(dh_ref)

    in_specs = [_row_spec(rt, D_MODEL), _row_spec(rt, D_MODEL), _gathered_spec(wg, "up"), _gathered_spec(wg, "down"),
                _mod_spec(rt), _vec_spec(D_MODEL)]
    args = [u2, h1, wg["up"][0], wg["down"][0], mod, g_post_mlp]
    out_specs = [_row_spec(rt, D_FF), _row_spec(rt, D_MODEL), _row_spec(rt, D_MODEL)]
    out_shape = [jax.ShapeDtypeStruct((rt.rows, D_FF), BF16), jax.ShapeDtypeStruct((rt.rows, D_MODEL), F32),
                 jax.ShapeDtypeStruct((rt.rows, D_MODEL), F32)]
    if target is not None:
        in_specs.append(pl.BlockSpec((rt.tm, D_MODEL), lambda i: (jnp.minimum(i, last), 0)))
        args.append(target)
        out_specs.append(_vec_spec(D_MODEL))
        out_shape.append(_vec_shape())
    return _comm_call(body, comm, name=name, grid=(rt.n_tiles,), in_specs=in_specs, out_specs=out_specs, out_shape=out_shape,
                      args=args, aliases={}, semantics=("parallel",) if target is None else ("arbitrary",))


def _mlp_down_bwd(rt, dh, y, ra, wg, mod, g_post_mlp, name, comm=None):
    def body(dh_ref, y_ref, ra_ref, wd_ref, mod_ref, g_ref, dy_ref, da_ref, dgate_ref, dg_ref):
        i = pl.program_id(0)
        dz, dgate, dg = _post_norm_bwd_val(dh_ref[...], y_ref[...], g_ref[...], mod_ref[0, 5:6, :])
        dyb = dz.astype(BF16)
        dy_ref[...] = dyb
        for k in range(D_FF // 1024):
            dr = lax.dot_general(dyb, _w_chunk(wd_ref, k), NT, preferred_element_type=F32)
            da_ref[:, k * 1024:(k + 1) * 1024] = (dr * (2.0 * ra_ref[:, k * 1024:(k + 1) * 1024].astype(F32))).astype(BF16)
        _accumulate(rt, i, [(dgate_ref, dgate)], [(dg_ref, dg)])

    return _comm_call(
        body, comm, name=name, grid=(rt.n_tiles,),
        in_specs=[_row_spec(rt, D_MODEL), _row_spec(rt, D_MODEL), _row_spec(rt, D_FF), _gathered_spec(wg, "down"),
                  _mod_spec(rt), _vec_spec(D_MODEL)],
        out_specs=[_row_spec(rt, D_MODEL), _row_spec(rt, D_FF), _group_spec(rt), _vec_spec(D_MODEL)],
        out_shape=[jax.ShapeDtypeStruct((rt.rows, D_MODEL), BF16), jax.ShapeDtypeStruct((rt.rows, D_FF), BF16),
                   _group_shape(rt), _vec_shape()],
        args=[dh, y, ra, wg["down"][0], mod, g_post_mlp], aliases={}, semantics=("arbitrary",))


def _mlp_up_bwd(rt, da, wg, h1, dh, mod, g_pre_mlp, name):
    def body(da_ref, wu_ref, h1_ref, dh_ref, mod_ref, g_ref, dh1_ref, dsh_ref, dsc_ref, dg_ref):
        i = pl.program_id(0)
        du = jnp.zeros((rt.tm, D_MODEL), F32)
        for k in range(D_FF // 1024):
            du = du + lax.dot_general(da_ref[:, k * 1024:(k + 1) * 1024], _w_chunk(wu_ref, k), NT, preferred_element_type=F32)
        d, dsh, dsc, dg = _norm_mod_bwd_val(du, h1_ref[...], g_ref[...], 1.0 + mod_ref[0, 4:5, :])
        dh1_ref[...] = dh_ref[...] + d
        _accumulate(rt, i, [(dsh_ref, dsh), (dsc_ref, dsc)], [(dg_ref, dg)])

    return pl.pallas_call(
        body, name=name, grid=(rt.n_tiles,),
        in_specs=[_row_spec(rt, D_FF), _gathered_spec(wg, "up"), _row_spec(rt, D_MODEL), _row_spec(rt, D_MODEL),
                  _mod_spec(rt), _vec_spec(D_MODEL)],
        out_specs=[_row_spec(rt, D_MODEL), _group_spec(rt), _group_spec(rt), _vec_spec(D_MODEL)],
        out_shape=[jax.ShapeDtypeStruct((rt.rows, D_MODEL), F32), _group_shape(rt), _group_shape(rt), _vec_shape()],
        compiler_params=_params(("arbitrary",)),
    )(da, wg["up"][0], h1, dh, mod, g_pre_mlp)


def _wgrad_packed(rt, a, b, kind, off, n_rows, p_prev, name, comm=None):
    h = PACK_HEIGHT[kind]
    tk = rt.tm
    assert off % h == 0, (kind, off)

    def body(a_ref, b_ref, *rest):
        o_ref = rest[-1]
        i = pl.program_id(0)

        @pl.when(i == 0)
        def _():
            o_ref[...] = jnp.zeros_like(o_ref)

        if kind == "in":
            res = lax.dot_general(a_ref[...], b_ref[...], TN, preferred_element_type=F32)
            for k in range(4):
                for c in range(2):
                    for t in range(2):
                        o_ref[c, k, :, t * IN_PIECE_COLS:(t + 1) * IN_PIECE_COLS] += \
                            res[c * 512 + t * h:c * 512 + (t + 1) * h, k * IN_PIECE_COLS:(k + 1) * IN_PIECE_COLS]
        elif kind == "out":
            res = lax.dot_general(a_ref[...], b_ref[...], TN, preferred_element_type=F32)
            for k in range(4):
                for c in range(2):
                    o_ref[c, k] += res[(2 * k + c) * h:(2 * k + c + 1) * h]
        else:
            for k in range(4):
                if kind == "up":
                    res = lax.dot_general(a_ref[...], b_ref[:, k * 1024:(k + 1) * 1024], TN, preferred_element_type=F32)
                else:
                    ra = a_ref[:, k * 1024:(k + 1) * 1024].astype(F32)
                    res = lax.dot_general((ra * ra).astype(BF16), b_ref[...], TN, preferred_element_type=F32)
                o_ref[0, k] += res[0:h]
                o_ref[1, k] += res[h:2 * h]

    in_specs = [pl.BlockSpec((tk, a.shape[1]), lambda i: (i, 0)), pl.BlockSpec((tk, b.shape[1]), lambda i: (i, 0))]
    args = [a, b]
    aliases = {}
    if p_prev is not None:
        in_specs.append(pl.BlockSpec(memory_space=pl.ANY))
        args.append(p_prev)
        aliases = {2: 0}
    outs = _comm_call(
        body, comm, name=name, grid=(rt.n_tiles,),
        in_specs=in_specs,
        out_specs=[pl.BlockSpec((2, 4, h, 1024), lambda i: (0, 0, off // h, 0))],
        out_shape=[jax.ShapeDtypeStruct((2, 4, n_rows, 1024), F32)],
        args=args, aliases=aliases, semantics=("arbitrary",))
    return outs[0] if comm is None else outs


def _ada_wgrad(xs, dm, name):
    depth, _, cols = dm.shape

    def body(x_ref, d_ref, o_ref):
        for l in range(depth):
            o_ref[l] = lax.dot_general(x_ref[...], d_ref[l], TN, preferred_element_type=F32)

    return pl.pallas_call(body, name=name, out_shape=jax.ShapeDtypeStruct((depth, xs.shape[1], cols), F32),
                          compiler_params=pltpu.CompilerParams(vmem_limit_bytes=VMEM_LIMIT))(xs, dm)


def _stack_heads(x, kvi):
    x = x.astype(F32)
    tq = x.shape[0]
    lane = lax.broadcasted_iota(jnp.int32, (tq, 128), 1)
    keep = lane < HEAD_DIM if kvi == 0 else lane >= HEAD_DIM
    parts = []
    for p in range(2):
        pair = x[:, p * 128:(p + 1) * 128]
        swapped = pltpu.roll(pair, HEAD_DIM, 1)
        lo_head, hi_head = (pair, swapped) if kvi == 0 else (swapped, pair)
        parts += [jnp.where(keep, lo_head, 0.0), jnp.where(keep, hi_head, 0.0)]
    return jnp.concatenate(parts, axis=0).astype(BF16)


def _unstack_heads(o4, kvi):
    tq = o4.shape[0] // GROUP
    lane = lax.broadcasted_iota(jnp.int32, (tq, 128), 1)
    outs = []
    for p in range(2):
        r_lo, r_hi = o4[(2 * p) * tq:(2 * p + 1) * tq], o4[(2 * p + 1) * tq:(2 * p + 2) * tq]
        if kvi == 0:
            lo, hi = r_lo, pltpu.roll(r_hi, HEAD_DIM, 1)
        else:
            lo, hi = pltpu.roll(r_lo, HEAD_DIM, 1), r_hi
        outs.append(jnp.where(lane < HEAD_DIM, lo, hi))
    return jnp.concatenate(outs, axis=1)


def _per_head(shape, axis, tq, values):
    head = lax.broadcasted_iota(jnp.int32, shape, axis) // tq
    out = jnp.zeros(shape, F32)
    for g in range(GROUP):
        out = jnp.where(head == g, values[g], out)
    return out


KEY_CHUNK = 512
Q_TILE = 128
Q_TILE_FWD = 256


def _key_chunks(k_ref, v_ref, n, kc=KEY_CHUNK):
    kc = min(kc, n)
    return [(k_ref[c * kc:(c + 1) * kc, :], v_ref[c * kc:(c + 1) * kc, :], None) for c in range(n // kc)]


def _softmax_fwd(qs, chunks, sink_col):
    logits = []
    for k, _, mask in chunks:
        s = lax.dot_general(qs, k, NT, preferred_element_type=F32)
        logits.append(s if mask is None else jnp.where(mask, s, NEG_BIG))
    m = functools.reduce(jnp.maximum, [jnp.max(s, axis=1, keepdims=True) for s in logits])
    if sink_col is not None:
        m = jnp.maximum(m, sink_col)
    l = jnp.zeros_like(m) if sink_col is None else jnp.exp(sink_col - m)
    acc = jnp.zeros((qs.shape[0], 128), F32)
    for s, (_, v, _) in zip(logits, chunks):
        p = jnp.exp(s - m)
        l = l + jnp.sum(p, axis=1, keepdims=True)
        acc = acc + jnp.dot(p.astype(BF16), v, preferred_element_type=F32)
    return acc / l, m + jnp.log(l)


def _to_rows(col):
    return jnp.transpose(jnp.broadcast_to(col, (col.shape[0], 128)))[0:8, :]


def _softmax_bwd(qs, dos, lse_row, delta_row, chunks):
    dq = jnp.zeros((qs.shape[0], 128), F32)
    grads = []
    for k, v, mask in chunks:
        s = lax.dot_general(k, qs, NT, preferred_element_type=F32)
        if mask is not None:
            s = jnp.where(mask, s, NEG_BIG)
        p = jnp.exp(s - lse_row)
        dp = lax.dot_general(v, dos, NT, preferred_element_type=F32)
        ds = (p * (dp - delta_row)).astype(BF16)
        dv = jnp.dot(p.astype(BF16), dos, preferred_element_type=F32)
        dk = jnp.dot(ds, qs, preferred_element_type=F32)
        dq = dq + lax.dot_general(ds, k, TN, preferred_element_type=F32)
        grads.append((dk, dv))
    return dq, grads


def _band(qi, tq, seq):
    span = tq + 2 * WINDOW
    start = pl.multiple_of(jnp.clip(qi * tq - WINDOW, 0, seq - span), 64)
    return start, span


def _band_mask(qi, tq, start, span, query_axis):
    shape = (GROUP * tq, span) if query_axis == 0 else (span, GROUP * tq)
    qpos = qi * tq + lax.broadcasted_iota(jnp.int32, shape, query_axis) % tq
    kpos = start + lax.broadcasted_iota(jnp.int32, shape, 1 - query_axis)
    return jnp.abs(kpos - qpos) <= WINDOW


def _qkv_specs(rt, tq, q_row, ctx_row, with_latent):
    specs = [pl.BlockSpec((tq, 256), functools.partial(lambda b, i, col: (q_row(b, i), col), col=col)) for col in (0, 1, 3, 4)]
    if with_latent:
        specs += [pl.BlockSpec((rt.seq, 128), functools.partial(lambda b, i, col: (b, col), col=col))
                  for col in (COL_KA, COL_VA, COL_KB, COL_VB)]
    specs += [pl.BlockSpec((rt.ctx, 128), functools.partial(lambda b, i, col: (ctx_row(b), col), col=col))
              for col in (COL_KA, COL_VA, COL_KB, COL_VB)]
    return specs


def _attn_fwd(rt, qkvp, sink, o_prev, name, comm=None):
    latent = o_prev is None
    seq, ctx, nb = rt.seq, rt.ctx, rt.nb
    tq = Q_TILE_FWD if latent else ctx
    tile = Q_TILE if latent else ctx
    parts = tq // tile
    nq = seq // tq if latent else 1
    ctx_blk0 = rt.n_lat // ctx
    q_row = (lambda b, i: b * nq + i) if latent else (lambda b, i: ctx_blk0 + b)

    def store_lse(lse_ref, j, lse_col):
        rows = _to_rows(lse_col)
        for part in range(parts):
            lse_ref[part, j] = jnp.concatenate([rows[:, g * tq + part * tile:g * tq + (part + 1) * tile] for g in range(GROUP)], axis=1)

    def body(sink_ref, qa0, qa1, qb0, qb1, *rest):
        if latent:
            kal, val, kbl, vbl, kac, vac, kbc, vbc, o_ref, lse_ref = rest
        else:
            kac, vac, kbc, vbc, _, o_ref, lse_ref = rest
        qi = pl.program_id(1)
        for kvi, (qa, qb) in enumerate(((qa0, qb0), (qa1, qb1))):
            src_a = _key_chunks(kac, vac, ctx)
            src_b = _key_chunks(kbc, vbc, ctx)
            if latent:
                src_a += _key_chunks(kal, val, seq, seq)
                start, span = _band(qi, tq, seq)
                src_b.append((kbl[pl.ds(start, span), :], vbl[pl.ds(start, span), :], _band_mask(qi, tq, start, span, 0)))
            oa, lse = _softmax_fwd(_stack_heads(qa[...], kvi), src_a, None)
            o_ref[:, kvi * 256:(kvi + 1) * 256] = _unstack_heads(oa, kvi).astype(BF16)
            store_lse(lse_ref, kvi, lse)
            sink_col = _per_head((GROUP * tq, 1), 0, tq, [sink_ref[kvi * GROUP + g] for g in range(GROUP)])
            ob, lse = _softmax_fwd(_stack_heads(qb[...], kvi), src_b, sink_col)
            o_ref[:, 512 + kvi * 256:512 + (kvi + 1) * 256] = _unstack_heads(ob, kvi).astype(BF16)
            store_lse(lse_ref, 2 + kvi, lse)

    specs = _qkv_specs(rt, tq, q_row, lambda b: ctx_blk0 + b, latent)
    args = [sink] + [qkvp] * len(specs)
    in_specs = [pl.BlockSpec(memory_space=pltpu.SMEM)] + specs
    aliases = {}
    if not latent:
        in_specs.append(pl.BlockSpec(memory_space=pl.ANY))
        args.append(o_prev)
        aliases = {len(args) - 1: 0}
    return _comm_call(
        body, comm, name=name, grid=(nb, nq),
        in_specs=in_specs,
        out_specs=[pl.BlockSpec((tq, 1024), lambda b, i: (q_row(b, i), 0)),
                   pl.BlockSpec((parts, 4, 8, GROUP * tile), lambda b, i: (b * nq + i, 0, 0, 0))],
        out_shape=[jax.ShapeDtypeStruct((rt.rows, 1024), BF16), jax.ShapeDtypeStruct((nb * nq * parts, 4, 8, GROUP * tile), F32)],
        args=args, aliases=aliases, semantics=("parallel", "parallel"))


def _attn_bwd(rt, qkvp, o, lse, do, sink, prev, name, comm=None):
    latent = prev is None
    seq, ctx, nb = rt.seq, rt.ctx, rt.nb
    tq = Q_TILE if latent else ctx
    nq = seq // tq if latent else 1
    ctx_blk0 = rt.n_lat // ctx
    q_row = (lambda b, i: b * nq + i) if latent else (lambda b, i: ctx_blk0 + b)
    kc = min(KEY_CHUNK, seq)

    def body(sink_ref, qa0, qa1, qb0, qb1, *rest):
        if latent:
            kal, val, kbl, vbl, kac, vac, kbc, vbc, do_ref, o_ref, lse_ref, dq_ref, dl_ref, dc_ref, dsink_ref = rest
        else:
            kac, vac, kbc, vbc, do_ref, o_ref, lse_ref, c1_ref, _, _, dq_ref, dc_ref, dsink_ref = rest
        b, qi = pl.program_id(0), pl.program_id(1)

        def rows_of(cols, kvi, mixer):
            dos = _stack_heads(do_ref[:, cols], kvi)
            delta = jnp.sum(dos.astype(F32) * _stack_heads(o_ref[:, cols], kvi).astype(F32), axis=1, keepdims=True)
            return dos, lse_ref[0, 2 * mixer + kvi, 0:1, :], _to_rows(delta)[0:1, :]

        @pl.when(jnp.logical_and(b == 0, qi == 0))
        def _():
            dsink_ref[...] = jnp.zeros_like(dsink_ref)

        if latent:
            @pl.when(qi == 0)
            def _():
                dc_ref[...] = jnp.zeros_like(dc_ref)
                dl_ref[...] = jnp.zeros_like(dl_ref)
        else:
            dc_ref[...] = c1_ref[...]

        head_row = lax.broadcasted_iota(jnp.int32, (8, 128), 0)
        for kvi, (qa, qb) in enumerate(((qa0, qb0), (qa1, qb1))):
            cols = slice(kvi * 256, (kvi + 1) * 256)
            dos, lse_row, delta_row = rows_of(cols, kvi, 0)
            src = _key_chunks(kac, vac, ctx)
            if latent:
                src += _key_chunks(kal, val, seq)
            dq4, grads = _softmax_bwd(_stack_heads(qa[...], kvi), dos, lse_row, delta_row, src)
            dq_ref[:, cols] = _unstack_heads(dq4, kvi)
            dc_ref[:, 0:128] += grads[0][0]
            dc_ref[:, 128:256] += grads[0][1]
            for c, (dk, dv) in enumerate(grads[1:]):
                dl_ref[c * kc:(c + 1) * kc, 0:128] += dk
                dl_ref[c * kc:(c + 1) * kc, 128:256] += dv
            cols = slice(512 + kvi * 256, 512 + (kvi + 1) * 256)
            dos, lse_row, delta_row = rows_of(cols, kvi, 1)
            src = _key_chunks(kbc, vbc, ctx)
            if latent:
                start, span = _band(qi, tq, seq)
                src.append((kbl[pl.ds(start, span), :], vbl[pl.ds(start, span), :], _band_mask(qi, tq, start, span, 1)))
            dq4, grads = _softmax_bwd(_stack_heads(qb[...], kvi), dos, lse_row, delta_row, src)
            dq_ref[:, cols] = _unstack_heads(dq4, kvi)
            dc_ref[:, 256:384] += grads[0][0]
            dc_ref[:, 384:512] += grads[0][1]
            if latent:
                dl_ref[pl.ds(start, span), 256:384] += grads[1][0]
                dl_ref[pl.ds(start, span), 384:512] += grads[1][1]
            sink_row = _per_head((1, GROUP * tq), 1, tq, [sink_ref[kvi * GROUP + g] for g in range(GROUP)])
            dsink = -jnp.exp(sink_row - lse_row) * delta_row
            head = lax.broadcasted_iota(jnp.int32, (1, GROUP * tq), 1) // tq
            upd = jnp.zeros((8, 128), F32)
            for g in range(GROUP):
                upd = jnp.where(head_row == kvi * GROUP + g, jnp.sum(jnp.where(head == g, dsink, 0.0)), upd)
            dsink_ref[...] += upd

    specs = _qkv_specs(rt, tq, q_row, lambda b: ctx_blk0 + b, latent)
    q_rows_spec = pl.BlockSpec((tq, 1024), lambda b, i: (q_row(b, i), 0))
    in_specs = ([pl.BlockSpec(memory_space=pltpu.SMEM)] + specs
                + [q_rows_spec, q_rows_spec, pl.BlockSpec((1, 4, 8, GROUP * tq), lambda b, i: (b * nq + i, 0, 0, 0))])
    args = [sink] + [qkvp] * len(specs) + [do, o, lse]
    dq_shape = jax.ShapeDtypeStruct((rt.rows, 1024), F32)
    dkv_shape = jax.ShapeDtypeStruct((rt.rows, 512), F32)
    dsink_spec, dsink_shape = pl.BlockSpec((8, 128), lambda b, i: (0, 0)), jax.ShapeDtypeStruct((8, 128), F32)
    dq_spec = pl.BlockSpec((tq, 1024), lambda b, i: (q_row(b, i), 0))
    if latent:
        out_specs = [dq_spec, pl.BlockSpec((seq, 512), lambda b, i: (b, 0)), pl.BlockSpec((ctx, 512), lambda b, i: (b, 0)), dsink_spec]
        out_shape = [dq_shape, dkv_shape, jax.ShapeDtypeStruct((rt.n_ctx, 512), F32), dsink_shape]
        aliases = {}
    else:
        dq_prev, dkv_prev, c1 = prev
        in_specs += [pl.BlockSpec((ctx, 512), lambda b, i: (b, 0)), pl.BlockSpec(memory_space=pl.ANY), pl.BlockSpec(memory_space=pl.ANY)]
        args += [c1, dq_prev, dkv_prev]
        out_specs = [dq_spec, pl.BlockSpec((ctx, 512), lambda b, i: (ctx_blk0 + b, 0)), dsink_spec]
        out_shape = [dq_shape, dkv_shape, dsink_shape]
        aliases = {len(args) - 2: 0, len(args) - 1: 1}
    return _comm_call(body, comm, name=name, grid=(nb, nq), in_specs=in_specs, out_specs=out_specs, out_shape=out_shape,
                      args=args, aliases=aliases, semantics=("arbitrary", "arbitrary"))


def _silu(x):
    return x / (1.0 + jnp.exp(-x))


def _whole(shape):
    return pl.BlockSpec(shape, lambda i, s: (0,) * len(shape))


def _ada_half_spec(cols):
    return pl.BlockSpec((DEPTH, D_MODEL, cols), lambda i, s: (0, 0, s[0]))


def _ada_fwd(cond, w_ada, b_half, c_idx, name):
    rows = cond.shape[0]
    cols = w_ada.shape[2] // 2

    def body(s_ref, c_ref, w_ref, b_ref, x_ref, o_ref):
        xs = _silu(c_ref[...]).astype(BF16)
        x_ref[...] = xs
        for l in range(DEPTH):
            o_ref[l] = jnp.dot(xs, w_ref[l].astype(BF16), preferred_element_type=F32) + b_ref[l]

    grid_spec = pltpu.PrefetchScalarGridSpec(
        num_scalar_prefetch=1, grid=(1,),
        in_specs=[_whole(cond.shape), _ada_half_spec(cols), _whole(b_half.shape)],
        out_specs=[_whole((rows, D_MODEL)), _whole((DEPTH, rows, cols))])
    return pl.pallas_call(
        body, name=name, grid_spec=grid_spec,
        out_shape=[jax.ShapeDtypeStruct((rows, D_MODEL), BF16), jax.ShapeDtypeStruct((DEPTH, rows, cols), F32)],
        compiler_params=_params(("arbitrary",)),
    )(c_idx, cond, w_ada, b_half)


def _ada_cond_bwd(dcx, w_ada, c_idx, name):
    _, rows, cols = dcx.shape

    def body(s_ref, d_ref, w_ref, o_ref):
        acc = jnp.zeros((rows, D_MODEL), F32)
        for l in range(DEPTH):
            acc = acc + lax.dot_general(d_ref[l], w_ref[l].astype(BF16), NT, preferred_element_type=F32)
        o_ref[...] = acc

    grid_spec = pltpu.PrefetchScalarGridSpec(
        num_scalar_prefetch=1, grid=(1,),
        in_specs=[_whole(dcx.shape), _ada_half_spec(cols)], out_specs=_whole((rows, D_MODEL)))
    return pl.pallas_call(body, name=name, grid_spec=grid_spec, out_shape=jax.ShapeDtypeStruct((rows, D_MODEL), F32),
                          compiler_params=_params(("arbitrary",)))(c_idx, dcx, w_ada)


def _dev_sum(x, name):
    _, r, c = x.shape

    def body(x_ref, o_ref):
        v = x_ref[0]
        for d in range(1, N_DEV):
            v = v + x_ref[d]
        o_ref[...] = v

    return pl.pallas_call(body, name=name, out_shape=jax.ShapeDtypeStruct((r, c), F32))(x)


def _adam_val(w, g, m, v):
    c1 = 1.0 / (1.0 - ADAM_B1 ** ADAM_STEP)
    c2 = 1.0 / (1.0 - ADAM_B2 ** ADAM_STEP)
    nm = ADAM_B1 * m + (1.0 - ADAM_B1) * g
    nv = ADAM_B2 * v + (1.0 - ADAM_B2) * (g * g)
    return -ADAM_LR * ((nm * c1) / (jnp.sqrt(nv * c2) + ADAM_EPS) + ADAM_WD * w), nm, nv


def _small_update(tot, dcc_parts, params, n_groups, name):
    n_p = len(params)
    mod_rows = n_groups * N_MOD
    head_row = DEPTH * mod_rows + 4 * DEPTH

    def body(tot_ref, dcc_ref, *refs):
        ins, outs = refs[:3 * n_p], refs[3 * n_p:]

        def update(p, rows, cols, g):
            w_ref, m_ref, v_ref = ins[3 * p:3 * p + 3]
            g_ref, d_ref, nm_ref, nv_ref = outs[4 * p:4 * p + 4]
            d, nm, nv = _adam_val(w_ref[rows, cols], g, m_ref[rows, cols], v_ref[rows, cols])
            g_ref[rows, cols] = g
            d_ref[rows, cols] = d
            nm_ref[rows, cols] = nm
            nv_ref[rows, cols] = nv

        acc = dcc_ref[0, 0:1, :]
        for d in range(1, N_DEV):
            acc = acc + dcc_ref[d, 0:1, :]
        c = ins[0][...]
        sg = 1.0 / (1.0 + jnp.exp(-c))
        update(0, slice(0, 1), slice(None), acc * (sg * (1.0 + c * (1.0 - sg))))
        for l in range(DEPTH):
            for i in range(N_MOD):
                g = tot_ref[l * mod_rows + i:l * mod_rows + i + 1, :]
                for grp in range(1, n_groups):
                    g = g + tot_ref[l * mod_rows + grp * N_MOD + i:l * mod_rows + grp * N_MOD + i + 1, :]
                update(1, slice(l, l + 1), slice(i * D_MODEL, (i + 1) * D_MODEL), g)
            for j in range(4):
                row = DEPTH * mod_rows + 4 * l + j
                update(2 + j, slice(l, l + 1), slice(None), tot_ref[row:row + 1, :])
            head = tot_ref[head_row + l:head_row + l + 1, :]
            update(6, slice(l, l + 1), slice(None), head[:, 0:HEAD_DIM] + head[:, HEAD_DIM:2 * HEAD_DIM])
            update(7, slice(l, l + 1), slice(None), head[:, 2 * HEAD_DIM:3 * HEAD_DIM] + head[:, 3 * HEAD_DIM:4 * HEAD_DIM])
            update(8, slice(l, l + 1), slice(None), head[:, 4 * HEAD_DIM:4 * HEAD_DIM + ins[3 * 8].shape[1]])

    shapes = [jax.ShapeDtypeStruct(w.shape, F32) for w, _, _ in params for _ in range(4)]
    outs = pl.pallas_call(body, name=name, out_shape=shapes)(tot, dcc_parts, *[a for p in params for a in p])
    return [tuple(outs[4 * p:4 * p + 4]) for p in range(n_p)]


def _adamw(w, g, m, v, name):
    r, c = w.shape
    tr = _pick(r, (256, 128, 64, 32, 24, 16, 8))

    def body(w_ref, g_ref, m_ref, v_ref, d_ref, nm_ref, nv_ref):
        d_ref[...], nm_ref[...], nv_ref[...] = _adam_val(w_ref[...], g_ref[...], m_ref[...], v_ref[...])

    spec = pl.BlockSpec((tr, c), lambda i: (i, 0))
    return pl.pallas_call(body, name=name, grid=(r // tr,), in_specs=[spec] * 4, out_specs=[spec] * 3,
                          out_shape=[jax.ShapeDtypeStruct((r, c), F32)] * 3, compiler_params=_params(("parallel",)))(w, g, m, v)


def _adamw_shard(kind, l, w, m, v, halves, off, prev, name):
    h = PACK_HEIGHT[kind]
    assert off % h == 0, (kind, off)
    _, r, c = w.shape
    rows = r // 2

    def body(w_ref, m_ref, v_ref, p_ref, *rest):
        g_ref, d_ref, nm_ref, nv_ref = rest[-4:]
        if kind == "in":
            for t in range(2):
                g = p_ref[:, t * IN_PIECE_COLS:(t + 1) * IN_PIECE_COLS]
                rs = slice(t * h, (t + 1) * h)
                g_ref[rs, :] = g
                d_ref[rs, :], nm_ref[rs, :], nv_ref[rs, :] = _adam_val(w_ref[rs, :], g, m_ref[rs, :], v_ref[rs, :])
        else:
            g = p_ref[...]
            g_ref[...] = g
            d_ref[...], nm_ref[...], nv_ref[...] = _adam_val(w_ref[...], g, m_ref[...], v_ref[...])

    blk = pl.BlockSpec((None, rows, c), lambda half: (l, half, 0))
    in_specs = [blk, blk, blk, pl.BlockSpec((None, h, 1024), lambda half: (half, off // h, 0))]
    args = [w, m, v, halves]
    aliases = {}
    if prev is not None:
        in_specs += [pl.BlockSpec(memory_space=pl.ANY)] * 4
        args += list(prev)
        aliases = {4 + j: j for j in range(4)}
    return pl.pallas_call(
        body, name=name, grid=(2,), in_specs=in_specs, out_specs=[blk] * 4,
        out_shape=[jax.ShapeDtypeStruct(w.shape, F32)] * 4, input_output_aliases=aliases,
        compiler_params=_params(("parallel",)))(*args)


SMALL_ROWS = 48


def _small_rows(small, sq):
    def lane_pad(v):
        return jnp.pad(v, (0, D_MODEL - v.shape[0]))[None]

    head_rows = [lane_pad(jnp.concatenate([s["q_norm"][0], s["k_norm"][0], s["sink"]])) for s in small]
    loss_row = lane_pad((0.5 / D_MODEL) * jnp.sum(sq, keepdims=True)[0])
    rows = jnp.concatenate([s["mod"].reshape(-1, D_MODEL) for s in small] + [s["gammas"] for s in small] + head_rows + [loss_row], axis=0)
    return jnp.pad(rows, ((0, SMALL_ROWS - rows.shape[0]), (0, 0)))


def _local_step(x, ctx, target, mods, gam, qn, kn, sink, w_first, w_layers, packed, kc_idx):
    nb, seq, _ = x.shape
    rt = _Rows(nb, seq, ctx.shape[1])
    rt_lat = rt.latent_only()
    tables = _rope_tables(rt)
    fuse = packed is not None
    h = (x.reshape(rt.n_lat, D_MODEL), ctx.reshape(rt.n_ctx, D_MODEL))
    wg = [{}, {}] if fuse else [dict(w) for w in w_layers]
    wg[0]["in"] = (w_first, 0)
    if fuse:
        wg[0]["in_own"] = (packed, W_FIRST[0])
    saved = []
    for l in range(DEPTH):
        g_pre_mix, g_post_mix, g_pre_mlp, g_post_mlp = gam[l]
        if l == 0:
            u, qkv, qkvp, h = _in_fwd(rt, h, g_pre_mix, mods[l], wg[l], tables, qn[l], kn[l], f"in_fwd{l}")
        else:
            u, qkv, qkvp = _in_fwd(rt, h, g_pre_mix, mods[l], wg[l], tables, qn[l], kn[l], f"in_fwd{l}")
        if fuse and l == 0:
            o, lse_lat, w_mlp0, w_out0, w_in1 = _attn_fwd(rt, qkvp, sink[l], None, f"attn_lat_fwd{l}",
                                                         comm=_gather_comm(packed, [W_MLP0, W_OUT0, W_IN1], lead=2))
            wg[0].update({kind: (w_mlp0, PACK_OFF[(kind, 0)] - W_MLP0[0]) for kind in ("up", "down")})
            wg[0]["out"] = (w_out0, 0)
            wg[1] = {"in": (w_in1, 0)}
        elif fuse:
            o, lse_lat, w_mlp1, w_out1 = _attn_fwd(rt, qkvp, sink[l], None, f"attn_lat_fwd{l}",
                                                   comm=_gather_comm(packed, [W_MLP1, W_OUT1], lead=2))
            wg[1].update({kind: (w_mlp1, PACK_OFF[(kind, 1)] - W_MLP1[0]) for kind in ("up", "down")})
            wg[1]["out"] = (w_out1, 0)
        else:
            o, lse_lat = _attn_fwd(rt, qkvp, sink[l], None, f"attn_lat_fwd{l}")
        if l < DEPTH - 1:
            o, lse_ctx = _attn_fwd(rt, qkvp, sink[l], o, f"attn_ctx_fwd{l}")
            mix, h1, u2 = _out_fwd(rt, o, wg[l], h, mods[l], g_post_mix, g_pre_mlp, f"out_fwd{l}")
            r, y, h2 = _mlp_fwd(rt, u2, h1, wg[l], mods[l], g_post_mlp, f"mlp_fwd{l}")
        else:
            lse_ctx = None
            mix, h1, u2 = _out_fwd(rt_lat, o, wg[l], h, mods[l], g_post_mix, g_pre_mlp, f"out_fwd{l}")
            r, y, dh, sq = _mlp_fwd(rt_lat, u2, h1, wg[l], mods[l], g_post_mlp, f"mlp_fwd{l}", target=target.reshape(rt.n_lat, D_MODEL))
        saved.append((h, u, qkv, qkvp, o, lse_lat, lse_ctx, mix, h1, u2, r, y))
        h = h2

    small = [None] * DEPTH
    groups = {}
    for l in reversed(range(DEPTH)):
        g_pre_mix, g_post_mix, g_pre_mlp, g_post_mlp = gam[l]
        h0, u, qkv, qkvp, o, lse_lat, lse_ctx, mix, h1, u2, r, y = saved[l]
        mlp_group, mix_group = (G_LAYER1, G_LAYER1) if l == 1 else (G_MLP0, G_MIX0)
        hide = fuse and l == 0

        dead_ctx = l == DEPTH - 1
        rt_b = rt_lat if dead_ctx else rt
        outs = _mlp_down_bwd(rt_b, dh, y, r, wg[l], mods[l], g_post_mlp, f"mlp_down_bwd{l}",
                             comm=_pair_comm(groups[G_LAYER1]) if hide else None)
        dy, da, d_gate_m, d_g_post_mlp = outs[:4]
        if hide:
            sum1 = _pair_sum(groups[G_LAYER1], outs[4], kc_idx, "grad_pair_sum_layer1")
        p_mlp = _wgrad_packed(rt_b, r, dy, "down", PACK_OFF[("down", l)] - mlp_group[0], mlp_group[1], None, f"mlp_down_wgrad{l}")
        dh1, d_sh_m, d_sc_m, d_g_pre_mlp = _mlp_up_bwd(rt_b, da, wg[l], h1, dh, mods[l], g_pre_mlp, f"mlp_up_bwd{l}")
        p_mlp = _wgrad_packed(rt_b, u2, da, "up", PACK_OFF[("up", l)] - mlp_group[0], mlp_group[1], p_mlp, f"mlp_up_wgrad{l}")
        outs = _out_bwd(rt_b, dh1, mix, wg[l], mods[l], g_post_mix, f"out_bwd{l}", comm=_pair_comm(p_mlp) if hide else None)
        dmix, do, d_gate_a, d_g_post_mix = outs[:4]
        if hide:
            sum0 = _pair_sum(p_mlp, outs[4], kc_idx, "grad_pair_sum_mlp0")
        p_mix = _wgrad_packed(rt_b, o, dmix, "out", PACK_OFF[("out", l)] - mix_group[0], mix_group[1],
                              p_mlp if l == 1 else None, f"out_wgrad{l}")
        outs = _attn_bwd(rt, qkvp, o, lse_lat, do, sink[l], None, f"attn_lat_bwd{l}",
                         comm=_chip_comm([sum1[1], sum0[1]]) if hide else None)
        dq, dkv, dkv_c, dsink1 = outs[:4]
        if hide:
            groups[G_LAYER1] = _owner_sum(sum1[0], outs[4], kc_idx, "grad_owner_sum_layer1")
            groups[G_MLP0] = _owner_sum(sum0[0], outs[5], kc_idx, "grad_owner_sum_mlp0")
        if dead_ctx:
            dsink2 = jnp.zeros_like(dsink1)
            d_gate_m, d_sh_m, d_sc_m, d_gate_a = [a.at[nb].set(0.0) for a in (d_gate_m, d_sh_m, d_sc_m, d_gate_a)]
        else:
            dq, dkv, dsink2 = _attn_bwd(rt, qkvp, o, lse_ctx, do, sink[l], (dq, dkv, dkv_c), f"attn_ctx_bwd{l}")
        dqkv, dh, dqn, dkn, d_sh_a, d_sc_a, d_g_pre_mix = _in_bwd(rt, dq, dkv, qkv, tables, qn[l], kn[l], wg[l], h0, dh1, mods[l],
                                                                  g_pre_mix, l == 0, f"in_bwd{l}",
                                                                  dead_ctx_dkv=dkv_c if dead_ctx else None)
        dmod = jnp.concatenate([d_sh_a, d_sc_a, d_gate_a, d_sh_m, d_sc_m, d_gate_m], axis=1)
        small[l] = dict(mod=dmod, gammas=jnp.concatenate([d_g_pre_mix, d_g_post_mix, d_g_pre_mlp, d_g_post_mlp], axis=0),
                        q_norm=dqn, k_norm=dkn, sink=(dsink1 + dsink2)[:, 0])
        tail = _merge([_gather_comm(_small_rows(small, sq), [(0, SMALL_ROWS)]),
                       _halves_comm([groups[G_LAYER1], groups[G_MLP0]])]) if hide else None
        outs = _wgrad_packed(rt, u, dqkv, "in", PACK_OFF[("in", l)] - mix_group[0], mix_group[1], p_mix, f"in_wgrad{l}", comm=tail)
        if hide:
            groups[mix_group], small_g, groups[G_LAYER1], groups[G_MLP0] = outs
        else:
            groups[mix_group], small_g = outs, None
            if l == 0:
                groups[G_MLP0] = p_mlp
    return sq, dh.reshape(nb, seq, D_MODEL), [groups[G_LAYER1], groups[G_MLP0], groups[G_MIX0]], small, small_g


def kernel(x, c, ctx, c_ctx, w_ada, b_ada, g_pre_mix, g_post_mix, g_pre_mlp, g_post_mlp, w_in, q_norm, k_norm, sink, w_out, w_up, w_down, loss_target, m_c_ctx, m_w_ada, m_b_ada, m_g_pre_mix, m_g_post_mix, m_g_pre_mlp, m_g_post_mlp, m_w_in, m_q_norm, m_k_norm, m_sink, m_w_out, m_w_up, m_w_down, v_c_ctx, v_w_ada, v_b_ada, v_g_pre_mix, v_g_post_mix, v_g_pre_mlp, v_g_post_mlp, v_w_in, v_q_norm, v_k_norm, v_sink, v_w_out, v_w_up, v_w_down):
    nb = x.shape[0]
    ix, iy, ic = lax.axis_index("x"), lax.axis_index("y"), lax.axis_index("c")
    chip = 2 * ix + iy
    dev = 2 * chip + ic
    ada_cols = w_ada.shape[2] // 2

    c_rows = c.reshape(8, (nb * D_MODEL) // 8)
    packed, c_all = _pack_local_half(w_in, w_out, w_up, w_down, _gather_comm(c_rows, [(0, c_rows.shape[0])]), "pack_gather_c")
    c_all = c_all.reshape(N_DEV * nb, D_MODEL)
    n_cond = N_DEV * nb + 1
    cond_rows = 16 * ((n_cond + 15) // 16)
    cond = jnp.concatenate([c_all, c_ctx[None, :], jnp.zeros((cond_rows - n_cond, D_MODEL), F32)], axis=0)
    c_idx = ic.reshape(1).astype(jnp.int32)
    kc_idx = jnp.stack([chip, ic]).astype(jnp.int32)
    b_ada_half = lax.dynamic_slice_in_dim(b_ada, dev * ada_cols, ada_cols, 1)[:, None, :]
    x_ada, mod_part = _ada_fwd(cond, w_ada, b_ada_half, c_idx, "ada_fwd")
    mod_rows2d = mod_part.reshape(DEPTH * cond_rows, ada_cols)
    mod_g, w_first = _comm_alone(_merge([_gather_comm(mod_rows2d, [(0, mod_rows2d.shape[0])]),
                                         _gather_comm(packed, [W_FIRST], copy_own=False)]), "gather_mod_w_first")
    mod_all = mod_g.reshape(N_DEV, DEPTH, cond_rows, ada_cols).transpose(1, 2, 0, 3).reshape(DEPTH, cond_rows, N_MOD * D_MODEL)
    mods = []
    for l in range(DEPTH):
        mine = lax.dynamic_slice_in_dim(mod_all[l], dev * nb, nb, 0)
        mods.append(jnp.concatenate([mine, mod_all[l, n_cond - 1:n_cond]], axis=0).reshape(nb + 1, N_MOD, D_MODEL))

    gam = [(g_pre_mix[l][None], g_post_mix[l][None], g_pre_mlp[l][None], g_post_mlp[l][None]) for l in range(DEPTH)]
    qn = [jnp.tile(q_norm[l], 2)[None] for l in range(DEPTH)]
    kn = [jnp.tile(k_norm[l], 2)[None] for l in range(DEPTH)]
    _, grad_x, (h_layer1, h_mlp0, p_mix0), _, small_g = _local_step(x, ctx, loss_target, mods, gam, qn, kn, [sink[l] for l in range(DEPTH)],
                                                                 w_first, None, packed, kc_idx)

    def step(w, g, m, v, name):
        shape = w.shape
        cols = shape[-1]
        outs = _adamw(w.reshape(-1, cols), g.reshape(-1, cols), m.reshape(-1, cols), v.reshape(-1, cols), name)
        return tuple(a.reshape(shape) for a in outs)

    def shard_update(kind, w, m, v, layer0, layer1):
        outs = None
        for l, (halves, group) in enumerate((layer0, layer1)):
            outs = _adamw_shard(kind, l, w, m, v, halves, PACK_OFF[(kind, l)] - group[0], outs, f"adamw_w_{kind}{l}")
        return tuple(outs)

    tot = _dev_sum(small_g, "small_sum")
    mod_rows = (nb + 1) * N_MOD
    loss = tot[DEPTH * mod_rows + 4 * DEPTH + DEPTH, 0]

    ex = small_g[:, :DEPTH * mod_rows].reshape(N_DEV, DEPTH, nb + 1, N_MOD * D_MODEL)[:, :, :nb]
    ex = ex.transpose(1, 0, 2, 3).reshape(DEPTH, N_DEV * nb, N_MOD * D_MODEL)
    cx = tot[:DEPTH * mod_rows].reshape(DEPTH, nb + 1, N_MOD * D_MODEL)[:, nb:]
    dm = jnp.concatenate([ex, cx, jnp.zeros((DEPTH, cond_rows - n_cond, N_MOD * D_MODEL), F32)], axis=1)
    shard_cols = w_ada.shape[2]
    grad_w_ada = _ada_wgrad(x_ada, lax.dynamic_slice_in_dim(dm, chip * shard_cols, shard_cols, 2).astype(BF16), "ada_wgrad")
    dcx = jnp.pad(lax.dynamic_slice_in_dim(cx, dev * ada_cols, ada_cols, 2), ((0, 0), (0, 15), (0, 0))).astype(BF16)
    dcc = _ada_cond_bwd(dcx, w_ada, c_idx, "ada_cond_bwd")[0:8]

    r1, = _comm_alone(_pair_comm(p_mix0), "grad_pair_exchange_mix0")
    a32, a16 = _pair_sum(p_mix0, r1, kc_idx, "grad_pair_sum_mix0")
    r2, dcc_g = _comm_alone(_merge([_chip_comm([a16]), _gather_comm(dcc, [(0, dcc.shape[0])])]), "grad_chip_exchange_mix0")
    h_mix0 = _owner_sum(a32, r2, kc_idx, "grad_owner_sum_mix0")
    h_mix0, = _comm_alone(_halves_comm([h_mix0]), "grad_halves_exchange_mix0")

    small_names = ["c_ctx", "b_ada", "g_pre_mix", "g_post_mix", "g_pre_mlp", "g_post_mlp", "q_norm", "k_norm", "sink"]
    assert q_norm.shape[1] == HEAD_DIM and k_norm.shape[1] == HEAD_DIM
    small_res = _small_update(tot, dcc_g, [(c_ctx[None], m_c_ctx[None], v_c_ctx[None]), (b_ada, m_b_ada, v_b_ada),
                                           (g_pre_mix, m_g_pre_mix, v_g_pre_mix), (g_post_mix, m_g_post_mix, v_g_post_mix),
                                           (g_pre_mlp, m_g_pre_mlp, v_g_pre_mlp), (g_post_mlp, m_g_post_mlp, v_g_post_mlp),
                                           (q_norm, m_q_norm, v_q_norm), (k_norm, m_k_norm, v_k_norm), (sink, m_sink, v_sink)],
                              nb + 1, "small_update")
    res = {n: r for n, r in zip(small_names, small_res)}
    res["c_ctx"] = tuple(a[0] for a in res["c_ctx"])
    res["w_ada"] = (grad_w_ada, *step(w_ada, grad_w_ada, m_w_ada, v_w_ada, "adamw_w_ada"))
    res["w_up"] = shard_update("up", w_up, m_w_up, v_w_up, (h_mlp0, G_MLP0), (h_layer1, G_LAYER1))
    res["w_down"] = shard_update("down", w_down, m_w_down, v_w_down, (h_mlp0, G_MLP0), (h_layer1, G_LAYER1))
    res["w_in"] = shard_update("in", w_in, m_w_in, v_w_in, (h_mix0, G_MIX0), (h_layer1, G_LAYER1))
    res["w_out"] = shard_update("out", w_out, m_w_out, v_w_out, (h_mix0, G_MIX0), (h_layer1, G_LAYER1))

    order = ["c_ctx", "w_ada", "b_ada", "g_pre_mix", "g_post_mix", "g_pre_mlp", "g_post_mlp", "w_in", "q_norm", "k_norm", "sink", "w_out", "w_up", "w_down"]
    return (loss, grad_x, *[res[n][0] for n in order], *[res[n][1] for n in order],
            *[res[n][2] for n in order], *[res[n][3] for n in order])
```

```python
import functools

import jax
import jax.numpy as jnp
import numpy as np
from jax import lax
from jax.experimental import pallas as pl
from jax.experimental.pallas import tpu as pltpu

F32 = jnp.float32
BF16 = jnp.bfloat16

D_MODEL = 1024
HEAD_DIM = 64
GROUP = 4
WINDOW = 128
N_MOD = 6
D_FF = 4 * D_MODEL
IN_COLS = 1536
GRID_W = 64
ROPE_THETA = 10000.0
EPS = 1e-6
NEG_BIG = -1e30
Q_SCALE = HEAD_DIM ** -0.5
DEPTH = 2
N_DEV = 8

ADAM_LR = 0.001
ADAM_B1 = 0.9
ADAM_B2 = 0.999
ADAM_EPS = 1e-08
ADAM_WD = 0.01
ADAM_STEP = 10

V7X_VMEM_BYTES = 64 * 1024 * 1024
VMEM_LIMIT = V7X_VMEM_BYTES - 8 * 1024 * 1024

MESH = pl.DeviceIdType.MESH
NT = (((1,), (1,)), ((), ()))
TN = (((0,), (0,)), ((), ()))

COL_KA, COL_VA, COL_KB, COL_VB = 4, 5, 10, 11
NORMED_COLS = 640

PACK_HEIGHT = {"up": 512, "down": 512, "in": 256, "out": 128}
IN_PIECE_COLS = 384
PACK_OFF = {("up", 0): 0, ("down", 0): 512, ("in", 0): 1024, ("out", 0): 1280,
            ("up", 1): 1408, ("down", 1): 1920, ("in", 1): 2432, ("out", 1): 2688}
PACK_ROWS = 2816
W_FIRST, W_MLP0, W_OUT0, W_IN1, W_MLP1, W_OUT1 = (1024, 256), (0, 1024), (1280, 128), (2432, 256), (1408, 1024), (2688, 128)
G_LAYER1, G_MLP0, G_MIX0 = (1408, 1408), (0, 1024), (1024, 384)


def _pick(n, cands):
    for t in cands:
        if n % t == 0:
            return t
    raise ValueError(f"no tile for {n}")


def _params(sem):
    return pltpu.CompilerParams(dimension_semantics=sem, vmem_limit_bytes=VMEM_LIMIT)


class _Comm:
    def __init__(self, inputs, out_shapes, aliases, n_send, n_recv, start, finish, relay=None, lead=0):
        self.inputs, self.out_shapes, self.aliases = list(inputs), list(out_shapes), dict(aliases)
        self.n_send, self.n_recv, self.start, self.finish, self.relay, self.lead = n_send, n_recv, start, finish, relay, lead


def _comm_call(compute, comm, *, name, grid, in_specs, out_specs, out_shape, args, aliases, semantics, scratch=()):
    in_specs, out_specs, out_shape, args, aliases = list(in_specs), list(out_specs), list(out_shape), list(args), dict(aliases)
    scratch = list(scratch)
    if comm is None:
        return pl.pallas_call(compute, name=name, grid=grid, in_specs=in_specs, out_specs=out_specs, out_shape=out_shape,
                              input_output_aliases=aliases, scratch_shapes=scratch, compiler_params=_params(semantics))(*args)
    n_in, n_out, n_ci, n_co = len(args), len(out_shape), len(comm.inputs), len(comm.out_shapes)
    hbm = pl.BlockSpec(memory_space=pl.ANY)
    aliases.update({n_in + i: n_out + o for i, o in comm.aliases.items()})

    def body(*refs):
        ins, c_ins = refs[:n_in], refs[n_in:n_in + n_ci]
        outs, c_outs = refs[n_in + n_ci:n_in + n_ci + n_out], refs[n_in + n_ci + n_out:n_in + n_ci + n_out + n_co]
        scr = refs[n_in + n_ci + n_out + n_co:-2]
        send_sems, recv_sems = refs[-2:]
        ids = [pl.program_id(a) for a in range(len(grid))]
        first = functools.reduce(jnp.logical_and, [i == 0 for i in ids])
        last = functools.reduce(jnp.logical_and, [i == g - 1 for i, g in zip(ids, grid)])

        @pl.when(first)
        def _():
            comm.start(c_ins, c_outs, send_sems, recv_sems)

        compute(*ins, *outs, *scr)

        if comm.relay is not None:
            step = functools.reduce(lambda acc, ig: acc * ig[1] + ig[0], zip(ids, grid), 0)

            @pl.when(step == int(np.prod(grid)) - 1 - comm.lead)
            def _():
                comm.relay(c_ins, c_outs, send_sems, recv_sems)

        @pl.when(last)
        def _():
            comm.finish(c_ins, c_outs, send_sems, recv_sems)

    return pl.pallas_call(
        body, name=name, grid=grid,
        in_specs=in_specs + [hbm] * n_ci, out_specs=out_specs + [hbm] * n_co, out_shape=out_shape + comm.out_shapes,
        input_output_aliases=aliases,
        scratch_shapes=scratch + [pltpu.SemaphoreType.DMA((comm.n_send,)), pltpu.SemaphoreType.DMA((comm.n_recv,))],
        compiler_params=_params(("arbitrary",) * len(grid)),
    )(*args, *comm.inputs)


def _place():
    x_, y_, c_ = lax.axis_index("x"), lax.axis_index("y"), lax.axis_index("c")
    return x_, y_, c_, [(1 - x_, y_), (x_, 1 - y_), (1 - x_, 1 - y_)]


GATHER_SENDS, GATHER_RECVS = 8, 7


def _gather_copies(packed_ref, wg_ref, send_sems, recv_sems, rows, nth=0):
    r0, n = rows
    x_, y_, c_, chips = _place()
    me, sibling = (x_, y_, c_), (x_, y_, 1 - c_)
    src = packed_ref.at[pl.ds(r0, n), :]

    def slot(px, py, pc):
        return wg_ref.at[4 * px + 2 * py + pc]

    def copy(k, block, to, from_packed=False):
        return pltpu.make_async_remote_copy(src_ref=src if from_packed else slot(*block), dst_ref=slot(*block),
                                            send_sem=send_sems.at[GATHER_SENDS * nth + k], recv_sem=recv_sems.at[GATHER_RECVS * nth + k],
                                            device_id=to, device_id_type=MESH)

    own = [copy(0, me, sibling, True)] + [copy(1 + j, me, (*chip, c_), True) for j, chip in enumerate(chips)]
    passed = [copy(4 + j, (*chip, c_), sibling) for j, chip in enumerate(chips)]
    over_ici = [copy(1 + j, (*chip, c_), me) for j, chip in enumerate(chips)]
    from_sibling = [copy(0, sibling, me)] + [copy(4 + j, (*chip, 1 - c_), me) for j, chip in enumerate(chips)]
    mine = pltpu.make_async_copy(src, slot(*me), send_sems.at[GATHER_SENDS * nth + 7])
    return mine, own, passed, over_ici, from_sibling


def _gather_start(packed_ref, wg_ref, send_sems, recv_sems, rows, nth=0, copy_own=True):
    mine, own, _, _, _ = _gather_copies(packed_ref, wg_ref, send_sems, recv_sems, rows, nth)
    if copy_own:
        mine.start()
    for cp in own:
        cp.start()


def _gather_relay(packed_ref, wg_ref, send_sems, recv_sems, rows, nth=0):
    _, _, passed, over_ici, _ = _gather_copies(packed_ref, wg_ref, send_sems, recv_sems, rows, nth)
    for arrived, onward in zip(over_ici, passed):
        arrived.wait_recv()
        onward.start()


def _gather_finish(packed_ref, wg_ref, send_sems, recv_sems, rows, nth=0, copy_own=True):
    mine, own, passed, _, from_sibling = _gather_copies(packed_ref, wg_ref, send_sems, recv_sems, rows, nth)
    for arrived in from_sibling:
        arrived.wait_recv()
    for cp in own + passed:
        cp.wait_send()
    if copy_own:
        mine.wait()


def _gather_comm(packed, ranges, copy_own=True, lead=0):
    shapes = [jax.ShapeDtypeStruct((N_DEV, n, packed.shape[1]), packed.dtype) for _, n in ranges]

    def start(ins, outs, ss, rs):
        for nth, rows in enumerate(ranges):
            _gather_start(ins[0], outs[nth], ss, rs, rows, nth, copy_own)

    def relay(ins, outs, ss, rs):
        for nth, rows in enumerate(ranges):
            _gather_relay(ins[0], outs[nth], ss, rs, rows, nth)

    def finish(ins, outs, ss, rs):
        for nth, rows in enumerate(ranges):
            _gather_finish(ins[0], outs[nth], ss, rs, rows, nth, copy_own)

    return _Comm([packed], shapes, {}, GATHER_SENDS * len(ranges), GATHER_RECVS * len(ranges), start, finish, relay, lead)


def _pair_copy(p_ref, out_ref, send_sems, recv_sems):
    x_, y_, c_, _ = _place()
    return pltpu.make_async_remote_copy(src_ref=p_ref.at[1 - c_], dst_ref=out_ref,
                                        send_sem=send_sems.at[0], recv_sem=recv_sems.at[0],
                                        device_id=(x_, y_, 1 - c_), device_id_type=MESH)


def _pair_comm(p):
    return _Comm([p], [jax.ShapeDtypeStruct(p.shape[1:], p.dtype)], {}, 1, 1,
                 lambda ins, outs, ss, rs: _pair_copy(ins[0], outs[0], ss, rs).start(),
                 lambda ins, outs, ss, rs: _pair_copy(ins[0], outs[0], ss, rs).wait())


def _chip_copies(a_refs, out_refs, send_sems, recv_sems):
    _, _, c_, chips = _place()
    return [pltpu.make_async_remote_copy(src_ref=a_ref.at[2 * tx + ty], dst_ref=o_ref.at[j],
                                         send_sem=send_sems.at[3 * g + j], recv_sem=recv_sems.at[3 * g + j],
                                         device_id=(tx, ty, c_), device_id_type=MESH)
            for g, (a_ref, o_ref) in enumerate(zip(a_refs, out_refs)) for j, (tx, ty) in enumerate(chips)]


def _chip_start(a_refs, out_refs, send_sems, recv_sems):
    for cp in _chip_copies(a_refs, out_refs, send_sems, recv_sems):
        cp.start()


def _chip_finish(a_refs, out_refs, send_sems, recv_sems):
    for cp in _chip_copies(a_refs, out_refs, send_sems, recv_sems):
        cp.wait()


def _chip_comm(arrays):
    shapes = [jax.ShapeDtypeStruct((3,) + a.shape[1:], a.dtype) for a in arrays]
    return _Comm(arrays, shapes, {}, 3 * len(arrays), 3 * len(arrays), _chip_start, _chip_finish)


def _halves_copies(in_refs, out_refs, send_sems, recv_sems):
    x_, y_, c_, _ = _place()
    return [pltpu.make_async_remote_copy(src_ref=o_ref.at[c_], dst_ref=o_ref.at[c_], send_sem=send_sems.at[i], recv_sem=recv_sems.at[i],
                                         device_id=(x_, y_, 1 - c_), device_id_type=MESH)
            for i, o_ref in enumerate(out_refs)]


def _halves_start(in_refs, out_refs, send_sems, recv_sems):
    for cp in _halves_copies(in_refs, out_refs, send_sems, recv_sems):
        cp.start()


def _halves_finish(in_refs, out_refs, send_sems, recv_sems):
    for cp in _halves_copies(in_refs, out_refs, send_sems, recv_sems):
        cp.wait()


def _halves_comm(arrays):
    shapes = [jax.ShapeDtypeStruct(a.shape, a.dtype) for a in arrays]
    return _Comm(arrays, shapes, {i: i for i in range(len(arrays))}, len(arrays), len(arrays), _halves_start, _halves_finish)


class _SemSlice:
    class _At:
        def __init__(self, sems, first):
            self.sems, self.first = sems, first

        def __getitem__(self, k):
            return self.sems.at[self.first + k]

    def __init__(self, sems, first):
        self.at = _SemSlice._At(sems, first)


def _merge(comms):
    inputs = [a for c in comms for a in c.inputs]
    shapes = [s for c in comms for s in c.out_shapes]
    aliases, spans = {}, []
    i0 = o0 = s0 = r0 = 0
    for c in comms:
        aliases.update({i0 + i: o0 + o for i, o in c.aliases.items()})
        spans.append((slice(i0, i0 + len(c.inputs)), slice(o0, o0 + len(c.out_shapes)), s0, r0))
        i0, o0, s0, r0 = i0 + len(c.inputs), o0 + len(c.out_shapes), s0 + c.n_send, r0 + c.n_recv

    def start(ins, outs, ss, rs):
        for c, (i, o, s, r) in zip(comms, spans):
            c.start(ins[i], outs[o], _SemSlice(ss, s), _SemSlice(rs, r))

    def finish(ins, outs, ss, rs):
        for c, (i, o, s, r) in zip(comms, spans):
            if c.relay is not None:
                c.relay(ins[i], outs[o], _SemSlice(ss, s), _SemSlice(rs, r))
            c.finish(ins[i], outs[o], _SemSlice(ss, s), _SemSlice(rs, r))

    return _Comm(inputs, shapes, aliases, s0, r0, start, finish)


def _comm_alone(comm, name):
    n_ci = len(comm.inputs)
    hbm = pl.BlockSpec(memory_space=pl.ANY)

    def body(*refs):
        c_ins, c_outs, send_sems, recv_sems = refs[:n_ci], refs[n_ci:-2], refs[-2], refs[-1]
        comm.start(c_ins, c_outs, send_sems, recv_sems)
        if comm.relay is not None:
            comm.relay(c_ins, c_outs, send_sems, recv_sems)
        comm.finish(c_ins, c_outs, send_sems, recv_sems)

    return pl.pallas_call(
        body, name=name, out_shape=comm.out_shapes, in_specs=[hbm] * n_ci, out_specs=[hbm] * len(comm.out_shapes),
        input_output_aliases=comm.aliases,
        scratch_shapes=[pltpu.SemaphoreType.DMA((comm.n_send,)), pltpu.SemaphoreType.DMA((comm.n_recv,))],
    )(*comm.inputs)


SUM_TILES = (704, 512, 384, 320, 256, 192, 128, 64)


def _pair_sum(p, r1, kc_idx, name):
    _, _, n, c = p.shape
    tr = _pick(n, SUM_TILES)

    def body(s_ref, p_ref, r_ref, o32_ref, o16_ref):
        v = p_ref[...] + r_ref[...]
        o16_ref[...] = v.astype(BF16)

        @pl.when(pl.program_id(1) == s_ref[0])
        def _():
            o32_ref[...] = v

    blk = pl.BlockSpec((None, tr, c), lambda i, j, s: (j, i, 0))
    grid_spec = pltpu.PrefetchScalarGridSpec(
        num_scalar_prefetch=1, grid=(n // tr, 4),
        in_specs=[pl.BlockSpec((None, None, tr, c), lambda i, j, s: (s[1], j, i, 0)), blk],
        out_specs=[pl.BlockSpec((tr, c), lambda i, j, s: (i, 0)), blk])
    return pl.pallas_call(
        body, name=name, grid_spec=grid_spec,
        out_shape=[jax.ShapeDtypeStruct((n, c), F32), jax.ShapeDtypeStruct((4, n, c), BF16)],
        compiler_params=_params(("arbitrary", "arbitrary")),
    )(kc_idx, p, r1)


def _owner_sum(a32, r2, kc_idx, name):
    r, c = a32.shape
    tr = _pick(r, SUM_TILES)

    def body(s_ref, a_ref, r_ref, o_ref):
        v = a_ref[...]
        for j in range(3):
            v = v + r_ref[j].astype(F32)
        o_ref[...] = v

    grid_spec = pltpu.PrefetchScalarGridSpec(
        num_scalar_prefetch=1, grid=(r // tr,),
        in_specs=[pl.BlockSpec((tr, c), lambda i, s: (i, 0)),
                  pl.BlockSpec((3, tr, c), lambda i, s: (0, i, 0))],
        out_specs=pl.BlockSpec((None, tr, c), lambda i, s: (s[1], i, 0)))
    return pl.pallas_call(
        body, name=name, grid_spec=grid_spec,
        out_shape=jax.ShapeDtypeStruct((2, r, c), F32),
        compiler_params=_params(("arbitrary",)),
    )(kc_idx, a32, r2)


def _pack_local_half(w_in_s, w_out_s, w_up_s, w_down_s, comm, name):
    shards = {"in": w_in_s, "out": w_out_s, "up": w_up_s, "down": w_down_s}
    kinds = list(shards)
    assert sorted(off + PACK_HEIGHT[kind] for (kind, _), off in PACK_OFF.items()) == sorted(PACK_OFF.values())[1:] + [PACK_ROWS]
    for kind in kinds:
        assert shards[kind].shape[1] == (4 if kind == "in" else 2) * PACK_HEIGHT[kind], (kind, shards[kind].shape)

    def body(*refs):
        w_refs, p_ref = dict(zip(kinds, refs[:4])), refs[4]
        scr, sems = dict(zip(kinds, refs[5:9])), refs[9]
        c = lax.axis_index("c")
        copies = {}
        for n, (kind, l) in enumerate(sorted(PACK_OFF)):
            rows = scr[kind].shape[1]
            copies[(kind, l)] = pltpu.make_async_copy(w_refs[kind].at[l, pl.ds(c * rows, rows)], scr[kind].at[l], sems.at[n])
            copies[(kind, l)].start()
        for (kind, l), off in sorted(PACK_OFF.items(), key=lambda kv: kv[1]):
            copies[(kind, l)].wait()
            h = PACK_HEIGHT[kind]
            if kind == "in":
                for t in range(2):
                    p_ref[off:off + h, t * IN_PIECE_COLS:(t + 1) * IN_PIECE_COLS] = scr[kind][l, t * h:(t + 1) * h, :].astype(BF16)
                p_ref[off:off + h, 2 * IN_PIECE_COLS:] = jnp.zeros((h, 1024 - 2 * IN_PIECE_COLS), BF16)
            else:
                p_ref[off:off + h, :] = scr[kind][l].astype(BF16)

    hbm = pl.BlockSpec(memory_space=pl.ANY)
    scratch = [pltpu.VMEM((DEPTH, shards[kind].shape[1] // 2, shards[kind].shape[2]), F32) for kind in kinds]
    outs = _comm_call(
        body, comm, name=name, grid=(1,), in_specs=[hbm] * 4,
        out_specs=[pl.BlockSpec((PACK_ROWS, 1024), lambda i: (0, 0))],
        out_shape=[jax.ShapeDtypeStruct((PACK_ROWS, 1024), BF16)],
        args=[shards[kind] for kind in kinds], aliases={}, semantics=("arbitrary",),
        scratch=scratch + [pltpu.SemaphoreType.DMA((len(PACK_OFF),))])
    return outs


def _unpack_in_pieces(w_ref, own_ref, w_scr):
    if own_ref is not None:
        me = 4 * lax.axis_index("x") + 2 * lax.axis_index("y") + lax.axis_index("c")
    for d in range(N_DEV):
        k, c = d // 2, d % 2
        for t in range(2):
            piece = w_ref[d, :, t * IN_PIECE_COLS:(t + 1) * IN_PIECE_COLS]
            if own_ref is not None:
                piece = jnp.where(me == d, own_ref[:, t * IN_PIECE_COLS:(t + 1) * IN_PIECE_COLS], piece)
            w_scr[c * 512 + t * 256:c * 512 + (t + 1) * 256, k * IN_PIECE_COLS:(k + 1) * IN_PIECE_COLS] = piece


def _in_weight_operands(wg):
    specs, args = [_gathered_spec(wg, "in")], [wg["in"][0]]
    if "in_own" in wg:
        own, off = wg["in_own"]
        h = PACK_HEIGHT["in"]
        assert off % h == 0
        specs.append(pl.BlockSpec((h, 1024), lambda *_: (off // h, 0), pipeline_mode=pl.Buffered(1)))
        args.append(own)
    return specs, args


class _Rows:
    def __init__(self, nb, seq, ctx):
        self.nb, self.seq, self.ctx = nb, seq, ctx
        self.n_lat, self.n_ctx = nb * seq, nb * ctx
        self.rows = self.n_lat + self.n_ctx
        self.tm = _pick(np.gcd(seq, self.n_ctx), (512, 256, 128))
        self.tiles_per_ex = seq // self.tm
        self.n_tiles = self.rows // self.tm
        self.n_lat_tiles = self.n_lat // self.tm
        self.groups = nb + 1

    def latent_only(self):
        rt = _Rows(self.nb, self.seq, self.ctx)
        rt.n_tiles = self.n_lat_tiles
        return rt

    def group(self, i):
        return jnp.minimum(i // self.tiles_per_ex, self.nb)

    def first_of_group(self, i):
        return jnp.logical_and(i % self.tiles_per_ex == 0, i <= self.n_lat_tiles)


def _mod_spec(rt):
    return pl.BlockSpec((1, N_MOD, D_MODEL), lambda i: (rt.group(i), 0, 0))


def _row_spec(rt, cols):
    return pl.BlockSpec((rt.tm, cols), lambda i: (i, 0))


def _vec_spec(cols):
    return pl.BlockSpec((1, cols), lambda i: (0, 0))


def _group_spec(rt):
    return pl.BlockSpec((1, 1, D_MODEL), lambda i: (rt.group(i), 0, 0))


def _gathered_spec(wg, kind):
    h, off = PACK_HEIGHT[kind], wg[kind][1]
    assert off % h == 0, (kind, off)
    return pl.BlockSpec((N_DEV, h, 1024), lambda *_: (0, off // h, 0), pipeline_mode=pl.Buffered(1))


def _group_shape(rt):
    return jax.ShapeDtypeStruct((rt.groups, 1, D_MODEL), F32)


def _vec_shape(cols=D_MODEL):
    return jax.ShapeDtypeStruct((1, cols), F32)


def _rms_inv(v):
    return lax.rsqrt(jnp.mean(v * v, axis=-1, keepdims=True) + EPS)


def _norm_mod_val(h_, g_, mod_ref, i_shift, i_scale):
    n = h_ * _rms_inv(h_) * g_
    return n * (1.0 + mod_ref[0, i_scale:i_scale + 1, :]) + mod_ref[0, i_shift:i_shift + 1, :]


def _post_norm_val(h_, z_, g_, mod_ref, i_gate):
    return h_ + mod_ref[0, i_gate:i_gate + 1, :] * (z_ * _rms_inv(z_) * g_)


def _post_norm_bwd_val(dh_, z_, g_, gate):
    rinv = _rms_inv(z_)
    n0 = z_ * rinv
    dn = dh_ * gate * g_
    dz = rinv * (dn - n0 * jnp.mean(dn * n0, axis=-1, keepdims=True))
    return dz, jnp.sum(dh_ * n0 * g_, axis=0, keepdims=True), jnp.sum(dh_ * gate * n0, axis=0, keepdims=True)


def _norm_mod_bwd_val(du_, h_, g_, one_sc):
    rinv = _rms_inv(h_)
    n0 = h_ * rinv
    dn = du_ * g_ * one_sc
    dh = rinv * (dn - n0 * jnp.mean(dn * n0, axis=-1, keepdims=True))
    return (dh, jnp.sum(du_, axis=0, keepdims=True), jnp.sum(du_ * n0 * g_, axis=0, keepdims=True),
            jnp.sum(du_ * one_sc * n0, axis=0, keepdims=True))


def _accumulate(rt, i, group_pairs, global_pairs):
    @pl.when(rt.first_of_group(i))
    def _():
        for ref, _ in group_pairs:
            ref[...] = jnp.zeros_like(ref)

    @pl.when(i == 0)
    def _():
        for ref, _ in global_pairs:
            ref[...] = jnp.zeros_like(ref)

    for ref, val in group_pairs:
        ref[0] += val
    for ref, val in global_pairs:
        ref[...] += val


def _rope_tables(rt):
    pos = np.arange(rt.seq)
    axis_dim = HEAD_DIM // 2
    inv = (ROPE_THETA ** (-np.arange(0, axis_dim, 2, dtype=np.float32) / axis_dim)).astype(np.float32)
    ang_r = (pos // GRID_W).astype(np.float32)[:, None] * inv[None, :]
    ang_c = (pos % GRID_W).astype(np.float32)[:, None] * inv[None, :]
    cr, sr, cc, sc = np.cos(ang_r), np.sin(ang_r), np.cos(ang_c), np.sin(ang_c)
    zero = np.zeros_like(sr)
    cos = np.concatenate([cr, cr, cc, cc], axis=1)
    s_lo = np.concatenate([zero, sr, zero, sc], axis=1)
    s_hi = np.concatenate([-sr, zero, -sc, zero], axis=1)

    def full(t, ctx_value):
        return jnp.asarray(np.concatenate([np.tile(t, (1, 2)), np.full((rt.tm, 128), ctx_value)], axis=0), F32)

    return full(cos, 1.0), full(s_lo, 0.0), full(s_hi, 0.0)


def _table_spec(rt):
    return pl.BlockSpec((rt.tm, 128), lambda i: (jnp.where(i < rt.n_lat_tiles, i % rt.tiles_per_ex, rt.tiles_per_ex), 0))


def _head_mean(x):
    r = lax.broadcasted_iota(jnp.int32, (128, 128), 0) // HEAD_DIM
    c = lax.broadcasted_iota(jnp.int32, (128, 128), 1) // HEAD_DIM
    ones = jnp.where(r == c, 1.0 / HEAD_DIM, 0.0).astype(F32)
    return jnp.dot(x, ones, preferred_element_type=F32, precision=lax.Precision.HIGH)


def _head_stats(t):
    return lax.rsqrt(_head_mean(t * t) + EPS)


def _prep_fwd_body(tm, qkv_ref, c, s1, s2, qn, kn, out_ref):
    def rope(t):
        return t * c + pltpu.roll(t, 16, 1) * s1 + pltpu.roll(t, 112, 1) * s2

    for j in range(12):
        t = qkv_ref[:, j * 128:(j + 1) * 128]
        if j < 4:
            t = rope(t * _head_stats(t) * qn) * Q_SCALE
        elif j == COL_KA:
            t = rope(t * _head_stats(t) * kn)
        elif 6 <= j < 10:
            t = rope(t) * Q_SCALE
        elif j == COL_KB:
            t = rope(t)
        out_ref[:, j * 128:(j + 1) * 128] = t.astype(BF16)


def _prep_bwd_body(dq, dkv, qkv_ref, c, s1, s2, qn, kn, out_ref):
    rows = slice(None)

    def rope_bwd(d):
        return d * c + pltpu.roll(d * s1, 112, 1) + pltpu.roll(d * s2, 16, 1)

    def norm_bwd(t, g, dy):
        rinv = _head_stats(t)
        n = t * rinv
        dn = dy * g
        return rinv * (dn - n * _head_mean(dn * n)), jnp.sum(dy * n, axis=0, keepdims=True)

    dqn = jnp.zeros((1, 128), F32)
    dkn = jnp.zeros((1, 128), F32)
    for j in range(12):
        if j < 4:
            d, dg = norm_bwd(qkv_ref[rows, j * 128:(j + 1) * 128], qn, rope_bwd(dq(slice(j * 128, (j + 1) * 128)) * Q_SCALE))
            dqn = dqn + dg
        elif j == COL_KA:
            d, dg = norm_bwd(qkv_ref[rows, j * 128:(j + 1) * 128], kn, rope_bwd(dkv(slice(0, 128))))
            dkn = dkn + dg
        elif j == COL_VA:
            d = dkv(slice(128, 256))
        elif j < 10:
            d = rope_bwd(dq(slice((j - 2) * 128, (j - 1) * 128)) * Q_SCALE)
        elif j == COL_KB:
            d = rope_bwd(dkv(slice(256, 384)))
        else:
            d = dkv(slice(384, 512))
        out_ref[rows, j * 128:(j + 1) * 128] = d.astype(BF16)
    return dqn, dkn


def _in_fwd(rt, h, gamma, mod, wg, tables, qn, kn, name):
    w_specs, w_args = _in_weight_operands(wg)
    n_w = len(w_args)
    joined = not isinstance(h, (tuple, list))
    n_h = 1 if joined else 2

    def body(*refs):
        g_ref, mod_ref = refs[n_h:n_h + 2]
        rest = refs[n_h + 2:]
        c_ref, s1_ref, s2_ref, qn_ref, kn_ref, u_ref, qkn_ref, qkvp_ref = rest[n_w:n_w + 8]
        qkv_ref, w_scr = rest[-2:]
        i = pl.program_id(0)

        @pl.when(i == 0)
        def _():
            _unpack_in_pieces(rest[0], rest[1] if n_w == 2 else None, w_scr)

        if joined:
            h_ = refs[0][...]
        else:
            h_ = jnp.where(i < rt.n_lat_tiles, refs[0][...], refs[1][...])
            rest[n_w + 8][...] = h_
        u = _norm_mod_val(h_, g_ref[...], mod_ref, 0, 1).astype(BF16)
        u_ref[...] = u
        qkv_ref[...] = jnp.dot(u, w_scr[...], preferred_element_type=F32)
        qkn_ref[...] = qkv_ref[:, 0:NORMED_COLS]
        _prep_fwd_body(rt.tm, qkv_ref, c_ref[...], s1_ref[...], s2_ref[...], qn_ref[...], kn_ref[...], qkvp_ref)

    if joined:
        h_specs, h_args = [_row_spec(rt, D_MODEL)], [h]
    else:
        h_specs = [pl.BlockSpec((rt.tm, D_MODEL), lambda i: (jnp.minimum(i, rt.n_lat_tiles - 1), 0)),
                   pl.BlockSpec((rt.tm, D_MODEL), lambda i: (jnp.maximum(i - rt.n_lat_tiles, 0), 0))]
        h_args = list(h)
    out_specs = [_row_spec(rt, D_MODEL), _row_spec(rt, NORMED_COLS), _row_spec(rt, IN_COLS)]
    out_shape = [jax.ShapeDtypeStruct((rt.rows, D_MODEL), BF16), jax.ShapeDtypeStruct((rt.rows, NORMED_COLS), F32),
                 jax.ShapeDtypeStruct((rt.rows, IN_COLS), BF16)]
    if not joined:
        out_specs.append(_row_spec(rt, D_MODEL))
        out_shape.append(jax.ShapeDtypeStruct((rt.rows, D_MODEL), F32))
    return pl.pallas_call(
        body, name=name, grid=(rt.n_tiles,),
        in_specs=h_specs + [_vec_spec(D_MODEL), _mod_spec(rt)] + w_specs + [_table_spec(rt)] * 3 + [_vec_spec(128)] * 2,
        out_specs=out_specs, out_shape=out_shape,
        scratch_shapes=[pltpu.VMEM((rt.tm, IN_COLS), F32), pltpu.VMEM((D_MODEL, IN_COLS), BF16)],
        compiler_params=_params(("arbitrary",)),
    )(*h_args, gamma, mod, *w_args, *tables, qn, kn)


def _in_bwd(rt, dq, dkv, qkv, tables, qn, kn, wg, h, dres, mod, gamma, latent_only, name, comm=None, dead_ctx_dkv=None):
    last = rt.n_lat_tiles - 1
    w_specs, w_args = _in_weight_operands(wg)
    n_w = len(w_args)
    n_dead = 0 if dead_ctx_dkv is None else 1

    def body(dq_ref, dkv_ref, qkv_ref, c_ref, s1_ref, s2_ref, qn_ref, kn_ref, *rest):
        h_ref, dres_ref, mod_ref, g_ref, dqkv_ref, dh_ref, dqn_ref, dkn_ref, dsh_ref, dsc_ref, dg_ref, w_scr = rest[n_w + n_dead:]
        i = pl.program_id(0)

        @pl.when(i == 0)
        def _():
            _unpack_in_pieces(rest[0], rest[1] if n_w == 2 else None, w_scr)

        if n_dead:
            c1_ref, lat = rest[n_w], i <= last
            load_dq = lambda cols: jnp.where(lat, dq_ref[:, cols], 0.0)
            load_dkv = lambda cols: jnp.where(lat, dkv_ref[:, cols], c1_ref[:, cols])
            dres_ = jnp.where(lat, dres_ref[...], 0.0)
        else:
            load_dq, load_dkv, dres_ = (lambda cols: dq_ref[:, cols]), (lambda cols: dkv_ref[:, cols]), dres_ref[...]
        dqn, dkn = _prep_bwd_body(load_dq, load_dkv, qkv_ref, c_ref[...], s1_ref[...], s2_ref[...], qn_ref[...], kn_ref[...], dqkv_ref)
        du = lax.dot_general(dqkv_ref[...], w_scr[...], NT, preferred_element_type=F32)
        dh, dsh, dsc, dg = _norm_mod_bwd_val(du, h_ref[...], g_ref[...], 1.0 + mod_ref[0, 1:2, :])
        if latent_only:
            @pl.when(i <= last)
            def _():
                dh_ref[...] = dres_ + dh
        else:
            dh_ref[...] = dres_ + dh
        _accumulate(rt, i, [(dsh_ref, dsh), (dsc_ref, dsc)], [(dg_ref, dg), (dqn_ref, dqn), (dkn_ref, dkn)])

    dh_spec = pl.BlockSpec((rt.tm, D_MODEL), lambda i: (jnp.minimum(i, last), 0)) if latent_only else _row_spec(rt, D_MODEL)
    dead_specs = [] if dead_ctx_dkv is None else [pl.BlockSpec((rt.tm, 512), lambda i: (jnp.maximum(i - rt.n_lat_tiles, 0), 0))]
    dead_args = [] if dead_ctx_dkv is None else [dead_ctx_dkv]
    return _comm_call(
        body, comm, name=name, grid=(rt.n_tiles,),
        in_specs=[_row_spec(rt, 1024), _row_spec(rt, 512), _row_spec(rt, NORMED_COLS)] + [_table_spec(rt)] * 3 + [_vec_spec(128)] * 2
        + w_specs + dead_specs + [_row_spec(rt, D_MODEL), _row_spec(rt, D_MODEL), _mod_spec(rt), _vec_spec(D_MODEL)],
        out_specs=[_row_spec(rt, IN_COLS), dh_spec, _vec_spec(128), _vec_spec(128),
                   _group_spec(rt), _group_spec(rt), _vec_spec(D_MODEL)],
        out_shape=[jax.ShapeDtypeStruct((rt.rows, IN_COLS), BF16),
                   jax.ShapeDtypeStruct((rt.n_lat if latent_only else rt.rows, D_MODEL), F32),
                   _vec_shape(128), _vec_shape(128), _group_shape(rt), _group_shape(rt), _vec_shape()],
        args=[dq, dkv, qkv, *tables, qn, kn, *w_args, *dead_args, h, dres, mod, gamma], aliases={}, semantics=("arbitrary",),
        scratch=[pltpu.VMEM((D_MODEL, IN_COLS), BF16)])


def _out_fwd(rt, o, wg, h, mod, g_post_mix, g_pre_mlp, name):
    def body(o_ref, w_ref, h_ref, mod_ref, gpost_ref, gpre_ref, mix_ref, h1_ref, u2_ref):
        mix = jnp.dot(o_ref[...], w_ref[...].reshape(D_MODEL, D_MODEL), preferred_element_type=F32)
        mix_ref[...] = mix
        h1 = _post_norm_val(h_ref[...], mix, gpost_ref[...], mod_ref, 2)
        h1_ref[...] = h1
        u2_ref[...] = _norm_mod_val(h1, gpre_ref[...], mod_ref, 3, 4).astype(BF16)

    return pl.pallas_call(
        body, name=name, grid=(rt.n_tiles,),
        in_specs=[_row_spec(rt, D_MODEL), _gathered_spec(wg, "out"), _row_spec(rt, D_MODEL), _mod_spec(rt),
                  _vec_spec(D_MODEL), _vec_spec(D_MODEL)],
        out_specs=[_row_spec(rt, D_MODEL)] * 3,
        out_shape=[jax.ShapeDtypeStruct((rt.rows, D_MODEL), F32), jax.ShapeDtypeStruct((rt.rows, D_MODEL), F32),
                   jax.ShapeDtypeStruct((rt.rows, D_MODEL), BF16)],
        compiler_params=_params(("parallel",)),
    )(o, wg["out"][0], h, mod, g_post_mix, g_pre_mlp)


def _out_bwd(rt, dh1, mix, wg, mod, g_post_mix, name, comm=None):
    def body(dh_ref, mix_ref, w_ref, mod_ref, g_ref, dmix_ref, do_ref, dgate_ref, dg_ref):
        i = pl.program_id(0)
        dz, dgate, dg = _post_norm_bwd_val(dh_ref[...], mix_ref[...], g_ref[...], mod_ref[0, 2:3, :])
        dzb = dz.astype(BF16)
        dmix_ref[...] = dzb
        do_ref[...] = lax.dot_general(dzb, w_ref[...].reshape(D_MODEL, D_MODEL), NT, preferred_element_type=F32).astype(BF16)
        _accumulate(rt, i, [(dgate_ref, dgate)], [(dg_ref, dg)])

    return _comm_call(
        body, comm, name=name, grid=(rt.n_tiles,),
        in_specs=[_row_spec(rt, D_MODEL), _row_spec(rt, D_MODEL), _gathered_spec(wg, "out"), _mod_spec(rt), _vec_spec(D_MODEL)],
        out_specs=[_row_spec(rt, D_MODEL), _row_spec(rt, D_MODEL), _group_spec(rt), _vec_spec(D_MODEL)],
        out_shape=[jax.ShapeDtypeStruct((rt.rows, D_MODEL), BF16), jax.ShapeDtypeStruct((rt.rows, D_MODEL), BF16),
                   _group_shape(rt), _vec_shape()],
        args=[dh1, mix, wg["out"][0], mod, g_post_mix], aliases={}, semantics=("arbitrary",))


def _w_chunk(w_ref, k):
    return w_ref[2 * k:2 * k + 2].reshape(1024, 1024)


def _mlp_fwd(rt, u2, h1, wg, mod, g_post_mlp, name, comm=None, target=None):
    last = rt.n_lat_tiles - 1

    def body(u2_ref, h1_ref, wu_ref, wd_ref, mod_ref, g_ref, *rest):
        u2_ = u2_ref[...]
        y = jnp.zeros((rt.tm, D_MODEL), F32)
        for k in range(D_FF // 1024):
            a = jnp.maximum(jnp.dot(u2_, _w_chunk(wu_ref, k), preferred_element_type=F32), 0.0)
            rest[-3 if target is None else -4][:, k * 1024:(k + 1) * 1024] = a.astype(BF16)
            y = y + jnp.dot((a * a).astype(BF16), _w_chunk(wd_ref, k), preferred_element_type=F32)
        h2 = _post_norm_val(h1_ref[...], y, g_ref[...], mod_ref, 5)
        if target is None:
            _, y_ref, h2_ref = rest
            y_ref[...] = y
            h2_ref[...] = h2
        else:
            t_ref, _, y_ref, dh_ref, sq_ref = rest
            y_ref[...] = y
            i = pl.program_id(0)

            @pl.when(i == 0)
            def _():
                sq_ref[...] = jnp.zeros_like(sq_ref)

            @pl.when(i <= last)
            def _():
                e = h2 - t_ref[...]
                dh_ref[...] = e * (1.0 / D_MODEL)
                sq_ref[...] += jnp.sum(e * e, axis=0, keepdims=True)

            @pl.when(i > last)
            def _():
                dh_ref[...] = jnp.zeros_like(dh_ref)

    in_specs = [_row_spec(rt, D_MODEL), _row_spec(rt, D_MODEL), _gathered_spec(wg, "up"), _gathered_spec(wg, "down"),
                _mod_spec(rt), _vec_spec(D_MODEL)]
    args = [u2, h1, wg["up"][0], wg["down"][0], mod, g_post_mlp]
    out_specs = [_row_spec(rt, D_FF), _row_spec(rt, D_MODEL), _row_spec(rt, D_MODEL)]
    out_shape = [jax.ShapeDtypeStruct((rt.rows, D_FF), BF16), jax.ShapeDtypeStruct((rt.rows, D_MODEL), F32),
                 jax.ShapeDtypeStruct((rt.rows, D_MODEL), F32)]
    if target is not None:
        in_specs.append(pl.BlockSpec((rt.tm, D_MODEL), lambda i: (jnp.minimum(i, last), 0)))
        args.append(target)
        out_specs.append(_vec_spec(D_MODEL))
        out_shape.append(_vec_shape())
    return _comm_call(body, comm, name=name, grid=(rt.n_tiles,), in_specs=in_specs, out_specs=out_specs, out_shape=out_shape,
                      args=args, aliases={}, semantics=("parallel",) if target is None else ("arbitrary",))


def _mlp_down_bwd(rt, dh, y, ra, wg, mod, g_post_mlp, name):
    def body(dh_ref, y_ref, ra_ref, wd_ref, mod_ref, g_ref, dy_ref, da_ref, dgate_ref, dg_ref):
        i = pl.program_id(0)
        dz, dgate, dg = _post_norm_bwd_val(dh_ref[...], y_ref[...], g_ref[...], mod_ref[0, 5:6, :])
        dyb = dz.astype(BF16)
        dy_ref[...] = dyb
        for k in range(D_FF // 1024):
            dr = lax.dot_general(dyb, _w_chunk(wd_ref, k), NT, preferred_element_type=F32)
            da_ref[:, k * 1024:(k + 1) * 1024] = (dr * (2.0 * ra_ref[:, k * 1024:(k + 1) * 1024].astype(F32))).astype(BF16)
        _accumulate(rt, i, [(dgate_ref, dgate)], [(dg_ref, dg)])

    return pl.pallas_call(
        body, name=name, grid=(rt.n_tiles,),
        in_specs=[_row_spec(rt, D_MODEL), _row_spec(rt, D_MODEL), _row_spec(rt, D_FF), _gathered_spec(wg, "down"),
                  _mod_spec(rt), _vec_spec(D_MODEL)],
        out_specs=[_row_spec(rt, D_MODEL), _row_spec(rt, D_FF), _group_spec(rt), _vec_spec(D_MODEL)],
        out_shape=[jax.ShapeDtypeStruct((rt.rows, D_MODEL), BF16), jax.ShapeDtypeStruct((rt.rows, D_FF), BF16),
                   _group_shape(rt), _vec_shape()],
        compiler_params=_params(("arbitrary",)),
    )(dh, y, ra, wg["down"][0], mod, g_post_mlp)


def _mlp_up_bwd(rt, da, wg, h1, dh, mod, g_pre_mlp, name):
    def body(da_ref, wu_ref, h1_ref, dh_ref, mod_ref, g_ref, dh1_ref, dsh_ref, dsc_ref, dg_ref):
        i = pl.program_id(0)
        du = jnp.zeros((rt.tm, D_MODEL), F32)
        for k in range(D_FF // 1024):
            du = du + lax.dot_general(da_ref[:, k * 1024:(k + 1) * 1024], _w_chunk(wu_ref, k), NT, preferred_element_type=F32)
        d, dsh, dsc, dg = _norm_mod_bwd_val(du, h1_ref[...], g_ref[...], 1.0 + mod_ref[0, 4:5, :])
        dh1_ref[...] = dh_ref[...] + d
        _accumulate(rt, i, [(dsh_ref, dsh), (dsc_ref, dsc)], [(dg_ref, dg)])

    return pl.pallas_call(
        body, name=name, grid=(rt.n_tiles,),
        in_specs=[_row_spec(rt, D_FF), _gathered_spec(wg, "up"), _row_spec(rt, D_MODEL), _row_spec(rt, D_MODEL),
                  _mod_spec(rt), _vec_spec(D_MODEL)],
        out_specs=[_row_spec(rt, D_MODEL), _group_spec(rt), _group_spec(rt), _vec_spec(D_MODEL)],
        out_shape=[jax.ShapeDtypeStruct((rt.rows, D_MODEL), F32), _group_shape(rt), _group_shape(rt), _vec_shape()],
        compiler_params=_params(("arbitrary",)),
    )(da, wg["up"][0], h1, dh, mod, g_pre_mlp)


def _wgrad_packed(rt, a, b, kind, off, n_rows, p_prev, name, comm=None):
    h = PACK_HEIGHT[kind]
    tk = rt.tm
    assert off % h == 0, (kind, off)

    def body(a_ref, b_ref, *rest):
        o_ref = rest[-1]
        i = pl.program_id(0)

        @pl.when(i == 0)
        def _():
            o_ref[...] = jnp.zeros_like(o_ref)

        if kind == "in":
            res = lax.dot_general(a_ref[...], b_ref[...], TN, preferred_element_type=F32)
            for k in range(4):
                for c in range(2):
                    for t in range(2):
                        o_ref[c, k, :, t * IN_PIECE_COLS:(t + 1) * IN_PIECE_COLS] += \
                            res[c * 512 + t * h:c * 512 + (t + 1) * h, k * IN_PIECE_COLS:(k + 1) * IN_PIECE_COLS]
        elif kind == "out":
            res = lax.dot_general(a_ref[...], b_ref[...], TN, preferred_element_type=F32)
            for k in range(4):
                for c in range(2):
                    o_ref[c, k] += res[(2 * k + c) * h:(2 * k + c + 1) * h]
        else:
            for k in range(4):
                if kind == "up":
                    res = lax.dot_general(a_ref[...], b_ref[:, k * 1024:(k + 1) * 1024], TN, preferred_element_type=F32)
                else:
                    ra = a_ref[:, k * 1024:(k + 1) * 1024].astype(F32)
                    res = lax.dot_general((ra * ra).astype(BF16), b_ref[...], TN, preferred_element_type=F32)
                o_ref[0, k] += res[0:h]
                o_ref[1, k] += res[h:2 * h]

    in_specs = [pl.BlockSpec((tk, a.shape[1]), lambda i: (i, 0)), pl.BlockSpec((tk, b.shape[1]), lambda i: (i, 0))]
    args = [a, b]
    aliases = {}
    if p_prev is not None:
        in_specs.append(pl.BlockSpec(memory_space=pl.ANY))
        args.append(p_prev)
        aliases = {2: 0}
    outs = _comm_call(
        body, comm, name=name, grid=(rt.n_tiles,),
        in_specs=in_specs,
        out_specs=[pl.BlockSpec((2, 4, h, 1024), lambda i: (0, 0, off // h, 0))],
        out_shape=[jax.ShapeDtypeStruct((2, 4, n_rows, 1024), F32)],
        args=args, aliases=aliases, semantics=("arbitrary",))
    return outs[0] if comm is None else outs


def _ada_wgrad(xs, dm, name):
    depth, _, cols = dm.shape

    def body(x_ref, d_ref, o_ref):
        for l in range(depth):
            o_ref[l] = lax.dot_general(x_ref[...], d_ref[l], TN, preferred_element_type=F32)

    return pl.pallas_call(body, name=name, out_shape=jax.ShapeDtypeStruct((depth, xs.shape[1], cols), F32),
                          compiler_params=pltpu.CompilerParams(vmem_limit_bytes=VMEM_LIMIT))(xs, dm)


def _stack_heads(x, kvi):
    x = x.astype(F32)
    tq = x.shape[0]
    lane = lax.broadcasted_iota(jnp.int32, (tq, 128), 1)
    keep = lane < HEAD_DIM if kvi == 0 else lane >= HEAD_DIM
    parts = []
    for p in range(2):
        pair = x[:, p * 128:(p + 1) * 128]
        swapped = pltpu.roll(pair, HEAD_DIM, 1)
        lo_head, hi_head = (pair, swapped) if kvi == 0 else (swapped, pair)
        parts += [jnp.where(keep, lo_head, 0.0), jnp.where(keep, hi_head, 0.0)]
    return jnp.concatenate(parts, axis=0).astype(BF16)


def _unstack_heads(o4, kvi):
    tq = o4.shape[0] // GROUP
    lane = lax.broadcasted_iota(jnp.int32, (tq, 128), 1)
    outs = []
    for p in range(2):
        r_lo, r_hi = o4[(2 * p) * tq:(2 * p + 1) * tq], o4[(2 * p + 1) * tq:(2 * p + 2) * tq]
        if kvi == 0:
            lo, hi = r_lo, pltpu.roll(r_hi, HEAD_DIM, 1)
        else:
            lo, hi = pltpu.roll(r_lo, HEAD_DIM, 1), r_hi
        outs.append(jnp.where(lane < HEAD_DIM, lo, hi))
    return jnp.concatenate(outs, axis=1)


def _per_head(shape, axis, tq, values):
    head = lax.broadcasted_iota(jnp.int32, shape, axis) // tq
    out = jnp.zeros(shape, F32)
    for g in range(GROUP):
        out = jnp.where(head == g, values[g], out)
    return out


KEY_CHUNK = 512
Q_TILE = 128
Q_TILE_FWD = 256


def _key_chunks(k_ref, v_ref, n, kc=KEY_CHUNK):
    kc = min(kc, n)
    return [(k_ref[c * kc:(c + 1) * kc, :], v_ref[c * kc:(c + 1) * kc, :], None) for c in range(n // kc)]


def _softmax_fwd(qs, chunks, sink_col):
    logits = []
    for k, _, mask in chunks:
        s = lax.dot_general(qs, k, NT, preferred_element_type=F32)
        logits.append(s if mask is None else jnp.where(mask, s, NEG_BIG))
    m = functools.reduce(jnp.maximum, [jnp.max(s, axis=1, keepdims=True) for s in logits])
    if sink_col is not None:
        m = jnp.maximum(m, sink_col)
    l = jnp.zeros_like(m) if sink_col is None else jnp.exp(sink_col - m)
    acc = jnp.zeros((qs.shape[0], 128), F32)
    for s, (_, v, _) in zip(logits, chunks):
        p = jnp.exp(s - m)
        l = l + jnp.sum(p, axis=1, keepdims=True)
        acc = acc + jnp.dot(p.astype(BF16), v, preferred_element_type=F32)
    return acc / l, m + jnp.log(l)


def _to_rows(col):
    return jnp.transpose(jnp.broadcast_to(col, (col.shape[0], 128)))[0:8, :]


def _softmax_bwd(qs, dos, lse_row, delta_row, chunks):
    dq = jnp.zeros((qs.shape[0], 128), F32)
    grads = []
    for k, v, mask in chunks:
        s = lax.dot_general(k, qs, NT, preferred_element_type=F32)
        if mask is not None:
            s = jnp.where(mask, s, NEG_BIG)
        p = jnp.exp(s - lse_row)
        dp = lax.dot_general(v, dos, NT, preferred_element_type=F32)
        ds = (p * (dp - delta_row)).astype(BF16)
        dv = jnp.dot(p.astype(BF16), dos, preferred_element_type=F32)
        dk = jnp.dot(ds, qs, preferred_element_type=F32)
        dq = dq + lax.dot_general(ds, k, TN, preferred_element_type=F32)
        grads.append((dk, dv))
    return dq, grads


def _band(qi, tq, seq):
    span = tq + 2 * WINDOW
    start = pl.multiple_of(jnp.clip(qi * tq - WINDOW, 0, seq - span), 64)
    return start, span


def _band_mask(qi, tq, start, span, query_axis):
    shape = (GROUP * tq, span) if query_axis == 0 else (span, GROUP * tq)
    qpos = qi * tq + lax.broadcasted_iota(jnp.int32, shape, query_axis) % tq
    kpos = start + lax.broadcasted_iota(jnp.int32, shape, 1 - query_axis)
    return jnp.abs(kpos - qpos) <= WINDOW


def _qkv_specs(rt, tq, q_row, ctx_row, with_latent):
    specs = [pl.BlockSpec((tq, 256), functools.partial(lambda b, i, col: (q_row(b, i), col), col=col)) for col in (0, 1, 3, 4)]
    if with_latent:
        specs += [pl.BlockSpec((rt.seq, 128), functools.partial(lambda b, i, col: (b, col), col=col))
                  for col in (COL_KA, COL_VA, COL_KB, COL_VB)]
    specs += [pl.BlockSpec((rt.ctx, 128), functools.partial(lambda b, i, col: (ctx_row(b), col), col=col))
              for col in (COL_KA, COL_VA, COL_KB, COL_VB)]
    return specs


def _attn_fwd(rt, qkvp, sink, o_prev, name, comm=None):
    latent = o_prev is None
    seq, ctx, nb = rt.seq, rt.ctx, rt.nb
    tq = Q_TILE_FWD if latent else ctx
    tile = Q_TILE if latent else ctx
    parts = tq // tile
    nq = seq // tq if latent else 1
    ctx_blk0 = rt.n_lat // ctx
    q_row = (lambda b, i: b * nq + i) if latent else (lambda b, i: ctx_blk0 + b)

    def store_lse(lse_ref, j, lse_col):
        rows = _to_rows(lse_col)
        for part in range(parts):
            lse_ref[part, j] = jnp.concatenate([rows[:, g * tq + part * tile:g * tq + (part + 1) * tile] for g in range(GROUP)], axis=1)

    def body(sink_ref, qa0, qa1, qb0, qb1, *rest):
        if latent:
            kal, val, kbl, vbl, kac, vac, kbc, vbc, o_ref, lse_ref = rest
        else:
            kac, vac, kbc, vbc, _, o_ref, lse_ref = rest
        qi = pl.program_id(1)
        for kvi, (qa, qb) in enumerate(((qa0, qb0), (qa1, qb1))):
            src_a = _key_chunks(kac, vac, ctx)
            src_b = _key_chunks(kbc, vbc, ctx)
            if latent:
                src_a += _key_chunks(kal, val, seq, seq)
                start, span = _band(qi, tq, seq)
                src_b.append((kbl[pl.ds(start, span), :], vbl[pl.ds(start, span), :], _band_mask(qi, tq, start, span, 0)))
            oa, lse = _softmax_fwd(_stack_heads(qa[...], kvi), src_a, None)
            o_ref[:, kvi * 256:(kvi + 1) * 256] = _unstack_heads(oa, kvi).astype(BF16)
            store_lse(lse_ref, kvi, lse)
            sink_col = _per_head((GROUP * tq, 1), 0, tq, [sink_ref[kvi * GROUP + g] for g in range(GROUP)])
            ob, lse = _softmax_fwd(_stack_heads(qb[...], kvi), src_b, sink_col)
            o_ref[:, 512 + kvi * 256:512 + (kvi + 1) * 256] = _unstack_heads(ob, kvi).astype(BF16)
            store_lse(lse_ref, 2 + kvi, lse)

    specs = _qkv_specs(rt, tq, q_row, lambda b: ctx_blk0 + b, latent)
    args = [sink] + [qkvp] * len(specs)
    in_specs = [pl.BlockSpec(memory_space=pltpu.SMEM)] + specs
    aliases = {}
    if not latent:
        in_specs.append(pl.BlockSpec(memory_space=pl.ANY))
        args.append(o_prev)
        aliases = {len(args) - 1: 0}
    return _comm_call(
        body, comm, name=name, grid=(nb, nq),
        in_specs=in_specs,
        out_specs=[pl.BlockSpec((tq, 1024), lambda b, i: (q_row(b, i), 0)),
                   pl.BlockSpec((parts, 4, 8, GROUP * tile), lambda b, i: (b * nq + i, 0, 0, 0))],
        out_shape=[jax.ShapeDtypeStruct((rt.rows, 1024), BF16), jax.ShapeDtypeStruct((nb * nq * parts, 4, 8, GROUP * tile), F32)],
        args=args, aliases=aliases, semantics=("parallel", "parallel"))


def _attn_bwd(rt, qkvp, o, lse, do, sink, prev, name, comm=None):
    latent = prev is None
    seq, ctx, nb = rt.seq, rt.ctx, rt.nb
    tq = Q_TILE if latent else ctx
    nq = seq // tq if latent else 1
    ctx_blk0 = rt.n_lat // ctx
    q_row = (lambda b, i: b * nq + i) if latent else (lambda b, i: ctx_blk0 + b)
    kc = min(KEY_CHUNK, seq)

    def body(sink_ref, qa0, qa1, qb0, qb1, *rest):
        if latent:
            kal, val, kbl, vbl, kac, vac, kbc, vbc, do_ref, o_ref, lse_ref, dq_ref, dl_ref, dc_ref, dsink_ref = rest
        else:
            kac, vac, kbc, vbc, do_ref, o_ref, lse_ref, c1_ref, _, _, dq_ref, dc_ref, dsink_ref = rest
        b, qi = pl.program_id(0), pl.program_id(1)

        def rows_of(cols, kvi, mixer):
            dos = _stack_heads(do_ref[:, cols], kvi)
            delta = jnp.sum(dos.astype(F32) * _stack_heads(o_ref[:, cols], kvi).astype(F32), axis=1, keepdims=True)
            return dos, lse_ref[0, 2 * mixer + kvi, 0:1, :], _to_rows(delta)[0:1, :]

        @pl.when(jnp.logical_and(b == 0, qi == 0))
        def _():
            dsink_ref[...] = jnp.zeros_like(dsink_ref)

        if latent:
            @pl.when(qi == 0)
            def _():
                dc_ref[...] = jnp.zeros_like(dc_ref)
                dl_ref[...] = jnp.zeros_like(dl_ref)
        else:
            dc_ref[...] = c1_ref[...]

        head_row = lax.broadcasted_iota(jnp.int32, (8, 128), 0)
        for kvi, (qa, qb) in enumerate(((qa0, qb0), (qa1, qb1))):
            cols = slice(kvi * 256, (kvi + 1) * 256)
            dos, lse_row, delta_row = rows_of(cols, kvi, 0)
            src = _key_chunks(kac, vac, ctx)
            if latent:
                src += _key_chunks(kal, val, seq)
            dq4, grads = _softmax_bwd(_stack_heads(qa[...], kvi), dos, lse_row, delta_row, src)
            dq_ref[:, cols] = _unstack_heads(dq4, kvi)
            dc_ref[:, 0:128] += grads[0][0]
            dc_ref[:, 128:256] += grads[0][1]
            for c, (dk, dv) in enumerate(grads[1:]):
                dl_ref[c * kc:(c + 1) * kc, 0:128] += dk
                dl_ref[c * kc:(c + 1) * kc, 128:256] += dv
            cols = slice(512 + kvi * 256, 512 + (kvi + 1) * 256)
            dos, lse_row, delta_row = rows_of(cols, kvi, 1)
            src = _key_chunks(kbc, vbc, ctx)
            if latent:
                start, span = _band(qi, tq, seq)
                src.append((kbl[pl.ds(start, span), :], vbl[pl.ds(start, span), :], _band_mask(qi, tq, start, span, 1)))
            dq4, grads = _softmax_bwd(_stack_heads(qb[...], kvi), dos, lse_row, delta_row, src)
            dq_ref[:, cols] = _unstack_heads(dq4, kvi)
            dc_ref[:, 256:384] += grads[0][0]
            dc_ref[:, 384:512] += grads[0][1]
            if latent:
                dl_ref[pl.ds(start, span), 256:384] += grads[1][0]
                dl_ref[pl.ds(start, span), 384:512] += grads[1][1]
            sink_row = _per_head((1, GROUP * tq), 1, tq, [sink_ref[kvi * GROUP + g] for g in range(GROUP)])
            dsink = -jnp.exp(sink_row - lse_row) * delta_row
            head = lax.broadcasted_iota(jnp.int32, (1, GROUP * tq), 1) // tq
            upd = jnp.zeros((8, 128), F32)
            for g in range(GROUP):
                upd = jnp.where(head_row == kvi * GROUP + g, jnp.sum(jnp.where(head == g, dsink, 0.0)), upd)
            dsink_ref[...] += upd

    specs = _qkv_specs(rt, tq, q_row, lambda b: ctx_blk0 + b, latent)
    q_rows_spec = pl.BlockSpec((tq, 1024), lambda b, i: (q_row(b, i), 0))
    in_specs = ([pl.BlockSpec(memory_space=pltpu.SMEM)] + specs
                + [q_rows_spec, q_rows_spec, pl.BlockSpec((1, 4, 8, GROUP * tq), lambda b, i: (b * nq + i, 0, 0, 0))])
    args = [sink] + [qkvp] * len(specs) + [do, o, lse]
    dq_shape = jax.ShapeDtypeStruct((rt.rows, 1024), F32)
    dkv_shape = jax.ShapeDtypeStruct((rt.rows, 512), F32)
    dsink_spec, dsink_shape = pl.BlockSpec((8, 128), lambda b, i: (0, 0)), jax.ShapeDtypeStruct((8, 128), F32)
    dq_spec = pl.BlockSpec((tq, 1024), lambda b, i: (q_row(b, i), 0))
    if latent:
        out_specs = [dq_spec, pl.BlockSpec((seq, 512), lambda b, i: (b, 0)), pl.BlockSpec((ctx, 512), lambda b, i: (b, 0)), dsink_spec]
        out_shape = [dq_shape, dkv_shape, jax.ShapeDtypeStruct((rt.n_ctx, 512), F32), dsink_shape]
        aliases = {}
    else:
        dq_prev, dkv_prev, c1 = prev
        in_specs += [pl.BlockSpec((ctx, 512), lambda b, i: (b, 0)), pl.BlockSpec(memory_space=pl.ANY), pl.BlockSpec(memory_space=pl.ANY)]
        args += [c1, dq_prev, dkv_prev]
        out_specs = [dq_spec, pl.BlockSpec((ctx, 512), lambda b, i: (ctx_blk0 + b, 0)), dsink_spec]
        out_shape = [dq_shape, dkv_shape, dsink_shape]
        aliases = {len(args) - 2: 0, len(args) - 1: 1}
    return _comm_call(body, comm, name=name, grid=(nb, nq), in_specs=in_specs, out_specs=out_specs, out_shape=out_shape,
                      args=args, aliases=aliases, semantics=("arbitrary", "arbitrary"))


def _silu(x):
    return x / (1.0 + jnp.exp(-x))


def _whole(shape):
    return pl.BlockSpec(shape, lambda i, s: (0,) * len(shape))


def _ada_half_spec(cols):
    return pl.BlockSpec((DEPTH, D_MODEL, cols), lambda i, s: (0, 0, s[0]))


def _ada_fwd(cond, w_ada, b_half, c_idx, name):
    rows = cond.shape[0]
    cols = w_ada.shape[2] // 2

    def body(s_ref, c_ref, w_ref, b_ref, x_ref, o_ref):
        xs = _silu(c_ref[...]).astype(BF16)
        x_ref[...] = xs
        for l in range(DEPTH):
            o_ref[l] = jnp.dot(xs, w_ref[l].astype(BF16), preferred_element_type=F32) + b_ref[l]

    grid_spec = pltpu.PrefetchScalarGridSpec(
        num_scalar_prefetch=1, grid=(1,),
        in_specs=[_whole(cond.shape), _ada_half_spec(cols), _whole(b_half.shape)],
        out_specs=[_whole((rows, D_MODEL)), _whole((DEPTH, rows, cols))])
    return pl.pallas_call(
        body, name=name, grid_spec=grid_spec,
        out_shape=[jax.ShapeDtypeStruct((rows, D_MODEL), BF16), jax.ShapeDtypeStruct((DEPTH, rows, cols), F32)],
        compiler_params=_params(("arbitrary",)),
    )(c_idx, cond, w_ada, b_half)


def _ada_cond_bwd(dcx, w_ada, c_idx, name):
    _, rows, cols = dcx.shape

    def body(s_ref, d_ref, w_ref, o_ref):
        acc = jnp.zeros((rows, D_MODEL), F32)
        for l in range(DEPTH):
            acc = acc + lax.dot_general(d_ref[l], w_ref[l].astype(BF16), NT, preferred_element_type=F32)
        o_ref[...] = acc

    grid_spec = pltpu.PrefetchScalarGridSpec(
        num_scalar_prefetch=1, grid=(1,),
        in_specs=[_whole(dcx.shape), _ada_half_spec(cols)], out_specs=_whole((rows, D_MODEL)))
    return pl.pallas_call(body, name=name, grid_spec=grid_spec, out_shape=jax.ShapeDtypeStruct((rows, D_MODEL), F32),
                          compiler_params=_params(("arbitrary",)))(c_idx, dcx, w_ada)


def _dev_sum(x, name):
    _, r, c = x.shape

    def body(x_ref, o_ref):
        v = x_ref[0]
        for d in range(1, N_DEV):
            v = v + x_ref[d]
        o_ref[...] = v

    return pl.pallas_call(body, name=name, out_shape=jax.ShapeDtypeStruct((r, c), F32))(x)


def _adam_val(w, g, m, v):
    c1 = 1.0 / (1.0 - ADAM_B1 ** ADAM_STEP)
    c2 = 1.0 / (1.0 - ADAM_B2 ** ADAM_STEP)
    nm = ADAM_B1 * m + (1.0 - ADAM_B1) * g
    nv = ADAM_B2 * v + (1.0 - ADAM_B2) * (g * g)
    return -ADAM_LR * ((nm * c1) / (jnp.sqrt(nv * c2) + ADAM_EPS) + ADAM_WD * w), nm, nv


def _small_update(tot, dcc_parts, params, n_groups, name):
    n_p = len(params)
    mod_rows = n_groups * N_MOD
    head_row = DEPTH * mod_rows + 4 * DEPTH

    def body(tot_ref, dcc_ref, *refs):
        ins, outs = refs[:3 * n_p], refs[3 * n_p:]

        def update(p, rows, cols, g):
            w_ref, m_ref, v_ref = ins[3 * p:3 * p + 3]
            g_ref, d_ref, nm_ref, nv_ref = outs[4 * p:4 * p + 4]
            d, nm, nv = _adam_val(w_ref[rows, cols], g, m_ref[rows, cols], v_ref[rows, cols])
            g_ref[rows, cols] = g
            d_ref[rows, cols] = d
            nm_ref[rows, cols] = nm
            nv_ref[rows, cols] = nv

        acc = dcc_ref[0, 0:1, :]
        for d in range(1, N_DEV):
            acc = acc + dcc_ref[d, 0:1, :]
        c = ins[0][...]
        sg = 1.0 / (1.0 + jnp.exp(-c))
        update(0, slice(0, 1), slice(None), acc * (sg * (1.0 + c * (1.0 - sg))))
        for l in range(DEPTH):
            for i in range(N_MOD):
                g = tot_ref[l * mod_rows + i:l * mod_rows + i + 1, :]
                for grp in range(1, n_groups):
                    g = g + tot_ref[l * mod_rows + grp * N_MOD + i:l * mod_rows + grp * N_MOD + i + 1, :]
                update(1, slice(l, l + 1), slice(i * D_MODEL, (i + 1) * D_MODEL), g)
            for j in range(4):
                row = DEPTH * mod_rows + 4 * l + j
                update(2 + j, slice(l, l + 1), slice(None), tot_ref[row:row + 1, :])
            head = tot_ref[head_row + l:head_row + l + 1, :]
            update(6, slice(l, l + 1), slice(None), head[:, 0:HEAD_DIM] + head[:, HEAD_DIM:2 * HEAD_DIM])
            update(7, slice(l, l + 1), slice(None), head[:, 2 * HEAD_DIM:3 * HEAD_DIM] + head[:, 3 * HEAD_DIM:4 * HEAD_DIM])
            update(8, slice(l, l + 1), slice(None), head[:, 4 * HEAD_DIM:4 * HEAD_DIM + ins[3 * 8].shape[1]])

    shapes = [jax.ShapeDtypeStruct(w.shape, F32) for w, _, _ in params for _ in range(4)]
    outs = pl.pallas_call(body, name=name, out_shape=shapes)(tot, dcc_parts, *[a for p in params for a in p])
    return [tuple(outs[4 * p:4 * p + 4]) for p in range(n_p)]


def _adamw(w, g, m, v, name):
    r, c = w.shape
    tr = _pick(r, (256, 128, 64, 32, 24, 16, 8))

    def body(w_ref, g_ref, m_ref, v_ref, d_ref, nm_ref, nv_ref):
        d_ref[...], nm_ref[...], nv_ref[...] = _adam_val(w_ref[...], g_ref[...], m_ref[...], v_ref[...])

    spec = pl.BlockSpec((tr, c), lambda i: (i, 0))
    return pl.pallas_call(body, name=name, grid=(r // tr,), in_specs=[spec] * 4, out_specs=[spec] * 3,
                          out_shape=[jax.ShapeDtypeStruct((r, c), F32)] * 3, compiler_params=_params(("parallel",)))(w, g, m, v)


def _adamw_shard(kind, l, w, m, v, halves, off, prev, name):
    h = PACK_HEIGHT[kind]
    assert off % h == 0, (kind, off)
    _, r, c = w.shape
    rows = r // 2

    def body(w_ref, m_ref, v_ref, p_ref, *rest):
        g_ref, d_ref, nm_ref, nv_ref = rest[-4:]
        if kind == "in":
            for t in range(2):
                g = p_ref[:, t * IN_PIECE_COLS:(t + 1) * IN_PIECE_COLS]
                rs = slice(t * h, (t + 1) * h)
                g_ref[rs, :] = g
                d_ref[rs, :], nm_ref[rs, :], nv_ref[rs, :] = _adam_val(w_ref[rs, :], g, m_ref[rs, :], v_ref[rs, :])
        else:
            g = p_ref[...]
            g_ref[...] = g
            d_ref[...], nm_ref[...], nv_ref[...] = _adam_val(w_ref[...], g, m_ref[...], v_ref[...])

    blk = pl.BlockSpec((None, rows, c), lambda half: (l, half, 0))
    in_specs = [blk, blk, blk, pl.BlockSpec((None, h, 1024), lambda half: (half, off // h, 0))]
    args = [w, m, v, halves]
    aliases = {}
    if prev is not None:
        in_specs += [pl.BlockSpec(memory_space=pl.ANY)] * 4
        args += list(prev)
        aliases = {4 + j: j for j in range(4)}
    return pl.pallas_call(
        body, name=name, grid=(2,), in_specs=in_specs, out_specs=[blk] * 4,
        out_shape=[jax.ShapeDtypeStruct(w.shape, F32)] * 4, input_output_aliases=aliases,
        compiler_params=_params(("parallel",)))(*args)


SMALL_ROWS = 48


def _small_rows(small, sq):
    def lane_pad(v):
        return jnp.pad(v, (0, D_MODEL - v.shape[0]))[None]

    head_rows = [lane_pad(jnp.concatenate([s["q_norm"][0], s["k_norm"][0], s["sink"]])) for s in small]
    loss_row = lane_pad((0.5 / D_MODEL) * jnp.sum(sq, keepdims=True)[0])
    rows = jnp.concatenate([s["mod"].reshape(-1, D_MODEL) for s in small] + [s["gammas"] for s in small] + head_rows + [loss_row], axis=0)
    return jnp.pad(rows, ((0, SMALL_ROWS - rows.shape[0]), (0, 0)))


def _local_step(x, ctx, target, mods, gam, qn, kn, sink, w_first, w_layers, packed, kc_idx):
    nb, seq, _ = x.shape
    rt = _Rows(nb, seq, ctx.shape[1])
    rt_lat = rt.latent_only()
    tables = _rope_tables(rt)
    fuse = packed is not None
    h = (x.reshape(rt.n_lat, D_MODEL), ctx.reshape(rt.n_ctx, D_MODEL))
    wg = [{}, {}] if fuse else [dict(w) for w in w_layers]
    wg[0]["in"] = (w_first, 0)
    if fuse:
        wg[0]["in_own"] = (packed, W_FIRST[0])
    saved = []
    for l in range(DEPTH):
        g_pre_mix, g_post_mix, g_pre_mlp, g_post_mlp = gam[l]
        if l == 0:
            u, qkv, qkvp, h = _in_fwd(rt, h, g_pre_mix, mods[l], wg[l], tables, qn[l], kn[l], f"in_fwd{l}")
        else:
            u, qkv, qkvp = _in_fwd(rt, h, g_pre_mix, mods[l], wg[l], tables, qn[l], kn[l], f"in_fwd{l}")
        if fuse and l == 0:
            o, lse_lat, w_mlp0, w_out0, w_in1 = _attn_fwd(rt, qkvp, sink[l], None, f"attn_lat_fwd{l}",
                                                         comm=_gather_comm(packed, [W_MLP0, W_OUT0, W_IN1], lead=2))
            wg[0].update({kind: (w_mlp0, PACK_OFF[(kind, 0)] - W_MLP0[0]) for kind in ("up", "down")})
            wg[0]["out"] = (w_out0, 0)
            wg[1] = {"in": (w_in1, 0)}
        elif fuse:
            o, lse_lat, w_mlp1, w_out1 = _attn_fwd(rt, qkvp, sink[l], None, f"attn_lat_fwd{l}",
                                                   comm=_gather_comm(packed, [W_MLP1, W_OUT1], lead=2))
            wg[1].update({kind: (w_mlp1, PACK_OFF[(kind, 1)] - W_MLP1[0]) for kind in ("up", "down")})
            wg[1]["out"] = (w_out1, 0)
        else:
            o, lse_lat = _attn_fwd(rt, qkvp, sink[l], None, f"attn_lat_fwd{l}")
        if l < DEPTH - 1:
            o, lse_ctx = _attn_fwd(rt, qkvp, sink[l], o, f"attn_ctx_fwd{l}")
            mix, h1, u2 = _out_fwd(rt, o, wg[l], h, mods[l], g_post_mix, g_pre_mlp, f"out_fwd{l}")
            r, y, h2 = _mlp_fwd(rt, u2, h1, wg[l], mods[l], g_post_mlp, f"mlp_fwd{l}")
        else:
            lse_ctx = None
            mix, h1, u2 = _out_fwd(rt_lat, o, wg[l], h, mods[l], g_post_mix, g_pre_mlp, f"out_fwd{l}")
            r, y, dh, sq = _mlp_fwd(rt_lat, u2, h1, wg[l], mods[l], g_post_mlp, f"mlp_fwd{l}", target=target.reshape(rt.n_lat, D_MODEL))
        saved.append((h, u, qkv, qkvp, o, lse_lat, lse_ctx, mix, h1, u2, r, y))
        h = h2

    small = [None] * DEPTH
    groups = {}
    for l in reversed(range(DEPTH)):
        g_pre_mix, g_post_mix, g_pre_mlp, g_post_mlp = gam[l]
        h0, u, qkv, qkvp, o, lse_lat, lse_ctx, mix, h1, u2, r, y = saved[l]
        mlp_group, mix_group = (G_LAYER1, G_LAYER1) if l == 1 else (G_MLP0, G_MIX0)
        hide = fuse and l == 0

        dead_ctx = l == DEPTH - 1
        rt_b = rt_lat if dead_ctx else rt
        dy, da, d_gate_m, d_g_post_mlp = _mlp_down_bwd(rt_b, dh, y, r, wg[l], mods[l], g_post_mlp, f"mlp_down_bwd{l}")
        p_mlp = _wgrad_packed(rt_b, r, dy, "down", PACK_OFF[("down", l)] - mlp_group[0], mlp_group[1], None, f"mlp_down_wgrad{l}",
                              comm=_pair_comm(groups[G_LAYER1]) if hide else None)
        if hide:
            p_mlp, r1 = p_mlp
            sum1 = _pair_sum(groups[G_LAYER1], r1, kc_idx, "grad_pair_sum_layer1")
        dh1, d_sh_m, d_sc_m, d_g_pre_mlp = _mlp_up_bwd(rt_b, da, wg[l], h1, dh, mods[l], g_pre_mlp, f"mlp_up_bwd{l}")
        p_mlp = _wgrad_packed(rt_b, u2, da, "up", PACK_OFF[("up", l)] - mlp_group[0], mlp_group[1], p_mlp, f"mlp_up_wgrad{l}")
        outs = _out_bwd(rt_b, dh1, mix, wg[l], mods[l], g_post_mix, f"out_bwd{l}", comm=_pair_comm(p_mlp) if hide else None)
        dmix, do, d_gate_a, d_g_post_mix = outs[:4]
        if hide:
            sum0 = _pair_sum(p_mlp, outs[4], kc_idx, "grad_pair_sum_mlp0")
        p_mix = _wgrad_packed(rt_b, o, dmix, "out", PACK_OFF[("out", l)] - mix_group[0], mix_group[1],
                              p_mlp if l == 1 else None, f"out_wgrad{l}")
        outs = _attn_bwd(rt, qkvp, o, lse_lat, do, sink[l], None, f"attn_lat_bwd{l}",
                         comm=_chip_comm([sum1[1], sum0[1]]) if hide else None)
        dq, dkv, dkv_c, dsink1 = outs[:4]
        if hide:
            groups[G_LAYER1] = _owner_sum(sum1[0], outs[4], kc_idx, "grad_owner_sum_layer1")
            groups[G_MLP0] = _owner_sum(sum0[0], outs[5], kc_idx, "grad_owner_sum_mlp0")
        if dead_ctx:
            dsink2 = jnp.zeros_like(dsink1)
            d_gate_m, d_sh_m, d_sc_m, d_gate_a = [a.at[nb].set(0.0) for a in (d_gate_m, d_sh_m, d_sc_m, d_gate_a)]
        else:
            dq, dkv, dsink2 = _attn_bwd(rt, qkvp, o, lse_ctx, do, sink[l], (dq, dkv, dkv_c), f"attn_ctx_bwd{l}")
        dqkv, dh, dqn, dkn, d_sh_a, d_sc_a, d_g_pre_mix = _in_bwd(rt, dq, dkv, qkv, tables, qn[l], kn[l], wg[l], h0, dh1, mods[l],
                                                                  g_pre_mix, l == 0, f"in_bwd{l}",
                                                                  dead_ctx_dkv=dkv_c if dead_ctx else None)
        dmod = jnp.concatenate([d_sh_a, d_sc_a, d_gate_a, d_sh_m, d_sc_m, d_gate_m], axis=1)
        small[l] = dict(mod=dmod, gammas=jnp.concatenate([d_g_pre_mix, d_g_post_mix, d_g_pre_mlp, d_g_post_mlp], axis=0),
                        q_norm=dqn, k_norm=dkn, sink=(dsink1 + dsink2)[:, 0])
        tail = _merge([_gather_comm(_small_rows(small, sq), [(0, SMALL_ROWS)]),
                       _halves_comm([groups[G_LAYER1], groups[G_MLP0]])]) if hide else None
        outs = _wgrad_packed(rt, u, dqkv, "in", PACK_OFF[("in", l)] - mix_group[0], mix_group[1], p_mix, f"in_wgrad{l}", comm=tail)
        if hide:
            groups[mix_group], small_g, groups[G_LAYER1], groups[G_MLP0] = outs
        else:
            groups[mix_group], small_g = outs, None
            if l == 0:
                groups[G_MLP0] = p_mlp
    return sq, dh.reshape(nb, seq, D_MODEL), [groups[G_LAYER1], groups[G_MLP0], groups[G_MIX0]], small, small_g


def kernel(x, c, ctx, c_ctx, w_ada, b_ada, g_pre_mix, g_post_mix, g_pre_mlp, g_post_mlp, w_in, q_norm, k_norm, sink, w_out, w_up, w_down, loss_target, m_c_ctx, m_w_ada, m_b_ada, m_g_pre_mix, m_g_post_mix, m_g_pre_mlp, m_g_post_mlp, m_w_in, m_q_norm, m_k_norm, m_sink, m_w_out, m_w_up, m_w_down, v_c_ctx, v_w_ada, v_b_ada, v_g_pre_mix, v_g_post_mix, v_g_pre_mlp, v_g_post_mlp, v_w_in, v_q_norm, v_k_norm, v_sink, v_w_out, v_w_up, v_w_down):
    nb = x.shape[0]
    ix, iy, ic = lax.axis_index("x"), lax.axis_index("y"), lax.axis_index("c")
    chip = 2 * ix + iy
    dev = 2 * chip + ic
    ada_cols = w_ada.shape[2] // 2

    c_rows = c.reshape(8, (nb * D_MODEL) // 8)
    packed, c_all = _pack_local_half(w_in, w_out, w_up, w_down, _gather_comm(c_rows, [(0, c_rows.shape[0])]), "pack_gather_c")
    c_all = c_all.reshape(N_DEV * nb, D_MODEL)
    n_cond = N_DEV * nb + 1
    cond_rows = 16 * ((n_cond + 15) // 16)
    cond = jnp.concatenate([c_all, c_ctx[None, :], jnp.zeros((cond_rows - n_cond, D_MODEL), F32)], axis=0)
    c_idx = ic.reshape(1).astype(jnp.int32)
    kc_idx = jnp.stack([chip, ic]).astype(jnp.int32)
    b_ada_half = lax.dynamic_slice_in_dim(b_ada, dev * ada_cols, ada_cols, 1)[:, None, :]
    x_ada, mod_part = _ada_fwd(cond, w_ada, b_ada_half, c_idx, "ada_fwd")
    mod_rows2d = mod_part.reshape(DEPTH * cond_rows, ada_cols)
    mod_g, w_first = _comm_alone(_merge([_gather_comm(mod_rows2d, [(0, mod_rows2d.shape[0])]),
                                         _gather_comm(packed, [W_FIRST], copy_own=False)]), "gather_mod_w_first")
    mod_all = mod_g.reshape(N_DEV, DEPTH, cond_rows, ada_cols).transpose(1, 2, 0, 3).reshape(DEPTH, cond_rows, N_MOD * D_MODEL)
    mods = []
    for l in range(DEPTH):
        mine = lax.dynamic_slice_in_dim(mod_all[l], dev * nb, nb, 0)
        mods.append(jnp.concatenate([mine, mod_all[l, n_cond - 1:n_cond]], axis=0).reshape(nb + 1, N_MOD, D_MODEL))

    gam = [(g_pre_mix[l][None], g_post_mix[l][None], g_pre_mlp[l][None], g_post_mlp[l][None]) for l in range(DEPTH)]
    qn = [jnp.tile(q_norm[l], 2)[None] for l in range(DEPTH)]
    kn = [jnp.tile(k_norm[l], 2)[None] for l in range(DEPTH)]
    _, grad_x, (h_layer1, h_mlp0, p_mix0), _, small_g = _local_step(x, ctx, loss_target, mods, gam, qn, kn, [sink[l] for l in range(DEPTH)],
                                                                 w_first, None, packed, kc_idx)

    def step(w, g, m, v, name):
        shape = w.shape
        cols = shape[-1]
        outs = _adamw(w.reshape(-1, cols), g.reshape(-1, cols), m.reshape(-1, cols), v.reshape(-1, cols), name)
        return tuple(a.reshape(shape) for a in outs)

    def shard_update(kind, w, m, v, layer0, layer1):
        outs = None
        for l, (halves, group) in enumerate((layer0, layer1)):
            outs = _adamw_shard(kind, l, w, m, v, halves, PACK_OFF[(kind, l)] - group[0], outs, f"adamw_w_{kind}{l}")
        return tuple(outs)

    tot = _dev_sum(small_g, "small_sum")
    mod_rows = (nb + 1) * N_MOD
    loss = tot[DEPTH * mod_rows + 4 * DEPTH + DEPTH, 0]

    ex = small_g[:, :DEPTH * mod_rows].reshape(N_DEV, DEPTH, nb + 1, N_MOD * D_MODEL)[:, :, :nb]
    ex = ex.transpose(1, 0, 2, 3).reshape(DEPTH, N_DEV * nb, N_MOD * D_MODEL)
    cx = tot[:DEPTH * mod_rows].reshape(DEPTH, nb + 1, N_MOD * D_MODEL)[:, nb:]
    dm = jnp.concatenate([ex, cx, jnp.zeros((DEPTH, cond_rows - n_cond, N_MOD * D_MODEL), F32)], axis=1)
    shard_cols = w_ada.shape[2]
    grad_w_ada = _ada_wgrad(x_ada, lax.dynamic_slice_in_dim(dm, chip * shard_cols, shard_cols, 2).astype(BF16), "ada_wgrad")
    dcx = jnp.pad(lax.dynamic_slice_in_dim(cx, dev * ada_cols, ada_cols, 2), ((0, 0), (0, 15), (0, 0))).astype(BF16)
    dcc = _ada_cond_bwd(dcx, w_ada, c_idx, "ada_cond_bwd")[0:8]

    r1, = _comm_alone(_pair_comm(p_mix0), "grad_pair_exchange_mix0")
    a32, a16 = _pair_sum(p_mix0, r1, kc_idx, "grad_pair_sum_mix0")
    r2, dcc_g = _comm_alone(_merge([_chip_comm([a16]), _gather_comm(dcc, [(0, dcc.shape[0])])]), "grad_chip_exchange_mix0")
    h_mix0 = _owner_sum(a32, r2, kc_idx, "grad_owner_sum_mix0")
    h_mix0, = _comm_alone(_halves_comm([h_mix0]), "grad_halves_exchange_mix0")

    small_names = ["c_ctx", "b_ada", "g_pre_mix", "g_post_mix", "g_pre_mlp", "g_post_mlp", "q_norm", "k_norm", "sink"]
    assert q_norm.shape[1] == HEAD_DIM and k_norm.shape[1] == HEAD_DIM
    small_res = _small_update(tot, dcc_g, [(c_ctx[None], m_c_ctx[None], v_c_ctx[None]), (b_ada, m_b_ada, v_b_ada),
                                           (g_pre_mix, m_g_pre_mix, v_g_pre_mix), (g_post_mix, m_g_post_mix, v_g_post_mix),
                                           (g_pre_mlp, m_g_pre_mlp, v_g_pre_mlp), (g_post_mlp, m_g_post_mlp, v_g_post_mlp),
                                           (q_norm, m_q_norm, v_q_norm), (k_norm, m_k_norm, v_k_norm), (sink, m_sink, v_sink)],
                              nb + 1, "small_update")
    res = {n: r for n, r in zip(small_names, small_res)}
    res["c_ctx"] = tuple(a[0] for a in res["c_ctx"])
    res["w_ada"] = (grad_w_ada, *step(w_ada, grad_w_ada, m_w_ada, v_w_ada, "adamw_w_ada"))
    res["w_up"] = shard_update("up", w_up, m_w_up, v_w_up, (h_mlp0, G_MLP0), (h_layer1, G_LAYER1))
    res["w_down"] = shard_update("down", w_down, m_w_down, v_w_down, (h_mlp0, G_MLP0), (h_layer1, G_LAYER1))
    res["w_in"] = shard_update("in", w_in, m_w_in, v_w_in, (h_mix0, G_MIX0), (h_layer1, G_LAYER1))
    res["w_out"] = shard_update("out", w_out, m_w_out, v_w_out, (h_mix0, G_MIX0), (h_layer1, G_LAYER1))

    order = ["c_ctx", "w_ada", "b_ada", "g_pre_mix", "g_post_mix", "g_pre_mlp", "g_post_mlp", "w_in", "q_norm", "k_norm", "sink", "w_out", "w_up", "w_down"]
    return (loss, grad_x, *[res[n][0] for n in order], *[res[n][1] for n in order],
            *[res[n][2] for n in order], *[res[n][3] for n in order])
```

```python
import functools

import jax
import jax.numpy as jnp
import numpy as np
from jax import lax
from jax.experimental import pallas as pl
from jax.experimental.pallas import tpu as pltpu

F32 = jnp.float32
BF16 = jnp.bfloat16

D_MODEL = 1024
HEAD_DIM = 64
GROUP = 4
WINDOW = 128
N_MOD = 6
D_FF = 4 * D_MODEL
IN_COLS = 1536
GRID_W = 64
ROPE_THETA = 10000.0
EPS = 1e-6
NEG_BIG = -1e30
Q_SCALE = HEAD_DIM ** -0.5
DEPTH = 2
N_DEV = 8

ADAM_LR = 0.001
ADAM_B1 = 0.9
ADAM_B2 = 0.999
ADAM_EPS = 1e-08
ADAM_WD = 0.01
ADAM_STEP = 10

V7X_VMEM_BYTES = 64 * 1024 * 1024
VMEM_LIMIT = V7X_VMEM_BYTES - 8 * 1024 * 1024

MESH = pl.DeviceIdType.MESH
NT = (((1,), (1,)), ((), ()))
TN = (((0,), (0,)), ((), ()))

COL_KA, COL_VA, COL_KB, COL_VB = 4, 5, 10, 11
NORMED_COLS = 640

PACK_HEIGHT = {"up": 512, "down": 512, "in": 256, "out": 128}
IN_PIECE_COLS = 384
PACK_OFF = {("up", 0): 0, ("down", 0): 512, ("in", 0): 1024, ("out", 0): 1280,
            ("up", 1): 1408, ("down", 1): 1920, ("in", 1): 2432, ("out", 1): 2688}
PACK_ROWS = 2816
W_FIRST, W_MLP0, W_OUT0, W_IN1, W_MLP1, W_OUT1 = (1024, 256), (0, 1024), (1280, 128), (2432, 256), (1408, 1024), (2688, 128)
G_LAYER1, G_MLP0, G_MIX0 = (1408, 1408), (0, 1024), (1024, 384)


def _pick(n, cands):
    for t in cands:
        if n % t == 0:
            return t
    raise ValueError(f"no tile for {n}")


def _params(sem):
    return pltpu.CompilerParams(dimension_semantics=sem, vmem_limit_bytes=VMEM_LIMIT)


class _Comm:
    def __init__(self, inputs, out_shapes, aliases, n_send, n_recv, start, finish, relay=None, lead=0):
        self.inputs, self.out_shapes, self.aliases = list(inputs), list(out_shapes), dict(aliases)
        self.n_send, self.n_recv, self.start, self.finish, self.relay, self.lead = n_send, n_recv, start, finish, relay, lead


def _comm_call(compute, comm, *, name, grid, in_specs, out_specs, out_shape, args, aliases, semantics, scratch=()):
    in_specs, out_specs, out_shape, args, aliases = list(in_specs), list(out_specs), list(out_shape), list(args), dict(aliases)
    scratch = list(scratch)
    if comm is None:
        return pl.pallas_call(compute, name=name, grid=grid, in_specs=in_specs, out_specs=out_specs, out_shape=out_shape,
                              input_output_aliases=aliases, scratch_shapes=scratch, compiler_params=_params(semantics))(*args)
    n_in, n_out, n_ci, n_co = len(args), len(out_shape), len(comm.inputs), len(comm.out_shapes)
    hbm = pl.BlockSpec(memory_space=pl.ANY)
    aliases.update({n_in + i: n_out + o for i, o in comm.aliases.items()})

    def body(*refs):
        ins, c_ins = refs[:n_in], refs[n_in:n_in + n_ci]
        outs, c_outs = refs[n_in + n_ci:n_in + n_ci + n_out], refs[n_in + n_ci + n_out:n_in + n_ci + n_out + n_co]
        scr = refs[n_in + n_ci + n_out + n_co:-2]
        send_sems, recv_sems = refs[-2:]
        ids = [pl.program_id(a) for a in range(len(grid))]
        first = functools.reduce(jnp.logical_and, [i == 0 for i in ids])
        last = functools.reduce(jnp.logical_and, [i == g - 1 for i, g in zip(ids, grid)])

        @pl.when(first)
        def _():
            comm.start(c_ins, c_outs, send_sems, recv_sems)

        compute(*ins, *outs, *scr)

        if comm.relay is not None:
            step = functools.reduce(lambda acc, ig: acc * ig[1] + ig[0], zip(ids, grid), 0)

            @pl.when(step == int(np.prod(grid)) - 1 - comm.lead)
            def _():
                comm.relay(c_ins, c_outs, send_sems, recv_sems)

        @pl.when(last)
        def _():
            comm.finish(c_ins, c_outs, send_sems, recv_sems)

    return pl.pallas_call(
        body, name=name, grid=grid,
        in_specs=in_specs + [hbm] * n_ci, out_specs=out_specs + [hbm] * n_co, out_shape=out_shape + comm.out_shapes,
        input_output_aliases=aliases,
        scratch_shapes=scratch + [pltpu.SemaphoreType.DMA((comm.n_send,)), pltpu.SemaphoreType.DMA((comm.n_recv,))],
        compiler_params=_params(("arbitrary",) * len(grid)),
    )(*args, *comm.inputs)


def _place():
    x_, y_, c_ = lax.axis_index("x"), lax.axis_index("y"), lax.axis_index("c")
    return x_, y_, c_, [(1 - x_, y_), (x_, 1 - y_), (1 - x_, 1 - y_)]


GATHER_SENDS, GATHER_RECVS = 8, 7


def _gather_copies(packed_ref, wg_ref, send_sems, recv_sems, rows, nth=0):
    r0, n = rows
    x_, y_, c_, chips = _place()
    me, sibling = (x_, y_, c_), (x_, y_, 1 - c_)
    src = packed_ref.at[pl.ds(r0, n), :]

    def slot(px, py, pc):
        return wg_ref.at[4 * px + 2 * py + pc]

    def copy(k, block, to, from_packed=False):
        return pltpu.make_async_remote_copy(src_ref=src if from_packed else slot(*block), dst_ref=slot(*block),
                                            send_sem=send_sems.at[GATHER_SENDS * nth + k], recv_sem=recv_sems.at[GATHER_RECVS * nth + k],
                                            device_id=to, device_id_type=MESH)

    own = [copy(0, me, sibling, True)] + [copy(1 + j, me, (*chip, c_), True) for j, chip in enumerate(chips)]
    passed = [copy(4 + j, (*chip, c_), sibling) for j, chip in enumerate(chips)]
    over_ici = [copy(1 + j, (*chip, c_), me) for j, chip in enumerate(chips)]
    from_sibling = [copy(0, sibling, me)] + [copy(4 + j, (*chip, 1 - c_), me) for j, chip in enumerate(chips)]
    mine = pltpu.make_async_copy(src, slot(*me), send_sems.at[GATHER_SENDS * nth + 7])
    return mine, own, passed, over_ici, from_sibling


def _gather_start(packed_ref, wg_ref, send_sems, recv_sems, rows, nth=0, copy_own=True):
    mine, own, _, _, _ = _gather_copies(packed_ref, wg_ref, send_sems, recv_sems, rows, nth)
    if copy_own:
        mine.start()
    for cp in own:
        cp.start()


def _gather_relay(packed_ref, wg_ref, send_sems, recv_sems, rows, nth=0):
    _, _, passed, over_ici, _ = _gather_copies(packed_ref, wg_ref, send_sems, recv_sems, rows, nth)
    for arrived, onward in zip(over_ici, passed):
        arrived.wait_recv()
        onward.start()


def _gather_finish(packed_ref, wg_ref, send_sems, recv_sems, rows, nth=0, copy_own=True):
    mine, own, passed, _, from_sibling = _gather_copies(packed_ref, wg_ref, send_sems, recv_sems, rows, nth)
    for arrived in from_sibling:
        arrived.wait_recv()
    for cp in own + passed:
        cp.wait_send()
    if copy_own:
        mine.wait()


def _gather_comm(packed, ranges, copy_own=True, lead=0):
    shapes = [jax.ShapeDtypeStruct((N_DEV, n, packed.shape[1]), packed.dtype) for _, n in ranges]

    def start(ins, outs, ss, rs):
        for nth, rows in enumerate(ranges):
            _gather_start(ins[0], outs[nth], ss, rs, rows, nth, copy_own)

    def relay(ins, outs, ss, rs):
        for nth, rows in enumerate(ranges):
            _gather_relay(ins[0], outs[nth], ss, rs, rows, nth)

    def finish(ins, outs, ss, rs):
        for nth, rows in enumerate(ranges):
            _gather_finish(ins[0], outs[nth], ss, rs, rows, nth, copy_own)

    return _Comm([packed], shapes, {}, GATHER_SENDS * len(ranges), GATHER_RECVS * len(ranges), start, finish, relay, lead)


def _pair_copy(p_ref, out_ref, send_sems, recv_sems):
    x_, y_, c_, _ = _place()
    return pltpu.make_async_remote_copy(src_ref=p_ref.at[1 - c_], dst_ref=out_ref,
                                        send_sem=send_sems.at[0], recv_sem=recv_sems.at[0],
                                        device_id=(x_, y_, 1 - c_), device_id_type=MESH)


def _pair_comm(p):
    return _Comm([p], [jax.ShapeDtypeStruct(p.shape[1:], p.dtype)], {}, 1, 1,
                 lambda ins, outs, ss, rs: _pair_copy(ins[0], outs[0], ss, rs).start(),
                 lambda ins, outs, ss, rs: _pair_copy(ins[0], outs[0], ss, rs).wait())


def _chip_copies(a_refs, out_refs, send_sems, recv_sems):
    _, _, c_, chips = _place()
    return [pltpu.make_async_remote_copy(src_ref=a_ref.at[2 * tx + ty], dst_ref=o_ref.at[j],
                                         send_sem=send_sems.at[3 * g + j], recv_sem=recv_sems.at[3 * g + j],
                                         device_id=(tx, ty, c_), device_id_type=MESH)
            for g, (a_ref, o_ref) in enumerate(zip(a_refs, out_refs)) for j, (tx, ty) in enumerate(chips)]


def _chip_start(a_refs, out_refs, send_sems, recv_sems):
    for cp in _chip_copies(a_refs, out_refs, send_sems, recv_sems):
        cp.start()


def _chip_finish(a_refs, out_refs, send_sems, recv_sems):
    for cp in _chip_copies(a_refs, out_refs, send_sems, recv_sems):
        cp.wait()


def _chip_comm(arrays):
    shapes = [jax.ShapeDtypeStruct((3,) + a.shape[1:], a.dtype) for a in arrays]
    return _Comm(arrays, shapes, {}, 3 * len(arrays), 3 * len(arrays), _chip_start, _chip_finish)


def _halves_copies(in_refs, out_refs, send_sems, recv_sems):
    x_, y_, c_, _ = _place()
    return [pltpu.make_async_remote_copy(src_ref=o_ref.at[c_], dst_ref=o_ref.at[c_], send_sem=send_sems.at[i], recv_sem=recv_sems.at[i],
                                         device_id=(x_, y_, 1 - c_), device_id_type=MESH)
            for i, o_ref in enumerate(out_refs)]


def _halves_start(in_refs, out_refs, send_sems, recv_sems):
    for cp in _halves_copies(in_refs, out_refs, send_sems, recv_sems):
        cp.start()


def _halves_finish(in_refs, out_refs, send_sems, recv_sems):
    for cp in _halves_copies(in_refs, out_refs, send_sems, recv_sems):
        cp.wait()


def _halves_comm(arrays):
    shapes = [jax.ShapeDtypeStruct(a.shape, a.dtype) for a in arrays]
    return _Comm(arrays, shapes, {i: i for i in range(len(arrays))}, len(arrays), len(arrays), _halves_start, _halves_finish)


class _SemSlice:
    class _At:
        def __init__(self, sems, first):
            self.sems, self.first = sems, first

        def __getitem__(self, k):
            return self.sems.at[self.first + k]

    def __init__(self, sems, first):
        self.at = _SemSlice._At(sems, first)


def _merge(comms):
    inputs = [a for c in comms for a in c.inputs]
    shapes = [s for c in comms for s in c.out_shapes]
    aliases, spans = {}, []
    i0 = o0 = s0 = r0 = 0
    for c in comms:
        aliases.update({i0 + i: o0 + o for i, o in c.aliases.items()})
        spans.append((slice(i0, i0 + len(c.inputs)), slice(o0, o0 + len(c.out_shapes)), s0, r0))
        i0, o0, s0, r0 = i0 + len(c.inputs), o0 + len(c.out_shapes), s0 + c.n_send, r0 + c.n_recv

    def start(ins, outs, ss, rs):
        for c, (i, o, s, r) in zip(comms, spans):
            c.start(ins[i], outs[o], _SemSlice(ss, s), _SemSlice(rs, r))

    def finish(ins, outs, ss, rs):
        for c, (i, o, s, r) in zip(comms, spans):
            if c.relay is not None:
                c.relay(ins[i], outs[o], _SemSlice(ss, s), _SemSlice(rs, r))
            c.finish(ins[i], outs[o], _SemSlice(ss, s), _SemSlice(rs, r))

    return _Comm(inputs, shapes, aliases, s0, r0, start, finish)


def _comm_alone(comm, name):
    n_ci = len(comm.inputs)
    hbm = pl.BlockSpec(memory_space=pl.ANY)

    def body(*refs):
        c_ins, c_outs, send_sems, recv_sems = refs[:n_ci], refs[n_ci:-2], refs[-2], refs[-1]
        comm.start(c_ins, c_outs, send_sems, recv_sems)
        if comm.relay is not None:
            comm.relay(c_ins, c_outs, send_sems, recv_sems)
        comm.finish(c_ins, c_outs, send_sems, recv_sems)

    return pl.pallas_call(
        body, name=name, out_shape=comm.out_shapes, in_specs=[hbm] * n_ci, out_specs=[hbm] * len(comm.out_shapes),
        input_output_aliases=comm.aliases,
        scratch_shapes=[pltpu.SemaphoreType.DMA((comm.n_send,)), pltpu.SemaphoreType.DMA((comm.n_recv,))],
    )(*comm.inputs)


SUM_TILES = (704, 512, 384, 320, 256, 192, 128, 64)


def _pair_sum(p, r1, kc_idx, name):
    _, _, n, c = p.shape
    tr = _pick(n, SUM_TILES)

    def body(s_ref, p_ref, r_ref, o32_ref, o16_ref):
        v = p_ref[...] + r_ref[...]
        o16_ref[...] = v.astype(BF16)

        @pl.when(pl.program_id(1) == s_ref[0])
        def _():
            o32_ref[...] = v

    blk = pl.BlockSpec((None, tr, c), lambda i, j, s: (j, i, 0))
    grid_spec = pltpu.PrefetchScalarGridSpec(
        num_scalar_prefetch=1, grid=(n // tr, 4),
        in_specs=[pl.BlockSpec((None, None, tr, c), lambda i, j, s: (s[1], j, i, 0)), blk],
        out_specs=[pl.BlockSpec((tr, c), lambda i, j, s: (i, 0)), blk])
    return pl.pallas_call(
        body, name=name, grid_spec=grid_spec,
        out_shape=[jax.ShapeDtypeStruct((n, c), F32), jax.ShapeDtypeStruct((4, n, c), BF16)],
        compiler_params=_params(("arbitrary", "arbitrary")),
    )(kc_idx, p, r1)


def _owner_sum(a32, r2, kc_idx, name):
    r, c = a32.shape
    tr = _pick(r, SUM_TILES)

    def body(s_ref, a_ref, r_ref, o_ref):
        v = a_ref[...]
        for j in range(3):
            v = v + r_ref[j].astype(F32)
        o_ref[...] = v

    grid_spec = pltpu.PrefetchScalarGridSpec(
        num_scalar_prefetch=1, grid=(r // tr,),
        in_specs=[pl.BlockSpec((tr, c), lambda i, s: (i, 0)),
                  pl.BlockSpec((3, tr, c), lambda i, s: (0, i, 0))],
        out_specs=pl.BlockSpec((None, tr, c), lambda i, s: (s[1], i, 0)))
    return pl.pallas_call(
        body, name=name, grid_spec=grid_spec,
        out_shape=jax.ShapeDtypeStruct((2, r, c), F32),
        compiler_params=_params(("arbitrary",)),
    )(kc_idx, a32, r2)


def _pack_local_half(w_in_s, w_out_s, w_up_s, w_down_s, comm, name):
    shards = {"in": w_in_s, "out": w_out_s, "up": w_up_s, "down": w_down_s}
    kinds = list(shards)
    assert sorted(off + PACK_HEIGHT[kind] for (kind, _), off in PACK_OFF.items()) == sorted(PACK_OFF.values())[1:] + [PACK_ROWS]
    for kind in kinds:
        assert shards[kind].shape[1] == (4 if kind == "in" else 2) * PACK_HEIGHT[kind], (kind, shards[kind].shape)

    def body(*refs):
        w_refs, p_ref = dict(zip(kinds, refs[:4])), refs[4]
        scr, sems = dict(zip(kinds, refs[5:9])), refs[9]
        c = lax.axis_index("c")
        copies = {}
        for n, (kind, l) in enumerate(sorted(PACK_OFF)):
            rows = scr[kind].shape[1]
            copies[(kind, l)] = pltpu.make_async_copy(w_refs[kind].at[l, pl.ds(c * rows, rows)], scr[kind].at[l], sems.at[n])
            copies[(kind, l)].start()
        for (kind, l), off in sorted(PACK_OFF.items(), key=lambda kv: kv[1]):
            copies[(kind, l)].wait()
            h = PACK_HEIGHT[kind]
            if kind == "in":
                for t in range(2):
                    p_ref[off:off + h, t * IN_PIECE_COLS:(t + 1) * IN_PIECE_COLS] = scr[kind][l, t * h:(t + 1) * h, :].astype(BF16)
                p_ref[off:off + h, 2 * IN_PIECE_COLS:] = jnp.zeros((h, 1024 - 2 * IN_PIECE_COLS), BF16)
            else:
                p_ref[off:off + h, :] = scr[kind][l].astype(BF16)

    hbm = pl.BlockSpec(memory_space=pl.ANY)
    scratch = [pltpu.VMEM((DEPTH, shards[kind].shape[1] // 2, shards[kind].shape[2]), F32) for kind in kinds]
    outs = _comm_call(
        body, comm, name=name, grid=(1,), in_specs=[hbm] * 4,
        out_specs=[pl.BlockSpec((PACK_ROWS, 1024), lambda i: (0, 0))],
        out_shape=[jax.ShapeDtypeStruct((PACK_ROWS, 1024), BF16)],
        args=[shards[kind] for kind in kinds], aliases={}, semantics=("arbitrary",),
        scratch=scratch + [pltpu.SemaphoreType.DMA((len(PACK_OFF),))])
    return outs


def _unpack_in_pieces(w_ref, own_ref, w_scr):
    if own_ref is not None:
        me = 4 * lax.axis_index("x") + 2 * lax.axis_index("y") + lax.axis_index("c")
    for d in range(N_DEV):
        k, c = d // 2, d % 2
        for t in range(2):
            piece = w_ref[d, :, t * IN_PIECE_COLS:(t + 1) * IN_PIECE_COLS]
            if own_ref is not None:
                piece = jnp.where(me == d, own_ref[:, t * IN_PIECE_COLS:(t + 1) * IN_PIECE_COLS], piece)
            w_scr[c * 512 + t * 256:c * 512 + (t + 1) * 256, k * IN_PIECE_COLS:(k + 1) * IN_PIECE_COLS] = piece


def _in_weight_operands(wg):
    specs, args = [_gathered_spec(wg, "in")], [wg["in"][0]]
    if "in_own" in wg:
        own, off = wg["in_own"]
        h = PACK_HEIGHT["in"]
        assert off % h == 0
        specs.append(pl.BlockSpec((h, 1024), lambda *_: (off // h, 0), pipeline_mode=pl.Buffered(1)))
        args.append(own)
    return specs, args


class _Rows:
    def __init__(self, nb, seq, ctx):
        self.nb, self.seq, self.ctx = nb, seq, ctx
        self.n_lat, self.n_ctx = nb * seq, nb * ctx
        self.rows = self.n_lat + self.n_ctx
        self.tm = _pick(np.gcd(seq, self.n_ctx), (512, 256, 128))
        self.tiles_per_ex = seq // self.tm
        self.n_tiles = self.rows // self.tm
        self.n_lat_tiles = self.n_lat // self.tm
        self.groups = nb + 1

    def latent_only(self):
        rt = _Rows(self.nb, self.seq, self.ctx)
        rt.n_tiles = self.n_lat_tiles
        return rt

    def group(self, i):
        return jnp.minimum(i // self.tiles_per_ex, self.nb)

    def first_of_group(self, i):
        return jnp.logical_and(i % self.tiles_per_ex == 0, i <= self.n_lat_tiles)


def _mod_spec(rt):
    return pl.BlockSpec((1, N_MOD, D_MODEL), lambda i: (rt.group(i), 0, 0))


def _row_spec(rt, cols):
    return pl.BlockSpec((rt.tm, cols), lambda i: (i, 0))


def _vec_spec(cols):
    return pl.BlockSpec((1, cols), lambda i: (0, 0))


def _group_spec(rt):
    return pl.BlockSpec((1, 1, D_MODEL), lambda i: (rt.group(i), 0, 0))


def _gathered_spec(wg, kind):
    h, off = PACK_HEIGHT[kind], wg[kind][1]
    assert off % h == 0, (kind, off)
    return pl.BlockSpec((N_DEV, h, 1024), lambda *_: (0, off // h, 0), pipeline_mode=pl.Buffered(1))


def _group_shape(rt):
    return jax.ShapeDtypeStruct((rt.groups, 1, D_MODEL), F32)


def _vec_shape(cols=D_MODEL):
    return jax.ShapeDtypeStruct((1, cols), F32)


def _rms_inv(v):
    return lax.rsqrt(jnp.mean(v * v, axis=-1, keepdims=True) + EPS)


def _norm_mod_val(h_, g_, mod_ref, i_shift, i_scale):
    n = h_ * _rms_inv(h_) * g_
    return n * (1.0 + mod_ref[0, i_scale:i_scale + 1, :]) + mod_ref[0, i_shift:i_shift + 1, :]


def _post_norm_val(h_, z_, g_, mod_ref, i_gate):
    return h_ + mod_ref[0, i_gate:i_gate + 1, :] * (z_ * _rms_inv(z_) * g_)


def _post_norm_bwd_val(dh_, z_, g_, gate):
    rinv = _rms_inv(z_)
    n0 = z_ * rinv
    dn = dh_ * gate * g_
    dz = rinv * (dn - n0 * jnp.mean(dn * n0, axis=-1, keepdims=True))
    return dz, jnp.sum(dh_ * n0 * g_, axis=0, keepdims=True), jnp.sum(dh_ * gate * n0, axis=0, keepdims=True)


def _norm_mod_bwd_val(du_, h_, g_, one_sc):
    rinv = _rms_inv(h_)
    n0 = h_ * rinv
    dn = du_ * g_ * one_sc
    dh = rinv * (dn - n0 * jnp.mean(dn * n0, axis=-1, keepdims=True))
    return (dh, jnp.sum(du_, axis=0, keepdims=True), jnp.sum(du_ * n0 * g_, axis=0, keepdims=True),
            jnp.sum(du_ * one_sc * n0, axis=0, keepdims=True))


def _accumulate(rt, i, group_pairs, global_pairs):
    @pl.when(rt.first_of_group(i))
    def _():
        for ref, _ in group_pairs:
            ref[...] = jnp.zeros_like(ref)

    @pl.when(i == 0)
    def _():
        for ref, _ in global_pairs:
            ref[...] = jnp.zeros_like(ref)

    for ref, val in group_pairs:
        ref[0] += val
    for ref, val in global_pairs:
        ref[...] += val


def _rope_tables(rt):
    pos = np.arange(rt.seq)
    axis_dim = HEAD_DIM // 2
    inv = (ROPE_THETA ** (-np.arange(0, axis_dim, 2, dtype=np.float32) / axis_dim)).astype(np.float32)
    ang_r = (pos // GRID_W).astype(np.float32)[:, None] * inv[None, :]
    ang_c = (pos % GRID_W).astype(np.float32)[:, None] * inv[None, :]
    cr, sr, cc, sc = np.cos(ang_r), np.sin(ang_r), np.cos(ang_c), np.sin(ang_c)
    zero = np.zeros_like(sr)
    cos = np.concatenate([cr, cr, cc, cc], axis=1)
    s_lo = np.concatenate([zero, sr, zero, sc], axis=1)
    s_hi = np.concatenate([-sr, zero, -sc, zero], axis=1)

    def full(t, ctx_value):
        return jnp.asarray(np.concatenate([np.tile(t, (1, 2)), np.full((rt.tm, 128), ctx_value)], axis=0), F32)

    return full(cos, 1.0), full(s_lo, 0.0), full(s_hi, 0.0)


def _table_spec(rt):
    return pl.BlockSpec((rt.tm, 128), lambda i: (jnp.where(i < rt.n_lat_tiles, i % rt.tiles_per_ex, rt.tiles_per_ex), 0))


def _head_mean(x):
    r = lax.broadcasted_iota(jnp.int32, (128, 128), 0) // HEAD_DIM
    c = lax.broadcasted_iota(jnp.int32, (128, 128), 1) // HEAD_DIM
    ones = jnp.where(r == c, 1.0 / HEAD_DIM, 0.0).astype(F32)
    return jnp.dot(x, ones, preferred_element_type=F32, precision=lax.Precision.HIGH)


def _head_stats(t):
    return lax.rsqrt(_head_mean(t * t) + EPS)


def _prep_fwd_body(tm, qkv_ref, c, s1, s2, qn, kn, out_ref):
    def rope(t):
        return t * c + pltpu.roll(t, 16, 1) * s1 + pltpu.roll(t, 112, 1) * s2

    for j in range(12):
        t = qkv_ref[:, j * 128:(j + 1) * 128]
        if j < 4:
            t = rope(t * _head_stats(t) * qn) * Q_SCALE
        elif j == COL_KA:
            t = rope(t * _head_stats(t) * kn)
        elif 6 <= j < 10:
            t = rope(t) * Q_SCALE
        elif j == COL_KB:
            t = rope(t)
        out_ref[:, j * 128:(j + 1) * 128] = t.astype(BF16)


def _prep_bwd_body(dq, dkv, qkv_ref, c, s1, s2, qn, kn, out_ref):
    rows = slice(None)

    def rope_bwd(d):
        return d * c + pltpu.roll(d * s1, 112, 1) + pltpu.roll(d * s2, 16, 1)

    def norm_bwd(t, g, dy):
        rinv = _head_stats(t)
        n = t * rinv
        dn = dy * g
        return rinv * (dn - n * _head_mean(dn * n)), jnp.sum(dy * n, axis=0, keepdims=True)

    dqn = jnp.zeros((1, 128), F32)
    dkn = jnp.zeros((1, 128), F32)
    for j in range(12):
        if j < 4:
            d, dg = norm_bwd(qkv_ref[rows, j * 128:(j + 1) * 128], qn, rope_bwd(dq(slice(j * 128, (j + 1) * 128)) * Q_SCALE))
            dqn = dqn + dg
        elif j == COL_KA:
            d, dg = norm_bwd(qkv_ref[rows, j * 128:(j + 1) * 128], kn, rope_bwd(dkv(slice(0, 128))))
            dkn = dkn + dg
        elif j == COL_VA:
            d = dkv(slice(128, 256))
        elif j < 10:
            d = rope_bwd(dq(slice((j - 2) * 128, (j - 1) * 128)) * Q_SCALE)
        elif j == COL_KB:
            d = rope_bwd(dkv(slice(256, 384)))
        else:
            d = dkv(slice(384, 512))
        out_ref[rows, j * 128:(j + 1) * 128] = d.astype(BF16)
    return dqn, dkn


def _in_fwd(rt, h, gamma, mod, wg, tables, qn, kn, name):
    w_specs, w_args = _in_weight_operands(wg)
    n_w = len(w_args)
    joined = not isinstance(h, (tuple, list))
    n_h = 1 if joined else 2

    def body(*refs):
        g_ref, mod_ref = refs[n_h:n_h + 2]
        rest = refs[n_h + 2:]
        c_ref, s1_ref, s2_ref, qn_ref, kn_ref, u_ref, qkn_ref, qkvp_ref = rest[n_w:n_w + 8]
        qkv_ref, w_scr = rest[-2:]
        i = pl.program_id(0)

        @pl.when(i == 0)
        def _():
            _unpack_in_pieces(rest[0], rest[1] if n_w == 2 else None, w_scr)

        if joined:
            h_ = refs[0][...]
        else:
            h_ = jnp.where(i < rt.n_lat_tiles, refs[0][...], refs[1][...])
            rest[n_w + 8][...] = h_
        u = _norm_mod_val(h_, g_ref[...], mod_ref, 0, 1).astype(BF16)
        u_ref[...] = u
        qkv_ref[...] = jnp.dot(u, w_scr[...], preferred_element_type=F32)
        qkn_ref[...] = qkv_ref[:, 0:NORMED_COLS]
        _prep_fwd_body(rt.tm, qkv_ref, c_ref[...], s1_ref[...], s2_ref[...], qn_ref[...], kn_ref[...], qkvp_ref)

    if joined:
        h_specs, h_args = [_row_spec(rt, D_MODEL)], [h]
    else:
        h_specs = [pl.BlockSpec((rt.tm, D_MODEL), lambda i: (jnp.minimum(i, rt.n_lat_tiles - 1), 0)),
                   pl.BlockSpec((rt.tm, D_MODEL), lambda i: (jnp.maximum(i - rt.n_lat_tiles, 0), 0))]
        h_args = list(h)
    out_specs = [_row_spec(rt, D_MODEL), _row_spec(rt, NORMED_COLS), _row_spec(rt, IN_COLS)]
    out_shape = [jax.ShapeDtypeStruct((rt.rows, D_MODEL), BF16), jax.ShapeDtypeStruct((rt.rows, NORMED_COLS), F32),
                 jax.ShapeDtypeStruct((rt.rows, IN_COLS), BF16)]
    if not joined:
        out_specs.append(_row_spec(rt, D_MODEL))
        out_shape.append(jax.ShapeDtypeStruct((rt.rows, D_MODEL), F32))
    return pl.pallas_call(
        body, name=name, grid=(rt.n_tiles,),
        in_specs=h_specs + [_vec_spec(D_MODEL), _mod_spec(rt)] + w_specs + [_table_spec(rt)] * 3 + [_vec_spec(128)] * 2,
        out_specs=out_specs, out_shape=out_shape,
        scratch_shapes=[pltpu.VMEM((rt.tm, IN_COLS), F32), pltpu.VMEM((D_MODEL, IN_COLS), BF16)],
        compiler_params=_params(("arbitrary",)),
    )(*h_args, gamma, mod, *w_args, *tables, qn, kn)


def _in_bwd(rt, dq, dkv, qkv, tables, qn, kn, wg, h, dres, mod, gamma, latent_only, name, comm=None, dead_ctx_dkv=None):
    last = rt.n_lat_tiles - 1
    w_specs, w_args = _in_weight_operands(wg)
    n_w = len(w_args)
    n_dead = 0 if dead_ctx_dkv is None else 1

    def body(dq_ref, dkv_ref, qkv_ref, c_ref, s1_ref, s2_ref, qn_ref, kn_ref, *rest):
        h_ref, dres_ref, mod_ref, g_ref, dqkv_ref, dh_ref, dqn_ref, dkn_ref, dsh_ref, dsc_ref, dg_ref, w_scr = rest[n_w + n_dead:]
        i = pl.program_id(0)

        @pl.when(i == 0)
        def _():
            _unpack_in_pieces(rest[0], rest[1] if n_w == 2 else None, w_scr)

        if n_dead:
            c1_ref, lat = rest[n_w], i <= last
            load_dq = lambda cols: jnp.where(lat, dq_ref[:, cols], 0.0)
            load_dkv = lambda cols: jnp.where(lat, dkv_ref[:, cols], c1_ref[:, cols])
            dres_ = jnp.where(lat, dres_ref[...], 0.0)
        else:
            load_dq, load_dkv, dres_ = (lambda cols: dq_ref[:, cols]), (lambda cols: dkv_ref[:, cols]), dres_ref[...]
        dqn, dkn = _prep_bwd_body(load_dq, load_dkv, qkv_ref, c_ref[...], s1_ref[...], s2_ref[...], qn_ref[...], kn_ref[...], dqkv_ref)
        du = lax.dot_general(dqkv_ref[...], w_scr[...], NT, preferred_element_type=F32)
        dh, dsh, dsc, dg = _norm_mod_bwd_val(du, h_ref[...], g_ref[...], 1.0 + mod_ref[0, 1:2, :])
        if latent_only:
            @pl.when(i <= last)
            def _():
                dh_ref[...] = dres_ + dh
        else:
            dh_ref[...] = dres_ + dh
        _accumulate(rt, i, [(dsh_ref, dsh), (dsc_ref, dsc)], [(dg_ref, dg), (dqn_ref, dqn), (dkn_ref, dkn)])

    dh_spec = pl.BlockSpec((rt.tm, D_MODEL), lambda i: (jnp.minimum(i, last), 0)) if latent_only else _row_spec(rt, D_MODEL)
    dead_specs = [] if dead_ctx_dkv is None else [pl.BlockSpec((rt.tm, 512), lambda i: (jnp.maximum(i - rt.n_lat_tiles, 0), 0))]
    dead_args = [] if dead_ctx_dkv is None else [dead_ctx_dkv]
    return _comm_call(
        body, comm, name=name, grid=(rt.n_tiles,),
        in_specs=[_row_spec(rt, 1024), _row_spec(rt, 512), _row_spec(rt, NORMED_COLS)] + [_table_spec(rt)] * 3 + [_vec_spec(128)] * 2
        + w_specs + dead_specs + [_row_spec(rt, D_MODEL), _row_spec(rt, D_MODEL), _mod_spec(rt), _vec_spec(D_MODEL)],
        out_specs=[_row_spec(rt, IN_COLS), dh_spec, _vec_spec(128), _vec_spec(128),
                   _group_spec(rt), _group_spec(rt), _vec_spec(D_MODEL)],
        out_shape=[jax.ShapeDtypeStruct((rt.rows, IN_COLS), BF16),
                   jax.ShapeDtypeStruct((rt.n_lat if latent_only else rt.rows, D_MODEL), F32),
                   _vec_shape(128), _vec_shape(128), _group_shape(rt), _group_shape(rt), _vec_shape()],
        args=[dq, dkv, qkv, *tables, qn, kn, *w_args, *dead_args, h, dres, mod, gamma], aliases={}, semantics=("arbitrary",),
        scratch=[pltpu.VMEM((D_MODEL, IN_COLS), BF16)])


def _out_fwd(rt, o, wg, h, mod, g_post_mix, g_pre_mlp, name):
    def body(o_ref, w_ref, h_ref, mod_ref, gpost_ref, gpre_ref, mix_ref, h1_ref, u2_ref):
        mix = jnp.dot(o_ref[...], w_ref[...].reshape(D_MODEL, D_MODEL), preferred_element_type=F32)
        mix_ref[...] = mix
        h1 = _post_norm_val(h_ref[...], mix, gpost_ref[...], mod_ref, 2)
        h1_ref[...] = h1
        u2_ref[...] = _norm_mod_val(h1, gpre_ref[...], mod_ref, 3, 4).astype(BF16)

    return pl.pallas_call(
        body, name=name, grid=(rt.n_tiles,),
        in_specs=[_row_spec(rt, D_MODEL), _gathered_spec(wg, "out"), _row_spec(rt, D_MODEL), _mod_spec(rt),
                  _vec_spec(D_MODEL), _vec_spec(D_MODEL)],
        out_specs=[_row_spec(rt, D_MODEL)] * 3,
        out_shape=[jax.ShapeDtypeStruct((rt.rows, D_MODEL), F32), jax.ShapeDtypeStruct((rt.rows, D_MODEL), F32),
                   jax.ShapeDtypeStruct((rt.rows, D_MODEL), BF16)],
        compiler_params=_params(("parallel",)),
    )(o, wg["out"][0], h, mod, g_post_mix, g_pre_mlp)


def _out_bwd(rt, dh1, mix, wg, mod, g_post_mix, name):
    def body(dh_ref, mix_ref, w_ref, mod_ref, g_ref, dmix_ref, do_ref, dgate_ref, dg_ref):
        i = pl.program_id(0)
        dz, dgate, dg = _post_norm_bwd_val(dh_ref[...], mix_ref[...], g_ref[...], mod_ref[0, 2:3, :])
        dzb = dz.astype(BF16)
        dmix_ref[...] = dzb
        do_ref[...] = lax.dot_general(dzb, w_ref[...].reshape(D_MODEL, D_MODEL), NT, preferred_element_type=F32).astype(BF16)
        _accumulate(rt, i, [(dgate_ref, dgate)], [(dg_ref, dg)])

    return pl.pallas_call(
        body, name=name, grid=(rt.n_tiles,),
        in_specs=[_row_spec(rt, D_MODEL), _row_spec(rt, D_MODEL), _gathered_spec(wg, "out"), _mod_spec(rt), _vec_spec(D_MODEL)],
        out_specs=[_row_spec(rt, D_MODEL), _row_spec(rt, D_MODEL), _group_spec(rt), _vec_spec(D_MODEL)],
        out_shape=[jax.ShapeDtypeStruct((rt.rows, D_MODEL), BF16), jax.ShapeDtypeStruct((rt.rows, D_MODEL), BF16),
                   _group_shape(rt), _vec_shape()],
        compiler_params=_params(("arbitrary",)),
    )(dh1, mix, wg["out"][0], mod, g_post_mix)


def _w_chunk(w_ref, k):
    return w_ref[2 * k:2 * k + 2].reshape(1024, 1024)


def _mlp_fwd(rt, u2, h1, wg, mod, g_post_mlp, name, comm=None, target=None):
    last = rt.n_lat_tiles - 1

    def body(u2_ref, h1_ref, wu_ref, wd_ref, mod_ref, g_ref, *rest):
        u2_ = u2_ref[...]
        y = jnp.zeros((rt.tm, D_MODEL), F32)
        for k in range(D_FF // 1024):
            a = jnp.maximum(jnp.dot(u2_, _w_chunk(wu_ref, k), preferred_element_type=F32), 0.0)
            rest[-3 if target is None else -4][:, k * 1024:(k + 1) * 1024] = a.astype(BF16)
            y = y + jnp.dot((a * a).astype(BF16), _w_chunk(wd_ref, k), preferred_element_type=F32)
        h2 = _post_norm_val(h1_ref[...], y, g_ref[...], mod_ref, 5)
        if target is None:
            _, y_ref, h2_ref = rest
            y_ref[...] = y
            h2_ref[...] = h2
        else:
            t_ref, _, y_ref, dh_ref, sq_ref = rest
            y_ref[...] = y
            i = pl.program_id(0)

            @pl.when(i == 0)
            def _():
                sq_ref[...] = jnp.zeros_like(sq_ref)

            @pl.when(i <= last)
            def _():
                e = h2 - t_ref[...]
                dh_ref[...] = e * (1.0 / D_MODEL)
                sq_ref[...] += jnp.sum(e * e, axis=0, keepdims=True)

            @pl.when(i > last)
            def _():
                dh_ref[...] = jnp.zeros_like(dh_ref)

    in_specs = [_row_spec(rt, D_MODEL), _row_spec(rt, D_MODEL), _gathered_spec(wg, "up"), _gathered_spec(wg, "down"),
                _mod_spec(rt), _vec_spec(D_MODEL)]
    args = [u2, h1, wg["up"][0], wg["down"][0], mod, g_post_mlp]
    out_specs = [_row_spec(rt, D_FF), _row_spec(rt, D_MODEL), _row_spec(rt, D_MODEL)]
    out_shape = [jax.ShapeDtypeStruct((rt.rows, D_FF), BF16), jax.ShapeDtypeStruct((rt.rows, D_MODEL), F32),
                 jax.ShapeDtypeStruct((rt.rows, D_MODEL), F32)]
    if target is not None:
        in_specs.append(pl.BlockSpec((rt.tm, D_MODEL), lambda i: (jnp.minimum(i, last), 0)))
        args.append(target)
        out_specs.append(_vec_spec(D_MODEL))
        out_shape.append(_vec_shape())
    return _comm_call(body, comm, name=name, grid=(rt.n_tiles,), in_specs=in_specs, out_specs=out_specs, out_shape=out_shape,
                      args=args, aliases={}, semantics=("parallel",) if target is None else ("arbitrary",))


def _mlp_down_bwd(rt, dh, y, ra, wg, mod, g_post_mlp, name):
    def body(dh_ref, y_ref, ra_ref, wd_ref, mod_ref, g_ref, dy_ref, da_ref, dgate_ref, dg_ref):
        i = pl.program_id(0)
        dz, dgate, dg = _post_norm_bwd_val(dh_ref[...], y_ref[...], g_ref[...], mod_ref[0, 5:6, :])
        dyb = dz.astype(BF16)
        dy_ref[...] = dyb
        for k in range(D_FF // 1024):
            dr = lax.dot_general(dyb, _w_chunk(wd_ref, k), NT, preferred_element_type=F32)
            da_ref[:, k * 1024:(k + 1) * 1024] = (dr * (2.0 * ra_ref[:, k * 1024:(k + 1) * 1024].astype(F32))).astype(BF16)
        _accumulate(rt, i, [(dgate_ref, dgate)], [(dg_ref, dg)])

    return pl.pallas_call(
        body, name=name, grid=(rt.n_tiles,),
        in_specs=[_row_spec(rt, D_MODEL), _row_spec(rt, D_MODEL), _row_spec(rt, D_FF), _gathered_spec(wg, "down"),
                  _mod_spec(rt), _vec_spec(D_MODEL)],
        out_specs=[_row_spec(rt, D_MODEL), _row_spec(rt, D_FF), _group_spec(rt), _vec_spec(D_MODEL)],
        out_shape=[jax.ShapeDtypeStruct((rt.rows, D_MODEL), BF16), jax.ShapeDtypeStruct((rt.rows, D_FF), BF16),
                   _group_shape(rt), _vec_shape()],
        compiler_params=_params(("arbitrary",)),
    )(dh, y, ra, wg["down"][0], mod, g_post_mlp)


def _mlp_up_bwd(rt, da, wg, h1, dh, mod, g_pre_mlp, name, comm=None):
    def body(da_ref, wu_ref, h1_ref, dh_ref, mod_ref, g_ref, dh1_ref, dsh_ref, dsc_ref, dg_ref):
        i = pl.program_id(0)
        du = jnp.zeros((rt.tm, D_MODEL), F32)
        for k in range(D_FF // 1024):
            du = du + lax.dot_general(da_ref[:, k * 1024:(k + 1) * 1024], _w_chunk(wu_ref, k), NT, preferred_element_type=F32)
        d, dsh, dsc, dg = _norm_mod_bwd_val(du, h1_ref[...], g_ref[...], 1.0 + mod_ref[0, 4:5, :])
        dh1_ref[...] = dh_ref[...] + d
        _accumulate(rt, i, [(dsh_ref, dsh), (dsc_ref, dsc)], [(dg_ref, dg)])

    return _comm_call(
        body, comm, name=name, grid=(rt.n_tiles,),
        in_specs=[_row_spec(rt, D_FF), _gathered_spec(wg, "up"), _row_spec(rt, D_MODEL), _row_spec(rt, D_MODEL),
                  _mod_spec(rt), _vec_spec(D_MODEL)],
        out_specs=[_row_spec(rt, D_MODEL), _group_spec(rt), _group_spec(rt), _vec_spec(D_MODEL)],
        out_shape=[jax.ShapeDtypeStruct((rt.rows, D_MODEL), F32), _group_shape(rt), _group_shape(rt), _vec_shape()],
        args=[da, wg["up"][0], h1, dh, mod, g_pre_mlp], aliases={}, semantics=("arbitrary",))


def _wgrad_packed(rt, a, b, kind, off, n_rows, p_prev, name, comm=None):
    h = PACK_HEIGHT[kind]
    tk = rt.tm
    assert off % h == 0, (kind, off)

    def body(a_ref, b_ref, *rest):
        o_ref = rest[-1]
        i = pl.program_id(0)

        @pl.when(i == 0)
        def _():
            o_ref[...] = jnp.zeros_like(o_ref)

        if kind == "in":
            res = lax.dot_general(a_ref[...], b_ref[...], TN, preferred_element_type=F32)
            for k in range(4):
                for c in range(2):
                    for t in range(2):
                        o_ref[c, k, :, t * IN_PIECE_COLS:(t + 1) * IN_PIECE_COLS] += \
                            res[c * 512 + t * h:c * 512 + (t + 1) * h, k * IN_PIECE_COLS:(k + 1) * IN_PIECE_COLS]
        elif kind == "out":
            res = lax.dot_general(a_ref[...], b_ref[...], TN, preferred_element_type=F32)
            for k in range(4):
                for c in range(2):
                    o_ref[c, k] += res[(2 * k + c) * h:(2 * k + c + 1) * h]
        else:
            for k in range(4):
                if kind == "up":
                    res = lax.dot_general(a_ref[...], b_ref[:, k * 1024:(k + 1) * 1024], TN, preferred_element_type=F32)
                else:
                    ra = a_ref[:, k * 1024:(k + 1) * 1024].astype(F32)
                    res = lax.dot_general((ra * ra).astype(BF16), b_ref[...], TN, preferred_element_type=F32)
                o_ref[0, k] += res[0:h]
                o_ref[1, k] += res[h:2 * h]

    in_specs = [pl.BlockSpec((tk, a.shape[1]), lambda i: (i, 0)), pl.BlockSpec((tk, b.shape[1]), lambda i: (i, 0))]
    args = [a, b]
    aliases = {}
    if p_prev is not None:
        in_specs.append(pl.BlockSpec(memory_space=pl.ANY))
        args.append(p_prev)
        aliases = {2: 0}
    outs = _comm_call(
        body, comm, name=name, grid=(rt.n_tiles,),
        in_specs=in_specs,
        out_specs=[pl.BlockSpec((2, 4, h, 1024), lambda i: (0, 0, off // h, 0))],
        out_shape=[jax.ShapeDtypeStruct((2, 4, n_rows, 1024), F32)],
        args=args, aliases=aliases, semantics=("arbitrary",))
    return outs[0] if comm is None else outs


def _ada_wgrad(xs, dm, name):
    depth, _, cols = dm.shape

    def body(x_ref, d_ref, o_ref):
        for l in range(depth):
            o_ref[l] = lax.dot_general(x_ref[...], d_ref[l], TN, preferred_element_type=F32)

    return pl.pallas_call(body, name=name, out_shape=jax.ShapeDtypeStruct((depth, xs.shape[1], cols), F32),
                          compiler_params=pltpu.CompilerParams(vmem_limit_bytes=VMEM_LIMIT))(xs, dm)


def _stack_heads(x, kvi):
    x = x.astype(F32)
    tq = x.shape[0]
    lane = lax.broadcasted_iota(jnp.int32, (tq, 128), 1)
    keep = lane < HEAD_DIM if kvi == 0 else lane >= HEAD_DIM
    parts = []
    for p in range(2):
        pair = x[:, p * 128:(p + 1) * 128]
        swapped = pltpu.roll(pair, HEAD_DIM, 1)
        lo_head, hi_head = (pair, swapped) if kvi == 0 else (swapped, pair)
        parts += [jnp.where(keep, lo_head, 0.0), jnp.where(keep, hi_head, 0.0)]
    return jnp.concatenate(parts, axis=0).astype(BF16)


def _unstack_heads(o4, kvi):
    tq = o4.shape[0] // GROUP
    lane = lax.broadcasted_iota(jnp.int32, (tq, 128), 1)
    outs = []
    for p in range(2):
        r_lo, r_hi = o4[(2 * p) * tq:(2 * p + 1) * tq], o4[(2 * p + 1) * tq:(2 * p + 2) * tq]
        if kvi == 0:
            lo, hi = r_lo, pltpu.roll(r_hi, HEAD_DIM, 1)
        else:
            lo, hi = pltpu.roll(r_lo, HEAD_DIM, 1), r_hi
        outs.append(jnp.where(lane < HEAD_DIM, lo, hi))
    return jnp.concatenate(outs, axis=1)


def _per_head(shape, axis, tq, values):
    head = lax.broadcasted_iota(jnp.int32, shape, axis) // tq
    out = jnp.zeros(shape, F32)
    for g in range(GROUP):
        out = jnp.where(head == g, values[g], out)
    return out


KEY_CHUNK = 512
Q_TILE = 128
Q_TILE_FWD = 256


def _key_chunks(k_ref, v_ref, n, kc=KEY_CHUNK):
    kc = min(kc, n)
    return [(k_ref[c * kc:(c + 1) * kc, :], v_ref[c * kc:(c + 1) * kc, :], None) for c in range(n // kc)]


def _softmax_fwd(qs, chunks, sink_col):
    logits = []
    for k, _, mask in chunks:
        s = lax.dot_general(qs, k, NT, preferred_element_type=F32)
        logits.append(s if mask is None else jnp.where(mask, s, NEG_BIG))
    m = functools.reduce(jnp.maximum, [jnp.max(s, axis=1, keepdims=True) for s in logits])
    if sink_col is not None:
        m = jnp.maximum(m, sink_col)
    l = jnp.zeros_like(m) if sink_col is None else jnp.exp(sink_col - m)
    acc = jnp.zeros((qs.shape[0], 128), F32)
    for s, (_, v, _) in zip(logits, chunks):
        p = jnp.exp(s - m)
        l = l + jnp.sum(p, axis=1, keepdims=True)
        acc = acc + jnp.dot(p.astype(BF16), v, preferred_element_type=F32)
    return acc / l, m + jnp.log(l)


def _to_rows(col):
    return jnp.transpose(jnp.broadcast_to(col, (col.shape[0], 128)))[0:8, :]


def _softmax_bwd(qs, dos, lse_row, delta_row, chunks):
    dq = jnp.zeros((qs.shape[0], 128), F32)
    grads = []
    for k, v, mask in chunks:
        s = lax.dot_general(k, qs, NT, preferred_element_type=F32)
        if mask is not None:
            s = jnp.where(mask, s, NEG_BIG)
        p = jnp.exp(s - lse_row)
        dp = lax.dot_general(v, dos, NT, preferred_element_type=F32)
        ds = (p * (dp - delta_row)).astype(BF16)
        dv = jnp.dot(p.astype(BF16), dos, preferred_element_type=F32)
        dk = jnp.dot(ds, qs, preferred_element_type=F32)
        dq = dq + lax.dot_general(ds, k, TN, preferred_element_type=F32)
        grads.append((dk, dv))
    return dq, grads


def _band(qi, tq, seq):
    span = tq + 2 * WINDOW
    start = pl.multiple_of(jnp.clip(qi * tq - WINDOW, 0, seq - span), 64)
    return start, span


def _band_mask(qi, tq, start, span, query_axis):
    shape = (GROUP * tq, span) if query_axis == 0 else (span, GROUP * tq)
    qpos = qi * tq + lax.broadcasted_iota(jnp.int32, shape, query_axis) % tq
    kpos = start + lax.broadcasted_iota(jnp.int32, shape, 1 - query_axis)
    return jnp.abs(kpos - qpos) <= WINDOW


def _qkv_specs(rt, tq, q_row, ctx_row, with_latent):
    specs = [pl.BlockSpec((tq, 256), functools.partial(lambda b, i, col: (q_row(b, i), col), col=col)) for col in (0, 1, 3, 4)]
    if with_latent:
        specs += [pl.BlockSpec((rt.seq, 128), functools.partial(lambda b, i, col: (b, col), col=col))
                  for col in (COL_KA, COL_VA, COL_KB, COL_VB)]
    specs += [pl.BlockSpec((rt.ctx, 128), functools.partial(lambda b, i, col: (ctx_row(b), col), col=col))
              for col in (COL_KA, COL_VA, COL_KB, COL_VB)]
    return specs


def _attn_fwd(rt, qkvp, sink, o_prev, name, comm=None):
    latent = o_prev is None
    seq, ctx, nb = rt.seq, rt.ctx, rt.nb
    tq = Q_TILE_FWD if latent else ctx
    tile = Q_TILE if latent else ctx
    parts = tq // tile
    nq = seq // tq if latent else 1
    ctx_blk0 = rt.n_lat // ctx
    q_row = (lambda b, i: b * nq + i) if latent else (lambda b, i: ctx_blk0 + b)

    def store_lse(lse_ref, j, lse_col):
        rows = _to_rows(lse_col)
        for part in range(parts):
            lse_ref[part, j] = jnp.concatenate([rows[:, g * tq + part * tile:g * tq + (part + 1) * tile] for g in range(GROUP)], axis=1)

    def body(sink_ref, qa0, qa1, qb0, qb1, *rest):
        if latent:
            kal, val, kbl, vbl, kac, vac, kbc, vbc, o_ref, lse_ref = rest
        else:
            kac, vac, kbc, vbc, _, o_ref, lse_ref = rest
        qi = pl.program_id(1)
        for kvi, (qa, qb) in enumerate(((qa0, qb0), (qa1, qb1))):
            src_a = _key_chunks(kac, vac, ctx)
            src_b = _key_chunks(kbc, vbc, ctx)
            if latent:
                src_a += _key_chunks(kal, val, seq, seq)
                start, span = _band(qi, tq, seq)
                src_b.append((kbl[pl.ds(start, span), :], vbl[pl.ds(start, span), :], _band_mask(qi, tq, start, span, 0)))
            oa, lse = _softmax_fwd(_stack_heads(qa[...], kvi), src_a, None)
            o_ref[:, kvi * 256:(kvi + 1) * 256] = _unstack_heads(oa, kvi).astype(BF16)
            store_lse(lse_ref, kvi, lse)
            sink_col = _per_head((GROUP * tq, 1), 0, tq, [sink_ref[kvi * GROUP + g] for g in range(GROUP)])
            ob, lse = _softmax_fwd(_stack_heads(qb[...], kvi), src_b, sink_col)
            o_ref[:, 512 + kvi * 256:512 + (kvi + 1) * 256] = _unstack_heads(ob, kvi).astype(BF16)
            store_lse(lse_ref, 2 + kvi, lse)

    specs = _qkv_specs(rt, tq, q_row, lambda b: ctx_blk0 + b, latent)
    args = [sink] + [qkvp] * len(specs)
    in_specs = [pl.BlockSpec(memory_space=pltpu.SMEM)] + specs
    aliases = {}
    if not latent:
        in_specs.append(pl.BlockSpec(memory_space=pl.ANY))
        args.append(o_prev)
        aliases = {len(args) - 1: 0}
    return _comm_call(
        body, comm, name=name, grid=(nb, nq),
        in_specs=in_specs,
        out_specs=[pl.BlockSpec((tq, 1024), lambda b, i: (q_row(b, i), 0)),
                   pl.BlockSpec((parts, 4, 8, GROUP * tile), lambda b, i: (b * nq + i, 0, 0, 0))],
        out_shape=[jax.ShapeDtypeStruct((rt.rows, 1024), BF16), jax.ShapeDtypeStruct((nb * nq * parts, 4, 8, GROUP * tile), F32)],
        args=args, aliases=aliases, semantics=("parallel", "parallel"))


def _attn_bwd(rt, qkvp, o, lse, do, sink, prev, name, comm=None):
    latent = prev is None
    seq, ctx, nb = rt.seq, rt.ctx, rt.nb
    tq = Q_TILE if latent else ctx
    nq = seq // tq if latent else 1
    ctx_blk0 = rt.n_lat // ctx
    q_row = (lambda b, i: b * nq + i) if latent else (lambda b, i: ctx_blk0 + b)
    kc = min(KEY_CHUNK, seq)

    def body(sink_ref, qa0, qa1, qb0, qb1, *rest):
        if latent:
            kal, val, kbl, vbl, kac, vac, kbc, vbc, do_ref, o_ref, lse_ref, dq_ref, dl_ref, dc_ref, dsink_ref = rest
        else:
            kac, vac, kbc, vbc, do_ref, o_ref, lse_ref, c1_ref, _, _, dq_ref, dc_ref, dsink_ref = rest
        b, qi = pl.program_id(0), pl.program_id(1)

        def rows_of(cols, kvi, mixer):
            dos = _stack_heads(do_ref[:, cols], kvi)
            delta = jnp.sum(dos.astype(F32) * _stack_heads(o_ref[:, cols], kvi).astype(F32), axis=1, keepdims=True)
            return dos, lse_ref[0, 2 * mixer + kvi, 0:1, :], _to_rows(delta)[0:1, :]

        @pl.when(jnp.logical_and(b == 0, qi == 0))
        def _():
            dsink_ref[...] = jnp.zeros_like(dsink_ref)

        if latent:
            @pl.when(qi == 0)
            def _():
                dc_ref[...] = jnp.zeros_like(dc_ref)
                dl_ref[...] = jnp.zeros_like(dl_ref)
        else:
            dc_ref[...] = c1_ref[...]

        head_row = lax.broadcasted_iota(jnp.int32, (8, 128), 0)
        for kvi, (qa, qb) in enumerate(((qa0, qb0), (qa1, qb1))):
            cols = slice(kvi * 256, (kvi + 1) * 256)
            dos, lse_row, delta_row = rows_of(cols, kvi, 0)
            src = _key_chunks(kac, vac, ctx)
            if latent:
                src += _key_chunks(kal, val, seq)
            dq4, grads = _softmax_bwd(_stack_heads(qa[...], kvi), dos, lse_row, delta_row, src)
            dq_ref[:, cols] = _unstack_heads(dq4, kvi)
            dc_ref[:, 0:128] += grads[0][0]
            dc_ref[:, 128:256] += grads[0][1]
            for c, (dk, dv) in enumerate(grads[1:]):
                dl_ref[c * kc:(c + 1) * kc, 0:128] += dk
                dl_ref[c * kc:(c + 1) * kc, 128:256] += dv
            cols = slice(512 + kvi * 256, 512 + (kvi + 1) * 256)
            dos, lse_row, delta_row = rows_of(cols, kvi, 1)
            src = _key_chunks(kbc, vbc, ctx)
            if latent:
                start, span = _band(qi, tq, seq)
                src.append((kbl[pl.ds(start, span), :], vbl[pl.ds(start, span), :], _band_mask(qi, tq, start, span, 1)))
            dq4, grads = _softmax_bwd(_stack_heads(qb[...], kvi), dos, lse_row, delta_row, src)
            dq_ref[:, cols] = _unstack_heads(dq4, kvi)
            dc_ref[:, 256:384] += grads[0][0]
            dc_ref[:, 384:512] += grads[0][1]
            if latent:
                dl_ref[pl.ds(start, span), 256:384] += grads[1][0]
                dl_ref[pl.ds(start, span), 384:512] += grads[1][1]
            sink_row = _per_head((1, GROUP * tq), 1, tq, [sink_ref[kvi * GROUP + g] for g in range(GROUP)])
            dsink = -jnp.exp(sink_row - lse_row) * delta_row
            head = lax.broadcasted_iota(jnp.int32, (1, GROUP * tq), 1) // tq
            upd = jnp.zeros((8, 128), F32)
            for g in range(GROUP):
                upd = jnp.where(head_row == kvi * GROUP + g, jnp.sum(jnp.where(head == g, dsink, 0.0)), upd)
            dsink_ref[...] += upd

    specs = _qkv_specs(rt, tq, q_row, lambda b: ctx_blk0 + b, latent)
    q_rows_spec = pl.BlockSpec((tq, 1024), lambda b, i: (q_row(b, i), 0))
    in_specs = ([pl.BlockSpec(memory_space=pltpu.SMEM)] + specs
                + [q_rows_spec, q_rows_spec, pl.BlockSpec((1, 4, 8, GROUP * tq), lambda b, i: (b * nq + i, 0, 0, 0))])
    args = [sink] + [qkvp] * len(specs) + [do, o, lse]
    dq_shape = jax.ShapeDtypeStruct((rt.rows, 1024), F32)
    dkv_shape = jax.ShapeDtypeStruct((rt.rows, 512), F32)
    dsink_spec, dsink_shape = pl.BlockSpec((8, 128), lambda b, i: (0, 0)), jax.ShapeDtypeStruct((8, 128), F32)
    dq_spec = pl.BlockSpec((tq, 1024), lambda b, i: (q_row(b, i), 0))
    if latent:
        out_specs = [dq_spec, pl.BlockSpec((seq, 512), lambda b, i: (b, 0)), pl.BlockSpec((ctx, 512), lambda b, i: (b, 0)), dsink_spec]
        out_shape = [dq_shape, dkv_shape, jax.ShapeDtypeStruct((rt.n_ctx, 512), F32), dsink_shape]
        aliases = {}
    else:
        dq_prev, dkv_prev, c1 = prev
        in_specs += [pl.BlockSpec((ctx, 512), lambda b, i: (b, 0)), pl.BlockSpec(memory_space=pl.ANY), pl.BlockSpec(memory_space=pl.ANY)]
        args += [c1, dq_prev, dkv_prev]
        out_specs = [dq_spec, pl.BlockSpec((ctx, 512), lambda b, i: (ctx_blk0 + b, 0)), dsink_spec]
        out_shape = [dq_shape, dkv_shape, dsink_shape]
        aliases = {len(args) - 2: 0, len(args) - 1: 1}
    return _comm_call(body, comm, name=name, grid=(nb, nq), in_specs=in_specs, out_specs=out_specs, out_shape=out_shape,
                      args=args, aliases=aliases, semantics=("arbitrary", "arbitrary"))


def _silu(x):
    return x / (1.0 + jnp.exp(-x))


def _whole(shape):
    return pl.BlockSpec(shape, lambda i, s: (0,) * len(shape))


def _ada_half_spec(cols):
    return pl.BlockSpec((DEPTH, D_MODEL, cols), lambda i, s: (0, 0, s[0]))


def _ada_fwd(cond, w_ada, b_half, c_idx, name):
    rows = cond.shape[0]
    cols = w_ada.shape[2] // 2

    def body(s_ref, c_ref, w_ref, b_ref, x_ref, o_ref):
        xs = _silu(c_ref[...]).astype(BF16)
        x_ref[...] = xs
        for l in range(DEPTH):
            o_ref[l] = jnp.dot(xs, w_ref[l].astype(BF16), preferred_element_type=F32) + b_ref[l]

    grid_spec = pltpu.PrefetchScalarGridSpec(
        num_scalar_prefetch=1, grid=(1,),
        in_specs=[_whole(cond.shape), _ada_half_spec(cols), _whole(b_half.shape)],
        out_specs=[_whole((rows, D_MODEL)), _whole((DEPTH, rows, cols))])
    return pl.pallas_call(
        body, name=name, grid_spec=grid_spec,
        out_shape=[jax.ShapeDtypeStruct((rows, D_MODEL), BF16), jax.ShapeDtypeStruct((DEPTH, rows, cols), F32)],
        compiler_params=_params(("arbitrary",)),
    )(c_idx, cond, w_ada, b_half)


def _ada_cond_bwd(dcx, w_ada, c_idx, name):
    _, rows, cols = dcx.shape

    def body(s_ref, d_ref, w_ref, o_ref):
        acc = jnp.zeros((rows, D_MODEL), F32)
        for l in range(DEPTH):
            acc = acc + lax.dot_general(d_ref[l], w_ref[l].astype(BF16), NT, preferred_element_type=F32)
        o_ref[...] = acc

    grid_spec = pltpu.PrefetchScalarGridSpec(
        num_scalar_prefetch=1, grid=(1,),
        in_specs=[_whole(dcx.shape), _ada_half_spec(cols)], out_specs=_whole((rows, D_MODEL)))
    return pl.pallas_call(body, name=name, grid_spec=grid_spec, out_shape=jax.ShapeDtypeStruct((rows, D_MODEL), F32),
                          compiler_params=_params(("arbitrary",)))(c_idx, dcx, w_ada)


def _dev_sum(x, name):
    _, r, c = x.shape

    def body(x_ref, o_ref):
        v = x_ref[0]
        for d in range(1, N_DEV):
            v = v + x_ref[d]
        o_ref[...] = v

    return pl.pallas_call(body, name=name, out_shape=jax.ShapeDtypeStruct((r, c), F32))(x)


def _adam_val(w, g, m, v):
    c1 = 1.0 / (1.0 - ADAM_B1 ** ADAM_STEP)
    c2 = 1.0 / (1.0 - ADAM_B2 ** ADAM_STEP)
    nm = ADAM_B1 * m + (1.0 - ADAM_B1) * g
    nv = ADAM_B2 * v + (1.0 - ADAM_B2) * (g * g)
    return -ADAM_LR * ((nm * c1) / (jnp.sqrt(nv * c2) + ADAM_EPS) + ADAM_WD * w), nm, nv


def _small_update(tot, dcc_parts, params, n_groups, name):
    n_p = len(params)
    mod_rows = n_groups * N_MOD
    head_row = DEPTH * mod_rows + 4 * DEPTH

    def body(tot_ref, dcc_ref, *refs):
        ins, outs = refs[:3 * n_p], refs[3 * n_p:]

        def update(p, rows, cols, g):
            w_ref, m_ref, v_ref = ins[3 * p:3 * p + 3]
            g_ref, d_ref, nm_ref, nv_ref = outs[4 * p:4 * p + 4]
            d, nm, nv = _adam_val(w_ref[rows, cols], g, m_ref[rows, cols], v_ref[rows, cols])
            g_ref[rows, cols] = g
            d_ref[rows, cols] = d
            nm_ref[rows, cols] = nm
            nv_ref[rows, cols] = nv

        acc = dcc_ref[0, 0:1, :]
        for d in range(1, N_DEV):
            acc = acc + dcc_ref[d, 0:1, :]
        c = ins[0][...]
        sg = 1.0 / (1.0 + jnp.exp(-c))
        update(0, slice(0, 1), slice(None), acc * (sg * (1.0 + c * (1.0 - sg))))
        for l in range(DEPTH):
            for i in range(N_MOD):
                g = tot_ref[l * mod_rows + i:l * mod_rows + i + 1, :]
                for grp in range(1, n_groups):
                    g = g + tot_ref[l * mod_rows + grp * N_MOD + i:l * mod_rows + grp * N_MOD + i + 1, :]
                update(1, slice(l, l + 1), slice(i * D_MODEL, (i + 1) * D_MODEL), g)
            for j in range(4):
                row = DEPTH * mod_rows + 4 * l + j
                update(2 + j, slice(l, l + 1), slice(None), tot_ref[row:row + 1, :])
            head = tot_ref[head_row + l:head_row + l + 1, :]
            update(6, slice(l, l + 1), slice(None), head[:, 0:HEAD_DIM] + head[:, HEAD_DIM:2 * HEAD_DIM])
            update(7, slice(l, l + 1), slice(None), head[:, 2 * HEAD_DIM:3 * HEAD_DIM] + head[:, 3 * HEAD_DIM:4 * HEAD_DIM])
            update(8, slice(l, l + 1), slice(None), head[:, 4 * HEAD_DIM:4 * HEAD_DIM + ins[3 * 8].shape[1]])

    shapes = [jax.ShapeDtypeStruct(w.shape, F32) for w, _, _ in params for _ in range(4)]
    outs = pl.pallas_call(body, name=name, out_shape=shapes)(tot, dcc_parts, *[a for p in params for a in p])
    return [tuple(outs[4 * p:4 * p + 4]) for p in range(n_p)]


def _adamw(w, g, m, v, name):
    r, c = w.shape
    tr = _pick(r, (256, 128, 64, 32, 24, 16, 8))

    def body(w_ref, g_ref, m_ref, v_ref, d_ref, nm_ref, nv_ref):
        d_ref[...], nm_ref[...], nv_ref[...] = _adam_val(w_ref[...], g_ref[...], m_ref[...], v_ref[...])

    spec = pl.BlockSpec((tr, c), lambda i: (i, 0))
    return pl.pallas_call(body, name=name, grid=(r // tr,), in_specs=[spec] * 4, out_specs=[spec] * 3,
                          out_shape=[jax.ShapeDtypeStruct((r, c), F32)] * 3, compiler_params=_params(("parallel",)))(w, g, m, v)


def _adamw_shard(kind, l, w, m, v, halves, off, prev, name):
    h = PACK_HEIGHT[kind]
    assert off % h == 0, (kind, off)
    _, r, c = w.shape
    rows = r // 2

    def body(w_ref, m_ref, v_ref, p_ref, *rest):
        g_ref, d_ref, nm_ref, nv_ref = rest[-4:]
        if kind == "in":
            for t in range(2):
                g = p_ref[:, t * IN_PIECE_COLS:(t + 1) * IN_PIECE_COLS]
                rs = slice(t * h, (t + 1) * h)
                g_ref[rs, :] = g
                d_ref[rs, :], nm_ref[rs, :], nv_ref[rs, :] = _adam_val(w_ref[rs, :], g, m_ref[rs, :], v_ref[rs, :])
        else:
            g = p_ref[...]
            g_ref[...] = g
            d_ref[...], nm_ref[...], nv_ref[...] = _adam_val(w_ref[...], g, m_ref[...], v_ref[...])

    blk = pl.BlockSpec((None, rows, c), lambda half: (l, half, 0))
    in_specs = [blk, blk, blk, pl.BlockSpec((None, h, 1024), lambda half: (half, off // h, 0))]
    args = [w, m, v, halves]
    aliases = {}
    if prev is not None:
        in_specs += [pl.BlockSpec(memory_space=pl.ANY)] * 4
        args += list(prev)
        aliases = {4 + j: j for j in range(4)}
    return pl.pallas_call(
        body, name=name, grid=(2,), in_specs=in_specs, out_specs=[blk] * 4,
        out_shape=[jax.ShapeDtypeStruct(w.shape, F32)] * 4, input_output_aliases=aliases,
        compiler_params=_params(("parallel",)))(*args)


SMALL_ROWS = 48


def _small_rows(small, sq):
    def lane_pad(v):
        return jnp.pad(v, (0, D_MODEL - v.shape[0]))[None]

    head_rows = [lane_pad(jnp.concatenate([s["q_norm"][0], s["k_norm"][0], s["sink"]])) for s in small]
    loss_row = lane_pad((0.5 / D_MODEL) * jnp.sum(sq, keepdims=True)[0])
    rows = jnp.concatenate([s["mod"].reshape(-1, D_MODEL) for s in small] + [s["gammas"] for s in small] + head_rows + [loss_row], axis=0)
    return jnp.pad(rows, ((0, SMALL_ROWS - rows.shape[0]), (0, 0)))


def _local_step(x, ctx, target, mods, gam, qn, kn, sink, w_first, w_layers, packed, kc_idx):
    nb, seq, _ = x.shape
    rt = _Rows(nb, seq, ctx.shape[1])
    rt_lat = rt.latent_only()
    tables = _rope_tables(rt)
    fuse = packed is not None
    h = (x.reshape(rt.n_lat, D_MODEL), ctx.reshape(rt.n_ctx, D_MODEL))
    wg = [{}, {}] if fuse else [dict(w) for w in w_layers]
    wg[0]["in"] = (w_first, 0)
    if fuse:
        wg[0]["in_own"] = (packed, W_FIRST[0])
    saved = []
    for l in range(DEPTH):
        g_pre_mix, g_post_mix, g_pre_mlp, g_post_mlp = gam[l]
        if l == 0:
            u, qkv, qkvp, h = _in_fwd(rt, h, g_pre_mix, mods[l], wg[l], tables, qn[l], kn[l], f"in_fwd{l}")
        else:
            u, qkv, qkvp = _in_fwd(rt, h, g_pre_mix, mods[l], wg[l], tables, qn[l], kn[l], f"in_fwd{l}")
        if fuse and l == 0:
            o, lse_lat, w_mlp0, w_out0, w_in1 = _attn_fwd(rt, qkvp, sink[l], None, f"attn_lat_fwd{l}",
                                                         comm=_gather_comm(packed, [W_MLP0, W_OUT0, W_IN1], lead=2))
            wg[0].update({kind: (w_mlp0, PACK_OFF[(kind, 0)] - W_MLP0[0]) for kind in ("up", "down")})
            wg[0]["out"] = (w_out0, 0)
            wg[1] = {"in": (w_in1, 0)}
        elif fuse:
            o, lse_lat, w_mlp1, w_out1 = _attn_fwd(rt, qkvp, sink[l], None, f"attn_lat_fwd{l}",
                                                   comm=_gather_comm(packed, [W_MLP1, W_OUT1], lead=2))
            wg[1].update({kind: (w_mlp1, PACK_OFF[(kind, 1)] - W_MLP1[0]) for kind in ("up", "down")})
            wg[1]["out"] = (w_out1, 0)
        else:
            o, lse_lat = _attn_fwd(rt, qkvp, sink[l], None, f"attn_lat_fwd{l}")
        if l < DEPTH - 1:
            o, lse_ctx = _attn_fwd(rt, qkvp, sink[l], o, f"attn_ctx_fwd{l}")
            mix, h1, u2 = _out_fwd(rt, o, wg[l], h, mods[l], g_post_mix, g_pre_mlp, f"out_fwd{l}")
            r, y, h2 = _mlp_fwd(rt, u2, h1, wg[l], mods[l], g_post_mlp, f"mlp_fwd{l}")
        else:
            lse_ctx = None
            mix, h1, u2 = _out_fwd(rt_lat, o, wg[l], h, mods[l], g_post_mix, g_pre_mlp, f"out_fwd{l}")
            r, y, dh, sq = _mlp_fwd(rt_lat, u2, h1, wg[l], mods[l], g_post_mlp, f"mlp_fwd{l}", target=target.reshape(rt.n_lat, D_MODEL))
        saved.append((h, u, qkv, qkvp, o, lse_lat, lse_ctx, mix, h1, u2, r, y))
        h = h2

    small = [None] * DEPTH
    groups = {}
    for l in reversed(range(DEPTH)):
        g_pre_mix, g_post_mix, g_pre_mlp, g_post_mlp = gam[l]
        h0, u, qkv, qkvp, o, lse_lat, lse_ctx, mix, h1, u2, r, y = saved[l]
        mlp_group, mix_group = (G_LAYER1, G_LAYER1) if l == 1 else (G_MLP0, G_MIX0)
        hide = fuse and l == 0

        dead_ctx = l == DEPTH - 1
        rt_b = rt_lat if dead_ctx else rt
        dy, da, d_gate_m, d_g_post_mlp = _mlp_down_bwd(rt_b, dh, y, r, wg[l], mods[l], g_post_mlp, f"mlp_down_bwd{l}")
        p_mlp = _wgrad_packed(rt_b, r, dy, "down", PACK_OFF[("down", l)] - mlp_group[0], mlp_group[1], None, f"mlp_down_wgrad{l}",
                              comm=_pair_comm(groups[G_LAYER1]) if hide else None)
        if hide:
            p_mlp, r1 = p_mlp
            sum1 = _pair_sum(groups[G_LAYER1], r1, kc_idx, "grad_pair_sum_layer1")
        p_mlp = _wgrad_packed(rt_b, u2, da, "up", PACK_OFF[("up", l)] - mlp_group[0], mlp_group[1], p_mlp, f"mlp_up_wgrad{l}")
        outs = _mlp_up_bwd(rt_b, da, wg[l], h1, dh, mods[l], g_pre_mlp, f"mlp_up_bwd{l}", comm=_pair_comm(p_mlp) if hide else None)
        dh1, d_sh_m, d_sc_m, d_g_pre_mlp = outs[:4]
        if hide:
            sum0 = _pair_sum(p_mlp, outs[4], kc_idx, "grad_pair_sum_mlp0")
        dmix, do, d_gate_a, d_g_post_mix = _out_bwd(rt_b, dh1, mix, wg[l], mods[l], g_post_mix, f"out_bwd{l}")
        p_mix = _wgrad_packed(rt_b, o, dmix, "out", PACK_OFF[("out", l)] - mix_group[0], mix_group[1],
                              p_mlp if l == 1 else None, f"out_wgrad{l}")
        outs = _attn_bwd(rt, qkvp, o, lse_lat, do, sink[l], None, f"attn_lat_bwd{l}",
                         comm=_chip_comm([sum1[1], sum0[1]]) if hide else None)
        dq, dkv, dkv_c, dsink1 = outs[:4]
        if hide:
            groups[G_LAYER1] = _owner_sum(sum1[0], outs[4], kc_idx, "grad_owner_sum_layer1")
            groups[G_MLP0] = _owner_sum(sum0[0], outs[5], kc_idx, "grad_owner_sum_mlp0")
        if dead_ctx:
            dsink2 = jnp.zeros_like(dsink1)
            d_gate_m, d_sh_m, d_sc_m, d_gate_a = [a.at[nb].set(0.0) for a in (d_gate_m, d_sh_m, d_sc_m, d_gate_a)]
        else:
            dq, dkv, dsink2 = _attn_bwd(rt, qkvp, o, lse_ctx, do, sink[l], (dq, dkv, dkv_c), f"attn_ctx_bwd{l}")
        dqkv, dh, dqn, dkn, d_sh_a, d_sc_a, d_g_pre_mix = _in_bwd(rt, dq, dkv, qkv, tables, qn[l], kn[l], wg[l], h0, dh1, mods[l],
                                                                  g_pre_mix, l == 0, f"in_bwd{l}",
                                                                  dead_ctx_dkv=dkv_c if dead_ctx else None)
        dmod = jnp.concatenate([d_sh_a, d_sc_a, d_gate_a, d_sh_m, d_sc_m, d_gate_m], axis=1)
        small[l] = dict(mod=dmod, gammas=jnp.concatenate([d_g_pre_mix, d_g_post_mix, d_g_pre_mlp, d_g_post_mlp], axis=0),
                        q_norm=dqn, k_norm=dkn, sink=(dsink1 + dsink2)[:, 0])
        tail = _merge([_gather_comm(_small_rows(small, sq), [(0, SMALL_ROWS)]),
                       _halves_comm([groups[G_LAYER1], groups[G_MLP0]])]) if hide else None
        outs = _wgrad_packed(rt, u, dqkv, "in", PACK_OFF[("in", l)] - mix_group[0], mix_group[1], p_mix, f"in_wgrad{l}", comm=tail)
        if hide:
            groups[mix_group], small_g, groups[G_LAYER1], groups[G_MLP0] = outs
        else:
            groups[mix_group], small_g = outs, None
            if l == 0:
                groups[G_MLP0] = p_mlp
    return sq, dh.reshape(nb, seq, D_MODEL), [groups[G_LAYER1], groups[G_MLP0], groups[G_MIX0]], small, small_g


def kernel(x, c, ctx, c_ctx, w_ada, b_ada, g_pre_mix, g_post_mix, g_pre_mlp, g_post_mlp, w_in, q_norm, k_norm, sink, w_out, w_up, w_down, loss_target, m_c_ctx, m_w_ada, m_b_ada, m_g_pre_mix, m_g_post_mix, m_g_pre_mlp, m_g_post_mlp, m_w_in, m_q_norm, m_k_norm, m_sink, m_w_out, m_w_up, m_w_down, v_c_ctx, v_w_ada, v_b_ada, v_g_pre_mix, v_g_post_mix, v_g_pre_mlp, v_g_post_mlp, v_w_in, v_q_norm, v_k_norm, v_sink, v_w_out, v_w_up, v_w_down):
    nb = x.shape[0]
    ix, iy, ic = lax.axis_index("x"), lax.axis_index("y"), lax.axis_index("c")
    chip = 2 * ix + iy
    dev = 2 * chip + ic
    ada_cols = w_ada.shape[2] // 2

    c_rows = c.reshape(8, (nb * D_MODEL) // 8)
    packed, c_all = _pack_local_half(w_in, w_out, w_up, w_down, _gather_comm(c_rows, [(0, c_rows.shape[0])]), "pack_gather_c")
    c_all = c_all.reshape(N_DEV * nb, D_MODEL)
    n_cond = N_DEV * nb + 1
    cond_rows = 16 * ((n_cond + 15) // 16)
    cond = jnp.concatenate([c_all, c_ctx[None, :], jnp.zeros((cond_rows - n_cond, D_MODEL), F32)], axis=0)
    c_idx = ic.reshape(1).astype(jnp.int32)
    kc_idx = jnp.stack([chip, ic]).astype(jnp.int32)
    b_ada_half = lax.dynamic_slice_in_dim(b_ada, dev * ada_cols, ada_cols, 1)[:, None, :]
    x_ada, mod_part = _ada_fwd(cond, w_ada, b_ada_half, c_idx, "ada_fwd")
    mod_rows2d = mod_part.reshape(DEPTH * cond_rows, ada_cols)
    mod_g, w_first = _comm_alone(_merge([_gather_comm(mod_rows2d, [(0, mod_rows2d.shape[0])]),
                                         _gather_comm(packed, [W_FIRST], copy_own=False)]), "gather_mod_w_first")
    mod_all = mod_g.reshape(N_DEV, DEPTH, cond_rows, ada_cols).transpose(1, 2, 0, 3).reshape(DEPTH, cond_rows, N_MOD * D_MODEL)
    mods = []
    for l in range(DEPTH):
        mine = lax.dynamic_slice_in_dim(mod_all[l], dev * nb, nb, 0)
        mods.append(jnp.concatenate([mine, mod_all[l, n_cond - 1:n_cond]], axis=0).reshape(nb + 1, N_MOD, D_MODEL))

    gam = [(g_pre_mix[l][None], g_post_mix[l][None], g_pre_mlp[l][None], g_post_mlp[l][None]) for l in range(DEPTH)]
    qn = [jnp.tile(q_norm[l], 2)[None] for l in range(DEPTH)]
    kn = [jnp.tile(k_norm[l], 2)[None] for l in range(DEPTH)]
    _, grad_x, (h_layer1, h_mlp0, p_mix0), _, small_g = _local_step(x, ctx, loss_target, mods, gam, qn, kn, [sink[l] for l in range(DEPTH)],
                                                                 w_first, None, packed, kc_idx)

    def step(w, g, m, v, name):
        shape = w.shape
        cols = shape[-1]
        outs = _adamw(w.reshape(-1, cols), g.reshape(-1, cols), m.reshape(-1, cols), v.reshape(-1, cols), name)
        return tuple(a.reshape(shape) for a in outs)

    def shard_update(kind, w, m, v, layer0, layer1):
        outs = None
        for l, (halves, group) in enumerate((layer0, layer1)):
            outs = _adamw_shard(kind, l, w, m, v, halves, PACK_OFF[(kind, l)] - group[0], outs, f"adamw_w_{kind}{l}")
        return tuple(outs)

    tot = _dev_sum(small_g, "small_sum")
    mod_rows = (nb + 1) * N_MOD
    loss = tot[DEPTH * mod_rows + 4 * DEPTH + DEPTH, 0]

    ex = small_g[:, :DEPTH * mod_rows].reshape(N_DEV, DEPTH, nb + 1, N_MOD * D_MODEL)[:, :, :nb]
    ex = ex.transpose(1, 0, 2, 3).reshape(DEPTH, N_DEV * nb, N_MOD * D_MODEL)
    cx = tot[:DEPTH * mod_rows].reshape(DEPTH, nb + 1, N_MOD * D_MODEL)[:, nb:]
    dm = jnp.concatenate([ex, cx, jnp.zeros((DEPTH, cond_rows - n_cond, N_MOD * D_MODEL), F32)], axis=1)
    shard_cols = w_ada.shape[2]
    grad_w_ada = _ada_wgrad(x_ada, lax.dynamic_slice_in_dim(dm, chip * shard_cols, shard_cols, 2).astype(BF16), "ada_wgrad")
    dcx = jnp.pad(lax.dynamic_slice_in_dim(cx, dev * ada_cols, ada_cols, 2), ((0, 0), (0, 15), (0, 0))).astype(BF16)
    dcc = _ada_cond_bwd(dcx, w_ada, c_idx, "ada_cond_bwd")[0:8]

    r1, = _comm_alone(_pair_comm(p_mix0), "grad_pair_exchange_mix0")
    a32, a16 = _pair_sum(p_mix0, r1, kc_idx, "grad_pair_sum_mix0")
    r2, dcc_g = _comm_alone(_merge([_chip_comm([a16]), _gather_comm(dcc, [(0, dcc.shape[0])])]), "grad_chip_exchange_mix0")
    h_mix0 = _owner_sum(a32, r2, kc_idx, "grad_owner_sum_mix0")
    h_mix0, = _comm_alone(_halves_comm([h_mix0]), "grad_halves_exchange_mix0")

    small_names = ["c_ctx", "b_ada", "g_pre_mix", "g_post_mix", "g_pre_mlp", "g_post_mlp", "q_norm", "k_norm", "sink"]
    assert q_norm.shape[1] == HEAD_DIM and k_norm.shape[1] == HEAD_DIM
    small_res = _small_update(tot, dcc_g, [(c_ctx[None], m_c_ctx[None], v_c_ctx[None]), (b_ada, m_b_ada, v_b_ada),
                                           (g_pre_mix, m_g_pre_mix, v_g_pre_mix), (g_post_mix, m_g_post_mix, v_g_post_mix),
                                           (g_pre_mlp, m_g_pre_mlp, v_g_pre_mlp), (g_post_mlp, m_g_post_mlp, v_g_post_mlp),
                                           (q_norm, m_q_norm, v_q_norm), (k_norm, m_k_norm, v_k_norm), (sink, m_sink, v_sink)],
                              nb + 1, "small_update")
    res = {n: r for n, r in zip(small_names, small_res)}
    res["c_ctx"] = tuple(a[0] for a in res["c_ctx"])
    res["w_ada"] = (grad_w_ada, *step(w_ada, grad_w_ada, m_w_ada, v_w_ada, "adamw_w_ada"))
    res["w_up"] = shard_update("up", w_up, m_w_up, v_w_up, (h_mlp0, G_MLP0), (h_layer1, G_LAYER1))
    res["w_down"] = shard_update("down", w_down, m_w_down, v_w_down, (h_mlp0, G_MLP0), (h_layer1, G_LAYER1))
    res["w_in"] = shard_update("in", w_in, m_w_in, v_w_in, (h_mix0, G_MIX0), (h_layer1, G_LAYER1))
    res["w_out"] = shard_update("out", w_out, m_w_out, v_w_out, (h_mix0, G_MIX0), (h_layer1, G_LAYER1))

    order = ["c_ctx", "w_ada", "b_ada", "g_pre_mix", "g_post_mix", "g_pre_mlp", "g_post_mlp", "w_in", "q_norm", "k_norm", "sink", "w_out", "w_up", "w_down"]
    return (loss, grad_x, *[res[n][0] for n in order], *[res[n][1] for n in order],
            *[res[n][2] for n in order], *[res[n][3] for n in order])
```

```python
import functools

import jax
import jax.numpy as jnp
import numpy as np
from jax import lax
from jax.experimental import pallas as pl
from jax.experimental.pallas import tpu as pltpu

F32 = jnp.float32
BF16 = jnp.bfloat16

D_MODEL = 1024
HEAD_DIM = 64
GROUP = 4
WINDOW = 128
N_MOD = 6
D_FF = 4 * D_MODEL
IN_COLS = 1536
GRID_W = 64
ROPE_THETA = 10000.0
EPS = 1e-6
NEG_BIG = -1e30
Q_SCALE = HEAD_DIM ** -0.5
DEPTH = 2
N_DEV = 8

ADAM_LR = 0.001
ADAM_B1 = 0.9
ADAM_B2 = 0.999
ADAM_EPS = 1e-08
ADAM_WD = 0.01
ADAM_STEP = 10

V7X_VMEM_BYTES = 64 * 1024 * 1024
VMEM_LIMIT = V7X_VMEM_BYTES - 8 * 1024 * 1024

MESH = pl.DeviceIdType.MESH
NT = (((1,), (1,)), ((), ()))
TN = (((0,), (0,)), ((), ()))

COL_KA, COL_VA, COL_KB, COL_VB = 4, 5, 10, 11
NORMED_COLS = 640

PACK_HEIGHT = {"up": 512, "down": 512, "in": 256, "out": 128}
IN_PIECE_COLS = 384
PACK_OFF = {("up", 0): 0, ("down", 0): 512, ("in", 0): 1024, ("out", 0): 1280,
            ("up", 1): 1408, ("down", 1): 1920, ("in", 1): 2432, ("out", 1): 2688}
PACK_ROWS = 2816
W_FIRST, W_MLP0, W_OUT0, W_IN1, W_MLP1, W_OUT1 = (1024, 256), (0, 1024), (1280, 128), (2432, 256), (1408, 1024), (2688, 128)
G_LAYER1, G_MLP0, G_MIX0 = (1408, 1408), (0, 1024), (1024, 384)


def _pick(n, cands):
    for t in cands:
        if n % t == 0:
            return t
    raise ValueError(f"no tile for {n}")


def _params(sem):
    return pltpu.CompilerParams(dimension_semantics=sem, vmem_limit_bytes=VMEM_LIMIT)


class _Comm:
    def __init__(self, inputs, out_shapes, aliases, n_send, n_recv, start, finish, relay=None, lead=0):
        self.inputs, self.out_shapes, self.aliases = list(inputs), list(out_shapes), dict(aliases)
        self.n_send, self.n_recv, self.start, self.finish, self.relay, self.lead = n_send, n_recv, start, finish, relay, lead


def _comm_call(compute, comm, *, name, grid, in_specs, out_specs, out_shape, args, aliases, semantics, scratch=()):
    in_specs, out_specs, out_shape, args, aliases = list(in_specs), list(out_specs), list(out_shape), list(args), dict(aliases)
    scratch = list(scratch)
    if comm is None:
        return pl.pallas_call(compute, name=name, grid=grid, in_specs=in_specs, out_specs=out_specs, out_shape=out_shape,
                              input_output_aliases=aliases, scratch_shapes=scratch, compiler_params=_params(semantics))(*args)
    n_in, n_out, n_ci, n_co = len(args), len(out_shape), len(comm.inputs), len(comm.out_shapes)
    hbm = pl.BlockSpec(memory_space=pl.ANY)
    aliases.update({n_in + i: n_out + o for i, o in comm.aliases.items()})

    def body(*refs):
        ins, c_ins = refs[:n_in], refs[n_in:n_in + n_ci]
        outs, c_outs = refs[n_in + n_ci:n_in + n_ci + n_out], refs[n_in + n_ci + n_out:n_in + n_ci + n_out + n_co]
        scr = refs[n_in + n_ci + n_out + n_co:-2]
        send_sems, recv_sems = refs[-2:]
        ids = [pl.program_id(a) for a in range(len(grid))]
        first = functools.reduce(jnp.logical_and, [i == 0 for i in ids])
        last = functools.reduce(jnp.logical_and, [i == g - 1 for i, g in zip(ids, grid)])

        @pl.when(first)
        def _():
            comm.start(c_ins, c_outs, send_sems, recv_sems)

        compute(*ins, *outs, *scr)

        if comm.relay is not None:
            step = functools.reduce(lambda acc, ig: acc * ig[1] + ig[0], zip(ids, grid), 0)

            @pl.when(step == int(np.prod(grid)) - 1 - comm.lead)
            def _():
                comm.relay(c_ins, c_outs, send_sems, recv_sems)

        @pl.when(last)
        def _():
            comm.finish(c_ins, c_outs, send_sems, recv_sems)

    return pl.pallas_call(
        body, name=name, grid=grid,
        in_specs=in_specs + [hbm] * n_ci, out_specs=out_specs + [hbm] * n_co, out_shape=out_shape + comm.out_shapes,
        input_output_aliases=aliases,
        scratch_shapes=scratch + [pltpu.SemaphoreType.DMA((comm.n_send,)), pltpu.SemaphoreType.DMA((comm.n_recv,))],
        compiler_params=_params(("arbitrary",) * len(grid)),
    )(*args, *comm.inputs)


def _place():
    x_, y_, c_ = lax.axis_index("x"), lax.axis_index("y"), lax.axis_index("c")
    return x_, y_, c_, [(1 - x_, y_), (x_, 1 - y_), (1 - x_, 1 - y_)]


GATHER_SENDS, GATHER_RECVS = 8, 7


def _gather_copies(packed_ref, wg_ref, send_sems, recv_sems, rows, nth=0):
    r0, n = rows
    x_, y_, c_, chips = _place()
    me, sibling = (x_, y_, c_), (x_, y_, 1 - c_)
    src = packed_ref.at[pl.ds(r0, n), pl.ds(0, wg_ref.shape[2])]

    def slot(px, py, pc):
        return wg_ref.at[4 * px + 2 * py + pc]

    def copy(k, block, to, from_packed=False):
        return pltpu.make_async_remote_copy(src_ref=src if from_packed else slot(*block), dst_ref=slot(*block),
                                            send_sem=send_sems.at[GATHER_SENDS * nth + k], recv_sem=recv_sems.at[GATHER_RECVS * nth + k],
                                            device_id=to, device_id_type=MESH)

    own = [copy(0, me, sibling, True)] + [copy(1 + j, me, (*chip, c_), True) for j, chip in enumerate(chips)]
    passed = [copy(4 + j, (*chip, c_), sibling) for j, chip in enumerate(chips)]
    over_ici = [copy(1 + j, (*chip, c_), me) for j, chip in enumerate(chips)]
    from_sibling = [copy(0, sibling, me)] + [copy(4 + j, (*chip, 1 - c_), me) for j, chip in enumerate(chips)]
    mine = pltpu.make_async_copy(src, slot(*me), send_sems.at[GATHER_SENDS * nth + 7])
    return mine, own, passed, over_ici, from_sibling


def _gather_start(packed_ref, wg_ref, send_sems, recv_sems, rows, nth=0, copy_own=True):
    mine, own, _, _, _ = _gather_copies(packed_ref, wg_ref, send_sems, recv_sems, rows, nth)
    if copy_own:
        mine.start()
    for cp in own:
        cp.start()


def _gather_relay(packed_ref, wg_ref, send_sems, recv_sems, rows, nth=0):
    _, _, passed, over_ici, _ = _gather_copies(packed_ref, wg_ref, send_sems, recv_sems, rows, nth)
    for arrived, onward in zip(over_ici, passed):
        arrived.wait_recv()
        onward.start()


def _gather_finish(packed_ref, wg_ref, send_sems, recv_sems, rows, nth=0, copy_own=True):
    mine, own, passed, _, from_sibling = _gather_copies(packed_ref, wg_ref, send_sems, recv_sems, rows, nth)
    for arrived in from_sibling:
        arrived.wait_recv()
    for cp in own + passed:
        cp.wait_send()
    if copy_own:
        mine.wait()


def _gather_comm(packed, ranges, copy_own=True, lead=0, cols=None):
    shapes = [jax.ShapeDtypeStruct((N_DEV, n, cols or packed.shape[1]), packed.dtype) for _, n in ranges]

    def start(ins, outs, ss, rs):
        for nth, rows in enumerate(ranges):
            _gather_start(ins[0], outs[nth], ss, rs, rows, nth, copy_own)

    def relay(ins, outs, ss, rs):
        for nth, rows in enumerate(ranges):
            _gather_relay(ins[0], outs[nth], ss, rs, rows, nth)

    def finish(ins, outs, ss, rs):
        for nth, rows in enumerate(ranges):
            _gather_finish(ins[0], outs[nth], ss, rs, rows, nth, copy_own)

    return _Comm([packed], shapes, {}, GATHER_SENDS * len(ranges), GATHER_RECVS * len(ranges), start, finish, relay, lead)


def _pair_copy(p_ref, out_ref, send_sems, recv_sems):
    x_, y_, c_, _ = _place()
    return pltpu.make_async_remote_copy(src_ref=p_ref.at[1 - c_], dst_ref=out_ref,
                                        send_sem=send_sems.at[0], recv_sem=recv_sems.at[0],
                                        device_id=(x_, y_, 1 - c_), device_id_type=MESH)


def _pair_comm(p):
    return _Comm([p], [jax.ShapeDtypeStruct(p.shape[1:], p.dtype)], {}, 1, 1,
                 lambda ins, outs, ss, rs: _pair_copy(ins[0], outs[0], ss, rs).start(),
                 lambda ins, outs, ss, rs: _pair_copy(ins[0], outs[0], ss, rs).wait())


def _chip_copies(a_refs, out_refs, send_sems, recv_sems):
    _, _, c_, chips = _place()
    return [pltpu.make_async_remote_copy(src_ref=a_ref.at[2 * tx + ty], dst_ref=o_ref.at[j],
                                         send_sem=send_sems.at[3 * g + j], recv_sem=recv_sems.at[3 * g + j],
                                         device_id=(tx, ty, c_), device_id_type=MESH)
            for g, (a_ref, o_ref) in enumerate(zip(a_refs, out_refs)) for j, (tx, ty) in enumerate(chips)]


def _chip_start(a_refs, out_refs, send_sems, recv_sems):
    for cp in _chip_copies(a_refs, out_refs, send_sems, recv_sems):
        cp.start()


def _chip_finish(a_refs, out_refs, send_sems, recv_sems):
    for cp in _chip_copies(a_refs, out_refs, send_sems, recv_sems):
        cp.wait()


def _chip_comm(arrays):
    shapes = [jax.ShapeDtypeStruct((3,) + a.shape[1:], a.dtype) for a in arrays]
    return _Comm(arrays, shapes, {}, 3 * len(arrays), 3 * len(arrays), _chip_start, _chip_finish)


def _halves_copies(in_refs, out_refs, send_sems, recv_sems):
    x_, y_, c_, _ = _place()
    return [pltpu.make_async_remote_copy(src_ref=o_ref.at[c_], dst_ref=o_ref.at[c_], send_sem=send_sems.at[i], recv_sem=recv_sems.at[i],
                                         device_id=(x_, y_, 1 - c_), device_id_type=MESH)
            for i, o_ref in enumerate(out_refs)]


def _halves_start(in_refs, out_refs, send_sems, recv_sems):
    for cp in _halves_copies(in_refs, out_refs, send_sems, recv_sems):
        cp.start()


def _halves_finish(in_refs, out_refs, send_sems, recv_sems):
    for cp in _halves_copies(in_refs, out_refs, send_sems, recv_sems):
        cp.wait()


def _halves_comm(arrays):
    shapes = [jax.ShapeDtypeStruct(a.shape, a.dtype) for a in arrays]
    return _Comm(arrays, shapes, {i: i for i in range(len(arrays))}, len(arrays), len(arrays), _halves_start, _halves_finish)


class _SemSlice:
    class _At:
        def __init__(self, sems, first):
            self.sems, self.first = sems, first

        def __getitem__(self, k):
            return self.sems.at[self.first + k]

    def __init__(self, sems, first):
        self.at = _SemSlice._At(sems, first)


def _merge(comms):
    inputs = [a for c in comms for a in c.inputs]
    shapes = [s for c in comms for s in c.out_shapes]
    aliases, spans = {}, []
    i0 = o0 = s0 = r0 = 0
    for c in comms:
        aliases.update({i0 + i: o0 + o for i, o in c.aliases.items()})
        spans.append((slice(i0, i0 + len(c.inputs)), slice(o0, o0 + len(c.out_shapes)), s0, r0))
        i0, o0, s0, r0 = i0 + len(c.inputs), o0 + len(c.out_shapes), s0 + c.n_send, r0 + c.n_recv

    def start(ins, outs, ss, rs):
        for c, (i, o, s, r) in zip(comms, spans):
            c.start(ins[i], outs[o], _SemSlice(ss, s), _SemSlice(rs, r))

    def finish(ins, outs, ss, rs):
        for c, (i, o, s, r) in zip(comms, spans):
            if c.relay is not None:
                c.relay(ins[i], outs[o], _SemSlice(ss, s), _SemSlice(rs, r))
            c.finish(ins[i], outs[o], _SemSlice(ss, s), _SemSlice(rs, r))

    return _Comm(inputs, shapes, aliases, s0, r0, start, finish)


def _comm_alone(comm, name):
    n_ci = len(comm.inputs)
    hbm = pl.BlockSpec(memory_space=pl.ANY)

    def body(*refs):
        c_ins, c_outs, send_sems, recv_sems = refs[:n_ci], refs[n_ci:-2], refs[-2], refs[-1]
        comm.start(c_ins, c_outs, send_sems, recv_sems)
        if comm.relay is not None:
            comm.relay(c_ins, c_outs, send_sems, recv_sems)
        comm.finish(c_ins, c_outs, send_sems, recv_sems)

    return pl.pallas_call(
        body, name=name, out_shape=comm.out_shapes, in_specs=[hbm] * n_ci, out_specs=[hbm] * len(comm.out_shapes),
        input_output_aliases=comm.aliases,
        scratch_shapes=[pltpu.SemaphoreType.DMA((comm.n_send,)), pltpu.SemaphoreType.DMA((comm.n_recv,))],
    )(*comm.inputs)


SUM_TILES = (704, 512, 384, 320, 256, 192, 128, 64)


def _pair_sum(p, r1, kc_idx, name):
    _, _, n, c = p.shape
    tr = _pick(n, SUM_TILES)

    def body(s_ref, p_ref, r_ref, o32_ref, o16_ref):
        v = p_ref[...] + r_ref[...]
        o16_ref[...] = v.astype(BF16)

        @pl.when(pl.program_id(1) == s_ref[0])
        def _():
            o32_ref[...] = v

    blk = pl.BlockSpec((None, tr, c), lambda i, j, s: (j, i, 0))
    grid_spec = pltpu.PrefetchScalarGridSpec(
        num_scalar_prefetch=1, grid=(n // tr, 4),
        in_specs=[pl.BlockSpec((None, None, tr, c), lambda i, j, s: (s[1], j, i, 0)), blk],
        out_specs=[pl.BlockSpec((tr, c), lambda i, j, s: (i, 0)), blk])
    return pl.pallas_call(
        body, name=name, grid_spec=grid_spec,
        out_shape=[jax.ShapeDtypeStruct((n, c), F32), jax.ShapeDtypeStruct((4, n, c), BF16)],
        compiler_params=_params(("arbitrary", "arbitrary")),
    )(kc_idx, p, r1)


def _owner_sum(a32, r2, kc_idx, name):
    r, c = a32.shape
    tr = _pick(r, SUM_TILES)

    def body(s_ref, a_ref, r_ref, o_ref):
        v = a_ref[...]
        for j in range(3):
            v = v + r_ref[j].astype(F32)
        o_ref[...] = v

    grid_spec = pltpu.PrefetchScalarGridSpec(
        num_scalar_prefetch=1, grid=(r // tr,),
        in_specs=[pl.BlockSpec((tr, c), lambda i, s: (i, 0)),
                  pl.BlockSpec((3, tr, c), lambda i, s: (0, i, 0))],
        out_specs=pl.BlockSpec((None, tr, c), lambda i, s: (s[1], i, 0)))
    return pl.pallas_call(
        body, name=name, grid_spec=grid_spec,
        out_shape=jax.ShapeDtypeStruct((2, r, c), F32),
        compiler_params=_params(("arbitrary",)),
    )(kc_idx, a32, r2)


def _pack_local_half(w_in_s, w_out_s, w_up_s, w_down_s, comm, name):
    shards = {"in": w_in_s, "out": w_out_s, "up": w_up_s, "down": w_down_s}
    kinds = list(shards)
    assert sorted(off + PACK_HEIGHT[kind] for (kind, _), off in PACK_OFF.items()) == sorted(PACK_OFF.values())[1:] + [PACK_ROWS]
    for kind in kinds:
        assert shards[kind].shape[1] == (4 if kind == "in" else 2) * PACK_HEIGHT[kind], (kind, shards[kind].shape)

    def body(*refs):
        w_refs, p_ref = dict(zip(kinds, refs[:4])), refs[4]
        scr, sems = dict(zip(kinds, refs[5:9])), refs[9]
        c = lax.axis_index("c")
        copies = {}
        for n, (kind, l) in enumerate(sorted(PACK_OFF)):
            rows = scr[kind].shape[1]
            copies[(kind, l)] = pltpu.make_async_copy(w_refs[kind].at[l, pl.ds(c * rows, rows)], scr[kind].at[l], sems.at[n])
            copies[(kind, l)].start()
        for (kind, l), off in sorted(PACK_OFF.items(), key=lambda kv: kv[1]):
            copies[(kind, l)].wait()
            h = PACK_HEIGHT[kind]
            if kind == "in":
                for t in range(2):
                    p_ref[off:off + h, t * IN_PIECE_COLS:(t + 1) * IN_PIECE_COLS] = scr[kind][l, t * h:(t + 1) * h, :].astype(BF16)
                p_ref[off:off + h, 2 * IN_PIECE_COLS:] = jnp.zeros((h, 1024 - 2 * IN_PIECE_COLS), BF16)
            else:
                p_ref[off:off + h, :] = scr[kind][l].astype(BF16)

    hbm = pl.BlockSpec(memory_space=pl.ANY)
    scratch = [pltpu.VMEM((DEPTH, shards[kind].shape[1] // 2, shards[kind].shape[2]), F32) for kind in kinds]
    outs = _comm_call(
        body, comm, name=name, grid=(1,), in_specs=[hbm] * 4,
        out_specs=[pl.BlockSpec((PACK_ROWS, 1024), lambda i: (0, 0))],
        out_shape=[jax.ShapeDtypeStruct((PACK_ROWS, 1024), BF16)],
        args=[shards[kind] for kind in kinds], aliases={}, semantics=("arbitrary",),
        scratch=scratch + [pltpu.SemaphoreType.DMA((len(PACK_OFF),))])
    return outs


def _unpack_in_pieces(w_ref, own_ref, w_scr):
    if own_ref is not None:
        me = 4 * lax.axis_index("x") + 2 * lax.axis_index("y") + lax.axis_index("c")
    for d in range(N_DEV):
        k, c = d // 2, d % 2
        for t in range(2):
            piece = w_ref[d, :, t * IN_PIECE_COLS:(t + 1) * IN_PIECE_COLS]
            if own_ref is not None:
                piece = jnp.where(me == d, own_ref[:, t * IN_PIECE_COLS:(t + 1) * IN_PIECE_COLS], piece)
            w_scr[c * 512 + t * 256:c * 512 + (t + 1) * 256, k * IN_PIECE_COLS:(k + 1) * IN_PIECE_COLS] = piece


def _in_weight_operands(wg):
    specs, args = [_gathered_spec(wg, "in")], [wg["in"][0]]
    if "in_own" in wg:
        own, off = wg["in_own"]
        h = PACK_HEIGHT["in"]
        assert off % h == 0
        specs.append(pl.BlockSpec((h, 1024), lambda *_: (off // h, 0), pipeline_mode=pl.Buffered(1)))
        args.append(own)
    return specs, args


class _Rows:
    def __init__(self, nb, seq, ctx):
        self.nb, self.seq, self.ctx = nb, seq, ctx
        self.n_lat, self.n_ctx = nb * seq, nb * ctx
        self.rows = self.n_lat + self.n_ctx
        self.tm = _pick(np.gcd(seq, self.n_ctx), (512, 256, 128))
        self.tiles_per_ex = seq // self.tm
        self.n_tiles = self.rows // self.tm
        self.n_lat_tiles = self.n_lat // self.tm
        self.groups = nb + 1

    def latent_only(self):
        rt = _Rows(self.nb, self.seq, self.ctx)
        rt.n_tiles = self.n_lat_tiles
        return rt

    def group(self, i):
        return jnp.minimum(i // self.tiles_per_ex, self.nb)

    def first_of_group(self, i):
        return jnp.logical_and(i % self.tiles_per_ex == 0, i <= self.n_lat_tiles)


def _mod_spec(rt):
    return pl.BlockSpec((1, N_MOD, D_MODEL), lambda i: (rt.group(i), 0, 0))


def _row_spec(rt, cols):
    return pl.BlockSpec((rt.tm, cols), lambda i: (i, 0))


def _vec_spec(cols):
    return pl.BlockSpec((1, cols), lambda i: (0, 0))


def _group_spec(rt):
    return pl.BlockSpec((1, 1, D_MODEL), lambda i: (rt.group(i), 0, 0))


def _gathered_spec(wg, kind):
    h, off = PACK_HEIGHT[kind], wg[kind][1]
    assert off % h == 0, (kind, off)
    return pl.BlockSpec((N_DEV, h, wg[kind][0].shape[2]), lambda *_: (0, off // h, 0), pipeline_mode=pl.Buffered(1))


def _group_shape(rt):
    return jax.ShapeDtypeStruct((rt.groups, 1, D_MODEL), F32)


def _vec_shape(cols=D_MODEL):
    return jax.ShapeDtypeStruct((1, cols), F32)


def _rms_inv(v):
    return lax.rsqrt(jnp.mean(v * v, axis=-1, keepdims=True) + EPS)


def _norm_mod_val(h_, g_, mod_ref, i_shift, i_scale):
    n = h_ * _rms_inv(h_) * g_
    return n * (1.0 + mod_ref[0, i_scale:i_scale + 1, :]) + mod_ref[0, i_shift:i_shift + 1, :]


def _post_norm_val(h_, z_, g_, mod_ref, i_gate):
    return h_ + mod_ref[0, i_gate:i_gate + 1, :] * (z_ * _rms_inv(z_) * g_)


def _post_norm_bwd_val(dh_, z_, g_, gate):
    rinv = _rms_inv(z_)
    n0 = z_ * rinv
    dn = dh_ * gate * g_
    dz = rinv * (dn - n0 * jnp.mean(dn * n0, axis=-1, keepdims=True))
    return dz, jnp.sum(dh_ * n0 * g_, axis=0, keepdims=True), jnp.sum(dh_ * gate * n0, axis=0, keepdims=True)


def _norm_mod_bwd_val(du_, h_, g_, one_sc):
    rinv = _rms_inv(h_)
    n0 = h_ * rinv
    dn = du_ * g_ * one_sc
    dh = rinv * (dn - n0 * jnp.mean(dn * n0, axis=-1, keepdims=True))
    return (dh, jnp.sum(du_, axis=0, keepdims=True), jnp.sum(du_ * n0 * g_, axis=0, keepdims=True),
            jnp.sum(du_ * one_sc * n0, axis=0, keepdims=True))


def _accumulate(rt, i, group_pairs, global_pairs):
    @pl.when(rt.first_of_group(i))
    def _():
        for ref, _ in group_pairs:
            ref[...] = jnp.zeros_like(ref)

    @pl.when(i == 0)
    def _():
        for ref, _ in global_pairs:
            ref[...] = jnp.zeros_like(ref)

    for ref, val in group_pairs:
        ref[0] += val
    for ref, val in global_pairs:
        ref[...] += val


def _rope_tables(rt):
    pos = np.arange(rt.seq)
    axis_dim = HEAD_DIM // 2
    inv = (ROPE_THETA ** (-np.arange(0, axis_dim, 2, dtype=np.float32) / axis_dim)).astype(np.float32)
    ang_r = (pos // GRID_W).astype(np.float32)[:, None] * inv[None, :]
    ang_c = (pos % GRID_W).astype(np.float32)[:, None] * inv[None, :]
    cr, sr, cc, sc = np.cos(ang_r), np.sin(ang_r), np.cos(ang_c), np.sin(ang_c)
    zero = np.zeros_like(sr)
    cos = np.concatenate([cr, cr, cc, cc], axis=1)
    s_lo = np.concatenate([zero, sr, zero, sc], axis=1)
    s_hi = np.concatenate([-sr, zero, -sc, zero], axis=1)

    def full(t, ctx_value):
        return jnp.asarray(np.concatenate([np.tile(t, (1, 2)), np.full((rt.tm, 128), ctx_value)], axis=0), F32)

    return full(cos, 1.0), full(s_lo, 0.0), full(s_hi, 0.0)


def _table_spec(rt):
    return pl.BlockSpec((rt.tm, 128), lambda i: (jnp.where(i < rt.n_lat_tiles, i % rt.tiles_per_ex, rt.tiles_per_ex), 0))


def _head_mean(x):
    r = lax.broadcasted_iota(jnp.int32, (128, 128), 0) // HEAD_DIM
    c = lax.broadcasted_iota(jnp.int32, (128, 128), 1) // HEAD_DIM
    ones = jnp.where(r == c, 1.0 / HEAD_DIM, 0.0).astype(F32)
    return jnp.dot(x, ones, preferred_element_type=F32, precision=lax.Precision.HIGH)


def _head_stats(t):
    return lax.rsqrt(_head_mean(t * t) + EPS)


def _prep_fwd_body(tm, qkv_ref, c, s1, s2, qn, kn, out_ref):
    def rope(t):
        return t * c + pltpu.roll(t, 16, 1) * s1 + pltpu.roll(t, 112, 1) * s2

    for j in range(12):
        t = qkv_ref[:, j * 128:(j + 1) * 128]
        if j < 4:
            t = rope(t * _head_stats(t) * qn) * Q_SCALE
        elif j == COL_KA:
            t = rope(t * _head_stats(t) * kn)
        elif 6 <= j < 10:
            t = rope(t) * Q_SCALE
        elif j == COL_KB:
            t = rope(t)
        out_ref[:, j * 128:(j + 1) * 128] = t.astype(BF16)


def _prep_bwd_body(dq, dkv, qkv_ref, c, s1, s2, qn, kn, out_ref):
    rows = slice(None)

    def rope_bwd(d):
        return d * c + pltpu.roll(d * s1, 112, 1) + pltpu.roll(d * s2, 16, 1)

    def norm_bwd(t, g, dy):
        rinv = _head_stats(t)
        n = t * rinv
        dn = dy * g
        return rinv * (dn - n * _head_mean(dn * n)), jnp.sum(dy * n, axis=0, keepdims=True)

    dqn = jnp.zeros((1, 128), F32)
    dkn = jnp.zeros((1, 128), F32)
    for j in range(12):
        if j < 4:
            d, dg = norm_bwd(qkv_ref[rows, j * 128:(j + 1) * 128], qn, rope_bwd(dq(slice(j * 128, (j + 1) * 128)) * Q_SCALE))
            dqn = dqn + dg
        elif j == COL_KA:
            d, dg = norm_bwd(qkv_ref[rows, j * 128:(j + 1) * 128], kn, rope_bwd(dkv(slice(0, 128))))
            dkn = dkn + dg
        elif j == COL_VA:
            d = dkv(slice(128, 256))
        elif j < 10:
            d = rope_bwd(dq(slice((j - 2) * 128, (j - 1) * 128)) * Q_SCALE)
        elif j == COL_KB:
            d = rope_bwd(dkv(slice(256, 384)))
        else:
            d = dkv(slice(384, 512))
        out_ref[rows, j * 128:(j + 1) * 128] = d.astype(BF16)
    return dqn, dkn


def _in_fwd(rt, h, gamma, mod, wg, tables, qn, kn, name):
    w_specs, w_args = _in_weight_operands(wg)
    n_w = len(w_args)
    joined = not isinstance(h, (tuple, list))
    n_h = 1 if joined else 2

    def body(*refs):
        g_ref, mod_ref = refs[n_h:n_h + 2]
        rest = refs[n_h + 2:]
        c_ref, s1_ref, s2_ref, qn_ref, kn_ref, u_ref, qkn_ref, qkvp_ref = rest[n_w:n_w + 8]
        qkv_ref, w_scr = rest[-2:]
        i = pl.program_id(0)

        @pl.when(i == 0)
        def _():
            _unpack_in_pieces(rest[0], rest[1] if n_w == 2 else None, w_scr)

        if joined:
            h_ = refs[0][...]
        else:
            h_ = jnp.where(i < rt.n_lat_tiles, refs[0][...], refs[1][...])
            rest[n_w + 8][...] = h_
        u = _norm_mod_val(h_, g_ref[...], mod_ref, 0, 1).astype(BF16)
        u_ref[...] = u
        qkv_ref[...] = jnp.dot(u, w_scr[...], preferred_element_type=F32)
        qkn_ref[...] = qkv_ref[:, 0:NORMED_COLS]
        _prep_fwd_body(rt.tm, qkv_ref, c_ref[...], s1_ref[...], s2_ref[...], qn_ref[...], kn_ref[...], qkvp_ref)

    if joined:
        h_specs, h_args = [_row_spec(rt, D_MODEL)], [h]
    else:
        h_specs = [pl.BlockSpec((rt.tm, D_MODEL), lambda i: (jnp.minimum(i, rt.n_lat_tiles - 1), 0)),
                   pl.BlockSpec((rt.tm, D_MODEL), lambda i: (jnp.maximum(i - rt.n_lat_tiles, 0), 0))]
        h_args = list(h)
    out_specs = [_row_spec(rt, D_MODEL), _row_spec(rt, NORMED_COLS), _row_spec(rt, IN_COLS)]
    out_shape = [jax.ShapeDtypeStruct((rt.rows, D_MODEL), BF16), jax.ShapeDtypeStruct((rt.rows, NORMED_COLS), F32),
                 jax.ShapeDtypeStruct((rt.rows, IN_COLS), BF16)]
    if not joined:
        out_specs.append(_row_spec(rt, D_MODEL))
        out_shape.append(jax.ShapeDtypeStruct((rt.rows, D_MODEL), F32))
    return pl.pallas_call(
        body, name=name, grid=(rt.n_tiles,),
        in_specs=h_specs + [_vec_spec(D_MODEL), _mod_spec(rt)] + w_specs + [_table_spec(rt)] * 3 + [_vec_spec(128)] * 2,
        out_specs=out_specs, out_shape=out_shape,
        scratch_shapes=[pltpu.VMEM((rt.tm, IN_COLS), F32), pltpu.VMEM((D_MODEL, IN_COLS), BF16)],
        compiler_params=_params(("arbitrary",)),
    )(*h_args, gamma, mod, *w_args, *tables, qn, kn)


def _in_bwd(rt, dq, dkv, qkv, tables, qn, kn, wg, h, dres, mod, gamma, latent_only, name, comm=None, dead_ctx_dkv=None):
    last = rt.n_lat_tiles - 1
    w_specs, w_args = _in_weight_operands(wg)
    n_w = len(w_args)
    n_dead = 0 if dead_ctx_dkv is None else 1

    def body(dq_ref, dkv_ref, qkv_ref, c_ref, s1_ref, s2_ref, qn_ref, kn_ref, *rest):
        h_ref, dres_ref, mod_ref, g_ref, dqkv_ref, dh_ref, dqn_ref, dkn_ref, dsh_ref, dsc_ref, dg_ref, w_scr = rest[n_w + n_dead:]
        i = pl.program_id(0)

        @pl.when(i == 0)
        def _():
            _unpack_in_pieces(rest[0], rest[1] if n_w == 2 else None, w_scr)

        if n_dead:
            c1_ref, lat = rest[n_w], i <= last
            load_dq = lambda cols: jnp.where(lat, dq_ref[:, cols], 0.0)
            load_dkv = lambda cols: jnp.where(lat, dkv_ref[:, cols], c1_ref[:, cols])
            dres_ = jnp.where(lat, dres_ref[...], 0.0)
        else:
            load_dq, load_dkv, dres_ = (lambda cols: dq_ref[:, cols]), (lambda cols: dkv_ref[:, cols]), dres_ref[...]
        dqn, dkn = _prep_bwd_body(load_dq, load_dkv, qkv_ref, c_ref[...], s1_ref[...], s2_ref[...], qn_ref[...], kn_ref[...], dqkv_ref)
        du = lax.dot_general(dqkv_ref[...], w_scr[...], NT, preferred_element_type=F32)
        dh, dsh, dsc, dg = _norm_mod_bwd_val(du, h_ref[...], g_ref[...], 1.0 + mod_ref[0, 1:2, :])
        if latent_only:
            @pl.when(i <= last)
            def _():
                dh_ref[...] = dres_ + dh
        else:
            dh_ref[...] = dres_ + dh
        _accumulate(rt, i, [(dsh_ref, dsh), (dsc_ref, dsc)], [(dg_ref, dg), (dqn_ref, dqn), (dkn_ref, dkn)])

    dh_spec = pl.BlockSpec((rt.tm, D_MODEL), lambda i: (jnp.minimum(i, last), 0)) if latent_only else _row_spec(rt, D_MODEL)
    dead_specs = [] if dead_ctx_dkv is None else [pl.BlockSpec((rt.tm, 512), lambda i: (jnp.maximum(i - rt.n_lat_tiles, 0), 0))]
    dead_args = [] if dead_ctx_dkv is None else [dead_ctx_dkv]
    return _comm_call(
        body, comm, name=name, grid=(rt.n_tiles,),
        in_specs=[_row_spec(rt, 1024), _row_spec(rt, 512), _row_spec(rt, NORMED_COLS)] + [_table_spec(rt)] * 3 + [_vec_spec(128)] * 2
        + w_specs + dead_specs + [_row_spec(rt, D_MODEL), _row_spec(rt, D_MODEL), _mod_spec(rt), _vec_spec(D_MODEL)],
        out_specs=[_row_spec(rt, IN_COLS), dh_spec, _vec_spec(128), _vec_spec(128),
                   _group_spec(rt), _group_spec(rt), _vec_spec(D_MODEL)],
        out_shape=[jax.ShapeDtypeStruct((rt.rows, IN_COLS), BF16),
                   jax.ShapeDtypeStruct((rt.n_lat if latent_only else rt.rows, D_MODEL), F32),
                   _vec_shape(128), _vec_shape(128), _group_shape(rt), _group_shape(rt), _vec_shape()],
        args=[dq, dkv, qkv, *tables, qn, kn, *w_args, *dead_args, h, dres, mod, gamma], aliases={}, semantics=("arbitrary",),
        scratch=[pltpu.VMEM((D_MODEL, IN_COLS), BF16)])


def _out_fwd(rt, o, wg, h, mod, g_post_mix, g_pre_mlp, name):
    def body(o_ref, w_ref, h_ref, mod_ref, gpost_ref, gpre_ref, mix_ref, h1_ref, u2_ref):
        mix = jnp.dot(o_ref[...], w_ref[...].reshape(D_MODEL, D_MODEL), preferred_element_type=F32)
        mix_ref[...] = mix
        h1 = _post_norm_val(h_ref[...], mix, gpost_ref[...], mod_ref, 2)
        h1_ref[...] = h1
        u2_ref[...] = _norm_mod_val(h1, gpre_ref[...], mod_ref, 3, 4).astype(BF16)

    return pl.pallas_call(
        body, name=name, grid=(rt.n_tiles,),
        in_specs=[_row_spec(rt, D_MODEL), _gathered_spec(wg, "out"), _row_spec(rt, D_MODEL), _mod_spec(rt),
                  _vec_spec(D_MODEL), _vec_spec(D_MODEL)],
        out_specs=[_row_spec(rt, D_MODEL)] * 3,
        out_shape=[jax.ShapeDtypeStruct((rt.rows, D_MODEL), F32), jax.ShapeDtypeStruct((rt.rows, D_MODEL), F32),
                   jax.ShapeDtypeStruct((rt.rows, D_MODEL), BF16)],
        compiler_params=_params(("parallel",)),
    )(o, wg["out"][0], h, mod, g_post_mix, g_pre_mlp)


def _out_bwd(rt, dh1, mix, wg, mod, g_post_mix, name):
    def body(dh_ref, mix_ref, w_ref, mod_ref, g_ref, dmix_ref, do_ref, dgate_ref, dg_ref):
        i = pl.program_id(0)
        dz, dgate, dg = _post_norm_bwd_val(dh_ref[...], mix_ref[...], g_ref[...], mod_ref[0, 2:3, :])
        dzb = dz.astype(BF16)
        dmix_ref[...] = dzb
        do_ref[...] = lax.dot_general(dzb, w_ref[...].reshape(D_MODEL, D_MODEL), NT, preferred_element_type=F32).astype(BF16)
        _accumulate(rt, i, [(dgate_ref, dgate)], [(dg_ref, dg)])

    return pl.pallas_call(
        body, name=name, grid=(rt.n_tiles,),
        in_specs=[_row_spec(rt, D_MODEL), _row_spec(rt, D_MODEL), _gathered_spec(wg, "out"), _mod_spec(rt), _vec_spec(D_MODEL)],
        out_specs=[_row_spec(rt, D_MODEL), _row_spec(rt, D_MODEL), _group_spec(rt), _vec_spec(D_MODEL)],
        out_shape=[jax.ShapeDtypeStruct((rt.rows, D_MODEL), BF16), jax.ShapeDtypeStruct((rt.rows, D_MODEL), BF16),
                   _group_shape(rt), _vec_shape()],
        compiler_params=_params(("arbitrary",)),
    )(dh1, mix, wg["out"][0], mod, g_post_mix)


def _w_chunk(w_ref, k):
    return w_ref[2 * k:2 * k + 2].reshape(1024, 1024)


def _mlp_fwd(rt, u2, h1, wg, mod, g_post_mlp, name, comm=None, target=None):
    last = rt.n_lat_tiles - 1

    def body(u2_ref, h1_ref, wu_ref, wd_ref, mod_ref, g_ref, *rest):
        u2_ = u2_ref[...]
        y = jnp.zeros((rt.tm, D_MODEL), F32)
        for k in range(D_FF // 1024):
            a = jnp.maximum(jnp.dot(u2_, _w_chunk(wu_ref, k), preferred_element_type=F32), 0.0)
            rest[-3 if target is None else -4][:, k * 1024:(k + 1) * 1024] = a.astype(BF16)
            y = y + jnp.dot((a * a).astype(BF16), _w_chunk(wd_ref, k), preferred_element_type=F32)
        h2 = _post_norm_val(h1_ref[...], y, g_ref[...], mod_ref, 5)
        if target is None:
            _, y_ref, h2_ref = rest
            y_ref[...] = y
            h2_ref[...] = h2
        else:
            t_ref, _, y_ref, dh_ref, sq_ref = rest
            y_ref[...] = y
            i = pl.program_id(0)

            @pl.when(i == 0)
            def _():
                sq_ref[...] = jnp.zeros_like(sq_ref)

            @pl.when(i <= last)
            def _():
                e = h2 - t_ref[...]
                dh_ref[...] = e * (1.0 / D_MODEL)
                sq_ref[...] += jnp.sum(e * e, axis=0, keepdims=True)

            @pl.when(i > last)
            def _():
                dh_ref[...] = jnp.zeros_like(dh_ref)

    in_specs = [_row_spec(rt, D_MODEL), _row_spec(rt, D_MODEL), _gathered_spec(wg, "up"), _gathered_spec(wg, "down"),
                _mod_spec(rt), _vec_spec(D_MODEL)]
    args = [u2, h1, wg["up"][0], wg["down"][0], mod, g_post_mlp]
    out_specs = [_row_spec(rt, D_FF), _row_spec(rt, D_MODEL), _row_spec(rt, D_MODEL)]
    out_shape = [jax.ShapeDtypeStruct((rt.rows, D_FF), BF16), jax.ShapeDtypeStruct((rt.rows, D_MODEL), F32),
                 jax.ShapeDtypeStruct((rt.rows, D_MODEL), F32)]
    if target is not None:
        in_specs.append(pl.BlockSpec((rt.tm, D_MODEL), lambda i: (jnp.minimum(i, last), 0)))
        args.append(target)
        out_specs.append(_vec_spec(D_MODEL))
        out_shape.append(_vec_shape())
    return _comm_call(body, comm, name=name, grid=(rt.n_tiles,), in_specs=in_specs, out_specs=out_specs, out_shape=out_shape,
                      args=args, aliases={}, semantics=("parallel",) if target is None else ("arbitrary",))


def _mlp_down_bwd(rt, dh, y, ra, wg, mod, g_post_mlp, name):
    def body(dh_ref, y_ref, ra_ref, wd_ref, mod_ref, g_ref, dy_ref, da_ref, dgate_ref, dg_ref):
        i = pl.program_id(0)
        dz, dgate, dg = _post_norm_bwd_val(dh_ref[...], y_ref[...], g_ref[...], mod_ref[0, 5:6, :])
        dyb = dz.astype(BF16)
        dy_ref[...] = dyb
        for k in range(D_FF // 1024):
            dr = lax.dot_general(dyb, _w_chunk(wd_ref, k), NT, preferred_element_type=F32)
            da_ref[:, k * 1024:(k + 1) * 1024] = (dr * (2.0 * ra_ref[:, k * 1024:(k + 1) * 1024].astype(F32))).astype(BF16)
        _accumulate(rt, i, [(dgate_ref, dgate)], [(dg_ref, dg)])

    return pl.pallas_call(
        body, name=name, grid=(rt.n_tiles,),
        in_specs=[_row_spec(rt, D_MODEL), _row_spec(rt, D_MODEL), _row_spec(rt, D_FF), _gathered_spec(wg, "down"),
                  _mod_spec(rt), _vec_spec(D_MODEL)],
        out_specs=[_row_spec(rt, D_MODEL), _row_spec(rt, D_FF), _group_spec(rt), _vec_spec(D_MODEL)],
        out_shape=[jax.ShapeDtypeStruct((rt.rows, D_MODEL), BF16), jax.ShapeDtypeStruct((rt.rows, D_FF), BF16),
                   _group_shape(rt), _vec_shape()],
        compiler_params=_params(("arbitrary",)),
    )(dh, y, ra, wg["down"][0], mod, g_post_mlp)


def _mlp_up_bwd(rt, da, wg, h1, dh, mod, g_pre_mlp, name, comm=None):
    def body(da_ref, wu_ref, h1_ref, dh_ref, mod_ref, g_ref, dh1_ref, dsh_ref, dsc_ref, dg_ref):
        i = pl.program_id(0)
        du = jnp.zeros((rt.tm, D_MODEL), F32)
        for k in range(D_FF // 1024):
            du = du + lax.dot_general(da_ref[:, k * 1024:(k + 1) * 1024], _w_chunk(wu_ref, k), NT, preferred_element_type=F32)
        d, dsh, dsc, dg = _norm_mod_bwd_val(du, h1_ref[...], g_ref[...], 1.0 + mod_ref[0, 4:5, :])
        dh1_ref[...] = dh_ref[...] + d
        _accumulate(rt, i, [(dsh_ref, dsh), (dsc_ref, dsc)], [(dg_ref, dg)])

    return _comm_call(
        body, comm, name=name, grid=(rt.n_tiles,),
        in_specs=[_row_spec(rt, D_FF), _gathered_spec(wg, "up"), _row_spec(rt, D_MODEL), _row_spec(rt, D_MODEL),
                  _mod_spec(rt), _vec_spec(D_MODEL)],
        out_specs=[_row_spec(rt, D_MODEL), _group_spec(rt), _group_spec(rt), _vec_spec(D_MODEL)],
        out_shape=[jax.ShapeDtypeStruct((rt.rows, D_MODEL), F32), _group_shape(rt), _group_shape(rt), _vec_shape()],
        args=[da, wg["up"][0], h1, dh, mod, g_pre_mlp], aliases={}, semantics=("arbitrary",))


def _wgrad_packed(rt, a, b, kind, off, n_rows, p_prev, name, comm=None):
    h = PACK_HEIGHT[kind]
    tk = rt.tm
    assert off % h == 0, (kind, off)

    def body(a_ref, b_ref, *rest):
        o_ref = rest[-1]
        i = pl.program_id(0)

        @pl.when(i == 0)
        def _():
            o_ref[...] = jnp.zeros_like(o_ref)

        if kind == "in":
            res = lax.dot_general(a_ref[...], b_ref[...], TN, preferred_element_type=F32)
            for k in range(4):
                for c in range(2):
                    for t in range(2):
                        o_ref[c, k, :, t * IN_PIECE_COLS:(t + 1) * IN_PIECE_COLS] += \
                            res[c * 512 + t * h:c * 512 + (t + 1) * h, k * IN_PIECE_COLS:(k + 1) * IN_PIECE_COLS]
        elif kind == "out":
            res = lax.dot_general(a_ref[...], b_ref[...], TN, preferred_element_type=F32)
            for k in range(4):
                for c in range(2):
                    o_ref[c, k] += res[(2 * k + c) * h:(2 * k + c + 1) * h]
        else:
            for k in range(4):
                if kind == "up":
                    res = lax.dot_general(a_ref[...], b_ref[:, k * 1024:(k + 1) * 1024], TN, preferred_element_type=F32)
                else:
                    ra = a_ref[:, k * 1024:(k + 1) * 1024].astype(F32)
                    res = lax.dot_general((ra * ra).astype(BF16), b_ref[...], TN, preferred_element_type=F32)
                o_ref[0, k] += res[0:h]
                o_ref[1, k] += res[h:2 * h]

    in_specs = [pl.BlockSpec((tk, a.shape[1]), lambda i: (i, 0)), pl.BlockSpec((tk, b.shape[1]), lambda i: (i, 0))]
    args = [a, b]
    aliases = {}
    if p_prev is not None:
        in_specs.append(pl.BlockSpec(memory_space=pl.ANY))
        args.append(p_prev)
        aliases = {2: 0}
    outs = _comm_call(
        body, comm, name=name, grid=(rt.n_tiles,),
        in_specs=in_specs,
        out_specs=[pl.BlockSpec((2, 4, h, 1024), lambda i: (0, 0, off // h, 0))],
        out_shape=[jax.ShapeDtypeStruct((2, 4, n_rows, 1024), F32)],
        args=args, aliases=aliases, semantics=("arbitrary",))
    return outs[0] if comm is None else outs


def _ada_wgrad(xs, dm, name):
    depth, _, cols = dm.shape

    def body(x_ref, d_ref, o_ref):
        for l in range(depth):
            o_ref[l] = lax.dot_general(x_ref[...], d_ref[l], TN, preferred_element_type=F32)

    return pl.pallas_call(body, name=name, out_shape=jax.ShapeDtypeStruct((depth, xs.shape[1], cols), F32),
                          compiler_params=pltpu.CompilerParams(vmem_limit_bytes=VMEM_LIMIT))(xs, dm)


def _stack_heads(x, kvi):
    x = x.astype(F32)
    tq = x.shape[0]
    lane = lax.broadcasted_iota(jnp.int32, (tq, 128), 1)
    keep = lane < HEAD_DIM if kvi == 0 else lane >= HEAD_DIM
    parts = []
    for p in range(2):
        pair = x[:, p * 128:(p + 1) * 128]
        swapped = pltpu.roll(pair, HEAD_DIM, 1)
        lo_head, hi_head = (pair, swapped) if kvi == 0 else (swapped, pair)
        parts += [jnp.where(keep, lo_head, 0.0), jnp.where(keep, hi_head, 0.0)]
    return jnp.concatenate(parts, axis=0).astype(BF16)


def _unstack_heads(o4, kvi):
    tq = o4.shape[0] // GROUP
    lane = lax.broadcasted_iota(jnp.int32, (tq, 128), 1)
    outs = []
    for p in range(2):
        r_lo, r_hi = o4[(2 * p) * tq:(2 * p + 1) * tq], o4[(2 * p + 1) * tq:(2 * p + 2) * tq]
        if kvi == 0:
            lo, hi = r_lo, pltpu.roll(r_hi, HEAD_DIM, 1)
        else:
            lo, hi = pltpu.roll(r_lo, HEAD_DIM, 1), r_hi
        outs.append(jnp.where(lane < HEAD_DIM, lo, hi))
    return jnp.concatenate(outs, axis=1)


def _per_head(shape, axis, tq, values):
    head = lax.broadcasted_iota(jnp.int32, shape, axis) // tq
    out = jnp.zeros(shape, F32)
    for g in range(GROUP):
        out = jnp.where(head == g, values[g], out)
    return out


KEY_CHUNK = 512
Q_TILE = 128
Q_TILE_FWD = 256


def _key_chunks(k_ref, v_ref, n, kc=KEY_CHUNK):
    kc = min(kc, n)
    return [(k_ref[c * kc:(c + 1) * kc, :], v_ref[c * kc:(c + 1) * kc, :], None) for c in range(n // kc)]


def _softmax_fwd(qs, chunks, sink_col):
    logits = []
    for k, _, mask in chunks:
        s = lax.dot_general(qs, k, NT, preferred_element_type=F32)
        logits.append(s if mask is None else jnp.where(mask, s, NEG_BIG))
    m = functools.reduce(jnp.maximum, [jnp.max(s, axis=1, keepdims=True) for s in logits])
    if sink_col is not None:
        m = jnp.maximum(m, sink_col)
    l = jnp.zeros_like(m) if sink_col is None else jnp.exp(sink_col - m)
    acc = jnp.zeros((qs.shape[0], 128), F32)
    for s, (_, v, _) in zip(logits, chunks):
        p = jnp.exp(s - m)
        l = l + jnp.sum(p, axis=1, keepdims=True)
        acc = acc + jnp.dot(p.astype(BF16), v, preferred_element_type=F32)
    return acc / l, m + jnp.log(l)


def _to_rows(col):
    return jnp.transpose(jnp.broadcast_to(col, (col.shape[0], 128)))[0:8, :]


def _softmax_bwd(qs, dos, lse_row, delta_row, chunks):
    dq = jnp.zeros((qs.shape[0], 128), F32)
    grads = []
    for k, v, mask in chunks:
        s = lax.dot_general(k, qs, NT, preferred_element_type=F32)
        if mask is not None:
            s = jnp.where(mask, s, NEG_BIG)
        p = jnp.exp(s - lse_row)
        dp = lax.dot_general(v, dos, NT, preferred_element_type=F32)
        ds = (p * (dp - delta_row)).astype(BF16)
        dv = jnp.dot(p.astype(BF16), dos, preferred_element_type=F32)
        dk = jnp.dot(ds, qs, preferred_element_type=F32)
        dq = dq + lax.dot_general(ds, k, TN, preferred_element_type=F32)
        grads.append((dk, dv))
    return dq, grads


def _band(qi, tq, seq):
    span = tq + 2 * WINDOW
    start = pl.multiple_of(jnp.clip(qi * tq - WINDOW, 0, seq - span), 64)
    return start, span


def _band_mask(qi, tq, start, span, query_axis):
    shape = (GROUP * tq, span) if query_axis == 0 else (span, GROUP * tq)
    qpos = qi * tq + lax.broadcasted_iota(jnp.int32, shape, query_axis) % tq
    kpos = start + lax.broadcasted_iota(jnp.int32, shape, 1 - query_axis)
    return jnp.abs(kpos - qpos) <= WINDOW


def _qkv_specs(rt, tq, q_row, ctx_row, with_latent):
    specs = [pl.BlockSpec((tq, 256), functools.partial(lambda b, i, col: (q_row(b, i), col), col=col)) for col in (0, 1, 3, 4)]
    if with_latent:
        specs += [pl.BlockSpec((rt.seq, 128), functools.partial(lambda b, i, col: (b, col), col=col))
                  for col in (COL_KA, COL_VA, COL_KB, COL_VB)]
    specs += [pl.BlockSpec((rt.ctx, 128), functools.partial(lambda b, i, col: (ctx_row(b), col), col=col))
              for col in (COL_KA, COL_VA, COL_KB, COL_VB)]
    return specs


def _attn_fwd(rt, qkvp, sink, o_prev, name, comm=None):
    latent = o_prev is None
    seq, ctx, nb = rt.seq, rt.ctx, rt.nb
    tq = Q_TILE_FWD if latent else ctx
    tile = Q_TILE if latent else ctx
    parts = tq // tile
    nq = seq // tq if latent else 1
    ctx_blk0 = rt.n_lat // ctx
    q_row = (lambda b, i: b * nq + i) if latent else (lambda b, i: ctx_blk0 + b)

    def store_lse(lse_ref, j, lse_col):
        rows = _to_rows(lse_col)
        for part in range(parts):
            lse_ref[part, j] = jnp.concatenate([rows[:, g * tq + part * tile:g * tq + (part + 1) * tile] for g in range(GROUP)], axis=1)

    def body(sink_ref, qa0, qa1, qb0, qb1, *rest):
        if latent:
            kal, val, kbl, vbl, kac, vac, kbc, vbc, o_ref, lse_ref = rest
        else:
            kac, vac, kbc, vbc, _, o_ref, lse_ref = rest
        qi = pl.program_id(1)
        for kvi, (qa, qb) in enumerate(((qa0, qb0), (qa1, qb1))):
            src_a = _key_chunks(kac, vac, ctx)
            src_b = _key_chunks(kbc, vbc, ctx)
            if latent:
                src_a += _key_chunks(kal, val, seq, seq)
                start, span = _band(qi, tq, seq)
                src_b.append((kbl[pl.ds(start, span), :], vbl[pl.ds(start, span), :], _band_mask(qi, tq, start, span, 0)))
            oa, lse = _softmax_fwd(_stack_heads(qa[...], kvi), src_a, None)
            o_ref[:, kvi * 256:(kvi + 1) * 256] = _unstack_heads(oa, kvi).astype(BF16)
            store_lse(lse_ref, kvi, lse)
            sink_col = _per_head((GROUP * tq, 1), 0, tq, [sink_ref[kvi * GROUP + g] for g in range(GROUP)])
            ob, lse = _softmax_fwd(_stack_heads(qb[...], kvi), src_b, sink_col)
            o_ref[:, 512 + kvi * 256:512 + (kvi + 1) * 256] = _unstack_heads(ob, kvi).astype(BF16)
            store_lse(lse_ref, 2 + kvi, lse)

    specs = _qkv_specs(rt, tq, q_row, lambda b: ctx_blk0 + b, latent)
    args = [sink] + [qkvp] * len(specs)
    in_specs = [pl.BlockSpec(memory_space=pltpu.SMEM)] + specs
    aliases = {}
    if not latent:
        in_specs.append(pl.BlockSpec(memory_space=pl.ANY))
        args.append(o_prev)
        aliases = {len(args) - 1: 0}
    return _comm_call(
        body, comm, name=name, grid=(nb, nq),
        in_specs=in_specs,
        out_specs=[pl.BlockSpec((tq, 1024), lambda b, i: (q_row(b, i), 0)),
                   pl.BlockSpec((parts, 4, 8, GROUP * tile), lambda b, i: (b * nq + i, 0, 0, 0))],
        out_shape=[jax.ShapeDtypeStruct((rt.rows, 1024), BF16), jax.ShapeDtypeStruct((nb * nq * parts, 4, 8, GROUP * tile), F32)],
        args=args, aliases=aliases, semantics=("parallel", "parallel"))


def _attn_bwd(rt, qkvp, o, lse, do, sink, prev, name, comm=None):
    latent = prev is None
    seq, ctx, nb = rt.seq, rt.ctx, rt.nb
    tq = Q_TILE if latent else ctx
    nq = seq // tq if latent else 1
    ctx_blk0 = rt.n_lat // ctx
    q_row = (lambda b, i: b * nq + i) if latent else (lambda b, i: ctx_blk0 + b)
    kc = min(KEY_CHUNK, seq)

    def body(sink_ref, qa0, qa1, qb0, qb1, *rest):
        if latent:
            kal, val, kbl, vbl, kac, vac, kbc, vbc, do_ref, o_ref, lse_ref, dq_ref, dl_ref, dc_ref, dsink_ref = rest
        else:
            kac, vac, kbc, vbc, do_ref, o_ref, lse_ref, c1_ref, _, _, dq_ref, dc_ref, dsink_ref = rest
        b, qi = pl.program_id(0), pl.program_id(1)

        def rows_of(cols, kvi, mixer):
            dos = _stack_heads(do_ref[:, cols], kvi)
            delta = jnp.sum(dos.astype(F32) * _stack_heads(o_ref[:, cols], kvi).astype(F32), axis=1, keepdims=True)
            return dos, lse_ref[0, 2 * mixer + kvi, 0:1, :], _to_rows(delta)[0:1, :]

        @pl.when(jnp.logical_and(b == 0, qi == 0))
        def _():
            dsink_ref[...] = jnp.zeros_like(dsink_ref)

        if latent:
            @pl.when(qi == 0)
            def _():
                dc_ref[...] = jnp.zeros_like(dc_ref)
                dl_ref[...] = jnp.zeros_like(dl_ref)
        else:
            dc_ref[...] = c1_ref[...]

        head_row = lax.broadcasted_iota(jnp.int32, (8, 128), 0)
        for kvi, (qa, qb) in enumerate(((qa0, qb0), (qa1, qb1))):
            cols = slice(kvi * 256, (kvi + 1) * 256)
            dos, lse_row, delta_row = rows_of(cols, kvi, 0)
            src = _key_chunks(kac, vac, ctx)
            if latent:
                src += _key_chunks(kal, val, seq)
            dq4, grads = _softmax_bwd(_stack_heads(qa[...], kvi), dos, lse_row, delta_row, src)
            dq_ref[:, cols] = _unstack_heads(dq4, kvi)
            dc_ref[:, 0:128] += grads[0][0]
            dc_ref[:, 128:256] += grads[0][1]
            for c, (dk, dv) in enumerate(grads[1:]):
                dl_ref[c * kc:(c + 1) * kc, 0:128] += dk
                dl_ref[c * kc:(c + 1) * kc, 128:256] += dv
            cols = slice(512 + kvi * 256, 512 + (kvi + 1) * 256)
            dos, lse_row, delta_row = rows_of(cols, kvi, 1)
            src = _key_chunks(kbc, vbc, ctx)
            if latent:
                start, span = _band(qi, tq, seq)
                src.append((kbl[pl.ds(start, span), :], vbl[pl.ds(start, span), :], _band_mask(qi, tq, start, span, 1)))
            dq4, grads = _softmax_bwd(_stack_heads(qb[...], kvi), dos, lse_row, delta_row, src)
            dq_ref[:, cols] = _unstack_heads(dq4, kvi)
            dc_ref[:, 256:384] += grads[0][0]
            dc_ref[:, 384:512] += grads[0][1]
            if latent:
                dl_ref[pl.ds(start, span), 256:384] += grads[1][0]
                dl_ref[pl.ds(start, span), 384:512] += grads[1][1]
            sink_row = _per_head((1, GROUP * tq), 1, tq, [sink_ref[kvi * GROUP + g] for g in range(GROUP)])
            dsink = -jnp.exp(sink_row - lse_row) * delta_row
            head = lax.broadcasted_iota(jnp.int32, (1, GROUP * tq), 1) // tq
            upd = jnp.zeros((8, 128), F32)
            for g in range(GROUP):
                upd = jnp.where(head_row == kvi * GROUP + g, jnp.sum(jnp.where(head == g, dsink, 0.0)), upd)
            dsink_ref[...] += upd

    specs = _qkv_specs(rt, tq, q_row, lambda b: ctx_blk0 + b, latent)
    q_rows_spec = pl.BlockSpec((tq, 1024), lambda b, i: (q_row(b, i), 0))
    in_specs = ([pl.BlockSpec(memory_space=pltpu.SMEM)] + specs
                + [q_rows_spec, q_rows_spec, pl.BlockSpec((1, 4, 8, GROUP * tq), lambda b, i: (b * nq + i, 0, 0, 0))])
    args = [sink] + [qkvp] * len(specs) + [do, o, lse]
    dq_shape = jax.ShapeDtypeStruct((rt.rows, 1024), F32)
    dkv_shape = jax.ShapeDtypeStruct((rt.rows, 512), F32)
    dsink_spec, dsink_shape = pl.BlockSpec((8, 128), lambda b, i: (0, 0)), jax.ShapeDtypeStruct((8, 128), F32)
    dq_spec = pl.BlockSpec((tq, 1024), lambda b, i: (q_row(b, i), 0))
    if latent:
        out_specs = [dq_spec, pl.BlockSpec((seq, 512), lambda b, i: (b, 0)), pl.BlockSpec((ctx, 512), lambda b, i: (b, 0)), dsink_spec]
        out_shape = [dq_shape, dkv_shape, jax.ShapeDtypeStruct((rt.n_ctx, 512), F32), dsink_shape]
        aliases = {}
    else:
        dq_prev, dkv_prev, c1 = prev
        in_specs += [pl.BlockSpec((ctx, 512), lambda b, i: (b, 0)), pl.BlockSpec(memory_space=pl.ANY), pl.BlockSpec(memory_space=pl.ANY)]
        args += [c1, dq_prev, dkv_prev]
        out_specs = [dq_spec, pl.BlockSpec((ctx, 512), lambda b, i: (ctx_blk0 + b, 0)), dsink_spec]
        out_shape = [dq_shape, dkv_shape, dsink_shape]
        aliases = {len(args) - 2: 0, len(args) - 1: 1}
    return _comm_call(body, comm, name=name, grid=(nb, nq), in_specs=in_specs, out_specs=out_specs, out_shape=out_shape,
                      args=args, aliases=aliases, semantics=("arbitrary", "arbitrary"))


def _silu(x):
    return x / (1.0 + jnp.exp(-x))


def _whole(shape):
    return pl.BlockSpec(shape, lambda i, s: (0,) * len(shape))


def _ada_half_spec(cols):
    return pl.BlockSpec((DEPTH, D_MODEL, cols), lambda i, s: (0, 0, s[0]))


def _ada_fwd(cond, w_ada, b_half, c_idx, name):
    rows = cond.shape[0]
    cols = w_ada.shape[2] // 2

    def body(s_ref, c_ref, w_ref, b_ref, x_ref, o_ref):
        xs = _silu(c_ref[...]).astype(BF16)
        x_ref[...] = xs
        for l in range(DEPTH):
            o_ref[l] = jnp.dot(xs, w_ref[l].astype(BF16), preferred_element_type=F32) + b_ref[l]

    grid_spec = pltpu.PrefetchScalarGridSpec(
        num_scalar_prefetch=1, grid=(1,),
        in_specs=[_whole(cond.shape), _ada_half_spec(cols), _whole(b_half.shape)],
        out_specs=[_whole((rows, D_MODEL)), _whole((DEPTH, rows, cols))])
    return pl.pallas_call(
        body, name=name, grid_spec=grid_spec,
        out_shape=[jax.ShapeDtypeStruct((rows, D_MODEL), BF16), jax.ShapeDtypeStruct((DEPTH, rows, cols), F32)],
        compiler_params=_params(("arbitrary",)),
    )(c_idx, cond, w_ada, b_half)


def _ada_cond_bwd(dcx, w_ada, c_idx, name):
    _, rows, cols = dcx.shape

    def body(s_ref, d_ref, w_ref, o_ref):
        acc = jnp.zeros((rows, D_MODEL), F32)
        for l in range(DEPTH):
            acc = acc + lax.dot_general(d_ref[l], w_ref[l].astype(BF16), NT, preferred_element_type=F32)
        o_ref[...] = acc

    grid_spec = pltpu.PrefetchScalarGridSpec(
        num_scalar_prefetch=1, grid=(1,),
        in_specs=[_whole(dcx.shape), _ada_half_spec(cols)], out_specs=_whole((rows, D_MODEL)))
    return pl.pallas_call(body, name=name, grid_spec=grid_spec, out_shape=jax.ShapeDtypeStruct((rows, D_MODEL), F32),
                          compiler_params=_params(("arbitrary",)))(c_idx, dcx, w_ada)


def _dev_sum(x, name):
    _, r, c = x.shape

    def body(x_ref, o_ref):
        v = x_ref[0]
        for d in range(1, N_DEV):
            v = v + x_ref[d]
        o_ref[...] = v

    return pl.pallas_call(body, name=name, out_shape=jax.ShapeDtypeStruct((r, c), F32))(x)


def _adam_val(w, g, m, v):
    c1 = 1.0 / (1.0 - ADAM_B1 ** ADAM_STEP)
    c2 = 1.0 / (1.0 - ADAM_B2 ** ADAM_STEP)
    nm = ADAM_B1 * m + (1.0 - ADAM_B1) * g
    nv = ADAM_B2 * v + (1.0 - ADAM_B2) * (g * g)
    return -ADAM_LR * ((nm * c1) / (jnp.sqrt(nv * c2) + ADAM_EPS) + ADAM_WD * w), nm, nv


def _small_update(tot, dcc_parts, params, n_groups, name):
    n_p = len(params)
    mod_rows = n_groups * N_MOD
    head_row = DEPTH * mod_rows + 4 * DEPTH

    def body(tot_ref, dcc_ref, *refs):
        ins, outs = refs[:3 * n_p], refs[3 * n_p:]

        def update(p, rows, cols, g):
            w_ref, m_ref, v_ref = ins[3 * p:3 * p + 3]
            g_ref, d_ref, nm_ref, nv_ref = outs[4 * p:4 * p + 4]
            d, nm, nv = _adam_val(w_ref[rows, cols], g, m_ref[rows, cols], v_ref[rows, cols])
            g_ref[rows, cols] = g
            d_ref[rows, cols] = d
            nm_ref[rows, cols] = nm
            nv_ref[rows, cols] = nv

        acc = dcc_ref[0, 0:1, :]
        for d in range(1, N_DEV):
            acc = acc + dcc_ref[d, 0:1, :]
        c = ins[0][...]
        sg = 1.0 / (1.0 + jnp.exp(-c))
        update(0, slice(0, 1), slice(None), acc * (sg * (1.0 + c * (1.0 - sg))))
        for l in range(DEPTH):
            for i in range(N_MOD):
                g = tot_ref[l * mod_rows + i:l * mod_rows + i + 1, :]
                for grp in range(1, n_groups):
                    g = g + tot_ref[l * mod_rows + grp * N_MOD + i:l * mod_rows + grp * N_MOD + i + 1, :]
                update(1, slice(l, l + 1), slice(i * D_MODEL, (i + 1) * D_MODEL), g)
            for j in range(4):
                row = DEPTH * mod_rows + 4 * l + j
                update(2 + j, slice(l, l + 1), slice(None), tot_ref[row:row + 1, :])
            head = tot_ref[head_row + l:head_row + l + 1, :]
            update(6, slice(l, l + 1), slice(None), head[:, 0:HEAD_DIM] + head[:, HEAD_DIM:2 * HEAD_DIM])
            update(7, slice(l, l + 1), slice(None), head[:, 2 * HEAD_DIM:3 * HEAD_DIM] + head[:, 3 * HEAD_DIM:4 * HEAD_DIM])
            update(8, slice(l, l + 1), slice(None), head[:, 4 * HEAD_DIM:4 * HEAD_DIM + ins[3 * 8].shape[1]])

    shapes = [jax.ShapeDtypeStruct(w.shape, F32) for w, _, _ in params for _ in range(4)]
    outs = pl.pallas_call(body, name=name, out_shape=shapes)(tot, dcc_parts, *[a for p in params for a in p])
    return [tuple(outs[4 * p:4 * p + 4]) for p in range(n_p)]


def _adamw(w, g, m, v, name):
    r, c = w.shape
    tr = _pick(r, (256, 128, 64, 32, 24, 16, 8))

    def body(w_ref, g_ref, m_ref, v_ref, d_ref, nm_ref, nv_ref):
        d_ref[...], nm_ref[...], nv_ref[...] = _adam_val(w_ref[...], g_ref[...], m_ref[...], v_ref[...])

    spec = pl.BlockSpec((tr, c), lambda i: (i, 0))
    return pl.pallas_call(body, name=name, grid=(r // tr,), in_specs=[spec] * 4, out_specs=[spec] * 3,
                          out_shape=[jax.ShapeDtypeStruct((r, c), F32)] * 3, compiler_params=_params(("parallel",)))(w, g, m, v)


def _adamw_shard(kind, l, w, m, v, halves, off, prev, name):
    h = PACK_HEIGHT[kind]
    assert off % h == 0, (kind, off)
    _, r, c = w.shape
    rows = r // 2

    def body(w_ref, m_ref, v_ref, p_ref, *rest):
        g_ref, d_ref, nm_ref, nv_ref = rest[-4:]
        if kind == "in":
            for t in range(2):
                g = p_ref[:, t * IN_PIECE_COLS:(t + 1) * IN_PIECE_COLS]
                rs = slice(t * h, (t + 1) * h)
                g_ref[rs, :] = g
                d_ref[rs, :], nm_ref[rs, :], nv_ref[rs, :] = _adam_val(w_ref[rs, :], g, m_ref[rs, :], v_ref[rs, :])
        else:
            g = p_ref[...]
            g_ref[...] = g
            d_ref[...], nm_ref[...], nv_ref[...] = _adam_val(w_ref[...], g, m_ref[...], v_ref[...])

    blk = pl.BlockSpec((None, rows, c), lambda half: (l, half, 0))
    in_specs = [blk, blk, blk, pl.BlockSpec((None, h, 1024), lambda half: (half, off // h, 0))]
    args = [w, m, v, halves]
    aliases = {}
    if prev is not None:
        in_specs += [pl.BlockSpec(memory_space=pl.ANY)] * 4
        args += list(prev)
        aliases = {4 + j: j for j in range(4)}
    return pl.pallas_call(
        body, name=name, grid=(2,), in_specs=in_specs, out_specs=[blk] * 4,
        out_shape=[jax.ShapeDtypeStruct(w.shape, F32)] * 4, input_output_aliases=aliases,
        compiler_params=_params(("parallel",)))(*args)


SMALL_ROWS = 48


def _small_rows(small, sq):
    def lane_pad(v):
        return jnp.pad(v, (0, D_MODEL - v.shape[0]))[None]

    head_rows = [lane_pad(jnp.concatenate([s["q_norm"][0], s["k_norm"][0], s["sink"]])) for s in small]
    loss_row = lane_pad((0.5 / D_MODEL) * jnp.sum(sq, keepdims=True)[0])
    rows = jnp.concatenate([s["mod"].reshape(-1, D_MODEL) for s in small] + [s["gammas"] for s in small] + head_rows + [loss_row], axis=0)
    return jnp.pad(rows, ((0, SMALL_ROWS - rows.shape[0]), (0, 0)))


def _local_step(x, ctx, target, mods, gam, qn, kn, sink, w_first, w_layers, packed, kc_idx):
    nb, seq, _ = x.shape
    rt = _Rows(nb, seq, ctx.shape[1])
    rt_lat = rt.latent_only()
    tables = _rope_tables(rt)
    fuse = packed is not None
    h = (x.reshape(rt.n_lat, D_MODEL), ctx.reshape(rt.n_ctx, D_MODEL))
    wg = [{}, {}] if fuse else [dict(w) for w in w_layers]
    wg[0]["in"] = (w_first, 0)
    if fuse:
        wg[0]["in_own"] = (packed, W_FIRST[0])
    saved = []
    for l in range(DEPTH):
        g_pre_mix, g_post_mix, g_pre_mlp, g_post_mlp = gam[l]
        if l == 0:
            u, qkv, qkvp, h = _in_fwd(rt, h, g_pre_mix, mods[l], wg[l], tables, qn[l], kn[l], f"in_fwd{l}")
        else:
            u, qkv, qkvp = _in_fwd(rt, h, g_pre_mix, mods[l], wg[l], tables, qn[l], kn[l], f"in_fwd{l}")
        if fuse and l == 0:
            o, lse_lat, w_mlp0, w_out0, w_in1 = _attn_fwd(rt, qkvp, sink[l], None, f"attn_lat_fwd{l}",
                                                         comm=_gather_comm(packed, [W_MLP0, W_OUT0, W_IN1], lead=2))
            wg[0].update({kind: (w_mlp0, PACK_OFF[(kind, 0)] - W_MLP0[0]) for kind in ("up", "down")})
            wg[0]["out"] = (w_out0, 0)
            wg[1] = {"in": (w_in1, 0)}
        elif fuse:
            o, lse_lat, w_mlp1, w_out1 = _attn_fwd(rt, qkvp, sink[l], None, f"attn_lat_fwd{l}",
                                                   comm=_gather_comm(packed, [W_MLP1, W_OUT1], lead=2))
            wg[1].update({kind: (w_mlp1, PACK_OFF[(kind, 1)] - W_MLP1[0]) for kind in ("up", "down")})
            wg[1]["out"] = (w_out1, 0)
        else:
            o, lse_lat = _attn_fwd(rt, qkvp, sink[l], None, f"attn_lat_fwd{l}")
        if l < DEPTH - 1:
            o, lse_ctx = _attn_fwd(rt, qkvp, sink[l], o, f"attn_ctx_fwd{l}")
            mix, h1, u2 = _out_fwd(rt, o, wg[l], h, mods[l], g_post_mix, g_pre_mlp, f"out_fwd{l}")
            r, y, h2 = _mlp_fwd(rt, u2, h1, wg[l], mods[l], g_post_mlp, f"mlp_fwd{l}")
        else:
            lse_ctx = None
            mix, h1, u2 = _out_fwd(rt_lat, o, wg[l], h, mods[l], g_post_mix, g_pre_mlp, f"out_fwd{l}")
            r, y, dh, sq = _mlp_fwd(rt_lat, u2, h1, wg[l], mods[l], g_post_mlp, f"mlp_fwd{l}", target=target.reshape(rt.n_lat, D_MODEL))
        saved.append((h, u, qkv, qkvp, o, lse_lat, lse_ctx, mix, h1, u2, r, y))
        h = h2

    small = [None] * DEPTH
    groups = {}
    for l in reversed(range(DEPTH)):
        g_pre_mix, g_post_mix, g_pre_mlp, g_post_mlp = gam[l]
        h0, u, qkv, qkvp, o, lse_lat, lse_ctx, mix, h1, u2, r, y = saved[l]
        mlp_group, mix_group = (G_LAYER1, G_LAYER1) if l == 1 else (G_MLP0, G_MIX0)
        hide = fuse and l == 0

        dead_ctx = l == DEPTH - 1
        rt_b = rt_lat if dead_ctx else rt
        dy, da, d_gate_m, d_g_post_mlp = _mlp_down_bwd(rt_b, dh, y, r, wg[l], mods[l], g_post_mlp, f"mlp_down_bwd{l}")
        p_mlp = _wgrad_packed(rt_b, r, dy, "down", PACK_OFF[("down", l)] - mlp_group[0], mlp_group[1], None, f"mlp_down_wgrad{l}",
                              comm=_pair_comm(groups[G_LAYER1]) if hide else None)
        if hide:
            p_mlp, r1 = p_mlp
            sum1 = _pair_sum(groups[G_LAYER1], r1, kc_idx, "grad_pair_sum_layer1")
        p_mlp = _wgrad_packed(rt_b, u2, da, "up", PACK_OFF[("up", l)] - mlp_group[0], mlp_group[1], p_mlp, f"mlp_up_wgrad{l}")
        outs = _mlp_up_bwd(rt_b, da, wg[l], h1, dh, mods[l], g_pre_mlp, f"mlp_up_bwd{l}", comm=_pair_comm(p_mlp) if hide else None)
        dh1, d_sh_m, d_sc_m, d_g_pre_mlp = outs[:4]
        if hide:
            sum0 = _pair_sum(p_mlp, outs[4], kc_idx, "grad_pair_sum_mlp0")
        dmix, do, d_gate_a, d_g_post_mix = _out_bwd(rt_b, dh1, mix, wg[l], mods[l], g_post_mix, f"out_bwd{l}")
        p_mix = _wgrad_packed(rt_b, o, dmix, "out", PACK_OFF[("out", l)] - mix_group[0], mix_group[1],
                              p_mlp if l == 1 else None, f"out_wgrad{l}")
        outs = _attn_bwd(rt, qkvp, o, lse_lat, do, sink[l], None, f"attn_lat_bwd{l}",
                         comm=_chip_comm([sum1[1], sum0[1]]) if hide else None)
        dq, dkv, dkv_c, dsink1 = outs[:4]
        if hide:
            groups[G_LAYER1] = _owner_sum(sum1[0], outs[4], kc_idx, "grad_owner_sum_layer1")
            groups[G_MLP0] = _owner_sum(sum0[0], outs[5], kc_idx, "grad_owner_sum_mlp0")
        if dead_ctx:
            dsink2 = jnp.zeros_like(dsink1)
            d_gate_m, d_sh_m, d_sc_m, d_gate_a = [a.at[nb].set(0.0) for a in (d_gate_m, d_sh_m, d_sc_m, d_gate_a)]
        else:
            dq, dkv, dsink2 = _attn_bwd(rt, qkvp, o, lse_ctx, do, sink[l], (dq, dkv, dkv_c), f"attn_ctx_bwd{l}")
        dqkv, dh, dqn, dkn, d_sh_a, d_sc_a, d_g_pre_mix = _in_bwd(rt, dq, dkv, qkv, tables, qn[l], kn[l], wg[l], h0, dh1, mods[l],
                                                                  g_pre_mix, l == 0, f"in_bwd{l}",
                                                                  dead_ctx_dkv=dkv_c if dead_ctx else None)
        dmod = jnp.concatenate([d_sh_a, d_sc_a, d_gate_a, d_sh_m, d_sc_m, d_gate_m], axis=1)
        small[l] = dict(mod=dmod, gammas=jnp.concatenate([d_g_pre_mix, d_g_post_mix, d_g_pre_mlp, d_g_post_mlp], axis=0),
                        q_norm=dqn, k_norm=dkn, sink=(dsink1 + dsink2)[:, 0])
        tail = _merge([_gather_comm(_small_rows(small, sq), [(0, SMALL_ROWS)]),
                       _halves_comm([groups[G_LAYER1], groups[G_MLP0]])]) if hide else None
        outs = _wgrad_packed(rt, u, dqkv, "in", PACK_OFF[("in", l)] - mix_group[0], mix_group[1], p_mix, f"in_wgrad{l}", comm=tail)
        if hide:
            groups[mix_group], small_g, groups[G_LAYER1], groups[G_MLP0] = outs
        else:
            groups[mix_group], small_g = outs, None
            if l == 0:
                groups[G_MLP0] = p_mlp
    return sq, dh.reshape(nb, seq, D_MODEL), [groups[G_LAYER1], groups[G_MLP0], groups[G_MIX0]], small, small_g


def kernel(x, c, ctx, c_ctx, w_ada, b_ada, g_pre_mix, g_post_mix, g_pre_mlp, g_post_mlp, w_in, q_norm, k_norm, sink, w_out, w_up, w_down, loss_target, m_c_ctx, m_w_ada, m_b_ada, m_g_pre_mix, m_g_post_mix, m_g_pre_mlp, m_g_post_mlp, m_w_in, m_q_norm, m_k_norm, m_sink, m_w_out, m_w_up, m_w_down, v_c_ctx, v_w_ada, v_b_ada, v_g_pre_mix, v_g_post_mix, v_g_pre_mlp, v_g_post_mlp, v_w_in, v_q_norm, v_k_norm, v_sink, v_w_out, v_w_up, v_w_down):
    nb = x.shape[0]
    ix, iy, ic = lax.axis_index("x"), lax.axis_index("y"), lax.axis_index("c")
    chip = 2 * ix + iy
    dev = 2 * chip + ic
    ada_cols = w_ada.shape[2] // 2

    c_rows = c.reshape(8, (nb * D_MODEL) // 8)
    packed, c_all = _pack_local_half(w_in, w_out, w_up, w_down, _gather_comm(c_rows, [(0, c_rows.shape[0])]), "pack_gather_c")
    c_all = c_all.reshape(N_DEV * nb, D_MODEL)
    n_cond = N_DEV * nb + 1
    cond_rows = 16 * ((n_cond + 15) // 16)
    cond = jnp.concatenate([c_all, c_ctx[None, :], jnp.zeros((cond_rows - n_cond, D_MODEL), F32)], axis=0)
    c_idx = ic.reshape(1).astype(jnp.int32)
    kc_idx = jnp.stack([chip, ic]).astype(jnp.int32)
    b_ada_half = lax.dynamic_slice_in_dim(b_ada, dev * ada_cols, ada_cols, 1)[:, None, :]
    x_ada, mod_part = _ada_fwd(cond, w_ada, b_ada_half, c_idx, "ada_fwd")
    mod_rows2d = mod_part.reshape(DEPTH * cond_rows, ada_cols)
    mod_g, w_first = _comm_alone(_merge([_gather_comm(mod_rows2d, [(0, mod_rows2d.shape[0])]),
                                         _gather_comm(packed, [W_FIRST], copy_own=False, cols=2 * IN_PIECE_COLS)]),
                               "gather_mod_w_first")
    mod_all = mod_g.reshape(N_DEV, DEPTH, cond_rows, ada_cols).transpose(1, 2, 0, 3).reshape(DEPTH, cond_rows, N_MOD * D_MODEL)
    mods = []
    for l in range(DEPTH):
        mine = lax.dynamic_slice_in_dim(mod_all[l], dev * nb, nb, 0)
        mods.append(jnp.concatenate([mine, mod_all[l, n_cond - 1:n_cond]], axis=0).reshape(nb + 1, N_MOD, D_MODEL))

    gam = [(g_pre_mix[l][None], g_post_mix[l][None], g_pre_mlp[l][None], g_post_mlp[l][None]) for l in range(DEPTH)]
    qn = [jnp.tile(q_norm[l], 2)[None] for l in range(DEPTH)]
    kn = [jnp.tile(k_norm[l], 2)[None] for l in range(DEPTH)]
    _, grad_x, (h_layer1, h_mlp0, p_mix0), _, small_g = _local_step(x, ctx, loss_target, mods, gam, qn, kn, [sink[l] for l in range(DEPTH)],
                                                                 w_first, None, packed, kc_idx)

    def step(w, g, m, v, name):
        shape = w.shape
        cols = shape[-1]
        outs = _adamw(w.reshape(-1, cols), g.reshape(-1, cols), m.reshape(-1, cols), v.reshape(-1, cols), name)
        return tuple(a.reshape(shape) for a in outs)

    def shard_update(kind, w, m, v, layer0, layer1):
        outs = None
        for l, (halves, group) in enumerate((layer0, layer1)):
            outs = _adamw_shard(kind, l, w, m, v, halves, PACK_OFF[(kind, l)] - group[0], outs, f"adamw_w_{kind}{l}")
        return tuple(outs)

    tot = _dev_sum(small_g, "small_sum")
    mod_rows = (nb + 1) * N_MOD
    loss = tot[DEPTH * mod_rows + 4 * DEPTH + DEPTH, 0]

    ex = small_g[:, :DEPTH * mod_rows].reshape(N_DEV, DEPTH, nb + 1, N_MOD * D_MODEL)[:, :, :nb]
    ex = ex.transpose(1, 0, 2, 3).reshape(DEPTH, N_DEV * nb, N_MOD * D_MODEL)
    cx = tot[:DEPTH * mod_rows].reshape(DEPTH, nb + 1, N_MOD * D_MODEL)[:, nb:]
    dm = jnp.concatenate([ex, cx, jnp.zeros((DEPTH, cond_rows - n_cond, N_MOD * D_MODEL), F32)], axis=1)
    shard_cols = w_ada.shape[2]
    grad_w_ada = _ada_wgrad(x_ada, lax.dynamic_slice_in_dim(dm, chip * shard_cols, shard_cols, 2).astype(BF16), "ada_wgrad")
    dcx = jnp.pad(lax.dynamic_slice_in_dim(cx, dev * ada_cols, ada_cols, 2), ((0, 0), (0, 15), (0, 0))).astype(BF16)
    dcc = _ada_cond_bwd(dcx, w_ada, c_idx, "ada_cond_bwd")[0:8]

    r1, = _comm_alone(_pair_comm(p_mix0), "grad_pair_exchange_mix0")
    a32, a16 = _pair_sum(p_mix0, r1, kc_idx, "grad_pair_sum_mix0")
    r2, dcc_g = _comm_alone(_merge([_chip_comm([a16]), _gather_comm(dcc, [(0, dcc.shape[0])])]), "grad_chip_exchange_mix0")
    h_mix0 = _owner_sum(a32, r2, kc_idx, "grad_owner_sum_mix0")
    h_mix0, = _comm_alone(_halves_comm([h_mix0]), "grad_halves_exchange_mix0")

    small_names = ["c_ctx", "b_ada", "g_pre_mix", "g_post_mix", "g_pre_mlp", "g_post_mlp", "q_norm", "k_norm", "sink"]
    assert q_norm.shape[1] == HEAD_DIM and k_norm.shape[1] == HEAD_DIM
    small_res = _small_update(tot, dcc_g, [(c_ctx[None], m_c_ctx[None], v_c_ctx[None]), (b_ada, m_b_ada, v_b_ada),
                                           (g_pre_mix, m_g_pre_mix, v_g_pre_mix), (g_post_mix, m_g_post_mix, v_g_post_mix),
                                           (g_pre_mlp, m_g_pre_mlp, v_g_pre_mlp), (g_post_mlp, m_g_post_mlp, v_g_post_mlp),
                                           (q_norm, m_q_norm, v_q_norm), (k_norm, m_k_norm, v_k_norm), (sink, m_sink, v_sink)],
                              nb + 1, "small_update")
    res = {n: r for n, r in zip(small_names, small_res)}
    res["c_ctx"] = tuple(a[0] for a in res["c_ctx"])
    res["w_ada"] = (grad_w_ada, *step(w_ada, grad_w_ada, m_w_ada, v_w_ada, "adamw_w_ada"))
    res["w_up"] = shard_update("up", w_up, m_w_up, v_w_up, (h_mlp0, G_MLP0), (h_layer1, G_LAYER1))
    res["w_down"] = shard_update("down", w_down, m_w_down, v_w_down, (h_mlp0, G_MLP0), (h_layer1, G_LAYER1))
    res["w_in"] = shard_update("in", w_in, m_w_in, v_w_in, (h_mix0, G_MIX0), (h_layer1, G_LAYER1))
    res["w_out"] = shard_update("out", w_out, m_w_out, v_w_out, (h_mix0, G_MIX0), (h_layer1, G_LAYER1))

    order = ["c_ctx", "w_ada", "b_ada", "g_pre_mix", "g_post_mix", "g_pre_mlp", "g_post_mlp", "w_in", "q_norm", "k_norm", "sink", "w_out", "w_up", "w_down"]
    return (loss, grad_x, *[res[n][0] for n in order], *[res[n][1] for n in order],
            *[res[n][2] for n in order], *[res[n][3] for n in order])
```

```python
import functools

import jax
import jax.numpy as jnp
import numpy as np
from jax import lax
from jax.experimental import pallas as pl
from jax.experimental.pallas import tpu as pltpu

F32 = jnp.float32
BF16 = jnp.bfloat16

D_MODEL = 1024
HEAD_DIM = 64
GROUP = 4
WINDOW = 128
N_MOD = 6
D_FF = 4 * D_MODEL
IN_COLS = 1536
GRID_W = 64
ROPE_THETA = 10000.0
EPS = 1e-6
NEG_BIG = -1e30
Q_SCALE = HEAD_DIM ** -0.5
DEPTH = 2
N_DEV = 8

ADAM_LR = 0.001
ADAM_B1 = 0.9
ADAM_B2 = 0.999
ADAM_EPS = 1e-08
ADAM_WD = 0.01
ADAM_STEP = 10

V7X_VMEM_BYTES = 64 * 1024 * 1024
VMEM_LIMIT = V7X_VMEM_BYTES - 8 * 1024 * 1024

MESH = pl.DeviceIdType.MESH
NT = (((1,), (1,)), ((), ()))
TN = (((0,), (0,)), ((), ()))

COL_KA, COL_VA, COL_KB, COL_VB = 4, 5, 10, 11
NORMED_COLS = 640

PACK_HEIGHT = {"up": 512, "down": 512, "in": 256, "out": 128}
IN_PIECE_COLS = 384
PACK_OFF = {("up", 0): 0, ("down", 0): 512, ("in", 0): 1024, ("out", 0): 1280,
            ("up", 1): 1408, ("down", 1): 1920, ("in", 1): 2432, ("out", 1): 2688}
PACK_ROWS = 2816
W_FIRST, W_MLP0, W_OUT0, W_IN1, W_MLP1, W_OUT1 = (1024, 256), (0, 1024), (1280, 128), (2432, 256), (1408, 1024), (2688, 128)
G_LAYER1, G_MLP0, G_MIX0 = (1408, 1408), (0, 1024), (1024, 384)


def _pick(n, cands):
    for t in cands:
        if n % t == 0:
            return t
    raise ValueError(f"no tile for {n}")


def _params(sem):
    return pltpu.CompilerParams(dimension_semantics=sem, vmem_limit_bytes=VMEM_LIMIT)


class _Comm:
    def __init__(self, inputs, out_shapes, aliases, n_send, n_recv, start, finish, relay=None, lead=0):
        self.inputs, self.out_shapes, self.aliases = list(inputs), list(out_shapes), dict(aliases)
        self.n_send, self.n_recv, self.start, self.finish, self.relay, self.lead = n_send, n_recv, start, finish, relay, lead


def _comm_call(compute, comm, *, name, grid, in_specs, out_specs, out_shape, args, aliases, semantics, scratch=()):
    in_specs, out_specs, out_shape, args, aliases = list(in_specs), list(out_specs), list(out_shape), list(args), dict(aliases)
    scratch = list(scratch)
    if comm is None:
        return pl.pallas_call(compute, name=name, grid=grid, in_specs=in_specs, out_specs=out_specs, out_shape=out_shape,
                              input_output_aliases=aliases, scratch_shapes=scratch, compiler_params=_params(semantics))(*args)
    n_in, n_out, n_ci, n_co = len(args), len(out_shape), len(comm.inputs), len(comm.out_shapes)
    hbm = pl.BlockSpec(memory_space=pl.ANY)
    aliases.update({n_in + i: n_out + o for i, o in comm.aliases.items()})

    def body(*refs):
        ins, c_ins = refs[:n_in], refs[n_in:n_in + n_ci]
        outs, c_outs = refs[n_in + n_ci:n_in + n_ci + n_out], refs[n_in + n_ci + n_out:n_in + n_ci + n_out + n_co]
        scr = refs[n_in + n_ci + n_out + n_co:-2]
        send_sems, recv_sems = refs[-2:]
        ids = [pl.program_id(a) for a in range(len(grid))]
        first = functools.reduce(jnp.logical_and, [i == 0 for i in ids])
        last = functools.reduce(jnp.logical_and, [i == g - 1 for i, g in zip(ids, grid)])

        @pl.when(first)
        def _():
            comm.start(c_ins, c_outs, send_sems, recv_sems)

        compute(*ins, *outs, *scr)

        if comm.relay is not None:
            step = functools.reduce(lambda acc, ig: acc * ig[1] + ig[0], zip(ids, grid), 0)

            @pl.when(step == int(np.prod(grid)) - 1 - comm.lead)
            def _():
                comm.relay(c_ins, c_outs, send_sems, recv_sems)

        @pl.when(last)
        def _():
            comm.finish(c_ins, c_outs, send_sems, recv_sems)

    return pl.pallas_call(
        body, name=name, grid=grid,
        in_specs=in_specs + [hbm] * n_ci, out_specs=out_specs + [hbm] * n_co, out_shape=out_shape + comm.out_shapes,
        input_output_aliases=aliases,
        scratch_shapes=scratch + [pltpu.SemaphoreType.DMA((comm.n_send,)), pltpu.SemaphoreType.DMA((comm.n_recv,))],
        compiler_params=_params(("arbitrary",) * len(grid)),
    )(*args, *comm.inputs)


def _place():
    x_, y_, c_ = lax.axis_index("x"), lax.axis_index("y"), lax.axis_index("c")
    return x_, y_, c_, [(1 - x_, y_), (x_, 1 - y_), (1 - x_, 1 - y_)]


GATHER_SENDS, GATHER_RECVS = 8, 7


def _gather_copies(packed_ref, wg_ref, send_sems, recv_sems, rows, nth=0):
    r0, n = rows
    x_, y_, c_, chips = _place()
    me, sibling = (x_, y_, c_), (x_, y_, 1 - c_)
    src = packed_ref.at[pl.ds(r0, n), pl.ds(0, wg_ref.shape[2])]

    def slot(px, py, pc):
        return wg_ref.at[4 * px + 2 * py + pc]

    def copy(k, block, to, from_packed=False):
        return pltpu.make_async_remote_copy(src_ref=src if from_packed else slot(*block), dst_ref=slot(*block),
                                            send_sem=send_sems.at[GATHER_SENDS * nth + k], recv_sem=recv_sems.at[GATHER_RECVS * nth + k],
                                            device_id=to, device_id_type=MESH)

    own = [copy(0, me, sibling, True)] + [copy(1 + j, me, (*chip, c_), True) for j, chip in enumerate(chips)]
    passed = [copy(4 + j, (*chip, c_), sibling) for j, chip in enumerate(chips)]
    over_ici = [copy(1 + j, (*chip, c_), me) for j, chip in enumerate(chips)]
    from_sibling = [copy(0, sibling, me)] + [copy(4 + j, (*chip, 1 - c_), me) for j, chip in enumerate(chips)]
    mine = pltpu.make_async_copy(src, slot(*me), send_sems.at[GATHER_SENDS * nth + 7])
    return mine, own, passed, over_ici, from_sibling


def _gather_start(packed_ref, wg_ref, send_sems, recv_sems, rows, nth=0, copy_own=True):
    mine, own, _, _, _ = _gather_copies(packed_ref, wg_ref, send_sems, recv_sems, rows, nth)
    if copy_own:
        mine.start()
    for cp in own:
        cp.start()


def _gather_relay(packed_ref, wg_ref, send_sems, recv_sems, rows, nth=0):
    _, _, passed, over_ici, _ = _gather_copies(packed_ref, wg_ref, send_sems, recv_sems, rows, nth)
    for arrived, onward in zip(over_ici, passed):
        arrived.wait_recv()
        onward.start()


def _gather_finish(packed_ref, wg_ref, send_sems, recv_sems, rows, nth=0, copy_own=True):
    mine, own, passed, _, from_sibling = _gather_copies(packed_ref, wg_ref, send_sems, recv_sems, rows, nth)
    for arrived in from_sibling:
        arrived.wait_recv()
    for cp in own + passed:
        cp.wait_send()
    if copy_own:
        mine.wait()


def _gather_comm(packed, ranges, copy_own=True, lead=0, cols=None):
    shapes = [jax.ShapeDtypeStruct((N_DEV, n, cols or packed.shape[1]), packed.dtype) for _, n in ranges]

    def start(ins, outs, ss, rs):
        for nth, rows in enumerate(ranges):
            _gather_start(ins[0], outs[nth], ss, rs, rows, nth, copy_own)

    def relay(ins, outs, ss, rs):
        for nth, rows in enumerate(ranges):
            _gather_relay(ins[0], outs[nth], ss, rs, rows, nth)

    def finish(ins, outs, ss, rs):
        for nth, rows in enumerate(ranges):
            _gather_finish(ins[0], outs[nth], ss, rs, rows, nth, copy_own)

    return _Comm([packed], shapes, {}, GATHER_SENDS * len(ranges), GATHER_RECVS * len(ranges), start, finish, relay, lead)


def _pair_copy(p_ref, out_ref, send_sems, recv_sems):
    x_, y_, c_, _ = _place()
    return pltpu.make_async_remote_copy(src_ref=p_ref.at[1 - c_], dst_ref=out_ref,
                                        send_sem=send_sems.at[0], recv_sem=recv_sems.at[0],
                                        device_id=(x_, y_, 1 - c_), device_id_type=MESH)


def _pair_comm(p):
    return _Comm([p], [jax.ShapeDtypeStruct(p.shape[1:], p.dtype)], {}, 1, 1,
                 lambda ins, outs, ss, rs: _pair_copy(ins[0], outs[0], ss, rs).start(),
                 lambda ins, outs, ss, rs: _pair_copy(ins[0], outs[0], ss, rs).wait())


def _chip_copies(a_refs, out_refs, send_sems, recv_sems):
    _, _, c_, chips = _place()
    return [pltpu.make_async_remote_copy(src_ref=a_ref.at[2 * tx + ty], dst_ref=o_ref.at[j],
                                         send_sem=send_sems.at[3 * g + j], recv_sem=recv_sems.at[3 * g + j],
                                         device_id=(tx, ty, c_), device_id_type=MESH)
            for g, (a_ref, o_ref) in enumerate(zip(a_refs, out_refs)) for j, (tx, ty) in enumerate(chips)]


def _chip_start(a_refs, out_refs, send_sems, recv_sems):
    for cp in _chip_copies(a_refs, out_refs, send_sems, recv_sems):
        cp.start()


def _chip_finish(a_refs, out_refs, send_sems, recv_sems):
    for cp in _chip_copies(a_refs, out_refs, send_sems, recv_sems):
        cp.wait()


def _chip_comm(arrays):
    shapes = [jax.ShapeDtypeStruct((3,) + a.shape[1:], a.dtype) for a in arrays]
    return _Comm(arrays, shapes, {}, 3 * len(arrays), 3 * len(arrays), _chip_start, _chip_finish)


def _halves_copies(in_refs, out_refs, send_sems, recv_sems):
    x_, y_, c_, _ = _place()
    return [pltpu.make_async_remote_copy(src_ref=o_ref.at[c_], dst_ref=o_ref.at[c_], send_sem=send_sems.at[i], recv_sem=recv_sems.at[i],
                                         device_id=(x_, y_, 1 - c_), device_id_type=MESH)
            for i, o_ref in enumerate(out_refs)]


def _halves_start(in_refs, out_refs, send_sems, recv_sems):
    for cp in _halves_copies(in_refs, out_refs, send_sems, recv_sems):
        cp.start()


def _halves_finish(in_refs, out_refs, send_sems, recv_sems):
    for cp in _halves_copies(in_refs, out_refs, send_sems, recv_sems):
        cp.wait()


def _halves_comm(arrays):
    shapes = [jax.ShapeDtypeStruct(a.shape, a.dtype) for a in arrays]
    return _Comm(arrays, shapes, {i: i for i in range(len(arrays))}, len(arrays), len(arrays), _halves_start, _halves_finish)


class _SemSlice:
    class _At:
        def __init__(self, sems, first):
            self.sems, self.first = sems, first

        def __getitem__(self, k):
            return self.sems.at[self.first + k]

    def __init__(self, sems, first):
        self.at = _SemSlice._At(sems, first)


def _merge(comms):
    inputs = [a for c in comms for a in c.inputs]
    shapes = [s for c in comms for s in c.out_shapes]
    aliases, spans = {}, []
    i0 = o0 = s0 = r0 = 0
    for c in comms:
        aliases.update({i0 + i: o0 + o for i, o in c.aliases.items()})
        spans.append((slice(i0, i0 + len(c.inputs)), slice(o0, o0 + len(c.out_shapes)), s0, r0))
        i0, o0, s0, r0 = i0 + len(c.inputs), o0 + len(c.out_shapes), s0 + c.n_send, r0 + c.n_recv

    def start(ins, outs, ss, rs):
        for c, (i, o, s, r) in zip(comms, spans):
            c.start(ins[i], outs[o], _SemSlice(ss, s), _SemSlice(rs, r))

    def relay(ins, outs, ss, rs):
        for c, (i, o, s, r) in zip(comms, spans):
            if c.relay is not None:
                c.relay(ins[i], outs[o], _SemSlice(ss, s), _SemSlice(rs, r))

    def finish(ins, outs, ss, rs):
        for c, (i, o, s, r) in zip(comms, spans):
            c.finish(ins[i], outs[o], _SemSlice(ss, s), _SemSlice(rs, r))

    leads = [c.lead for c in comms if c.relay is not None]
    return _Comm(inputs, shapes, aliases, s0, r0, start, finish, relay if leads else None, max(leads, default=0))


def _comm_alone(comm, name):
    n_ci = len(comm.inputs)
    hbm = pl.BlockSpec(memory_space=pl.ANY)

    def body(*refs):
        c_ins, c_outs, send_sems, recv_sems = refs[:n_ci], refs[n_ci:-2], refs[-2], refs[-1]
        comm.start(c_ins, c_outs, send_sems, recv_sems)
        if comm.relay is not None:
            comm.relay(c_ins, c_outs, send_sems, recv_sems)
        comm.finish(c_ins, c_outs, send_sems, recv_sems)

    return pl.pallas_call(
        body, name=name, out_shape=comm.out_shapes, in_specs=[hbm] * n_ci, out_specs=[hbm] * len(comm.out_shapes),
        input_output_aliases=comm.aliases,
        scratch_shapes=[pltpu.SemaphoreType.DMA((comm.n_send,)), pltpu.SemaphoreType.DMA((comm.n_recv,))],
    )(*comm.inputs)


SUM_TILES = (704, 512, 384, 320, 256, 192, 128, 64)


def _pair_sum(p, r1, kc_idx, name):
    _, _, n, c = p.shape
    tr = _pick(n, SUM_TILES)

    def body(s_ref, p_ref, r_ref, o32_ref, o16_ref):
        v = p_ref[...] + r_ref[...]
        o16_ref[...] = v.astype(BF16)

        @pl.when(pl.program_id(1) == s_ref[0])
        def _():
            o32_ref[...] = v

    blk = pl.BlockSpec((None, tr, c), lambda i, j, s: (j, i, 0))
    grid_spec = pltpu.PrefetchScalarGridSpec(
        num_scalar_prefetch=1, grid=(n // tr, 4),
        in_specs=[pl.BlockSpec((None, None, tr, c), lambda i, j, s: (s[1], j, i, 0)), blk],
        out_specs=[pl.BlockSpec((tr, c), lambda i, j, s: (i, 0)), blk])
    return pl.pallas_call(
        body, name=name, grid_spec=grid_spec,
        out_shape=[jax.ShapeDtypeStruct((n, c), F32), jax.ShapeDtypeStruct((4, n, c), BF16)],
        compiler_params=_params(("arbitrary", "arbitrary")),
    )(kc_idx, p, r1)


def _owner_sum(a32, r2, kc_idx, name):
    r, c = a32.shape
    tr = _pick(r, SUM_TILES)

    def body(s_ref, a_ref, r_ref, o_ref):
        v = a_ref[...]
        for j in range(3):
            v = v + r_ref[j].astype(F32)
        o_ref[...] = v

    grid_spec = pltpu.PrefetchScalarGridSpec(
        num_scalar_prefetch=1, grid=(r // tr,),
        in_specs=[pl.BlockSpec((tr, c), lambda i, s: (i, 0)),
                  pl.BlockSpec((3, tr, c), lambda i, s: (0, i, 0))],
        out_specs=pl.BlockSpec((None, tr, c), lambda i, s: (s[1], i, 0)))
    return pl.pallas_call(
        body, name=name, grid_spec=grid_spec,
        out_shape=jax.ShapeDtypeStruct((2, r, c), F32),
        compiler_params=_params(("arbitrary",)),
    )(kc_idx, a32, r2)


def _pack_local_half(w_in_s, w_out_s, w_up_s, w_down_s, comm, name):
    shards = {"in": w_in_s, "out": w_out_s, "up": w_up_s, "down": w_down_s}
    kinds = list(shards)
    assert sorted(off + PACK_HEIGHT[kind] for (kind, _), off in PACK_OFF.items()) == sorted(PACK_OFF.values())[1:] + [PACK_ROWS]
    for kind in kinds:
        assert shards[kind].shape[1] == (4 if kind == "in" else 2) * PACK_HEIGHT[kind], (kind, shards[kind].shape)

    def body(*refs):
        w_refs, p_ref = dict(zip(kinds, refs[:4])), refs[4]
        scr, sems = dict(zip(kinds, refs[5:9])), refs[9]
        c = lax.axis_index("c")
        copies = {}
        for n, (kind, l) in enumerate(sorted(PACK_OFF)):
            rows = scr[kind].shape[1]
            copies[(kind, l)] = pltpu.make_async_copy(w_refs[kind].at[l, pl.ds(c * rows, rows)], scr[kind].at[l], sems.at[n])
            copies[(kind, l)].start()
        for (kind, l), off in sorted(PACK_OFF.items(), key=lambda kv: kv[1]):
            copies[(kind, l)].wait()
            h = PACK_HEIGHT[kind]
            if kind == "in":
                for t in range(2):
                    p_ref[off:off + h, t * IN_PIECE_COLS:(t + 1) * IN_PIECE_COLS] = scr[kind][l, t * h:(t + 1) * h, :].astype(BF16)
                p_ref[off:off + h, 2 * IN_PIECE_COLS:] = jnp.zeros((h, 1024 - 2 * IN_PIECE_COLS), BF16)
            else:
                p_ref[off:off + h, :] = scr[kind][l].astype(BF16)

    hbm = pl.BlockSpec(memory_space=pl.ANY)
    scratch = [pltpu.VMEM((DEPTH, shards[kind].shape[1] // 2, shards[kind].shape[2]), F32) for kind in kinds]
    outs = _comm_call(
        body, comm, name=name, grid=(1,), in_specs=[hbm] * 4,
        out_specs=[pl.BlockSpec((PACK_ROWS, 1024), lambda i: (0, 0))],
        out_shape=[jax.ShapeDtypeStruct((PACK_ROWS, 1024), BF16)],
        args=[shards[kind] for kind in kinds], aliases={}, semantics=("arbitrary",),
        scratch=scratch + [pltpu.SemaphoreType.DMA((len(PACK_OFF),))])
    return outs


def _unpack_in_pieces(w_ref, own_ref, w_scr):
    if own_ref is not None:
        me = 4 * lax.axis_index("x") + 2 * lax.axis_index("y") + lax.axis_index("c")
    for d in range(N_DEV):
        k, c = d // 2, d % 2
        for t in range(2):
            piece = w_ref[d, :, t * IN_PIECE_COLS:(t + 1) * IN_PIECE_COLS]
            if own_ref is not None:
                piece = jnp.where(me == d, own_ref[:, t * IN_PIECE_COLS:(t + 1) * IN_PIECE_COLS], piece)
            w_scr[c * 512 + t * 256:c * 512 + (t + 1) * 256, k * IN_PIECE_COLS:(k + 1) * IN_PIECE_COLS] = piece


def _in_weight_operands(wg):
    specs, args = [_gathered_spec(wg, "in")], [wg["in"][0]]
    if "in_own" in wg:
        own, off = wg["in_own"]
        h = PACK_HEIGHT["in"]
        assert off % h == 0
        specs.append(pl.BlockSpec((h, 1024), lambda *_: (off // h, 0), pipeline_mode=pl.Buffered(1)))
        args.append(own)
    return specs, args


class _Rows:
    def __init__(self, nb, seq, ctx):
        self.nb, self.seq, self.ctx = nb, seq, ctx
        self.n_lat, self.n_ctx = nb * seq, nb * ctx
        self.rows = self.n_lat + self.n_ctx
        self.tm = _pick(np.gcd(seq, self.n_ctx), (512, 256, 128))
        self.tiles_per_ex = seq // self.tm
        self.n_tiles = self.rows // self.tm
        self.n_lat_tiles = self.n_lat // self.tm
        self.groups = nb + 1

    def latent_only(self):
        rt = _Rows(self.nb, self.seq, self.ctx)
        rt.n_tiles = self.n_lat_tiles
        return rt

    def group(self, i):
        return jnp.minimum(i // self.tiles_per_ex, self.nb)

    def first_of_group(self, i):
        return jnp.logical_and(i % self.tiles_per_ex == 0, i <= self.n_lat_tiles)


def _mod_spec(rt):
    return pl.BlockSpec((1, N_MOD, D_MODEL), lambda i: (rt.group(i), 0, 0))


def _row_spec(rt, cols):
    return pl.BlockSpec((rt.tm, cols), lambda i: (i, 0))


def _vec_spec(cols):
    return pl.BlockSpec((1, cols), lambda i: (0, 0))


def _group_spec(rt):
    return pl.BlockSpec((1, 1, D_MODEL), lambda i: (rt.group(i), 0, 0))


def _gathered_spec(wg, kind):
    h, off = PACK_HEIGHT[kind], wg[kind][1]
    assert off % h == 0, (kind, off)
    return pl.BlockSpec((N_DEV, h, wg[kind][0].shape[2]), lambda *_: (0, off // h, 0), pipeline_mode=pl.Buffered(1))


def _group_shape(rt):
    return jax.ShapeDtypeStruct((rt.groups, 1, D_MODEL), F32)


def _vec_shape(cols=D_MODEL):
    return jax.ShapeDtypeStruct((1, cols), F32)


def _rms_inv(v):
    return lax.rsqrt(jnp.mean(v * v, axis=-1, keepdims=True) + EPS)


def _norm_mod_val(h_, g_, mod_ref, i_shift, i_scale):
    n = h_ * _rms_inv(h_) * g_
    return n * (1.0 + mod_ref[0, i_scale:i_scale + 1, :]) + mod_ref[0, i_shift:i_shift + 1, :]


def _post_norm_val(h_, z_, g_, mod_ref, i_gate):
    return h_ + mod_ref[0, i_gate:i_gate + 1, :] * (z_ * _rms_inv(z_) * g_)


def _post_norm_bwd_val(dh_, z_, g_, gate):
    rinv = _rms_inv(z_)
    n0 = z_ * rinv
    dn = dh_ * gate * g_
    dz = rinv * (dn - n0 * jnp.mean(dn * n0, axis=-1, keepdims=True))
    return dz, jnp.sum(dh_ * n0 * g_, axis=0, keepdims=True), jnp.sum(dh_ * gate * n0, axis=0, keepdims=True)


def _norm_mod_bwd_val(du_, h_, g_, one_sc):
    rinv = _rms_inv(h_)
    n0 = h_ * rinv
    dn = du_ * g_ * one_sc
    dh = rinv * (dn - n0 * jnp.mean(dn * n0, axis=-1, keepdims=True))
    return (dh, jnp.sum(du_, axis=0, keepdims=True), jnp.sum(du_ * n0 * g_, axis=0, keepdims=True),
            jnp.sum(du_ * one_sc * n0, axis=0, keepdims=True))


def _accumulate(rt, i, group_pairs, global_pairs):
    @pl.when(rt.first_of_group(i))
    def _():
        for ref, _ in group_pairs:
            ref[...] = jnp.zeros_like(ref)

    @pl.when(i == 0)
    def _():
        for ref, _ in global_pairs:
            ref[...] = jnp.zeros_like(ref)

    for ref, val in group_pairs:
        ref[0] += val
    for ref, val in global_pairs:
        ref[...] += val


def _rope_tables(rt):
    pos = np.arange(rt.seq)
    axis_dim = HEAD_DIM // 2
    inv = (ROPE_THETA ** (-np.arange(0, axis_dim, 2, dtype=np.float32) / axis_dim)).astype(np.float32)
    ang_r = (pos // GRID_W).astype(np.float32)[:, None] * inv[None, :]
    ang_c = (pos % GRID_W).astype(np.float32)[:, None] * inv[None, :]
    cr, sr, cc, sc = np.cos(ang_r), np.sin(ang_r), np.cos(ang_c), np.sin(ang_c)
    zero = np.zeros_like(sr)
    cos = np.concatenate([cr, cr, cc, cc], axis=1)
    s_lo = np.concatenate([zero, sr, zero, sc], axis=1)
    s_hi = np.concatenate([-sr, zero, -sc, zero], axis=1)

    def full(t, ctx_value):
        return jnp.asarray(np.concatenate([np.tile(t, (1, 2)), np.full((rt.tm, 128), ctx_value)], axis=0), F32)

    return full(cos, 1.0), full(s_lo, 0.0), full(s_hi, 0.0)


def _table_spec(rt):
    return pl.BlockSpec((rt.tm, 128), lambda i: (jnp.where(i < rt.n_lat_tiles, i % rt.tiles_per_ex, rt.tiles_per_ex), 0))


def _head_mean(x):
    r = lax.broadcasted_iota(jnp.int32, (128, 128), 0) // HEAD_DIM
    c = lax.broadcasted_iota(jnp.int32, (128, 128), 1) // HEAD_DIM
    ones = jnp.where(r == c, 1.0 / HEAD_DIM, 0.0).astype(F32)
    return jnp.dot(x, ones, preferred_element_type=F32, precision=lax.Precision.HIGH)


def _head_stats(t):
    return lax.rsqrt(_head_mean(t * t) + EPS)


def _prep_fwd_body(tm, qkv_ref, c, s1, s2, qn, kn, out_ref):
    def rope(t):
        return t * c + pltpu.roll(t, 16, 1) * s1 + pltpu.roll(t, 112, 1) * s2

    for j in range(12):
        t = qkv_ref[:, j * 128:(j + 1) * 128]
        if j < 4:
            t = rope(t * _head_stats(t) * qn) * Q_SCALE
        elif j == COL_KA:
            t = rope(t * _head_stats(t) * kn)
        elif 6 <= j < 10:
            t = rope(t) * Q_SCALE
        elif j == COL_KB:
            t = rope(t)
        out_ref[:, j * 128:(j + 1) * 128] = t.astype(BF16)


def _prep_bwd_body(dq, dkv, qkv_ref, c, s1, s2, qn, kn, out_ref):
    rows = slice(None)

    def rope_bwd(d):
        return d * c + pltpu.roll(d * s1, 112, 1) + pltpu.roll(d * s2, 16, 1)

    def norm_bwd(t, g, dy):
        rinv = _head_stats(t)
        n = t * rinv
        dn = dy * g
        return rinv * (dn - n * _head_mean(dn * n)), jnp.sum(dy * n, axis=0, keepdims=True)

    dqn = jnp.zeros((1, 128), F32)
    dkn = jnp.zeros((1, 128), F32)
    for j in range(12):
        if j < 4:
            d, dg = norm_bwd(qkv_ref[rows, j * 128:(j + 1) * 128], qn, rope_bwd(dq(slice(j * 128, (j + 1) * 128)) * Q_SCALE))
            dqn = dqn + dg
        elif j == COL_KA:
            d, dg = norm_bwd(qkv_ref[rows, j * 128:(j + 1) * 128], kn, rope_bwd(dkv(slice(0, 128))))
            dkn = dkn + dg
        elif j == COL_VA:
            d = dkv(slice(128, 256))
        elif j < 10:
            d = rope_bwd(dq(slice((j - 2) * 128, (j - 1) * 128)) * Q_SCALE)
        elif j == COL_KB:
            d = rope_bwd(dkv(slice(256, 384)))
        else:
            d = dkv(slice(384, 512))
        out_ref[rows, j * 128:(j + 1) * 128] = d.astype(BF16)
    return dqn, dkn


def _in_fwd(rt, h, gamma, mod, wg, tables, qn, kn, name):
    w_specs, w_args = _in_weight_operands(wg)
    n_w = len(w_args)
    joined = not isinstance(h, (tuple, list))
    n_h = 1 if joined else 2

    def body(*refs):
        g_ref, mod_ref = refs[n_h:n_h + 2]
        rest = refs[n_h + 2:]
        c_ref, s1_ref, s2_ref, qn_ref, kn_ref, u_ref, qkn_ref, qkvp_ref = rest[n_w:n_w + 8]
        qkv_ref, w_scr = rest[-2:]
        i = pl.program_id(0)

        @pl.when(i == 0)
        def _():
            _unpack_in_pieces(rest[0], rest[1] if n_w == 2 else None, w_scr)

        if joined:
            h_ = refs[0][...]
        else:
            h_ = jnp.where(i < rt.n_lat_tiles, refs[0][...], refs[1][...])
            rest[n_w + 8][...] = h_
        u = _norm_mod_val(h_, g_ref[...], mod_ref, 0, 1).astype(BF16)
        u_ref[...] = u
        qkv_ref[...] = jnp.dot(u, w_scr[...], preferred_element_type=F32)
        qkn_ref[...] = qkv_ref[:, 0:NORMED_COLS]
        _prep_fwd_body(rt.tm, qkv_ref, c_ref[...], s1_ref[...], s2_ref[...], qn_ref[...], kn_ref[...], qkvp_ref)

    if joined:
        h_specs, h_args = [_row_spec(rt, D_MODEL)], [h]
    else:
        h_specs = [pl.BlockSpec((rt.tm, D_MODEL), lambda i: (jnp.minimum(i, rt.n_lat_tiles - 1), 0)),
                   pl.BlockSpec((rt.tm, D_MODEL), lambda i: (jnp.maximum(i - rt.n_lat_tiles, 0), 0))]
        h_args = list(h)
    out_specs = [_row_spec(rt, D_MODEL), _row_spec(rt, NORMED_COLS), _row_spec(rt, IN_COLS)]
    out_shape = [jax.ShapeDtypeStruct((rt.rows, D_MODEL), BF16), jax.ShapeDtypeStruct((rt.rows, NORMED_COLS), F32),
                 jax.ShapeDtypeStruct((rt.rows, IN_COLS), BF16)]
    if not joined:
        out_specs.append(_row_spec(rt, D_MODEL))
        out_shape.append(jax.ShapeDtypeStruct((rt.rows, D_MODEL), F32))
    return pl.pallas_call(
        body, name=name, grid=(rt.n_tiles,),
        in_specs=h_specs + [_vec_spec(D_MODEL), _mod_spec(rt)] + w_specs + [_table_spec(rt)] * 3 + [_vec_spec(128)] * 2,
        out_specs=out_specs, out_shape=out_shape,
        scratch_shapes=[pltpu.VMEM((rt.tm, IN_COLS), F32), pltpu.VMEM((D_MODEL, IN_COLS), BF16)],
        compiler_params=_params(("arbitrary",)),
    )(*h_args, gamma, mod, *w_args, *tables, qn, kn)


def _in_bwd(rt, dq, dkv, qkv, tables, qn, kn, wg, h, dres, mod, gamma, latent_only, name, comm=None, dead_ctx_dkv=None):
    last = rt.n_lat_tiles - 1
    w_specs, w_args = _in_weight_operands(wg)
    n_w = len(w_args)
    n_dead = 0 if dead_ctx_dkv is None else 1

    def body(dq_ref, dkv_ref, qkv_ref, c_ref, s1_ref, s2_ref, qn_ref, kn_ref, *rest):
        h_ref, dres_ref, mod_ref, g_ref, dqkv_ref, dh_ref, dqn_ref, dkn_ref, dsh_ref, dsc_ref, dg_ref, w_scr = rest[n_w + n_dead:]
        i = pl.program_id(0)

        @pl.when(i == 0)
        def _():
            _unpack_in_pieces(rest[0], rest[1] if n_w == 2 else None, w_scr)

        if n_dead:
            c1_ref, lat = rest[n_w], i <= last
            load_dq = lambda cols: jnp.where(lat, dq_ref[:, cols], 0.0)
            load_dkv = lambda cols: jnp.where(lat, dkv_ref[:, cols], c1_ref[:, cols])
            dres_ = jnp.where(lat, dres_ref[...], 0.0)
        else:
            load_dq, load_dkv, dres_ = (lambda cols: dq_ref[:, cols]), (lambda cols: dkv_ref[:, cols]), dres_ref[...]
        dqn, dkn = _prep_bwd_body(load_dq, load_dkv, qkv_ref, c_ref[...], s1_ref[...], s2_ref[...], qn_ref[...], kn_ref[...], dqkv_ref)
        du = lax.dot_general(dqkv_ref[...], w_scr[...], NT, preferred_element_type=F32)
        dh, dsh, dsc, dg = _norm_mod_bwd_val(du, h_ref[...], g_ref[...], 1.0 + mod_ref[0, 1:2, :])
        if latent_only:
            @pl.when(i <= last)
            def _():
                dh_ref[...] = dres_ + dh
        else:
            dh_ref[...] = dres_ + dh
        _accumulate(rt, i, [(dsh_ref, dsh), (dsc_ref, dsc)], [(dg_ref, dg), (dqn_ref, dqn), (dkn_ref, dkn)])

    dh_spec = pl.BlockSpec((rt.tm, D_MODEL), lambda i: (jnp.minimum(i, last), 0)) if latent_only else _row_spec(rt, D_MODEL)
    dead_specs = [] if dead_ctx_dkv is None else [pl.BlockSpec((rt.tm, 512), lambda i: (jnp.maximum(i - rt.n_lat_tiles, 0), 0))]
    dead_args = [] if dead_ctx_dkv is None else [dead_ctx_dkv]
    return _comm_call(
        body, comm, name=name, grid=(rt.n_tiles,),
        in_specs=[_row_spec(rt, 1024), _row_spec(rt, 512), _row_spec(rt, NORMED_COLS)] + [_table_spec(rt)] * 3 + [_vec_spec(128)] * 2
        + w_specs + dead_specs + [_row_spec(rt, D_MODEL), _row_spec(rt, D_MODEL), _mod_spec(rt), _vec_spec(D_MODEL)],
        out_specs=[_row_spec(rt, IN_COLS), dh_spec, _vec_spec(128), _vec_spec(128),
                   _group_spec(rt), _group_spec(rt), _vec_spec(D_MODEL)],
        out_shape=[jax.ShapeDtypeStruct((rt.rows, IN_COLS), BF16),
                   jax.ShapeDtypeStruct((rt.n_lat if latent_only else rt.rows, D_MODEL), F32),
                   _vec_shape(128), _vec_shape(128), _group_shape(rt), _group_shape(rt), _vec_shape()],
        args=[dq, dkv, qkv, *tables, qn, kn, *w_args, *dead_args, h, dres, mod, gamma], aliases={}, semantics=("arbitrary",),
        scratch=[pltpu.VMEM((D_MODEL, IN_COLS), BF16)])


def _out_fwd(rt, o, wg, h, mod, g_post_mix, g_pre_mlp, name):
    def body(o_ref, w_ref, h_ref, mod_ref, gpost_ref, gpre_ref, mix_ref, h1_ref, u2_ref):
        mix = jnp.dot(o_ref[...], w_ref[...].reshape(D_MODEL, D_MODEL), preferred_element_type=F32)
        mix_ref[...] = mix
        h1 = _post_norm_val(h_ref[...], mix, gpost_ref[...], mod_ref, 2)
        h1_ref[...] = h1
        u2_ref[...] = _norm_mod_val(h1, gpre_ref[...], mod_ref, 3, 4).astype(BF16)

    return pl.pallas_call(
        body, name=name, grid=(rt.n_tiles,),
        in_specs=[_row_spec(rt, D_MODEL), _gathered_spec(wg, "out"), _row_spec(rt, D_MODEL), _mod_spec(rt),
                  _vec_spec(D_MODEL), _vec_spec(D_MODEL)],
        out_specs=[_row_spec(rt, D_MODEL)] * 3,
        out_shape=[jax.ShapeDtypeStruct((rt.rows, D_MODEL), F32), jax.ShapeDtypeStruct((rt.rows, D_MODEL), F32),
                   jax.ShapeDtypeStruct((rt.rows, D_MODEL), BF16)],
        compiler_params=_params(("parallel",)),
    )(o, wg["out"][0], h, mod, g_post_mix, g_pre_mlp)


def _out_bwd(rt, dh1, mix, wg, mod, g_post_mix, name):
    def body(dh_ref, mix_ref, w_ref, mod_ref, g_ref, dmix_ref, do_ref, dgate_ref, dg_ref):
        i = pl.program_id(0)
        dz, dgate, dg = _post_norm_bwd_val(dh_ref[...], mix_ref[...], g_ref[...], mod_ref[0, 2:3, :])
        dzb = dz.astype(BF16)
        dmix_ref[...] = dzb
        do_ref[...] = lax.dot_general(dzb, w_ref[...].reshape(D_MODEL, D_MODEL), NT, preferred_element_type=F32).astype(BF16)
        _accumulate(rt, i, [(dgate_ref, dgate)], [(dg_ref, dg)])

    return pl.pallas_call(
        body, name=name, grid=(rt.n_tiles,),
        in_specs=[_row_spec(rt, D_MODEL), _row_spec(rt, D_MODEL), _gathered_spec(wg, "out"), _mod_spec(rt), _vec_spec(D_MODEL)],
        out_specs=[_row_spec(rt, D_MODEL), _row_spec(rt, D_MODEL), _group_spec(rt), _vec_spec(D_MODEL)],
        out_shape=[jax.ShapeDtypeStruct((rt.rows, D_MODEL), BF16), jax.ShapeDtypeStruct((rt.rows, D_MODEL), BF16),
                   _group_shape(rt), _vec_shape()],
        compiler_params=_params(("arbitrary",)),
    )(dh1, mix, wg["out"][0], mod, g_post_mix)


def _w_chunk(w_ref, k):
    return w_ref[2 * k:2 * k + 2].reshape(1024, 1024)


def _mlp_fwd(rt, u2, h1, wg, mod, g_post_mlp, name, comm=None, target=None):
    last = rt.n_lat_tiles - 1

    def body(u2_ref, h1_ref, wu_ref, wd_ref, mod_ref, g_ref, *rest):
        u2_ = u2_ref[...]
        y = jnp.zeros((rt.tm, D_MODEL), F32)
        for k in range(D_FF // 1024):
            a = jnp.maximum(jnp.dot(u2_, _w_chunk(wu_ref, k), preferred_element_type=F32), 0.0)
            rest[-3 if target is None else -4][:, k * 1024:(k + 1) * 1024] = a.astype(BF16)
            y = y + jnp.dot((a * a).astype(BF16), _w_chunk(wd_ref, k), preferred_element_type=F32)
        h2 = _post_norm_val(h1_ref[...], y, g_ref[...], mod_ref, 5)
        if target is None:
            _, y_ref, h2_ref = rest
            y_ref[...] = y
            h2_ref[...] = h2
        else:
            t_ref, _, y_ref, dh_ref, sq_ref = rest
            y_ref[...] = y
            i = pl.program_id(0)

            @pl.when(i == 0)
            def _():
                sq_ref[...] = jnp.zeros_like(sq_ref)

            @pl.when(i <= last)
            def _():
                e = h2 - t_ref[...]
                dh_ref[...] = e * (1.0 / D_MODEL)
                sq_ref[...] += jnp.sum(e * e, axis=0, keepdims=True)

            @pl.when(i > last)
            def _():
                dh_ref[...] = jnp.zeros_like(dh_ref)

    in_specs = [_row_spec(rt, D_MODEL), _row_spec(rt, D_MODEL), _gathered_spec(wg, "up"), _gathered_spec(wg, "down"),
                _mod_spec(rt), _vec_spec(D_MODEL)]
    args = [u2, h1, wg["up"][0], wg["down"][0], mod, g_post_mlp]
    out_specs = [_row_spec(rt, D_FF), _row_spec(rt, D_MODEL), _row_spec(rt, D_MODEL)]
    out_shape = [jax.ShapeDtypeStruct((rt.rows, D_FF), BF16), jax.ShapeDtypeStruct((rt.rows, D_MODEL), F32),
                 jax.ShapeDtypeStruct((rt.rows, D_MODEL), F32)]
    if target is not None:
        in_specs.append(pl.BlockSpec((rt.tm, D_MODEL), lambda i: (jnp.minimum(i, last), 0)))
        args.append(target)
        out_specs.append(_vec_spec(D_MODEL))
        out_shape.append(_vec_shape())
    return _comm_call(body, comm, name=name, grid=(rt.n_tiles,), in_specs=in_specs, out_specs=out_specs, out_shape=out_shape,
                      args=args, aliases={}, semantics=("parallel",) if target is None else ("arbitrary",))


def _mlp_down_bwd(rt, dh, y, ra, wg, mod, g_post_mlp, name):
    def body(dh_ref, y_ref, ra_ref, wd_ref, mod_ref, g_ref, dy_ref, da_ref, dgate_ref, dg_ref):
        i = pl.program_id(0)
        dz, dgate, dg = _post_norm_bwd_val(dh_ref[...], y_ref[...], g_ref[...], mod_ref[0, 5:6, :])
        dyb = dz.astype(BF16)
        dy_ref[...] = dyb
        for k in range(D_FF // 1024):
            dr = lax.dot_general(dyb, _w_chunk(wd_ref, k), NT, preferred_element_type=F32)
            da_ref[:, k * 1024:(k + 1) * 1024] = (dr * (2.0 * ra_ref[:, k * 1024:(k + 1) * 1024].astype(F32))).astype(BF16)
        _accumulate(rt, i, [(dgate_ref, dgate)], [(dg_ref, dg)])

    return pl.pallas_call(
        body, name=name, grid=(rt.n_tiles,),
        in_specs=[_row_spec(rt, D_MODEL), _row_spec(rt, D_MODEL), _row_spec(rt, D_FF), _gathered_spec(wg, "down"),
                  _mod_spec(rt), _vec_spec(D_MODEL)],
        out_specs=[_row_spec(rt, D_MODEL), _row_spec(rt, D_FF), _group_spec(rt), _vec_spec(D_MODEL)],
        out_shape=[jax.ShapeDtypeStruct((rt.rows, D_MODEL), BF16), jax.ShapeDtypeStruct((rt.rows, D_FF), BF16),
                   _group_shape(rt), _vec_shape()],
        compiler_params=_params(("arbitrary",)),
    )(dh, y, ra, wg["down"][0], mod, g_post_mlp)


def _mlp_up_bwd(rt, da, wg, h1, dh, mod, g_pre_mlp, name, comm=None):
    def body(da_ref, wu_ref, h1_ref, dh_ref, mod_ref, g_ref, dh1_ref, dsh_ref, dsc_ref, dg_ref):
        i = pl.program_id(0)
        du = jnp.zeros((rt.tm, D_MODEL), F32)
        for k in range(D_FF // 1024):
            du = du + lax.dot_general(da_ref[:, k * 1024:(k + 1) * 1024], _w_chunk(wu_ref, k), NT, preferred_element_type=F32)
        d, dsh, dsc, dg = _norm_mod_bwd_val(du, h1_ref[...], g_ref[...], 1.0 + mod_ref[0, 4:5, :])
        dh1_ref[...] = dh_ref[...] + d
        _accumulate(rt, i, [(dsh_ref, dsh), (dsc_ref, dsc)], [(dg_ref, dg)])

    return _comm_call(
        body, comm, name=name, grid=(rt.n_tiles,),
        in_specs=[_row_spec(rt, D_FF), _gathered_spec(wg, "up"), _row_spec(rt, D_MODEL), _row_spec(rt, D_MODEL),
                  _mod_spec(rt), _vec_spec(D_MODEL)],
        out_specs=[_row_spec(rt, D_MODEL), _group_spec(rt), _group_spec(rt), _vec_spec(D_MODEL)],
        out_shape=[jax.ShapeDtypeStruct((rt.rows, D_MODEL), F32), _group_shape(rt), _group_shape(rt), _vec_shape()],
        args=[da, wg["up"][0], h1, dh, mod, g_pre_mlp], aliases={}, semantics=("arbitrary",))


def _wgrad_packed(rt, a, b, kind, off, n_rows, p_prev, name, comm=None):
    h = PACK_HEIGHT[kind]
    tk = rt.tm
    assert off % h == 0, (kind, off)

    def body(a_ref, b_ref, *rest):
        o_ref = rest[-1]
        i = pl.program_id(0)

        @pl.when(i == 0)
        def _():
            o_ref[...] = jnp.zeros_like(o_ref)

        if kind == "in":
            res = lax.dot_general(a_ref[...], b_ref[...], TN, preferred_element_type=F32)
            for k in range(4):
                for c in range(2):
                    for t in range(2):
                        o_ref[c, k, :, t * IN_PIECE_COLS:(t + 1) * IN_PIECE_COLS] += \
                            res[c * 512 + t * h:c * 512 + (t + 1) * h, k * IN_PIECE_COLS:(k + 1) * IN_PIECE_COLS]
        elif kind == "out":
            res = lax.dot_general(a_ref[...], b_ref[...], TN, preferred_element_type=F32)
            for k in range(4):
                for c in range(2):
                    o_ref[c, k] += res[(2 * k + c) * h:(2 * k + c + 1) * h]
        else:
            for k in range(4):
                if kind == "up":
                    res = lax.dot_general(a_ref[...], b_ref[:, k * 1024:(k + 1) * 1024], TN, preferred_element_type=F32)
                else:
                    ra = a_ref[:, k * 1024:(k + 1) * 1024].astype(F32)
                    res = lax.dot_general((ra * ra).astype(BF16), b_ref[...], TN, preferred_element_type=F32)
                o_ref[0, k] += res[0:h]
                o_ref[1, k] += res[h:2 * h]

    in_specs = [pl.BlockSpec((tk, a.shape[1]), lambda i: (i, 0)), pl.BlockSpec((tk, b.shape[1]), lambda i: (i, 0))]
    args = [a, b]
    aliases = {}
    if p_prev is not None:
        in_specs.append(pl.BlockSpec(memory_space=pl.ANY))
        args.append(p_prev)
        aliases = {2: 0}
    outs = _comm_call(
        body, comm, name=name, grid=(rt.n_tiles,),
        in_specs=in_specs,
        out_specs=[pl.BlockSpec((2, 4, h, 1024), lambda i: (0, 0, off // h, 0))],
        out_shape=[jax.ShapeDtypeStruct((2, 4, n_rows, 1024), F32)],
        args=args, aliases=aliases, semantics=("arbitrary",))
    return outs[0] if comm is None else outs


def _ada_wgrad(xs, dm, name):
    depth, _, cols = dm.shape

    def body(x_ref, d_ref, o_ref):
        for l in range(depth):
            o_ref[l] = lax.dot_general(x_ref[...], d_ref[l], TN, preferred_element_type=F32)

    return pl.pallas_call(body, name=name, out_shape=jax.ShapeDtypeStruct((depth, xs.shape[1], cols), F32),
                          compiler_params=pltpu.CompilerParams(vmem_limit_bytes=VMEM_LIMIT))(xs, dm)


def _stack_heads(x, kvi):
    x = x.astype(F32)
    tq = x.shape[0]
    lane = lax.broadcasted_iota(jnp.int32, (tq, 128), 1)
    keep = lane < HEAD_DIM if kvi == 0 else lane >= HEAD_DIM
    parts = []
    for p in range(2):
        pair = x[:, p * 128:(p + 1) * 128]
        swapped = pltpu.roll(pair, HEAD_DIM, 1)
        lo_head, hi_head = (pair, swapped) if kvi == 0 else (swapped, pair)
        parts += [jnp.where(keep, lo_head, 0.0), jnp.where(keep, hi_head, 0.0)]
    return jnp.concatenate(parts, axis=0).astype(BF16)


def _unstack_heads(o4, kvi):
    tq = o4.shape[0] // GROUP
    lane = lax.broadcasted_iota(jnp.int32, (tq, 128), 1)
    outs = []
    for p in range(2):
        r_lo, r_hi = o4[(2 * p) * tq:(2 * p + 1) * tq], o4[(2 * p + 1) * tq:(2 * p + 2) * tq]
        if kvi == 0:
            lo, hi = r_lo, pltpu.roll(r_hi, HEAD_DIM, 1)
        else:
            lo, hi = pltpu.roll(r_lo, HEAD_DIM, 1), r_hi
        outs.append(jnp.where(lane < HEAD_DIM, lo, hi))
    return jnp.concatenate(outs, axis=1)


def _per_head(shape, axis, tq, values):
    head = lax.broadcasted_iota(jnp.int32, shape, axis) // tq
    out = jnp.zeros(shape, F32)
    for g in range(GROUP):
        out = jnp.where(head == g, values[g], out)
    return out


KEY_CHUNK = 512
Q_TILE = 128
Q_TILE_FWD = 256


def _key_chunks(k_ref, v_ref, n, kc=KEY_CHUNK):
    kc = min(kc, n)
    return [(k_ref[c * kc:(c + 1) * kc, :], v_ref[c * kc:(c + 1) * kc, :], None) for c in range(n // kc)]


def _softmax_fwd(qs, chunks, sink_col):
    logits = []
    for k, _, mask in chunks:
        s = lax.dot_general(qs, k, NT, preferred_element_type=F32)
        logits.append(s if mask is None else jnp.where(mask, s, NEG_BIG))
    m = functools.reduce(jnp.maximum, [jnp.max(s, axis=1, keepdims=True) for s in logits])
    if sink_col is not None:
        m = jnp.maximum(m, sink_col)
    l = jnp.zeros_like(m) if sink_col is None else jnp.exp(sink_col - m)
    acc = jnp.zeros((qs.shape[0], 128), F32)
    for s, (_, v, _) in zip(logits, chunks):
        p = jnp.exp(s - m)
        l = l + jnp.sum(p, axis=1, keepdims=True)
        acc = acc + jnp.dot(p.astype(BF16), v, preferred_element_type=F32)
    return acc / l, m + jnp.log(l)


def _to_rows(col):
    return jnp.transpose(jnp.broadcast_to(col, (col.shape[0], 128)))[0:8, :]


def _softmax_bwd(qs, dos, lse_row, delta_row, chunks):
    dq = jnp.zeros((qs.shape[0], 128), F32)
    grads = []
    for k, v, mask in chunks:
        s = lax.dot_general(k, qs, NT, preferred_element_type=F32)
        if mask is not None:
            s = jnp.where(mask, s, NEG_BIG)
        p = jnp.exp(s - lse_row)
        dp = lax.dot_general(v, dos, NT, preferred_element_type=F32)
        ds = (p * (dp - delta_row)).astype(BF16)
        dv = jnp.dot(p.astype(BF16), dos, preferred_element_type=F32)
        dk = jnp.dot(ds, qs, preferred_element_type=F32)
        dq = dq + lax.dot_general(ds, k, TN, preferred_element_type=F32)
        grads.append((dk, dv))
    return dq, grads


def _band(qi, tq, seq):
    span = tq + 2 * WINDOW
    start = pl.multiple_of(jnp.clip(qi * tq - WINDOW, 0, seq - span), 64)
    return start, span


def _band_mask(qi, tq, start, span, query_axis):
    shape = (GROUP * tq, span) if query_axis == 0 else (span, GROUP * tq)
    qpos = qi * tq + lax.broadcasted_iota(jnp.int32, shape, query_axis) % tq
    kpos = start + lax.broadcasted_iota(jnp.int32, shape, 1 - query_axis)
    return jnp.abs(kpos - qpos) <= WINDOW


def _qkv_specs(rt, tq, q_row, ctx_row, with_latent):
    specs = [pl.BlockSpec((tq, 256), functools.partial(lambda b, i, col: (q_row(b, i), col), col=col)) for col in (0, 1, 3, 4)]
    if with_latent:
        specs += [pl.BlockSpec((rt.seq, 128), functools.partial(lambda b, i, col: (b, col), col=col))
                  for col in (COL_KA, COL_VA, COL_KB, COL_VB)]
    specs += [pl.BlockSpec((rt.ctx, 128), functools.partial(lambda b, i, col: (ctx_row(b), col), col=col))
              for col in (COL_KA, COL_VA, COL_KB, COL_VB)]
    return specs


def _attn_fwd(rt, qkvp, sink, o_prev, name, comm=None):
    latent = o_prev is None
    seq, ctx, nb = rt.seq, rt.ctx, rt.nb
    tq = Q_TILE_FWD if latent else ctx
    tile = Q_TILE if latent else ctx
    parts = tq // tile
    nq = seq // tq if latent else 1
    ctx_blk0 = rt.n_lat // ctx
    q_row = (lambda b, i: b * nq + i) if latent else (lambda b, i: ctx_blk0 + b)

    def store_lse(lse_ref, j, lse_col):
        rows = _to_rows(lse_col)
        for part in range(parts):
            lse_ref[part, j] = jnp.concatenate([rows[:, g * tq + part * tile:g * tq + (part + 1) * tile] for g in range(GROUP)], axis=1)

    def body(sink_ref, qa0, qa1, qb0, qb1, *rest):
        if latent:
            kal, val, kbl, vbl, kac, vac, kbc, vbc, o_ref, lse_ref = rest
        else:
            kac, vac, kbc, vbc, _, o_ref, lse_ref = rest
        qi = pl.program_id(1)
        for kvi, (qa, qb) in enumerate(((qa0, qb0), (qa1, qb1))):
            src_a = _key_chunks(kac, vac, ctx)
            src_b = _key_chunks(kbc, vbc, ctx)
            if latent:
                src_a += _key_chunks(kal, val, seq, seq)
                start, span = _band(qi, tq, seq)
                src_b.append((kbl[pl.ds(start, span), :], vbl[pl.ds(start, span), :], _band_mask(qi, tq, start, span, 0)))
            oa, lse = _softmax_fwd(_stack_heads(qa[...], kvi), src_a, None)
            o_ref[:, kvi * 256:(kvi + 1) * 256] = _unstack_heads(oa, kvi).astype(BF16)
            store_lse(lse_ref, kvi, lse)
            sink_col = _per_head((GROUP * tq, 1), 0, tq, [sink_ref[kvi * GROUP + g] for g in range(GROUP)])
            ob, lse = _softmax_fwd(_stack_heads(qb[...], kvi), src_b, sink_col)
            o_ref[:, 512 + kvi * 256:512 + (kvi + 1) * 256] = _unstack_heads(ob, kvi).astype(BF16)
            store_lse(lse_ref, 2 + kvi, lse)

    specs = _qkv_specs(rt, tq, q_row, lambda b: ctx_blk0 + b, latent)
    args = [sink] + [qkvp] * len(specs)
    in_specs = [pl.BlockSpec(memory_space=pltpu.SMEM)] + specs
    aliases = {}
    if not latent:
        in_specs.append(pl.BlockSpec(memory_space=pl.ANY))
        args.append(o_prev)
        aliases = {len(args) - 1: 0}
    return _comm_call(
        body, comm, name=name, grid=(nb, nq),
        in_specs=in_specs,
        out_specs=[pl.BlockSpec((tq, 1024), lambda b, i: (q_row(b, i), 0)),
                   pl.BlockSpec((parts, 4, 8, GROUP * tile), lambda b, i: (b * nq + i, 0, 0, 0))],
        out_shape=[jax.ShapeDtypeStruct((rt.rows, 1024), BF16), jax.ShapeDtypeStruct((nb * nq * parts, 4, 8, GROUP * tile), F32)],
        args=args, aliases=aliases, semantics=("parallel", "parallel"))


def _attn_bwd(rt, qkvp, o, lse, do, sink, prev, name, comm=None):
    latent = prev is None
    seq, ctx, nb = rt.seq, rt.ctx, rt.nb
    tq = Q_TILE if latent else ctx
    nq = seq // tq if latent else 1
    ctx_blk0 = rt.n_lat // ctx
    q_row = (lambda b, i: b * nq + i) if latent else (lambda b, i: ctx_blk0 + b)
    kc = min(KEY_CHUNK, seq)

    def body(sink_ref, qa0, qa1, qb0, qb1, *rest):
        if latent:
            kal, val, kbl, vbl, kac, vac, kbc, vbc, do_ref, o_ref, lse_ref, dq_ref, dl_ref, dc_ref, dsink_ref = rest
        else:
            kac, vac, kbc, vbc, do_ref, o_ref, lse_ref, c1_ref, _, _, dq_ref, dc_ref, dsink_ref = rest
        b, qi = pl.program_id(0), pl.program_id(1)

        def rows_of(cols, kvi, mixer):
            dos = _stack_heads(do_ref[:, cols], kvi)
            delta = jnp.sum(dos.astype(F32) * _stack_heads(o_ref[:, cols], kvi).astype(F32), axis=1, keepdims=True)
            return dos, lse_ref[0, 2 * mixer + kvi, 0:1, :], _to_rows(delta)[0:1, :]

        @pl.when(jnp.logical_and(b == 0, qi == 0))
        def _():
            dsink_ref[...] = jnp.zeros_like(dsink_ref)

        if latent:
            @pl.when(qi == 0)
            def _():
                dc_ref[...] = jnp.zeros_like(dc_ref)
                dl_ref[...] = jnp.zeros_like(dl_ref)
        else:
            dc_ref[...] = c1_ref[...]

        head_row = lax.broadcasted_iota(jnp.int32, (8, 128), 0)
        for kvi, (qa, qb) in enumerate(((qa0, qb0), (qa1, qb1))):
            cols = slice(kvi * 256, (kvi + 1) * 256)
            dos, lse_row, delta_row = rows_of(cols, kvi, 0)
            src = _key_chunks(kac, vac, ctx)
            if latent:
                src += _key_chunks(kal, val, seq)
            dq4, grads = _softmax_bwd(_stack_heads(qa[...], kvi), dos, lse_row, delta_row, src)
            dq_ref[:, cols] = _unstack_heads(dq4, kvi)
            dc_ref[:, 0:128] += grads[0][0]
            dc_ref[:, 128:256] += grads[0][1]
            for c, (dk, dv) in enumerate(grads[1:]):
                dl_ref[c * kc:(c + 1) * kc, 0:128] += dk
                dl_ref[c * kc:(c + 1) * kc, 128:256] += dv
            cols = slice(512 + kvi * 256, 512 + (kvi + 1) * 256)
            dos, lse_row, delta_row = rows_of(cols, kvi, 1)
            src = _key_chunks(kbc, vbc, ctx)
            if latent:
                start, span = _band(qi, tq, seq)
                src.append((kbl[pl.ds(start, span), :], vbl[pl.ds(start, span), :], _band_mask(qi, tq, start, span, 1)))
            dq4, grads = _softmax_bwd(_stack_heads(qb[...], kvi), dos, lse_row, delta_row, src)
            dq_ref[:, cols] = _unstack_heads(dq4, kvi)
            dc_ref[:, 256:384] += grads[0][0]
            dc_ref[:, 384:512] += grads[0][1]
            if latent:
                dl_ref[pl.ds(start, span), 256:384] += grads[1][0]
                dl_ref[pl.ds(start, span), 384:512] += grads[1][1]
            sink_row = _per_head((1, GROUP * tq), 1, tq, [sink_ref[kvi * GROUP + g] for g in range(GROUP)])
            dsink = -jnp.exp(sink_row - lse_row) * delta_row
            head = lax.broadcasted_iota(jnp.int32, (1, GROUP * tq), 1) // tq
            upd = jnp.zeros((8, 128), F32)
            for g in range(GROUP):
                upd = jnp.where(head_row == kvi * GROUP + g, jnp.sum(jnp.where(head == g, dsink, 0.0)), upd)
            dsink_ref[...] += upd

    specs = _qkv_specs(rt, tq, q_row, lambda b: ctx_blk0 + b, latent)
    q_rows_spec = pl.BlockSpec((tq, 1024), lambda b, i: (q_row(b, i), 0))
    in_specs = ([pl.BlockSpec(memory_space=pltpu.SMEM)] + specs
                + [q_rows_spec, q_rows_spec, pl.BlockSpec((1, 4, 8, GROUP * tq), lambda b, i: (b * nq + i, 0, 0, 0))])
    args = [sink] + [qkvp] * len(specs) + [do, o, lse]
    dq_shape = jax.ShapeDtypeStruct((rt.rows, 1024), F32)
    dkv_shape = jax.ShapeDtypeStruct((rt.rows, 512), F32)
    dsink_spec, dsink_shape = pl.BlockSpec((8, 128), lambda b, i: (0, 0)), jax.ShapeDtypeStruct((8, 128), F32)
    dq_spec = pl.BlockSpec((tq, 1024), lambda b, i: (q_row(b, i), 0))
    if latent:
        out_specs = [dq_spec, pl.BlockSpec((seq, 512), lambda b, i: (b, 0)), pl.BlockSpec((ctx, 512), lambda b, i: (b, 0)), dsink_spec]
        out_shape = [dq_shape, dkv_shape, jax.ShapeDtypeStruct((rt.n_ctx, 512), F32), dsink_shape]
        aliases = {}
    else:
        dq_prev, dkv_prev, c1 = prev
        in_specs += [pl.BlockSpec((ctx, 512), lambda b, i: (b, 0)), pl.BlockSpec(memory_space=pl.ANY), pl.BlockSpec(memory_space=pl.ANY)]
        args += [c1, dq_prev, dkv_prev]
        out_specs = [dq_spec, pl.BlockSpec((ctx, 512), lambda b, i: (ctx_blk0 + b, 0)), dsink_spec]
        out_shape = [dq_shape, dkv_shape, dsink_shape]
        aliases = {len(args) - 2: 0, len(args) - 1: 1}
    return _comm_call(body, comm, name=name, grid=(nb, nq), in_specs=in_specs, out_specs=out_specs, out_shape=out_shape,
                      args=args, aliases=aliases, semantics=("arbitrary", "arbitrary"))


def _silu(x):
    return x / (1.0 + jnp.exp(-x))


def _whole(shape):
    return pl.BlockSpec(shape, lambda i, s: (0,) * len(shape))


def _ada_half_spec(cols):
    return pl.BlockSpec((DEPTH, D_MODEL, cols), lambda i, s: (0, 0, s[0]))


def _ada_fwd(cond, w_ada, b_half, c_idx, name):
    rows = cond.shape[0]
    cols = w_ada.shape[2] // 2

    def body(s_ref, c_ref, w_ref, b_ref, x_ref, o_ref):
        xs = _silu(c_ref[...]).astype(BF16)
        x_ref[...] = xs
        for l in range(DEPTH):
            o_ref[l] = jnp.dot(xs, w_ref[l].astype(BF16), preferred_element_type=F32) + b_ref[l]

    grid_spec = pltpu.PrefetchScalarGridSpec(
        num_scalar_prefetch=1, grid=(1,),
        in_specs=[_whole(cond.shape), _ada_half_spec(cols), _whole(b_half.shape)],
        out_specs=[_whole((rows, D_MODEL)), _whole((DEPTH, rows, cols))])
    return pl.pallas_call(
        body, name=name, grid_spec=grid_spec,
        out_shape=[jax.ShapeDtypeStruct((rows, D_MODEL), BF16), jax.ShapeDtypeStruct((DEPTH, rows, cols), F32)],
        compiler_params=_params(("arbitrary",)),
    )(c_idx, cond, w_ada, b_half)


def _ada_cond_bwd(dcx, w_ada, c_idx, name):
    _, rows, cols = dcx.shape

    def body(s_ref, d_ref, w_ref, o_ref):
        acc = jnp.zeros((rows, D_MODEL), F32)
        for l in range(DEPTH):
            acc = acc + lax.dot_general(d_ref[l], w_ref[l].astype(BF16), NT, preferred_element_type=F32)
        o_ref[...] = acc

    grid_spec = pltpu.PrefetchScalarGridSpec(
        num_scalar_prefetch=1, grid=(1,),
        in_specs=[_whole(dcx.shape), _ada_half_spec(cols)], out_specs=_whole((rows, D_MODEL)))
    return pl.pallas_call(body, name=name, grid_spec=grid_spec, out_shape=jax.ShapeDtypeStruct((rows, D_MODEL), F32),
                          compiler_params=_params(("arbitrary",)))(c_idx, dcx, w_ada)


def _dev_sum(x, name, comm=None):
    _, r, c = x.shape

    def body(x_ref, o_ref):
        v = x_ref[0]
        for d in range(1, N_DEV):
            v = v + x_ref[d]
        o_ref[...] = v

    return _comm_call(body, comm, name=name, grid=(1,), in_specs=[pl.BlockSpec(x.shape, lambda i: (0, 0, 0))],
                      out_specs=[pl.BlockSpec((r, c), lambda i: (0, 0))], out_shape=[jax.ShapeDtypeStruct((r, c), F32)],
                      args=[x], aliases={}, semantics=("arbitrary",))


def _adam_val(w, g, m, v):
    c1 = 1.0 / (1.0 - ADAM_B1 ** ADAM_STEP)
    c2 = 1.0 / (1.0 - ADAM_B2 ** ADAM_STEP)
    nm = ADAM_B1 * m + (1.0 - ADAM_B1) * g
    nv = ADAM_B2 * v + (1.0 - ADAM_B2) * (g * g)
    return -ADAM_LR * ((nm * c1) / (jnp.sqrt(nv * c2) + ADAM_EPS) + ADAM_WD * w), nm, nv


def _small_update(tot, dcc_parts, params, n_groups, name):
    n_p = len(params)
    mod_rows = n_groups * N_MOD
    head_row = DEPTH * mod_rows + 4 * DEPTH

    def body(tot_ref, dcc_ref, *refs):
        ins, outs = refs[:3 * n_p], refs[3 * n_p:]

        def update(p, rows, cols, g):
            w_ref, m_ref, v_ref = ins[3 * p:3 * p + 3]
            g_ref, d_ref, nm_ref, nv_ref = outs[4 * p:4 * p + 4]
            d, nm, nv = _adam_val(w_ref[rows, cols], g, m_ref[rows, cols], v_ref[rows, cols])
            g_ref[rows, cols] = g
            d_ref[rows, cols] = d
            nm_ref[rows, cols] = nm
            nv_ref[rows, cols] = nv

        acc = dcc_ref[0, 0:1, :]
        for d in range(1, N_DEV):
            acc = acc + dcc_ref[d, 0:1, :]
        c = ins[0][...]
        sg = 1.0 / (1.0 + jnp.exp(-c))
        update(0, slice(0, 1), slice(None), acc * (sg * (1.0 + c * (1.0 - sg))))
        for l in range(DEPTH):
            for i in range(N_MOD):
                g = tot_ref[l * mod_rows + i:l * mod_rows + i + 1, :]
                for grp in range(1, n_groups):
                    g = g + tot_ref[l * mod_rows + grp * N_MOD + i:l * mod_rows + grp * N_MOD + i + 1, :]
                update(1, slice(l, l + 1), slice(i * D_MODEL, (i + 1) * D_MODEL), g)
            for j in range(4):
                row = DEPTH * mod_rows + 4 * l + j
                update(2 + j, slice(l, l + 1), slice(None), tot_ref[row:row + 1, :])
            head = tot_ref[head_row + l:head_row + l + 1, :]
            update(6, slice(l, l + 1), slice(None), head[:, 0:HEAD_DIM] + head[:, HEAD_DIM:2 * HEAD_DIM])
            update(7, slice(l, l + 1), slice(None), head[:, 2 * HEAD_DIM:3 * HEAD_DIM] + head[:, 3 * HEAD_DIM:4 * HEAD_DIM])
            update(8, slice(l, l + 1), slice(None), head[:, 4 * HEAD_DIM:4 * HEAD_DIM + ins[3 * 8].shape[1]])

    shapes = [jax.ShapeDtypeStruct(w.shape, F32) for w, _, _ in params for _ in range(4)]
    outs = pl.pallas_call(body, name=name, out_shape=shapes)(tot, dcc_parts, *[a for p in params for a in p])
    return [tuple(outs[4 * p:4 * p + 4]) for p in range(n_p)]


def _adamw(w, g, m, v, name):
    r, c = w.shape
    tr = _pick(r, (256, 128, 64, 32, 24, 16, 8))

    def body(w_ref, g_ref, m_ref, v_ref, d_ref, nm_ref, nv_ref):
        d_ref[...], nm_ref[...], nv_ref[...] = _adam_val(w_ref[...], g_ref[...], m_ref[...], v_ref[...])

    spec = pl.BlockSpec((tr, c), lambda i: (i, 0))
    return pl.pallas_call(body, name=name, grid=(r // tr,), in_specs=[spec] * 4, out_specs=[spec] * 3,
                          out_shape=[jax.ShapeDtypeStruct((r, c), F32)] * 3, compiler_params=_params(("parallel",)))(w, g, m, v)


def _adamw_shard(kind, l, w, m, v, halves, off, prev, name):
    h = PACK_HEIGHT[kind]
    assert off % h == 0, (kind, off)
    _, r, c = w.shape
    rows = r // 2

    def body(w_ref, m_ref, v_ref, p_ref, *rest):
        g_ref, d_ref, nm_ref, nv_ref = rest[-4:]
        if kind == "in":
            for t in range(2):
                g = p_ref[:, t * IN_PIECE_COLS:(t + 1) * IN_PIECE_COLS]
                rs = slice(t * h, (t + 1) * h)
                g_ref[rs, :] = g
                d_ref[rs, :], nm_ref[rs, :], nv_ref[rs, :] = _adam_val(w_ref[rs, :], g, m_ref[rs, :], v_ref[rs, :])
        else:
            g = p_ref[...]
            g_ref[...] = g
            d_ref[...], nm_ref[...], nv_ref[...] = _adam_val(w_ref[...], g, m_ref[...], v_ref[...])

    blk = pl.BlockSpec((None, rows, c), lambda half: (l, half, 0))
    in_specs = [blk, blk, blk, pl.BlockSpec((None, h, 1024), lambda half: (half, off // h, 0))]
    args = [w, m, v, halves]
    aliases = {}
    if prev is not None:
        in_specs += [pl.BlockSpec(memory_space=pl.ANY)] * 4
        args += list(prev)
        aliases = {4 + j: j for j in range(4)}
    return pl.pallas_call(
        body, name=name, grid=(2,), in_specs=in_specs, out_specs=[blk] * 4,
        out_shape=[jax.ShapeDtypeStruct(w.shape, F32)] * 4, input_output_aliases=aliases,
        compiler_params=_params(("parallel",)))(*args)


SMALL_ROWS = 48


def _small_rows(small, sq):
    def lane_pad(v):
        return jnp.pad(v, (0, D_MODEL - v.shape[0]))[None]

    head_rows = [lane_pad(jnp.concatenate([s["q_norm"][0], s["k_norm"][0], s["sink"]])) for s in small]
    loss_row = lane_pad((0.5 / D_MODEL) * jnp.sum(sq, keepdims=True)[0])
    rows = jnp.concatenate([s["mod"].reshape(-1, D_MODEL) for s in small] + [s["gammas"] for s in small] + head_rows + [loss_row], axis=0)
    return jnp.pad(rows, ((0, SMALL_ROWS - rows.shape[0]), (0, 0)))


def _local_step(x, ctx, target, mods, gam, qn, kn, sink, w_first, w_layers, packed, kc_idx):
    nb, seq, _ = x.shape
    rt = _Rows(nb, seq, ctx.shape[1])
    rt_lat = rt.latent_only()
    tables = _rope_tables(rt)
    fuse = packed is not None
    h = (x.reshape(rt.n_lat, D_MODEL), ctx.reshape(rt.n_ctx, D_MODEL))
    wg = [{}, {}] if fuse else [dict(w) for w in w_layers]
    wg[0]["in"] = (w_first, 0)
    if fuse:
        wg[0]["in_own"] = (packed, W_FIRST[0])
    saved = []
    for l in range(DEPTH):
        g_pre_mix, g_post_mix, g_pre_mlp, g_post_mlp = gam[l]
        if l == 0:
            u, qkv, qkvp, h = _in_fwd(rt, h, g_pre_mix, mods[l], wg[l], tables, qn[l], kn[l], f"in_fwd{l}")
        else:
            u, qkv, qkvp = _in_fwd(rt, h, g_pre_mix, mods[l], wg[l], tables, qn[l], kn[l], f"in_fwd{l}")
        if fuse and l == 0:
            o, lse_lat, w_mlp0, w_out0, w_in1 = _attn_fwd(rt, qkvp, sink[l], None, f"attn_lat_fwd{l}",
                                                         comm=_gather_comm(packed, [W_MLP0, W_OUT0, W_IN1], lead=2))
            wg[0].update({kind: (w_mlp0, PACK_OFF[(kind, 0)] - W_MLP0[0]) for kind in ("up", "down")})
            wg[0]["out"] = (w_out0, 0)
            wg[1] = {"in": (w_in1, 0)}
        elif fuse:
            o, lse_lat, w_mlp1, w_out1 = _attn_fwd(rt, qkvp, sink[l], None, f"attn_lat_fwd{l}",
                                                   comm=_gather_comm(packed, [W_MLP1, W_OUT1], lead=2))
            wg[1].update({kind: (w_mlp1, PACK_OFF[(kind, 1)] - W_MLP1[0]) for kind in ("up", "down")})
            wg[1]["out"] = (w_out1, 0)
        else:
            o, lse_lat = _attn_fwd(rt, qkvp, sink[l], None, f"attn_lat_fwd{l}")
        if l < DEPTH - 1:
            o, lse_ctx = _attn_fwd(rt, qkvp, sink[l], o, f"attn_ctx_fwd{l}")
            mix, h1, u2 = _out_fwd(rt, o, wg[l], h, mods[l], g_post_mix, g_pre_mlp, f"out_fwd{l}")
            r, y, h2 = _mlp_fwd(rt, u2, h1, wg[l], mods[l], g_post_mlp, f"mlp_fwd{l}")
        else:
            lse_ctx = None
            mix, h1, u2 = _out_fwd(rt_lat, o, wg[l], h, mods[l], g_post_mix, g_pre_mlp, f"out_fwd{l}")
            r, y, dh, sq = _mlp_fwd(rt_lat, u2, h1, wg[l], mods[l], g_post_mlp, f"mlp_fwd{l}", target=target.reshape(rt.n_lat, D_MODEL))
        saved.append((h, u, qkv, qkvp, o, lse_lat, lse_ctx, mix, h1, u2, r, y))
        h = h2

    small = [None] * DEPTH
    groups = {}
    for l in reversed(range(DEPTH)):
        g_pre_mix, g_post_mix, g_pre_mlp, g_post_mlp = gam[l]
        h0, u, qkv, qkvp, o, lse_lat, lse_ctx, mix, h1, u2, r, y = saved[l]
        mlp_group, mix_group = (G_LAYER1, G_LAYER1) if l == 1 else (G_MLP0, G_MIX0)
        hide = fuse and l == 0

        dead_ctx = l == DEPTH - 1
        rt_b = rt_lat if dead_ctx else rt
        dy, da, d_gate_m, d_g_post_mlp = _mlp_down_bwd(rt_b, dh, y, r, wg[l], mods[l], g_post_mlp, f"mlp_down_bwd{l}")
        p_mlp = _wgrad_packed(rt_b, r, dy, "down", PACK_OFF[("down", l)] - mlp_group[0], mlp_group[1], None, f"mlp_down_wgrad{l}",
                              comm=_pair_comm(groups[G_LAYER1]) if hide else None)
        if hide:
            p_mlp, r1 = p_mlp
            sum1 = _pair_sum(groups[G_LAYER1], r1, kc_idx, "grad_pair_sum_layer1")
        p_mlp = _wgrad_packed(rt_b, u2, da, "up", PACK_OFF[("up", l)] - mlp_group[0], mlp_group[1], p_mlp, f"mlp_up_wgrad{l}")
        outs = _mlp_up_bwd(rt_b, da, wg[l], h1, dh, mods[l], g_pre_mlp, f"mlp_up_bwd{l}", comm=_pair_comm(p_mlp) if hide else None)
        dh1, d_sh_m, d_sc_m, d_g_pre_mlp = outs[:4]
        if hide:
            sum0 = _pair_sum(p_mlp, outs[4], kc_idx, "grad_pair_sum_mlp0")
        dmix, do, d_gate_a, d_g_post_mix = _out_bwd(rt_b, dh1, mix, wg[l], mods[l], g_post_mix, f"out_bwd{l}")
        p_mix = _wgrad_packed(rt_b, o, dmix, "out", PACK_OFF[("out", l)] - mix_group[0], mix_group[1],
                              p_mlp if l == 1 else None, f"out_wgrad{l}")
        outs = _attn_bwd(rt, qkvp, o, lse_lat, do, sink[l], None, f"attn_lat_bwd{l}",
                         comm=_chip_comm([sum1[1], sum0[1]]) if hide else None)
        dq, dkv, dkv_c, dsink1 = outs[:4]
        if hide:
            groups[G_LAYER1] = _owner_sum(sum1[0], outs[4], kc_idx, "grad_owner_sum_layer1")
            groups[G_MLP0] = _owner_sum(sum0[0], outs[5], kc_idx, "grad_owner_sum_mlp0")
        if dead_ctx:
            dsink2 = jnp.zeros_like(dsink1)
            d_gate_m, d_sh_m, d_sc_m, d_gate_a = [a.at[nb].set(0.0) for a in (d_gate_m, d_sh_m, d_sc_m, d_gate_a)]
        else:
            dq, dkv, dsink2 = _attn_bwd(rt, qkvp, o, lse_ctx, do, sink[l], (dq, dkv, dkv_c), f"attn_ctx_bwd{l}")
        dqkv, dh, dqn, dkn, d_sh_a, d_sc_a, d_g_pre_mix = _in_bwd(rt, dq, dkv, qkv, tables, qn[l], kn[l], wg[l], h0, dh1, mods[l],
                                                                  g_pre_mix, l == 0, f"in_bwd{l}",
                                                                  dead_ctx_dkv=dkv_c if dead_ctx else None)
        dmod = jnp.concatenate([d_sh_a, d_sc_a, d_gate_a, d_sh_m, d_sc_m, d_gate_m], axis=1)
        small[l] = dict(mod=dmod, gammas=jnp.concatenate([d_g_pre_mix, d_g_post_mix, d_g_pre_mlp, d_g_post_mlp], axis=0),
                        q_norm=dqn, k_norm=dkn, sink=(dsink1 + dsink2)[:, 0])
        tail = _merge([_gather_comm(_small_rows(small, sq), [(0, SMALL_ROWS)], lead=2),
                       _halves_comm([groups[G_LAYER1], groups[G_MLP0]])]) if hide else None
        outs = _wgrad_packed(rt, u, dqkv, "in", PACK_OFF[("in", l)] - mix_group[0], mix_group[1], p_mix, f"in_wgrad{l}", comm=tail)
        if hide:
            groups[mix_group], small_g, groups[G_LAYER1], groups[G_MLP0] = outs
        else:
            groups[mix_group], small_g = outs, None
            if l == 0:
                groups[G_MLP0] = p_mlp
    return sq, dh.reshape(nb, seq, D_MODEL), [groups[G_LAYER1], groups[G_MLP0], groups[G_MIX0]], small, small_g


def kernel(x, c, ctx, c_ctx, w_ada, b_ada, g_pre_mix, g_post_mix, g_pre_mlp, g_post_mlp, w_in, q_norm, k_norm, sink, w_out, w_up, w_down, loss_target, m_c_ctx, m_w_ada, m_b_ada, m_g_pre_mix, m_g_post_mix, m_g_pre_mlp, m_g_post_mlp, m_w_in, m_q_norm, m_k_norm, m_sink, m_w_out, m_w_up, m_w_down, v_c_ctx, v_w_ada, v_b_ada, v_g_pre_mix, v_g_post_mix, v_g_pre_mlp, v_g_post_mlp, v_w_in, v_q_norm, v_k_norm, v_sink, v_w_out, v_w_up, v_w_down):
    nb = x.shape[0]
    ix, iy, ic = lax.axis_index("x"), lax.axis_index("y"), lax.axis_index("c")
    chip = 2 * ix + iy
    dev = 2 * chip + ic
    ada_cols = w_ada.shape[2] // 2

    c_rows = c.reshape(8, (nb * D_MODEL) // 8)
    packed, c_all = _pack_local_half(w_in, w_out, w_up, w_down, _gather_comm(c_rows, [(0, c_rows.shape[0])]), "pack_gather_c")
    c_all = c_all.reshape(N_DEV * nb, D_MODEL)
    n_cond = N_DEV * nb + 1
    cond_rows = 16 * ((n_cond + 15) // 16)
    cond = jnp.concatenate([c_all, c_ctx[None, :], jnp.zeros((cond_rows - n_cond, D_MODEL), F32)], axis=0)
    c_idx = ic.reshape(1).astype(jnp.int32)
    kc_idx = jnp.stack([chip, ic]).astype(jnp.int32)
    b_ada_half = lax.dynamic_slice_in_dim(b_ada, dev * ada_cols, ada_cols, 1)[:, None, :]
    x_ada, mod_part = _ada_fwd(cond, w_ada, b_ada_half, c_idx, "ada_fwd")
    mod_rows2d = mod_part.reshape(DEPTH * cond_rows, ada_cols)
    mod_g, w_first = _comm_alone(_merge([_gather_comm(mod_rows2d, [(0, mod_rows2d.shape[0])]),
                                         _gather_comm(packed, [W_FIRST], copy_own=False, cols=2 * IN_PIECE_COLS)]),
                               "gather_mod_w_first")
    mod_all = mod_g.reshape(N_DEV, DEPTH, cond_rows, ada_cols).transpose(1, 2, 0, 3).reshape(DEPTH, cond_rows, N_MOD * D_MODEL)
    mods = []
    for l in range(DEPTH):
        mine = lax.dynamic_slice_in_dim(mod_all[l], dev * nb, nb, 0)
        mods.append(jnp.concatenate([mine, mod_all[l, n_cond - 1:n_cond]], axis=0).reshape(nb + 1, N_MOD, D_MODEL))

    gam = [(g_pre_mix[l][None], g_post_mix[l][None], g_pre_mlp[l][None], g_post_mlp[l][None]) for l in range(DEPTH)]
    qn = [jnp.tile(q_norm[l], 2)[None] for l in range(DEPTH)]
    kn = [jnp.tile(k_norm[l], 2)[None] for l in range(DEPTH)]
    _, grad_x, (h_layer1, h_mlp0, p_mix0), _, small_g = _local_step(x, ctx, loss_target, mods, gam, qn, kn, [sink[l] for l in range(DEPTH)],
                                                                 w_first, None, packed, kc_idx)

    def step(w, g, m, v, name):
        shape = w.shape
        cols = shape[-1]
        outs = _adamw(w.reshape(-1, cols), g.reshape(-1, cols), m.reshape(-1, cols), v.reshape(-1, cols), name)
        return tuple(a.reshape(shape) for a in outs)

    def shard_update(kind, w, m, v, layer0, layer1):
        outs = None
        for l, (halves, group) in enumerate((layer0, layer1)):
            outs = _adamw_shard(kind, l, w, m, v, halves, PACK_OFF[(kind, l)] - group[0], outs, f"adamw_w_{kind}{l}")
        return tuple(outs)

    tot, r1 = _dev_sum(small_g, "small_sum", comm=_pair_comm(p_mix0))
    mod_rows = (nb + 1) * N_MOD
    loss = tot[DEPTH * mod_rows + 4 * DEPTH + DEPTH, 0]

    ex = small_g[:, :DEPTH * mod_rows].reshape(N_DEV, DEPTH, nb + 1, N_MOD * D_MODEL)[:, :, :nb]
    ex = ex.transpose(1, 0, 2, 3).reshape(DEPTH, N_DEV * nb, N_MOD * D_MODEL)
    cx = tot[:DEPTH * mod_rows].reshape(DEPTH, nb + 1, N_MOD * D_MODEL)[:, nb:]
    dm = jnp.concatenate([ex, cx, jnp.zeros((DEPTH, cond_rows - n_cond, N_MOD * D_MODEL), F32)], axis=1)
    shard_cols = w_ada.shape[2]
    grad_w_ada = _ada_wgrad(x_ada, lax.dynamic_slice_in_dim(dm, chip * shard_cols, shard_cols, 2).astype(BF16), "ada_wgrad")
    dcx = jnp.pad(lax.dynamic_slice_in_dim(cx, dev * ada_cols, ada_cols, 2), ((0, 0), (0, 15), (0, 0))).astype(BF16)
    dcc = _ada_cond_bwd(dcx, w_ada, c_idx, "ada_cond_bwd")[0:8]

    a32, a16 = _pair_sum(p_mix0, r1, kc_idx, "grad_pair_sum_mix0")
    r2, dcc_g = _comm_alone(_merge([_chip_comm([a16]), _gather_comm(dcc, [(0, dcc.shape[0])])]), "grad_chip_exchange_mix0")
    h_mix0 = _owner_sum(a32, r2, kc_idx, "grad_owner_sum_mix0")
    h_mix0, = _comm_alone(_halves_comm([h_mix0]), "grad_halves_exchange_mix0")

    small_names = ["c_ctx", "b_ada", "g_pre_mix", "g_post_mix", "g_pre_mlp", "g_post_mlp", "q_norm", "k_norm", "sink"]
    assert q_norm.shape[1] == HEAD_DIM and k_norm.shape[1] == HEAD_DIM
    small_res = _small_update(tot, dcc_g, [(c_ctx[None], m_c_ctx[None], v_c_ctx[None]), (b_ada, m_b_ada, v_b_ada),
                                           (g_pre_mix, m_g_pre_mix, v_g_pre_mix), (g_post_mix, m_g_post_mix, v_g_post_mix),
                                           (g_pre_mlp, m_g_pre_mlp, v_g_pre_mlp), (g_post_mlp, m_g_post_mlp, v_g_post_mlp),
                                           (q_norm, m_q_norm, v_q_norm), (k_norm, m_k_norm, v_k_norm), (sink, m_sink, v_sink)],
                              nb + 1, "small_update")
    res = {n: r for n, r in zip(small_names, small_res)}
    res["c_ctx"] = tuple(a[0] for a in res["c_ctx"])
    res["w_ada"] = (grad_w_ada, *step(w_ada, grad_w_ada, m_w_ada, v_w_ada, "adamw_w_ada"))
    res["w_up"] = shard_update("up", w_up, m_w_up, v_w_up, (h_mlp0, G_MLP0), (h_layer1, G_LAYER1))
    res["w_down"] = shard_update("down", w_down, m_w_down, v_w_down, (h_mlp0, G_MLP0), (h_layer1, G_LAYER1))
    res["w_in"] = shard_update("in", w_in, m_w_in, v_w_in, (h_mix0, G_MIX0), (h_layer1, G_LAYER1))
    res["w_out"] = shard_update("out", w_out, m_w_out, v_w_out, (h_mix0, G_MIX0), (h_layer1, G_LAYER1))

    order = ["c_ctx", "w_ada", "b_ada", "g_pre_mix", "g_post_mix", "g_pre_mlp", "g_post_mlp", "w_in", "q_norm", "k_norm", "sink", "w_out", "w_up", "w_down"]
    return (loss, grad_x, *[res[n][0] for n in order], *[res[n][1] for n in order],
            *[res[n][2] for n in order], *[res[n][3] for n in order])
```

```python
import functools

import jax
import jax.numpy as jnp
import numpy as np
from jax import lax
from jax.experimental import pallas as pl
from jax.experimental.pallas import tpu as pltpu

F32 = jnp.float32
BF16 = jnp.bfloat16

D_MODEL = 1024
HEAD_DIM = 64
GROUP = 4
WINDOW = 128
N_MOD = 6
D_FF = 4 * D_MODEL
IN_COLS = 1536
GRID_W = 64
ROPE_THETA = 10000.0
EPS = 1e-6
NEG_BIG = -1e30
Q_SCALE = HEAD_DIM ** -0.5
DEPTH = 2
N_DEV = 8

ADAM_LR = 0.001
ADAM_B1 = 0.9
ADAM_B2 = 0.999
ADAM_EPS = 1e-08
ADAM_WD = 0.01
ADAM_STEP = 10

V7X_VMEM_BYTES = 64 * 1024 * 1024
VMEM_LIMIT = V7X_VMEM_BYTES - 8 * 1024 * 1024

MESH = pl.DeviceIdType.MESH
NT = (((1,), (1,)), ((), ()))
TN = (((0,), (0,)), ((), ()))

COL_KA, COL_VA, COL_KB, COL_VB = 4, 5, 10, 11
NORMED_COLS = 640

PACK_HEIGHT = {"up": 512, "down": 512, "in": 256, "out": 128}
IN_PIECE_COLS = 384
PACK_OFF = {("up", 0): 0, ("down", 0): 512, ("in", 0): 1024, ("out", 0): 1280,
            ("up", 1): 1408, ("down", 1): 1920, ("in", 1): 2432, ("out", 1): 2688}
PACK_ROWS = 2816
W_FIRST, W_MLP0, W_OUT0, W_IN1, W_MLP1, W_OUT1 = (1024, 256), (0, 1024), (1280, 128), (2432, 256), (1408, 1024), (2688, 128)
G_LAYER1, G_MLP0, G_OUT0, G_IN0 = (1408, 1408), (0, 1024), (1280, 128), (1024, 256)


def _pick(n, cands):
    for t in cands:
        if n % t == 0:
            return t
    raise ValueError(f"no tile for {n}")


def _params(sem):
    return pltpu.CompilerParams(dimension_semantics=sem, vmem_limit_bytes=VMEM_LIMIT)


class _Comm:
    def __init__(self, inputs, out_shapes, aliases, n_send, n_recv, start, finish, relay=None, lead=0):
        self.inputs, self.out_shapes, self.aliases = list(inputs), list(out_shapes), dict(aliases)
        self.n_send, self.n_recv, self.start, self.finish, self.relay, self.lead = n_send, n_recv, start, finish, relay, lead


def _comm_call(compute, comm, *, name, grid, in_specs, out_specs, out_shape, args, aliases, semantics, scratch=()):
    in_specs, out_specs, out_shape, args, aliases = list(in_specs), list(out_specs), list(out_shape), list(args), dict(aliases)
    scratch = list(scratch)
    if comm is None:
        return pl.pallas_call(compute, name=name, grid=grid, in_specs=in_specs, out_specs=out_specs, out_shape=out_shape,
                              input_output_aliases=aliases, scratch_shapes=scratch, compiler_params=_params(semantics))(*args)
    n_in, n_out, n_ci, n_co = len(args), len(out_shape), len(comm.inputs), len(comm.out_shapes)
    hbm = pl.BlockSpec(memory_space=pl.ANY)
    aliases.update({n_in + i: n_out + o for i, o in comm.aliases.items()})

    def body(*refs):
        ins, c_ins = refs[:n_in], refs[n_in:n_in + n_ci]
        outs, c_outs = refs[n_in + n_ci:n_in + n_ci + n_out], refs[n_in + n_ci + n_out:n_in + n_ci + n_out + n_co]
        scr = refs[n_in + n_ci + n_out + n_co:-2]
        send_sems, recv_sems = refs[-2:]
        ids = [pl.program_id(a) for a in range(len(grid))]
        first = functools.reduce(jnp.logical_and, [i == 0 for i in ids])
        last = functools.reduce(jnp.logical_and, [i == g - 1 for i, g in zip(ids, grid)])

        @pl.when(first)
        def _():
            comm.start(c_ins, c_outs, send_sems, recv_sems)

        compute(*ins, *outs, *scr)

        if comm.relay is not None:
            step = functools.reduce(lambda acc, ig: acc * ig[1] + ig[0], zip(ids, grid), 0)

            @pl.when(step == int(np.prod(grid)) - 1 - comm.lead)
            def _():
                comm.relay(c_ins, c_outs, send_sems, recv_sems)

        @pl.when(last)
        def _():
            comm.finish(c_ins, c_outs, send_sems, recv_sems)

    return pl.pallas_call(
        body, name=name, grid=grid,
        in_specs=in_specs + [hbm] * n_ci, out_specs=out_specs + [hbm] * n_co, out_shape=out_shape + comm.out_shapes,
        input_output_aliases=aliases,
        scratch_shapes=scratch + [pltpu.SemaphoreType.DMA((comm.n_send,)), pltpu.SemaphoreType.DMA((comm.n_recv,))],
        compiler_params=_params(("arbitrary",) * len(grid)),
    )(*args, *comm.inputs)


def _place():
    x_, y_, c_ = lax.axis_index("x"), lax.axis_index("y"), lax.axis_index("c")
    return x_, y_, c_, [(1 - x_, y_), (x_, 1 - y_), (1 - x_, 1 - y_)]


GATHER_SENDS, GATHER_RECVS = 8, 7


def _gather_copies(packed_ref, wg_ref, send_sems, recv_sems, rows, nth=0):
    r0, n = rows
    x_, y_, c_, chips = _place()
    me, sibling = (x_, y_, c_), (x_, y_, 1 - c_)
    src = packed_ref.at[pl.ds(r0, n), pl.ds(0, wg_ref.shape[2])]

    def slot(px, py, pc):
        return wg_ref.at[4 * px + 2 * py + pc]

    def copy(k, block, to, from_packed=False):
        return pltpu.make_async_remote_copy(src_ref=src if from_packed else slot(*block), dst_ref=slot(*block),
                                            send_sem=send_sems.at[GATHER_SENDS * nth + k], recv_sem=recv_sems.at[GATHER_RECVS * nth + k],
                                            device_id=to, device_id_type=MESH)

    own = [copy(0, me, sibling, True)] + [copy(1 + j, me, (*chip, c_), True) for j, chip in enumerate(chips)]
    passed = [copy(4 + j, (*chip, c_), sibling) for j, chip in enumerate(chips)]
    over_ici = [copy(1 + j, (*chip, c_), me) for j, chip in enumerate(chips)]
    from_sibling = [copy(0, sibling, me)] + [copy(4 + j, (*chip, 1 - c_), me) for j, chip in enumerate(chips)]
    mine = pltpu.make_async_copy(src, slot(*me), send_sems.at[GATHER_SENDS * nth + 7])
    return mine, own, passed, over_ici, from_sibling


def _gather_start(packed_ref, wg_ref, send_sems, recv_sems, rows, nth=0, copy_own=True):
    mine, own, _, _, _ = _gather_copies(packed_ref, wg_ref, send_sems, recv_sems, rows, nth)
    if copy_own:
        mine.start()
    for cp in own:
        cp.start()


def _gather_relay(packed_ref, wg_ref, send_sems, recv_sems, rows, nth=0):
    _, _, passed, over_ici, _ = _gather_copies(packed_ref, wg_ref, send_sems, recv_sems, rows, nth)
    for arrived, onward in zip(over_ici, passed):
        arrived.wait_recv()
        onward.start()


def _gather_finish(packed_ref, wg_ref, send_sems, recv_sems, rows, nth=0, copy_own=True):
    mine, own, passed, _, from_sibling = _gather_copies(packed_ref, wg_ref, send_sems, recv_sems, rows, nth)
    for arrived in from_sibling:
        arrived.wait_recv()
    for cp in own + passed:
        cp.wait_send()
    if copy_own:
        mine.wait()


def _gather_comm(packed, ranges, copy_own=True, lead=0, cols=None):
    shapes = [jax.ShapeDtypeStruct((N_DEV, n, cols or packed.shape[1]), packed.dtype) for _, n in ranges]

    def start(ins, outs, ss, rs):
        for nth, rows in enumerate(ranges):
            _gather_start(ins[0], outs[nth], ss, rs, rows, nth, copy_own)

    def relay(ins, outs, ss, rs):
        for nth, rows in enumerate(ranges):
            _gather_relay(ins[0], outs[nth], ss, rs, rows, nth)

    def finish(ins, outs, ss, rs):
        for nth, rows in enumerate(ranges):
            _gather_finish(ins[0], outs[nth], ss, rs, rows, nth, copy_own)

    return _Comm([packed], shapes, {}, GATHER_SENDS * len(ranges), GATHER_RECVS * len(ranges), start, finish, relay, lead)


def _pair_copy(p_ref, out_ref, send_sems, recv_sems):
    x_, y_, c_, _ = _place()
    return pltpu.make_async_remote_copy(src_ref=p_ref.at[1 - c_], dst_ref=out_ref,
                                        send_sem=send_sems.at[0], recv_sem=recv_sems.at[0],
                                        device_id=(x_, y_, 1 - c_), device_id_type=MESH)


def _pair_comm(p):
    return _Comm([p], [jax.ShapeDtypeStruct(p.shape[1:], p.dtype)], {}, 1, 1,
                 lambda ins, outs, ss, rs: _pair_copy(ins[0], outs[0], ss, rs).start(),
                 lambda ins, outs, ss, rs: _pair_copy(ins[0], outs[0], ss, rs).wait())


def _chip_copies(a_refs, out_refs, send_sems, recv_sems):
    _, _, c_, chips = _place()
    return [pltpu.make_async_remote_copy(src_ref=a_ref.at[2 * tx + ty], dst_ref=o_ref.at[j],
                                         send_sem=send_sems.at[3 * g + j], recv_sem=recv_sems.at[3 * g + j],
                                         device_id=(tx, ty, c_), device_id_type=MESH)
            for g, (a_ref, o_ref) in enumerate(zip(a_refs, out_refs)) for j, (tx, ty) in enumerate(chips)]


def _chip_start(a_refs, out_refs, send_sems, recv_sems):
    for cp in _chip_copies(a_refs, out_refs, send_sems, recv_sems):
        cp.start()


def _chip_finish(a_refs, out_refs, send_sems, recv_sems):
    for cp in _chip_copies(a_refs, out_refs, send_sems, recv_sems):
        cp.wait()


def _chip_comm(arrays):
    shapes = [jax.ShapeDtypeStruct((3,) + a.shape[1:], a.dtype) for a in arrays]
    return _Comm(arrays, shapes, {}, 3 * len(arrays), 3 * len(arrays), _chip_start, _chip_finish)


def _halves_copies(in_refs, out_refs, send_sems, recv_sems):
    x_, y_, c_, _ = _place()
    return [pltpu.make_async_remote_copy(src_ref=o_ref.at[c_], dst_ref=o_ref.at[c_], send_sem=send_sems.at[i], recv_sem=recv_sems.at[i],
                                         device_id=(x_, y_, 1 - c_), device_id_type=MESH)
            for i, o_ref in enumerate(out_refs)]


def _halves_start(in_refs, out_refs, send_sems, recv_sems):
    for cp in _halves_copies(in_refs, out_refs, send_sems, recv_sems):
        cp.start()


def _halves_finish(in_refs, out_refs, send_sems, recv_sems):
    for cp in _halves_copies(in_refs, out_refs, send_sems, recv_sems):
        cp.wait()


def _halves_comm(arrays):
    shapes = [jax.ShapeDtypeStruct(a.shape, a.dtype) for a in arrays]
    return _Comm(arrays, shapes, {i: i for i in range(len(arrays))}, len(arrays), len(arrays), _halves_start, _halves_finish)


class _SemSlice:
    class _At:
        def __init__(self, sems, first):
            self.sems, self.first = sems, first

        def __getitem__(self, k):
            return self.sems.at[self.first + k]

    def __init__(self, sems, first):
        self.at = _SemSlice._At(sems, first)


def _merge(comms):
    inputs = [a for c in comms for a in c.inputs]
    shapes = [s for c in comms for s in c.out_shapes]
    aliases, spans = {}, []
    i0 = o0 = s0 = r0 = 0
    for c in comms:
        aliases.update({i0 + i: o0 + o for i, o in c.aliases.items()})
        spans.append((slice(i0, i0 + len(c.inputs)), slice(o0, o0 + len(c.out_shapes)), s0, r0))
        i0, o0, s0, r0 = i0 + len(c.inputs), o0 + len(c.out_shapes), s0 + c.n_send, r0 + c.n_recv

    def start(ins, outs, ss, rs):
        for c, (i, o, s, r) in zip(comms, spans):
            c.start(ins[i], outs[o], _SemSlice(ss, s), _SemSlice(rs, r))

    def relay(ins, outs, ss, rs):
        for c, (i, o, s, r) in zip(comms, spans):
            if c.relay is not None:
                c.relay(ins[i], outs[o], _SemSlice(ss, s), _SemSlice(rs, r))

    def finish(ins, outs, ss, rs):
        for c, (i, o, s, r) in zip(comms, spans):
            c.finish(ins[i], outs[o], _SemSlice(ss, s), _SemSlice(rs, r))

    leads = [c.lead for c in comms if c.relay is not None]
    return _Comm(inputs, shapes, aliases, s0, r0, start, finish, relay if leads else None, max(leads, default=0))


def _comm_alone(comm, name):
    n_ci = len(comm.inputs)
    hbm = pl.BlockSpec(memory_space=pl.ANY)

    def body(*refs):
        c_ins, c_outs, send_sems, recv_sems = refs[:n_ci], refs[n_ci:-2], refs[-2], refs[-1]
        comm.start(c_ins, c_outs, send_sems, recv_sems)
        if comm.relay is not None:
            comm.relay(c_ins, c_outs, send_sems, recv_sems)
        comm.finish(c_ins, c_outs, send_sems, recv_sems)

    return pl.pallas_call(
        body, name=name, out_shape=comm.out_shapes, in_specs=[hbm] * n_ci, out_specs=[hbm] * len(comm.out_shapes),
        input_output_aliases=comm.aliases,
        scratch_shapes=[pltpu.SemaphoreType.DMA((comm.n_send,)), pltpu.SemaphoreType.DMA((comm.n_recv,))],
    )(*comm.inputs)


SUM_TILES = (704, 512, 384, 320, 256, 192, 128, 64)


def _pair_sum(p, r1, kc_idx, name):
    _, _, n, c = p.shape
    tr = _pick(n, SUM_TILES)

    def body(s_ref, p_ref, r_ref, o32_ref, o16_ref):
        v = p_ref[...] + r_ref[...]
        o16_ref[...] = v.astype(BF16)

        @pl.when(pl.program_id(1) == s_ref[0])
        def _():
            o32_ref[...] = v

    blk = pl.BlockSpec((None, tr, c), lambda i, j, s: (j, i, 0))
    grid_spec = pltpu.PrefetchScalarGridSpec(
        num_scalar_prefetch=1, grid=(n // tr, 4),
        in_specs=[pl.BlockSpec((None, None, tr, c), lambda i, j, s: (s[1], j, i, 0)), blk],
        out_specs=[pl.BlockSpec((tr, c), lambda i, j, s: (i, 0)), blk])
    return pl.pallas_call(
        body, name=name, grid_spec=grid_spec,
        out_shape=[jax.ShapeDtypeStruct((n, c), F32), jax.ShapeDtypeStruct((4, n, c), BF16)],
        compiler_params=_params(("arbitrary", "arbitrary")),
    )(kc_idx, p, r1)


def _owner_sum(a32, r2, kc_idx, name):
    r, c = a32.shape
    tr = _pick(r, SUM_TILES)

    def body(s_ref, a_ref, r_ref, o_ref):
        v = a_ref[...]
        for j in range(3):
            v = v + r_ref[j].astype(F32)
        o_ref[...] = v

    grid_spec = pltpu.PrefetchScalarGridSpec(
        num_scalar_prefetch=1, grid=(r // tr,),
        in_specs=[pl.BlockSpec((tr, c), lambda i, s: (i, 0)),
                  pl.BlockSpec((3, tr, c), lambda i, s: (0, i, 0))],
        out_specs=pl.BlockSpec((None, tr, c), lambda i, s: (s[1], i, 0)))
    return pl.pallas_call(
        body, name=name, grid_spec=grid_spec,
        out_shape=jax.ShapeDtypeStruct((2, r, c), F32),
        compiler_params=_params(("arbitrary",)),
    )(kc_idx, a32, r2)


def _pack_local_half(w_in_s, w_out_s, w_up_s, w_down_s, comm, name):
    shards = {"in": w_in_s, "out": w_out_s, "up": w_up_s, "down": w_down_s}
    kinds = list(shards)
    assert sorted(off + PACK_HEIGHT[kind] for (kind, _), off in PACK_OFF.items()) == sorted(PACK_OFF.values())[1:] + [PACK_ROWS]
    for kind in kinds:
        assert shards[kind].shape[1] == (4 if kind == "in" else 2) * PACK_HEIGHT[kind], (kind, shards[kind].shape)

    def body(*refs):
        w_refs, p_ref = dict(zip(kinds, refs[:4])), refs[4]
        scr, sems = dict(zip(kinds, refs[5:9])), refs[9]
        c = lax.axis_index("c")
        copies = {}
        for n, (kind, l) in enumerate(sorted(PACK_OFF)):
            rows = scr[kind].shape[1]
            copies[(kind, l)] = pltpu.make_async_copy(w_refs[kind].at[l, pl.ds(c * rows, rows)], scr[kind].at[l], sems.at[n])
            copies[(kind, l)].start()
        for (kind, l), off in sorted(PACK_OFF.items(), key=lambda kv: kv[1]):
            copies[(kind, l)].wait()
            h = PACK_HEIGHT[kind]
            if kind == "in":
                for t in range(2):
                    p_ref[off:off + h, t * IN_PIECE_COLS:(t + 1) * IN_PIECE_COLS] = scr[kind][l, t * h:(t + 1) * h, :].astype(BF16)
                p_ref[off:off + h, 2 * IN_PIECE_COLS:] = jnp.zeros((h, 1024 - 2 * IN_PIECE_COLS), BF16)
            else:
                p_ref[off:off + h, :] = scr[kind][l].astype(BF16)

    hbm = pl.BlockSpec(memory_space=pl.ANY)
    scratch = [pltpu.VMEM((DEPTH, shards[kind].shape[1] // 2, shards[kind].shape[2]), F32) for kind in kinds]
    outs = _comm_call(
        body, comm, name=name, grid=(1,), in_specs=[hbm] * 4,
        out_specs=[pl.BlockSpec((PACK_ROWS, 1024), lambda i: (0, 0))],
        out_shape=[jax.ShapeDtypeStruct((PACK_ROWS, 1024), BF16)],
        args=[shards[kind] for kind in kinds], aliases={}, semantics=("arbitrary",),
        scratch=scratch + [pltpu.SemaphoreType.DMA((len(PACK_OFF),))])
    return outs


def _unpack_in_pieces(w_ref, own_ref, w_scr):
    if own_ref is not None:
        me = 4 * lax.axis_index("x") + 2 * lax.axis_index("y") + lax.axis_index("c")
    for d in range(N_DEV):
        k, c = d // 2, d % 2
        for t in range(2):
            piece = w_ref[d, :, t * IN_PIECE_COLS:(t + 1) * IN_PIECE_COLS]
            if own_ref is not None:
                piece = jnp.where(me == d, own_ref[:, t * IN_PIECE_COLS:(t + 1) * IN_PIECE_COLS], piece)
            w_scr[c * 512 + t * 256:c * 512 + (t + 1) * 256, k * IN_PIECE_COLS:(k + 1) * IN_PIECE_COLS] = piece


def _in_weight_operands(wg):
    specs, args = [_gathered_spec(wg, "in")], [wg["in"][0]]
    if "in_own" in wg:
        own, off = wg["in_own"]
        h = PACK_HEIGHT["in"]
        assert off % h == 0
        specs.append(pl.BlockSpec((h, 1024), lambda *_: (off // h, 0), pipeline_mode=pl.Buffered(1)))
        args.append(own)
    return specs, args


class _Rows:
    def __init__(self, nb, seq, ctx):
        self.nb, self.seq, self.ctx = nb, seq, ctx
        self.n_lat, self.n_ctx = nb * seq, nb * ctx
        self.rows = self.n_lat + self.n_ctx
        self.tm = _pick(np.gcd(seq, self.n_ctx), (512, 256, 128))
        self.tiles_per_ex = seq // self.tm
        self.n_tiles = self.rows // self.tm
        self.n_lat_tiles = self.n_lat // self.tm
        self.groups = nb + 1

    def latent_only(self):
        rt = _Rows(self.nb, self.seq, self.ctx)
        rt.n_tiles = self.n_lat_tiles
        return rt

    def group(self, i):
        return jnp.minimum(i // self.tiles_per_ex, self.nb)

    def first_of_group(self, i):
        return jnp.logical_and(i % self.tiles_per_ex == 0, i <= self.n_lat_tiles)


def _mod_spec(rt):
    return pl.BlockSpec((1, N_MOD, D_MODEL), lambda i: (rt.group(i), 0, 0))


def _row_spec(rt, cols):
    return pl.BlockSpec((rt.tm, cols), lambda i: (i, 0))


def _vec_spec(cols):
    return pl.BlockSpec((1, cols), lambda i: (0, 0))


def _group_spec(rt):
    return pl.BlockSpec((1, 1, D_MODEL), lambda i: (rt.group(i), 0, 0))


def _gathered_spec(wg, kind):
    h, off = PACK_HEIGHT[kind], wg[kind][1]
    assert off % h == 0, (kind, off)
    return pl.BlockSpec((N_DEV, h, wg[kind][0].shape[2]), lambda *_: (0, off // h, 0), pipeline_mode=pl.Buffered(1))


def _group_shape(rt):
    return jax.ShapeDtypeStruct((rt.groups, 1, D_MODEL), F32)


def _vec_shape(cols=D_MODEL):
    return jax.ShapeDtypeStruct((1, cols), F32)


def _rms_inv(v):
    return lax.rsqrt(jnp.mean(v * v, axis=-1, keepdims=True) + EPS)


def _norm_mod_val(h_, g_, mod_ref, i_shift, i_scale):
    n = h_ * _rms_inv(h_) * g_
    return n * (1.0 + mod_ref[0, i_scale:i_scale + 1, :]) + mod_ref[0, i_shift:i_shift + 1, :]


def _post_norm_val(h_, z_, g_, mod_ref, i_gate):
    return h_ + mod_ref[0, i_gate:i_gate + 1, :] * (z_ * _rms_inv(z_) * g_)


def _post_norm_bwd_val(dh_, z_, g_, gate):
    rinv = _rms_inv(z_)
    n0 = z_ * rinv
    dn = dh_ * gate * g_
    dz = rinv * (dn - n0 * jnp.mean(dn * n0, axis=-1, keepdims=True))
    return dz, jnp.sum(dh_ * n0 * g_, axis=0, keepdims=True), jnp.sum(dh_ * gate * n0, axis=0, keepdims=True)


def _norm_mod_bwd_val(du_, h_, g_, one_sc):
    rinv = _rms_inv(h_)
    n0 = h_ * rinv
    dn = du_ * g_ * one_sc
    dh = rinv * (dn - n0 * jnp.mean(dn * n0, axis=-1, keepdims=True))
    return (dh, jnp.sum(du_, axis=0, keepdims=True), jnp.sum(du_ * n0 * g_, axis=0, keepdims=True),
            jnp.sum(du_ * one_sc * n0, axis=0, keepdims=True))


def _accumulate(rt, i, group_pairs, global_pairs):
    @pl.when(rt.first_of_group(i))
    def _():
        for ref, _ in group_pairs:
            ref[...] = jnp.zeros_like(ref)

    @pl.when(i == 0)
    def _():
        for ref, _ in global_pairs:
            ref[...] = jnp.zeros_like(ref)

    for ref, val in group_pairs:
        ref[0] += val
    for ref, val in global_pairs:
        ref[...] += val


def _rope_tables(rt):
    pos = np.arange(rt.seq)
    axis_dim = HEAD_DIM // 2
    inv = (ROPE_THETA ** (-np.arange(0, axis_dim, 2, dtype=np.float32) / axis_dim)).astype(np.float32)
    ang_r = (pos // GRID_W).astype(np.float32)[:, None] * inv[None, :]
    ang_c = (pos % GRID_W).astype(np.float32)[:, None] * inv[None, :]
    cr, sr, cc, sc = np.cos(ang_r), np.sin(ang_r), np.cos(ang_c), np.sin(ang_c)
    zero = np.zeros_like(sr)
    cos = np.concatenate([cr, cr, cc, cc], axis=1)
    s_lo = np.concatenate([zero, sr, zero, sc], axis=1)
    s_hi = np.concatenate([-sr, zero, -sc, zero], axis=1)

    def full(t, ctx_value):
        return jnp.asarray(np.concatenate([np.tile(t, (1, 2)), np.full((rt.tm, 128), ctx_value)], axis=0), F32)

    return full(cos, 1.0), full(s_lo, 0.0), full(s_hi, 0.0)


def _table_spec(rt):
    return pl.BlockSpec((rt.tm, 128), lambda i: (jnp.where(i < rt.n_lat_tiles, i % rt.tiles_per_ex, rt.tiles_per_ex), 0))


def _head_mean(x):
    r = lax.broadcasted_iota(jnp.int32, (128, 128), 0) // HEAD_DIM
    c = lax.broadcasted_iota(jnp.int32, (128, 128), 1) // HEAD_DIM
    ones = jnp.where(r == c, 1.0 / HEAD_DIM, 0.0).astype(F32)
    return jnp.dot(x, ones, preferred_element_type=F32, precision=lax.Precision.HIGH)


def _head_stats(t):
    return lax.rsqrt(_head_mean(t * t) + EPS)


def _prep_fwd_body(tm, qkv_ref, c, s1, s2, qn, kn, out_ref):
    def rope(t):
        return t * c + pltpu.roll(t, 16, 1) * s1 + pltpu.roll(t, 112, 1) * s2

    for j in range(12):
        t = qkv_ref[:, j * 128:(j + 1) * 128]
        if j < 4:
            t = rope(t * _head_stats(t) * qn) * Q_SCALE
        elif j == COL_KA:
            t = rope(t * _head_stats(t) * kn)
        elif 6 <= j < 10:
            t = rope(t) * Q_SCALE
        elif j == COL_KB:
            t = rope(t)
        out_ref[:, j * 128:(j + 1) * 128] = t.astype(BF16)


def _prep_bwd_body(dq, dkv, qkv_ref, c, s1, s2, qn, kn, out_ref):
    rows = slice(None)

    def rope_bwd(d):
        return d * c + pltpu.roll(d * s1, 112, 1) + pltpu.roll(d * s2, 16, 1)

    def norm_bwd(t, g, dy):
        rinv = _head_stats(t)
        n = t * rinv
        dn = dy * g
        return rinv * (dn - n * _head_mean(dn * n)), jnp.sum(dy * n, axis=0, keepdims=True)

    dqn = jnp.zeros((1, 128), F32)
    dkn = jnp.zeros((1, 128), F32)
    for j in range(12):
        if j < 4:
            d, dg = norm_bwd(qkv_ref[rows, j * 128:(j + 1) * 128], qn, rope_bwd(dq(slice(j * 128, (j + 1) * 128)) * Q_SCALE))
            dqn = dqn + dg
        elif j == COL_KA:
            d, dg = norm_bwd(qkv_ref[rows, j * 128:(j + 1) * 128], kn, rope_bwd(dkv(slice(0, 128))))
            dkn = dkn + dg
        elif j == COL_VA:
            d = dkv(slice(128, 256))
        elif j < 10:
            d = rope_bwd(dq(slice((j - 2) * 128, (j - 1) * 128)) * Q_SCALE)
        elif j == COL_KB:
            d = rope_bwd(dkv(slice(256, 384)))
        else:
            d = dkv(slice(384, 512))
        out_ref[rows, j * 128:(j + 1) * 128] = d.astype(BF16)
    return dqn, dkn


def _in_fwd(rt, h, gamma, mod, wg, tables, qn, kn, name):
    w_specs, w_args = _in_weight_operands(wg)
    n_w = len(w_args)
    joined = not isinstance(h, (tuple, list))
    n_h = 1 if joined else 2

    def body(*refs):
        g_ref, mod_ref = refs[n_h:n_h + 2]
        rest = refs[n_h + 2:]
        c_ref, s1_ref, s2_ref, qn_ref, kn_ref, u_ref, qkn_ref, qkvp_ref = rest[n_w:n_w + 8]
        qkv_ref, w_scr = rest[-2:]
        i = pl.program_id(0)

        @pl.when(i == 0)
        def _():
            _unpack_in_pieces(rest[0], rest[1] if n_w == 2 else None, w_scr)

        if joined:
            h_ = refs[0][...]
        else:
            h_ = jnp.where(i < rt.n_lat_tiles, refs[0][...], refs[1][...])
            rest[n_w + 8][...] = h_
        u = _norm_mod_val(h_, g_ref[...], mod_ref, 0, 1).astype(BF16)
        u_ref[...] = u
        qkv_ref[...] = jnp.dot(u, w_scr[...], preferred_element_type=F32)
        qkn_ref[...] = qkv_ref[:, 0:NORMED_COLS]
        _prep_fwd_body(rt.tm, qkv_ref, c_ref[...], s1_ref[...], s2_ref[...], qn_ref[...], kn_ref[...], qkvp_ref)

    if joined:
        h_specs, h_args = [_row_spec(rt, D_MODEL)], [h]
    else:
        h_specs = [pl.BlockSpec((rt.tm, D_MODEL), lambda i: (jnp.minimum(i, rt.n_lat_tiles - 1), 0)),
                   pl.BlockSpec((rt.tm, D_MODEL), lambda i: (jnp.maximum(i - rt.n_lat_tiles, 0), 0))]
        h_args = list(h)
    out_specs = [_row_spec(rt, D_MODEL), _row_spec(rt, NORMED_COLS), _row_spec(rt, IN_COLS)]
    out_shape = [jax.ShapeDtypeStruct((rt.rows, D_MODEL), BF16), jax.ShapeDtypeStruct((rt.rows, NORMED_COLS), F32),
                 jax.ShapeDtypeStruct((rt.rows, IN_COLS), BF16)]
    if not joined:
        out_specs.append(_row_spec(rt, D_MODEL))
        out_shape.append(jax.ShapeDtypeStruct((rt.rows, D_MODEL), F32))
    return pl.pallas_call(
        body, name=name, grid=(rt.n_tiles,),
        in_specs=h_specs + [_vec_spec(D_MODEL), _mod_spec(rt)] + w_specs + [_table_spec(rt)] * 3 + [_vec_spec(128)] * 2,
        out_specs=out_specs, out_shape=out_shape,
        scratch_shapes=[pltpu.VMEM((rt.tm, IN_COLS), F32), pltpu.VMEM((D_MODEL, IN_COLS), BF16)],
        compiler_params=_params(("arbitrary",)),
    )(*h_args, gamma, mod, *w_args, *tables, qn, kn)


def _in_bwd(rt, dq, dkv, qkv, tables, qn, kn, wg, h, dres, mod, gamma, latent_only, name, comm=None, dead_ctx_dkv=None):
    last = rt.n_lat_tiles - 1
    w_specs, w_args = _in_weight_operands(wg)
    n_w = len(w_args)
    n_dead = 0 if dead_ctx_dkv is None else 1

    def body(dq_ref, dkv_ref, qkv_ref, c_ref, s1_ref, s2_ref, qn_ref, kn_ref, *rest):
        h_ref, dres_ref, mod_ref, g_ref, dqkv_ref, dh_ref, dqn_ref, dkn_ref, dsh_ref, dsc_ref, dg_ref, w_scr = rest[n_w + n_dead:]
        i = pl.program_id(0)

        @pl.when(i == 0)
        def _():
            _unpack_in_pieces(rest[0], rest[1] if n_w == 2 else None, w_scr)

        if n_dead:
            c1_ref, lat = rest[n_w], i <= last
            load_dq = lambda cols: jnp.where(lat, dq_ref[:, cols], 0.0)
            load_dkv = lambda cols: jnp.where(lat, dkv_ref[:, cols], c1_ref[:, cols])
            dres_ = jnp.where(lat, dres_ref[...], 0.0)
        else:
            load_dq, load_dkv, dres_ = (lambda cols: dq_ref[:, cols]), (lambda cols: dkv_ref[:, cols]), dres_ref[...]
        dqn, dkn = _prep_bwd_body(load_dq, load_dkv, qkv_ref, c_ref[...], s1_ref[...], s2_ref[...], qn_ref[...], kn_ref[...], dqkv_ref)
        du = lax.dot_general(dqkv_ref[...], w_scr[...], NT, preferred_element_type=F32)
        dh, dsh, dsc, dg = _norm_mod_bwd_val(du, h_ref[...], g_ref[...], 1.0 + mod_ref[0, 1:2, :])
        if latent_only:
            @pl.when(i <= last)
            def _():
                dh_ref[...] = dres_ + dh
        else:
            dh_ref[...] = dres_ + dh
        _accumulate(rt, i, [(dsh_ref, dsh), (dsc_ref, dsc)], [(dg_ref, dg), (dqn_ref, dqn), (dkn_ref, dkn)])

    dh_spec = pl.BlockSpec((rt.tm, D_MODEL), lambda i: (jnp.minimum(i, last), 0)) if latent_only else _row_spec(rt, D_MODEL)
    dead_specs = [] if dead_ctx_dkv is None else [pl.BlockSpec((rt.tm, 512), lambda i: (jnp.maximum(i - rt.n_lat_tiles, 0), 0))]
    dead_args = [] if dead_ctx_dkv is None else [dead_ctx_dkv]
    return _comm_call(
        body, comm, name=name, grid=(rt.n_tiles,),
        in_specs=[_row_spec(rt, 1024), _row_spec(rt, 512), _row_spec(rt, NORMED_COLS)] + [_table_spec(rt)] * 3 + [_vec_spec(128)] * 2
        + w_specs + dead_specs + [_row_spec(rt, D_MODEL), _row_spec(rt, D_MODEL), _mod_spec(rt), _vec_spec(D_MODEL)],
        out_specs=[_row_spec(rt, IN_COLS), dh_spec, _vec_spec(128), _vec_spec(128),
                   _group_spec(rt), _group_spec(rt), _vec_spec(D_MODEL)],
        out_shape=[jax.ShapeDtypeStruct((rt.rows, IN_COLS), BF16),
                   jax.ShapeDtypeStruct((rt.n_lat if latent_only else rt.rows, D_MODEL), F32),
                   _vec_shape(128), _vec_shape(128), _group_shape(rt), _group_shape(rt), _vec_shape()],
        args=[dq, dkv, qkv, *tables, qn, kn, *w_args, *dead_args, h, dres, mod, gamma], aliases={}, semantics=("arbitrary",),
        scratch=[pltpu.VMEM((D_MODEL, IN_COLS), BF16)])


def _out_fwd(rt, o, wg, h, mod, g_post_mix, g_pre_mlp, name):
    def body(o_ref, w_ref, h_ref, mod_ref, gpost_ref, gpre_ref, mix_ref, h1_ref, u2_ref):
        mix = jnp.dot(o_ref[...], w_ref[...].reshape(D_MODEL, D_MODEL), preferred_element_type=F32)
        mix_ref[...] = mix
        h1 = _post_norm_val(h_ref[...], mix, gpost_ref[...], mod_ref, 2)
        h1_ref[...] = h1
        u2_ref[...] = _norm_mod_val(h1, gpre_ref[...], mod_ref, 3, 4).astype(BF16)

    return pl.pallas_call(
        body, name=name, grid=(rt.n_tiles,),
        in_specs=[_row_spec(rt, D_MODEL), _gathered_spec(wg, "out"), _row_spec(rt, D_MODEL), _mod_spec(rt),
                  _vec_spec(D_MODEL), _vec_spec(D_MODEL)],
        out_specs=[_row_spec(rt, D_MODEL)] * 3,
        out_shape=[jax.ShapeDtypeStruct((rt.rows, D_MODEL), F32), jax.ShapeDtypeStruct((rt.rows, D_MODEL), F32),
                   jax.ShapeDtypeStruct((rt.rows, D_MODEL), BF16)],
        compiler_params=_params(("parallel",)),
    )(o, wg["out"][0], h, mod, g_post_mix, g_pre_mlp)


def _out_bwd(rt, dh1, mix, wg, mod, g_post_mix, name):
    def body(dh_ref, mix_ref, w_ref, mod_ref, g_ref, dmix_ref, do_ref, dgate_ref, dg_ref):
        i = pl.program_id(0)
        dz, dgate, dg = _post_norm_bwd_val(dh_ref[...], mix_ref[...], g_ref[...], mod_ref[0, 2:3, :])
        dzb = dz.astype(BF16)
        dmix_ref[...] = dzb
        do_ref[...] = lax.dot_general(dzb, w_ref[...].reshape(D_MODEL, D_MODEL), NT, preferred_element_type=F32).astype(BF16)
        _accumulate(rt, i, [(dgate_ref, dgate)], [(dg_ref, dg)])

    return pl.pallas_call(
        body, name=name, grid=(rt.n_tiles,),
        in_specs=[_row_spec(rt, D_MODEL), _row_spec(rt, D_MODEL), _gathered_spec(wg, "out"), _mod_spec(rt), _vec_spec(D_MODEL)],
        out_specs=[_row_spec(rt, D_MODEL), _row_spec(rt, D_MODEL), _group_spec(rt), _vec_spec(D_MODEL)],
        out_shape=[jax.ShapeDtypeStruct((rt.rows, D_MODEL), BF16), jax.ShapeDtypeStruct((rt.rows, D_MODEL), BF16),
                   _group_shape(rt), _vec_shape()],
        compiler_params=_params(("arbitrary",)),
    )(dh1, mix, wg["out"][0], mod, g_post_mix)


def _w_chunk(w_ref, k):
    return w_ref[2 * k:2 * k + 2].reshape(1024, 1024)


def _mlp_fwd(rt, u2, h1, wg, mod, g_post_mlp, name, comm=None, target=None):
    last = rt.n_lat_tiles - 1

    def body(u2_ref, h1_ref, wu_ref, wd_ref, mod_ref, g_ref, *rest):
        u2_ = u2_ref[...]
        y = jnp.zeros((rt.tm, D_MODEL), F32)
        for k in range(D_FF // 1024):
            a = jnp.maximum(jnp.dot(u2_, _w_chunk(wu_ref, k), preferred_element_type=F32), 0.0)
            rest[-3 if target is None else -4][:, k * 1024:(k + 1) * 1024] = a.astype(BF16)
            y = y + jnp.dot((a * a).astype(BF16), _w_chunk(wd_ref, k), preferred_element_type=F32)
        h2 = _post_norm_val(h1_ref[...], y, g_ref[...], mod_ref, 5)
        if target is None:
            _, y_ref, h2_ref = rest
            y_ref[...] = y
            h2_ref[...] = h2
        else:
            t_ref, _, y_ref, dh_ref, sq_ref = rest
            y_ref[...] = y
            i = pl.program_id(0)

            @pl.when(i == 0)
            def _():
                sq_ref[...] = jnp.zeros_like(sq_ref)

            @pl.when(i <= last)
            def _():
                e = h2 - t_ref[...]
                dh_ref[...] = e * (1.0 / D_MODEL)
                sq_ref[...] += jnp.sum(e * e, axis=0, keepdims=True)

            @pl.when(i > last)
            def _():
                dh_ref[...] = jnp.zeros_like(dh_ref)

    in_specs = [_row_spec(rt, D_MODEL), _row_spec(rt, D_MODEL), _gathered_spec(wg, "up"), _gathered_spec(wg, "down"),
                _mod_spec(rt), _vec_spec(D_MODEL)]
    args = [u2, h1, wg["up"][0], wg["down"][0], mod, g_post_mlp]
    out_specs = [_row_spec(rt, D_FF), _row_spec(rt, D_MODEL), _row_spec(rt, D_MODEL)]
    out_shape = [jax.ShapeDtypeStruct((rt.rows, D_FF), BF16), jax.ShapeDtypeStruct((rt.rows, D_MODEL), F32),
                 jax.ShapeDtypeStruct((rt.rows, D_MODEL), F32)]
    if target is not None:
        in_specs.append(pl.BlockSpec((rt.tm, D_MODEL), lambda i: (jnp.minimum(i, last), 0)))
        args.append(target)
        out_specs.append(_vec_spec(D_MODEL))
        out_shape.append(_vec_shape())
    return _comm_call(body, comm, name=name, grid=(rt.n_tiles,), in_specs=in_specs, out_specs=out_specs, out_shape=out_shape,
                      args=args, aliases={}, semantics=("parallel",) if target is None else ("arbitrary",))


def _mlp_down_bwd(rt, dh, y, ra, wg, mod, g_post_mlp, name):
    def body(dh_ref, y_ref, ra_ref, wd_ref, mod_ref, g_ref, dy_ref, da_ref, dgate_ref, dg_ref):
        i = pl.program_id(0)
        dz, dgate, dg = _post_norm_bwd_val(dh_ref[...], y_ref[...], g_ref[...], mod_ref[0, 5:6, :])
        dyb = dz.astype(BF16)
        dy_ref[...] = dyb
        for k in range(D_FF // 1024):
            dr = lax.dot_general(dyb, _w_chunk(wd_ref, k), NT, preferred_element_type=F32)
            da_ref[:, k * 1024:(k + 1) * 1024] = (dr * (2.0 * ra_ref[:, k * 1024:(k + 1) * 1024].astype(F32))).astype(BF16)
        _accumulate(rt, i, [(dgate_ref, dgate)], [(dg_ref, dg)])

    return pl.pallas_call(
        body, name=name, grid=(rt.n_tiles,),
        in_specs=[_row_spec(rt, D_MODEL), _row_spec(rt, D_MODEL), _row_spec(rt, D_FF), _gathered_spec(wg, "down"),
                  _mod_spec(rt), _vec_spec(D_MODEL)],
        out_specs=[_row_spec(rt, D_MODEL), _row_spec(rt, D_FF), _group_spec(rt), _vec_spec(D_MODEL)],
        out_shape=[jax.ShapeDtypeStruct((rt.rows, D_MODEL), BF16), jax.ShapeDtypeStruct((rt.rows, D_FF), BF16),
                   _group_shape(rt), _vec_shape()],
        compiler_params=_params(("arbitrary",)),
    )(dh, y, ra, wg["down"][0], mod, g_post_mlp)


def _mlp_up_bwd(rt, da, wg, h1, dh, mod, g_pre_mlp, name, comm=None):
    def body(da_ref, wu_ref, h1_ref, dh_ref, mod_ref, g_ref, dh1_ref, dsh_ref, dsc_ref, dg_ref):
        i = pl.program_id(0)
        du = jnp.zeros((rt.tm, D_MODEL), F32)
        for k in range(D_FF // 1024):
            du = du + lax.dot_general(da_ref[:, k * 1024:(k + 1) * 1024], _w_chunk(wu_ref, k), NT, preferred_element_type=F32)
        d, dsh, dsc, dg = _norm_mod_bwd_val(du, h1_ref[...], g_ref[...], 1.0 + mod_ref[0, 4:5, :])
        dh1_ref[...] = dh_ref[...] + d
        _accumulate(rt, i, [(dsh_ref, dsh), (dsc_ref, dsc)], [(dg_ref, dg)])

    return _comm_call(
        body, comm, name=name, grid=(rt.n_tiles,),
        in_specs=[_row_spec(rt, D_FF), _gathered_spec(wg, "up"), _row_spec(rt, D_MODEL), _row_spec(rt, D_MODEL),
                  _mod_spec(rt), _vec_spec(D_MODEL)],
        out_specs=[_row_spec(rt, D_MODEL), _group_spec(rt), _group_spec(rt), _vec_spec(D_MODEL)],
        out_shape=[jax.ShapeDtypeStruct((rt.rows, D_MODEL), F32), _group_shape(rt), _group_shape(rt), _vec_shape()],
        args=[da, wg["up"][0], h1, dh, mod, g_pre_mlp], aliases={}, semantics=("arbitrary",))


def _wgrad_packed(rt, a, b, kind, off, n_rows, p_prev, name, comm=None):
    h = PACK_HEIGHT[kind]
    tk = rt.tm
    assert off % h == 0, (kind, off)

    def body(a_ref, b_ref, *rest):
        o_ref = rest[-1]
        i = pl.program_id(0)

        @pl.when(i == 0)
        def _():
            o_ref[...] = jnp.zeros_like(o_ref)

        if kind == "in":
            res = lax.dot_general(a_ref[...], b_ref[...], TN, preferred_element_type=F32)
            for k in range(4):
                for c in range(2):
                    for t in range(2):
                        o_ref[c, k, :, t * IN_PIECE_COLS:(t + 1) * IN_PIECE_COLS] += \
                            res[c * 512 + t * h:c * 512 + (t + 1) * h, k * IN_PIECE_COLS:(k + 1) * IN_PIECE_COLS]
        elif kind == "out":
            res = lax.dot_general(a_ref[...], b_ref[...], TN, preferred_element_type=F32)
            for k in range(4):
                for c in range(2):
                    o_ref[c, k] += res[(2 * k + c) * h:(2 * k + c + 1) * h]
        else:
            for k in range(4):
                if kind == "up":
                    res = lax.dot_general(a_ref[...], b_ref[:, k * 1024:(k + 1) * 1024], TN, preferred_element_type=F32)
                else:
                    ra = a_ref[:, k * 1024:(k + 1) * 1024].astype(F32)
                    res = lax.dot_general((ra * ra).astype(BF16), b_ref[...], TN, preferred_element_type=F32)
                o_ref[0, k] += res[0:h]
                o_ref[1, k] += res[h:2 * h]

    in_specs = [pl.BlockSpec((tk, a.shape[1]), lambda i: (i, 0)), pl.BlockSpec((tk, b.shape[1]), lambda i: (i, 0))]
    args = [a, b]
    aliases = {}
    if p_prev is not None:
        in_specs.append(pl.BlockSpec(memory_space=pl.ANY))
        args.append(p_prev)
        aliases = {2: 0}
    outs = _comm_call(
        body, comm, name=name, grid=(rt.n_tiles,),
        in_specs=in_specs,
        out_specs=[pl.BlockSpec((2, 4, h, 1024), lambda i: (0, 0, off // h, 0))],
        out_shape=[jax.ShapeDtypeStruct((2, 4, n_rows, 1024), F32)],
        args=args, aliases=aliases, semantics=("arbitrary",))
    return outs[0] if comm is None else outs


def _ada_wgrad(xs, dm, name):
    depth, _, cols = dm.shape

    def body(x_ref, d_ref, o_ref):
        for l in range(depth):
            o_ref[l] = lax.dot_general(x_ref[...], d_ref[l], TN, preferred_element_type=F32)

    return pl.pallas_call(body, name=name, out_shape=jax.ShapeDtypeStruct((depth, xs.shape[1], cols), F32),
                          compiler_params=pltpu.CompilerParams(vmem_limit_bytes=VMEM_LIMIT))(xs, dm)


def _stack_heads(x, kvi):
    x = x.astype(F32)
    tq = x.shape[0]
    lane = lax.broadcasted_iota(jnp.int32, (tq, 128), 1)
    keep = lane < HEAD_DIM if kvi == 0 else lane >= HEAD_DIM
    parts = []
    for p in range(2):
        pair = x[:, p * 128:(p + 1) * 128]
        swapped = pltpu.roll(pair, HEAD_DIM, 1)
        lo_head, hi_head = (pair, swapped) if kvi == 0 else (swapped, pair)
        parts += [jnp.where(keep, lo_head, 0.0), jnp.where(keep, hi_head, 0.0)]
    return jnp.concatenate(parts, axis=0).astype(BF16)


def _unstack_heads(o4, kvi):
    tq = o4.shape[0] // GROUP
    lane = lax.broadcasted_iota(jnp.int32, (tq, 128), 1)
    outs = []
    for p in range(2):
        r_lo, r_hi = o4[(2 * p) * tq:(2 * p + 1) * tq], o4[(2 * p + 1) * tq:(2 * p + 2) * tq]
        if kvi == 0:
            lo, hi = r_lo, pltpu.roll(r_hi, HEAD_DIM, 1)
        else:
            lo, hi = pltpu.roll(r_lo, HEAD_DIM, 1), r_hi
        outs.append(jnp.where(lane < HEAD_DIM, lo, hi))
    return jnp.concatenate(outs, axis=1)


def _per_head(shape, axis, tq, values):
    head = lax.broadcasted_iota(jnp.int32, shape, axis) // tq
    out = jnp.zeros(shape, F32)
    for g in range(GROUP):
        out = jnp.where(head == g, values[g], out)
    return out


KEY_CHUNK = 512
Q_TILE = 128
Q_TILE_FWD = 256


def _key_chunks(k_ref, v_ref, n, kc=KEY_CHUNK):
    kc = min(kc, n)
    return [(k_ref[c * kc:(c + 1) * kc, :], v_ref[c * kc:(c + 1) * kc, :], None) for c in range(n // kc)]


def _softmax_fwd(qs, chunks, sink_col):
    logits = []
    for k, _, mask in chunks:
        s = lax.dot_general(qs, k, NT, preferred_element_type=F32)
        logits.append(s if mask is None else jnp.where(mask, s, NEG_BIG))
    m = functools.reduce(jnp.maximum, [jnp.max(s, axis=1, keepdims=True) for s in logits])
    if sink_col is not None:
        m = jnp.maximum(m, sink_col)
    l = jnp.zeros_like(m) if sink_col is None else jnp.exp(sink_col - m)
    acc = jnp.zeros((qs.shape[0], 128), F32)
    for s, (_, v, _) in zip(logits, chunks):
        p = jnp.exp(s - m)
        l = l + jnp.sum(p, axis=1, keepdims=True)
        acc = acc + jnp.dot(p.astype(BF16), v, preferred_element_type=F32)
    return acc / l, m + jnp.log(l)


def _to_rows(col):
    return jnp.transpose(jnp.broadcast_to(col, (col.shape[0], 128)))[0:8, :]


def _softmax_bwd(qs, dos, lse_row, delta_row, chunks):
    dq = jnp.zeros((qs.shape[0], 128), F32)
    grads = []
    for k, v, mask in chunks:
        s = lax.dot_general(k, qs, NT, preferred_element_type=F32)
        if mask is not None:
            s = jnp.where(mask, s, NEG_BIG)
        p = jnp.exp(s - lse_row)
        dp = lax.dot_general(v, dos, NT, preferred_element_type=F32)
        ds = (p * (dp - delta_row)).astype(BF16)
        dv = jnp.dot(p.astype(BF16), dos, preferred_element_type=F32)
        dk = jnp.dot(ds, qs, preferred_element_type=F32)
        dq = dq + lax.dot_general(ds, k, TN, preferred_element_type=F32)
        grads.append((dk, dv))
    return dq, grads


def _band(qi, tq, seq):
    span = tq + 2 * WINDOW
    start = pl.multiple_of(jnp.clip(qi * tq - WINDOW, 0, seq - span), 64)
    return start, span


def _band_mask(qi, tq, start, span, query_axis):
    shape = (GROUP * tq, span) if query_axis == 0 else (span, GROUP * tq)
    qpos = qi * tq + lax.broadcasted_iota(jnp.int32, shape, query_axis) % tq
    kpos = start + lax.broadcasted_iota(jnp.int32, shape, 1 - query_axis)
    return jnp.abs(kpos - qpos) <= WINDOW


def _qkv_specs(rt, tq, q_row, ctx_row, with_latent):
    specs = [pl.BlockSpec((tq, 256), functools.partial(lambda b, i, col: (q_row(b, i), col), col=col)) for col in (0, 1, 3, 4)]
    if with_latent:
        specs += [pl.BlockSpec((rt.seq, 128), functools.partial(lambda b, i, col: (b, col), col=col))
                  for col in (COL_KA, COL_VA, COL_KB, COL_VB)]
    specs += [pl.BlockSpec((rt.ctx, 128), functools.partial(lambda b, i, col: (ctx_row(b), col), col=col))
              for col in (COL_KA, COL_VA, COL_KB, COL_VB)]
    return specs


def _attn_fwd(rt, qkvp, sink, o_prev, name, comm=None):
    latent = o_prev is None
    seq, ctx, nb = rt.seq, rt.ctx, rt.nb
    tq = Q_TILE_FWD if latent else ctx
    tile = Q_TILE if latent else ctx
    parts = tq // tile
    nq = seq // tq if latent else 1
    ctx_blk0 = rt.n_lat // ctx
    q_row = (lambda b, i: b * nq + i) if latent else (lambda b, i: ctx_blk0 + b)

    def store_lse(lse_ref, j, lse_col):
        rows = _to_rows(lse_col)
        for part in range(parts):
            lse_ref[part, j] = jnp.concatenate([rows[:, g * tq + part * tile:g * tq + (part + 1) * tile] for g in range(GROUP)], axis=1)

    def body(sink_ref, qa0, qa1, qb0, qb1, *rest):
        if latent:
            kal, val, kbl, vbl, kac, vac, kbc, vbc, o_ref, lse_ref = rest
        else:
            kac, vac, kbc, vbc, _, o_ref, lse_ref = rest
        qi = pl.program_id(1)
        for kvi, (qa, qb) in enumerate(((qa0, qb0), (qa1, qb1))):
            src_a = _key_chunks(kac, vac, ctx)
            src_b = _key_chunks(kbc, vbc, ctx)
            if latent:
                src_a += _key_chunks(kal, val, seq, seq)
                start, span = _band(qi, tq, seq)
                src_b.append((kbl[pl.ds(start, span), :], vbl[pl.ds(start, span), :], _band_mask(qi, tq, start, span, 0)))
            oa, lse = _softmax_fwd(_stack_heads(qa[...], kvi), src_a, None)
            o_ref[:, kvi * 256:(kvi + 1) * 256] = _unstack_heads(oa, kvi).astype(BF16)
            store_lse(lse_ref, kvi, lse)
            sink_col = _per_head((GROUP * tq, 1), 0, tq, [sink_ref[kvi * GROUP + g] for g in range(GROUP)])
            ob, lse = _softmax_fwd(_stack_heads(qb[...], kvi), src_b, sink_col)
            o_ref[:, 512 + kvi * 256:512 + (kvi + 1) * 256] = _unstack_heads(ob, kvi).astype(BF16)
            store_lse(lse_ref, 2 + kvi, lse)

    specs = _qkv_specs(rt, tq, q_row, lambda b: ctx_blk0 + b, latent)
    args = [sink] + [qkvp] * len(specs)
    in_specs = [pl.BlockSpec(memory_space=pltpu.SMEM)] + specs
    aliases = {}
    if not latent:
        in_specs.append(pl.BlockSpec(memory_space=pl.ANY))
        args.append(o_prev)
        aliases = {len(args) - 1: 0}
    return _comm_call(
        body, comm, name=name, grid=(nb, nq),
        in_specs=in_specs,
        out_specs=[pl.BlockSpec((tq, 1024), lambda b, i: (q_row(b, i), 0)),
                   pl.BlockSpec((parts, 4, 8, GROUP * tile), lambda b, i: (b * nq + i, 0, 0, 0))],
        out_shape=[jax.ShapeDtypeStruct((rt.rows, 1024), BF16), jax.ShapeDtypeStruct((nb * nq * parts, 4, 8, GROUP * tile), F32)],
        args=args, aliases=aliases, semantics=("parallel", "parallel"))


def _attn_bwd(rt, qkvp, o, lse, do, sink, prev, name, comm=None):
    latent = prev is None
    seq, ctx, nb = rt.seq, rt.ctx, rt.nb
    tq = Q_TILE if latent else ctx
    nq = seq // tq if latent else 1
    ctx_blk0 = rt.n_lat // ctx
    q_row = (lambda b, i: b * nq + i) if latent else (lambda b, i: ctx_blk0 + b)
    kc = min(KEY_CHUNK, seq)

    def body(sink_ref, qa0, qa1, qb0, qb1, *rest):
        if latent:
            kal, val, kbl, vbl, kac, vac, kbc, vbc, do_ref, o_ref, lse_ref, dq_ref, dl_ref, dc_ref, dsink_ref = rest
        else:
            kac, vac, kbc, vbc, do_ref, o_ref, lse_ref, c1_ref, _, _, dq_ref, dc_ref, dsink_ref = rest
        b, qi = pl.program_id(0), pl.program_id(1)

        def rows_of(cols, kvi, mixer):
            dos = _stack_heads(do_ref[:, cols], kvi)
            delta = jnp.sum(dos.astype(F32) * _stack_heads(o_ref[:, cols], kvi).astype(F32), axis=1, keepdims=True)
            return dos, lse_ref[0, 2 * mixer + kvi, 0:1, :], _to_rows(delta)[0:1, :]

        @pl.when(jnp.logical_and(b == 0, qi == 0))
        def _():
            dsink_ref[...] = jnp.zeros_like(dsink_ref)

        if latent:
            @pl.when(qi == 0)
            def _():
                dc_ref[...] = jnp.zeros_like(dc_ref)
                dl_ref[...] = jnp.zeros_like(dl_ref)
        else:
            dc_ref[...] = c1_ref[...]

        head_row = lax.broadcasted_iota(jnp.int32, (8, 128), 0)
        for kvi, (qa, qb) in enumerate(((qa0, qb0), (qa1, qb1))):
            cols = slice(kvi * 256, (kvi + 1) * 256)
            dos, lse_row, delta_row = rows_of(cols, kvi, 0)
            src = _key_chunks(kac, vac, ctx)
            if latent:
                src += _key_chunks(kal, val, seq)
            dq4, grads = _softmax_bwd(_stack_heads(qa[...], kvi), dos, lse_row, delta_row, src)
            dq_ref[:, cols] = _unstack_heads(dq4, kvi)
            dc_ref[:, 0:128] += grads[0][0]
            dc_ref[:, 128:256] += grads[0][1]
            for c, (dk, dv) in enumerate(grads[1:]):
                dl_ref[c * kc:(c + 1) * kc, 0:128] += dk
                dl_ref[c * kc:(c + 1) * kc, 128:256] += dv
            cols = slice(512 + kvi * 256, 512 + (kvi + 1) * 256)
            dos, lse_row, delta_row = rows_of(cols, kvi, 1)
            src = _key_chunks(kbc, vbc, ctx)
            if latent:
                start, span = _band(qi, tq, seq)
                src.append((kbl[pl.ds(start, span), :], vbl[pl.ds(start, span), :], _band_mask(qi, tq, start, span, 1)))
            dq4, grads = _softmax_bwd(_stack_heads(qb[...], kvi), dos, lse_row, delta_row, src)
            dq_ref[:, cols] = _unstack_heads(dq4, kvi)
            dc_ref[:, 256:384] += grads[0][0]
            dc_ref[:, 384:512] += grads[0][1]
            if latent:
                dl_ref[pl.ds(start, span), 256:384] += grads[1][0]
                dl_ref[pl.ds(start, span), 384:512] += grads[1][1]
            sink_row = _per_head((1, GROUP * tq), 1, tq, [sink_ref[kvi * GROUP + g] for g in range(GROUP)])
            dsink = -jnp.exp(sink_row - lse_row) * delta_row
            head = lax.broadcasted_iota(jnp.int32, (1, GROUP * tq), 1) // tq
            upd = jnp.zeros((8, 128), F32)
            for g in range(GROUP):
                upd = jnp.where(head_row == kvi * GROUP + g, jnp.sum(jnp.where(head == g, dsink, 0.0)), upd)
            dsink_ref[...] += upd

    specs = _qkv_specs(rt, tq, q_row, lambda b: ctx_blk0 + b, latent)
    q_rows_spec = pl.BlockSpec((tq, 1024), lambda b, i: (q_row(b, i), 0))
    in_specs = ([pl.BlockSpec(memory_space=pltpu.SMEM)] + specs
                + [q_rows_spec, q_rows_spec, pl.BlockSpec((1, 4, 8, GROUP * tq), lambda b, i: (b * nq + i, 0, 0, 0))])
    args = [sink] + [qkvp] * len(specs) + [do, o, lse]
    dq_shape = jax.ShapeDtypeStruct((rt.rows, 1024), F32)
    dkv_shape = jax.ShapeDtypeStruct((rt.rows, 512), F32)
    dsink_spec, dsink_shape = pl.BlockSpec((8, 128), lambda b, i: (0, 0)), jax.ShapeDtypeStruct((8, 128), F32)
    dq_spec = pl.BlockSpec((tq, 1024), lambda b, i: (q_row(b, i), 0))
    if latent:
        out_specs = [dq_spec, pl.BlockSpec((seq, 512), lambda b, i: (b, 0)), pl.BlockSpec((ctx, 512), lambda b, i: (b, 0)), dsink_spec]
        out_shape = [dq_shape, dkv_shape, jax.ShapeDtypeStruct((rt.n_ctx, 512), F32), dsink_shape]
        aliases = {}
    else:
        dq_prev, dkv_prev, c1 = prev
        in_specs += [pl.BlockSpec((ctx, 512), lambda b, i: (b, 0)), pl.BlockSpec(memory_space=pl.ANY), pl.BlockSpec(memory_space=pl.ANY)]
        args += [c1, dq_prev, dkv_prev]
        out_specs = [dq_spec, pl.BlockSpec((ctx, 512), lambda b, i: (ctx_blk0 + b, 0)), dsink_spec]
        out_shape = [dq_shape, dkv_shape, dsink_shape]
        aliases = {len(args) - 2: 0, len(args) - 1: 1}
    return _comm_call(body, comm, name=name, grid=(nb, nq), in_specs=in_specs, out_specs=out_specs, out_shape=out_shape,
                      args=args, aliases=aliases, semantics=("arbitrary", "arbitrary"))


def _silu(x):
    return x / (1.0 + jnp.exp(-x))


def _whole(shape):
    return pl.BlockSpec(shape, lambda i, s: (0,) * len(shape))


def _ada_half_spec(cols):
    return pl.BlockSpec((DEPTH, D_MODEL, cols), lambda i, s: (0, 0, s[0]))


def _ada_fwd(cond, w_ada, b_half, c_idx, name):
    rows = cond.shape[0]
    cols = w_ada.shape[2] // 2

    def body(s_ref, c_ref, w_ref, b_ref, x_ref, o_ref):
        xs = _silu(c_ref[...]).astype(BF16)
        x_ref[...] = xs
        for l in range(DEPTH):
            o_ref[l] = jnp.dot(xs, w_ref[l].astype(BF16), preferred_element_type=F32) + b_ref[l]

    grid_spec = pltpu.PrefetchScalarGridSpec(
        num_scalar_prefetch=1, grid=(1,),
        in_specs=[_whole(cond.shape), _ada_half_spec(cols), _whole(b_half.shape)],
        out_specs=[_whole((rows, D_MODEL)), _whole((DEPTH, rows, cols))])
    return pl.pallas_call(
        body, name=name, grid_spec=grid_spec,
        out_shape=[jax.ShapeDtypeStruct((rows, D_MODEL), BF16), jax.ShapeDtypeStruct((DEPTH, rows, cols), F32)],
        compiler_params=_params(("arbitrary",)),
    )(c_idx, cond, w_ada, b_half)


def _ada_cond_bwd(dcx, w_ada, c_idx, name):
    _, rows, cols = dcx.shape

    def body(s_ref, d_ref, w_ref, o_ref):
        acc = jnp.zeros((rows, D_MODEL), F32)
        for l in range(DEPTH):
            acc = acc + lax.dot_general(d_ref[l], w_ref[l].astype(BF16), NT, preferred_element_type=F32)
        o_ref[...] = acc

    grid_spec = pltpu.PrefetchScalarGridSpec(
        num_scalar_prefetch=1, grid=(1,),
        in_specs=[_whole(dcx.shape), _ada_half_spec(cols)], out_specs=_whole((rows, D_MODEL)))
    return pl.pallas_call(body, name=name, grid_spec=grid_spec, out_shape=jax.ShapeDtypeStruct((rows, D_MODEL), F32),
                          compiler_params=_params(("arbitrary",)))(c_idx, dcx, w_ada)


def _dev_sum(x, name, comm=None):
    _, r, c = x.shape

    def body(x_ref, o_ref):
        v = x_ref[0]
        for d in range(1, N_DEV):
            v = v + x_ref[d]
        o_ref[...] = v

    return _comm_call(body, comm, name=name, grid=(1,), in_specs=[pl.BlockSpec(x.shape, lambda i: (0, 0, 0))],
                      out_specs=[pl.BlockSpec((r, c), lambda i: (0, 0))], out_shape=[jax.ShapeDtypeStruct((r, c), F32)],
                      args=[x], aliases={}, semantics=("arbitrary",))


def _adam_val(w, g, m, v):
    c1 = 1.0 / (1.0 - ADAM_B1 ** ADAM_STEP)
    c2 = 1.0 / (1.0 - ADAM_B2 ** ADAM_STEP)
    nm = ADAM_B1 * m + (1.0 - ADAM_B1) * g
    nv = ADAM_B2 * v + (1.0 - ADAM_B2) * (g * g)
    return -ADAM_LR * ((nm * c1) / (jnp.sqrt(nv * c2) + ADAM_EPS) + ADAM_WD * w), nm, nv


def _small_update(tot, dcc_parts, params, n_groups, name):
    n_p = len(params)
    mod_rows = n_groups * N_MOD
    head_row = DEPTH * mod_rows + 4 * DEPTH

    def body(tot_ref, dcc_ref, *refs):
        ins, outs = refs[:3 * n_p], refs[3 * n_p:]

        def update(p, rows, cols, g):
            w_ref, m_ref, v_ref = ins[3 * p:3 * p + 3]
            g_ref, d_ref, nm_ref, nv_ref = outs[4 * p:4 * p + 4]
            d, nm, nv = _adam_val(w_ref[rows, cols], g, m_ref[rows, cols], v_ref[rows, cols])
            g_ref[rows, cols] = g
            d_ref[rows, cols] = d
            nm_ref[rows, cols] = nm
            nv_ref[rows, cols] = nv

        acc = dcc_ref[0, 0:1, :]
        for d in range(1, N_DEV):
            acc = acc + dcc_ref[d, 0:1, :]
        c = ins[0][...]
        sg = 1.0 / (1.0 + jnp.exp(-c))
        update(0, slice(0, 1), slice(None), acc * (sg * (1.0 + c * (1.0 - sg))))
        for l in range(DEPTH):
            for i in range(N_MOD):
                g = tot_ref[l * mod_rows + i:l * mod_rows + i + 1, :]
                for grp in range(1, n_groups):
                    g = g + tot_ref[l * mod_rows + grp * N_MOD + i:l * mod_rows + grp * N_MOD + i + 1, :]
                update(1, slice(l, l + 1), slice(i * D_MODEL, (i + 1) * D_MODEL), g)
            for j in range(4):
                row = DEPTH * mod_rows + 4 * l + j
                update(2 + j, slice(l, l + 1), slice(None), tot_ref[row:row + 1, :])
            head = tot_ref[head_row + l:head_row + l + 1, :]
            update(6, slice(l, l + 1), slice(None), head[:, 0:HEAD_DIM] + head[:, HEAD_DIM:2 * HEAD_DIM])
            update(7, slice(l, l + 1), slice(None), head[:, 2 * HEAD_DIM:3 * HEAD_DIM] + head[:, 3 * HEAD_DIM:4 * HEAD_DIM])
            update(8, slice(l, l + 1), slice(None), head[:, 4 * HEAD_DIM:4 * HEAD_DIM + ins[3 * 8].shape[1]])

    shapes = [jax.ShapeDtypeStruct(w.shape, F32) for w, _, _ in params for _ in range(4)]
    outs = pl.pallas_call(body, name=name, out_shape=shapes)(tot, dcc_parts, *[a for p in params for a in p])
    return [tuple(outs[4 * p:4 * p + 4]) for p in range(n_p)]


def _adamw(w, g, m, v, name):
    r, c = w.shape
    tr = _pick(r, (256, 128, 64, 32, 24, 16, 8))

    def body(w_ref, g_ref, m_ref, v_ref, d_ref, nm_ref, nv_ref):
        d_ref[...], nm_ref[...], nv_ref[...] = _adam_val(w_ref[...], g_ref[...], m_ref[...], v_ref[...])

    spec = pl.BlockSpec((tr, c), lambda i: (i, 0))
    return pl.pallas_call(body, name=name, grid=(r // tr,), in_specs=[spec] * 4, out_specs=[spec] * 3,
                          out_shape=[jax.ShapeDtypeStruct((r, c), F32)] * 3, compiler_params=_params(("parallel",)))(w, g, m, v)


def _adamw_shard(kind, l, w, m, v, halves, off, prev, name):
    h = PACK_HEIGHT[kind]
    assert off % h == 0, (kind, off)
    _, r, c = w.shape
    rows = r // 2

    def body(w_ref, m_ref, v_ref, p_ref, *rest):
        g_ref, d_ref, nm_ref, nv_ref = rest[-4:]
        if kind == "in":
            for t in range(2):
                g = p_ref[:, t * IN_PIECE_COLS:(t + 1) * IN_PIECE_COLS]
                rs = slice(t * h, (t + 1) * h)
                g_ref[rs, :] = g
                d_ref[rs, :], nm_ref[rs, :], nv_ref[rs, :] = _adam_val(w_ref[rs, :], g, m_ref[rs, :], v_ref[rs, :])
        else:
            g = p_ref[...]
            g_ref[...] = g
            d_ref[...], nm_ref[...], nv_ref[...] = _adam_val(w_ref[...], g, m_ref[...], v_ref[...])

    blk = pl.BlockSpec((None, rows, c), lambda half: (l, half, 0))
    in_specs = [blk, blk, blk, pl.BlockSpec((None, h, 1024), lambda half: (half, off // h, 0))]
    args = [w, m, v, halves]
    aliases = {}
    if prev is not None:
        in_specs += [pl.BlockSpec(memory_space=pl.ANY)] * 4
        args += list(prev)
        aliases = {4 + j: j for j in range(4)}
    return pl.pallas_call(
        body, name=name, grid=(2,), in_specs=in_specs, out_specs=[blk] * 4,
        out_shape=[jax.ShapeDtypeStruct(w.shape, F32)] * 4, input_output_aliases=aliases,
        compiler_params=_params(("parallel",)))(*args)


SMALL_ROWS = 48


def _small_rows(small, sq):
    def lane_pad(v):
        return jnp.pad(v, (0, D_MODEL - v.shape[0]))[None]

    head_rows = [lane_pad(jnp.concatenate([s["q_norm"][0], s["k_norm"][0], s["sink"]])) for s in small]
    loss_row = lane_pad((0.5 / D_MODEL) * jnp.sum(sq, keepdims=True)[0])
    rows = jnp.concatenate([s["mod"].reshape(-1, D_MODEL) for s in small] + [s["gammas"] for s in small] + head_rows + [loss_row], axis=0)
    return jnp.pad(rows, ((0, SMALL_ROWS - rows.shape[0]), (0, 0)))


def _local_step(x, ctx, target, mods, gam, qn, kn, sink, w_first, w_layers, packed, kc_idx):
    nb, seq, _ = x.shape
    rt = _Rows(nb, seq, ctx.shape[1])
    rt_lat = rt.latent_only()
    tables = _rope_tables(rt)
    fuse = packed is not None
    h = (x.reshape(rt.n_lat, D_MODEL), ctx.reshape(rt.n_ctx, D_MODEL))
    wg = [{}, {}] if fuse else [dict(w) for w in w_layers]
    wg[0]["in"] = (w_first, 0)
    if fuse:
        wg[0]["in_own"] = (packed, W_FIRST[0])
    saved = []
    for l in range(DEPTH):
        g_pre_mix, g_post_mix, g_pre_mlp, g_post_mlp = gam[l]
        if l == 0:
            u, qkv, qkvp, h = _in_fwd(rt, h, g_pre_mix, mods[l], wg[l], tables, qn[l], kn[l], f"in_fwd{l}")
        else:
            u, qkv, qkvp = _in_fwd(rt, h, g_pre_mix, mods[l], wg[l], tables, qn[l], kn[l], f"in_fwd{l}")
        if fuse and l == 0:
            o, lse_lat, w_mlp0, w_out0, w_in1 = _attn_fwd(rt, qkvp, sink[l], None, f"attn_lat_fwd{l}",
                                                         comm=_gather_comm(packed, [W_MLP0, W_OUT0, W_IN1], lead=2))
            wg[0].update({kind: (w_mlp0, PACK_OFF[(kind, 0)] - W_MLP0[0]) for kind in ("up", "down")})
            wg[0]["out"] = (w_out0, 0)
            wg[1] = {"in": (w_in1, 0)}
        elif fuse:
            o, lse_lat, w_mlp1, w_out1 = _attn_fwd(rt, qkvp, sink[l], None, f"attn_lat_fwd{l}",
                                                   comm=_gather_comm(packed, [W_MLP1, W_OUT1], lead=2))
            wg[1].update({kind: (w_mlp1, PACK_OFF[(kind, 1)] - W_MLP1[0]) for kind in ("up", "down")})
            wg[1]["out"] = (w_out1, 0)
        else:
            o, lse_lat = _attn_fwd(rt, qkvp, sink[l], None, f"attn_lat_fwd{l}")
        if l < DEPTH - 1:
            o, lse_ctx = _attn_fwd(rt, qkvp, sink[l], o, f"attn_ctx_fwd{l}")
            mix, h1, u2 = _out_fwd(rt, o, wg[l], h, mods[l], g_post_mix, g_pre_mlp, f"out_fwd{l}")
            r, y, h2 = _mlp_fwd(rt, u2, h1, wg[l], mods[l], g_post_mlp, f"mlp_fwd{l}")
        else:
            lse_ctx = None
            mix, h1, u2 = _out_fwd(rt_lat, o, wg[l], h, mods[l], g_post_mix, g_pre_mlp, f"out_fwd{l}")
            r, y, dh, sq = _mlp_fwd(rt_lat, u2, h1, wg[l], mods[l], g_post_mlp, f"mlp_fwd{l}", target=target.reshape(rt.n_lat, D_MODEL))
        saved.append((h, u, qkv, qkvp, o, lse_lat, lse_ctx, mix, h1, u2, r, y))
        h = h2

    small = [None] * DEPTH
    groups = {}
    for l in reversed(range(DEPTH)):
        g_pre_mix, g_post_mix, g_pre_mlp, g_post_mlp = gam[l]
        h0, u, qkv, qkvp, o, lse_lat, lse_ctx, mix, h1, u2, r, y = saved[l]
        mlp_group, out_group, in_group = (G_LAYER1, G_LAYER1, G_LAYER1) if l == 1 else (G_MLP0, G_OUT0, G_IN0)
        hide = fuse and l == 0

        dead_ctx = l == DEPTH - 1
        rt_b = rt_lat if dead_ctx else rt
        dy, da, d_gate_m, d_g_post_mlp = _mlp_down_bwd(rt_b, dh, y, r, wg[l], mods[l], g_post_mlp, f"mlp_down_bwd{l}")
        p_mlp = _wgrad_packed(rt_b, r, dy, "down", PACK_OFF[("down", l)] - mlp_group[0], mlp_group[1], None, f"mlp_down_wgrad{l}",
                              comm=_pair_comm(groups[G_LAYER1]) if hide else None)
        if hide:
            p_mlp, r1 = p_mlp
            sum1 = _pair_sum(groups[G_LAYER1], r1, kc_idx, "grad_pair_sum_layer1")
        p_mlp = _wgrad_packed(rt_b, u2, da, "up", PACK_OFF[("up", l)] - mlp_group[0], mlp_group[1], p_mlp, f"mlp_up_wgrad{l}")
        outs = _mlp_up_bwd(rt_b, da, wg[l], h1, dh, mods[l], g_pre_mlp, f"mlp_up_bwd{l}", comm=_pair_comm(p_mlp) if hide else None)
        dh1, d_sh_m, d_sc_m, d_g_pre_mlp = outs[:4]
        if hide:
            sum0 = _pair_sum(p_mlp, outs[4], kc_idx, "grad_pair_sum_mlp0")
        dmix, do, d_gate_a, d_g_post_mix = _out_bwd(rt_b, dh1, mix, wg[l], mods[l], g_post_mix, f"out_bwd{l}")
        p_out = _wgrad_packed(rt_b, o, dmix, "out", PACK_OFF[("out", l)] - out_group[0], out_group[1],
                              p_mlp if l == 1 else None, f"out_wgrad{l}")
        outs = _attn_bwd(rt, qkvp, o, lse_lat, do, sink[l], None, f"attn_lat_bwd{l}",
                         comm=_merge([_chip_comm([sum1[1], sum0[1]]), _pair_comm(p_out)]) if hide else None)
        dq, dkv, dkv_c, dsink1 = outs[:4]
        if hide:
            groups[G_LAYER1] = _owner_sum(sum1[0], outs[4], kc_idx, "grad_owner_sum_layer1")
            groups[G_MLP0] = _owner_sum(sum0[0], outs[5], kc_idx, "grad_owner_sum_mlp0")
            sum_out = _pair_sum(p_out, outs[6], kc_idx, "grad_pair_sum_out0")
        if dead_ctx:
            dsink2 = jnp.zeros_like(dsink1)
            d_gate_m, d_sh_m, d_sc_m, d_gate_a = [a.at[nb].set(0.0) for a in (d_gate_m, d_sh_m, d_sc_m, d_gate_a)]
        else:
            dq, dkv, dsink2 = _attn_bwd(rt, qkvp, o, lse_ctx, do, sink[l], (dq, dkv, dkv_c), f"attn_ctx_bwd{l}")
        dqkv, dh, dqn, dkn, d_sh_a, d_sc_a, d_g_pre_mix = _in_bwd(rt, dq, dkv, qkv, tables, qn[l], kn[l], wg[l], h0, dh1, mods[l],
                                                                  g_pre_mix, l == 0, f"in_bwd{l}",
                                                                  dead_ctx_dkv=dkv_c if dead_ctx else None)
        dmod = jnp.concatenate([d_sh_a, d_sc_a, d_gate_a, d_sh_m, d_sc_m, d_gate_m], axis=1)
        small[l] = dict(mod=dmod, gammas=jnp.concatenate([d_g_pre_mix, d_g_post_mix, d_g_pre_mlp, d_g_post_mlp], axis=0),
                        q_norm=dqn, k_norm=dkn, sink=(dsink1 + dsink2)[:, 0])
        tail = _merge([_gather_comm(_small_rows(small, sq), [(0, SMALL_ROWS)], lead=2),
                       _halves_comm([groups[G_LAYER1], groups[G_MLP0]]), _chip_comm([sum_out[1]])]) if hide else None
        outs = _wgrad_packed(rt, u, dqkv, "in", PACK_OFF[("in", l)] - in_group[0], in_group[1], p_out if l == 1 else None,
                             f"in_wgrad{l}", comm=tail)
        if hide:
            groups[in_group], small_g, groups[G_LAYER1], groups[G_MLP0], r2_out = outs
            groups[G_OUT0] = _owner_sum(sum_out[0], r2_out, kc_idx, "grad_owner_sum_out0")
        else:
            groups[in_group], small_g = outs, None
            if l == 0:
                groups[G_MLP0], groups[G_OUT0] = p_mlp, p_out
    return sq, dh.reshape(nb, seq, D_MODEL), [groups[g] for g in (G_LAYER1, G_MLP0, G_OUT0, G_IN0)], small, small_g


def kernel(x, c, ctx, c_ctx, w_ada, b_ada, g_pre_mix, g_post_mix, g_pre_mlp, g_post_mlp, w_in, q_norm, k_norm, sink, w_out, w_up, w_down, loss_target, m_c_ctx, m_w_ada, m_b_ada, m_g_pre_mix, m_g_post_mix, m_g_pre_mlp, m_g_post_mlp, m_w_in, m_q_norm, m_k_norm, m_sink, m_w_out, m_w_up, m_w_down, v_c_ctx, v_w_ada, v_b_ada, v_g_pre_mix, v_g_post_mix, v_g_pre_mlp, v_g_post_mlp, v_w_in, v_q_norm, v_k_norm, v_sink, v_w_out, v_w_up, v_w_down):
    nb = x.shape[0]
    ix, iy, ic = lax.axis_index("x"), lax.axis_index("y"), lax.axis_index("c")
    chip = 2 * ix + iy
    dev = 2 * chip + ic
    ada_cols = w_ada.shape[2] // 2

    c_rows = c.reshape(8, (nb * D_MODEL) // 8)
    packed, c_all = _pack_local_half(w_in, w_out, w_up, w_down, _gather_comm(c_rows, [(0, c_rows.shape[0])]), "pack_gather_c")
    c_all = c_all.reshape(N_DEV * nb, D_MODEL)
    n_cond = N_DEV * nb + 1
    cond_rows = 16 * ((n_cond + 15) // 16)
    cond = jnp.concatenate([c_all, c_ctx[None, :], jnp.zeros((cond_rows - n_cond, D_MODEL), F32)], axis=0)
    c_idx = ic.reshape(1).astype(jnp.int32)
    kc_idx = jnp.stack([chip, ic]).astype(jnp.int32)
    b_ada_half = lax.dynamic_slice_in_dim(b_ada, dev * ada_cols, ada_cols, 1)[:, None, :]
    x_ada, mod_part = _ada_fwd(cond, w_ada, b_ada_half, c_idx, "ada_fwd")
    mod_rows2d = mod_part.reshape(DEPTH * cond_rows, ada_cols)
    mod_g, w_first = _comm_alone(_merge([_gather_comm(mod_rows2d, [(0, mod_rows2d.shape[0])]),
                                         _gather_comm(packed, [W_FIRST], copy_own=False, cols=2 * IN_PIECE_COLS)]),
                               "gather_mod_w_first")
    mod_all = mod_g.reshape(N_DEV, DEPTH, cond_rows, ada_cols).transpose(1, 2, 0, 3).reshape(DEPTH, cond_rows, N_MOD * D_MODEL)
    mods = []
    for l in range(DEPTH):
        mine = lax.dynamic_slice_in_dim(mod_all[l], dev * nb, nb, 0)
        mods.append(jnp.concatenate([mine, mod_all[l, n_cond - 1:n_cond]], axis=0).reshape(nb + 1, N_MOD, D_MODEL))

    gam = [(g_pre_mix[l][None], g_post_mix[l][None], g_pre_mlp[l][None], g_post_mlp[l][None]) for l in range(DEPTH)]
    qn = [jnp.tile(q_norm[l], 2)[None] for l in range(DEPTH)]
    kn = [jnp.tile(k_norm[l], 2)[None] for l in range(DEPTH)]
    _, grad_x, (h_layer1, h_mlp0, h_out0, p_in0), _, small_g = _local_step(x, ctx, loss_target, mods, gam, qn, kn, [sink[l] for l in range(DEPTH)],
                                                                 w_first, None, packed, kc_idx)

    def step(w, g, m, v, name):
        shape = w.shape
        cols = shape[-1]
        outs = _adamw(w.reshape(-1, cols), g.reshape(-1, cols), m.reshape(-1, cols), v.reshape(-1, cols), name)
        return tuple(a.reshape(shape) for a in outs)

    def shard_update(kind, w, m, v, layer0, layer1):
        outs = None
        for l, (halves, group) in enumerate((layer0, layer1)):
            outs = _adamw_shard(kind, l, w, m, v, halves, PACK_OFF[(kind, l)] - group[0], outs, f"adamw_w_{kind}{l}")
        return tuple(outs)

    tot, r1 = _dev_sum(small_g, "small_sum", comm=_pair_comm(p_in0))
    mod_rows = (nb + 1) * N_MOD
    loss = tot[DEPTH * mod_rows + 4 * DEPTH + DEPTH, 0]

    ex = small_g[:, :DEPTH * mod_rows].reshape(N_DEV, DEPTH, nb + 1, N_MOD * D_MODEL)[:, :, :nb]
    ex = ex.transpose(1, 0, 2, 3).reshape(DEPTH, N_DEV * nb, N_MOD * D_MODEL)
    cx = tot[:DEPTH * mod_rows].reshape(DEPTH, nb + 1, N_MOD * D_MODEL)[:, nb:]
    dm = jnp.concatenate([ex, cx, jnp.zeros((DEPTH, cond_rows - n_cond, N_MOD * D_MODEL), F32)], axis=1)
    shard_cols = w_ada.shape[2]
    grad_w_ada = _ada_wgrad(x_ada, lax.dynamic_slice_in_dim(dm, chip * shard_cols, shard_cols, 2).astype(BF16), "ada_wgrad")
    dcx = jnp.pad(lax.dynamic_slice_in_dim(cx, dev * ada_cols, ada_cols, 2), ((0, 0), (0, 15), (0, 0))).astype(BF16)
    dcc = _ada_cond_bwd(dcx, w_ada, c_idx, "ada_cond_bwd")[0:8]

    a32, a16 = _pair_sum(p_in0, r1, kc_idx, "grad_pair_sum_in0")
    r2, dcc_g = _comm_alone(_merge([_chip_comm([a16]), _gather_comm(dcc, [(0, dcc.shape[0])])]), "grad_chip_exchange_in0")
    h_in0 = _owner_sum(a32, r2, kc_idx, "grad_owner_sum_in0")
    h_in0, h_out0 = _comm_alone(_halves_comm([h_in0, h_out0]), "grad_halves_exchange_mix0")

    small_names = ["c_ctx", "b_ada", "g_pre_mix", "g_post_mix", "g_pre_mlp", "g_post_mlp", "q_norm", "k_norm", "sink"]
    assert q_norm.shape[1] == HEAD_DIM and k_norm.shape[1] == HEAD_DIM
    small_res = _small_update(tot, dcc_g, [(c_ctx[None], m_c_ctx[None], v_c_ctx[None]), (b_ada, m_b_ada, v_b_ada),
                                           (g_pre_mix, m_g_pre_mix, v_g_pre_mix), (g_post_mix, m_g_post_mix, v_g_post_mix),
                                           (g_pre_mlp, m_g_pre_mlp, v_g_pre_mlp), (g_post_mlp, m_g_post_mlp, v_g_post_mlp),
                                           (q_norm, m_q_norm, v_q_norm), (k_norm, m_k_norm, v_k_norm), (sink, m_sink, v_sink)],
                              nb + 1, "small_update")
    res = {n: r for n, r in zip(small_names, small_res)}
    res["c_ctx"] = tuple(a[0] for a in res["c_ctx"])
    res["w_ada"] = (grad_w_ada, *step(w_ada, grad_w_ada, m_w_ada, v_w_ada, "adamw_w_ada"))
    res["w_up"] = shard_update("up", w_up, m_w_up, v_w_up, (h_mlp0, G_MLP0), (h_layer1, G_LAYER1))
    res["w_down"] = shard_update("down", w_down, m_w_down, v_w_down, (h_mlp0, G_MLP0), (h_layer1, G_LAYER1))
    res["w_in"] = shard_update("in", w_in, m_w_in, v_w_in, (h_in0, G_IN0), (h_layer1, G_LAYER1))
    res["w_out"] = shard_update("out", w_out, m_w_out, v_w_out, (h_out0, G_OUT0), (h_layer1, G_LAYER1))

    order = ["c_ctx", "w_ada", "b_ada", "g_pre_mix", "g_post_mix", "g_pre_mlp", "g_post_mlp", "w_in", "q_norm", "k_norm", "sink", "w_out", "w_up", "w_down"]
    return (loss, grad_x, *[res[n][0] for n in order], *[res[n][1] for n in order],
            *[res[n][2] for n in order], *[res[n][3] for n in order])
```

```python
import functools

import jax
import jax.numpy as jnp
import numpy as np
from jax import lax
from jax.experimental import pallas as pl
from jax.experimental.pallas import tpu as pltpu

F32 = jnp.float32
BF16 = jnp.bfloat16

D_MODEL = 1024
HEAD_DIM = 64
GROUP = 4
WINDOW = 128
N_MOD = 6
D_FF = 4 * D_MODEL
IN_COLS = 1536
GRID_W = 64
ROPE_THETA = 10000.0
EPS = 1e-6
NEG_BIG = -1e30
Q_SCALE = HEAD_DIM ** -0.5
DEPTH = 2
N_DEV = 8

ADAM_LR = 0.001
ADAM_B1 = 0.9
ADAM_B2 = 0.999
ADAM_EPS = 1e-08
ADAM_WD = 0.01
ADAM_STEP = 10

V7X_VMEM_BYTES = 64 * 1024 * 1024
VMEM_LIMIT = V7X_VMEM_BYTES - 8 * 1024 * 1024

MESH = pl.DeviceIdType.MESH
NT = (((1,), (1,)), ((), ()))
TN = (((0,), (0,)), ((), ()))

COL_KA, COL_VA, COL_KB, COL_VB = 4, 5, 10, 11
NORMED_COLS = 640

PACK_HEIGHT = {"up": 512, "down": 512, "in": 256, "out": 128}
IN_PIECE_COLS = 384
PACK_OFF = {("up", 0): 0, ("down", 0): 512, ("in", 0): 1024, ("out", 0): 1280,
            ("up", 1): 1408, ("down", 1): 1920, ("in", 1): 2432, ("out", 1): 2688}
PACK_ROWS = 2816
W_FIRST, W_MLP0, W_OUT0, W_IN1, W_MLP1, W_OUT1 = (1024, 256), (0, 1024), (1280, 128), (2432, 256), (1408, 1024), (2688, 128)
G_LAYER1, G_MLP0, G_OUT0, G_IN0 = (1408, 1408), (0, 1024), (1280, 128), (1024, 256)


def _pick(n, cands):
    for t in cands:
        if n % t == 0:
            return t
    raise ValueError(f"no tile for {n}")


def _params(sem):
    return pltpu.CompilerParams(dimension_semantics=sem, vmem_limit_bytes=VMEM_LIMIT)


class _Comm:
    def __init__(self, inputs, out_shapes, aliases, n_send, n_recv, start, finish, relay=None, lead=0):
        self.inputs, self.out_shapes, self.aliases = list(inputs), list(out_shapes), dict(aliases)
        self.n_send, self.n_recv, self.start, self.finish, self.relay, self.lead = n_send, n_recv, start, finish, relay, lead


def _comm_call(compute, comm, *, name, grid, in_specs, out_specs, out_shape, args, aliases, semantics, scratch=()):
    in_specs, out_specs, out_shape, args, aliases = list(in_specs), list(out_specs), list(out_shape), list(args), dict(aliases)
    scratch = list(scratch)
    if comm is None:
        return pl.pallas_call(compute, name=name, grid=grid, in_specs=in_specs, out_specs=out_specs, out_shape=out_shape,
                              input_output_aliases=aliases, scratch_shapes=scratch, compiler_params=_params(semantics))(*args)
    n_in, n_out, n_ci, n_co = len(args), len(out_shape), len(comm.inputs), len(comm.out_shapes)
    hbm = pl.BlockSpec(memory_space=pl.ANY)
    aliases.update({n_in + i: n_out + o for i, o in comm.aliases.items()})

    def body(*refs):
        ins, c_ins = refs[:n_in], refs[n_in:n_in + n_ci]
        outs, c_outs = refs[n_in + n_ci:n_in + n_ci + n_out], refs[n_in + n_ci + n_out:n_in + n_ci + n_out + n_co]
        scr = refs[n_in + n_ci + n_out + n_co:-2]
        send_sems, recv_sems = refs[-2:]
        ids = [pl.program_id(a) for a in range(len(grid))]
        first = functools.reduce(jnp.logical_and, [i == 0 for i in ids])
        last = functools.reduce(jnp.logical_and, [i == g - 1 for i, g in zip(ids, grid)])

        @pl.when(first)
        def _():
            comm.start(c_ins, c_outs, send_sems, recv_sems)

        compute(*ins, *outs, *scr)

        if comm.relay is not None:
            step = functools.reduce(lambda acc, ig: acc * ig[1] + ig[0], zip(ids, grid), 0)

            @pl.when(step == int(np.prod(grid)) - 1 - comm.lead)
            def _():
                comm.relay(c_ins, c_outs, send_sems, recv_sems)

        @pl.when(last)
        def _():
            comm.finish(c_ins, c_outs, send_sems, recv_sems)

    return pl.pallas_call(
        body, name=name, grid=grid,
        in_specs=in_specs + [hbm] * n_ci, out_specs=out_specs + [hbm] * n_co, out_shape=out_shape + comm.out_shapes,
        input_output_aliases=aliases,
        scratch_shapes=scratch + [pltpu.SemaphoreType.DMA((comm.n_send,)), pltpu.SemaphoreType.DMA((comm.n_recv,))],
        compiler_params=_params(("arbitrary",) * len(grid)),
    )(*args, *comm.inputs)


def _place():
    x_, y_, c_ = lax.axis_index("x"), lax.axis_index("y"), lax.axis_index("c")
    return x_, y_, c_, [(1 - x_, y_), (x_, 1 - y_), (1 - x_, 1 - y_)]


GATHER_SENDS, GATHER_RECVS = 8, 7


def _gather_copies(packed_ref, wg_ref, send_sems, recv_sems, rows, nth=0):
    r0, n = rows
    x_, y_, c_, chips = _place()
    me, sibling = (x_, y_, c_), (x_, y_, 1 - c_)
    src = packed_ref.at[pl.ds(r0, n), pl.ds(0, wg_ref.shape[2])]

    def slot(px, py, pc):
        return wg_ref.at[4 * px + 2 * py + pc]

    def copy(k, block, to, from_packed=False):
        return pltpu.make_async_remote_copy(src_ref=src if from_packed else slot(*block), dst_ref=slot(*block),
                                            send_sem=send_sems.at[GATHER_SENDS * nth + k], recv_sem=recv_sems.at[GATHER_RECVS * nth + k],
                                            device_id=to, device_id_type=MESH)

    own = [copy(0, me, sibling, True)] + [copy(1 + j, me, (*chip, c_), True) for j, chip in enumerate(chips)]
    passed = [copy(4 + j, (*chip, c_), sibling) for j, chip in enumerate(chips)]
    over_ici = [copy(1 + j, (*chip, c_), me) for j, chip in enumerate(chips)]
    from_sibling = [copy(0, sibling, me)] + [copy(4 + j, (*chip, 1 - c_), me) for j, chip in enumerate(chips)]
    mine = pltpu.make_async_copy(src, slot(*me), send_sems.at[GATHER_SENDS * nth + 7])
    return mine, own, passed, over_ici, from_sibling


def _gather_start(packed_ref, wg_ref, send_sems, recv_sems, rows, nth=0, copy_own=True):
    mine, own, _, _, _ = _gather_copies(packed_ref, wg_ref, send_sems, recv_sems, rows, nth)
    if copy_own:
        mine.start()
    for cp in own:
        cp.start()


def _gather_relay(packed_ref, wg_ref, send_sems, recv_sems, rows, nth=0):
    _, _, passed, over_ici, _ = _gather_copies(packed_ref, wg_ref, send_sems, recv_sems, rows, nth)
    for arrived, onward in zip(over_ici, passed):
        arrived.wait_recv()
        onward.start()


def _gather_finish(packed_ref, wg_ref, send_sems, recv_sems, rows, nth=0, copy_own=True):
    mine, own, passed, _, from_sibling = _gather_copies(packed_ref, wg_ref, send_sems, recv_sems, rows, nth)
    for arrived in from_sibling:
        arrived.wait_recv()
    for cp in own + passed:
        cp.wait_send()
    if copy_own:
        mine.wait()


def _gather_comm(packed, ranges, copy_own=True, lead=0, cols=None):
    shapes = [jax.ShapeDtypeStruct((N_DEV, n, cols or packed.shape[1]), packed.dtype) for _, n in ranges]

    def start(ins, outs, ss, rs):
        for nth, rows in enumerate(ranges):
            _gather_start(ins[0], outs[nth], ss, rs, rows, nth, copy_own)

    def relay(ins, outs, ss, rs):
        for nth, rows in enumerate(ranges):
            _gather_relay(ins[0], outs[nth], ss, rs, rows, nth)

    def finish(ins, outs, ss, rs):
        for nth, rows in enumerate(ranges):
            _gather_finish(ins[0], outs[nth], ss, rs, rows, nth, copy_own)

    return _Comm([packed], shapes, {}, GATHER_SENDS * len(ranges), GATHER_RECVS * len(ranges), start, finish, relay, lead)


def _pair_copy(p_ref, out_ref, send_sems, recv_sems):
    x_, y_, c_, _ = _place()
    return pltpu.make_async_remote_copy(src_ref=p_ref.at[1 - c_], dst_ref=out_ref,
                                        send_sem=send_sems.at[0], recv_sem=recv_sems.at[0],
                                        device_id=(x_, y_, 1 - c_), device_id_type=MESH)


def _pair_comm(p):
    return _Comm([p], [jax.ShapeDtypeStruct(p.shape[1:], p.dtype)], {}, 1, 1,
                 lambda ins, outs, ss, rs: _pair_copy(ins[0], outs[0], ss, rs).start(),
                 lambda ins, outs, ss, rs: _pair_copy(ins[0], outs[0], ss, rs).wait())


def _chip_copies(a_refs, out_refs, send_sems, recv_sems):
    _, _, c_, chips = _place()
    return [pltpu.make_async_remote_copy(src_ref=a_ref.at[2 * tx + ty], dst_ref=o_ref.at[j],
                                         send_sem=send_sems.at[3 * g + j], recv_sem=recv_sems.at[3 * g + j],
                                         device_id=(tx, ty, c_), device_id_type=MESH)
            for g, (a_ref, o_ref) in enumerate(zip(a_refs, out_refs)) for j, (tx, ty) in enumerate(chips)]


def _chip_start(a_refs, out_refs, send_sems, recv_sems):
    for cp in _chip_copies(a_refs, out_refs, send_sems, recv_sems):
        cp.start()


def _chip_finish(a_refs, out_refs, send_sems, recv_sems):
    for cp in _chip_copies(a_refs, out_refs, send_sems, recv_sems):
        cp.wait()


def _chip_comm(arrays):
    shapes = [jax.ShapeDtypeStruct((3,) + a.shape[1:], a.dtype) for a in arrays]
    return _Comm(arrays, shapes, {}, 3 * len(arrays), 3 * len(arrays), _chip_start, _chip_finish)


def _halves_copies(in_refs, out_refs, send_sems, recv_sems):
    x_, y_, c_, _ = _place()
    return [pltpu.make_async_remote_copy(src_ref=o_ref.at[c_], dst_ref=o_ref.at[c_], send_sem=send_sems.at[i], recv_sem=recv_sems.at[i],
                                         device_id=(x_, y_, 1 - c_), device_id_type=MESH)
            for i, o_ref in enumerate(out_refs)]


def _halves_start(in_refs, out_refs, send_sems, recv_sems):
    for cp in _halves_copies(in_refs, out_refs, send_sems, recv_sems):
        cp.start()


def _halves_finish(in_refs, out_refs, send_sems, recv_sems):
    for cp in _halves_copies(in_refs, out_refs, send_sems, recv_sems):
        cp.wait()


def _halves_comm(arrays):
    shapes = [jax.ShapeDtypeStruct(a.shape, a.dtype) for a in arrays]
    return _Comm(arrays, shapes, {i: i for i in range(len(arrays))}, len(arrays), len(arrays), _halves_start, _halves_finish)


class _SemSlice:
    class _At:
        def __init__(self, sems, first):
            self.sems, self.first = sems, first

        def __getitem__(self, k):
            return self.sems.at[self.first + k]

    def __init__(self, sems, first):
        self.at = _SemSlice._At(sems, first)


def _merge(comms):
    inputs = [a for c in comms for a in c.inputs]
    shapes = [s for c in comms for s in c.out_shapes]
    aliases, spans = {}, []
    i0 = o0 = s0 = r0 = 0
    for c in comms:
        aliases.update({i0 + i: o0 + o for i, o in c.aliases.items()})
        spans.append((slice(i0, i0 + len(c.inputs)), slice(o0, o0 + len(c.out_shapes)), s0, r0))
        i0, o0, s0, r0 = i0 + len(c.inputs), o0 + len(c.out_shapes), s0 + c.n_send, r0 + c.n_recv

    def start(ins, outs, ss, rs):
        for c, (i, o, s, r) in zip(comms, spans):
            c.start(ins[i], outs[o], _SemSlice(ss, s), _SemSlice(rs, r))

    def relay(ins, outs, ss, rs):
        for c, (i, o, s, r) in zip(comms, spans):
            if c.relay is not None:
                c.relay(ins[i], outs[o], _SemSlice(ss, s), _SemSlice(rs, r))

    def finish(ins, outs, ss, rs):
        for c, (i, o, s, r) in zip(comms, spans):
            c.finish(ins[i], outs[o], _SemSlice(ss, s), _SemSlice(rs, r))

    leads = [c.lead for c in comms if c.relay is not None]
    return _Comm(inputs, shapes, aliases, s0, r0, start, finish, relay if leads else None, max(leads, default=0))


def _comm_alone(comm, name):
    n_ci = len(comm.inputs)
    hbm = pl.BlockSpec(memory_space=pl.ANY)

    def body(*refs):
        c_ins, c_outs, send_sems, recv_sems = refs[:n_ci], refs[n_ci:-2], refs[-2], refs[-1]
        comm.start(c_ins, c_outs, send_sems, recv_sems)
        if comm.relay is not None:
            comm.relay(c_ins, c_outs, send_sems, recv_sems)
        comm.finish(c_ins, c_outs, send_sems, recv_sems)

    return pl.pallas_call(
        body, name=name, out_shape=comm.out_shapes, in_specs=[hbm] * n_ci, out_specs=[hbm] * len(comm.out_shapes),
        input_output_aliases=comm.aliases,
        scratch_shapes=[pltpu.SemaphoreType.DMA((comm.n_send,)), pltpu.SemaphoreType.DMA((comm.n_recv,))],
    )(*comm.inputs)


SUM_TILES = (704, 512, 384, 320, 256, 192, 128, 64)


def _pair_sum(p, r1, kc_idx, name):
    _, _, n, c = p.shape
    tr = _pick(n, SUM_TILES)

    def body(s_ref, p_ref, r_ref, o32_ref, o16_ref):
        v = p_ref[...] + r_ref[...]
        o16_ref[...] = v.astype(BF16)

        @pl.when(pl.program_id(1) == s_ref[0])
        def _():
            o32_ref[...] = v

    blk = pl.BlockSpec((None, tr, c), lambda i, j, s: (j, i, 0))
    grid_spec = pltpu.PrefetchScalarGridSpec(
        num_scalar_prefetch=1, grid=(n // tr, 4),
        in_specs=[pl.BlockSpec((None, None, tr, c), lambda i, j, s: (s[1], j, i, 0)), blk],
        out_specs=[pl.BlockSpec((tr, c), lambda i, j, s: (i, 0)), blk])
    return pl.pallas_call(
        body, name=name, grid_spec=grid_spec,
        out_shape=[jax.ShapeDtypeStruct((n, c), F32), jax.ShapeDtypeStruct((4, n, c), BF16)],
        compiler_params=_params(("arbitrary", "arbitrary")),
    )(kc_idx, p, r1)


def _owner_sum(a32, r2, kc_idx, name):
    r, c = a32.shape
    tr = _pick(r, SUM_TILES)

    def body(s_ref, a_ref, r_ref, o_ref):
        v = a_ref[...]
        for j in range(3):
            v = v + r_ref[j].astype(F32)
        o_ref[...] = v

    grid_spec = pltpu.PrefetchScalarGridSpec(
        num_scalar_prefetch=1, grid=(r // tr,),
        in_specs=[pl.BlockSpec((tr, c), lambda i, s: (i, 0)),
                  pl.BlockSpec((3, tr, c), lambda i, s: (0, i, 0))],
        out_specs=pl.BlockSpec((None, tr, c), lambda i, s: (s[1], i, 0)))
    return pl.pallas_call(
        body, name=name, grid_spec=grid_spec,
        out_shape=jax.ShapeDtypeStruct((2, r, c), F32),
        compiler_params=_params(("arbitrary",)),
    )(kc_idx, a32, r2)


def _pack_local_half(w_in_s, w_out_s, w_up_s, w_down_s, comm, name):
    shards = {"in": w_in_s, "out": w_out_s, "up": w_up_s, "down": w_down_s}
    kinds = list(shards)
    assert sorted(off + PACK_HEIGHT[kind] for (kind, _), off in PACK_OFF.items()) == sorted(PACK_OFF.values())[1:] + [PACK_ROWS]
    for kind in kinds:
        assert shards[kind].shape[1] == (4 if kind == "in" else 2) * PACK_HEIGHT[kind], (kind, shards[kind].shape)

    def body(*refs):
        w_refs, p_ref = dict(zip(kinds, refs[:4])), refs[4]
        scr, sems = dict(zip(kinds, refs[5:9])), refs[9]
        c = lax.axis_index("c")
        copies = {}
        for n, (kind, l) in enumerate(sorted(PACK_OFF)):
            rows = scr[kind].shape[1]
            copies[(kind, l)] = pltpu.make_async_copy(w_refs[kind].at[l, pl.ds(c * rows, rows)], scr[kind].at[l], sems.at[n])
            copies[(kind, l)].start()
        for (kind, l), off in sorted(PACK_OFF.items(), key=lambda kv: kv[1]):
            copies[(kind, l)].wait()
            h = PACK_HEIGHT[kind]
            if kind == "in":
                for t in range(2):
                    p_ref[off:off + h, t * IN_PIECE_COLS:(t + 1) * IN_PIECE_COLS] = scr[kind][l, t * h:(t + 1) * h, :].astype(BF16)
                p_ref[off:off + h, 2 * IN_PIECE_COLS:] = jnp.zeros((h, 1024 - 2 * IN_PIECE_COLS), BF16)
            else:
                p_ref[off:off + h, :] = scr[kind][l].astype(BF16)

    hbm = pl.BlockSpec(memory_space=pl.ANY)
    scratch = [pltpu.VMEM((DEPTH, shards[kind].shape[1] // 2, shards[kind].shape[2]), F32) for kind in kinds]
    outs = _comm_call(
        body, comm, name=name, grid=(1,), in_specs=[hbm] * 4,
        out_specs=[pl.BlockSpec((PACK_ROWS, 1024), lambda i: (0, 0))],
        out_shape=[jax.ShapeDtypeStruct((PACK_ROWS, 1024), BF16)],
        args=[shards[kind] for kind in kinds], aliases={}, semantics=("arbitrary",),
        scratch=scratch + [pltpu.SemaphoreType.DMA((len(PACK_OFF),))])
    return outs


def _unpack_in_pieces(w_ref, own_ref, w_scr):
    if own_ref is not None:
        me = 4 * lax.axis_index("x") + 2 * lax.axis_index("y") + lax.axis_index("c")
    for d in range(N_DEV):
        k, c = d // 2, d % 2
        for t in range(2):
            piece = w_ref[d, :, t * IN_PIECE_COLS:(t + 1) * IN_PIECE_COLS]
            if own_ref is not None:
                piece = jnp.where(me == d, own_ref[:, t * IN_PIECE_COLS:(t + 1) * IN_PIECE_COLS], piece)
            w_scr[c * 512 + t * 256:c * 512 + (t + 1) * 256, k * IN_PIECE_COLS:(k + 1) * IN_PIECE_COLS] = piece


def _in_weight_operands(wg):
    specs, args = [_gathered_spec(wg, "in")], [wg["in"][0]]
    if "in_own" in wg:
        own, off = wg["in_own"]
        h = PACK_HEIGHT["in"]
        assert off % h == 0
        specs.append(pl.BlockSpec((h, 1024), lambda *_: (off // h, 0), pipeline_mode=pl.Buffered(1)))
        args.append(own)
    return specs, args


class _Rows:
    def __init__(self, nb, seq, ctx):
        self.nb, self.seq, self.ctx = nb, seq, ctx
        self.n_lat, self.n_ctx = nb * seq, nb * ctx
        self.rows = self.n_lat + self.n_ctx
        self.tm = _pick(np.gcd(seq, self.n_ctx), (512, 256, 128))
        self.tiles_per_ex = seq // self.tm
        self.n_tiles = self.rows // self.tm
        self.n_lat_tiles = self.n_lat // self.tm
        self.groups = nb + 1

    def latent_only(self):
        rt = _Rows(self.nb, self.seq, self.ctx)
        rt.n_tiles = self.n_lat_tiles
        return rt

    def group(self, i):
        return jnp.minimum(i // self.tiles_per_ex, self.nb)

    def first_of_group(self, i):
        return jnp.logical_and(i % self.tiles_per_ex == 0, i <= self.n_lat_tiles)


def _mod_spec(rt):
    return pl.BlockSpec((1, N_MOD, D_MODEL), lambda i: (rt.group(i), 0, 0))


def _row_spec(rt, cols):
    return pl.BlockSpec((rt.tm, cols), lambda i: (i, 0))


def _vec_spec(cols):
    return pl.BlockSpec((1, cols), lambda i: (0, 0))


def _group_spec(rt):
    return pl.BlockSpec((1, 1, D_MODEL), lambda i: (rt.group(i), 0, 0))


def _gathered_spec(wg, kind):
    h, off = PACK_HEIGHT[kind], wg[kind][1]
    assert off % h == 0, (kind, off)
    return pl.BlockSpec((N_DEV, h, wg[kind][0].shape[2]), lambda *_: (0, off // h, 0), pipeline_mode=pl.Buffered(1))


def _group_shape(rt):
    return jax.ShapeDtypeStruct((rt.groups, 1, D_MODEL), F32)


def _vec_shape(cols=D_MODEL):
    return jax.ShapeDtypeStruct((1, cols), F32)


def _rms_inv(v):
    return lax.rsqrt(jnp.mean(v * v, axis=-1, keepdims=True) + EPS)


def _norm_mod_val(h_, g_, mod_ref, i_shift, i_scale):
    n = h_ * _rms_inv(h_) * g_
    return n * (1.0 + mod_ref[0, i_scale:i_scale + 1, :]) + mod_ref[0, i_shift:i_shift + 1, :]


def _post_norm_val(h_, z_, g_, mod_ref, i_gate):
    return h_ + mod_ref[0, i_gate:i_gate + 1, :] * (z_ * _rms_inv(z_) * g_)


def _post_norm_bwd_val(dh_, z_, g_, gate):
    rinv = _rms_inv(z_)
    n0 = z_ * rinv
    dn = dh_ * gate * g_
    dz = rinv * (dn - n0 * jnp.mean(dn * n0, axis=-1, keepdims=True))
    return dz, jnp.sum(dh_ * n0 * g_, axis=0, keepdims=True), jnp.sum(dh_ * gate * n0, axis=0, keepdims=True)


def _norm_mod_bwd_val(du_, h_, g_, one_sc):
    rinv = _rms_inv(h_)
    n0 = h_ * rinv
    dn = du_ * g_ * one_sc
    dh = rinv * (dn - n0 * jnp.mean(dn * n0, axis=-1, keepdims=True))
    return (dh, jnp.sum(du_, axis=0, keepdims=True), jnp.sum(du_ * n0 * g_, axis=0, keepdims=True),
            jnp.sum(du_ * one_sc * n0, axis=0, keepdims=True))


def _accumulate(rt, i, group_pairs, global_pairs):
    @pl.when(rt.first_of_group(i))
    def _():
        for ref, _ in group_pairs:
            ref[...] = jnp.zeros_like(ref)

    @pl.when(i == 0)
    def _():
        for ref, _ in global_pairs:
            ref[...] = jnp.zeros_like(ref)

    for ref, val in group_pairs:
        ref[0] += val
    for ref, val in global_pairs:
        ref[...] += val


def _rope_tables(rt):
    pos = np.arange(rt.seq)
    axis_dim = HEAD_DIM // 2
    inv = (ROPE_THETA ** (-np.arange(0, axis_dim, 2, dtype=np.float32) / axis_dim)).astype(np.float32)
    ang_r = (pos // GRID_W).astype(np.float32)[:, None] * inv[None, :]
    ang_c = (pos % GRID_W).astype(np.float32)[:, None] * inv[None, :]
    cr, sr, cc, sc = np.cos(ang_r), np.sin(ang_r), np.cos(ang_c), np.sin(ang_c)
    zero = np.zeros_like(sr)
    cos = np.concatenate([cr, cr, cc, cc], axis=1)
    s_lo = np.concatenate([zero, sr, zero, sc], axis=1)
    s_hi = np.concatenate([-sr, zero, -sc, zero], axis=1)

    def full(t, ctx_value):
        return jnp.asarray(np.concatenate([np.tile(t, (1, 2)), np.full((rt.tm, 128), ctx_value)], axis=0), F32)

    return full(cos, 1.0), full(s_lo, 0.0), full(s_hi, 0.0)


def _table_spec(rt):
    return pl.BlockSpec((rt.tm, 128), lambda i: (jnp.where(i < rt.n_lat_tiles, i % rt.tiles_per_ex, rt.tiles_per_ex), 0))


def _head_mean(x):
    r = lax.broadcasted_iota(jnp.int32, (128, 128), 0) // HEAD_DIM
    c = lax.broadcasted_iota(jnp.int32, (128, 128), 1) // HEAD_DIM
    ones = jnp.where(r == c, 1.0 / HEAD_DIM, 0.0).astype(F32)
    return jnp.dot(x, ones, preferred_element_type=F32, precision=lax.Precision.HIGH)


def _head_stats(t):
    return lax.rsqrt(_head_mean(t * t) + EPS)


def _prep_fwd_body(tm, qkv_ref, c, s1, s2, qn, kn, out_ref):
    def rope(t):
        return t * c + pltpu.roll(t, 16, 1) * s1 + pltpu.roll(t, 112, 1) * s2

    for j in range(12):
        t = qkv_ref[:, j * 128:(j + 1) * 128]
        if j < 4:
            t = rope(t * _head_stats(t) * qn) * Q_SCALE
        elif j == COL_KA:
            t = rope(t * _head_stats(t) * kn)
        elif 6 <= j < 10:
            t = rope(t) * Q_SCALE
        elif j == COL_KB:
            t = rope(t)
        out_ref[:, j * 128:(j + 1) * 128] = t.astype(BF16)


def _prep_bwd_body(dq, dkv, qkv_ref, c, s1, s2, qn, kn, out_ref):
    rows = slice(None)

    def rope_bwd(d):
        return d * c + pltpu.roll(d * s1, 112, 1) + pltpu.roll(d * s2, 16, 1)

    def norm_bwd(t, g, dy):
        rinv = _head_stats(t)
        n = t * rinv
        dn = dy * g
        return rinv * (dn - n * _head_mean(dn * n)), jnp.sum(dy * n, axis=0, keepdims=True)

    dqn = jnp.zeros((1, 128), F32)
    dkn = jnp.zeros((1, 128), F32)
    for j in range(12):
        if j < 4:
            d, dg = norm_bwd(qkv_ref[rows, j * 128:(j + 1) * 128], qn, rope_bwd(dq(slice(j * 128, (j + 1) * 128)) * Q_SCALE))
            dqn = dqn + dg
        elif j == COL_KA:
            d, dg = norm_bwd(qkv_ref[rows, j * 128:(j + 1) * 128], kn, rope_bwd(dkv(slice(0, 128))))
            dkn = dkn + dg
        elif j == COL_VA:
            d = dkv(slice(128, 256))
        elif j < 10:
            d = rope_bwd(dq(slice((j - 2) * 128, (j - 1) * 128)) * Q_SCALE)
        elif j == COL_KB:
            d = rope_bwd(dkv(slice(256, 384)))
        else:
            d = dkv(slice(384, 512))
        out_ref[rows, j * 128:(j + 1) * 128] = d.astype(BF16)
    return dqn, dkn


def _in_fwd(rt, h, gamma, mod, wg, tables, qn, kn, name):
    w_specs, w_args = _in_weight_operands(wg)
    n_w = len(w_args)
    joined = not isinstance(h, (tuple, list))
    n_h = 1 if joined else 2

    def body(*refs):
        g_ref, mod_ref = refs[n_h:n_h + 2]
        rest = refs[n_h + 2:]
        c_ref, s1_ref, s2_ref, qn_ref, kn_ref, u_ref, qkn_ref, qkvp_ref = rest[n_w:n_w + 8]
        qkv_ref, w_scr = rest[-2:]
        i = pl.program_id(0)

        @pl.when(i == 0)
        def _():
            _unpack_in_pieces(rest[0], rest[1] if n_w == 2 else None, w_scr)

        if joined:
            h_ = refs[0][...]
        else:
            h_ = jnp.where(i < rt.n_lat_tiles, refs[0][...], refs[1][...])
            rest[n_w + 8][...] = h_
        u = _norm_mod_val(h_, g_ref[...], mod_ref, 0, 1).astype(BF16)
        u_ref[...] = u
        qkv_ref[...] = jnp.dot(u, w_scr[...], preferred_element_type=F32)
        qkn_ref[...] = qkv_ref[:, 0:NORMED_COLS]
        _prep_fwd_body(rt.tm, qkv_ref, c_ref[...], s1_ref[...], s2_ref[...], qn_ref[...], kn_ref[...], qkvp_ref)

    if joined:
        h_specs, h_args = [_row_spec(rt, D_MODEL)], [h]
    else:
        h_specs = [pl.BlockSpec((rt.tm, D_MODEL), lambda i: (jnp.minimum(i, rt.n_lat_tiles - 1), 0)),
                   pl.BlockSpec((rt.tm, D_MODEL), lambda i: (jnp.maximum(i - rt.n_lat_tiles, 0), 0))]
        h_args = list(h)
    out_specs = [_row_spec(rt, D_MODEL), _row_spec(rt, NORMED_COLS), _row_spec(rt, IN_COLS)]
    out_shape = [jax.ShapeDtypeStruct((rt.rows, D_MODEL), BF16), jax.ShapeDtypeStruct((rt.rows, NORMED_COLS), F32),
                 jax.ShapeDtypeStruct((rt.rows, IN_COLS), BF16)]
    if not joined:
        out_specs.append(_row_spec(rt, D_MODEL))
        out_shape.append(jax.ShapeDtypeStruct((rt.rows, D_MODEL), F32))
    return pl.pallas_call(
        body, name=name, grid=(rt.n_tiles,),
        in_specs=h_specs + [_vec_spec(D_MODEL), _mod_spec(rt)] + w_specs + [_table_spec(rt)] * 3 + [_vec_spec(128)] * 2,
        out_specs=out_specs, out_shape=out_shape,
        scratch_shapes=[pltpu.VMEM((rt.tm, IN_COLS), F32), pltpu.VMEM((D_MODEL, IN_COLS), BF16)],
        compiler_params=_params(("arbitrary",)),
    )(*h_args, gamma, mod, *w_args, *tables, qn, kn)


def _in_bwd(rt, dq, dkv, qkv, tables, qn, kn, wg, h, dres, mod, gamma, latent_only, name, comm=None, dead_ctx_dkv=None):
    last = rt.n_lat_tiles - 1
    w_specs, w_args = _in_weight_operands(wg)
    n_w = len(w_args)
    n_dead = 0 if dead_ctx_dkv is None else 1

    def body(dq_ref, dkv_ref, qkv_ref, c_ref, s1_ref, s2_ref, qn_ref, kn_ref, *rest):
        h_ref, dres_ref, mod_ref, g_ref, dqkv_ref, dh_ref, dqn_ref, dkn_ref, dsh_ref, dsc_ref, dg_ref, w_scr = rest[n_w + n_dead:]
        i = pl.program_id(0)

        @pl.when(i == 0)
        def _():
            _unpack_in_pieces(rest[0], rest[1] if n_w == 2 else None, w_scr)

        if n_dead:
            c1_ref, lat = rest[n_w], i <= last
            load_dq = lambda cols: jnp.where(lat, dq_ref[:, cols], 0.0)
            load_dkv = lambda cols: jnp.where(lat, dkv_ref[:, cols], c1_ref[:, cols])
            dres_ = jnp.where(lat, dres_ref[...], 0.0)
        else:
            load_dq, load_dkv, dres_ = (lambda cols: dq_ref[:, cols]), (lambda cols: dkv_ref[:, cols]), dres_ref[...]
        dqn, dkn = _prep_bwd_body(load_dq, load_dkv, qkv_ref, c_ref[...], s1_ref[...], s2_ref[...], qn_ref[...], kn_ref[...], dqkv_ref)
        du = lax.dot_general(dqkv_ref[...], w_scr[...], NT, preferred_element_type=F32)
        dh, dsh, dsc, dg = _norm_mod_bwd_val(du, h_ref[...], g_ref[...], 1.0 + mod_ref[0, 1:2, :])
        if latent_only:
            @pl.when(i <= last)
            def _():
                dh_ref[...] = dres_ + dh
        else:
            dh_ref[...] = dres_ + dh
        _accumulate(rt, i, [(dsh_ref, dsh), (dsc_ref, dsc)], [(dg_ref, dg), (dqn_ref, dqn), (dkn_ref, dkn)])

    dh_spec = pl.BlockSpec((rt.tm, D_MODEL), lambda i: (jnp.minimum(i, last), 0)) if latent_only else _row_spec(rt, D_MODEL)
    dead_specs = [] if dead_ctx_dkv is None else [pl.BlockSpec((rt.tm, 512), lambda i: (jnp.maximum(i - rt.n_lat_tiles, 0), 0))]
    dead_args = [] if dead_ctx_dkv is None else [dead_ctx_dkv]
    return _comm_call(
        body, comm, name=name, grid=(rt.n_tiles,),
        in_specs=[_row_spec(rt, 1024), _row_spec(rt, 512), _row_spec(rt, NORMED_COLS)] + [_table_spec(rt)] * 3 + [_vec_spec(128)] * 2
        + w_specs + dead_specs + [_row_spec(rt, D_MODEL), _row_spec(rt, D_MODEL), _mod_spec(rt), _vec_spec(D_MODEL)],
        out_specs=[_row_spec(rt, IN_COLS), dh_spec, _vec_spec(128), _vec_spec(128),
                   _group_spec(rt), _group_spec(rt), _vec_spec(D_MODEL)],
        out_shape=[jax.ShapeDtypeStruct((rt.rows, IN_COLS), BF16),
                   jax.ShapeDtypeStruct((rt.n_lat if latent_only else rt.rows, D_MODEL), F32),
                   _vec_shape(128), _vec_shape(128), _group_shape(rt), _group_shape(rt), _vec_shape()],
        args=[dq, dkv, qkv, *tables, qn, kn, *w_args, *dead_args, h, dres, mod, gamma], aliases={}, semantics=("arbitrary",),
        scratch=[pltpu.VMEM((D_MODEL, IN_COLS), BF16)])


def _out_fwd(rt, o, wg, h, mod, g_post_mix, g_pre_mlp, name):
    def body(o_ref, w_ref, h_ref, mod_ref, gpost_ref, gpre_ref, mix_ref, h1_ref, u2_ref):
        mix = jnp.dot(o_ref[...], w_ref[...].reshape(D_MODEL, D_MODEL), preferred_element_type=F32)
        mix_ref[...] = mix
        h1 = _post_norm_val(h_ref[...], mix, gpost_ref[...], mod_ref, 2)
        h1_ref[...] = h1
        u2_ref[...] = _norm_mod_val(h1, gpre_ref[...], mod_ref, 3, 4).astype(BF16)

    return pl.pallas_call(
        body, name=name, grid=(rt.n_tiles,),
        in_specs=[_row_spec(rt, D_MODEL), _gathered_spec(wg, "out"), _row_spec(rt, D_MODEL), _mod_spec(rt),
                  _vec_spec(D_MODEL), _vec_spec(D_MODEL)],
        out_specs=[_row_spec(rt, D_MODEL)] * 3,
        out_shape=[jax.ShapeDtypeStruct((rt.rows, D_MODEL), F32), jax.ShapeDtypeStruct((rt.rows, D_MODEL), F32),
                   jax.ShapeDtypeStruct((rt.rows, D_MODEL), BF16)],
        compiler_params=_params(("parallel",)),
    )(o, wg["out"][0], h, mod, g_post_mix, g_pre_mlp)


def _out_bwd(rt, dh1, mix, wg, mod, g_post_mix, name):
    def body(dh_ref, mix_ref, w_ref, mod_ref, g_ref, dmix_ref, do_ref, dgate_ref, dg_ref):
        i = pl.program_id(0)
        dz, dgate, dg = _post_norm_bwd_val(dh_ref[...], mix_ref[...], g_ref[...], mod_ref[0, 2:3, :])
        dzb = dz.astype(BF16)
        dmix_ref[...] = dzb
        do_ref[...] = lax.dot_general(dzb, w_ref[...].reshape(D_MODEL, D_MODEL), NT, preferred_element_type=F32).astype(BF16)
        _accumulate(rt, i, [(dgate_ref, dgate)], [(dg_ref, dg)])

    return pl.pallas_call(
        body, name=name, grid=(rt.n_tiles,),
        in_specs=[_row_spec(rt, D_MODEL), _row_spec(rt, D_MODEL), _gathered_spec(wg, "out"), _mod_spec(rt), _vec_spec(D_MODEL)],
        out_specs=[_row_spec(rt, D_MODEL), _row_spec(rt, D_MODEL), _group_spec(rt), _vec_spec(D_MODEL)],
        out_shape=[jax.ShapeDtypeStruct((rt.rows, D_MODEL), BF16), jax.ShapeDtypeStruct((rt.rows, D_MODEL), BF16),
                   _group_shape(rt), _vec_shape()],
        compiler_params=_params(("arbitrary",)),
    )(dh1, mix, wg["out"][0], mod, g_post_mix)


def _w_chunk(w_ref, k):
    return w_ref[2 * k:2 * k + 2].reshape(1024, 1024)


def _mlp_fwd(rt, u2, h1, wg, mod, g_post_mlp, name, comm=None, target=None):
    last = rt.n_lat_tiles - 1

    def body(u2_ref, h1_ref, wu_ref, wd_ref, mod_ref, g_ref, *rest):
        u2_ = u2_ref[...]
        y = jnp.zeros((rt.tm, D_MODEL), F32)
        for k in range(D_FF // 1024):
            a = jnp.maximum(jnp.dot(u2_, _w_chunk(wu_ref, k), preferred_element_type=F32), 0.0)
            rest[-3 if target is None else -4][:, k * 1024:(k + 1) * 1024] = a.astype(BF16)
            y = y + jnp.dot((a * a).astype(BF16), _w_chunk(wd_ref, k), preferred_element_type=F32)
        h2 = _post_norm_val(h1_ref[...], y, g_ref[...], mod_ref, 5)
        if target is None:
            _, y_ref, h2_ref = rest
            y_ref[...] = y
            h2_ref[...] = h2
        else:
            t_ref, _, y_ref, dh_ref, sq_ref = rest
            y_ref[...] = y
            i = pl.program_id(0)

            @pl.when(i == 0)
            def _():
                sq_ref[...] = jnp.zeros_like(sq_ref)

            @pl.when(i <= last)
            def _():
                e = h2 - t_ref[...]
                dh_ref[...] = e * (1.0 / D_MODEL)
                sq_ref[...] += jnp.sum(e * e, axis=0, keepdims=True)

            @pl.when(i > last)
            def _():
                dh_ref[...] = jnp.zeros_like(dh_ref)

    in_specs = [_row_spec(rt, D_MODEL), _row_spec(rt, D_MODEL), _gathered_spec(wg, "up"), _gathered_spec(wg, "down"),
                _mod_spec(rt), _vec_spec(D_MODEL)]
    args = [u2, h1, wg["up"][0], wg["down"][0], mod, g_post_mlp]
    out_specs = [_row_spec(rt, D_FF), _row_spec(rt, D_MODEL), _row_spec(rt, D_MODEL)]
    out_shape = [jax.ShapeDtypeStruct((rt.rows, D_FF), BF16), jax.ShapeDtypeStruct((rt.rows, D_MODEL), F32),
                 jax.ShapeDtypeStruct((rt.rows, D_MODEL), F32)]
    if target is not None:
        in_specs.append(pl.BlockSpec((rt.tm, D_MODEL), lambda i: (jnp.minimum(i, last), 0)))
        args.append(target)
        out_specs.append(_vec_spec(D_MODEL))
        out_shape.append(_vec_shape())
    return _comm_call(body, comm, name=name, grid=(rt.n_tiles,), in_specs=in_specs, out_specs=out_specs, out_shape=out_shape,
                      args=args, aliases={}, semantics=("parallel",) if target is None else ("arbitrary",))


def _mlp_down_bwd(rt, dh, y, ra, wg, mod, g_post_mlp, name):
    def body(dh_ref, y_ref, ra_ref, wd_ref, mod_ref, g_ref, dy_ref, da_ref, dgate_ref, dg_ref):
        i = pl.program_id(0)
        dz, dgate, dg = _post_norm_bwd_val(dh_ref[...], y_ref[...], g_ref[...], mod_ref[0, 5:6, :])
        dyb = dz.astype(BF16)
        dy_ref[...] = dyb
        for k in range(D_FF // 1024):
            dr = lax.dot_general(dyb, _w_chunk(wd_ref, k), NT, preferred_element_type=F32)
            da_ref[:, k * 1024:(k + 1) * 1024] = (dr * (2.0 * ra_ref[:, k * 1024:(k + 1) * 1024].astype(F32))).astype(BF16)
        _accumulate(rt, i, [(dgate_ref, dgate)], [(dg_ref, dg)])

    return pl.pallas_call(
        body, name=name, grid=(rt.n_tiles,),
        in_specs=[_row_spec(rt, D_MODEL), _row_spec(rt, D_MODEL), _row_spec(rt, D_FF), _gathered_spec(wg, "down"),
                  _mod_spec(rt), _vec_spec(D_MODEL)],
        out_specs=[_row_spec(rt, D_MODEL), _row_spec(rt, D_FF), _group_spec(rt), _vec_spec(D_MODEL)],
        out_shape=[jax.ShapeDtypeStruct((rt.rows, D_MODEL), BF16), jax.ShapeDtypeStruct((rt.rows, D_FF), BF16),
                   _group_shape(rt), _vec_shape()],
        compiler_params=_params(("arbitrary",)),
    )(dh, y, ra, wg["down"][0], mod, g_post_mlp)


def _mlp_up_bwd(rt, da, wg, h1, dh, mod, g_pre_mlp, name, comm=None):
    def body(da_ref, wu_ref, h1_ref, dh_ref, mod_ref, g_ref, dh1_ref, dsh_ref, dsc_ref, dg_ref):
        i = pl.program_id(0)
        du = jnp.zeros((rt.tm, D_MODEL), F32)
        for k in range(D_FF // 1024):
            du = du + lax.dot_general(da_ref[:, k * 1024:(k + 1) * 1024], _w_chunk(wu_ref, k), NT, preferred_element_type=F32)
        d, dsh, dsc, dg = _norm_mod_bwd_val(du, h1_ref[...], g_ref[...], 1.0 + mod_ref[0, 4:5, :])
        dh1_ref[...] = dh_ref[...] + d
        _accumulate(rt, i, [(dsh_ref, dsh), (dsc_ref, dsc)], [(dg_ref, dg)])

    return _comm_call(
        body, comm, name=name, grid=(rt.n_tiles,),
        in_specs=[_row_spec(rt, D_FF), _gathered_spec(wg, "up"), _row_spec(rt, D_MODEL), _row_spec(rt, D_MODEL),
                  _mod_spec(rt), _vec_spec(D_MODEL)],
        out_specs=[_row_spec(rt, D_MODEL), _group_spec(rt), _group_spec(rt), _vec_spec(D_MODEL)],
        out_shape=[jax.ShapeDtypeStruct((rt.rows, D_MODEL), F32), _group_shape(rt), _group_shape(rt), _vec_shape()],
        args=[da, wg["up"][0], h1, dh, mod, g_pre_mlp], aliases={}, semantics=("arbitrary",))


def _wgrad_packed(rt, a, b, kind, off, n_rows, p_prev, name, comm=None, cols=1024):
    h = PACK_HEIGHT[kind]
    assert cols == 1024 or (kind == "in" and cols == 2 * IN_PIECE_COLS and p_prev is None), (kind, cols)
    tk = rt.tm
    assert off % h == 0, (kind, off)

    def body(a_ref, b_ref, *rest):
        o_ref = rest[-1]
        i = pl.program_id(0)

        @pl.when(i == 0)
        def _():
            o_ref[...] = jnp.zeros_like(o_ref)

        if kind == "in":
            res = lax.dot_general(a_ref[...], b_ref[...], TN, preferred_element_type=F32)
            for k in range(4):
                for c in range(2):
                    for t in range(2):
                        o_ref[c, k, :, t * IN_PIECE_COLS:(t + 1) * IN_PIECE_COLS] += \
                            res[c * 512 + t * h:c * 512 + (t + 1) * h, k * IN_PIECE_COLS:(k + 1) * IN_PIECE_COLS]
        elif kind == "out":
            res = lax.dot_general(a_ref[...], b_ref[...], TN, preferred_element_type=F32)
            for k in range(4):
                for c in range(2):
                    o_ref[c, k] += res[(2 * k + c) * h:(2 * k + c + 1) * h]
        else:
            for k in range(4):
                if kind == "up":
                    res = lax.dot_general(a_ref[...], b_ref[:, k * 1024:(k + 1) * 1024], TN, preferred_element_type=F32)
                else:
                    ra = a_ref[:, k * 1024:(k + 1) * 1024].astype(F32)
                    res = lax.dot_general((ra * ra).astype(BF16), b_ref[...], TN, preferred_element_type=F32)
                o_ref[0, k] += res[0:h]
                o_ref[1, k] += res[h:2 * h]

    in_specs = [pl.BlockSpec((tk, a.shape[1]), lambda i: (i, 0)), pl.BlockSpec((tk, b.shape[1]), lambda i: (i, 0))]
    args = [a, b]
    aliases = {}
    if p_prev is not None:
        in_specs.append(pl.BlockSpec(memory_space=pl.ANY))
        args.append(p_prev)
        aliases = {2: 0}
    outs = _comm_call(
        body, comm, name=name, grid=(rt.n_tiles,),
        in_specs=in_specs,
        out_specs=[pl.BlockSpec((2, 4, h, cols), lambda i: (0, 0, off // h, 0))],
        out_shape=[jax.ShapeDtypeStruct((2, 4, n_rows, cols), F32)],
        args=args, aliases=aliases, semantics=("arbitrary",))
    return outs[0] if comm is None else outs


def _ada_wgrad(xs, dm, name):
    depth, _, cols = dm.shape

    def body(x_ref, d_ref, o_ref):
        for l in range(depth):
            o_ref[l] = lax.dot_general(x_ref[...], d_ref[l], TN, preferred_element_type=F32)

    return pl.pallas_call(body, name=name, out_shape=jax.ShapeDtypeStruct((depth, xs.shape[1], cols), F32),
                          compiler_params=pltpu.CompilerParams(vmem_limit_bytes=VMEM_LIMIT))(xs, dm)


def _stack_heads(x, kvi):
    x = x.astype(F32)
    tq = x.shape[0]
    lane = lax.broadcasted_iota(jnp.int32, (tq, 128), 1)
    keep = lane < HEAD_DIM if kvi == 0 else lane >= HEAD_DIM
    parts = []
    for p in range(2):
        pair = x[:, p * 128:(p + 1) * 128]
        swapped = pltpu.roll(pair, HEAD_DIM, 1)
        lo_head, hi_head = (pair, swapped) if kvi == 0 else (swapped, pair)
        parts += [jnp.where(keep, lo_head, 0.0), jnp.where(keep, hi_head, 0.0)]
    return jnp.concatenate(parts, axis=0).astype(BF16)


def _unstack_heads(o4, kvi):
    tq = o4.shape[0] // GROUP
    lane = lax.broadcasted_iota(jnp.int32, (tq, 128), 1)
    outs = []
    for p in range(2):
        r_lo, r_hi = o4[(2 * p) * tq:(2 * p + 1) * tq], o4[(2 * p + 1) * tq:(2 * p + 2) * tq]
        if kvi == 0:
            lo, hi = r_lo, pltpu.roll(r_hi, HEAD_DIM, 1)
        else:
            lo, hi = pltpu.roll(r_lo, HEAD_DIM, 1), r_hi
        outs.append(jnp.where(lane < HEAD_DIM, lo, hi))
    return jnp.concatenate(outs, axis=1)


def _per_head(shape, axis, tq, values):
    head = lax.broadcasted_iota(jnp.int32, shape, axis) // tq
    out = jnp.zeros(shape, F32)
    for g in range(GROUP):
        out = jnp.where(head == g, values[g], out)
    return out


KEY_CHUNK = 512
Q_TILE = 128
Q_TILE_FWD = 256


def _key_chunks(k_ref, v_ref, n, kc=KEY_CHUNK):
    kc = min(kc, n)
    return [(k_ref[c * kc:(c + 1) * kc, :], v_ref[c * kc:(c + 1) * kc, :], None) for c in range(n // kc)]


def _softmax_fwd(qs, chunks, sink_col):
    logits = []
    for k, _, mask in chunks:
        s = lax.dot_general(qs, k, NT, preferred_element_type=F32)
        logits.append(s if mask is None else jnp.where(mask, s, NEG_BIG))
    m = functools.reduce(jnp.maximum, [jnp.max(s, axis=1, keepdims=True) for s in logits])
    if sink_col is not None:
        m = jnp.maximum(m, sink_col)
    l = jnp.zeros_like(m) if sink_col is None else jnp.exp(sink_col - m)
    acc = jnp.zeros((qs.shape[0], 128), F32)
    for s, (_, v, _) in zip(logits, chunks):
        p = jnp.exp(s - m)
        l = l + jnp.sum(p, axis=1, keepdims=True)
        acc = acc + jnp.dot(p.astype(BF16), v, preferred_element_type=F32)
    return acc / l, m + jnp.log(l)


def _to_rows(col):
    return jnp.transpose(jnp.broadcast_to(col, (col.shape[0], 128)))[0:8, :]


def _softmax_bwd(qs, dos, lse_row, delta_row, chunks):
    dq = jnp.zeros((qs.shape[0], 128), F32)
    grads = []
    for k, v, mask in chunks:
        s = lax.dot_general(k, qs, NT, preferred_element_type=F32)
        if mask is not None:
            s = jnp.where(mask, s, NEG_BIG)
        p = jnp.exp(s - lse_row)
        dp = lax.dot_general(v, dos, NT, preferred_element_type=F32)
        ds = (p * (dp - delta_row)).astype(BF16)
        dv = jnp.dot(p.astype(BF16), dos, preferred_element_type=F32)
        dk = jnp.dot(ds, qs, preferred_element_type=F32)
        dq = dq + lax.dot_general(ds, k, TN, preferred_element_type=F32)
        grads.append((dk, dv))
    return dq, grads


def _band(qi, tq, seq):
    span = tq + 2 * WINDOW
    start = pl.multiple_of(jnp.clip(qi * tq - WINDOW, 0, seq - span), 64)
    return start, span


def _band_mask(qi, tq, start, span, query_axis):
    shape = (GROUP * tq, span) if query_axis == 0 else (span, GROUP * tq)
    qpos = qi * tq + lax.broadcasted_iota(jnp.int32, shape, query_axis) % tq
    kpos = start + lax.broadcasted_iota(jnp.int32, shape, 1 - query_axis)
    return jnp.abs(kpos - qpos) <= WINDOW


def _qkv_specs(rt, tq, q_row, ctx_row, with_latent):
    specs = [pl.BlockSpec((tq, 256), functools.partial(lambda b, i, col: (q_row(b, i), col), col=col)) for col in (0, 1, 3, 4)]
    if with_latent:
        specs += [pl.BlockSpec((rt.seq, 128), functools.partial(lambda b, i, col: (b, col), col=col))
                  for col in (COL_KA, COL_VA, COL_KB, COL_VB)]
    specs += [pl.BlockSpec((rt.ctx, 128), functools.partial(lambda b, i, col: (ctx_row(b), col), col=col))
              for col in (COL_KA, COL_VA, COL_KB, COL_VB)]
    return specs


def _attn_fwd(rt, qkvp, sink, o_prev, name, comm=None):
    latent = o_prev is None
    seq, ctx, nb = rt.seq, rt.ctx, rt.nb
    tq = Q_TILE_FWD if latent else ctx
    tile = Q_TILE if latent else ctx
    parts = tq // tile
    nq = seq // tq if latent else 1
    ctx_blk0 = rt.n_lat // ctx
    q_row = (lambda b, i: b * nq + i) if latent else (lambda b, i: ctx_blk0 + b)

    def store_lse(lse_ref, j, lse_col):
        rows = _to_rows(lse_col)
        for part in range(parts):
            lse_ref[part, j] = jnp.concatenate([rows[:, g * tq + part * tile:g * tq + (part + 1) * tile] for g in range(GROUP)], axis=1)

    def body(sink_ref, qa0, qa1, qb0, qb1, *rest):
        if latent:
            kal, val, kbl, vbl, kac, vac, kbc, vbc, o_ref, lse_ref = rest
        else:
            kac, vac, kbc, vbc, _, o_ref, lse_ref = rest
        qi = pl.program_id(1)
        for kvi, (qa, qb) in enumerate(((qa0, qb0), (qa1, qb1))):
            src_a = _key_chunks(kac, vac, ctx)
            src_b = _key_chunks(kbc, vbc, ctx)
            if latent:
                src_a += _key_chunks(kal, val, seq, seq)
                start, span = _band(qi, tq, seq)
                src_b.append((kbl[pl.ds(start, span), :], vbl[pl.ds(start, span), :], _band_mask(qi, tq, start, span, 0)))
            oa, lse = _softmax_fwd(_stack_heads(qa[...], kvi), src_a, None)
            o_ref[:, kvi * 256:(kvi + 1) * 256] = _unstack_heads(oa, kvi).astype(BF16)
            store_lse(lse_ref, kvi, lse)
            sink_col = _per_head((GROUP * tq, 1), 0, tq, [sink_ref[kvi * GROUP + g] for g in range(GROUP)])
            ob, lse = _softmax_fwd(_stack_heads(qb[...], kvi), src_b, sink_col)
            o_ref[:, 512 + kvi * 256:512 + (kvi + 1) * 256] = _unstack_heads(ob, kvi).astype(BF16)
            store_lse(lse_ref, 2 + kvi, lse)

    specs = _qkv_specs(rt, tq, q_row, lambda b: ctx_blk0 + b, latent)
    args = [sink] + [qkvp] * len(specs)
    in_specs = [pl.BlockSpec(memory_space=pltpu.SMEM)] + specs
    aliases = {}
    if not latent:
        in_specs.append(pl.BlockSpec(memory_space=pl.ANY))
        args.append(o_prev)
        aliases = {len(args) - 1: 0}
    return _comm_call(
        body, comm, name=name, grid=(nb, nq),
        in_specs=in_specs,
        out_specs=[pl.BlockSpec((tq, 1024), lambda b, i: (q_row(b, i), 0)),
                   pl.BlockSpec((parts, 4, 8, GROUP * tile), lambda b, i: (b * nq + i, 0, 0, 0))],
        out_shape=[jax.ShapeDtypeStruct((rt.rows, 1024), BF16), jax.ShapeDtypeStruct((nb * nq * parts, 4, 8, GROUP * tile), F32)],
        args=args, aliases=aliases, semantics=("parallel", "parallel"))


def _attn_bwd(rt, qkvp, o, lse, do, sink, prev, name, comm=None):
    latent = prev is None
    seq, ctx, nb = rt.seq, rt.ctx, rt.nb
    tq = Q_TILE if latent else ctx
    nq = seq // tq if latent else 1
    ctx_blk0 = rt.n_lat // ctx
    q_row = (lambda b, i: b * nq + i) if latent else (lambda b, i: ctx_blk0 + b)
    kc = min(KEY_CHUNK, seq)

    def body(sink_ref, qa0, qa1, qb0, qb1, *rest):
        if latent:
            kal, val, kbl, vbl, kac, vac, kbc, vbc, do_ref, o_ref, lse_ref, dq_ref, dl_ref, dc_ref, dsink_ref = rest
        else:
            kac, vac, kbc, vbc, do_ref, o_ref, lse_ref, c1_ref, _, _, dq_ref, dc_ref, dsink_ref = rest
        b, qi = pl.program_id(0), pl.program_id(1)

        def rows_of(cols, kvi, mixer):
            dos = _stack_heads(do_ref[:, cols], kvi)
            delta = jnp.sum(dos.astype(F32) * _stack_heads(o_ref[:, cols], kvi).astype(F32), axis=1, keepdims=True)
            return dos, lse_ref[0, 2 * mixer + kvi, 0:1, :], _to_rows(delta)[0:1, :]

        @pl.when(jnp.logical_and(b == 0, qi == 0))
        def _():
            dsink_ref[...] = jnp.zeros_like(dsink_ref)

        if latent:
            @pl.when(qi == 0)
            def _():
                dc_ref[...] = jnp.zeros_like(dc_ref)
                dl_ref[...] = jnp.zeros_like(dl_ref)
        else:
            dc_ref[...] = c1_ref[...]

        head_row = lax.broadcasted_iota(jnp.int32, (8, 128), 0)
        for kvi, (qa, qb) in enumerate(((qa0, qb0), (qa1, qb1))):
            cols = slice(kvi * 256, (kvi + 1) * 256)
            dos, lse_row, delta_row = rows_of(cols, kvi, 0)
            src = _key_chunks(kac, vac, ctx)
            if latent:
                src += _key_chunks(kal, val, seq)
            dq4, grads = _softmax_bwd(_stack_heads(qa[...], kvi), dos, lse_row, delta_row, src)
            dq_ref[:, cols] = _unstack_heads(dq4, kvi)
            dc_ref[:, 0:128] += grads[0][0]
            dc_ref[:, 128:256] += grads[0][1]
            for c, (dk, dv) in enumerate(grads[1:]):
                dl_ref[c * kc:(c + 1) * kc, 0:128] += dk
                dl_ref[c * kc:(c + 1) * kc, 128:256] += dv
            cols = slice(512 + kvi * 256, 512 + (kvi + 1) * 256)
            dos, lse_row, delta_row = rows_of(cols, kvi, 1)
            src = _key_chunks(kbc, vbc, ctx)
            if latent:
                start, span = _band(qi, tq, seq)
                src.append((kbl[pl.ds(start, span), :], vbl[pl.ds(start, span), :], _band_mask(qi, tq, start, span, 1)))
            dq4, grads = _softmax_bwd(_stack_heads(qb[...], kvi), dos, lse_row, delta_row, src)
            dq_ref[:, cols] = _unstack_heads(dq4, kvi)
            dc_ref[:, 256:384] += grads[0][0]
            dc_ref[:, 384:512] += grads[0][1]
            if latent:
                dl_ref[pl.ds(start, span), 256:384] += grads[1][0]
                dl_ref[pl.ds(start, span), 384:512] += grads[1][1]
            sink_row = _per_head((1, GROUP * tq), 1, tq, [sink_ref[kvi * GROUP + g] for g in range(GROUP)])
            dsink = -jnp.exp(sink_row - lse_row) * delta_row
            head = lax.broadcasted_iota(jnp.int32, (1, GROUP * tq), 1) // tq
            upd = jnp.zeros((8, 128), F32)
            for g in range(GROUP):
                upd = jnp.where(head_row == kvi * GROUP + g, jnp.sum(jnp.where(head == g, dsink, 0.0)), upd)
            dsink_ref[...] += upd

    specs = _qkv_specs(rt, tq, q_row, lambda b: ctx_blk0 + b, latent)
    q_rows_spec = pl.BlockSpec((tq, 1024), lambda b, i: (q_row(b, i), 0))
    in_specs = ([pl.BlockSpec(memory_space=pltpu.SMEM)] + specs
                + [q_rows_spec, q_rows_spec, pl.BlockSpec((1, 4, 8, GROUP * tq), lambda b, i: (b * nq + i, 0, 0, 0))])
    args = [sink] + [qkvp] * len(specs) + [do, o, lse]
    dq_shape = jax.ShapeDtypeStruct((rt.rows, 1024), F32)
    dkv_shape = jax.ShapeDtypeStruct((rt.rows, 512), F32)
    dsink_spec, dsink_shape = pl.BlockSpec((8, 128), lambda b, i: (0, 0)), jax.ShapeDtypeStruct((8, 128), F32)
    dq_spec = pl.BlockSpec((tq, 1024), lambda b, i: (q_row(b, i), 0))
    if latent:
        out_specs = [dq_spec, pl.BlockSpec((seq, 512), lambda b, i: (b, 0)), pl.BlockSpec((ctx, 512), lambda b, i: (b, 0)), dsink_spec]
        out_shape = [dq_shape, dkv_shape, jax.ShapeDtypeStruct((rt.n_ctx, 512), F32), dsink_shape]
        aliases = {}
    else:
        dq_prev, dkv_prev, c1 = prev
        in_specs += [pl.BlockSpec((ctx, 512), lambda b, i: (b, 0)), pl.BlockSpec(memory_space=pl.ANY), pl.BlockSpec(memory_space=pl.ANY)]
        args += [c1, dq_prev, dkv_prev]
        out_specs = [dq_spec, pl.BlockSpec((ctx, 512), lambda b, i: (ctx_blk0 + b, 0)), dsink_spec]
        out_shape = [dq_shape, dkv_shape, dsink_shape]
        aliases = {len(args) - 2: 0, len(args) - 1: 1}
    return _comm_call(body, comm, name=name, grid=(nb, nq), in_specs=in_specs, out_specs=out_specs, out_shape=out_shape,
                      args=args, aliases=aliases, semantics=("arbitrary", "arbitrary"))


def _silu(x):
    return x / (1.0 + jnp.exp(-x))


def _whole(shape):
    return pl.BlockSpec(shape, lambda i, s: (0,) * len(shape))


def _ada_half_spec(cols):
    return pl.BlockSpec((DEPTH, D_MODEL, cols), lambda i, s: (0, 0, s[0]))


def _ada_fwd(cond, w_ada, b_half, c_idx, name):
    rows = cond.shape[0]
    cols = w_ada.shape[2] // 2

    def body(s_ref, c_ref, w_ref, b_ref, x_ref, o_ref):
        xs = _silu(c_ref[...]).astype(BF16)
        x_ref[...] = xs
        for l in range(DEPTH):
            o_ref[l] = jnp.dot(xs, w_ref[l].astype(BF16), preferred_element_type=F32) + b_ref[l]

    grid_spec = pltpu.PrefetchScalarGridSpec(
        num_scalar_prefetch=1, grid=(1,),
        in_specs=[_whole(cond.shape), _ada_half_spec(cols), _whole(b_half.shape)],
        out_specs=[_whole((rows, D_MODEL)), _whole((DEPTH, rows, cols))])
    return pl.pallas_call(
        body, name=name, grid_spec=grid_spec,
        out_shape=[jax.ShapeDtypeStruct((rows, D_MODEL), BF16), jax.ShapeDtypeStruct((DEPTH, rows, cols), F32)],
        compiler_params=_params(("arbitrary",)),
    )(c_idx, cond, w_ada, b_half)


def _ada_cond_bwd(dcx, w_ada, c_idx, name):
    _, rows, cols = dcx.shape

    def body(s_ref, d_ref, w_ref, o_ref):
        acc = jnp.zeros((rows, D_MODEL), F32)
        for l in range(DEPTH):
            acc = acc + lax.dot_general(d_ref[l], w_ref[l].astype(BF16), NT, preferred_element_type=F32)
        o_ref[...] = acc

    grid_spec = pltpu.PrefetchScalarGridSpec(
        num_scalar_prefetch=1, grid=(1,),
        in_specs=[_whole(dcx.shape), _ada_half_spec(cols)], out_specs=_whole((rows, D_MODEL)))
    return pl.pallas_call(body, name=name, grid_spec=grid_spec, out_shape=jax.ShapeDtypeStruct((rows, D_MODEL), F32),
                          compiler_params=_params(("arbitrary",)))(c_idx, dcx, w_ada)


def _dev_sum(x, name, comm=None):
    _, r, c = x.shape

    def body(x_ref, o_ref):
        v = x_ref[0]
        for d in range(1, N_DEV):
            v = v + x_ref[d]
        o_ref[...] = v

    return _comm_call(body, comm, name=name, grid=(1,), in_specs=[pl.BlockSpec(x.shape, lambda i: (0, 0, 0))],
                      out_specs=[pl.BlockSpec((r, c), lambda i: (0, 0))], out_shape=[jax.ShapeDtypeStruct((r, c), F32)],
                      args=[x], aliases={}, semantics=("arbitrary",))


def _adam_val(w, g, m, v):
    c1 = 1.0 / (1.0 - ADAM_B1 ** ADAM_STEP)
    c2 = 1.0 / (1.0 - ADAM_B2 ** ADAM_STEP)
    nm = ADAM_B1 * m + (1.0 - ADAM_B1) * g
    nv = ADAM_B2 * v + (1.0 - ADAM_B2) * (g * g)
    return -ADAM_LR * ((nm * c1) / (jnp.sqrt(nv * c2) + ADAM_EPS) + ADAM_WD * w), nm, nv


def _small_update(tot, dcc_parts, params, n_groups, name):
    n_p = len(params)
    mod_rows = n_groups * N_MOD
    head_row = DEPTH * mod_rows + 4 * DEPTH

    def body(tot_ref, dcc_ref, *refs):
        ins, outs = refs[:3 * n_p], refs[3 * n_p:]

        def update(p, rows, cols, g):
            w_ref, m_ref, v_ref = ins[3 * p:3 * p + 3]
            g_ref, d_ref, nm_ref, nv_ref = outs[4 * p:4 * p + 4]
            d, nm, nv = _adam_val(w_ref[rows, cols], g, m_ref[rows, cols], v_ref[rows, cols])
            g_ref[rows, cols] = g
            d_ref[rows, cols] = d
            nm_ref[rows, cols] = nm
            nv_ref[rows, cols] = nv

        acc = dcc_ref[0, 0:1, :]
        for d in range(1, N_DEV):
            acc = acc + dcc_ref[d, 0:1, :]
        c = ins[0][...]
        sg = 1.0 / (1.0 + jnp.exp(-c))
        update(0, slice(0, 1), slice(None), acc * (sg * (1.0 + c * (1.0 - sg))))
        for l in range(DEPTH):
            for i in range(N_MOD):
                g = tot_ref[l * mod_rows + i:l * mod_rows + i + 1, :]
                for grp in range(1, n_groups):
                    g = g + tot_ref[l * mod_rows + grp * N_MOD + i:l * mod_rows + grp * N_MOD + i + 1, :]
                update(1, slice(l, l + 1), slice(i * D_MODEL, (i + 1) * D_MODEL), g)
            for j in range(4):
                row = DEPTH * mod_rows + 4 * l + j
                update(2 + j, slice(l, l + 1), slice(None), tot_ref[row:row + 1, :])
            head = tot_ref[head_row + l:head_row + l + 1, :]
            update(6, slice(l, l + 1), slice(None), head[:, 0:HEAD_DIM] + head[:, HEAD_DIM:2 * HEAD_DIM])
            update(7, slice(l, l + 1), slice(None), head[:, 2 * HEAD_DIM:3 * HEAD_DIM] + head[:, 3 * HEAD_DIM:4 * HEAD_DIM])
            update(8, slice(l, l + 1), slice(None), head[:, 4 * HEAD_DIM:4 * HEAD_DIM + ins[3 * 8].shape[1]])

    shapes = [jax.ShapeDtypeStruct(w.shape, F32) for w, _, _ in params for _ in range(4)]
    outs = pl.pallas_call(body, name=name, out_shape=shapes)(tot, dcc_parts, *[a for p in params for a in p])
    return [tuple(outs[4 * p:4 * p + 4]) for p in range(n_p)]


def _adamw(w, g, m, v, name):
    r, c = w.shape
    tr = _pick(r, (256, 128, 64, 32, 24, 16, 8))

    def body(w_ref, g_ref, m_ref, v_ref, d_ref, nm_ref, nv_ref):
        d_ref[...], nm_ref[...], nv_ref[...] = _adam_val(w_ref[...], g_ref[...], m_ref[...], v_ref[...])

    spec = pl.BlockSpec((tr, c), lambda i: (i, 0))
    return pl.pallas_call(body, name=name, grid=(r // tr,), in_specs=[spec] * 4, out_specs=[spec] * 3,
                          out_shape=[jax.ShapeDtypeStruct((r, c), F32)] * 3, compiler_params=_params(("parallel",)))(w, g, m, v)


def _adamw_shard(kind, l, w, m, v, halves, off, prev, name):
    h = PACK_HEIGHT[kind]
    assert off % h == 0, (kind, off)
    _, r, c = w.shape
    rows = r // 2

    def body(w_ref, m_ref, v_ref, p_ref, *rest):
        g_ref, d_ref, nm_ref, nv_ref = rest[-4:]
        if kind == "in":
            for t in range(2):
                g = p_ref[:, t * IN_PIECE_COLS:(t + 1) * IN_PIECE_COLS]
                rs = slice(t * h, (t + 1) * h)
                g_ref[rs, :] = g
                d_ref[rs, :], nm_ref[rs, :], nv_ref[rs, :] = _adam_val(w_ref[rs, :], g, m_ref[rs, :], v_ref[rs, :])
        else:
            g = p_ref[...]
            g_ref[...] = g
            d_ref[...], nm_ref[...], nv_ref[...] = _adam_val(w_ref[...], g, m_ref[...], v_ref[...])

    blk = pl.BlockSpec((None, rows, c), lambda half: (l, half, 0))
    in_specs = [blk, blk, blk, pl.BlockSpec((None, h, halves.shape[2]), lambda half: (half, off // h, 0))]
    args = [w, m, v, halves]
    aliases = {}
    if prev is not None:
        in_specs += [pl.BlockSpec(memory_space=pl.ANY)] * 4
        args += list(prev)
        aliases = {4 + j: j for j in range(4)}
    return pl.pallas_call(
        body, name=name, grid=(2,), in_specs=in_specs, out_specs=[blk] * 4,
        out_shape=[jax.ShapeDtypeStruct(w.shape, F32)] * 4, input_output_aliases=aliases,
        compiler_params=_params(("parallel",)))(*args)


SMALL_ROWS = 48


def _small_rows(small, sq):
    def lane_pad(v):
        return jnp.pad(v, (0, D_MODEL - v.shape[0]))[None]

    head_rows = [lane_pad(jnp.concatenate([s["q_norm"][0], s["k_norm"][0], s["sink"]])) for s in small]
    loss_row = lane_pad((0.5 / D_MODEL) * jnp.sum(sq, keepdims=True)[0])
    rows = jnp.concatenate([s["mod"].reshape(-1, D_MODEL) for s in small] + [s["gammas"] for s in small] + head_rows + [loss_row], axis=0)
    return jnp.pad(rows, ((0, SMALL_ROWS - rows.shape[0]), (0, 0)))


def _local_step(x, ctx, target, mods, gam, qn, kn, sink, w_first, w_layers, packed, kc_idx):
    nb, seq, _ = x.shape
    rt = _Rows(nb, seq, ctx.shape[1])
    rt_lat = rt.latent_only()
    tables = _rope_tables(rt)
    fuse = packed is not None
    h = (x.reshape(rt.n_lat, D_MODEL), ctx.reshape(rt.n_ctx, D_MODEL))
    wg = [{}, {}] if fuse else [dict(w) for w in w_layers]
    wg[0]["in"] = (w_first, 0)
    if fuse:
        wg[0]["in_own"] = (packed, W_FIRST[0])
    saved = []
    for l in range(DEPTH):
        g_pre_mix, g_post_mix, g_pre_mlp, g_post_mlp = gam[l]
        if l == 0:
            u, qkv, qkvp, h = _in_fwd(rt, h, g_pre_mix, mods[l], wg[l], tables, qn[l], kn[l], f"in_fwd{l}")
        else:
            u, qkv, qkvp = _in_fwd(rt, h, g_pre_mix, mods[l], wg[l], tables, qn[l], kn[l], f"in_fwd{l}")
        if fuse and l == 0:
            o, lse_lat, w_mlp0, w_out0, w_in1 = _attn_fwd(rt, qkvp, sink[l], None, f"attn_lat_fwd{l}",
                                                         comm=_gather_comm(packed, [W_MLP0, W_OUT0, W_IN1], lead=2))
            wg[0].update({kind: (w_mlp0, PACK_OFF[(kind, 0)] - W_MLP0[0]) for kind in ("up", "down")})
            wg[0]["out"] = (w_out0, 0)
            wg[1] = {"in": (w_in1, 0)}
        elif fuse:
            o, lse_lat, w_mlp1, w_out1 = _attn_fwd(rt, qkvp, sink[l], None, f"attn_lat_fwd{l}",
                                                   comm=_gather_comm(packed, [W_MLP1, W_OUT1], lead=2))
            wg[1].update({kind: (w_mlp1, PACK_OFF[(kind, 1)] - W_MLP1[0]) for kind in ("up", "down")})
            wg[1]["out"] = (w_out1, 0)
        else:
            o, lse_lat = _attn_fwd(rt, qkvp, sink[l], None, f"attn_lat_fwd{l}")
        if l < DEPTH - 1:
            o, lse_ctx = _attn_fwd(rt, qkvp, sink[l], o, f"attn_ctx_fwd{l}")
            mix, h1, u2 = _out_fwd(rt, o, wg[l], h, mods[l], g_post_mix, g_pre_mlp, f"out_fwd{l}")
            r, y, h2 = _mlp_fwd(rt, u2, h1, wg[l], mods[l], g_post_mlp, f"mlp_fwd{l}")
        else:
            lse_ctx = None
            mix, h1, u2 = _out_fwd(rt_lat, o, wg[l], h, mods[l], g_post_mix, g_pre_mlp, f"out_fwd{l}")
            r, y, dh, sq = _mlp_fwd(rt_lat, u2, h1, wg[l], mods[l], g_post_mlp, f"mlp_fwd{l}", target=target.reshape(rt.n_lat, D_MODEL))
        saved.append((h, u, qkv, qkvp, o, lse_lat, lse_ctx, mix, h1, u2, r, y))
        h = h2

    small = [None] * DEPTH
    groups = {}
    for l in reversed(range(DEPTH)):
        g_pre_mix, g_post_mix, g_pre_mlp, g_post_mlp = gam[l]
        h0, u, qkv, qkvp, o, lse_lat, lse_ctx, mix, h1, u2, r, y = saved[l]
        mlp_group, out_group, in_group = (G_LAYER1, G_LAYER1, G_LAYER1) if l == 1 else (G_MLP0, G_OUT0, G_IN0)
        hide = fuse and l == 0

        dead_ctx = l == DEPTH - 1
        rt_b = rt_lat if dead_ctx else rt
        dy, da, d_gate_m, d_g_post_mlp = _mlp_down_bwd(rt_b, dh, y, r, wg[l], mods[l], g_post_mlp, f"mlp_down_bwd{l}")
        p_mlp = _wgrad_packed(rt_b, r, dy, "down", PACK_OFF[("down", l)] - mlp_group[0], mlp_group[1], None, f"mlp_down_wgrad{l}",
                              comm=_pair_comm(groups[G_LAYER1]) if hide else None)
        if hide:
            p_mlp, r1 = p_mlp
            sum1 = _pair_sum(groups[G_LAYER1], r1, kc_idx, "grad_pair_sum_layer1")
        p_mlp = _wgrad_packed(rt_b, u2, da, "up", PACK_OFF[("up", l)] - mlp_group[0], mlp_group[1], p_mlp, f"mlp_up_wgrad{l}")
        outs = _mlp_up_bwd(rt_b, da, wg[l], h1, dh, mods[l], g_pre_mlp, f"mlp_up_bwd{l}", comm=_pair_comm(p_mlp) if hide else None)
        dh1, d_sh_m, d_sc_m, d_g_pre_mlp = outs[:4]
        if hide:
            sum0 = _pair_sum(p_mlp, outs[4], kc_idx, "grad_pair_sum_mlp0")
        dmix, do, d_gate_a, d_g_post_mix = _out_bwd(rt_b, dh1, mix, wg[l], mods[l], g_post_mix, f"out_bwd{l}")
        p_out = _wgrad_packed(rt_b, o, dmix, "out", PACK_OFF[("out", l)] - out_group[0], out_group[1],
                              p_mlp if l == 1 else None, f"out_wgrad{l}")
        outs = _attn_bwd(rt, qkvp, o, lse_lat, do, sink[l], None, f"attn_lat_bwd{l}",
                         comm=_merge([_chip_comm([sum1[1], sum0[1]]), _pair_comm(p_out)]) if hide else None)
        dq, dkv, dkv_c, dsink1 = outs[:4]
        if hide:
            groups[G_LAYER1] = _owner_sum(sum1[0], outs[4], kc_idx, "grad_owner_sum_layer1")
            groups[G_MLP0] = _owner_sum(sum0[0], outs[5], kc_idx, "grad_owner_sum_mlp0")
            sum_out = _pair_sum(p_out, outs[6], kc_idx, "grad_pair_sum_out0")
        if dead_ctx:
            dsink2 = jnp.zeros_like(dsink1)
            d_gate_m, d_sh_m, d_sc_m, d_gate_a = [a.at[nb].set(0.0) for a in (d_gate_m, d_sh_m, d_sc_m, d_gate_a)]
        else:
            dq, dkv, dsink2 = _attn_bwd(rt, qkvp, o, lse_ctx, do, sink[l], (dq, dkv, dkv_c), f"attn_ctx_bwd{l}")
        dqkv, dh, dqn, dkn, d_sh_a, d_sc_a, d_g_pre_mix = _in_bwd(rt, dq, dkv, qkv, tables, qn[l], kn[l], wg[l], h0, dh1, mods[l],
                                                                  g_pre_mix, l == 0, f"in_bwd{l}",
                                                                  dead_ctx_dkv=dkv_c if dead_ctx else None)
        dmod = jnp.concatenate([d_sh_a, d_sc_a, d_gate_a, d_sh_m, d_sc_m, d_gate_m], axis=1)
        small[l] = dict(mod=dmod, gammas=jnp.concatenate([d_g_pre_mix, d_g_post_mix, d_g_pre_mlp, d_g_post_mlp], axis=0),
                        q_norm=dqn, k_norm=dkn, sink=(dsink1 + dsink2)[:, 0])
        tail = _merge([_gather_comm(_small_rows(small, sq), [(0, SMALL_ROWS)], lead=2),
                       _halves_comm([groups[G_LAYER1], groups[G_MLP0]]), _chip_comm([sum_out[1]])]) if hide else None
        outs = _wgrad_packed(rt, u, dqkv, "in", PACK_OFF[("in", l)] - in_group[0], in_group[1], p_out if l == 1 else None,
                             f"in_wgrad{l}", comm=tail, cols=1024 if l == 1 else 2 * IN_PIECE_COLS)
        if hide:
            groups[in_group], small_g, groups[G_LAYER1], groups[G_MLP0], r2_out = outs
            groups[G_OUT0] = _owner_sum(sum_out[0], r2_out, kc_idx, "grad_owner_sum_out0")
        else:
            groups[in_group], small_g = outs, None
            if l == 0:
                groups[G_MLP0], groups[G_OUT0] = p_mlp, p_out
    return sq, dh.reshape(nb, seq, D_MODEL), [groups[g] for g in (G_LAYER1, G_MLP0, G_OUT0, G_IN0)], small, small_g


def kernel(x, c, ctx, c_ctx, w_ada, b_ada, g_pre_mix, g_post_mix, g_pre_mlp, g_post_mlp, w_in, q_norm, k_norm, sink, w_out, w_up, w_down, loss_target, m_c_ctx, m_w_ada, m_b_ada, m_g_pre_mix, m_g_post_mix, m_g_pre_mlp, m_g_post_mlp, m_w_in, m_q_norm, m_k_norm, m_sink, m_w_out, m_w_up, m_w_down, v_c_ctx, v_w_ada, v_b_ada, v_g_pre_mix, v_g_post_mix, v_g_pre_mlp, v_g_post_mlp, v_w_in, v_q_norm, v_k_norm, v_sink, v_w_out, v_w_up, v_w_down):
    nb = x.shape[0]
    ix, iy, ic = lax.axis_index("x"), lax.axis_index("y"), lax.axis_index("c")
    chip = 2 * ix + iy
    dev = 2 * chip + ic
    ada_cols = w_ada.shape[2] // 2

    c_rows = c.reshape(8, (nb * D_MODEL) // 8)
    packed, c_all = _pack_local_half(w_in, w_out, w_up, w_down, _gather_comm(c_rows, [(0, c_rows.shape[0])]), "pack_gather_c")
    c_all = c_all.reshape(N_DEV * nb, D_MODEL)
    n_cond = N_DEV * nb + 1
    cond_rows = 16 * ((n_cond + 15) // 16)
    cond = jnp.concatenate([c_all, c_ctx[None, :], jnp.zeros((cond_rows - n_cond, D_MODEL), F32)], axis=0)
    c_idx = ic.reshape(1).astype(jnp.int32)
    kc_idx = jnp.stack([chip, ic]).astype(jnp.int32)
    b_ada_half = lax.dynamic_slice_in_dim(b_ada, dev * ada_cols, ada_cols, 1)[:, None, :]
    x_ada, mod_part = _ada_fwd(cond, w_ada, b_ada_half, c_idx, "ada_fwd")
    mod_rows2d = mod_part.reshape(DEPTH * cond_rows, ada_cols)
    mod_g, w_first = _comm_alone(_merge([_gather_comm(mod_rows2d, [(0, mod_rows2d.shape[0])]),
                                         _gather_comm(packed, [W_FIRST], copy_own=False, cols=2 * IN_PIECE_COLS)]),
                               "gather_mod_w_first")
    mod_all = mod_g.reshape(N_DEV, DEPTH, cond_rows, ada_cols).transpose(1, 2, 0, 3).reshape(DEPTH, cond_rows, N_MOD * D_MODEL)
    mods = []
    for l in range(DEPTH):
        mine = lax.dynamic_slice_in_dim(mod_all[l], dev * nb, nb, 0)
        mods.append(jnp.concatenate([mine, mod_all[l, n_cond - 1:n_cond]], axis=0).reshape(nb + 1, N_MOD, D_MODEL))

    gam = [(g_pre_mix[l][None], g_post_mix[l][None], g_pre_mlp[l][None], g_post_mlp[l][None]) for l in range(DEPTH)]
    qn = [jnp.tile(q_norm[l], 2)[None] for l in range(DEPTH)]
    kn = [jnp.tile(k_norm[l], 2)[None] for l in range(DEPTH)]
    _, grad_x, (h_layer1, h_mlp0, h_out0, p_in0), _, small_g = _local_step(x, ctx, loss_target, mods, gam, qn, kn, [sink[l] for l in range(DEPTH)],
                                                                 w_first, None, packed, kc_idx)

    def step(w, g, m, v, name):
        shape = w.shape
        cols = shape[-1]
        outs = _adamw(w.reshape(-1, cols), g.reshape(-1, cols), m.reshape(-1, cols), v.reshape(-1, cols), name)
        return tuple(a.reshape(shape) for a in outs)

    def shard_update(kind, w, m, v, layer0, layer1):
        outs = None
        for l, (halves, group) in enumerate((layer0, layer1)):
            outs = _adamw_shard(kind, l, w, m, v, halves, PACK_OFF[(kind, l)] - group[0], outs, f"adamw_w_{kind}{l}")
        return tuple(outs)

    tot, r1 = _dev_sum(small_g, "small_sum", comm=_pair_comm(p_in0))
    mod_rows = (nb + 1) * N_MOD
    loss = tot[DEPTH * mod_rows + 4 * DEPTH + DEPTH, 0]

    ex = small_g[:, :DEPTH * mod_rows].reshape(N_DEV, DEPTH, nb + 1, N_MOD * D_MODEL)[:, :, :nb]
    ex = ex.transpose(1, 0, 2, 3).reshape(DEPTH, N_DEV * nb, N_MOD * D_MODEL)
    cx = tot[:DEPTH * mod_rows].reshape(DEPTH, nb + 1, N_MOD * D_MODEL)[:, nb:]
    dm = jnp.concatenate([ex, cx, jnp.zeros((DEPTH, cond_rows - n_cond, N_MOD * D_MODEL), F32)], axis=1)
    shard_cols = w_ada.shape[2]
    grad_w_ada = _ada_wgrad(x_ada, lax.dynamic_slice_in_dim(dm, chip * shard_cols, shard_cols, 2).astype(BF16), "ada_wgrad")
    dcx = jnp.pad(lax.dynamic_slice_in_dim(cx, dev * ada_cols, ada_cols, 2), ((0, 0), (0, 15), (0, 0))).astype(BF16)
    dcc = _ada_cond_bwd(dcx, w_ada, c_idx, "ada_cond_bwd")[0:8]

    a32, a16 = _pair_sum(p_in0, r1, kc_idx, "grad_pair_sum_in0")
    r2, dcc_g = _comm_alone(_merge([_chip_comm([a16]), _gather_comm(dcc, [(0, dcc.shape[0])])]), "grad_chip_exchange_in0")
    h_in0 = _owner_sum(a32, r2, kc_idx, "grad_owner_sum_in0")
    h_in0, h_out0 = _comm_alone(_halves_comm([h_in0, h_out0]), "grad_halves_exchange_mix0")

    small_names = ["c_ctx", "b_ada", "g_pre_mix", "g_post_mix", "g_pre_mlp", "g_post_mlp", "q_norm", "k_norm", "sink"]
    assert q_norm.shape[1] == HEAD_DIM and k_norm.shape[1] == HEAD_DIM
    small_res = _small_update(tot, dcc_g, [(c_ctx[None], m_c_ctx[None], v_c_ctx[None]), (b_ada, m_b_ada, v_b_ada),
                                           (g_pre_mix, m_g_pre_mix, v_g_pre_mix), (g_post_mix, m_g_post_mix, v_g_post_mix),
                                           (g_pre_mlp, m_g_pre_mlp, v_g_pre_mlp), (g_post_mlp, m_g_post_mlp, v_g_post_mlp),
                                           (q_norm, m_q_norm, v_q_norm), (k_norm, m_k_norm, v_k_norm), (sink, m_sink, v_sink)],
                              nb + 1, "small_update")
    res = {n: r for n, r in zip(small_names, small_res)}
    res["c_ctx"] = tuple(a[0] for a in res["c_ctx"])
    res["w_ada"] = (grad_w_ada, *step(w_ada, grad_w_ada, m_w_ada, v_w_ada, "adamw_w_ada"))
    res["w_up"] = shard_update("up", w_up, m_w_up, v_w_up, (h_mlp0, G_MLP0), (h_layer1, G_LAYER1))
    res["w_down"] = shard_update("down", w_down, m_w_down, v_w_down, (h_mlp0, G_MLP0), (h_layer1, G_LAYER1))
    res["w_in"] = shard_update("in", w_in, m_w_in, v_w_in, (h_in0, G_IN0), (h_layer1, G_LAYER1))
    res["w_out"] = shard_update("out", w_out, m_w_out, v_w_out, (h_out0, G_OUT0), (h_layer1, G_LAYER1))

    order = ["c_ctx", "w_ada", "b_ada", "g_pre_mix", "g_post_mix", "g_pre_mlp", "g_post_mlp", "w_in", "q_norm", "k_norm", "sink", "w_out", "w_up", "w_down"]
    return (loss, grad_x, *[res[n][0] for n in order], *[res[n][1] for n in order],
            *[res[n][2] for n in order], *[res[n][3] for n in order])
```
